```python
import jax, jax.numpy as jnp
from jax import lax
import numpy as np

D_MODEL = 1024
BATCH = 8
SEQ = 4096
DEPTH = 2

N_GROUPS = 4
GROUP_WIDTH = D_MODEL // N_GROUPS
D_MIX = N_GROUPS * GROUP_WIDTH
N_HEADS = 4
HEAD_DIM = GROUP_WIDTH // N_HEADS
CHUNK = 128
RET_CHUNK = 128
Q_BLOCK = 128
MLA_Q_LORA = D_MODEL // 4
MLA_KV_LORA = D_MODEL // 8
MLA_NOPE = HEAD_DIM
MLA_ROPE = HEAD_DIM // 2
MLA_V = HEAD_DIM
ROPE_BASE = 10000.0
D_FF = 4 * D_MODEL
EPS = 1e-6

IN_SPLIT_SIZES = (
    2 * GROUP_WIDTH,
    GROUP_WIDTH, GROUP_WIDTH, GROUP_WIDTH, GROUP_WIDTH,
    GROUP_WIDTH, GROUP_WIDTH, GROUP_WIDTH, N_HEADS,
    MLA_Q_LORA, MLA_KV_LORA, MLA_ROPE,
)
N_IN = sum(IN_SPLIT_SIZES)
IN_SPLIT_POINTS = tuple(np.cumsum(IN_SPLIT_SIZES)[:-1].tolist())

kernel_name = 'hymba_style_four_mixer_hybrid'


def rms_norm(t, g):
    tf = t.astype(jnp.float32)
    y = tf * lax.rsqrt(jnp.mean(tf * tf, axis=-1, keepdims=True) + EPS)
    return (y * g.astype(jnp.float32)).astype(t.dtype)


def standardize(t):
    mu = jnp.mean(t, axis=-1, keepdims=True)
    var = jnp.mean(jnp.square(t - mu), axis=-1, keepdims=True)
    return (t - mu) * lax.rsqrt(var + EPS)


def to_heads(t, h=N_HEADS):
    b, s, _ = t.shape
    return t.reshape(b, s, h, -1).transpose(0, 2, 1, 3)


def from_heads(t):
    b, h, s, d = t.shape
    return t.transpose(0, 2, 1, 3).reshape(b, s, h * d)


def rotary(t):
    s, d = t.shape[-2], t.shape[-1]
    half = d // 2
    inv_freq = jnp.power(ROPE_BASE, -jnp.arange(half, dtype=jnp.float32) / half)
    ang = jnp.arange(s, dtype=jnp.float32)[:, None] * inv_freq[None, :]
    cos, sin = jnp.cos(ang), jnp.sin(ang)
    t1 = t[..., :half].astype(jnp.float32)
    t2 = t[..., half:].astype(jnp.float32)
    return jnp.concatenate([t1 * cos - t2 * sin, t1 * sin + t2 * cos], axis=-1)


def causal_block_attention(q, k, v, scale, cum_log_f=None):
    b, h, s, dq = q.shape
    dv = v.shape[-1]
    nb = s // Q_BLOCK
    kf = k.astype(jnp.float32)
    vf = v.astype(jnp.float32)
    q_blocks = q.astype(jnp.float32).reshape(b, h, nb, Q_BLOCK, dq).transpose(2, 0, 1, 3, 4)
    key_pos = jnp.arange(s)
    blk_ids = jnp.arange(nb)

    def attend(i, q_blk, cum_blk):
        logits = jnp.einsum('bhqd,bhkd->bhqk', q_blk, kf) * scale
        if cum_blk is not None:
            logits = logits + cum_blk[..., :, None] - cum_log_f[:, :, None, :]
        q_pos = i * Q_BLOCK + jnp.arange(Q_BLOCK)
        mask = key_pos[None, :] <= q_pos[:, None]
        logits = jnp.where(mask, logits, -jnp.inf)
        p = jax.nn.softmax(logits, axis=-1)
        return jnp.einsum('bhqk,bhkd->bhqd', p, vf)

    if cum_log_f is None:
        out = lax.map(lambda a: attend(a[0], a[1], None), (blk_ids, q_blocks))
    else:
        cum_log_f = cum_log_f.astype(jnp.float32)
        cum_blocks = cum_log_f.reshape(b, h, nb, Q_BLOCK).transpose(2, 0, 1, 3)
        out = lax.map(lambda a: attend(a[0], a[1], a[2]), (blk_ids, q_blocks, cum_blocks))
    return out.transpose(1, 2, 0, 3, 4).reshape(b, h, s, dv)


def spatial_gating_chunked(uv, ln_gain, w_s, b_s):
    b, s, _ = uv.shape
    uvf = jax.nn.gelu(uv.astype(jnp.float32))
    u, v = jnp.split(uvf, 2, axis=-1)
    v = standardize(v.reshape(b, s, N_HEADS, HEAD_DIM)) * ln_gain.astype(jnp.float32).reshape(N_HEADS, HEAD_DIM)
    v = v.reshape(b, s // CHUNK, CHUNK, N_HEADS, HEAD_DIM)
    w_causal = jnp.tril(w_s.astype(jnp.float32))
    mixed = jnp.einsum('hts,bnshd->bnthd', w_causal, v) + b_s.astype(jnp.float32).T[None, None, :, :, None]
    return (u * mixed.reshape(b, s, GROUP_WIDTH)).astype(uv.dtype)


def retention_chunkwise(q, k, v, log_gamma):
    b, h, s, dk = q.shape
    dv = v.shape[-1]
    c = RET_CHUNK
    nc = s // c
    qc = q.reshape(b, h, nc, c, dk)
    kc = k.reshape(b, h, nc, c, dk)
    vc = v.reshape(b, h, nc, c, dv)
    j = jnp.arange(c, dtype=jnp.float32)
    lg = log_gamma[:, None]
    rel = j[:, None] - j[None, :]
    intra_decay = jnp.where(rel[None] >= 0, jnp.exp(jnp.maximum(rel, 0.0)[None] * log_gamma[:, None, None]), 0.0)
    scores = jnp.einsum('bhncd,bhnmd->bhncm', qc, kc) * intra_decay[None, :, None]
    intra = jnp.einsum('bhncm,bhnme->bhnce', scores, vc)
    key_w = jnp.exp((c - 1 - j)[None, :] * lg)
    chunk_kv = jnp.einsum('bhncd,bhnce->nbhde', kc * key_w[None, :, None, :, None], vc)
    chunk_decay = jnp.exp(c * log_gamma)[None, :, None, None]

    def step(state, kv):
        return chunk_decay * state + kv, state

    _, prev_states = lax.scan(step, jnp.zeros((b, h, dk, dv), jnp.float32), chunk_kv)
    query_w = jnp.exp((j + 1.0)[None, :] * lg)
    cross = jnp.einsum('bhncd,nbhde->bhnce', qc * query_w[None, :, None, :, None], prev_states)
    return (intra + cross).reshape(b, h, s, dv)


def retention_mixer(q, k, v, g, log_gamma):
    qh = rotary(to_heads(q))
    kh = rotary(to_heads(k)) * (HEAD_DIM ** -0.5)
    vh = to_heads(v).astype(jnp.float32)
    y = standardize(retention_chunkwise(qh, kh, vh, log_gamma))
    return (jax.nn.silu(g.astype(jnp.float32)) * from_heads(y)).astype(q.dtype)


def forgetting_attention(q, k, v, f_logit, b_f):
    log_f = jax.nn.log_sigmoid(f_logit.astype(jnp.float32) + b_f.astype(jnp.float32))
    cum = jnp.cumsum(log_f, axis=1).transpose(0, 2, 1)
    y = causal_block_attention(to_heads(q), to_heads(k), to_heads(v), HEAD_DIM ** -0.5, cum)
    return from_heads(y).astype(q.dtype)


def mla_mixer(c_q, c_kv, k_rope, g_q, w_uq, g_kv, w_ukv):
    b, s, _ = c_q.shape
    q = (rms_norm(c_q, g_q) @ w_uq).reshape(b, s, N_HEADS, MLA_NOPE + MLA_ROPE).transpose(0, 2, 1, 3)
    kv = (rms_norm(c_kv, g_kv) @ w_ukv).reshape(b, s, N_HEADS, MLA_NOPE + MLA_V).transpose(0, 2, 1, 3)
    q = jnp.concatenate([q[..., :MLA_NOPE].astype(jnp.float32), rotary(q[..., MLA_NOPE:])], axis=-1)
    k_r = jnp.broadcast_to(rotary(k_rope[:, None]), (b, N_HEADS, s, MLA_ROPE))
    k = jnp.concatenate([kv[..., :MLA_NOPE].astype(jnp.float32), k_r], axis=-1)
    v = kv[..., MLA_NOPE:]
    y = causal_block_attention(q, k, v, (MLA_NOPE + MLA_ROPE) ** -0.5)
    return from_heads(y).astype(c_q.dtype)


def _fwd_setup_inputs(seed: int = 0) -> dict:
    key = jax.random.key(seed)
    ks = jax.random.split(key, 18)
    f32 = jnp.float32

    def nrm(k, shape, scale):
        return jax.random.normal(k, shape, f32) * scale

    def gain(k, shape):
        return 1.0 + 0.02 * jax.random.normal(k, shape, f32)

    return {
        'x': nrm(ks[0], (BATCH, SEQ, D_MODEL), 1.0),
        'g_mix_norm': gain(ks[1], (DEPTH, D_MODEL)),
        'w_in': nrm(ks[2], (DEPTH, D_MODEL, N_IN), D_MODEL ** -0.5),
        'b_forget': 2.0 + 0.1 * jax.random.normal(ks[3], (DEPTH, N_HEADS), f32),
        'g_sgu': gain(ks[4], (DEPTH, GROUP_WIDTH)),
        'w_spatial': nrm(ks[5], (DEPTH, N_HEADS, CHUNK, CHUNK), CHUNK ** -0.5),
        'b_spatial': gain(ks[6], (DEPTH, N_HEADS, CHUNK)),
        'g_mla_q': gain(ks[7], (DEPTH, MLA_Q_LORA)),
        'w_uq': nrm(ks[8], (DEPTH, MLA_Q_LORA, N_HEADS * (MLA_NOPE + MLA_ROPE)), MLA_Q_LORA ** -0.5),
        'g_mla_kv': gain(ks[9], (DEPTH, MLA_KV_LORA)),
        'w_ukv': nrm(ks[10], (DEPTH, MLA_KV_LORA, N_HEADS * (MLA_NOPE + MLA_V)), MLA_KV_LORA ** -0.5),
        'g_group_out': gain(ks[11], (DEPTH, D_MIX)),
        'w_out': nrm(ks[12], (DEPTH, D_MIX, D_MODEL), D_MIX ** -0.5),
        'g_ffn_norm': gain(ks[13], (DEPTH, D_MODEL)),
        'w_up': nrm(ks[14], (DEPTH, D_MODEL, D_FF), D_MODEL ** -0.5),
        'w_down': nrm(ks[15], (DEPTH, D_FF, D_MODEL), D_FF ** -0.5),
        'g_final': gain(ks[16], (D_MODEL,)),
    }


def _fwd_reference(x, g_mix_norm, w_in, b_forget, g_sgu, w_spatial, b_spatial, g_mla_q, w_uq, g_mla_kv, w_ukv,
              g_group_out, w_out, g_ffn_norm, w_up, w_down, g_final):
    b, s, _ = x.shape
    log_gamma = jnp.log1p(-jnp.exp2(-5.0 - jnp.arange(N_HEADS, dtype=jnp.float32)))
    for l in range(DEPTH):
        h = rms_norm(x, g_mix_norm[l])
        z = h @ w_in[l]
        (a_uv, b_q, b_k, b_v, b_g, c_q, c_k, c_v, c_f, d_cq, d_ckv, d_kr) = jnp.split(z, IN_SPLIT_POINTS, axis=-1)
        y_a = spatial_gating_chunked(a_uv, g_sgu[l], w_spatial[l], b_spatial[l])
        y_b = retention_mixer(b_q, b_k, b_v, b_g, log_gamma)
        y_c = forgetting_attention(c_q, c_k, c_v, c_f, b_forget[l])
        y_d = mla_mixer(d_cq, d_ckv, d_kr, g_mla_q[l], w_uq[l], g_mla_kv[l], w_ukv[l])
        y = jnp.stack([y_a, y_b, y_c, y_d], axis=2)
        y = rms_norm(y, g_group_out[l].reshape(N_GROUPS, GROUP_WIDTH)).reshape(b, s, D_MIX)
        x = x + y @ w_out[l]
        h = rms_norm(x, g_ffn_norm[l])
        x = x + jnp.square(jax.nn.relu(h @ w_up[l])) @ w_down[l]
    return rms_norm(x, g_final)


import jax as _jax
import jax.numpy as _jnp

TWIN_FORMAT = 'train_step'
FWD_PARAMS = ['x', 'g_mix_norm', 'w_in', 'b_forget', 'g_sgu', 'w_spatial', 'b_spatial', 'g_mla_q', 'w_uq', 'g_mla_kv', 'w_ukv', 'g_group_out', 'w_out', 'g_ffn_norm', 'w_up', 'w_down', 'g_final']
TWIN_WEIGHTS = ['g_mix_norm', 'w_in', 'b_forget', 'g_sgu', 'w_spatial', 'b_spatial', 'g_mla_q', 'w_uq', 'g_mla_kv', 'w_ukv', 'g_group_out', 'w_out', 'g_ffn_norm', 'w_up', 'w_down', 'g_final']
TWIN_DIFF_INPUT = 'x'
TWIN_INPUTS = ['x', 'g_mix_norm', 'w_in', 'b_forget', 'g_sgu', 'w_spatial', 'b_spatial', 'g_mla_q', 'w_uq', 'g_mla_kv', 'w_ukv', 'g_group_out', 'w_out', 'g_ffn_norm', 'w_up', 'w_down', 'g_final', 'loss_target', 'm_g_mix_norm', 'm_w_in', 'm_b_forget', 'm_g_sgu', 'm_w_spatial', 'm_b_spatial', 'm_g_mla_q', 'm_w_uq', 'm_g_mla_kv', 'm_w_ukv', 'm_g_group_out', 'm_w_out', 'm_g_ffn_norm', 'm_w_up', 'm_w_down', 'm_g_final', 'v_g_mix_norm', 'v_w_in', 'v_b_forget', 'v_g_sgu', 'v_w_spatial', 'v_b_spatial', 'v_g_mla_q', 'v_w_uq', 'v_g_mla_kv', 'v_w_ukv', 'v_g_group_out', 'v_w_out', 'v_g_ffn_norm', 'v_w_up', 'v_w_down', 'v_g_final']
TWIN_OUTPUTS = ['loss', 'grad_x', 'grad_g_mix_norm', 'grad_w_in', 'grad_b_forget', 'grad_g_sgu', 'grad_w_spatial', 'grad_b_spatial', 'grad_g_mla_q', 'grad_w_uq', 'grad_g_mla_kv', 'grad_w_ukv', 'grad_g_group_out', 'grad_w_out', 'grad_g_ffn_norm', 'grad_w_up', 'grad_w_down', 'grad_g_final', 'delta_g_mix_norm', 'delta_w_in', 'delta_b_forget', 'delta_g_sgu', 'delta_w_spatial', 'delta_b_spatial', 'delta_g_mla_q', 'delta_w_uq', 'delta_g_mla_kv', 'delta_w_ukv', 'delta_g_group_out', 'delta_w_out', 'delta_g_ffn_norm', 'delta_w_up', 'delta_w_down', 'delta_g_final', 'new_m_g_mix_norm', 'new_m_w_in', 'new_m_b_forget', 'new_m_g_sgu', 'new_m_w_spatial', 'new_m_b_spatial', 'new_m_g_mla_q', 'new_m_w_uq', 'new_m_g_mla_kv', 'new_m_w_ukv', 'new_m_g_group_out', 'new_m_w_out', 'new_m_g_ffn_norm', 'new_m_w_up', 'new_m_w_down', 'new_m_g_final', 'new_v_g_mix_norm', 'new_v_w_in', 'new_v_b_forget', 'new_v_g_sgu', 'new_v_w_spatial', 'new_v_b_spatial', 'new_v_g_mla_q', 'new_v_w_uq', 'new_v_g_mla_kv', 'new_v_w_ukv', 'new_v_g_group_out', 'new_v_w_out', 'new_v_g_ffn_norm', 'new_v_w_up', 'new_v_w_down', 'new_v_g_final']
TWIN_LEAF_KINDS = {'loss': 'loss', 'grad_x': 'grad_x', 'grad_g_mix_norm': 'grad_w', 'grad_w_in': 'grad_w', 'grad_b_forget': 'grad_w', 'grad_g_sgu': 'grad_w', 'grad_w_spatial': 'grad_w', 'grad_b_spatial': 'grad_w', 'grad_g_mla_q': 'grad_w', 'grad_w_uq': 'grad_w', 'grad_g_mla_kv': 'grad_w', 'grad_w_ukv': 'grad_w', 'grad_g_group_out': 'grad_w', 'grad_w_out': 'grad_w', 'grad_g_ffn_norm': 'grad_w', 'grad_w_up': 'grad_w', 'grad_w_down': 'grad_w', 'grad_g_final': 'grad_w', 'delta_g_mix_norm': 'delta_w', 'delta_w_in': 'delta_w', 'delta_b_forget': 'delta_w', 'delta_g_sgu': 'delta_w', 'delta_w_spatial': 'delta_w', 'delta_b_spatial': 'delta_w', 'delta_g_mla_q': 'delta_w', 'delta_w_uq': 'delta_w', 'delta_g_mla_kv': 'delta_w', 'delta_w_ukv': 'delta_w', 'delta_g_group_out': 'delta_w', 'delta_w_out': 'delta_w', 'delta_g_ffn_norm': 'delta_w', 'delta_w_up': 'delta_w', 'delta_w_down': 'delta_w', 'delta_g_final': 'delta_w', 'new_m_g_mix_norm': 'new_m', 'new_m_w_in': 'new_m', 'new_m_b_forget': 'new_m', 'new_m_g_sgu': 'new_m', 'new_m_w_spatial': 'new_m', 'new_m_b_spatial': 'new_m', 'new_m_g_mla_q': 'new_m', 'new_m_w_uq': 'new_m', 'new_m_g_mla_kv': 'new_m', 'new_m_w_ukv': 'new_m', 'new_m_g_group_out': 'new_m', 'new_m_w_out': 'new_m', 'new_m_g_ffn_norm': 'new_m', 'new_m_w_up': 'new_m', 'new_m_w_down': 'new_m', 'new_m_g_final': 'new_m', 'new_v_g_mix_norm': 'new_v', 'new_v_w_in': 'new_v', 'new_v_b_forget': 'new_v', 'new_v_g_sgu': 'new_v', 'new_v_w_spatial': 'new_v', 'new_v_b_spatial': 'new_v', 'new_v_g_mla_q': 'new_v', 'new_v_w_uq': 'new_v', 'new_v_g_mla_kv': 'new_v', 'new_v_w_ukv': 'new_v', 'new_v_g_group_out': 'new_v', 'new_v_w_out': 'new_v', 'new_v_g_ffn_norm': 'new_v', 'new_v_w_up': 'new_v', 'new_v_w_down': 'new_v', 'new_v_g_final': 'new_v'}


def _forward(args):
    return _fwd_reference(*[args[k] for k in FWD_PARAMS])


def _output_shape():
    def fwd():
        inp = _fwd_setup_inputs(0)
        return _fwd_reference(*[inp[k] for k in FWD_PARAMS])
    out = _jax.eval_shape(fwd)
    return out.shape, out.dtype

N_MICROBATCH = 1
ADAM_LR = 0.001
ADAM_B1 = 0.9
ADAM_B2 = 0.999
ADAM_EPS = 1e-08
ADAM_WD = 0.01
ADAM_STEP = 10
PER_EXAMPLE_BATCH_AXIS = {'x': 0, 'loss_target': 0}
SHARED_INPUTS = []
_WEIGHT_DTYPES = {'g_mix_norm': _jnp.float32, 'w_in': _jnp.float32, 'b_forget': _jnp.float32, 'g_sgu': _jnp.float32, 'w_spatial': _jnp.float32, 'b_spatial': _jnp.float32, 'g_mla_q': _jnp.float32, 'w_uq': _jnp.float32, 'g_mla_kv': _jnp.float32, 'w_ukv': _jnp.float32, 'g_group_out': _jnp.float32, 'w_out': _jnp.float32, 'g_ffn_norm': _jnp.float32, 'w_up': _jnp.float32, 'w_down': _jnp.float32, 'g_final': _jnp.float32}
MOMENT_SCALE = {'g_mix_norm': 1.973864e-01, 'w_in': 1.209095e-01, 'b_forget': 1.199546e+00, 'g_sgu': 7.304467e-02, 'w_spatial': 4.761519e-02, 'b_spatial': 6.659816e-02, 'g_mla_q': 1.086513e-01, 'w_uq': 8.727121e-02, 'g_mla_kv': 2.817888e-01, 'w_ukv': 1.342631e-01, 'g_group_out': 1.390581e-01, 'w_out': 1.446754e-01, 'g_ffn_norm': 1.367405e-01, 'w_up': 6.711945e-02, 'w_down': 1.573080e-01, 'g_final': 3.291102e+01}


def _to_microbatches(a, axis):
    t = _jnp.moveaxis(a, axis, 0)
    t = t.reshape((N_MICROBATCH, t.shape[0] // N_MICROBATCH) + t.shape[1:])
    return _jnp.moveaxis(t, 1, axis + 1)


def setup_inputs(seed: int = 0) -> dict:
    inp = _fwd_setup_inputs(seed)
    key = _jax.random.fold_in(_jax.random.key(seed), 7919)
    shape, _ = _output_shape()
    out = dict(inp)
    out["loss_target"] = _jax.random.normal(_jax.random.fold_in(key, 0), shape, _jnp.float32)
    for i, name in enumerate(TWIN_WEIGHTS):
        w = inp[name].astype(_jnp.float32)
        if MOMENT_SCALE is None:
            s = _jnp.sqrt(_jnp.mean(_jnp.square(w)) + 1e-30)
        else:
            s = MOMENT_SCALE[name]
        km, kv = _jax.random.split(_jax.random.fold_in(key, i + 1))
        out[name] = w
        out["m_" + name] = s * _jax.random.normal(km, w.shape, _jnp.float32)
        out["v_" + name] = (s * s) * _jax.random.uniform(kv, w.shape, _jnp.float32, 0.5, 1.5)
    if N_MICROBATCH > 1:
        for name, axis in PER_EXAMPLE_BATCH_AXIS.items():
            out[name] = _to_microbatches(out[name], axis)
    return {'x': out['x'], 'g_mix_norm': out['g_mix_norm'], 'w_in': out['w_in'], 'b_forget': out['b_forget'], 'g_sgu': out['g_sgu'], 'w_spatial': out['w_spatial'], 'b_spatial': out['b_spatial'], 'g_mla_q': out['g_mla_q'], 'w_uq': out['w_uq'], 'g_mla_kv': out['g_mla_kv'], 'w_ukv': out['w_ukv'], 'g_group_out': out['g_group_out'], 'w_out': out['w_out'], 'g_ffn_norm': out['g_ffn_norm'], 'w_up': out['w_up'], 'w_down': out['w_down'], 'g_final': out['g_final'], 'loss_target': out['loss_target'], 'm_g_mix_norm': out['m_g_mix_norm'], 'm_w_in': out['m_w_in'], 'm_b_forget': out['m_b_forget'], 'm_g_sgu': out['m_g_sgu'], 'm_w_spatial': out['m_w_spatial'], 'm_b_spatial': out['m_b_spatial'], 'm_g_mla_q': out['m_g_mla_q'], 'm_w_uq': out['m_w_uq'], 'm_g_mla_kv': out['m_g_mla_kv'], 'm_w_ukv': out['m_w_ukv'], 'm_g_group_out': out['m_g_group_out'], 'm_w_out': out['m_w_out'], 'm_g_ffn_norm': out['m_g_ffn_norm'], 'm_w_up': out['m_w_up'], 'm_w_down': out['m_w_down'], 'm_g_final': out['m_g_final'], 'v_g_mix_norm': out['v_g_mix_norm'], 'v_w_in': out['v_w_in'], 'v_b_forget': out['v_b_forget'], 'v_g_sgu': out['v_g_sgu'], 'v_w_spatial': out['v_w_spatial'], 'v_b_spatial': out['v_b_spatial'], 'v_g_mla_q': out['v_g_mla_q'], 'v_w_uq': out['v_w_uq'], 'v_g_mla_kv': out['v_g_mla_kv'], 'v_w_ukv': out['v_w_ukv'], 'v_g_group_out': out['v_g_group_out'], 'v_w_out': out['v_w_out'], 'v_g_ffn_norm': out['v_g_ffn_norm'], 'v_w_up': out['v_w_up'], 'v_w_down': out['v_w_down'], 'v_g_final': out['v_g_final']}


def _loss(weights, diff, rest, loss_target):
    with _jax.named_scope("forward"):
        args = {**rest, TWIN_DIFF_INPUT: diff, **{k: w.astype(_WEIGHT_DTYPES[k]) for k, w in weights.items()}}
        y = _forward(args)
    with _jax.named_scope("loss_head"):
        err = _jnp.square(y.astype(_jnp.float32) - loss_target)
        return 0.5 * _jnp.sum(_jnp.mean(err, axis=-1)) if err.ndim else 0.5 * err


def _adamw(w, g, m, v):
    m = ADAM_B1 * m + (1.0 - ADAM_B1) * g
    v = ADAM_B2 * v + (1.0 - ADAM_B2) * _jnp.square(g)
    m_hat = m / (1.0 - ADAM_B1 ** ADAM_STEP)
    v_hat = v / (1.0 - ADAM_B2 ** ADAM_STEP)
    delta = -ADAM_LR * (m_hat / (_jnp.sqrt(v_hat) + ADAM_EPS) + ADAM_WD * w)
    return delta, m, v


def reference(x, g_mix_norm, w_in, b_forget, g_sgu, w_spatial, b_spatial, g_mla_q, w_uq, g_mla_kv, w_ukv, g_group_out, w_out, g_ffn_norm, w_up, w_down, g_final, loss_target, m_g_mix_norm, m_w_in, m_b_forget, m_g_sgu, m_w_spatial, m_b_spatial, m_g_mla_q, m_w_uq, m_g_mla_kv, m_w_ukv, m_g_group_out, m_w_out, m_g_ffn_norm, m_w_up, m_w_down, m_g_final, v_g_mix_norm, v_w_in, v_b_forget, v_g_sgu, v_w_spatial, v_b_spatial, v_g_mla_q, v_w_uq, v_g_mla_kv, v_w_ukv, v_g_group_out, v_w_out, v_g_ffn_norm, v_w_up, v_w_down, v_g_final):
    given = dict(x=x, g_mix_norm=g_mix_norm, w_in=w_in, b_forget=b_forget, g_sgu=g_sgu, w_spatial=w_spatial, b_spatial=b_spatial, g_mla_q=g_mla_q, w_uq=w_uq, g_mla_kv=g_mla_kv, w_ukv=w_ukv, g_group_out=g_group_out, w_out=w_out, g_ffn_norm=g_ffn_norm, w_up=w_up, w_down=w_down, g_final=g_final, loss_target=loss_target, m_g_mix_norm=m_g_mix_norm, m_w_in=m_w_in, m_b_forget=m_b_forget, m_g_sgu=m_g_sgu, m_w_spatial=m_w_spatial, m_b_spatial=m_b_spatial, m_g_mla_q=m_g_mla_q, m_w_uq=m_w_uq, m_g_mla_kv=m_g_mla_kv, m_w_ukv=m_w_ukv, m_g_group_out=m_g_group_out, m_w_out=m_w_out, m_g_ffn_norm=m_g_ffn_norm, m_w_up=m_w_up, m_w_down=m_w_down, m_g_final=m_g_final, v_g_mix_norm=v_g_mix_norm, v_w_in=v_w_in, v_b_forget=v_b_forget, v_g_sgu=v_g_sgu, v_w_spatial=v_w_spatial, v_b_spatial=v_b_spatial, v_g_mla_q=v_g_mla_q, v_w_uq=v_w_uq, v_g_mla_kv=v_g_mla_kv, v_w_ukv=v_w_ukv, v_g_group_out=v_g_group_out, v_w_out=v_w_out, v_g_ffn_norm=v_g_ffn_norm, v_w_up=v_w_up, v_w_down=v_w_down, v_g_final=v_g_final)
    weights = {n: given[n] for n in TWIN_WEIGHTS}
    shared = {n: given[n] for n in SHARED_INPUTS}
    per_example = {n: given[n] for n in ['x']}
    grad_fn = _jax.value_and_grad(_loss, argnums=(0, 1))

    def one_microbatch(ex, loss_target):
        ex = dict(ex)
        diff = ex.pop(TWIN_DIFF_INPUT)
        return grad_fn(weights, diff, {**shared, **ex}, loss_target)

    if N_MICROBATCH == 1:
        loss, (grad_w, grad_x) = one_microbatch(per_example, given["loss_target"])
    else:
        def body(carry, xs):
            loss_sum, grad_sum = carry
            l_k, (gw_k, gx_k) = one_microbatch(xs[0], xs[1])
            with _jax.named_scope("update"):
                return (loss_sum + l_k, _jax.tree.map(_jnp.add, grad_sum, gw_k)), gx_k

        init = (_jnp.zeros((), _jnp.float32), _jax.tree.map(_jnp.zeros_like, weights))
        (loss, grad_w), grad_x = _jax.lax.scan(body, init, (per_example, given["loss_target"]))
    with _jax.named_scope("update"):
        delta_w, new_m, new_v = {}, {}, {}
        for n in TWIN_WEIGHTS:
            delta_w[n], new_m[n], new_v[n] = _adamw(weights[n], grad_w[n], given["m_" + n], given["v_" + n])
    return (loss, grad_x, *[grad_w[n] for n in TWIN_WEIGHTS], *[delta_w[n] for n in TWIN_WEIGHTS],
            *[new_m[n] for n in TWIN_WEIGHTS], *[new_v[n] for n in TWIN_WEIGHTS])
```

```python
import functools

import jax
import jax.numpy as jnp
import numpy as np
from jax import lax
from jax.experimental import pallas as pl
from jax.experimental.pallas import tpu as pltpu

F32 = jnp.float32
_MXU = jnp.bfloat16
_WIRE = jnp.bfloat16
EPS = 1e-6
NDEV = 8
AXES = ("x", "y", "c")
MESH = pl.DeviceIdType.MESH

N_HEADS = 4
HEAD_DIM = 64
GROUP = 256
CHUNK = 128
NZ = 2816
N_IN = 2724
MISC_F, MISC_KR = 0, 32
VMEM_LIMIT = 56 * 1024 * 1024

ADAM_LR, ADAM_B1, ADAM_B2, ADAM_EPS, ADAM_WD, ADAM_STEP = 0.001, 0.9, 0.999, 1e-08, 0.01, 10

SDS = jax.ShapeDtypeStruct


def _cp(*sem):
    return pltpu.CompilerParams(dimension_semantics=sem, vmem_limit_bytes=VMEM_LIMIT)


def _dot(a, b):
    return jnp.dot(a.astype(_MXU), b.astype(_MXU), preferred_element_type=F32)


def _dot_nt(a, b):
    return lax.dot_general(a.astype(_MXU), b.astype(_MXU), (((1,), (1,)), ((), ())), preferred_element_type=F32)


def _dot_tn(a, b):
    return lax.dot_general(a.astype(_MXU), b.astype(_MXU), (((0,), (0,)), ((), ())), preferred_element_type=F32)


def _dot_exact(a, b, dims=(((1,), (0,)), ((), ()))):
    return lax.dot_general(a, b, dims, precision=lax.Precision.HIGHEST, preferred_element_type=F32)


def _rms(x, g):
    return x * lax.rsqrt(jnp.mean(x * x, axis=-1, keepdims=True) + EPS) * g


def _rms_bwd(x, g, dy):
    xh = x * lax.rsqrt(jnp.mean(x * x, axis=-1, keepdims=True) + EPS)
    dxh = dy * g
    r = lax.rsqrt(jnp.mean(x * x, axis=-1, keepdims=True) + EPS)
    dx = r * (dxh - xh * jnp.mean(dxh * xh, axis=-1, keepdims=True))
    return dx, jnp.sum(dy * xh, axis=0, keepdims=True)


def _standardize(t):
    mu = jnp.mean(t, axis=-1, keepdims=True)
    tc = t - mu
    rs = lax.rsqrt(jnp.mean(tc * tc, axis=-1, keepdims=True) + EPS)
    return tc * rs, rs


def _standardize_bwd(yh, rs, dy):
    return rs * (dy - jnp.mean(dy, axis=-1, keepdims=True) - yh * jnp.mean(dy * yh, axis=-1, keepdims=True))


_GELU_C = 0.7978845608028654


def _gelu(x):
    return 0.5 * x * (1.0 + jnp.tanh(_GELU_C * (x + 0.044715 * x * x * x)))


def _gelu_grad(x):
    t = jnp.tanh(_GELU_C * (x + 0.044715 * x * x * x))
    return 0.5 * (1.0 + t) + 0.5 * x * (1.0 - t * t) * _GELU_C * (1.0 + 3 * 0.044715 * x * x)


def _sigmoid(x):
    return 1.0 / (1.0 + jnp.exp(-x))


def _swap_half(t, half):
    n = t.shape[-1]
    lane = lax.broadcasted_iota(jnp.int32, t.shape, t.ndim - 1)
    return jnp.where((lane % (2 * half)) < half, pltpu.roll(t, n - half, t.ndim - 1), pltpu.roll(t, half, t.ndim - 1))


def _rope(t, cos, sin, half):
    return t * cos + _swap_half(t, half) * sin


def _rope_bwd(d, cos, sin, half):
    return d * cos - _swap_half(d, half) * sin


def _tables(s):
    pos = jnp.arange(s, dtype=F32)[:, None]

    def cs(half):
        inv = jnp.power(10000.0, -jnp.arange(half, dtype=F32) / half)
        ang = pos * inv[None, :]
        return jnp.cos(ang), jnp.sin(ang)

    c32, s32 = cs(32)
    c16, s16 = cs(16)
    z = lambda w: jnp.zeros((s, w), F32)
    o = lambda w: jnp.ones((s, w), F32)
    t = {}
    t["b_cos"] = jnp.tile(jnp.concatenate([c32, c32], 1), (1, 4))
    t["b_sin"] = jnp.tile(jnp.concatenate([-s32, s32], 1), (1, 4))
    t["q_cos"] = jnp.tile(jnp.concatenate([o(64), c16, c16, z(32)], 1), (1, 4))
    t["q_sin"] = jnp.tile(jnp.concatenate([z(64), -s16, s16, z(32)], 1), (1, 4))
    t["k_cos"] = jnp.concatenate([z(32), c16, c16, z(64)], 1)
    t["k_sin"] = jnp.concatenate([z(32), -s16, s16, z(64)], 1)
    lg = jnp.log1p(-jnp.exp2(-5.0 - jnp.arange(N_HEADS, dtype=F32)))
    j = jnp.arange(CHUNK, dtype=F32)
    rel = j[:, None] - j[None, :]
    t["decay"] = jnp.where(rel[None] >= 0, jnp.exp(jnp.maximum(rel, 0.0)[None] * lg[:, None, None]), 0.0)

    def rows(e):
        return jnp.repeat(e.T, HEAD_DIM, axis=1)

    t["qw"] = rows(jnp.exp((j + 1.0)[None, :] * lg[:, None]))
    t["kw"] = rows(jnp.exp((CHUNK - 1 - j)[None, :] * lg[:, None]))
    t["kw2"] = rows(jnp.exp((CHUNK - j)[None, :] * lg[:, None]))
    t["qw0"] = rows(jnp.exp(j[None, :] * lg[:, None]))
    t["cd"] = jnp.repeat(jnp.exp(CHUNK * lg), HEAD_DIM)[None, :]
    e = np.zeros((128, 512), np.float32)
    for h in range(N_HEADS):
        for r in range(32):
            e[MISC_KR + r, 128 * h + 64 + r] = 1.0
    t["place"] = jnp.asarray(e)
    return t


def _norm_matmul(x, g, w, name):
    s, d = x.shape
    n = w.shape[1]
    tm, tn = min(512, s), 256

    def body(x_ref, g_ref, w_ref, z_ref, h_ref):
        @pl.when(pl.program_id(1) == 0)
        def _():
            h_ref[...] = _rms(x_ref[...], g_ref[...]).astype(h_ref.dtype)

        z_ref[...] = jnp.dot(h_ref[...], w_ref[...], preferred_element_type=F32)

    return pl.pallas_call(
        body, grid=(s // tm, n // tn),
        in_specs=[pl.BlockSpec((tm, d), lambda i, j: (i, 0)), pl.BlockSpec((1, d), lambda i, j: (0, 0)),
                  pl.BlockSpec((d, tn), lambda i, j: (0, j))],
        out_specs=[pl.BlockSpec((tm, tn), lambda i, j: (i, j)), pl.BlockSpec((tm, d), lambda i, j: (i, 0))],
        out_shape=[SDS((s, n), F32), SDS((s, d), _MXU)],
        compiler_params=_cp("parallel", "arbitrary"), name=name)(x, g, w)


def _mm_tn(a, b, name, *, a_fn=None, blocked=False, out_dtype=F32):
    k, m = a.shape
    n = b.shape[1]
    tm, tk = min(512, m), min(1024, k)
    tn = next(t for t in (512, 256, 128) if n % t == 0)
    assert m % tm == 0 and k % tk == 0
    nk = k // tk

    def body(a_ref, b_ref, o_ref, acc):
        kk = pl.program_id(2)

        @pl.when(kk == 0)
        def _():
            acc[...] = jnp.zeros_like(acc)

        av = a_ref[...]
        if a_fn is not None:
            av = a_fn(av.astype(F32))
        acc[...] += _dot_tn(av, b_ref[...])

        @pl.when(kk == nk - 1)
        def _():
            o_ref[...] = acc[...].reshape(o_ref.shape).astype(o_ref.dtype)

    if blocked:
        assert tn == 512
        out_spec = pl.BlockSpec((1, tm, tn), lambda i, j, kk: (j, i, 0))
        out_shape = SDS((n // tn, m, tn), out_dtype)
    else:
        out_spec = pl.BlockSpec((tm, tn), lambda i, j, kk: (i, j))
        out_shape = SDS((m, n), out_dtype)
    return pl.pallas_call(
        body, grid=(m // tm, n // tn, nk),
        in_specs=[pl.BlockSpec((tk, tm), lambda i, j, kk: (kk, i)), pl.BlockSpec((tk, tn), lambda i, j, kk: (kk, j))],
        out_specs=out_spec, out_shape=out_shape, scratch_shapes=[pltpu.VMEM((tm, tn), F32)],
        compiler_params=_cp("parallel", "parallel", "arbitrary"), name=name)(a, b)


def _sgu_parts(u_pre, v_pre, gain):
    u = _gelu(u_pre)
    v = _gelu(v_pre)
    vh, rs, vg = [], [], []
    for h in range(N_HEADS):
        sl = slice(HEAD_DIM * h, HEAD_DIM * (h + 1))
        a, r = _standardize(v[:, sl])
        vh.append(a)
        rs.append(r)
        vg.append(a * gain[:, sl])
    return u, vh, rs, vg


def _tril(w):
    r = lax.broadcasted_iota(jnp.int32, w.shape, 0)
    c = lax.broadcasted_iota(jnp.int32, w.shape, 1)
    return jnp.where(r >= c, w, 0.0)


def _sgu_fwd(z, gain, w_s, b_t, name):
    s = z.shape[0]
    tm = min(512, s)

    def body(u_ref, v_ref, g_ref, w_ref, b_ref, y_ref):
        u, _, _, vg = _sgu_parts(u_ref[...], v_ref[...], g_ref[...])
        for h in range(N_HEADS):
            sl = slice(HEAD_DIM * h, HEAD_DIM * (h + 1))
            wc = _tril(w_ref[h])
            for c in range(tm // CHUNK):
                r = slice(CHUNK * c, CHUNK * (c + 1))
                mixed = _dot(wc, vg[h][r]) + b_ref[:, h:h + 1]
                y_ref[r, sl] = u[r, sl] * mixed

    return pl.pallas_call(
        body, grid=(s // tm,),
        in_specs=[pl.BlockSpec((tm, GROUP), lambda i: (i, 0)), pl.BlockSpec((tm, GROUP), lambda i: (i, 1)),
                  pl.BlockSpec((1, GROUP), lambda i: (0, 0)), pl.BlockSpec((N_HEADS, CHUNK, CHUNK), lambda i: (0, 0, 0)),
                  pl.BlockSpec((CHUNK, 128), lambda i: (0, 0))],
        out_specs=pl.BlockSpec((tm, GROUP), lambda i: (i, 0)), out_shape=SDS((s, GROUP), F32),
        compiler_params=_cp("parallel"), name=name)(z, z, gain, w_s, b_t)


def _sgu_bwd(dy, z, gain, w_s, b_t, name):
    s = z.shape[0]
    tm = min(512, s)

    def body(dy_ref, u_ref, v_ref, g_ref, w_ref, b_ref, dz_ref, dg_ref, dw_ref, db_ref):
        @pl.when(pl.program_id(0) == 0)
        def _():
            dg_ref[...] = jnp.zeros_like(dg_ref)
            dw_ref[...] = jnp.zeros_like(dw_ref)
            db_ref[...] = jnp.zeros_like(db_ref)

        u_pre, v_pre, gain_v = u_ref[...], v_ref[...], g_ref[...]
        u, vh, rs, vg = _sgu_parts(u_pre, v_pre, gain_v)
        dyv = dy_ref[...]
        gu = _gelu_grad(u_pre)
        gv = _gelu_grad(v_pre)
        for h in range(N_HEADS):
            sl = slice(HEAD_DIM * h, HEAD_DIM * (h + 1))
            wc = _tril(w_ref[h])
            dwh = jnp.zeros((CHUNK, CHUNK), F32)
            dbh = jnp.zeros((CHUNK, 1), F32)
            dgh = jnp.zeros((1, HEAD_DIM), F32)
            for c in range(tm // CHUNK):
                r = slice(CHUNK * c, CHUNK * (c + 1))
                mixed = _dot(wc, vg[h][r]) + b_ref[:, h:h + 1]
                dz_ref[r, sl] = (dyv[r, sl] * mixed * gu[r, sl]).astype(dz_ref.dtype)
                dm = dyv[r, sl] * u[r, sl]
                dwh += _dot_nt(dm, vg[h][r])
                dbh += jnp.sum(dm, axis=1, keepdims=True)
                dvg = _dot_tn(wc, dm)
                dgh += jnp.sum(dvg * vh[h][r], axis=0, keepdims=True)
                dv = _standardize_bwd(vh[h][r], rs[h][r], dvg * gain_v[:, sl])
                dz_ref[r, GROUP + HEAD_DIM * h:GROUP + HEAD_DIM * (h + 1)] = (dv * gv[r, sl]).astype(dz_ref.dtype)
            dw_ref[h] += _tril(dwh)
            db_ref[:, h:h + 1] += dbh
            dg_ref[:, sl] += dgh

    return pl.pallas_call(
        body, grid=(s // tm,),
        in_specs=[pl.BlockSpec((tm, GROUP), lambda i: (i, 0)),
                  pl.BlockSpec((tm, GROUP), lambda i: (i, 0)), pl.BlockSpec((tm, GROUP), lambda i: (i, 1)),
                  pl.BlockSpec((1, GROUP), lambda i: (0, 0)), pl.BlockSpec((N_HEADS, CHUNK, CHUNK), lambda i: (0, 0, 0)),
                  pl.BlockSpec((CHUNK, 128), lambda i: (0, 0))],
        out_specs=[pl.BlockSpec((tm, 2 * GROUP), lambda i: (i, 0)), pl.BlockSpec((1, GROUP), lambda i: (0, 0)),
                   pl.BlockSpec((N_HEADS, CHUNK, CHUNK), lambda i: (0, 0, 0)), pl.BlockSpec((CHUNK, 128), lambda i: (0, 0))],
        out_shape=[SDS((s, 2 * GROUP), _MXU), SDS((1, GROUP), F32), SDS((N_HEADS, CHUNK, CHUNK), F32), SDS((CHUNK, 128), F32)],
        compiler_params=_cp("arbitrary"), name=name)(dy, z, z, gain, w_s, b_t)


_SCALE_B = HEAD_DIM ** -0.5


def _ret_fwd(z, tb, name):
    s = z.shape[0]
    nc = s // CHUNK
    row = lambda col: pl.BlockSpec((CHUNK, GROUP), lambda n, col=col: (n, col))
    const = lambda shape: pl.BlockSpec(shape, lambda n: (0,) * len(shape))

    def body(q_ref, k_ref, v_ref, g_ref, cos_ref, sin_ref, dec_ref, qw_ref, kw_ref, cd_ref, y_ref, o_ref, st_ref, state):
        @pl.when(pl.program_id(0) == 0)
        def _():
            state[...] = jnp.zeros_like(state)

        q = _rope(q_ref[...], cos_ref[...], sin_ref[...], 32)
        k = _rope(k_ref[...], cos_ref[...], sin_ref[...], 32) * _SCALE_B
        v = v_ref[...]
        g = g_ref[...]
        st_ref[0] = state[...]
        qs = q * qw_ref[...]
        ks = k * kw_ref[...]
        for h in range(N_HEADS):
            sl = slice(HEAD_DIM * h, HEAD_DIM * (h + 1))
            sc = _dot_nt(q[:, sl], k[:, sl]) * dec_ref[h]
            o = _dot(sc, v[:, sl]) + _dot(qs[:, sl], state[:, sl])
            o_ref[:, sl] = o
            yh, _ = _standardize(o)
            gh = g[:, sl]
            y_ref[:, sl] = gh * _sigmoid(gh) * yh
            state[:, sl] = cd_ref[:, sl] * state[:, sl] + _dot_tn(ks[:, sl], v[:, sl])

    return pl.pallas_call(
        body, grid=(nc,),
        in_specs=[row(2), row(3), row(4), row(5), pl.BlockSpec((CHUNK, GROUP), lambda n: (n, 0)),
                  pl.BlockSpec((CHUNK, GROUP), lambda n: (n, 0)), const((N_HEADS, CHUNK, CHUNK)),
                  const((CHUNK, GROUP)), const((CHUNK, GROUP)), const((1, GROUP))],
        out_specs=[pl.BlockSpec((CHUNK, GROUP), lambda n: (n, 0)), pl.BlockSpec((CHUNK, GROUP), lambda n: (n, 0)),
                   pl.BlockSpec((1, HEAD_DIM, GROUP), lambda n: (n, 0, 0))],
        out_shape=[SDS((s, GROUP), F32), SDS((s, GROUP), F32), SDS((nc, HEAD_DIM, GROUP), F32)],
        scratch_shapes=[pltpu.VMEM((HEAD_DIM, GROUP), F32)],
        compiler_params=_cp("arbitrary"), name=name)(z, z, z, z, tb["b_cos"], tb["b_sin"], tb["decay"], tb["qw"], tb["kw"], tb["cd"])


def _ret_bwd(dy, z, o_pre, states, tb, name):
    s = z.shape[0]
    nc = s // CHUNK
    rev = lambda col: pl.BlockSpec((CHUNK, GROUP), lambda n, col=col: (nc - 1 - n, col))
    const = lambda shape: pl.BlockSpec(shape, lambda n: (0,) * len(shape))

    def body(dy_ref, q_ref, k_ref, v_ref, g_ref, o_ref, st_ref, cos_ref, sin_ref, dec_ref, qw_ref, kw2_ref, qw0_ref, cd_ref,
             dz_ref, rstate):
        @pl.when(pl.program_id(0) == 0)
        def _():
            rstate[...] = jnp.zeros_like(rstate)

        cos, sin = cos_ref[...], sin_ref[...]
        q = _rope(q_ref[...], cos, sin, 32)
        k = _rope(k_ref[...], cos, sin, 32) * _SCALE_B
        v = v_ref[...]
        g = g_ref[...]
        dyv = dy_ref[...]
        sg = _sigmoid(g)
        silu = g * sg
        dos, dgs = [], []
        for h in range(N_HEADS):
            sl = slice(HEAD_DIM * h, HEAD_DIM * (h + 1))
            yh, rs = _standardize(o_ref[:, sl])
            dgs.append(dyv[:, sl] * yh * (sg[:, sl] * (1.0 + g[:, sl] * (1.0 - sg[:, sl]))))
            dos.append(_standardize_bwd(yh, rs, dyv[:, sl] * silu[:, sl]))
        do = jnp.concatenate(dos, axis=1)
        dow = do * qw_ref[...]
        vw = v * kw2_ref[...]
        kw = k * kw2_ref[...]
        q0 = q * qw0_ref[...]
        dqs, dks = [], []
        for h in range(N_HEADS):
            sl = slice(HEAD_DIM * h, HEAD_DIM * (h + 1))
            dec = dec_ref[h]
            p = _dot_nt(q[:, sl], k[:, sl]) * dec
            dp = _dot_nt(do[:, sl], v[:, sl]) * dec
            sn = st_ref[0][:, sl]
            rr = rstate[:, sl]
            dqs.append(_dot(dp, k[:, sl]) + _dot_nt(dow[:, sl], sn))
            dks.append(_dot_tn(dp, q[:, sl]) + _dot_nt(vw[:, sl], rr))
            dv = _dot_tn(p, do[:, sl]) + _dot(kw[:, sl], rr)
            dz_ref[:, 2 * GROUP + HEAD_DIM * h:2 * GROUP + HEAD_DIM * (h + 1)] = dv.astype(dz_ref.dtype)
            rstate[:, sl] = cd_ref[:, sl] * rr + _dot_tn(q0[:, sl], do[:, sl])
        dq = _rope_bwd(jnp.concatenate(dqs, axis=1), cos, sin, 32)
        dk = _rope_bwd(jnp.concatenate(dks, axis=1) * _SCALE_B, cos, sin, 32)
        dz_ref[:, 0:GROUP] = dq.astype(dz_ref.dtype)
        dz_ref[:, GROUP:2 * GROUP] = dk.astype(dz_ref.dtype)
        dz_ref[:, 3 * GROUP:4 * GROUP] = jnp.concatenate(dgs, axis=1).astype(dz_ref.dtype)

    r0 = lambda: pl.BlockSpec((CHUNK, GROUP), lambda n: (nc - 1 - n, 0))
    return pl.pallas_call(
        body, grid=(nc,),
        in_specs=[r0(), rev(2), rev(3), rev(4), rev(5), r0(), pl.BlockSpec((1, HEAD_DIM, GROUP), lambda n: (nc - 1 - n, 0, 0)),
                  r0(), r0(), const((N_HEADS, CHUNK, CHUNK)), const((CHUNK, GROUP)), const((CHUNK, GROUP)),
                  const((CHUNK, GROUP)), const((1, GROUP))],
        out_specs=pl.BlockSpec((CHUNK, 4 * GROUP), lambda n: (nc - 1 - n, 0)),
        out_shape=SDS((s, 4 * GROUP), _MXU), scratch_shapes=[pltpu.VMEM((HEAD_DIM, GROUP), F32)],
        compiler_params=_cp("arbitrary"), name=name)(
            dy, z, z, z, z, o_pre, states, tb["b_cos"], tb["b_sin"], tb["decay"], tb["qw"], tb["kw2"], tb["qw0"], tb["cd"])


TQ = 256


def _log_sigmoid(x):
    return jnp.minimum(x, 0.0) - jnp.log1p(jnp.exp(-jnp.abs(x)))


def _fox_prep(z, b_f, name):
    s = z.shape[0]
    nb = s // TQ

    def body(m_ref, b_ref, cc_ref, cr_ref, carry):
        @pl.when(pl.program_id(0) == 0)
        def _():
            carry[...] = jnp.zeros_like(carry)

        lane = lax.broadcasted_iota(jnp.int32, (TQ, 128), 1)
        logf = jnp.where(lane < N_HEADS, _log_sigmoid(m_ref[...] + b_ref[...]), 0.0)
        r = lax.broadcasted_iota(jnp.int32, (TQ, TQ), 0)
        c = lax.broadcasted_iota(jnp.int32, (TQ, TQ), 1)
        tri = jnp.where(r >= c, 1.0, 0.0).astype(F32)
        cum = _dot_exact(tri, logf) + carry[...]
        cc_ref[...] = cum
        cr_ref[0] = cum.T[0:8, :]
        carry[...] = cum[TQ - 1:TQ, :]

    return pl.pallas_call(
        body, grid=(nb,),
        in_specs=[pl.BlockSpec((TQ, 128), lambda i: (i, NZ // 128 - 1)), pl.BlockSpec((1, 128), lambda i: (0, 0))],
        out_specs=[pl.BlockSpec((TQ, 128), lambda i: (i, 0)), pl.BlockSpec((1, 8, TQ), lambda i: (i, 0, 0))],
        out_shape=[SDS((s, 128), F32), SDS((nb, 8, TQ), F32)], scratch_shapes=[pltpu.VMEM((1, 128), F32)],
        compiler_params=_cp("arbitrary"), name=name)(z, b_f)


def _fox_post(dcr, dcq, z, b_f, dkr, name):
    s = z.shape[0]
    nb = s // TQ

    def body(dc_ref, dcq_ref, m_ref, b_ref, dkr_ref, dz_ref, db_ref, carry):
        @pl.when(pl.program_id(0) == 0)
        def _():
            carry[...] = jnp.zeros_like(carry)
            db_ref[...] = jnp.zeros_like(db_ref)

        r = lax.broadcasted_iota(jnp.int32, (TQ, TQ), 0)
        c = lax.broadcasted_iota(jnp.int32, (TQ, TQ), 1)
        triu = jnp.where(c >= r, 1.0, 0.0).astype(F32)
        dc = jnp.concatenate([dc_ref[0], jnp.zeros((120, TQ), F32)], axis=0)
        dlogf = _dot_exact(triu, dc, (((1,), (1,)), ((), ()))) + _dot_exact(triu, dcq_ref[...]) + carry[...]
        carry[...] = dlogf[0:1, :]
        x = m_ref[...] + b_ref[...]
        lane = lax.broadcasted_iota(jnp.int32, (TQ, 128), 1)
        df = jnp.where(lane < N_HEADS, dlogf * _sigmoid(-x), 0.0)
        db_ref[...] += jnp.sum(df, axis=0, keepdims=True)
        dz_ref[...] = (df + dkr_ref[...]).astype(dz_ref.dtype)

    rv = lambda i: nb - 1 - i
    return pl.pallas_call(
        body, grid=(nb,),
        in_specs=[pl.BlockSpec((1, 8, TQ), lambda i: (rv(i), 0, 0)), pl.BlockSpec((TQ, 128), lambda i: (rv(i), 0)),
                  pl.BlockSpec((TQ, 128), lambda i: (rv(i), NZ // 128 - 1)),
                  pl.BlockSpec((1, 128), lambda i: (0, 0)), pl.BlockSpec((TQ, 128), lambda i: (rv(i), 0))],
        out_specs=[pl.BlockSpec((TQ, 128), lambda i: (rv(i), 0)), pl.BlockSpec((1, 128), lambda i: (0, 0))],
        out_shape=[SDS((s, 128), _MXU), SDS((1, 128), F32)], scratch_shapes=[pltpu.VMEM((1, 128), F32)],
        compiler_params=_cp("arbitrary"), name=name)(dcr, dcq, z, b_f, dkr)


NEG = -1e30


def _causal_mask(shape):
    r = lax.broadcasted_iota(jnp.int32, shape, 0)
    c = lax.broadcasted_iota(jnp.int32, shape, 1)
    return r >= c


def _flash_fwd(q, k, v, cols, dqk, scale, cum, name):
    s = q.shape[0]
    nq = s // TQ
    wq = N_HEADS * dqk
    bias = cum is not None

    def body(*refs):
        if bias:
            q_ref, k_ref, v_ref, cc_ref, cr_ref, o_ref, l_ref = refs
        else:
            q_ref, k_ref, v_ref, o_ref, l_ref = refs
        i = pl.program_id(0)
        l_ref[...] = jnp.zeros_like(l_ref)
        for h in range(N_HEADS):
            qh = q_ref[:, dqk * h:dqk * (h + 1)].astype(_MXU)
            cq = cc_ref[:, h:h + 1] if bias else None

            def step(j, carry, masked):
                m, l, acc = carry
                r0 = pl.multiple_of(j * TQ, TQ)
                kh = k_ref[pl.ds(r0, TQ), dqk * h:dqk * (h + 1)]
                vh = v_ref[pl.ds(r0, TQ), HEAD_DIM * h:HEAD_DIM * (h + 1)]
                sc = _dot_nt(qh, kh) * scale
                if bias:
                    sc = sc + (cq - cr_ref[j][h:h + 1, :])
                if masked:
                    sc = jnp.where(_causal_mask(sc.shape), sc, NEG)
                m_new = jnp.maximum(m, jnp.max(sc, axis=-1, keepdims=True))
                alpha = jnp.exp(m - m_new)
                p = jnp.exp(sc - m_new)
                return m_new, alpha * l + jnp.sum(p, axis=-1, keepdims=True), alpha * acc + _dot(p, vh)

            init = (jnp.full((TQ, 1), NEG, F32), jnp.zeros((TQ, 1), F32), jnp.zeros((TQ, HEAD_DIM), F32))
            carry = lax.fori_loop(0, i, functools.partial(step, masked=False), init)
            m, l, acc = step(i, carry, True)
            o_ref[:, HEAD_DIM * h:HEAD_DIM * (h + 1)] = acc / l
            l_ref[:, h:h + 1] = m + jnp.log(l)

    in_specs = [pl.BlockSpec((TQ, wq), lambda i: (i, cols[0])), pl.BlockSpec((s, wq), lambda i: (0, cols[1])),
                pl.BlockSpec((s, GROUP), lambda i: (0, cols[2]))]
    args = [q, k, v]
    if bias:
        in_specs += [pl.BlockSpec((TQ, 128), lambda i: (i, 0)), pl.BlockSpec((nq, 8, TQ), lambda i: (0, 0, 0))]
        args += list(cum)
    return pl.pallas_call(
        body, grid=(nq,), in_specs=in_specs,
        out_specs=[pl.BlockSpec((TQ, GROUP), lambda i: (i, 0)), pl.BlockSpec((TQ, 128), lambda i: (i, 0))],
        out_shape=[SDS((s, GROUP), F32), SDS((s, 128), F32)],
        compiler_params=_cp("parallel"), name=name)(*args)


def _flash_bwd(q, k, v, cols, dqk, scale, cum, do, lse, delta, name, kv_dtype):
    s = q.shape[0]
    nq = s // TQ
    wq = N_HEADS * dqk
    bias = cum is not None

    def body(*refs):
        if bias:
            q_ref, k_ref, v_ref, do_ref, l_ref, d_ref, cc_ref, cr_ref, dq_ref, dk_ref, dv_ref, dc_ref, dcq_ref = refs
        else:
            q_ref, k_ref, v_ref, do_ref, l_ref, d_ref, dq_ref, dk_ref, dv_ref = refs
        j = pl.program_id(0)

        @pl.when(j == 0)
        def _():
            dq_ref[...] = jnp.zeros_like(dq_ref)
            if bias:
                dcq_ref[...] = jnp.zeros_like(dcq_ref)

        if bias:
            dc_ref[...] = jnp.zeros_like(dc_ref)
        for h in range(N_HEADS):
            hq = slice(dqk * h, dqk * (h + 1))
            hv = slice(HEAD_DIM * h, HEAD_DIM * (h + 1))
            kh = k_ref[:, hq].astype(_MXU)
            vh = v_ref[:, hv].astype(_MXU)
            ck = cr_ref[0][h:h + 1, :] if bias else None

            def step(i, carry, masked):
                dk, dv, dc = carry
                r0 = pl.multiple_of(i * TQ, TQ)
                rows = pl.ds(r0, TQ)
                qh = q_ref[rows, hq].astype(_MXU)
                doh = do_ref[rows, hv].astype(_MXU)
                sc = _dot_nt(qh, kh) * scale
                if bias:
                    sc = sc + (cc_ref[rows, h:h + 1] - ck)
                p = jnp.exp(sc - l_ref[rows, h:h + 1])
                if masked:
                    p = jnp.where(_causal_mask(p.shape), p, 0.0)
                dv = dv + _dot_tn(p, doh)
                dp = _dot_nt(doh, vh)
                ds = p * (dp - d_ref[rows, h:h + 1])
                dk = dk + _dot_tn(ds, qh) * scale
                dq_ref[rows, hq] += _dot(ds, kh) * scale
                if bias:
                    dc = dc + jnp.sum(ds, axis=0, keepdims=True)
                    dcq_ref[rows, h:h + 1] += jnp.sum(ds, axis=1, keepdims=True)
                return dk, dv, dc

            init = (jnp.zeros((TQ, dqk), F32), jnp.zeros((TQ, HEAD_DIM), F32), jnp.zeros((1, TQ), F32))
            carry = step(j, init, True)
            dk, dv, dc = lax.fori_loop(j + 1, nq, functools.partial(step, masked=False), carry)
            dk_ref[:, hq] = dk.astype(dk_ref.dtype)
            dv_ref[:, hv] = dv.astype(dv_ref.dtype)
            if bias:
                dc_ref[0, h:h + 1, :] = -dc

    full = lambda w, c=0: pl.BlockSpec((s, w), lambda j, c=c: (0, c))
    in_specs = [full(wq, cols[0]), pl.BlockSpec((TQ, wq), lambda j: (j, cols[1])), pl.BlockSpec((TQ, GROUP), lambda j: (j, cols[2])),
                full(GROUP), full(128), full(128)]
    args = [q, k, v, do, lse, delta]
    out_specs = [full(wq), pl.BlockSpec((TQ, wq), lambda j: (j, 0)), pl.BlockSpec((TQ, GROUP), lambda j: (j, 0))]
    out_shape = [SDS((s, wq), F32), SDS((s, wq), kv_dtype), SDS((s, GROUP), kv_dtype)]
    if bias:
        in_specs += [full(128), pl.BlockSpec((1, 8, TQ), lambda j: (j, 0, 0))]
        args += list(cum)
        out_specs += [pl.BlockSpec((1, 8, TQ), lambda j: (j, 0, 0)), full(128)]
        out_shape += [SDS((nq, 8, TQ), F32), SDS((s, 128), F32)]
    return pl.pallas_call(body, grid=(nq,), in_specs=in_specs, out_specs=out_specs, out_shape=out_shape,
                          compiler_params=_cp("arbitrary"), name=name)(*args)


_SCALE_D = (64 + 32) ** -0.5
_COL_CQ, _COL_CKV, _COL_MISC = 2304 // 256, 2560 // 128, 2688 // 128


def _mla_prep(z, gq, gkv, wq, wk, wv, tb, name):
    s = z.shape[0]
    tm = min(512, s)
    row = lambda w, c: pl.BlockSpec((tm, w), lambda i, c=c: (i, c))
    const = lambda a: pl.BlockSpec(a.shape, lambda i: (0,) * a.ndim)

    def body(cq_ref, ckv_ref, m_ref, gq_ref, gkv_ref, wq_ref, wk_ref, wv_ref, e_ref, qc_ref, qs_ref, kc_ref, ks_ref,
             q_ref, k_ref, v_ref, cqn_ref, ckvn_ref):
        cqn = _rms(cq_ref[...], gq_ref[...]).astype(_MXU)
        ckvn = _rms(ckv_ref[...], gkv_ref[...]).astype(_MXU)
        cqn_ref[...] = cqn
        ckvn_ref[...] = ckvn
        q_ref[...] = _rope(_dot(cqn, wq_ref[...]), qc_ref[...], qs_ref[...], 16).astype(q_ref.dtype)
        kr = _rope(m_ref[...], kc_ref[...], ks_ref[...], 16)
        k_ref[...] = (_dot(ckvn, wk_ref[...]) + _dot(kr, e_ref[...])).astype(k_ref.dtype)
        v_ref[...] = _dot(ckvn, wv_ref[...]).astype(v_ref.dtype)

    e = tb["place"]
    return pl.pallas_call(
        body, grid=(s // tm,),
        in_specs=[row(256, _COL_CQ), row(128, _COL_CKV), row(128, _COL_MISC), const(gq), const(gkv), const(wq), const(wk),
                  const(wv), const(e), row(512, 0), row(512, 0), row(128, 0), row(128, 0)],
        out_specs=[row(512, 0), row(512, 0), row(256, 0), row(256, 0), row(128, 0)],
        out_shape=[SDS((s, 512), _MXU), SDS((s, 512), _MXU), SDS((s, 256), _MXU), SDS((s, 256), _MXU), SDS((s, 128), _MXU)],
        compiler_params=_cp("parallel"), name=name)(
            z, z, z, gq, gkv, wq, wk, wv, e, tb["q_cos"], tb["q_sin"], tb["k_cos"], tb["k_sin"])


def _mla_prep_bwd(dq, dk, dv, z, cqn, ckvn, gq, gkv, wq, wk, wv, tb, name):
    s = z.shape[0]
    tm = min(512, s)
    row = lambda w, c: pl.BlockSpec((tm, w), lambda i, c=c: (i, c))
    const = lambda a: pl.BlockSpec(a.shape, lambda i: (0,) * a.ndim)
    acc = lambda shape: pl.BlockSpec(shape, lambda i: (0, 0))

    def body(dq_ref, dk_ref, dv_ref, cq_ref, ckv_ref, cqn_ref, ckvn_ref, gq_ref, gkv_ref, wq_ref, wk_ref, wv_ref, e_ref,
             qc_ref, qs_ref, kc_ref, ks_ref, dcq_ref, dckv_ref, dkr_ref, dwq_ref, dwk_ref, dwv_ref, dgq_ref, dgkv_ref):
        @pl.when(pl.program_id(0) == 0)
        def _():
            for r in (dwq_ref, dwk_ref, dwv_ref, dgq_ref, dgkv_ref):
                r[...] = jnp.zeros_like(r)

        dqp = _rope_bwd(dq_ref[...], qc_ref[...], qs_ref[...], 16)
        dkd = dk_ref[...]
        dvd = dv_ref[...]
        dwq_ref[...] += _dot_tn(cqn_ref[...], dqp)
        dwk_ref[...] += _dot_tn(ckvn_ref[...], dkd)
        dwv_ref[...] += _dot_tn(ckvn_ref[...], dvd)
        dcq, dgq = _rms_bwd(cq_ref[...], gq_ref[...], _dot_nt(dqp, wq_ref[...]))
        dckv, dgkv = _rms_bwd(ckv_ref[...], gkv_ref[...], _dot_nt(dkd, wk_ref[...]) + _dot_nt(dvd, wv_ref[...]))
        dcq_ref[...] = dcq.astype(dcq_ref.dtype)
        dckv_ref[...] = dckv.astype(dckv_ref.dtype)
        dgq_ref[...] += dgq
        dgkv_ref[...] += dgkv
        dkr = _dot_exact(dkd, e_ref[...], (((1,), (1,)), ((), ())))
        dkr_ref[...] = _rope_bwd(dkr, kc_ref[...], ks_ref[...], 16)

    e = tb["place"]
    return pl.pallas_call(
        body, grid=(s // tm,),
        in_specs=[row(512, 0), row(512, 0), row(256, 0), row(256, _COL_CQ), row(128, _COL_CKV), row(256, 0), row(128, 0),
                  const(gq), const(gkv), const(wq), const(wk), const(wv), const(e), row(512, 0), row(512, 0), row(128, 0), row(128, 0)],
        out_specs=[row(256, 0), row(128, 0), row(128, 0), acc((256, 512)), acc((128, 512)), acc((128, 256)), acc((1, 256)),
                   acc((1, 128))],
        out_shape=[SDS((s, 256), _MXU), SDS((s, 128), _MXU), SDS((s, 128), F32), SDS((256, 512), F32), SDS((128, 512), F32),
                   SDS((128, 256), F32), SDS((1, 256), F32), SDS((1, 128), F32)],
        compiler_params=_cp("arbitrary"), name=name)(
            dq, dk, dv, z, z, cqn, ckvn, gq, gkv, wq, wk, wv, e, tb["q_cos"], tb["q_sin"], tb["k_cos"], tb["k_sin"])


def _out_proj(ys, g, w, x, name):
    s, d = x.shape
    tm = min(512, s)

    def body(ya, yb, yc, yd, g_ref, w_ref, x_ref, o_ref, yn_ref):
        acc = x_ref[...]
        for i, y_ref in enumerate((ya, yb, yc, yd)):
            sl = slice(GROUP * i, GROUP * (i + 1))
            yn = _rms(y_ref[...], g_ref[:, sl]).astype(_MXU)
            yn_ref[:, sl] = yn
            acc = acc + jnp.dot(yn, w_ref[sl, :], preferred_element_type=F32)
        o_ref[...] = acc

    yspec = pl.BlockSpec((tm, GROUP), lambda i: (i, 0))
    return pl.pallas_call(
        body, grid=(s // tm,),
        in_specs=[yspec, yspec, yspec, yspec, pl.BlockSpec((1, d), lambda i: (0, 0)), pl.BlockSpec((d, d), lambda i: (0, 0)),
                  pl.BlockSpec((tm, d), lambda i: (i, 0))],
        out_specs=[pl.BlockSpec((tm, d), lambda i: (i, 0)), pl.BlockSpec((tm, d), lambda i: (i, 0))],
        out_shape=[SDS((s, d), F32), SDS((s, d), _MXU)], compiler_params=_cp("parallel"), name=name)(*ys, g, w, x)


def _out_proj_bwd(dx, w, ys, g, name):
    s, d = dx.shape
    tm = min(512, s)

    def body(dx_ref, w_ref, ya, yb, yc, yd, g_ref, da, db, dc, dd, dg_ref, dlc_ref, dld_ref):
        @pl.when(pl.program_id(0) == 0)
        def _():
            dg_ref[...] = jnp.zeros_like(dg_ref)

        dyn = _dot_nt(dx_ref[...], w_ref[...])
        outs = (da, db, dc, dd)
        for i, y_ref in enumerate((ya, yb, yc, yd)):
            sl = slice(GROUP * i, GROUP * (i + 1))
            y = y_ref[...]
            dy, dg = _rms_bwd(y, g_ref[:, sl], dyn[:, sl])
            outs[i][...] = dy
            dg_ref[:, sl] += dg
            if i >= 2:
                dl_ref = dlc_ref if i == 2 else dld_ref
                dl_ref[...] = jnp.zeros_like(dl_ref)
                for h in range(N_HEADS):
                    hs = slice(HEAD_DIM * h, HEAD_DIM * (h + 1))
                    dl_ref[:, h:h + 1] = jnp.sum(dy[:, hs] * y[:, hs], axis=1, keepdims=True)

    yspec = pl.BlockSpec((tm, GROUP), lambda i: (i, 0))
    lspec = pl.BlockSpec((tm, 128), lambda i: (i, 0))
    return pl.pallas_call(
        body, grid=(s // tm,),
        in_specs=[pl.BlockSpec((tm, d), lambda i: (i, 0)), pl.BlockSpec((d, d), lambda i: (0, 0)), yspec, yspec, yspec, yspec,
                  pl.BlockSpec((1, d), lambda i: (0, 0))],
        out_specs=[yspec, yspec, yspec, yspec, pl.BlockSpec((1, d), lambda i: (0, 0)), lspec, lspec],
        out_shape=[SDS((s, GROUP), F32)] * 4 + [SDS((1, d), F32), SDS((s, 128), F32), SDS((s, 128), F32)],
        compiler_params=_cp("arbitrary"), name=name)(dx, w, *ys, g)


FF_BLOCK = 512


def _ffn_fwd(x, g, wu, wd, name):
    s, d = x.shape
    nj = wu.shape[0]
    tm = min(512, s)

    def body(x_ref, g_ref, wu_ref, wd_ref, o_ref, u_ref, h_ref, acc):
        j = pl.program_id(1)

        @pl.when(j == 0)
        def _():
            h_ref[...] = _rms(x_ref[...], g_ref[...]).astype(h_ref.dtype)
            acc[...] = jnp.zeros_like(acc)

        u = jnp.dot(h_ref[...], wu_ref[0], preferred_element_type=F32)
        u_ref[...] = u.astype(u_ref.dtype)
        acc[...] += _dot(jnp.square(jnp.maximum(u, 0.0)), wd_ref[...])

        @pl.when(j == nj - 1)
        def _():
            o_ref[...] = x_ref[...] + acc[...]

    return pl.pallas_call(
        body, grid=(s // tm, nj),
        in_specs=[pl.BlockSpec((tm, d), lambda i, j: (i, 0)), pl.BlockSpec((1, d), lambda i, j: (0, 0)),
                  pl.BlockSpec((1, d, FF_BLOCK), lambda i, j: (j, 0, 0)), pl.BlockSpec((FF_BLOCK, d), lambda i, j: (j, 0))],
        out_specs=[pl.BlockSpec((tm, d), lambda i, j: (i, 0)), pl.BlockSpec((tm, FF_BLOCK), lambda i, j: (i, j)),
                   pl.BlockSpec((tm, d), lambda i, j: (i, 0))],
        out_shape=[SDS((s, d), F32), SDS((s, nj * FF_BLOCK), _MXU), SDS((s, d), _MXU)],
        scratch_shapes=[pltpu.VMEM((tm, d), F32)], compiler_params=_cp("parallel", "arbitrary"), name=name)(x, g, wu, wd)


def _ffn_bwd(dx2, x, u, g, wu, wd, name):
    s, d = x.shape
    nj = wu.shape[0]
    tm = min(512, s)

    def body(dx_ref, x_ref, u_ref, g_ref, wu_ref, wd_ref, o_ref, du_ref, dg_ref, acc, dxb):
        i, j = pl.program_id(0), pl.program_id(1)

        @pl.when((i == 0) & (j == 0))
        def _():
            dg_ref[...] = jnp.zeros_like(dg_ref)

        @pl.when(j == 0)
        def _():
            dxb[...] = dx_ref[...].astype(dxb.dtype)
            acc[...] = jnp.zeros_like(acc)

        da = lax.dot_general(dxb[...], wd_ref[...], (((1,), (1,)), ((), ())), preferred_element_type=F32)
        du = (da * 2.0 * jnp.maximum(u_ref[...].astype(F32), 0.0)).astype(du_ref.dtype)
        du_ref[...] = du
        acc[...] += lax.dot_general(du, wu_ref[0], (((1,), (1,)), ((), ())), preferred_element_type=F32)

        @pl.when(j == nj - 1)
        def _():
            dxn, dg = _rms_bwd(x_ref[...], g_ref[...], acc[...])
            o_ref[...] = dx_ref[...] + dxn
            dg_ref[...] += dg

    return pl.pallas_call(
        body, grid=(s // tm, nj),
        in_specs=[pl.BlockSpec((tm, d), lambda i, j: (i, 0)), pl.BlockSpec((tm, d), lambda i, j: (i, 0)),
                  pl.BlockSpec((tm, FF_BLOCK), lambda i, j: (i, j)), pl.BlockSpec((1, d), lambda i, j: (0, 0)),
                  pl.BlockSpec((1, d, FF_BLOCK), lambda i, j: (j, 0, 0)), pl.BlockSpec((FF_BLOCK, d), lambda i, j: (j, 0))],
        out_specs=[pl.BlockSpec((tm, d), lambda i, j: (i, 0)), pl.BlockSpec((tm, FF_BLOCK), lambda i, j: (i, j)),
                   pl.BlockSpec((1, d), lambda i, j: (0, 0))],
        out_shape=[SDS((s, d), F32), SDS((s, nj * FF_BLOCK), _MXU), SDS((1, d), F32)],
        scratch_shapes=[pltpu.VMEM((tm, d), F32), pltpu.VMEM((tm, d), _MXU)],
        compiler_params=_cp("arbitrary", "arbitrary"), name=name)(dx2, x, u, g, wu, wd)


def _in_proj_bwd(dz, w, x, g, dx_up, name):
    s, d = x.shape
    n = w.shape[1]
    tm = min(512, s)

    def body(dz_ref, w_ref, x_ref, g_ref, up_ref, o_ref, dg_ref):
        @pl.when(pl.program_id(0) == 0)
        def _():
            dg_ref[...] = jnp.zeros_like(dg_ref)

        dh = lax.dot_general(dz_ref[...], w_ref[...], (((1,), (1,)), ((), ())), preferred_element_type=F32)
        dxn, dg = _rms_bwd(x_ref[...], g_ref[...], dh)
        o_ref[...] = up_ref[...] + dxn
        dg_ref[...] += dg

    return pl.pallas_call(
        body, grid=(s // tm,),
        in_specs=[pl.BlockSpec((tm, n), lambda i: (i, 0)), pl.BlockSpec((d, n), lambda i: (0, 0)),
                  pl.BlockSpec((tm, d), lambda i: (i, 0)), pl.BlockSpec((1, d), lambda i: (0, 0)),
                  pl.BlockSpec((tm, d), lambda i: (i, 0))],
        out_specs=[pl.BlockSpec((tm, d), lambda i: (i, 0)), pl.BlockSpec((1, d), lambda i: (0, 0))],
        out_shape=[SDS((s, d), F32), SDS((1, d), F32)], compiler_params=_cp("arbitrary"), name=name)(dz, w, x, g, dx_up)


def _loss_head(x, g, target, name):
    s, d = x.shape
    tm = min(512, s)

    def body(x_ref, g_ref, t_ref, l_ref, dx_ref, dg_ref):
        @pl.when(pl.program_id(0) == 0)
        def _():
            l_ref[...] = jnp.zeros_like(l_ref)
            dg_ref[...] = jnp.zeros_like(dg_ref)

        xv = x_ref[...]
        err = _rms(xv, g_ref[...]) - t_ref[...]
        l_ref[...] += jnp.sum(err * err, axis=0, keepdims=True) * (0.5 / d)
        dx, dg = _rms_bwd(xv, g_ref[...], err * (1.0 / d))
        dx_ref[...] = dx
        dg_ref[...] += dg

    return pl.pallas_call(
        body, grid=(s // tm,),
        in_specs=[pl.BlockSpec((tm, d), lambda i: (i, 0)), pl.BlockSpec((1, d), lambda i: (0, 0)),
                  pl.BlockSpec((tm, d), lambda i: (i, 0))],
        out_specs=[pl.BlockSpec((1, d), lambda i: (0, 0)), pl.BlockSpec((tm, d), lambda i: (i, 0)),
                   pl.BlockSpec((1, d), lambda i: (0, 0))],
        out_shape=[SDS((1, d), F32), SDS((s, d), F32), SDS((1, d), F32)], compiler_params=_cp("arbitrary"), name=name)(x, g, target)


def _me_and_peer():
    x, y, c = lax.axis_index("x"), lax.axis_index("y"), lax.axis_index("c")
    me = 4 * x + 2 * y + c

    def peer(k):
        px, py, pc = x ^ (k >> 2), y ^ ((k >> 1) & 1), c ^ (k & 1)
        return (px, py, pc), 4 * px + 2 * py + pc

    return me, peer


def _all_gather(arrs, name):
    n = len(arrs)

    def body(*refs):
        ins, outs, (send, recv, loc) = refs[:n], refs[n:2 * n], refs[2 * n:]
        me, peer = _me_and_peer()
        copies = []
        for a in range(n):
            lc = pltpu.make_async_copy(ins[a], outs[a].at[me], loc.at[a])
            lc.start()
            copies.append(lc)
        rdma = []
        for k in range(1, NDEV):
            dev, _ = peer(k)
            for a in range(n):
                cp = pltpu.make_async_remote_copy(src_ref=ins[a], dst_ref=outs[a].at[me], send_sem=send.at[a, k - 1],
                                                  recv_sem=recv.at[a, k - 1], device_id=dev, device_id_type=MESH)
                cp.start()
                rdma.append((cp, a, k))
        for cp, a, k in rdma:
            _, pid = peer(k)
            pltpu.make_async_remote_copy(src_ref=ins[a], dst_ref=outs[a].at[pid], send_sem=send.at[a, k - 1],
                                         recv_sem=recv.at[a, k - 1], device_id=peer(k)[0], device_id_type=MESH).wait_recv()
        for cp, a, k in rdma:
            cp.wait_send()
        for lc in copies:
            lc.wait()

    anyspec = pl.BlockSpec(memory_space=pl.ANY)
    return pl.pallas_call(
        body, in_specs=[anyspec] * n, out_specs=[anyspec] * n,
        out_shape=[SDS((NDEV,) + a.shape, a.dtype) for a in arrs],
        scratch_shapes=[pltpu.SemaphoreType.DMA((n, NDEV - 1)), pltpu.SemaphoreType.DMA((n, NDEV - 1)),
                        pltpu.SemaphoreType.DMA((n,))],
        compiler_params=pltpu.CompilerParams(has_side_effects=True), name=name)(*arrs)


def _exchange(arrs, name):
    n = len(arrs)

    def body(*refs):
        ins, outs, (send, recv, loc) = refs[:n], refs[n:2 * n], refs[2 * n:]
        me, peer = _me_and_peer()
        copies = []
        for a in range(n):
            lc = pltpu.make_async_copy(ins[a].at[me], outs[a].at[me], loc.at[a])
            lc.start()
            copies.append(lc)
        rdma = []
        for k in range(1, NDEV):
            dev, pid = peer(k)
            for a in range(n):
                cp = pltpu.make_async_remote_copy(src_ref=ins[a].at[pid], dst_ref=outs[a].at[me], send_sem=send.at[a, k - 1],
                                                  recv_sem=recv.at[a, k - 1], device_id=dev, device_id_type=MESH)
                cp.start()
                rdma.append((cp, a, k))
        for cp, a, k in rdma:
            dev, pid = peer(k)
            pltpu.make_async_remote_copy(src_ref=ins[a].at[me], dst_ref=outs[a].at[pid], send_sem=send.at[a, k - 1],
                                         recv_sem=recv.at[a, k - 1], device_id=dev, device_id_type=MESH).wait_recv()
        for cp, a, k in rdma:
            cp.wait_send()
        for lc in copies:
            lc.wait()

    anyspec = pl.BlockSpec(memory_space=pl.ANY)
    return pl.pallas_call(
        body, in_specs=[anyspec] * n, out_specs=[anyspec] * n,
        out_shape=[SDS(a.shape, a.dtype) for a in arrs],
        scratch_shapes=[pltpu.SemaphoreType.DMA((n, NDEV - 1)), pltpu.SemaphoreType.DMA((n, NDEV - 1)),
                        pltpu.SemaphoreType.DMA((n,))],
        compiler_params=pltpu.CompilerParams(has_side_effects=True), name=name)(*arrs)


def _sum_slots(parts, name):
    _, r, c = parts.shape
    tr = r if r <= 512 else 512

    def body(p_ref, o_ref):
        acc = p_ref[0].astype(F32)
        for q in range(1, NDEV):
            acc = acc + p_ref[q].astype(F32)
        o_ref[...] = acc

    return pl.pallas_call(
        body, grid=(r // tr,), in_specs=[pl.BlockSpec((NDEV, tr, c), lambda i: (0, i, 0))],
        out_specs=pl.BlockSpec((tr, c), lambda i: (i, 0)), out_shape=SDS((r, c), F32),
        compiler_params=_cp("parallel"), name=name)(parts)


def _adamw(g, w, m, v, name):
    r, c = w.shape
    parts = g.ndim == 3
    tr = r
    for cand in (512, 256, 128, 64, 32, 16, 8):
        if r > cand and r % cand == 0 and cand * c * 4 <= 2 * 1024 * 1024:
            tr = cand
            break
    bc1 = 1.0 / (1.0 - ADAM_B1 ** ADAM_STEP)
    bc2 = 1.0 / (1.0 - ADAM_B2 ** ADAM_STEP)

    def body(g_ref, w_ref, m_ref, v_ref, go_ref, d_ref, mo_ref, vo_ref):
        if parts:
            gv = g_ref[0].astype(F32)
            for q in range(1, NDEV):
                gv = gv + g_ref[q].astype(F32)
        else:
            gv = g_ref[...]
        mn = ADAM_B1 * m_ref[...] + (1.0 - ADAM_B1) * gv
        vn = ADAM_B2 * v_ref[...] + (1.0 - ADAM_B2) * (gv * gv)
        go_ref[...] = gv
        mo_ref[...] = mn
        vo_ref[...] = vn
        d_ref[...] = -ADAM_LR * ((mn * bc1) / (jnp.sqrt(vn * bc2) + ADAM_EPS) + ADAM_WD * w_ref[...])

    spec = pl.BlockSpec((tr, c), lambda i: (i, 0))
    gspec = pl.BlockSpec((NDEV, tr, c), lambda i: (0, i, 0)) if parts else spec
    return pl.pallas_call(
        body, grid=(r // tr,), in_specs=[gspec, spec, spec, spec], out_specs=[spec] * 4,
        out_shape=[SDS((r, c), F32)] * 4, compiler_params=_cp("parallel"), name=name)(g, w, m, v)


def _pad_in_cols(w):
    r = w.shape[0]
    zeros = lambda n: jnp.zeros((r, n), w.dtype)
    return jnp.concatenate([w[:, :2304], w[:, 2308:2692], w[:, 2304:2308], zeros(28), w[:, 2692:2724], zeros(64)], axis=1)


def _unpad_in_cols(w):
    return jnp.concatenate([w[..., :2304], w[..., 2688:2692], w[..., 2304:2688], w[..., 2720:2752]], axis=-1)


def _pad_uq(w):
    return jnp.pad(w.reshape(256, N_HEADS, 96), ((0, 0), (0, 0), (0, 32))).reshape(256, 512)


def _unpad_uq(w):
    return w.reshape(256, N_HEADS, 128)[:, :, :96].reshape(256, 384)


def _split_ukv(w):
    r = w.reshape(128, N_HEADS, 128)
    return jnp.pad(r[:, :, :64], ((0, 0), (0, 0), (0, 64))).reshape(128, 512), r[:, :, 64:].reshape(128, 256)


def _join_ukv(dk, dv):
    return jnp.concatenate([dk.reshape(128, N_HEADS, 128)[:, :, :64], dv.reshape(128, N_HEADS, 64)], axis=-1).reshape(128, 512)


def _cols_to_full(g):
    return jnp.transpose(g, (1, 0, 2)).reshape(g.shape[1], NDEV * g.shape[2])


def kernel(x, g_mix_norm, w_in, b_forget, g_sgu, w_spatial, b_spatial, g_mla_q, w_uq, g_mla_kv, w_ukv, g_group_out, w_out, g_ffn_norm, w_up, w_down, g_final, loss_target, m_g_mix_norm, m_w_in, m_b_forget, m_g_sgu, m_w_spatial, m_b_spatial, m_g_mla_q, m_w_uq, m_g_mla_kv, m_w_ukv, m_g_group_out, m_w_out, m_g_ffn_norm, m_w_up, m_w_down, m_g_final, v_g_mix_norm, v_w_in, v_b_forget, v_g_sgu, v_w_spatial, v_b_spatial, v_g_mla_q, v_w_uq, v_g_mla_kv, v_w_ukv, v_g_group_out, v_w_out, v_g_ffn_norm, v_w_up, v_w_down, v_g_final):
    depth = w_in.shape[0]
    s, d = x.shape[1], x.shape[2]
    x0 = x.reshape(s, d)
    target = loss_target.reshape(s, d)
    tb = _tables(s)
    me = 4 * lax.axis_index("x") + 2 * lax.axis_index("y") + lax.axis_index("c")

    shards = []
    for l in range(depth):
        shards += [_pad_in_cols(w_in[l]).astype(_WIRE), w_out[l].astype(_WIRE), w_up[l].astype(_WIRE),
                   w_down[l].astype(_WIRE), w_uq[l].astype(_WIRE), w_ukv[l].astype(_WIRE)]
    gathered = _all_gather(shards, "gather_weights")
    wts = []
    for l in range(depth):
        gi, go, gu, gd, gq, gkv = gathered[6 * l:6 * l + 6]
        wk, wv = _split_ukv(_cols_to_full(gkv))
        wts.append(dict(w_in=gi.reshape(d, NZ), w_out=go.reshape(d, d), w_up=gu, w_down=gd.reshape(NDEV * gd.shape[1], d),
                        wq=_pad_uq(_cols_to_full(gq)), wk=wk, wv=wv))

    row = lambda a: a.reshape(1, -1)

    def small(l):
        bf = jnp.pad(b_forget[l].reshape(1, N_HEADS), ((0, 0), (0, 128 - N_HEADS)))
        bt = jnp.pad(b_spatial[l].T, ((0, 0), (0, 128 - N_HEADS)))
        return dict(g_mix=row(g_mix_norm[l]), g_sgu=row(g_sgu[l]), w_s=w_spatial[l], b_t=bt, b_f=bf, gq=row(g_mla_q[l]),
                    gkv=row(g_mla_kv[l]), g_go=row(g_group_out[l]), g_ffn=row(g_ffn_norm[l]))

    smalls = [small(l) for l in range(depth)]
    lrow, dx, big, sm, dg_final = _local_step(x0, target, wts, smalls, row(g_final), tb)
    loss = lax.psum(jnp.sum(lrow), AXES)
    grad_x = dx.reshape(1, s, d)
    return _reduce_and_update(loss, grad_x, big, sm, dg_final, me, dict(
        g_mix_norm=(g_mix_norm, m_g_mix_norm, v_g_mix_norm), w_in=(w_in, m_w_in, v_w_in),
        b_forget=(b_forget, m_b_forget, v_b_forget), g_sgu=(g_sgu, m_g_sgu, v_g_sgu),
        w_spatial=(w_spatial, m_w_spatial, v_w_spatial), b_spatial=(b_spatial, m_b_spatial, v_b_spatial),
        g_mla_q=(g_mla_q, m_g_mla_q, v_g_mla_q), w_uq=(w_uq, m_w_uq, v_w_uq), g_mla_kv=(g_mla_kv, m_g_mla_kv, v_g_mla_kv),
        w_ukv=(w_ukv, m_w_ukv, v_w_ukv), g_group_out=(g_group_out, m_g_group_out, v_g_group_out),
        w_out=(w_out, m_w_out, v_w_out), g_ffn_norm=(g_ffn_norm, m_g_ffn_norm, v_g_ffn_norm), w_up=(w_up, m_w_up, v_w_up),
        w_down=(w_down, m_w_down, v_w_down), g_final=(g_final, m_g_final, v_g_final)))


def _local_step(x0, target, wts, smalls, g_final, tb):
    depth = len(wts)
    s, d = x0.shape
    saved = []
    xl = x0
    for l in range(depth):
        w, p = wts[l], smalls[l]
        z, h = _norm_matmul(xl, p["g_mix"], w["w_in"], f"in_proj{l}")
        ya = _sgu_fwd(z, p["g_sgu"], p["w_s"], p["b_t"], f"sgu_fwd{l}")
        yb, ret, states = _ret_fwd(z, tb, f"ret_fwd{l}")
        cum = _fox_prep(z, p["b_f"], f"fox_prep{l}")
        yc, lse_c = _flash_fwd(z, z, z, (6, 7, 8), HEAD_DIM, HEAD_DIM ** -0.5, cum, f"fox_attn{l}")
        qd, kd, vd, cqn, ckvn = _mla_prep(z, p["gq"], p["gkv"], w["wq"], w["wk"], w["wv"], tb, f"mla_prep{l}")
        yd, lse_d = _flash_fwd(qd, kd, vd, (0, 0, 0), 128, _SCALE_D, None, f"mla_attn{l}")
        ys = (ya, yb, yc, yd)
        x1, yn = _out_proj(ys, p["g_go"], w["w_out"], xl, f"out_proj{l}")
        x2, u, h2 = _ffn_fwd(x1, p["g_ffn"], w["w_up"], w["w_down"], f"ffn_fwd{l}")
        saved.append(dict(x=xl, z=z, h=h, ys=ys, ret=ret, states=states, cum=cum, lse_c=lse_c, qd=qd, kd=kd, vd=vd, cqn=cqn,
                          ckvn=ckvn, lse_d=lse_d, x1=x1, yn=yn, u=u, h2=h2))
        xl = x2

    lrow, dx, dg_final = _loss_head(xl, g_final, target, "loss_head")

    big = [None] * depth
    sm = [None] * depth
    for l in reversed(range(depth)):
        w, p, a = wts[l], smalls[l], saved[l]
        dx1, du, dg_ffn = _ffn_bwd(dx, a["x1"], a["u"], p["g_ffn"], w["w_up"], w["w_down"], f"ffn_bwd{l}")
        dw_down = _mm_tn(a["u"], dx, f"dw_down{l}", a_fn=lambda t: jnp.square(jnp.maximum(t, 0.0)), out_dtype=_WIRE)
        dw_up = _mm_tn(a["h2"], du, f"dw_up{l}", blocked=True, out_dtype=_WIRE)
        dya, dyb, dyc, dyd, dg_go, dl_c, dl_d = _out_proj_bwd(dx1, w["w_out"], a["ys"], p["g_go"], f"out_proj_bwd{l}")
        dw_out = _mm_tn(a["yn"], dx1, f"dw_out{l}", out_dtype=_WIRE)
        dz_a, dg_sgu, dw_s, db_t = _sgu_bwd(dya, a["z"], p["g_sgu"], p["w_s"], p["b_t"], f"sgu_bwd{l}")
        dz_b = _ret_bwd(dyb, a["z"], a["ret"], a["states"], tb, f"ret_bwd{l}")
        dq_c, dk_c, dv_c, dcr, dcq = _flash_bwd(a["z"], a["z"], a["z"], (6, 7, 8), HEAD_DIM, HEAD_DIM ** -0.5, a["cum"], dyc,
                                           a["lse_c"], dl_c, f"fox_attn_bwd{l}", _MXU)
        dq_d, dk_d, dv_d = _flash_bwd(a["qd"], a["kd"], a["vd"], (0, 0, 0), 128, _SCALE_D, None, dyd, a["lse_d"], dl_d,
                                      f"mla_attn_bwd{l}", F32)
        dz_cq, dz_ckv, dkr, dwq, dwk, dwv, dgq, dgkv = _mla_prep_bwd(dq_d, dk_d, dv_d, a["z"], a["cqn"], a["ckvn"], p["gq"],
                                                                     p["gkv"], w["wq"], w["wk"], w["wv"], tb, f"mla_prep_bwd{l}")
        dz_misc, db_f = _fox_post(dcr, dcq, a["z"], p["b_f"], dkr, f"fox_post{l}")
        dz = jnp.concatenate([dz_a, dz_b, dq_c.astype(_MXU), dk_c, dv_c, dz_cq, dz_ckv, dz_misc], axis=1)
        dx, dg_mix = _in_proj_bwd(dz, w["w_in"], a["x"], p["g_mix"], dx1, f"in_proj_bwd{l}")
        dw_in = _mm_tn(a["h"], dz, f"dw_in{l}", out_dtype=_WIRE)
        big[l] = [_unpad_in_cols(dw_in).reshape(NDEV, d // NDEV, N_IN), dw_out.reshape(NDEV, d // NDEV, d), dw_up,
                  dw_down.reshape(NDEV, dw_down.shape[0] // NDEV, d)]
        sm[l] = [dg_mix, dg_go, dg_ffn, dg_sgu, dw_s, db_t[:, :N_HEADS].T, db_f[0, :N_HEADS], dgq, dgkv, _unpad_uq(dwq),
                 _join_ukv(dwk, dwv)]
    return lrow, dx, big, sm, dg_final


def _reduce_and_update(loss, grad_x, big, sm, dg_final, me, given):
    depth = len(big)
    recv = _exchange([t for l in range(depth) for t in big[l]], "scatter_grads")
    pieces = [t for l in range(depth) for t in sm[l]] + [dg_final]
    flat = jnp.concatenate([t.reshape(-1) for t in pieces])
    n_flat = flat.shape[0]
    unit = NDEV * 8 * 128
    n_pad = -(-n_flat // unit) * unit
    packed = jnp.pad(flat, (0, n_pad - n_flat)).reshape(NDEV, n_pad // (NDEV * 128), 128)
    red = _sum_slots(_exchange([packed], "scatter_small")[0], "sum_small")
    full = _all_gather([red], "gather_small")[0].reshape(-1)
    offs = np.cumsum([0] + [int(np.prod(t.shape)) for t in pieces])
    red_pieces = [full[int(offs[i]):int(offs[i + 1])].reshape(pieces[i].shape) for i in range(len(pieces))]
    per = len(sm[0])
    stack = lambda i: jnp.stack([red_pieces[l * per + i] for l in range(depth)])
    g_small = dict(g_mix_norm=stack(0), g_group_out=stack(1), g_ffn_norm=stack(2), g_sgu=stack(3), w_spatial=stack(4),
                   b_spatial=stack(5), b_forget=stack(6), g_mla_q=stack(7), g_mla_kv=stack(8), g_final=red_pieces[-1])
    cq, ckv = given["w_uq"][0].shape[2], given["w_ukv"][0].shape[2]
    g_small["w_uq"] = lax.dynamic_slice_in_dim(stack(9), me * cq, cq, axis=2)
    g_small["w_ukv"] = lax.dynamic_slice_in_dim(stack(10), me * ckv, ckv, axis=2)

    names = list(given)
    big_idx = dict(w_in=0, w_out=1, w_up=2, w_down=3)
    outs = {}
    for nme in names:
        wv_, mv_, vv_ = given[nme]
        shape = wv_.shape
        if nme in big_idx:
            res = []
            for l in range(depth):
                parts = recv[4 * l + big_idx[nme]]
                two = lambda t: t[l].reshape(-1, shape[-1])
                res.append(_adamw(parts, two(wv_), two(mv_), two(vv_), f"adamw_{nme}{l}"))
            outs[nme] = [jnp.stack([res[l][i] for l in range(depth)]).reshape(shape) for i in range(4)]
        else:
            two = lambda t: t.reshape(-1, shape[-1]) if t.ndim > 1 else t.reshape(1, -1)
            res = _adamw(two(g_small[nme]), two(wv_), two(mv_), two(vv_), f"adamw_{nme}")
            outs[nme] = [r.reshape(shape) for r in res]
    return (loss, grad_x, *[outs[n][0] for n in names], *[outs[n][1] for n in names], *[outs[n][2] for n in names],
            *[outs[n][3] for n in names])
```

```python
import functools

import jax
import jax.numpy as jnp
import numpy as np
from jax import lax
from jax.experimental import pallas as pl
from jax.experimental.pallas import tpu as pltpu

F32 = jnp.float32
_MXU = jnp.bfloat16
_WIRE = jnp.bfloat16
EPS = 1e-6
NDEV = 8
AXES = ("x", "y", "c")
MESH = pl.DeviceIdType.MESH

N_HEADS = 4
HEAD_DIM = 64
GROUP = 256
CHUNK = 128
NZ = 2816
N_IN = 2724
MISC_F, MISC_KR = 0, 32
VMEM_LIMIT = 56 * 1024 * 1024

ADAM_LR, ADAM_B1, ADAM_B2, ADAM_EPS, ADAM_WD, ADAM_STEP = 0.001, 0.9, 0.999, 1e-08, 0.01, 10

SDS = jax.ShapeDtypeStruct


def _cp(*sem):
    return pltpu.CompilerParams(dimension_semantics=sem, vmem_limit_bytes=VMEM_LIMIT)


def _dot(a, b):
    return jnp.dot(a.astype(_MXU), b.astype(_MXU), preferred_element_type=F32)


def _dot_nt(a, b):
    return lax.dot_general(a.astype(_MXU), b.astype(_MXU), (((1,), (1,)), ((), ())), preferred_element_type=F32)


def _dot_tn(a, b):
    return lax.dot_general(a.astype(_MXU), b.astype(_MXU), (((0,), (0,)), ((), ())), preferred_element_type=F32)


def _dot_exact(a, b, dims=(((1,), (0,)), ((), ()))):
    return lax.dot_general(a, b, dims, precision=lax.Precision.HIGHEST, preferred_element_type=F32)


def _rms(x, g):
    return x * lax.rsqrt(jnp.mean(x * x, axis=-1, keepdims=True) + EPS) * g


def _rms_bwd(x, g, dy):
    xh = x * lax.rsqrt(jnp.mean(x * x, axis=-1, keepdims=True) + EPS)
    dxh = dy * g
    r = lax.rsqrt(jnp.mean(x * x, axis=-1, keepdims=True) + EPS)
    dx = r * (dxh - xh * jnp.mean(dxh * xh, axis=-1, keepdims=True))
    return dx, jnp.sum(dy * xh, axis=0, keepdims=True)


def _standardize(t):
    mu = jnp.mean(t, axis=-1, keepdims=True)
    tc = t - mu
    rs = lax.rsqrt(jnp.mean(tc * tc, axis=-1, keepdims=True) + EPS)
    return tc * rs, rs


def _standardize_bwd(yh, rs, dy):
    return rs * (dy - jnp.mean(dy, axis=-1, keepdims=True) - yh * jnp.mean(dy * yh, axis=-1, keepdims=True))


_GELU_C = 0.7978845608028654


def _gelu(x):
    return 0.5 * x * (1.0 + jnp.tanh(_GELU_C * (x + 0.044715 * x * x * x)))


def _gelu_grad(x):
    t = jnp.tanh(_GELU_C * (x + 0.044715 * x * x * x))
    return 0.5 * (1.0 + t) + 0.5 * x * (1.0 - t * t) * _GELU_C * (1.0 + 3 * 0.044715 * x * x)


def _sigmoid(x):
    return 1.0 / (1.0 + jnp.exp(-x))


def _swap_half(t, half):
    n = t.shape[-1]
    lane = lax.broadcasted_iota(jnp.int32, t.shape, t.ndim - 1)
    return jnp.where((lane % (2 * half)) < half, pltpu.roll(t, n - half, t.ndim - 1), pltpu.roll(t, half, t.ndim - 1))


def _rope(t, cos, sin, half):
    return t * cos + _swap_half(t, half) * sin


def _rope_bwd(d, cos, sin, half):
    return d * cos - _swap_half(d, half) * sin


def _tables(s):
    pos = jnp.arange(s, dtype=F32)[:, None]

    def cs(half):
        inv = jnp.power(10000.0, -jnp.arange(half, dtype=F32) / half)
        ang = pos * inv[None, :]
        return jnp.cos(ang), jnp.sin(ang)

    c32, s32 = cs(32)
    c16, s16 = cs(16)
    z = lambda w: jnp.zeros((s, w), F32)
    o = lambda w: jnp.ones((s, w), F32)
    t = {}
    t["b_cos"] = jnp.tile(jnp.concatenate([c32, c32], 1), (1, 4))
    t["b_sin"] = jnp.tile(jnp.concatenate([-s32, s32], 1), (1, 4))
    t["q_cos"] = jnp.tile(jnp.concatenate([o(64), c16, c16, z(32)], 1), (1, 4))
    t["q_sin"] = jnp.tile(jnp.concatenate([z(64), -s16, s16, z(32)], 1), (1, 4))
    t["k_cos"] = jnp.concatenate([z(32), c16, c16, z(64)], 1)
    t["k_sin"] = jnp.concatenate([z(32), -s16, s16, z(64)], 1)
    lg = jnp.log1p(-jnp.exp2(-5.0 - jnp.arange(N_HEADS, dtype=F32)))
    j = jnp.arange(CHUNK, dtype=F32)
    rel = j[:, None] - j[None, :]
    t["decay"] = jnp.where(rel[None] >= 0, jnp.exp(jnp.maximum(rel, 0.0)[None] * lg[:, None, None]), 0.0)

    def rows(e):
        return jnp.repeat(e.T, HEAD_DIM, axis=1)

    t["qw"] = rows(jnp.exp((j + 1.0)[None, :] * lg[:, None]))
    t["kw"] = rows(jnp.exp((CHUNK - 1 - j)[None, :] * lg[:, None]))
    t["kw2"] = rows(jnp.exp((CHUNK - j)[None, :] * lg[:, None]))
    t["qw0"] = rows(jnp.exp(j[None, :] * lg[:, None]))
    t["cd"] = jnp.repeat(jnp.exp(CHUNK * lg), HEAD_DIM)[None, :]
    e = np.zeros((128, 512), np.float32)
    for h in range(N_HEADS):
        for r in range(32):
            e[MISC_KR + r, 128 * h + 64 + r] = 1.0
    t["place"] = jnp.asarray(e)
    return t


def _norm_matmul(x, g, w, name):
    s, d = x.shape
    n = w.shape[1]
    tm, tn = min(512, s), 256

    def body(x_ref, g_ref, w_ref, z_ref, h_ref):
        @pl.when(pl.program_id(1) == 0)
        def _():
            h_ref[...] = _rms(x_ref[...], g_ref[...]).astype(h_ref.dtype)

        z_ref[...] = jnp.dot(h_ref[...], w_ref[...], preferred_element_type=F32)

    return pl.pallas_call(
        body, grid=(s // tm, n // tn),
        in_specs=[pl.BlockSpec((tm, d), lambda i, j: (i, 0)), pl.BlockSpec((1, d), lambda i, j: (0, 0)),
                  pl.BlockSpec((d, tn), lambda i, j: (0, j))],
        out_specs=[pl.BlockSpec((tm, tn), lambda i, j: (i, j)), pl.BlockSpec((tm, d), lambda i, j: (i, 0))],
        out_shape=[SDS((s, n), F32), SDS((s, d), _MXU)],
        compiler_params=_cp("parallel", "arbitrary"), name=name)(x, g, w)


def _mm_tn(a, b, name, *, a_fn=None, blocked=False, out_dtype=F32):
    k, m = a.shape
    n = b.shape[1]
    tm, tk = min(512, m), min(1024, k)
    tn = next(t for t in (512, 256, 128) if n % t == 0)
    assert m % tm == 0 and k % tk == 0
    nk = k // tk

    def body(a_ref, b_ref, o_ref, acc):
        kk = pl.program_id(2)

        @pl.when(kk == 0)
        def _():
            acc[...] = jnp.zeros_like(acc)

        av = a_ref[...]
        if a_fn is not None:
            av = a_fn(av.astype(F32))
        acc[...] += _dot_tn(av, b_ref[...])

        @pl.when(kk == nk - 1)
        def _():
            o_ref[...] = acc[...].reshape(o_ref.shape).astype(o_ref.dtype)

    if blocked:
        assert tn == 512
        out_spec = pl.BlockSpec((1, tm, tn), lambda i, j, kk: (j, i, 0))
        out_shape = SDS((n // tn, m, tn), out_dtype)
    else:
        out_spec = pl.BlockSpec((tm, tn), lambda i, j, kk: (i, j))
        out_shape = SDS((m, n), out_dtype)
    return pl.pallas_call(
        body, grid=(m // tm, n // tn, nk),
        in_specs=[pl.BlockSpec((tk, tm), lambda i, j, kk: (kk, i)), pl.BlockSpec((tk, tn), lambda i, j, kk: (kk, j))],
        out_specs=out_spec, out_shape=out_shape, scratch_shapes=[pltpu.VMEM((tm, tn), F32)],
        compiler_params=_cp("parallel", "parallel", "arbitrary"), name=name)(a, b)


def _sgu_parts(u_pre, v_pre, gain):
    u = _gelu(u_pre)
    v = _gelu(v_pre)
    vh, rs, vg = [], [], []
    for h in range(N_HEADS):
        sl = slice(HEAD_DIM * h, HEAD_DIM * (h + 1))
        a, r = _standardize(v[:, sl])
        vh.append(a)
        rs.append(r)
        vg.append(a * gain[:, sl])
    return u, vh, rs, vg


def _tril(w):
    r = lax.broadcasted_iota(jnp.int32, w.shape, 0)
    c = lax.broadcasted_iota(jnp.int32, w.shape, 1)
    return jnp.where(r >= c, w, 0.0)


def _sgu_fwd(z, gain, w_s, b_t, name):
    s = z.shape[0]
    tm = min(512, s)

    def body(u_ref, v_ref, g_ref, w_ref, b_ref, y_ref):
        u, _, _, vg = _sgu_parts(u_ref[...], v_ref[...], g_ref[...])
        for h in range(N_HEADS):
            sl = slice(HEAD_DIM * h, HEAD_DIM * (h + 1))
            wc = _tril(w_ref[h])
            for c in range(tm // CHUNK):
                r = slice(CHUNK * c, CHUNK * (c + 1))
                mixed = _dot(wc, vg[h][r]) + b_ref[:, h:h + 1]
                y_ref[r, sl] = u[r, sl] * mixed

    return pl.pallas_call(
        body, grid=(s // tm,),
        in_specs=[pl.BlockSpec((tm, GROUP), lambda i: (i, 0)), pl.BlockSpec((tm, GROUP), lambda i: (i, 1)),
                  pl.BlockSpec((1, GROUP), lambda i: (0, 0)), pl.BlockSpec((N_HEADS, CHUNK, CHUNK), lambda i: (0, 0, 0)),
                  pl.BlockSpec((CHUNK, 128), lambda i: (0, 0))],
        out_specs=pl.BlockSpec((tm, GROUP), lambda i: (i, 0)), out_shape=SDS((s, GROUP), F32),
        compiler_params=_cp("parallel"), name=name)(z, z, gain, w_s, b_t)


def _sgu_bwd(dy, z, gain, w_s, b_t, name):
    s = z.shape[0]
    tm = min(512, s)

    def body(dy_ref, u_ref, v_ref, g_ref, w_ref, b_ref, dz_ref, dg_ref, dw_ref, db_ref):
        @pl.when(pl.program_id(0) == 0)
        def _():
            dg_ref[...] = jnp.zeros_like(dg_ref)
            dw_ref[...] = jnp.zeros_like(dw_ref)
            db_ref[...] = jnp.zeros_like(db_ref)

        u_pre, v_pre, gain_v = u_ref[...], v_ref[...], g_ref[...]
        u, vh, rs, vg = _sgu_parts(u_pre, v_pre, gain_v)
        dyv = dy_ref[...]
        gu = _gelu_grad(u_pre)
        gv = _gelu_grad(v_pre)
        for h in range(N_HEADS):
            sl = slice(HEAD_DIM * h, HEAD_DIM * (h + 1))
            wc = _tril(w_ref[h])
            dwh = jnp.zeros((CHUNK, CHUNK), F32)
            dbh = jnp.zeros((CHUNK, 1), F32)
            dgh = jnp.zeros((1, HEAD_DIM), F32)
            for c in range(tm // CHUNK):
                r = slice(CHUNK * c, CHUNK * (c + 1))
                mixed = _dot(wc, vg[h][r]) + b_ref[:, h:h + 1]
                dz_ref[r, sl] = (dyv[r, sl] * mixed * gu[r, sl]).astype(dz_ref.dtype)
                dm = dyv[r, sl] * u[r, sl]
                dwh += _dot_nt(dm, vg[h][r])
                dbh += jnp.sum(dm, axis=1, keepdims=True)
                dvg = _dot_tn(wc, dm)
                dgh += jnp.sum(dvg * vh[h][r], axis=0, keepdims=True)
                dv = _standardize_bwd(vh[h][r], rs[h][r], dvg * gain_v[:, sl])
                dz_ref[r, GROUP + HEAD_DIM * h:GROUP + HEAD_DIM * (h + 1)] = (dv * gv[r, sl]).astype(dz_ref.dtype)
            dw_ref[h] += _tril(dwh)
            db_ref[:, h:h + 1] += dbh
            dg_ref[:, sl] += dgh

    return pl.pallas_call(
        body, grid=(s // tm,),
        in_specs=[pl.BlockSpec((tm, GROUP), lambda i: (i, 0)),
                  pl.BlockSpec((tm, GROUP), lambda i: (i, 0)), pl.BlockSpec((tm, GROUP), lambda i: (i, 1)),
                  pl.BlockSpec((1, GROUP), lambda i: (0, 0)), pl.BlockSpec((N_HEADS, CHUNK, CHUNK), lambda i: (0, 0, 0)),
                  pl.BlockSpec((CHUNK, 128), lambda i: (0, 0))],
        out_specs=[pl.BlockSpec((tm, 2 * GROUP), lambda i: (i, 0)), pl.BlockSpec((1, GROUP), lambda i: (0, 0)),
                   pl.BlockSpec((N_HEADS, CHUNK, CHUNK), lambda i: (0, 0, 0)), pl.BlockSpec((CHUNK, 128), lambda i: (0, 0))],
        out_shape=[SDS((s, 2 * GROUP), _MXU), SDS((1, GROUP), F32), SDS((N_HEADS, CHUNK, CHUNK), F32), SDS((CHUNK, 128), F32)],
        compiler_params=_cp("arbitrary"), name=name)(dy, z, z, gain, w_s, b_t)


_SCALE_B = HEAD_DIM ** -0.5


def _ret_fwd(z, tb, name):
    s = z.shape[0]
    nc = s // CHUNK
    row = lambda col: pl.BlockSpec((CHUNK, GROUP), lambda n, col=col: (n, col))
    const = lambda shape: pl.BlockSpec(shape, lambda n: (0,) * len(shape))

    def body(q_ref, k_ref, v_ref, g_ref, cos_ref, sin_ref, dec_ref, qw_ref, kw_ref, cd_ref, y_ref, o_ref, st_ref, state):
        @pl.when(pl.program_id(0) == 0)
        def _():
            state[...] = jnp.zeros_like(state)

        q = _rope(q_ref[...], cos_ref[...], sin_ref[...], 32)
        k = _rope(k_ref[...], cos_ref[...], sin_ref[...], 32) * _SCALE_B
        v = v_ref[...]
        g = g_ref[...]
        st_ref[0] = state[...]
        qs = q * qw_ref[...]
        ks = k * kw_ref[...]
        for h in range(N_HEADS):
            sl = slice(HEAD_DIM * h, HEAD_DIM * (h + 1))
            sc = _dot_nt(q[:, sl], k[:, sl]) * dec_ref[h]
            o = _dot(sc, v[:, sl]) + _dot(qs[:, sl], state[:, sl])
            o_ref[:, sl] = o
            yh, _ = _standardize(o)
            gh = g[:, sl]
            y_ref[:, sl] = gh * _sigmoid(gh) * yh
            state[:, sl] = cd_ref[:, sl] * state[:, sl] + _dot_tn(ks[:, sl], v[:, sl])

    return pl.pallas_call(
        body, grid=(nc,),
        in_specs=[row(2), row(3), row(4), row(5), pl.BlockSpec((CHUNK, GROUP), lambda n: (n, 0)),
                  pl.BlockSpec((CHUNK, GROUP), lambda n: (n, 0)), const((N_HEADS, CHUNK, CHUNK)),
                  const((CHUNK, GROUP)), const((CHUNK, GROUP)), const((1, GROUP))],
        out_specs=[pl.BlockSpec((CHUNK, GROUP), lambda n: (n, 0)), pl.BlockSpec((CHUNK, GROUP), lambda n: (n, 0)),
                   pl.BlockSpec((1, HEAD_DIM, GROUP), lambda n: (n, 0, 0))],
        out_shape=[SDS((s, GROUP), F32), SDS((s, GROUP), F32), SDS((nc, HEAD_DIM, GROUP), F32)],
        scratch_shapes=[pltpu.VMEM((HEAD_DIM, GROUP), F32)],
        compiler_params=_cp("arbitrary"), name=name)(z, z, z, z, tb["b_cos"], tb["b_sin"], tb["decay"], tb["qw"], tb["kw"], tb["cd"])


def _ret_bwd(dy, z, o_pre, states, tb, name):
    s = z.shape[0]
    nc = s // CHUNK
    rev = lambda col: pl.BlockSpec((CHUNK, GROUP), lambda n, col=col: (nc - 1 - n, col))
    const = lambda shape: pl.BlockSpec(shape, lambda n: (0,) * len(shape))

    def body(dy_ref, q_ref, k_ref, v_ref, g_ref, o_ref, st_ref, cos_ref, sin_ref, dec_ref, qw_ref, kw2_ref, qw0_ref, cd_ref,
             dz_ref, rstate):
        @pl.when(pl.program_id(0) == 0)
        def _():
            rstate[...] = jnp.zeros_like(rstate)

        cos, sin = cos_ref[...], sin_ref[...]
        q = _rope(q_ref[...], cos, sin, 32)
        k = _rope(k_ref[...], cos, sin, 32) * _SCALE_B
        v = v_ref[...]
        g = g_ref[...]
        dyv = dy_ref[...]
        sg = _sigmoid(g)
        silu = g * sg
        dos, dgs = [], []
        for h in range(N_HEADS):
            sl = slice(HEAD_DIM * h, HEAD_DIM * (h + 1))
            yh, rs = _standardize(o_ref[:, sl])
            dgs.append(dyv[:, sl] * yh * (sg[:, sl] * (1.0 + g[:, sl] * (1.0 - sg[:, sl]))))
            dos.append(_standardize_bwd(yh, rs, dyv[:, sl] * silu[:, sl]))
        do = jnp.concatenate(dos, axis=1)
        dow = do * qw_ref[...]
        vw = v * kw2_ref[...]
        kw = k * kw2_ref[...]
        q0 = q * qw0_ref[...]
        dqs, dks = [], []
        for h in range(N_HEADS):
            sl = slice(HEAD_DIM * h, HEAD_DIM * (h + 1))
            dec = dec_ref[h]
            p = _dot_nt(q[:, sl], k[:, sl]) * dec
            dp = _dot_nt(do[:, sl], v[:, sl]) * dec
            sn = st_ref[0][:, sl]
            rr = rstate[:, sl]
            dqs.append(_dot(dp, k[:, sl]) + _dot_nt(dow[:, sl], sn))
            dks.append(_dot_tn(dp, q[:, sl]) + _dot_nt(vw[:, sl], rr))
            dv = _dot_tn(p, do[:, sl]) + _dot(kw[:, sl], rr)
            dz_ref[:, 2 * GROUP + HEAD_DIM * h:2 * GROUP + HEAD_DIM * (h + 1)] = dv.astype(dz_ref.dtype)
            rstate[:, sl] = cd_ref[:, sl] * rr + _dot_tn(q0[:, sl], do[:, sl])
        dq = _rope_bwd(jnp.concatenate(dqs, axis=1), cos, sin, 32)
        dk = _rope_bwd(jnp.concatenate(dks, axis=1) * _SCALE_B, cos, sin, 32)
        dz_ref[:, 0:GROUP] = dq.astype(dz_ref.dtype)
        dz_ref[:, GROUP:2 * GROUP] = dk.astype(dz_ref.dtype)
        dz_ref[:, 3 * GROUP:4 * GROUP] = jnp.concatenate(dgs, axis=1).astype(dz_ref.dtype)

    r0 = lambda: pl.BlockSpec((CHUNK, GROUP), lambda n: (nc - 1 - n, 0))
    return pl.pallas_call(
        body, grid=(nc,),
        in_specs=[r0(), rev(2), rev(3), rev(4), rev(5), r0(), pl.BlockSpec((1, HEAD_DIM, GROUP), lambda n: (nc - 1 - n, 0, 0)),
                  r0(), r0(), const((N_HEADS, CHUNK, CHUNK)), const((CHUNK, GROUP)), const((CHUNK, GROUP)),
                  const((CHUNK, GROUP)), const((1, GROUP))],
        out_specs=pl.BlockSpec((CHUNK, 4 * GROUP), lambda n: (nc - 1 - n, 0)),
        out_shape=SDS((s, 4 * GROUP), _MXU), scratch_shapes=[pltpu.VMEM((HEAD_DIM, GROUP), F32)],
        compiler_params=_cp("arbitrary"), name=name)(
            dy, z, z, z, z, o_pre, states, tb["b_cos"], tb["b_sin"], tb["decay"], tb["qw"], tb["kw2"], tb["qw0"], tb["cd"])


TQ = 256


def _log_sigmoid(x):
    return jnp.minimum(x, 0.0) - jnp.log1p(jnp.exp(-jnp.abs(x)))


def _fox_prep(z, b_f, name):
    s = z.shape[0]
    nb = s // TQ

    def body(m_ref, b_ref, cc_ref, cr_ref, carry):
        @pl.when(pl.program_id(0) == 0)
        def _():
            carry[...] = jnp.zeros_like(carry)

        lane = lax.broadcasted_iota(jnp.int32, (TQ, 128), 1)
        logf = jnp.where(lane < N_HEADS, _log_sigmoid(m_ref[...] + b_ref[...]), 0.0)
        r = lax.broadcasted_iota(jnp.int32, (TQ, TQ), 0)
        c = lax.broadcasted_iota(jnp.int32, (TQ, TQ), 1)
        tri = jnp.where(r >= c, 1.0, 0.0).astype(F32)
        cum = _dot_exact(tri, logf) + carry[...]
        cc_ref[...] = cum
        cr_ref[0] = cum.T[0:8, :]
        carry[...] = cum[TQ - 1:TQ, :]

    return pl.pallas_call(
        body, grid=(nb,),
        in_specs=[pl.BlockSpec((TQ, 128), lambda i: (i, NZ // 128 - 1)), pl.BlockSpec((1, 128), lambda i: (0, 0))],
        out_specs=[pl.BlockSpec((TQ, 128), lambda i: (i, 0)), pl.BlockSpec((1, 8, TQ), lambda i: (i, 0, 0))],
        out_shape=[SDS((s, 128), F32), SDS((nb, 8, TQ), F32)], scratch_shapes=[pltpu.VMEM((1, 128), F32)],
        compiler_params=_cp("arbitrary"), name=name)(z, b_f)


def _fox_post(dcr, dcq, z, b_f, dkr, name):
    s = z.shape[0]
    nb = s // TQ

    def body(dc_ref, dcq_ref, m_ref, b_ref, dkr_ref, dz_ref, db_ref, carry):
        @pl.when(pl.program_id(0) == 0)
        def _():
            carry[...] = jnp.zeros_like(carry)
            db_ref[...] = jnp.zeros_like(db_ref)

        r = lax.broadcasted_iota(jnp.int32, (TQ, TQ), 0)
        c = lax.broadcasted_iota(jnp.int32, (TQ, TQ), 1)
        triu = jnp.where(c >= r, 1.0, 0.0).astype(F32)
        dc = jnp.concatenate([dc_ref[0], jnp.zeros((120, TQ), F32)], axis=0)
        dlogf = _dot_exact(triu, dc, (((1,), (1,)), ((), ()))) + _dot_exact(triu, dcq_ref[...]) + carry[...]
        carry[...] = dlogf[0:1, :]
        x = m_ref[...] + b_ref[...]
        lane = lax.broadcasted_iota(jnp.int32, (TQ, 128), 1)
        df = jnp.where(lane < N_HEADS, dlogf * _sigmoid(-x), 0.0)
        db_ref[...] += jnp.sum(df, axis=0, keepdims=True)
        dz_ref[...] = (df + dkr_ref[...]).astype(dz_ref.dtype)

    rv = lambda i: nb - 1 - i
    return pl.pallas_call(
        body, grid=(nb,),
        in_specs=[pl.BlockSpec((1, 8, TQ), lambda i: (rv(i), 0, 0)), pl.BlockSpec((TQ, 128), lambda i: (rv(i), 0)),
                  pl.BlockSpec((TQ, 128), lambda i: (rv(i), NZ // 128 - 1)),
                  pl.BlockSpec((1, 128), lambda i: (0, 0)), pl.BlockSpec((TQ, 128), lambda i: (rv(i), 0))],
        out_specs=[pl.BlockSpec((TQ, 128), lambda i: (rv(i), 0)), pl.BlockSpec((1, 128), lambda i: (0, 0))],
        out_shape=[SDS((s, 128), _MXU), SDS((1, 128), F32)], scratch_shapes=[pltpu.VMEM((1, 128), F32)],
        compiler_params=_cp("arbitrary"), name=name)(dcr, dcq, z, b_f, dkr)


NEG = -1e30


def _causal_mask(shape, transposed=False):
    r = lax.broadcasted_iota(jnp.int32, shape, 0)
    c = lax.broadcasted_iota(jnp.int32, shape, 1)
    return (c >= r) if transposed else (r >= c)


def _flash_fwd(q, k, v, cols, dqk, scale, cum, name):
    s = q.shape[0]
    nq = s // TQ
    wq = N_HEADS * dqk
    bias = cum is not None

    def body(*refs):
        if bias:
            q_ref, k_ref, v_ref, cc_ref, cr_ref, o_ref, l_ref = refs
        else:
            q_ref, k_ref, v_ref, o_ref, l_ref = refs
        i = pl.program_id(0)
        l_ref[...] = jnp.zeros_like(l_ref)
        for h in range(N_HEADS):
            qh = q_ref[:, dqk * h:dqk * (h + 1)].astype(_MXU)
            cq = cc_ref[:, h:h + 1] if bias else None

            def step(j, carry, masked):
                m, l, acc = carry
                r0 = pl.multiple_of(j * TQ, TQ)
                kh = k_ref[pl.ds(r0, TQ), dqk * h:dqk * (h + 1)]
                vh = v_ref[pl.ds(r0, TQ), HEAD_DIM * h:HEAD_DIM * (h + 1)]
                sc = _dot_nt(qh, kh) * scale
                if bias:
                    sc = sc + (cq - cr_ref[j][h:h + 1, :])
                if masked:
                    sc = jnp.where(_causal_mask(sc.shape), sc, NEG)
                m_new = jnp.maximum(m, jnp.max(sc, axis=-1, keepdims=True))
                alpha = jnp.exp(m - m_new)
                p = jnp.exp(sc - m_new)
                return m_new, alpha * l + jnp.sum(p, axis=-1, keepdims=True), alpha * acc + _dot(p, vh)

            init = (jnp.full((TQ, 1), NEG, F32), jnp.zeros((TQ, 1), F32), jnp.zeros((TQ, HEAD_DIM), F32))
            carry = lax.fori_loop(0, i, functools.partial(step, masked=False), init)
            m, l, acc = step(i, carry, True)
            o_ref[:, HEAD_DIM * h:HEAD_DIM * (h + 1)] = acc / l
            l_ref[:, h:h + 1] = m + jnp.log(l)

    in_specs = [pl.BlockSpec((TQ, wq), lambda i: (i, cols[0])), pl.BlockSpec((s, wq), lambda i: (0, cols[1])),
                pl.BlockSpec((s, GROUP), lambda i: (0, cols[2]))]
    args = [q, k, v]
    if bias:
        in_specs += [pl.BlockSpec((TQ, 128), lambda i: (i, 0)), pl.BlockSpec((nq, 8, TQ), lambda i: (0, 0, 0))]
        args += list(cum)
    return pl.pallas_call(
        body, grid=(nq,), in_specs=in_specs,
        out_specs=[pl.BlockSpec((TQ, GROUP), lambda i: (i, 0)), pl.BlockSpec((TQ, 128), lambda i: (i, 0))],
        out_shape=[SDS((s, GROUP), F32), SDS((s, 128), F32)],
        compiler_params=_cp("parallel"), name=name)(*args)


def _flash_bwd(q, k, v, cols, dqk, scale, cum, do, lse, delta, name, kv_dtype):
    s = q.shape[0]
    nq = s // TQ
    wq = N_HEADS * dqk
    bias = cum is not None

    def body(*refs):
        if bias:
            q_ref, k_ref, v_ref, do_ref, l_ref, d_ref, cc_ref, cr_ref, dq_ref, dk_ref, dv_ref, dc_ref, dcq_ref = refs
        else:
            q_ref, k_ref, v_ref, do_ref, l_ref, d_ref, dq_ref, dk_ref, dv_ref = refs
        j = pl.program_id(0)

        @pl.when(j == 0)
        def _():
            dq_ref[...] = jnp.zeros_like(dq_ref)
            if bias:
                dcq_ref[...] = jnp.zeros_like(dcq_ref)

        if bias:
            dc_ref[...] = jnp.zeros_like(dc_ref)
        for h in range(N_HEADS):
            hq = slice(dqk * h, dqk * (h + 1))
            hv = slice(HEAD_DIM * h, HEAD_DIM * (h + 1))
            kh = k_ref[:, hq].astype(_MXU)
            vh = v_ref[:, hv].astype(_MXU)
            ck = cr_ref[0][h:h + 1, :] if bias else None

            def step(i, carry, masked):
                dk, dv, dc = carry
                r0 = pl.multiple_of(i * TQ, TQ)
                rows = pl.ds(r0, TQ)
                qh = q_ref[rows, hq].astype(_MXU)
                doh = do_ref[rows, hv].astype(_MXU)
                sc = _dot_nt(qh, kh) * scale
                if bias:
                    sc = sc + (cc_ref[rows, h:h + 1] - ck)
                p = jnp.exp(sc - l_ref[rows, h:h + 1])
                if masked:
                    p = jnp.where(_causal_mask(p.shape), p, 0.0)
                dv = dv + _dot_tn(p, doh)
                dp = _dot_nt(doh, vh)
                ds = p * (dp - d_ref[rows, h:h + 1])
                dk = dk + _dot_tn(ds, qh) * scale
                dq_ref[rows, hq] += _dot(ds, kh) * scale
                if bias:
                    dc = dc + jnp.sum(ds, axis=0, keepdims=True)
                    dcq_ref[rows, h:h + 1] += jnp.sum(ds, axis=1, keepdims=True)
                return dk, dv, dc

            init = (jnp.zeros((TQ, dqk), F32), jnp.zeros((TQ, HEAD_DIM), F32), jnp.zeros((1, TQ), F32))
            carry = step(j, init, True)
            dk, dv, dc = lax.fori_loop(j + 1, nq, functools.partial(step, masked=False), carry)
            dk_ref[:, hq] = dk.astype(dk_ref.dtype)
            dv_ref[:, hv] = dv.astype(dv_ref.dtype)
            if bias:
                dc_ref[0, h:h + 1, :] = -dc

    full = lambda w, c=0: pl.BlockSpec((s, w), lambda j, c=c: (0, c))
    in_specs = [full(wq, cols[0]), pl.BlockSpec((TQ, wq), lambda j: (j, cols[1])), pl.BlockSpec((TQ, GROUP), lambda j: (j, cols[2])),
                full(GROUP), full(128), full(128)]
    args = [q, k, v, do, lse, delta]
    out_specs = [full(wq), pl.BlockSpec((TQ, wq), lambda j: (j, 0)), pl.BlockSpec((TQ, GROUP), lambda j: (j, 0))]
    out_shape = [SDS((s, wq), F32), SDS((s, wq), kv_dtype), SDS((s, GROUP), kv_dtype)]
    if bias:
        in_specs += [full(128), pl.BlockSpec((1, 8, TQ), lambda j: (j, 0, 0))]
        args += list(cum)
        out_specs += [pl.BlockSpec((1, 8, TQ), lambda j: (j, 0, 0)), full(128)]
        out_shape += [SDS((nq, 8, TQ), F32), SDS((s, 128), F32)]
    return pl.pallas_call(body, grid=(nq,), in_specs=in_specs, out_specs=out_specs, out_shape=out_shape,
                          compiler_params=_cp("arbitrary"), name=name)(*args)


def _head_lanes(h, dqk):
    return slice(128 * (h // 2), 128 * (h // 2) + 128) if dqk == HEAD_DIM else slice(128 * h, 128 * h + 128)


def _keep_half(x, a, axis):
    idx = lax.broadcasted_iota(jnp.int32, x.shape, axis)
    return jnp.where((idx < HEAD_DIM) if a == 0 else (idx >= HEAD_DIM), x, jnp.zeros_like(x))


def _kv_prep(z, kcol, vcol, name):
    s = z.shape[0]
    nk = s // TQ

    def body(k_ref, v_ref, kb_ref, vb_ref, vt_ref):
        kb_ref[...] = k_ref[...].astype(_MXU)
        v = v_ref[...]
        vb_ref[...] = v.astype(_MXU)
        vt_ref[0] = v.T.astype(_MXU)

    blk = pl.BlockSpec((TQ, GROUP), lambda i: (i, 0))
    return pl.pallas_call(
        body, grid=(nk,),
        in_specs=[pl.BlockSpec((TQ, GROUP), lambda i: (i, kcol)), pl.BlockSpec((TQ, GROUP), lambda i: (i, vcol))],
        out_specs=[blk, blk, pl.BlockSpec((1, GROUP, TQ), lambda i: (i, 0, 0))],
        out_shape=[SDS((s, GROUP), _MXU), SDS((s, GROUP), _MXU), SDS((nk, GROUP, TQ), _MXU)],
        compiler_params=_cp("parallel"), name=name)(z, z)


def _attn_fwd(q, qcol, dqk, kb, vt, scale, cum, name):
    s = q.shape[0]
    nq = s // TQ
    wq = N_HEADS * dqk
    bias = cum is not None

    def body(*refs):
        if bias:
            q_ref, k_ref, vt_ref, cc_ref, cr_ref, o_ref, l_ref = refs
        else:
            q_ref, k_ref, vt_ref, o_ref, l_ref = refs
        i = pl.program_id(0)
        qts = []
        for h in range(N_HEADS):
            qt = q_ref[:, _head_lanes(h, dqk)].astype(F32).T
            qts.append((_keep_half(qt, h % 2, 0) if dqk == HEAD_DIM else qt).astype(_MXU))
        cqs = [cr_ref[0][h:h + 1, :] for h in range(N_HEADS)] if bias else None

        def step(j, carry, masked):
            r0 = pl.multiple_of(j * TQ, TQ)
            vtj = vt_ref[j]
            out = []
            for h in range(N_HEADS):
                m, l, acc = carry[3 * h:3 * h + 3]
                st = jnp.dot(k_ref[pl.ds(r0, TQ), _head_lanes(h, dqk)], qts[h], preferred_element_type=F32) * scale
                if bias:
                    st = st + (cqs[h] - cc_ref[pl.ds(r0, TQ), h:h + 1])
                if masked:
                    st = jnp.where(_causal_mask(st.shape, transposed=True), st, NEG)
                m_new = jnp.maximum(m, jnp.max(st, axis=0, keepdims=True))
                alpha = jnp.exp(m - m_new)
                p = jnp.exp(st - m_new)
                l = alpha * l + jnp.sum(p, axis=0, keepdims=True)
                acc = alpha * acc + jnp.dot(vtj[HEAD_DIM * h:HEAD_DIM * (h + 1), :], p.astype(_MXU),
                                            preferred_element_type=F32)
                out += [m_new, l, acc]
            return tuple(out)

        init = (jnp.full((1, TQ), NEG, F32), jnp.zeros((1, TQ), F32), jnp.zeros((HEAD_DIM, TQ), F32)) * N_HEADS
        carry = lax.fori_loop(0, i, functools.partial(step, masked=False), init)
        carry = step(i, carry, True)
        l_ref[...] = jnp.zeros_like(l_ref)
        for h in range(N_HEADS):
            l_ref[0, h:h + 1, :] = carry[3 * h] + jnp.log(carry[3 * h + 1])
        for p in range(2):
            ot = jnp.concatenate([carry[6 * p + 2] / carry[6 * p + 1], carry[6 * p + 5] / carry[6 * p + 4]], axis=0)
            o_ref[:, 128 * p:128 * (p + 1)] = ot.T

    rows = pl.BlockSpec((1, 8, TQ), lambda i: (i, 0, 0))
    in_specs = [pl.BlockSpec((TQ, wq), lambda i: (i, qcol)), pl.BlockSpec((s, wq), lambda i: (0, 0)),
                pl.BlockSpec((nq, GROUP, TQ), lambda i: (0, 0, 0))]
    args = [q, kb, vt]
    if bias:
        in_specs += [pl.BlockSpec((s, 128), lambda i: (0, 0)), rows]
        args += list(cum)
    return pl.pallas_call(
        body, grid=(nq,), in_specs=in_specs, out_specs=[pl.BlockSpec((TQ, GROUP), lambda i: (i, 0)), rows],
        out_shape=[SDS((s, GROUP), F32), SDS((nq, 8, TQ), F32)],
        compiler_params=_cp("parallel"), name=name)(*args)


def _attn_bwd_prep(q, qcol, dqk, o, do, name):
    s = q.shape[0]
    nq = s // TQ
    wq = N_HEADS * dqk

    def body(q_ref, o_ref, do_ref, qb_ref, qt_ref, dob_ref, dot_ref, dl_ref):
        qv = q_ref[...].astype(F32)
        qb_ref[...] = qv.astype(_MXU)
        qt_ref[0] = qv.T.astype(_MXU)
        dov = do_ref[...]
        dob_ref[...] = dov.astype(_MXU)
        dot_ref[0] = dov.T.astype(_MXU)
        pt = (dov * o_ref[...]).T
        dl_ref[...] = jnp.zeros_like(dl_ref)
        for h in range(N_HEADS):
            dl_ref[0, h:h + 1, :] = jnp.sum(pt[HEAD_DIM * h:HEAD_DIM * (h + 1), :], axis=0, keepdims=True)

    nat = lambda w: pl.BlockSpec((TQ, w), lambda i: (i, 0))
    tr = lambda w: pl.BlockSpec((1, w, TQ), lambda i: (i, 0, 0))
    return pl.pallas_call(
        body, grid=(nq,),
        in_specs=[pl.BlockSpec((TQ, wq), lambda i: (i, qcol)), nat(GROUP), nat(GROUP)],
        out_specs=[nat(wq), tr(wq), nat(GROUP), tr(GROUP), tr(8)],
        out_shape=[SDS((s, wq), _MXU), SDS((nq, wq, TQ), _MXU), SDS((s, GROUP), _MXU), SDS((nq, GROUP, TQ), _MXU),
                   SDS((nq, 8, TQ), F32)],
        compiler_params=_cp("parallel"), name=name)(q, o, do)


def _attn_bwd(kb, vb, qb, qt, dob, dot, lse, dl, dqk, scale, cum, name, kv_dtype):
    s = kb.shape[0]
    nq = s // TQ
    wq = N_HEADS * dqk
    bias = cum is not None

    def body(*refs):
        if bias:
            (k_ref, v_ref, q_ref, qt_ref, do_ref, dot_ref, l_ref, d_ref, cc_ref, cr_ref,
             dqt_ref, dk_ref, dv_ref, dck_ref, dcq_ref) = refs
        else:
            k_ref, v_ref, q_ref, qt_ref, do_ref, dot_ref, l_ref, d_ref, dqt_ref, dk_ref, dv_ref = refs
        j = pl.program_id(0)

        @pl.when(j == 0)
        def _():
            dqt_ref[...] = jnp.zeros_like(dqt_ref)
            if bias:
                dcq_ref[...] = jnp.zeros_like(dcq_ref)

        ks, kts, vs = [], [], []
        for h in range(N_HEADS):
            k2 = k_ref[:, _head_lanes(h, dqk)]
            if dqk == HEAD_DIM:
                k2 = _keep_half(k2, h % 2, 1)
            ks.append(k2)
            kts.append(k2.astype(F32).T.astype(_MXU))
            vs.append(_keep_half(v_ref[:, _head_lanes(h, HEAD_DIM)], h % 2, 1))
        cks = [cc_ref[:, h:h + 1] for h in range(N_HEADS)] if bias else None

        def step(i, carry, masked):
            r0 = pl.multiple_of(i * TQ, TQ)
            rows = pl.ds(r0, TQ)
            qti, doti, li, di = qt_ref[i], dot_ref[i], l_ref[i], d_ref[i]
            cri = cr_ref[i] if bias else None
            out = []
            for h in range(N_HEADS):
                dk, dv, dck = carry[3 * h:3 * h + 3]
                ql, vl = _head_lanes(h, dqk), _head_lanes(h, HEAD_DIM)
                st = jnp.dot(ks[h], qti[ql, :], preferred_element_type=F32) * scale
                rowterm = li[h:h + 1, :]
                if bias:
                    st = st + ((cri[h:h + 1, :] - rowterm) - cks[h])
                else:
                    st = st - rowterm
                p = jnp.exp(st)
                if masked:
                    p = jnp.where(_causal_mask(p.shape, transposed=True), p, 0.0)
                dv = dv + jnp.dot(p.astype(_MXU), do_ref[rows, vl], preferred_element_type=F32)
                dpt = jnp.dot(vs[h], doti[vl, :], preferred_element_type=F32)
                dst = p * (dpt - di[h:h + 1, :])
                dsb = dst.astype(_MXU)
                dk = dk + jnp.dot(dsb, q_ref[rows, ql], preferred_element_type=F32)
                dqt_ref[i, ql, :] += jnp.dot(kts[h], dsb, preferred_element_type=F32) * scale
                if bias:
                    dck = dck + jnp.sum(dst, axis=1, keepdims=True)
                    dcq_ref[i, h:h + 1, :] += jnp.sum(dst, axis=0, keepdims=True)
                out += [dk, dv, dck]
            return tuple(out)

        init = (jnp.zeros((TQ, 128), F32), jnp.zeros((TQ, 128), F32), jnp.zeros((TQ, 1), F32)) * N_HEADS
        carry = step(j, init, True)
        carry = lax.fori_loop(j + 1, nq, functools.partial(step, masked=False), carry)
        lane = lax.broadcasted_iota(jnp.int32, (TQ, 128), 1)
        for p in range(2):
            dv_ref[:, 128 * p:128 * (p + 1)] = jnp.where(lane < HEAD_DIM, carry[6 * p + 1], carry[6 * p + 4]).astype(dv_ref.dtype)
            if dqk == HEAD_DIM:
                dk_ref[:, 128 * p:128 * (p + 1)] = (jnp.where(lane < HEAD_DIM, carry[6 * p], carry[6 * p + 3]) * scale).astype(dk_ref.dtype)
        if dqk != HEAD_DIM:
            for h in range(N_HEADS):
                dk_ref[:, 128 * h:128 * (h + 1)] = (carry[3 * h] * scale).astype(dk_ref.dtype)
        if bias:
            dck_ref[...] = jnp.zeros_like(dck_ref)
            for h in range(N_HEADS):
                dck_ref[:, h:h + 1] = -carry[3 * h + 2]

    blk = lambda w: pl.BlockSpec((TQ, w), lambda j: (j, 0))
    full = lambda w: pl.BlockSpec((s, w), lambda j: (0, 0))
    full3 = lambda w: pl.BlockSpec((nq, w, TQ), lambda j: (0, 0, 0))
    in_specs = [blk(wq), blk(GROUP), full(wq), full3(wq), full(GROUP), full3(GROUP), full3(8), full3(8)]
    args = [kb, vb, qb, qt, dob, dot, lse, dl]
    out_specs = [full3(wq), blk(wq), blk(GROUP)]
    out_shape = [SDS((nq, wq, TQ), F32), SDS((s, wq), kv_dtype), SDS((s, GROUP), kv_dtype)]
    if bias:
        in_specs += [blk(128), full3(8)]
        args += list(cum)
        out_specs += [blk(128), full3(8)]
        out_shape += [SDS((s, 128), F32), SDS((nq, 8, TQ), F32)]
    return pl.pallas_call(body, grid=(nq,), in_specs=in_specs, out_specs=out_specs, out_shape=out_shape,
                          compiler_params=_cp("arbitrary"), name=name)(*args)


def _untranspose(xt, dtype, name):
    nq, w, _ = xt.shape

    def body(x_ref, o_ref):
        o_ref[...] = x_ref[0].T.astype(o_ref.dtype)

    return pl.pallas_call(
        body, grid=(nq,), in_specs=[pl.BlockSpec((1, w, TQ), lambda i: (i, 0, 0))],
        out_specs=pl.BlockSpec((TQ, w), lambda i: (i, 0)), out_shape=SDS((nq * TQ, w), dtype),
        compiler_params=_cp("parallel"), name=name)(xt)


_SCALE_D = (64 + 32) ** -0.5
_COL_CQ, _COL_CKV, _COL_MISC = 2304 // 256, 2560 // 128, 2688 // 128


def _mla_prep(z, gq, gkv, wq, wk, wv, tb, name):
    s = z.shape[0]
    tm = TQ
    row = lambda w, c: pl.BlockSpec((tm, w), lambda i, c=c: (i, c))
    const = lambda a: pl.BlockSpec(a.shape, lambda i: (0,) * a.ndim)

    def body(cq_ref, ckv_ref, m_ref, gq_ref, gkv_ref, wq_ref, wk_ref, wv_ref, e_ref, qc_ref, qs_ref, kc_ref, ks_ref,
             q_ref, k_ref, v_ref, vt_ref, cqn_ref, ckvn_ref):
        cqn = _rms(cq_ref[...], gq_ref[...]).astype(_MXU)
        ckvn = _rms(ckv_ref[...], gkv_ref[...]).astype(_MXU)
        cqn_ref[...] = cqn
        ckvn_ref[...] = ckvn
        q_ref[...] = _rope(_dot(cqn, wq_ref[...]), qc_ref[...], qs_ref[...], 16).astype(q_ref.dtype)
        kr = _rope(m_ref[...], kc_ref[...], ks_ref[...], 16)
        k_ref[...] = (_dot(ckvn, wk_ref[...]) + _dot(kr, e_ref[...])).astype(k_ref.dtype)
        v = _dot(ckvn, wv_ref[...])
        v_ref[...] = v.astype(v_ref.dtype)
        vt_ref[0] = v.T.astype(vt_ref.dtype)

    e = tb["place"]
    return pl.pallas_call(
        body, grid=(s // tm,),
        in_specs=[row(256, _COL_CQ), row(128, _COL_CKV), row(128, _COL_MISC), const(gq), const(gkv), const(wq), const(wk),
                  const(wv), const(e), row(512, 0), row(512, 0), row(128, 0), row(128, 0)],
        out_specs=[row(512, 0), row(512, 0), row(256, 0), pl.BlockSpec((1, GROUP, TQ), lambda i: (i, 0, 0)), row(256, 0),
                   row(128, 0)],
        out_shape=[SDS((s, 512), _MXU), SDS((s, 512), _MXU), SDS((s, 256), _MXU), SDS((s // TQ, GROUP, TQ), _MXU),
                   SDS((s, 256), _MXU), SDS((s, 128), _MXU)],
        compiler_params=_cp("parallel"), name=name)(
            z, z, z, gq, gkv, wq, wk, wv, e, tb["q_cos"], tb["q_sin"], tb["k_cos"], tb["k_sin"])


def _mla_prep_bwd(dq, dk, dv, z, cqn, ckvn, gq, gkv, wq, wk, wv, tb, name):
    s = z.shape[0]
    tm = min(512, s)
    row = lambda w, c: pl.BlockSpec((tm, w), lambda i, c=c: (i, c))
    const = lambda a: pl.BlockSpec(a.shape, lambda i: (0,) * a.ndim)
    acc = lambda shape: pl.BlockSpec(shape, lambda i: (0, 0))

    def body(dq_ref, dk_ref, dv_ref, cq_ref, ckv_ref, cqn_ref, ckvn_ref, gq_ref, gkv_ref, wq_ref, wk_ref, wv_ref, e_ref,
             qc_ref, qs_ref, kc_ref, ks_ref, dcq_ref, dckv_ref, dkr_ref, dwq_ref, dwk_ref, dwv_ref, dgq_ref, dgkv_ref):
        @pl.when(pl.program_id(0) == 0)
        def _():
            for r in (dwq_ref, dwk_ref, dwv_ref, dgq_ref, dgkv_ref):
                r[...] = jnp.zeros_like(r)

        dqp = _rope_bwd(dq_ref[...], qc_ref[...], qs_ref[...], 16)
        dkd = dk_ref[...]
        dvd = dv_ref[...]
        dwq_ref[...] += _dot_tn(cqn_ref[...], dqp)
        dwk_ref[...] += _dot_tn(ckvn_ref[...], dkd)
        dwv_ref[...] += _dot_tn(ckvn_ref[...], dvd)
        dcq, dgq = _rms_bwd(cq_ref[...], gq_ref[...], _dot_nt(dqp, wq_ref[...]))
        dckv, dgkv = _rms_bwd(ckv_ref[...], gkv_ref[...], _dot_nt(dkd, wk_ref[...]) + _dot_nt(dvd, wv_ref[...]))
        dcq_ref[...] = dcq.astype(dcq_ref.dtype)
        dckv_ref[...] = dckv.astype(dckv_ref.dtype)
        dgq_ref[...] += dgq
        dgkv_ref[...] += dgkv
        dkr = _dot_exact(dkd, e_ref[...], (((1,), (1,)), ((), ())))
        dkr_ref[...] = _rope_bwd(dkr, kc_ref[...], ks_ref[...], 16)

    e = tb["place"]
    return pl.pallas_call(
        body, grid=(s // tm,),
        in_specs=[row(512, 0), row(512, 0), row(256, 0), row(256, _COL_CQ), row(128, _COL_CKV), row(256, 0), row(128, 0),
                  const(gq), const(gkv), const(wq), const(wk), const(wv), const(e), row(512, 0), row(512, 0), row(128, 0), row(128, 0)],
        out_specs=[row(256, 0), row(128, 0), row(128, 0), acc((256, 512)), acc((128, 512)), acc((128, 256)), acc((1, 256)),
                   acc((1, 128))],
        out_shape=[SDS((s, 256), _MXU), SDS((s, 128), _MXU), SDS((s, 128), F32), SDS((256, 512), F32), SDS((128, 512), F32),
                   SDS((128, 256), F32), SDS((1, 256), F32), SDS((1, 128), F32)],
        compiler_params=_cp("arbitrary"), name=name)(
            dq, dk, dv, z, z, cqn, ckvn, gq, gkv, wq, wk, wv, e, tb["q_cos"], tb["q_sin"], tb["k_cos"], tb["k_sin"])


def _out_proj(ys, g, w, x, name):
    s, d = x.shape
    tm = min(512, s)

    def body(ya, yb, yc, yd, g_ref, w_ref, x_ref, o_ref, yn_ref):
        acc = x_ref[...]
        for i, y_ref in enumerate((ya, yb, yc, yd)):
            sl = slice(GROUP * i, GROUP * (i + 1))
            yn = _rms(y_ref[...], g_ref[:, sl]).astype(_MXU)
            yn_ref[:, sl] = yn
            acc = acc + jnp.dot(yn, w_ref[sl, :], preferred_element_type=F32)
        o_ref[...] = acc

    yspec = pl.BlockSpec((tm, GROUP), lambda i: (i, 0))
    return pl.pallas_call(
        body, grid=(s // tm,),
        in_specs=[yspec, yspec, yspec, yspec, pl.BlockSpec((1, d), lambda i: (0, 0)), pl.BlockSpec((d, d), lambda i: (0, 0)),
                  pl.BlockSpec((tm, d), lambda i: (i, 0))],
        out_specs=[pl.BlockSpec((tm, d), lambda i: (i, 0)), pl.BlockSpec((tm, d), lambda i: (i, 0))],
        out_shape=[SDS((s, d), F32), SDS((s, d), _MXU)], compiler_params=_cp("parallel"), name=name)(*ys, g, w, x)


def _out_proj_bwd(dx, w, ys, g, name):
    s, d = dx.shape
    tm = min(512, s)

    def body(dx_ref, w_ref, ya, yb, yc, yd, g_ref, da, db, dc, dd, dg_ref):
        @pl.when(pl.program_id(0) == 0)
        def _():
            dg_ref[...] = jnp.zeros_like(dg_ref)

        dyn = _dot_nt(dx_ref[...], w_ref[...])
        outs = (da, db, dc, dd)
        for i, y_ref in enumerate((ya, yb, yc, yd)):
            sl = slice(GROUP * i, GROUP * (i + 1))
            dy, dg = _rms_bwd(y_ref[...], g_ref[:, sl], dyn[:, sl])
            outs[i][...] = dy
            dg_ref[:, sl] += dg

    yspec = pl.BlockSpec((tm, GROUP), lambda i: (i, 0))
    return pl.pallas_call(
        body, grid=(s // tm,),
        in_specs=[pl.BlockSpec((tm, d), lambda i: (i, 0)), pl.BlockSpec((d, d), lambda i: (0, 0)), yspec, yspec, yspec, yspec,
                  pl.BlockSpec((1, d), lambda i: (0, 0))],
        out_specs=[yspec, yspec, yspec, yspec, pl.BlockSpec((1, d), lambda i: (0, 0))],
        out_shape=[SDS((s, GROUP), F32)] * 4 + [SDS((1, d), F32)],
        compiler_params=_cp("arbitrary"), name=name)(dx, w, *ys, g)


FF_BLOCK = 512


def _ffn_fwd(x, g, wu, wd, name):
    s, d = x.shape
    nj = wu.shape[0]
    tm = min(512, s)

    def body(x_ref, g_ref, wu_ref, wd_ref, o_ref, u_ref, h_ref, acc):
        j = pl.program_id(1)

        @pl.when(j == 0)
        def _():
            h_ref[...] = _rms(x_ref[...], g_ref[...]).astype(h_ref.dtype)
            acc[...] = jnp.zeros_like(acc)

        u = jnp.dot(h_ref[...], wu_ref[0], preferred_element_type=F32)
        u_ref[...] = u.astype(u_ref.dtype)
        acc[...] += _dot(jnp.square(jnp.maximum(u, 0.0)), wd_ref[...])

        @pl.when(j == nj - 1)
        def _():
            o_ref[...] = x_ref[...] + acc[...]

    return pl.pallas_call(
        body, grid=(s // tm, nj),
        in_specs=[pl.BlockSpec((tm, d), lambda i, j: (i, 0)), pl.BlockSpec((1, d), lambda i, j: (0, 0)),
                  pl.BlockSpec((1, d, FF_BLOCK), lambda i, j: (j, 0, 0)), pl.BlockSpec((FF_BLOCK, d), lambda i, j: (j, 0))],
        out_specs=[pl.BlockSpec((tm, d), lambda i, j: (i, 0)), pl.BlockSpec((tm, FF_BLOCK), lambda i, j: (i, j)),
                   pl.BlockSpec((tm, d), lambda i, j: (i, 0))],
        out_shape=[SDS((s, d), F32), SDS((s, nj * FF_BLOCK), _MXU), SDS((s, d), _MXU)],
        scratch_shapes=[pltpu.VMEM((tm, d), F32)], compiler_params=_cp("parallel", "arbitrary"), name=name)(x, g, wu, wd)


def _ffn_bwd(dx2, x, u, g, wu, wd, name):
    s, d = x.shape
    nj = wu.shape[0]
    tm = min(512, s)

    def body(dx_ref, x_ref, u_ref, g_ref, wu_ref, wd_ref, o_ref, du_ref, dg_ref, acc, dxb):
        i, j = pl.program_id(0), pl.program_id(1)

        @pl.when((i == 0) & (j == 0))
        def _():
            dg_ref[...] = jnp.zeros_like(dg_ref)

        @pl.when(j == 0)
        def _():
            dxb[...] = dx_ref[...].astype(dxb.dtype)
            acc[...] = jnp.zeros_like(acc)

        da = lax.dot_general(dxb[...], wd_ref[...], (((1,), (1,)), ((), ())), preferred_element_type=F32)
        du = (da * 2.0 * jnp.maximum(u_ref[...].astype(F32), 0.0)).astype(du_ref.dtype)
        du_ref[...] = du
        acc[...] += lax.dot_general(du, wu_ref[0], (((1,), (1,)), ((), ())), preferred_element_type=F32)

        @pl.when(j == nj - 1)
        def _():
            dxn, dg = _rms_bwd(x_ref[...], g_ref[...], acc[...])
            o_ref[...] = dx_ref[...] + dxn
            dg_ref[...] += dg

    return pl.pallas_call(
        body, grid=(s // tm, nj),
        in_specs=[pl.BlockSpec((tm, d), lambda i, j: (i, 0)), pl.BlockSpec((tm, d), lambda i, j: (i, 0)),
                  pl.BlockSpec((tm, FF_BLOCK), lambda i, j: (i, j)), pl.BlockSpec((1, d), lambda i, j: (0, 0)),
                  pl.BlockSpec((1, d, FF_BLOCK), lambda i, j: (j, 0, 0)), pl.BlockSpec((FF_BLOCK, d), lambda i, j: (j, 0))],
        out_specs=[pl.BlockSpec((tm, d), lambda i, j: (i, 0)), pl.BlockSpec((tm, FF_BLOCK), lambda i, j: (i, j)),
                   pl.BlockSpec((1, d), lambda i, j: (0, 0))],
        out_shape=[SDS((s, d), F32), SDS((s, nj * FF_BLOCK), _MXU), SDS((1, d), F32)],
        scratch_shapes=[pltpu.VMEM((tm, d), F32), pltpu.VMEM((tm, d), _MXU)],
        compiler_params=_cp("arbitrary", "arbitrary"), name=name)(dx2, x, u, g, wu, wd)


def _in_proj_bwd(dz, w, x, g, dx_up, name):
    s, d = x.shape
    n = w.shape[1]
    tm = min(512, s)

    def body(dz_ref, w_ref, x_ref, g_ref, up_ref, o_ref, dg_ref):
        @pl.when(pl.program_id(0) == 0)
        def _():
            dg_ref[...] = jnp.zeros_like(dg_ref)

        dh = lax.dot_general(dz_ref[...], w_ref[...], (((1,), (1,)), ((), ())), preferred_element_type=F32)
        dxn, dg = _rms_bwd(x_ref[...], g_ref[...], dh)
        o_ref[...] = up_ref[...] + dxn
        dg_ref[...] += dg

    return pl.pallas_call(
        body, grid=(s // tm,),
        in_specs=[pl.BlockSpec((tm, n), lambda i: (i, 0)), pl.BlockSpec((d, n), lambda i: (0, 0)),
                  pl.BlockSpec((tm, d), lambda i: (i, 0)), pl.BlockSpec((1, d), lambda i: (0, 0)),
                  pl.BlockSpec((tm, d), lambda i: (i, 0))],
        out_specs=[pl.BlockSpec((tm, d), lambda i: (i, 0)), pl.BlockSpec((1, d), lambda i: (0, 0))],
        out_shape=[SDS((s, d), F32), SDS((1, d), F32)], compiler_params=_cp("arbitrary"), name=name)(dz, w, x, g, dx_up)


def _loss_head(x, g, target, name):
    s, d = x.shape
    tm = min(512, s)

    def body(x_ref, g_ref, t_ref, l_ref, dx_ref, dg_ref):
        @pl.when(pl.program_id(0) == 0)
        def _():
            l_ref[...] = jnp.zeros_like(l_ref)
            dg_ref[...] = jnp.zeros_like(dg_ref)

        xv = x_ref[...]
        err = _rms(xv, g_ref[...]) - t_ref[...]
        l_ref[...] += jnp.sum(err * err, axis=0, keepdims=True) * (0.5 / d)
        dx, dg = _rms_bwd(xv, g_ref[...], err * (1.0 / d))
        dx_ref[...] = dx
        dg_ref[...] += dg

    return pl.pallas_call(
        body, grid=(s // tm,),
        in_specs=[pl.BlockSpec((tm, d), lambda i: (i, 0)), pl.BlockSpec((1, d), lambda i: (0, 0)),
                  pl.BlockSpec((tm, d), lambda i: (i, 0))],
        out_specs=[pl.BlockSpec((1, d), lambda i: (0, 0)), pl.BlockSpec((tm, d), lambda i: (i, 0)),
                   pl.BlockSpec((1, d), lambda i: (0, 0))],
        out_shape=[SDS((1, d), F32), SDS((s, d), F32), SDS((1, d), F32)], compiler_params=_cp("arbitrary"), name=name)(x, g, target)


def _me_and_peer():
    x, y, c = lax.axis_index("x"), lax.axis_index("y"), lax.axis_index("c")
    me = 4 * x + 2 * y + c

    def peer(k):
        px, py, pc = x ^ (k >> 2), y ^ ((k >> 1) & 1), c ^ (k & 1)
        return (px, py, pc), 4 * px + 2 * py + pc

    return me, peer


def _all_gather(arrs, name):
    n = len(arrs)

    def body(*refs):
        ins, outs, (send, recv, loc) = refs[:n], refs[n:2 * n], refs[2 * n:]
        me, peer = _me_and_peer()
        copies = []
        for a in range(n):
            lc = pltpu.make_async_copy(ins[a], outs[a].at[me], loc.at[a])
            lc.start()
            copies.append(lc)
        rdma = []
        for k in range(1, NDEV):
            dev, _ = peer(k)
            for a in range(n):
                cp = pltpu.make_async_remote_copy(src_ref=ins[a], dst_ref=outs[a].at[me], send_sem=send.at[a, k - 1],
                                                  recv_sem=recv.at[a, k - 1], device_id=dev, device_id_type=MESH)
                cp.start()
                rdma.append((cp, a, k))
        for cp, a, k in rdma:
            _, pid = peer(k)
            pltpu.make_async_remote_copy(src_ref=ins[a], dst_ref=outs[a].at[pid], send_sem=send.at[a, k - 1],
                                         recv_sem=recv.at[a, k - 1], device_id=peer(k)[0], device_id_type=MESH).wait_recv()
        for cp, a, k in rdma:
            cp.wait_send()
        for lc in copies:
            lc.wait()

    anyspec = pl.BlockSpec(memory_space=pl.ANY)
    return pl.pallas_call(
        body, in_specs=[anyspec] * n, out_specs=[anyspec] * n,
        out_shape=[SDS((NDEV,) + a.shape, a.dtype) for a in arrs],
        scratch_shapes=[pltpu.SemaphoreType.DMA((n, NDEV - 1)), pltpu.SemaphoreType.DMA((n, NDEV - 1)),
                        pltpu.SemaphoreType.DMA((n,))],
        compiler_params=pltpu.CompilerParams(has_side_effects=True), name=name)(*arrs)


def _exchange(arrs, name):
    n = len(arrs)

    def body(*refs):
        ins, outs, (send, recv, loc) = refs[:n], refs[n:2 * n], refs[2 * n:]
        me, peer = _me_and_peer()
        copies = []
        for a in range(n):
            lc = pltpu.make_async_copy(ins[a].at[me], outs[a].at[me], loc.at[a])
            lc.start()
            copies.append(lc)
        rdma = []
        for k in range(1, NDEV):
            dev, pid = peer(k)
            for a in range(n):
                cp = pltpu.make_async_remote_copy(src_ref=ins[a].at[pid], dst_ref=outs[a].at[me], send_sem=send.at[a, k - 1],
                                                  recv_sem=recv.at[a, k - 1], device_id=dev, device_id_type=MESH)
                cp.start()
                rdma.append((cp, a, k))
        for cp, a, k in rdma:
            dev, pid = peer(k)
            pltpu.make_async_remote_copy(src_ref=ins[a].at[me], dst_ref=outs[a].at[pid], send_sem=send.at[a, k - 1],
                                         recv_sem=recv.at[a, k - 1], device_id=dev, device_id_type=MESH).wait_recv()
        for cp, a, k in rdma:
            cp.wait_send()
        for lc in copies:
            lc.wait()

    anyspec = pl.BlockSpec(memory_space=pl.ANY)
    return pl.pallas_call(
        body, in_specs=[anyspec] * n, out_specs=[anyspec] * n,
        out_shape=[SDS(a.shape, a.dtype) for a in arrs],
        scratch_shapes=[pltpu.SemaphoreType.DMA((n, NDEV - 1)), pltpu.SemaphoreType.DMA((n, NDEV - 1)),
                        pltpu.SemaphoreType.DMA((n,))],
        compiler_params=pltpu.CompilerParams(has_side_effects=True), name=name)(*arrs)


def _sum_slots(parts, name):
    _, r, c = parts.shape
    tr = r if r <= 512 else 512

    def body(p_ref, o_ref):
        acc = p_ref[0].astype(F32)
        for q in range(1, NDEV):
            acc = acc + p_ref[q].astype(F32)
        o_ref[...] = acc

    return pl.pallas_call(
        body, grid=(r // tr,), in_specs=[pl.BlockSpec((NDEV, tr, c), lambda i: (0, i, 0))],
        out_specs=pl.BlockSpec((tr, c), lambda i: (i, 0)), out_shape=SDS((r, c), F32),
        compiler_params=_cp("parallel"), name=name)(parts)


def _adamw(g, w, m, v, name):
    r, c = w.shape
    parts = g.ndim == 3
    tr = r
    for cand in (512, 256, 128, 64, 32, 16, 8):
        if r > cand and r % cand == 0 and cand * c * 4 <= 2 * 1024 * 1024:
            tr = cand
            break
    bc1 = 1.0 / (1.0 - ADAM_B1 ** ADAM_STEP)
    bc2 = 1.0 / (1.0 - ADAM_B2 ** ADAM_STEP)

    def body(g_ref, w_ref, m_ref, v_ref, go_ref, d_ref, mo_ref, vo_ref):
        if parts:
            gv = g_ref[0].astype(F32)
            for q in range(1, NDEV):
                gv = gv + g_ref[q].astype(F32)
        else:
            gv = g_ref[...]
        mn = ADAM_B1 * m_ref[...] + (1.0 - ADAM_B1) * gv
        vn = ADAM_B2 * v_ref[...] + (1.0 - ADAM_B2) * (gv * gv)
        go_ref[...] = gv
        mo_ref[...] = mn
        vo_ref[...] = vn
        d_ref[...] = -ADAM_LR * ((mn * bc1) / (jnp.sqrt(vn * bc2) + ADAM_EPS) + ADAM_WD * w_ref[...])

    spec = pl.BlockSpec((tr, c), lambda i: (i, 0))
    gspec = pl.BlockSpec((NDEV, tr, c), lambda i: (0, i, 0)) if parts else spec
    return pl.pallas_call(
        body, grid=(r // tr,), in_specs=[gspec, spec, spec, spec], out_specs=[spec] * 4,
        out_shape=[SDS((r, c), F32)] * 4, compiler_params=_cp("parallel"), name=name)(g, w, m, v)


def _pad_in_cols(w):
    r = w.shape[0]
    zeros = lambda n: jnp.zeros((r, n), w.dtype)
    return jnp.concatenate([w[:, :2304], w[:, 2308:2692], w[:, 2304:2308], zeros(28), w[:, 2692:2724], zeros(64)], axis=1)


def _unpad_in_cols(w):
    return jnp.concatenate([w[..., :2304], w[..., 2688:2692], w[..., 2304:2688], w[..., 2720:2752]], axis=-1)


def _pad_uq(w):
    return jnp.pad(w.reshape(256, N_HEADS, 96), ((0, 0), (0, 0), (0, 32))).reshape(256, 512)


def _unpad_uq(w):
    return w.reshape(256, N_HEADS, 128)[:, :, :96].reshape(256, 384)


def _split_ukv(w):
    r = w.reshape(128, N_HEADS, 128)
    return jnp.pad(r[:, :, :64], ((0, 0), (0, 0), (0, 64))).reshape(128, 512), r[:, :, 64:].reshape(128, 256)


def _join_ukv(dk, dv):
    return jnp.concatenate([dk.reshape(128, N_HEADS, 128)[:, :, :64], dv.reshape(128, N_HEADS, 64)], axis=-1).reshape(128, 512)


def _cols_to_full(g):
    return jnp.transpose(g, (1, 0, 2)).reshape(g.shape[1], NDEV * g.shape[2])


def kernel(x, g_mix_norm, w_in, b_forget, g_sgu, w_spatial, b_spatial, g_mla_q, w_uq, g_mla_kv, w_ukv, g_group_out, w_out, g_ffn_norm, w_up, w_down, g_final, loss_target, m_g_mix_norm, m_w_in, m_b_forget, m_g_sgu, m_w_spatial, m_b_spatial, m_g_mla_q, m_w_uq, m_g_mla_kv, m_w_ukv, m_g_group_out, m_w_out, m_g_ffn_norm, m_w_up, m_w_down, m_g_final, v_g_mix_norm, v_w_in, v_b_forget, v_g_sgu, v_w_spatial, v_b_spatial, v_g_mla_q, v_w_uq, v_g_mla_kv, v_w_ukv, v_g_group_out, v_w_out, v_g_ffn_norm, v_w_up, v_w_down, v_g_final):
    depth = w_in.shape[0]
    s, d = x.shape[1], x.shape[2]
    x0 = x.reshape(s, d)
    target = loss_target.reshape(s, d)
    tb = _tables(s)
    me = 4 * lax.axis_index("x") + 2 * lax.axis_index("y") + lax.axis_index("c")

    shards = []
    for l in range(depth):
        shards += [_pad_in_cols(w_in[l]).astype(_WIRE), w_out[l].astype(_WIRE), w_up[l].astype(_WIRE),
                   w_down[l].astype(_WIRE), w_uq[l].astype(_WIRE), w_ukv[l].astype(_WIRE)]
    gathered = _all_gather(shards, "gather_weights")
    wts = []
    for l in range(depth):
        gi, go, gu, gd, gq, gkv = gathered[6 * l:6 * l + 6]
        wk, wv = _split_ukv(_cols_to_full(gkv))
        wts.append(dict(w_in=gi.reshape(d, NZ), w_out=go.reshape(d, d), w_up=gu, w_down=gd.reshape(NDEV * gd.shape[1], d),
                        wq=_pad_uq(_cols_to_full(gq)), wk=wk, wv=wv))

    row = lambda a: a.reshape(1, -1)

    def small(l):
        bf = jnp.pad(b_forget[l].reshape(1, N_HEADS), ((0, 0), (0, 128 - N_HEADS)))
        bt = jnp.pad(b_spatial[l].T, ((0, 0), (0, 128 - N_HEADS)))
        return dict(g_mix=row(g_mix_norm[l]), g_sgu=row(g_sgu[l]), w_s=w_spatial[l], b_t=bt, b_f=bf, gq=row(g_mla_q[l]),
                    gkv=row(g_mla_kv[l]), g_go=row(g_group_out[l]), g_ffn=row(g_ffn_norm[l]))

    smalls = [small(l) for l in range(depth)]
    lrow, dx, big, sm, dg_final = _local_step(x0, target, wts, smalls, row(g_final), tb)
    loss = lax.psum(jnp.sum(lrow), AXES)
    grad_x = dx.reshape(1, s, d)
    return _reduce_and_update(loss, grad_x, big, sm, dg_final, me, dict(
        g_mix_norm=(g_mix_norm, m_g_mix_norm, v_g_mix_norm), w_in=(w_in, m_w_in, v_w_in),
        b_forget=(b_forget, m_b_forget, v_b_forget), g_sgu=(g_sgu, m_g_sgu, v_g_sgu),
        w_spatial=(w_spatial, m_w_spatial, v_w_spatial), b_spatial=(b_spatial, m_b_spatial, v_b_spatial),
        g_mla_q=(g_mla_q, m_g_mla_q, v_g_mla_q), w_uq=(w_uq, m_w_uq, v_w_uq), g_mla_kv=(g_mla_kv, m_g_mla_kv, v_g_mla_kv),
        w_ukv=(w_ukv, m_w_ukv, v_w_ukv), g_group_out=(g_group_out, m_g_group_out, v_g_group_out),
        w_out=(w_out, m_w_out, v_w_out), g_ffn_norm=(g_ffn_norm, m_g_ffn_norm, v_g_ffn_norm), w_up=(w_up, m_w_up, v_w_up),
        w_down=(w_down, m_w_down, v_w_down), g_final=(g_final, m_g_final, v_g_final)))


def _local_step(x0, target, wts, smalls, g_final, tb):
    depth = len(wts)
    s, d = x0.shape
    saved = []
    xl = x0
    for l in range(depth):
        w, p = wts[l], smalls[l]
        z, h = _norm_matmul(xl, p["g_mix"], w["w_in"], f"in_proj{l}")
        ya = _sgu_fwd(z, p["g_sgu"], p["w_s"], p["b_t"], f"sgu_fwd{l}")
        yb, ret, states = _ret_fwd(z, tb, f"ret_fwd{l}")
        cum = _fox_prep(z, p["b_f"], f"fox_prep{l}")
        kc, vc, vtc = _kv_prep(z, 7, 8, f"fox_kv{l}")
        yc, lse_c = _attn_fwd(z, 6, HEAD_DIM, kc, vtc, HEAD_DIM ** -0.5, cum, f"fox_attn{l}")
        qd, kd, vd, vtd, cqn, ckvn = _mla_prep(z, p["gq"], p["gkv"], w["wq"], w["wk"], w["wv"], tb, f"mla_prep{l}")
        yd, lse_d = _attn_fwd(qd, 0, 128, kd, vtd, _SCALE_D, None, f"mla_attn{l}")
        ys = (ya, yb, yc, yd)
        x1, yn = _out_proj(ys, p["g_go"], w["w_out"], xl, f"out_proj{l}")
        x2, u, h2 = _ffn_fwd(x1, p["g_ffn"], w["w_up"], w["w_down"], f"ffn_fwd{l}")
        saved.append(dict(x=xl, z=z, h=h, ys=ys, ret=ret, states=states, cum=cum, lse_c=lse_c, kc=kc, vc=vc, qd=qd, kd=kd, vd=vd,
                          cqn=cqn, ckvn=ckvn, lse_d=lse_d, x1=x1, yn=yn, u=u, h2=h2))
        xl = x2

    lrow, dx, dg_final = _loss_head(xl, g_final, target, "loss_head")

    big = [None] * depth
    sm = [None] * depth
    for l in reversed(range(depth)):
        w, p, a = wts[l], smalls[l], saved[l]
        dx1, du, dg_ffn = _ffn_bwd(dx, a["x1"], a["u"], p["g_ffn"], w["w_up"], w["w_down"], f"ffn_bwd{l}")
        dw_down = _mm_tn(a["u"], dx, f"dw_down{l}", a_fn=lambda t: jnp.square(jnp.maximum(t, 0.0)), out_dtype=_WIRE)
        dw_up = _mm_tn(a["h2"], du, f"dw_up{l}", blocked=True, out_dtype=_WIRE)
        dya, dyb, dyc, dyd, dg_go = _out_proj_bwd(dx1, w["w_out"], a["ys"], p["g_go"], f"out_proj_bwd{l}")
        dw_out = _mm_tn(a["yn"], dx1, f"dw_out{l}", out_dtype=_WIRE)
        dz_a, dg_sgu, dw_s, db_t = _sgu_bwd(dya, a["z"], p["g_sgu"], p["w_s"], p["b_t"], f"sgu_bwd{l}")
        dz_b = _ret_bwd(dyb, a["z"], a["ret"], a["states"], tb, f"ret_bwd{l}")
        qb, qt, dob, dot, dl = _attn_bwd_prep(a["z"], 6, HEAD_DIM, a["ys"][2], dyc, f"fox_bwd_prep{l}")
        dqt_c, dk_c, dv_c, dck, dcq = _attn_bwd(a["kc"], a["vc"], qb, qt, dob, dot, a["lse_c"], dl, HEAD_DIM, HEAD_DIM ** -0.5,
                                                a["cum"], f"fox_attn_bwd{l}", _MXU)
        dq_c = _untranspose(dqt_c, _MXU, f"fox_dq{l}")
        qb, qt, dob, dot, dl = _attn_bwd_prep(a["qd"], 0, 128, a["ys"][3], dyd, f"mla_bwd_prep{l}")
        dqt_d, dk_d, dv_d = _attn_bwd(a["kd"], a["vd"], qb, qt, dob, dot, a["lse_d"], dl, 128, _SCALE_D, None,
                                      f"mla_attn_bwd{l}", F32)
        dq_d = _untranspose(dqt_d, F32, f"mla_dq{l}")
        dz_cq, dz_ckv, dkr, dwq, dwk, dwv, dgq, dgkv = _mla_prep_bwd(dq_d, dk_d, dv_d, a["z"], a["cqn"], a["ckvn"], p["gq"],
                                                                     p["gkv"], w["wq"], w["wk"], w["wv"], tb, f"mla_prep_bwd{l}")
        dz_misc, db_f = _fox_post(dcq, dck, a["z"], p["b_f"], dkr, f"fox_post{l}")
        dz = jnp.concatenate([dz_a, dz_b, dq_c, dk_c, dv_c, dz_cq, dz_ckv, dz_misc], axis=1)
        dx, dg_mix = _in_proj_bwd(dz, w["w_in"], a["x"], p["g_mix"], dx1, f"in_proj_bwd{l}")
        dw_in = _mm_tn(a["h"], dz, f"dw_in{l}", out_dtype=_WIRE)
        big[l] = [_unpad_in_cols(dw_in).reshape(NDEV, d // NDEV, N_IN), dw_out.reshape(NDEV, d // NDEV, d), dw_up,
                  dw_down.reshape(NDEV, dw_down.shape[0] // NDEV, d)]
        sm[l] = [dg_mix, dg_go, dg_ffn, dg_sgu, dw_s, db_t[:, :N_HEADS].T, db_f[0, :N_HEADS], dgq, dgkv, _unpad_uq(dwq),
                 _join_ukv(dwk, dwv)]
    return lrow, dx, big, sm, dg_final


def _reduce_and_update(loss, grad_x, big, sm, dg_final, me, given):
    depth = len(big)
    recv = _exchange([t for l in range(depth) for t in big[l]], "scatter_grads")
    pieces = [t for l in range(depth) for t in sm[l]] + [dg_final]
    flat = jnp.concatenate([t.reshape(-1) for t in pieces])
    n_flat = flat.shape[0]
    unit = NDEV * 8 * 128
    n_pad = -(-n_flat // unit) * unit
    packed = jnp.pad(flat, (0, n_pad - n_flat)).reshape(NDEV, n_pad // (NDEV * 128), 128)
    red = _sum_slots(_exchange([packed], "scatter_small")[0], "sum_small")
    full = _all_gather([red], "gather_small")[0].reshape(-1)
    offs = np.cumsum([0] + [int(np.prod(t.shape)) for t in pieces])
    red_pieces = [full[int(offs[i]):int(offs[i + 1])].reshape(pieces[i].shape) for i in range(len(pieces))]
    per = len(sm[0])
    stack = lambda i: jnp.stack([red_pieces[l * per + i] for l in range(depth)])
    g_small = dict(g_mix_norm=stack(0), g_group_out=stack(1), g_ffn_norm=stack(2), g_sgu=stack(3), w_spatial=stack(4),
                   b_spatial=stack(5), b_forget=stack(6), g_mla_q=stack(7), g_mla_kv=stack(8), g_final=red_pieces[-1])
    cq, ckv = given["w_uq"][0].shape[2], given["w_ukv"][0].shape[2]
    g_small["w_uq"] = lax.dynamic_slice_in_dim(stack(9), me * cq, cq, axis=2)
    g_small["w_ukv"] = lax.dynamic_slice_in_dim(stack(10), me * ckv, ckv, axis=2)

    names = list(given)
    big_idx = dict(w_in=0, w_out=1, w_up=2, w_down=3)
    outs = {}
    for nme in names:
        wv_, mv_, vv_ = given[nme]
        shape = wv_.shape
        if nme in big_idx:
            res = []
            for l in range(depth):
                parts = recv[4 * l + big_idx[nme]]
                two = lambda t: t[l].reshape(-1, shape[-1])
                res.append(_adamw(parts, two(wv_), two(mv_), two(vv_), f"adamw_{nme}{l}"))
            outs[nme] = [jnp.stack([res[l][i] for l in range(depth)]).reshape(shape) for i in range(4)]
        else:
            two = lambda t: t.reshape(-1, shape[-1]) if t.ndim > 1 else t.reshape(1, -1)
            res = _adamw(two(g_small[nme]), two(wv_), two(mv_), two(vv_), f"adamw_{nme}")
            outs[nme] = [r.reshape(shape) for r in res]
    return (loss, grad_x, *[outs[n][0] for n in names], *[outs[n][1] for n in names], *[outs[n][2] for n in names],
            *[outs[n][3] for n in names])
```

```python
import functools

import jax
import jax.numpy as jnp
import numpy as np
from jax import lax
from jax.experimental import pallas as pl
from jax.experimental.pallas import tpu as pltpu

F32 = jnp.float32
_MXU = jnp.bfloat16
_WIRE = jnp.bfloat16
EPS = 1e-6
NDEV = 8
AXES = ("x", "y", "c")
MESH = pl.DeviceIdType.MESH

N_HEADS = 4
HEAD_DIM = 64
GROUP = 256
CHUNK = 128
NZ = 2816
N_IN = 2724
MISC_F, MISC_KR = 0, 32
VMEM_LIMIT = 56 * 1024 * 1024

ADAM_LR, ADAM_B1, ADAM_B2, ADAM_EPS, ADAM_WD, ADAM_STEP = 0.001, 0.9, 0.999, 1e-08, 0.01, 10

SDS = jax.ShapeDtypeStruct


def _cp(*sem):
    return pltpu.CompilerParams(dimension_semantics=sem, vmem_limit_bytes=VMEM_LIMIT)


def _dot(a, b):
    return jnp.dot(a.astype(_MXU), b.astype(_MXU), preferred_element_type=F32)


def _dot_nt(a, b):
    return lax.dot_general(a.astype(_MXU), b.astype(_MXU), (((1,), (1,)), ((), ())), preferred_element_type=F32)


def _dot_tn(a, b):
    return lax.dot_general(a.astype(_MXU), b.astype(_MXU), (((0,), (0,)), ((), ())), preferred_element_type=F32)


def _dot_exact(a, b, dims=(((1,), (0,)), ((), ()))):
    return lax.dot_general(a, b, dims, precision=lax.Precision.HIGHEST, preferred_element_type=F32)


def _rms(x, g):
    return x * lax.rsqrt(jnp.mean(x * x, axis=-1, keepdims=True) + EPS) * g


def _rms_bwd(x, g, dy):
    xh = x * lax.rsqrt(jnp.mean(x * x, axis=-1, keepdims=True) + EPS)
    dxh = dy * g
    r = lax.rsqrt(jnp.mean(x * x, axis=-1, keepdims=True) + EPS)
    dx = r * (dxh - xh * jnp.mean(dxh * xh, axis=-1, keepdims=True))
    return dx, jnp.sum(dy * xh, axis=0, keepdims=True)


def _standardize(t):
    mu = jnp.mean(t, axis=-1, keepdims=True)
    tc = t - mu
    rs = lax.rsqrt(jnp.mean(tc * tc, axis=-1, keepdims=True) + EPS)
    return tc * rs, rs


def _standardize_bwd(yh, rs, dy):
    return rs * (dy - jnp.mean(dy, axis=-1, keepdims=True) - yh * jnp.mean(dy * yh, axis=-1, keepdims=True))


_GELU_C = 0.7978845608028654


def _gelu(x):
    return 0.5 * x * (1.0 + jnp.tanh(_GELU_C * (x + 0.044715 * x * x * x)))


def _gelu_grad(x):
    t = jnp.tanh(_GELU_C * (x + 0.044715 * x * x * x))
    return 0.5 * (1.0 + t) + 0.5 * x * (1.0 - t * t) * _GELU_C * (1.0 + 3 * 0.044715 * x * x)


def _sigmoid(x):
    return 1.0 / (1.0 + jnp.exp(-x))


def _swap_half(t, half):
    n = t.shape[-1]
    lane = lax.broadcasted_iota(jnp.int32, t.shape, t.ndim - 1)
    return jnp.where((lane % (2 * half)) < half, pltpu.roll(t, n - half, t.ndim - 1), pltpu.roll(t, half, t.ndim - 1))


def _rope(t, cos, sin, half):
    return t * cos + _swap_half(t, half) * sin


def _rope_bwd(d, cos, sin, half):
    return d * cos - _swap_half(d, half) * sin


def _tables(s):
    pos = jnp.arange(s, dtype=F32)[:, None]

    def cs(half):
        inv = jnp.power(10000.0, -jnp.arange(half, dtype=F32) / half)
        ang = pos * inv[None, :]
        return jnp.cos(ang), jnp.sin(ang)

    c32, s32 = cs(32)
    c16, s16 = cs(16)
    z = lambda w: jnp.zeros((s, w), F32)
    o = lambda w: jnp.ones((s, w), F32)
    t = {}
    t["b_cos"] = jnp.tile(jnp.concatenate([c32, c32], 1), (1, 4))
    t["b_sin"] = jnp.tile(jnp.concatenate([-s32, s32], 1), (1, 4))
    t["q_cos"] = jnp.tile(jnp.concatenate([o(64), c16, c16, z(32)], 1), (1, 4))
    t["q_sin"] = jnp.tile(jnp.concatenate([z(64), -s16, s16, z(32)], 1), (1, 4))
    t["k_cos"] = jnp.concatenate([z(32), c16, c16, z(64)], 1)
    t["k_sin"] = jnp.concatenate([z(32), -s16, s16, z(64)], 1)
    lg = jnp.log1p(-jnp.exp2(-5.0 - jnp.arange(N_HEADS, dtype=F32)))
    j = jnp.arange(CHUNK, dtype=F32)
    rel = j[:, None] - j[None, :]
    t["decay"] = jnp.where(rel[None] >= 0, jnp.exp(jnp.maximum(rel, 0.0)[None] * lg[:, None, None]), 0.0)

    def rows(e):
        return jnp.repeat(e.T, HEAD_DIM, axis=1)

    t["qw"] = rows(jnp.exp((j + 1.0)[None, :] * lg[:, None]))
    t["kw"] = rows(jnp.exp((CHUNK - 1 - j)[None, :] * lg[:, None]))
    t["kw2"] = rows(jnp.exp((CHUNK - j)[None, :] * lg[:, None]))
    t["qw0"] = rows(jnp.exp(j[None, :] * lg[:, None]))
    t["cd"] = jnp.repeat(jnp.exp(CHUNK * lg), HEAD_DIM)[None, :]
    e = np.zeros((128, 512), np.float32)
    for h in range(N_HEADS):
        for r in range(32):
            e[MISC_KR + r, 128 * h + 64 + r] = 1.0
    t["place"] = jnp.asarray(e)
    return t


def _norm_matmul(x, g, w, name):
    s, d = x.shape
    n = w.shape[1]
    tm, tn = min(512, s), 256

    def body(x_ref, g_ref, w_ref, z_ref, h_ref):
        @pl.when(pl.program_id(1) == 0)
        def _():
            h_ref[...] = _rms(x_ref[...], g_ref[...]).astype(h_ref.dtype)

        z_ref[...] = jnp.dot(h_ref[...], w_ref[...], preferred_element_type=F32)

    return pl.pallas_call(
        body, grid=(s // tm, n // tn),
        in_specs=[pl.BlockSpec((tm, d), lambda i, j: (i, 0)), pl.BlockSpec((1, d), lambda i, j: (0, 0)),
                  pl.BlockSpec((d, tn), lambda i, j: (0, j))],
        out_specs=[pl.BlockSpec((tm, tn), lambda i, j: (i, j)), pl.BlockSpec((tm, d), lambda i, j: (i, 0))],
        out_shape=[SDS((s, n), F32), SDS((s, d), _MXU)],
        compiler_params=_cp("parallel", "arbitrary"), name=name)(x, g, w)


def _mm_tn(a, b, name, *, a_fn=None, blocked=False, out_dtype=F32):
    k, m = a.shape
    n = b.shape[1]
    tm, tk = min(512, m), min(1024, k)
    tn = next(t for t in (512, 256, 128) if n % t == 0)
    assert m % tm == 0 and k % tk == 0
    nk = k // tk

    def body(a_ref, b_ref, o_ref, acc):
        kk = pl.program_id(2)

        @pl.when(kk == 0)
        def _():
            acc[...] = jnp.zeros_like(acc)

        av = a_ref[...]
        if a_fn is not None:
            av = a_fn(av.astype(F32))
        acc[...] += _dot_tn(av, b_ref[...])

        @pl.when(kk == nk - 1)
        def _():
            o_ref[...] = acc[...].reshape(o_ref.shape).astype(o_ref.dtype)

    if blocked:
        assert tn == 512
        out_spec = pl.BlockSpec((1, tm, tn), lambda i, j, kk: (j, i, 0))
        out_shape = SDS((n // tn, m, tn), out_dtype)
    else:
        out_spec = pl.BlockSpec((tm, tn), lambda i, j, kk: (i, j))
        out_shape = SDS((m, n), out_dtype)
    return pl.pallas_call(
        body, grid=(m // tm, n // tn, nk),
        in_specs=[pl.BlockSpec((tk, tm), lambda i, j, kk: (kk, i)), pl.BlockSpec((tk, tn), lambda i, j, kk: (kk, j))],
        out_specs=out_spec, out_shape=out_shape, scratch_shapes=[pltpu.VMEM((tm, tn), F32)],
        compiler_params=_cp("parallel", "parallel", "arbitrary"), name=name)(a, b)


def _sgu_parts(u_pre, v_pre, gain):
    u = _gelu(u_pre)
    v = _gelu(v_pre)
    vh, rs, vg = [], [], []
    for h in range(N_HEADS):
        sl = slice(HEAD_DIM * h, HEAD_DIM * (h + 1))
        a, r = _standardize(v[:, sl])
        vh.append(a)
        rs.append(r)
        vg.append(a * gain[:, sl])
    return u, vh, rs, vg


def _tril(w):
    r = lax.broadcasted_iota(jnp.int32, w.shape, 0)
    c = lax.broadcasted_iota(jnp.int32, w.shape, 1)
    return jnp.where(r >= c, w, 0.0)


def _sgu_fwd(z, gain, w_s, b_t, name):
    s = z.shape[0]
    tm = min(512, s)

    def body(u_ref, v_ref, g_ref, w_ref, b_ref, y_ref):
        u, _, _, vg = _sgu_parts(u_ref[...], v_ref[...], g_ref[...])
        for h in range(N_HEADS):
            sl = slice(HEAD_DIM * h, HEAD_DIM * (h + 1))
            wc = _tril(w_ref[h])
            for c in range(tm // CHUNK):
                r = slice(CHUNK * c, CHUNK * (c + 1))
                mixed = _dot(wc, vg[h][r]) + b_ref[:, h:h + 1]
                y_ref[r, sl] = u[r, sl] * mixed

    return pl.pallas_call(
        body, grid=(s // tm,),
        in_specs=[pl.BlockSpec((tm, GROUP), lambda i: (i, 0)), pl.BlockSpec((tm, GROUP), lambda i: (i, 1)),
                  pl.BlockSpec((1, GROUP), lambda i: (0, 0)), pl.BlockSpec((N_HEADS, CHUNK, CHUNK), lambda i: (0, 0, 0)),
                  pl.BlockSpec((CHUNK, 128), lambda i: (0, 0))],
        out_specs=pl.BlockSpec((tm, GROUP), lambda i: (i, 0)), out_shape=SDS((s, GROUP), F32),
        compiler_params=_cp("parallel"), name=name)(z, z, gain, w_s, b_t)


def _sgu_bwd(dy, z, gain, w_s, b_t, name):
    s = z.shape[0]
    tm = min(512, s)

    def body(dy_ref, u_ref, v_ref, g_ref, w_ref, b_ref, dz_ref, dg_ref, dw_ref, db_ref):
        @pl.when(pl.program_id(0) == 0)
        def _():
            dg_ref[...] = jnp.zeros_like(dg_ref)
            dw_ref[...] = jnp.zeros_like(dw_ref)
            db_ref[...] = jnp.zeros_like(db_ref)

        u_pre, v_pre, gain_v = u_ref[...], v_ref[...], g_ref[...]
        u, vh, rs, vg = _sgu_parts(u_pre, v_pre, gain_v)
        dyv = dy_ref[...]
        gu = _gelu_grad(u_pre)
        gv = _gelu_grad(v_pre)
        for h in range(N_HEADS):
            sl = slice(HEAD_DIM * h, HEAD_DIM * (h + 1))
            wc = _tril(w_ref[h])
            dwh = jnp.zeros((CHUNK, CHUNK), F32)
            dbh = jnp.zeros((CHUNK, 1), F32)
            dgh = jnp.zeros((1, HEAD_DIM), F32)
            for c in range(tm // CHUNK):
                r = slice(CHUNK * c, CHUNK * (c + 1))
                mixed = _dot(wc, vg[h][r]) + b_ref[:, h:h + 1]
                dz_ref[r, sl] = (dyv[r, sl] * mixed * gu[r, sl]).astype(dz_ref.dtype)
                dm = dyv[r, sl] * u[r, sl]
                dwh += _dot_nt(dm, vg[h][r])
                dbh += jnp.sum(dm, axis=1, keepdims=True)
                dvg = _dot_tn(wc, dm)
                dgh += jnp.sum(dvg * vh[h][r], axis=0, keepdims=True)
                dv = _standardize_bwd(vh[h][r], rs[h][r], dvg * gain_v[:, sl])
                dz_ref[r, GROUP + HEAD_DIM * h:GROUP + HEAD_DIM * (h + 1)] = (dv * gv[r, sl]).astype(dz_ref.dtype)
            dw_ref[h] += _tril(dwh)
            db_ref[:, h:h + 1] += dbh
            dg_ref[:, sl] += dgh

    return pl.pallas_call(
        body, grid=(s // tm,),
        in_specs=[pl.BlockSpec((tm, GROUP), lambda i: (i, 0)),
                  pl.BlockSpec((tm, GROUP), lambda i: (i, 0)), pl.BlockSpec((tm, GROUP), lambda i: (i, 1)),
                  pl.BlockSpec((1, GROUP), lambda i: (0, 0)), pl.BlockSpec((N_HEADS, CHUNK, CHUNK), lambda i: (0, 0, 0)),
                  pl.BlockSpec((CHUNK, 128), lambda i: (0, 0))],
        out_specs=[pl.BlockSpec((tm, 2 * GROUP), lambda i: (i, 0)), pl.BlockSpec((1, GROUP), lambda i: (0, 0)),
                   pl.BlockSpec((N_HEADS, CHUNK, CHUNK), lambda i: (0, 0, 0)), pl.BlockSpec((CHUNK, 128), lambda i: (0, 0))],
        out_shape=[SDS((s, 2 * GROUP), _MXU), SDS((1, GROUP), F32), SDS((N_HEADS, CHUNK, CHUNK), F32), SDS((CHUNK, 128), F32)],
        compiler_params=_cp("arbitrary"), name=name)(dy, z, z, gain, w_s, b_t)


_SCALE_B = HEAD_DIM ** -0.5


def _ret_fwd(z, tb, name):
    s = z.shape[0]
    nc = s // CHUNK
    row = lambda col: pl.BlockSpec((CHUNK, GROUP), lambda n, col=col: (n, col))
    const = lambda shape: pl.BlockSpec(shape, lambda n: (0,) * len(shape))

    def body(q_ref, k_ref, v_ref, g_ref, cos_ref, sin_ref, dec_ref, qw_ref, kw_ref, cd_ref, y_ref, o_ref, st_ref, state):
        @pl.when(pl.program_id(0) == 0)
        def _():
            state[...] = jnp.zeros_like(state)

        q = _rope(q_ref[...], cos_ref[...], sin_ref[...], 32)
        k = _rope(k_ref[...], cos_ref[...], sin_ref[...], 32) * _SCALE_B
        v = v_ref[...]
        g = g_ref[...]
        st_ref[0] = state[...]
        qs = q * qw_ref[...]
        ks = k * kw_ref[...]
        for h in range(N_HEADS):
            sl = slice(HEAD_DIM * h, HEAD_DIM * (h + 1))
            sc = _dot_nt(q[:, sl], k[:, sl]) * dec_ref[h]
            o = _dot(sc, v[:, sl]) + _dot(qs[:, sl], state[:, sl])
            o_ref[:, sl] = o
            yh, _ = _standardize(o)
            gh = g[:, sl]
            y_ref[:, sl] = gh * _sigmoid(gh) * yh
            state[:, sl] = cd_ref[:, sl] * state[:, sl] + _dot_tn(ks[:, sl], v[:, sl])

    return pl.pallas_call(
        body, grid=(nc,),
        in_specs=[row(2), row(3), row(4), row(5), pl.BlockSpec((CHUNK, GROUP), lambda n: (n, 0)),
                  pl.BlockSpec((CHUNK, GROUP), lambda n: (n, 0)), const((N_HEADS, CHUNK, CHUNK)),
                  const((CHUNK, GROUP)), const((CHUNK, GROUP)), const((1, GROUP))],
        out_specs=[pl.BlockSpec((CHUNK, GROUP), lambda n: (n, 0)), pl.BlockSpec((CHUNK, GROUP), lambda n: (n, 0)),
                   pl.BlockSpec((1, HEAD_DIM, GROUP), lambda n: (n, 0, 0))],
        out_shape=[SDS((s, GROUP), F32), SDS((s, GROUP), F32), SDS((nc, HEAD_DIM, GROUP), F32)],
        scratch_shapes=[pltpu.VMEM((HEAD_DIM, GROUP), F32)],
        compiler_params=_cp("arbitrary"), name=name)(z, z, z, z, tb["b_cos"], tb["b_sin"], tb["decay"], tb["qw"], tb["kw"], tb["cd"])


def _ret_bwd(dy, z, o_pre, states, tb, name):
    s = z.shape[0]
    nc = s // CHUNK
    rev = lambda col: pl.BlockSpec((CHUNK, GROUP), lambda n, col=col: (nc - 1 - n, col))
    const = lambda shape: pl.BlockSpec(shape, lambda n: (0,) * len(shape))

    def body(dy_ref, q_ref, k_ref, v_ref, g_ref, o_ref, st_ref, cos_ref, sin_ref, dec_ref, qw_ref, kw2_ref, qw0_ref, cd_ref,
             dz_ref, rstate):
        @pl.when(pl.program_id(0) == 0)
        def _():
            rstate[...] = jnp.zeros_like(rstate)

        cos, sin = cos_ref[...], sin_ref[...]
        q = _rope(q_ref[...], cos, sin, 32)
        k = _rope(k_ref[...], cos, sin, 32) * _SCALE_B
        v = v_ref[...]
        g = g_ref[...]
        dyv = dy_ref[...]
        sg = _sigmoid(g)
        silu = g * sg
        dos, dgs = [], []
        for h in range(N_HEADS):
            sl = slice(HEAD_DIM * h, HEAD_DIM * (h + 1))
            yh, rs = _standardize(o_ref[:, sl])
            dgs.append(dyv[:, sl] * yh * (sg[:, sl] * (1.0 + g[:, sl] * (1.0 - sg[:, sl]))))
            dos.append(_standardize_bwd(yh, rs, dyv[:, sl] * silu[:, sl]))
        do = jnp.concatenate(dos, axis=1)
        dow = do * qw_ref[...]
        vw = v * kw2_ref[...]
        kw = k * kw2_ref[...]
        q0 = q * qw0_ref[...]
        dqs, dks = [], []
        for h in range(N_HEADS):
            sl = slice(HEAD_DIM * h, HEAD_DIM * (h + 1))
            dec = dec_ref[h]
            p = _dot_nt(q[:, sl], k[:, sl]) * dec
            dp = _dot_nt(do[:, sl], v[:, sl]) * dec
            sn = st_ref[0][:, sl]
            rr = rstate[:, sl]
            dqs.append(_dot(dp, k[:, sl]) + _dot_nt(dow[:, sl], sn))
            dks.append(_dot_tn(dp, q[:, sl]) + _dot_nt(vw[:, sl], rr))
            dv = _dot_tn(p, do[:, sl]) + _dot(kw[:, sl], rr)
            dz_ref[:, 2 * GROUP + HEAD_DIM * h:2 * GROUP + HEAD_DIM * (h + 1)] = dv.astype(dz_ref.dtype)
            rstate[:, sl] = cd_ref[:, sl] * rr + _dot_tn(q0[:, sl], do[:, sl])
        dq = _rope_bwd(jnp.concatenate(dqs, axis=1), cos, sin, 32)
        dk = _rope_bwd(jnp.concatenate(dks, axis=1) * _SCALE_B, cos, sin, 32)
        dz_ref[:, 0:GROUP] = dq.astype(dz_ref.dtype)
        dz_ref[:, GROUP:2 * GROUP] = dk.astype(dz_ref.dtype)
        dz_ref[:, 3 * GROUP:4 * GROUP] = jnp.concatenate(dgs, axis=1).astype(dz_ref.dtype)

    r0 = lambda: pl.BlockSpec((CHUNK, GROUP), lambda n: (nc - 1 - n, 0))
    return pl.pallas_call(
        body, grid=(nc,),
        in_specs=[r0(), rev(2), rev(3), rev(4), rev(5), r0(), pl.BlockSpec((1, HEAD_DIM, GROUP), lambda n: (nc - 1 - n, 0, 0)),
                  r0(), r0(), const((N_HEADS, CHUNK, CHUNK)), const((CHUNK, GROUP)), const((CHUNK, GROUP)),
                  const((CHUNK, GROUP)), const((1, GROUP))],
        out_specs=pl.BlockSpec((CHUNK, 4 * GROUP), lambda n: (nc - 1 - n, 0)),
        out_shape=SDS((s, 4 * GROUP), _MXU), scratch_shapes=[pltpu.VMEM((HEAD_DIM, GROUP), F32)],
        compiler_params=_cp("arbitrary"), name=name)(
            dy, z, z, z, z, o_pre, states, tb["b_cos"], tb["b_sin"], tb["decay"], tb["qw"], tb["kw2"], tb["qw0"], tb["cd"])


TQ = 256


def _log_sigmoid(x):
    return jnp.minimum(x, 0.0) - jnp.log1p(jnp.exp(-jnp.abs(x)))


def _fox_prep(z, b_f, name):
    s = z.shape[0]
    nb = s // TQ

    def body(m_ref, b_ref, cc_ref, cr_ref, carry):
        @pl.when(pl.program_id(0) == 0)
        def _():
            carry[...] = jnp.zeros_like(carry)

        lane = lax.broadcasted_iota(jnp.int32, (TQ, 128), 1)
        logf = jnp.where(lane < N_HEADS, _log_sigmoid(m_ref[...] + b_ref[...]), 0.0)
        r = lax.broadcasted_iota(jnp.int32, (TQ, TQ), 0)
        c = lax.broadcasted_iota(jnp.int32, (TQ, TQ), 1)
        tri = jnp.where(r >= c, 1.0, 0.0).astype(F32)
        cum = _dot_exact(tri, logf) + carry[...]
        cc_ref[...] = cum
        cr_ref[0] = cum.T[0:8, :]
        carry[...] = cum[TQ - 1:TQ, :]

    return pl.pallas_call(
        body, grid=(nb,),
        in_specs=[pl.BlockSpec((TQ, 128), lambda i: (i, NZ // 128 - 1)), pl.BlockSpec((1, 128), lambda i: (0, 0))],
        out_specs=[pl.BlockSpec((TQ, 128), lambda i: (i, 0)), pl.BlockSpec((1, 8, TQ), lambda i: (i, 0, 0))],
        out_shape=[SDS((s, 128), F32), SDS((nb, 8, TQ), F32)], scratch_shapes=[pltpu.VMEM((1, 128), F32)],
        compiler_params=_cp("arbitrary"), name=name)(z, b_f)


def _fox_post(dcr, dcq, z, b_f, dkr, name):
    s = z.shape[0]
    nb = s // TQ

    def body(dc_ref, dcq_ref, m_ref, b_ref, dkr_ref, dz_ref, db_ref, carry):
        @pl.when(pl.program_id(0) == 0)
        def _():
            carry[...] = jnp.zeros_like(carry)
            db_ref[...] = jnp.zeros_like(db_ref)

        r = lax.broadcasted_iota(jnp.int32, (TQ, TQ), 0)
        c = lax.broadcasted_iota(jnp.int32, (TQ, TQ), 1)
        triu = jnp.where(c >= r, 1.0, 0.0).astype(F32)
        dc = jnp.concatenate([dc_ref[0], jnp.zeros((120, TQ), F32)], axis=0)
        dlogf = _dot_exact(triu, dc, (((1,), (1,)), ((), ()))) + _dot_exact(triu, dcq_ref[...]) + carry[...]
        carry[...] = dlogf[0:1, :]
        x = m_ref[...] + b_ref[...]
        lane = lax.broadcasted_iota(jnp.int32, (TQ, 128), 1)
        df = jnp.where(lane < N_HEADS, dlogf * _sigmoid(-x), 0.0)
        db_ref[...] += jnp.sum(df, axis=0, keepdims=True)
        dz_ref[...] = (df + dkr_ref[...]).astype(dz_ref.dtype)

    rv = lambda i: nb - 1 - i
    return pl.pallas_call(
        body, grid=(nb,),
        in_specs=[pl.BlockSpec((1, 8, TQ), lambda i: (rv(i), 0, 0)), pl.BlockSpec((TQ, 128), lambda i: (rv(i), 0)),
                  pl.BlockSpec((TQ, 128), lambda i: (rv(i), NZ // 128 - 1)),
                  pl.BlockSpec((1, 128), lambda i: (0, 0)), pl.BlockSpec((TQ, 128), lambda i: (rv(i), 0))],
        out_specs=[pl.BlockSpec((TQ, 128), lambda i: (rv(i), 0)), pl.BlockSpec((1, 128), lambda i: (0, 0))],
        out_shape=[SDS((s, 128), _MXU), SDS((1, 128), F32)], scratch_shapes=[pltpu.VMEM((1, 128), F32)],
        compiler_params=_cp("arbitrary"), name=name)(dcr, dcq, z, b_f, dkr)


NEG = -1e30


def _causal_mask(shape, transposed=False):
    r = lax.broadcasted_iota(jnp.int32, shape, 0)
    c = lax.broadcasted_iota(jnp.int32, shape, 1)
    return (c >= r) if transposed else (r >= c)


def _flash_fwd(q, k, v, cols, dqk, scale, cum, name):
    s = q.shape[0]
    nq = s // TQ
    wq = N_HEADS * dqk
    bias = cum is not None

    def body(*refs):
        if bias:
            q_ref, k_ref, v_ref, cc_ref, cr_ref, o_ref, l_ref = refs
        else:
            q_ref, k_ref, v_ref, o_ref, l_ref = refs
        i = pl.program_id(0)
        l_ref[...] = jnp.zeros_like(l_ref)
        for h in range(N_HEADS):
            qh = q_ref[:, dqk * h:dqk * (h + 1)].astype(_MXU)
            cq = cc_ref[:, h:h + 1] if bias else None

            def step(j, carry, masked):
                m, l, acc = carry
                r0 = pl.multiple_of(j * TQ, TQ)
                kh = k_ref[pl.ds(r0, TQ), dqk * h:dqk * (h + 1)]
                vh = v_ref[pl.ds(r0, TQ), HEAD_DIM * h:HEAD_DIM * (h + 1)]
                sc = _dot_nt(qh, kh) * scale
                if bias:
                    sc = sc + (cq - cr_ref[j][h:h + 1, :])
                if masked:
                    sc = jnp.where(_causal_mask(sc.shape), sc, NEG)
                m_new = jnp.maximum(m, jnp.max(sc, axis=-1, keepdims=True))
                alpha = jnp.exp(m - m_new)
                p = jnp.exp(sc - m_new)
                return m_new, alpha * l + jnp.sum(p, axis=-1, keepdims=True), alpha * acc + _dot(p, vh)

            init = (jnp.full((TQ, 1), NEG, F32), jnp.zeros((TQ, 1), F32), jnp.zeros((TQ, HEAD_DIM), F32))
            carry = lax.fori_loop(0, i, functools.partial(step, masked=False), init)
            m, l, acc = step(i, carry, True)
            o_ref[:, HEAD_DIM * h:HEAD_DIM * (h + 1)] = acc / l
            l_ref[:, h:h + 1] = m + jnp.log(l)

    in_specs = [pl.BlockSpec((TQ, wq), lambda i: (i, cols[0])), pl.BlockSpec((s, wq), lambda i: (0, cols[1])),
                pl.BlockSpec((s, GROUP), lambda i: (0, cols[2]))]
    args = [q, k, v]
    if bias:
        in_specs += [pl.BlockSpec((TQ, 128), lambda i: (i, 0)), pl.BlockSpec((nq, 8, TQ), lambda i: (0, 0, 0))]
        args += list(cum)
    return pl.pallas_call(
        body, grid=(nq,), in_specs=in_specs,
        out_specs=[pl.BlockSpec((TQ, GROUP), lambda i: (i, 0)), pl.BlockSpec((TQ, 128), lambda i: (i, 0))],
        out_shape=[SDS((s, GROUP), F32), SDS((s, 128), F32)],
        compiler_params=_cp("parallel"), name=name)(*args)


def _flash_bwd(q, k, v, cols, dqk, scale, cum, do, lse, delta, name, kv_dtype):
    s = q.shape[0]
    nq = s // TQ
    wq = N_HEADS * dqk
    bias = cum is not None

    def body(*refs):
        if bias:
            q_ref, k_ref, v_ref, do_ref, l_ref, d_ref, cc_ref, cr_ref, dq_ref, dk_ref, dv_ref, dc_ref, dcq_ref = refs
        else:
            q_ref, k_ref, v_ref, do_ref, l_ref, d_ref, dq_ref, dk_ref, dv_ref = refs
        j = pl.program_id(0)

        @pl.when(j == 0)
        def _():
            dq_ref[...] = jnp.zeros_like(dq_ref)
            if bias:
                dcq_ref[...] = jnp.zeros_like(dcq_ref)

        if bias:
            dc_ref[...] = jnp.zeros_like(dc_ref)
        for h in range(N_HEADS):
            hq = slice(dqk * h, dqk * (h + 1))
            hv = slice(HEAD_DIM * h, HEAD_DIM * (h + 1))
            kh = k_ref[:, hq].astype(_MXU)
            vh = v_ref[:, hv].astype(_MXU)
            ck = cr_ref[0][h:h + 1, :] if bias else None

            def step(i, carry, masked):
                dk, dv, dc = carry
                r0 = pl.multiple_of(i * TQ, TQ)
                rows = pl.ds(r0, TQ)
                qh = q_ref[rows, hq].astype(_MXU)
                doh = do_ref[rows, hv].astype(_MXU)
                sc = _dot_nt(qh, kh) * scale
                if bias:
                    sc = sc + (cc_ref[rows, h:h + 1] - ck)
                p = jnp.exp(sc - l_ref[rows, h:h + 1])
                if masked:
                    p = jnp.where(_causal_mask(p.shape), p, 0.0)
                dv = dv + _dot_tn(p, doh)
                dp = _dot_nt(doh, vh)
                ds = p * (dp - d_ref[rows, h:h + 1])
                dk = dk + _dot_tn(ds, qh) * scale
                dq_ref[rows, hq] += _dot(ds, kh) * scale
                if bias:
                    dc = dc + jnp.sum(ds, axis=0, keepdims=True)
                    dcq_ref[rows, h:h + 1] += jnp.sum(ds, axis=1, keepdims=True)
                return dk, dv, dc

            init = (jnp.zeros((TQ, dqk), F32), jnp.zeros((TQ, HEAD_DIM), F32), jnp.zeros((1, TQ), F32))
            carry = step(j, init, True)
            dk, dv, dc = lax.fori_loop(j + 1, nq, functools.partial(step, masked=False), carry)
            dk_ref[:, hq] = dk.astype(dk_ref.dtype)
            dv_ref[:, hv] = dv.astype(dv_ref.dtype)
            if bias:
                dc_ref[0, h:h + 1, :] = -dc

    full = lambda w, c=0: pl.BlockSpec((s, w), lambda j, c=c: (0, c))
    in_specs = [full(wq, cols[0]), pl.BlockSpec((TQ, wq), lambda j: (j, cols[1])), pl.BlockSpec((TQ, GROUP), lambda j: (j, cols[2])),
                full(GROUP), full(128), full(128)]
    args = [q, k, v, do, lse, delta]
    out_specs = [full(wq), pl.BlockSpec((TQ, wq), lambda j: (j, 0)), pl.BlockSpec((TQ, GROUP), lambda j: (j, 0))]
    out_shape = [SDS((s, wq), F32), SDS((s, wq), kv_dtype), SDS((s, GROUP), kv_dtype)]
    if bias:
        in_specs += [full(128), pl.BlockSpec((1, 8, TQ), lambda j: (j, 0, 0))]
        args += list(cum)
        out_specs += [pl.BlockSpec((1, 8, TQ), lambda j: (j, 0, 0)), full(128)]
        out_shape += [SDS((nq, 8, TQ), F32), SDS((s, 128), F32)]
    return pl.pallas_call(body, grid=(nq,), in_specs=in_specs, out_specs=out_specs, out_shape=out_shape,
                          compiler_params=_cp("arbitrary"), name=name)(*args)


def _head_lanes(h, dqk):
    return slice(128 * (h // 2), 128 * (h // 2) + 128) if dqk == HEAD_DIM else slice(128 * h, 128 * h + 128)


def _keep_half(x, a, axis):
    idx = lax.broadcasted_iota(jnp.int32, x.shape, axis)
    return jnp.where((idx < HEAD_DIM) if a == 0 else (idx >= HEAD_DIM), x, jnp.zeros_like(x))


def _kv_prep(z, kcol, vcol, name):
    s = z.shape[0]
    nk = s // TQ

    def body(k_ref, v_ref, kb_ref, vb_ref, vt_ref):
        kb_ref[...] = k_ref[...].astype(_MXU)
        v = v_ref[...]
        vb_ref[...] = v.astype(_MXU)
        vt_ref[0] = v.T.astype(_MXU)

    blk = pl.BlockSpec((TQ, GROUP), lambda i: (i, 0))
    return pl.pallas_call(
        body, grid=(nk,),
        in_specs=[pl.BlockSpec((TQ, GROUP), lambda i: (i, kcol)), pl.BlockSpec((TQ, GROUP), lambda i: (i, vcol))],
        out_specs=[blk, blk, pl.BlockSpec((1, GROUP, TQ), lambda i: (i, 0, 0))],
        out_shape=[SDS((s, GROUP), _MXU), SDS((s, GROUP), _MXU), SDS((nk, GROUP, TQ), _MXU)],
        compiler_params=_cp("parallel"), name=name)(z, z)


def _attn_fwd(q, qcol, dqk, kb, vt, scale, cum, name, comm=None):
    s = q.shape[0]
    nq = s // TQ
    wq = N_HEADS * dqk
    bias = cum is not None

    def body(*refs):
        ins, (o_ref, l_ref), _, cc = _split_refs(refs, 5 if bias else 3, 2, comm)
        if bias:
            q_ref, k_ref, vt_ref, cc_ref, cr_ref = ins
        else:
            q_ref, k_ref, vt_ref = ins
        i = pl.program_id(0)
        if comm is not None:
            @pl.when(i == 0)
            def _():
                comm.start(*cc)

        qts = []
        for h in range(N_HEADS):
            qt = q_ref[:, _head_lanes(h, dqk)].astype(F32).T
            qts.append((_keep_half(qt, h % 2, 0) if dqk == HEAD_DIM else qt).astype(_MXU))
        cqs = [cr_ref[0][h:h + 1, :] for h in range(N_HEADS)] if bias else None

        def step(j, carry, masked):
            r0 = pl.multiple_of(j * TQ, TQ)
            vtj = vt_ref[j]
            out = []
            for h in range(N_HEADS):
                m, l, acc = carry[3 * h:3 * h + 3]
                st = jnp.dot(k_ref[pl.ds(r0, TQ), _head_lanes(h, dqk)], qts[h], preferred_element_type=F32) * scale
                if bias:
                    st = st + (cqs[h] - cc_ref[pl.ds(r0, TQ), h:h + 1])
                if masked:
                    st = jnp.where(_causal_mask(st.shape, transposed=True), st, NEG)
                m_new = jnp.maximum(m, jnp.max(st, axis=0, keepdims=True))
                alpha = jnp.exp(m - m_new)
                p = jnp.exp(st - m_new)
                l = alpha * l + jnp.sum(p, axis=0, keepdims=True)
                acc = alpha * acc + jnp.dot(vtj[HEAD_DIM * h:HEAD_DIM * (h + 1), :], p.astype(_MXU),
                                            preferred_element_type=F32)
                out += [m_new, l, acc]
            return tuple(out)

        init = (jnp.full((1, TQ), NEG, F32), jnp.zeros((1, TQ), F32), jnp.zeros((HEAD_DIM, TQ), F32)) * N_HEADS
        carry = lax.fori_loop(0, i, functools.partial(step, masked=False), init)
        carry = step(i, carry, True)
        l_ref[...] = jnp.zeros_like(l_ref)
        for h in range(N_HEADS):
            l_ref[0, h:h + 1, :] = carry[3 * h] + jnp.log(carry[3 * h + 1])
        for p in range(2):
            ot = jnp.concatenate([carry[6 * p + 2] / carry[6 * p + 1], carry[6 * p + 5] / carry[6 * p + 4]], axis=0)
            o_ref[:, 128 * p:128 * (p + 1)] = ot.T
        if comm is not None:
            @pl.when(i == nq - 1)
            def _():
                comm.wait(*cc)

    rows = pl.BlockSpec((1, 8, TQ), lambda i: (i, 0, 0))
    in_specs = [pl.BlockSpec((TQ, wq), lambda i: (i, qcol)), pl.BlockSpec((s, wq), lambda i: (0, 0)),
                pl.BlockSpec((nq, GROUP, TQ), lambda i: (0, 0, 0))]
    args = [q, kb, vt]
    if bias:
        in_specs += [pl.BlockSpec((s, 128), lambda i: (0, 0)), rows]
        args += list(cum)
    out_specs = [pl.BlockSpec((TQ, GROUP), lambda i: (i, 0)), rows]
    out_shape = [SDS((s, GROUP), F32), SDS((nq, 8, TQ), F32)]
    return _call_with_comm(body, (nq,), in_specs, out_specs, out_shape, [], args, comm, ("arbitrary",), name)


def _call_with_comm(body, grid, in_specs, out_specs, out_shape, scratch, args, comm, semantics, name):
    n_out = len(out_shape)
    if comm is not None:
        in_specs, out_specs = in_specs + comm.in_specs, out_specs + comm.out_specs
        out_shape, scratch, args = out_shape + comm.out_shape, scratch + comm.scratch, list(args) + comm.arrs
    res = pl.pallas_call(body, grid=grid, in_specs=in_specs, out_specs=out_specs, out_shape=out_shape,
                         scratch_shapes=scratch, compiler_params=_cp(*semantics), name=name)(*args)
    return (*res[:n_out], list(res[n_out:]))


def _attn_bwd_prep(q, qcol, dqk, o, do, name):
    s = q.shape[0]
    nq = s // TQ
    wq = N_HEADS * dqk

    def body(q_ref, o_ref, do_ref, qb_ref, qt_ref, dob_ref, dot_ref, dl_ref):
        qv = q_ref[...].astype(F32)
        qb_ref[...] = qv.astype(_MXU)
        qt_ref[0] = qv.T.astype(_MXU)
        dov = do_ref[...]
        dob_ref[...] = dov.astype(_MXU)
        dot_ref[0] = dov.T.astype(_MXU)
        pt = (dov * o_ref[...]).T
        dl_ref[...] = jnp.zeros_like(dl_ref)
        for h in range(N_HEADS):
            dl_ref[0, h:h + 1, :] = jnp.sum(pt[HEAD_DIM * h:HEAD_DIM * (h + 1), :], axis=0, keepdims=True)

    nat = lambda w: pl.BlockSpec((TQ, w), lambda i: (i, 0))
    tr = lambda w: pl.BlockSpec((1, w, TQ), lambda i: (i, 0, 0))
    return pl.pallas_call(
        body, grid=(nq,),
        in_specs=[pl.BlockSpec((TQ, wq), lambda i: (i, qcol)), nat(GROUP), nat(GROUP)],
        out_specs=[nat(wq), tr(wq), nat(GROUP), tr(GROUP), tr(8)],
        out_shape=[SDS((s, wq), _MXU), SDS((nq, wq, TQ), _MXU), SDS((s, GROUP), _MXU), SDS((nq, GROUP, TQ), _MXU),
                   SDS((nq, 8, TQ), F32)],
        compiler_params=_cp("parallel"), name=name)(q, o, do)


def _attn_bwd(kb, vb, qb, qt, dob, dot, lse, dl, dqk, scale, cum, name, kv_dtype, comm=None):
    s = kb.shape[0]
    nq = s // TQ
    wq = N_HEADS * dqk
    bias = cum is not None

    def body(*refs):
        ins, outs, _, cc = _split_refs(refs, 10 if bias else 8, 5 if bias else 3, comm)
        if bias:
            k_ref, v_ref, q_ref, qt_ref, do_ref, dot_ref, l_ref, d_ref, cc_ref, cr_ref = ins
            dqt_ref, dk_ref, dv_ref, dck_ref, dcq_ref = outs
        else:
            k_ref, v_ref, q_ref, qt_ref, do_ref, dot_ref, l_ref, d_ref = ins
            dqt_ref, dk_ref, dv_ref = outs
        j = pl.program_id(0)

        @pl.when(j == 0)
        def _():
            if comm is not None:
                comm.start(*cc)
            dqt_ref[...] = jnp.zeros_like(dqt_ref)
            if bias:
                dcq_ref[...] = jnp.zeros_like(dcq_ref)

        ks, kts, vs = [], [], []
        for h in range(N_HEADS):
            k2 = k_ref[:, _head_lanes(h, dqk)]
            if dqk == HEAD_DIM:
                k2 = _keep_half(k2, h % 2, 1)
            ks.append(k2)
            kts.append(k2.astype(F32).T.astype(_MXU))
            vs.append(_keep_half(v_ref[:, _head_lanes(h, HEAD_DIM)], h % 2, 1))
        cks = [cc_ref[:, h:h + 1] for h in range(N_HEADS)] if bias else None

        def step(i, carry, masked):
            r0 = pl.multiple_of(i * TQ, TQ)
            rows = pl.ds(r0, TQ)
            qti, doti, li, di = qt_ref[i], dot_ref[i], l_ref[i], d_ref[i]
            cri = cr_ref[i] if bias else None
            out = []
            for h in range(N_HEADS):
                dk, dv, dck = carry[3 * h:3 * h + 3]
                ql, vl = _head_lanes(h, dqk), _head_lanes(h, HEAD_DIM)
                st = jnp.dot(ks[h], qti[ql, :], preferred_element_type=F32) * scale
                rowterm = li[h:h + 1, :]
                if bias:
                    st = st + ((cri[h:h + 1, :] - rowterm) - cks[h])
                else:
                    st = st - rowterm
                p = jnp.exp(st)
                if masked:
                    p = jnp.where(_causal_mask(p.shape, transposed=True), p, 0.0)
                dv = dv + jnp.dot(p.astype(_MXU), do_ref[rows, vl], preferred_element_type=F32)
                dpt = jnp.dot(vs[h], doti[vl, :], preferred_element_type=F32)
                dst = p * (dpt - di[h:h + 1, :])
                dsb = dst.astype(_MXU)
                dk = dk + jnp.dot(dsb, q_ref[rows, ql], preferred_element_type=F32)
                dqt_ref[i, ql, :] += jnp.dot(kts[h], dsb, preferred_element_type=F32) * scale
                if bias:
                    dck = dck + jnp.sum(dst, axis=1, keepdims=True)
                    dcq_ref[i, h:h + 1, :] += jnp.sum(dst, axis=0, keepdims=True)
                out += [dk, dv, dck]
            return tuple(out)

        init = (jnp.zeros((TQ, 128), F32), jnp.zeros((TQ, 128), F32), jnp.zeros((TQ, 1), F32)) * N_HEADS
        carry = step(j, init, True)
        carry = lax.fori_loop(j + 1, nq, functools.partial(step, masked=False), carry)
        lane = lax.broadcasted_iota(jnp.int32, (TQ, 128), 1)
        for p in range(2):
            dv_ref[:, 128 * p:128 * (p + 1)] = jnp.where(lane < HEAD_DIM, carry[6 * p + 1], carry[6 * p + 4]).astype(dv_ref.dtype)
            if dqk == HEAD_DIM:
                dk_ref[:, 128 * p:128 * (p + 1)] = (jnp.where(lane < HEAD_DIM, carry[6 * p], carry[6 * p + 3]) * scale).astype(dk_ref.dtype)
        if dqk != HEAD_DIM:
            for h in range(N_HEADS):
                dk_ref[:, 128 * h:128 * (h + 1)] = (carry[3 * h] * scale).astype(dk_ref.dtype)
        if bias:
            dck_ref[...] = jnp.zeros_like(dck_ref)
            for h in range(N_HEADS):
                dck_ref[:, h:h + 1] = -carry[3 * h + 2]
        if comm is not None:
            @pl.when(j == nq - 1)
            def _():
                comm.wait(*cc)

    blk = lambda w: pl.BlockSpec((TQ, w), lambda j: (j, 0))
    full = lambda w: pl.BlockSpec((s, w), lambda j: (0, 0))
    full3 = lambda w: pl.BlockSpec((nq, w, TQ), lambda j: (0, 0, 0))
    in_specs = [blk(wq), blk(GROUP), full(wq), full3(wq), full(GROUP), full3(GROUP), full3(8), full3(8)]
    args = [kb, vb, qb, qt, dob, dot, lse, dl]
    out_specs = [full3(wq), blk(wq), blk(GROUP)]
    out_shape = [SDS((nq, wq, TQ), F32), SDS((s, wq), kv_dtype), SDS((s, GROUP), kv_dtype)]
    if bias:
        in_specs += [blk(128), full3(8)]
        args += list(cum)
        out_specs += [blk(128), full3(8)]
        out_shape += [SDS((s, 128), F32), SDS((nq, 8, TQ), F32)]
    return _call_with_comm(body, (nq,), in_specs, out_specs, out_shape, [], args, comm, ("arbitrary",), name)


def _untranspose(xt, dtype, name):
    nq, w, _ = xt.shape

    def body(x_ref, o_ref):
        o_ref[...] = x_ref[0].T.astype(o_ref.dtype)

    return pl.pallas_call(
        body, grid=(nq,), in_specs=[pl.BlockSpec((1, w, TQ), lambda i: (i, 0, 0))],
        out_specs=pl.BlockSpec((TQ, w), lambda i: (i, 0)), out_shape=SDS((nq * TQ, w), dtype),
        compiler_params=_cp("parallel"), name=name)(xt)


_SCALE_D = (64 + 32) ** -0.5
_COL_CQ, _COL_CKV, _COL_MISC = 2304 // 256, 2560 // 128, 2688 // 128


def _mla_prep(z, gq, gkv, wq, wk, wv, tb, name):
    s = z.shape[0]
    tm = TQ
    row = lambda w, c: pl.BlockSpec((tm, w), lambda i, c=c: (i, c))
    const = lambda a: pl.BlockSpec(a.shape, lambda i: (0,) * a.ndim)

    def body(cq_ref, ckv_ref, m_ref, gq_ref, gkv_ref, wq_ref, wk_ref, wv_ref, e_ref, qc_ref, qs_ref, kc_ref, ks_ref,
             q_ref, k_ref, v_ref, vt_ref, cqn_ref, ckvn_ref):
        cqn = _rms(cq_ref[...], gq_ref[...]).astype(_MXU)
        ckvn = _rms(ckv_ref[...], gkv_ref[...]).astype(_MXU)
        cqn_ref[...] = cqn
        ckvn_ref[...] = ckvn
        q_ref[...] = _rope(_dot(cqn, wq_ref[...]), qc_ref[...], qs_ref[...], 16).astype(q_ref.dtype)
        kr = _rope(m_ref[...], kc_ref[...], ks_ref[...], 16)
        k_ref[...] = (_dot(ckvn, wk_ref[...]) + _dot(kr, e_ref[...])).astype(k_ref.dtype)
        v = _dot(ckvn, wv_ref[...])
        v_ref[...] = v.astype(v_ref.dtype)
        vt_ref[0] = v.T.astype(vt_ref.dtype)

    e = tb["place"]
    return pl.pallas_call(
        body, grid=(s // tm,),
        in_specs=[row(256, _COL_CQ), row(128, _COL_CKV), row(128, _COL_MISC), const(gq), const(gkv), const(wq), const(wk),
                  const(wv), const(e), row(512, 0), row(512, 0), row(128, 0), row(128, 0)],
        out_specs=[row(512, 0), row(512, 0), row(256, 0), pl.BlockSpec((1, GROUP, TQ), lambda i: (i, 0, 0)), row(256, 0),
                   row(128, 0)],
        out_shape=[SDS((s, 512), _MXU), SDS((s, 512), _MXU), SDS((s, 256), _MXU), SDS((s // TQ, GROUP, TQ), _MXU),
                   SDS((s, 256), _MXU), SDS((s, 128), _MXU)],
        compiler_params=_cp("parallel"), name=name)(
            z, z, z, gq, gkv, wq, wk, wv, e, tb["q_cos"], tb["q_sin"], tb["k_cos"], tb["k_sin"])


def _mla_prep_bwd(dq, dk, dv, z, cqn, ckvn, gq, gkv, wq, wk, wv, tb, name):
    s = z.shape[0]
    tm = min(512, s)
    row = lambda w, c: pl.BlockSpec((tm, w), lambda i, c=c: (i, c))
    const = lambda a: pl.BlockSpec(a.shape, lambda i: (0,) * a.ndim)
    acc = lambda shape: pl.BlockSpec(shape, lambda i: (0, 0))

    def body(dq_ref, dk_ref, dv_ref, cq_ref, ckv_ref, cqn_ref, ckvn_ref, gq_ref, gkv_ref, wq_ref, wk_ref, wv_ref, e_ref,
             qc_ref, qs_ref, kc_ref, ks_ref, dcq_ref, dckv_ref, dkr_ref, dwq_ref, dwk_ref, dwv_ref, dgq_ref, dgkv_ref):
        @pl.when(pl.program_id(0) == 0)
        def _():
            for r in (dwq_ref, dwk_ref, dwv_ref, dgq_ref, dgkv_ref):
                r[...] = jnp.zeros_like(r)

        dqp = _rope_bwd(dq_ref[...], qc_ref[...], qs_ref[...], 16)
        dkd = dk_ref[...]
        dvd = dv_ref[...]
        dwq_ref[...] += _dot_tn(cqn_ref[...], dqp)
        dwk_ref[...] += _dot_tn(ckvn_ref[...], dkd)
        dwv_ref[...] += _dot_tn(ckvn_ref[...], dvd)
        dcq, dgq = _rms_bwd(cq_ref[...], gq_ref[...], _dot_nt(dqp, wq_ref[...]))
        dckv, dgkv = _rms_bwd(ckv_ref[...], gkv_ref[...], _dot_nt(dkd, wk_ref[...]) + _dot_nt(dvd, wv_ref[...]))
        dcq_ref[...] = dcq.astype(dcq_ref.dtype)
        dckv_ref[...] = dckv.astype(dckv_ref.dtype)
        dgq_ref[...] += dgq
        dgkv_ref[...] += dgkv
        dkr = _dot_exact(dkd, e_ref[...], (((1,), (1,)), ((), ())))
        dkr_ref[...] = _rope_bwd(dkr, kc_ref[...], ks_ref[...], 16)

    e = tb["place"]
    return pl.pallas_call(
        body, grid=(s // tm,),
        in_specs=[row(512, 0), row(512, 0), row(256, 0), row(256, _COL_CQ), row(128, _COL_CKV), row(256, 0), row(128, 0),
                  const(gq), const(gkv), const(wq), const(wk), const(wv), const(e), row(512, 0), row(512, 0), row(128, 0), row(128, 0)],
        out_specs=[row(256, 0), row(128, 0), row(128, 0), acc((256, 512)), acc((128, 512)), acc((128, 256)), acc((1, 256)),
                   acc((1, 128))],
        out_shape=[SDS((s, 256), _MXU), SDS((s, 128), _MXU), SDS((s, 128), F32), SDS((256, 512), F32), SDS((128, 512), F32),
                   SDS((128, 256), F32), SDS((1, 256), F32), SDS((1, 128), F32)],
        compiler_params=_cp("arbitrary"), name=name)(
            dq, dk, dv, z, z, cqn, ckvn, gq, gkv, wq, wk, wv, e, tb["q_cos"], tb["q_sin"], tb["k_cos"], tb["k_sin"])


def _out_proj(ys, g, w, x, name):
    s, d = x.shape
    tm = min(512, s)

    def body(ya, yb, yc, yd, g_ref, w_ref, x_ref, o_ref, yn_ref):
        acc = x_ref[...]
        for i, y_ref in enumerate((ya, yb, yc, yd)):
            sl = slice(GROUP * i, GROUP * (i + 1))
            yn = _rms(y_ref[...], g_ref[:, sl]).astype(_MXU)
            yn_ref[:, sl] = yn
            acc = acc + jnp.dot(yn, w_ref[sl, :], preferred_element_type=F32)
        o_ref[...] = acc

    yspec = pl.BlockSpec((tm, GROUP), lambda i: (i, 0))
    return pl.pallas_call(
        body, grid=(s // tm,),
        in_specs=[yspec, yspec, yspec, yspec, pl.BlockSpec((1, d), lambda i: (0, 0)), pl.BlockSpec((d, d), lambda i: (0, 0)),
                  pl.BlockSpec((tm, d), lambda i: (i, 0))],
        out_specs=[pl.BlockSpec((tm, d), lambda i: (i, 0)), pl.BlockSpec((tm, d), lambda i: (i, 0))],
        out_shape=[SDS((s, d), F32), SDS((s, d), _MXU)], compiler_params=_cp("parallel"), name=name)(*ys, g, w, x)


def _out_proj_bwd(dx, w, ys, g, name):
    s, d = dx.shape
    tm = min(512, s)

    def body(dx_ref, w_ref, ya, yb, yc, yd, g_ref, da, db, dc, dd, dg_ref):
        @pl.when(pl.program_id(0) == 0)
        def _():
            dg_ref[...] = jnp.zeros_like(dg_ref)

        dyn = _dot_nt(dx_ref[...], w_ref[...])
        outs = (da, db, dc, dd)
        for i, y_ref in enumerate((ya, yb, yc, yd)):
            sl = slice(GROUP * i, GROUP * (i + 1))
            dy, dg = _rms_bwd(y_ref[...], g_ref[:, sl], dyn[:, sl])
            outs[i][...] = dy
            dg_ref[:, sl] += dg

    yspec = pl.BlockSpec((tm, GROUP), lambda i: (i, 0))
    return pl.pallas_call(
        body, grid=(s // tm,),
        in_specs=[pl.BlockSpec((tm, d), lambda i: (i, 0)), pl.BlockSpec((d, d), lambda i: (0, 0)), yspec, yspec, yspec, yspec,
                  pl.BlockSpec((1, d), lambda i: (0, 0))],
        out_specs=[yspec, yspec, yspec, yspec, pl.BlockSpec((1, d), lambda i: (0, 0))],
        out_shape=[SDS((s, GROUP), F32)] * 4 + [SDS((1, d), F32)],
        compiler_params=_cp("arbitrary"), name=name)(dx, w, *ys, g)


FF_BLOCK = 512


def _ffn_fwd(x, g, wu, wd, name, comm=None):
    s, d = x.shape
    nj = wu.shape[0]
    tm = min(512, s)
    ni = s // tm

    def body(*refs):
        (x_ref, g_ref, wu_ref, wd_ref), (o_ref, u_ref, h_ref), (acc,), cc = _split_refs(refs, 4, 3, comm)
        i, j = pl.program_id(0), pl.program_id(1)
        if comm is not None:
            @pl.when((i == 0) & (j == 0))
            def _():
                comm.start(*cc)

        @pl.when(j == 0)
        def _():
            h_ref[...] = _rms(x_ref[...], g_ref[...]).astype(h_ref.dtype)
            acc[...] = jnp.zeros_like(acc)

        u = jnp.dot(h_ref[...], wu_ref[0], preferred_element_type=F32)
        u_ref[...] = u.astype(u_ref.dtype)
        acc[...] += _dot(jnp.square(jnp.maximum(u, 0.0)), wd_ref[...])

        @pl.when(j == nj - 1)
        def _():
            o_ref[...] = x_ref[...] + acc[...]

        if comm is not None:
            @pl.when((i == ni - 1) & (j == nj - 1))
            def _():
                comm.wait(*cc)

    in_specs = [pl.BlockSpec((tm, d), lambda i, j: (i, 0)), pl.BlockSpec((1, d), lambda i, j: (0, 0)),
                pl.BlockSpec((1, d, FF_BLOCK), lambda i, j: (j, 0, 0)), pl.BlockSpec((FF_BLOCK, d), lambda i, j: (j, 0))]
    out_specs = [pl.BlockSpec((tm, d), lambda i, j: (i, 0)), pl.BlockSpec((tm, FF_BLOCK), lambda i, j: (i, j)),
                 pl.BlockSpec((tm, d), lambda i, j: (i, 0))]
    out_shape = [SDS((s, d), F32), SDS((s, nj * FF_BLOCK), _MXU), SDS((s, d), _MXU)]
    return _call_with_comm(body, (ni, nj), in_specs, out_specs, out_shape, [pltpu.VMEM((tm, d), F32)], [x, g, wu, wd], comm,
                           ("arbitrary", "arbitrary"), name)


def _ffn_bwd(dx2, x, u, g, wu, wd, name, comm=None):
    s, d = x.shape
    nj = wu.shape[0]
    tm = min(512, s)
    ni = s // tm

    def body(*refs):
        (dx_ref, x_ref, u_ref, g_ref, wu_ref, wd_ref), (o_ref, du_ref, dg_ref), (acc, dxb), cc = _split_refs(refs, 6, 3, comm)
        i, j = pl.program_id(0), pl.program_id(1)

        @pl.when((i == 0) & (j == 0))
        def _():
            if comm is not None:
                comm.start(*cc)
            dg_ref[...] = jnp.zeros_like(dg_ref)

        @pl.when(j == 0)
        def _():
            dxb[...] = dx_ref[...].astype(dxb.dtype)
            acc[...] = jnp.zeros_like(acc)

        da = lax.dot_general(dxb[...], wd_ref[...], (((1,), (1,)), ((), ())), preferred_element_type=F32)
        du = (da * 2.0 * jnp.maximum(u_ref[...].astype(F32), 0.0)).astype(du_ref.dtype)
        du_ref[...] = du
        acc[...] += lax.dot_general(du, wu_ref[0], (((1,), (1,)), ((), ())), preferred_element_type=F32)

        @pl.when(j == nj - 1)
        def _():
            dxn, dg = _rms_bwd(x_ref[...], g_ref[...], acc[...])
            o_ref[...] = dx_ref[...] + dxn
            dg_ref[...] += dg

        if comm is not None:
            @pl.when((i == ni - 1) & (j == nj - 1))
            def _():
                comm.wait(*cc)

    in_specs = [pl.BlockSpec((tm, d), lambda i, j: (i, 0)), pl.BlockSpec((tm, d), lambda i, j: (i, 0)),
                pl.BlockSpec((tm, FF_BLOCK), lambda i, j: (i, j)), pl.BlockSpec((1, d), lambda i, j: (0, 0)),
                pl.BlockSpec((1, d, FF_BLOCK), lambda i, j: (j, 0, 0)), pl.BlockSpec((FF_BLOCK, d), lambda i, j: (j, 0))]
    out_specs = [pl.BlockSpec((tm, d), lambda i, j: (i, 0)), pl.BlockSpec((tm, FF_BLOCK), lambda i, j: (i, j)),
                 pl.BlockSpec((1, d), lambda i, j: (0, 0))]
    out_shape = [SDS((s, d), F32), SDS((s, nj * FF_BLOCK), _MXU), SDS((1, d), F32)]
    return _call_with_comm(body, (ni, nj), in_specs, out_specs, out_shape,
                           [pltpu.VMEM((tm, d), F32), pltpu.VMEM((tm, d), _MXU)], [dx2, x, u, g, wu, wd], comm,
                           ("arbitrary", "arbitrary"), name)


def _in_proj_bwd(dz, w, x, g, dx_up, name):
    s, d = x.shape
    n = w.shape[1]
    tm = min(512, s)

    def body(dz_ref, w_ref, x_ref, g_ref, up_ref, o_ref, dg_ref):
        @pl.when(pl.program_id(0) == 0)
        def _():
            dg_ref[...] = jnp.zeros_like(dg_ref)

        dh = lax.dot_general(dz_ref[...], w_ref[...], (((1,), (1,)), ((), ())), preferred_element_type=F32)
        dxn, dg = _rms_bwd(x_ref[...], g_ref[...], dh)
        o_ref[...] = up_ref[...] + dxn
        dg_ref[...] += dg

    return pl.pallas_call(
        body, grid=(s // tm,),
        in_specs=[pl.BlockSpec((tm, n), lambda i: (i, 0)), pl.BlockSpec((d, n), lambda i: (0, 0)),
                  pl.BlockSpec((tm, d), lambda i: (i, 0)), pl.BlockSpec((1, d), lambda i: (0, 0)),
                  pl.BlockSpec((tm, d), lambda i: (i, 0))],
        out_specs=[pl.BlockSpec((tm, d), lambda i: (i, 0)), pl.BlockSpec((1, d), lambda i: (0, 0))],
        out_shape=[SDS((s, d), F32), SDS((1, d), F32)], compiler_params=_cp("arbitrary"), name=name)(dz, w, x, g, dx_up)


def _loss_head(x, g, target, name):
    s, d = x.shape
    tm = min(512, s)

    def body(x_ref, g_ref, t_ref, l_ref, dx_ref, dg_ref):
        @pl.when(pl.program_id(0) == 0)
        def _():
            l_ref[...] = jnp.zeros_like(l_ref)
            dg_ref[...] = jnp.zeros_like(dg_ref)

        xv = x_ref[...]
        err = _rms(xv, g_ref[...]) - t_ref[...]
        l_ref[...] += jnp.sum(err * err, axis=0, keepdims=True) * (0.5 / d)
        dx, dg = _rms_bwd(xv, g_ref[...], err * (1.0 / d))
        dx_ref[...] = dx
        dg_ref[...] += dg

    return pl.pallas_call(
        body, grid=(s // tm,),
        in_specs=[pl.BlockSpec((tm, d), lambda i: (i, 0)), pl.BlockSpec((1, d), lambda i: (0, 0)),
                  pl.BlockSpec((tm, d), lambda i: (i, 0))],
        out_specs=[pl.BlockSpec((1, d), lambda i: (0, 0)), pl.BlockSpec((tm, d), lambda i: (i, 0)),
                   pl.BlockSpec((1, d), lambda i: (0, 0))],
        out_shape=[SDS((1, d), F32), SDS((s, d), F32), SDS((1, d), F32)], compiler_params=_cp("arbitrary"), name=name)(x, g, target)


def _me_and_peer():
    x, y, c = lax.axis_index("x"), lax.axis_index("y"), lax.axis_index("c")
    me = 4 * x + 2 * y + c

    def peer(k):
        px, py, pc = x ^ (k >> 2), y ^ ((k >> 1) & 1), c ^ (k & 1)
        return (px, py, pc), 4 * px + 2 * py + pc

    return me, peer


class _Comm:
    def __init__(self, kind, arrs):
        assert kind in ("gather", "exchange")
        self.kind, self.arrs, self.n = kind, list(arrs), len(arrs)
        anyspec = pl.BlockSpec(memory_space=pl.ANY)
        self.in_specs = [anyspec] * self.n
        self.out_specs = [anyspec] * self.n
        self.out_shape = [SDS(((NDEV,) + a.shape) if kind == "gather" else a.shape, a.dtype) for a in self.arrs]
        self.scratch = [pltpu.SemaphoreType.DMA((self.n, NDEV - 1)), pltpu.SemaphoreType.DMA((self.n, NDEV - 1)),
                        pltpu.SemaphoreType.DMA((self.n,))]

    def _copies(self, ins, outs, sems):
        send, recv, loc = sems
        me, peer = _me_and_peer()
        gather = self.kind == "gather"
        local = [pltpu.make_async_copy(ins[a] if gather else ins[a].at[me], outs[a].at[me], loc.at[a]) for a in range(self.n)]
        outgoing, incoming = [], []
        for k in range(1, NDEV):
            dev, pid = peer(k)
            for a in range(self.n):
                pair = dict(send_sem=send.at[a, k - 1], recv_sem=recv.at[a, k - 1], device_id=dev, device_id_type=MESH)
                outgoing.append(pltpu.make_async_remote_copy(src_ref=ins[a] if gather else ins[a].at[pid],
                                                             dst_ref=outs[a].at[me], **pair))
                incoming.append(pltpu.make_async_remote_copy(src_ref=ins[a] if gather else ins[a].at[me],
                                                             dst_ref=outs[a].at[pid], **pair))
        return local, outgoing, incoming

    def start(self, ins, outs, sems):
        local, outgoing, _ = self._copies(ins, outs, sems)
        for cp in local + outgoing:
            cp.start()

    def wait(self, ins, outs, sems):
        local, outgoing, incoming = self._copies(ins, outs, sems)
        for cp in incoming:
            cp.wait_recv()
        for cp in outgoing:
            cp.wait_send()
        for cp in local:
            cp.wait()


def _split_refs(refs, n_in, n_out, comm):
    c = comm.n if comm is not None else 0
    ins, cin = refs[:n_in], refs[n_in:n_in + c]
    outs, cout = refs[n_in + c:n_in + c + n_out], refs[n_in + c + n_out:n_in + 2 * c + n_out]
    rest = refs[n_in + 2 * c + n_out:]
    scratch, csem = (rest[:len(rest) - 3], rest[len(rest) - 3:]) if c else (rest, ())
    return ins, outs, scratch, (cin, cout, csem)


def _comm_call(kind, arrs, name):
    comm = _Comm(kind, arrs)

    def body(*refs):
        _, _, _, c = _split_refs(refs, 0, 0, comm)
        comm.start(*c)
        comm.wait(*c)

    return pl.pallas_call(body, in_specs=comm.in_specs, out_specs=comm.out_specs, out_shape=comm.out_shape,
                          scratch_shapes=comm.scratch, compiler_params=pltpu.CompilerParams(has_side_effects=True),
                          name=name)(*arrs)


def _all_gather(arrs, name):
    return _comm_call("gather", arrs, name)


def _exchange(arrs, name):
    return _comm_call("exchange", arrs, name)


def _sum_slots(parts, name):
    _, r, c = parts.shape
    tr = r if r <= 512 else 512

    def body(p_ref, o_ref):
        acc = p_ref[0].astype(F32)
        for q in range(1, NDEV):
            acc = acc + p_ref[q].astype(F32)
        o_ref[...] = acc

    return pl.pallas_call(
        body, grid=(r // tr,), in_specs=[pl.BlockSpec((NDEV, tr, c), lambda i: (0, i, 0))],
        out_specs=pl.BlockSpec((tr, c), lambda i: (i, 0)), out_shape=SDS((r, c), F32),
        compiler_params=_cp("parallel"), name=name)(parts)


def _adamw(g, w, m, v, name):
    r, c = w.shape
    parts = g.ndim == 3
    tr = r
    for cand in (512, 256, 128, 64, 32, 16, 8):
        if r > cand and r % cand == 0 and cand * c * 4 <= 2 * 1024 * 1024:
            tr = cand
            break
    bc1 = 1.0 / (1.0 - ADAM_B1 ** ADAM_STEP)
    bc2 = 1.0 / (1.0 - ADAM_B2 ** ADAM_STEP)

    def body(g_ref, w_ref, m_ref, v_ref, go_ref, d_ref, mo_ref, vo_ref):
        if parts:
            gv = g_ref[0].astype(F32)
            for q in range(1, NDEV):
                gv = gv + g_ref[q].astype(F32)
        else:
            gv = g_ref[...]
        mn = ADAM_B1 * m_ref[...] + (1.0 - ADAM_B1) * gv
        vn = ADAM_B2 * v_ref[...] + (1.0 - ADAM_B2) * (gv * gv)
        go_ref[...] = gv
        mo_ref[...] = mn
        vo_ref[...] = vn
        d_ref[...] = -ADAM_LR * ((mn * bc1) / (jnp.sqrt(vn * bc2) + ADAM_EPS) + ADAM_WD * w_ref[...])

    spec = pl.BlockSpec((tr, c), lambda i: (i, 0))
    gspec = pl.BlockSpec((NDEV, tr, c), lambda i: (0, i, 0)) if parts else spec
    return pl.pallas_call(
        body, grid=(r // tr,), in_specs=[gspec, spec, spec, spec], out_specs=[spec] * 4,
        out_shape=[SDS((r, c), F32)] * 4, compiler_params=_cp("parallel"), name=name)(g, w, m, v)


def _pad_in_cols(w):
    r = w.shape[0]
    zeros = lambda n: jnp.zeros((r, n), w.dtype)
    return jnp.concatenate([w[:, :2304], w[:, 2308:2692], w[:, 2304:2308], zeros(28), w[:, 2692:2724], zeros(64)], axis=1)


def _unpad_in_cols(w):
    return jnp.concatenate([w[..., :2304], w[..., 2688:2692], w[..., 2304:2688], w[..., 2720:2752]], axis=-1)


def _pad_uq(w):
    return jnp.pad(w.reshape(256, N_HEADS, 96), ((0, 0), (0, 0), (0, 32))).reshape(256, 512)


def _unpad_uq(w):
    return w.reshape(256, N_HEADS, 128)[:, :, :96].reshape(256, 384)


def _split_ukv(w):
    r = w.reshape(128, N_HEADS, 128)
    return jnp.pad(r[:, :, :64], ((0, 0), (0, 0), (0, 64))).reshape(128, 512), r[:, :, 64:].reshape(128, 256)


def _join_ukv(dk, dv):
    return jnp.concatenate([dk.reshape(128, N_HEADS, 128)[:, :, :64], dv.reshape(128, N_HEADS, 64)], axis=-1).reshape(128, 512)


def _cols_to_full(g):
    return jnp.transpose(g, (1, 0, 2)).reshape(g.shape[1], NDEV * g.shape[2])


def kernel(x, g_mix_norm, w_in, b_forget, g_sgu, w_spatial, b_spatial, g_mla_q, w_uq, g_mla_kv, w_ukv, g_group_out, w_out, g_ffn_norm, w_up, w_down, g_final, loss_target, m_g_mix_norm, m_w_in, m_b_forget, m_g_sgu, m_w_spatial, m_b_spatial, m_g_mla_q, m_w_uq, m_g_mla_kv, m_w_ukv, m_g_group_out, m_w_out, m_g_ffn_norm, m_w_up, m_w_down, m_g_final, v_g_mix_norm, v_w_in, v_b_forget, v_g_sgu, v_w_spatial, v_b_spatial, v_g_mla_q, v_w_uq, v_g_mla_kv, v_w_ukv, v_g_group_out, v_w_out, v_g_ffn_norm, v_w_up, v_w_down, v_g_final):
    depth = w_in.shape[0]
    s, d = x.shape[1], x.shape[2]
    x0 = x.reshape(s, d)
    target = loss_target.reshape(s, d)
    tb = _tables(s)
    me = 4 * lax.axis_index("x") + 2 * lax.axis_index("y") + lax.axis_index("c")

    assert depth == 2
    shards = {}
    for l in range(depth):
        shards.update({(l, "w_in"): _pad_in_cols(w_in[l]).astype(_WIRE), (l, "w_out"): w_out[l].astype(_WIRE),
                       (l, "w_up"): w_up[l].astype(_WIRE), (l, "w_down"): w_down[l].astype(_WIRE),
                       (l, "w_uq"): w_uq[l].astype(_WIRE), (l, "w_ukv"): w_ukv[l].astype(_WIRE)})
    wts = _ShardedWeights(shards)
    wts.full[(0, "w_in")] = _all_gather([shards[(0, "w_in")]], "gather_w_in0")[0]

    row = lambda a: a.reshape(1, -1)

    def small(l):
        bf = jnp.pad(b_forget[l].reshape(1, N_HEADS), ((0, 0), (0, 128 - N_HEADS)))
        bt = jnp.pad(b_spatial[l].T, ((0, 0), (0, 128 - N_HEADS)))
        return dict(g_mix=row(g_mix_norm[l]), g_sgu=row(g_sgu[l]), w_s=w_spatial[l], b_t=bt, b_f=bf, gq=row(g_mla_q[l]),
                    gkv=row(g_mla_kv[l]), g_go=row(g_group_out[l]), g_ffn=row(g_ffn_norm[l]))

    smalls = [small(l) for l in range(depth)]
    lrow, dx, sm, dg_final = _local_step(x0, target, wts, smalls, row(g_final), tb)
    loss = lax.psum(jnp.sum(lrow), AXES)
    grad_x = dx.reshape(1, s, d)
    wts.recv[(0, "w_in")] = _exchange([wts.grads[(0, "w_in")]], "scatter_dw_in0")[0]
    return _reduce_and_update(loss, grad_x, wts.recv, sm, dg_final, me, dict(
        g_mix_norm=(g_mix_norm, m_g_mix_norm, v_g_mix_norm), w_in=(w_in, m_w_in, v_w_in),
        b_forget=(b_forget, m_b_forget, v_b_forget), g_sgu=(g_sgu, m_g_sgu, v_g_sgu),
        w_spatial=(w_spatial, m_w_spatial, v_w_spatial), b_spatial=(b_spatial, m_b_spatial, v_b_spatial),
        g_mla_q=(g_mla_q, m_g_mla_q, v_g_mla_q), w_uq=(w_uq, m_w_uq, v_w_uq), g_mla_kv=(g_mla_kv, m_g_mla_kv, v_g_mla_kv),
        w_ukv=(w_ukv, m_w_ukv, v_w_ukv), g_group_out=(g_group_out, m_g_group_out, v_g_group_out),
        w_out=(w_out, m_w_out, v_w_out), g_ffn_norm=(g_ffn_norm, m_g_ffn_norm, v_g_ffn_norm), w_up=(w_up, m_w_up, v_w_up),
        w_down=(w_down, m_w_down, v_w_down), g_final=(g_final, m_g_final, v_g_final)))


_GATHER_AT = {
    "fox_attn0": [(0, "w_uq"), (0, "w_ukv"), (0, "w_out"), (0, "w_up")],
    "mla_attn0": [(0, "w_down"), (1, "w_in")],
    "ffn_fwd0": [(1, "w_uq"), (1, "w_ukv"), (1, "w_out"), (1, "w_up")],
    "fox_attn1": [(1, "w_down")],
}
_SCATTER_AT = {
    "fox_attn_bwd1": [(1, "w_down"), (1, "w_up")],
    "mla_attn_bwd1": [(1, "w_out")],
    "ffn_bwd0": [(1, "w_in")],
    "fox_attn_bwd0": [(0, "w_down"), (0, "w_up")],
    "mla_attn_bwd0": [(0, "w_out")],
}


class _FullWeights:
    def __init__(self, per_layer):
        self.per_layer, self.grads = per_layer, {}

    def get(self, l, name):
        return self.per_layer[l][name]

    def comm(self, host):
        return None

    def done(self, host, results):
        pass

    def grad(self, l, name, blocks):
        self.grads[(l, name)] = blocks


class _ShardedWeights(_FullWeights):
    def __init__(self, shards):
        self.shards, self.full, self.grads, self.recv = shards, {}, {}, {}

    def get(self, l, name):
        if name in ("wk", "wv"):
            return _split_ukv(_cols_to_full(self.full[(l, "w_ukv")]))[0 if name == "wk" else 1]
        if name == "wq":
            return _pad_uq(_cols_to_full(self.full[(l, "w_uq")]))
        g = self.full[(l, name)]
        return g if name == "w_up" else g.reshape(NDEV * g.shape[1], g.shape[2])

    def comm(self, host):
        if host in _GATHER_AT:
            return _Comm("gather", [self.shards[k] for k in _GATHER_AT[host]])
        if host in _SCATTER_AT:
            return _Comm("exchange", [self.grads[k] for k in _SCATTER_AT[host]])
        return None

    def done(self, host, results):
        if host in _GATHER_AT:
            self.full.update(zip(_GATHER_AT[host], results))
        if host in _SCATTER_AT:
            self.recv.update(zip(_SCATTER_AT[host], results))


def _local_step(x0, target, wts, smalls, g_final, tb):
    depth = len(smalls)
    s, d = x0.shape
    saved = []
    xl = x0
    for l in range(depth):
        p = smalls[l]
        z, h = _norm_matmul(xl, p["g_mix"], wts.get(l, "w_in"), f"in_proj{l}")
        ya = _sgu_fwd(z, p["g_sgu"], p["w_s"], p["b_t"], f"sgu_fwd{l}")
        yb, ret, states = _ret_fwd(z, tb, f"ret_fwd{l}")
        cum = _fox_prep(z, p["b_f"], f"fox_prep{l}")
        kc, vc, vtc = _kv_prep(z, 7, 8, f"fox_kv{l}")
        yc, lse_c, got = _attn_fwd(z, 6, HEAD_DIM, kc, vtc, HEAD_DIM ** -0.5, cum, f"fox_attn{l}", wts.comm(f"fox_attn{l}"))
        wts.done(f"fox_attn{l}", got)
        wq, wk, wv = wts.get(l, "wq"), wts.get(l, "wk"), wts.get(l, "wv")
        qd, kd, vd, vtd, cqn, ckvn = _mla_prep(z, p["gq"], p["gkv"], wq, wk, wv, tb, f"mla_prep{l}")
        yd, lse_d, got = _attn_fwd(qd, 0, 128, kd, vtd, _SCALE_D, None, f"mla_attn{l}", wts.comm(f"mla_attn{l}"))
        wts.done(f"mla_attn{l}", got)
        ys = (ya, yb, yc, yd)
        x1, yn = _out_proj(ys, p["g_go"], wts.get(l, "w_out"), xl, f"out_proj{l}")
        x2, u, h2, got = _ffn_fwd(x1, p["g_ffn"], wts.get(l, "w_up"), wts.get(l, "w_down"), f"ffn_fwd{l}", wts.comm(f"ffn_fwd{l}"))
        wts.done(f"ffn_fwd{l}", got)
        saved.append(dict(x=xl, z=z, h=h, ys=ys, ret=ret, states=states, cum=cum, lse_c=lse_c, kc=kc, vc=vc, qd=qd, kd=kd, vd=vd,
                          cqn=cqn, ckvn=ckvn, lse_d=lse_d, x1=x1, yn=yn, u=u, h2=h2, wq=wq, wk=wk, wv=wv))
        xl = x2

    lrow, dx, dg_final = _loss_head(xl, g_final, target, "loss_head")

    sm = [None] * depth
    for l in reversed(range(depth)):
        p, a = smalls[l], saved[l]
        dx1, du, dg_ffn, got = _ffn_bwd(dx, a["x1"], a["u"], p["g_ffn"], wts.get(l, "w_up"), wts.get(l, "w_down"), f"ffn_bwd{l}",
                                        wts.comm(f"ffn_bwd{l}"))
        wts.done(f"ffn_bwd{l}", got)
        dw_down = _mm_tn(a["u"], dx, f"dw_down{l}", a_fn=lambda t: jnp.square(jnp.maximum(t, 0.0)), out_dtype=_WIRE)
        wts.grad(l, "w_down", dw_down.reshape(NDEV, dw_down.shape[0] // NDEV, d))
        wts.grad(l, "w_up", _mm_tn(a["h2"], du, f"dw_up{l}", blocked=True, out_dtype=_WIRE))
        dya, dyb, dyc, dyd, dg_go = _out_proj_bwd(dx1, wts.get(l, "w_out"), a["ys"], p["g_go"], f"out_proj_bwd{l}")
        wts.grad(l, "w_out", _mm_tn(a["yn"], dx1, f"dw_out{l}", out_dtype=_WIRE).reshape(NDEV, d // NDEV, d))
        dz_a, dg_sgu, dw_s, db_t = _sgu_bwd(dya, a["z"], p["g_sgu"], p["w_s"], p["b_t"], f"sgu_bwd{l}")
        dz_b = _ret_bwd(dyb, a["z"], a["ret"], a["states"], tb, f"ret_bwd{l}")
        qb, qt, dob, dot, dl = _attn_bwd_prep(a["z"], 6, HEAD_DIM, a["ys"][2], dyc, f"fox_bwd_prep{l}")
        dqt_c, dk_c, dv_c, dck, dcq, got = _attn_bwd(a["kc"], a["vc"], qb, qt, dob, dot, a["lse_c"], dl, HEAD_DIM,
                                                     HEAD_DIM ** -0.5, a["cum"], f"fox_attn_bwd{l}", _MXU,
                                                     wts.comm(f"fox_attn_bwd{l}"))
        wts.done(f"fox_attn_bwd{l}", got)
        dq_c = _untranspose(dqt_c, _MXU, f"fox_dq{l}")
        qb, qt, dob, dot, dl = _attn_bwd_prep(a["qd"], 0, 128, a["ys"][3], dyd, f"mla_bwd_prep{l}")
        dqt_d, dk_d, dv_d, got = _attn_bwd(a["kd"], a["vd"], qb, qt, dob, dot, a["lse_d"], dl, 128, _SCALE_D, None,
                                           f"mla_attn_bwd{l}", F32, wts.comm(f"mla_attn_bwd{l}"))
        wts.done(f"mla_attn_bwd{l}", got)
        dq_d = _untranspose(dqt_d, F32, f"mla_dq{l}")
        dz_cq, dz_ckv, dkr, dwq, dwk, dwv, dgq, dgkv = _mla_prep_bwd(dq_d, dk_d, dv_d, a["z"], a["cqn"], a["ckvn"], p["gq"],
                                                                     p["gkv"], a["wq"], a["wk"], a["wv"], tb, f"mla_prep_bwd{l}")
        dz_misc, db_f = _fox_post(dcq, dck, a["z"], p["b_f"], dkr, f"fox_post{l}")
        dz = jnp.concatenate([dz_a, dz_b, dq_c, dk_c, dv_c, dz_cq, dz_ckv, dz_misc], axis=1)
        dx, dg_mix = _in_proj_bwd(dz, wts.get(l, "w_in"), a["x"], p["g_mix"], dx1, f"in_proj_bwd{l}")
        wts.grad(l, "w_in", _unpad_in_cols(_mm_tn(a["h"], dz, f"dw_in{l}", out_dtype=_WIRE)).reshape(NDEV, d // NDEV, N_IN))
        sm[l] = [dg_mix, dg_go, dg_ffn, dg_sgu, dw_s, db_t[:, :N_HEADS].T, db_f[0, :N_HEADS], dgq, dgkv, _unpad_uq(dwq),
                 _join_ukv(dwk, dwv)]
    return lrow, dx, sm, dg_final


def _reduce_and_update(loss, grad_x, recv, sm, dg_final, me, given):
    depth = len(sm)
    pieces = [t for l in range(depth) for t in sm[l]] + [dg_final]
    flat = jnp.concatenate([t.reshape(-1) for t in pieces])
    n_flat = flat.shape[0]
    unit = NDEV * 8 * 128
    n_pad = -(-n_flat // unit) * unit
    packed = jnp.pad(flat, (0, n_pad - n_flat)).reshape(NDEV, n_pad // (NDEV * 128), 128)
    red = _sum_slots(_exchange([packed], "scatter_small")[0], "sum_small")
    full = _all_gather([red], "gather_small")[0].reshape(-1)
    offs = np.cumsum([0] + [int(np.prod(t.shape)) for t in pieces])
    red_pieces = [full[int(offs[i]):int(offs[i + 1])].reshape(pieces[i].shape) for i in range(len(pieces))]
    per = len(sm[0])
    stack = lambda i: jnp.stack([red_pieces[l * per + i] for l in range(depth)])
    g_small = dict(g_mix_norm=stack(0), g_group_out=stack(1), g_ffn_norm=stack(2), g_sgu=stack(3), w_spatial=stack(4),
                   b_spatial=stack(5), b_forget=stack(6), g_mla_q=stack(7), g_mla_kv=stack(8), g_final=red_pieces[-1])
    cq, ckv = given["w_uq"][0].shape[2], given["w_ukv"][0].shape[2]
    g_small["w_uq"] = lax.dynamic_slice_in_dim(stack(9), me * cq, cq, axis=2)
    g_small["w_ukv"] = lax.dynamic_slice_in_dim(stack(10), me * ckv, ckv, axis=2)

    names = list(given)
    outs = {}
    for nme in names:
        wv_, mv_, vv_ = given[nme]
        shape = wv_.shape
        if nme in ("w_in", "w_out", "w_up", "w_down"):
            res = []
            for l in range(depth):
                parts = recv[(l, nme)]
                two = lambda t: t[l].reshape(-1, shape[-1])
                res.append(_adamw(parts, two(wv_), two(mv_), two(vv_), f"adamw_{nme}{l}"))
            outs[nme] = [jnp.stack([res[l][i] for l in range(depth)]).reshape(shape) for i in range(4)]
        else:
            two = lambda t: t.reshape(-1, shape[-1]) if t.ndim > 1 else t.reshape(1, -1)
            res = _adamw(two(g_small[nme]), two(wv_), two(mv_), two(vv_), f"adamw_{nme}")
            outs[nme] = [r.reshape(shape) for r in res]
    return (loss, grad_x, *[outs[n][0] for n in names], *[outs[n][1] for n in names], *[outs[n][2] for n in names],
            *[outs[n][3] for n in names])
```

```python
import functools

import jax
import jax.numpy as jnp
import numpy as np
from jax import lax
from jax.experimental import pallas as pl
from jax.experimental.pallas import tpu as pltpu

F32 = jnp.float32
_MXU = jnp.bfloat16
_WIRE = jnp.bfloat16
EPS = 1e-6
NDEV = 8
AXES = ("x", "y", "c")
MESH = pl.DeviceIdType.MESH

N_HEADS = 4
HEAD_DIM = 64
GROUP = 256
CHUNK = 128
NZ = 2816
N_IN = 2724
MISC_F, MISC_KR = 0, 32
VMEM_LIMIT = 56 * 1024 * 1024

ADAM_LR, ADAM_B1, ADAM_B2, ADAM_EPS, ADAM_WD, ADAM_STEP = 0.001, 0.9, 0.999, 1e-08, 0.01, 10

SDS = jax.ShapeDtypeStruct


def _cp(*sem):
    return pltpu.CompilerParams(dimension_semantics=sem, vmem_limit_bytes=VMEM_LIMIT)


def _dot(a, b):
    return jnp.dot(a.astype(_MXU), b.astype(_MXU), preferred_element_type=F32)


def _dot_nt(a, b):
    return lax.dot_general(a.astype(_MXU), b.astype(_MXU), (((1,), (1,)), ((), ())), preferred_element_type=F32)


def _dot_tn(a, b):
    return lax.dot_general(a.astype(_MXU), b.astype(_MXU), (((0,), (0,)), ((), ())), preferred_element_type=F32)


def _dot_exact(a, b, dims=(((1,), (0,)), ((), ()))):
    return lax.dot_general(a, b, dims, precision=lax.Precision.HIGHEST, preferred_element_type=F32)


def _rms(x, g):
    return x * lax.rsqrt(jnp.mean(x * x, axis=-1, keepdims=True) + EPS) * g


def _rms_bwd(x, g, dy):
    xh = x * lax.rsqrt(jnp.mean(x * x, axis=-1, keepdims=True) + EPS)
    dxh = dy * g
    r = lax.rsqrt(jnp.mean(x * x, axis=-1, keepdims=True) + EPS)
    dx = r * (dxh - xh * jnp.mean(dxh * xh, axis=-1, keepdims=True))
    return dx, jnp.sum(dy * xh, axis=0, keepdims=True)


def _standardize(t):
    mu = jnp.mean(t, axis=-1, keepdims=True)
    tc = t - mu
    rs = lax.rsqrt(jnp.mean(tc * tc, axis=-1, keepdims=True) + EPS)
    return tc * rs, rs


def _standardize_bwd(yh, rs, dy):
    return rs * (dy - jnp.mean(dy, axis=-1, keepdims=True) - yh * jnp.mean(dy * yh, axis=-1, keepdims=True))


_GELU_C = 0.7978845608028654


def _gelu(x):
    return 0.5 * x * (1.0 + jnp.tanh(_GELU_C * (x + 0.044715 * x * x * x)))


def _gelu_grad(x):
    t = jnp.tanh(_GELU_C * (x + 0.044715 * x * x * x))
    return 0.5 * (1.0 + t) + 0.5 * x * (1.0 - t * t) * _GELU_C * (1.0 + 3 * 0.044715 * x * x)


def _sigmoid(x):
    return 1.0 / (1.0 + jnp.exp(-x))


def _swap_half(t, half):
    n = t.shape[-1]
    lane = lax.broadcasted_iota(jnp.int32, t.shape, t.ndim - 1)
    return jnp.where((lane % (2 * half)) < half, pltpu.roll(t, n - half, t.ndim - 1), pltpu.roll(t, half, t.ndim - 1))


def _rope(t, cos, sin, half):
    return t * cos + _swap_half(t, half) * sin


def _rope_bwd(d, cos, sin, half):
    return d * cos - _swap_half(d, half) * sin


def _tables(s):
    pos = jnp.arange(s, dtype=F32)[:, None]

    def cs(half):
        inv = jnp.power(10000.0, -jnp.arange(half, dtype=F32) / half)
        ang = pos * inv[None, :]
        return jnp.cos(ang), jnp.sin(ang)

    c32, s32 = cs(32)
    c16, s16 = cs(16)
    z = lambda w: jnp.zeros((s, w), F32)
    o = lambda w: jnp.ones((s, w), F32)
    t = {}
    t["b_cos"] = jnp.tile(jnp.concatenate([c32, c32], 1), (1, 4))
    t["b_sin"] = jnp.tile(jnp.concatenate([-s32, s32], 1), (1, 4))
    t["q_cos"] = jnp.tile(jnp.concatenate([o(64), c16, c16, z(32)], 1), (1, 4))
    t["q_sin"] = jnp.tile(jnp.concatenate([z(64), -s16, s16, z(32)], 1), (1, 4))
    t["k_cos"] = jnp.concatenate([z(32), c16, c16, z(64)], 1)
    t["k_sin"] = jnp.concatenate([z(32), -s16, s16, z(64)], 1)
    lg = jnp.log1p(-jnp.exp2(-5.0 - jnp.arange(N_HEADS, dtype=F32)))
    j = jnp.arange(CHUNK, dtype=F32)
    rel = j[:, None] - j[None, :]
    t["decay"] = jnp.where(rel[None] >= 0, jnp.exp(jnp.maximum(rel, 0.0)[None] * lg[:, None, None]), 0.0)

    def rows(e):
        return jnp.repeat(e.T, HEAD_DIM, axis=1)

    t["qw"] = rows(jnp.exp((j + 1.0)[None, :] * lg[:, None]))
    t["kw"] = rows(jnp.exp((CHUNK - 1 - j)[None, :] * lg[:, None]))
    t["kw2"] = rows(jnp.exp((CHUNK - j)[None, :] * lg[:, None]))
    t["qw0"] = rows(jnp.exp(j[None, :] * lg[:, None]))
    t["cd"] = jnp.repeat(jnp.exp(CHUNK * lg), HEAD_DIM)[None, :]
    e = np.zeros((128, 512), np.float32)
    for h in range(N_HEADS):
        for r in range(32):
            e[MISC_KR + r, 128 * h + 64 + r] = 1.0
    t["place"] = jnp.asarray(e)
    return t


def _norm_matmul(x, g, w, name):
    s, d = x.shape
    n = w.shape[1]
    tm, tn = min(512, s), 256

    def body(x_ref, g_ref, w_ref, z_ref, h_ref):
        @pl.when(pl.program_id(1) == 0)
        def _():
            h_ref[...] = _rms(x_ref[...], g_ref[...]).astype(h_ref.dtype)

        z_ref[...] = jnp.dot(h_ref[...], w_ref[...], preferred_element_type=F32)

    return pl.pallas_call(
        body, grid=(s // tm, n // tn),
        in_specs=[pl.BlockSpec((tm, d), lambda i, j: (i, 0)), pl.BlockSpec((1, d), lambda i, j: (0, 0)),
                  pl.BlockSpec((d, tn), lambda i, j: (0, j))],
        out_specs=[pl.BlockSpec((tm, tn), lambda i, j: (i, j)), pl.BlockSpec((tm, d), lambda i, j: (i, 0))],
        out_shape=[SDS((s, n), F32), SDS((s, d), _MXU)],
        compiler_params=_cp("parallel", "arbitrary"), name=name)(x, g, w)


def _mm_tn(a, b, name, *, a_fn=None, blocked=False, out_dtype=F32):
    k, m = a.shape
    n = b.shape[1]
    tm, tk = min(512, m), min(1024, k)
    tn = next(t for t in (512, 256, 128) if n % t == 0)
    assert m % tm == 0 and k % tk == 0
    nk = k // tk

    def body(a_ref, b_ref, o_ref, acc):
        kk = pl.program_id(2)

        @pl.when(kk == 0)
        def _():
            acc[...] = jnp.zeros_like(acc)

        av = a_ref[...]
        if a_fn is not None:
            av = a_fn(av.astype(F32))
        acc[...] += _dot_tn(av, b_ref[...])

        @pl.when(kk == nk - 1)
        def _():
            o_ref[...] = acc[...].reshape(o_ref.shape).astype(o_ref.dtype)

    if blocked:
        assert tn == 512
        out_spec = pl.BlockSpec((1, tm, tn), lambda i, j, kk: (j, i, 0))
        out_shape = SDS((n // tn, m, tn), out_dtype)
    else:
        out_spec = pl.BlockSpec((tm, tn), lambda i, j, kk: (i, j))
        out_shape = SDS((m, n), out_dtype)
    return pl.pallas_call(
        body, grid=(m // tm, n // tn, nk),
        in_specs=[pl.BlockSpec((tk, tm), lambda i, j, kk: (kk, i)), pl.BlockSpec((tk, tn), lambda i, j, kk: (kk, j))],
        out_specs=out_spec, out_shape=out_shape, scratch_shapes=[pltpu.VMEM((tm, tn), F32)],
        compiler_params=_cp("parallel", "parallel", "arbitrary"), name=name)(a, b)


def _sgu_parts(u_pre, v_pre, gain):
    u = _gelu(u_pre)
    v = _gelu(v_pre)
    vh, rs, vg = [], [], []
    for h in range(N_HEADS):
        sl = slice(HEAD_DIM * h, HEAD_DIM * (h + 1))
        a, r = _standardize(v[:, sl])
        vh.append(a)
        rs.append(r)
        vg.append(a * gain[:, sl])
    return u, vh, rs, vg


def _tril(w):
    r = lax.broadcasted_iota(jnp.int32, w.shape, 0)
    c = lax.broadcasted_iota(jnp.int32, w.shape, 1)
    return jnp.where(r >= c, w, 0.0)


def _sgu_fwd(z, gain, w_s, b_t, name):
    s = z.shape[0]
    tm = min(512, s)

    def body(u_ref, v_ref, g_ref, w_ref, b_ref, y_ref):
        u, _, _, vg = _sgu_parts(u_ref[...], v_ref[...], g_ref[...])
        for h in range(N_HEADS):
            sl = slice(HEAD_DIM * h, HEAD_DIM * (h + 1))
            wc = _tril(w_ref[h])
            for c in range(tm // CHUNK):
                r = slice(CHUNK * c, CHUNK * (c + 1))
                mixed = _dot(wc, vg[h][r]) + b_ref[:, h:h + 1]
                y_ref[r, sl] = u[r, sl] * mixed

    return pl.pallas_call(
        body, grid=(s // tm,),
        in_specs=[pl.BlockSpec((tm, GROUP), lambda i: (i, 0)), pl.BlockSpec((tm, GROUP), lambda i: (i, 1)),
                  pl.BlockSpec((1, GROUP), lambda i: (0, 0)), pl.BlockSpec((N_HEADS, CHUNK, CHUNK), lambda i: (0, 0, 0)),
                  pl.BlockSpec((CHUNK, 128), lambda i: (0, 0))],
        out_specs=pl.BlockSpec((tm, GROUP), lambda i: (i, 0)), out_shape=SDS((s, GROUP), F32),
        compiler_params=_cp("parallel"), name=name)(z, z, gain, w_s, b_t)


def _sgu_bwd(dy, z, gain, w_s, b_t, name):
    s = z.shape[0]
    tm = min(512, s)

    def body(dy_ref, u_ref, v_ref, g_ref, w_ref, b_ref, dz_ref, dg_ref, dw_ref, db_ref):
        @pl.when(pl.program_id(0) == 0)
        def _():
            dg_ref[...] = jnp.zeros_like(dg_ref)
            dw_ref[...] = jnp.zeros_like(dw_ref)
            db_ref[...] = jnp.zeros_like(db_ref)

        u_pre, v_pre, gain_v = u_ref[...], v_ref[...], g_ref[...]
        u, vh, rs, vg = _sgu_parts(u_pre, v_pre, gain_v)
        dyv = dy_ref[...]
        gu = _gelu_grad(u_pre)
        gv = _gelu_grad(v_pre)
        for h in range(N_HEADS):
            sl = slice(HEAD_DIM * h, HEAD_DIM * (h + 1))
            wc = _tril(w_ref[h])
            dwh = jnp.zeros((CHUNK, CHUNK), F32)
            dbh = jnp.zeros((CHUNK, 1), F32)
            dgh = jnp.zeros((1, HEAD_DIM), F32)
            for c in range(tm // CHUNK):
                r = slice(CHUNK * c, CHUNK * (c + 1))
                mixed = _dot(wc, vg[h][r]) + b_ref[:, h:h + 1]
                dz_ref[r, sl] = (dyv[r, sl] * mixed * gu[r, sl]).astype(dz_ref.dtype)
                dm = dyv[r, sl] * u[r, sl]
                dwh += _dot_nt(dm, vg[h][r])
                dbh += jnp.sum(dm, axis=1, keepdims=True)
                dvg = _dot_tn(wc, dm)
                dgh += jnp.sum(dvg * vh[h][r], axis=0, keepdims=True)
                dv = _standardize_bwd(vh[h][r], rs[h][r], dvg * gain_v[:, sl])
                dz_ref[r, GROUP + HEAD_DIM * h:GROUP + HEAD_DIM * (h + 1)] = (dv * gv[r, sl]).astype(dz_ref.dtype)
            dw_ref[h] += _tril(dwh)
            db_ref[:, h:h + 1] += dbh
            dg_ref[:, sl] += dgh

    return pl.pallas_call(
        body, grid=(s // tm,),
        in_specs=[pl.BlockSpec((tm, GROUP), lambda i: (i, 0)),
                  pl.BlockSpec((tm, GROUP), lambda i: (i, 0)), pl.BlockSpec((tm, GROUP), lambda i: (i, 1)),
                  pl.BlockSpec((1, GROUP), lambda i: (0, 0)), pl.BlockSpec((N_HEADS, CHUNK, CHUNK), lambda i: (0, 0, 0)),
                  pl.BlockSpec((CHUNK, 128), lambda i: (0, 0))],
        out_specs=[pl.BlockSpec((tm, 2 * GROUP), lambda i: (i, 0)), pl.BlockSpec((1, GROUP), lambda i: (0, 0)),
                   pl.BlockSpec((N_HEADS, CHUNK, CHUNK), lambda i: (0, 0, 0)), pl.BlockSpec((CHUNK, 128), lambda i: (0, 0))],
        out_shape=[SDS((s, 2 * GROUP), _MXU), SDS((1, GROUP), F32), SDS((N_HEADS, CHUNK, CHUNK), F32), SDS((CHUNK, 128), F32)],
        compiler_params=_cp("arbitrary"), name=name)(dy, z, z, gain, w_s, b_t)


_SCALE_B = HEAD_DIM ** -0.5


def _ret_fwd(z, tb, name):
    s = z.shape[0]
    nc = s // CHUNK
    row = lambda col: pl.BlockSpec((CHUNK, GROUP), lambda n, col=col: (n, col))
    const = lambda shape: pl.BlockSpec(shape, lambda n: (0,) * len(shape))

    def body(q_ref, k_ref, v_ref, g_ref, cos_ref, sin_ref, dec_ref, qw_ref, kw_ref, cd_ref, y_ref, o_ref, st_ref, state):
        @pl.when(pl.program_id(0) == 0)
        def _():
            state[...] = jnp.zeros_like(state)

        q = _rope(q_ref[...], cos_ref[...], sin_ref[...], 32)
        k = _rope(k_ref[...], cos_ref[...], sin_ref[...], 32) * _SCALE_B
        v = v_ref[...]
        g = g_ref[...]
        st_ref[0] = state[...]
        qs = q * qw_ref[...]
        ks = k * kw_ref[...]
        for h in range(N_HEADS):
            sl = slice(HEAD_DIM * h, HEAD_DIM * (h + 1))
            sc = _dot_nt(q[:, sl], k[:, sl]) * dec_ref[h]
            o = _dot(sc, v[:, sl]) + _dot(qs[:, sl], state[:, sl])
            o_ref[:, sl] = o
            yh, _ = _standardize(o)
            gh = g[:, sl]
            y_ref[:, sl] = gh * _sigmoid(gh) * yh
            state[:, sl] = cd_ref[:, sl] * state[:, sl] + _dot_tn(ks[:, sl], v[:, sl])

    return pl.pallas_call(
        body, grid=(nc,),
        in_specs=[row(2), row(3), row(4), row(5), pl.BlockSpec((CHUNK, GROUP), lambda n: (n, 0)),
                  pl.BlockSpec((CHUNK, GROUP), lambda n: (n, 0)), const((N_HEADS, CHUNK, CHUNK)),
                  const((CHUNK, GROUP)), const((CHUNK, GROUP)), const((1, GROUP))],
        out_specs=[pl.BlockSpec((CHUNK, GROUP), lambda n: (n, 0)), pl.BlockSpec((CHUNK, GROUP), lambda n: (n, 0)),
                   pl.BlockSpec((1, HEAD_DIM, GROUP), lambda n: (n, 0, 0))],
        out_shape=[SDS((s, GROUP), F32), SDS((s, GROUP), F32), SDS((nc, HEAD_DIM, GROUP), F32)],
        scratch_shapes=[pltpu.VMEM((HEAD_DIM, GROUP), F32)],
        compiler_params=_cp("arbitrary"), name=name)(z, z, z, z, tb["b_cos"], tb["b_sin"], tb["decay"], tb["qw"], tb["kw"], tb["cd"])


def _ret_bwd(dy, z, o_pre, states, tb, name):
    s = z.shape[0]
    nc = s // CHUNK
    rev = lambda col: pl.BlockSpec((CHUNK, GROUP), lambda n, col=col: (nc - 1 - n, col))
    const = lambda shape: pl.BlockSpec(shape, lambda n: (0,) * len(shape))

    def body(dy_ref, q_ref, k_ref, v_ref, g_ref, o_ref, st_ref, cos_ref, sin_ref, dec_ref, qw_ref, kw2_ref, qw0_ref, cd_ref,
             dz_ref, rstate):
        @pl.when(pl.program_id(0) == 0)
        def _():
            rstate[...] = jnp.zeros_like(rstate)

        cos, sin = cos_ref[...], sin_ref[...]
        q = _rope(q_ref[...], cos, sin, 32)
        k = _rope(k_ref[...], cos, sin, 32) * _SCALE_B
        v = v_ref[...]
        g = g_ref[...]
        dyv = dy_ref[...]
        sg = _sigmoid(g)
        silu = g * sg
        dos, dgs = [], []
        for h in range(N_HEADS):
            sl = slice(HEAD_DIM * h, HEAD_DIM * (h + 1))
            yh, rs = _standardize(o_ref[:, sl])
            dgs.append(dyv[:, sl] * yh * (sg[:, sl] * (1.0 + g[:, sl] * (1.0 - sg[:, sl]))))
            dos.append(_standardize_bwd(yh, rs, dyv[:, sl] * silu[:, sl]))
        do = jnp.concatenate(dos, axis=1)
        dow = do * qw_ref[...]
        vw = v * kw2_ref[...]
        kw = k * kw2_ref[...]
        q0 = q * qw0_ref[...]
        dqs, dks = [], []
        for h in range(N_HEADS):
            sl = slice(HEAD_DIM * h, HEAD_DIM * (h + 1))
            dec = dec_ref[h]
            p = _dot_nt(q[:, sl], k[:, sl]) * dec
            dp = _dot_nt(do[:, sl], v[:, sl]) * dec
            sn = st_ref[0][:, sl]
            rr = rstate[:, sl]
            dqs.append(_dot(dp, k[:, sl]) + _dot_nt(dow[:, sl], sn))
            dks.append(_dot_tn(dp, q[:, sl]) + _dot_nt(vw[:, sl], rr))
            dv = _dot_tn(p, do[:, sl]) + _dot(kw[:, sl], rr)
            dz_ref[:, 2 * GROUP + HEAD_DIM * h:2 * GROUP + HEAD_DIM * (h + 1)] = dv.astype(dz_ref.dtype)
            rstate[:, sl] = cd_ref[:, sl] * rr + _dot_tn(q0[:, sl], do[:, sl])
        dq = _rope_bwd(jnp.concatenate(dqs, axis=1), cos, sin, 32)
        dk = _rope_bwd(jnp.concatenate(dks, axis=1) * _SCALE_B, cos, sin, 32)
        dz_ref[:, 0:GROUP] = dq.astype(dz_ref.dtype)
        dz_ref[:, GROUP:2 * GROUP] = dk.astype(dz_ref.dtype)
        dz_ref[:, 3 * GROUP:4 * GROUP] = jnp.concatenate(dgs, axis=1).astype(dz_ref.dtype)

    r0 = lambda: pl.BlockSpec((CHUNK, GROUP), lambda n: (nc - 1 - n, 0))
    return pl.pallas_call(
        body, grid=(nc,),
        in_specs=[r0(), rev(2), rev(3), rev(4), rev(5), r0(), pl.BlockSpec((1, HEAD_DIM, GROUP), lambda n: (nc - 1 - n, 0, 0)),
                  r0(), r0(), const((N_HEADS, CHUNK, CHUNK)), const((CHUNK, GROUP)), const((CHUNK, GROUP)),
                  const((CHUNK, GROUP)), const((1, GROUP))],
        out_specs=pl.BlockSpec((CHUNK, 4 * GROUP), lambda n: (nc - 1 - n, 0)),
        out_shape=SDS((s, 4 * GROUP), _MXU), scratch_shapes=[pltpu.VMEM((HEAD_DIM, GROUP), F32)],
        compiler_params=_cp("arbitrary"), name=name)(
            dy, z, z, z, z, o_pre, states, tb["b_cos"], tb["b_sin"], tb["decay"], tb["qw"], tb["kw2"], tb["qw0"], tb["cd"])


TQ = 256


def _log_sigmoid(x):
    return jnp.minimum(x, 0.0) - jnp.log1p(jnp.exp(-jnp.abs(x)))


def _fox_prep(z, b_f, name):
    s = z.shape[0]
    nb = s // TQ

    def body(m_ref, b_ref, cc_ref, cr_ref, carry):
        @pl.when(pl.program_id(0) == 0)
        def _():
            carry[...] = jnp.zeros_like(carry)

        lane = lax.broadcasted_iota(jnp.int32, (TQ, 128), 1)
        logf = jnp.where(lane < N_HEADS, _log_sigmoid(m_ref[...] + b_ref[...]), 0.0)
        r = lax.broadcasted_iota(jnp.int32, (TQ, TQ), 0)
        c = lax.broadcasted_iota(jnp.int32, (TQ, TQ), 1)
        tri = jnp.where(r >= c, 1.0, 0.0).astype(F32)
        cum = _dot_exact(tri, logf) + carry[...]
        cc_ref[...] = cum
        cr_ref[0] = cum.T[0:8, :]
        carry[...] = cum[TQ - 1:TQ, :]

    return pl.pallas_call(
        body, grid=(nb,),
        in_specs=[pl.BlockSpec((TQ, 128), lambda i: (i, NZ // 128 - 1)), pl.BlockSpec((1, 128), lambda i: (0, 0))],
        out_specs=[pl.BlockSpec((TQ, 128), lambda i: (i, 0)), pl.BlockSpec((1, 8, TQ), lambda i: (i, 0, 0))],
        out_shape=[SDS((s, 128), F32), SDS((nb, 8, TQ), F32)], scratch_shapes=[pltpu.VMEM((1, 128), F32)],
        compiler_params=_cp("arbitrary"), name=name)(z, b_f)


def _fox_post(dcr, dcq, z, b_f, dkr, name):
    s = z.shape[0]
    nb = s // TQ

    def body(dc_ref, dcq_ref, m_ref, b_ref, dkr_ref, dz_ref, db_ref, carry):
        @pl.when(pl.program_id(0) == 0)
        def _():
            carry[...] = jnp.zeros_like(carry)
            db_ref[...] = jnp.zeros_like(db_ref)

        r = lax.broadcasted_iota(jnp.int32, (TQ, TQ), 0)
        c = lax.broadcasted_iota(jnp.int32, (TQ, TQ), 1)
        triu = jnp.where(c >= r, 1.0, 0.0).astype(F32)
        dc = jnp.concatenate([dc_ref[0], jnp.zeros((120, TQ), F32)], axis=0)
        dlogf = _dot_exact(triu, dc, (((1,), (1,)), ((), ()))) + _dot_exact(triu, dcq_ref[...]) + carry[...]
        carry[...] = dlogf[0:1, :]
        x = m_ref[...] + b_ref[...]
        lane = lax.broadcasted_iota(jnp.int32, (TQ, 128), 1)
        df = jnp.where(lane < N_HEADS, dlogf * _sigmoid(-x), 0.0)
        db_ref[...] += jnp.sum(df, axis=0, keepdims=True)
        dz_ref[...] = (df + dkr_ref[...]).astype(dz_ref.dtype)

    rv = lambda i: nb - 1 - i
    return pl.pallas_call(
        body, grid=(nb,),
        in_specs=[pl.BlockSpec((1, 8, TQ), lambda i: (rv(i), 0, 0)), pl.BlockSpec((TQ, 128), lambda i: (rv(i), 0)),
                  pl.BlockSpec((TQ, 128), lambda i: (rv(i), NZ // 128 - 1)),
                  pl.BlockSpec((1, 128), lambda i: (0, 0)), pl.BlockSpec((TQ, 128), lambda i: (rv(i), 0))],
        out_specs=[pl.BlockSpec((TQ, 128), lambda i: (rv(i), 0)), pl.BlockSpec((1, 128), lambda i: (0, 0))],
        out_shape=[SDS((s, 128), _MXU), SDS((1, 128), F32)], scratch_shapes=[pltpu.VMEM((1, 128), F32)],
        compiler_params=_cp("arbitrary"), name=name)(dcr, dcq, z, b_f, dkr)


NEG = -1e30


def _causal_mask(shape, transposed=False):
    r = lax.broadcasted_iota(jnp.int32, shape, 0)
    c = lax.broadcasted_iota(jnp.int32, shape, 1)
    return (c >= r) if transposed else (r >= c)


def _flash_fwd(q, k, v, cols, dqk, scale, cum, name):
    s = q.shape[0]
    nq = s // TQ
    wq = N_HEADS * dqk
    bias = cum is not None

    def body(*refs):
        if bias:
            q_ref, k_ref, v_ref, cc_ref, cr_ref, o_ref, l_ref = refs
        else:
            q_ref, k_ref, v_ref, o_ref, l_ref = refs
        i = pl.program_id(0)
        l_ref[...] = jnp.zeros_like(l_ref)
        for h in range(N_HEADS):
            qh = q_ref[:, dqk * h:dqk * (h + 1)].astype(_MXU)
            cq = cc_ref[:, h:h + 1] if bias else None

            def step(j, carry, masked):
                m, l, acc = carry
                r0 = pl.multiple_of(j * TQ, TQ)
                kh = k_ref[pl.ds(r0, TQ), dqk * h:dqk * (h + 1)]
                vh = v_ref[pl.ds(r0, TQ), HEAD_DIM * h:HEAD_DIM * (h + 1)]
                sc = _dot_nt(qh, kh) * scale
                if bias:
                    sc = sc + (cq - cr_ref[j][h:h + 1, :])
                if masked:
                    sc = jnp.where(_causal_mask(sc.shape), sc, NEG)
                m_new = jnp.maximum(m, jnp.max(sc, axis=-1, keepdims=True))
                alpha = jnp.exp(m - m_new)
                p = jnp.exp(sc - m_new)
                return m_new, alpha * l + jnp.sum(p, axis=-1, keepdims=True), alpha * acc + _dot(p, vh)

            init = (jnp.full((TQ, 1), NEG, F32), jnp.zeros((TQ, 1), F32), jnp.zeros((TQ, HEAD_DIM), F32))
            carry = lax.fori_loop(0, i, functools.partial(step, masked=False), init)
            m, l, acc = step(i, carry, True)
            o_ref[:, HEAD_DIM * h:HEAD_DIM * (h + 1)] = acc / l
            l_ref[:, h:h + 1] = m + jnp.log(l)

    in_specs = [pl.BlockSpec((TQ, wq), lambda i: (i, cols[0])), pl.BlockSpec((s, wq), lambda i: (0, cols[1])),
                pl.BlockSpec((s, GROUP), lambda i: (0, cols[2]))]
    args = [q, k, v]
    if bias:
        in_specs += [pl.BlockSpec((TQ, 128), lambda i: (i, 0)), pl.BlockSpec((nq, 8, TQ), lambda i: (0, 0, 0))]
        args += list(cum)
    return pl.pallas_call(
        body, grid=(nq,), in_specs=in_specs,
        out_specs=[pl.BlockSpec((TQ, GROUP), lambda i: (i, 0)), pl.BlockSpec((TQ, 128), lambda i: (i, 0))],
        out_shape=[SDS((s, GROUP), F32), SDS((s, 128), F32)],
        compiler_params=_cp("parallel"), name=name)(*args)


def _flash_bwd(q, k, v, cols, dqk, scale, cum, do, lse, delta, name, kv_dtype):
    s = q.shape[0]
    nq = s // TQ
    wq = N_HEADS * dqk
    bias = cum is not None

    def body(*refs):
        if bias:
            q_ref, k_ref, v_ref, do_ref, l_ref, d_ref, cc_ref, cr_ref, dq_ref, dk_ref, dv_ref, dc_ref, dcq_ref = refs
        else:
            q_ref, k_ref, v_ref, do_ref, l_ref, d_ref, dq_ref, dk_ref, dv_ref = refs
        j = pl.program_id(0)

        @pl.when(j == 0)
        def _():
            dq_ref[...] = jnp.zeros_like(dq_ref)
            if bias:
                dcq_ref[...] = jnp.zeros_like(dcq_ref)

        if bias:
            dc_ref[...] = jnp.zeros_like(dc_ref)
        for h in range(N_HEADS):
            hq = slice(dqk * h, dqk * (h + 1))
            hv = slice(HEAD_DIM * h, HEAD_DIM * (h + 1))
            kh = k_ref[:, hq].astype(_MXU)
            vh = v_ref[:, hv].astype(_MXU)
            ck = cr_ref[0][h:h + 1, :] if bias else None

            def step(i, carry, masked):
                dk, dv, dc = carry
                r0 = pl.multiple_of(i * TQ, TQ)
                rows = pl.ds(r0, TQ)
                qh = q_ref[rows, hq].astype(_MXU)
                doh = do_ref[rows, hv].astype(_MXU)
                sc = _dot_nt(qh, kh) * scale
                if bias:
                    sc = sc + (cc_ref[rows, h:h + 1] - ck)
                p = jnp.exp(sc - l_ref[rows, h:h + 1])
                if masked:
                    p = jnp.where(_causal_mask(p.shape), p, 0.0)
                dv = dv + _dot_tn(p, doh)
                dp = _dot_nt(doh, vh)
                ds = p * (dp - d_ref[rows, h:h + 1])
                dk = dk + _dot_tn(ds, qh) * scale
                dq_ref[rows, hq] += _dot(ds, kh) * scale
                if bias:
                    dc = dc + jnp.sum(ds, axis=0, keepdims=True)
                    dcq_ref[rows, h:h + 1] += jnp.sum(ds, axis=1, keepdims=True)
                return dk, dv, dc

            init = (jnp.zeros((TQ, dqk), F32), jnp.zeros((TQ, HEAD_DIM), F32), jnp.zeros((1, TQ), F32))
            carry = step(j, init, True)
            dk, dv, dc = lax.fori_loop(j + 1, nq, functools.partial(step, masked=False), carry)
            dk_ref[:, hq] = dk.astype(dk_ref.dtype)
            dv_ref[:, hv] = dv.astype(dv_ref.dtype)
            if bias:
                dc_ref[0, h:h + 1, :] = -dc

    full = lambda w, c=0: pl.BlockSpec((s, w), lambda j, c=c: (0, c))
    in_specs = [full(wq, cols[0]), pl.BlockSpec((TQ, wq), lambda j: (j, cols[1])), pl.BlockSpec((TQ, GROUP), lambda j: (j, cols[2])),
                full(GROUP), full(128), full(128)]
    args = [q, k, v, do, lse, delta]
    out_specs = [full(wq), pl.BlockSpec((TQ, wq), lambda j: (j, 0)), pl.BlockSpec((TQ, GROUP), lambda j: (j, 0))]
    out_shape = [SDS((s, wq), F32), SDS((s, wq), kv_dtype), SDS((s, GROUP), kv_dtype)]
    if bias:
        in_specs += [full(128), pl.BlockSpec((1, 8, TQ), lambda j: (j, 0, 0))]
        args += list(cum)
        out_specs += [pl.BlockSpec((1, 8, TQ), lambda j: (j, 0, 0)), full(128)]
        out_shape += [SDS((nq, 8, TQ), F32), SDS((s, 128), F32)]
    return pl.pallas_call(body, grid=(nq,), in_specs=in_specs, out_specs=out_specs, out_shape=out_shape,
                          compiler_params=_cp("arbitrary"), name=name)(*args)


def _head_lanes(h, dqk):
    return slice(128 * (h // 2), 128 * (h // 2) + 128) if dqk == HEAD_DIM else slice(128 * h, 128 * h + 128)


def _keep_half(x, a, axis):
    idx = lax.broadcasted_iota(jnp.int32, x.shape, axis)
    return jnp.where((idx < HEAD_DIM) if a == 0 else (idx >= HEAD_DIM), x, jnp.zeros_like(x))


def _kv_prep(z, kcol, vcol, name):
    s = z.shape[0]
    nk = s // TQ

    def body(k_ref, v_ref, kb_ref, vb_ref, vt_ref):
        kb_ref[...] = k_ref[...].astype(_MXU)
        v = v_ref[...]
        vb_ref[...] = v.astype(_MXU)
        vt_ref[0] = v.T.astype(_MXU)

    blk = pl.BlockSpec((TQ, GROUP), lambda i: (i, 0))
    return pl.pallas_call(
        body, grid=(nk,),
        in_specs=[pl.BlockSpec((TQ, GROUP), lambda i: (i, kcol)), pl.BlockSpec((TQ, GROUP), lambda i: (i, vcol))],
        out_specs=[blk, blk, pl.BlockSpec((1, GROUP, TQ), lambda i: (i, 0, 0))],
        out_shape=[SDS((s, GROUP), _MXU), SDS((s, GROUP), _MXU), SDS((nk, GROUP, TQ), _MXU)],
        compiler_params=_cp("parallel"), name=name)(z, z)


def _attn_fwd(q, qcol, dqk, kb, vt, scale, cum, name, comm=None):
    s = q.shape[0]
    nq = s // TQ
    wq = N_HEADS * dqk
    bias = cum is not None

    def body(*refs):
        ins, (o_ref, l_ref), _, cc = _split_refs(refs, 5 if bias else 3, 2, comm)
        if bias:
            q_ref, k_ref, vt_ref, cc_ref, cr_ref = ins
        else:
            q_ref, k_ref, vt_ref = ins
        i = pl.program_id(0)
        if comm is not None:
            @pl.when(i == 0)
            def _():
                comm.start(*cc)

        qts = []
        for h in range(N_HEADS):
            qt = q_ref[:, _head_lanes(h, dqk)].astype(F32).T
            qts.append((_keep_half(qt, h % 2, 0) if dqk == HEAD_DIM else qt).astype(_MXU))
        cqs = [cr_ref[0][h:h + 1, :] for h in range(N_HEADS)] if bias else None

        def step(j, carry, masked):
            r0 = pl.multiple_of(j * TQ, TQ)
            vtj = vt_ref[j]
            sts = [jnp.dot(k_ref[pl.ds(r0, TQ), _head_lanes(h, dqk)], qts[h], preferred_element_type=F32)
                   for h in range(N_HEADS)]
            stats, ps = [], []
            for h in range(N_HEADS):
                m, l, _ = carry[3 * h:3 * h + 3]
                st = sts[h] * scale
                if bias:
                    st = st + (cqs[h] - cc_ref[pl.ds(r0, TQ), h:h + 1])
                if masked:
                    st = jnp.where(_causal_mask(st.shape, transposed=True), st, NEG)
                m_new = jnp.maximum(m, jnp.max(st, axis=0, keepdims=True))
                alpha = jnp.exp(m - m_new)
                p = jnp.exp(st - m_new)
                stats.append((m_new, alpha * l + jnp.sum(p, axis=0, keepdims=True), alpha))
                ps.append(p.astype(_MXU))
            out = []
            for h in range(N_HEADS):
                m_new, l, alpha = stats[h]
                acc = alpha * carry[3 * h + 2] + jnp.dot(vtj[HEAD_DIM * h:HEAD_DIM * (h + 1), :], ps[h],
                                                         preferred_element_type=F32)
                out += [m_new, l, acc]
            return tuple(out)

        init = (jnp.full((1, TQ), NEG, F32), jnp.zeros((1, TQ), F32), jnp.zeros((HEAD_DIM, TQ), F32)) * N_HEADS
        carry = lax.fori_loop(0, i, functools.partial(step, masked=False), init)
        carry = step(i, carry, True)
        l_ref[...] = jnp.zeros_like(l_ref)
        for h in range(N_HEADS):
            l_ref[0, h:h + 1, :] = carry[3 * h] + jnp.log(carry[3 * h + 1])
        for p in range(2):
            ot = jnp.concatenate([carry[6 * p + 2] / carry[6 * p + 1], carry[6 * p + 5] / carry[6 * p + 4]], axis=0)
            o_ref[:, 128 * p:128 * (p + 1)] = ot.T
        if comm is not None:
            @pl.when(i == nq - 1)
            def _():
                comm.wait(*cc)

    rows = pl.BlockSpec((1, 8, TQ), lambda i: (i, 0, 0))
    in_specs = [pl.BlockSpec((TQ, wq), lambda i: (i, qcol)), pl.BlockSpec((s, wq), lambda i: (0, 0)),
                pl.BlockSpec((nq, GROUP, TQ), lambda i: (0, 0, 0))]
    args = [q, kb, vt]
    if bias:
        in_specs += [pl.BlockSpec((s, 128), lambda i: (0, 0)), rows]
        args += list(cum)
    out_specs = [pl.BlockSpec((TQ, GROUP), lambda i: (i, 0)), rows]
    out_shape = [SDS((s, GROUP), F32), SDS((nq, 8, TQ), F32)]
    return _call_with_comm(body, (nq,), in_specs, out_specs, out_shape, [], args, comm, ("arbitrary",), name)


def _call_with_comm(body, grid, in_specs, out_specs, out_shape, scratch, args, comm, semantics, name):
    n_out = len(out_shape)
    if comm is not None:
        in_specs, out_specs = in_specs + comm.in_specs, out_specs + comm.out_specs
        out_shape, scratch, args = out_shape + comm.out_shape, scratch + comm.scratch, list(args) + comm.arrs
    res = pl.pallas_call(body, grid=grid, in_specs=in_specs, out_specs=out_specs, out_shape=out_shape,
                         scratch_shapes=scratch, compiler_params=_cp(*semantics), name=name)(*args)
    return (*res[:n_out], list(res[n_out:]))


def _attn_bwd_prep(q, qcol, dqk, o, do, name):
    s = q.shape[0]
    nq = s // TQ
    wq = N_HEADS * dqk

    def body(q_ref, o_ref, do_ref, qt_ref, dot_ref, dl_ref):
        qt_ref[0] = q_ref[...].astype(F32).T.astype(_MXU)
        dov = do_ref[...]
        dot_ref[0] = dov.T.astype(_MXU)
        pt = (dov * o_ref[...]).T
        dl_ref[...] = jnp.zeros_like(dl_ref)
        for h in range(N_HEADS):
            dl_ref[0, h:h + 1, :] = jnp.sum(pt[HEAD_DIM * h:HEAD_DIM * (h + 1), :], axis=0, keepdims=True)

    nat = lambda w: pl.BlockSpec((TQ, w), lambda i: (i, 0))
    tr = lambda w: pl.BlockSpec((1, w, TQ), lambda i: (i, 0, 0))
    return pl.pallas_call(
        body, grid=(nq,),
        in_specs=[pl.BlockSpec((TQ, wq), lambda i: (i, qcol)), nat(GROUP), nat(GROUP)],
        out_specs=[tr(wq), tr(GROUP), tr(8)],
        out_shape=[SDS((nq, wq, TQ), _MXU), SDS((nq, GROUP, TQ), _MXU), SDS((nq, 8, TQ), F32)],
        compiler_params=_cp("parallel"), name=name)(q, o, do)


def _attn_bwd(kb, vb, qt, dot, lse, dl, dqk, scale, cum, name, kv_dtype, comm=None):
    s = kb.shape[0]
    nq = s // TQ
    wq = N_HEADS * dqk
    bias = cum is not None

    def body(*refs):
        ins, outs, _, cc = _split_refs(refs, 8 if bias else 6, 5 if bias else 3, comm)
        if bias:
            k_ref, v_ref, qt_ref, dot_ref, l_ref, d_ref, cc_ref, cr_ref = ins
            dqt_ref, dk_ref, dv_ref, dck_ref, dcq_ref = outs
        else:
            k_ref, v_ref, qt_ref, dot_ref, l_ref, d_ref = ins
            dqt_ref, dk_ref, dv_ref = outs
        j = pl.program_id(0)

        @pl.when(j == 0)
        def _():
            if comm is not None:
                comm.start(*cc)
            dqt_ref[...] = jnp.zeros_like(dqt_ref)
            if bias:
                dcq_ref[...] = jnp.zeros_like(dcq_ref)

        ks, kts, vs = [], [], []
        for h in range(N_HEADS):
            k2 = k_ref[:, _head_lanes(h, dqk)]
            if dqk == HEAD_DIM:
                k2 = _keep_half(k2, h % 2, 1)
            ks.append(k2)
            kts.append(k2.astype(F32).T.astype(_MXU))
            vs.append(_keep_half(v_ref[:, _head_lanes(h, HEAD_DIM)], h % 2, 1))
        cks = [cc_ref[:, h:h + 1] for h in range(N_HEADS)] if bias else None

        nt = (((1,), (1,)), ((), ()))

        def step(i, carry, masked):
            qti, doti, li, di = qt_ref[i], dot_ref[i], l_ref[i], d_ref[i]
            cri = cr_ref[i] if bias else None
            qls = [_head_lanes(h, dqk) for h in range(N_HEADS)]
            vls = [_head_lanes(h, HEAD_DIM) for h in range(N_HEADS)]
            sts = [jnp.dot(ks[h], qti[qls[h], :], preferred_element_type=F32) for h in range(N_HEADS)]
            dpts = [jnp.dot(vs[h], doti[vls[h], :], preferred_element_type=F32) for h in range(N_HEADS)]
            pbs, dsbs, dcks = [], [], []
            for h in range(N_HEADS):
                st = sts[h] * scale
                rowterm = li[h:h + 1, :]
                if bias:
                    st = st + ((cri[h:h + 1, :] - rowterm) - cks[h])
                else:
                    st = st - rowterm
                p = jnp.exp(st)
                if masked:
                    p = jnp.where(_causal_mask(p.shape, transposed=True), p, 0.0)
                dst = p * (dpts[h] - di[h:h + 1, :])
                pbs.append(p.astype(_MXU))
                dsbs.append(dst.astype(_MXU))
                if bias:
                    dcks.append(carry[3 * h + 2] + jnp.sum(dst, axis=1, keepdims=True))
                    dcq_ref[i, h:h + 1, :] += jnp.sum(dst, axis=0, keepdims=True)
                else:
                    dcks.append(carry[3 * h + 2])
            out = []
            for h in range(N_HEADS):
                dvt = carry[3 * h + 1] + lax.dot_general(doti[HEAD_DIM * h:HEAD_DIM * (h + 1), :], pbs[h], nt,
                                                         preferred_element_type=F32)
                dkt = carry[3 * h] + lax.dot_general(qti[dqk * h:dqk * (h + 1), :], dsbs[h], nt, preferred_element_type=F32)
                dqt_ref[i, qls[h], :] += jnp.dot(kts[h], dsbs[h], preferred_element_type=F32) * scale
                out += [dkt, dvt, dcks[h]]
            return tuple(out)

        init = (jnp.zeros((dqk, TQ), F32), jnp.zeros((HEAD_DIM, TQ), F32), jnp.zeros((TQ, 1), F32)) * N_HEADS
        carry = step(j, init, True)
        carry = lax.fori_loop(j + 1, nq, functools.partial(step, masked=False), carry)
        for p in range(2):
            dv_ref[:, 128 * p:128 * (p + 1)] = jnp.concatenate([carry[6 * p + 1], carry[6 * p + 4]], axis=0).T.astype(dv_ref.dtype)
            if dqk == HEAD_DIM:
                dk_ref[:, 128 * p:128 * (p + 1)] = (jnp.concatenate([carry[6 * p], carry[6 * p + 3]], axis=0).T * scale).astype(dk_ref.dtype)
        if dqk != HEAD_DIM:
            for h in range(N_HEADS):
                dk_ref[:, 128 * h:128 * (h + 1)] = (carry[3 * h].T * scale).astype(dk_ref.dtype)
        if bias:
            dck_ref[...] = jnp.zeros_like(dck_ref)
            for h in range(N_HEADS):
                dck_ref[:, h:h + 1] = -carry[3 * h + 2]
        if comm is not None:
            @pl.when(j == nq - 1)
            def _():
                comm.wait(*cc)

    blk = lambda w: pl.BlockSpec((TQ, w), lambda j: (j, 0))
    full3 = lambda w: pl.BlockSpec((nq, w, TQ), lambda j: (0, 0, 0))
    in_specs = [blk(wq), blk(GROUP), full3(wq), full3(GROUP), full3(8), full3(8)]
    args = [kb, vb, qt, dot, lse, dl]
    out_specs = [full3(wq), blk(wq), blk(GROUP)]
    out_shape = [SDS((nq, wq, TQ), F32), SDS((s, wq), kv_dtype), SDS((s, GROUP), kv_dtype)]
    if bias:
        in_specs += [blk(128), full3(8)]
        args += list(cum)
        out_specs += [blk(128), full3(8)]
        out_shape += [SDS((s, 128), F32), SDS((nq, 8, TQ), F32)]
    return _call_with_comm(body, (nq,), in_specs, out_specs, out_shape, [], args, comm, ("arbitrary",), name)


def _untranspose(xt, dtype, name):
    nq, w, _ = xt.shape

    def body(x_ref, o_ref):
        o_ref[...] = x_ref[0].T.astype(o_ref.dtype)

    return pl.pallas_call(
        body, grid=(nq,), in_specs=[pl.BlockSpec((1, w, TQ), lambda i: (i, 0, 0))],
        out_specs=pl.BlockSpec((TQ, w), lambda i: (i, 0)), out_shape=SDS((nq * TQ, w), dtype),
        compiler_params=_cp("parallel"), name=name)(xt)


_SCALE_D = (64 + 32) ** -0.5
_COL_CQ, _COL_CKV, _COL_MISC = 2304 // 256, 2560 // 128, 2688 // 128


def _mla_prep(z, gq, gkv, wq, wk, wv, tb, name):
    s = z.shape[0]
    tm = TQ
    row = lambda w, c: pl.BlockSpec((tm, w), lambda i, c=c: (i, c))
    const = lambda a: pl.BlockSpec(a.shape, lambda i: (0,) * a.ndim)

    def body(cq_ref, ckv_ref, m_ref, gq_ref, gkv_ref, wq_ref, wk_ref, wv_ref, e_ref, qc_ref, qs_ref, kc_ref, ks_ref,
             q_ref, k_ref, v_ref, vt_ref, cqn_ref, ckvn_ref):
        cqn = _rms(cq_ref[...], gq_ref[...]).astype(_MXU)
        ckvn = _rms(ckv_ref[...], gkv_ref[...]).astype(_MXU)
        cqn_ref[...] = cqn
        ckvn_ref[...] = ckvn
        q_ref[...] = _rope(_dot(cqn, wq_ref[...]), qc_ref[...], qs_ref[...], 16).astype(q_ref.dtype)
        kr = _rope(m_ref[...], kc_ref[...], ks_ref[...], 16)
        k_ref[...] = (_dot(ckvn, wk_ref[...]) + _dot(kr, e_ref[...])).astype(k_ref.dtype)
        v = _dot(ckvn, wv_ref[...])
        v_ref[...] = v.astype(v_ref.dtype)
        vt_ref[0] = v.T.astype(vt_ref.dtype)

    e = tb["place"]
    return pl.pallas_call(
        body, grid=(s // tm,),
        in_specs=[row(256, _COL_CQ), row(128, _COL_CKV), row(128, _COL_MISC), const(gq), const(gkv), const(wq), const(wk),
                  const(wv), const(e), row(512, 0), row(512, 0), row(128, 0), row(128, 0)],
        out_specs=[row(512, 0), row(512, 0), row(256, 0), pl.BlockSpec((1, GROUP, TQ), lambda i: (i, 0, 0)), row(256, 0),
                   row(128, 0)],
        out_shape=[SDS((s, 512), _MXU), SDS((s, 512), _MXU), SDS((s, 256), _MXU), SDS((s // TQ, GROUP, TQ), _MXU),
                   SDS((s, 256), _MXU), SDS((s, 128), _MXU)],
        compiler_params=_cp("parallel"), name=name)(
            z, z, z, gq, gkv, wq, wk, wv, e, tb["q_cos"], tb["q_sin"], tb["k_cos"], tb["k_sin"])


def _mla_prep_bwd(dq, dk, dv, z, cqn, ckvn, gq, gkv, wq, wk, wv, tb, name):
    s = z.shape[0]
    tm = min(512, s)
    row = lambda w, c: pl.BlockSpec((tm, w), lambda i, c=c: (i, c))
    const = lambda a: pl.BlockSpec(a.shape, lambda i: (0,) * a.ndim)
    acc = lambda shape: pl.BlockSpec(shape, lambda i: (0, 0))

    def body(dq_ref, dk_ref, dv_ref, cq_ref, ckv_ref, cqn_ref, ckvn_ref, gq_ref, gkv_ref, wq_ref, wk_ref, wv_ref, e_ref,
             qc_ref, qs_ref, kc_ref, ks_ref, dcq_ref, dckv_ref, dkr_ref, dwq_ref, dwk_ref, dwv_ref, dgq_ref, dgkv_ref):
        @pl.when(pl.program_id(0) == 0)
        def _():
            for r in (dwq_ref, dwk_ref, dwv_ref, dgq_ref, dgkv_ref):
                r[...] = jnp.zeros_like(r)

        dqp = _rope_bwd(dq_ref[...], qc_ref[...], qs_ref[...], 16)
        dkd = dk_ref[...]
        dvd = dv_ref[...]
        dwq_ref[...] += _dot_tn(cqn_ref[...], dqp)
        dwk_ref[...] += _dot_tn(ckvn_ref[...], dkd)
        dwv_ref[...] += _dot_tn(ckvn_ref[...], dvd)
        dcq, dgq = _rms_bwd(cq_ref[...], gq_ref[...], _dot_nt(dqp, wq_ref[...]))
        dckv, dgkv = _rms_bwd(ckv_ref[...], gkv_ref[...], _dot_nt(dkd, wk_ref[...]) + _dot_nt(dvd, wv_ref[...]))
        dcq_ref[...] = dcq.astype(dcq_ref.dtype)
        dckv_ref[...] = dckv.astype(dckv_ref.dtype)
        dgq_ref[...] += dgq
        dgkv_ref[...] += dgkv
        dkr = _dot_exact(dkd, e_ref[...], (((1,), (1,)), ((), ())))
        dkr_ref[...] = _rope_bwd(dkr, kc_ref[...], ks_ref[...], 16)

    e = tb["place"]
    return pl.pallas_call(
        body, grid=(s // tm,),
        in_specs=[row(512, 0), row(512, 0), row(256, 0), row(256, _COL_CQ), row(128, _COL_CKV), row(256, 0), row(128, 0),
                  const(gq), const(gkv), const(wq), const(wk), const(wv), const(e), row(512, 0), row(512, 0), row(128, 0), row(128, 0)],
        out_specs=[row(256, 0), row(128, 0), row(128, 0), acc((256, 512)), acc((128, 512)), acc((128, 256)), acc((1, 256)),
                   acc((1, 128))],
        out_shape=[SDS((s, 256), _MXU), SDS((s, 128), _MXU), SDS((s, 128), F32), SDS((256, 512), F32), SDS((128, 512), F32),
                   SDS((128, 256), F32), SDS((1, 256), F32), SDS((1, 128), F32)],
        compiler_params=_cp("arbitrary"), name=name)(
            dq, dk, dv, z, z, cqn, ckvn, gq, gkv, wq, wk, wv, e, tb["q_cos"], tb["q_sin"], tb["k_cos"], tb["k_sin"])


def _out_proj(ys, g, w, x, name):
    s, d = x.shape
    tm = min(512, s)

    def body(ya, yb, yc, yd, g_ref, w_ref, x_ref, o_ref, yn_ref):
        acc = x_ref[...]
        for i, y_ref in enumerate((ya, yb, yc, yd)):
            sl = slice(GROUP * i, GROUP * (i + 1))
            yn = _rms(y_ref[...], g_ref[:, sl]).astype(_MXU)
            yn_ref[:, sl] = yn
            acc = acc + jnp.dot(yn, w_ref[sl, :], preferred_element_type=F32)
        o_ref[...] = acc

    yspec = pl.BlockSpec((tm, GROUP), lambda i: (i, 0))
    return pl.pallas_call(
        body, grid=(s // tm,),
        in_specs=[yspec, yspec, yspec, yspec, pl.BlockSpec((1, d), lambda i: (0, 0)), pl.BlockSpec((d, d), lambda i: (0, 0)),
                  pl.BlockSpec((tm, d), lambda i: (i, 0))],
        out_specs=[pl.BlockSpec((tm, d), lambda i: (i, 0)), pl.BlockSpec((tm, d), lambda i: (i, 0))],
        out_shape=[SDS((s, d), F32), SDS((s, d), _MXU)], compiler_params=_cp("parallel"), name=name)(*ys, g, w, x)


def _out_proj_bwd(dx, w, ys, g, name):
    s, d = dx.shape
    tm = min(512, s)

    def body(dx_ref, w_ref, ya, yb, yc, yd, g_ref, da, db, dc, dd, dg_ref):
        @pl.when(pl.program_id(0) == 0)
        def _():
            dg_ref[...] = jnp.zeros_like(dg_ref)

        dyn = _dot_nt(dx_ref[...], w_ref[...])
        outs = (da, db, dc, dd)
        for i, y_ref in enumerate((ya, yb, yc, yd)):
            sl = slice(GROUP * i, GROUP * (i + 1))
            dy, dg = _rms_bwd(y_ref[...], g_ref[:, sl], dyn[:, sl])
            outs[i][...] = dy
            dg_ref[:, sl] += dg

    yspec = pl.BlockSpec((tm, GROUP), lambda i: (i, 0))
    return pl.pallas_call(
        body, grid=(s // tm,),
        in_specs=[pl.BlockSpec((tm, d), lambda i: (i, 0)), pl.BlockSpec((d, d), lambda i: (0, 0)), yspec, yspec, yspec, yspec,
                  pl.BlockSpec((1, d), lambda i: (0, 0))],
        out_specs=[yspec, yspec, yspec, yspec, pl.BlockSpec((1, d), lambda i: (0, 0))],
        out_shape=[SDS((s, GROUP), F32)] * 4 + [SDS((1, d), F32)],
        compiler_params=_cp("arbitrary"), name=name)(dx, w, *ys, g)


FF_BLOCK = 512


def _ffn_fwd(x, g, wu, wd, name, comm=None):
    s, d = x.shape
    nj = wu.shape[0]
    tm = min(512, s)
    ni = s // tm

    def body(*refs):
        (x_ref, g_ref, wu_ref, wd_ref), (o_ref, u_ref, h_ref), (acc,), cc = _split_refs(refs, 4, 3, comm)
        i, j = pl.program_id(0), pl.program_id(1)
        if comm is not None:
            @pl.when((i == 0) & (j == 0))
            def _():
                comm.start(*cc)

        @pl.when(j == 0)
        def _():
            h_ref[...] = _rms(x_ref[...], g_ref[...]).astype(h_ref.dtype)
            acc[...] = jnp.zeros_like(acc)

        u = jnp.dot(h_ref[...], wu_ref[0], preferred_element_type=F32)
        u_ref[...] = u.astype(u_ref.dtype)
        acc[...] += _dot(jnp.square(jnp.maximum(u, 0.0)), wd_ref[...])

        @pl.when(j == nj - 1)
        def _():
            o_ref[...] = x_ref[...] + acc[...]

        if comm is not None:
            @pl.when((i == ni - 1) & (j == nj - 1))
            def _():
                comm.wait(*cc)

    in_specs = [pl.BlockSpec((tm, d), lambda i, j: (i, 0)), pl.BlockSpec((1, d), lambda i, j: (0, 0)),
                pl.BlockSpec((1, d, FF_BLOCK), lambda i, j: (j, 0, 0)), pl.BlockSpec((FF_BLOCK, d), lambda i, j: (j, 0))]
    out_specs = [pl.BlockSpec((tm, d), lambda i, j: (i, 0)), pl.BlockSpec((tm, FF_BLOCK), lambda i, j: (i, j)),
                 pl.BlockSpec((tm, d), lambda i, j: (i, 0))]
    out_shape = [SDS((s, d), F32), SDS((s, nj * FF_BLOCK), _MXU), SDS((s, d), _MXU)]
    return _call_with_comm(body, (ni, nj), in_specs, out_specs, out_shape, [pltpu.VMEM((tm, d), F32)], [x, g, wu, wd], comm,
                           ("arbitrary", "arbitrary"), name)


def _ffn_bwd(dx2, x, u, g, wu, wd, name, comm=None):
    s, d = x.shape
    nj = wu.shape[0]
    tm = min(512, s)
    ni = s // tm

    def body(*refs):
        (dx_ref, x_ref, u_ref, g_ref, wu_ref, wd_ref), (o_ref, du_ref, dg_ref), (acc, dxb), cc = _split_refs(refs, 6, 3, comm)
        i, j = pl.program_id(0), pl.program_id(1)

        @pl.when((i == 0) & (j == 0))
        def _():
            if comm is not None:
                comm.start(*cc)
            dg_ref[...] = jnp.zeros_like(dg_ref)

        @pl.when(j == 0)
        def _():
            dxb[...] = dx_ref[...].astype(dxb.dtype)
            acc[...] = jnp.zeros_like(acc)

        da = lax.dot_general(dxb[...], wd_ref[...], (((1,), (1,)), ((), ())), preferred_element_type=F32)
        du = (da * 2.0 * jnp.maximum(u_ref[...].astype(F32), 0.0)).astype(du_ref.dtype)
        du_ref[...] = du
        acc[...] += lax.dot_general(du, wu_ref[0], (((1,), (1,)), ((), ())), preferred_element_type=F32)

        @pl.when(j == nj - 1)
        def _():
            dxn, dg = _rms_bwd(x_ref[...], g_ref[...], acc[...])
            o_ref[...] = dx_ref[...] + dxn
            dg_ref[...] += dg

        if comm is not None:
            @pl.when((i == ni - 1) & (j == nj - 1))
            def _():
                comm.wait(*cc)

    in_specs = [pl.BlockSpec((tm, d), lambda i, j: (i, 0)), pl.BlockSpec((tm, d), lambda i, j: (i, 0)),
                pl.BlockSpec((tm, FF_BLOCK), lambda i, j: (i, j)), pl.BlockSpec((1, d), lambda i, j: (0, 0)),
                pl.BlockSpec((1, d, FF_BLOCK), lambda i, j: (j, 0, 0)), pl.BlockSpec((FF_BLOCK, d), lambda i, j: (j, 0))]
    out_specs = [pl.BlockSpec((tm, d), lambda i, j: (i, 0)), pl.BlockSpec((tm, FF_BLOCK), lambda i, j: (i, j)),
                 pl.BlockSpec((1, d), lambda i, j: (0, 0))]
    out_shape = [SDS((s, d), F32), SDS((s, nj * FF_BLOCK), _MXU), SDS((1, d), F32)]
    return _call_with_comm(body, (ni, nj), in_specs, out_specs, out_shape,
                           [pltpu.VMEM((tm, d), F32), pltpu.VMEM((tm, d), _MXU)], [dx2, x, u, g, wu, wd], comm,
                           ("arbitrary", "arbitrary"), name)


def _in_proj_bwd(dz, w, x, g, dx_up, name):
    s, d = x.shape
    n = w.shape[1]
    tm = min(512, s)

    def body(dz_ref, w_ref, x_ref, g_ref, up_ref, o_ref, dg_ref):
        @pl.when(pl.program_id(0) == 0)
        def _():
            dg_ref[...] = jnp.zeros_like(dg_ref)

        dh = lax.dot_general(dz_ref[...], w_ref[...], (((1,), (1,)), ((), ())), preferred_element_type=F32)
        dxn, dg = _rms_bwd(x_ref[...], g_ref[...], dh)
        o_ref[...] = up_ref[...] + dxn
        dg_ref[...] += dg

    return pl.pallas_call(
        body, grid=(s // tm,),
        in_specs=[pl.BlockSpec((tm, n), lambda i: (i, 0)), pl.BlockSpec((d, n), lambda i: (0, 0)),
                  pl.BlockSpec((tm, d), lambda i: (i, 0)), pl.BlockSpec((1, d), lambda i: (0, 0)),
                  pl.BlockSpec((tm, d), lambda i: (i, 0))],
        out_specs=[pl.BlockSpec((tm, d), lambda i: (i, 0)), pl.BlockSpec((1, d), lambda i: (0, 0))],
        out_shape=[SDS((s, d), F32), SDS((1, d), F32)], compiler_params=_cp("arbitrary"), name=name)(dz, w, x, g, dx_up)


def _loss_head(x, g, target, name):
    s, d = x.shape
    tm = min(512, s)

    def body(x_ref, g_ref, t_ref, l_ref, dx_ref, dg_ref):
        @pl.when(pl.program_id(0) == 0)
        def _():
            l_ref[...] = jnp.zeros_like(l_ref)
            dg_ref[...] = jnp.zeros_like(dg_ref)

        xv = x_ref[...]
        err = _rms(xv, g_ref[...]) - t_ref[...]
        l_ref[...] += jnp.sum(err * err, axis=0, keepdims=True) * (0.5 / d)
        dx, dg = _rms_bwd(xv, g_ref[...], err * (1.0 / d))
        dx_ref[...] = dx
        dg_ref[...] += dg

    return pl.pallas_call(
        body, grid=(s // tm,),
        in_specs=[pl.BlockSpec((tm, d), lambda i: (i, 0)), pl.BlockSpec((1, d), lambda i: (0, 0)),
                  pl.BlockSpec((tm, d), lambda i: (i, 0))],
        out_specs=[pl.BlockSpec((1, d), lambda i: (0, 0)), pl.BlockSpec((tm, d), lambda i: (i, 0)),
                   pl.BlockSpec((1, d), lambda i: (0, 0))],
        out_shape=[SDS((1, d), F32), SDS((s, d), F32), SDS((1, d), F32)], compiler_params=_cp("arbitrary"), name=name)(x, g, target)


def _me_and_peer():
    x, y, c = lax.axis_index("x"), lax.axis_index("y"), lax.axis_index("c")
    me = 4 * x + 2 * y + c

    def peer(k):
        px, py, pc = x ^ (k >> 2), y ^ ((k >> 1) & 1), c ^ (k & 1)
        return (px, py, pc), 4 * px + 2 * py + pc

    return me, peer


class _Comm:
    def __init__(self, kind, arrs):
        assert kind in ("gather", "exchange")
        self.kind, self.arrs, self.n = kind, list(arrs), len(arrs)
        anyspec = pl.BlockSpec(memory_space=pl.ANY)
        self.in_specs = [anyspec] * self.n
        self.out_specs = [anyspec] * self.n
        self.out_shape = [SDS(((NDEV,) + a.shape) if kind == "gather" else a.shape, a.dtype) for a in self.arrs]
        self.scratch = [pltpu.SemaphoreType.DMA((self.n, NDEV - 1)), pltpu.SemaphoreType.DMA((self.n, NDEV - 1)),
                        pltpu.SemaphoreType.DMA((self.n,))]

    def _copies(self, ins, outs, sems):
        send, recv, loc = sems
        me, peer = _me_and_peer()
        gather = self.kind == "gather"
        local = [pltpu.make_async_copy(ins[a] if gather else ins[a].at[me], outs[a].at[me], loc.at[a]) for a in range(self.n)]
        outgoing, incoming = [], []
        for k in range(1, NDEV):
            dev, pid = peer(k)
            for a in range(self.n):
                pair = dict(send_sem=send.at[a, k - 1], recv_sem=recv.at[a, k - 1], device_id=dev, device_id_type=MESH)
                outgoing.append(pltpu.make_async_remote_copy(src_ref=ins[a] if gather else ins[a].at[pid],
                                                             dst_ref=outs[a].at[me], **pair))
                incoming.append(pltpu.make_async_remote_copy(src_ref=ins[a] if gather else ins[a].at[me],
                                                             dst_ref=outs[a].at[pid], **pair))
        return local, outgoing, incoming

    def start(self, ins, outs, sems):
        local, outgoing, _ = self._copies(ins, outs, sems)
        for cp in local + outgoing:
            cp.start()

    def wait(self, ins, outs, sems):
        local, outgoing, incoming = self._copies(ins, outs, sems)
        for cp in incoming:
            cp.wait_recv()
        for cp in outgoing:
            cp.wait_send()
        for cp in local:
            cp.wait()


def _split_refs(refs, n_in, n_out, comm):
    c = comm.n if comm is not None else 0
    ins, cin = refs[:n_in], refs[n_in:n_in + c]
    outs, cout = refs[n_in + c:n_in + c + n_out], refs[n_in + c + n_out:n_in + 2 * c + n_out]
    rest = refs[n_in + 2 * c + n_out:]
    scratch, csem = (rest[:len(rest) - 3], rest[len(rest) - 3:]) if c else (rest, ())
    return ins, outs, scratch, (cin, cout, csem)


def _comm_call(kind, arrs, name):
    comm = _Comm(kind, arrs)

    def body(*refs):
        _, _, _, c = _split_refs(refs, 0, 0, comm)
        comm.start(*c)
        comm.wait(*c)

    return pl.pallas_call(body, in_specs=comm.in_specs, out_specs=comm.out_specs, out_shape=comm.out_shape,
                          scratch_shapes=comm.scratch, compiler_params=pltpu.CompilerParams(has_side_effects=True),
                          name=name)(*arrs)


def _all_gather(arrs, name):
    return _comm_call("gather", arrs, name)


def _exchange(arrs, name):
    return _comm_call("exchange", arrs, name)


def _sum_slots(parts, name):
    _, r, c = parts.shape
    tr = r if r <= 512 else 512

    def body(p_ref, o_ref):
        acc = p_ref[0].astype(F32)
        for q in range(1, NDEV):
            acc = acc + p_ref[q].astype(F32)
        o_ref[...] = acc

    return pl.pallas_call(
        body, grid=(r // tr,), in_specs=[pl.BlockSpec((NDEV, tr, c), lambda i: (0, i, 0))],
        out_specs=pl.BlockSpec((tr, c), lambda i: (i, 0)), out_shape=SDS((r, c), F32),
        compiler_params=_cp("parallel"), name=name)(parts)


def _adamw(g, w, m, v, name):
    r, c = w.shape
    parts = g.ndim == 3
    tr = r
    for cand in (512, 256, 128, 64, 32, 16, 8):
        if r > cand and r % cand == 0 and cand * c * 4 <= 2 * 1024 * 1024:
            tr = cand
            break
    bc1 = 1.0 / (1.0 - ADAM_B1 ** ADAM_STEP)
    bc2 = 1.0 / (1.0 - ADAM_B2 ** ADAM_STEP)

    def body(g_ref, w_ref, m_ref, v_ref, go_ref, d_ref, mo_ref, vo_ref):
        if parts:
            gv = g_ref[0].astype(F32)
            for q in range(1, NDEV):
                gv = gv + g_ref[q].astype(F32)
        else:
            gv = g_ref[...]
        mn = ADAM_B1 * m_ref[...] + (1.0 - ADAM_B1) * gv
        vn = ADAM_B2 * v_ref[...] + (1.0 - ADAM_B2) * (gv * gv)
        go_ref[...] = gv
        mo_ref[...] = mn
        vo_ref[...] = vn
        d_ref[...] = -ADAM_LR * ((mn * bc1) / (jnp.sqrt(vn * bc2) + ADAM_EPS) + ADAM_WD * w_ref[...])

    spec = pl.BlockSpec((tr, c), lambda i: (i, 0))
    gspec = pl.BlockSpec((NDEV, tr, c), lambda i: (0, i, 0)) if parts else spec
    return pl.pallas_call(
        body, grid=(r // tr,), in_specs=[gspec, spec, spec, spec], out_specs=[spec] * 4,
        out_shape=[SDS((r, c), F32)] * 4, compiler_params=_cp("parallel"), name=name)(g, w, m, v)


def _pad_in_cols(w):
    r = w.shape[0]
    zeros = lambda n: jnp.zeros((r, n), w.dtype)
    return jnp.concatenate([w[:, :2304], w[:, 2308:2692], w[:, 2304:2308], zeros(28), w[:, 2692:2724], zeros(64)], axis=1)


def _unpad_in_cols(w):
    return jnp.concatenate([w[..., :2304], w[..., 2688:2692], w[..., 2304:2688], w[..., 2720:2752]], axis=-1)


def _pad_uq(w):
    return jnp.pad(w.reshape(256, N_HEADS, 96), ((0, 0), (0, 0), (0, 32))).reshape(256, 512)


def _unpad_uq(w):
    return w.reshape(256, N_HEADS, 128)[:, :, :96].reshape(256, 384)


def _split_ukv(w):
    r = w.reshape(128, N_HEADS, 128)
    return jnp.pad(r[:, :, :64], ((0, 0), (0, 0), (0, 64))).reshape(128, 512), r[:, :, 64:].reshape(128, 256)


def _join_ukv(dk, dv):
    return jnp.concatenate([dk.reshape(128, N_HEADS, 128)[:, :, :64], dv.reshape(128, N_HEADS, 64)], axis=-1).reshape(128, 512)


def _cols_to_full(g):
    return jnp.transpose(g, (1, 0, 2)).reshape(g.shape[1], NDEV * g.shape[2])


def kernel(x, g_mix_norm, w_in, b_forget, g_sgu, w_spatial, b_spatial, g_mla_q, w_uq, g_mla_kv, w_ukv, g_group_out, w_out, g_ffn_norm, w_up, w_down, g_final, loss_target, m_g_mix_norm, m_w_in, m_b_forget, m_g_sgu, m_w_spatial, m_b_spatial, m_g_mla_q, m_w_uq, m_g_mla_kv, m_w_ukv, m_g_group_out, m_w_out, m_g_ffn_norm, m_w_up, m_w_down, m_g_final, v_g_mix_norm, v_w_in, v_b_forget, v_g_sgu, v_w_spatial, v_b_spatial, v_g_mla_q, v_w_uq, v_g_mla_kv, v_w_ukv, v_g_group_out, v_w_out, v_g_ffn_norm, v_w_up, v_w_down, v_g_final):
    depth = w_in.shape[0]
    s, d = x.shape[1], x.shape[2]
    x0 = x.reshape(s, d)
    target = loss_target.reshape(s, d)
    tb = _tables(s)
    me = 4 * lax.axis_index("x") + 2 * lax.axis_index("y") + lax.axis_index("c")

    assert depth == 2
    shards = {}
    for l in range(depth):
        shards.update({(l, "w_in"): _pad_in_cols(w_in[l]).astype(_WIRE), (l, "w_out"): w_out[l].astype(_WIRE),
                       (l, "w_up"): w_up[l].astype(_WIRE), (l, "w_down"): w_down[l].astype(_WIRE),
                       (l, "w_uq"): w_uq[l].astype(_WIRE), (l, "w_ukv"): w_ukv[l].astype(_WIRE)})
    wts = _ShardedWeights(shards)
    wts.full[(0, "w_in")] = _all_gather([shards[(0, "w_in")]], "gather_w_in0")[0]

    row = lambda a: a.reshape(1, -1)

    def small(l):
        bf = jnp.pad(b_forget[l].reshape(1, N_HEADS), ((0, 0), (0, 128 - N_HEADS)))
        bt = jnp.pad(b_spatial[l].T, ((0, 0), (0, 128 - N_HEADS)))
        return dict(g_mix=row(g_mix_norm[l]), g_sgu=row(g_sgu[l]), w_s=w_spatial[l], b_t=bt, b_f=bf, gq=row(g_mla_q[l]),
                    gkv=row(g_mla_kv[l]), g_go=row(g_group_out[l]), g_ffn=row(g_ffn_norm[l]))

    smalls = [small(l) for l in range(depth)]
    lrow, dx, sm, dg_final = _local_step(x0, target, wts, smalls, row(g_final), tb)
    loss = lax.psum(jnp.sum(lrow), AXES)
    grad_x = dx.reshape(1, s, d)
    wts.recv[(0, "w_in")] = _exchange([wts.grads[(0, "w_in")]], "scatter_dw_in0")[0]
    return _reduce_and_update(loss, grad_x, wts.recv, sm, dg_final, me, dict(
        g_mix_norm=(g_mix_norm, m_g_mix_norm, v_g_mix_norm), w_in=(w_in, m_w_in, v_w_in),
        b_forget=(b_forget, m_b_forget, v_b_forget), g_sgu=(g_sgu, m_g_sgu, v_g_sgu),
        w_spatial=(w_spatial, m_w_spatial, v_w_spatial), b_spatial=(b_spatial, m_b_spatial, v_b_spatial),
        g_mla_q=(g_mla_q, m_g_mla_q, v_g_mla_q), w_uq=(w_uq, m_w_uq, v_w_uq), g_mla_kv=(g_mla_kv, m_g_mla_kv, v_g_mla_kv),
        w_ukv=(w_ukv, m_w_ukv, v_w_ukv), g_group_out=(g_group_out, m_g_group_out, v_g_group_out),
        w_out=(w_out, m_w_out, v_w_out), g_ffn_norm=(g_ffn_norm, m_g_ffn_norm, v_g_ffn_norm), w_up=(w_up, m_w_up, v_w_up),
        w_down=(w_down, m_w_down, v_w_down), g_final=(g_final, m_g_final, v_g_final)))


_GATHER_AT = {
    "fox_attn0": [(0, "w_uq"), (0, "w_ukv"), (0, "w_out"), (0, "w_up")],
    "mla_attn0": [(0, "w_down"), (1, "w_in")],
    "ffn_fwd0": [(1, "w_uq"), (1, "w_ukv"), (1, "w_out"), (1, "w_up")],
    "fox_attn1": [(1, "w_down")],
}
_SCATTER_AT = {
    "fox_attn_bwd1": [(1, "w_down"), (1, "w_up")],
    "mla_attn_bwd1": [(1, "w_out")],
    "ffn_bwd0": [(1, "w_in")],
    "fox_attn_bwd0": [(0, "w_down"), (0, "w_up")],
    "mla_attn_bwd0": [(0, "w_out")],
}


class _FullWeights:
    def __init__(self, per_layer):
        self.per_layer, self.grads = per_layer, {}

    def get(self, l, name):
        return self.per_layer[l][name]

    def comm(self, host):
        return None

    def done(self, host, results):
        pass

    def grad(self, l, name, blocks):
        self.grads[(l, name)] = blocks


class _ShardedWeights(_FullWeights):
    def __init__(self, shards):
        self.shards, self.full, self.grads, self.recv = shards, {}, {}, {}

    def get(self, l, name):
        if name in ("wk", "wv"):
            return _split_ukv(_cols_to_full(self.full[(l, "w_ukv")]))[0 if name == "wk" else 1]
        if name == "wq":
            return _pad_uq(_cols_to_full(self.full[(l, "w_uq")]))
        g = self.full[(l, name)]
        return g if name == "w_up" else g.reshape(NDEV * g.shape[1], g.shape[2])

    def comm(self, host):
        if host in _GATHER_AT:
            return _Comm("gather", [self.shards[k] for k in _GATHER_AT[host]])
        if host in _SCATTER_AT:
            return _Comm("exchange", [self.grads[k] for k in _SCATTER_AT[host]])
        return None

    def done(self, host, results):
        if host in _GATHER_AT:
            self.full.update(zip(_GATHER_AT[host], results))
        if host in _SCATTER_AT:
            self.recv.update(zip(_SCATTER_AT[host], results))


def _local_step(x0, target, wts, smalls, g_final, tb):
    depth = len(smalls)
    s, d = x0.shape
    saved = []
    xl = x0
    for l in range(depth):
        p = smalls[l]
        z, h = _norm_matmul(xl, p["g_mix"], wts.get(l, "w_in"), f"in_proj{l}")
        ya = _sgu_fwd(z, p["g_sgu"], p["w_s"], p["b_t"], f"sgu_fwd{l}")
        yb, ret, states = _ret_fwd(z, tb, f"ret_fwd{l}")
        cum = _fox_prep(z, p["b_f"], f"fox_prep{l}")
        kc, vc, vtc = _kv_prep(z, 7, 8, f"fox_kv{l}")
        yc, lse_c, got = _attn_fwd(z, 6, HEAD_DIM, kc, vtc, HEAD_DIM ** -0.5, cum, f"fox_attn{l}", wts.comm(f"fox_attn{l}"))
        wts.done(f"fox_attn{l}", got)
        wq, wk, wv = wts.get(l, "wq"), wts.get(l, "wk"), wts.get(l, "wv")
        qd, kd, vd, vtd, cqn, ckvn = _mla_prep(z, p["gq"], p["gkv"], wq, wk, wv, tb, f"mla_prep{l}")
        yd, lse_d, got = _attn_fwd(qd, 0, 128, kd, vtd, _SCALE_D, None, f"mla_attn{l}", wts.comm(f"mla_attn{l}"))
        wts.done(f"mla_attn{l}", got)
        ys = (ya, yb, yc, yd)
        x1, yn = _out_proj(ys, p["g_go"], wts.get(l, "w_out"), xl, f"out_proj{l}")
        x2, u, h2, got = _ffn_fwd(x1, p["g_ffn"], wts.get(l, "w_up"), wts.get(l, "w_down"), f"ffn_fwd{l}", wts.comm(f"ffn_fwd{l}"))
        wts.done(f"ffn_fwd{l}", got)
        saved.append(dict(x=xl, z=z, h=h, ys=ys, ret=ret, states=states, cum=cum, lse_c=lse_c, kc=kc, vc=vc, qd=qd, kd=kd, vd=vd,
                          cqn=cqn, ckvn=ckvn, lse_d=lse_d, x1=x1, yn=yn, u=u, h2=h2, wq=wq, wk=wk, wv=wv))
        xl = x2

    lrow, dx, dg_final = _loss_head(xl, g_final, target, "loss_head")

    sm = [None] * depth
    for l in reversed(range(depth)):
        p, a = smalls[l], saved[l]
        dx1, du, dg_ffn, got = _ffn_bwd(dx, a["x1"], a["u"], p["g_ffn"], wts.get(l, "w_up"), wts.get(l, "w_down"), f"ffn_bwd{l}",
                                        wts.comm(f"ffn_bwd{l}"))
        wts.done(f"ffn_bwd{l}", got)
        dw_down = _mm_tn(a["u"], dx, f"dw_down{l}", a_fn=lambda t: jnp.square(jnp.maximum(t, 0.0)), out_dtype=_WIRE)
        wts.grad(l, "w_down", dw_down.reshape(NDEV, dw_down.shape[0] // NDEV, d))
        wts.grad(l, "w_up", _mm_tn(a["h2"], du, f"dw_up{l}", blocked=True, out_dtype=_WIRE))
        dya, dyb, dyc, dyd, dg_go = _out_proj_bwd(dx1, wts.get(l, "w_out"), a["ys"], p["g_go"], f"out_proj_bwd{l}")
        wts.grad(l, "w_out", _mm_tn(a["yn"], dx1, f"dw_out{l}", out_dtype=_WIRE).reshape(NDEV, d // NDEV, d))
        dz_a, dg_sgu, dw_s, db_t = _sgu_bwd(dya, a["z"], p["g_sgu"], p["w_s"], p["b_t"], f"sgu_bwd{l}")
        dz_b = _ret_bwd(dyb, a["z"], a["ret"], a["states"], tb, f"ret_bwd{l}")
        qt, dot, dl = _attn_bwd_prep(a["z"], 6, HEAD_DIM, a["ys"][2], dyc, f"fox_bwd_prep{l}")
        dqt_c, dk_c, dv_c, dck, dcq, got = _attn_bwd(a["kc"], a["vc"], qt, dot, a["lse_c"], dl, HEAD_DIM,
                                                     HEAD_DIM ** -0.5, a["cum"], f"fox_attn_bwd{l}", _MXU,
                                                     wts.comm(f"fox_attn_bwd{l}"))
        wts.done(f"fox_attn_bwd{l}", got)
        dq_c = _untranspose(dqt_c, _MXU, f"fox_dq{l}")
        qt, dot, dl = _attn_bwd_prep(a["qd"], 0, 128, a["ys"][3], dyd, f"mla_bwd_prep{l}")
        dqt_d, dk_d, dv_d, got = _attn_bwd(a["kd"], a["vd"], qt, dot, a["lse_d"], dl, 128, _SCALE_D, None,
                                           f"mla_attn_bwd{l}", F32, wts.comm(f"mla_attn_bwd{l}"))
        wts.done(f"mla_attn_bwd{l}", got)
        dq_d = _untranspose(dqt_d, F32, f"mla_dq{l}")
        dz_cq, dz_ckv, dkr, dwq, dwk, dwv, dgq, dgkv = _mla_prep_bwd(dq_d, dk_d, dv_d, a["z"], a["cqn"], a["ckvn"], p["gq"],
                                                                     p["gkv"], a["wq"], a["wk"], a["wv"], tb, f"mla_prep_bwd{l}")
        dz_misc, db_f = _fox_post(dcq, dck, a["z"], p["b_f"], dkr, f"fox_post{l}")
        dz = jnp.concatenate([dz_a, dz_b, dq_c, dk_c, dv_c, dz_cq, dz_ckv, dz_misc], axis=1)
        dx, dg_mix = _in_proj_bwd(dz, wts.get(l, "w_in"), a["x"], p["g_mix"], dx1, f"in_proj_bwd{l}")
        wts.grad(l, "w_in", _unpad_in_cols(_mm_tn(a["h"], dz, f"dw_in{l}", out_dtype=_WIRE)).reshape(NDEV, d // NDEV, N_IN))
        sm[l] = [dg_mix, dg_go, dg_ffn, dg_sgu, dw_s, db_t[:, :N_HEADS].T, db_f[0, :N_HEADS], dgq, dgkv, _unpad_uq(dwq),
                 _join_ukv(dwk, dwv)]
    return lrow, dx, sm, dg_final


def _reduce_and_update(loss, grad_x, recv, sm, dg_final, me, given):
    depth = len(sm)
    pieces = [t for l in range(depth) for t in sm[l]] + [dg_final]
    flat = jnp.concatenate([t.reshape(-1) for t in pieces])
    n_flat = flat.shape[0]
    unit = NDEV * 8 * 128
    n_pad = -(-n_flat // unit) * unit
    packed = jnp.pad(flat, (0, n_pad - n_flat)).reshape(NDEV, n_pad // (NDEV * 128), 128)
    red = _sum_slots(_exchange([packed], "scatter_small")[0], "sum_small")
    full = _all_gather([red], "gather_small")[0].reshape(-1)
    offs = np.cumsum([0] + [int(np.prod(t.shape)) for t in pieces])
    red_pieces = [full[int(offs[i]):int(offs[i + 1])].reshape(pieces[i].shape) for i in range(len(pieces))]
    per = len(sm[0])
    stack = lambda i: jnp.stack([red_pieces[l * per + i] for l in range(depth)])
    g_small = dict(g_mix_norm=stack(0), g_group_out=stack(1), g_ffn_norm=stack(2), g_sgu=stack(3), w_spatial=stack(4),
                   b_spatial=stack(5), b_forget=stack(6), g_mla_q=stack(7), g_mla_kv=stack(8), g_final=red_pieces[-1])
    cq, ckv = given["w_uq"][0].shape[2], given["w_ukv"][0].shape[2]
    g_small["w_uq"] = lax.dynamic_slice_in_dim(stack(9), me * cq, cq, axis=2)
    g_small["w_ukv"] = lax.dynamic_slice_in_dim(stack(10), me * ckv, ckv, axis=2)

    names = list(given)
    outs = {}
    for nme in names:
        wv_, mv_, vv_ = given[nme]
        shape = wv_.shape
        if nme in ("w_in", "w_out", "w_up", "w_down"):
            res = []
            for l in range(depth):
                parts = recv[(l, nme)]
                two = lambda t: t[l].reshape(-1, shape[-1])
                res.append(_adamw(parts, two(wv_), two(mv_), two(vv_), f"adamw_{nme}{l}"))
            outs[nme] = [jnp.stack([res[l][i] for l in range(depth)]).reshape(shape) for i in range(4)]
        else:
            two = lambda t: t.reshape(-1, shape[-1]) if t.ndim > 1 else t.reshape(1, -1)
            res = _adamw(two(g_small[nme]), two(wv_), two(mv_), two(vv_), f"adamw_{nme}")
            outs[nme] = [r.reshape(shape) for r in res]
    return (loss, grad_x, *[outs[n][0] for n in names], *[outs[n][1] for n in names], *[outs[n][2] for n in names],
            *[outs[n][3] for n in names])
```

```python
import functools

import jax
import jax.numpy as jnp
import numpy as np
from jax import lax
from jax.experimental import pallas as pl
from jax.experimental.pallas import tpu as pltpu

F32 = jnp.float32
_MXU = jnp.bfloat16
_WIRE = jnp.bfloat16
EPS = 1e-6
NDEV = 8
AXES = ("x", "y", "c")
MESH = pl.DeviceIdType.MESH

N_HEADS = 4
HEAD_DIM = 64
GROUP = 256
CHUNK = 128
NZ = 2816
N_IN = 2724
MISC_F, MISC_KR = 0, 32
VMEM_LIMIT = 56 * 1024 * 1024

ADAM_LR, ADAM_B1, ADAM_B2, ADAM_EPS, ADAM_WD, ADAM_STEP = 0.001, 0.9, 0.999, 1e-08, 0.01, 10

SDS = jax.ShapeDtypeStruct


def _cp(*sem):
    return pltpu.CompilerParams(dimension_semantics=sem, vmem_limit_bytes=VMEM_LIMIT)


def _dot(a, b):
    return jnp.dot(a.astype(_MXU), b.astype(_MXU), preferred_element_type=F32)


def _dot_nt(a, b):
    return lax.dot_general(a.astype(_MXU), b.astype(_MXU), (((1,), (1,)), ((), ())), preferred_element_type=F32)


def _dot_tn(a, b):
    return lax.dot_general(a.astype(_MXU), b.astype(_MXU), (((0,), (0,)), ((), ())), preferred_element_type=F32)


def _dot_exact(a, b, dims=(((1,), (0,)), ((), ()))):
    return lax.dot_general(a, b, dims, precision=lax.Precision.HIGHEST, preferred_element_type=F32)


def _rms(x, g):
    return x * lax.rsqrt(jnp.mean(x * x, axis=-1, keepdims=True) + EPS) * g


def _rms_bwd(x, g, dy):
    xh = x * lax.rsqrt(jnp.mean(x * x, axis=-1, keepdims=True) + EPS)
    dxh = dy * g
    r = lax.rsqrt(jnp.mean(x * x, axis=-1, keepdims=True) + EPS)
    dx = r * (dxh - xh * jnp.mean(dxh * xh, axis=-1, keepdims=True))
    return dx, jnp.sum(dy * xh, axis=0, keepdims=True)


def _standardize(t):
    mu = jnp.mean(t, axis=-1, keepdims=True)
    tc = t - mu
    rs = lax.rsqrt(jnp.mean(tc * tc, axis=-1, keepdims=True) + EPS)
    return tc * rs, rs


def _standardize_bwd(yh, rs, dy):
    return rs * (dy - jnp.mean(dy, axis=-1, keepdims=True) - yh * jnp.mean(dy * yh, axis=-1, keepdims=True))


_GELU_C = 0.7978845608028654


def _gelu(x):
    return 0.5 * x * (1.0 + jnp.tanh(_GELU_C * (x + 0.044715 * x * x * x)))


def _gelu_grad(x):
    t = jnp.tanh(_GELU_C * (x + 0.044715 * x * x * x))
    return 0.5 * (1.0 + t) + 0.5 * x * (1.0 - t * t) * _GELU_C * (1.0 + 3 * 0.044715 * x * x)


def _sigmoid(x):
    return 1.0 / (1.0 + jnp.exp(-x))


def _swap_half(t, half):
    n = t.shape[-1]
    lane = lax.broadcasted_iota(jnp.int32, t.shape, t.ndim - 1)
    return jnp.where((lane % (2 * half)) < half, pltpu.roll(t, n - half, t.ndim - 1), pltpu.roll(t, half, t.ndim - 1))


def _rope(t, cos, sin, half):
    return t * cos + _swap_half(t, half) * sin


def _rope_bwd(d, cos, sin, half):
    return d * cos - _swap_half(d, half) * sin


def _tables(s):
    pos = jnp.arange(s, dtype=F32)[:, None]

    def cs(half):
        inv = jnp.power(10000.0, -jnp.arange(half, dtype=F32) / half)
        ang = pos * inv[None, :]
        return jnp.cos(ang), jnp.sin(ang)

    c32, s32 = cs(32)
    c16, s16 = cs(16)
    z = lambda w: jnp.zeros((s, w), F32)
    o = lambda w: jnp.ones((s, w), F32)
    t = {}
    t["b_cos"] = jnp.tile(jnp.concatenate([c32, c32], 1), (1, 4))
    t["b_sin"] = jnp.tile(jnp.concatenate([-s32, s32], 1), (1, 4))
    t["q_cos"] = jnp.tile(jnp.concatenate([o(64), c16, c16, z(32)], 1), (1, 4))
    t["q_sin"] = jnp.tile(jnp.concatenate([z(64), -s16, s16, z(32)], 1), (1, 4))
    t["k_cos"] = jnp.concatenate([z(32), c16, c16, z(64)], 1)
    t["k_sin"] = jnp.concatenate([z(32), -s16, s16, z(64)], 1)
    lg = jnp.log1p(-jnp.exp2(-5.0 - jnp.arange(N_HEADS, dtype=F32)))
    j = jnp.arange(CHUNK, dtype=F32)
    rel = j[:, None] - j[None, :]
    t["decay"] = jnp.where(rel[None] >= 0, jnp.exp(jnp.maximum(rel, 0.0)[None] * lg[:, None, None]), 0.0)

    def rows(e):
        return jnp.repeat(e.T, HEAD_DIM, axis=1)

    t["qw"] = rows(jnp.exp((j + 1.0)[None, :] * lg[:, None]))
    t["kw"] = rows(jnp.exp((CHUNK - 1 - j)[None, :] * lg[:, None]))
    t["kw2"] = rows(jnp.exp((CHUNK - j)[None, :] * lg[:, None]))
    t["qw0"] = rows(jnp.exp(j[None, :] * lg[:, None]))
    t["cd"] = jnp.repeat(jnp.exp(CHUNK * lg), HEAD_DIM)[None, :]
    e = np.zeros((128, 512), np.float32)
    for h in range(N_HEADS):
        for r in range(32):
            e[MISC_KR + r, 128 * h + 64 + r] = 1.0
    t["place"] = jnp.asarray(e)
    return t


def _norm_matmul(x, g, w, name, comm=None):
    s, d = x.shape
    n = w.shape[1]
    tm, tn = min(512, s), 256
    ni = s // tm

    def body(*refs):
        (x_ref, g_ref, w_ref), (z_ref, h_ref), _, cc = _split_refs(refs, 3, 2, comm)
        i = pl.program_id(0)
        if comm is not None:
            @pl.when(i == 0)
            def _():
                comm.start(*cc)

        h = _rms(x_ref[...], g_ref[...]).astype(h_ref.dtype)
        h_ref[...] = h
        for j in range(n // tn):
            z_ref[:, tn * j:tn * (j + 1)] = jnp.dot(h, w_ref[:, tn * j:tn * (j + 1)], preferred_element_type=F32)
        if comm is not None:
            @pl.when(i == ni - 1)
            def _():
                comm.wait(*cc)

    in_specs = [pl.BlockSpec((tm, d), lambda i: (i, 0)), pl.BlockSpec((1, d), lambda i: (0, 0)),
                pl.BlockSpec((d, n), lambda i: (0, 0))]
    out_specs = [pl.BlockSpec((tm, n), lambda i: (i, 0)), pl.BlockSpec((tm, d), lambda i: (i, 0))]
    out_shape = [SDS((s, n), F32), SDS((s, d), _MXU)]
    return _call_with_comm(body, (ni,), in_specs, out_specs, out_shape, [], [x, g, w], comm, ("arbitrary",), name)


def _mm_tn(a, b, name, *, a_fn=None, blocked=False, out_dtype=F32):
    k, m = a.shape
    n = b.shape[1]
    tm, tk = min(1024, m), min(512, k)
    tn = next(t for t in (1408, 1024, 512, 256, 128) if n % t == 0)
    assert m % tm == 0 and k % tk == 0
    nk = k // tk

    def body(a_ref, b_ref, o_ref, acc):
        kk = pl.program_id(2)

        @pl.when(kk == 0)
        def _():
            acc[...] = jnp.zeros_like(acc)

        av = a_ref[...]
        if a_fn is not None:
            av = a_fn(av.astype(F32))
        acc[...] += _dot_tn(av, b_ref[...])

        @pl.when(kk == nk - 1)
        def _():
            if blocked:
                for c in range(tn // 512):
                    o_ref[c] = acc[:, 512 * c:512 * (c + 1)].astype(o_ref.dtype)
            else:
                o_ref[...] = acc[...].astype(o_ref.dtype)

    if blocked:
        assert tn % 512 == 0
        out_spec = pl.BlockSpec((tn // 512, tm, 512), lambda i, j, kk: (j, i, 0))
        out_shape = SDS((n // 512, m, 512), out_dtype)
    else:
        out_spec = pl.BlockSpec((tm, tn), lambda i, j, kk: (i, j))
        out_shape = SDS((m, n), out_dtype)
    return pl.pallas_call(
        body, grid=(m // tm, n // tn, nk),
        in_specs=[pl.BlockSpec((tk, tm), lambda i, j, kk: (kk, i)), pl.BlockSpec((tk, tn), lambda i, j, kk: (kk, j))],
        out_specs=out_spec, out_shape=out_shape, scratch_shapes=[pltpu.VMEM((tm, tn), F32)],
        compiler_params=_cp("parallel", "parallel", "arbitrary"), name=name)(a, b)


def _sgu_parts(u_pre, v_pre, gain):
    u = _gelu(u_pre)
    v = _gelu(v_pre)
    vh, rs, vg = [], [], []
    for h in range(N_HEADS):
        sl = slice(HEAD_DIM * h, HEAD_DIM * (h + 1))
        a, r = _standardize(v[:, sl])
        vh.append(a)
        rs.append(r)
        vg.append(a * gain[:, sl])
    return u, vh, rs, vg


def _tril(w):
    r = lax.broadcasted_iota(jnp.int32, w.shape, 0)
    c = lax.broadcasted_iota(jnp.int32, w.shape, 1)
    return jnp.where(r >= c, w, 0.0)


def _sgu_fwd(z, gain, w_s, b_t, name):
    s = z.shape[0]
    tm = min(512, s)

    def body(u_ref, v_ref, g_ref, w_ref, b_ref, y_ref):
        u, _, _, vg = _sgu_parts(u_ref[...], v_ref[...], g_ref[...])
        for h in range(N_HEADS):
            sl = slice(HEAD_DIM * h, HEAD_DIM * (h + 1))
            wc = _tril(w_ref[h])
            for c in range(tm // CHUNK):
                r = slice(CHUNK * c, CHUNK * (c + 1))
                mixed = _dot(wc, vg[h][r]) + b_ref[:, h:h + 1]
                y_ref[r, sl] = u[r, sl] * mixed

    return pl.pallas_call(
        body, grid=(s // tm,),
        in_specs=[pl.BlockSpec((tm, GROUP), lambda i: (i, 0)), pl.BlockSpec((tm, GROUP), lambda i: (i, 1)),
                  pl.BlockSpec((1, GROUP), lambda i: (0, 0)), pl.BlockSpec((N_HEADS, CHUNK, CHUNK), lambda i: (0, 0, 0)),
                  pl.BlockSpec((CHUNK, 128), lambda i: (0, 0))],
        out_specs=pl.BlockSpec((tm, GROUP), lambda i: (i, 0)), out_shape=SDS((s, GROUP), F32),
        compiler_params=_cp("parallel"), name=name)(z, z, gain, w_s, b_t)


def _sgu_bwd(dy, z, gain, w_s, b_t, name):
    s = z.shape[0]
    tm = min(512, s)

    def body(dy_ref, u_ref, v_ref, g_ref, w_ref, b_ref, dz_ref, dg_ref, dw_ref, db_ref):
        @pl.when(pl.program_id(0) == 0)
        def _():
            dg_ref[...] = jnp.zeros_like(dg_ref)
            dw_ref[...] = jnp.zeros_like(dw_ref)
            db_ref[...] = jnp.zeros_like(db_ref)

        u_pre, v_pre, gain_v = u_ref[...], v_ref[...], g_ref[...]
        u, vh, rs, vg = _sgu_parts(u_pre, v_pre, gain_v)
        dyv = dy_ref[...]
        gu = _gelu_grad(u_pre)
        gv = _gelu_grad(v_pre)
        for h in range(N_HEADS):
            sl = slice(HEAD_DIM * h, HEAD_DIM * (h + 1))
            wc = _tril(w_ref[h])
            dwh = jnp.zeros((CHUNK, CHUNK), F32)
            dbh = jnp.zeros((CHUNK, 1), F32)
            dgh = jnp.zeros((1, HEAD_DIM), F32)
            for c in range(tm // CHUNK):
                r = slice(CHUNK * c, CHUNK * (c + 1))
                mixed = _dot(wc, vg[h][r]) + b_ref[:, h:h + 1]
                dz_ref[r, sl] = (dyv[r, sl] * mixed * gu[r, sl]).astype(dz_ref.dtype)
                dm = dyv[r, sl] * u[r, sl]
                dwh += _dot_nt(dm, vg[h][r])
                dbh += jnp.sum(dm, axis=1, keepdims=True)
                dvg = _dot_tn(wc, dm)
                dgh += jnp.sum(dvg * vh[h][r], axis=0, keepdims=True)
                dv = _standardize_bwd(vh[h][r], rs[h][r], dvg * gain_v[:, sl])
                dz_ref[r, GROUP + HEAD_DIM * h:GROUP + HEAD_DIM * (h + 1)] = (dv * gv[r, sl]).astype(dz_ref.dtype)
            dw_ref[h] += _tril(dwh)
            db_ref[:, h:h + 1] += dbh
            dg_ref[:, sl] += dgh

    return pl.pallas_call(
        body, grid=(s // tm,),
        in_specs=[pl.BlockSpec((tm, GROUP), lambda i: (i, 0)),
                  pl.BlockSpec((tm, GROUP), lambda i: (i, 0)), pl.BlockSpec((tm, GROUP), lambda i: (i, 1)),
                  pl.BlockSpec((1, GROUP), lambda i: (0, 0)), pl.BlockSpec((N_HEADS, CHUNK, CHUNK), lambda i: (0, 0, 0)),
                  pl.BlockSpec((CHUNK, 128), lambda i: (0, 0))],
        out_specs=[pl.BlockSpec((tm, 2 * GROUP), lambda i: (i, 0)), pl.BlockSpec((1, GROUP), lambda i: (0, 0)),
                   pl.BlockSpec((N_HEADS, CHUNK, CHUNK), lambda i: (0, 0, 0)), pl.BlockSpec((CHUNK, 128), lambda i: (0, 0))],
        out_shape=[SDS((s, 2 * GROUP), _MXU), SDS((1, GROUP), F32), SDS((N_HEADS, CHUNK, CHUNK), F32), SDS((CHUNK, 128), F32)],
        compiler_params=_cp("arbitrary"), name=name)(dy, z, z, gain, w_s, b_t)


_SCALE_B = HEAD_DIM ** -0.5


def _ret_fwd(z, tb, name):
    s = z.shape[0]
    nc = s // CHUNK
    row = lambda col: pl.BlockSpec((CHUNK, GROUP), lambda n, col=col: (n, col))
    const = lambda shape: pl.BlockSpec(shape, lambda n: (0,) * len(shape))

    def body(q_ref, k_ref, v_ref, g_ref, cos_ref, sin_ref, dec_ref, qw_ref, kw_ref, cd_ref, y_ref, o_ref, st_ref, state):
        @pl.when(pl.program_id(0) == 0)
        def _():
            state[...] = jnp.zeros_like(state)

        q = _rope(q_ref[...], cos_ref[...], sin_ref[...], 32)
        k = _rope(k_ref[...], cos_ref[...], sin_ref[...], 32) * _SCALE_B
        v = v_ref[...]
        g = g_ref[...]
        st_ref[0] = state[...]
        qs = q * qw_ref[...]
        ks = k * kw_ref[...]
        for h in range(N_HEADS):
            sl = slice(HEAD_DIM * h, HEAD_DIM * (h + 1))
            sc = _dot_nt(q[:, sl], k[:, sl]) * dec_ref[h]
            o = _dot(sc, v[:, sl]) + _dot(qs[:, sl], state[:, sl])
            o_ref[:, sl] = o
            yh, _ = _standardize(o)
            gh = g[:, sl]
            y_ref[:, sl] = gh * _sigmoid(gh) * yh
            state[:, sl] = cd_ref[:, sl] * state[:, sl] + _dot_tn(ks[:, sl], v[:, sl])

    return pl.pallas_call(
        body, grid=(nc,),
        in_specs=[row(2), row(3), row(4), row(5), pl.BlockSpec((CHUNK, GROUP), lambda n: (n, 0)),
                  pl.BlockSpec((CHUNK, GROUP), lambda n: (n, 0)), const((N_HEADS, CHUNK, CHUNK)),
                  const((CHUNK, GROUP)), const((CHUNK, GROUP)), const((1, GROUP))],
        out_specs=[pl.BlockSpec((CHUNK, GROUP), lambda n: (n, 0)), pl.BlockSpec((CHUNK, GROUP), lambda n: (n, 0)),
                   pl.BlockSpec((1, HEAD_DIM, GROUP), lambda n: (n, 0, 0))],
        out_shape=[SDS((s, GROUP), F32), SDS((s, GROUP), F32), SDS((nc, HEAD_DIM, GROUP), F32)],
        scratch_shapes=[pltpu.VMEM((HEAD_DIM, GROUP), F32)],
        compiler_params=_cp("arbitrary"), name=name)(z, z, z, z, tb["b_cos"], tb["b_sin"], tb["decay"], tb["qw"], tb["kw"], tb["cd"])


def _ret_bwd(dy, z, o_pre, states, tb, name):
    s = z.shape[0]
    nc = s // CHUNK
    rev = lambda col: pl.BlockSpec((CHUNK, GROUP), lambda n, col=col: (nc - 1 - n, col))
    const = lambda shape: pl.BlockSpec(shape, lambda n: (0,) * len(shape))

    def body(dy_ref, q_ref, k_ref, v_ref, g_ref, o_ref, st_ref, cos_ref, sin_ref, dec_ref, qw_ref, kw2_ref, qw0_ref, cd_ref,
             dz_ref, rstate):
        @pl.when(pl.program_id(0) == 0)
        def _():
            rstate[...] = jnp.zeros_like(rstate)

        cos, sin = cos_ref[...], sin_ref[...]
        q = _rope(q_ref[...], cos, sin, 32)
        k = _rope(k_ref[...], cos, sin, 32) * _SCALE_B
        v = v_ref[...]
        g = g_ref[...]
        dyv = dy_ref[...]
        sg = _sigmoid(g)
        silu = g * sg
        dos, dgs = [], []
        for h in range(N_HEADS):
            sl = slice(HEAD_DIM * h, HEAD_DIM * (h + 1))
            yh, rs = _standardize(o_ref[:, sl])
            dgs.append(dyv[:, sl] * yh * (sg[:, sl] * (1.0 + g[:, sl] * (1.0 - sg[:, sl]))))
            dos.append(_standardize_bwd(yh, rs, dyv[:, sl] * silu[:, sl]))
        do = jnp.concatenate(dos, axis=1)
        dow = do * qw_ref[...]
        vw = v * kw2_ref[...]
        kw = k * kw2_ref[...]
        q0 = q * qw0_ref[...]
        dqs, dks = [], []
        for h in range(N_HEADS):
            sl = slice(HEAD_DIM * h, HEAD_DIM * (h + 1))
            dec = dec_ref[h]
            p = _dot_nt(q[:, sl], k[:, sl]) * dec
            dp = _dot_nt(do[:, sl], v[:, sl]) * dec
            sn = st_ref[0][:, sl]
            rr = rstate[:, sl]
            dqs.append(_dot(dp, k[:, sl]) + _dot_nt(dow[:, sl], sn))
            dks.append(_dot_tn(dp, q[:, sl]) + _dot_nt(vw[:, sl], rr))
            dv = _dot_tn(p, do[:, sl]) + _dot(kw[:, sl], rr)
            dz_ref[:, 2 * GROUP + HEAD_DIM * h:2 * GROUP + HEAD_DIM * (h + 1)] = dv.astype(dz_ref.dtype)
            rstate[:, sl] = cd_ref[:, sl] * rr + _dot_tn(q0[:, sl], do[:, sl])
        dq = _rope_bwd(jnp.concatenate(dqs, axis=1), cos, sin, 32)
        dk = _rope_bwd(jnp.concatenate(dks, axis=1) * _SCALE_B, cos, sin, 32)
        dz_ref[:, 0:GROUP] = dq.astype(dz_ref.dtype)
        dz_ref[:, GROUP:2 * GROUP] = dk.astype(dz_ref.dtype)
        dz_ref[:, 3 * GROUP:4 * GROUP] = jnp.concatenate(dgs, axis=1).astype(dz_ref.dtype)

    r0 = lambda: pl.BlockSpec((CHUNK, GROUP), lambda n: (nc - 1 - n, 0))
    return pl.pallas_call(
        body, grid=(nc,),
        in_specs=[r0(), rev(2), rev(3), rev(4), rev(5), r0(), pl.BlockSpec((1, HEAD_DIM, GROUP), lambda n: (nc - 1 - n, 0, 0)),
                  r0(), r0(), const((N_HEADS, CHUNK, CHUNK)), const((CHUNK, GROUP)), const((CHUNK, GROUP)),
                  const((CHUNK, GROUP)), const((1, GROUP))],
        out_specs=pl.BlockSpec((CHUNK, 4 * GROUP), lambda n: (nc - 1 - n, 0)),
        out_shape=SDS((s, 4 * GROUP), _MXU), scratch_shapes=[pltpu.VMEM((HEAD_DIM, GROUP), F32)],
        compiler_params=_cp("arbitrary"), name=name)(
            dy, z, z, z, z, o_pre, states, tb["b_cos"], tb["b_sin"], tb["decay"], tb["qw"], tb["kw2"], tb["qw0"], tb["cd"])


TQ = 256


def _log_sigmoid(x):
    return jnp.minimum(x, 0.0) - jnp.log1p(jnp.exp(-jnp.abs(x)))


def _fox_prep(z, b_f, name):
    s = z.shape[0]
    nb = s // TQ

    def body(m_ref, b_ref, cc_ref, cr_ref, carry):
        @pl.when(pl.program_id(0) == 0)
        def _():
            carry[...] = jnp.zeros_like(carry)

        lane = lax.broadcasted_iota(jnp.int32, (TQ, 128), 1)
        logf = jnp.where(lane < N_HEADS, _log_sigmoid(m_ref[...] + b_ref[...]), 0.0)
        r = lax.broadcasted_iota(jnp.int32, (TQ, TQ), 0)
        c = lax.broadcasted_iota(jnp.int32, (TQ, TQ), 1)
        tri = jnp.where(r >= c, 1.0, 0.0).astype(F32)
        cum = _dot_exact(tri, logf) + carry[...]
        cc_ref[...] = cum
        cr_ref[0] = cum.T[0:8, :]
        carry[...] = cum[TQ - 1:TQ, :]

    return pl.pallas_call(
        body, grid=(nb,),
        in_specs=[pl.BlockSpec((TQ, 128), lambda i: (i, NZ // 128 - 1)), pl.BlockSpec((1, 128), lambda i: (0, 0))],
        out_specs=[pl.BlockSpec((TQ, 128), lambda i: (i, 0)), pl.BlockSpec((1, 8, TQ), lambda i: (i, 0, 0))],
        out_shape=[SDS((s, 128), F32), SDS((nb, 8, TQ), F32)], scratch_shapes=[pltpu.VMEM((1, 128), F32)],
        compiler_params=_cp("arbitrary"), name=name)(z, b_f)


def _fox_post(dcr, dcq, z, b_f, dkr, name):
    s = z.shape[0]
    nb = s // TQ

    def body(dc_ref, dcq_ref, m_ref, b_ref, dkr_ref, dz_ref, db_ref, carry):
        @pl.when(pl.program_id(0) == 0)
        def _():
            carry[...] = jnp.zeros_like(carry)
            db_ref[...] = jnp.zeros_like(db_ref)

        r = lax.broadcasted_iota(jnp.int32, (TQ, TQ), 0)
        c = lax.broadcasted_iota(jnp.int32, (TQ, TQ), 1)
        triu = jnp.where(c >= r, 1.0, 0.0).astype(F32)
        dc = jnp.concatenate([dc_ref[0], jnp.zeros((120, TQ), F32)], axis=0)
        dlogf = _dot_exact(triu, dc, (((1,), (1,)), ((), ()))) + _dot_exact(triu, dcq_ref[...]) + carry[...]
        carry[...] = dlogf[0:1, :]
        x = m_ref[...] + b_ref[...]
        lane = lax.broadcasted_iota(jnp.int32, (TQ, 128), 1)
        df = jnp.where(lane < N_HEADS, dlogf * _sigmoid(-x), 0.0)
        db_ref[...] += jnp.sum(df, axis=0, keepdims=True)
        dz_ref[...] = (df + dkr_ref[...]).astype(dz_ref.dtype)

    rv = lambda i: nb - 1 - i
    return pl.pallas_call(
        body, grid=(nb,),
        in_specs=[pl.BlockSpec((1, 8, TQ), lambda i: (rv(i), 0, 0)), pl.BlockSpec((TQ, 128), lambda i: (rv(i), 0)),
                  pl.BlockSpec((TQ, 128), lambda i: (rv(i), NZ // 128 - 1)),
                  pl.BlockSpec((1, 128), lambda i: (0, 0)), pl.BlockSpec((TQ, 128), lambda i: (rv(i), 0))],
        out_specs=[pl.BlockSpec((TQ, 128), lambda i: (rv(i), 0)), pl.BlockSpec((1, 128), lambda i: (0, 0))],
        out_shape=[SDS((s, 128), _MXU), SDS((1, 128), F32)], scratch_shapes=[pltpu.VMEM((1, 128), F32)],
        compiler_params=_cp("arbitrary"), name=name)(dcr, dcq, z, b_f, dkr)


NEG = -1e30


def _causal_mask(shape, transposed=False):
    r = lax.broadcasted_iota(jnp.int32, shape, 0)
    c = lax.broadcasted_iota(jnp.int32, shape, 1)
    return (c >= r) if transposed else (r >= c)


def _flash_fwd(q, k, v, cols, dqk, scale, cum, name):
    s = q.shape[0]
    nq = s // TQ
    wq = N_HEADS * dqk
    bias = cum is not None

    def body(*refs):
        if bias:
            q_ref, k_ref, v_ref, cc_ref, cr_ref, o_ref, l_ref = refs
        else:
            q_ref, k_ref, v_ref, o_ref, l_ref = refs
        i = pl.program_id(0)
        l_ref[...] = jnp.zeros_like(l_ref)
        for h in range(N_HEADS):
            qh = q_ref[:, dqk * h:dqk * (h + 1)].astype(_MXU)
            cq = cc_ref[:, h:h + 1] if bias else None

            def step(j, carry, masked):
                m, l, acc = carry
                r0 = pl.multiple_of(j * TQ, TQ)
                kh = k_ref[pl.ds(r0, TQ), dqk * h:dqk * (h + 1)]
                vh = v_ref[pl.ds(r0, TQ), HEAD_DIM * h:HEAD_DIM * (h + 1)]
                sc = _dot_nt(qh, kh) * scale
                if bias:
                    sc = sc + (cq - cr_ref[j][h:h + 1, :])
                if masked:
                    sc = jnp.where(_causal_mask(sc.shape), sc, NEG)
                m_new = jnp.maximum(m, jnp.max(sc, axis=-1, keepdims=True))
                alpha = jnp.exp(m - m_new)
                p = jnp.exp(sc - m_new)
                return m_new, alpha * l + jnp.sum(p, axis=-1, keepdims=True), alpha * acc + _dot(p, vh)

            init = (jnp.full((TQ, 1), NEG, F32), jnp.zeros((TQ, 1), F32), jnp.zeros((TQ, HEAD_DIM), F32))
            carry = lax.fori_loop(0, i, functools.partial(step, masked=False), init)
            m, l, acc = step(i, carry, True)
            o_ref[:, HEAD_DIM * h:HEAD_DIM * (h + 1)] = acc / l
            l_ref[:, h:h + 1] = m + jnp.log(l)

    in_specs = [pl.BlockSpec((TQ, wq), lambda i: (i, cols[0])), pl.BlockSpec((s, wq), lambda i: (0, cols[1])),
                pl.BlockSpec((s, GROUP), lambda i: (0, cols[2]))]
    args = [q, k, v]
    if bias:
        in_specs += [pl.BlockSpec((TQ, 128), lambda i: (i, 0)), pl.BlockSpec((nq, 8, TQ), lambda i: (0, 0, 0))]
        args += list(cum)
    return pl.pallas_call(
        body, grid=(nq,), in_specs=in_specs,
        out_specs=[pl.BlockSpec((TQ, GROUP), lambda i: (i, 0)), pl.BlockSpec((TQ, 128), lambda i: (i, 0))],
        out_shape=[SDS((s, GROUP), F32), SDS((s, 128), F32)],
        compiler_params=_cp("parallel"), name=name)(*args)


def _flash_bwd(q, k, v, cols, dqk, scale, cum, do, lse, delta, name, kv_dtype):
    s = q.shape[0]
    nq = s // TQ
    wq = N_HEADS * dqk
    bias = cum is not None

    def body(*refs):
        if bias:
            q_ref, k_ref, v_ref, do_ref, l_ref, d_ref, cc_ref, cr_ref, dq_ref, dk_ref, dv_ref, dc_ref, dcq_ref = refs
        else:
            q_ref, k_ref, v_ref, do_ref, l_ref, d_ref, dq_ref, dk_ref, dv_ref = refs
        j = pl.program_id(0)

        @pl.when(j == 0)
        def _():
            dq_ref[...] = jnp.zeros_like(dq_ref)
            if bias:
                dcq_ref[...] = jnp.zeros_like(dcq_ref)

        if bias:
            dc_ref[...] = jnp.zeros_like(dc_ref)
        for h in range(N_HEADS):
            hq = slice(dqk * h, dqk * (h + 1))
            hv = slice(HEAD_DIM * h, HEAD_DIM * (h + 1))
            kh = k_ref[:, hq].astype(_MXU)
            vh = v_ref[:, hv].astype(_MXU)
            ck = cr_ref[0][h:h + 1, :] if bias else None

            def step(i, carry, masked):
                dk, dv, dc = carry
                r0 = pl.multiple_of(i * TQ, TQ)
                rows = pl.ds(r0, TQ)
                qh = q_ref[rows, hq].astype(_MXU)
                doh = do_ref[rows, hv].astype(_MXU)
                sc = _dot_nt(qh, kh) * scale
                if bias:
                    sc = sc + (cc_ref[rows, h:h + 1] - ck)
                p = jnp.exp(sc - l_ref[rows, h:h + 1])
                if masked:
                    p = jnp.where(_causal_mask(p.shape), p, 0.0)
                dv = dv + _dot_tn(p, doh)
                dp = _dot_nt(doh, vh)
                ds = p * (dp - d_ref[rows, h:h + 1])
                dk = dk + _dot_tn(ds, qh) * scale
                dq_ref[rows, hq] += _dot(ds, kh) * scale
                if bias:
                    dc = dc + jnp.sum(ds, axis=0, keepdims=True)
                    dcq_ref[rows, h:h + 1] += jnp.sum(ds, axis=1, keepdims=True)
                return dk, dv, dc

            init = (jnp.zeros((TQ, dqk), F32), jnp.zeros((TQ, HEAD_DIM), F32), jnp.zeros((1, TQ), F32))
            carry = step(j, init, True)
            dk, dv, dc = lax.fori_loop(j + 1, nq, functools.partial(step, masked=False), carry)
            dk_ref[:, hq] = dk.astype(dk_ref.dtype)
            dv_ref[:, hv] = dv.astype(dv_ref.dtype)
            if bias:
                dc_ref[0, h:h + 1, :] = -dc

    full = lambda w, c=0: pl.BlockSpec((s, w), lambda j, c=c: (0, c))
    in_specs = [full(wq, cols[0]), pl.BlockSpec((TQ, wq), lambda j: (j, cols[1])), pl.BlockSpec((TQ, GROUP), lambda j: (j, cols[2])),
                full(GROUP), full(128), full(128)]
    args = [q, k, v, do, lse, delta]
    out_specs = [full(wq), pl.BlockSpec((TQ, wq), lambda j: (j, 0)), pl.BlockSpec((TQ, GROUP), lambda j: (j, 0))]
    out_shape = [SDS((s, wq), F32), SDS((s, wq), kv_dtype), SDS((s, GROUP), kv_dtype)]
    if bias:
        in_specs += [full(128), pl.BlockSpec((1, 8, TQ), lambda j: (j, 0, 0))]
        args += list(cum)
        out_specs += [pl.BlockSpec((1, 8, TQ), lambda j: (j, 0, 0)), full(128)]
        out_shape += [SDS((nq, 8, TQ), F32), SDS((s, 128), F32)]
    return pl.pallas_call(body, grid=(nq,), in_specs=in_specs, out_specs=out_specs, out_shape=out_shape,
                          compiler_params=_cp("arbitrary"), name=name)(*args)


def _head_lanes(h, dqk):
    return slice(128 * (h // 2), 128 * (h // 2) + 128) if dqk == HEAD_DIM else slice(128 * h, 128 * h + 128)


def _keep_half(x, a, axis):
    idx = lax.broadcasted_iota(jnp.int32, x.shape, axis)
    return jnp.where((idx < HEAD_DIM) if a == 0 else (idx >= HEAD_DIM), x, jnp.zeros_like(x))


def _kv_prep(z, kcol, vcol, name):
    s = z.shape[0]
    nk = s // TQ

    def body(k_ref, v_ref, kb_ref, vb_ref, vt_ref):
        kb_ref[...] = k_ref[...].astype(_MXU)
        v = v_ref[...]
        vb_ref[...] = v.astype(_MXU)
        vt_ref[0] = v.T.astype(_MXU)

    blk = pl.BlockSpec((TQ, GROUP), lambda i: (i, 0))
    return pl.pallas_call(
        body, grid=(nk,),
        in_specs=[pl.BlockSpec((TQ, GROUP), lambda i: (i, kcol)), pl.BlockSpec((TQ, GROUP), lambda i: (i, vcol))],
        out_specs=[blk, blk, pl.BlockSpec((1, GROUP, TQ), lambda i: (i, 0, 0))],
        out_shape=[SDS((s, GROUP), _MXU), SDS((s, GROUP), _MXU), SDS((nk, GROUP, TQ), _MXU)],
        compiler_params=_cp("parallel"), name=name)(z, z)


def _attn_fwd(q, qcol, dqk, kb, vt, scale, cum, name, comm=None):
    s = q.shape[0]
    nq = s // TQ
    wq = N_HEADS * dqk
    bias = cum is not None

    def body(*refs):
        ins, (o_ref, l_ref), _, cc = _split_refs(refs, 5 if bias else 3, 2, comm)
        if bias:
            q_ref, k_ref, vt_ref, cc_ref, cr_ref = ins
        else:
            q_ref, k_ref, vt_ref = ins
        i = pl.program_id(0)
        if comm is not None:
            @pl.when(i == 0)
            def _():
                comm.start(*cc)

        qts = []
        for h in range(N_HEADS):
            qt = q_ref[:, _head_lanes(h, dqk)].astype(F32).T
            qts.append((_keep_half(qt, h % 2, 0) if dqk == HEAD_DIM else qt).astype(_MXU))
        cqs = [cr_ref[0][h:h + 1, :] for h in range(N_HEADS)] if bias else None

        def step(j, carry, masked):
            r0 = pl.multiple_of(j * TQ, TQ)
            vtj = vt_ref[j]
            sts = [jnp.dot(k_ref[pl.ds(r0, TQ), _head_lanes(h, dqk)], qts[h], preferred_element_type=F32)
                   for h in range(N_HEADS)]
            stats, ps = [], []
            for h in range(N_HEADS):
                m, l, _ = carry[3 * h:3 * h + 3]
                st = sts[h] * scale
                if bias:
                    st = st + (cqs[h] - cc_ref[pl.ds(r0, TQ), h:h + 1])
                if masked:
                    st = jnp.where(_causal_mask(st.shape, transposed=True), st, NEG)
                m_new = jnp.maximum(m, jnp.max(st, axis=0, keepdims=True))
                alpha = jnp.exp(m - m_new)
                p = jnp.exp(st - m_new)
                stats.append((m_new, alpha * l + jnp.sum(p, axis=0, keepdims=True), alpha))
                ps.append(p.astype(_MXU))
            out = []
            for h in range(N_HEADS):
                m_new, l, alpha = stats[h]
                acc = alpha * carry[3 * h + 2] + jnp.dot(vtj[HEAD_DIM * h:HEAD_DIM * (h + 1), :], ps[h],
                                                         preferred_element_type=F32)
                out += [m_new, l, acc]
            return tuple(out)

        init = (jnp.full((1, TQ), NEG, F32), jnp.zeros((1, TQ), F32), jnp.zeros((HEAD_DIM, TQ), F32)) * N_HEADS
        carry = lax.fori_loop(0, i, functools.partial(step, masked=False), init)
        carry = step(i, carry, True)
        l_ref[...] = jnp.zeros_like(l_ref)
        for h in range(N_HEADS):
            l_ref[0, h:h + 1, :] = carry[3 * h] + jnp.log(carry[3 * h + 1])
        for p in range(2):
            ot = jnp.concatenate([carry[6 * p + 2] / carry[6 * p + 1], carry[6 * p + 5] / carry[6 * p + 4]], axis=0)
            o_ref[:, 128 * p:128 * (p + 1)] = ot.T
        if comm is not None:
            @pl.when(i == nq - 1)
            def _():
                comm.wait(*cc)

    rows = pl.BlockSpec((1, 8, TQ), lambda i: (i, 0, 0))
    in_specs = [pl.BlockSpec((TQ, wq), lambda i: (i, qcol)), pl.BlockSpec((s, wq), lambda i: (0, 0)),
                pl.BlockSpec((nq, GROUP, TQ), lambda i: (0, 0, 0))]
    args = [q, kb, vt]
    if bias:
        in_specs += [pl.BlockSpec((s, 128), lambda i: (0, 0)), rows]
        args += list(cum)
    out_specs = [pl.BlockSpec((TQ, GROUP), lambda i: (i, 0)), rows]
    out_shape = [SDS((s, GROUP), F32), SDS((nq, 8, TQ), F32)]
    return _call_with_comm(body, (nq,), in_specs, out_specs, out_shape, [], args, comm, ("arbitrary",), name)


def _call_with_comm(body, grid, in_specs, out_specs, out_shape, scratch, args, comm, semantics, name):
    n_out = len(out_shape)
    if comm is not None:
        in_specs, out_specs = in_specs + comm.in_specs, out_specs + comm.out_specs
        out_shape, scratch, args = out_shape + comm.out_shape, scratch + comm.scratch, list(args) + comm.arrs
    res = pl.pallas_call(body, grid=grid, in_specs=in_specs, out_specs=out_specs, out_shape=out_shape,
                         scratch_shapes=scratch, compiler_params=_cp(*semantics), name=name)(*args)
    return (*res[:n_out], list(res[n_out:]))


def _attn_bwd_prep(q, qcol, dqk, o, do, name):
    s = q.shape[0]
    nq = s // TQ
    wq = N_HEADS * dqk

    def body(q_ref, o_ref, do_ref, qt_ref, dot_ref, dl_ref):
        qt_ref[0] = q_ref[...].astype(F32).T.astype(_MXU)
        dov = do_ref[...]
        dot_ref[0] = dov.T.astype(_MXU)
        pt = (dov * o_ref[...]).T
        dl_ref[...] = jnp.zeros_like(dl_ref)
        for h in range(N_HEADS):
            dl_ref[0, h:h + 1, :] = jnp.sum(pt[HEAD_DIM * h:HEAD_DIM * (h + 1), :], axis=0, keepdims=True)

    nat = lambda w: pl.BlockSpec((TQ, w), lambda i: (i, 0))
    tr = lambda w: pl.BlockSpec((1, w, TQ), lambda i: (i, 0, 0))
    return pl.pallas_call(
        body, grid=(nq,),
        in_specs=[pl.BlockSpec((TQ, wq), lambda i: (i, qcol)), nat(GROUP), nat(GROUP)],
        out_specs=[tr(wq), tr(GROUP), tr(8)],
        out_shape=[SDS((nq, wq, TQ), _MXU), SDS((nq, GROUP, TQ), _MXU), SDS((nq, 8, TQ), F32)],
        compiler_params=_cp("parallel"), name=name)(q, o, do)


def _attn_bwd(kb, vb, qt, dot, lse, dl, dqk, scale, cum, name, kv_dtype, comm=None):
    s = kb.shape[0]
    nq = s // TQ
    wq = N_HEADS * dqk
    bias = cum is not None

    def body(*refs):
        ins, outs, _, cc = _split_refs(refs, 8 if bias else 6, 5 if bias else 3, comm)
        if bias:
            k_ref, v_ref, qt_ref, dot_ref, l_ref, d_ref, cc_ref, cr_ref = ins
            dqt_ref, dk_ref, dv_ref, dck_ref, dcq_ref = outs
        else:
            k_ref, v_ref, qt_ref, dot_ref, l_ref, d_ref = ins
            dqt_ref, dk_ref, dv_ref = outs
        j = pl.program_id(0)

        @pl.when(j == 0)
        def _():
            if comm is not None:
                comm.start(*cc)
            dqt_ref[...] = jnp.zeros_like(dqt_ref)
            if bias:
                dcq_ref[...] = jnp.zeros_like(dcq_ref)

        ks, kts, vs = [], [], []
        for h in range(N_HEADS):
            k2 = k_ref[:, _head_lanes(h, dqk)]
            if dqk == HEAD_DIM:
                k2 = _keep_half(k2, h % 2, 1)
            ks.append(k2)
            kts.append(k2.astype(F32).T.astype(_MXU))
            vs.append(_keep_half(v_ref[:, _head_lanes(h, HEAD_DIM)], h % 2, 1))
        cks = [cc_ref[:, h:h + 1] for h in range(N_HEADS)] if bias else None

        nt = (((1,), (1,)), ((), ()))

        def step(i, carry, masked):
            qti, doti, li, di = qt_ref[i], dot_ref[i], l_ref[i], d_ref[i]
            cri = cr_ref[i] if bias else None
            qls = [_head_lanes(h, dqk) for h in range(N_HEADS)]
            vls = [_head_lanes(h, HEAD_DIM) for h in range(N_HEADS)]
            sts = [jnp.dot(ks[h], qti[qls[h], :], preferred_element_type=F32) for h in range(N_HEADS)]
            dpts = [jnp.dot(vs[h], doti[vls[h], :], preferred_element_type=F32) for h in range(N_HEADS)]
            pbs, dsbs, dcks = [], [], []
            for h in range(N_HEADS):
                st = sts[h] * scale
                rowterm = li[h:h + 1, :]
                if bias:
                    st = st + ((cri[h:h + 1, :] - rowterm) - cks[h])
                else:
                    st = st - rowterm
                p = jnp.exp(st)
                if masked:
                    p = jnp.where(_causal_mask(p.shape, transposed=True), p, 0.0)
                dst = p * (dpts[h] - di[h:h + 1, :])
                pbs.append(p.astype(_MXU))
                dsbs.append(dst.astype(_MXU))
                if bias:
                    dcks.append(carry[3 * h + 2] + jnp.sum(dst, axis=1, keepdims=True))
                    dcq_ref[i, h:h + 1, :] += jnp.sum(dst, axis=0, keepdims=True)
                else:
                    dcks.append(carry[3 * h + 2])
            out = []
            for h in range(N_HEADS):
                dvt = carry[3 * h + 1] + lax.dot_general(doti[HEAD_DIM * h:HEAD_DIM * (h + 1), :], pbs[h], nt,
                                                         preferred_element_type=F32)
                dkt = carry[3 * h] + lax.dot_general(qti[dqk * h:dqk * (h + 1), :], dsbs[h], nt, preferred_element_type=F32)
                dqt_ref[i, qls[h], :] += jnp.dot(kts[h], dsbs[h], preferred_element_type=F32) * scale
                out += [dkt, dvt, dcks[h]]
            return tuple(out)

        init = (jnp.zeros((dqk, TQ), F32), jnp.zeros((HEAD_DIM, TQ), F32), jnp.zeros((TQ, 1), F32)) * N_HEADS
        carry = step(j, init, True)
        carry = lax.fori_loop(j + 1, nq, functools.partial(step, masked=False), carry)
        for p in range(2):
            dv_ref[:, 128 * p:128 * (p + 1)] = jnp.concatenate([carry[6 * p + 1], carry[6 * p + 4]], axis=0).T.astype(dv_ref.dtype)
            if dqk == HEAD_DIM:
                dk_ref[:, 128 * p:128 * (p + 1)] = (jnp.concatenate([carry[6 * p], carry[6 * p + 3]], axis=0).T * scale).astype(dk_ref.dtype)
        if dqk != HEAD_DIM:
            for h in range(N_HEADS):
                dk_ref[:, 128 * h:128 * (h + 1)] = (carry[3 * h].T * scale).astype(dk_ref.dtype)
        if bias:
            dck_ref[...] = jnp.zeros_like(dck_ref)
            for h in range(N_HEADS):
                dck_ref[:, h:h + 1] = -carry[3 * h + 2]
        if comm is not None:
            @pl.when(j == nq - 1)
            def _():
                comm.wait(*cc)

    blk = lambda w: pl.BlockSpec((TQ, w), lambda j: (j, 0))
    full3 = lambda w: pl.BlockSpec((nq, w, TQ), lambda j: (0, 0, 0))
    in_specs = [blk(wq), blk(GROUP), full3(wq), full3(GROUP), full3(8), full3(8)]
    args = [kb, vb, qt, dot, lse, dl]
    out_specs = [full3(wq), blk(wq), blk(GROUP)]
    out_shape = [SDS((nq, wq, TQ), F32), SDS((s, wq), kv_dtype), SDS((s, GROUP), kv_dtype)]
    if bias:
        in_specs += [blk(128), full3(8)]
        args += list(cum)
        out_specs += [blk(128), full3(8)]
        out_shape += [SDS((s, 128), F32), SDS((nq, 8, TQ), F32)]
    return _call_with_comm(body, (nq,), in_specs, out_specs, out_shape, [], args, comm, ("arbitrary",), name)


def _untranspose(xt, dtype, name):
    nq, w, _ = xt.shape

    def body(x_ref, o_ref):
        o_ref[...] = x_ref[0].T.astype(o_ref.dtype)

    return pl.pallas_call(
        body, grid=(nq,), in_specs=[pl.BlockSpec((1, w, TQ), lambda i: (i, 0, 0))],
        out_specs=pl.BlockSpec((TQ, w), lambda i: (i, 0)), out_shape=SDS((nq * TQ, w), dtype),
        compiler_params=_cp("parallel"), name=name)(xt)


_SCALE_D = (64 + 32) ** -0.5
_COL_CQ, _COL_CKV, _COL_MISC = 2304 // 256, 2560 // 128, 2688 // 128


def _mla_prep(z, gq, gkv, wq, wk, wv, tb, name):
    s = z.shape[0]
    tm = TQ
    row = lambda w, c: pl.BlockSpec((tm, w), lambda i, c=c: (i, c))
    const = lambda a: pl.BlockSpec(a.shape, lambda i: (0,) * a.ndim)

    def body(cq_ref, ckv_ref, m_ref, gq_ref, gkv_ref, wq_ref, wk_ref, wv_ref, e_ref, qc_ref, qs_ref, kc_ref, ks_ref,
             q_ref, k_ref, v_ref, vt_ref, cqn_ref, ckvn_ref):
        cqn = _rms(cq_ref[...], gq_ref[...]).astype(_MXU)
        ckvn = _rms(ckv_ref[...], gkv_ref[...]).astype(_MXU)
        cqn_ref[...] = cqn
        ckvn_ref[...] = ckvn
        q_ref[...] = _rope(_dot(cqn, wq_ref[...]), qc_ref[...], qs_ref[...], 16).astype(q_ref.dtype)
        kr = _rope(m_ref[...], kc_ref[...], ks_ref[...], 16)
        k_ref[...] = (_dot(ckvn, wk_ref[...]) + _dot(kr, e_ref[...])).astype(k_ref.dtype)
        v = _dot(ckvn, wv_ref[...])
        v_ref[...] = v.astype(v_ref.dtype)
        vt_ref[0] = v.T.astype(vt_ref.dtype)

    e = tb["place"]
    return pl.pallas_call(
        body, grid=(s // tm,),
        in_specs=[row(256, _COL_CQ), row(128, _COL_CKV), row(128, _COL_MISC), const(gq), const(gkv), const(wq), const(wk),
                  const(wv), const(e), row(512, 0), row(512, 0), row(128, 0), row(128, 0)],
        out_specs=[row(512, 0), row(512, 0), row(256, 0), pl.BlockSpec((1, GROUP, TQ), lambda i: (i, 0, 0)), row(256, 0),
                   row(128, 0)],
        out_shape=[SDS((s, 512), _MXU), SDS((s, 512), _MXU), SDS((s, 256), _MXU), SDS((s // TQ, GROUP, TQ), _MXU),
                   SDS((s, 256), _MXU), SDS((s, 128), _MXU)],
        compiler_params=_cp("parallel"), name=name)(
            z, z, z, gq, gkv, wq, wk, wv, e, tb["q_cos"], tb["q_sin"], tb["k_cos"], tb["k_sin"])


def _mla_prep_bwd(dq, dk, dv, z, cqn, ckvn, gq, gkv, wq, wk, wv, tb, name):
    s = z.shape[0]
    tm = min(512, s)
    row = lambda w, c: pl.BlockSpec((tm, w), lambda i, c=c: (i, c))
    const = lambda a: pl.BlockSpec(a.shape, lambda i: (0,) * a.ndim)
    acc = lambda shape: pl.BlockSpec(shape, lambda i: (0, 0))

    def body(dq_ref, dk_ref, dv_ref, cq_ref, ckv_ref, cqn_ref, ckvn_ref, gq_ref, gkv_ref, wq_ref, wk_ref, wv_ref, e_ref,
             qc_ref, qs_ref, kc_ref, ks_ref, dcq_ref, dckv_ref, dkr_ref, dwq_ref, dwk_ref, dwv_ref, dgq_ref, dgkv_ref):
        @pl.when(pl.program_id(0) == 0)
        def _():
            for r in (dwq_ref, dwk_ref, dwv_ref, dgq_ref, dgkv_ref):
                r[...] = jnp.zeros_like(r)

        dqp = _rope_bwd(dq_ref[...], qc_ref[...], qs_ref[...], 16)
        dkd = dk_ref[...]
        dvd = dv_ref[...]
        dwq_ref[...] += _dot_tn(cqn_ref[...], dqp)
        dwk_ref[...] += _dot_tn(ckvn_ref[...], dkd)
        dwv_ref[...] += _dot_tn(ckvn_ref[...], dvd)
        dcq, dgq = _rms_bwd(cq_ref[...], gq_ref[...], _dot_nt(dqp, wq_ref[...]))
        dckv, dgkv = _rms_bwd(ckv_ref[...], gkv_ref[...], _dot_nt(dkd, wk_ref[...]) + _dot_nt(dvd, wv_ref[...]))
        dcq_ref[...] = dcq.astype(dcq_ref.dtype)
        dckv_ref[...] = dckv.astype(dckv_ref.dtype)
        dgq_ref[...] += dgq
        dgkv_ref[...] += dgkv
        dkr = _dot_exact(dkd, e_ref[...], (((1,), (1,)), ((), ())))
        dkr_ref[...] = _rope_bwd(dkr, kc_ref[...], ks_ref[...], 16)

    e = tb["place"]
    return pl.pallas_call(
        body, grid=(s // tm,),
        in_specs=[row(512, 0), row(512, 0), row(256, 0), row(256, _COL_CQ), row(128, _COL_CKV), row(256, 0), row(128, 0),
                  const(gq), const(gkv), const(wq), const(wk), const(wv), const(e), row(512, 0), row(512, 0), row(128, 0), row(128, 0)],
        out_specs=[row(256, 0), row(128, 0), row(128, 0), acc((256, 512)), acc((128, 512)), acc((128, 256)), acc((1, 256)),
                   acc((1, 128))],
        out_shape=[SDS((s, 256), _MXU), SDS((s, 128), _MXU), SDS((s, 128), F32), SDS((256, 512), F32), SDS((128, 512), F32),
                   SDS((128, 256), F32), SDS((1, 256), F32), SDS((1, 128), F32)],
        compiler_params=_cp("arbitrary"), name=name)(
            dq, dk, dv, z, z, cqn, ckvn, gq, gkv, wq, wk, wv, e, tb["q_cos"], tb["q_sin"], tb["k_cos"], tb["k_sin"])


def _out_proj(ys, g, w, x, name):
    s, d = x.shape
    tm = min(512, s)

    def body(ya, yb, yc, yd, g_ref, w_ref, x_ref, o_ref, yn_ref):
        acc = x_ref[...]
        for i, y_ref in enumerate((ya, yb, yc, yd)):
            sl = slice(GROUP * i, GROUP * (i + 1))
            yn = _rms(y_ref[...], g_ref[:, sl]).astype(_MXU)
            yn_ref[:, sl] = yn
            acc = acc + jnp.dot(yn, w_ref[sl, :], preferred_element_type=F32)
        o_ref[...] = acc

    yspec = pl.BlockSpec((tm, GROUP), lambda i: (i, 0))
    return pl.pallas_call(
        body, grid=(s // tm,),
        in_specs=[yspec, yspec, yspec, yspec, pl.BlockSpec((1, d), lambda i: (0, 0)), pl.BlockSpec((d, d), lambda i: (0, 0)),
                  pl.BlockSpec((tm, d), lambda i: (i, 0))],
        out_specs=[pl.BlockSpec((tm, d), lambda i: (i, 0)), pl.BlockSpec((tm, d), lambda i: (i, 0))],
        out_shape=[SDS((s, d), F32), SDS((s, d), _MXU)], compiler_params=_cp("parallel"), name=name)(*ys, g, w, x)


def _out_proj_bwd(dx, w, ys, g, name):
    s, d = dx.shape
    tm = min(512, s)

    def body(dx_ref, w_ref, ya, yb, yc, yd, g_ref, da, db, dc, dd, dg_ref):
        @pl.when(pl.program_id(0) == 0)
        def _():
            dg_ref[...] = jnp.zeros_like(dg_ref)

        dyn = _dot_nt(dx_ref[...], w_ref[...])
        outs = (da, db, dc, dd)
        for i, y_ref in enumerate((ya, yb, yc, yd)):
            sl = slice(GROUP * i, GROUP * (i + 1))
            dy, dg = _rms_bwd(y_ref[...], g_ref[:, sl], dyn[:, sl])
            outs[i][...] = dy
            dg_ref[:, sl] += dg

    yspec = pl.BlockSpec((tm, GROUP), lambda i: (i, 0))
    return pl.pallas_call(
        body, grid=(s // tm,),
        in_specs=[pl.BlockSpec((tm, d), lambda i: (i, 0)), pl.BlockSpec((d, d), lambda i: (0, 0)), yspec, yspec, yspec, yspec,
                  pl.BlockSpec((1, d), lambda i: (0, 0))],
        out_specs=[yspec, yspec, yspec, yspec, pl.BlockSpec((1, d), lambda i: (0, 0))],
        out_shape=[SDS((s, GROUP), F32)] * 4 + [SDS((1, d), F32)],
        compiler_params=_cp("arbitrary"), name=name)(dx, w, *ys, g)


FF_BLOCK = 512


def _ffn_fwd(x, g, wu, wd, name, comm=None):
    s, d = x.shape
    nj = wu.shape[0]
    tm = min(512, s)
    ni = s // tm

    def body(*refs):
        (x_ref, g_ref, wu_ref, wd_ref), (o_ref, u_ref, h_ref), (acc,), cc = _split_refs(refs, 4, 3, comm)
        i, j = pl.program_id(0), pl.program_id(1)
        if comm is not None:
            @pl.when((i == 0) & (j == 0))
            def _():
                comm.start(*cc)

        @pl.when(j == 0)
        def _():
            h_ref[...] = _rms(x_ref[...], g_ref[...]).astype(h_ref.dtype)
            acc[...] = jnp.zeros_like(acc)

        u = jnp.dot(h_ref[...], wu_ref[0], preferred_element_type=F32)
        u_ref[...] = u.astype(u_ref.dtype)
        acc[...] += _dot(jnp.square(jnp.maximum(u, 0.0)), wd_ref[...])

        @pl.when(j == nj - 1)
        def _():
            o_ref[...] = x_ref[...] + acc[...]

        if comm is not None:
            @pl.when((i == ni - 1) & (j == nj - 1))
            def _():
                comm.wait(*cc)

    in_specs = [pl.BlockSpec((tm, d), lambda i, j: (i, 0)), pl.BlockSpec((1, d), lambda i, j: (0, 0)),
                pl.BlockSpec((1, d, FF_BLOCK), lambda i, j: (j, 0, 0)), pl.BlockSpec((FF_BLOCK, d), lambda i, j: (j, 0))]
    out_specs = [pl.BlockSpec((tm, d), lambda i, j: (i, 0)), pl.BlockSpec((tm, FF_BLOCK), lambda i, j: (i, j)),
                 pl.BlockSpec((tm, d), lambda i, j: (i, 0))]
    out_shape = [SDS((s, d), F32), SDS((s, nj * FF_BLOCK), _MXU), SDS((s, d), _MXU)]
    return _call_with_comm(body, (ni, nj), in_specs, out_specs, out_shape, [pltpu.VMEM((tm, d), F32)], [x, g, wu, wd], comm,
                           ("arbitrary", "arbitrary"), name)


def _ffn_bwd(dx2, x, u, g, wu, wd, name, comm=None):
    s, d = x.shape
    nj = wu.shape[0]
    tm = min(512, s)
    ni = s // tm

    def body(*refs):
        (dx_ref, x_ref, u_ref, g_ref, wu_ref, wd_ref), (o_ref, du_ref, dg_ref), (acc, dxb), cc = _split_refs(refs, 6, 3, comm)
        i, j = pl.program_id(0), pl.program_id(1)

        @pl.when((i == 0) & (j == 0))
        def _():
            if comm is not None:
                comm.start(*cc)
            dg_ref[...] = jnp.zeros_like(dg_ref)

        @pl.when(j == 0)
        def _():
            dxb[...] = dx_ref[...].astype(dxb.dtype)
            acc[...] = jnp.zeros_like(acc)

        da = lax.dot_general(dxb[...], wd_ref[...], (((1,), (1,)), ((), ())), preferred_element_type=F32)
        du = (da * 2.0 * jnp.maximum(u_ref[...].astype(F32), 0.0)).astype(du_ref.dtype)
        du_ref[...] = du
        acc[...] += lax.dot_general(du, wu_ref[0], (((1,), (1,)), ((), ())), preferred_element_type=F32)

        @pl.when(j == nj - 1)
        def _():
            dxn, dg = _rms_bwd(x_ref[...], g_ref[...], acc[...])
            o_ref[...] = dx_ref[...] + dxn
            dg_ref[...] += dg

        if comm is not None:
            @pl.when((i == ni - 1) & (j == nj - 1))
            def _():
                comm.wait(*cc)

    in_specs = [pl.BlockSpec((tm, d), lambda i, j: (i, 0)), pl.BlockSpec((tm, d), lambda i, j: (i, 0)),
                pl.BlockSpec((tm, FF_BLOCK), lambda i, j: (i, j)), pl.BlockSpec((1, d), lambda i, j: (0, 0)),
                pl.BlockSpec((1, d, FF_BLOCK), lambda i, j: (j, 0, 0)), pl.BlockSpec((FF_BLOCK, d), lambda i, j: (j, 0))]
    out_specs = [pl.BlockSpec((tm, d), lambda i, j: (i, 0)), pl.BlockSpec((tm, FF_BLOCK), lambda i, j: (i, j)),
                 pl.BlockSpec((1, d), lambda i, j: (0, 0))]
    out_shape = [SDS((s, d), F32), SDS((s, nj * FF_BLOCK), _MXU), SDS((1, d), F32)]
    return _call_with_comm(body, (ni, nj), in_specs, out_specs, out_shape,
                           [pltpu.VMEM((tm, d), F32), pltpu.VMEM((tm, d), _MXU)], [dx2, x, u, g, wu, wd], comm,
                           ("arbitrary", "arbitrary"), name)


def _in_proj_bwd(dz, w, x, g, dx_up, name):
    s, d = x.shape
    n = w.shape[1]
    tm = min(512, s)

    def body(dz_ref, w_ref, x_ref, g_ref, up_ref, o_ref, dg_ref):
        @pl.when(pl.program_id(0) == 0)
        def _():
            dg_ref[...] = jnp.zeros_like(dg_ref)

        dh = lax.dot_general(dz_ref[...], w_ref[...], (((1,), (1,)), ((), ())), preferred_element_type=F32)
        dxn, dg = _rms_bwd(x_ref[...], g_ref[...], dh)
        o_ref[...] = up_ref[...] + dxn
        dg_ref[...] += dg

    return pl.pallas_call(
        body, grid=(s // tm,),
        in_specs=[pl.BlockSpec((tm, n), lambda i: (i, 0)), pl.BlockSpec((d, n), lambda i: (0, 0)),
                  pl.BlockSpec((tm, d), lambda i: (i, 0)), pl.BlockSpec((1, d), lambda i: (0, 0)),
                  pl.BlockSpec((tm, d), lambda i: (i, 0))],
        out_specs=[pl.BlockSpec((tm, d), lambda i: (i, 0)), pl.BlockSpec((1, d), lambda i: (0, 0))],
        out_shape=[SDS((s, d), F32), SDS((1, d), F32)], compiler_params=_cp("arbitrary"), name=name)(dz, w, x, g, dx_up)


def _loss_head(x, g, target, name):
    s, d = x.shape
    tm = min(512, s)

    def body(x_ref, g_ref, t_ref, l_ref, dx_ref, dg_ref):
        @pl.when(pl.program_id(0) == 0)
        def _():
            l_ref[...] = jnp.zeros_like(l_ref)
            dg_ref[...] = jnp.zeros_like(dg_ref)

        xv = x_ref[...]
        err = _rms(xv, g_ref[...]) - t_ref[...]
        l_ref[...] += jnp.sum(err * err, axis=0, keepdims=True) * (0.5 / d)
        dx, dg = _rms_bwd(xv, g_ref[...], err * (1.0 / d))
        dx_ref[...] = dx
        dg_ref[...] += dg

    return pl.pallas_call(
        body, grid=(s // tm,),
        in_specs=[pl.BlockSpec((tm, d), lambda i: (i, 0)), pl.BlockSpec((1, d), lambda i: (0, 0)),
                  pl.BlockSpec((tm, d), lambda i: (i, 0))],
        out_specs=[pl.BlockSpec((1, d), lambda i: (0, 0)), pl.BlockSpec((tm, d), lambda i: (i, 0)),
                   pl.BlockSpec((1, d), lambda i: (0, 0))],
        out_shape=[SDS((1, d), F32), SDS((s, d), F32), SDS((1, d), F32)], compiler_params=_cp("arbitrary"), name=name)(x, g, target)


def _me_and_peer():
    x, y, c = lax.axis_index("x"), lax.axis_index("y"), lax.axis_index("c")
    me = 4 * x + 2 * y + c

    def peer(k):
        px, py, pc = x ^ (k >> 2), y ^ ((k >> 1) & 1), c ^ (k & 1)
        return (px, py, pc), 4 * px + 2 * py + pc

    return me, peer


class _Comm:
    def __init__(self, kind, arrs):
        assert kind in ("gather", "exchange")
        self.kind, self.arrs, self.n = kind, list(arrs), len(arrs)
        anyspec = pl.BlockSpec(memory_space=pl.ANY)
        self.in_specs = [anyspec] * self.n
        self.out_specs = [anyspec] * self.n
        self.out_shape = [SDS(((NDEV,) + a.shape) if kind == "gather" else a.shape, a.dtype) for a in self.arrs]
        self.scratch = [pltpu.SemaphoreType.DMA((self.n, NDEV - 1)), pltpu.SemaphoreType.DMA((self.n, NDEV - 1)),
                        pltpu.SemaphoreType.DMA((self.n,))]

    def _copies(self, ins, outs, sems):
        send, recv, loc = sems
        me, peer = _me_and_peer()
        gather = self.kind == "gather"
        local = [pltpu.make_async_copy(ins[a] if gather else ins[a].at[me], outs[a].at[me], loc.at[a]) for a in range(self.n)]
        outgoing, incoming = [], []
        for k in range(1, NDEV):
            dev, pid = peer(k)
            for a in range(self.n):
                pair = dict(send_sem=send.at[a, k - 1], recv_sem=recv.at[a, k - 1], device_id=dev, device_id_type=MESH)
                outgoing.append(pltpu.make_async_remote_copy(src_ref=ins[a] if gather else ins[a].at[pid],
                                                             dst_ref=outs[a].at[me], **pair))
                incoming.append(pltpu.make_async_remote_copy(src_ref=ins[a] if gather else ins[a].at[me],
                                                             dst_ref=outs[a].at[pid], **pair))
        return local, outgoing, incoming

    def start(self, ins, outs, sems):
        local, outgoing, _ = self._copies(ins, outs, sems)
        for cp in local + outgoing:
            cp.start()

    def wait(self, ins, outs, sems):
        local, outgoing, incoming = self._copies(ins, outs, sems)
        for cp in incoming:
            cp.wait_recv()
        for cp in outgoing:
            cp.wait_send()
        for cp in local:
            cp.wait()


def _split_refs(refs, n_in, n_out, comm):
    c = comm.n if comm is not None else 0
    ins, cin = refs[:n_in], refs[n_in:n_in + c]
    outs, cout = refs[n_in + c:n_in + c + n_out], refs[n_in + c + n_out:n_in + 2 * c + n_out]
    rest = refs[n_in + 2 * c + n_out:]
    scratch, csem = (rest[:len(rest) - 3], rest[len(rest) - 3:]) if c else (rest, ())
    return ins, outs, scratch, (cin, cout, csem)


def _comm_call(kind, arrs, name):
    comm = _Comm(kind, arrs)

    def body(*refs):
        _, _, _, c = _split_refs(refs, 0, 0, comm)
        comm.start(*c)
        comm.wait(*c)

    return pl.pallas_call(body, in_specs=comm.in_specs, out_specs=comm.out_specs, out_shape=comm.out_shape,
                          scratch_shapes=comm.scratch, compiler_params=pltpu.CompilerParams(has_side_effects=True),
                          name=name)(*arrs)


def _all_gather(arrs, name):
    return _comm_call("gather", arrs, name)


def _exchange(arrs, name):
    return _comm_call("exchange", arrs, name)


def _sum_slots(parts, name):
    _, r, c = parts.shape
    tr = r if r <= 512 else 512

    def body(p_ref, o_ref):
        acc = p_ref[0].astype(F32)
        for q in range(1, NDEV):
            acc = acc + p_ref[q].astype(F32)
        o_ref[...] = acc

    return pl.pallas_call(
        body, grid=(r // tr,), in_specs=[pl.BlockSpec((NDEV, tr, c), lambda i: (0, i, 0))],
        out_specs=pl.BlockSpec((tr, c), lambda i: (i, 0)), out_shape=SDS((r, c), F32),
        compiler_params=_cp("parallel"), name=name)(parts)


def _adamw(g, w, m, v, name):
    r, c = w.shape
    parts = g.ndim == 3
    tr = r
    for cand in (512, 256, 128, 64, 32, 16, 8):
        if r > cand and r % cand == 0 and cand * c * 4 <= 2 * 1024 * 1024:
            tr = cand
            break
    bc1 = 1.0 / (1.0 - ADAM_B1 ** ADAM_STEP)
    bc2 = 1.0 / (1.0 - ADAM_B2 ** ADAM_STEP)

    def body(g_ref, w_ref, m_ref, v_ref, go_ref, d_ref, mo_ref, vo_ref):
        if parts:
            gv = g_ref[0].astype(F32)
            for q in range(1, NDEV):
                gv = gv + g_ref[q].astype(F32)
        else:
            gv = g_ref[...]
        mn = ADAM_B1 * m_ref[...] + (1.0 - ADAM_B1) * gv
        vn = ADAM_B2 * v_ref[...] + (1.0 - ADAM_B2) * (gv * gv)
        go_ref[...] = gv
        mo_ref[...] = mn
        vo_ref[...] = vn
        d_ref[...] = -ADAM_LR * ((mn * bc1) / (jnp.sqrt(vn * bc2) + ADAM_EPS) + ADAM_WD * w_ref[...])

    spec = pl.BlockSpec((tr, c), lambda i: (i, 0))
    gspec = pl.BlockSpec((NDEV, tr, c), lambda i: (0, i, 0)) if parts else spec
    return pl.pallas_call(
        body, grid=(r // tr,), in_specs=[gspec, spec, spec, spec], out_specs=[spec] * 4,
        out_shape=[SDS((r, c), F32)] * 4, compiler_params=_cp("parallel"), name=name)(g, w, m, v)


def _pad_in_cols(w):
    r = w.shape[0]
    zeros = lambda n: jnp.zeros((r, n), w.dtype)
    return jnp.concatenate([w[:, :2304], w[:, 2308:2692], w[:, 2304:2308], zeros(28), w[:, 2692:2724], zeros(64)], axis=1)


def _unpad_in_cols(w):
    return jnp.concatenate([w[..., :2304], w[..., 2688:2692], w[..., 2304:2688], w[..., 2720:2752]], axis=-1)


def _pad_uq(w):
    return jnp.pad(w.reshape(256, N_HEADS, 96), ((0, 0), (0, 0), (0, 32))).reshape(256, 512)


def _unpad_uq(w):
    return w.reshape(256, N_HEADS, 128)[:, :, :96].reshape(256, 384)


def _split_ukv(w):
    r = w.reshape(128, N_HEADS, 128)
    return jnp.pad(r[:, :, :64], ((0, 0), (0, 0), (0, 64))).reshape(128, 512), r[:, :, 64:].reshape(128, 256)


def _join_ukv(dk, dv):
    return jnp.concatenate([dk.reshape(128, N_HEADS, 128)[:, :, :64], dv.reshape(128, N_HEADS, 64)], axis=-1).reshape(128, 512)


def _cols_to_full(g):
    return jnp.transpose(g, (1, 0, 2)).reshape(g.shape[1], NDEV * g.shape[2])


def kernel(x, g_mix_norm, w_in, b_forget, g_sgu, w_spatial, b_spatial, g_mla_q, w_uq, g_mla_kv, w_ukv, g_group_out, w_out, g_ffn_norm, w_up, w_down, g_final, loss_target, m_g_mix_norm, m_w_in, m_b_forget, m_g_sgu, m_w_spatial, m_b_spatial, m_g_mla_q, m_w_uq, m_g_mla_kv, m_w_ukv, m_g_group_out, m_w_out, m_g_ffn_norm, m_w_up, m_w_down, m_g_final, v_g_mix_norm, v_w_in, v_b_forget, v_g_sgu, v_w_spatial, v_b_spatial, v_g_mla_q, v_w_uq, v_g_mla_kv, v_w_ukv, v_g_group_out, v_w_out, v_g_ffn_norm, v_w_up, v_w_down, v_g_final):
    depth = w_in.shape[0]
    s, d = x.shape[1], x.shape[2]
    x0 = x.reshape(s, d)
    target = loss_target.reshape(s, d)
    tb = _tables(s)
    me = 4 * lax.axis_index("x") + 2 * lax.axis_index("y") + lax.axis_index("c")

    assert depth == 2
    shards = {}
    for l in range(depth):
        shards.update({(l, "w_in"): _pad_in_cols(w_in[l]).astype(_WIRE), (l, "w_out"): w_out[l].astype(_WIRE),
                       (l, "w_up"): w_up[l].astype(_WIRE), (l, "w_down"): w_down[l].astype(_WIRE),
                       (l, "w_uq"): w_uq[l].astype(_WIRE), (l, "w_ukv"): w_ukv[l].astype(_WIRE)})
    wts = _ShardedWeights(shards)
    wts.full[(0, "w_in")] = _all_gather([shards[(0, "w_in")]], "gather_w_in0")[0]

    row = lambda a: a.reshape(1, -1)

    def small(l):
        bf = jnp.pad(b_forget[l].reshape(1, N_HEADS), ((0, 0), (0, 128 - N_HEADS)))
        bt = jnp.pad(b_spatial[l].T, ((0, 0), (0, 128 - N_HEADS)))
        return dict(g_mix=row(g_mix_norm[l]), g_sgu=row(g_sgu[l]), w_s=w_spatial[l], b_t=bt, b_f=bf, gq=row(g_mla_q[l]),
                    gkv=row(g_mla_kv[l]), g_go=row(g_group_out[l]), g_ffn=row(g_ffn_norm[l]))

    smalls = [small(l) for l in range(depth)]
    lrow, dx, sm, dg_final = _local_step(x0, target, wts, smalls, row(g_final), tb)
    loss = lax.psum(jnp.sum(lrow), AXES)
    grad_x = dx.reshape(1, s, d)
    wts.recv[(0, "w_in")] = _exchange([wts.grads[(0, "w_in")]], "scatter_dw_in0")[0]
    return _reduce_and_update(loss, grad_x, wts.recv, sm, dg_final, me, dict(
        g_mix_norm=(g_mix_norm, m_g_mix_norm, v_g_mix_norm), w_in=(w_in, m_w_in, v_w_in),
        b_forget=(b_forget, m_b_forget, v_b_forget), g_sgu=(g_sgu, m_g_sgu, v_g_sgu),
        w_spatial=(w_spatial, m_w_spatial, v_w_spatial), b_spatial=(b_spatial, m_b_spatial, v_b_spatial),
        g_mla_q=(g_mla_q, m_g_mla_q, v_g_mla_q), w_uq=(w_uq, m_w_uq, v_w_uq), g_mla_kv=(g_mla_kv, m_g_mla_kv, v_g_mla_kv),
        w_ukv=(w_ukv, m_w_ukv, v_w_ukv), g_group_out=(g_group_out, m_g_group_out, v_g_group_out),
        w_out=(w_out, m_w_out, v_w_out), g_ffn_norm=(g_ffn_norm, m_g_ffn_norm, v_g_ffn_norm), w_up=(w_up, m_w_up, v_w_up),
        w_down=(w_down, m_w_down, v_w_down), g_final=(g_final, m_g_final, v_g_final)))


_GATHER_AT = {
    "in_proj0": [(0, "w_up")],
    "fox_attn0": [(0, "w_uq"), (0, "w_ukv"), (0, "w_down")],
    "mla_attn0": [(0, "w_out"), (1, "w_in")],
    "ffn_fwd0": [(1, "w_uq"), (1, "w_ukv"), (1, "w_down")],
    "in_proj1": [(1, "w_up")],
    "fox_attn1": [(1, "w_out")],
}
_SCATTER_AT = {
    "fox_attn_bwd1": [(1, "w_down")],
    "mla_attn_bwd1": [(1, "w_up"), (1, "w_out")],
    "ffn_bwd0": [(1, "w_in")],
    "fox_attn_bwd0": [(0, "w_down")],
    "mla_attn_bwd0": [(0, "w_up"), (0, "w_out")],
}


class _FullWeights:
    def __init__(self, per_layer):
        self.per_layer, self.grads = per_layer, {}

    def get(self, l, name):
        return self.per_layer[l][name]

    def comm(self, host):
        return None

    def done(self, host, results):
        pass

    def grad(self, l, name, blocks):
        self.grads[(l, name)] = blocks


class _ShardedWeights(_FullWeights):
    def __init__(self, shards):
        self.shards, self.full, self.grads, self.recv = shards, {}, {}, {}

    def get(self, l, name):
        if name in ("wk", "wv"):
            return _split_ukv(_cols_to_full(self.full[(l, "w_ukv")]))[0 if name == "wk" else 1]
        if name == "wq":
            return _pad_uq(_cols_to_full(self.full[(l, "w_uq")]))
        g = self.full[(l, name)]
        return g if name == "w_up" else g.reshape(NDEV * g.shape[1], g.shape[2])

    def comm(self, host):
        if host in _GATHER_AT:
            return _Comm("gather", [self.shards[k] for k in _GATHER_AT[host]])
        if host in _SCATTER_AT:
            return _Comm("exchange", [self.grads[k] for k in _SCATTER_AT[host]])
        return None

    def done(self, host, results):
        if host in _GATHER_AT:
            self.full.update(zip(_GATHER_AT[host], results))
        if host in _SCATTER_AT:
            self.recv.update(zip(_SCATTER_AT[host], results))


def _local_step(x0, target, wts, smalls, g_final, tb):
    depth = len(smalls)
    s, d = x0.shape
    saved = []
    xl = x0
    for l in range(depth):
        p = smalls[l]
        z, h, got = _norm_matmul(xl, p["g_mix"], wts.get(l, "w_in"), f"in_proj{l}", wts.comm(f"in_proj{l}"))
        wts.done(f"in_proj{l}", got)
        ya = _sgu_fwd(z, p["g_sgu"], p["w_s"], p["b_t"], f"sgu_fwd{l}")
        yb, ret, states = _ret_fwd(z, tb, f"ret_fwd{l}")
        cum = _fox_prep(z, p["b_f"], f"fox_prep{l}")
        kc, vc, vtc = _kv_prep(z, 7, 8, f"fox_kv{l}")
        yc, lse_c, got = _attn_fwd(z, 6, HEAD_DIM, kc, vtc, HEAD_DIM ** -0.5, cum, f"fox_attn{l}", wts.comm(f"fox_attn{l}"))
        wts.done(f"fox_attn{l}", got)
        wq, wk, wv = wts.get(l, "wq"), wts.get(l, "wk"), wts.get(l, "wv")
        qd, kd, vd, vtd, cqn, ckvn = _mla_prep(z, p["gq"], p["gkv"], wq, wk, wv, tb, f"mla_prep{l}")
        yd, lse_d, got = _attn_fwd(qd, 0, 128, kd, vtd, _SCALE_D, None, f"mla_attn{l}", wts.comm(f"mla_attn{l}"))
        wts.done(f"mla_attn{l}", got)
        ys = (ya, yb, yc, yd)
        x1, yn = _out_proj(ys, p["g_go"], wts.get(l, "w_out"), xl, f"out_proj{l}")
        x2, u, h2, got = _ffn_fwd(x1, p["g_ffn"], wts.get(l, "w_up"), wts.get(l, "w_down"), f"ffn_fwd{l}", wts.comm(f"ffn_fwd{l}"))
        wts.done(f"ffn_fwd{l}", got)
        saved.append(dict(x=xl, z=z, h=h, ys=ys, ret=ret, states=states, cum=cum, lse_c=lse_c, kc=kc, vc=vc, qd=qd, kd=kd, vd=vd,
                          cqn=cqn, ckvn=ckvn, lse_d=lse_d, x1=x1, yn=yn, u=u, h2=h2, wq=wq, wk=wk, wv=wv))
        xl = x2

    lrow, dx, dg_final = _loss_head(xl, g_final, target, "loss_head")

    sm = [None] * depth
    for l in reversed(range(depth)):
        p, a = smalls[l], saved[l]
        dx1, du, dg_ffn, got = _ffn_bwd(dx, a["x1"], a["u"], p["g_ffn"], wts.get(l, "w_up"), wts.get(l, "w_down"), f"ffn_bwd{l}",
                                        wts.comm(f"ffn_bwd{l}"))
        wts.done(f"ffn_bwd{l}", got)
        dw_down = _mm_tn(a["u"], dx, f"dw_down{l}", a_fn=lambda t: jnp.square(jnp.maximum(t, 0.0)), out_dtype=_WIRE)
        wts.grad(l, "w_down", dw_down.reshape(NDEV, dw_down.shape[0] // NDEV, d))
        wts.grad(l, "w_up", _mm_tn(a["h2"], du, f"dw_up{l}", blocked=True, out_dtype=_WIRE))
        dya, dyb, dyc, dyd, dg_go = _out_proj_bwd(dx1, wts.get(l, "w_out"), a["ys"], p["g_go"], f"out_proj_bwd{l}")
        wts.grad(l, "w_out", _mm_tn(a["yn"], dx1, f"dw_out{l}", out_dtype=_WIRE).reshape(NDEV, d // NDEV, d))
        dz_a, dg_sgu, dw_s, db_t = _sgu_bwd(dya, a["z"], p["g_sgu"], p["w_s"], p["b_t"], f"sgu_bwd{l}")
        dz_b = _ret_bwd(dyb, a["z"], a["ret"], a["states"], tb, f"ret_bwd{l}")
        qt, dot, dl = _attn_bwd_prep(a["z"], 6, HEAD_DIM, a["ys"][2], dyc, f"fox_bwd_prep{l}")
        dqt_c, dk_c, dv_c, dck, dcq, got = _attn_bwd(a["kc"], a["vc"], qt, dot, a["lse_c"], dl, HEAD_DIM,
                                                     HEAD_DIM ** -0.5, a["cum"], f"fox_attn_bwd{l}", _MXU,
                                                     wts.comm(f"fox_attn_bwd{l}"))
        wts.done(f"fox_attn_bwd{l}", got)
        dq_c = _untranspose(dqt_c, _MXU, f"fox_dq{l}")
        qt, dot, dl = _attn_bwd_prep(a["qd"], 0, 128, a["ys"][3], dyd, f"mla_bwd_prep{l}")
        dqt_d, dk_d, dv_d, got = _attn_bwd(a["kd"], a["vd"], qt, dot, a["lse_d"], dl, 128, _SCALE_D, None,
                                           f"mla_attn_bwd{l}", F32, wts.comm(f"mla_attn_bwd{l}"))
        wts.done(f"mla_attn_bwd{l}", got)
        dq_d = _untranspose(dqt_d, F32, f"mla_dq{l}")
        dz_cq, dz_ckv, dkr, dwq, dwk, dwv, dgq, dgkv = _mla_prep_bwd(dq_d, dk_d, dv_d, a["z"], a["cqn"], a["ckvn"], p["gq"],
                                                                     p["gkv"], a["wq"], a["wk"], a["wv"], tb, f"mla_prep_bwd{l}")
        dz_misc, db_f = _fox_post(dcq, dck, a["z"], p["b_f"], dkr, f"fox_post{l}")
        dz = jnp.concatenate([dz_a, dz_b, dq_c, dk_c, dv_c, dz_cq, dz_ckv, dz_misc], axis=1)
        dx, dg_mix = _in_proj_bwd(dz, wts.get(l, "w_in"), a["x"], p["g_mix"], dx1, f"in_proj_bwd{l}")
        wts.grad(l, "w_in", _unpad_in_cols(_mm_tn(a["h"], dz, f"dw_in{l}", out_dtype=_WIRE)).reshape(NDEV, d // NDEV, N_IN))
        sm[l] = [dg_mix, dg_go, dg_ffn, dg_sgu, dw_s, db_t[:, :N_HEADS].T, db_f[0, :N_HEADS], dgq, dgkv, _unpad_uq(dwq),
                 _join_ukv(dwk, dwv)]
    return lrow, dx, sm, dg_final


def _reduce_and_update(loss, grad_x, recv, sm, dg_final, me, given):
    depth = len(sm)
    pieces = [t for l in range(depth) for t in sm[l]] + [dg_final]
    flat = jnp.concatenate([t.reshape(-1) for t in pieces])
    n_flat = flat.shape[0]
    unit = NDEV * 8 * 128
    n_pad = -(-n_flat // unit) * unit
    packed = jnp.pad(flat, (0, n_pad - n_flat)).reshape(NDEV, n_pad // (NDEV * 128), 128)
    red = _sum_slots(_exchange([packed], "scatter_small")[0], "sum_small")
    full = _all_gather([red], "gather_small")[0].reshape(-1)
    offs = np.cumsum([0] + [int(np.prod(t.shape)) for t in pieces])
    red_pieces = [full[int(offs[i]):int(offs[i + 1])].reshape(pieces[i].shape) for i in range(len(pieces))]
    per = len(sm[0])
    stack = lambda i: jnp.stack([red_pieces[l * per + i] for l in range(depth)])
    g_small = dict(g_mix_norm=stack(0), g_group_out=stack(1), g_ffn_norm=stack(2), g_sgu=stack(3), w_spatial=stack(4),
                   b_spatial=stack(5), b_forget=stack(6), g_mla_q=stack(7), g_mla_kv=stack(8), g_final=red_pieces[-1])
    cq, ckv = given["w_uq"][0].shape[2], given["w_ukv"][0].shape[2]
    g_small["w_uq"] = lax.dynamic_slice_in_dim(stack(9), me * cq, cq, axis=2)
    g_small["w_ukv"] = lax.dynamic_slice_in_dim(stack(10), me * ckv, ckv, axis=2)

    names = list(given)
    outs = {}
    for nme in names:
        wv_, mv_, vv_ = given[nme]
        shape = wv_.shape
        if nme in ("w_in", "w_out", "w_up", "w_down"):
            res = []
            for l in range(depth):
                parts = recv[(l, nme)]
                two = lambda t: t[l].reshape(-1, shape[-1])
                res.append(_adamw(parts, two(wv_), two(mv_), two(vv_), f"adamw_{nme}{l}"))
            outs[nme] = [jnp.stack([res[l][i] for l in range(depth)]).reshape(shape) for i in range(4)]
        else:
            two = lambda t: t.reshape(-1, shape[-1]) if t.ndim > 1 else t.reshape(1, -1)
            res = _adamw(two(g_small[nme]), two(wv_), two(mv_), two(vv_), f"adamw_{nme}")
            outs[nme] = [r.reshape(shape) for r in res]
    return (loss, grad_x, *[outs[n][0] for n in names], *[outs[n][1] for n in names], *[outs[n][2] for n in names],
            *[outs[n][3] for n in names])
```

```python
import functools

import jax
import jax.numpy as jnp
import numpy as np
from jax import lax
from jax.experimental import pallas as pl
from jax.experimental.pallas import tpu as pltpu

F32 = jnp.float32
_MXU = jnp.bfloat16
_WIRE = jnp.bfloat16
EPS = 1e-6
NDEV = 8
AXES = ("x", "y", "c")
MESH = pl.DeviceIdType.MESH

N_HEADS = 4
HEAD_DIM = 64
GROUP = 256
CHUNK = 128
NZ = 2816
N_IN = 2724
MISC_F, MISC_KR = 0, 32
VMEM_LIMIT = 56 * 1024 * 1024

ADAM_LR, ADAM_B1, ADAM_B2, ADAM_EPS, ADAM_WD, ADAM_STEP = 0.001, 0.9, 0.999, 1e-08, 0.01, 10

SDS = jax.ShapeDtypeStruct


def _cp(*sem):
    return pltpu.CompilerParams(dimension_semantics=sem, vmem_limit_bytes=VMEM_LIMIT)


def _dot(a, b):
    return jnp.dot(a.astype(_MXU), b.astype(_MXU), preferred_element_type=F32)


def _dot_nt(a, b):
    return lax.dot_general(a.astype(_MXU), b.astype(_MXU), (((1,), (1,)), ((), ())), preferred_element_type=F32)


def _dot_tn(a, b):
    return lax.dot_general(a.astype(_MXU), b.astype(_MXU), (((0,), (0,)), ((), ())), preferred_element_type=F32)


def _dot_exact(a, b, dims=(((1,), (0,)), ((), ()))):
    return lax.dot_general(a, b, dims, precision=lax.Precision.HIGHEST, preferred_element_type=F32)


def _rms(x, g):
    return x * lax.rsqrt(jnp.mean(x * x, axis=-1, keepdims=True) + EPS) * g


def _rms_bwd(x, g, dy):
    xh = x * lax.rsqrt(jnp.mean(x * x, axis=-1, keepdims=True) + EPS)
    dxh = dy * g
    r = lax.rsqrt(jnp.mean(x * x, axis=-1, keepdims=True) + EPS)
    dx = r * (dxh - xh * jnp.mean(dxh * xh, axis=-1, keepdims=True))
    return dx, jnp.sum(dy * xh, axis=0, keepdims=True)


def _standardize(t):
    mu = jnp.mean(t, axis=-1, keepdims=True)
    tc = t - mu
    rs = lax.rsqrt(jnp.mean(tc * tc, axis=-1, keepdims=True) + EPS)
    return tc * rs, rs


def _standardize_bwd(yh, rs, dy):
    return rs * (dy - jnp.mean(dy, axis=-1, keepdims=True) - yh * jnp.mean(dy * yh, axis=-1, keepdims=True))


_GELU_C = 0.7978845608028654


def _gelu(x):
    return 0.5 * x * (1.0 + jnp.tanh(_GELU_C * (x + 0.044715 * x * x * x)))


def _gelu_grad(x):
    t = jnp.tanh(_GELU_C * (x + 0.044715 * x * x * x))
    return 0.5 * (1.0 + t) + 0.5 * x * (1.0 - t * t) * _GELU_C * (1.0 + 3 * 0.044715 * x * x)


def _sigmoid(x):
    return 1.0 / (1.0 + jnp.exp(-x))


def _swap_half(t, half):
    n = t.shape[-1]
    lane = lax.broadcasted_iota(jnp.int32, t.shape, t.ndim - 1)
    return jnp.where((lane % (2 * half)) < half, pltpu.roll(t, n - half, t.ndim - 1), pltpu.roll(t, half, t.ndim - 1))


def _rope(t, cos, sin, half):
    return t * cos + _swap_half(t, half) * sin


def _rope_bwd(d, cos, sin, half):
    return d * cos - _swap_half(d, half) * sin


def _tables(s):
    pos = jnp.arange(s, dtype=F32)[:, None]

    def cs(half):
        inv = jnp.power(10000.0, -jnp.arange(half, dtype=F32) / half)
        ang = pos * inv[None, :]
        return jnp.cos(ang), jnp.sin(ang)

    c32, s32 = cs(32)
    c16, s16 = cs(16)
    z = lambda w: jnp.zeros((s, w), F32)
    o = lambda w: jnp.ones((s, w), F32)
    t = {}
    t["b_cos"] = jnp.tile(jnp.concatenate([c32, c32], 1), (1, 4))
    t["b_sin"] = jnp.tile(jnp.concatenate([-s32, s32], 1), (1, 4))
    t["q_cos"] = jnp.tile(jnp.concatenate([o(64), c16, c16, z(32)], 1), (1, 4))
    t["q_sin"] = jnp.tile(jnp.concatenate([z(64), -s16, s16, z(32)], 1), (1, 4))
    t["k_cos"] = jnp.concatenate([z(32), c16, c16, z(64)], 1)
    t["k_sin"] = jnp.concatenate([z(32), -s16, s16, z(64)], 1)
    lg = jnp.log1p(-jnp.exp2(-5.0 - jnp.arange(N_HEADS, dtype=F32)))
    j = jnp.arange(CHUNK, dtype=F32)
    rel = j[:, None] - j[None, :]
    t["decay"] = jnp.where(rel[None] >= 0, jnp.exp(jnp.maximum(rel, 0.0)[None] * lg[:, None, None]), 0.0)

    def rows(e):
        return jnp.repeat(e.T, HEAD_DIM, axis=1)

    t["qw"] = rows(jnp.exp((j + 1.0)[None, :] * lg[:, None]))
    t["kw"] = rows(jnp.exp((CHUNK - 1 - j)[None, :] * lg[:, None]))
    t["kw2"] = rows(jnp.exp((CHUNK - j)[None, :] * lg[:, None]))
    t["qw0"] = rows(jnp.exp(j[None, :] * lg[:, None]))
    t["cd"] = jnp.repeat(jnp.exp(CHUNK * lg), HEAD_DIM)[None, :]
    e = np.zeros((128, 512), np.float32)
    for h in range(N_HEADS):
        for r in range(32):
            e[MISC_KR + r, 128 * h + 64 + r] = 1.0
    t["place"] = jnp.asarray(e)
    return t


def _norm_matmul(x, g, w, name, comm=None):
    s, d = x.shape
    n = w.shape[1]
    tm, tn = min(512, s), 256
    ni = s // tm

    def body(*refs):
        (x_ref, g_ref, w_ref), (z_ref, h_ref), _, cc = _split_refs(refs, 3, 2, comm)
        i = pl.program_id(0)
        if comm is not None:
            @pl.when(i == 0)
            def _():
                comm.start(*cc)

        h = _rms(x_ref[...], g_ref[...]).astype(h_ref.dtype)
        h_ref[...] = h
        for j in range(n // tn):
            z_ref[:, tn * j:tn * (j + 1)] = jnp.dot(h, w_ref[:, tn * j:tn * (j + 1)], preferred_element_type=F32)
        if comm is not None:
            @pl.when(i == ni - 1)
            def _():
                comm.wait(*cc)

    in_specs = [pl.BlockSpec((tm, d), lambda i: (i, 0)), pl.BlockSpec((1, d), lambda i: (0, 0)),
                pl.BlockSpec((d, n), lambda i: (0, 0))]
    out_specs = [pl.BlockSpec((tm, n), lambda i: (i, 0)), pl.BlockSpec((tm, d), lambda i: (i, 0))]
    out_shape = [SDS((s, n), F32), SDS((s, d), _MXU)]
    return _call_with_comm(body, (ni,), in_specs, out_specs, out_shape, [], [x, g, w], comm, ("arbitrary",), name)


def _mm_tn(a, b, name, *, a_fn=None, blocked=False, out_dtype=F32):
    k, m = a.shape
    n = b.shape[1]
    tm, tk = min(1024, m), min(512, k)
    tn = next(t for t in (1408, 1024, 512, 256, 128) if n % t == 0)
    assert m % tm == 0 and k % tk == 0
    nk = k // tk

    def body(a_ref, b_ref, o_ref, acc):
        kk = pl.program_id(2)

        @pl.when(kk == 0)
        def _():
            acc[...] = jnp.zeros_like(acc)

        av = a_ref[...]
        if a_fn is not None:
            av = a_fn(av.astype(F32))
        acc[...] += _dot_tn(av, b_ref[...])

        @pl.when(kk == nk - 1)
        def _():
            if blocked:
                for c in range(tn // 512):
                    o_ref[c] = acc[:, 512 * c:512 * (c + 1)].astype(o_ref.dtype)
            else:
                o_ref[...] = acc[...].astype(o_ref.dtype)

    if blocked:
        assert tn % 512 == 0
        out_spec = pl.BlockSpec((tn // 512, tm, 512), lambda i, j, kk: (j, i, 0))
        out_shape = SDS((n // 512, m, 512), out_dtype)
    else:
        out_spec = pl.BlockSpec((tm, tn), lambda i, j, kk: (i, j))
        out_shape = SDS((m, n), out_dtype)
    return pl.pallas_call(
        body, grid=(m // tm, n // tn, nk),
        in_specs=[pl.BlockSpec((tk, tm), lambda i, j, kk: (kk, i)), pl.BlockSpec((tk, tn), lambda i, j, kk: (kk, j))],
        out_specs=out_spec, out_shape=out_shape, scratch_shapes=[pltpu.VMEM((tm, tn), F32)],
        compiler_params=_cp("parallel", "parallel", "arbitrary"), name=name)(a, b)


def _sgu_parts(u_pre, v_pre, gain):
    u = _gelu(u_pre)
    v = _gelu(v_pre)
    vh, rs, vg = [], [], []
    for h in range(N_HEADS):
        sl = slice(HEAD_DIM * h, HEAD_DIM * (h + 1))
        a, r = _standardize(v[:, sl])
        vh.append(a)
        rs.append(r)
        vg.append(a * gain[:, sl])
    return u, vh, rs, vg


def _tril(w):
    r = lax.broadcasted_iota(jnp.int32, w.shape, 0)
    c = lax.broadcasted_iota(jnp.int32, w.shape, 1)
    return jnp.where(r >= c, w, 0.0)


def _sgu_fwd(z, gain, w_s, b_t, name):
    s = z.shape[0]
    tm = min(512, s)

    def body(u_ref, v_ref, g_ref, w_ref, b_ref, y_ref):
        u, _, _, vg = _sgu_parts(u_ref[...], v_ref[...], g_ref[...])
        hc = [(h, c) for h in range(N_HEADS) for c in range(tm // CHUNK)]
        wcs = [_tril(w_ref[h]) for h in range(N_HEADS)]
        mixed = {(h, c): _dot(wcs[h], vg[h][CHUNK * c:CHUNK * (c + 1)]) for h, c in hc}
        for h, c in hc:
            r, sl = slice(CHUNK * c, CHUNK * (c + 1)), slice(HEAD_DIM * h, HEAD_DIM * (h + 1))
            y_ref[r, sl] = u[r, sl] * (mixed[h, c] + b_ref[:, h:h + 1])

    return pl.pallas_call(
        body, grid=(s // tm,),
        in_specs=[pl.BlockSpec((tm, GROUP), lambda i: (i, 0)), pl.BlockSpec((tm, GROUP), lambda i: (i, 1)),
                  pl.BlockSpec((1, GROUP), lambda i: (0, 0)), pl.BlockSpec((N_HEADS, CHUNK, CHUNK), lambda i: (0, 0, 0)),
                  pl.BlockSpec((CHUNK, 128), lambda i: (0, 0))],
        out_specs=pl.BlockSpec((tm, GROUP), lambda i: (i, 0)), out_shape=SDS((s, GROUP), F32),
        compiler_params=_cp("parallel"), name=name)(z, z, gain, w_s, b_t)


def _sgu_bwd(dy, z, gain, w_s, b_t, name):
    s = z.shape[0]
    tm = min(512, s)

    def body(dy_ref, u_ref, v_ref, g_ref, w_ref, b_ref, dz_ref, dg_ref, dw_ref, db_ref):
        @pl.when(pl.program_id(0) == 0)
        def _():
            dg_ref[...] = jnp.zeros_like(dg_ref)
            dw_ref[...] = jnp.zeros_like(dw_ref)
            db_ref[...] = jnp.zeros_like(db_ref)

        u_pre, v_pre, gain_v = u_ref[...], v_ref[...], g_ref[...]
        u, vh, rs, vg = _sgu_parts(u_pre, v_pre, gain_v)
        dyv = dy_ref[...]
        gu = _gelu_grad(u_pre)
        gv = _gelu_grad(v_pre)
        hc = [(h, c) for h in range(N_HEADS) for c in range(tm // CHUNK)]
        sls = [slice(HEAD_DIM * h, HEAD_DIM * (h + 1)) for h in range(N_HEADS)]
        rws = [slice(CHUNK * c, CHUNK * (c + 1)) for c in range(tm // CHUNK)]
        wcs = [_tril(w_ref[h]) for h in range(N_HEADS)]
        mixed = {(h, c): _dot(wcs[h], vg[h][rws[c]]) for h, c in hc}
        dms = {}
        for h, c in hc:
            r, sl = rws[c], sls[h]
            dz_ref[r, sl] = (dyv[r, sl] * (mixed[h, c] + b_ref[:, h:h + 1]) * gu[r, sl]).astype(dz_ref.dtype)
            dms[h, c] = dyv[r, sl] * u[r, sl]
        dws = {(h, c): _dot_nt(dms[h, c], vg[h][rws[c]]) for h, c in hc}
        dvgs = {(h, c): _dot_tn(wcs[h], dms[h, c]) for h, c in hc}
        for h in range(N_HEADS):
            sl = sls[h]
            dwh = jnp.zeros((CHUNK, CHUNK), F32)
            dbh = jnp.zeros((CHUNK, 1), F32)
            dgh = jnp.zeros((1, HEAD_DIM), F32)
            for c in range(tm // CHUNK):
                r = rws[c]
                dwh += dws[h, c]
                dbh += jnp.sum(dms[h, c], axis=1, keepdims=True)
                dvg = dvgs[h, c]
                dgh += jnp.sum(dvg * vh[h][r], axis=0, keepdims=True)
                dv = _standardize_bwd(vh[h][r], rs[h][r], dvg * gain_v[:, sl])
                dz_ref[r, GROUP + HEAD_DIM * h:GROUP + HEAD_DIM * (h + 1)] = (dv * gv[r, sl]).astype(dz_ref.dtype)
            dw_ref[h] += _tril(dwh)
            db_ref[:, h:h + 1] += dbh
            dg_ref[:, sl] += dgh

    return pl.pallas_call(
        body, grid=(s // tm,),
        in_specs=[pl.BlockSpec((tm, GROUP), lambda i: (i, 0)),
                  pl.BlockSpec((tm, GROUP), lambda i: (i, 0)), pl.BlockSpec((tm, GROUP), lambda i: (i, 1)),
                  pl.BlockSpec((1, GROUP), lambda i: (0, 0)), pl.BlockSpec((N_HEADS, CHUNK, CHUNK), lambda i: (0, 0, 0)),
                  pl.BlockSpec((CHUNK, 128), lambda i: (0, 0))],
        out_specs=[pl.BlockSpec((tm, 2 * GROUP), lambda i: (i, 0)), pl.BlockSpec((1, GROUP), lambda i: (0, 0)),
                   pl.BlockSpec((N_HEADS, CHUNK, CHUNK), lambda i: (0, 0, 0)), pl.BlockSpec((CHUNK, 128), lambda i: (0, 0))],
        out_shape=[SDS((s, 2 * GROUP), _MXU), SDS((1, GROUP), F32), SDS((N_HEADS, CHUNK, CHUNK), F32), SDS((CHUNK, 128), F32)],
        compiler_params=_cp("arbitrary"), name=name)(dy, z, z, gain, w_s, b_t)


_SCALE_B = HEAD_DIM ** -0.5


def _ret_fwd(z, tb, name):
    s = z.shape[0]
    nc = s // CHUNK
    row = lambda col: pl.BlockSpec((CHUNK, GROUP), lambda n, col=col: (n, col))
    const = lambda shape: pl.BlockSpec(shape, lambda n: (0,) * len(shape))

    def body(q_ref, k_ref, v_ref, g_ref, cos_ref, sin_ref, dec_ref, qw_ref, kw_ref, cd_ref, y_ref, o_ref, st_ref, state):
        @pl.when(pl.program_id(0) == 0)
        def _():
            state[...] = jnp.zeros_like(state)

        q = _rope(q_ref[...], cos_ref[...], sin_ref[...], 32)
        k = _rope(k_ref[...], cos_ref[...], sin_ref[...], 32) * _SCALE_B
        v = v_ref[...]
        g = g_ref[...]
        st_ref[0] = state[...]
        qs = q * qw_ref[...]
        ks = k * kw_ref[...]
        sls = [slice(HEAD_DIM * h, HEAD_DIM * (h + 1)) for h in range(N_HEADS)]
        scs = [_dot_nt(q[:, sl], k[:, sl]) for sl in sls]
        crs = [_dot(qs[:, sl], state[:, sl]) for sl in sls]
        kvs = [_dot_tn(ks[:, sl], v[:, sl]) for sl in sls]
        scd = [(scs[h] * dec_ref[h]).astype(_MXU) for h in range(N_HEADS)]
        ins = [_dot(scd[h], v[:, sls[h]]) for h in range(N_HEADS)]
        for h, sl in enumerate(sls):
            o = ins[h] + crs[h]
            o_ref[:, sl] = o
            yh, _ = _standardize(o)
            gh = g[:, sl]
            y_ref[:, sl] = gh * _sigmoid(gh) * yh
            state[:, sl] = cd_ref[:, sl] * state[:, sl] + kvs[h]

    return pl.pallas_call(
        body, grid=(nc,),
        in_specs=[row(2), row(3), row(4), row(5), pl.BlockSpec((CHUNK, GROUP), lambda n: (n, 0)),
                  pl.BlockSpec((CHUNK, GROUP), lambda n: (n, 0)), const((N_HEADS, CHUNK, CHUNK)),
                  const((CHUNK, GROUP)), const((CHUNK, GROUP)), const((1, GROUP))],
        out_specs=[pl.BlockSpec((CHUNK, GROUP), lambda n: (n, 0)), pl.BlockSpec((CHUNK, GROUP), lambda n: (n, 0)),
                   pl.BlockSpec((1, HEAD_DIM, GROUP), lambda n: (n, 0, 0))],
        out_shape=[SDS((s, GROUP), F32), SDS((s, GROUP), F32), SDS((nc, HEAD_DIM, GROUP), F32)],
        scratch_shapes=[pltpu.VMEM((HEAD_DIM, GROUP), F32)],
        compiler_params=_cp("arbitrary"), name=name)(z, z, z, z, tb["b_cos"], tb["b_sin"], tb["decay"], tb["qw"], tb["kw"], tb["cd"])


def _ret_bwd(dy, z, o_pre, states, tb, name):
    s = z.shape[0]
    nc = s // CHUNK
    rev = lambda col: pl.BlockSpec((CHUNK, GROUP), lambda n, col=col: (nc - 1 - n, col))
    const = lambda shape: pl.BlockSpec(shape, lambda n: (0,) * len(shape))

    def body(dy_ref, q_ref, k_ref, v_ref, g_ref, o_ref, st_ref, cos_ref, sin_ref, dec_ref, qw_ref, kw2_ref, qw0_ref, cd_ref,
             dz_ref, rstate):
        @pl.when(pl.program_id(0) == 0)
        def _():
            rstate[...] = jnp.zeros_like(rstate)

        cos, sin = cos_ref[...], sin_ref[...]
        q = _rope(q_ref[...], cos, sin, 32)
        k = _rope(k_ref[...], cos, sin, 32) * _SCALE_B
        v = v_ref[...]
        g = g_ref[...]
        dyv = dy_ref[...]
        sg = _sigmoid(g)
        silu = g * sg
        dos, dgs = [], []
        for h in range(N_HEADS):
            sl = slice(HEAD_DIM * h, HEAD_DIM * (h + 1))
            yh, rs = _standardize(o_ref[:, sl])
            dgs.append(dyv[:, sl] * yh * (sg[:, sl] * (1.0 + g[:, sl] * (1.0 - sg[:, sl]))))
            dos.append(_standardize_bwd(yh, rs, dyv[:, sl] * silu[:, sl]))
        do = jnp.concatenate(dos, axis=1)
        dow = do * qw_ref[...]
        vw = v * kw2_ref[...]
        kw = k * kw2_ref[...]
        q0 = q * qw0_ref[...]
        sls = [slice(HEAD_DIM * h, HEAD_DIM * (h + 1)) for h in range(N_HEADS)]
        sn = st_ref[0]
        rrs = [rstate[:, sl] for sl in sls]
        ps = [_dot_nt(q[:, sl], k[:, sl]) for sl in sls]
        dps = [_dot_nt(do[:, sl], v[:, sl]) for sl in sls]
        dq_x = [_dot_nt(dow[:, sl], sn[:, sl]) for sl in sls]
        dk_x = [_dot_nt(vw[:, sl], rrs[h]) for h, sl in enumerate(sls)]
        dv_x = [_dot(kw[:, sl], rrs[h]) for h, sl in enumerate(sls)]
        r_new = [_dot_tn(q0[:, sl], do[:, sl]) for sl in sls]
        pd = [(ps[h] * dec_ref[h]).astype(_MXU) for h in range(N_HEADS)]
        dpd = [(dps[h] * dec_ref[h]).astype(_MXU) for h in range(N_HEADS)]
        dq_i = [_dot(dpd[h], k[:, sl]) for h, sl in enumerate(sls)]
        dk_i = [_dot_tn(dpd[h], q[:, sl]) for h, sl in enumerate(sls)]
        dv_i = [_dot_tn(pd[h], do[:, sl]) for h, sl in enumerate(sls)]
        dqs, dks = [], []
        for h, sl in enumerate(sls):
            dqs.append(dq_i[h] + dq_x[h])
            dks.append(dk_i[h] + dk_x[h])
            dz_ref[:, 2 * GROUP + HEAD_DIM * h:2 * GROUP + HEAD_DIM * (h + 1)] = (dv_i[h] + dv_x[h]).astype(dz_ref.dtype)
            rstate[:, sl] = cd_ref[:, sl] * rrs[h] + r_new[h]
        dq = _rope_bwd(jnp.concatenate(dqs, axis=1), cos, sin, 32)
        dk = _rope_bwd(jnp.concatenate(dks, axis=1) * _SCALE_B, cos, sin, 32)
        dz_ref[:, 0:GROUP] = dq.astype(dz_ref.dtype)
        dz_ref[:, GROUP:2 * GROUP] = dk.astype(dz_ref.dtype)
        dz_ref[:, 3 * GROUP:4 * GROUP] = jnp.concatenate(dgs, axis=1).astype(dz_ref.dtype)

    r0 = lambda: pl.BlockSpec((CHUNK, GROUP), lambda n: (nc - 1 - n, 0))
    return pl.pallas_call(
        body, grid=(nc,),
        in_specs=[r0(), rev(2), rev(3), rev(4), rev(5), r0(), pl.BlockSpec((1, HEAD_DIM, GROUP), lambda n: (nc - 1 - n, 0, 0)),
                  r0(), r0(), const((N_HEADS, CHUNK, CHUNK)), const((CHUNK, GROUP)), const((CHUNK, GROUP)),
                  const((CHUNK, GROUP)), const((1, GROUP))],
        out_specs=pl.BlockSpec((CHUNK, 4 * GROUP), lambda n: (nc - 1 - n, 0)),
        out_shape=SDS((s, 4 * GROUP), _MXU), scratch_shapes=[pltpu.VMEM((HEAD_DIM, GROUP), F32)],
        compiler_params=_cp("arbitrary"), name=name)(
            dy, z, z, z, z, o_pre, states, tb["b_cos"], tb["b_sin"], tb["decay"], tb["qw"], tb["kw2"], tb["qw0"], tb["cd"])


TQ = 256


def _log_sigmoid(x):
    return jnp.minimum(x, 0.0) - jnp.log1p(jnp.exp(-jnp.abs(x)))


def _fox_prep(z, b_f, name):
    s = z.shape[0]
    nb = s // TQ

    def body(m_ref, b_ref, cc_ref, carry):
        @pl.when(pl.program_id(0) == 0)
        def _():
            carry[...] = jnp.zeros_like(carry)

        lane = lax.broadcasted_iota(jnp.int32, (TQ, 128), 1)
        logf = jnp.where(lane < N_HEADS, _log_sigmoid(m_ref[...] + b_ref[...]), 0.0)
        r = lax.broadcasted_iota(jnp.int32, (TQ, TQ), 0)
        c = lax.broadcasted_iota(jnp.int32, (TQ, TQ), 1)
        tri = jnp.where(r >= c, 1.0, 0.0).astype(F32)
        cum = _dot_exact(tri, logf) + carry[...]
        cc_ref[...] = cum * LOG2E
        carry[...] = cum[TQ - 1:TQ, :]

    return pl.pallas_call(
        body, grid=(nb,),
        in_specs=[pl.BlockSpec((TQ, 128), lambda i: (i, NZ // 128 - 1)), pl.BlockSpec((1, 128), lambda i: (0, 0))],
        out_specs=pl.BlockSpec((TQ, 128), lambda i: (i, 0)),
        out_shape=SDS((s, 128), F32), scratch_shapes=[pltpu.VMEM((1, 128), F32)],
        compiler_params=_cp("arbitrary"), name=name)(z, b_f)


def _fox_post(dcr, dcq, z, b_f, dkr, name):
    s = z.shape[0]
    nb = s // TQ

    def body(dc_ref, dcq_ref, m_ref, b_ref, dkr_ref, dz_ref, db_ref, carry):
        @pl.when(pl.program_id(0) == 0)
        def _():
            carry[...] = jnp.zeros_like(carry)
            db_ref[...] = jnp.zeros_like(db_ref)

        r = lax.broadcasted_iota(jnp.int32, (TQ, TQ), 0)
        c = lax.broadcasted_iota(jnp.int32, (TQ, TQ), 1)
        triu = jnp.where(c >= r, 1.0, 0.0).astype(F32)
        dc = jnp.concatenate([dc_ref[0], jnp.zeros((120, TQ), F32)], axis=0)
        dlogf = _dot_exact(triu, dc, (((1,), (1,)), ((), ()))) + _dot_exact(triu, dcq_ref[...]) + carry[...]
        carry[...] = dlogf[0:1, :]
        x = m_ref[...] + b_ref[...]
        lane = lax.broadcasted_iota(jnp.int32, (TQ, 128), 1)
        df = jnp.where(lane < N_HEADS, dlogf * _sigmoid(-x), 0.0)
        db_ref[...] += jnp.sum(df, axis=0, keepdims=True)
        dz_ref[...] = (df + dkr_ref[...]).astype(dz_ref.dtype)

    rv = lambda i: nb - 1 - i
    return pl.pallas_call(
        body, grid=(nb,),
        in_specs=[pl.BlockSpec((1, 8, TQ), lambda i: (rv(i), 0, 0)), pl.BlockSpec((TQ, 128), lambda i: (rv(i), 0)),
                  pl.BlockSpec((TQ, 128), lambda i: (rv(i), NZ // 128 - 1)),
                  pl.BlockSpec((1, 128), lambda i: (0, 0)), pl.BlockSpec((TQ, 128), lambda i: (rv(i), 0))],
        out_specs=[pl.BlockSpec((TQ, 128), lambda i: (rv(i), 0)), pl.BlockSpec((1, 128), lambda i: (0, 0))],
        out_shape=[SDS((s, 128), _MXU), SDS((1, 128), F32)], scratch_shapes=[pltpu.VMEM((1, 128), F32)],
        compiler_params=_cp("arbitrary"), name=name)(dcr, dcq, z, b_f, dkr)


NEG = -1e30


def _causal_mask(shape, transposed=False):
    r = lax.broadcasted_iota(jnp.int32, shape, 0)
    c = lax.broadcasted_iota(jnp.int32, shape, 1)
    return (c >= r) if transposed else (r >= c)


def _head_lanes(h, dqk):
    return slice(128 * (h // 2), 128 * (h // 2) + 128) if dqk == HEAD_DIM else slice(128 * h, 128 * h + 128)


def _keep_half(x, a, axis):
    idx = lax.broadcasted_iota(jnp.int32, x.shape, axis)
    return jnp.where((idx < HEAD_DIM) if a == 0 else (idx >= HEAD_DIM), x, jnp.zeros_like(x))


def _kv_prep(z, kcol, vcol, name):
    s = z.shape[0]
    nk = s // TQ

    def body(k_ref, v_ref, kb_ref, vb_ref, vt_ref):
        kb_ref[...] = k_ref[...].astype(_MXU)
        v = v_ref[...]
        vb_ref[...] = v.astype(_MXU)
        vt_ref[0] = v.T.astype(_MXU)

    blk = pl.BlockSpec((TQ, GROUP), lambda i: (i, 0))
    return pl.pallas_call(
        body, grid=(nk,),
        in_specs=[pl.BlockSpec((TQ, GROUP), lambda i: (i, kcol)), pl.BlockSpec((TQ, GROUP), lambda i: (i, vcol))],
        out_specs=[blk, blk, pl.BlockSpec((1, GROUP, TQ), lambda i: (i, 0, 0))],
        out_shape=[SDS((s, GROUP), _MXU), SDS((s, GROUP), _MXU), SDS((nk, GROUP, TQ), _MXU)],
        compiler_params=_cp("parallel"), name=name)(z, z)


LOG2E = 1.4426950408889634


def _attn_fwd(q, qcol, dqk, kb, vt, scale, ck2, name, comm=None):
    s = q.shape[0]
    nq = s // TQ
    wq = N_HEADS * dqk
    bias = ck2 is not None

    def body(*refs):
        ins, (o_ref, l_ref), _, cc = _split_refs(refs, 4 if bias else 3, 2, comm)
        if bias:
            q_ref, k_ref, vt_ref, cc_ref = ins
        else:
            q_ref, k_ref, vt_ref = ins
        i = pl.program_id(0)
        if comm is not None:
            @pl.when(i == 0)
            def _():
                comm.start(*cc)

        qts = []
        for h in range(N_HEADS):
            qt = (q_ref[:, _head_lanes(h, dqk)].astype(F32) * (scale * LOG2E)).T
            qts.append((_keep_half(qt, h % 2, 0) if dqk == HEAD_DIM else qt).astype(_MXU))

        def step(j, carry, masked):
            r0 = pl.multiple_of(j * TQ, TQ)
            vtj = vt_ref[j]
            sts = [jnp.dot(k_ref[pl.ds(r0, TQ), _head_lanes(h, dqk)], qts[h], preferred_element_type=F32)
                   for h in range(N_HEADS)]
            stats, ps = [], []
            for h in range(N_HEADS):
                m, l, _ = carry[3 * h:3 * h + 3]
                st = sts[h]
                if bias:
                    st = st - cc_ref[pl.ds(r0, TQ), h:h + 1]
                if masked:
                    st = jnp.where(_causal_mask(st.shape, transposed=True), st, NEG)
                m_new = jnp.maximum(m, jnp.max(st, axis=0, keepdims=True))
                alpha = jnp.exp2(m - m_new)
                p = jnp.exp2(st - m_new)
                stats.append((m_new, alpha * l + jnp.sum(p, axis=0, keepdims=True), alpha))
                ps.append(p.astype(_MXU))
            out = []
            for h in range(N_HEADS):
                m_new, l, alpha = stats[h]
                acc = alpha * carry[3 * h + 2] + jnp.dot(vtj[HEAD_DIM * h:HEAD_DIM * (h + 1), :], ps[h],
                                                         preferred_element_type=F32)
                out += [m_new, l, acc]
            return tuple(out)

        init = (jnp.full((1, TQ), NEG, F32), jnp.zeros((1, TQ), F32), jnp.zeros((HEAD_DIM, TQ), F32)) * N_HEADS
        carry = lax.fori_loop(0, i, functools.partial(step, masked=False), init)
        carry = step(i, carry, True)
        l_ref[...] = jnp.zeros_like(l_ref)
        for h in range(N_HEADS):
            l_ref[0, h:h + 1, :] = carry[3 * h] + jnp.log2(carry[3 * h + 1])
        for p in range(2):
            ot = jnp.concatenate([carry[6 * p + 2] / carry[6 * p + 1], carry[6 * p + 5] / carry[6 * p + 4]], axis=0)
            o_ref[:, 128 * p:128 * (p + 1)] = ot.T
        if comm is not None:
            @pl.when(i == nq - 1)
            def _():
                comm.wait(*cc)

    rows = pl.BlockSpec((1, 8, TQ), lambda i: (i, 0, 0))
    in_specs = [pl.BlockSpec((TQ, wq), lambda i: (i, qcol)), pl.BlockSpec((s, wq), lambda i: (0, 0)),
                pl.BlockSpec((nq, GROUP, TQ), lambda i: (0, 0, 0))]
    args = [q, kb, vt]
    if bias:
        in_specs.append(pl.BlockSpec((s, 128), lambda i: (0, 0)))
        args.append(ck2)
    out_specs = [pl.BlockSpec((TQ, GROUP), lambda i: (i, 0)), rows]
    out_shape = [SDS((s, GROUP), F32), SDS((nq, 8, TQ), F32)]
    return _call_with_comm(body, (nq,), in_specs, out_specs, out_shape, [], args, comm, ("arbitrary",), name)


def _call_with_comm(body, grid, in_specs, out_specs, out_shape, scratch, args, comm, semantics, name):
    n_out = len(out_shape)
    if comm is not None:
        in_specs, out_specs = in_specs + comm.in_specs, out_specs + comm.out_specs
        out_shape, scratch, args = out_shape + comm.out_shape, scratch + comm.scratch, list(args) + comm.arrs
    res = pl.pallas_call(body, grid=grid, in_specs=in_specs, out_specs=out_specs, out_shape=out_shape,
                         scratch_shapes=scratch, compiler_params=_cp(*semantics), name=name)(*args)
    return (*res[:n_out], list(res[n_out:]))


def _attn_bwd_prep(q, qcol, dqk, scale, o, do, name):
    s = q.shape[0]
    nq = s // TQ
    wq = N_HEADS * dqk

    def body(q_ref, o_ref, do_ref, qt_ref, dot_ref, dl_ref):
        qt_ref[0] = (q_ref[...].astype(F32) * (scale * LOG2E)).T.astype(_MXU)
        dov = do_ref[...]
        dot_ref[0] = dov.T.astype(_MXU)
        pt = (dov * o_ref[...]).T
        dl_ref[...] = jnp.zeros_like(dl_ref)
        for h in range(N_HEADS):
            dl_ref[0, h:h + 1, :] = jnp.sum(pt[HEAD_DIM * h:HEAD_DIM * (h + 1), :], axis=0, keepdims=True)

    nat = lambda w: pl.BlockSpec((TQ, w), lambda i: (i, 0))
    tr = lambda w: pl.BlockSpec((1, w, TQ), lambda i: (i, 0, 0))
    return pl.pallas_call(
        body, grid=(nq,),
        in_specs=[pl.BlockSpec((TQ, wq), lambda i: (i, qcol)), nat(GROUP), nat(GROUP)],
        out_specs=[tr(wq), tr(GROUP), tr(8)],
        out_shape=[SDS((nq, wq, TQ), _MXU), SDS((nq, GROUP, TQ), _MXU), SDS((nq, 8, TQ), F32)],
        compiler_params=_cp("parallel"), name=name)(q, o, do)


def _attn_bwd(kb, vb, qt, dot, lse, dl, dqk, scale, ck2, name, kv_dtype, comm=None):
    s = kb.shape[0]
    nq = s // TQ
    wq = N_HEADS * dqk
    bias = ck2 is not None

    def body(*refs):
        ins, outs, _, cc = _split_refs(refs, 7 if bias else 6, 5 if bias else 3, comm)
        if bias:
            k_ref, v_ref, qt_ref, dot_ref, l_ref, d_ref, cc_ref = ins
            dqt_ref, dk_ref, dv_ref, dck_ref, dcq_ref = outs
        else:
            k_ref, v_ref, qt_ref, dot_ref, l_ref, d_ref = ins
            dqt_ref, dk_ref, dv_ref = outs
        j = pl.program_id(0)

        @pl.when(j == 0)
        def _():
            if comm is not None:
                comm.start(*cc)
            dqt_ref[...] = jnp.zeros_like(dqt_ref)
            if bias:
                dcq_ref[...] = jnp.zeros_like(dcq_ref)

        ks, kts, vs = [], [], []
        for h in range(N_HEADS):
            k2 = k_ref[:, _head_lanes(h, dqk)]
            if dqk == HEAD_DIM:
                k2 = _keep_half(k2, h % 2, 1)
            ks.append(k2)
            kts.append(k2.astype(F32).T.astype(_MXU))
            vs.append(_keep_half(v_ref[:, _head_lanes(h, HEAD_DIM)], h % 2, 1))
        cks = [cc_ref[:, h:h + 1] for h in range(N_HEADS)] if bias else None

        nt = (((1,), (1,)), ((), ()))

        def step(i, carry, masked):
            qti, doti, li, di = qt_ref[i], dot_ref[i], l_ref[i], d_ref[i]
            qls = [_head_lanes(h, dqk) for h in range(N_HEADS)]
            vls = [_head_lanes(h, HEAD_DIM) for h in range(N_HEADS)]
            sts = [jnp.dot(ks[h], qti[qls[h], :], preferred_element_type=F32) for h in range(N_HEADS)]
            dpts = [jnp.dot(vs[h], doti[vls[h], :], preferred_element_type=F32) for h in range(N_HEADS)]
            pbs, dsbs, dcks = [], [], []
            for h in range(N_HEADS):
                st = sts[h] - li[h:h + 1, :]
                if bias:
                    st = st - cks[h]
                p = jnp.exp2(st)
                if masked:
                    p = jnp.where(_causal_mask(p.shape, transposed=True), p, 0.0)
                dst = p * (dpts[h] - di[h:h + 1, :])
                pbs.append(p.astype(_MXU))
                dsbs.append(dst.astype(_MXU))
                if bias:
                    dcks.append(carry[3 * h + 2] + jnp.sum(dst, axis=1, keepdims=True))
                    dcq_ref[i, h:h + 1, :] += jnp.sum(dst, axis=0, keepdims=True)
                else:
                    dcks.append(carry[3 * h + 2])
            out = []
            for h in range(N_HEADS):
                dvt = carry[3 * h + 1] + lax.dot_general(doti[HEAD_DIM * h:HEAD_DIM * (h + 1), :], pbs[h], nt,
                                                         preferred_element_type=F32)
                dkt = carry[3 * h] + lax.dot_general(qti[dqk * h:dqk * (h + 1), :], dsbs[h], nt, preferred_element_type=F32)
                dqt_ref[i, qls[h], :] += jnp.dot(kts[h], dsbs[h], preferred_element_type=F32) * scale
                out += [dkt, dvt, dcks[h]]
            return tuple(out)

        init = (jnp.zeros((dqk, TQ), F32), jnp.zeros((HEAD_DIM, TQ), F32), jnp.zeros((TQ, 1), F32)) * N_HEADS
        carry = step(j, init, True)
        carry = lax.fori_loop(j + 1, nq, functools.partial(step, masked=False), carry)
        for p in range(2):
            dv_ref[:, 128 * p:128 * (p + 1)] = jnp.concatenate([carry[6 * p + 1], carry[6 * p + 4]], axis=0).T.astype(dv_ref.dtype)
            if dqk == HEAD_DIM:
                dk_ref[:, 128 * p:128 * (p + 1)] = (jnp.concatenate([carry[6 * p], carry[6 * p + 3]], axis=0).T
                                                    * (1.0 / LOG2E)).astype(dk_ref.dtype)
        if dqk != HEAD_DIM:
            for h in range(N_HEADS):
                dk_ref[:, 128 * h:128 * (h + 1)] = (carry[3 * h].T * (1.0 / LOG2E)).astype(dk_ref.dtype)
        if bias:
            dck_ref[...] = jnp.zeros_like(dck_ref)
            for h in range(N_HEADS):
                dck_ref[:, h:h + 1] = -carry[3 * h + 2]
        if comm is not None:
            @pl.when(j == nq - 1)
            def _():
                comm.wait(*cc)

    blk = lambda w: pl.BlockSpec((TQ, w), lambda j: (j, 0))
    full3 = lambda w: pl.BlockSpec((nq, w, TQ), lambda j: (0, 0, 0))
    in_specs = [blk(wq), blk(GROUP), full3(wq), full3(GROUP), full3(8), full3(8)]
    args = [kb, vb, qt, dot, lse, dl]
    out_specs = [full3(wq), blk(wq), blk(GROUP)]
    out_shape = [SDS((nq, wq, TQ), F32), SDS((s, wq), kv_dtype), SDS((s, GROUP), kv_dtype)]
    if bias:
        in_specs.append(blk(128))
        args.append(ck2)
        out_specs += [blk(128), full3(8)]
        out_shape += [SDS((s, 128), F32), SDS((nq, 8, TQ), F32)]
    return _call_with_comm(body, (nq,), in_specs, out_specs, out_shape, [], args, comm, ("arbitrary",), name)


def _untranspose(xt, dtype, name):
    nq, w, _ = xt.shape

    def body(x_ref, o_ref):
        o_ref[...] = x_ref[0].T.astype(o_ref.dtype)

    return pl.pallas_call(
        body, grid=(nq,), in_specs=[pl.BlockSpec((1, w, TQ), lambda i: (i, 0, 0))],
        out_specs=pl.BlockSpec((TQ, w), lambda i: (i, 0)), out_shape=SDS((nq * TQ, w), dtype),
        compiler_params=_cp("parallel"), name=name)(xt)


_SCALE_D = (64 + 32) ** -0.5
_COL_CQ, _COL_CKV, _COL_MISC = 2304 // 256, 2560 // 128, 2688 // 128


def _mla_prep(z, gq, gkv, wq, wk, wv, tb, name):
    s = z.shape[0]
    tm = TQ
    row = lambda w, c: pl.BlockSpec((tm, w), lambda i, c=c: (i, c))
    const = lambda a: pl.BlockSpec(a.shape, lambda i: (0,) * a.ndim)

    def body(cq_ref, ckv_ref, m_ref, gq_ref, gkv_ref, wq_ref, wk_ref, wv_ref, e_ref, qc_ref, qs_ref, kc_ref, ks_ref,
             q_ref, k_ref, v_ref, vt_ref, cqn_ref, ckvn_ref):
        cqn = _rms(cq_ref[...], gq_ref[...]).astype(_MXU)
        ckvn = _rms(ckv_ref[...], gkv_ref[...]).astype(_MXU)
        cqn_ref[...] = cqn
        ckvn_ref[...] = ckvn
        q_ref[...] = _rope(_dot(cqn, wq_ref[...]), qc_ref[...], qs_ref[...], 16).astype(q_ref.dtype)
        kr = _rope(m_ref[...], kc_ref[...], ks_ref[...], 16)
        k_ref[...] = (_dot(ckvn, wk_ref[...]) + _dot(kr, e_ref[...])).astype(k_ref.dtype)
        v = _dot(ckvn, wv_ref[...])
        v_ref[...] = v.astype(v_ref.dtype)
        vt_ref[0] = v.T.astype(vt_ref.dtype)

    e = tb["place"]
    return pl.pallas_call(
        body, grid=(s // tm,),
        in_specs=[row(256, _COL_CQ), row(128, _COL_CKV), row(128, _COL_MISC), const(gq), const(gkv), const(wq), const(wk),
                  const(wv), const(e), row(512, 0), row(512, 0), row(128, 0), row(128, 0)],
        out_specs=[row(512, 0), row(512, 0), row(256, 0), pl.BlockSpec((1, GROUP, TQ), lambda i: (i, 0, 0)), row(256, 0),
                   row(128, 0)],
        out_shape=[SDS((s, 512), _MXU), SDS((s, 512), _MXU), SDS((s, 256), _MXU), SDS((s // TQ, GROUP, TQ), _MXU),
                   SDS((s, 256), _MXU), SDS((s, 128), _MXU)],
        compiler_params=_cp("parallel"), name=name)(
            z, z, z, gq, gkv, wq, wk, wv, e, tb["q_cos"], tb["q_sin"], tb["k_cos"], tb["k_sin"])


def _mla_prep_bwd(dq, dk, dv, z, cqn, ckvn, gq, gkv, wq, wk, wv, tb, name):
    s = z.shape[0]
    tm = min(512, s)
    row = lambda w, c: pl.BlockSpec((tm, w), lambda i, c=c: (i, c))
    const = lambda a: pl.BlockSpec(a.shape, lambda i: (0,) * a.ndim)
    acc = lambda shape: pl.BlockSpec(shape, lambda i: (0, 0))

    def body(dq_ref, dk_ref, dv_ref, cq_ref, ckv_ref, cqn_ref, ckvn_ref, gq_ref, gkv_ref, wq_ref, wk_ref, wv_ref, e_ref,
             qc_ref, qs_ref, kc_ref, ks_ref, dcq_ref, dckv_ref, dkr_ref, dwq_ref, dwk_ref, dwv_ref, dgq_ref, dgkv_ref):
        @pl.when(pl.program_id(0) == 0)
        def _():
            for r in (dwq_ref, dwk_ref, dwv_ref, dgq_ref, dgkv_ref):
                r[...] = jnp.zeros_like(r)

        dqp = _rope_bwd(dq_ref[...], qc_ref[...], qs_ref[...], 16)
        dkd = dk_ref[...]
        dvd = dv_ref[...]
        dwq_ref[...] += _dot_tn(cqn_ref[...], dqp)
        dwk_ref[...] += _dot_tn(ckvn_ref[...], dkd)
        dwv_ref[...] += _dot_tn(ckvn_ref[...], dvd)
        dcq, dgq = _rms_bwd(cq_ref[...], gq_ref[...], _dot_nt(dqp, wq_ref[...]))
        dckv, dgkv = _rms_bwd(ckv_ref[...], gkv_ref[...], _dot_nt(dkd, wk_ref[...]) + _dot_nt(dvd, wv_ref[...]))
        dcq_ref[...] = dcq.astype(dcq_ref.dtype)
        dckv_ref[...] = dckv.astype(dckv_ref.dtype)
        dgq_ref[...] += dgq
        dgkv_ref[...] += dgkv
        dkr = _dot_exact(dkd, e_ref[...], (((1,), (1,)), ((), ())))
        dkr_ref[...] = _rope_bwd(dkr, kc_ref[...], ks_ref[...], 16)

    e = tb["place"]
    return pl.pallas_call(
        body, grid=(s // tm,),
        in_specs=[row(512, 0), row(512, 0), row(256, 0), row(256, _COL_CQ), row(128, _COL_CKV), row(256, 0), row(128, 0),
                  const(gq), const(gkv), const(wq), const(wk), const(wv), const(e), row(512, 0), row(512, 0), row(128, 0), row(128, 0)],
        out_specs=[row(256, 0), row(128, 0), row(128, 0), acc((256, 512)), acc((128, 512)), acc((128, 256)), acc((1, 256)),
                   acc((1, 128))],
        out_shape=[SDS((s, 256), _MXU), SDS((s, 128), _MXU), SDS((s, 128), F32), SDS((256, 512), F32), SDS((128, 512), F32),
                   SDS((128, 256), F32), SDS((1, 256), F32), SDS((1, 128), F32)],
        compiler_params=_cp("arbitrary"), name=name)(
            dq, dk, dv, z, z, cqn, ckvn, gq, gkv, wq, wk, wv, e, tb["q_cos"], tb["q_sin"], tb["k_cos"], tb["k_sin"])


def _out_proj(ys, g, w, x, name):
    s, d = x.shape
    tm = min(512, s)

    def body(ya, yb, yc, yd, g_ref, w_ref, x_ref, o_ref, yn_ref):
        acc = x_ref[...]
        for i, y_ref in enumerate((ya, yb, yc, yd)):
            sl = slice(GROUP * i, GROUP * (i + 1))
            yn = _rms(y_ref[...], g_ref[:, sl]).astype(_MXU)
            yn_ref[:, sl] = yn
            acc = acc + jnp.dot(yn, w_ref[sl, :], preferred_element_type=F32)
        o_ref[...] = acc

    yspec = pl.BlockSpec((tm, GROUP), lambda i: (i, 0))
    return pl.pallas_call(
        body, grid=(s // tm,),
        in_specs=[yspec, yspec, yspec, yspec, pl.BlockSpec((1, d), lambda i: (0, 0)), pl.BlockSpec((d, d), lambda i: (0, 0)),
                  pl.BlockSpec((tm, d), lambda i: (i, 0))],
        out_specs=[pl.BlockSpec((tm, d), lambda i: (i, 0)), pl.BlockSpec((tm, d), lambda i: (i, 0))],
        out_shape=[SDS((s, d), F32), SDS((s, d), _MXU)], compiler_params=_cp("parallel"), name=name)(*ys, g, w, x)


def _out_proj_bwd(dx, w, ys, g, name):
    s, d = dx.shape
    tm = min(512, s)

    def body(dx_ref, w_ref, ya, yb, yc, yd, g_ref, da, db, dc, dd, dg_ref):
        @pl.when(pl.program_id(0) == 0)
        def _():
            dg_ref[...] = jnp.zeros_like(dg_ref)

        dyn = _dot_nt(dx_ref[...], w_ref[...])
        outs = (da, db, dc, dd)
        for i, y_ref in enumerate((ya, yb, yc, yd)):
            sl = slice(GROUP * i, GROUP * (i + 1))
            dy, dg = _rms_bwd(y_ref[...], g_ref[:, sl], dyn[:, sl])
            outs[i][...] = dy
            dg_ref[:, sl] += dg

    yspec = pl.BlockSpec((tm, GROUP), lambda i: (i, 0))
    return pl.pallas_call(
        body, grid=(s // tm,),
        in_specs=[pl.BlockSpec((tm, d), lambda i: (i, 0)), pl.BlockSpec((d, d), lambda i: (0, 0)), yspec, yspec, yspec, yspec,
                  pl.BlockSpec((1, d), lambda i: (0, 0))],
        out_specs=[yspec, yspec, yspec, yspec, pl.BlockSpec((1, d), lambda i: (0, 0))],
        out_shape=[SDS((s, GROUP), F32)] * 4 + [SDS((1, d), F32)],
        compiler_params=_cp("arbitrary"), name=name)(dx, w, *ys, g)


FF_BLOCK = 512


def _ffn_fwd(x, g, wu, wd, name, comm=None):
    s, d = x.shape
    nj = wu.shape[0]
    tm = min(512, s)
    ni = s // tm

    def body(*refs):
        (x_ref, g_ref, wu_ref, wd_ref), (o_ref, u_ref, h_ref), (acc,), cc = _split_refs(refs, 4, 3, comm)
        i, j = pl.program_id(0), pl.program_id(1)
        if comm is not None:
            @pl.when((i == 0) & (j == 0))
            def _():
                comm.start(*cc)

        @pl.when(j == 0)
        def _():
            h_ref[...] = _rms(x_ref[...], g_ref[...]).astype(h_ref.dtype)
            acc[...] = jnp.zeros_like(acc)

        u = jnp.dot(h_ref[...], wu_ref[0], preferred_element_type=F32)
        u_ref[...] = u.astype(u_ref.dtype)
        acc[...] += _dot(jnp.square(jnp.maximum(u, 0.0)), wd_ref[...])

        @pl.when(j == nj - 1)
        def _():
            o_ref[...] = x_ref[...] + acc[...]

        if comm is not None:
            @pl.when((i == ni - 1) & (j == nj - 1))
            def _():
                comm.wait(*cc)

    in_specs = [pl.BlockSpec((tm, d), lambda i, j: (i, 0)), pl.BlockSpec((1, d), lambda i, j: (0, 0)),
                pl.BlockSpec((1, d, FF_BLOCK), lambda i, j: (j, 0, 0)), pl.BlockSpec((FF_BLOCK, d), lambda i, j: (j, 0))]
    out_specs = [pl.BlockSpec((tm, d), lambda i, j: (i, 0)), pl.BlockSpec((tm, FF_BLOCK), lambda i, j: (i, j)),
                 pl.BlockSpec((tm, d), lambda i, j: (i, 0))]
    out_shape = [SDS((s, d), F32), SDS((s, nj * FF_BLOCK), _MXU), SDS((s, d), _MXU)]
    return _call_with_comm(body, (ni, nj), in_specs, out_specs, out_shape, [pltpu.VMEM((tm, d), F32)], [x, g, wu, wd], comm,
                           ("arbitrary", "arbitrary"), name)


def _ffn_bwd(dx2, x, u, g, wu, wd, name, comm=None):
    s, d = x.shape
    nj = wu.shape[0]
    tm = min(512, s)
    ni = s // tm

    def body(*refs):
        (dx_ref, x_ref, u_ref, g_ref, wu_ref, wd_ref), (o_ref, du_ref, dg_ref), (acc, dxb), cc = _split_refs(refs, 6, 3, comm)
        i, j = pl.program_id(0), pl.program_id(1)

        @pl.when((i == 0) & (j == 0))
        def _():
            if comm is not None:
                comm.start(*cc)
            dg_ref[...] = jnp.zeros_like(dg_ref)

        @pl.when(j == 0)
        def _():
            dxb[...] = dx_ref[...].astype(dxb.dtype)
            acc[...] = jnp.zeros_like(acc)

        da = lax.dot_general(dxb[...], wd_ref[...], (((1,), (1,)), ((), ())), preferred_element_type=F32)
        du = (da * 2.0 * jnp.maximum(u_ref[...].astype(F32), 0.0)).astype(du_ref.dtype)
        du_ref[...] = du
        acc[...] += lax.dot_general(du, wu_ref[0], (((1,), (1,)), ((), ())), preferred_element_type=F32)

        @pl.when(j == nj - 1)
        def _():
            dxn, dg = _rms_bwd(x_ref[...], g_ref[...], acc[...])
            o_ref[...] = dx_ref[...] + dxn
            dg_ref[...] += dg

        if comm is not None:
            @pl.when((i == ni - 1) & (j == nj - 1))
            def _():
                comm.wait(*cc)

    in_specs = [pl.BlockSpec((tm, d), lambda i, j: (i, 0)), pl.BlockSpec((tm, d), lambda i, j: (i, 0)),
                pl.BlockSpec((tm, FF_BLOCK), lambda i, j: (i, j)), pl.BlockSpec((1, d), lambda i, j: (0, 0)),
                pl.BlockSpec((1, d, FF_BLOCK), lambda i, j: (j, 0, 0)), pl.BlockSpec((FF_BLOCK, d), lambda i, j: (j, 0))]
    out_specs = [pl.BlockSpec((tm, d), lambda i, j: (i, 0)), pl.BlockSpec((tm, FF_BLOCK), lambda i, j: (i, j)),
                 pl.BlockSpec((1, d), lambda i, j: (0, 0))]
    out_shape = [SDS((s, d), F32), SDS((s, nj * FF_BLOCK), _MXU), SDS((1, d), F32)]
    return _call_with_comm(body, (ni, nj), in_specs, out_specs, out_shape,
                           [pltpu.VMEM((tm, d), F32), pltpu.VMEM((tm, d), _MXU)], [dx2, x, u, g, wu, wd], comm,
                           ("arbitrary", "arbitrary"), name)


def _in_proj_bwd(dz, w, x, g, dx_up, name, comm=None):
    s, d = x.shape
    n = w.shape[1]
    tm = min(512, s)
    ni = s // tm

    def body(*refs):
        (dz_ref, w_ref, x_ref, g_ref, up_ref), (o_ref, dg_ref), _, cc = _split_refs(refs, 5, 2, comm)
        i = pl.program_id(0)

        @pl.when(i == 0)
        def _():
            if comm is not None:
                comm.start(*cc)
            dg_ref[...] = jnp.zeros_like(dg_ref)

        dh = lax.dot_general(dz_ref[...], w_ref[...], (((1,), (1,)), ((), ())), preferred_element_type=F32)
        dxn, dg = _rms_bwd(x_ref[...], g_ref[...], dh)
        o_ref[...] = up_ref[...] + dxn
        dg_ref[...] += dg
        if comm is not None:
            @pl.when(i == ni - 1)
            def _():
                comm.wait(*cc)

    in_specs = [pl.BlockSpec((tm, n), lambda i: (i, 0)), pl.BlockSpec((d, n), lambda i: (0, 0)),
                pl.BlockSpec((tm, d), lambda i: (i, 0)), pl.BlockSpec((1, d), lambda i: (0, 0)),
                pl.BlockSpec((tm, d), lambda i: (i, 0))]
    out_specs = [pl.BlockSpec((tm, d), lambda i: (i, 0)), pl.BlockSpec((1, d), lambda i: (0, 0))]
    out_shape = [SDS((s, d), F32), SDS((1, d), F32)]
    return _call_with_comm(body, (ni,), in_specs, out_specs, out_shape, [], [dz, w, x, g, dx_up], comm, ("arbitrary",), name)


def _loss_head(x, g, target, name):
    s, d = x.shape
    tm = min(512, s)

    def body(x_ref, g_ref, t_ref, l_ref, dx_ref, dg_ref):
        @pl.when(pl.program_id(0) == 0)
        def _():
            l_ref[...] = jnp.zeros_like(l_ref)
            dg_ref[...] = jnp.zeros_like(dg_ref)

        xv = x_ref[...]
        err = _rms(xv, g_ref[...]) - t_ref[...]
        l_ref[...] += jnp.sum(err * err, axis=0, keepdims=True) * (0.5 / d)
        dx, dg = _rms_bwd(xv, g_ref[...], err * (1.0 / d))
        dx_ref[...] = dx
        dg_ref[...] += dg

    return pl.pallas_call(
        body, grid=(s // tm,),
        in_specs=[pl.BlockSpec((tm, d), lambda i: (i, 0)), pl.BlockSpec((1, d), lambda i: (0, 0)),
                  pl.BlockSpec((tm, d), lambda i: (i, 0))],
        out_specs=[pl.BlockSpec((1, d), lambda i: (0, 0)), pl.BlockSpec((tm, d), lambda i: (i, 0)),
                   pl.BlockSpec((1, d), lambda i: (0, 0))],
        out_shape=[SDS((1, d), F32), SDS((s, d), F32), SDS((1, d), F32)], compiler_params=_cp("arbitrary"), name=name)(x, g, target)


def _me_and_peer():
    x, y, c = lax.axis_index("x"), lax.axis_index("y"), lax.axis_index("c")
    me = 4 * x + 2 * y + c

    def peer(k):
        px, py, pc = x ^ (k >> 2), y ^ ((k >> 1) & 1), c ^ (k & 1)
        return (px, py, pc), 4 * px + 2 * py + pc

    return me, peer


class _Comm:
    def __init__(self, kind, arrs):
        assert kind in ("gather", "exchange")
        self.kind, self.arrs, self.n = kind, list(arrs), len(arrs)
        anyspec = pl.BlockSpec(memory_space=pl.ANY)
        self.in_specs = [anyspec] * self.n
        self.out_specs = [anyspec] * self.n
        self.out_shape = [SDS(((NDEV,) + a.shape) if kind == "gather" else a.shape, a.dtype) for a in self.arrs]
        self.scratch = [pltpu.SemaphoreType.DMA((self.n, NDEV - 1)), pltpu.SemaphoreType.DMA((self.n, NDEV - 1)),
                        pltpu.SemaphoreType.DMA((self.n,))]

    def _copies(self, ins, outs, sems):
        send, recv, loc = sems
        me, peer = _me_and_peer()
        gather = self.kind == "gather"
        local = [pltpu.make_async_copy(ins[a] if gather else ins[a].at[me], outs[a].at[me], loc.at[a]) for a in range(self.n)]
        outgoing, incoming = [], []
        for k in range(1, NDEV):
            dev, pid = peer(k)
            for a in range(self.n):
                pair = dict(send_sem=send.at[a, k - 1], recv_sem=recv.at[a, k - 1], device_id=dev, device_id_type=MESH)
                outgoing.append(pltpu.make_async_remote_copy(src_ref=ins[a] if gather else ins[a].at[pid],
                                                             dst_ref=outs[a].at[me], **pair))
                incoming.append(pltpu.make_async_remote_copy(src_ref=ins[a] if gather else ins[a].at[me],
                                                             dst_ref=outs[a].at[pid], **pair))
        return local, outgoing, incoming

    def start(self, ins, outs, sems):
        local, outgoing, _ = self._copies(ins, outs, sems)
        for cp in local + outgoing:
            cp.start()

    def wait(self, ins, outs, sems):
        local, outgoing, incoming = self._copies(ins, outs, sems)
        for cp in incoming:
            cp.wait_recv()
        for cp in outgoing:
            cp.wait_send()
        for cp in local:
            cp.wait()


def _split_refs(refs, n_in, n_out, comm):
    c = comm.n if comm is not None else 0
    ins, cin = refs[:n_in], refs[n_in:n_in + c]
    outs, cout = refs[n_in + c:n_in + c + n_out], refs[n_in + c + n_out:n_in + 2 * c + n_out]
    rest = refs[n_in + 2 * c + n_out:]
    scratch, csem = (rest[:len(rest) - 3], rest[len(rest) - 3:]) if c else (rest, ())
    return ins, outs, scratch, (cin, cout, csem)


def _comm_call(kind, arrs, name):
    comm = _Comm(kind, arrs)

    def body(*refs):
        _, _, _, c = _split_refs(refs, 0, 0, comm)
        comm.start(*c)
        comm.wait(*c)

    return pl.pallas_call(body, in_specs=comm.in_specs, out_specs=comm.out_specs, out_shape=comm.out_shape,
                          scratch_shapes=comm.scratch, compiler_params=pltpu.CompilerParams(has_side_effects=True),
                          name=name)(*arrs)


def _all_gather(arrs, name):
    return _comm_call("gather", arrs, name)


def _exchange(arrs, name):
    return _comm_call("exchange", arrs, name)


def _sum_slots(parts, name):
    _, r, c = parts.shape
    tr = r if r <= 512 else 512

    def body(p_ref, o_ref):
        acc = p_ref[0].astype(F32)
        for q in range(1, NDEV):
            acc = acc + p_ref[q].astype(F32)
        o_ref[...] = acc

    return pl.pallas_call(
        body, grid=(r // tr,), in_specs=[pl.BlockSpec((NDEV, tr, c), lambda i: (0, i, 0))],
        out_specs=pl.BlockSpec((tr, c), lambda i: (i, 0)), out_shape=SDS((r, c), F32),
        compiler_params=_cp("parallel"), name=name)(parts)


def _adamw(g, w, m, v, name):
    r, c = w.shape
    parts = g.ndim == 3
    tr = r
    for cand in (512, 256, 128, 64, 32, 16, 8):
        if r > cand and r % cand == 0 and cand * c * 4 <= 2 * 1024 * 1024:
            tr = cand
            break
    bc1 = 1.0 / (1.0 - ADAM_B1 ** ADAM_STEP)
    bc2 = 1.0 / (1.0 - ADAM_B2 ** ADAM_STEP)

    def body(g_ref, w_ref, m_ref, v_ref, go_ref, d_ref, mo_ref, vo_ref):
        if parts:
            gv = g_ref[0].astype(F32)
            for q in range(1, NDEV):
                gv = gv + g_ref[q].astype(F32)
        else:
            gv = g_ref[...]
        mn = ADAM_B1 * m_ref[...] + (1.0 - ADAM_B1) * gv
        vn = ADAM_B2 * v_ref[...] + (1.0 - ADAM_B2) * (gv * gv)
        go_ref[...] = gv
        mo_ref[...] = mn
        vo_ref[...] = vn
        d_ref[...] = -ADAM_LR * ((mn * bc1) / (jnp.sqrt(vn * bc2) + ADAM_EPS) + ADAM_WD * w_ref[...])

    spec = pl.BlockSpec((tr, c), lambda i: (i, 0))
    gspec = pl.BlockSpec((NDEV, tr, c), lambda i: (0, i, 0)) if parts else spec
    return pl.pallas_call(
        body, grid=(r // tr,), in_specs=[gspec, spec, spec, spec], out_specs=[spec] * 4,
        out_shape=[SDS((r, c), F32)] * 4, compiler_params=_cp("parallel"), name=name)(g, w, m, v)


def _pad_in_cols(w):
    r = w.shape[0]
    zeros = lambda n: jnp.zeros((r, n), w.dtype)
    return jnp.concatenate([w[:, :2304], w[:, 2308:2692], w[:, 2304:2308], zeros(28), w[:, 2692:2724], zeros(64)], axis=1)


def _unpad_in_cols(w):
    return jnp.concatenate([w[..., :2304], w[..., 2688:2692], w[..., 2304:2688], w[..., 2720:2752]], axis=-1)


def _pad_uq(w):
    return jnp.pad(w.reshape(256, N_HEADS, 96), ((0, 0), (0, 0), (0, 32))).reshape(256, 512)


def _unpad_uq(w):
    return w.reshape(256, N_HEADS, 128)[:, :, :96].reshape(256, 384)


def _split_ukv(w):
    r = w.reshape(128, N_HEADS, 128)
    return jnp.pad(r[:, :, :64], ((0, 0), (0, 0), (0, 64))).reshape(128, 512), r[:, :, 64:].reshape(128, 256)


def _join_ukv(dk, dv):
    return jnp.concatenate([dk.reshape(128, N_HEADS, 128)[:, :, :64], dv.reshape(128, N_HEADS, 64)], axis=-1).reshape(128, 512)


def _cols_to_full(g):
    return jnp.transpose(g, (1, 0, 2)).reshape(g.shape[1], NDEV * g.shape[2])


def kernel(x, g_mix_norm, w_in, b_forget, g_sgu, w_spatial, b_spatial, g_mla_q, w_uq, g_mla_kv, w_ukv, g_group_out, w_out, g_ffn_norm, w_up, w_down, g_final, loss_target, m_g_mix_norm, m_w_in, m_b_forget, m_g_sgu, m_w_spatial, m_b_spatial, m_g_mla_q, m_w_uq, m_g_mla_kv, m_w_ukv, m_g_group_out, m_w_out, m_g_ffn_norm, m_w_up, m_w_down, m_g_final, v_g_mix_norm, v_w_in, v_b_forget, v_g_sgu, v_w_spatial, v_b_spatial, v_g_mla_q, v_w_uq, v_g_mla_kv, v_w_ukv, v_g_group_out, v_w_out, v_g_ffn_norm, v_w_up, v_w_down, v_g_final):
    depth = w_in.shape[0]
    s, d = x.shape[1], x.shape[2]
    x0 = x.reshape(s, d)
    target = loss_target.reshape(s, d)
    tb = _tables(s)
    me = 4 * lax.axis_index("x") + 2 * lax.axis_index("y") + lax.axis_index("c")

    assert depth == 2
    shards = {}
    for l in range(depth):
        shards.update({(l, "w_in"): _pad_in_cols(w_in[l]).astype(_WIRE), (l, "w_out"): w_out[l].astype(_WIRE),
                       (l, "w_up"): w_up[l].astype(_WIRE), (l, "w_down"): w_down[l].astype(_WIRE),
                       (l, "w_uq"): w_uq[l].astype(_WIRE), (l, "w_ukv"): w_ukv[l].astype(_WIRE)})
    wts = _ShardedWeights(shards)
    wts.full[(0, "w_in")] = _all_gather([shards[(0, "w_in")]], "gather_w_in0")[0]

    row = lambda a: a.reshape(1, -1)

    def small(l):
        bf = jnp.pad(b_forget[l].reshape(1, N_HEADS), ((0, 0), (0, 128 - N_HEADS)))
        bt = jnp.pad(b_spatial[l].T, ((0, 0), (0, 128 - N_HEADS)))
        return dict(g_mix=row(g_mix_norm[l]), g_sgu=row(g_sgu[l]), w_s=w_spatial[l], b_t=bt, b_f=bf, gq=row(g_mla_q[l]),
                    gkv=row(g_mla_kv[l]), g_go=row(g_group_out[l]), g_ffn=row(g_ffn_norm[l]))

    smalls = [small(l) for l in range(depth)]
    lrow, dx, sm, dg_final = _local_step(x0, target, wts, smalls, row(g_final), tb)
    loss = lax.psum(jnp.sum(lrow), AXES)
    grad_x = dx.reshape(1, s, d)
    return _reduce_and_update(loss, grad_x, wts.recv, sm, dg_final, me, dict(
        g_mix_norm=(g_mix_norm, m_g_mix_norm, v_g_mix_norm), w_in=(w_in, m_w_in, v_w_in),
        b_forget=(b_forget, m_b_forget, v_b_forget), g_sgu=(g_sgu, m_g_sgu, v_g_sgu),
        w_spatial=(w_spatial, m_w_spatial, v_w_spatial), b_spatial=(b_spatial, m_b_spatial, v_b_spatial),
        g_mla_q=(g_mla_q, m_g_mla_q, v_g_mla_q), w_uq=(w_uq, m_w_uq, v_w_uq), g_mla_kv=(g_mla_kv, m_g_mla_kv, v_g_mla_kv),
        w_ukv=(w_ukv, m_w_ukv, v_w_ukv), g_group_out=(g_group_out, m_g_group_out, v_g_group_out),
        w_out=(w_out, m_w_out, v_w_out), g_ffn_norm=(g_ffn_norm, m_g_ffn_norm, v_g_ffn_norm), w_up=(w_up, m_w_up, v_w_up),
        w_down=(w_down, m_w_down, v_w_down), g_final=(g_final, m_g_final, v_g_final)))


_GATHER_AT = {
    "in_proj0": [(0, "w_up")],
    "fox_attn0": [(0, "w_uq"), (0, "w_ukv"), (0, "w_down")],
    "mla_attn0": [(0, "w_out"), (1, "w_in")],
    "ffn_fwd0": [(1, "w_uq"), (1, "w_ukv"), (1, "w_down")],
    "in_proj1": [(1, "w_up")],
    "fox_attn1": [(1, "w_out")],
}
_SCATTER_AT = {
    "fox_attn_bwd1": [(1, "w_down")],
    "mla_attn_bwd1": [(1, "w_up"), (1, "w_out")],
    "ffn_bwd0": [(1, "w_in")],
    "fox_attn_bwd0": [(0, "w_down")],
    "mla_attn_bwd0": [(0, "w_up"), (0, "w_out")],
    "in_proj_bwd0": [(0, "w_in")],
}


class _FullWeights:
    def __init__(self, per_layer):
        self.per_layer, self.grads = per_layer, {}

    def get(self, l, name):
        return self.per_layer[l][name]

    def comm(self, host):
        return None

    def done(self, host, results):
        pass

    def grad(self, l, name, blocks):
        self.grads[(l, name)] = blocks


class _ShardedWeights(_FullWeights):
    def __init__(self, shards):
        self.shards, self.full, self.grads, self.recv = shards, {}, {}, {}

    def get(self, l, name):
        if name in ("wk", "wv"):
            return _split_ukv(_cols_to_full(self.full[(l, "w_ukv")]))[0 if name == "wk" else 1]
        if name == "wq":
            return _pad_uq(_cols_to_full(self.full[(l, "w_uq")]))
        g = self.full[(l, name)]
        return g if name == "w_up" else g.reshape(NDEV * g.shape[1], g.shape[2])

    def comm(self, host):
        if host in _GATHER_AT:
            return _Comm("gather", [self.shards[k] for k in _GATHER_AT[host]])
        if host in _SCATTER_AT:
            return _Comm("exchange", [self.grads[k] for k in _SCATTER_AT[host]])
        return None

    def done(self, host, results):
        if host in _GATHER_AT:
            self.full.update(zip(_GATHER_AT[host], results))
        if host in _SCATTER_AT:
            self.recv.update(zip(_SCATTER_AT[host], results))


def _local_step(x0, target, wts, smalls, g_final, tb):
    depth = len(smalls)
    s, d = x0.shape
    saved = []
    xl = x0
    for l in range(depth):
        p = smalls[l]
        z, h, got = _norm_matmul(xl, p["g_mix"], wts.get(l, "w_in"), f"in_proj{l}", wts.comm(f"in_proj{l}"))
        wts.done(f"in_proj{l}", got)
        ya = _sgu_fwd(z, p["g_sgu"], p["w_s"], p["b_t"], f"sgu_fwd{l}")
        yb, ret, states = _ret_fwd(z, tb, f"ret_fwd{l}")
        cum = _fox_prep(z, p["b_f"], f"fox_prep{l}")
        kc, vc, vtc = _kv_prep(z, 7, 8, f"fox_kv{l}")
        yc, lse_c, got = _attn_fwd(z, 6, HEAD_DIM, kc, vtc, HEAD_DIM ** -0.5, cum, f"fox_attn{l}", wts.comm(f"fox_attn{l}"))
        wts.done(f"fox_attn{l}", got)
        wq, wk, wv = wts.get(l, "wq"), wts.get(l, "wk"), wts.get(l, "wv")
        qd, kd, vd, vtd, cqn, ckvn = _mla_prep(z, p["gq"], p["gkv"], wq, wk, wv, tb, f"mla_prep{l}")
        yd, lse_d, got = _attn_fwd(qd, 0, 128, kd, vtd, _SCALE_D, None, f"mla_attn{l}", wts.comm(f"mla_attn{l}"))
        wts.done(f"mla_attn{l}", got)
        ys = (ya, yb, yc, yd)
        x1, yn = _out_proj(ys, p["g_go"], wts.get(l, "w_out"), xl, f"out_proj{l}")
        x2, u, h2, got = _ffn_fwd(x1, p["g_ffn"], wts.get(l, "w_up"), wts.get(l, "w_down"), f"ffn_fwd{l}", wts.comm(f"ffn_fwd{l}"))
        wts.done(f"ffn_fwd{l}", got)
        saved.append(dict(x=xl, z=z, h=h, ys=ys, ret=ret, states=states, cum=cum, lse_c=lse_c, kc=kc, vc=vc, qd=qd, kd=kd, vd=vd,
                          cqn=cqn, ckvn=ckvn, lse_d=lse_d, x1=x1, yn=yn, u=u, h2=h2, wq=wq, wk=wk, wv=wv))
        xl = x2

    lrow, dx, dg_final = _loss_head(xl, g_final, target, "loss_head")

    sm = [None] * depth
    for l in reversed(range(depth)):
        p, a = smalls[l], saved[l]
        dx1, du, dg_ffn, got = _ffn_bwd(dx, a["x1"], a["u"], p["g_ffn"], wts.get(l, "w_up"), wts.get(l, "w_down"), f"ffn_bwd{l}",
                                        wts.comm(f"ffn_bwd{l}"))
        wts.done(f"ffn_bwd{l}", got)
        dw_down = _mm_tn(a["u"], dx, f"dw_down{l}", a_fn=lambda t: jnp.square(jnp.maximum(t, 0.0)), out_dtype=_WIRE)
        wts.grad(l, "w_down", dw_down.reshape(NDEV, dw_down.shape[0] // NDEV, d))
        wts.grad(l, "w_up", _mm_tn(a["h2"], du, f"dw_up{l}", blocked=True, out_dtype=_WIRE))
        dya, dyb, dyc, dyd, dg_go = _out_proj_bwd(dx1, wts.get(l, "w_out"), a["ys"], p["g_go"], f"out_proj_bwd{l}")
        wts.grad(l, "w_out", _mm_tn(a["yn"], dx1, f"dw_out{l}", out_dtype=_WIRE).reshape(NDEV, d // NDEV, d))
        dz_a, dg_sgu, dw_s, db_t = _sgu_bwd(dya, a["z"], p["g_sgu"], p["w_s"], p["b_t"], f"sgu_bwd{l}")
        dz_b = _ret_bwd(dyb, a["z"], a["ret"], a["states"], tb, f"ret_bwd{l}")
        qt, dot, dl = _attn_bwd_prep(a["z"], 6, HEAD_DIM, HEAD_DIM ** -0.5, a["ys"][2], dyc, f"fox_bwd_prep{l}")
        dqt_c, dk_c, dv_c, dck, dcq, got = _attn_bwd(a["kc"], a["vc"], qt, dot, a["lse_c"], dl, HEAD_DIM,
                                                     HEAD_DIM ** -0.5, a["cum"], f"fox_attn_bwd{l}", _MXU,
                                                     wts.comm(f"fox_attn_bwd{l}"))
        wts.done(f"fox_attn_bwd{l}", got)
        dq_c = _untranspose(dqt_c, _MXU, f"fox_dq{l}")
        qt, dot, dl = _attn_bwd_prep(a["qd"], 0, 128, _SCALE_D, a["ys"][3], dyd, f"mla_bwd_prep{l}")
        dqt_d, dk_d, dv_d, got = _attn_bwd(a["kd"], a["vd"], qt, dot, a["lse_d"], dl, 128, _SCALE_D, None,
                                           f"mla_attn_bwd{l}", F32, wts.comm(f"mla_attn_bwd{l}"))
        wts.done(f"mla_attn_bwd{l}", got)
        dq_d = _untranspose(dqt_d, F32, f"mla_dq{l}")
        dz_cq, dz_ckv, dkr, dwq, dwk, dwv, dgq, dgkv = _mla_prep_bwd(dq_d, dk_d, dv_d, a["z"], a["cqn"], a["ckvn"], p["gq"],
                                                                     p["gkv"], a["wq"], a["wk"], a["wv"], tb, f"mla_prep_bwd{l}")
        dz_misc, db_f = _fox_post(dcq, dck, a["z"], p["b_f"], dkr, f"fox_post{l}")
        dz = jnp.concatenate([dz_a, dz_b, dq_c, dk_c, dv_c, dz_cq, dz_ckv, dz_misc], axis=1)
        wts.grad(l, "w_in", _unpad_in_cols(_mm_tn(a["h"], dz, f"dw_in{l}", out_dtype=_WIRE)).reshape(NDEV, d // NDEV, N_IN))
        dx, dg_mix, got = _in_proj_bwd(dz, wts.get(l, "w_in"), a["x"], p["g_mix"], dx1, f"in_proj_bwd{l}",
                                       wts.comm(f"in_proj_bwd{l}"))
        wts.done(f"in_proj_bwd{l}", got)
        sm[l] = [dg_mix, dg_go, dg_ffn, dg_sgu, dw_s, db_t[:, :N_HEADS].T, db_f[0, :N_HEADS], dgq, dgkv, _unpad_uq(dwq),
                 _join_ukv(dwk, dwv)]
    return lrow, dx, sm, dg_final


def _reduce_and_update(loss, grad_x, recv, sm, dg_final, me, given):
    depth = len(sm)
    pieces = [t for l in range(depth) for t in sm[l]] + [dg_final]
    flat = jnp.concatenate([t.reshape(-1) for t in pieces])
    n_flat = flat.shape[0]
    unit = NDEV * 8 * 128
    n_pad = -(-n_flat // unit) * unit
    packed = jnp.pad(flat, (0, n_pad - n_flat)).reshape(NDEV, n_pad // (NDEV * 128), 128)
    red = _sum_slots(_exchange([packed], "scatter_small")[0], "sum_small")
    full = _all_gather([red], "gather_small")[0].reshape(-1)
    offs = np.cumsum([0] + [int(np.prod(t.shape)) for t in pieces])
    red_pieces = [full[int(offs[i]):int(offs[i + 1])].reshape(pieces[i].shape) for i in range(len(pieces))]
    per = len(sm[0])
    stack = lambda i: jnp.stack([red_pieces[l * per + i] for l in range(depth)])
    g_small = dict(g_mix_norm=stack(0), g_group_out=stack(1), g_ffn_norm=stack(2), g_sgu=stack(3), w_spatial=stack(4),
                   b_spatial=stack(5), b_forget=stack(6), g_mla_q=stack(7), g_mla_kv=stack(8), g_final=red_pieces[-1])
    cq, ckv = given["w_uq"][0].shape[2], given["w_ukv"][0].shape[2]
    g_small["w_uq"] = lax.dynamic_slice_in_dim(stack(9), me * cq, cq, axis=2)
    g_small["w_ukv"] = lax.dynamic_slice_in_dim(stack(10), me * ckv, ckv, axis=2)

    names = list(given)
    outs = {}
    for nme in names:
        wv_, mv_, vv_ = given[nme]
        shape = wv_.shape
        if nme in ("w_in", "w_out", "w_up", "w_down"):
            res = []
            for l in range(depth):
                parts = recv[(l, nme)]
                two = lambda t: t[l].reshape(-1, shape[-1])
                res.append(_adamw(parts, two(wv_), two(mv_), two(vv_), f"adamw_{nme}{l}"))
            outs[nme] = [jnp.stack([res[l][i] for l in range(depth)]).reshape(shape) for i in range(4)]
        else:
            two = lambda t: t.reshape(-1, shape[-1]) if t.ndim > 1 else t.reshape(1, -1)
            res = _adamw(two(g_small[nme]), two(wv_), two(mv_), two(vv_), f"adamw_{nme}")
            outs[nme] = [r.reshape(shape) for r in res]
    return (loss, grad_x, *[outs[n][0] for n in names], *[outs[n][1] for n in names], *[outs[n][2] for n in names],
            *[outs[n][3] for n in names])
```

```python
import functools

import jax
import jax.numpy as jnp
import numpy as np
from jax import lax
from jax.experimental import pallas as pl
from jax.experimental.pallas import tpu as pltpu

F32 = jnp.float32
_MXU = jnp.bfloat16
_WIRE = jnp.bfloat16
EPS = 1e-6
NDEV = 8
AXES = ("x", "y", "c")
MESH = pl.DeviceIdType.MESH

N_HEADS = 4
HEAD_DIM = 64
GROUP = 256
CHUNK = 128
NZ = 2816
N_IN = 2724
MISC_F, MISC_KR = 0, 32
VMEM_LIMIT = 56 * 1024 * 1024

ADAM_LR, ADAM_B1, ADAM_B2, ADAM_EPS, ADAM_WD, ADAM_STEP = 0.001, 0.9, 0.999, 1e-08, 0.01, 10

SDS = jax.ShapeDtypeStruct


def _cp(*sem):
    return pltpu.CompilerParams(dimension_semantics=sem, vmem_limit_bytes=VMEM_LIMIT)


def _dot(a, b):
    return jnp.dot(a.astype(_MXU), b.astype(_MXU), preferred_element_type=F32)


def _dot_nt(a, b):
    return lax.dot_general(a.astype(_MXU), b.astype(_MXU), (((1,), (1,)), ((), ())), preferred_element_type=F32)


def _dot_tn(a, b):
    return lax.dot_general(a.astype(_MXU), b.astype(_MXU), (((0,), (0,)), ((), ())), preferred_element_type=F32)


def _dot_exact(a, b, dims=(((1,), (0,)), ((), ()))):
    return lax.dot_general(a, b, dims, precision=lax.Precision.HIGHEST, preferred_element_type=F32)


def _rms(x, g):
    return x * lax.rsqrt(jnp.mean(x * x, axis=-1, keepdims=True) + EPS) * g


def _rms_bwd(x, g, dy):
    xh = x * lax.rsqrt(jnp.mean(x * x, axis=-1, keepdims=True) + EPS)
    dxh = dy * g
    r = lax.rsqrt(jnp.mean(x * x, axis=-1, keepdims=True) + EPS)
    dx = r * (dxh - xh * jnp.mean(dxh * xh, axis=-1, keepdims=True))
    return dx, jnp.sum(dy * xh, axis=0, keepdims=True)


def _standardize(t):
    mu = jnp.mean(t, axis=-1, keepdims=True)
    tc = t - mu
    rs = lax.rsqrt(jnp.mean(tc * tc, axis=-1, keepdims=True) + EPS)
    return tc * rs, rs


def _standardize_bwd(yh, rs, dy):
    return rs * (dy - jnp.mean(dy, axis=-1, keepdims=True) - yh * jnp.mean(dy * yh, axis=-1, keepdims=True))


_GELU_C = 0.7978845608028654


def _gelu(x):
    return 0.5 * x * (1.0 + jnp.tanh(_GELU_C * (x + 0.044715 * x * x * x)))


def _gelu_grad(x):
    t = jnp.tanh(_GELU_C * (x + 0.044715 * x * x * x))
    return 0.5 * (1.0 + t) + 0.5 * x * (1.0 - t * t) * _GELU_C * (1.0 + 3 * 0.044715 * x * x)


def _sigmoid(x):
    return 1.0 / (1.0 + jnp.exp(-x))


def _swap_half(t, half):
    n = t.shape[-1]
    lane = lax.broadcasted_iota(jnp.int32, t.shape, t.ndim - 1)
    return jnp.where((lane % (2 * half)) < half, pltpu.roll(t, n - half, t.ndim - 1), pltpu.roll(t, half, t.ndim - 1))


def _rope(t, cos, sin, half):
    return t * cos + _swap_half(t, half) * sin


def _rope_bwd(d, cos, sin, half):
    return d * cos - _swap_half(d, half) * sin


def _tables(s):
    pos = jnp.arange(s, dtype=F32)[:, None]

    def cs(half):
        inv = jnp.power(10000.0, -jnp.arange(half, dtype=F32) / half)
        ang = pos * inv[None, :]
        return jnp.cos(ang), jnp.sin(ang)

    c32, s32 = cs(32)
    c16, s16 = cs(16)
    z = lambda w: jnp.zeros((s, w), F32)
    o = lambda w: jnp.ones((s, w), F32)
    t = {}
    t["b_cos"] = jnp.tile(jnp.concatenate([c32, c32], 1), (1, 4))
    t["b_sin"] = jnp.tile(jnp.concatenate([-s32, s32], 1), (1, 4))
    t["q_cos"] = jnp.tile(jnp.concatenate([o(64), c16, c16, z(32)], 1), (1, 4))
    t["q_sin"] = jnp.tile(jnp.concatenate([z(64), -s16, s16, z(32)], 1), (1, 4))
    t["k_cos"] = jnp.concatenate([z(32), c16, c16, z(64)], 1)
    t["k_sin"] = jnp.concatenate([z(32), -s16, s16, z(64)], 1)
    lg = jnp.log1p(-jnp.exp2(-5.0 - jnp.arange(N_HEADS, dtype=F32)))
    j = jnp.arange(CHUNK, dtype=F32)
    rel = j[:, None] - j[None, :]
    t["decay"] = jnp.where(rel[None] >= 0, jnp.exp(jnp.maximum(rel, 0.0)[None] * lg[:, None, None]), 0.0)

    def rows(e):
        return jnp.repeat(e.T, HEAD_DIM, axis=1)

    t["qw"] = rows(jnp.exp((j + 1.0)[None, :] * lg[:, None]))
    t["kw"] = rows(jnp.exp((CHUNK - 1 - j)[None, :] * lg[:, None]))
    t["kw2"] = rows(jnp.exp((CHUNK - j)[None, :] * lg[:, None]))
    t["qw0"] = rows(jnp.exp(j[None, :] * lg[:, None]))
    t["cd"] = jnp.repeat(jnp.exp(CHUNK * lg), HEAD_DIM)[None, :]
    e = np.zeros((128, 512), np.float32)
    for h in range(N_HEADS):
        for r in range(32):
            e[MISC_KR + r, 128 * h + 64 + r] = 1.0
    t["place"] = jnp.asarray(e)
    return t


def _norm_matmul(x, g, w, name, comm=None):
    s, d = x.shape
    n = w.shape[1]
    tm, tn = min(512, s), 256
    ni = s // tm

    def body(*refs):
        (x_ref, g_ref, w_ref), (z_ref, h_ref), _, cc = _split_refs(refs, 3, 2, comm)
        i = pl.program_id(0)
        _host_gather(comm, cc, i, ni)
        h = _rms(x_ref[...], g_ref[...]).astype(h_ref.dtype)
        h_ref[...] = h
        for j in range(n // tn):
            z_ref[:, tn * j:tn * (j + 1)] = jnp.dot(h, w_ref[:, tn * j:tn * (j + 1)], preferred_element_type=F32)
        if comm is not None:
            @pl.when(i == ni - 1)
            def _():
                comm.wait(*cc)

    in_specs = [pl.BlockSpec((tm, d), lambda i: (i, 0)), pl.BlockSpec((1, d), lambda i: (0, 0)),
                pl.BlockSpec((d, n), lambda i: (0, 0))]
    out_specs = [pl.BlockSpec((tm, n), lambda i: (i, 0)), pl.BlockSpec((tm, d), lambda i: (i, 0))]
    out_shape = [SDS((s, n), F32), SDS((s, d), _MXU)]
    return _call_with_comm(body, (ni,), in_specs, out_specs, out_shape, [], [x, g, w], comm, ("arbitrary",), name)


def _mm_tn(a, b, name, *, a_fn=None, blocked=False, out_dtype=F32):
    k, m = a.shape
    n = b.shape[1]
    tm, tk = min(1024, m), min(512, k)
    tn = next(t for t in (1408, 1024, 512, 256, 128) if n % t == 0)
    assert m % tm == 0 and k % tk == 0
    nk = k // tk

    def body(a_ref, b_ref, o_ref, acc):
        kk = pl.program_id(2)

        @pl.when(kk == 0)
        def _():
            acc[...] = jnp.zeros_like(acc)

        av = a_ref[...]
        if a_fn is not None:
            av = a_fn(av.astype(F32))
        acc[...] += _dot_tn(av, b_ref[...])

        @pl.when(kk == nk - 1)
        def _():
            if blocked:
                for c in range(tn // 512):
                    o_ref[c] = acc[:, 512 * c:512 * (c + 1)].astype(o_ref.dtype)
            else:
                o_ref[...] = acc[...].astype(o_ref.dtype)

    if blocked:
        assert tn % 512 == 0
        out_spec = pl.BlockSpec((tn // 512, tm, 512), lambda i, j, kk: (j, i, 0))
        out_shape = SDS((n // 512, m, 512), out_dtype)
    else:
        out_spec = pl.BlockSpec((tm, tn), lambda i, j, kk: (i, j))
        out_shape = SDS((m, n), out_dtype)
    return pl.pallas_call(
        body, grid=(m // tm, n // tn, nk),
        in_specs=[pl.BlockSpec((tk, tm), lambda i, j, kk: (kk, i)), pl.BlockSpec((tk, tn), lambda i, j, kk: (kk, j))],
        out_specs=out_spec, out_shape=out_shape, scratch_shapes=[pltpu.VMEM((tm, tn), F32)],
        compiler_params=_cp("parallel", "parallel", "arbitrary"), name=name)(a, b)


def _sgu_parts(u_pre, v_pre, gain):
    u = _gelu(u_pre)
    v = _gelu(v_pre)
    vh, rs, vg = [], [], []
    for h in range(N_HEADS):
        sl = slice(HEAD_DIM * h, HEAD_DIM * (h + 1))
        a, r = _standardize(v[:, sl])
        vh.append(a)
        rs.append(r)
        vg.append(a * gain[:, sl])
    return u, vh, rs, vg


def _tril(w):
    r = lax.broadcasted_iota(jnp.int32, w.shape, 0)
    c = lax.broadcasted_iota(jnp.int32, w.shape, 1)
    return jnp.where(r >= c, w, 0.0)


def _sgu_fwd(z, gain, w_s, b_t, name):
    s = z.shape[0]
    tm = min(512, s)

    def body(u_ref, v_ref, g_ref, w_ref, b_ref, y_ref):
        u, _, _, vg = _sgu_parts(u_ref[...], v_ref[...], g_ref[...])
        hc = [(h, c) for h in range(N_HEADS) for c in range(tm // CHUNK)]
        wcs = [_tril(w_ref[h]) for h in range(N_HEADS)]
        mixed = {(h, c): _dot(wcs[h], vg[h][CHUNK * c:CHUNK * (c + 1)]) for h, c in hc}
        for h, c in hc:
            r, sl = slice(CHUNK * c, CHUNK * (c + 1)), slice(HEAD_DIM * h, HEAD_DIM * (h + 1))
            y_ref[r, sl] = u[r, sl] * (mixed[h, c] + b_ref[:, h:h + 1])

    return pl.pallas_call(
        body, grid=(s // tm,),
        in_specs=[pl.BlockSpec((tm, GROUP), lambda i: (i, 0)), pl.BlockSpec((tm, GROUP), lambda i: (i, 1)),
                  pl.BlockSpec((1, GROUP), lambda i: (0, 0)), pl.BlockSpec((N_HEADS, CHUNK, CHUNK), lambda i: (0, 0, 0)),
                  pl.BlockSpec((CHUNK, 128), lambda i: (0, 0))],
        out_specs=pl.BlockSpec((tm, GROUP), lambda i: (i, 0)), out_shape=SDS((s, GROUP), F32),
        compiler_params=_cp("parallel"), name=name)(z, z, gain, w_s, b_t)


def _sgu_bwd(dy, z, gain, w_s, b_t, name):
    s = z.shape[0]
    tm = min(512, s)

    def body(dy_ref, u_ref, v_ref, g_ref, w_ref, b_ref, dz_ref, dg_ref, dw_ref, db_ref):
        @pl.when(pl.program_id(0) == 0)
        def _():
            dg_ref[...] = jnp.zeros_like(dg_ref)
            dw_ref[...] = jnp.zeros_like(dw_ref)
            db_ref[...] = jnp.zeros_like(db_ref)

        u_pre, v_pre, gain_v = u_ref[...], v_ref[...], g_ref[...]
        u, vh, rs, vg = _sgu_parts(u_pre, v_pre, gain_v)
        dyv = dy_ref[...]
        gu = _gelu_grad(u_pre)
        gv = _gelu_grad(v_pre)
        hc = [(h, c) for h in range(N_HEADS) for c in range(tm // CHUNK)]
        sls = [slice(HEAD_DIM * h, HEAD_DIM * (h + 1)) for h in range(N_HEADS)]
        rws = [slice(CHUNK * c, CHUNK * (c + 1)) for c in range(tm // CHUNK)]
        wcs = [_tril(w_ref[h]) for h in range(N_HEADS)]
        mixed = {(h, c): _dot(wcs[h], vg[h][rws[c]]) for h, c in hc}
        dms = {}
        for h, c in hc:
            r, sl = rws[c], sls[h]
            dz_ref[r, sl] = (dyv[r, sl] * (mixed[h, c] + b_ref[:, h:h + 1]) * gu[r, sl]).astype(dz_ref.dtype)
            dms[h, c] = dyv[r, sl] * u[r, sl]
        dws = {(h, c): _dot_nt(dms[h, c], vg[h][rws[c]]) for h, c in hc}
        dvgs = {(h, c): _dot_tn(wcs[h], dms[h, c]) for h, c in hc}
        for h in range(N_HEADS):
            sl = sls[h]
            dwh = jnp.zeros((CHUNK, CHUNK), F32)
            dbh = jnp.zeros((CHUNK, 1), F32)
            dgh = jnp.zeros((1, HEAD_DIM), F32)
            for c in range(tm // CHUNK):
                r = rws[c]
                dwh += dws[h, c]
                dbh += jnp.sum(dms[h, c], axis=1, keepdims=True)
                dvg = dvgs[h, c]
                dgh += jnp.sum(dvg * vh[h][r], axis=0, keepdims=True)
                dv = _standardize_bwd(vh[h][r], rs[h][r], dvg * gain_v[:, sl])
                dz_ref[r, GROUP + HEAD_DIM * h:GROUP + HEAD_DIM * (h + 1)] = (dv * gv[r, sl]).astype(dz_ref.dtype)
            dw_ref[h] += _tril(dwh)
            db_ref[:, h:h + 1] += dbh
            dg_ref[:, sl] += dgh

    return pl.pallas_call(
        body, grid=(s // tm,),
        in_specs=[pl.BlockSpec((tm, GROUP), lambda i: (i, 0)),
                  pl.BlockSpec((tm, GROUP), lambda i: (i, 0)), pl.BlockSpec((tm, GROUP), lambda i: (i, 1)),
                  pl.BlockSpec((1, GROUP), lambda i: (0, 0)), pl.BlockSpec((N_HEADS, CHUNK, CHUNK), lambda i: (0, 0, 0)),
                  pl.BlockSpec((CHUNK, 128), lambda i: (0, 0))],
        out_specs=[pl.BlockSpec((tm, 2 * GROUP), lambda i: (i, 0)), pl.BlockSpec((1, GROUP), lambda i: (0, 0)),
                   pl.BlockSpec((N_HEADS, CHUNK, CHUNK), lambda i: (0, 0, 0)), pl.BlockSpec((CHUNK, 128), lambda i: (0, 0))],
        out_shape=[SDS((s, 2 * GROUP), _MXU), SDS((1, GROUP), F32), SDS((N_HEADS, CHUNK, CHUNK), F32), SDS((CHUNK, 128), F32)],
        compiler_params=_cp("arbitrary"), name=name)(dy, z, z, gain, w_s, b_t)


_SCALE_B = HEAD_DIM ** -0.5


def _ret_fwd(z, tb, name):
    s = z.shape[0]
    nc = s // CHUNK
    row = lambda col: pl.BlockSpec((CHUNK, GROUP), lambda n, col=col: (n, col))
    const = lambda shape: pl.BlockSpec(shape, lambda n: (0,) * len(shape))

    def body(q_ref, k_ref, v_ref, g_ref, cos_ref, sin_ref, dec_ref, qw_ref, kw_ref, cd_ref, y_ref, o_ref, st_ref, state):
        @pl.when(pl.program_id(0) == 0)
        def _():
            state[...] = jnp.zeros_like(state)

        q = _rope(q_ref[...], cos_ref[...], sin_ref[...], 32)
        k = _rope(k_ref[...], cos_ref[...], sin_ref[...], 32) * _SCALE_B
        v = v_ref[...]
        g = g_ref[...]
        st_ref[0] = state[...]
        qs = q * qw_ref[...]
        ks = k * kw_ref[...]
        sls = [slice(HEAD_DIM * h, HEAD_DIM * (h + 1)) for h in range(N_HEADS)]
        scs = [_dot_nt(q[:, sl], k[:, sl]) for sl in sls]
        crs = [_dot(qs[:, sl], state[:, sl]) for sl in sls]
        kvs = [_dot_tn(ks[:, sl], v[:, sl]) for sl in sls]
        scd = [(scs[h] * dec_ref[h]).astype(_MXU) for h in range(N_HEADS)]
        ins = [_dot(scd[h], v[:, sls[h]]) for h in range(N_HEADS)]
        for h, sl in enumerate(sls):
            o = ins[h] + crs[h]
            o_ref[:, sl] = o
            yh, _ = _standardize(o)
            gh = g[:, sl]
            y_ref[:, sl] = gh * _sigmoid(gh) * yh
            state[:, sl] = cd_ref[:, sl] * state[:, sl] + kvs[h]

    return pl.pallas_call(
        body, grid=(nc,),
        in_specs=[row(2), row(3), row(4), row(5), pl.BlockSpec((CHUNK, GROUP), lambda n: (n, 0)),
                  pl.BlockSpec((CHUNK, GROUP), lambda n: (n, 0)), const((N_HEADS, CHUNK, CHUNK)),
                  const((CHUNK, GROUP)), const((CHUNK, GROUP)), const((1, GROUP))],
        out_specs=[pl.BlockSpec((CHUNK, GROUP), lambda n: (n, 0)), pl.BlockSpec((CHUNK, GROUP), lambda n: (n, 0)),
                   pl.BlockSpec((1, HEAD_DIM, GROUP), lambda n: (n, 0, 0))],
        out_shape=[SDS((s, GROUP), F32), SDS((s, GROUP), F32), SDS((nc, HEAD_DIM, GROUP), F32)],
        scratch_shapes=[pltpu.VMEM((HEAD_DIM, GROUP), F32)],
        compiler_params=_cp("arbitrary"), name=name)(z, z, z, z, tb["b_cos"], tb["b_sin"], tb["decay"], tb["qw"], tb["kw"], tb["cd"])


def _ret_bwd(dy, z, o_pre, states, tb, name):
    s = z.shape[0]
    nc = s // CHUNK
    rev = lambda col: pl.BlockSpec((CHUNK, GROUP), lambda n, col=col: (nc - 1 - n, col))
    const = lambda shape: pl.BlockSpec(shape, lambda n: (0,) * len(shape))

    def body(dy_ref, q_ref, k_ref, v_ref, g_ref, o_ref, st_ref, cos_ref, sin_ref, dec_ref, qw_ref, kw2_ref, qw0_ref, cd_ref,
             dz_ref, rstate):
        @pl.when(pl.program_id(0) == 0)
        def _():
            rstate[...] = jnp.zeros_like(rstate)

        cos, sin = cos_ref[...], sin_ref[...]
        q = _rope(q_ref[...], cos, sin, 32)
        k = _rope(k_ref[...], cos, sin, 32) * _SCALE_B
        v = v_ref[...]
        g = g_ref[...]
        dyv = dy_ref[...]
        sg = _sigmoid(g)
        silu = g * sg
        dos, dgs = [], []
        for h in range(N_HEADS):
            sl = slice(HEAD_DIM * h, HEAD_DIM * (h + 1))
            yh, rs = _standardize(o_ref[:, sl])
            dgs.append(dyv[:, sl] * yh * (sg[:, sl] * (1.0 + g[:, sl] * (1.0 - sg[:, sl]))))
            dos.append(_standardize_bwd(yh, rs, dyv[:, sl] * silu[:, sl]))
        do = jnp.concatenate(dos, axis=1)
        dow = do * qw_ref[...]
        vw = v * kw2_ref[...]
        kw = k * kw2_ref[...]
        q0 = q * qw0_ref[...]
        sls = [slice(HEAD_DIM * h, HEAD_DIM * (h + 1)) for h in range(N_HEADS)]
        sn = st_ref[0]
        rrs = [rstate[:, sl] for sl in sls]
        ps = [_dot_nt(q[:, sl], k[:, sl]) for sl in sls]
        dps = [_dot_nt(do[:, sl], v[:, sl]) for sl in sls]
        dq_x = [_dot_nt(dow[:, sl], sn[:, sl]) for sl in sls]
        dk_x = [_dot_nt(vw[:, sl], rrs[h]) for h, sl in enumerate(sls)]
        dv_x = [_dot(kw[:, sl], rrs[h]) for h, sl in enumerate(sls)]
        r_new = [_dot_tn(q0[:, sl], do[:, sl]) for sl in sls]
        pd = [(ps[h] * dec_ref[h]).astype(_MXU) for h in range(N_HEADS)]
        dpd = [(dps[h] * dec_ref[h]).astype(_MXU) for h in range(N_HEADS)]
        dq_i = [_dot(dpd[h], k[:, sl]) for h, sl in enumerate(sls)]
        dk_i = [_dot_tn(dpd[h], q[:, sl]) for h, sl in enumerate(sls)]
        dv_i = [_dot_tn(pd[h], do[:, sl]) for h, sl in enumerate(sls)]
        dqs, dks = [], []
        for h, sl in enumerate(sls):
            dqs.append(dq_i[h] + dq_x[h])
            dks.append(dk_i[h] + dk_x[h])
            dz_ref[:, 2 * GROUP + HEAD_DIM * h:2 * GROUP + HEAD_DIM * (h + 1)] = (dv_i[h] + dv_x[h]).astype(dz_ref.dtype)
            rstate[:, sl] = cd_ref[:, sl] * rrs[h] + r_new[h]
        dq = _rope_bwd(jnp.concatenate(dqs, axis=1), cos, sin, 32)
        dk = _rope_bwd(jnp.concatenate(dks, axis=1) * _SCALE_B, cos, sin, 32)
        dz_ref[:, 0:GROUP] = dq.astype(dz_ref.dtype)
        dz_ref[:, GROUP:2 * GROUP] = dk.astype(dz_ref.dtype)
        dz_ref[:, 3 * GROUP:4 * GROUP] = jnp.concatenate(dgs, axis=1).astype(dz_ref.dtype)

    r0 = lambda: pl.BlockSpec((CHUNK, GROUP), lambda n: (nc - 1 - n, 0))
    return pl.pallas_call(
        body, grid=(nc,),
        in_specs=[r0(), rev(2), rev(3), rev(4), rev(5), r0(), pl.BlockSpec((1, HEAD_DIM, GROUP), lambda n: (nc - 1 - n, 0, 0)),
                  r0(), r0(), const((N_HEADS, CHUNK, CHUNK)), const((CHUNK, GROUP)), const((CHUNK, GROUP)),
                  const((CHUNK, GROUP)), const((1, GROUP))],
        out_specs=pl.BlockSpec((CHUNK, 4 * GROUP), lambda n: (nc - 1 - n, 0)),
        out_shape=SDS((s, 4 * GROUP), _MXU), scratch_shapes=[pltpu.VMEM((HEAD_DIM, GROUP), F32)],
        compiler_params=_cp("arbitrary"), name=name)(
            dy, z, z, z, z, o_pre, states, tb["b_cos"], tb["b_sin"], tb["decay"], tb["qw"], tb["kw2"], tb["qw0"], tb["cd"])


TQ = 256


def _log_sigmoid(x):
    return jnp.minimum(x, 0.0) - jnp.log1p(jnp.exp(-jnp.abs(x)))


def _fox_prep(z, b_f, name):
    s = z.shape[0]
    nb = s // TQ

    def body(m_ref, b_ref, cc_ref, carry):
        @pl.when(pl.program_id(0) == 0)
        def _():
            carry[...] = jnp.zeros_like(carry)

        lane = lax.broadcasted_iota(jnp.int32, (TQ, 128), 1)
        logf = jnp.where(lane < N_HEADS, _log_sigmoid(m_ref[...] + b_ref[...]), 0.0)
        r = lax.broadcasted_iota(jnp.int32, (TQ, TQ), 0)
        c = lax.broadcasted_iota(jnp.int32, (TQ, TQ), 1)
        tri = jnp.where(r >= c, 1.0, 0.0).astype(F32)
        cum = _dot_exact(tri, logf) + carry[...]
        cc_ref[...] = cum * LOG2E
        carry[...] = cum[TQ - 1:TQ, :]

    return pl.pallas_call(
        body, grid=(nb,),
        in_specs=[pl.BlockSpec((TQ, 128), lambda i: (i, NZ // 128 - 1)), pl.BlockSpec((1, 128), lambda i: (0, 0))],
        out_specs=pl.BlockSpec((TQ, 128), lambda i: (i, 0)),
        out_shape=SDS((s, 128), F32), scratch_shapes=[pltpu.VMEM((1, 128), F32)],
        compiler_params=_cp("arbitrary"), name=name)(z, b_f)


def _fox_post(dcr, dcq, z, b_f, dkr, name):
    s = z.shape[0]
    nb = s // TQ

    def body(dc_ref, dcq_ref, m_ref, b_ref, dkr_ref, dz_ref, db_ref, carry):
        @pl.when(pl.program_id(0) == 0)
        def _():
            carry[...] = jnp.zeros_like(carry)
            db_ref[...] = jnp.zeros_like(db_ref)

        r = lax.broadcasted_iota(jnp.int32, (TQ, TQ), 0)
        c = lax.broadcasted_iota(jnp.int32, (TQ, TQ), 1)
        triu = jnp.where(c >= r, 1.0, 0.0).astype(F32)
        dc = jnp.concatenate([dc_ref[0], jnp.zeros((120, TQ), F32)], axis=0)
        dlogf = _dot_exact(triu, dc, (((1,), (1,)), ((), ()))) + _dot_exact(triu, dcq_ref[...]) + carry[...]
        carry[...] = dlogf[0:1, :]
        x = m_ref[...] + b_ref[...]
        lane = lax.broadcasted_iota(jnp.int32, (TQ, 128), 1)
        df = jnp.where(lane < N_HEADS, dlogf * _sigmoid(-x), 0.0)
        db_ref[...] += jnp.sum(df, axis=0, keepdims=True)
        dz_ref[...] = (df + dkr_ref[...]).astype(dz_ref.dtype)

    rv = lambda i: nb - 1 - i
    return pl.pallas_call(
        body, grid=(nb,),
        in_specs=[pl.BlockSpec((1, 8, TQ), lambda i: (rv(i), 0, 0)), pl.BlockSpec((TQ, 128), lambda i: (rv(i), 0)),
                  pl.BlockSpec((TQ, 128), lambda i: (rv(i), NZ // 128 - 1)),
                  pl.BlockSpec((1, 128), lambda i: (0, 0)), pl.BlockSpec((TQ, 128), lambda i: (rv(i), 0))],
        out_specs=[pl.BlockSpec((TQ, 128), lambda i: (rv(i), 0)), pl.BlockSpec((1, 128), lambda i: (0, 0))],
        out_shape=[SDS((s, 128), _MXU), SDS((1, 128), F32)], scratch_shapes=[pltpu.VMEM((1, 128), F32)],
        compiler_params=_cp("arbitrary"), name=name)(dcr, dcq, z, b_f, dkr)


NEG = -1e30


def _causal_mask(shape, transposed=False):
    r = lax.broadcasted_iota(jnp.int32, shape, 0)
    c = lax.broadcasted_iota(jnp.int32, shape, 1)
    return (c >= r) if transposed else (r >= c)


def _head_lanes(h, dqk):
    return slice(128 * (h // 2), 128 * (h // 2) + 128) if dqk == HEAD_DIM else slice(128 * h, 128 * h + 128)


def _keep_half(x, a, axis):
    idx = lax.broadcasted_iota(jnp.int32, x.shape, axis)
    return jnp.where((idx < HEAD_DIM) if a == 0 else (idx >= HEAD_DIM), x, jnp.zeros_like(x))


def _kv_prep(z, kcol, vcol, name):
    s = z.shape[0]
    nk = s // TQ

    def body(k_ref, v_ref, kb_ref, vb_ref, vt_ref):
        kb_ref[...] = k_ref[...].astype(_MXU)
        v = v_ref[...]
        vb_ref[...] = v.astype(_MXU)
        vt_ref[0] = v.T.astype(_MXU)

    blk = pl.BlockSpec((TQ, GROUP), lambda i: (i, 0))
    return pl.pallas_call(
        body, grid=(nk,),
        in_specs=[pl.BlockSpec((TQ, GROUP), lambda i: (i, kcol)), pl.BlockSpec((TQ, GROUP), lambda i: (i, vcol))],
        out_specs=[blk, blk, pl.BlockSpec((1, GROUP, TQ), lambda i: (i, 0, 0))],
        out_shape=[SDS((s, GROUP), _MXU), SDS((s, GROUP), _MXU), SDS((nk, GROUP, TQ), _MXU)],
        compiler_params=_cp("parallel"), name=name)(z, z)


LOG2E = 1.4426950408889634


def _attn_fwd(q, qcol, dqk, kb, vt, scale, ck2, name, comm=None):
    s = q.shape[0]
    nq = s // TQ
    wq = N_HEADS * dqk
    bias = ck2 is not None

    def body(*refs):
        ins, (o_ref, l_ref), _, cc = _split_refs(refs, 4 if bias else 3, 2, comm)
        if bias:
            q_ref, k_ref, vt_ref, cc_ref = ins
        else:
            q_ref, k_ref, vt_ref = ins
        i = pl.program_id(0)
        _host_gather(comm, cc, i, nq)
        qts = []
        for h in range(N_HEADS):
            qt = (q_ref[:, _head_lanes(h, dqk)].astype(F32) * (scale * LOG2E)).T
            qts.append((_keep_half(qt, h % 2, 0) if dqk == HEAD_DIM else qt).astype(_MXU))

        def step(j, carry, masked):
            r0 = pl.multiple_of(j * TQ, TQ)
            vtj = vt_ref[j]
            sts = [jnp.dot(k_ref[pl.ds(r0, TQ), _head_lanes(h, dqk)], qts[h], preferred_element_type=F32)
                   for h in range(N_HEADS)]
            stats, ps = [], []
            for h in range(N_HEADS):
                m, l, _ = carry[3 * h:3 * h + 3]
                st = sts[h]
                if bias:
                    st = st - cc_ref[pl.ds(r0, TQ), h:h + 1]
                if masked:
                    st = jnp.where(_causal_mask(st.shape, transposed=True), st, NEG)
                m_new = jnp.maximum(m, jnp.max(st, axis=0, keepdims=True))
                alpha = jnp.exp2(m - m_new)
                p = jnp.exp2(st - m_new)
                stats.append((m_new, alpha * l + jnp.sum(p, axis=0, keepdims=True), alpha))
                ps.append(p.astype(_MXU))
            out = []
            for h in range(N_HEADS):
                m_new, l, alpha = stats[h]
                acc = alpha * carry[3 * h + 2] + jnp.dot(vtj[HEAD_DIM * h:HEAD_DIM * (h + 1), :], ps[h],
                                                         preferred_element_type=F32)
                out += [m_new, l, acc]
            return tuple(out)

        init = (jnp.full((1, TQ), NEG, F32), jnp.zeros((1, TQ), F32), jnp.zeros((HEAD_DIM, TQ), F32)) * N_HEADS
        carry = lax.fori_loop(0, i, functools.partial(step, masked=False), init)
        carry = step(i, carry, True)
        l_ref[...] = jnp.zeros_like(l_ref)
        for h in range(N_HEADS):
            l_ref[0, h:h + 1, :] = carry[3 * h] + jnp.log2(carry[3 * h + 1])
        for p in range(2):
            ot = jnp.concatenate([carry[6 * p + 2] / carry[6 * p + 1], carry[6 * p + 5] / carry[6 * p + 4]], axis=0)
            o_ref[:, 128 * p:128 * (p + 1)] = ot.T
        if comm is not None:
            @pl.when(i == nq - 1)
            def _():
                comm.wait(*cc)

    rows = pl.BlockSpec((1, 8, TQ), lambda i: (i, 0, 0))
    in_specs = [pl.BlockSpec((TQ, wq), lambda i: (i, qcol)), pl.BlockSpec((s, wq), lambda i: (0, 0)),
                pl.BlockSpec((nq, GROUP, TQ), lambda i: (0, 0, 0))]
    args = [q, kb, vt]
    if bias:
        in_specs.append(pl.BlockSpec((s, 128), lambda i: (0, 0)))
        args.append(ck2)
    out_specs = [pl.BlockSpec((TQ, GROUP), lambda i: (i, 0)), rows]
    out_shape = [SDS((s, GROUP), F32), SDS((nq, 8, TQ), F32)]
    return _call_with_comm(body, (nq,), in_specs, out_specs, out_shape, [], args, comm, ("arbitrary",), name)


def _call_with_comm(body, grid, in_specs, out_specs, out_shape, scratch, args, comm, semantics, name):
    n_out = len(out_shape)
    if comm is not None:
        in_specs, out_specs = in_specs + comm.in_specs, out_specs + comm.out_specs
        out_shape, scratch, args = out_shape + comm.out_shape, scratch + comm.scratch, list(args) + comm.arrs
    res = pl.pallas_call(body, grid=grid, in_specs=in_specs, out_specs=out_specs, out_shape=out_shape,
                         scratch_shapes=scratch, compiler_params=_cp(*semantics), name=name)(*args)
    return (*res[:n_out], list(res[n_out:]))


def _attn_bwd_prep(q, qcol, dqk, scale, o, do, name):
    s = q.shape[0]
    nq = s // TQ
    wq = N_HEADS * dqk

    def body(q_ref, o_ref, do_ref, qt_ref, dot_ref, dl_ref):
        qt_ref[0] = (q_ref[...].astype(F32) * (scale * LOG2E)).T.astype(_MXU)
        dov = do_ref[...]
        dot_ref[0] = dov.T.astype(_MXU)
        pt = (dov * o_ref[...]).T
        dl_ref[...] = jnp.zeros_like(dl_ref)
        for h in range(N_HEADS):
            dl_ref[0, h:h + 1, :] = jnp.sum(pt[HEAD_DIM * h:HEAD_DIM * (h + 1), :], axis=0, keepdims=True)

    nat = lambda w: pl.BlockSpec((TQ, w), lambda i: (i, 0))
    tr = lambda w: pl.BlockSpec((1, w, TQ), lambda i: (i, 0, 0))
    return pl.pallas_call(
        body, grid=(nq,),
        in_specs=[pl.BlockSpec((TQ, wq), lambda i: (i, qcol)), nat(GROUP), nat(GROUP)],
        out_specs=[tr(wq), tr(GROUP), tr(8)],
        out_shape=[SDS((nq, wq, TQ), _MXU), SDS((nq, GROUP, TQ), _MXU), SDS((nq, 8, TQ), F32)],
        compiler_params=_cp("parallel"), name=name)(q, o, do)


def _attn_bwd(kb, vb, qt, dot, lse, dl, dqk, scale, ck2, name, kv_dtype, comm=None):
    s = kb.shape[0]
    nq = s // TQ
    wq = N_HEADS * dqk
    bias = ck2 is not None

    def body(*refs):
        ins, outs, _, cc = _split_refs(refs, 7 if bias else 6, 5 if bias else 3, comm)
        if bias:
            k_ref, v_ref, qt_ref, dot_ref, l_ref, d_ref, cc_ref = ins
            dqt_ref, dk_ref, dv_ref, dck_ref, dcq_ref = outs
        else:
            k_ref, v_ref, qt_ref, dot_ref, l_ref, d_ref = ins
            dqt_ref, dk_ref, dv_ref = outs
        j = pl.program_id(0)

        @pl.when(j == 0)
        def _():
            if comm is not None:
                comm.start(*cc)
            dqt_ref[...] = jnp.zeros_like(dqt_ref)
            if bias:
                dcq_ref[...] = jnp.zeros_like(dcq_ref)

        ks, kts, vs = [], [], []
        for h in range(N_HEADS):
            k2 = k_ref[:, _head_lanes(h, dqk)]
            if dqk == HEAD_DIM:
                k2 = _keep_half(k2, h % 2, 1)
            ks.append(k2)
            kts.append(k2.astype(F32).T.astype(_MXU))
            vs.append(_keep_half(v_ref[:, _head_lanes(h, HEAD_DIM)], h % 2, 1))
        cks = [cc_ref[:, h:h + 1] for h in range(N_HEADS)] if bias else None

        nt = (((1,), (1,)), ((), ()))

        def step(i, carry, masked):
            qti, doti, li, di = qt_ref[i], dot_ref[i], l_ref[i], d_ref[i]
            qls = [_head_lanes(h, dqk) for h in range(N_HEADS)]
            vls = [_head_lanes(h, HEAD_DIM) for h in range(N_HEADS)]
            sts = [jnp.dot(ks[h], qti[qls[h], :], preferred_element_type=F32) for h in range(N_HEADS)]
            dpts = [jnp.dot(vs[h], doti[vls[h], :], preferred_element_type=F32) for h in range(N_HEADS)]
            pbs, dsbs, dcks = [], [], []
            for h in range(N_HEADS):
                st = sts[h] - li[h:h + 1, :]
                if bias:
                    st = st - cks[h]
                p = jnp.exp2(st)
                if masked:
                    p = jnp.where(_causal_mask(p.shape, transposed=True), p, 0.0)
                dst = p * (dpts[h] - di[h:h + 1, :])
                pbs.append(p.astype(_MXU))
                dsbs.append(dst.astype(_MXU))
                if bias:
                    dcks.append(carry[3 * h + 2] + jnp.sum(dst, axis=1, keepdims=True))
                    dcq_ref[i, h:h + 1, :] += jnp.sum(dst, axis=0, keepdims=True)
                else:
                    dcks.append(carry[3 * h + 2])
            out = []
            for h in range(N_HEADS):
                dvt = carry[3 * h + 1] + lax.dot_general(doti[HEAD_DIM * h:HEAD_DIM * (h + 1), :], pbs[h], nt,
                                                         preferred_element_type=F32)
                dkt = carry[3 * h] + lax.dot_general(qti[dqk * h:dqk * (h + 1), :], dsbs[h], nt, preferred_element_type=F32)
                dqt_ref[i, qls[h], :] += jnp.dot(kts[h], dsbs[h], preferred_element_type=F32) * scale
                out += [dkt, dvt, dcks[h]]
            return tuple(out)

        init = (jnp.zeros((dqk, TQ), F32), jnp.zeros((HEAD_DIM, TQ), F32), jnp.zeros((TQ, 1), F32)) * N_HEADS
        carry = step(j, init, True)
        carry = lax.fori_loop(j + 1, nq, functools.partial(step, masked=False), carry)
        for p in range(2):
            dv_ref[:, 128 * p:128 * (p + 1)] = jnp.concatenate([carry[6 * p + 1], carry[6 * p + 4]], axis=0).T.astype(dv_ref.dtype)
            if dqk == HEAD_DIM:
                dk_ref[:, 128 * p:128 * (p + 1)] = (jnp.concatenate([carry[6 * p], carry[6 * p + 3]], axis=0).T
                                                    * (1.0 / LOG2E)).astype(dk_ref.dtype)
        if dqk != HEAD_DIM:
            for h in range(N_HEADS):
                dk_ref[:, 128 * h:128 * (h + 1)] = (carry[3 * h].T * (1.0 / LOG2E)).astype(dk_ref.dtype)
        if bias:
            dck_ref[...] = jnp.zeros_like(dck_ref)
            for h in range(N_HEADS):
                dck_ref[:, h:h + 1] = -carry[3 * h + 2]
        if comm is not None:
            @pl.when(j == nq - 1)
            def _():
                comm.wait(*cc)

    blk = lambda w: pl.BlockSpec((TQ, w), lambda j: (j, 0))
    full3 = lambda w: pl.BlockSpec((nq, w, TQ), lambda j: (0, 0, 0))
    in_specs = [blk(wq), blk(GROUP), full3(wq), full3(GROUP), full3(8), full3(8)]
    args = [kb, vb, qt, dot, lse, dl]
    out_specs = [full3(wq), blk(wq), blk(GROUP)]
    out_shape = [SDS((nq, wq, TQ), F32), SDS((s, wq), kv_dtype), SDS((s, GROUP), kv_dtype)]
    if bias:
        in_specs.append(blk(128))
        args.append(ck2)
        out_specs += [blk(128), full3(8)]
        out_shape += [SDS((s, 128), F32), SDS((nq, 8, TQ), F32)]
    return _call_with_comm(body, (nq,), in_specs, out_specs, out_shape, [], args, comm, ("arbitrary",), name)


def _untranspose(xt, dtype, name):
    nq, w, _ = xt.shape

    def body(x_ref, o_ref):
        o_ref[...] = x_ref[0].T.astype(o_ref.dtype)

    return pl.pallas_call(
        body, grid=(nq,), in_specs=[pl.BlockSpec((1, w, TQ), lambda i: (i, 0, 0))],
        out_specs=pl.BlockSpec((TQ, w), lambda i: (i, 0)), out_shape=SDS((nq * TQ, w), dtype),
        compiler_params=_cp("parallel"), name=name)(xt)


_SCALE_D = (64 + 32) ** -0.5
_COL_CQ, _COL_CKV, _COL_MISC = 2304 // 256, 2560 // 128, 2688 // 128


def _mla_prep(z, gq, gkv, wq, wk, wv, tb, name):
    s = z.shape[0]
    tm = TQ
    row = lambda w, c: pl.BlockSpec((tm, w), lambda i, c=c: (i, c))
    const = lambda a: pl.BlockSpec(a.shape, lambda i: (0,) * a.ndim)

    def body(cq_ref, ckv_ref, m_ref, gq_ref, gkv_ref, wq_ref, wk_ref, wv_ref, e_ref, qc_ref, qs_ref, kc_ref, ks_ref,
             q_ref, k_ref, v_ref, vt_ref, cqn_ref, ckvn_ref):
        cqn = _rms(cq_ref[...], gq_ref[...]).astype(_MXU)
        ckvn = _rms(ckv_ref[...], gkv_ref[...]).astype(_MXU)
        cqn_ref[...] = cqn
        ckvn_ref[...] = ckvn
        q_ref[...] = _rope(_dot(cqn, wq_ref[...]), qc_ref[...], qs_ref[...], 16).astype(q_ref.dtype)
        kr = _rope(m_ref[...], kc_ref[...], ks_ref[...], 16)
        k_ref[...] = (_dot(ckvn, wk_ref[...]) + _dot(kr, e_ref[...])).astype(k_ref.dtype)
        v = _dot(ckvn, wv_ref[...])
        v_ref[...] = v.astype(v_ref.dtype)
        vt_ref[0] = v.T.astype(vt_ref.dtype)

    e = tb["place"]
    return pl.pallas_call(
        body, grid=(s // tm,),
        in_specs=[row(256, _COL_CQ), row(128, _COL_CKV), row(128, _COL_MISC), const(gq), const(gkv), const(wq), const(wk),
                  const(wv), const(e), row(512, 0), row(512, 0), row(128, 0), row(128, 0)],
        out_specs=[row(512, 0), row(512, 0), row(256, 0), pl.BlockSpec((1, GROUP, TQ), lambda i: (i, 0, 0)), row(256, 0),
                   row(128, 0)],
        out_shape=[SDS((s, 512), _MXU), SDS((s, 512), _MXU), SDS((s, 256), _MXU), SDS((s // TQ, GROUP, TQ), _MXU),
                   SDS((s, 256), _MXU), SDS((s, 128), _MXU)],
        compiler_params=_cp("parallel"), name=name)(
            z, z, z, gq, gkv, wq, wk, wv, e, tb["q_cos"], tb["q_sin"], tb["k_cos"], tb["k_sin"])


def _mla_prep_bwd(dq, dk, dv, z, cqn, ckvn, gq, gkv, wq, wk, wv, tb, name):
    s = z.shape[0]
    tm = min(512, s)
    row = lambda w, c: pl.BlockSpec((tm, w), lambda i, c=c: (i, c))
    const = lambda a: pl.BlockSpec(a.shape, lambda i: (0,) * a.ndim)
    acc = lambda shape: pl.BlockSpec(shape, lambda i: (0, 0))

    def body(dq_ref, dk_ref, dv_ref, cq_ref, ckv_ref, cqn_ref, ckvn_ref, gq_ref, gkv_ref, wq_ref, wk_ref, wv_ref, e_ref,
             qc_ref, qs_ref, kc_ref, ks_ref, dcq_ref, dckv_ref, dkr_ref, dwq_ref, dwk_ref, dwv_ref, dgq_ref, dgkv_ref):
        @pl.when(pl.program_id(0) == 0)
        def _():
            for r in (dwq_ref, dwk_ref, dwv_ref, dgq_ref, dgkv_ref):
                r[...] = jnp.zeros_like(r)

        dqp = _rope_bwd(dq_ref[...], qc_ref[...], qs_ref[...], 16)
        dkd = dk_ref[...]
        dvd = dv_ref[...]
        dwq_ref[...] += _dot_tn(cqn_ref[...], dqp)
        dwk_ref[...] += _dot_tn(ckvn_ref[...], dkd)
        dwv_ref[...] += _dot_tn(ckvn_ref[...], dvd)
        dcq, dgq = _rms_bwd(cq_ref[...], gq_ref[...], _dot_nt(dqp, wq_ref[...]))
        dckv, dgkv = _rms_bwd(ckv_ref[...], gkv_ref[...], _dot_nt(dkd, wk_ref[...]) + _dot_nt(dvd, wv_ref[...]))
        dcq_ref[...] = dcq.astype(dcq_ref.dtype)
        dckv_ref[...] = dckv.astype(dckv_ref.dtype)
        dgq_ref[...] += dgq
        dgkv_ref[...] += dgkv
        dkr = _dot_exact(dkd, e_ref[...], (((1,), (1,)), ((), ())))
        dkr_ref[...] = _rope_bwd(dkr, kc_ref[...], ks_ref[...], 16)

    e = tb["place"]
    return pl.pallas_call(
        body, grid=(s // tm,),
        in_specs=[row(512, 0), row(512, 0), row(256, 0), row(256, _COL_CQ), row(128, _COL_CKV), row(256, 0), row(128, 0),
                  const(gq), const(gkv), const(wq), const(wk), const(wv), const(e), row(512, 0), row(512, 0), row(128, 0), row(128, 0)],
        out_specs=[row(256, 0), row(128, 0), row(128, 0), acc((256, 512)), acc((128, 512)), acc((128, 256)), acc((1, 256)),
                   acc((1, 128))],
        out_shape=[SDS((s, 256), _MXU), SDS((s, 128), _MXU), SDS((s, 128), F32), SDS((256, 512), F32), SDS((128, 512), F32),
                   SDS((128, 256), F32), SDS((1, 256), F32), SDS((1, 128), F32)],
        compiler_params=_cp("arbitrary"), name=name)(
            dq, dk, dv, z, z, cqn, ckvn, gq, gkv, wq, wk, wv, e, tb["q_cos"], tb["q_sin"], tb["k_cos"], tb["k_sin"])


def _out_proj(ys, g, w, x, name):
    s, d = x.shape
    tm = min(512, s)

    def body(ya, yb, yc, yd, g_ref, w_ref, x_ref, o_ref, yn_ref):
        acc = x_ref[...]
        for i, y_ref in enumerate((ya, yb, yc, yd)):
            sl = slice(GROUP * i, GROUP * (i + 1))
            yn = _rms(y_ref[...], g_ref[:, sl]).astype(_MXU)
            yn_ref[:, sl] = yn
            acc = acc + jnp.dot(yn, w_ref[sl, :], preferred_element_type=F32)
        o_ref[...] = acc

    yspec = pl.BlockSpec((tm, GROUP), lambda i: (i, 0))
    return pl.pallas_call(
        body, grid=(s // tm,),
        in_specs=[yspec, yspec, yspec, yspec, pl.BlockSpec((1, d), lambda i: (0, 0)), pl.BlockSpec((d, d), lambda i: (0, 0)),
                  pl.BlockSpec((tm, d), lambda i: (i, 0))],
        out_specs=[pl.BlockSpec((tm, d), lambda i: (i, 0)), pl.BlockSpec((tm, d), lambda i: (i, 0))],
        out_shape=[SDS((s, d), F32), SDS((s, d), _MXU)], compiler_params=_cp("parallel"), name=name)(*ys, g, w, x)


def _out_proj_bwd(dx, w, ys, g, name):
    s, d = dx.shape
    tm = min(512, s)

    def body(dx_ref, w_ref, ya, yb, yc, yd, g_ref, da, db, dc, dd, dg_ref):
        @pl.when(pl.program_id(0) == 0)
        def _():
            dg_ref[...] = jnp.zeros_like(dg_ref)

        dyn = _dot_nt(dx_ref[...], w_ref[...])
        outs = (da, db, dc, dd)
        for i, y_ref in enumerate((ya, yb, yc, yd)):
            sl = slice(GROUP * i, GROUP * (i + 1))
            dy, dg = _rms_bwd(y_ref[...], g_ref[:, sl], dyn[:, sl])
            outs[i][...] = dy
            dg_ref[:, sl] += dg

    yspec = pl.BlockSpec((tm, GROUP), lambda i: (i, 0))
    return pl.pallas_call(
        body, grid=(s // tm,),
        in_specs=[pl.BlockSpec((tm, d), lambda i: (i, 0)), pl.BlockSpec((d, d), lambda i: (0, 0)), yspec, yspec, yspec, yspec,
                  pl.BlockSpec((1, d), lambda i: (0, 0))],
        out_specs=[yspec, yspec, yspec, yspec, pl.BlockSpec((1, d), lambda i: (0, 0))],
        out_shape=[SDS((s, GROUP), F32)] * 4 + [SDS((1, d), F32)],
        compiler_params=_cp("arbitrary"), name=name)(dx, w, *ys, g)


FF_BLOCK = 512
FF_ROWS = 1024


def _ffn_fwd(x, g, wu, wd, name, comm=None):
    s, d = x.shape
    nj = wu.shape[0]
    tm = min(FF_ROWS, s)
    ni = s // tm

    def body(*refs):
        (x_ref, g_ref, wu_ref, wd_ref), (o_ref, u_ref, h_ref), (acc,), cc = _split_refs(refs, 4, 3, comm)
        i, j = pl.program_id(0), pl.program_id(1)
        _host_gather(comm, cc, i * nj + j, ni * nj)

        @pl.when(j == 0)
        def _():
            h_ref[...] = _rms(x_ref[...], g_ref[...]).astype(h_ref.dtype)
            acc[...] = jnp.zeros_like(acc)

        halves = [slice(r, r + tm // 2) for r in range(0, tm, tm // 2)]
        us = [jnp.dot(h_ref[r, :], wu_ref[0], preferred_element_type=F32) for r in halves]
        for r, u in zip(halves, us):
            u_ref[r, :] = u.astype(u_ref.dtype)
            acc[r, :] += _dot(jnp.square(jnp.maximum(u, 0.0)), wd_ref[...])

        @pl.when(j == nj - 1)
        def _():
            o_ref[...] = x_ref[...] + acc[...]

        if comm is not None:
            @pl.when((i == ni - 1) & (j == nj - 1))
            def _():
                comm.wait(*cc)

    in_specs = [pl.BlockSpec((tm, d), lambda i, j: (i, 0)), pl.BlockSpec((1, d), lambda i, j: (0, 0)),
                pl.BlockSpec((1, d, FF_BLOCK), lambda i, j: (j, 0, 0)), pl.BlockSpec((FF_BLOCK, d), lambda i, j: (j, 0))]
    out_specs = [pl.BlockSpec((tm, d), lambda i, j: (i, 0)), pl.BlockSpec((tm, FF_BLOCK), lambda i, j: (i, j)),
                 pl.BlockSpec((tm, d), lambda i, j: (i, 0))]
    out_shape = [SDS((s, d), F32), SDS((s, nj * FF_BLOCK), _MXU), SDS((s, d), _MXU)]
    return _call_with_comm(body, (ni, nj), in_specs, out_specs, out_shape, [pltpu.VMEM((tm, d), F32)], [x, g, wu, wd], comm,
                           ("arbitrary", "arbitrary"), name)


def _ffn_bwd(dx2, x, u, g, wu, wd, name, comm=None):
    s, d = x.shape
    nj = wu.shape[0]
    tm = min(FF_ROWS, s)
    ni = s // tm

    def body(*refs):
        (dx_ref, x_ref, u_ref, g_ref, wu_ref, wd_ref), (o_ref, du_ref, dg_ref), (acc, dxb), cc = _split_refs(refs, 6, 3, comm)
        i, j = pl.program_id(0), pl.program_id(1)

        @pl.when((i == 0) & (j == 0))
        def _():
            if comm is not None:
                comm.start(*cc)
            dg_ref[...] = jnp.zeros_like(dg_ref)

        @pl.when(j == 0)
        def _():
            dxb[...] = dx_ref[...].astype(dxb.dtype)
            acc[...] = jnp.zeros_like(acc)

        nt = (((1,), (1,)), ((), ()))
        halves = [slice(r, r + tm // 2) for r in range(0, tm, tm // 2)]
        das = [lax.dot_general(dxb[r, :], wd_ref[...], nt, preferred_element_type=F32) for r in halves]
        for r, da in zip(halves, das):
            du = (da * 2.0 * jnp.maximum(u_ref[r, :].astype(F32), 0.0)).astype(du_ref.dtype)
            du_ref[r, :] = du
            acc[r, :] += lax.dot_general(du, wu_ref[0], nt, preferred_element_type=F32)

        @pl.when(j == nj - 1)
        def _():
            dxn, dg = _rms_bwd(x_ref[...], g_ref[...], acc[...])
            o_ref[...] = dx_ref[...] + dxn
            dg_ref[...] += dg

        if comm is not None:
            @pl.when((i == ni - 1) & (j == nj - 1))
            def _():
                comm.wait(*cc)

    in_specs = [pl.BlockSpec((tm, d), lambda i, j: (i, 0)), pl.BlockSpec((tm, d), lambda i, j: (i, 0)),
                pl.BlockSpec((tm, FF_BLOCK), lambda i, j: (i, j)), pl.BlockSpec((1, d), lambda i, j: (0, 0)),
                pl.BlockSpec((1, d, FF_BLOCK), lambda i, j: (j, 0, 0)), pl.BlockSpec((FF_BLOCK, d), lambda i, j: (j, 0))]
    out_specs = [pl.BlockSpec((tm, d), lambda i, j: (i, 0)), pl.BlockSpec((tm, FF_BLOCK), lambda i, j: (i, j)),
                 pl.BlockSpec((1, d), lambda i, j: (0, 0))]
    out_shape = [SDS((s, d), F32), SDS((s, nj * FF_BLOCK), _MXU), SDS((1, d), F32)]
    return _call_with_comm(body, (ni, nj), in_specs, out_specs, out_shape,
                           [pltpu.VMEM((tm, d), F32), pltpu.VMEM((tm, d), _MXU)], [dx2, x, u, g, wu, wd], comm,
                           ("arbitrary", "arbitrary"), name)


def _in_proj_bwd(dz, w, x, g, dx_up, name, comm=None):
    s, d = x.shape
    n = w.shape[1]
    tm = min(512, s)
    ni = s // tm

    def body(*refs):
        (dz_ref, w_ref, x_ref, g_ref, up_ref), (o_ref, dg_ref), _, cc = _split_refs(refs, 5, 2, comm)
        i = pl.program_id(0)

        @pl.when(i == 0)
        def _():
            if comm is not None:
                comm.start(*cc)
            dg_ref[...] = jnp.zeros_like(dg_ref)

        dh = lax.dot_general(dz_ref[...], w_ref[...], (((1,), (1,)), ((), ())), preferred_element_type=F32)
        dxn, dg = _rms_bwd(x_ref[...], g_ref[...], dh)
        o_ref[...] = up_ref[...] + dxn
        dg_ref[...] += dg
        if comm is not None:
            @pl.when(i == ni - 1)
            def _():
                comm.wait(*cc)

    in_specs = [pl.BlockSpec((tm, n), lambda i: (i, 0)), pl.BlockSpec((d, n), lambda i: (0, 0)),
                pl.BlockSpec((tm, d), lambda i: (i, 0)), pl.BlockSpec((1, d), lambda i: (0, 0)),
                pl.BlockSpec((tm, d), lambda i: (i, 0))]
    out_specs = [pl.BlockSpec((tm, d), lambda i: (i, 0)), pl.BlockSpec((1, d), lambda i: (0, 0))]
    out_shape = [SDS((s, d), F32), SDS((1, d), F32)]
    return _call_with_comm(body, (ni,), in_specs, out_specs, out_shape, [], [dz, w, x, g, dx_up], comm, ("arbitrary",), name)


def _loss_head(x, g, target, name):
    s, d = x.shape
    tm = min(512, s)

    def body(x_ref, g_ref, t_ref, l_ref, dx_ref, dg_ref):
        @pl.when(pl.program_id(0) == 0)
        def _():
            l_ref[...] = jnp.zeros_like(l_ref)
            dg_ref[...] = jnp.zeros_like(dg_ref)

        xv = x_ref[...]
        err = _rms(xv, g_ref[...]) - t_ref[...]
        l_ref[...] += jnp.sum(err * err, axis=0, keepdims=True) * (0.5 / d)
        dx, dg = _rms_bwd(xv, g_ref[...], err * (1.0 / d))
        dx_ref[...] = dx
        dg_ref[...] += dg

    return pl.pallas_call(
        body, grid=(s // tm,),
        in_specs=[pl.BlockSpec((tm, d), lambda i: (i, 0)), pl.BlockSpec((1, d), lambda i: (0, 0)),
                  pl.BlockSpec((tm, d), lambda i: (i, 0))],
        out_specs=[pl.BlockSpec((1, d), lambda i: (0, 0)), pl.BlockSpec((tm, d), lambda i: (i, 0)),
                   pl.BlockSpec((1, d), lambda i: (0, 0))],
        out_shape=[SDS((1, d), F32), SDS((s, d), F32), SDS((1, d), F32)], compiler_params=_cp("arbitrary"), name=name)(x, g, target)


def _me_and_peer():
    x, y, c = lax.axis_index("x"), lax.axis_index("y"), lax.axis_index("c")
    me = 4 * x + 2 * y + c

    def peer(k):
        px, py, pc = x ^ (k >> 2), y ^ ((k >> 1) & 1), c ^ (k & 1)
        return (px, py, pc), 4 * px + 2 * py + pc

    return me, peer


class _Comm:
    CHIPS = (2, 4, 6)

    def __init__(self, kind, arrs):
        assert kind in ("gather", "exchange")
        self.kind, self.arrs, self.n = kind, list(arrs), len(arrs)
        anyspec = pl.BlockSpec(memory_space=pl.ANY)
        self.in_specs = [anyspec] * self.n
        self.out_specs = [anyspec] * self.n
        self.out_shape = [SDS(((NDEV,) + a.shape) if kind == "gather" else a.shape, a.dtype) for a in self.arrs]
        npair = NDEV - 1 + len(self.CHIPS)
        self.scratch = [pltpu.SemaphoreType.DMA((self.n, npair)), pltpu.SemaphoreType.DMA((self.n, npair)),
                        pltpu.SemaphoreType.DMA((self.n,))]

    def _copies(self, ins, outs, sems):
        send, recv, loc = sems
        me, peer = _me_and_peer()
        gather = self.kind == "gather"
        sibling = peer(1)[0]
        local = [pltpu.make_async_copy(ins[a] if gather else ins[a].at[me], outs[a].at[me], loc.at[a]) for a in range(self.n)]
        outgoing, incoming, forwards, forwarded = [], [], [], []
        for k in ((1,) + self.CHIPS) if gather else range(1, NDEV):
            dev, pid = peer(k)
            for a in range(self.n):
                pair = dict(send_sem=send.at[a, k - 1], recv_sem=recv.at[a, k - 1], device_id=dev, device_id_type=MESH)
                outgoing.append(pltpu.make_async_remote_copy(src_ref=ins[a] if gather else ins[a].at[pid],
                                                             dst_ref=outs[a].at[me], **pair))
                incoming.append(pltpu.make_async_remote_copy(src_ref=ins[a] if gather else ins[a].at[me],
                                                             dst_ref=outs[a].at[pid], **pair))
        if gather:
            for idx, k in enumerate(self.CHIPS):
                got, theirs = peer(k)[1], peer(k + 1)[1]
                for a in range(self.n):
                    pair = dict(send_sem=send.at[a, NDEV - 1 + idx], recv_sem=recv.at[a, NDEV - 1 + idx], device_id=sibling,
                                device_id_type=MESH)
                    forwards.append(pltpu.make_async_remote_copy(src_ref=outs[a].at[got], dst_ref=outs[a].at[got], **pair))
                    forwarded.append(pltpu.make_async_remote_copy(src_ref=outs[a].at[theirs], dst_ref=outs[a].at[theirs], **pair))
        return local, outgoing, incoming, forwards, forwarded

    def start(self, ins, outs, sems):
        local, outgoing, _, _, _ = self._copies(ins, outs, sems)
        for cp in local + outgoing:
            cp.start()

    def forward(self, ins, outs, sems):
        _, _, incoming, forwards, _ = self._copies(ins, outs, sems)
        per = self.n
        for idx in range(len(forwards) // per if per else 0):
            for a in range(per):
                incoming[(1 + idx) * per + a].wait_recv()
                forwards[idx * per + a].start()

    def wait(self, ins, outs, sems):
        local, outgoing, incoming, forwards, forwarded = self._copies(ins, outs, sems)
        for cp in (incoming[:self.n] if self.kind == "gather" else incoming) + forwarded:
            cp.wait_recv()
        for cp in outgoing + forwards:
            cp.wait_send()
        for cp in local:
            cp.wait()


def _host_gather(comm, cc, step, nsteps):
    if comm is None:
        return

    @pl.when(step == 0)
    def _():
        comm.start(*cc)

    @pl.when(step == (2 * nsteps) // 3)
    def _():
        comm.forward(*cc)


def _split_refs(refs, n_in, n_out, comm):
    c = comm.n if comm is not None else 0
    ins, cin = refs[:n_in], refs[n_in:n_in + c]
    outs, cout = refs[n_in + c:n_in + c + n_out], refs[n_in + c + n_out:n_in + 2 * c + n_out]
    rest = refs[n_in + 2 * c + n_out:]
    scratch, csem = (rest[:len(rest) - 3], rest[len(rest) - 3:]) if c else (rest, ())
    return ins, outs, scratch, (cin, cout, csem)


def _comm_call(kind, arrs, name):
    comm = _Comm(kind, arrs)

    def body(*refs):
        _, _, _, c = _split_refs(refs, 0, 0, comm)
        comm.start(*c)
        if kind == "gather":
            comm.forward(*c)
        comm.wait(*c)

    return pl.pallas_call(body, in_specs=comm.in_specs, out_specs=comm.out_specs, out_shape=comm.out_shape,
                          scratch_shapes=comm.scratch, compiler_params=pltpu.CompilerParams(has_side_effects=True),
                          name=name)(*arrs)


def _all_gather(arrs, name):
    return _comm_call("gather", arrs, name)


def _exchange(arrs, name):
    return _comm_call("exchange", arrs, name)


def _sum_slots(parts, name):
    _, r, c = parts.shape
    tr = r if r <= 512 else 512

    def body(p_ref, o_ref):
        acc = p_ref[0].astype(F32)
        for q in range(1, NDEV):
            acc = acc + p_ref[q].astype(F32)
        o_ref[...] = acc

    return pl.pallas_call(
        body, grid=(r // tr,), in_specs=[pl.BlockSpec((NDEV, tr, c), lambda i: (0, i, 0))],
        out_specs=pl.BlockSpec((tr, c), lambda i: (i, 0)), out_shape=SDS((r, c), F32),
        compiler_params=_cp("parallel"), name=name)(parts)


def _adamw(g, w, m, v, name):
    r, c = w.shape
    parts = g.ndim == 3
    tr = r
    for cand in (512, 256, 128, 64, 32, 16, 8):
        if r > cand and r % cand == 0 and cand * c * 4 <= 2 * 1024 * 1024:
            tr = cand
            break
    bc1 = 1.0 / (1.0 - ADAM_B1 ** ADAM_STEP)
    bc2 = 1.0 / (1.0 - ADAM_B2 ** ADAM_STEP)

    def body(g_ref, w_ref, m_ref, v_ref, go_ref, d_ref, mo_ref, vo_ref):
        if parts:
            gv = g_ref[0].astype(F32)
            for q in range(1, NDEV):
                gv = gv + g_ref[q].astype(F32)
        else:
            gv = g_ref[...]
        mn = ADAM_B1 * m_ref[...] + (1.0 - ADAM_B1) * gv
        vn = ADAM_B2 * v_ref[...] + (1.0 - ADAM_B2) * (gv * gv)
        go_ref[...] = gv
        mo_ref[...] = mn
        vo_ref[...] = vn
        d_ref[...] = -ADAM_LR * ((mn * bc1) / (jnp.sqrt(vn * bc2) + ADAM_EPS) + ADAM_WD * w_ref[...])

    spec = pl.BlockSpec((tr, c), lambda i: (i, 0))
    gspec = pl.BlockSpec((NDEV, tr, c), lambda i: (0, i, 0)) if parts else spec
    return pl.pallas_call(
        body, grid=(r // tr,), in_specs=[gspec, spec, spec, spec], out_specs=[spec] * 4,
        out_shape=[SDS((r, c), F32)] * 4, compiler_params=_cp("parallel"), name=name)(g, w, m, v)


def _pad_in_cols(w):
    r = w.shape[0]
    zeros = lambda n: jnp.zeros((r, n), w.dtype)
    return jnp.concatenate([w[:, :2304], w[:, 2308:2692], w[:, 2304:2308], zeros(28), w[:, 2692:2724], zeros(64)], axis=1)


def _unpad_in_cols(w):
    return jnp.concatenate([w[..., :2304], w[..., 2688:2692], w[..., 2304:2688], w[..., 2720:2752]], axis=-1)


def _pad_uq(w):
    return jnp.pad(w.reshape(256, N_HEADS, 96), ((0, 0), (0, 0), (0, 32))).reshape(256, 512)


def _unpad_uq(w):
    return w.reshape(256, N_HEADS, 128)[:, :, :96].reshape(256, 384)


def _split_ukv(w):
    r = w.reshape(128, N_HEADS, 128)
    return jnp.pad(r[:, :, :64], ((0, 0), (0, 0), (0, 64))).reshape(128, 512), r[:, :, 64:].reshape(128, 256)


def _join_ukv(dk, dv):
    return jnp.concatenate([dk.reshape(128, N_HEADS, 128)[:, :, :64], dv.reshape(128, N_HEADS, 64)], axis=-1).reshape(128, 512)


def _cols_to_full(g):
    return jnp.transpose(g, (1, 0, 2)).reshape(g.shape[1], NDEV * g.shape[2])


def kernel(x, g_mix_norm, w_in, b_forget, g_sgu, w_spatial, b_spatial, g_mla_q, w_uq, g_mla_kv, w_ukv, g_group_out, w_out, g_ffn_norm, w_up, w_down, g_final, loss_target, m_g_mix_norm, m_w_in, m_b_forget, m_g_sgu, m_w_spatial, m_b_spatial, m_g_mla_q, m_w_uq, m_g_mla_kv, m_w_ukv, m_g_group_out, m_w_out, m_g_ffn_norm, m_w_up, m_w_down, m_g_final, v_g_mix_norm, v_w_in, v_b_forget, v_g_sgu, v_w_spatial, v_b_spatial, v_g_mla_q, v_w_uq, v_g_mla_kv, v_w_ukv, v_g_group_out, v_w_out, v_g_ffn_norm, v_w_up, v_w_down, v_g_final):
    depth = w_in.shape[0]
    s, d = x.shape[1], x.shape[2]
    x0 = x.reshape(s, d)
    target = loss_target.reshape(s, d)
    tb = _tables(s)
    me = 4 * lax.axis_index("x") + 2 * lax.axis_index("y") + lax.axis_index("c")

    assert depth == 2
    shards = {}
    for l in range(depth):
        shards.update({(l, "w_in"): _pad_in_cols(w_in[l]).astype(_WIRE), (l, "w_out"): w_out[l].astype(_WIRE),
                       (l, "w_up"): w_up[l].astype(_WIRE), (l, "w_down"): w_down[l].astype(_WIRE),
                       (l, "w_uq"): w_uq[l].astype(_WIRE), (l, "w_ukv"): w_ukv[l].astype(_WIRE)})
    wts = _ShardedWeights(shards)
    wts.full[(0, "w_in")] = _all_gather([shards[(0, "w_in")]], "gather_w_in0")[0]

    row = lambda a: a.reshape(1, -1)

    def small(l):
        bf = jnp.pad(b_forget[l].reshape(1, N_HEADS), ((0, 0), (0, 128 - N_HEADS)))
        bt = jnp.pad(b_spatial[l].T, ((0, 0), (0, 128 - N_HEADS)))
        return dict(g_mix=row(g_mix_norm[l]), g_sgu=row(g_sgu[l]), w_s=w_spatial[l], b_t=bt, b_f=bf, gq=row(g_mla_q[l]),
                    gkv=row(g_mla_kv[l]), g_go=row(g_group_out[l]), g_ffn=row(g_ffn_norm[l]))

    smalls = [small(l) for l in range(depth)]
    lrow, dx, sm, dg_final = _local_step(x0, target, wts, smalls, row(g_final), tb)
    loss = lax.psum(jnp.sum(lrow), AXES)
    grad_x = dx.reshape(1, s, d)
    return _reduce_and_update(loss, grad_x, wts.recv, sm, dg_final, me, dict(
        g_mix_norm=(g_mix_norm, m_g_mix_norm, v_g_mix_norm), w_in=(w_in, m_w_in, v_w_in),
        b_forget=(b_forget, m_b_forget, v_b_forget), g_sgu=(g_sgu, m_g_sgu, v_g_sgu),
        w_spatial=(w_spatial, m_w_spatial, v_w_spatial), b_spatial=(b_spatial, m_b_spatial, v_b_spatial),
        g_mla_q=(g_mla_q, m_g_mla_q, v_g_mla_q), w_uq=(w_uq, m_w_uq, v_w_uq), g_mla_kv=(g_mla_kv, m_g_mla_kv, v_g_mla_kv),
        w_ukv=(w_ukv, m_w_ukv, v_w_ukv), g_group_out=(g_group_out, m_g_group_out, v_g_group_out),
        w_out=(w_out, m_w_out, v_w_out), g_ffn_norm=(g_ffn_norm, m_g_ffn_norm, v_g_ffn_norm), w_up=(w_up, m_w_up, v_w_up),
        w_down=(w_down, m_w_down, v_w_down), g_final=(g_final, m_g_final, v_g_final)))


_GATHER_AT = {
    "in_proj0": [(0, "w_up")],
    "fox_attn0": [(0, "w_uq"), (0, "w_ukv"), (0, "w_down")],
    "mla_attn0": [(0, "w_out"), (1, "w_in")],
    "ffn_fwd0": [(1, "w_uq"), (1, "w_ukv"), (1, "w_down")],
    "fox_attn1": [(1, "w_out")],
    "mla_attn1": [(1, "w_up")],
}
_SCATTER_AT = {
    "fox_attn_bwd1": [(1, "w_down")],
    "mla_attn_bwd1": [(1, "w_up"), (1, "w_out")],
    "ffn_bwd0": [(1, "w_in")],
    "fox_attn_bwd0": [(0, "w_down")],
    "mla_attn_bwd0": [(0, "w_up"), (0, "w_out")],
    "in_proj_bwd0": [(0, "w_in")],
}


class _FullWeights:
    def __init__(self, per_layer):
        self.per_layer, self.grads = per_layer, {}

    def get(self, l, name):
        return self.per_layer[l][name]

    def comm(self, host):
        return None

    def done(self, host, results):
        pass

    def grad(self, l, name, blocks):
        self.grads[(l, name)] = blocks


class _ShardedWeights(_FullWeights):
    def __init__(self, shards):
        self.shards, self.full, self.grads, self.recv = shards, {}, {}, {}

    def get(self, l, name):
        if name in ("wk", "wv"):
            return _split_ukv(_cols_to_full(self.full[(l, "w_ukv")]))[0 if name == "wk" else 1]
        if name == "wq":
            return _pad_uq(_cols_to_full(self.full[(l, "w_uq")]))
        g = self.full[(l, name)]
        return g if name == "w_up" else g.reshape(NDEV * g.shape[1], g.shape[2])

    def comm(self, host):
        if host in _GATHER_AT:
            return _Comm("gather", [self.shards[k] for k in _GATHER_AT[host]])
        if host in _SCATTER_AT:
            return _Comm("exchange", [self.grads[k] for k in _SCATTER_AT[host]])
        return None

    def done(self, host, results):
        if host in _GATHER_AT:
            self.full.update(zip(_GATHER_AT[host], results))
        if host in _SCATTER_AT:
            self.recv.update(zip(_SCATTER_AT[host], results))


def _local_step(x0, target, wts, smalls, g_final, tb):
    depth = len(smalls)
    s, d = x0.shape
    saved = []
    xl = x0
    for l in range(depth):
        p = smalls[l]
        z, h, got = _norm_matmul(xl, p["g_mix"], wts.get(l, "w_in"), f"in_proj{l}", wts.comm(f"in_proj{l}"))
        wts.done(f"in_proj{l}", got)
        ya = _sgu_fwd(z, p["g_sgu"], p["w_s"], p["b_t"], f"sgu_fwd{l}")
        yb, ret, states = _ret_fwd(z, tb, f"ret_fwd{l}")
        cum = _fox_prep(z, p["b_f"], f"fox_prep{l}")
        kc, vc, vtc = _kv_prep(z, 7, 8, f"fox_kv{l}")
        yc, lse_c, got = _attn_fwd(z, 6, HEAD_DIM, kc, vtc, HEAD_DIM ** -0.5, cum, f"fox_attn{l}", wts.comm(f"fox_attn{l}"))
        wts.done(f"fox_attn{l}", got)
        wq, wk, wv = wts.get(l, "wq"), wts.get(l, "wk"), wts.get(l, "wv")
        qd, kd, vd, vtd, cqn, ckvn = _mla_prep(z, p["gq"], p["gkv"], wq, wk, wv, tb, f"mla_prep{l}")
        yd, lse_d, got = _attn_fwd(qd, 0, 128, kd, vtd, _SCALE_D, None, f"mla_attn{l}", wts.comm(f"mla_attn{l}"))
        wts.done(f"mla_attn{l}", got)
        ys = (ya, yb, yc, yd)
        x1, yn = _out_proj(ys, p["g_go"], wts.get(l, "w_out"), xl, f"out_proj{l}")
        x2, u, h2, got = _ffn_fwd(x1, p["g_ffn"], wts.get(l, "w_up"), wts.get(l, "w_down"), f"ffn_fwd{l}", wts.comm(f"ffn_fwd{l}"))
        wts.done(f"ffn_fwd{l}", got)
        saved.append(dict(x=xl, z=z, h=h, ys=ys, ret=ret, states=states, cum=cum, lse_c=lse_c, kc=kc, vc=vc, qd=qd, kd=kd, vd=vd,
                          cqn=cqn, ckvn=ckvn, lse_d=lse_d, x1=x1, yn=yn, u=u, h2=h2, wq=wq, wk=wk, wv=wv))
        xl = x2

    lrow, dx, dg_final = _loss_head(xl, g_final, target, "loss_head")

    sm = [None] * depth
    for l in reversed(range(depth)):
        p, a = smalls[l], saved[l]
        dx1, du, dg_ffn, got = _ffn_bwd(dx, a["x1"], a["u"], p["g_ffn"], wts.get(l, "w_up"), wts.get(l, "w_down"), f"ffn_bwd{l}",
                                        wts.comm(f"ffn_bwd{l}"))
        wts.done(f"ffn_bwd{l}", got)
        dw_down = _mm_tn(a["u"], dx, f"dw_down{l}", a_fn=lambda t: jnp.square(jnp.maximum(t, 0.0)), out_dtype=_WIRE)
        wts.grad(l, "w_down", dw_down.reshape(NDEV, dw_down.shape[0] // NDEV, d))
        wts.grad(l, "w_up", _mm_tn(a["h2"], du, f"dw_up{l}", blocked=True, out_dtype=_WIRE))
        dya, dyb, dyc, dyd, dg_go = _out_proj_bwd(dx1, wts.get(l, "w_out"), a["ys"], p["g_go"], f"out_proj_bwd{l}")
        wts.grad(l, "w_out", _mm_tn(a["yn"], dx1, f"dw_out{l}", out_dtype=_WIRE).reshape(NDEV, d // NDEV, d))
        dz_a, dg_sgu, dw_s, db_t = _sgu_bwd(dya, a["z"], p["g_sgu"], p["w_s"], p["b_t"], f"sgu_bwd{l}")
        dz_b = _ret_bwd(dyb, a["z"], a["ret"], a["states"], tb, f"ret_bwd{l}")
        qt, dot, dl = _attn_bwd_prep(a["z"], 6, HEAD_DIM, HEAD_DIM ** -0.5, a["ys"][2], dyc, f"fox_bwd_prep{l}")
        dqt_c, dk_c, dv_c, dck, dcq, got = _attn_bwd(a["kc"], a["vc"], qt, dot, a["lse_c"], dl, HEAD_DIM,
                                                     HEAD_DIM ** -0.5, a["cum"], f"fox_attn_bwd{l}", _MXU,
                                                     wts.comm(f"fox_attn_bwd{l}"))
        wts.done(f"fox_attn_bwd{l}", got)
        dq_c = _untranspose(dqt_c, _MXU, f"fox_dq{l}")
        qt, dot, dl = _attn_bwd_prep(a["qd"], 0, 128, _SCALE_D, a["ys"][3], dyd, f"mla_bwd_prep{l}")
        dqt_d, dk_d, dv_d, got = _attn_bwd(a["kd"], a["vd"], qt, dot, a["lse_d"], dl, 128, _SCALE_D, None,
                                           f"mla_attn_bwd{l}", F32, wts.comm(f"mla_attn_bwd{l}"))
        wts.done(f"mla_attn_bwd{l}", got)
        dq_d = _untranspose(dqt_d, F32, f"mla_dq{l}")
        dz_cq, dz_ckv, dkr, dwq, dwk, dwv, dgq, dgkv = _mla_prep_bwd(dq_d, dk_d, dv_d, a["z"], a["cqn"], a["ckvn"], p["gq"],
                                                                     p["gkv"], a["wq"], a["wk"], a["wv"], tb, f"mla_prep_bwd{l}")
        dz_misc, db_f = _fox_post(dcq, dck, a["z"], p["b_f"], dkr, f"fox_post{l}")
        dz = jnp.concatenate([dz_a, dz_b, dq_c, dk_c, dv_c, dz_cq, dz_ckv, dz_misc], axis=1)
        wts.grad(l, "w_in", _unpad_in_cols(_mm_tn(a["h"], dz, f"dw_in{l}", out_dtype=_WIRE)).reshape(NDEV, d // NDEV, N_IN))
        dx, dg_mix, got = _in_proj_bwd(dz, wts.get(l, "w_in"), a["x"], p["g_mix"], dx1, f"in_proj_bwd{l}",
                                       wts.comm(f"in_proj_bwd{l}"))
        wts.done(f"in_proj_bwd{l}", got)
        sm[l] = [dg_mix, dg_go, dg_ffn, dg_sgu, dw_s, db_t[:, :N_HEADS].T, db_f[0, :N_HEADS], dgq, dgkv, _unpad_uq(dwq),
                 _join_ukv(dwk, dwv)]
    return lrow, dx, sm, dg_final


def _reduce_and_update(loss, grad_x, recv, sm, dg_final, me, given):
    depth = len(sm)
    pieces = [t for l in range(depth) for t in sm[l]] + [dg_final]
    flat = jnp.concatenate([t.reshape(-1) for t in pieces])
    n_flat = flat.shape[0]
    unit = NDEV * 8 * 128
    n_pad = -(-n_flat // unit) * unit
    packed = jnp.pad(flat, (0, n_pad - n_flat)).reshape(NDEV, n_pad // (NDEV * 128), 128)
    red = _sum_slots(_exchange([packed], "scatter_small")[0], "sum_small")
    full = _all_gather([red], "gather_small")[0].reshape(-1)
    offs = np.cumsum([0] + [int(np.prod(t.shape)) for t in pieces])
    red_pieces = [full[int(offs[i]):int(offs[i + 1])].reshape(pieces[i].shape) for i in range(len(pieces))]
    per = len(sm[0])
    stack = lambda i: jnp.stack([red_pieces[l * per + i] for l in range(depth)])
    g_small = dict(g_mix_norm=stack(0), g_group_out=stack(1), g_ffn_norm=stack(2), g_sgu=stack(3), w_spatial=stack(4),
                   b_spatial=stack(5), b_forget=stack(6), g_mla_q=stack(7), g_mla_kv=stack(8), g_final=red_pieces[-1])
    cq, ckv = given["w_uq"][0].shape[2], given["w_ukv"][0].shape[2]
    g_small["w_uq"] = lax.dynamic_slice_in_dim(stack(9), me * cq, cq, axis=2)
    g_small["w_ukv"] = lax.dynamic_slice_in_dim(stack(10), me * ckv, ckv, axis=2)

    names = list(given)
    outs = {}
    for nme in names:
        wv_, mv_, vv_ = given[nme]
        shape = wv_.shape
        if nme in ("w_in", "w_out", "w_up", "w_down"):
            res = []
            for l in range(depth):
                parts = recv[(l, nme)]
                two = lambda t: t[l].reshape(-1, shape[-1])
                res.append(_adamw(parts, two(wv_), two(mv_), two(vv_), f"adamw_{nme}{l}"))
            outs[nme] = [jnp.stack([res[l][i] for l in range(depth)]).reshape(shape) for i in range(4)]
        else:
            two = lambda t: t.reshape(-1, shape[-1]) if t.ndim > 1 else t.reshape(1, -1)
            res = _adamw(two(g_small[nme]), two(wv_), two(mv_), two(vv_), f"adamw_{nme}")
            outs[nme] = [r.reshape(shape) for r in res]
    return (loss, grad_x, *[outs[n][0] for n in names], *[outs[n][1] for n in names], *[outs[n][2] for n in names],
            *[outs[n][3] for n in names])
```

```python
import functools

import jax
import jax.numpy as jnp
import numpy as np
from jax import lax
from jax.experimental import pallas as pl
from jax.experimental.pallas import tpu as pltpu

F32 = jnp.float32
_MXU = jnp.bfloat16
_WIRE = jnp.bfloat16
EPS = 1e-6
NDEV = 8
AXES = ("x", "y", "c")
MESH = pl.DeviceIdType.MESH

N_HEADS = 4
HEAD_DIM = 64
GROUP = 256
CHUNK = 128
NZ = 2816
N_IN = 2724
MISC_F, MISC_KR = 0, 32
VMEM_LIMIT = 56 * 1024 * 1024

ADAM_LR, ADAM_B1, ADAM_B2, ADAM_EPS, ADAM_WD, ADAM_STEP = 0.001, 0.9, 0.999, 1e-08, 0.01, 10

SDS = jax.ShapeDtypeStruct


def _cp(*sem):
    return pltpu.CompilerParams(dimension_semantics=sem, vmem_limit_bytes=VMEM_LIMIT)


def _dot(a, b):
    return jnp.dot(a.astype(_MXU), b.astype(_MXU), preferred_element_type=F32)


def _dot_nt(a, b):
    return lax.dot_general(a.astype(_MXU), b.astype(_MXU), (((1,), (1,)), ((), ())), preferred_element_type=F32)


def _dot_tn(a, b):
    return lax.dot_general(a.astype(_MXU), b.astype(_MXU), (((0,), (0,)), ((), ())), preferred_element_type=F32)


def _dot_exact(a, b, dims=(((1,), (0,)), ((), ()))):
    return lax.dot_general(a, b, dims, precision=lax.Precision.HIGHEST, preferred_element_type=F32)


def _rms(x, g):
    return x * lax.rsqrt(jnp.mean(x * x, axis=-1, keepdims=True) + EPS) * g


def _rms_bwd(x, g, dy):
    xh = x * lax.rsqrt(jnp.mean(x * x, axis=-1, keepdims=True) + EPS)
    dxh = dy * g
    r = lax.rsqrt(jnp.mean(x * x, axis=-1, keepdims=True) + EPS)
    dx = r * (dxh - xh * jnp.mean(dxh * xh, axis=-1, keepdims=True))
    return dx, jnp.sum(dy * xh, axis=0, keepdims=True)


def _standardize(t):
    mu = jnp.mean(t, axis=-1, keepdims=True)
    tc = t - mu
    rs = lax.rsqrt(jnp.mean(tc * tc, axis=-1, keepdims=True) + EPS)
    return tc * rs, rs


def _standardize_bwd(yh, rs, dy):
    return rs * (dy - jnp.mean(dy, axis=-1, keepdims=True) - yh * jnp.mean(dy * yh, axis=-1, keepdims=True))


_GELU_C = 0.7978845608028654


def _gelu(x):
    return 0.5 * x * (1.0 + jnp.tanh(_GELU_C * (x + 0.044715 * x * x * x)))


def _gelu_grad(x):
    t = jnp.tanh(_GELU_C * (x + 0.044715 * x * x * x))
    return 0.5 * (1.0 + t) + 0.5 * x * (1.0 - t * t) * _GELU_C * (1.0 + 3 * 0.044715 * x * x)


def _sigmoid(x):
    return 1.0 / (1.0 + jnp.exp(-x))


def _swap_half(t, half):
    n = t.shape[-1]
    lane = lax.broadcasted_iota(jnp.int32, t.shape, t.ndim - 1)
    return jnp.where((lane % (2 * half)) < half, pltpu.roll(t, n - half, t.ndim - 1), pltpu.roll(t, half, t.ndim - 1))


def _lanes(table, width):
    return jnp.concatenate([table] * (width // table.shape[-1]), axis=-1)


def _rope(t, cos, sin, half):
    return t * cos + _swap_half(t, half) * sin


def _rope_bwd(d, cos, sin, half):
    return d * cos - _swap_half(d, half) * sin


def _tables(s):
    pos = jnp.arange(s, dtype=F32)[:, None]

    def cs(half):
        inv = jnp.power(10000.0, -jnp.arange(half, dtype=F32) / half)
        ang = pos * inv[None, :]
        return jnp.cos(ang), jnp.sin(ang)

    c32, s32 = cs(32)
    c16, s16 = cs(16)
    z = lambda w: jnp.zeros((s, w), F32)
    o = lambda w: jnp.ones((s, w), F32)
    t = {}
    t["b_cos"] = jnp.concatenate([c32, c32, c32, c32], 1)
    t["b_sin"] = jnp.concatenate([-s32, s32, -s32, s32], 1)
    t["q_cos"] = jnp.concatenate([o(64), c16, c16, z(32)], 1)
    t["q_sin"] = jnp.concatenate([z(64), -s16, s16, z(32)], 1)
    t["k_cos"] = jnp.concatenate([z(32), c16, c16, z(64)], 1)
    t["k_sin"] = jnp.concatenate([z(32), -s16, s16, z(64)], 1)
    lg = jnp.log1p(-jnp.exp2(-5.0 - jnp.arange(N_HEADS, dtype=F32)))
    j = jnp.arange(CHUNK, dtype=F32)
    rel = j[:, None] - j[None, :]
    t["decay"] = jnp.where(rel[None] >= 0, jnp.exp(jnp.maximum(rel, 0.0)[None] * lg[:, None, None]), 0.0)

    def rows(e):
        return jnp.repeat(e.T, HEAD_DIM, axis=1)

    t["qw"] = rows(jnp.exp((j + 1.0)[None, :] * lg[:, None]))
    t["kw"] = rows(jnp.exp((CHUNK - 1 - j)[None, :] * lg[:, None]))
    t["kw2"] = rows(jnp.exp((CHUNK - j)[None, :] * lg[:, None]))
    t["qw0"] = rows(jnp.exp(j[None, :] * lg[:, None]))
    t["cd"] = jnp.repeat(jnp.exp(CHUNK * lg), HEAD_DIM)[None, :]
    e = np.zeros((128, 512), np.float32)
    for h in range(N_HEADS):
        for r in range(32):
            e[MISC_KR + r, 128 * h + 64 + r] = 1.0
    t["place"] = jnp.asarray(e)
    return t


def _norm_matmul(x, g, w, name, comm=None):
    s, d = x.shape
    n = w.shape[1]
    tm, tn = min(512, s), 256
    ni = s // tm

    def body(*refs):
        (x_ref, g_ref, w_ref), (z_ref, h_ref), _, cc = _split_refs(refs, 3, 2, comm)
        i = pl.program_id(0)
        _host_gather(comm, cc, i, ni)
        h = _rms(x_ref[...], g_ref[...]).astype(h_ref.dtype)
        h_ref[...] = h
        for j in range(n // tn):
            z_ref[:, tn * j:tn * (j + 1)] = jnp.dot(h, w_ref[:, tn * j:tn * (j + 1)], preferred_element_type=F32)
        if comm is not None:
            @pl.when(i == ni - 1)
            def _():
                comm.wait(*cc)

    in_specs = [pl.BlockSpec((tm, d), lambda i: (i, 0)), pl.BlockSpec((1, d), lambda i: (0, 0)),
                pl.BlockSpec((d, n), lambda i: (0, 0))]
    out_specs = [pl.BlockSpec((tm, n), lambda i: (i, 0)), pl.BlockSpec((tm, d), lambda i: (i, 0))]
    out_shape = [SDS((s, n), F32), SDS((s, d), _MXU)]
    return _call_with_comm(body, (ni,), in_specs, out_specs, out_shape, [], [x, g, w], comm, ("arbitrary",), name)


def _mm_tn(a, b, name, *, a_fn=None, blocked=False, out_dtype=F32):
    k, m = a.shape
    n = b.shape[1]
    tm, tk = min(1024, m), min(512, k)
    tn = next(t for t in (1408, 1024, 512, 256, 128) if n % t == 0)
    assert m % tm == 0 and k % tk == 0
    nk = k // tk

    def body(a_ref, b_ref, o_ref, acc):
        kk = pl.program_id(2)

        @pl.when(kk == 0)
        def _():
            acc[...] = jnp.zeros_like(acc)

        av = a_ref[...]
        if a_fn is not None:
            av = a_fn(av.astype(F32))
        acc[...] += _dot_tn(av, b_ref[...])

        @pl.when(kk == nk - 1)
        def _():
            if blocked:
                for c in range(tn // 512):
                    o_ref[c] = acc[:, 512 * c:512 * (c + 1)].astype(o_ref.dtype)
            else:
                o_ref[...] = acc[...].astype(o_ref.dtype)

    if blocked:
        assert tn % 512 == 0
        out_spec = pl.BlockSpec((tn // 512, tm, 512), lambda i, j, kk: (j, i, 0))
        out_shape = SDS((n // 512, m, 512), out_dtype)
    else:
        out_spec = pl.BlockSpec((tm, tn), lambda i, j, kk: (i, j))
        out_shape = SDS((m, n), out_dtype)
    return pl.pallas_call(
        body, grid=(m // tm, n // tn, nk),
        in_specs=[pl.BlockSpec((tk, tm), lambda i, j, kk: (kk, i)), pl.BlockSpec((tk, tn), lambda i, j, kk: (kk, j))],
        out_specs=out_spec, out_shape=out_shape, scratch_shapes=[pltpu.VMEM((tm, tn), F32)],
        compiler_params=_cp("parallel", "parallel", "arbitrary"), name=name)(a, b)


def _sgu_parts(u_pre, v_pre, gain):
    u = _gelu(u_pre)
    v = _gelu(v_pre)
    vh, rs, vg = [], [], []
    for h in range(N_HEADS):
        sl = slice(HEAD_DIM * h, HEAD_DIM * (h + 1))
        a, r = _standardize(v[:, sl])
        vh.append(a)
        rs.append(r)
        vg.append(a * gain[:, sl])
    return u, vh, rs, vg


def _tril(w):
    r = lax.broadcasted_iota(jnp.int32, w.shape, 0)
    c = lax.broadcasted_iota(jnp.int32, w.shape, 1)
    return jnp.where(r >= c, w, 0.0)


def _sgu_fwd(z, gain, w_s, b_t, name):
    s = z.shape[0]
    tm = min(512, s)

    def body(u_ref, v_ref, g_ref, w_ref, b_ref, y_ref):
        u, _, _, vg = _sgu_parts(u_ref[...], v_ref[...], g_ref[...])
        hc = [(h, c) for h in range(N_HEADS) for c in range(tm // CHUNK)]
        wcs = [_tril(w_ref[h]) for h in range(N_HEADS)]
        mixed = {(h, c): _dot(wcs[h], vg[h][CHUNK * c:CHUNK * (c + 1)]) for h, c in hc}
        for h, c in hc:
            r, sl = slice(CHUNK * c, CHUNK * (c + 1)), slice(HEAD_DIM * h, HEAD_DIM * (h + 1))
            y_ref[r, sl] = u[r, sl] * (mixed[h, c] + b_ref[:, h:h + 1])

    return pl.pallas_call(
        body, grid=(s // tm,),
        in_specs=[pl.BlockSpec((tm, GROUP), lambda i: (i, 0)), pl.BlockSpec((tm, GROUP), lambda i: (i, 1)),
                  pl.BlockSpec((1, GROUP), lambda i: (0, 0)), pl.BlockSpec((N_HEADS, CHUNK, CHUNK), lambda i: (0, 0, 0)),
                  pl.BlockSpec((CHUNK, 128), lambda i: (0, 0))],
        out_specs=pl.BlockSpec((tm, GROUP), lambda i: (i, 0)), out_shape=SDS((s, GROUP), F32),
        compiler_params=_cp("parallel"), name=name)(z, z, gain, w_s, b_t)


def _sgu_bwd(dy, z, gain, w_s, b_t, name):
    s = z.shape[0]
    tm = min(512, s)

    def body(dy_ref, u_ref, v_ref, g_ref, w_ref, b_ref, dz_ref, dg_ref, dw_ref, db_ref):
        @pl.when(pl.program_id(0) == 0)
        def _():
            dg_ref[...] = jnp.zeros_like(dg_ref)
            dw_ref[...] = jnp.zeros_like(dw_ref)
            db_ref[...] = jnp.zeros_like(db_ref)

        u_pre, v_pre, gain_v = u_ref[...], v_ref[...], g_ref[...]
        u, vh, rs, vg = _sgu_parts(u_pre, v_pre, gain_v)
        dyv = dy_ref[...]
        gu = _gelu_grad(u_pre)
        gv = _gelu_grad(v_pre)
        hc = [(h, c) for h in range(N_HEADS) for c in range(tm // CHUNK)]
        sls = [slice(HEAD_DIM * h, HEAD_DIM * (h + 1)) for h in range(N_HEADS)]
        rws = [slice(CHUNK * c, CHUNK * (c + 1)) for c in range(tm // CHUNK)]
        wcs = [_tril(w_ref[h]) for h in range(N_HEADS)]
        mixed = {(h, c): _dot(wcs[h], vg[h][rws[c]]) for h, c in hc}
        dms = {}
        for h, c in hc:
            r, sl = rws[c], sls[h]
            dz_ref[r, sl] = (dyv[r, sl] * (mixed[h, c] + b_ref[:, h:h + 1]) * gu[r, sl]).astype(dz_ref.dtype)
            dms[h, c] = dyv[r, sl] * u[r, sl]
        dws = {(h, c): _dot_nt(dms[h, c], vg[h][rws[c]]) for h, c in hc}
        dvgs = {(h, c): _dot_tn(wcs[h], dms[h, c]) for h, c in hc}
        for h in range(N_HEADS):
            sl = sls[h]
            dwh = jnp.zeros((CHUNK, CHUNK), F32)
            dbh = jnp.zeros((CHUNK, 1), F32)
            dgh = jnp.zeros((1, HEAD_DIM), F32)
            for c in range(tm // CHUNK):
                r = rws[c]
                dwh += dws[h, c]
                dbh += jnp.sum(dms[h, c], axis=1, keepdims=True)
                dvg = dvgs[h, c]
                dgh += jnp.sum(dvg * vh[h][r], axis=0, keepdims=True)
                dv = _standardize_bwd(vh[h][r], rs[h][r], dvg * gain_v[:, sl])
                dz_ref[r, GROUP + HEAD_DIM * h:GROUP + HEAD_DIM * (h + 1)] = (dv * gv[r, sl]).astype(dz_ref.dtype)
            dw_ref[h] += _tril(dwh)
            db_ref[:, h:h + 1] += dbh
            dg_ref[:, sl] += dgh

    return pl.pallas_call(
        body, grid=(s // tm,),
        in_specs=[pl.BlockSpec((tm, GROUP), lambda i: (i, 0)),
                  pl.BlockSpec((tm, GROUP), lambda i: (i, 0)), pl.BlockSpec((tm, GROUP), lambda i: (i, 1)),
                  pl.BlockSpec((1, GROUP), lambda i: (0, 0)), pl.BlockSpec((N_HEADS, CHUNK, CHUNK), lambda i: (0, 0, 0)),
                  pl.BlockSpec((CHUNK, 128), lambda i: (0, 0))],
        out_specs=[pl.BlockSpec((tm, 2 * GROUP), lambda i: (i, 0)), pl.BlockSpec((1, GROUP), lambda i: (0, 0)),
                   pl.BlockSpec((N_HEADS, CHUNK, CHUNK), lambda i: (0, 0, 0)), pl.BlockSpec((CHUNK, 128), lambda i: (0, 0))],
        out_shape=[SDS((s, 2 * GROUP), _MXU), SDS((1, GROUP), F32), SDS((N_HEADS, CHUNK, CHUNK), F32), SDS((CHUNK, 128), F32)],
        compiler_params=_cp("arbitrary"), name=name)(dy, z, z, gain, w_s, b_t)


_SCALE_B = HEAD_DIM ** -0.5


def _ret_fwd(z, tb, name):
    s = z.shape[0]
    nc = s // CHUNK
    row = lambda col: pl.BlockSpec((CHUNK, GROUP), lambda n, col=col: (n, col))
    const = lambda shape: pl.BlockSpec(shape, lambda n: (0,) * len(shape))

    def body(q_ref, k_ref, v_ref, g_ref, cos_ref, sin_ref, dec_ref, qw_ref, kw_ref, cd_ref, y_ref, o_ref, st_ref, state):
        @pl.when(pl.program_id(0) == 0)
        def _():
            state[...] = jnp.zeros_like(state)

        cos, sin = _lanes(cos_ref[...], GROUP), _lanes(sin_ref[...], GROUP)
        q = _rope(q_ref[...], cos, sin, 32)
        k = _rope(k_ref[...], cos, sin, 32) * _SCALE_B
        v = v_ref[...]
        g = g_ref[...]
        st_ref[0] = state[...]
        qs = q * qw_ref[...]
        ks = k * kw_ref[...]
        sls = [slice(HEAD_DIM * h, HEAD_DIM * (h + 1)) for h in range(N_HEADS)]
        scs = [_dot_nt(q[:, sl], k[:, sl]) for sl in sls]
        crs = [_dot(qs[:, sl], state[:, sl]) for sl in sls]
        kvs = [_dot_tn(ks[:, sl], v[:, sl]) for sl in sls]
        scd = [(scs[h] * dec_ref[h]).astype(_MXU) for h in range(N_HEADS)]
        ins = [_dot(scd[h], v[:, sls[h]]) for h in range(N_HEADS)]
        for h, sl in enumerate(sls):
            o = ins[h] + crs[h]
            o_ref[:, sl] = o
            yh, _ = _standardize(o)
            gh = g[:, sl]
            y_ref[:, sl] = gh * _sigmoid(gh) * yh
            state[:, sl] = cd_ref[:, sl] * state[:, sl] + kvs[h]

    return pl.pallas_call(
        body, grid=(nc,),
        in_specs=[row(2), row(3), row(4), row(5), pl.BlockSpec((CHUNK, 128), lambda n: (n, 0)),
                  pl.BlockSpec((CHUNK, 128), lambda n: (n, 0)), const((N_HEADS, CHUNK, CHUNK)),
                  const((CHUNK, GROUP)), const((CHUNK, GROUP)), const((1, GROUP))],
        out_specs=[pl.BlockSpec((CHUNK, GROUP), lambda n: (n, 0)), pl.BlockSpec((CHUNK, GROUP), lambda n: (n, 0)),
                   pl.BlockSpec((1, HEAD_DIM, GROUP), lambda n: (n, 0, 0))],
        out_shape=[SDS((s, GROUP), F32), SDS((s, GROUP), F32), SDS((nc, HEAD_DIM, GROUP), F32)],
        scratch_shapes=[pltpu.VMEM((HEAD_DIM, GROUP), F32)],
        compiler_params=_cp("arbitrary"), name=name)(z, z, z, z, tb["b_cos"], tb["b_sin"], tb["decay"], tb["qw"], tb["kw"], tb["cd"])


def _ret_bwd(dy, z, o_pre, states, tb, name):
    s = z.shape[0]
    nc = s // CHUNK
    rev = lambda col: pl.BlockSpec((CHUNK, GROUP), lambda n, col=col: (nc - 1 - n, col))
    const = lambda shape: pl.BlockSpec(shape, lambda n: (0,) * len(shape))

    def body(dy_ref, q_ref, k_ref, v_ref, g_ref, o_ref, st_ref, cos_ref, sin_ref, dec_ref, qw_ref, kw2_ref, qw0_ref, cd_ref,
             dz_ref, rstate):
        @pl.when(pl.program_id(0) == 0)
        def _():
            rstate[...] = jnp.zeros_like(rstate)

        cos, sin = _lanes(cos_ref[...], GROUP), _lanes(sin_ref[...], GROUP)
        q = _rope(q_ref[...], cos, sin, 32)
        k = _rope(k_ref[...], cos, sin, 32) * _SCALE_B
        v = v_ref[...]
        g = g_ref[...]
        dyv = dy_ref[...]
        sg = _sigmoid(g)
        silu = g * sg
        dos, dgs = [], []
        for h in range(N_HEADS):
            sl = slice(HEAD_DIM * h, HEAD_DIM * (h + 1))
            yh, rs = _standardize(o_ref[:, sl])
            dgs.append(dyv[:, sl] * yh * (sg[:, sl] * (1.0 + g[:, sl] * (1.0 - sg[:, sl]))))
            dos.append(_standardize_bwd(yh, rs, dyv[:, sl] * silu[:, sl]))
        do = jnp.concatenate(dos, axis=1)
        dow = do * qw_ref[...]
        vw = v * kw2_ref[...]
        kw = k * kw2_ref[...]
        q0 = q * qw0_ref[...]
        sls = [slice(HEAD_DIM * h, HEAD_DIM * (h + 1)) for h in range(N_HEADS)]
        sn = st_ref[0]
        rrs = [rstate[:, sl] for sl in sls]
        ps = [_dot_nt(q[:, sl], k[:, sl]) for sl in sls]
        dps = [_dot_nt(do[:, sl], v[:, sl]) for sl in sls]
        dq_x = [_dot_nt(dow[:, sl], sn[:, sl]) for sl in sls]
        dk_x = [_dot_nt(vw[:, sl], rrs[h]) for h, sl in enumerate(sls)]
        dv_x = [_dot(kw[:, sl], rrs[h]) for h, sl in enumerate(sls)]
        r_new = [_dot_tn(q0[:, sl], do[:, sl]) for sl in sls]
        pd = [(ps[h] * dec_ref[h]).astype(_MXU) for h in range(N_HEADS)]
        dpd = [(dps[h] * dec_ref[h]).astype(_MXU) for h in range(N_HEADS)]
        dq_i = [_dot(dpd[h], k[:, sl]) for h, sl in enumerate(sls)]
        dk_i = [_dot_tn(dpd[h], q[:, sl]) for h, sl in enumerate(sls)]
        dv_i = [_dot_tn(pd[h], do[:, sl]) for h, sl in enumerate(sls)]
        dqs, dks = [], []
        for h, sl in enumerate(sls):
            dqs.append(dq_i[h] + dq_x[h])
            dks.append(dk_i[h] + dk_x[h])
            dz_ref[:, 2 * GROUP + HEAD_DIM * h:2 * GROUP + HEAD_DIM * (h + 1)] = (dv_i[h] + dv_x[h]).astype(dz_ref.dtype)
            rstate[:, sl] = cd_ref[:, sl] * rrs[h] + r_new[h]
        dq = _rope_bwd(jnp.concatenate(dqs, axis=1), cos, sin, 32)
        dk = _rope_bwd(jnp.concatenate(dks, axis=1) * _SCALE_B, cos, sin, 32)
        dz_ref[:, 0:GROUP] = dq.astype(dz_ref.dtype)
        dz_ref[:, GROUP:2 * GROUP] = dk.astype(dz_ref.dtype)
        dz_ref[:, 3 * GROUP:4 * GROUP] = jnp.concatenate(dgs, axis=1).astype(dz_ref.dtype)

    r0 = lambda: pl.BlockSpec((CHUNK, GROUP), lambda n: (nc - 1 - n, 0))
    r128 = lambda: pl.BlockSpec((CHUNK, 128), lambda n: (nc - 1 - n, 0))
    return pl.pallas_call(
        body, grid=(nc,),
        in_specs=[r0(), rev(2), rev(3), rev(4), rev(5), r0(), pl.BlockSpec((1, HEAD_DIM, GROUP), lambda n: (nc - 1 - n, 0, 0)),
                  r128(), r128(), const((N_HEADS, CHUNK, CHUNK)), const((CHUNK, GROUP)), const((CHUNK, GROUP)),
                  const((CHUNK, GROUP)), const((1, GROUP))],
        out_specs=pl.BlockSpec((CHUNK, 4 * GROUP), lambda n: (nc - 1 - n, 0)),
        out_shape=SDS((s, 4 * GROUP), _MXU), scratch_shapes=[pltpu.VMEM((HEAD_DIM, GROUP), F32)],
        compiler_params=_cp("arbitrary"), name=name)(
            dy, z, z, z, z, o_pre, states, tb["b_cos"], tb["b_sin"], tb["decay"], tb["qw"], tb["kw2"], tb["qw0"], tb["cd"])


TQ = 256


def _log_sigmoid(x):
    return jnp.minimum(x, 0.0) - jnp.log1p(jnp.exp(-jnp.abs(x)))


def _fox_prep(z, b_f, name):
    s = z.shape[0]
    nb = s // TQ

    def body(m_ref, b_ref, cc_ref, carry):
        @pl.when(pl.program_id(0) == 0)
        def _():
            carry[...] = jnp.zeros_like(carry)

        lane = lax.broadcasted_iota(jnp.int32, (TQ, 128), 1)
        logf = jnp.where(lane < N_HEADS, _log_sigmoid(m_ref[...] + b_ref[...]), 0.0)
        r = lax.broadcasted_iota(jnp.int32, (TQ, TQ), 0)
        c = lax.broadcasted_iota(jnp.int32, (TQ, TQ), 1)
        tri = jnp.where(r >= c, 1.0, 0.0).astype(F32)
        cum = _dot_exact(tri, logf) + carry[...]
        cc_ref[...] = cum * LOG2E
        carry[...] = cum[TQ - 1:TQ, :]

    return pl.pallas_call(
        body, grid=(nb,),
        in_specs=[pl.BlockSpec((TQ, 128), lambda i: (i, NZ // 128 - 1)), pl.BlockSpec((1, 128), lambda i: (0, 0))],
        out_specs=pl.BlockSpec((TQ, 128), lambda i: (i, 0)),
        out_shape=SDS((s, 128), F32), scratch_shapes=[pltpu.VMEM((1, 128), F32)],
        compiler_params=_cp("arbitrary"), name=name)(z, b_f)


def _fox_post(dcr, dcq, z, b_f, dkr, name):
    s = z.shape[0]
    nb = s // TQ

    def body(dc_ref, dcq_ref, m_ref, b_ref, dkr_ref, dz_ref, db_ref, carry):
        @pl.when(pl.program_id(0) == 0)
        def _():
            carry[...] = jnp.zeros_like(carry)
            db_ref[...] = jnp.zeros_like(db_ref)

        r = lax.broadcasted_iota(jnp.int32, (TQ, TQ), 0)
        c = lax.broadcasted_iota(jnp.int32, (TQ, TQ), 1)
        triu = jnp.where(c >= r, 1.0, 0.0).astype(F32)
        dc = jnp.concatenate([dc_ref[0], jnp.zeros((120, TQ), F32)], axis=0)
        dlogf = _dot_exact(triu, dc, (((1,), (1,)), ((), ()))) + _dot_exact(triu, dcq_ref[...]) + carry[...]
        carry[...] = dlogf[0:1, :]
        x = m_ref[...] + b_ref[...]
        lane = lax.broadcasted_iota(jnp.int32, (TQ, 128), 1)
        df = jnp.where(lane < N_HEADS, dlogf * _sigmoid(-x), 0.0)
        db_ref[...] += jnp.sum(df, axis=0, keepdims=True)
        dz_ref[...] = (df + dkr_ref[...]).astype(dz_ref.dtype)

    rv = lambda i: nb - 1 - i
    return pl.pallas_call(
        body, grid=(nb,),
        in_specs=[pl.BlockSpec((1, 8, TQ), lambda i: (rv(i), 0, 0)), pl.BlockSpec((TQ, 128), lambda i: (rv(i), 0)),
                  pl.BlockSpec((TQ, 128), lambda i: (rv(i), NZ // 128 - 1)),
                  pl.BlockSpec((1, 128), lambda i: (0, 0)), pl.BlockSpec((TQ, 128), lambda i: (rv(i), 0))],
        out_specs=[pl.BlockSpec((TQ, 128), lambda i: (rv(i), 0)), pl.BlockSpec((1, 128), lambda i: (0, 0))],
        out_shape=[SDS((s, 128), _MXU), SDS((1, 128), F32)], scratch_shapes=[pltpu.VMEM((1, 128), F32)],
        compiler_params=_cp("arbitrary"), name=name)(dcr, dcq, z, b_f, dkr)


NEG = -1e30


def _causal_mask(shape, transposed=False):
    r = lax.broadcasted_iota(jnp.int32, shape, 0)
    c = lax.broadcasted_iota(jnp.int32, shape, 1)
    return (c >= r) if transposed else (r >= c)


def _head_lanes(h, dqk):
    return slice(128 * (h // 2), 128 * (h // 2) + 128) if dqk == HEAD_DIM else slice(128 * h, 128 * h + 128)


def _keep_half(x, a, axis):
    idx = lax.broadcasted_iota(jnp.int32, x.shape, axis)
    return jnp.where((idx < HEAD_DIM) if a == 0 else (idx >= HEAD_DIM), x, jnp.zeros_like(x))


def _kv_prep(z, kcol, vcol, name):
    s = z.shape[0]
    nk = s // TQ

    def body(k_ref, v_ref, kb_ref, vb_ref, vt_ref):
        kb_ref[...] = k_ref[...].astype(_MXU)
        v = v_ref[...]
        vb_ref[...] = v.astype(_MXU)
        vt_ref[0] = v.T.astype(_MXU)

    blk = pl.BlockSpec((TQ, GROUP), lambda i: (i, 0))
    return pl.pallas_call(
        body, grid=(nk,),
        in_specs=[pl.BlockSpec((TQ, GROUP), lambda i: (i, kcol)), pl.BlockSpec((TQ, GROUP), lambda i: (i, vcol))],
        out_specs=[blk, blk, pl.BlockSpec((1, GROUP, TQ), lambda i: (i, 0, 0))],
        out_shape=[SDS((s, GROUP), _MXU), SDS((s, GROUP), _MXU), SDS((nk, GROUP, TQ), _MXU)],
        compiler_params=_cp("parallel"), name=name)(z, z)


LOG2E = 1.4426950408889634


def _attn_fwd(q, qcol, dqk, kb, vt, scale, ck2, name, comm=None):
    s = q.shape[0]
    nq = s // TQ
    wq = N_HEADS * dqk
    bias = ck2 is not None

    def body(*refs):
        ins, (o_ref, l_ref), _, cc = _split_refs(refs, 4 if bias else 3, 2, comm)
        if bias:
            q_ref, k_ref, vt_ref, cc_ref = ins
        else:
            q_ref, k_ref, vt_ref = ins
        i = pl.program_id(0)
        _host_gather(comm, cc, i, nq)
        qts = []
        for h in range(N_HEADS):
            qt = (q_ref[:, _head_lanes(h, dqk)].astype(F32) * (scale * LOG2E)).T
            qts.append((_keep_half(qt, h % 2, 0) if dqk == HEAD_DIM else qt).astype(_MXU))

        def step(j, carry, masked):
            r0 = pl.multiple_of(j * TQ, TQ)
            vtj = vt_ref[j]
            sts = [jnp.dot(k_ref[pl.ds(r0, TQ), _head_lanes(h, dqk)], qts[h], preferred_element_type=F32)
                   for h in range(N_HEADS)]
            stats, ps = [], []
            for h in range(N_HEADS):
                m, l, _ = carry[3 * h:3 * h + 3]
                st = sts[h]
                if bias:
                    st = st - cc_ref[pl.ds(r0, TQ), h:h + 1]
                if masked:
                    st = jnp.where(_causal_mask(st.shape, transposed=True), st, NEG)
                m_new = jnp.maximum(m, jnp.max(st, axis=0, keepdims=True))
                alpha = jnp.exp2(m - m_new)
                p = jnp.exp2(st - m_new)
                stats.append((m_new, alpha * l + jnp.sum(p, axis=0, keepdims=True), alpha))
                ps.append(p.astype(_MXU))
            out = []
            for h in range(N_HEADS):
                m_new, l, alpha = stats[h]
                acc = alpha * carry[3 * h + 2] + jnp.dot(vtj[HEAD_DIM * h:HEAD_DIM * (h + 1), :], ps[h],
                                                         preferred_element_type=F32)
                out += [m_new, l, acc]
            return tuple(out)

        init = (jnp.full((1, TQ), NEG, F32), jnp.zeros((1, TQ), F32), jnp.zeros((HEAD_DIM, TQ), F32)) * N_HEADS
        carry = lax.fori_loop(0, i, functools.partial(step, masked=False), init)
        carry = step(i, carry, True)
        l_ref[...] = jnp.zeros_like(l_ref)
        for h in range(N_HEADS):
            l_ref[0, h:h + 1, :] = carry[3 * h] + jnp.log2(carry[3 * h + 1])
        for p in range(2):
            ot = jnp.concatenate([carry[6 * p + 2] / carry[6 * p + 1], carry[6 * p + 5] / carry[6 * p + 4]], axis=0)
            o_ref[:, 128 * p:128 * (p + 1)] = ot.T
        if comm is not None:
            @pl.when(i == nq - 1)
            def _():
                comm.wait(*cc)

    rows = pl.BlockSpec((1, 8, TQ), lambda i: (i, 0, 0))
    in_specs = [pl.BlockSpec((TQ, wq), lambda i: (i, qcol)), pl.BlockSpec((s, wq), lambda i: (0, 0)),
                pl.BlockSpec((nq, GROUP, TQ), lambda i: (0, 0, 0))]
    args = [q, kb, vt]
    if bias:
        in_specs.append(pl.BlockSpec((s, 128), lambda i: (0, 0)))
        args.append(ck2)
    out_specs = [pl.BlockSpec((TQ, GROUP), lambda i: (i, 0)), rows]
    out_shape = [SDS((s, GROUP), F32), SDS((nq, 8, TQ), F32)]
    return _call_with_comm(body, (nq,), in_specs, out_specs, out_shape, [], args, comm, ("arbitrary",), name)


def _call_with_comm(body, grid, in_specs, out_specs, out_shape, scratch, args, comm, semantics, name):
    n_out = len(out_shape)
    if comm is not None:
        in_specs, out_specs = in_specs + comm.in_specs, out_specs + comm.out_specs
        out_shape, scratch, args = out_shape + comm.out_shape, scratch + comm.scratch, list(args) + comm.arrs
    res = pl.pallas_call(body, grid=grid, in_specs=in_specs, out_specs=out_specs, out_shape=out_shape,
                         scratch_shapes=scratch, compiler_params=_cp(*semantics), name=name)(*args)
    return (*res[:n_out], list(res[n_out:]))


def _attn_bwd_prep(q, qcol, dqk, scale, o, do, name):
    s = q.shape[0]
    nq = s // TQ
    wq = N_HEADS * dqk

    def body(q_ref, o_ref, do_ref, qt_ref, dot_ref, dl_ref):
        qt_ref[0] = (q_ref[...].astype(F32) * (scale * LOG2E)).T.astype(_MXU)
        dov = do_ref[...]
        dot_ref[0] = dov.T.astype(_MXU)
        pt = (dov * o_ref[...]).T
        dl_ref[...] = jnp.zeros_like(dl_ref)
        for h in range(N_HEADS):
            dl_ref[0, h:h + 1, :] = jnp.sum(pt[HEAD_DIM * h:HEAD_DIM * (h + 1), :], axis=0, keepdims=True)

    nat = lambda w: pl.BlockSpec((TQ, w), lambda i: (i, 0))
    tr = lambda w: pl.BlockSpec((1, w, TQ), lambda i: (i, 0, 0))
    return pl.pallas_call(
        body, grid=(nq,),
        in_specs=[pl.BlockSpec((TQ, wq), lambda i: (i, qcol)), nat(GROUP), nat(GROUP)],
        out_specs=[tr(wq), tr(GROUP), tr(8)],
        out_shape=[SDS((nq, wq, TQ), _MXU), SDS((nq, GROUP, TQ), _MXU), SDS((nq, 8, TQ), F32)],
        compiler_params=_cp("parallel"), name=name)(q, o, do)


def _attn_bwd(kb, vb, qt, dot, lse, dl, dqk, scale, ck2, name, kv_dtype, comm=None):
    s = kb.shape[0]
    nq = s // TQ
    wq = N_HEADS * dqk
    bias = ck2 is not None

    def body(*refs):
        ins, outs, _, cc = _split_refs(refs, 7 if bias else 6, 5 if bias else 3, comm)
        if bias:
            k_ref, v_ref, qt_ref, dot_ref, l_ref, d_ref, cc_ref = ins
            dqt_ref, dk_ref, dv_ref, dck_ref, dcq_ref = outs
        else:
            k_ref, v_ref, qt_ref, dot_ref, l_ref, d_ref = ins
            dqt_ref, dk_ref, dv_ref = outs
        j = pl.program_id(0)

        @pl.when(j == 0)
        def _():
            if comm is not None:
                comm.start(*cc)
            dqt_ref[...] = jnp.zeros_like(dqt_ref)
            if bias:
                dcq_ref[...] = jnp.zeros_like(dcq_ref)

        ks, kts, vs = [], [], []
        for h in range(N_HEADS):
            k2 = k_ref[:, _head_lanes(h, dqk)]
            if dqk == HEAD_DIM:
                k2 = _keep_half(k2, h % 2, 1)
            ks.append(k2)
            kts.append(k2.astype(F32).T.astype(_MXU))
            vs.append(_keep_half(v_ref[:, _head_lanes(h, HEAD_DIM)], h % 2, 1))
        cks = [cc_ref[:, h:h + 1] for h in range(N_HEADS)] if bias else None

        nt = (((1,), (1,)), ((), ()))

        def step(i, carry, masked):
            qti, doti, li, di = qt_ref[i], dot_ref[i], l_ref[i], d_ref[i]
            qls = [_head_lanes(h, dqk) for h in range(N_HEADS)]
            vls = [_head_lanes(h, HEAD_DIM) for h in range(N_HEADS)]
            sts = [jnp.dot(ks[h], qti[qls[h], :], preferred_element_type=F32) for h in range(N_HEADS)]
            dpts = [jnp.dot(vs[h], doti[vls[h], :], preferred_element_type=F32) for h in range(N_HEADS)]
            pbs, dsbs, dcks = [], [], []
            for h in range(N_HEADS):
                st = sts[h] - li[h:h + 1, :]
                if bias:
                    st = st - cks[h]
                p = jnp.exp2(st)
                if masked:
                    p = jnp.where(_causal_mask(p.shape, transposed=True), p, 0.0)
                dst = p * (dpts[h] - di[h:h + 1, :])
                pbs.append(p.astype(_MXU))
                dsbs.append(dst.astype(_MXU))
                if bias:
                    dcks.append(carry[3 * h + 2] + jnp.sum(dst, axis=1, keepdims=True))
                    dcq_ref[i, h:h + 1, :] += jnp.sum(dst, axis=0, keepdims=True)
                else:
                    dcks.append(carry[3 * h + 2])
            out = []
            for h in range(N_HEADS):
                dvt = carry[3 * h + 1] + lax.dot_general(doti[HEAD_DIM * h:HEAD_DIM * (h + 1), :], pbs[h], nt,
                                                         preferred_element_type=F32)
                dkt = carry[3 * h] + lax.dot_general(qti[dqk * h:dqk * (h + 1), :], dsbs[h], nt, preferred_element_type=F32)
                dqt_ref[i, qls[h], :] += jnp.dot(kts[h], dsbs[h], preferred_element_type=F32) * scale
                out += [dkt, dvt, dcks[h]]
            return tuple(out)

        init = (jnp.zeros((dqk, TQ), F32), jnp.zeros((HEAD_DIM, TQ), F32), jnp.zeros((TQ, 1), F32)) * N_HEADS
        carry = step(j, init, True)
        carry = lax.fori_loop(j + 1, nq, functools.partial(step, masked=False), carry)
        for p in range(2):
            dv_ref[:, 128 * p:128 * (p + 1)] = jnp.concatenate([carry[6 * p + 1], carry[6 * p + 4]], axis=0).T.astype(dv_ref.dtype)
            if dqk == HEAD_DIM:
                dk_ref[:, 128 * p:128 * (p + 1)] = (jnp.concatenate([carry[6 * p], carry[6 * p + 3]], axis=0).T
                                                    * (1.0 / LOG2E)).astype(dk_ref.dtype)
        if dqk != HEAD_DIM:
            for h in range(N_HEADS):
                dk_ref[:, 128 * h:128 * (h + 1)] = (carry[3 * h].T * (1.0 / LOG2E)).astype(dk_ref.dtype)
        if bias:
            dck_ref[...] = jnp.zeros_like(dck_ref)
            for h in range(N_HEADS):
                dck_ref[:, h:h + 1] = -carry[3 * h + 2]
        if comm is not None:
            @pl.when(j == nq - 1)
            def _():
                comm.wait(*cc)

    blk = lambda w: pl.BlockSpec((TQ, w), lambda j: (j, 0))
    full3 = lambda w: pl.BlockSpec((nq, w, TQ), lambda j: (0, 0, 0))
    in_specs = [blk(wq), blk(GROUP), full3(wq), full3(GROUP), full3(8), full3(8)]
    args = [kb, vb, qt, dot, lse, dl]
    out_specs = [full3(wq), blk(wq), blk(GROUP)]
    out_shape = [SDS((nq, wq, TQ), F32), SDS((s, wq), kv_dtype), SDS((s, GROUP), kv_dtype)]
    if bias:
        in_specs.append(blk(128))
        args.append(ck2)
        out_specs += [blk(128), full3(8)]
        out_shape += [SDS((s, 128), F32), SDS((nq, 8, TQ), F32)]
    return _call_with_comm(body, (nq,), in_specs, out_specs, out_shape, [], args, comm, ("arbitrary",), name)


def _untranspose(xt, dtype, name):
    nq, w, _ = xt.shape

    def body(x_ref, o_ref):
        o_ref[...] = x_ref[0].T.astype(o_ref.dtype)

    return pl.pallas_call(
        body, grid=(nq,), in_specs=[pl.BlockSpec((1, w, TQ), lambda i: (i, 0, 0))],
        out_specs=pl.BlockSpec((TQ, w), lambda i: (i, 0)), out_shape=SDS((nq * TQ, w), dtype),
        compiler_params=_cp("parallel"), name=name)(xt)


_SCALE_D = (64 + 32) ** -0.5
_COL_CQ, _COL_CKV, _COL_MISC = 2304 // 256, 2560 // 128, 2688 // 128


def _mla_prep(z, gq, gkv, wq, wk, wv, tb, name):
    s = z.shape[0]
    tm = TQ
    row = lambda w, c: pl.BlockSpec((tm, w), lambda i, c=c: (i, c))
    const = lambda a: pl.BlockSpec(a.shape, lambda i: (0,) * a.ndim)

    def body(cq_ref, ckv_ref, m_ref, gq_ref, gkv_ref, wq_ref, wk_ref, wv_ref, e_ref, qc_ref, qs_ref, kc_ref, ks_ref,
             q_ref, k_ref, v_ref, vt_ref, cqn_ref, ckvn_ref):
        cqn = _rms(cq_ref[...], gq_ref[...]).astype(_MXU)
        ckvn = _rms(ckv_ref[...], gkv_ref[...]).astype(_MXU)
        cqn_ref[...] = cqn
        ckvn_ref[...] = ckvn
        q_ref[...] = _rope(_dot(cqn, wq_ref[...]), _lanes(qc_ref[...], 512), _lanes(qs_ref[...], 512), 16).astype(q_ref.dtype)
        kr = _rope(m_ref[...], kc_ref[...], ks_ref[...], 16)
        k_ref[...] = (_dot(ckvn, wk_ref[...]) + _dot(kr, e_ref[...])).astype(k_ref.dtype)
        v = _dot(ckvn, wv_ref[...])
        v_ref[...] = v.astype(v_ref.dtype)
        vt_ref[0] = v.T.astype(vt_ref.dtype)

    e = tb["place"]
    return pl.pallas_call(
        body, grid=(s // tm,),
        in_specs=[row(256, _COL_CQ), row(128, _COL_CKV), row(128, _COL_MISC), const(gq), const(gkv), const(wq), const(wk),
                  const(wv), const(e), row(128, 0), row(128, 0), row(128, 0), row(128, 0)],
        out_specs=[row(512, 0), row(512, 0), row(256, 0), pl.BlockSpec((1, GROUP, TQ), lambda i: (i, 0, 0)), row(256, 0),
                   row(128, 0)],
        out_shape=[SDS((s, 512), _MXU), SDS((s, 512), _MXU), SDS((s, 256), _MXU), SDS((s // TQ, GROUP, TQ), _MXU),
                   SDS((s, 256), _MXU), SDS((s, 128), _MXU)],
        compiler_params=_cp("parallel"), name=name)(
            z, z, z, gq, gkv, wq, wk, wv, e, tb["q_cos"], tb["q_sin"], tb["k_cos"], tb["k_sin"])


def _mla_prep_bwd(dq, dk, dv, z, cqn, ckvn, gq, gkv, wq, wk, wv, tb, name):
    s = z.shape[0]
    tm = min(512, s)
    row = lambda w, c: pl.BlockSpec((tm, w), lambda i, c=c: (i, c))
    const = lambda a: pl.BlockSpec(a.shape, lambda i: (0,) * a.ndim)
    acc = lambda shape: pl.BlockSpec(shape, lambda i: (0, 0))

    def body(dq_ref, dk_ref, dv_ref, cq_ref, ckv_ref, cqn_ref, ckvn_ref, gq_ref, gkv_ref, wq_ref, wk_ref, wv_ref, e_ref,
             qc_ref, qs_ref, kc_ref, ks_ref, dcq_ref, dckv_ref, dkr_ref, dwq_ref, dwk_ref, dwv_ref, dgq_ref, dgkv_ref):
        @pl.when(pl.program_id(0) == 0)
        def _():
            for r in (dwq_ref, dwk_ref, dwv_ref, dgq_ref, dgkv_ref):
                r[...] = jnp.zeros_like(r)

        dqp = _rope_bwd(dq_ref[...], _lanes(qc_ref[...], 512), _lanes(qs_ref[...], 512), 16)
        dkd = dk_ref[...]
        dvd = dv_ref[...]
        dwq_ref[...] += _dot_tn(cqn_ref[...], dqp)
        dwk_ref[...] += _dot_tn(ckvn_ref[...], dkd)
        dwv_ref[...] += _dot_tn(ckvn_ref[...], dvd)
        dcq, dgq = _rms_bwd(cq_ref[...], gq_ref[...], _dot_nt(dqp, wq_ref[...]))
        dckv, dgkv = _rms_bwd(ckv_ref[...], gkv_ref[...], _dot_nt(dkd, wk_ref[...]) + _dot_nt(dvd, wv_ref[...]))
        dcq_ref[...] = dcq.astype(dcq_ref.dtype)
        dckv_ref[...] = dckv.astype(dckv_ref.dtype)
        dgq_ref[...] += dgq
        dgkv_ref[...] += dgkv
        dkr = _dot_exact(dkd, e_ref[...], (((1,), (1,)), ((), ())))
        dkr_ref[...] = _rope_bwd(dkr, kc_ref[...], ks_ref[...], 16)

    e = tb["place"]
    return pl.pallas_call(
        body, grid=(s // tm,),
        in_specs=[row(512, 0), row(512, 0), row(256, 0), row(256, _COL_CQ), row(128, _COL_CKV), row(256, 0), row(128, 0),
                  const(gq), const(gkv), const(wq), const(wk), const(wv), const(e), row(128, 0), row(128, 0), row(128, 0), row(128, 0)],
        out_specs=[row(256, 0), row(128, 0), row(128, 0), acc((256, 512)), acc((128, 512)), acc((128, 256)), acc((1, 256)),
                   acc((1, 128))],
        out_shape=[SDS((s, 256), _MXU), SDS((s, 128), _MXU), SDS((s, 128), F32), SDS((256, 512), F32), SDS((128, 512), F32),
                   SDS((128, 256), F32), SDS((1, 256), F32), SDS((1, 128), F32)],
        compiler_params=_cp("arbitrary"), name=name)(
            dq, dk, dv, z, z, cqn, ckvn, gq, gkv, wq, wk, wv, e, tb["q_cos"], tb["q_sin"], tb["k_cos"], tb["k_sin"])


def _out_proj(ys, g, w, x, name):
    s, d = x.shape
    tm = min(512, s)

    def body(ya, yb, yc, yd, g_ref, w_ref, x_ref, o_ref, yn_ref):
        acc = x_ref[...]
        for i, y_ref in enumerate((ya, yb, yc, yd)):
            sl = slice(GROUP * i, GROUP * (i + 1))
            yn = _rms(y_ref[...], g_ref[:, sl]).astype(_MXU)
            yn_ref[:, sl] = yn
            acc = acc + jnp.dot(yn, w_ref[sl, :], preferred_element_type=F32)
        o_ref[...] = acc

    yspec = pl.BlockSpec((tm, GROUP), lambda i: (i, 0))
    return pl.pallas_call(
        body, grid=(s // tm,),
        in_specs=[yspec, yspec, yspec, yspec, pl.BlockSpec((1, d), lambda i: (0, 0)), pl.BlockSpec((d, d), lambda i: (0, 0)),
                  pl.BlockSpec((tm, d), lambda i: (i, 0))],
        out_specs=[pl.BlockSpec((tm, d), lambda i: (i, 0)), pl.BlockSpec((tm, d), lambda i: (i, 0))],
        out_shape=[SDS((s, d), F32), SDS((s, d), _MXU)], compiler_params=_cp("parallel"), name=name)(*ys, g, w, x)


def _out_proj_bwd(dx, w, ys, g, name):
    s, d = dx.shape
    tm = min(512, s)

    def body(dx_ref, w_ref, ya, yb, yc, yd, g_ref, da, db, dc, dd, dg_ref):
        @pl.when(pl.program_id(0) == 0)
        def _():
            dg_ref[...] = jnp.zeros_like(dg_ref)

        dyn = _dot_nt(dx_ref[...], w_ref[...])
        outs = (da, db, dc, dd)
        for i, y_ref in enumerate((ya, yb, yc, yd)):
            sl = slice(GROUP * i, GROUP * (i + 1))
            dy, dg = _rms_bwd(y_ref[...], g_ref[:, sl], dyn[:, sl])
            outs[i][...] = dy
            dg_ref[:, sl] += dg

    yspec = pl.BlockSpec((tm, GROUP), lambda i: (i, 0))
    return pl.pallas_call(
        body, grid=(s // tm,),
        in_specs=[pl.BlockSpec((tm, d), lambda i: (i, 0)), pl.BlockSpec((d, d), lambda i: (0, 0)), yspec, yspec, yspec, yspec,
                  pl.BlockSpec((1, d), lambda i: (0, 0))],
        out_specs=[yspec, yspec, yspec, yspec, pl.BlockSpec((1, d), lambda i: (0, 0))],
        out_shape=[SDS((s, GROUP), F32)] * 4 + [SDS((1, d), F32)],
        compiler_params=_cp("arbitrary"), name=name)(dx, w, *ys, g)


FF_BLOCK = 512
FF_ROWS = 1024


def _ffn_fwd(x, g, wu, wd, name, comm=None):
    s, d = x.shape
    nj = wu.shape[0]
    tm = min(FF_ROWS, s)
    ni = s // tm

    def body(*refs):
        (x_ref, g_ref, wu_ref, wd_ref), (o_ref, u_ref, h_ref), (acc,), cc = _split_refs(refs, 4, 3, comm)
        i, j = pl.program_id(0), pl.program_id(1)
        _host_gather(comm, cc, i * nj + j, ni * nj)

        @pl.when(j == 0)
        def _():
            h_ref[...] = _rms(x_ref[...], g_ref[...]).astype(h_ref.dtype)
            acc[...] = jnp.zeros_like(acc)

        halves = [slice(r, r + tm // 2) for r in range(0, tm, tm // 2)]
        us = [jnp.dot(h_ref[r, :], wu_ref[0], preferred_element_type=F32) for r in halves]
        for r, u in zip(halves, us):
            u_ref[r, :] = u.astype(u_ref.dtype)
            acc[r, :] += _dot(jnp.square(jnp.maximum(u, 0.0)), wd_ref[...])

        @pl.when(j == nj - 1)
        def _():
            o_ref[...] = x_ref[...] + acc[...]

        if comm is not None:
            @pl.when((i == ni - 1) & (j == nj - 1))
            def _():
                comm.wait(*cc)

    in_specs = [pl.BlockSpec((tm, d), lambda i, j: (i, 0)), pl.BlockSpec((1, d), lambda i, j: (0, 0)),
                pl.BlockSpec((1, d, FF_BLOCK), lambda i, j: (j, 0, 0)), pl.BlockSpec((FF_BLOCK, d), lambda i, j: (j, 0))]
    out_specs = [pl.BlockSpec((tm, d), lambda i, j: (i, 0)), pl.BlockSpec((tm, FF_BLOCK), lambda i, j: (i, j)),
                 pl.BlockSpec((tm, d), lambda i, j: (i, 0))]
    out_shape = [SDS((s, d), F32), SDS((s, nj * FF_BLOCK), _MXU), SDS((s, d), _MXU)]
    return _call_with_comm(body, (ni, nj), in_specs, out_specs, out_shape, [pltpu.VMEM((tm, d), F32)], [x, g, wu, wd], comm,
                           ("arbitrary", "arbitrary"), name)


def _ffn_bwd(dx2, x, u, g, wu, wd, name, comm=None):
    s, d = x.shape
    nj = wu.shape[0]
    tm = min(FF_ROWS, s)
    ni = s // tm

    def body(*refs):
        (dx_ref, x_ref, u_ref, g_ref, wu_ref, wd_ref), (o_ref, du_ref, dg_ref), (acc, dxb), cc = _split_refs(refs, 6, 3, comm)
        i, j = pl.program_id(0), pl.program_id(1)

        @pl.when((i == 0) & (j == 0))
        def _():
            if comm is not None:
                comm.start(*cc)
            dg_ref[...] = jnp.zeros_like(dg_ref)

        @pl.when(j == 0)
        def _():
            dxb[...] = dx_ref[...].astype(dxb.dtype)
            acc[...] = jnp.zeros_like(acc)

        nt = (((1,), (1,)), ((), ()))
        halves = [slice(r, r + tm // 2) for r in range(0, tm, tm // 2)]
        das = [lax.dot_general(dxb[r, :], wd_ref[...], nt, preferred_element_type=F32) for r in halves]
        for r, da in zip(halves, das):
            du = (da * 2.0 * jnp.maximum(u_ref[r, :].astype(F32), 0.0)).astype(du_ref.dtype)
            du_ref[r, :] = du
            acc[r, :] += lax.dot_general(du, wu_ref[0], nt, preferred_element_type=F32)

        @pl.when(j == nj - 1)
        def _():
            dxn, dg = _rms_bwd(x_ref[...], g_ref[...], acc[...])
            o_ref[...] = dx_ref[...] + dxn
            dg_ref[...] += dg

        if comm is not None:
            @pl.when((i == ni - 1) & (j == nj - 1))
            def _():
                comm.wait(*cc)

    in_specs = [pl.BlockSpec((tm, d), lambda i, j: (i, 0)), pl.BlockSpec((tm, d), lambda i, j: (i, 0)),
                pl.BlockSpec((tm, FF_BLOCK), lambda i, j: (i, j)), pl.BlockSpec((1, d), lambda i, j: (0, 0)),
                pl.BlockSpec((1, d, FF_BLOCK), lambda i, j: (j, 0, 0)), pl.BlockSpec((FF_BLOCK, d), lambda i, j: (j, 0))]
    out_specs = [pl.BlockSpec((tm, d), lambda i, j: (i, 0)), pl.BlockSpec((tm, FF_BLOCK), lambda i, j: (i, j)),
                 pl.BlockSpec((1, d), lambda i, j: (0, 0))]
    out_shape = [SDS((s, d), F32), SDS((s, nj * FF_BLOCK), _MXU), SDS((1, d), F32)]
    return _call_with_comm(body, (ni, nj), in_specs, out_specs, out_shape,
                           [pltpu.VMEM((tm, d), F32), pltpu.VMEM((tm, d), _MXU)], [dx2, x, u, g, wu, wd], comm,
                           ("arbitrary", "arbitrary"), name)


def _in_proj_bwd(dz, w, x, g, dx_up, name, comm=None):
    s, d = x.shape
    n = w.shape[1]
    tm = min(512, s)
    ni = s // tm

    def body(*refs):
        (dz_ref, w_ref, x_ref, g_ref, up_ref), (o_ref, dg_ref), _, cc = _split_refs(refs, 5, 2, comm)
        i = pl.program_id(0)

        @pl.when(i == 0)
        def _():
            if comm is not None:
                comm.start(*cc)
            dg_ref[...] = jnp.zeros_like(dg_ref)

        dh = lax.dot_general(dz_ref[...], w_ref[...], (((1,), (1,)), ((), ())), preferred_element_type=F32)
        dxn, dg = _rms_bwd(x_ref[...], g_ref[...], dh)
        o_ref[...] = up_ref[...] + dxn
        dg_ref[...] += dg
        if comm is not None:
            @pl.when(i == ni - 1)
            def _():
                comm.wait(*cc)

    in_specs = [pl.BlockSpec((tm, n), lambda i: (i, 0)), pl.BlockSpec((d, n), lambda i: (0, 0)),
                pl.BlockSpec((tm, d), lambda i: (i, 0)), pl.BlockSpec((1, d), lambda i: (0, 0)),
                pl.BlockSpec((tm, d), lambda i: (i, 0))]
    out_specs = [pl.BlockSpec((tm, d), lambda i: (i, 0)), pl.BlockSpec((1, d), lambda i: (0, 0))]
    out_shape = [SDS((s, d), F32), SDS((1, d), F32)]
    return _call_with_comm(body, (ni,), in_specs, out_specs, out_shape, [], [dz, w, x, g, dx_up], comm, ("arbitrary",), name)


def _loss_head(x, g, target, name):
    s, d = x.shape
    tm = min(512, s)

    def body(x_ref, g_ref, t_ref, l_ref, dx_ref, dg_ref):
        @pl.when(pl.program_id(0) == 0)
        def _():
            l_ref[...] = jnp.zeros_like(l_ref)
            dg_ref[...] = jnp.zeros_like(dg_ref)

        xv = x_ref[...]
        err = _rms(xv, g_ref[...]) - t_ref[...]
        l_ref[...] += jnp.sum(err * err, axis=0, keepdims=True) * (0.5 / d)
        dx, dg = _rms_bwd(xv, g_ref[...], err * (1.0 / d))
        dx_ref[...] = dx
        dg_ref[...] += dg

    return pl.pallas_call(
        body, grid=(s // tm,),
        in_specs=[pl.BlockSpec((tm, d), lambda i: (i, 0)), pl.BlockSpec((1, d), lambda i: (0, 0)),
                  pl.BlockSpec((tm, d), lambda i: (i, 0))],
        out_specs=[pl.BlockSpec((1, d), lambda i: (0, 0)), pl.BlockSpec((tm, d), lambda i: (i, 0)),
                   pl.BlockSpec((1, d), lambda i: (0, 0))],
        out_shape=[SDS((1, d), F32), SDS((s, d), F32), SDS((1, d), F32)], compiler_params=_cp("arbitrary"), name=name)(x, g, target)


def _me_and_peer():
    x, y, c = lax.axis_index("x"), lax.axis_index("y"), lax.axis_index("c")
    me = 4 * x + 2 * y + c

    def peer(k):
        px, py, pc = x ^ (k >> 2), y ^ ((k >> 1) & 1), c ^ (k & 1)
        return (px, py, pc), 4 * px + 2 * py + pc

    return me, peer


class _Comm:
    CHIPS = (2, 4, 6)

    def __init__(self, kind, arrs):
        assert kind in ("gather", "exchange")
        self.kind, self.arrs, self.n = kind, list(arrs), len(arrs)
        anyspec = pl.BlockSpec(memory_space=pl.ANY)
        self.in_specs = [anyspec] * self.n
        self.out_specs = [anyspec] * self.n
        self.out_shape = [SDS(((NDEV,) + a.shape) if kind == "gather" else a.shape, a.dtype) for a in self.arrs]
        npair = NDEV - 1 + len(self.CHIPS)
        self.scratch = [pltpu.SemaphoreType.DMA((self.n, npair)), pltpu.SemaphoreType.DMA((self.n, npair)),
                        pltpu.SemaphoreType.DMA((self.n,))]

    def _copies(self, ins, outs, sems):
        send, recv, loc = sems
        me, peer = _me_and_peer()
        gather = self.kind == "gather"
        sibling = peer(1)[0]
        local = [pltpu.make_async_copy(ins[a] if gather else ins[a].at[me], outs[a].at[me], loc.at[a]) for a in range(self.n)]
        outgoing, incoming, forwards, forwarded = [], [], [], []
        for k in ((1,) + self.CHIPS) if gather else range(1, NDEV):
            dev, pid = peer(k)
            for a in range(self.n):
                pair = dict(send_sem=send.at[a, k - 1], recv_sem=recv.at[a, k - 1], device_id=dev, device_id_type=MESH)
                outgoing.append(pltpu.make_async_remote_copy(src_ref=ins[a] if gather else ins[a].at[pid],
                                                             dst_ref=outs[a].at[me], **pair))
                incoming.append(pltpu.make_async_remote_copy(src_ref=ins[a] if gather else ins[a].at[me],
                                                             dst_ref=outs[a].at[pid], **pair))
        if gather:
            for idx, k in enumerate(self.CHIPS):
                got, theirs = peer(k)[1], peer(k + 1)[1]
                for a in range(self.n):
                    pair = dict(send_sem=send.at[a, NDEV - 1 + idx], recv_sem=recv.at[a, NDEV - 1 + idx], device_id=sibling,
                                device_id_type=MESH)
                    forwards.append(pltpu.make_async_remote_copy(src_ref=outs[a].at[got], dst_ref=outs[a].at[got], **pair))
                    forwarded.append(pltpu.make_async_remote_copy(src_ref=outs[a].at[theirs], dst_ref=outs[a].at[theirs], **pair))
        return local, outgoing, incoming, forwards, forwarded

    def start(self, ins, outs, sems):
        local, outgoing, _, _, _ = self._copies(ins, outs, sems)
        for cp in local + outgoing:
            cp.start()

    def forward(self, ins, outs, sems):
        _, _, incoming, forwards, _ = self._copies(ins, outs, sems)
        per = self.n
        for idx in range(len(forwards) // per if per else 0):
            for a in range(per):
                incoming[(1 + idx) * per + a].wait_recv()
                forwards[idx * per + a].start()

    def wait(self, ins, outs, sems):
        local, outgoing, incoming, forwards, forwarded = self._copies(ins, outs, sems)
        for cp in (incoming[:self.n] if self.kind == "gather" else incoming) + forwarded:
            cp.wait_recv()
        for cp in outgoing + forwards:
            cp.wait_send()
        for cp in local:
            cp.wait()


def _host_gather(comm, cc, step, nsteps):
    if comm is None:
        return

    @pl.when(step == 0)
    def _():
        comm.start(*cc)

    @pl.when(step == (2 * nsteps) // 3)
    def _():
        comm.forward(*cc)


def _split_refs(refs, n_in, n_out, comm):
    c = comm.n if comm is not None else 0
    ins, cin = refs[:n_in], refs[n_in:n_in + c]
    outs, cout = refs[n_in + c:n_in + c + n_out], refs[n_in + c + n_out:n_in + 2 * c + n_out]
    rest = refs[n_in + 2 * c + n_out:]
    scratch, csem = (rest[:len(rest) - 3], rest[len(rest) - 3:]) if c else (rest, ())
    return ins, outs, scratch, (cin, cout, csem)


def _comm_call(kind, arrs, name):
    comm = _Comm(kind, arrs)

    def body(*refs):
        _, _, _, c = _split_refs(refs, 0, 0, comm)
        comm.start(*c)
        if kind == "gather":
            comm.forward(*c)
        comm.wait(*c)

    return pl.pallas_call(body, in_specs=comm.in_specs, out_specs=comm.out_specs, out_shape=comm.out_shape,
                          scratch_shapes=comm.scratch, compiler_params=pltpu.CompilerParams(has_side_effects=True),
                          name=name)(*arrs)


def _all_gather(arrs, name):
    return _comm_call("gather", arrs, name)


def _exchange(arrs, name):
    return _comm_call("exchange", arrs, name)


def _sum_slots(parts, name):
    _, r, c = parts.shape
    tr = r if r <= 512 else 512

    def body(p_ref, o_ref):
        acc = p_ref[0].astype(F32)
        for q in range(1, NDEV):
            acc = acc + p_ref[q].astype(F32)
        o_ref[...] = acc

    return pl.pallas_call(
        body, grid=(r // tr,), in_specs=[pl.BlockSpec((NDEV, tr, c), lambda i: (0, i, 0))],
        out_specs=pl.BlockSpec((tr, c), lambda i: (i, 0)), out_shape=SDS((r, c), F32),
        compiler_params=_cp("parallel"), name=name)(parts)


def _adamw(g, w, m, v, name):
    r, c = w.shape
    parts = g.ndim == 3
    tr = r
    for cand in (512, 256, 128, 64, 32, 16, 8):
        if r > cand and r % cand == 0 and cand * c * 4 <= 2 * 1024 * 1024:
            tr = cand
            break
    bc1 = 1.0 / (1.0 - ADAM_B1 ** ADAM_STEP)
    bc2 = 1.0 / (1.0 - ADAM_B2 ** ADAM_STEP)

    def body(g_ref, w_ref, m_ref, v_ref, go_ref, d_ref, mo_ref, vo_ref):
        if parts:
            gv = g_ref[0].astype(F32)
            for q in range(1, NDEV):
                gv = gv + g_ref[q].astype(F32)
        else:
            gv = g_ref[...]
        mn = ADAM_B1 * m_ref[...] + (1.0 - ADAM_B1) * gv
        vn = ADAM_B2 * v_ref[...] + (1.0 - ADAM_B2) * (gv * gv)
        go_ref[...] = gv
        mo_ref[...] = mn
        vo_ref[...] = vn
        d_ref[...] = -ADAM_LR * ((mn * bc1) / (jnp.sqrt(vn * bc2) + ADAM_EPS) + ADAM_WD * w_ref[...])

    spec = pl.BlockSpec((tr, c), lambda i: (i, 0))
    gspec = pl.BlockSpec((NDEV, tr, c), lambda i: (0, i, 0)) if parts else spec
    return pl.pallas_call(
        body, grid=(r // tr,), in_specs=[gspec, spec, spec, spec], out_specs=[spec] * 4,
        out_shape=[SDS((r, c), F32)] * 4, compiler_params=_cp("parallel"), name=name)(g, w, m, v)


def _adamw_layer(parts, w, m, v, l, prev, name):
    depth, r, c = w.shape
    tr = next(t for t in (512, 256, 128, 64, 32, 16, 8) if r % t == 0 and t * c * 4 <= 2 * 1024 * 1024)
    bc1 = 1.0 / (1.0 - ADAM_B1 ** ADAM_STEP)
    bc2 = 1.0 / (1.0 - ADAM_B2 ** ADAM_STEP)

    def body(g_ref, w_ref, m_ref, v_ref, *rest):
        go_ref, d_ref, mo_ref, vo_ref = rest[-4:]
        gv = g_ref[0].astype(F32)
        for q in range(1, NDEV):
            gv = gv + g_ref[q].astype(F32)
        mn = ADAM_B1 * m_ref[0] + (1.0 - ADAM_B1) * gv
        vn = ADAM_B2 * v_ref[0] + (1.0 - ADAM_B2) * (gv * gv)
        go_ref[0] = gv
        mo_ref[0] = mn
        vo_ref[0] = vn
        d_ref[0] = -ADAM_LR * ((mn * bc1) / (jnp.sqrt(vn * bc2) + ADAM_EPS) + ADAM_WD * w_ref[0])

    spec = pl.BlockSpec((1, tr, c), lambda i: (l, i, 0))
    in_specs = [pl.BlockSpec((NDEV, tr, c), lambda i: (0, i, 0)), spec, spec, spec]
    args = [parts, w, m, v]
    aliases = {}
    if prev is not None:
        in_specs += [pl.BlockSpec(memory_space=pl.ANY)] * 4
        args += list(prev)
        aliases = {4 + k: k for k in range(4)}
    return pl.pallas_call(
        body, grid=(r // tr,), in_specs=in_specs, out_specs=[spec] * 4, out_shape=[SDS((depth, r, c), F32)] * 4,
        input_output_aliases=aliases, compiler_params=_cp("parallel"), name=name)(*args)


def _pad_in_cols(w):
    r = w.shape[0]
    zeros = lambda n: jnp.zeros((r, n), w.dtype)
    return jnp.concatenate([w[:, :2304], w[:, 2308:2692], w[:, 2304:2308], zeros(28), w[:, 2692:2724], zeros(64)], axis=1)


def _unpad_in_cols(w):
    return jnp.concatenate([w[..., :2304], w[..., 2688:2692], w[..., 2304:2688], w[..., 2720:2752]], axis=-1)


def _pad_uq(w):
    return jnp.pad(w.reshape(256, N_HEADS, 96), ((0, 0), (0, 0), (0, 32))).reshape(256, 512)


def _unpad_uq(w):
    return w.reshape(256, N_HEADS, 128)[:, :, :96].reshape(256, 384)


def _split_ukv(w):
    r = w.reshape(128, N_HEADS, 128)
    return jnp.pad(r[:, :, :64], ((0, 0), (0, 0), (0, 64))).reshape(128, 512), r[:, :, 64:].reshape(128, 256)


def _join_ukv(dk, dv):
    return jnp.concatenate([dk.reshape(128, N_HEADS, 128)[:, :, :64], dv.reshape(128, N_HEADS, 64)], axis=-1).reshape(128, 512)


def _cols_to_full(g):
    return jnp.transpose(g, (1, 0, 2)).reshape(g.shape[1], NDEV * g.shape[2])


def kernel(x, g_mix_norm, w_in, b_forget, g_sgu, w_spatial, b_spatial, g_mla_q, w_uq, g_mla_kv, w_ukv, g_group_out, w_out, g_ffn_norm, w_up, w_down, g_final, loss_target, m_g_mix_norm, m_w_in, m_b_forget, m_g_sgu, m_w_spatial, m_b_spatial, m_g_mla_q, m_w_uq, m_g_mla_kv, m_w_ukv, m_g_group_out, m_w_out, m_g_ffn_norm, m_w_up, m_w_down, m_g_final, v_g_mix_norm, v_w_in, v_b_forget, v_g_sgu, v_w_spatial, v_b_spatial, v_g_mla_q, v_w_uq, v_g_mla_kv, v_w_ukv, v_g_group_out, v_w_out, v_g_ffn_norm, v_w_up, v_w_down, v_g_final):
    depth = w_in.shape[0]
    s, d = x.shape[1], x.shape[2]
    x0 = x.reshape(s, d)
    target = loss_target.reshape(s, d)
    tb = _tables(s)
    me = 4 * lax.axis_index("x") + 2 * lax.axis_index("y") + lax.axis_index("c")

    assert depth == 2
    shards = {}
    for l in range(depth):
        shards.update({(l, "w_in"): _pad_in_cols(w_in[l]).astype(_WIRE), (l, "w_out"): w_out[l].astype(_WIRE),
                       (l, "w_up"): w_up[l].astype(_WIRE), (l, "w_down"): w_down[l].astype(_WIRE),
                       (l, "w_uq"): w_uq[l].astype(_WIRE), (l, "w_ukv"): w_ukv[l].astype(_WIRE)})
    wts = _ShardedWeights(shards)
    wts.full[(0, "w_in")] = _all_gather([shards[(0, "w_in")]], "gather_w_in0")[0]

    row = lambda a: a.reshape(1, -1)

    def small(l):
        bf = jnp.pad(b_forget[l].reshape(1, N_HEADS), ((0, 0), (0, 128 - N_HEADS)))
        bt = jnp.pad(b_spatial[l].T, ((0, 0), (0, 128 - N_HEADS)))
        return dict(g_mix=row(g_mix_norm[l]), g_sgu=row(g_sgu[l]), w_s=w_spatial[l], b_t=bt, b_f=bf, gq=row(g_mla_q[l]),
                    gkv=row(g_mla_kv[l]), g_go=row(g_group_out[l]), g_ffn=row(g_ffn_norm[l]))

    smalls = [small(l) for l in range(depth)]
    lrow, dx, sm, dg_final = _local_step(x0, target, wts, smalls, row(g_final), tb)
    loss = lax.psum(jnp.sum(lrow), AXES)
    grad_x = dx.reshape(1, s, d)
    return _reduce_and_update(loss, grad_x, wts.recv, sm, dg_final, me, dict(
        g_mix_norm=(g_mix_norm, m_g_mix_norm, v_g_mix_norm), w_in=(w_in, m_w_in, v_w_in),
        b_forget=(b_forget, m_b_forget, v_b_forget), g_sgu=(g_sgu, m_g_sgu, v_g_sgu),
        w_spatial=(w_spatial, m_w_spatial, v_w_spatial), b_spatial=(b_spatial, m_b_spatial, v_b_spatial),
        g_mla_q=(g_mla_q, m_g_mla_q, v_g_mla_q), w_uq=(w_uq, m_w_uq, v_w_uq), g_mla_kv=(g_mla_kv, m_g_mla_kv, v_g_mla_kv),
        w_ukv=(w_ukv, m_w_ukv, v_w_ukv), g_group_out=(g_group_out, m_g_group_out, v_g_group_out),
        w_out=(w_out, m_w_out, v_w_out), g_ffn_norm=(g_ffn_norm, m_g_ffn_norm, v_g_ffn_norm), w_up=(w_up, m_w_up, v_w_up),
        w_down=(w_down, m_w_down, v_w_down), g_final=(g_final, m_g_final, v_g_final)))


_GATHER_AT = {
    "in_proj0": [(0, "w_up")],
    "fox_attn0": [(0, "w_uq"), (0, "w_ukv"), (0, "w_down")],
    "mla_attn0": [(0, "w_out"), (1, "w_in")],
    "ffn_fwd0": [(1, "w_uq"), (1, "w_ukv"), (1, "w_down")],
    "fox_attn1": [(1, "w_out")],
    "mla_attn1": [(1, "w_up")],
}
_SCATTER_AT = {
    "fox_attn_bwd1": [(1, "w_down")],
    "mla_attn_bwd1": [(1, "w_up"), (1, "w_out")],
    "ffn_bwd0": [(1, "w_in")],
    "fox_attn_bwd0": [(0, "w_down")],
    "mla_attn_bwd0": [(0, "w_up"), (0, "w_out")],
    "in_proj_bwd0": [(0, "w_in")],
}


class _FullWeights:
    def __init__(self, per_layer):
        self.per_layer, self.grads = per_layer, {}

    def get(self, l, name):
        return self.per_layer[l][name]

    def comm(self, host):
        return None

    def done(self, host, results):
        pass

    def grad(self, l, name, blocks):
        self.grads[(l, name)] = blocks


class _ShardedWeights(_FullWeights):
    def __init__(self, shards):
        self.shards, self.full, self.grads, self.recv = shards, {}, {}, {}

    def get(self, l, name):
        if name in ("wk", "wv"):
            return _split_ukv(_cols_to_full(self.full[(l, "w_ukv")]))[0 if name == "wk" else 1]
        if name == "wq":
            return _pad_uq(_cols_to_full(self.full[(l, "w_uq")]))
        g = self.full[(l, name)]
        return g if name == "w_up" else g.reshape(NDEV * g.shape[1], g.shape[2])

    def comm(self, host):
        if host in _GATHER_AT:
            return _Comm("gather", [self.shards[k] for k in _GATHER_AT[host]])
        if host in _SCATTER_AT:
            return _Comm("exchange", [self.grads[k] for k in _SCATTER_AT[host]])
        return None

    def done(self, host, results):
        if host in _GATHER_AT:
            self.full.update(zip(_GATHER_AT[host], results))
        if host in _SCATTER_AT:
            self.recv.update(zip(_SCATTER_AT[host], results))


def _local_step(x0, target, wts, smalls, g_final, tb):
    depth = len(smalls)
    s, d = x0.shape
    saved = []
    xl = x0
    for l in range(depth):
        p = smalls[l]
        z, h, got = _norm_matmul(xl, p["g_mix"], wts.get(l, "w_in"), f"in_proj{l}", wts.comm(f"in_proj{l}"))
        wts.done(f"in_proj{l}", got)
        ya = _sgu_fwd(z, p["g_sgu"], p["w_s"], p["b_t"], f"sgu_fwd{l}")
        yb, ret, states = _ret_fwd(z, tb, f"ret_fwd{l}")
        cum = _fox_prep(z, p["b_f"], f"fox_prep{l}")
        kc, vc, vtc = _kv_prep(z, 7, 8, f"fox_kv{l}")
        yc, lse_c, got = _attn_fwd(z, 6, HEAD_DIM, kc, vtc, HEAD_DIM ** -0.5, cum, f"fox_attn{l}", wts.comm(f"fox_attn{l}"))
        wts.done(f"fox_attn{l}", got)
        wq, wk, wv = wts.get(l, "wq"), wts.get(l, "wk"), wts.get(l, "wv")
        qd, kd, vd, vtd, cqn, ckvn = _mla_prep(z, p["gq"], p["gkv"], wq, wk, wv, tb, f"mla_prep{l}")
        yd, lse_d, got = _attn_fwd(qd, 0, 128, kd, vtd, _SCALE_D, None, f"mla_attn{l}", wts.comm(f"mla_attn{l}"))
        wts.done(f"mla_attn{l}", got)
        ys = (ya, yb, yc, yd)
        x1, yn = _out_proj(ys, p["g_go"], wts.get(l, "w_out"), xl, f"out_proj{l}")
        x2, u, h2, got = _ffn_fwd(x1, p["g_ffn"], wts.get(l, "w_up"), wts.get(l, "w_down"), f"ffn_fwd{l}", wts.comm(f"ffn_fwd{l}"))
        wts.done(f"ffn_fwd{l}", got)
        saved.append(dict(x=xl, z=z, h=h, ys=ys, ret=ret, states=states, cum=cum, lse_c=lse_c, kc=kc, vc=vc, qd=qd, kd=kd, vd=vd,
                          cqn=cqn, ckvn=ckvn, lse_d=lse_d, x1=x1, yn=yn, u=u, h2=h2, wq=wq, wk=wk, wv=wv))
        xl = x2

    lrow, dx, dg_final = _loss_head(xl, g_final, target, "loss_head")

    sm = [None] * depth
    for l in reversed(range(depth)):
        p, a = smalls[l], saved[l]
        dx1, du, dg_ffn, got = _ffn_bwd(dx, a["x1"], a["u"], p["g_ffn"], wts.get(l, "w_up"), wts.get(l, "w_down"), f"ffn_bwd{l}",
                                        wts.comm(f"ffn_bwd{l}"))
        wts.done(f"ffn_bwd{l}", got)
        dw_down = _mm_tn(a["u"], dx, f"dw_down{l}", a_fn=lambda t: jnp.square(jnp.maximum(t, 0.0)), out_dtype=_WIRE)
        wts.grad(l, "w_down", dw_down.reshape(NDEV, dw_down.shape[0] // NDEV, d))
        wts.grad(l, "w_up", _mm_tn(a["h2"], du, f"dw_up{l}", blocked=True, out_dtype=_WIRE))
        dya, dyb, dyc, dyd, dg_go = _out_proj_bwd(dx1, wts.get(l, "w_out"), a["ys"], p["g_go"], f"out_proj_bwd{l}")
        wts.grad(l, "w_out", _mm_tn(a["yn"], dx1, f"dw_out{l}", out_dtype=_WIRE).reshape(NDEV, d // NDEV, d))
        dz_a, dg_sgu, dw_s, db_t = _sgu_bwd(dya, a["z"], p["g_sgu"], p["w_s"], p["b_t"], f"sgu_bwd{l}")
        dz_b = _ret_bwd(dyb, a["z"], a["ret"], a["states"], tb, f"ret_bwd{l}")
        qt, dot, dl = _attn_bwd_prep(a["z"], 6, HEAD_DIM, HEAD_DIM ** -0.5, a["ys"][2], dyc, f"fox_bwd_prep{l}")
        dqt_c, dk_c, dv_c, dck, dcq, got = _attn_bwd(a["kc"], a["vc"], qt, dot, a["lse_c"], dl, HEAD_DIM,
                                                     HEAD_DIM ** -0.5, a["cum"], f"fox_attn_bwd{l}", _MXU,
                                                     wts.comm(f"fox_attn_bwd{l}"))
        wts.done(f"fox_attn_bwd{l}", got)
        dq_c = _untranspose(dqt_c, _MXU, f"fox_dq{l}")
        qt, dot, dl = _attn_bwd_prep(a["qd"], 0, 128, _SCALE_D, a["ys"][3], dyd, f"mla_bwd_prep{l}")
        dqt_d, dk_d, dv_d, got = _attn_bwd(a["kd"], a["vd"], qt, dot, a["lse_d"], dl, 128, _SCALE_D, None,
                                           f"mla_attn_bwd{l}", F32, wts.comm(f"mla_attn_bwd{l}"))
        wts.done(f"mla_attn_bwd{l}", got)
        dq_d = _untranspose(dqt_d, F32, f"mla_dq{l}")
        dz_cq, dz_ckv, dkr, dwq, dwk, dwv, dgq, dgkv = _mla_prep_bwd(dq_d, dk_d, dv_d, a["z"], a["cqn"], a["ckvn"], p["gq"],
                                                                     p["gkv"], a["wq"], a["wk"], a["wv"], tb, f"mla_prep_bwd{l}")
        dz_misc, db_f = _fox_post(dcq, dck, a["z"], p["b_f"], dkr, f"fox_post{l}")
        dz = jnp.concatenate([dz_a, dz_b, dq_c, dk_c, dv_c, dz_cq, dz_ckv, dz_misc], axis=1)
        wts.grad(l, "w_in", _unpad_in_cols(_mm_tn(a["h"], dz, f"dw_in{l}", out_dtype=_WIRE)).reshape(NDEV, d // NDEV, N_IN))
        dx, dg_mix, got = _in_proj_bwd(dz, wts.get(l, "w_in"), a["x"], p["g_mix"], dx1, f"in_proj_bwd{l}",
                                       wts.comm(f"in_proj_bwd{l}"))
        wts.done(f"in_proj_bwd{l}", got)
        sm[l] = [dg_mix, dg_go, dg_ffn, dg_sgu, dw_s, db_t[:, :N_HEADS].T, db_f[0, :N_HEADS], dgq, dgkv, _unpad_uq(dwq),
                 _join_ukv(dwk, dwv)]
    return lrow, dx, sm, dg_final


def _reduce_and_update(loss, grad_x, recv, sm, dg_final, me, given):
    depth = len(sm)
    pieces = [t for l in range(depth) for t in sm[l]] + [dg_final]
    flat = jnp.concatenate([t.reshape(-1) for t in pieces])
    n_flat = flat.shape[0]
    unit = NDEV * 8 * 128
    n_pad = -(-n_flat // unit) * unit
    packed = jnp.pad(flat, (0, n_pad - n_flat)).reshape(NDEV, n_pad // (NDEV * 128), 128)
    red = _sum_slots(_exchange([packed], "scatter_small")[0], "sum_small")
    full = _all_gather([red], "gather_small")[0].reshape(-1)
    offs = np.cumsum([0] + [int(np.prod(t.shape)) for t in pieces])
    red_pieces = [full[int(offs[i]):int(offs[i + 1])].reshape(pieces[i].shape) for i in range(len(pieces))]
    per = len(sm[0])
    stack = lambda i: jnp.stack([red_pieces[l * per + i] for l in range(depth)])
    g_small = dict(g_mix_norm=stack(0), g_group_out=stack(1), g_ffn_norm=stack(2), g_sgu=stack(3), w_spatial=stack(4),
                   b_spatial=stack(5), b_forget=stack(6), g_mla_q=stack(7), g_mla_kv=stack(8), g_final=red_pieces[-1])
    cq, ckv = given["w_uq"][0].shape[2], given["w_ukv"][0].shape[2]
    g_small["w_uq"] = lax.dynamic_slice_in_dim(stack(9), me * cq, cq, axis=2)
    g_small["w_ukv"] = lax.dynamic_slice_in_dim(stack(10), me * ckv, ckv, axis=2)

    names = list(given)
    outs = {}
    for nme in names:
        wv_, mv_, vv_ = given[nme]
        shape = wv_.shape
        if nme in ("w_in", "w_out", "w_up", "w_down"):
            res = None
            for l in range(depth):
                res = _adamw_layer(recv[(l, nme)], wv_, mv_, vv_, l, res, f"adamw_{nme}{l}")
            outs[nme] = list(res)
        else:
            two = lambda t: t.reshape(-1, shape[-1]) if t.ndim > 1 else t.reshape(1, -1)
            res = _adamw(two(g_small[nme]), two(wv_), two(mv_), two(vv_), f"adamw_{nme}")
            outs[nme] = [r.reshape(shape) for r in res]
    return (loss, grad_x, *[outs[n][0] for n in names], *[outs[n][1] for n in names], *[outs[n][2] for n in names],
            *[outs[n][3] for n in names])
```

```python
import functools

import jax
import jax.numpy as jnp
import numpy as np
from jax import lax
from jax.experimental import pallas as pl
from jax.experimental.pallas import tpu as pltpu

F32 = jnp.float32
_MXU = jnp.bfloat16
_WIRE = jnp.bfloat16
EPS = 1e-6
NDEV = 8
AXES = ("x", "y", "c")
MESH = pl.DeviceIdType.MESH

N_HEADS = 4
HEAD_DIM = 64
GROUP = 256
CHUNK = 128
NZ = 2816
N_IN = 2724
MISC_F, MISC_KR = 0, 32
VMEM_LIMIT = 56 * 1024 * 1024

ADAM_LR, ADAM_B1, ADAM_B2, ADAM_EPS, ADAM_WD, ADAM_STEP = 0.001, 0.9, 0.999, 1e-08, 0.01, 10

SDS = jax.ShapeDtypeStruct


def _cp(*sem):
    return pltpu.CompilerParams(dimension_semantics=sem, vmem_limit_bytes=VMEM_LIMIT)


def _dot(a, b):
    return jnp.dot(a.astype(_MXU), b.astype(_MXU), preferred_element_type=F32)


def _dot_nt(a, b):
    return lax.dot_general(a.astype(_MXU), b.astype(_MXU), (((1,), (1,)), ((), ())), preferred_element_type=F32)


def _dot_tn(a, b):
    return lax.dot_general(a.astype(_MXU), b.astype(_MXU), (((0,), (0,)), ((), ())), preferred_element_type=F32)


def _dot_exact(a, b, dims=(((1,), (0,)), ((), ()))):
    return lax.dot_general(a, b, dims, precision=lax.Precision.HIGHEST, preferred_element_type=F32)


def _rms(x, g):
    return x * lax.rsqrt(jnp.mean(x * x, axis=-1, keepdims=True) + EPS) * g


def _rms_bwd(x, g, dy):
    xh = x * lax.rsqrt(jnp.mean(x * x, axis=-1, keepdims=True) + EPS)
    dxh = dy * g
    r = lax.rsqrt(jnp.mean(x * x, axis=-1, keepdims=True) + EPS)
    dx = r * (dxh - xh * jnp.mean(dxh * xh, axis=-1, keepdims=True))
    return dx, jnp.sum(dy * xh, axis=0, keepdims=True)


def _standardize(t):
    mu = jnp.mean(t, axis=-1, keepdims=True)
    tc = t - mu
    rs = lax.rsqrt(jnp.mean(tc * tc, axis=-1, keepdims=True) + EPS)
    return tc * rs, rs


def _standardize_bwd(yh, rs, dy):
    return rs * (dy - jnp.mean(dy, axis=-1, keepdims=True) - yh * jnp.mean(dy * yh, axis=-1, keepdims=True))


_GELU_C = 0.7978845608028654


def _gelu(x):
    return 0.5 * x * (1.0 + jnp.tanh(_GELU_C * (x + 0.044715 * x * x * x)))


def _gelu_grad(x):
    t = jnp.tanh(_GELU_C * (x + 0.044715 * x * x * x))
    return 0.5 * (1.0 + t) + 0.5 * x * (1.0 - t * t) * _GELU_C * (1.0 + 3 * 0.044715 * x * x)


def _sigmoid(x):
    return 1.0 / (1.0 + jnp.exp(-x))


def _swap_half(t, half):
    n = t.shape[-1]
    lane = lax.broadcasted_iota(jnp.int32, t.shape, t.ndim - 1)
    return jnp.where((lane % (2 * half)) < half, pltpu.roll(t, n - half, t.ndim - 1), pltpu.roll(t, half, t.ndim - 1))


def _lanes(table, width):
    return jnp.concatenate([table] * (width // table.shape[-1]), axis=-1)


def _rope(t, cos, sin, half):
    return t * cos + _swap_half(t, half) * sin


def _rope_bwd(d, cos, sin, half):
    return d * cos - _swap_half(d, half) * sin


def _tables(s):
    pos = jnp.arange(s, dtype=F32)[:, None]

    def cs(half):
        inv = jnp.power(10000.0, -jnp.arange(half, dtype=F32) / half)
        ang = pos * inv[None, :]
        return jnp.cos(ang), jnp.sin(ang)

    c32, s32 = cs(32)
    c16, s16 = cs(16)
    z = lambda w: jnp.zeros((s, w), F32)
    o = lambda w: jnp.ones((s, w), F32)
    t = {}
    t["b_cos"] = jnp.concatenate([c32, c32, c32, c32], 1)
    t["b_sin"] = jnp.concatenate([-s32, s32, -s32, s32], 1)
    t["q_cos"] = jnp.concatenate([o(64), c16, c16, z(32)], 1)
    t["q_sin"] = jnp.concatenate([z(64), -s16, s16, z(32)], 1)
    t["k_cos"] = jnp.concatenate([z(32), c16, c16, z(64)], 1)
    t["k_sin"] = jnp.concatenate([z(32), -s16, s16, z(64)], 1)
    lg = jnp.log1p(-jnp.exp2(-5.0 - jnp.arange(N_HEADS, dtype=F32)))
    j = jnp.arange(CHUNK, dtype=F32)
    rel = j[:, None] - j[None, :]
    t["decay"] = jnp.where(rel[None] >= 0, jnp.exp(jnp.maximum(rel, 0.0)[None] * lg[:, None, None]), 0.0)

    def rows(e):
        return jnp.repeat(e.T, HEAD_DIM, axis=1)

    t["qw"] = rows(jnp.exp((j + 1.0)[None, :] * lg[:, None]))
    t["kw"] = rows(jnp.exp((CHUNK - 1 - j)[None, :] * lg[:, None]))
    t["kw2"] = rows(jnp.exp((CHUNK - j)[None, :] * lg[:, None]))
    t["qw0"] = rows(jnp.exp(j[None, :] * lg[:, None]))
    t["cd"] = jnp.repeat(jnp.exp(CHUNK * lg), HEAD_DIM)[None, :]
    e = np.zeros((128, 512), np.float32)
    for h in range(N_HEADS):
        for r in range(32):
            e[MISC_KR + r, 128 * h + 64 + r] = 1.0
    t["place"] = jnp.asarray(e)
    return t


def _norm_matmul(x, g, w, name, comm=None):
    s, d = x.shape
    n = w.shape[1]
    tm, tn = min(512, s), 256
    ni = s // tm

    def body(*refs):
        (x_ref, g_ref, w_ref), (z_ref, h_ref), _, cc = _split_refs(refs, 3, 2, comm)
        i = pl.program_id(0)
        _host_gather(comm, cc, i, ni)
        h = _rms(x_ref[...], g_ref[...]).astype(h_ref.dtype)
        h_ref[...] = h
        for j in range(n // tn):
            z_ref[:, tn * j:tn * (j + 1)] = jnp.dot(h, w_ref[:, tn * j:tn * (j + 1)], preferred_element_type=F32)
        if comm is not None:
            @pl.when(i == ni - 1)
            def _():
                comm.wait(*cc)

    in_specs = [pl.BlockSpec((tm, d), lambda i: (i, 0)), pl.BlockSpec((1, d), lambda i: (0, 0)),
                pl.BlockSpec((d, n), lambda i: (0, 0))]
    out_specs = [pl.BlockSpec((tm, n), lambda i: (i, 0)), pl.BlockSpec((tm, d), lambda i: (i, 0))]
    out_shape = [SDS((s, n), F32), SDS((s, d), _MXU)]
    return _call_with_comm(body, (ni,), in_specs, out_specs, out_shape, [], [x, g, w], comm, ("arbitrary",), name)


def _mm_tn(a, b, name, *, a_fn=None, blocked=False, out_dtype=F32):
    k, m = a.shape
    n = b.shape[1]
    tm, tk = min(1024, m), min(512, k)
    tn = next(t for t in (1408, 1024, 512, 256, 128) if n % t == 0)
    assert m % tm == 0 and k % tk == 0
    nk = k // tk

    def body(a_ref, b_ref, o_ref, acc):
        kk = pl.program_id(2)

        @pl.when(kk == 0)
        def _():
            acc[...] = jnp.zeros_like(acc)

        av = a_ref[...]
        if a_fn is not None:
            av = a_fn(av.astype(F32))
        acc[...] += _dot_tn(av, b_ref[...])

        @pl.when(kk == nk - 1)
        def _():
            if blocked:
                for c in range(tn // 512):
                    o_ref[c] = acc[:, 512 * c:512 * (c + 1)].astype(o_ref.dtype)
            else:
                o_ref[...] = acc[...].astype(o_ref.dtype)

    if blocked:
        assert tn % 512 == 0
        out_spec = pl.BlockSpec((tn // 512, tm, 512), lambda i, j, kk: (j, i, 0))
        out_shape = SDS((n // 512, m, 512), out_dtype)
    else:
        out_spec = pl.BlockSpec((tm, tn), lambda i, j, kk: (i, j))
        out_shape = SDS((m, n), out_dtype)
    return pl.pallas_call(
        body, grid=(m // tm, n // tn, nk),
        in_specs=[pl.BlockSpec((tk, tm), lambda i, j, kk: (kk, i)), pl.BlockSpec((tk, tn), lambda i, j, kk: (kk, j))],
        out_specs=out_spec, out_shape=out_shape, scratch_shapes=[pltpu.VMEM((tm, tn), F32)],
        compiler_params=_cp("parallel", "parallel", "arbitrary"), name=name)(a, b)


def _sgu_parts(u_pre, v_pre, gain):
    u = _gelu(u_pre)
    v = _gelu(v_pre)
    vh, rs, vg = [], [], []
    for h in range(N_HEADS):
        sl = slice(HEAD_DIM * h, HEAD_DIM * (h + 1))
        a, r = _standardize(v[:, sl])
        vh.append(a)
        rs.append(r)
        vg.append(a * gain[:, sl])
    return u, vh, rs, vg


def _tril(w):
    r = lax.broadcasted_iota(jnp.int32, w.shape, 0)
    c = lax.broadcasted_iota(jnp.int32, w.shape, 1)
    return jnp.where(r >= c, w, 0.0)


def _sgu_fwd(z, gain, w_s, b_t, name):
    s = z.shape[0]
    tm = min(512, s)

    def body(u_ref, v_ref, g_ref, w_ref, b_ref, y_ref):
        u, _, _, vg = _sgu_parts(u_ref[...], v_ref[...], g_ref[...])
        hc = [(h, c) for h in range(N_HEADS) for c in range(tm // CHUNK)]
        wcs = [_tril(w_ref[h]) for h in range(N_HEADS)]
        mixed = {(h, c): _dot(wcs[h], vg[h][CHUNK * c:CHUNK * (c + 1)]) for h, c in hc}
        for h, c in hc:
            r, sl = slice(CHUNK * c, CHUNK * (c + 1)), slice(HEAD_DIM * h, HEAD_DIM * (h + 1))
            y_ref[r, sl] = u[r, sl] * (mixed[h, c] + b_ref[:, h:h + 1])

    return pl.pallas_call(
        body, grid=(s // tm,),
        in_specs=[pl.BlockSpec((tm, GROUP), lambda i: (i, 0)), pl.BlockSpec((tm, GROUP), lambda i: (i, 1)),
                  pl.BlockSpec((1, GROUP), lambda i: (0, 0)), pl.BlockSpec((N_HEADS, CHUNK, CHUNK), lambda i: (0, 0, 0)),
                  pl.BlockSpec((CHUNK, 128), lambda i: (0, 0))],
        out_specs=pl.BlockSpec((tm, GROUP), lambda i: (i, 0)), out_shape=SDS((s, GROUP), F32),
        compiler_params=_cp("parallel"), name=name)(z, z, gain, w_s, b_t)


def _sgu_bwd(dy, z, gain, w_s, b_t, name):
    s = z.shape[0]
    tm = min(512, s)

    def body(dy_ref, u_ref, v_ref, g_ref, w_ref, b_ref, dz_ref, dg_ref, dw_ref, db_ref):
        @pl.when(pl.program_id(0) == 0)
        def _():
            dg_ref[...] = jnp.zeros_like(dg_ref)
            dw_ref[...] = jnp.zeros_like(dw_ref)
            db_ref[...] = jnp.zeros_like(db_ref)

        u_pre, v_pre, gain_v = u_ref[...], v_ref[...], g_ref[...]
        u, vh, rs, vg = _sgu_parts(u_pre, v_pre, gain_v)
        dyv = dy_ref[...]
        gu = _gelu_grad(u_pre)
        gv = _gelu_grad(v_pre)
        hc = [(h, c) for h in range(N_HEADS) for c in range(tm // CHUNK)]
        sls = [slice(HEAD_DIM * h, HEAD_DIM * (h + 1)) for h in range(N_HEADS)]
        rws = [slice(CHUNK * c, CHUNK * (c + 1)) for c in range(tm // CHUNK)]
        wcs = [_tril(w_ref[h]) for h in range(N_HEADS)]
        mixed = {(h, c): _dot(wcs[h], vg[h][rws[c]]) for h, c in hc}
        dms = {}
        for h, c in hc:
            r, sl = rws[c], sls[h]
            dz_ref[r, sl] = (dyv[r, sl] * (mixed[h, c] + b_ref[:, h:h + 1]) * gu[r, sl]).astype(dz_ref.dtype)
            dms[h, c] = dyv[r, sl] * u[r, sl]
        dws = {(h, c): _dot_nt(dms[h, c], vg[h][rws[c]]) for h, c in hc}
        dvgs = {(h, c): _dot_tn(wcs[h], dms[h, c]) for h, c in hc}
        for h in range(N_HEADS):
            sl = sls[h]
            dwh = jnp.zeros((CHUNK, CHUNK), F32)
            dbh = jnp.zeros((CHUNK, 1), F32)
            dgh = jnp.zeros((1, HEAD_DIM), F32)
            for c in range(tm // CHUNK):
                r = rws[c]
                dwh += dws[h, c]
                dbh += jnp.sum(dms[h, c], axis=1, keepdims=True)
                dvg = dvgs[h, c]
                dgh += jnp.sum(dvg * vh[h][r], axis=0, keepdims=True)
                dv = _standardize_bwd(vh[h][r], rs[h][r], dvg * gain_v[:, sl])
                dz_ref[r, GROUP + HEAD_DIM * h:GROUP + HEAD_DIM * (h + 1)] = (dv * gv[r, sl]).astype(dz_ref.dtype)
            dw_ref[h] += _tril(dwh)
            db_ref[:, h:h + 1] += dbh
            dg_ref[:, sl] += dgh

    return pl.pallas_call(
        body, grid=(s // tm,),
        in_specs=[pl.BlockSpec((tm, GROUP), lambda i: (i, 0)),
                  pl.BlockSpec((tm, GROUP), lambda i: (i, 0)), pl.BlockSpec((tm, GROUP), lambda i: (i, 1)),
                  pl.BlockSpec((1, GROUP), lambda i: (0, 0)), pl.BlockSpec((N_HEADS, CHUNK, CHUNK), lambda i: (0, 0, 0)),
                  pl.BlockSpec((CHUNK, 128), lambda i: (0, 0))],
        out_specs=[pl.BlockSpec((tm, 2 * GROUP), lambda i: (i, 0)), pl.BlockSpec((1, GROUP), lambda i: (0, 0)),
                   pl.BlockSpec((N_HEADS, CHUNK, CHUNK), lambda i: (0, 0, 0)), pl.BlockSpec((CHUNK, 128), lambda i: (0, 0))],
        out_shape=[SDS((s, 2 * GROUP), _MXU), SDS((1, GROUP), F32), SDS((N_HEADS, CHUNK, CHUNK), F32), SDS((CHUNK, 128), F32)],
        compiler_params=_cp("arbitrary"), name=name)(dy, z, z, gain, w_s, b_t)


_SCALE_B = HEAD_DIM ** -0.5


def _ret_fwd(z, tb, name):
    s = z.shape[0]
    nc = s // CHUNK
    row = lambda col: pl.BlockSpec((CHUNK, GROUP), lambda n, col=col: (n, col))
    const = lambda shape: pl.BlockSpec(shape, lambda n: (0,) * len(shape))

    def body(q_ref, k_ref, v_ref, g_ref, cos_ref, sin_ref, dec_ref, qw_ref, kw_ref, cd_ref, y_ref, o_ref, st_ref, state):
        @pl.when(pl.program_id(0) == 0)
        def _():
            state[...] = jnp.zeros_like(state)

        cos, sin = _lanes(cos_ref[...], GROUP), _lanes(sin_ref[...], GROUP)
        q = _rope(q_ref[...], cos, sin, 32)
        k = _rope(k_ref[...], cos, sin, 32) * _SCALE_B
        v = v_ref[...]
        g = g_ref[...]
        st_ref[0] = state[...]
        qs = q * qw_ref[...]
        ks = k * kw_ref[...]
        sls = [slice(HEAD_DIM * h, HEAD_DIM * (h + 1)) for h in range(N_HEADS)]
        scs = [_dot_nt(q[:, sl], k[:, sl]) for sl in sls]
        crs = [_dot(qs[:, sl], state[:, sl]) for sl in sls]
        kvs = [_dot_tn(ks[:, sl], v[:, sl]) for sl in sls]
        scd = [(scs[h] * dec_ref[h]).astype(_MXU) for h in range(N_HEADS)]
        ins = [_dot(scd[h], v[:, sls[h]]) for h in range(N_HEADS)]
        for h, sl in enumerate(sls):
            o = ins[h] + crs[h]
            o_ref[:, sl] = o
            yh, _ = _standardize(o)
            gh = g[:, sl]
            y_ref[:, sl] = gh * _sigmoid(gh) * yh
            state[:, sl] = cd_ref[:, sl] * state[:, sl] + kvs[h]

    return pl.pallas_call(
        body, grid=(nc,),
        in_specs=[row(2), row(3), row(4), row(5), pl.BlockSpec((CHUNK, 128), lambda n: (n, 0)),
                  pl.BlockSpec((CHUNK, 128), lambda n: (n, 0)), const((N_HEADS, CHUNK, CHUNK)),
                  const((CHUNK, GROUP)), const((CHUNK, GROUP)), const((1, GROUP))],
        out_specs=[pl.BlockSpec((CHUNK, GROUP), lambda n: (n, 0)), pl.BlockSpec((CHUNK, GROUP), lambda n: (n, 0)),
                   pl.BlockSpec((1, HEAD_DIM, GROUP), lambda n: (n, 0, 0))],
        out_shape=[SDS((s, GROUP), F32), SDS((s, GROUP), F32), SDS((nc, HEAD_DIM, GROUP), F32)],
        scratch_shapes=[pltpu.VMEM((HEAD_DIM, GROUP), F32)],
        compiler_params=_cp("arbitrary"), name=name)(z, z, z, z, tb["b_cos"], tb["b_sin"], tb["decay"], tb["qw"], tb["kw"], tb["cd"])


def _ret_bwd(dy, z, o_pre, states, tb, name):
    s = z.shape[0]
    nc = s // CHUNK
    rev = lambda col: pl.BlockSpec((CHUNK, GROUP), lambda n, col=col: (nc - 1 - n, col))
    const = lambda shape: pl.BlockSpec(shape, lambda n: (0,) * len(shape))

    def body(dy_ref, q_ref, k_ref, v_ref, g_ref, o_ref, st_ref, cos_ref, sin_ref, dec_ref, qw_ref, kw2_ref, qw0_ref, cd_ref,
             dz_ref, rstate):
        @pl.when(pl.program_id(0) == 0)
        def _():
            rstate[...] = jnp.zeros_like(rstate)

        cos, sin = _lanes(cos_ref[...], GROUP), _lanes(sin_ref[...], GROUP)
        q = _rope(q_ref[...], cos, sin, 32)
        k = _rope(k_ref[...], cos, sin, 32) * _SCALE_B
        v = v_ref[...]
        g = g_ref[...]
        dyv = dy_ref[...]
        sg = _sigmoid(g)
        silu = g * sg
        dos, dgs = [], []
        for h in range(N_HEADS):
            sl = slice(HEAD_DIM * h, HEAD_DIM * (h + 1))
            yh, rs = _standardize(o_ref[:, sl])
            dgs.append(dyv[:, sl] * yh * (sg[:, sl] * (1.0 + g[:, sl] * (1.0 - sg[:, sl]))))
            dos.append(_standardize_bwd(yh, rs, dyv[:, sl] * silu[:, sl]))
        do = jnp.concatenate(dos, axis=1)
        dow = do * qw_ref[...]
        vw = v * kw2_ref[...]
        kw = k * kw2_ref[...]
        q0 = q * qw0_ref[...]
        sls = [slice(HEAD_DIM * h, HEAD_DIM * (h + 1)) for h in range(N_HEADS)]
        sn = st_ref[0]
        rrs = [rstate[:, sl] for sl in sls]
        ps = [_dot_nt(q[:, sl], k[:, sl]) for sl in sls]
        dps = [_dot_nt(do[:, sl], v[:, sl]) for sl in sls]
        dq_x = [_dot_nt(dow[:, sl], sn[:, sl]) for sl in sls]
        dk_x = [_dot_nt(vw[:, sl], rrs[h]) for h, sl in enumerate(sls)]
        dv_x = [_dot(kw[:, sl], rrs[h]) for h, sl in enumerate(sls)]
        r_new = [_dot_tn(q0[:, sl], do[:, sl]) for sl in sls]
        pd = [(ps[h] * dec_ref[h]).astype(_MXU) for h in range(N_HEADS)]
        dpd = [(dps[h] * dec_ref[h]).astype(_MXU) for h in range(N_HEADS)]
        dq_i = [_dot(dpd[h], k[:, sl]) for h, sl in enumerate(sls)]
        dk_i = [_dot_tn(dpd[h], q[:, sl]) for h, sl in enumerate(sls)]
        dv_i = [_dot_tn(pd[h], do[:, sl]) for h, sl in enumerate(sls)]
        dqs, dks = [], []
        for h, sl in enumerate(sls):
            dqs.append(dq_i[h] + dq_x[h])
            dks.append(dk_i[h] + dk_x[h])
            dz_ref[:, 2 * GROUP + HEAD_DIM * h:2 * GROUP + HEAD_DIM * (h + 1)] = (dv_i[h] + dv_x[h]).astype(dz_ref.dtype)
            rstate[:, sl] = cd_ref[:, sl] * rrs[h] + r_new[h]
        dq = _rope_bwd(jnp.concatenate(dqs, axis=1), cos, sin, 32)
        dk = _rope_bwd(jnp.concatenate(dks, axis=1) * _SCALE_B, cos, sin, 32)
        dz_ref[:, 0:GROUP] = dq.astype(dz_ref.dtype)
        dz_ref[:, GROUP:2 * GROUP] = dk.astype(dz_ref.dtype)
        dz_ref[:, 3 * GROUP:4 * GROUP] = jnp.concatenate(dgs, axis=1).astype(dz_ref.dtype)

    r0 = lambda: pl.BlockSpec((CHUNK, GROUP), lambda n: (nc - 1 - n, 0))
    r128 = lambda: pl.BlockSpec((CHUNK, 128), lambda n: (nc - 1 - n, 0))
    return pl.pallas_call(
        body, grid=(nc,),
        in_specs=[r0(), rev(2), rev(3), rev(4), rev(5), r0(), pl.BlockSpec((1, HEAD_DIM, GROUP), lambda n: (nc - 1 - n, 0, 0)),
                  r128(), r128(), const((N_HEADS, CHUNK, CHUNK)), const((CHUNK, GROUP)), const((CHUNK, GROUP)),
                  const((CHUNK, GROUP)), const((1, GROUP))],
        out_specs=pl.BlockSpec((CHUNK, 4 * GROUP), lambda n: (nc - 1 - n, 0)),
        out_shape=SDS((s, 4 * GROUP), _MXU), scratch_shapes=[pltpu.VMEM((HEAD_DIM, GROUP), F32)],
        compiler_params=_cp("arbitrary"), name=name)(
            dy, z, z, z, z, o_pre, states, tb["b_cos"], tb["b_sin"], tb["decay"], tb["qw"], tb["kw2"], tb["qw0"], tb["cd"])


TQ = 256


def _log_sigmoid(x):
    return jnp.minimum(x, 0.0) - jnp.log1p(jnp.exp(-jnp.abs(x)))


def _fox_prep(z, b_f, name):
    s = z.shape[0]
    nb = s // TQ

    def body(m_ref, b_ref, cc_ref, carry):
        @pl.when(pl.program_id(0) == 0)
        def _():
            carry[...] = jnp.zeros_like(carry)

        lane = lax.broadcasted_iota(jnp.int32, (TQ, 128), 1)
        logf = jnp.where(lane < N_HEADS, _log_sigmoid(m_ref[...] + b_ref[...]), 0.0)
        r = lax.broadcasted_iota(jnp.int32, (TQ, TQ), 0)
        c = lax.broadcasted_iota(jnp.int32, (TQ, TQ), 1)
        tri = jnp.where(r >= c, 1.0, 0.0).astype(F32)
        cum = _dot_exact(tri, logf) + carry[...]
        cc_ref[...] = cum * LOG2E
        carry[...] = cum[TQ - 1:TQ, :]

    return pl.pallas_call(
        body, grid=(nb,),
        in_specs=[pl.BlockSpec((TQ, 128), lambda i: (i, NZ // 128 - 1)), pl.BlockSpec((1, 128), lambda i: (0, 0))],
        out_specs=pl.BlockSpec((TQ, 128), lambda i: (i, 0)),
        out_shape=SDS((s, 128), F32), scratch_shapes=[pltpu.VMEM((1, 128), F32)],
        compiler_params=_cp("arbitrary"), name=name)(z, b_f)


def _fox_post(dcr, dcq, z, b_f, dkr, name):
    s = z.shape[0]
    nb = s // TQ

    def body(dc_ref, dcq_ref, m_ref, b_ref, dkr_ref, dz_ref, db_ref, carry):
        @pl.when(pl.program_id(0) == 0)
        def _():
            carry[...] = jnp.zeros_like(carry)
            db_ref[...] = jnp.zeros_like(db_ref)

        r = lax.broadcasted_iota(jnp.int32, (TQ, TQ), 0)
        c = lax.broadcasted_iota(jnp.int32, (TQ, TQ), 1)
        triu = jnp.where(c >= r, 1.0, 0.0).astype(F32)
        dc = jnp.concatenate([dc_ref[0], jnp.zeros((120, TQ), F32)], axis=0)
        dlogf = _dot_exact(triu, dc, (((1,), (1,)), ((), ()))) + _dot_exact(triu, dcq_ref[...]) + carry[...]
        carry[...] = dlogf[0:1, :]
        x = m_ref[...] + b_ref[...]
        lane = lax.broadcasted_iota(jnp.int32, (TQ, 128), 1)
        df = jnp.where(lane < N_HEADS, dlogf * _sigmoid(-x), 0.0)
        db_ref[...] += jnp.sum(df, axis=0, keepdims=True)
        dz_ref[...] = (df + dkr_ref[...]).astype(dz_ref.dtype)

    rv = lambda i: nb - 1 - i
    return pl.pallas_call(
        body, grid=(nb,),
        in_specs=[pl.BlockSpec((1, 8, TQ), lambda i: (rv(i), 0, 0)), pl.BlockSpec((TQ, 128), lambda i: (rv(i), 0)),
                  pl.BlockSpec((TQ, 128), lambda i: (rv(i), NZ // 128 - 1)),
                  pl.BlockSpec((1, 128), lambda i: (0, 0)), pl.BlockSpec((TQ, 128), lambda i: (rv(i), 0))],
        out_specs=[pl.BlockSpec((TQ, 128), lambda i: (rv(i), 0)), pl.BlockSpec((1, 128), lambda i: (0, 0))],
        out_shape=[SDS((s, 128), _MXU), SDS((1, 128), F32)], scratch_shapes=[pltpu.VMEM((1, 128), F32)],
        compiler_params=_cp("arbitrary"), name=name)(dcr, dcq, z, b_f, dkr)


NEG = -1e30


def _causal_mask(shape, transposed=False):
    r = lax.broadcasted_iota(jnp.int32, shape, 0)
    c = lax.broadcasted_iota(jnp.int32, shape, 1)
    return (c >= r) if transposed else (r >= c)


TKV = 512


def _key_block(s):
    return min(TKV, s)


def _diag_mask(shape, off):
    r = lax.broadcasted_iota(jnp.int32, shape, 0)
    c = lax.broadcasted_iota(jnp.int32, shape, 1)
    return c + off >= r


def _head_lanes(h, dqk):
    return slice(128 * (h // 2), 128 * (h // 2) + 128) if dqk == HEAD_DIM else slice(128 * h, 128 * h + 128)


def _keep_half(x, a, axis):
    idx = lax.broadcasted_iota(jnp.int32, x.shape, axis)
    return jnp.where((idx < HEAD_DIM) if a == 0 else (idx >= HEAD_DIM), x, jnp.zeros_like(x))


def _kv_prep(z, kcol, vcol, name):
    s = z.shape[0]
    tk = _key_block(s)
    nk = s // tk

    def body(k_ref, v_ref, kb_ref, vb_ref, vt_ref):
        kb_ref[...] = k_ref[...].astype(_MXU)
        v = v_ref[...]
        vb_ref[...] = v.astype(_MXU)
        vt_ref[0] = v.T.astype(_MXU)

    blk = pl.BlockSpec((tk, GROUP), lambda i: (i, 0))
    return pl.pallas_call(
        body, grid=(nk,),
        in_specs=[pl.BlockSpec((tk, GROUP), lambda i: (i, kcol)), pl.BlockSpec((tk, GROUP), lambda i: (i, vcol))],
        out_specs=[blk, blk, pl.BlockSpec((1, GROUP, tk), lambda i: (i, 0, 0))],
        out_shape=[SDS((s, GROUP), _MXU), SDS((s, GROUP), _MXU), SDS((nk, GROUP, tk), _MXU)],
        compiler_params=_cp("parallel"), name=name)(z, z)


LOG2E = 1.4426950408889634


def _attn_fwd(q, qcol, dqk, kb, vt, scale, ck2, name, comm=None):
    s = q.shape[0]
    nq = s // TQ
    tk = _key_block(s)
    ratio = tk // TQ
    wq = N_HEADS * dqk
    bias = ck2 is not None

    def body(*refs):
        ins, (o_ref, l_ref), _, cc = _split_refs(refs, 4 if bias else 3, 2, comm)
        if bias:
            q_ref, k_ref, vt_ref, cc_ref = ins
        else:
            q_ref, k_ref, vt_ref = ins
        i = pl.program_id(0)
        _host_gather(comm, cc, i, nq)
        qts = []
        for h in range(N_HEADS):
            qt = (q_ref[:, _head_lanes(h, dqk)].astype(F32) * (scale * LOG2E)).T
            qts.append((_keep_half(qt, h % 2, 0) if dqk == HEAD_DIM else qt).astype(_MXU))

        def step(j, carry, off):
            r0 = pl.multiple_of(j * tk, tk)
            vtj = vt_ref[j]
            sts = [jnp.dot(k_ref[pl.ds(r0, tk), _head_lanes(h, dqk)], qts[h], preferred_element_type=F32)
                   for h in range(N_HEADS)]
            stats, ps = [], []
            for h in range(N_HEADS):
                m, l, _ = carry[3 * h:3 * h + 3]
                st = sts[h]
                if bias:
                    st = st - cc_ref[pl.ds(r0, tk), h:h + 1]
                if off is not None:
                    st = jnp.where(_diag_mask(st.shape, off), st, NEG)
                m_new = jnp.maximum(m, jnp.max(st, axis=0, keepdims=True))
                alpha = jnp.exp2(m - m_new)
                p = jnp.exp2(st - m_new)
                stats.append((m_new, alpha * l + jnp.sum(p, axis=0, keepdims=True), alpha))
                ps.append(p.astype(_MXU))
            out = []
            for h in range(N_HEADS):
                m_new, l, alpha = stats[h]
                acc = alpha * carry[3 * h + 2] + jnp.dot(vtj[HEAD_DIM * h:HEAD_DIM * (h + 1), :], ps[h],
                                                         preferred_element_type=F32)
                out += [m_new, l, acc]
            return tuple(out)

        init = (jnp.full((1, TQ), NEG, F32), jnp.zeros((1, TQ), F32), jnp.zeros((HEAD_DIM, TQ), F32)) * N_HEADS
        jd = i // ratio
        carry = lax.fori_loop(0, jd, functools.partial(step, off=None), init)
        carry = step(jd, carry, TQ * (i % ratio))
        l_ref[...] = jnp.zeros_like(l_ref)
        for h in range(N_HEADS):
            l_ref[0, h:h + 1, :] = carry[3 * h] + jnp.log2(carry[3 * h + 1])
        for p in range(2):
            ot = jnp.concatenate([carry[6 * p + 2] / carry[6 * p + 1], carry[6 * p + 5] / carry[6 * p + 4]], axis=0)
            o_ref[:, 128 * p:128 * (p + 1)] = ot.T
        if comm is not None:
            @pl.when(i == nq - 1)
            def _():
                comm.wait(*cc)

    rows = pl.BlockSpec((1, 8, TQ), lambda i: (i, 0, 0))
    in_specs = [pl.BlockSpec((TQ, wq), lambda i: (i, qcol)), pl.BlockSpec((s, wq), lambda i: (0, 0)),
                pl.BlockSpec((s // tk, GROUP, tk), lambda i: (0, 0, 0))]
    args = [q, kb, vt]
    if bias:
        in_specs.append(pl.BlockSpec((s, 128), lambda i: (0, 0)))
        args.append(ck2)
    out_specs = [pl.BlockSpec((TQ, GROUP), lambda i: (i, 0)), rows]
    out_shape = [SDS((s, GROUP), F32), SDS((nq, 8, TQ), F32)]
    return _call_with_comm(body, (nq,), in_specs, out_specs, out_shape, [], args, comm, ("arbitrary",), name)


def _call_with_comm(body, grid, in_specs, out_specs, out_shape, scratch, args, comm, semantics, name):
    n_out = len(out_shape)
    if comm is not None:
        in_specs, out_specs = in_specs + comm.in_specs, out_specs + comm.out_specs
        out_shape, scratch, args = out_shape + comm.out_shape, scratch + comm.scratch, list(args) + comm.arrs
    res = pl.pallas_call(body, grid=grid, in_specs=in_specs, out_specs=out_specs, out_shape=out_shape,
                         scratch_shapes=scratch, compiler_params=_cp(*semantics), name=name)(*args)
    return (*res[:n_out], list(res[n_out:]))


def _attn_bwd_prep(q, qcol, dqk, scale, o, do, name):
    s = q.shape[0]
    nq = s // TQ
    wq = N_HEADS * dqk

    def body(q_ref, o_ref, do_ref, qt_ref, dot_ref, dl_ref):
        qt_ref[0] = (q_ref[...].astype(F32) * (scale * LOG2E)).T.astype(_MXU)
        dov = do_ref[...]
        dot_ref[0] = dov.T.astype(_MXU)
        pt = (dov * o_ref[...]).T
        dl_ref[...] = jnp.zeros_like(dl_ref)
        for h in range(N_HEADS):
            dl_ref[0, h:h + 1, :] = jnp.sum(pt[HEAD_DIM * h:HEAD_DIM * (h + 1), :], axis=0, keepdims=True)

    nat = lambda w: pl.BlockSpec((TQ, w), lambda i: (i, 0))
    tr = lambda w: pl.BlockSpec((1, w, TQ), lambda i: (i, 0, 0))
    return pl.pallas_call(
        body, grid=(nq,),
        in_specs=[pl.BlockSpec((TQ, wq), lambda i: (i, qcol)), nat(GROUP), nat(GROUP)],
        out_specs=[tr(wq), tr(GROUP), tr(8)],
        out_shape=[SDS((nq, wq, TQ), _MXU), SDS((nq, GROUP, TQ), _MXU), SDS((nq, 8, TQ), F32)],
        compiler_params=_cp("parallel"), name=name)(q, o, do)


def _attn_bwd(kb, vb, qt, dot, lse, dl, dqk, scale, ck2, name, kv_dtype, comm=None):
    s = kb.shape[0]
    nq = s // TQ
    tk = _key_block(s)
    ratio = tk // TQ
    nkb = s // tk
    wq = N_HEADS * dqk
    bias = ck2 is not None

    def body(*refs):
        ins, outs, _, cc = _split_refs(refs, 7 if bias else 6, 5 if bias else 3, comm)
        if bias:
            k_ref, v_ref, qt_ref, dot_ref, l_ref, d_ref, cc_ref = ins
            dqt_ref, dk_ref, dv_ref, dck_ref, dcq_ref = outs
        else:
            k_ref, v_ref, qt_ref, dot_ref, l_ref, d_ref = ins
            dqt_ref, dk_ref, dv_ref = outs
        j = pl.program_id(0)

        @pl.when(j == 0)
        def _():
            if comm is not None:
                comm.start(*cc)
            dqt_ref[...] = jnp.zeros_like(dqt_ref)
            if bias:
                dcq_ref[...] = jnp.zeros_like(dcq_ref)

        ks, kts, vs = [], [], []
        for h in range(N_HEADS):
            k2 = k_ref[:, _head_lanes(h, dqk)]
            if dqk == HEAD_DIM:
                k2 = _keep_half(k2, h % 2, 1)
            ks.append(k2)
            kts.append(k2.astype(F32).T.astype(_MXU))
            vs.append(_keep_half(v_ref[:, _head_lanes(h, HEAD_DIM)], h % 2, 1))
        cks = [cc_ref[:, h:h + 1] for h in range(N_HEADS)] if bias else None

        nt = (((1,), (1,)), ((), ()))

        def step(i, carry, off):
            qti, doti, li, di = qt_ref[i], dot_ref[i], l_ref[i], d_ref[i]
            qls = [_head_lanes(h, dqk) for h in range(N_HEADS)]
            vls = [_head_lanes(h, HEAD_DIM) for h in range(N_HEADS)]
            sts, dpts = [], []
            for h in range(N_HEADS):
                sts.append(jnp.dot(ks[h], qti[qls[h], :], preferred_element_type=F32))
                dpts.append(jnp.dot(vs[h], doti[vls[h], :], preferred_element_type=F32))
            pbs, dsbs, dcks = [], [], []
            for h in range(N_HEADS):
                st = sts[h] - li[h:h + 1, :]
                if bias:
                    st = st - cks[h]
                p = jnp.exp2(st)
                if off is not None:
                    p = jnp.where(_diag_mask(p.shape, off), p, 0.0)
                dst = p * (dpts[h] - di[h:h + 1, :])
                pbs.append(p.astype(_MXU))
                dsbs.append(dst.astype(_MXU))
                if bias:
                    dcks.append(carry[3 * h + 2] + jnp.sum(dst, axis=1, keepdims=True))
                    dcq_ref[i, h:h + 1, :] += jnp.sum(dst, axis=0, keepdims=True)
                else:
                    dcks.append(carry[3 * h + 2])
            out = []
            for h in range(N_HEADS):
                dvt = carry[3 * h + 1] + lax.dot_general(doti[HEAD_DIM * h:HEAD_DIM * (h + 1), :], pbs[h], nt,
                                                         preferred_element_type=F32)
                dkt = carry[3 * h] + lax.dot_general(qti[dqk * h:dqk * (h + 1), :], dsbs[h], nt, preferred_element_type=F32)
                dqt_ref[i, qls[h], :] += jnp.dot(kts[h], dsbs[h], preferred_element_type=F32) * scale
                out += [dkt, dvt, dcks[h]]
            return tuple(out)

        carry = (jnp.zeros((dqk, tk), F32), jnp.zeros((HEAD_DIM, tk), F32), jnp.zeros((tk, 1), F32)) * N_HEADS
        for r in range(ratio):
            carry = step(ratio * j + r, carry, TQ * r)
        carry = lax.fori_loop(ratio * (j + 1), nq, functools.partial(step, off=None), carry)
        for p in range(2):
            dv_ref[:, 128 * p:128 * (p + 1)] = jnp.concatenate([carry[6 * p + 1], carry[6 * p + 4]], axis=0).T.astype(dv_ref.dtype)
            if dqk == HEAD_DIM:
                dk_ref[:, 128 * p:128 * (p + 1)] = (jnp.concatenate([carry[6 * p], carry[6 * p + 3]], axis=0).T
                                                    * (1.0 / LOG2E)).astype(dk_ref.dtype)
        if dqk != HEAD_DIM:
            for h in range(N_HEADS):
                dk_ref[:, 128 * h:128 * (h + 1)] = (carry[3 * h].T * (1.0 / LOG2E)).astype(dk_ref.dtype)
        if bias:
            dck_ref[...] = jnp.zeros_like(dck_ref)
            for h in range(N_HEADS):
                dck_ref[:, h:h + 1] = -carry[3 * h + 2]
        if comm is not None:
            @pl.when(j == nkb - 1)
            def _():
                comm.wait(*cc)

    blk = lambda w: pl.BlockSpec((tk, w), lambda j: (j, 0))
    full3 = lambda w: pl.BlockSpec((nq, w, TQ), lambda j: (0, 0, 0))
    in_specs = [blk(wq), blk(GROUP), full3(wq), full3(GROUP), full3(8), full3(8)]
    args = [kb, vb, qt, dot, lse, dl]
    out_specs = [full3(wq), blk(wq), blk(GROUP)]
    out_shape = [SDS((nq, wq, TQ), F32), SDS((s, wq), kv_dtype), SDS((s, GROUP), kv_dtype)]
    if bias:
        in_specs.append(blk(128))
        args.append(ck2)
        out_specs += [blk(128), full3(8)]
        out_shape += [SDS((s, 128), F32), SDS((nq, 8, TQ), F32)]
    return _call_with_comm(body, (nkb,), in_specs, out_specs, out_shape, [], args, comm, ("arbitrary",), name)


def _untranspose(xt, dtype, name):
    nq, w, _ = xt.shape

    def body(x_ref, o_ref):
        o_ref[...] = x_ref[0].T.astype(o_ref.dtype)

    return pl.pallas_call(
        body, grid=(nq,), in_specs=[pl.BlockSpec((1, w, TQ), lambda i: (i, 0, 0))],
        out_specs=pl.BlockSpec((TQ, w), lambda i: (i, 0)), out_shape=SDS((nq * TQ, w), dtype),
        compiler_params=_cp("parallel"), name=name)(xt)


_SCALE_D = (64 + 32) ** -0.5
_COL_CQ, _COL_CKV, _COL_MISC = 2304 // 256, 2560 // 128, 2688 // 128


def _mla_prep(z, gq, gkv, wq, wk, wv, tb, name):
    s = z.shape[0]
    tm = _key_block(s)
    row = lambda w, c: pl.BlockSpec((tm, w), lambda i, c=c: (i, c))
    const = lambda a: pl.BlockSpec(a.shape, lambda i: (0,) * a.ndim)

    def body(cq_ref, ckv_ref, m_ref, gq_ref, gkv_ref, wq_ref, wk_ref, wv_ref, e_ref, qc_ref, qs_ref, kc_ref, ks_ref,
             q_ref, k_ref, v_ref, vt_ref, cqn_ref, ckvn_ref):
        cqn = _rms(cq_ref[...], gq_ref[...]).astype(_MXU)
        ckvn = _rms(ckv_ref[...], gkv_ref[...]).astype(_MXU)
        cqn_ref[...] = cqn
        ckvn_ref[...] = ckvn
        q_ref[...] = _rope(_dot(cqn, wq_ref[...]), _lanes(qc_ref[...], 512), _lanes(qs_ref[...], 512), 16).astype(q_ref.dtype)
        kr = _rope(m_ref[...], kc_ref[...], ks_ref[...], 16)
        k_ref[...] = (_dot(ckvn, wk_ref[...]) + _dot(kr, e_ref[...])).astype(k_ref.dtype)
        v = _dot(ckvn, wv_ref[...])
        v_ref[...] = v.astype(v_ref.dtype)
        vt_ref[0] = v.T.astype(vt_ref.dtype)

    e = tb["place"]
    return pl.pallas_call(
        body, grid=(s // tm,),
        in_specs=[row(256, _COL_CQ), row(128, _COL_CKV), row(128, _COL_MISC), const(gq), const(gkv), const(wq), const(wk),
                  const(wv), const(e), row(128, 0), row(128, 0), row(128, 0), row(128, 0)],
        out_specs=[row(512, 0), row(512, 0), row(256, 0), pl.BlockSpec((1, GROUP, tm), lambda i: (i, 0, 0)), row(256, 0),
                   row(128, 0)],
        out_shape=[SDS((s, 512), _MXU), SDS((s, 512), _MXU), SDS((s, 256), _MXU), SDS((s // tm, GROUP, tm), _MXU),
                   SDS((s, 256), _MXU), SDS((s, 128), _MXU)],
        compiler_params=_cp("parallel"), name=name)(
            z, z, z, gq, gkv, wq, wk, wv, e, tb["q_cos"], tb["q_sin"], tb["k_cos"], tb["k_sin"])


def _mla_prep_bwd(dq, dk, dv, z, cqn, ckvn, gq, gkv, wq, wk, wv, tb, name):
    s = z.shape[0]
    tm = min(512, s)
    row = lambda w, c: pl.BlockSpec((tm, w), lambda i, c=c: (i, c))
    const = lambda a: pl.BlockSpec(a.shape, lambda i: (0,) * a.ndim)
    acc = lambda shape: pl.BlockSpec(shape, lambda i: (0, 0))

    def body(dq_ref, dk_ref, dv_ref, cq_ref, ckv_ref, cqn_ref, ckvn_ref, gq_ref, gkv_ref, wq_ref, wk_ref, wv_ref, e_ref,
             qc_ref, qs_ref, kc_ref, ks_ref, dcq_ref, dckv_ref, dkr_ref, dwq_ref, dwk_ref, dwv_ref, dgq_ref, dgkv_ref):
        @pl.when(pl.program_id(0) == 0)
        def _():
            for r in (dwq_ref, dwk_ref, dwv_ref, dgq_ref, dgkv_ref):
                r[...] = jnp.zeros_like(r)

        dqp = _rope_bwd(dq_ref[...], _lanes(qc_ref[...], 512), _lanes(qs_ref[...], 512), 16)
        dkd = dk_ref[...]
        dvd = dv_ref[...]
        dwq_ref[...] += _dot_tn(cqn_ref[...], dqp)
        dwk_ref[...] += _dot_tn(ckvn_ref[...], dkd)
        dwv_ref[...] += _dot_tn(ckvn_ref[...], dvd)
        dcq, dgq = _rms_bwd(cq_ref[...], gq_ref[...], _dot_nt(dqp, wq_ref[...]))
        dckv, dgkv = _rms_bwd(ckv_ref[...], gkv_ref[...], _dot_nt(dkd, wk_ref[...]) + _dot_nt(dvd, wv_ref[...]))
        dcq_ref[...] = dcq.astype(dcq_ref.dtype)
        dckv_ref[...] = dckv.astype(dckv_ref.dtype)
        dgq_ref[...] += dgq
        dgkv_ref[...] += dgkv
        dkr = _dot_exact(dkd, e_ref[...], (((1,), (1,)), ((), ())))
        dkr_ref[...] = _rope_bwd(dkr, kc_ref[...], ks_ref[...], 16)

    e = tb["place"]
    return pl.pallas_call(
        body, grid=(s // tm,),
        in_specs=[row(512, 0), row(512, 0), row(256, 0), row(256, _COL_CQ), row(128, _COL_CKV), row(256, 0), row(128, 0),
                  const(gq), const(gkv), const(wq), const(wk), const(wv), const(e), row(128, 0), row(128, 0), row(128, 0), row(128, 0)],
        out_specs=[row(256, 0), row(128, 0), row(128, 0), acc((256, 512)), acc((128, 512)), acc((128, 256)), acc((1, 256)),
                   acc((1, 128))],
        out_shape=[SDS((s, 256), _MXU), SDS((s, 128), _MXU), SDS((s, 128), F32), SDS((256, 512), F32), SDS((128, 512), F32),
                   SDS((128, 256), F32), SDS((1, 256), F32), SDS((1, 128), F32)],
        compiler_params=_cp("arbitrary"), name=name)(
            dq, dk, dv, z, z, cqn, ckvn, gq, gkv, wq, wk, wv, e, tb["q_cos"], tb["q_sin"], tb["k_cos"], tb["k_sin"])


def _out_proj(ys, g, w, x, name):
    s, d = x.shape
    tm = min(512, s)

    def body(ya, yb, yc, yd, g_ref, w_ref, x_ref, o_ref, yn_ref):
        acc = x_ref[...]
        for i, y_ref in enumerate((ya, yb, yc, yd)):
            sl = slice(GROUP * i, GROUP * (i + 1))
            yn = _rms(y_ref[...], g_ref[:, sl]).astype(_MXU)
            yn_ref[:, sl] = yn
            acc = acc + jnp.dot(yn, w_ref[sl, :], preferred_element_type=F32)
        o_ref[...] = acc

    yspec = pl.BlockSpec((tm, GROUP), lambda i: (i, 0))
    return pl.pallas_call(
        body, grid=(s // tm,),
        in_specs=[yspec, yspec, yspec, yspec, pl.BlockSpec((1, d), lambda i: (0, 0)), pl.BlockSpec((d, d), lambda i: (0, 0)),
                  pl.BlockSpec((tm, d), lambda i: (i, 0))],
        out_specs=[pl.BlockSpec((tm, d), lambda i: (i, 0)), pl.BlockSpec((tm, d), lambda i: (i, 0))],
        out_shape=[SDS((s, d), F32), SDS((s, d), _MXU)], compiler_params=_cp("parallel"), name=name)(*ys, g, w, x)


def _out_proj_bwd(dx, w, ys, g, name):
    s, d = dx.shape
    tm = min(512, s)

    def body(dx_ref, w_ref, ya, yb, yc, yd, g_ref, da, db, dc, dd, dg_ref):
        @pl.when(pl.program_id(0) == 0)
        def _():
            dg_ref[...] = jnp.zeros_like(dg_ref)

        dyn = _dot_nt(dx_ref[...], w_ref[...])
        outs = (da, db, dc, dd)
        for i, y_ref in enumerate((ya, yb, yc, yd)):
            sl = slice(GROUP * i, GROUP * (i + 1))
            dy, dg = _rms_bwd(y_ref[...], g_ref[:, sl], dyn[:, sl])
            outs[i][...] = dy
            dg_ref[:, sl] += dg

    yspec = pl.BlockSpec((tm, GROUP), lambda i: (i, 0))
    return pl.pallas_call(
        body, grid=(s // tm,),
        in_specs=[pl.BlockSpec((tm, d), lambda i: (i, 0)), pl.BlockSpec((d, d), lambda i: (0, 0)), yspec, yspec, yspec, yspec,
                  pl.BlockSpec((1, d), lambda i: (0, 0))],
        out_specs=[yspec, yspec, yspec, yspec, pl.BlockSpec((1, d), lambda i: (0, 0))],
        out_shape=[SDS((s, GROUP), F32)] * 4 + [SDS((1, d), F32)],
        compiler_params=_cp("arbitrary"), name=name)(dx, w, *ys, g)


FF_BLOCK = 512
FF_ROWS = 1024


def _ffn_fwd(x, g, wu, wd, name, comm=None):
    s, d = x.shape
    nj = wu.shape[0]
    tm = min(FF_ROWS, s)
    ni = s // tm

    def body(*refs):
        (x_ref, g_ref, wu_ref, wd_ref), (o_ref, u_ref, h_ref), (acc,), cc = _split_refs(refs, 4, 3, comm)
        i, j = pl.program_id(0), pl.program_id(1)
        _host_gather(comm, cc, i * nj + j, ni * nj)

        @pl.when(j == 0)
        def _():
            h_ref[...] = _rms(x_ref[...], g_ref[...]).astype(h_ref.dtype)
            acc[...] = jnp.zeros_like(acc)

        halves = [slice(r, r + tm // 2) for r in range(0, tm, tm // 2)]
        us = [jnp.dot(h_ref[r, :], wu_ref[0], preferred_element_type=F32) for r in halves]
        for r, u in zip(halves, us):
            u_ref[r, :] = u.astype(u_ref.dtype)
            acc[r, :] += _dot(jnp.square(jnp.maximum(u, 0.0)), wd_ref[...])

        @pl.when(j == nj - 1)
        def _():
            o_ref[...] = x_ref[...] + acc[...]

        if comm is not None:
            @pl.when((i == ni - 1) & (j == nj - 1))
            def _():
                comm.wait(*cc)

    in_specs = [pl.BlockSpec((tm, d), lambda i, j: (i, 0)), pl.BlockSpec((1, d), lambda i, j: (0, 0)),
                pl.BlockSpec((1, d, FF_BLOCK), lambda i, j: (j, 0, 0)), pl.BlockSpec((FF_BLOCK, d), lambda i, j: (j, 0))]
    out_specs = [pl.BlockSpec((tm, d), lambda i, j: (i, 0)), pl.BlockSpec((tm, FF_BLOCK), lambda i, j: (i, j)),
                 pl.BlockSpec((tm, d), lambda i, j: (i, 0))]
    out_shape = [SDS((s, d), F32), SDS((s, nj * FF_BLOCK), _MXU), SDS((s, d), _MXU)]
    return _call_with_comm(body, (ni, nj), in_specs, out_specs, out_shape, [pltpu.VMEM((tm, d), F32)], [x, g, wu, wd], comm,
                           ("arbitrary", "arbitrary"), name)


def _ffn_bwd(dx2, x, u, g, wu, wd, name, comm=None):
    s, d = x.shape
    nj = wu.shape[0]
    tm = min(FF_ROWS, s)
    ni = s // tm

    def body(*refs):
        (dx_ref, x_ref, u_ref, g_ref, wu_ref, wd_ref), (o_ref, du_ref, dg_ref), (acc, dxb), cc = _split_refs(refs, 6, 3, comm)
        i, j = pl.program_id(0), pl.program_id(1)

        @pl.when((i == 0) & (j == 0))
        def _():
            if comm is not None:
                comm.start(*cc)
            dg_ref[...] = jnp.zeros_like(dg_ref)

        @pl.when(j == 0)
        def _():
            dxb[...] = dx_ref[...].astype(dxb.dtype)
            acc[...] = jnp.zeros_like(acc)

        nt = (((1,), (1,)), ((), ()))
        halves = [slice(r, r + tm // 2) for r in range(0, tm, tm // 2)]
        das = [lax.dot_general(dxb[r, :], wd_ref[...], nt, preferred_element_type=F32) for r in halves]
        for r, da in zip(halves, das):
            du = (da * 2.0 * jnp.maximum(u_ref[r, :].astype(F32), 0.0)).astype(du_ref.dtype)
            du_ref[r, :] = du
            acc[r, :] += lax.dot_general(du, wu_ref[0], nt, preferred_element_type=F32)

        @pl.when(j == nj - 1)
        def _():
            dxn, dg = _rms_bwd(x_ref[...], g_ref[...], acc[...])
            o_ref[...] = dx_ref[...] + dxn
            dg_ref[...] += dg

        if comm is not None:
            @pl.when((i == ni - 1) & (j == nj - 1))
            def _():
                comm.wait(*cc)

    in_specs = [pl.BlockSpec((tm, d), lambda i, j: (i, 0)), pl.BlockSpec((tm, d), lambda i, j: (i, 0)),
                pl.BlockSpec((tm, FF_BLOCK), lambda i, j: (i, j)), pl.BlockSpec((1, d), lambda i, j: (0, 0)),
                pl.BlockSpec((1, d, FF_BLOCK), lambda i, j: (j, 0, 0)), pl.BlockSpec((FF_BLOCK, d), lambda i, j: (j, 0))]
    out_specs = [pl.BlockSpec((tm, d), lambda i, j: (i, 0)), pl.BlockSpec((tm, FF_BLOCK), lambda i, j: (i, j)),
                 pl.BlockSpec((1, d), lambda i, j: (0, 0))]
    out_shape = [SDS((s, d), F32), SDS((s, nj * FF_BLOCK), _MXU), SDS((1, d), F32)]
    return _call_with_comm(body, (ni, nj), in_specs, out_specs, out_shape,
                           [pltpu.VMEM((tm, d), F32), pltpu.VMEM((tm, d), _MXU)], [dx2, x, u, g, wu, wd], comm,
                           ("arbitrary", "arbitrary"), name)


def _in_proj_bwd(dz, w, x, g, dx_up, name, comm=None):
    s, d = x.shape
    n = w.shape[1]
    tm = min(512, s)
    ni = s // tm

    def body(*refs):
        (dz_ref, w_ref, x_ref, g_ref, up_ref), (o_ref, dg_ref), _, cc = _split_refs(refs, 5, 2, comm)
        i = pl.program_id(0)

        @pl.when(i == 0)
        def _():
            if comm is not None:
                comm.start(*cc)
            dg_ref[...] = jnp.zeros_like(dg_ref)

        dh = lax.dot_general(dz_ref[...], w_ref[...], (((1,), (1,)), ((), ())), preferred_element_type=F32)
        dxn, dg = _rms_bwd(x_ref[...], g_ref[...], dh)
        o_ref[...] = up_ref[...] + dxn
        dg_ref[...] += dg
        if comm is not None:
            @pl.when(i == ni - 1)
            def _():
                comm.wait(*cc)

    in_specs = [pl.BlockSpec((tm, n), lambda i: (i, 0)), pl.BlockSpec((d, n), lambda i: (0, 0)),
                pl.BlockSpec((tm, d), lambda i: (i, 0)), pl.BlockSpec((1, d), lambda i: (0, 0)),
                pl.BlockSpec((tm, d), lambda i: (i, 0))]
    out_specs = [pl.BlockSpec((tm, d), lambda i: (i, 0)), pl.BlockSpec((1, d), lambda i: (0, 0))]
    out_shape = [SDS((s, d), F32), SDS((1, d), F32)]
    return _call_with_comm(body, (ni,), in_specs, out_specs, out_shape, [], [dz, w, x, g, dx_up], comm, ("arbitrary",), name)


def _loss_head(x, g, target, name):
    s, d = x.shape
    tm = min(512, s)

    def body(x_ref, g_ref, t_ref, l_ref, dx_ref, dg_ref):
        @pl.when(pl.program_id(0) == 0)
        def _():
            l_ref[...] = jnp.zeros_like(l_ref)
            dg_ref[...] = jnp.zeros_like(dg_ref)

        xv = x_ref[...]
        err = _rms(xv, g_ref[...]) - t_ref[...]
        l_ref[...] += jnp.sum(err * err, axis=0, keepdims=True) * (0.5 / d)
        dx, dg = _rms_bwd(xv, g_ref[...], err * (1.0 / d))
        dx_ref[...] = dx
        dg_ref[...] += dg

    return pl.pallas_call(
        body, grid=(s // tm,),
        in_specs=[pl.BlockSpec((tm, d), lambda i: (i, 0)), pl.BlockSpec((1, d), lambda i: (0, 0)),
                  pl.BlockSpec((tm, d), lambda i: (i, 0))],
        out_specs=[pl.BlockSpec((1, d), lambda i: (0, 0)), pl.BlockSpec((tm, d), lambda i: (i, 0)),
                   pl.BlockSpec((1, d), lambda i: (0, 0))],
        out_shape=[SDS((1, d), F32), SDS((s, d), F32), SDS((1, d), F32)], compiler_params=_cp("arbitrary"), name=name)(x, g, target)


def _me_and_peer():
    x, y, c = lax.axis_index("x"), lax.axis_index("y"), lax.axis_index("c")
    me = 4 * x + 2 * y + c

    def peer(k):
        px, py, pc = x ^ (k >> 2), y ^ ((k >> 1) & 1), c ^ (k & 1)
        return (px, py, pc), 4 * px + 2 * py + pc

    return me, peer


class _Comm:
    CHIPS = (2, 4, 6)

    def __init__(self, kind, arrs):
        assert kind in ("gather", "exchange")
        self.kind, self.arrs, self.n = kind, list(arrs), len(arrs)
        anyspec = pl.BlockSpec(memory_space=pl.ANY)
        self.in_specs = [anyspec] * self.n
        self.out_specs = [anyspec] * self.n
        self.out_shape = [SDS(((NDEV,) + a.shape) if kind == "gather" else a.shape, a.dtype) for a in self.arrs]
        npair = NDEV - 1 + len(self.CHIPS)
        self.scratch = [pltpu.SemaphoreType.DMA((self.n, npair)), pltpu.SemaphoreType.DMA((self.n, npair)),
                        pltpu.SemaphoreType.DMA((self.n,))]

    def _copies(self, ins, outs, sems):
        send, recv, loc = sems
        me, peer = _me_and_peer()
        gather = self.kind == "gather"
        sibling = peer(1)[0]
        local = [pltpu.make_async_copy(ins[a] if gather else ins[a].at[me], outs[a].at[me], loc.at[a]) for a in range(self.n)]
        outgoing, incoming, forwards, forwarded = [], [], [], []
        for k in ((1,) + self.CHIPS) if gather else range(1, NDEV):
            dev, pid = peer(k)
            for a in range(self.n):
                pair = dict(send_sem=send.at[a, k - 1], recv_sem=recv.at[a, k - 1], device_id=dev, device_id_type=MESH)
                outgoing.append(pltpu.make_async_remote_copy(src_ref=ins[a] if gather else ins[a].at[pid],
                                                             dst_ref=outs[a].at[me], **pair))
                incoming.append(pltpu.make_async_remote_copy(src_ref=ins[a] if gather else ins[a].at[me],
                                                             dst_ref=outs[a].at[pid], **pair))
        if gather:
            for idx, k in enumerate(self.CHIPS):
                got, theirs = peer(k)[1], peer(k + 1)[1]
                for a in range(self.n):
                    pair = dict(send_sem=send.at[a, NDEV - 1 + idx], recv_sem=recv.at[a, NDEV - 1 + idx], device_id=sibling,
                                device_id_type=MESH)
                    forwards.append(pltpu.make_async_remote_copy(src_ref=outs[a].at[got], dst_ref=outs[a].at[got], **pair))
                    forwarded.append(pltpu.make_async_remote_copy(src_ref=outs[a].at[theirs], dst_ref=outs[a].at[theirs], **pair))
        return local, outgoing, incoming, forwards, forwarded

    def start(self, ins, outs, sems):
        local, outgoing, _, _, _ = self._copies(ins, outs, sems)
        for cp in local + outgoing:
            cp.start()

    def forward(self, ins, outs, sems):
        _, _, incoming, forwards, _ = self._copies(ins, outs, sems)
        per = self.n
        for idx in range(len(forwards) // per if per else 0):
            for a in range(per):
                incoming[(1 + idx) * per + a].wait_recv()
                forwards[idx * per + a].start()

    def wait(self, ins, outs, sems):
        local, outgoing, incoming, forwards, forwarded = self._copies(ins, outs, sems)
        for cp in (incoming[:self.n] if self.kind == "gather" else incoming) + forwarded:
            cp.wait_recv()
        for cp in outgoing + forwards:
            cp.wait_send()
        for cp in local:
            cp.wait()


def _host_gather(comm, cc, step, nsteps):
    if comm is None:
        return

    @pl.when(step == 0)
    def _():
        comm.start(*cc)

    @pl.when(step == (2 * nsteps) // 3)
    def _():
        comm.forward(*cc)


def _split_refs(refs, n_in, n_out, comm):
    c = comm.n if comm is not None else 0
    ins, cin = refs[:n_in], refs[n_in:n_in + c]
    outs, cout = refs[n_in + c:n_in + c + n_out], refs[n_in + c + n_out:n_in + 2 * c + n_out]
    rest = refs[n_in + 2 * c + n_out:]
    scratch, csem = (rest[:len(rest) - 3], rest[len(rest) - 3:]) if c else (rest, ())
    return ins, outs, scratch, (cin, cout, csem)


def _comm_call(kind, arrs, name):
    comm = _Comm(kind, arrs)

    def body(*refs):
        _, _, _, c = _split_refs(refs, 0, 0, comm)
        comm.start(*c)
        if kind == "gather":
            comm.forward(*c)
        comm.wait(*c)

    return pl.pallas_call(body, in_specs=comm.in_specs, out_specs=comm.out_specs, out_shape=comm.out_shape,
                          scratch_shapes=comm.scratch, compiler_params=pltpu.CompilerParams(has_side_effects=True),
                          name=name)(*arrs)


def _all_gather(arrs, name):
    return _comm_call("gather", arrs, name)


def _exchange(arrs, name):
    return _comm_call("exchange", arrs, name)


def _sum_slots(parts, name):
    _, r, c = parts.shape
    tr = r if r <= 512 else 512

    def body(p_ref, o_ref):
        acc = p_ref[0].astype(F32)
        for q in range(1, NDEV):
            acc = acc + p_ref[q].astype(F32)
        o_ref[...] = acc

    return pl.pallas_call(
        body, grid=(r // tr,), in_specs=[pl.BlockSpec((NDEV, tr, c), lambda i: (0, i, 0))],
        out_specs=pl.BlockSpec((tr, c), lambda i: (i, 0)), out_shape=SDS((r, c), F32),
        compiler_params=_cp("parallel"), name=name)(parts)


def _adamw(g, w, m, v, name):
    r, c = w.shape
    parts = g.ndim == 3
    tr = r
    for cand in (512, 256, 128, 64, 32, 16, 8):
        if r > cand and r % cand == 0 and cand * c * 4 <= 2 * 1024 * 1024:
            tr = cand
            break
    bc1 = 1.0 / (1.0 - ADAM_B1 ** ADAM_STEP)
    bc2 = 1.0 / (1.0 - ADAM_B2 ** ADAM_STEP)

    def body(g_ref, w_ref, m_ref, v_ref, go_ref, d_ref, mo_ref, vo_ref):
        if parts:
            gv = g_ref[0].astype(F32)
            for q in range(1, NDEV):
                gv = gv + g_ref[q].astype(F32)
        else:
            gv = g_ref[...]
        mn = ADAM_B1 * m_ref[...] + (1.0 - ADAM_B1) * gv
        vn = ADAM_B2 * v_ref[...] + (1.0 - ADAM_B2) * (gv * gv)
        go_ref[...] = gv
        mo_ref[...] = mn
        vo_ref[...] = vn
        d_ref[...] = -ADAM_LR * ((mn * bc1) / (jnp.sqrt(vn * bc2) + ADAM_EPS) + ADAM_WD * w_ref[...])

    spec = pl.BlockSpec((tr, c), lambda i: (i, 0))
    gspec = pl.BlockSpec((NDEV, tr, c), lambda i: (0, i, 0)) if parts else spec
    return pl.pallas_call(
        body, grid=(r // tr,), in_specs=[gspec, spec, spec, spec], out_specs=[spec] * 4,
        out_shape=[SDS((r, c), F32)] * 4, compiler_params=_cp("parallel"), name=name)(g, w, m, v)


def _adamw_layer(parts, w, m, v, l, prev, name):
    depth, r, c = w.shape
    tr = next(t for t in (512, 256, 128, 64, 32, 16, 8) if r % t == 0 and t * c * 4 <= 2 * 1024 * 1024)
    bc1 = 1.0 / (1.0 - ADAM_B1 ** ADAM_STEP)
    bc2 = 1.0 / (1.0 - ADAM_B2 ** ADAM_STEP)

    def body(g_ref, w_ref, m_ref, v_ref, *rest):
        go_ref, d_ref, mo_ref, vo_ref = rest[-4:]
        gv = g_ref[0].astype(F32)
        for q in range(1, NDEV):
            gv = gv + g_ref[q].astype(F32)
        mn = ADAM_B1 * m_ref[0] + (1.0 - ADAM_B1) * gv
        vn = ADAM_B2 * v_ref[0] + (1.0 - ADAM_B2) * (gv * gv)
        go_ref[0] = gv
        mo_ref[0] = mn
        vo_ref[0] = vn
        d_ref[0] = -ADAM_LR * ((mn * bc1) / (jnp.sqrt(vn * bc2) + ADAM_EPS) + ADAM_WD * w_ref[0])

    spec = pl.BlockSpec((1, tr, c), lambda i: (l, i, 0))
    in_specs = [pl.BlockSpec((NDEV, tr, c), lambda i: (0, i, 0)), spec, spec, spec]
    args = [parts, w, m, v]
    aliases = {}
    if prev is not None:
        in_specs += [pl.BlockSpec(memory_space=pl.ANY)] * 4
        args += list(prev)
        aliases = {4 + k: k for k in range(4)}
    return pl.pallas_call(
        body, grid=(r // tr,), in_specs=in_specs, out_specs=[spec] * 4, out_shape=[SDS((depth, r, c), F32)] * 4,
        input_output_aliases=aliases, compiler_params=_cp("parallel"), name=name)(*args)


def _pad_in_cols(w):
    r = w.shape[0]
    zeros = lambda n: jnp.zeros((r, n), w.dtype)
    return jnp.concatenate([w[:, :2304], w[:, 2308:2692], w[:, 2304:2308], zeros(28), w[:, 2692:2724], zeros(64)], axis=1)


def _unpad_in_cols(w):
    return jnp.concatenate([w[..., :2304], w[..., 2688:2692], w[..., 2304:2688], w[..., 2720:2752]], axis=-1)


def _pad_uq(w):
    return jnp.pad(w.reshape(256, N_HEADS, 96), ((0, 0), (0, 0), (0, 32))).reshape(256, 512)


def _unpad_uq(w):
    return w.reshape(256, N_HEADS, 128)[:, :, :96].reshape(256, 384)


def _split_ukv(w):
    r = w.reshape(128, N_HEADS, 128)
    return jnp.pad(r[:, :, :64], ((0, 0), (0, 0), (0, 64))).reshape(128, 512), r[:, :, 64:].reshape(128, 256)


def _join_ukv(dk, dv):
    return jnp.concatenate([dk.reshape(128, N_HEADS, 128)[:, :, :64], dv.reshape(128, N_HEADS, 64)], axis=-1).reshape(128, 512)


def _cols_to_full(g):
    return jnp.transpose(g, (1, 0, 2)).reshape(g.shape[1], NDEV * g.shape[2])


def kernel(x, g_mix_norm, w_in, b_forget, g_sgu, w_spatial, b_spatial, g_mla_q, w_uq, g_mla_kv, w_ukv, g_group_out, w_out, g_ffn_norm, w_up, w_down, g_final, loss_target, m_g_mix_norm, m_w_in, m_b_forget, m_g_sgu, m_w_spatial, m_b_spatial, m_g_mla_q, m_w_uq, m_g_mla_kv, m_w_ukv, m_g_group_out, m_w_out, m_g_ffn_norm, m_w_up, m_w_down, m_g_final, v_g_mix_norm, v_w_in, v_b_forget, v_g_sgu, v_w_spatial, v_b_spatial, v_g_mla_q, v_w_uq, v_g_mla_kv, v_w_ukv, v_g_group_out, v_w_out, v_g_ffn_norm, v_w_up, v_w_down, v_g_final):
    depth = w_in.shape[0]
    s, d = x.shape[1], x.shape[2]
    x0 = x.reshape(s, d)
    target = loss_target.reshape(s, d)
    tb = _tables(s)
    me = 4 * lax.axis_index("x") + 2 * lax.axis_index("y") + lax.axis_index("c")

    assert depth == 2
    shards = {}
    for l in range(depth):
        shards.update({(l, "w_in"): _pad_in_cols(w_in[l]).astype(_WIRE), (l, "w_out"): w_out[l].astype(_WIRE),
                       (l, "w_up"): w_up[l].astype(_WIRE), (l, "w_down"): w_down[l].astype(_WIRE),
                       (l, "w_uq"): w_uq[l].astype(_WIRE), (l, "w_ukv"): w_ukv[l].astype(_WIRE)})
    wts = _ShardedWeights(shards)
    wts.full[(0, "w_in")] = _all_gather([shards[(0, "w_in")]], "gather_w_in0")[0]

    row = lambda a: a.reshape(1, -1)

    def small(l):
        bf = jnp.pad(b_forget[l].reshape(1, N_HEADS), ((0, 0), (0, 128 - N_HEADS)))
        bt = jnp.pad(b_spatial[l].T, ((0, 0), (0, 128 - N_HEADS)))
        return dict(g_mix=row(g_mix_norm[l]), g_sgu=row(g_sgu[l]), w_s=w_spatial[l], b_t=bt, b_f=bf, gq=row(g_mla_q[l]),
                    gkv=row(g_mla_kv[l]), g_go=row(g_group_out[l]), g_ffn=row(g_ffn_norm[l]))

    smalls = [small(l) for l in range(depth)]
    lrow, dx, sm, dg_final = _local_step(x0, target, wts, smalls, row(g_final), tb)
    loss = lax.psum(jnp.sum(lrow), AXES)
    grad_x = dx.reshape(1, s, d)
    return _reduce_and_update(loss, grad_x, wts.recv, sm, dg_final, me, dict(
        g_mix_norm=(g_mix_norm, m_g_mix_norm, v_g_mix_norm), w_in=(w_in, m_w_in, v_w_in),
        b_forget=(b_forget, m_b_forget, v_b_forget), g_sgu=(g_sgu, m_g_sgu, v_g_sgu),
        w_spatial=(w_spatial, m_w_spatial, v_w_spatial), b_spatial=(b_spatial, m_b_spatial, v_b_spatial),
        g_mla_q=(g_mla_q, m_g_mla_q, v_g_mla_q), w_uq=(w_uq, m_w_uq, v_w_uq), g_mla_kv=(g_mla_kv, m_g_mla_kv, v_g_mla_kv),
        w_ukv=(w_ukv, m_w_ukv, v_w_ukv), g_group_out=(g_group_out, m_g_group_out, v_g_group_out),
        w_out=(w_out, m_w_out, v_w_out), g_ffn_norm=(g_ffn_norm, m_g_ffn_norm, v_g_ffn_norm), w_up=(w_up, m_w_up, v_w_up),
        w_down=(w_down, m_w_down, v_w_down), g_final=(g_final, m_g_final, v_g_final)))


_GATHER_AT = {
    "in_proj0": [(0, "w_up")],
    "fox_attn0": [(0, "w_uq"), (0, "w_ukv"), (0, "w_down")],
    "mla_attn0": [(0, "w_out"), (1, "w_in")],
    "ffn_fwd0": [(1, "w_uq"), (1, "w_ukv"), (1, "w_down")],
    "fox_attn1": [(1, "w_out")],
    "mla_attn1": [(1, "w_up")],
}
_SCATTER_AT = {
    "fox_attn_bwd1": [(1, "w_down")],
    "mla_attn_bwd1": [(1, "w_up"), (1, "w_out")],
    "ffn_bwd0": [(1, "w_in")],
    "fox_attn_bwd0": [(0, "w_down")],
    "mla_attn_bwd0": [(0, "w_up"), (0, "w_out")],
    "in_proj_bwd0": [(0, "w_in")],
}


class _FullWeights:
    def __init__(self, per_layer):
        self.per_layer, self.grads = per_layer, {}

    def get(self, l, name):
        return self.per_layer[l][name]

    def comm(self, host):
        return None

    def done(self, host, results):
        pass

    def grad(self, l, name, blocks):
        self.grads[(l, name)] = blocks


class _ShardedWeights(_FullWeights):
    def __init__(self, shards):
        self.shards, self.full, self.grads, self.recv = shards, {}, {}, {}

    def get(self, l, name):
        if name in ("wk", "wv"):
            return _split_ukv(_cols_to_full(self.full[(l, "w_ukv")]))[0 if name == "wk" else 1]
        if name == "wq":
            return _pad_uq(_cols_to_full(self.full[(l, "w_uq")]))
        g = self.full[(l, name)]
        return g if name == "w_up" else g.reshape(NDEV * g.shape[1], g.shape[2])

    def comm(self, host):
        if host in _GATHER_AT:
            return _Comm("gather", [self.shards[k] for k in _GATHER_AT[host]])
        if host in _SCATTER_AT:
            return _Comm("exchange", [self.grads[k] for k in _SCATTER_AT[host]])
        return None

    def done(self, host, results):
        if host in _GATHER_AT:
            self.full.update(zip(_GATHER_AT[host], results))
        if host in _SCATTER_AT:
            self.recv.update(zip(_SCATTER_AT[host], results))


def _local_step(x0, target, wts, smalls, g_final, tb):
    depth = len(smalls)
    s, d = x0.shape
    saved = []
    xl = x0
    for l in range(depth):
        p = smalls[l]
        z, h, got = _norm_matmul(xl, p["g_mix"], wts.get(l, "w_in"), f"in_proj{l}", wts.comm(f"in_proj{l}"))
        wts.done(f"in_proj{l}", got)
        ya = _sgu_fwd(z, p["g_sgu"], p["w_s"], p["b_t"], f"sgu_fwd{l}")
        yb, ret, states = _ret_fwd(z, tb, f"ret_fwd{l}")
        cum = _fox_prep(z, p["b_f"], f"fox_prep{l}")
        kc, vc, vtc = _kv_prep(z, 7, 8, f"fox_kv{l}")
        yc, lse_c, got = _attn_fwd(z, 6, HEAD_DIM, kc, vtc, HEAD_DIM ** -0.5, cum, f"fox_attn{l}", wts.comm(f"fox_attn{l}"))
        wts.done(f"fox_attn{l}", got)
        wq, wk, wv = wts.get(l, "wq"), wts.get(l, "wk"), wts.get(l, "wv")
        qd, kd, vd, vtd, cqn, ckvn = _mla_prep(z, p["gq"], p["gkv"], wq, wk, wv, tb, f"mla_prep{l}")
        yd, lse_d, got = _attn_fwd(qd, 0, 128, kd, vtd, _SCALE_D, None, f"mla_attn{l}", wts.comm(f"mla_attn{l}"))
        wts.done(f"mla_attn{l}", got)
        ys = (ya, yb, yc, yd)
        x1, yn = _out_proj(ys, p["g_go"], wts.get(l, "w_out"), xl, f"out_proj{l}")
        x2, u, h2, got = _ffn_fwd(x1, p["g_ffn"], wts.get(l, "w_up"), wts.get(l, "w_down"), f"ffn_fwd{l}", wts.comm(f"ffn_fwd{l}"))
        wts.done(f"ffn_fwd{l}", got)
        saved.append(dict(x=xl, z=z, h=h, ys=ys, ret=ret, states=states, cum=cum, lse_c=lse_c, kc=kc, vc=vc, qd=qd, kd=kd, vd=vd,
                          cqn=cqn, ckvn=ckvn, lse_d=lse_d, x1=x1, yn=yn, u=u, h2=h2, wq=wq, wk=wk, wv=wv))
        xl = x2

    lrow, dx, dg_final = _loss_head(xl, g_final, target, "loss_head")

    sm = [None] * depth
    for l in reversed(range(depth)):
        p, a = smalls[l], saved[l]
        dx1, du, dg_ffn, got = _ffn_bwd(dx, a["x1"], a["u"], p["g_ffn"], wts.get(l, "w_up"), wts.get(l, "w_down"), f"ffn_bwd{l}",
                                        wts.comm(f"ffn_bwd{l}"))
        wts.done(f"ffn_bwd{l}", got)
        dw_down = _mm_tn(a["u"], dx, f"dw_down{l}", a_fn=lambda t: jnp.square(jnp.maximum(t, 0.0)), out_dtype=_WIRE)
        wts.grad(l, "w_down", dw_down.reshape(NDEV, dw_down.shape[0] // NDEV, d))
        wts.grad(l, "w_up", _mm_tn(a["h2"], du, f"dw_up{l}", blocked=True, out_dtype=_WIRE))
        dya, dyb, dyc, dyd, dg_go = _out_proj_bwd(dx1, wts.get(l, "w_out"), a["ys"], p["g_go"], f"out_proj_bwd{l}")
        wts.grad(l, "w_out", _mm_tn(a["yn"], dx1, f"dw_out{l}", out_dtype=_WIRE).reshape(NDEV, d // NDEV, d))
        dz_a, dg_sgu, dw_s, db_t = _sgu_bwd(dya, a["z"], p["g_sgu"], p["w_s"], p["b_t"], f"sgu_bwd{l}")
        dz_b = _ret_bwd(dyb, a["z"], a["ret"], a["states"], tb, f"ret_bwd{l}")
        qt, dot, dl = _attn_bwd_prep(a["z"], 6, HEAD_DIM, HEAD_DIM ** -0.5, a["ys"][2], dyc, f"fox_bwd_prep{l}")
        dqt_c, dk_c, dv_c, dck, dcq, got = _attn_bwd(a["kc"], a["vc"], qt, dot, a["lse_c"], dl, HEAD_DIM,
                                                     HEAD_DIM ** -0.5, a["cum"], f"fox_attn_bwd{l}", _MXU,
                                                     wts.comm(f"fox_attn_bwd{l}"))
        wts.done(f"fox_attn_bwd{l}", got)
        dq_c = _untranspose(dqt_c, _MXU, f"fox_dq{l}")
        qt, dot, dl = _attn_bwd_prep(a["qd"], 0, 128, _SCALE_D, a["ys"][3], dyd, f"mla_bwd_prep{l}")
        dqt_d, dk_d, dv_d, got = _attn_bwd(a["kd"], a["vd"], qt, dot, a["lse_d"], dl, 128, _SCALE_D, None,
                                           f"mla_attn_bwd{l}", F32, wts.comm(f"mla_attn_bwd{l}"))
        wts.done(f"mla_attn_bwd{l}", got)
        dq_d = _untranspose(dqt_d, F32, f"mla_dq{l}")
        dz_cq, dz_ckv, dkr, dwq, dwk, dwv, dgq, dgkv = _mla_prep_bwd(dq_d, dk_d, dv_d, a["z"], a["cqn"], a["ckvn"], p["gq"],
                                                                     p["gkv"], a["wq"], a["wk"], a["wv"], tb, f"mla_prep_bwd{l}")
        dz_misc, db_f = _fox_post(dcq, dck, a["z"], p["b_f"], dkr, f"fox_post{l}")
        dz = jnp.concatenate([dz_a, dz_b, dq_c, dk_c, dv_c, dz_cq, dz_ckv, dz_misc], axis=1)
        wts.grad(l, "w_in", _unpad_in_cols(_mm_tn(a["h"], dz, f"dw_in{l}", out_dtype=_WIRE)).reshape(NDEV, d // NDEV, N_IN))
        dx, dg_mix, got = _in_proj_bwd(dz, wts.get(l, "w_in"), a["x"], p["g_mix"], dx1, f"in_proj_bwd{l}",
                                       wts.comm(f"in_proj_bwd{l}"))
        wts.done(f"in_proj_bwd{l}", got)
        sm[l] = [dg_mix, dg_go, dg_ffn, dg_sgu, dw_s, db_t[:, :N_HEADS].T, db_f[0, :N_HEADS], dgq, dgkv, _unpad_uq(dwq),
                 _join_ukv(dwk, dwv)]
    return lrow, dx, sm, dg_final


def _reduce_and_update(loss, grad_x, recv, sm, dg_final, me, given):
    depth = len(sm)
    pieces = [t for l in range(depth) for t in sm[l]] + [dg_final]
    flat = jnp.concatenate([t.reshape(-1) for t in pieces])
    n_flat = flat.shape[0]
    unit = NDEV * 8 * 128
    n_pad = -(-n_flat // unit) * unit
    packed = jnp.pad(flat, (0, n_pad - n_flat)).reshape(NDEV, n_pad // (NDEV * 128), 128)
    red = _sum_slots(_exchange([packed], "scatter_small")[0], "sum_small")
    full = _all_gather([red], "gather_small")[0].reshape(-1)
    offs = np.cumsum([0] + [int(np.prod(t.shape)) for t in pieces])
    red_pieces = [full[int(offs[i]):int(offs[i + 1])].reshape(pieces[i].shape) for i in range(len(pieces))]
    per = len(sm[0])
    stack = lambda i: jnp.stack([red_pieces[l * per + i] for l in range(depth)])
    g_small = dict(g_mix_norm=stack(0), g_group_out=stack(1), g_ffn_norm=stack(2), g_sgu=stack(3), w_spatial=stack(4),
                   b_spatial=stack(5), b_forget=stack(6), g_mla_q=stack(7), g_mla_kv=stack(8), g_final=red_pieces[-1])
    cq, ckv = given["w_uq"][0].shape[2], given["w_ukv"][0].shape[2]
    g_small["w_uq"] = lax.dynamic_slice_in_dim(stack(9), me * cq, cq, axis=2)
    g_small["w_ukv"] = lax.dynamic_slice_in_dim(stack(10), me * ckv, ckv, axis=2)

    names = list(given)
    outs = {}
    for nme in names:
        wv_, mv_, vv_ = given[nme]
        shape = wv_.shape
        if nme in ("w_in", "w_out", "w_up", "w_down"):
            res = None
            for l in range(depth):
                res = _adamw_layer(recv[(l, nme)], wv_, mv_, vv_, l, res, f"adamw_{nme}{l}")
            outs[nme] = list(res)
        else:
            two = lambda t: t.reshape(-1, shape[-1]) if t.ndim > 1 else t.reshape(1, -1)
            res = _adamw(two(g_small[nme]), two(wv_), two(mv_), two(vv_), f"adamw_{nme}")
            outs[nme] = [r.reshape(shape) for r in res]
    return (loss, grad_x, *[outs[n][0] for n in names], *[outs[n][1] for n in names], *[outs[n][2] for n in names],
            *[outs[n][3] for n in names])
```

```python
import functools

import jax
import jax.numpy as jnp
import numpy as np
from jax import lax
from jax.experimental import pallas as pl
from jax.experimental.pallas import tpu as pltpu

F32 = jnp.float32
_MXU = jnp.bfloat16
_WIRE = jnp.bfloat16
EPS = 1e-6
NDEV = 8
AXES = ("x", "y", "c")
MESH = pl.DeviceIdType.MESH

N_HEADS = 4
HEAD_DIM = 64
GROUP = 256
CHUNK = 128
NZ = 2816
N_IN = 2724
MISC_F, MISC_KR = 0, 32
VMEM_LIMIT = 56 * 1024 * 1024

ADAM_LR, ADAM_B1, ADAM_B2, ADAM_EPS, ADAM_WD, ADAM_STEP = 0.001, 0.9, 0.999, 1e-08, 0.01, 10

SDS = jax.ShapeDtypeStruct


def _cp(*sem):
    return pltpu.CompilerParams(dimension_semantics=sem, vmem_limit_bytes=VMEM_LIMIT)


def _dot(a, b):
    return jnp.dot(a.astype(_MXU), b.astype(_MXU), preferred_element_type=F32)


def _dot_nt(a, b):
    return lax.dot_general(a.astype(_MXU), b.astype(_MXU), (((1,), (1,)), ((), ())), preferred_element_type=F32)


def _dot_tn(a, b):
    return lax.dot_general(a.astype(_MXU), b.astype(_MXU), (((0,), (0,)), ((), ())), preferred_element_type=F32)


def _dot_exact(a, b, dims=(((1,), (0,)), ((), ()))):
    return lax.dot_general(a, b, dims, precision=lax.Precision.HIGHEST, preferred_element_type=F32)


def _rms(x, g):
    return x * lax.rsqrt(jnp.mean(x * x, axis=-1, keepdims=True) + EPS) * g


def _rms_bwd(x, g, dy):
    xh = x * lax.rsqrt(jnp.mean(x * x, axis=-1, keepdims=True) + EPS)
    dxh = dy * g
    r = lax.rsqrt(jnp.mean(x * x, axis=-1, keepdims=True) + EPS)
    dx = r * (dxh - xh * jnp.mean(dxh * xh, axis=-1, keepdims=True))
    return dx, jnp.sum(dy * xh, axis=0, keepdims=True)


def _standardize(t):
    mu = jnp.mean(t, axis=-1, keepdims=True)
    tc = t - mu
    rs = lax.rsqrt(jnp.mean(tc * tc, axis=-1, keepdims=True) + EPS)
    return tc * rs, rs


def _standardize_bwd(yh, rs, dy):
    return rs * (dy - jnp.mean(dy, axis=-1, keepdims=True) - yh * jnp.mean(dy * yh, axis=-1, keepdims=True))


_GELU_C = 0.7978845608028654


def _gelu(x):
    return 0.5 * x * (1.0 + jnp.tanh(_GELU_C * (x + 0.044715 * x * x * x)))


def _gelu_grad(x):
    t = jnp.tanh(_GELU_C * (x + 0.044715 * x * x * x))
    return 0.5 * (1.0 + t) + 0.5 * x * (1.0 - t * t) * _GELU_C * (1.0 + 3 * 0.044715 * x * x)


def _sigmoid(x):
    return 1.0 / (1.0 + jnp.exp(-x))


def _swap_half(t, half):
    n = t.shape[-1]
    lane = lax.broadcasted_iota(jnp.int32, t.shape, t.ndim - 1)
    return jnp.where((lane % (2 * half)) < half, pltpu.roll(t, n - half, t.ndim - 1), pltpu.roll(t, half, t.ndim - 1))


def _lanes(table, width):
    return jnp.concatenate([table] * (width // table.shape[-1]), axis=-1)


def _rope(t, cos, sin, half):
    return t * cos + _swap_half(t, half) * sin


def _rope_bwd(d, cos, sin, half):
    return d * cos - _swap_half(d, half) * sin


def _tables(s):
    pos = jnp.arange(s, dtype=F32)[:, None]

    def cs(half):
        inv = jnp.power(10000.0, -jnp.arange(half, dtype=F32) / half)
        ang = pos * inv[None, :]
        return jnp.cos(ang), jnp.sin(ang)

    c32, s32 = cs(32)
    c16, s16 = cs(16)
    z = lambda w: jnp.zeros((s, w), F32)
    o = lambda w: jnp.ones((s, w), F32)
    t = {}
    t["b_cos"] = jnp.concatenate([c32, c32, c32, c32], 1)
    t["b_sin"] = jnp.concatenate([-s32, s32, -s32, s32], 1)
    t["q_cos"] = jnp.concatenate([o(64), c16, c16, z(32)], 1)
    t["q_sin"] = jnp.concatenate([z(64), -s16, s16, z(32)], 1)
    t["k_cos"] = jnp.concatenate([z(32), c16, c16, z(64)], 1)
    t["k_sin"] = jnp.concatenate([z(32), -s16, s16, z(64)], 1)
    lg = jnp.log1p(-jnp.exp2(-5.0 - jnp.arange(N_HEADS, dtype=F32)))
    j = jnp.arange(CHUNK, dtype=F32)
    rel = j[:, None] - j[None, :]
    t["decay"] = jnp.where(rel[None] >= 0, jnp.exp(jnp.maximum(rel, 0.0)[None] * lg[:, None, None]), 0.0)
    t["decay_t"] = jnp.swapaxes(t["decay"], 1, 2)

    def rows(e):
        return jnp.repeat(e.T, HEAD_DIM, axis=1)

    t["qw"] = rows(jnp.exp((j + 1.0)[None, :] * lg[:, None]))
    t["kw"] = rows(jnp.exp((CHUNK - 1 - j)[None, :] * lg[:, None]))
    t["kw2"] = rows(jnp.exp((CHUNK - j)[None, :] * lg[:, None]))
    t["qw0"] = rows(jnp.exp(j[None, :] * lg[:, None]))
    t["cd"] = jnp.repeat(jnp.exp(CHUNK * lg), HEAD_DIM)[None, :]
    e = np.zeros((128, 512), np.float32)
    for h in range(N_HEADS):
        for r in range(32):
            e[MISC_KR + r, 128 * h + 64 + r] = 1.0
    t["place"] = jnp.asarray(e)
    lane_head = np.arange(GROUP) // HEAD_DIM
    t["grp"] = jnp.asarray((lane_head[:, None] == lane_head[None, :]) / HEAD_DIM, _MXU)
    hsel = (np.arange(128)[:, None] == lane_head[None, :]).astype(np.float32)
    t["hsel"] = jnp.asarray(hsel)
    t["hselt"] = jnp.asarray(hsel.T, _MXU)
    return t


def _norm_matmul(x, g, w, name, comm=None):
    s, d = x.shape
    n = w.shape[1]
    tm, tn = min(512, s), 256
    ni = s // tm

    def body(*refs):
        (x_ref, g_ref, w_ref), (z_ref, h_ref), _, cc = _split_refs(refs, 3, 2, comm)
        i = pl.program_id(0)
        _host_gather(comm, cc, i, ni)
        h = _rms(x_ref[...], g_ref[...]).astype(h_ref.dtype)
        h_ref[...] = h
        for j in range(n // tn):
            z_ref[:, tn * j:tn * (j + 1)] = jnp.dot(h, w_ref[:, tn * j:tn * (j + 1)], preferred_element_type=F32)
        if comm is not None:
            @pl.when(i == ni - 1)
            def _():
                comm.wait(*cc)

    in_specs = [pl.BlockSpec((tm, d), lambda i: (i, 0)), pl.BlockSpec((1, d), lambda i: (0, 0)),
                pl.BlockSpec((d, n), lambda i: (0, 0))]
    out_specs = [pl.BlockSpec((tm, n), lambda i: (i, 0)), pl.BlockSpec((tm, d), lambda i: (i, 0))]
    out_shape = [SDS((s, n), F32), SDS((s, d), _MXU)]
    return _call_with_comm(body, (ni,), in_specs, out_specs, out_shape, [], [x, g, w], comm, ("arbitrary",), name)


def _mm_tn(a, b, name, *, a_fn=None, blocked=False, out_dtype=F32):
    k, m = a.shape
    n = b.shape[1]
    tm, tk = min(1024, m), min(1024, k)
    tn = next(t for t in (1408, 1024, 512, 256, 128) if n % t == 0)
    assert m % tm == 0 and k % tk == 0
    nk = k // tk

    def body(a_ref, b_ref, o_ref, acc):
        kk = pl.program_id(2)

        @pl.when(kk == 0)
        def _():
            acc[...] = jnp.zeros_like(acc)

        av = a_ref[...]
        if a_fn is not None:
            av = a_fn(av.astype(F32))
        acc[...] += _dot_tn(av, b_ref[...])

        @pl.when(kk == nk - 1)
        def _():
            if blocked:
                for c in range(tn // 512):
                    o_ref[c] = acc[:, 512 * c:512 * (c + 1)].astype(o_ref.dtype)
            else:
                o_ref[...] = acc[...].astype(o_ref.dtype)

    if blocked:
        assert tn % 512 == 0
        out_spec = pl.BlockSpec((tn // 512, tm, 512), lambda i, j, kk: (j, i, 0))
        out_shape = SDS((n // 512, m, 512), out_dtype)
    else:
        out_spec = pl.BlockSpec((tm, tn), lambda i, j, kk: (i, j))
        out_shape = SDS((m, n), out_dtype)
    return pl.pallas_call(
        body, grid=(m // tm, n // tn, nk),
        in_specs=[pl.BlockSpec((tk, tm), lambda i, j, kk: (kk, i)), pl.BlockSpec((tk, tn), lambda i, j, kk: (kk, j))],
        out_specs=out_spec, out_shape=out_shape, scratch_shapes=[pltpu.VMEM((tm, tn), F32)],
        compiler_params=_cp("parallel", "parallel", "arbitrary"), name=name)(a, b)


def _split_dot(x, m):
    hi = x.astype(_MXU)
    lo = (x - hi.astype(F32)).astype(_MXU)
    return jnp.dot(hi, m, preferred_element_type=F32) + jnp.dot(lo, m, preferred_element_type=F32)


def _gstandardize(t, grp):
    tc = t - _split_dot(t, grp)
    rs = lax.rsqrt(_split_dot(tc * tc, grp) + EPS)
    return tc * rs, rs


def _gstandardize_bwd(yh, rs, dy, grp):
    return rs * (dy - _split_dot(dy, grp) - yh * _split_dot(dy * yh, grp))


def _head_select(parts):
    hid = lax.broadcasted_iota(jnp.int32, parts[0].shape, 1) // HEAD_DIM
    return jnp.where(hid == 0, parts[0], jnp.where(hid == 1, parts[1], jnp.where(hid == 2, parts[2], parts[3])))


def _head_masked(x):
    hid = lax.broadcasted_iota(jnp.int32, x.shape, 1) // HEAD_DIM
    return [jnp.where(hid == h, x, jnp.zeros_like(x)) for h in range(N_HEADS)]


def _tril(w):
    r = lax.broadcasted_iota(jnp.int32, w.shape, 0)
    c = lax.broadcasted_iota(jnp.int32, w.shape, 1)
    return jnp.where(r >= c, w, 0.0)


def _sgu_mixed(vgb, wcs, bias, nchunk):
    ms = [[jnp.dot(wcs[h], vgb[CHUNK * c:CHUNK * (c + 1)], preferred_element_type=F32) for h in range(N_HEADS)]
          for c in range(nchunk)]
    return [_head_select(ms[c]) + bias for c in range(nchunk)]


def _sgu_fwd(z, gain, w_s, b_t, tb, name):
    s = z.shape[0]
    tm = min(512, s)
    const = lambda a: pl.BlockSpec(a.shape, lambda i: (0,) * a.ndim)

    def body(u_ref, v_ref, g_ref, w_ref, b_ref, grp_ref, hsel_ref, y_ref):
        u = _gelu(u_ref[...])
        vh, _ = _gstandardize(_gelu(v_ref[...]), grp_ref[...])
        vgb = (vh * g_ref[...]).astype(_MXU)
        bias = _dot_exact(b_ref[...], hsel_ref[...])
        wcs = [_tril(w_ref[h]).astype(_MXU) for h in range(N_HEADS)]
        for c, mixed in enumerate(_sgu_mixed(vgb, wcs, bias, tm // CHUNK)):
            r = slice(CHUNK * c, CHUNK * (c + 1))
            y_ref[r, :] = u[r] * mixed

    return pl.pallas_call(
        body, grid=(s // tm,),
        in_specs=[pl.BlockSpec((tm, GROUP), lambda i: (i, 0)), pl.BlockSpec((tm, GROUP), lambda i: (i, 1)),
                  pl.BlockSpec((1, GROUP), lambda i: (0, 0)), pl.BlockSpec((N_HEADS, CHUNK, CHUNK), lambda i: (0, 0, 0)),
                  pl.BlockSpec((CHUNK, 128), lambda i: (0, 0)), const(tb["grp"]), const(tb["hsel"])],
        out_specs=pl.BlockSpec((tm, GROUP), lambda i: (i, 0)), out_shape=SDS((s, GROUP), F32),
        compiler_params=_cp("parallel"), name=name)(z, z, gain, w_s, b_t, tb["grp"], tb["hsel"])


def _sgu_bwd(dy, z, gain, w_s, b_t, tb, name):
    s = z.shape[0]
    tm = min(512, s)
    nchunk = tm // CHUNK
    const = lambda a: pl.BlockSpec(a.shape, lambda i: (0,) * a.ndim)

    def body(dy_ref, u_ref, v_ref, g_ref, w_ref, b_ref, grp_ref, hsel_ref, hselt_ref, dz_ref, dg_ref, dw_ref, db_ref):
        @pl.when(pl.program_id(0) == 0)
        def _():
            dg_ref[...] = jnp.zeros_like(dg_ref)
            dw_ref[...] = jnp.zeros_like(dw_ref)
            db_ref[...] = jnp.zeros_like(db_ref)

        grp = grp_ref[...]
        u_pre, v_pre, gain_v = u_ref[...], v_ref[...], g_ref[...]
        u = _gelu(u_pre)
        vh, rs = _gstandardize(_gelu(v_pre), grp)
        vgb = (vh * gain_v).astype(_MXU)
        dyv = dy_ref[...]
        bias = _dot_exact(b_ref[...], hsel_ref[...])
        wfs = [_tril(w_ref[h]) for h in range(N_HEADS)]
        wcs = [w.astype(_MXU) for w in wfs]
        wts = [w.T.astype(_MXU) for w in wfs]
        mixed = _sgu_mixed(vgb, wcs, bias, nchunk)
        gu = _gelu_grad(u_pre)
        dms, dmh = [], []
        for c in range(nchunk):
            r = slice(CHUNK * c, CHUNK * (c + 1))
            dz_ref[r, 0:GROUP] = (dyv[r] * mixed[c] * gu[r]).astype(dz_ref.dtype)
            dm = dyv[r] * u[r]
            dms.append(dm)
            dmh.append([m.astype(_MXU) for m in _head_masked(dm)])
        dws = [sum(lax.dot_general(dmh[c][h], vgb[CHUNK * c:CHUNK * (c + 1)], (((1,), (1,)), ((), ())),
                                   preferred_element_type=F32) for c in range(nchunk)) for h in range(N_HEADS)]
        dvg = jnp.concatenate([sum(jnp.dot(wts[h], dmh[c][h], preferred_element_type=F32) for h in range(N_HEADS))
                               for c in range(nchunk)], axis=0)
        for h in range(N_HEADS):
            dw_ref[h] += _tril(dws[h])
        db_ref[...] += sum(_split_dot(dm, hselt_ref[...]) for dm in dms)
        dg_ref[...] += jnp.sum(dvg * vh, axis=0, keepdims=True)
        dv = _gstandardize_bwd(vh, rs, dvg * gain_v, grp)
        dz_ref[:, GROUP:2 * GROUP] = (dv * _gelu_grad(v_pre)).astype(dz_ref.dtype)

    consts = [tb["grp"], tb["hsel"], tb["hselt"]]
    return pl.pallas_call(
        body, grid=(s // tm,),
        in_specs=[pl.BlockSpec((tm, GROUP), lambda i: (i, 0)),
                  pl.BlockSpec((tm, GROUP), lambda i: (i, 0)), pl.BlockSpec((tm, GROUP), lambda i: (i, 1)),
                  pl.BlockSpec((1, GROUP), lambda i: (0, 0)), pl.BlockSpec((N_HEADS, CHUNK, CHUNK), lambda i: (0, 0, 0)),
                  pl.BlockSpec((CHUNK, 128), lambda i: (0, 0))] + [const(a) for a in consts],
        out_specs=[pl.BlockSpec((tm, 2 * GROUP), lambda i: (i, 0)), pl.BlockSpec((1, GROUP), lambda i: (0, 0)),
                   pl.BlockSpec((N_HEADS, CHUNK, CHUNK), lambda i: (0, 0, 0)), pl.BlockSpec((CHUNK, 128), lambda i: (0, 0))],
        out_shape=[SDS((s, 2 * GROUP), _MXU), SDS((1, GROUP), F32), SDS((N_HEADS, CHUNK, CHUNK), F32), SDS((CHUNK, 128), F32)],
        compiler_params=_cp("arbitrary"), name=name)(dy, z, z, gain, w_s, b_t, *consts)


_SCALE_B = HEAD_DIM ** -0.5


def _block_diag(compact):
    full = jnp.concatenate([compact] * N_HEADS, axis=0)
    r = lax.broadcasted_iota(jnp.int32, full.shape, 0) // HEAD_DIM
    c = lax.broadcasted_iota(jnp.int32, full.shape, 1) // HEAD_DIM
    return jnp.where(r == c, full, 0.0)


def _diag_blocks(full):
    c = lax.broadcasted_iota(jnp.int32, (HEAD_DIM, GROUP), 1) // HEAD_DIM
    return sum(jnp.where(c == h, full[HEAD_DIM * h:HEAD_DIM * (h + 1), :], 0.0) for h in range(N_HEADS))


def _ret_fwd(z, tb, name):
    s = z.shape[0]
    nc = s // CHUNK
    row = lambda col: pl.BlockSpec((CHUNK, GROUP), lambda n, col=col: (n, col))
    const = lambda shape: pl.BlockSpec(shape, lambda n: (0,) * len(shape))

    def body(q_ref, k_ref, v_ref, g_ref, cos_ref, sin_ref, dec_ref, qw_ref, kw_ref, cd_ref, grp_ref, y_ref, o_ref, st_ref, state):
        @pl.when(pl.program_id(0) == 0)
        def _():
            state[...] = jnp.zeros_like(state)

        cos, sin = _lanes(cos_ref[...], GROUP), _lanes(sin_ref[...], GROUP)
        q = _rope(q_ref[...], cos, sin, 32)
        k = _rope(k_ref[...], cos, sin, 32) * _SCALE_B
        v = v_ref[...]
        g = g_ref[...]
        st_ref[0] = state[...]
        qm = [t.astype(_MXU) for t in _head_masked(q)]
        vm = [t.astype(_MXU) for t in _head_masked(v)]
        scs = [_dot_nt(qm[h], k) for h in range(N_HEADS)]
        cross = _dot(q * qw_ref[...], _block_diag(state[...]))
        kv = _dot_tn(k * kw_ref[...], v)
        scd = [(scs[h] * dec_ref[h]).astype(_MXU) for h in range(N_HEADS)]
        o = cross + sum(jnp.dot(scd[h], vm[h], preferred_element_type=F32) for h in range(N_HEADS))
        o_ref[...] = o
        yh, _ = _gstandardize(o, grp_ref[...])
        y_ref[...] = g * _sigmoid(g) * yh
        state[...] = cd_ref[...] * state[...] + _diag_blocks(kv)

    return pl.pallas_call(
        body, grid=(nc,),
        in_specs=[row(2), row(3), row(4), row(5), pl.BlockSpec((CHUNK, 128), lambda n: (n, 0)),
                  pl.BlockSpec((CHUNK, 128), lambda n: (n, 0)), const((N_HEADS, CHUNK, CHUNK)),
                  const((CHUNK, GROUP)), const((CHUNK, GROUP)), const((1, GROUP)), const((GROUP, GROUP))],
        out_specs=[pl.BlockSpec((CHUNK, GROUP), lambda n: (n, 0)), pl.BlockSpec((CHUNK, GROUP), lambda n: (n, 0)),
                   pl.BlockSpec((1, HEAD_DIM, GROUP), lambda n: (n, 0, 0))],
        out_shape=[SDS((s, GROUP), F32), SDS((s, GROUP), F32), SDS((nc, HEAD_DIM, GROUP), F32)],
        scratch_shapes=[pltpu.VMEM((HEAD_DIM, GROUP), F32)],
        compiler_params=_cp("arbitrary"), name=name)(z, z, z, z, tb["b_cos"], tb["b_sin"], tb["decay"], tb["qw"], tb["kw"], tb["cd"],
                                                       tb["grp"])


def _ret_bwd(dy, z, o_pre, states, tb, name):
    s = z.shape[0]
    nc = s // CHUNK
    rev = lambda col: pl.BlockSpec((CHUNK, GROUP), lambda n, col=col: (nc - 1 - n, col))
    const = lambda shape: pl.BlockSpec(shape, lambda n: (0,) * len(shape))

    def body(dy_ref, q_ref, k_ref, v_ref, g_ref, o_ref, st_ref, cos_ref, sin_ref, dec_ref, dect_ref, qw_ref, kw2_ref, qw0_ref,
             cd_ref, grp_ref, dz_ref, rstate):
        @pl.when(pl.program_id(0) == 0)
        def _():
            rstate[...] = jnp.zeros_like(rstate)

        cos, sin = _lanes(cos_ref[...], GROUP), _lanes(sin_ref[...], GROUP)
        q = _rope(q_ref[...], cos, sin, 32)
        k = _rope(k_ref[...], cos, sin, 32) * _SCALE_B
        v = v_ref[...]
        g = g_ref[...]
        dyv = dy_ref[...]
        sg = _sigmoid(g)
        yh, rs = _gstandardize(o_ref[...], grp_ref[...])
        dz_ref[:, 3 * GROUP:4 * GROUP] = (dyv * yh * (sg * (1.0 + g * (1.0 - sg)))).astype(dz_ref.dtype)
        do = _gstandardize_bwd(yh, rs, dyv * (g * sg), grp_ref[...])
        qm = [t.astype(_MXU) for t in _head_masked(q)]
        km = [t.astype(_MXU) for t in _head_masked(k)]
        vm = [t.astype(_MXU) for t in _head_masked(v)]
        dom = [t.astype(_MXU) for t in _head_masked(do)]
        s_bd = _block_diag(st_ref[0])
        r_bd = _block_diag(rstate[...])
        hs = range(N_HEADS)
        dps = [_dot_nt(dom[h], v) for h in hs]
        pts = [_dot_nt(km[h], q) for h in hs]
        dpts = [_dot_nt(vm[h], do) for h in hs]
        dq_x = _dot_nt(do * qw_ref[...], s_bd)
        dk_x = _dot_nt(v * kw2_ref[...], r_bd)
        dv_x = _dot(k * kw2_ref[...], r_bd)
        r_new = _dot_tn(q * qw0_ref[...], do)
        dpd = [(dps[h] * dec_ref[h]).astype(_MXU) for h in hs]
        dptd = [(dpts[h] * dect_ref[h]).astype(_MXU) for h in hs]
        ptd = [(pts[h] * dect_ref[h]).astype(_MXU) for h in hs]
        dq = dq_x + sum(jnp.dot(dpd[h], km[h], preferred_element_type=F32) for h in hs)
        dk = dk_x + sum(jnp.dot(dptd[h], qm[h], preferred_element_type=F32) for h in hs)
        dv = dv_x + sum(jnp.dot(ptd[h], dom[h], preferred_element_type=F32) for h in hs)
        dz_ref[:, 0:GROUP] = _rope_bwd(dq, cos, sin, 32).astype(dz_ref.dtype)
        dz_ref[:, GROUP:2 * GROUP] = _rope_bwd(dk * _SCALE_B, cos, sin, 32).astype(dz_ref.dtype)
        dz_ref[:, 2 * GROUP:3 * GROUP] = dv.astype(dz_ref.dtype)
        rstate[...] = cd_ref[...] * rstate[...] + _diag_blocks(r_new)

    r0 = lambda: pl.BlockSpec((CHUNK, GROUP), lambda n: (nc - 1 - n, 0))
    r128 = lambda: pl.BlockSpec((CHUNK, 128), lambda n: (nc - 1 - n, 0))
    return pl.pallas_call(
        body, grid=(nc,),
        in_specs=[r0(), rev(2), rev(3), rev(4), rev(5), r0(), pl.BlockSpec((1, HEAD_DIM, GROUP), lambda n: (nc - 1 - n, 0, 0)),
                  r128(), r128(), const((N_HEADS, CHUNK, CHUNK)), const((N_HEADS, CHUNK, CHUNK)), const((CHUNK, GROUP)),
                  const((CHUNK, GROUP)), const((CHUNK, GROUP)), const((1, GROUP)), const((GROUP, GROUP))],
        out_specs=pl.BlockSpec((CHUNK, 4 * GROUP), lambda n: (nc - 1 - n, 0)),
        out_shape=SDS((s, 4 * GROUP), _MXU), scratch_shapes=[pltpu.VMEM((HEAD_DIM, GROUP), F32)],
        compiler_params=_cp("arbitrary"), name=name)(
            dy, z, z, z, z, o_pre, states, tb["b_cos"], tb["b_sin"], tb["decay"], tb["decay_t"], tb["qw"], tb["kw2"], tb["qw0"],
            tb["cd"], tb["grp"])


TQ = 256


def _log_sigmoid(x):
    return jnp.minimum(x, 0.0) - jnp.log1p(jnp.exp(-jnp.abs(x)))


def _fox_prep(z, b_f, name):
    s = z.shape[0]
    nb = s // TQ

    def body(m_ref, b_ref, cc_ref, carry):
        @pl.when(pl.program_id(0) == 0)
        def _():
            carry[...] = jnp.zeros_like(carry)

        lane = lax.broadcasted_iota(jnp.int32, (TQ, 128), 1)
        logf = jnp.where(lane < N_HEADS, _log_sigmoid(m_ref[...] + b_ref[...]), 0.0)
        r = lax.broadcasted_iota(jnp.int32, (TQ, TQ), 0)
        c = lax.broadcasted_iota(jnp.int32, (TQ, TQ), 1)
        tri = jnp.where(r >= c, 1.0, 0.0).astype(F32)
        cum = _dot_exact(tri, logf) + carry[...]
        cc_ref[...] = cum * LOG2E
        carry[...] = cum[TQ - 1:TQ, :]

    return pl.pallas_call(
        body, grid=(nb,),
        in_specs=[pl.BlockSpec((TQ, 128), lambda i: (i, NZ // 128 - 1)), pl.BlockSpec((1, 128), lambda i: (0, 0))],
        out_specs=pl.BlockSpec((TQ, 128), lambda i: (i, 0)),
        out_shape=SDS((s, 128), F32), scratch_shapes=[pltpu.VMEM((1, 128), F32)],
        compiler_params=_cp("arbitrary"), name=name)(z, b_f)


def _fox_post(dcr, dcq, z, b_f, dkr, name):
    s = z.shape[0]
    nb = s // TQ

    def body(dc_ref, dcq_ref, m_ref, b_ref, dkr_ref, dz_ref, db_ref, carry):
        @pl.when(pl.program_id(0) == 0)
        def _():
            carry[...] = jnp.zeros_like(carry)
            db_ref[...] = jnp.zeros_like(db_ref)

        r = lax.broadcasted_iota(jnp.int32, (TQ, TQ), 0)
        c = lax.broadcasted_iota(jnp.int32, (TQ, TQ), 1)
        triu = jnp.where(c >= r, 1.0, 0.0).astype(F32)
        dc = jnp.concatenate([dc_ref[0], jnp.zeros((120, TQ), F32)], axis=0)
        dlogf = _dot_exact(triu, dc, (((1,), (1,)), ((), ()))) + _dot_exact(triu, dcq_ref[...]) + carry[...]
        carry[...] = dlogf[0:1, :]
        x = m_ref[...] + b_ref[...]
        lane = lax.broadcasted_iota(jnp.int32, (TQ, 128), 1)
        df = jnp.where(lane < N_HEADS, dlogf * _sigmoid(-x), 0.0)
        db_ref[...] += jnp.sum(df, axis=0, keepdims=True)
        dz_ref[...] = (df + dkr_ref[...]).astype(dz_ref.dtype)

    rv = lambda i: nb - 1 - i
    return pl.pallas_call(
        body, grid=(nb,),
        in_specs=[pl.BlockSpec((1, 8, TQ), lambda i: (rv(i), 0, 0)), pl.BlockSpec((TQ, 128), lambda i: (rv(i), 0)),
                  pl.BlockSpec((TQ, 128), lambda i: (rv(i), NZ // 128 - 1)),
                  pl.BlockSpec((1, 128), lambda i: (0, 0)), pl.BlockSpec((TQ, 128), lambda i: (rv(i), 0))],
        out_specs=[pl.BlockSpec((TQ, 128), lambda i: (rv(i), 0)), pl.BlockSpec((1, 128), lambda i: (0, 0))],
        out_shape=[SDS((s, 128), _MXU), SDS((1, 128), F32)], scratch_shapes=[pltpu.VMEM((1, 128), F32)],
        compiler_params=_cp("arbitrary"), name=name)(dcr, dcq, z, b_f, dkr)


NEG = -1e30


def _causal_mask(shape, transposed=False):
    r = lax.broadcasted_iota(jnp.int32, shape, 0)
    c = lax.broadcasted_iota(jnp.int32, shape, 1)
    return (c >= r) if transposed else (r >= c)


TKV = 512


def _key_block(s):
    return min(TKV, s)


def _diag_mask(shape, off):
    r = lax.broadcasted_iota(jnp.int32, shape, 0)
    c = lax.broadcasted_iota(jnp.int32, shape, 1)
    return c + off >= r


def _head_lanes(h, dqk):
    return slice(128 * (h // 2), 128 * (h // 2) + 128) if dqk == HEAD_DIM else slice(128 * h, 128 * h + 128)


def _keep_half(x, a, axis):
    idx = lax.broadcasted_iota(jnp.int32, x.shape, axis)
    return jnp.where((idx < HEAD_DIM) if a == 0 else (idx >= HEAD_DIM), x, jnp.zeros_like(x))


def _kv_prep(z, kcol, vcol, name):
    s = z.shape[0]
    tk = _key_block(s)
    nk = s // tk

    def body(k_ref, v_ref, kb_ref, vb_ref, vt_ref):
        kb_ref[...] = k_ref[...].astype(_MXU)
        v = v_ref[...]
        vb_ref[...] = v.astype(_MXU)
        vt_ref[0] = v.T.astype(_MXU)

    blk = pl.BlockSpec((tk, GROUP), lambda i: (i, 0))
    return pl.pallas_call(
        body, grid=(nk,),
        in_specs=[pl.BlockSpec((tk, GROUP), lambda i: (i, kcol)), pl.BlockSpec((tk, GROUP), lambda i: (i, vcol))],
        out_specs=[blk, blk, pl.BlockSpec((1, GROUP, tk), lambda i: (i, 0, 0))],
        out_shape=[SDS((s, GROUP), _MXU), SDS((s, GROUP), _MXU), SDS((nk, GROUP, tk), _MXU)],
        compiler_params=_cp("parallel"), name=name)(z, z)


LOG2E = 1.4426950408889634


def _attn_fwd(q, qcol, dqk, kb, vt, scale, ck2, name, comm=None):
    s = q.shape[0]
    nq = s // TQ
    tk = _key_block(s)
    ratio = tk // TQ
    wq = N_HEADS * dqk
    bias = ck2 is not None

    def body(*refs):
        ins, (o_ref, l_ref), _, cc = _split_refs(refs, 4 if bias else 3, 2, comm)
        if bias:
            q_ref, k_ref, vt_ref, cc_ref = ins
        else:
            q_ref, k_ref, vt_ref = ins
        i = pl.program_id(0)
        _host_gather(comm, cc, i, nq)
        qts = []
        for h in range(N_HEADS):
            qt = (q_ref[:, _head_lanes(h, dqk)].astype(F32) * (scale * LOG2E)).T
            qts.append((_keep_half(qt, h % 2, 0) if dqk == HEAD_DIM else qt).astype(_MXU))

        def step(j, carry, off):
            r0 = pl.multiple_of(j * tk, tk)
            vtj = vt_ref[j]
            sts = [jnp.dot(k_ref[pl.ds(r0, tk), _head_lanes(h, dqk)], qts[h], preferred_element_type=F32)
                   for h in range(N_HEADS)]
            stats, ps = [], []
            for h in range(N_HEADS):
                m, l, _ = carry[3 * h:3 * h + 3]
                st = sts[h]
                if bias:
                    st = st - cc_ref[pl.ds(r0, tk), h:h + 1]
                if off is not None:
                    st = jnp.where(_diag_mask(st.shape, off), st, NEG)
                m_new = jnp.maximum(m, jnp.max(st, axis=0, keepdims=True))
                alpha = jnp.exp2(m - m_new)
                p = jnp.exp2(st - m_new)
                stats.append((m_new, alpha * l + jnp.sum(p, axis=0, keepdims=True), alpha))
                ps.append(p.astype(_MXU))
            out = []
            for h in range(N_HEADS):
                m_new, l, alpha = stats[h]
                acc = alpha * carry[3 * h + 2] + jnp.dot(vtj[HEAD_DIM * h:HEAD_DIM * (h + 1), :], ps[h],
                                                         preferred_element_type=F32)
                out += [m_new, l, acc]
            return tuple(out)

        init = (jnp.full((1, TQ), NEG, F32), jnp.zeros((1, TQ), F32), jnp.zeros((HEAD_DIM, TQ), F32)) * N_HEADS
        jd = i // ratio
        carry = lax.fori_loop(0, jd, functools.partial(step, off=None), init)
        carry = step(jd, carry, TQ * (i % ratio))
        l_ref[...] = jnp.zeros_like(l_ref)
        for h in range(N_HEADS):
            l_ref[0, h:h + 1, :] = carry[3 * h] + jnp.log2(carry[3 * h + 1])
        for p in range(2):
            ot = jnp.concatenate([carry[6 * p + 2] / carry[6 * p + 1], carry[6 * p + 5] / carry[6 * p + 4]], axis=0)
            o_ref[:, 128 * p:128 * (p + 1)] = ot.T
        if comm is not None:
            @pl.when(i == nq - 1)
            def _():
                comm.wait(*cc)

    rows = pl.BlockSpec((1, 8, TQ), lambda i: (i, 0, 0))
    in_specs = [pl.BlockSpec((TQ, wq), lambda i: (i, qcol)), pl.BlockSpec((s, wq), lambda i: (0, 0)),
                pl.BlockSpec((s // tk, GROUP, tk), lambda i: (0, 0, 0))]
    args = [q, kb, vt]
    if bias:
        in_specs.append(pl.BlockSpec((s, 128), lambda i: (0, 0)))
        args.append(ck2)
    out_specs = [pl.BlockSpec((TQ, GROUP), lambda i: (i, 0)), rows]
    out_shape = [SDS((s, GROUP), F32), SDS((nq, 8, TQ), F32)]
    return _call_with_comm(body, (nq,), in_specs, out_specs, out_shape, [], args, comm, ("arbitrary",), name)


def _call_with_comm(body, grid, in_specs, out_specs, out_shape, scratch, args, comm, semantics, name):
    n_out = len(out_shape)
    if comm is not None:
        in_specs, out_specs = in_specs + comm.in_specs, out_specs + comm.out_specs
        out_shape, scratch, args = out_shape + comm.out_shape, scratch + comm.scratch, list(args) + comm.arrs
    res = pl.pallas_call(body, grid=grid, in_specs=in_specs, out_specs=out_specs, out_shape=out_shape,
                         scratch_shapes=scratch, compiler_params=_cp(*semantics), name=name)(*args)
    return (*res[:n_out], list(res[n_out:]))


def _attn_bwd_prep(q, qcol, dqk, scale, o, do, name):
    s = q.shape[0]
    nq = s // TQ
    wq = N_HEADS * dqk

    def body(q_ref, o_ref, do_ref, qt_ref, dot_ref, dl_ref):
        qt_ref[0] = (q_ref[...].astype(F32) * (scale * LOG2E)).T.astype(_MXU)
        dov = do_ref[...]
        dot_ref[0] = dov.T.astype(_MXU)
        pt = (dov * o_ref[...]).T
        dl_ref[...] = jnp.zeros_like(dl_ref)
        for h in range(N_HEADS):
            dl_ref[0, h:h + 1, :] = jnp.sum(pt[HEAD_DIM * h:HEAD_DIM * (h + 1), :], axis=0, keepdims=True)

    nat = lambda w: pl.BlockSpec((TQ, w), lambda i: (i, 0))
    tr = lambda w: pl.BlockSpec((1, w, TQ), lambda i: (i, 0, 0))
    return pl.pallas_call(
        body, grid=(nq,),
        in_specs=[pl.BlockSpec((TQ, wq), lambda i: (i, qcol)), nat(GROUP), nat(GROUP)],
        out_specs=[tr(wq), tr(GROUP), tr(8)],
        out_shape=[SDS((nq, wq, TQ), _MXU), SDS((nq, GROUP, TQ), _MXU), SDS((nq, 8, TQ), F32)],
        compiler_params=_cp("parallel"), name=name)(q, o, do)


def _attn_bwd(kb, vb, qt, dot, lse, dl, dqk, scale, ck2, name, kv_dtype, comm=None):
    s = kb.shape[0]
    nq = s // TQ
    tk = _key_block(s)
    ratio = tk // TQ
    nkb = s // tk
    wq = N_HEADS * dqk
    bias = ck2 is not None

    def body(*refs):
        ins, outs, _, cc = _split_refs(refs, 7 if bias else 6, 5 if bias else 3, comm)
        if bias:
            k_ref, v_ref, qt_ref, dot_ref, l_ref, d_ref, cc_ref = ins
            dqt_ref, dk_ref, dv_ref, dck_ref, dcq_ref = outs
        else:
            k_ref, v_ref, qt_ref, dot_ref, l_ref, d_ref = ins
            dqt_ref, dk_ref, dv_ref = outs
        j = pl.program_id(0)

        @pl.when(j == 0)
        def _():
            if comm is not None:
                comm.start(*cc)
            dqt_ref[...] = jnp.zeros_like(dqt_ref)
            if bias:
                dcq_ref[...] = jnp.zeros_like(dcq_ref)

        ks, kts, vs = [], [], []
        for h in range(N_HEADS):
            k2 = k_ref[:, _head_lanes(h, dqk)]
            if dqk == HEAD_DIM:
                k2 = _keep_half(k2, h % 2, 1)
            ks.append(k2)
            kts.append(k2.astype(F32).T.astype(_MXU))
            vs.append(_keep_half(v_ref[:, _head_lanes(h, HEAD_DIM)], h % 2, 1))
        cks = [cc_ref[:, h:h + 1] for h in range(N_HEADS)] if bias else None

        nt = (((1,), (1,)), ((), ()))

        def step(i, carry, off):
            qti, doti, li, di = qt_ref[i], dot_ref[i], l_ref[i], d_ref[i]
            qls = [_head_lanes(h, dqk) for h in range(N_HEADS)]
            vls = [_head_lanes(h, HEAD_DIM) for h in range(N_HEADS)]
            sts, dpts = [], []
            for h in range(N_HEADS):
                sts.append(jnp.dot(ks[h], qti[qls[h], :], preferred_element_type=F32))
                dpts.append(jnp.dot(vs[h], doti[vls[h], :], preferred_element_type=F32))
            pbs, dsbs, dcks = [], [], []
            for h in range(N_HEADS):
                st = sts[h] - li[h:h + 1, :]
                if bias:
                    st = st - cks[h]
                p = jnp.exp2(st)
                if off is not None:
                    p = jnp.where(_diag_mask(p.shape, off), p, 0.0)
                dst = p * (dpts[h] - di[h:h + 1, :])
                pbs.append(p.astype(_MXU))
                dsbs.append(dst.astype(_MXU))
                if bias:
                    dcks.append(carry[3 * h + 2] + jnp.sum(dst, axis=1, keepdims=True))
                    dcq_ref[i, h:h + 1, :] += jnp.sum(dst, axis=0, keepdims=True)
                else:
                    dcks.append(carry[3 * h + 2])
            out = []
            for h in range(N_HEADS):
                dvt = carry[3 * h + 1] + lax.dot_general(doti[HEAD_DIM * h:HEAD_DIM * (h + 1), :], pbs[h], nt,
                                                         preferred_element_type=F32)
                dkt = carry[3 * h] + lax.dot_general(qti[dqk * h:dqk * (h + 1), :], dsbs[h], nt, preferred_element_type=F32)
                dqt_ref[i, qls[h], :] += jnp.dot(kts[h], dsbs[h], preferred_element_type=F32) * scale
                out += [dkt, dvt, dcks[h]]
            return tuple(out)

        carry = (jnp.zeros((dqk, tk), F32), jnp.zeros((HEAD_DIM, tk), F32), jnp.zeros((tk, 1), F32)) * N_HEADS
        for r in range(ratio):
            carry = step(ratio * j + r, carry, TQ * r)
        carry = lax.fori_loop(ratio * (j + 1), nq, functools.partial(step, off=None), carry)
        for p in range(2):
            dv_ref[:, 128 * p:128 * (p + 1)] = jnp.concatenate([carry[6 * p + 1], carry[6 * p + 4]], axis=0).T.astype(dv_ref.dtype)
            if dqk == HEAD_DIM:
                dk_ref[:, 128 * p:128 * (p + 1)] = (jnp.concatenate([carry[6 * p], carry[6 * p + 3]], axis=0).T
                                                    * (1.0 / LOG2E)).astype(dk_ref.dtype)
        if dqk != HEAD_DIM:
            for h in range(N_HEADS):
                dk_ref[:, 128 * h:128 * (h + 1)] = (carry[3 * h].T * (1.0 / LOG2E)).astype(dk_ref.dtype)
        if bias:
            dck_ref[...] = jnp.zeros_like(dck_ref)
            for h in range(N_HEADS):
                dck_ref[:, h:h + 1] = -carry[3 * h + 2]
        if comm is not None:
            @pl.when(j == nkb - 1)
            def _():
                comm.wait(*cc)

    blk = lambda w: pl.BlockSpec((tk, w), lambda j: (j, 0))
    full3 = lambda w: pl.BlockSpec((nq, w, TQ), lambda j: (0, 0, 0))
    in_specs = [blk(wq), blk(GROUP), full3(wq), full3(GROUP), full3(8), full3(8)]
    args = [kb, vb, qt, dot, lse, dl]
    out_specs = [full3(wq), blk(wq), blk(GROUP)]
    out_shape = [SDS((nq, wq, TQ), F32), SDS((s, wq), kv_dtype), SDS((s, GROUP), kv_dtype)]
    if bias:
        in_specs.append(blk(128))
        args.append(ck2)
        out_specs += [blk(128), full3(8)]
        out_shape += [SDS((s, 128), F32), SDS((nq, 8, TQ), F32)]
    return _call_with_comm(body, (nkb,), in_specs, out_specs, out_shape, [], args, comm, ("arbitrary",), name)


def _untranspose(xt, dtype, name):
    nq, w, _ = xt.shape

    def body(x_ref, o_ref):
        o_ref[...] = x_ref[0].T.astype(o_ref.dtype)

    return pl.pallas_call(
        body, grid=(nq,), in_specs=[pl.BlockSpec((1, w, TQ), lambda i: (i, 0, 0))],
        out_specs=pl.BlockSpec((TQ, w), lambda i: (i, 0)), out_shape=SDS((nq * TQ, w), dtype),
        compiler_params=_cp("parallel"), name=name)(xt)


_SCALE_D = (64 + 32) ** -0.5
_COL_CQ, _COL_CKV, _COL_MISC = 2304 // 256, 2560 // 128, 2688 // 128


def _mla_prep(z, gq, gkv, wq, wk, wv, tb, name):
    s = z.shape[0]
    tm = _key_block(s)
    row = lambda w, c: pl.BlockSpec((tm, w), lambda i, c=c: (i, c))
    const = lambda a: pl.BlockSpec(a.shape, lambda i: (0,) * a.ndim)

    def body(cq_ref, ckv_ref, m_ref, gq_ref, gkv_ref, wq_ref, wk_ref, wv_ref, e_ref, qc_ref, qs_ref, kc_ref, ks_ref,
             q_ref, k_ref, v_ref, vt_ref, cqn_ref, ckvn_ref):
        cqn = _rms(cq_ref[...], gq_ref[...]).astype(_MXU)
        ckvn = _rms(ckv_ref[...], gkv_ref[...]).astype(_MXU)
        cqn_ref[...] = cqn
        ckvn_ref[...] = ckvn
        q_ref[...] = _rope(_dot(cqn, wq_ref[...]), _lanes(qc_ref[...], 512), _lanes(qs_ref[...], 512), 16).astype(q_ref.dtype)
        kr = _rope(m_ref[...], kc_ref[...], ks_ref[...], 16)
        k_ref[...] = (_dot(ckvn, wk_ref[...]) + _dot(kr, e_ref[...])).astype(k_ref.dtype)
        v = _dot(ckvn, wv_ref[...])
        v_ref[...] = v.astype(v_ref.dtype)
        vt_ref[0] = v.T.astype(vt_ref.dtype)

    e = tb["place"]
    return pl.pallas_call(
        body, grid=(s // tm,),
        in_specs=[row(256, _COL_CQ), row(128, _COL_CKV), row(128, _COL_MISC), const(gq), const(gkv), const(wq), const(wk),
                  const(wv), const(e), row(128, 0), row(128, 0), row(128, 0), row(128, 0)],
        out_specs=[row(512, 0), row(512, 0), row(256, 0), pl.BlockSpec((1, GROUP, tm), lambda i: (i, 0, 0)), row(256, 0),
                   row(128, 0)],
        out_shape=[SDS((s, 512), _MXU), SDS((s, 512), _MXU), SDS((s, 256), _MXU), SDS((s // tm, GROUP, tm), _MXU),
                   SDS((s, 256), _MXU), SDS((s, 128), _MXU)],
        compiler_params=_cp("parallel"), name=name)(
            z, z, z, gq, gkv, wq, wk, wv, e, tb["q_cos"], tb["q_sin"], tb["k_cos"], tb["k_sin"])


def _mla_prep_bwd(dq, dk, dv, z, cqn, ckvn, gq, gkv, wq, wk, wv, tb, name):
    s = z.shape[0]
    tm = min(512, s)
    row = lambda w, c: pl.BlockSpec((tm, w), lambda i, c=c: (i, c))
    const = lambda a: pl.BlockSpec(a.shape, lambda i: (0,) * a.ndim)
    acc = lambda shape: pl.BlockSpec(shape, lambda i: (0, 0))

    def body(dq_ref, dk_ref, dv_ref, cq_ref, ckv_ref, cqn_ref, ckvn_ref, gq_ref, gkv_ref, wq_ref, wk_ref, wv_ref, e_ref,
             qc_ref, qs_ref, kc_ref, ks_ref, dcq_ref, dckv_ref, dkr_ref, dwq_ref, dwk_ref, dwv_ref, dgq_ref, dgkv_ref):
        @pl.when(pl.program_id(0) == 0)
        def _():
            for r in (dwq_ref, dwk_ref, dwv_ref, dgq_ref, dgkv_ref):
                r[...] = jnp.zeros_like(r)

        dqp = _rope_bwd(dq_ref[...], _lanes(qc_ref[...], 512), _lanes(qs_ref[...], 512), 16)
        dkd = dk_ref[...]
        dvd = dv_ref[...]
        dwq_ref[...] += _dot_tn(cqn_ref[...], dqp)
        dwk_ref[...] += _dot_tn(ckvn_ref[...], dkd)
        dwv_ref[...] += _dot_tn(ckvn_ref[...], dvd)
        dcq, dgq = _rms_bwd(cq_ref[...], gq_ref[...], _dot_nt(dqp, wq_ref[...]))
        dckv, dgkv = _rms_bwd(ckv_ref[...], gkv_ref[...], _dot_nt(dkd, wk_ref[...]) + _dot_nt(dvd, wv_ref[...]))
        dcq_ref[...] = dcq.astype(dcq_ref.dtype)
        dckv_ref[...] = dckv.astype(dckv_ref.dtype)
        dgq_ref[...] += dgq
        dgkv_ref[...] += dgkv
        dkr = _dot_exact(dkd, e_ref[...], (((1,), (1,)), ((), ())))
        dkr_ref[...] = _rope_bwd(dkr, kc_ref[...], ks_ref[...], 16)

    e = tb["place"]
    return pl.pallas_call(
        body, grid=(s // tm,),
        in_specs=[row(512, 0), row(512, 0), row(256, 0), row(256, _COL_CQ), row(128, _COL_CKV), row(256, 0), row(128, 0),
                  const(gq), const(gkv), const(wq), const(wk), const(wv), const(e), row(128, 0), row(128, 0), row(128, 0), row(128, 0)],
        out_specs=[row(256, 0), row(128, 0), row(128, 0), acc((256, 512)), acc((128, 512)), acc((128, 256)), acc((1, 256)),
                   acc((1, 128))],
        out_shape=[SDS((s, 256), _MXU), SDS((s, 128), _MXU), SDS((s, 128), F32), SDS((256, 512), F32), SDS((128, 512), F32),
                   SDS((128, 256), F32), SDS((1, 256), F32), SDS((1, 128), F32)],
        compiler_params=_cp("arbitrary"), name=name)(
            dq, dk, dv, z, z, cqn, ckvn, gq, gkv, wq, wk, wv, e, tb["q_cos"], tb["q_sin"], tb["k_cos"], tb["k_sin"])


def _out_proj(ys, g, w, x, name):
    s, d = x.shape
    tm = min(512, s)

    def body(ya, yb, yc, yd, g_ref, w_ref, x_ref, o_ref, yn_ref):
        acc = x_ref[...]
        for i, y_ref in enumerate((ya, yb, yc, yd)):
            sl = slice(GROUP * i, GROUP * (i + 1))
            yn = _rms(y_ref[...], g_ref[:, sl]).astype(_MXU)
            yn_ref[:, sl] = yn
            acc = acc + jnp.dot(yn, w_ref[sl, :], preferred_element_type=F32)
        o_ref[...] = acc

    yspec = pl.BlockSpec((tm, GROUP), lambda i: (i, 0))
    return pl.pallas_call(
        body, grid=(s // tm,),
        in_specs=[yspec, yspec, yspec, yspec, pl.BlockSpec((1, d), lambda i: (0, 0)), pl.BlockSpec((d, d), lambda i: (0, 0)),
                  pl.BlockSpec((tm, d), lambda i: (i, 0))],
        out_specs=[pl.BlockSpec((tm, d), lambda i: (i, 0)), pl.BlockSpec((tm, d), lambda i: (i, 0))],
        out_shape=[SDS((s, d), F32), SDS((s, d), _MXU)], compiler_params=_cp("parallel"), name=name)(*ys, g, w, x)


def _out_proj_bwd(dx, w, ys, g, name):
    s, d = dx.shape
    tm = min(512, s)

    def body(dx_ref, w_ref, ya, yb, yc, yd, g_ref, da, db, dc, dd, dg_ref):
        @pl.when(pl.program_id(0) == 0)
        def _():
            dg_ref[...] = jnp.zeros_like(dg_ref)

        dyn = _dot_nt(dx_ref[...], w_ref[...])
        outs = (da, db, dc, dd)
        for i, y_ref in enumerate((ya, yb, yc, yd)):
            sl = slice(GROUP * i, GROUP * (i + 1))
            dy, dg = _rms_bwd(y_ref[...], g_ref[:, sl], dyn[:, sl])
            outs[i][...] = dy
            dg_ref[:, sl] += dg

    yspec = pl.BlockSpec((tm, GROUP), lambda i: (i, 0))
    return pl.pallas_call(
        body, grid=(s // tm,),
        in_specs=[pl.BlockSpec((tm, d), lambda i: (i, 0)), pl.BlockSpec((d, d), lambda i: (0, 0)), yspec, yspec, yspec, yspec,
                  pl.BlockSpec((1, d), lambda i: (0, 0))],
        out_specs=[yspec, yspec, yspec, yspec, pl.BlockSpec((1, d), lambda i: (0, 0))],
        out_shape=[SDS((s, GROUP), F32)] * 4 + [SDS((1, d), F32)],
        compiler_params=_cp("arbitrary"), name=name)(dx, w, *ys, g)


FF_BLOCK = 512
FF_ROWS = 1024


def _ffn_fwd(x, g, wu, wd, name, comm=None):
    s, d = x.shape
    nj = wu.shape[0]
    tm = min(FF_ROWS, s)
    ni = s // tm

    def body(*refs):
        (x_ref, g_ref, wu_ref, wd_ref), (o_ref, u_ref, h_ref), (acc,), cc = _split_refs(refs, 4, 3, comm)
        i, j = pl.program_id(0), pl.program_id(1)
        _host_gather(comm, cc, i * nj + j, ni * nj)

        @pl.when(j == 0)
        def _():
            h_ref[...] = _rms(x_ref[...], g_ref[...]).astype(h_ref.dtype)
            acc[...] = jnp.zeros_like(acc)

        halves = [slice(r, r + tm // 2) for r in range(0, tm, tm // 2)]
        us = [jnp.dot(h_ref[r, :], wu_ref[0], preferred_element_type=F32) for r in halves]
        for r, u in zip(halves, us):
            u_ref[r, :] = u.astype(u_ref.dtype)
            acc[r, :] += _dot(jnp.square(jnp.maximum(u, 0.0)), wd_ref[...])

        @pl.when(j == nj - 1)
        def _():
            o_ref[...] = x_ref[...] + acc[...]

        if comm is not None:
            @pl.when((i == ni - 1) & (j == nj - 1))
            def _():
                comm.wait(*cc)

    in_specs = [pl.BlockSpec((tm, d), lambda i, j: (i, 0)), pl.BlockSpec((1, d), lambda i, j: (0, 0)),
                pl.BlockSpec((1, d, FF_BLOCK), lambda i, j: (j, 0, 0)), pl.BlockSpec((FF_BLOCK, d), lambda i, j: (j, 0))]
    out_specs = [pl.BlockSpec((tm, d), lambda i, j: (i, 0)), pl.BlockSpec((tm, FF_BLOCK), lambda i, j: (i, j)),
                 pl.BlockSpec((tm, d), lambda i, j: (i, 0))]
    out_shape = [SDS((s, d), F32), SDS((s, nj * FF_BLOCK), _MXU), SDS((s, d), _MXU)]
    return _call_with_comm(body, (ni, nj), in_specs, out_specs, out_shape, [pltpu.VMEM((tm, d), F32)], [x, g, wu, wd], comm,
                           ("arbitrary", "arbitrary"), name)


def _ffn_bwd(dx2, x, u, g, wu, wd, name, comm=None):
    s, d = x.shape
    nj = wu.shape[0]
    tm = min(FF_ROWS, s)
    ni = s // tm

    def body(*refs):
        (dx_ref, x_ref, u_ref, g_ref, wu_ref, wd_ref), (o_ref, du_ref, dg_ref), (acc, dxb), cc = _split_refs(refs, 6, 3, comm)
        i, j = pl.program_id(0), pl.program_id(1)

        @pl.when((i == 0) & (j == 0))
        def _():
            if comm is not None:
                comm.start(*cc)
            dg_ref[...] = jnp.zeros_like(dg_ref)

        @pl.when(j == 0)
        def _():
            dxb[...] = dx_ref[...].astype(dxb.dtype)
            acc[...] = jnp.zeros_like(acc)

        nt = (((1,), (1,)), ((), ()))
        halves = [slice(r, r + tm // 2) for r in range(0, tm, tm // 2)]
        das = [lax.dot_general(dxb[r, :], wd_ref[...], nt, preferred_element_type=F32) for r in halves]
        for r, da in zip(halves, das):
            du = (da * 2.0 * jnp.maximum(u_ref[r, :].astype(F32), 0.0)).astype(du_ref.dtype)
            du_ref[r, :] = du
            acc[r, :] += lax.dot_general(du, wu_ref[0], nt, preferred_element_type=F32)

        @pl.when(j == nj - 1)
        def _():
            dxn, dg = _rms_bwd(x_ref[...], g_ref[...], acc[...])
            o_ref[...] = dx_ref[...] + dxn
            dg_ref[...] += dg

        if comm is not None:
            @pl.when((i == ni - 1) & (j == nj - 1))
            def _():
                comm.wait(*cc)

    in_specs = [pl.BlockSpec((tm, d), lambda i, j: (i, 0)), pl.BlockSpec((tm, d), lambda i, j: (i, 0)),
                pl.BlockSpec((tm, FF_BLOCK), lambda i, j: (i, j)), pl.BlockSpec((1, d), lambda i, j: (0, 0)),
                pl.BlockSpec((1, d, FF_BLOCK), lambda i, j: (j, 0, 0)), pl.BlockSpec((FF_BLOCK, d), lambda i, j: (j, 0))]
    out_specs = [pl.BlockSpec((tm, d), lambda i, j: (i, 0)), pl.BlockSpec((tm, FF_BLOCK), lambda i, j: (i, j)),
                 pl.BlockSpec((1, d), lambda i, j: (0, 0))]
    out_shape = [SDS((s, d), F32), SDS((s, nj * FF_BLOCK), _MXU), SDS((1, d), F32)]
    return _call_with_comm(body, (ni, nj), in_specs, out_specs, out_shape,
                           [pltpu.VMEM((tm, d), F32), pltpu.VMEM((tm, d), _MXU)], [dx2, x, u, g, wu, wd], comm,
                           ("arbitrary", "arbitrary"), name)


def _in_proj_bwd(dz, w, x, g, dx_up, name, comm=None):
    s, d = x.shape
    n = w.shape[1]
    tm = min(512, s)
    ni = s // tm

    def body(*refs):
        (dz_ref, w_ref, x_ref, g_ref, up_ref), (o_ref, dg_ref), _, cc = _split_refs(refs, 5, 2, comm)
        i = pl.program_id(0)

        @pl.when(i == 0)
        def _():
            if comm is not None:
                comm.start(*cc)
            dg_ref[...] = jnp.zeros_like(dg_ref)

        dh = lax.dot_general(dz_ref[...], w_ref[...], (((1,), (1,)), ((), ())), preferred_element_type=F32)
        dxn, dg = _rms_bwd(x_ref[...], g_ref[...], dh)
        o_ref[...] = up_ref[...] + dxn
        dg_ref[...] += dg
        if comm is not None:
            @pl.when(i == ni - 1)
            def _():
                comm.wait(*cc)

    in_specs = [pl.BlockSpec((tm, n), lambda i: (i, 0)), pl.BlockSpec((d, n), lambda i: (0, 0)),
                pl.BlockSpec((tm, d), lambda i: (i, 0)), pl.BlockSpec((1, d), lambda i: (0, 0)),
                pl.BlockSpec((tm, d), lambda i: (i, 0))]
    out_specs = [pl.BlockSpec((tm, d), lambda i: (i, 0)), pl.BlockSpec((1, d), lambda i: (0, 0))]
    out_shape = [SDS((s, d), F32), SDS((1, d), F32)]
    return _call_with_comm(body, (ni,), in_specs, out_specs, out_shape, [], [dz, w, x, g, dx_up], comm, ("arbitrary",), name)


def _loss_head(x, g, target, name):
    s, d = x.shape
    tm = min(512, s)

    def body(x_ref, g_ref, t_ref, l_ref, dx_ref, dg_ref):
        @pl.when(pl.program_id(0) == 0)
        def _():
            l_ref[...] = jnp.zeros_like(l_ref)
            dg_ref[...] = jnp.zeros_like(dg_ref)

        xv = x_ref[...]
        err = _rms(xv, g_ref[...]) - t_ref[...]
        l_ref[...] += jnp.sum(err * err, axis=0, keepdims=True) * (0.5 / d)
        dx, dg = _rms_bwd(xv, g_ref[...], err * (1.0 / d))
        dx_ref[...] = dx
        dg_ref[...] += dg

    return pl.pallas_call(
        body, grid=(s // tm,),
        in_specs=[pl.BlockSpec((tm, d), lambda i: (i, 0)), pl.BlockSpec((1, d), lambda i: (0, 0)),
                  pl.BlockSpec((tm, d), lambda i: (i, 0))],
        out_specs=[pl.BlockSpec((1, d), lambda i: (0, 0)), pl.BlockSpec((tm, d), lambda i: (i, 0)),
                   pl.BlockSpec((1, d), lambda i: (0, 0))],
        out_shape=[SDS((1, d), F32), SDS((s, d), F32), SDS((1, d), F32)], compiler_params=_cp("arbitrary"), name=name)(x, g, target)


def _me_and_peer():
    x, y, c = lax.axis_index("x"), lax.axis_index("y"), lax.axis_index("c")
    me = 4 * x + 2 * y + c

    def peer(k):
        px, py, pc = x ^ (k >> 2), y ^ ((k >> 1) & 1), c ^ (k & 1)
        return (px, py, pc), 4 * px + 2 * py + pc

    return me, peer


class _Comm:
    CHIPS = (2, 4, 6)

    def __init__(self, kind, arrs):
        assert kind in ("gather", "exchange")
        self.kind, self.arrs, self.n = kind, list(arrs), len(arrs)
        anyspec = pl.BlockSpec(memory_space=pl.ANY)
        self.in_specs = [anyspec] * self.n
        self.out_specs = [anyspec] * self.n
        self.out_shape = [SDS(((NDEV,) + a.shape) if kind == "gather" else a.shape, a.dtype) for a in self.arrs]
        npair = NDEV - 1 + len(self.CHIPS)
        self.scratch = [pltpu.SemaphoreType.DMA((self.n, npair)), pltpu.SemaphoreType.DMA((self.n, npair)),
                        pltpu.SemaphoreType.DMA((self.n,))]

    def _copies(self, ins, outs, sems):
        send, recv, loc = sems
        me, peer = _me_and_peer()
        gather = self.kind == "gather"
        sibling = peer(1)[0]
        local = [pltpu.make_async_copy(ins[a] if gather else ins[a].at[me], outs[a].at[me], loc.at[a]) for a in range(self.n)]
        outgoing, incoming, forwards, forwarded = [], [], [], []
        for k in ((1,) + self.CHIPS) if gather else range(1, NDEV):
            dev, pid = peer(k)
            for a in range(self.n):
                pair = dict(send_sem=send.at[a, k - 1], recv_sem=recv.at[a, k - 1], device_id=dev, device_id_type=MESH)
                outgoing.append(pltpu.make_async_remote_copy(src_ref=ins[a] if gather else ins[a].at[pid],
                                                             dst_ref=outs[a].at[me], **pair))
                incoming.append(pltpu.make_async_remote_copy(src_ref=ins[a] if gather else ins[a].at[me],
                                                             dst_ref=outs[a].at[pid], **pair))
        if gather:
            for idx, k in enumerate(self.CHIPS):
                got, theirs = peer(k)[1], peer(k + 1)[1]
                for a in range(self.n):
                    pair = dict(send_sem=send.at[a, NDEV - 1 + idx], recv_sem=recv.at[a, NDEV - 1 + idx], device_id=sibling,
                                device_id_type=MESH)
                    forwards.append(pltpu.make_async_remote_copy(src_ref=outs[a].at[got], dst_ref=outs[a].at[got], **pair))
                    forwarded.append(pltpu.make_async_remote_copy(src_ref=outs[a].at[theirs], dst_ref=outs[a].at[theirs], **pair))
        return local, outgoing, incoming, forwards, forwarded

    def start(self, ins, outs, sems):
        local, outgoing, _, _, _ = self._copies(ins, outs, sems)
        for cp in local + outgoing:
            cp.start()

    def forward(self, ins, outs, sems):
        _, _, incoming, forwards, _ = self._copies(ins, outs, sems)
        per = self.n
        for idx in range(len(forwards) // per if per else 0):
            for a in range(per):
                incoming[(1 + idx) * per + a].wait_recv()
                forwards[idx * per + a].start()

    def wait(self, ins, outs, sems):
        local, outgoing, incoming, forwards, forwarded = self._copies(ins, outs, sems)
        for cp in (incoming[:self.n] if self.kind == "gather" else incoming) + forwarded:
            cp.wait_recv()
        for cp in outgoing + forwards:
            cp.wait_send()
        for cp in local:
            cp.wait()


def _host_gather(comm, cc, step, nsteps):
    if comm is None:
        return

    @pl.when(step == 0)
    def _():
        comm.start(*cc)

    @pl.when(step == (2 * nsteps) // 3)
    def _():
        comm.forward(*cc)


def _split_refs(refs, n_in, n_out, comm):
    c = comm.n if comm is not None else 0
    ins, cin = refs[:n_in], refs[n_in:n_in + c]
    outs, cout = refs[n_in + c:n_in + c + n_out], refs[n_in + c + n_out:n_in + 2 * c + n_out]
    rest = refs[n_in + 2 * c + n_out:]
    scratch, csem = (rest[:len(rest) - 3], rest[len(rest) - 3:]) if c else (rest, ())
    return ins, outs, scratch, (cin, cout, csem)


def _comm_call(kind, arrs, name):
    comm = _Comm(kind, arrs)

    def body(*refs):
        _, _, _, c = _split_refs(refs, 0, 0, comm)
        comm.start(*c)
        if kind == "gather":
            comm.forward(*c)
        comm.wait(*c)

    return pl.pallas_call(body, in_specs=comm.in_specs, out_specs=comm.out_specs, out_shape=comm.out_shape,
                          scratch_shapes=comm.scratch, compiler_params=pltpu.CompilerParams(has_side_effects=True),
                          name=name)(*arrs)


def _all_gather(arrs, name):
    return _comm_call("gather", arrs, name)


def _exchange(arrs, name):
    return _comm_call("exchange", arrs, name)


def _sum_slots(parts, name):
    _, r, c = parts.shape
    tr = r if r <= 512 else 512

    def body(p_ref, o_ref):
        acc = p_ref[0].astype(F32)
        for q in range(1, NDEV):
            acc = acc + p_ref[q].astype(F32)
        o_ref[...] = acc

    return pl.pallas_call(
        body, grid=(r // tr,), in_specs=[pl.BlockSpec((NDEV, tr, c), lambda i: (0, i, 0))],
        out_specs=pl.BlockSpec((tr, c), lambda i: (i, 0)), out_shape=SDS((r, c), F32),
        compiler_params=_cp("parallel"), name=name)(parts)


def _adamw(g, w, m, v, name):
    r, c = w.shape
    parts = g.ndim == 3
    tr = r
    for cand in (512, 256, 128, 64, 32, 16, 8):
        if r > cand and r % cand == 0 and cand * c * 4 <= 2 * 1024 * 1024:
            tr = cand
            break
    bc1 = 1.0 / (1.0 - ADAM_B1 ** ADAM_STEP)
    bc2 = 1.0 / (1.0 - ADAM_B2 ** ADAM_STEP)

    def body(g_ref, w_ref, m_ref, v_ref, go_ref, d_ref, mo_ref, vo_ref):
        if parts:
            gv = g_ref[0].astype(F32)
            for q in range(1, NDEV):
                gv = gv + g_ref[q].astype(F32)
        else:
            gv = g_ref[...]
        mn = ADAM_B1 * m_ref[...] + (1.0 - ADAM_B1) * gv
        vn = ADAM_B2 * v_ref[...] + (1.0 - ADAM_B2) * (gv * gv)
        go_ref[...] = gv
        mo_ref[...] = mn
        vo_ref[...] = vn
        d_ref[...] = -ADAM_LR * ((mn * bc1) / (jnp.sqrt(vn * bc2) + ADAM_EPS) + ADAM_WD * w_ref[...])

    spec = pl.BlockSpec((tr, c), lambda i: (i, 0))
    gspec = pl.BlockSpec((NDEV, tr, c), lambda i: (0, i, 0)) if parts else spec
    return pl.pallas_call(
        body, grid=(r // tr,), in_specs=[gspec, spec, spec, spec], out_specs=[spec] * 4,
        out_shape=[SDS((r, c), F32)] * 4, compiler_params=_cp("parallel"), name=name)(g, w, m, v)


def _adamw_layer(parts, w, m, v, l, prev, name):
    depth, r, c = w.shape
    tr = next(t for t in (512, 256, 128, 64, 32, 16, 8) if r % t == 0 and t * c * 4 <= 2 * 1024 * 1024)
    bc1 = 1.0 / (1.0 - ADAM_B1 ** ADAM_STEP)
    bc2 = 1.0 / (1.0 - ADAM_B2 ** ADAM_STEP)

    def body(g_ref, w_ref, m_ref, v_ref, *rest):
        go_ref, d_ref, mo_ref, vo_ref = rest[-4:]
        gv = g_ref[0].astype(F32)
        for q in range(1, NDEV):
            gv = gv + g_ref[q].astype(F32)
        mn = ADAM_B1 * m_ref[0] + (1.0 - ADAM_B1) * gv
        vn = ADAM_B2 * v_ref[0] + (1.0 - ADAM_B2) * (gv * gv)
        go_ref[0] = gv
        mo_ref[0] = mn
        vo_ref[0] = vn
        d_ref[0] = -ADAM_LR * ((mn * bc1) / (jnp.sqrt(vn * bc2) + ADAM_EPS) + ADAM_WD * w_ref[0])

    spec = pl.BlockSpec((1, tr, c), lambda i: (l, i, 0))
    in_specs = [pl.BlockSpec((NDEV, tr, c), lambda i: (0, i, 0)), spec, spec, spec]
    args = [parts, w, m, v]
    aliases = {}
    if prev is not None:
        in_specs += [pl.BlockSpec(memory_space=pl.ANY)] * 4
        args += list(prev)
        aliases = {4 + k: k for k in range(4)}
    return pl.pallas_call(
        body, grid=(r // tr,), in_specs=in_specs, out_specs=[spec] * 4, out_shape=[SDS((depth, r, c), F32)] * 4,
        input_output_aliases=aliases, compiler_params=_cp("parallel"), name=name)(*args)


def _pad_in_cols(w):
    r = w.shape[0]
    zeros = lambda n: jnp.zeros((r, n), w.dtype)
    return jnp.concatenate([w[:, :2304], w[:, 2308:2692], w[:, 2304:2308], zeros(28), w[:, 2692:2724], zeros(64)], axis=1)


def _unpad_in_cols(w):
    return jnp.concatenate([w[..., :2304], w[..., 2688:2692], w[..., 2304:2688], w[..., 2720:2752]], axis=-1)


def _pad_uq(w):
    return jnp.pad(w.reshape(256, N_HEADS, 96), ((0, 0), (0, 0), (0, 32))).reshape(256, 512)


def _unpad_uq(w):
    return w.reshape(256, N_HEADS, 128)[:, :, :96].reshape(256, 384)


def _split_ukv(w):
    r = w.reshape(128, N_HEADS, 128)
    return jnp.pad(r[:, :, :64], ((0, 0), (0, 0), (0, 64))).reshape(128, 512), r[:, :, 64:].reshape(128, 256)


def _join_ukv(dk, dv):
    return jnp.concatenate([dk.reshape(128, N_HEADS, 128)[:, :, :64], dv.reshape(128, N_HEADS, 64)], axis=-1).reshape(128, 512)


def _cols_to_full(g):
    return jnp.transpose(g, (1, 0, 2)).reshape(g.shape[1], NDEV * g.shape[2])


def kernel(x, g_mix_norm, w_in, b_forget, g_sgu, w_spatial, b_spatial, g_mla_q, w_uq, g_mla_kv, w_ukv, g_group_out, w_out, g_ffn_norm, w_up, w_down, g_final, loss_target, m_g_mix_norm, m_w_in, m_b_forget, m_g_sgu, m_w_spatial, m_b_spatial, m_g_mla_q, m_w_uq, m_g_mla_kv, m_w_ukv, m_g_group_out, m_w_out, m_g_ffn_norm, m_w_up, m_w_down, m_g_final, v_g_mix_norm, v_w_in, v_b_forget, v_g_sgu, v_w_spatial, v_b_spatial, v_g_mla_q, v_w_uq, v_g_mla_kv, v_w_ukv, v_g_group_out, v_w_out, v_g_ffn_norm, v_w_up, v_w_down, v_g_final):
    depth = w_in.shape[0]
    s, d = x.shape[1], x.shape[2]
    x0 = x.reshape(s, d)
    target = loss_target.reshape(s, d)
    tb = _tables(s)
    me = 4 * lax.axis_index("x") + 2 * lax.axis_index("y") + lax.axis_index("c")

    assert depth == 2
    shards = {}
    for l in range(depth):
        shards.update({(l, "w_in"): _pad_in_cols(w_in[l]).astype(_WIRE), (l, "w_out"): w_out[l].astype(_WIRE),
                       (l, "w_up"): w_up[l].astype(_WIRE), (l, "w_down"): w_down[l].astype(_WIRE),
                       (l, "w_uq"): w_uq[l].astype(_WIRE), (l, "w_ukv"): w_ukv[l].astype(_WIRE)})
    wts = _ShardedWeights(shards)
    wts.full[(0, "w_in")] = _all_gather([shards[(0, "w_in")]], "gather_w_in0")[0]

    row = lambda a: a.reshape(1, -1)

    def small(l):
        bf = jnp.pad(b_forget[l].reshape(1, N_HEADS), ((0, 0), (0, 128 - N_HEADS)))
        bt = jnp.pad(b_spatial[l].T, ((0, 0), (0, 128 - N_HEADS)))
        return dict(g_mix=row(g_mix_norm[l]), g_sgu=row(g_sgu[l]), w_s=w_spatial[l], b_t=bt, b_f=bf, gq=row(g_mla_q[l]),
                    gkv=row(g_mla_kv[l]), g_go=row(g_group_out[l]), g_ffn=row(g_ffn_norm[l]))

    smalls = [small(l) for l in range(depth)]
    lrow, dx, sm, dg_final = _local_step(x0, target, wts, smalls, row(g_final), tb)
    loss = lax.psum(jnp.sum(lrow), AXES)
    grad_x = dx.reshape(1, s, d)
    return _reduce_and_update(loss, grad_x, wts.recv, sm, dg_final, me, dict(
        g_mix_norm=(g_mix_norm, m_g_mix_norm, v_g_mix_norm), w_in=(w_in, m_w_in, v_w_in),
        b_forget=(b_forget, m_b_forget, v_b_forget), g_sgu=(g_sgu, m_g_sgu, v_g_sgu),
        w_spatial=(w_spatial, m_w_spatial, v_w_spatial), b_spatial=(b_spatial, m_b_spatial, v_b_spatial),
        g_mla_q=(g_mla_q, m_g_mla_q, v_g_mla_q), w_uq=(w_uq, m_w_uq, v_w_uq), g_mla_kv=(g_mla_kv, m_g_mla_kv, v_g_mla_kv),
        w_ukv=(w_ukv, m_w_ukv, v_w_ukv), g_group_out=(g_group_out, m_g_group_out, v_g_group_out),
        w_out=(w_out, m_w_out, v_w_out), g_ffn_norm=(g_ffn_norm, m_g_ffn_norm, v_g_ffn_norm), w_up=(w_up, m_w_up, v_w_up),
        w_down=(w_down, m_w_down, v_w_down), g_final=(g_final, m_g_final, v_g_final)))


_GATHER_AT = {
    "in_proj0": [(0, "w_up")],
    "fox_attn0": [(0, "w_uq"), (0, "w_ukv"), (0, "w_down")],
    "mla_attn0": [(0, "w_out"), (1, "w_in")],
    "ffn_fwd0": [(1, "w_uq"), (1, "w_ukv"), (1, "w_down")],
    "fox_attn1": [(1, "w_out")],
    "mla_attn1": [(1, "w_up")],
}
_SCATTER_AT = {
    "fox_attn_bwd1": [(1, "w_down")],
    "mla_attn_bwd1": [(1, "w_up"), (1, "w_out")],
    "ffn_bwd0": [(1, "w_in")],
    "fox_attn_bwd0": [(0, "w_down")],
    "mla_attn_bwd0": [(0, "w_up"), (0, "w_out")],
    "in_proj_bwd0": [(0, "w_in")],
}


class _FullWeights:
    def __init__(self, per_layer):
        self.per_layer, self.grads = per_layer, {}

    def get(self, l, name):
        return self.per_layer[l][name]

    def comm(self, host):
        return None

    def done(self, host, results):
        pass

    def grad(self, l, name, blocks):
        self.grads[(l, name)] = blocks


class _ShardedWeights(_FullWeights):
    def __init__(self, shards):
        self.shards, self.full, self.grads, self.recv = shards, {}, {}, {}

    def get(self, l, name):
        if name in ("wk", "wv"):
            return _split_ukv(_cols_to_full(self.full[(l, "w_ukv")]))[0 if name == "wk" else 1]
        if name == "wq":
            return _pad_uq(_cols_to_full(self.full[(l, "w_uq")]))
        g = self.full[(l, name)]
        return g if name == "w_up" else g.reshape(NDEV * g.shape[1], g.shape[2])

    def comm(self, host):
        if host in _GATHER_AT:
            return _Comm("gather", [self.shards[k] for k in _GATHER_AT[host]])
        if host in _SCATTER_AT:
            return _Comm("exchange", [self.grads[k] for k in _SCATTER_AT[host]])
        return None

    def done(self, host, results):
        if host in _GATHER_AT:
            self.full.update(zip(_GATHER_AT[host], results))
        if host in _SCATTER_AT:
            self.recv.update(zip(_SCATTER_AT[host], results))


def _local_step(x0, target, wts, smalls, g_final, tb):
    depth = len(smalls)
    s, d = x0.shape
    saved = []
    xl = x0
    for l in range(depth):
        p = smalls[l]
        z, h, got = _norm_matmul(xl, p["g_mix"], wts.get(l, "w_in"), f"in_proj{l}", wts.comm(f"in_proj{l}"))
        wts.done(f"in_proj{l}", got)
        ya = _sgu_fwd(z, p["g_sgu"], p["w_s"], p["b_t"], tb, f"sgu_fwd{l}")
        yb, ret, states = _ret_fwd(z, tb, f"ret_fwd{l}")
        cum = _fox_prep(z, p["b_f"], f"fox_prep{l}")
        kc, vc, vtc = _kv_prep(z, 7, 8, f"fox_kv{l}")
        yc, lse_c, got = _attn_fwd(z, 6, HEAD_DIM, kc, vtc, HEAD_DIM ** -0.5, cum, f"fox_attn{l}", wts.comm(f"fox_attn{l}"))
        wts.done(f"fox_attn{l}", got)
        wq, wk, wv = wts.get(l, "wq"), wts.get(l, "wk"), wts.get(l, "wv")
        qd, kd, vd, vtd, cqn, ckvn = _mla_prep(z, p["gq"], p["gkv"], wq, wk, wv, tb, f"mla_prep{l}")
        yd, lse_d, got = _attn_fwd(qd, 0, 128, kd, vtd, _SCALE_D, None, f"mla_attn{l}", wts.comm(f"mla_attn{l}"))
        wts.done(f"mla_attn{l}", got)
        ys = (ya, yb, yc, yd)
        x1, yn = _out_proj(ys, p["g_go"], wts.get(l, "w_out"), xl, f"out_proj{l}")
        x2, u, h2, got = _ffn_fwd(x1, p["g_ffn"], wts.get(l, "w_up"), wts.get(l, "w_down"), f"ffn_fwd{l}", wts.comm(f"ffn_fwd{l}"))
        wts.done(f"ffn_fwd{l}", got)
        saved.append(dict(x=xl, z=z, h=h, ys=ys, ret=ret, states=states, cum=cum, lse_c=lse_c, kc=kc, vc=vc, qd=qd, kd=kd, vd=vd,
                          cqn=cqn, ckvn=ckvn, lse_d=lse_d, x1=x1, yn=yn, u=u, h2=h2, wq=wq, wk=wk, wv=wv))
        xl = x2

    lrow, dx, dg_final = _loss_head(xl, g_final, target, "loss_head")

    sm = [None] * depth
    for l in reversed(range(depth)):
        p, a = smalls[l], saved[l]
        dx1, du, dg_ffn, got = _ffn_bwd(dx, a["x1"], a["u"], p["g_ffn"], wts.get(l, "w_up"), wts.get(l, "w_down"), f"ffn_bwd{l}",
                                        wts.comm(f"ffn_bwd{l}"))
        wts.done(f"ffn_bwd{l}", got)
        dw_down = _mm_tn(a["u"], dx, f"dw_down{l}", a_fn=lambda t: jnp.square(jnp.maximum(t, 0.0)), out_dtype=_WIRE)
        wts.grad(l, "w_down", dw_down.reshape(NDEV, dw_down.shape[0] // NDEV, d))
        wts.grad(l, "w_up", _mm_tn(a["h2"], du, f"dw_up{l}", blocked=True, out_dtype=_WIRE))
        dya, dyb, dyc, dyd, dg_go = _out_proj_bwd(dx1, wts.get(l, "w_out"), a["ys"], p["g_go"], f"out_proj_bwd{l}")
        wts.grad(l, "w_out", _mm_tn(a["yn"], dx1, f"dw_out{l}", out_dtype=_WIRE).reshape(NDEV, d // NDEV, d))
        dz_a, dg_sgu, dw_s, db_t = _sgu_bwd(dya, a["z"], p["g_sgu"], p["w_s"], p["b_t"], tb, f"sgu_bwd{l}")
        dz_b = _ret_bwd(dyb, a["z"], a["ret"], a["states"], tb, f"ret_bwd{l}")
        qt, dot, dl = _attn_bwd_prep(a["z"], 6, HEAD_DIM, HEAD_DIM ** -0.5, a["ys"][2], dyc, f"fox_bwd_prep{l}")
        dqt_c, dk_c, dv_c, dck, dcq, got = _attn_bwd(a["kc"], a["vc"], qt, dot, a["lse_c"], dl, HEAD_DIM,
                                                     HEAD_DIM ** -0.5, a["cum"], f"fox_attn_bwd{l}", _MXU,
                                                     wts.comm(f"fox_attn_bwd{l}"))
        wts.done(f"fox_attn_bwd{l}", got)
        dq_c = _untranspose(dqt_c, _MXU, f"fox_dq{l}")
        qt, dot, dl = _attn_bwd_prep(a["qd"], 0, 128, _SCALE_D, a["ys"][3], dyd, f"mla_bwd_prep{l}")
        dqt_d, dk_d, dv_d, got = _attn_bwd(a["kd"], a["vd"], qt, dot, a["lse_d"], dl, 128, _SCALE_D, None,
                                           f"mla_attn_bwd{l}", F32, wts.comm(f"mla_attn_bwd{l}"))
        wts.done(f"mla_attn_bwd{l}", got)
        dq_d = _untranspose(dqt_d, F32, f"mla_dq{l}")
        dz_cq, dz_ckv, dkr, dwq, dwk, dwv, dgq, dgkv = _mla_prep_bwd(dq_d, dk_d, dv_d, a["z"], a["cqn"], a["ckvn"], p["gq"],
                                                                     p["gkv"], a["wq"], a["wk"], a["wv"], tb, f"mla_prep_bwd{l}")
        dz_misc, db_f = _fox_post(dcq, dck, a["z"], p["b_f"], dkr, f"fox_post{l}")
        dz = jnp.concatenate([dz_a, dz_b, dq_c, dk_c, dv_c, dz_cq, dz_ckv, dz_misc], axis=1)
        wts.grad(l, "w_in", _unpad_in_cols(_mm_tn(a["h"], dz, f"dw_in{l}", out_dtype=_WIRE)).reshape(NDEV, d // NDEV, N_IN))
        dx, dg_mix, got = _in_proj_bwd(dz, wts.get(l, "w_in"), a["x"], p["g_mix"], dx1, f"in_proj_bwd{l}",
                                       wts.comm(f"in_proj_bwd{l}"))
        wts.done(f"in_proj_bwd{l}", got)
        sm[l] = [dg_mix, dg_go, dg_ffn, dg_sgu, dw_s, db_t[:, :N_HEADS].T, db_f[0, :N_HEADS], dgq, dgkv, _unpad_uq(dwq),
                 _join_ukv(dwk, dwv)]
    return lrow, dx, sm, dg_final


def _reduce_and_update(loss, grad_x, recv, sm, dg_final, me, given):
    depth = len(sm)
    pieces = [t for l in range(depth) for t in sm[l]] + [dg_final]
    flat = jnp.concatenate([t.reshape(-1) for t in pieces])
    n_flat = flat.shape[0]
    unit = NDEV * 8 * 128
    n_pad = -(-n_flat // unit) * unit
    packed = jnp.pad(flat, (0, n_pad - n_flat)).reshape(NDEV, n_pad // (NDEV * 128), 128)
    red = _sum_slots(_exchange([packed], "scatter_small")[0], "sum_small")
    full = _all_gather([red], "gather_small")[0].reshape(-1)
    offs = np.cumsum([0] + [int(np.prod(t.shape)) for t in pieces])
    red_pieces = [full[int(offs[i]):int(offs[i + 1])].reshape(pieces[i].shape) for i in range(len(pieces))]
    per = len(sm[0])
    stack = lambda i: jnp.stack([red_pieces[l * per + i] for l in range(depth)])
    g_small = dict(g_mix_norm=stack(0), g_group_out=stack(1), g_ffn_norm=stack(2), g_sgu=stack(3), w_spatial=stack(4),
                   b_spatial=stack(5), b_forget=stack(6), g_mla_q=stack(7), g_mla_kv=stack(8), g_final=red_pieces[-1])
    cq, ckv = given["w_uq"][0].shape[2], given["w_ukv"][0].shape[2]
    g_small["w_uq"] = lax.dynamic_slice_in_dim(stack(9), me * cq, cq, axis=2)
    g_small["w_ukv"] = lax.dynamic_slice_in_dim(stack(10), me * ckv, ckv, axis=2)

    names = list(given)
    outs = {}
    for nme in names:
        wv_, mv_, vv_ = given[nme]
        shape = wv_.shape
        if nme in ("w_in", "w_out", "w_up", "w_down"):
            res = None
            for l in range(depth):
                res = _adamw_layer(recv[(l, nme)], wv_, mv_, vv_, l, res, f"adamw_{nme}{l}")
            outs[nme] = list(res)
        else:
            two = lambda t: t.reshape(-1, shape[-1]) if t.ndim > 1 else t.reshape(1, -1)
            res = _adamw(two(g_small[nme]), two(wv_), two(mv_), two(vv_), f"adamw_{nme}")
            outs[nme] = [r.reshape(shape) for r in res]
    return (loss, grad_x, *[outs[n][0] for n in names], *[outs[n][1] for n in names], *[outs[n][2] for n in names],
            *[outs[n][3] for n in names])
```

```python
import functools

import jax
import jax.numpy as jnp
import numpy as np
from jax import lax
from jax.experimental import pallas as pl
from jax.experimental.pallas import tpu as pltpu

F32 = jnp.float32
_MXU = jnp.bfloat16
_WIRE = jnp.bfloat16
EPS = 1e-6
NDEV = 8
AXES = ("x", "y", "c")
MESH = pl.DeviceIdType.MESH

N_HEADS = 4
HEAD_DIM = 64
GROUP = 256
CHUNK = 128
NZ = 2816
N_IN = 2724
MISC_F, MISC_KR = 0, 32
VMEM_LIMIT = 56 * 1024 * 1024

ADAM_LR, ADAM_B1, ADAM_B2, ADAM_EPS, ADAM_WD, ADAM_STEP = 0.001, 0.9, 0.999, 1e-08, 0.01, 10

SDS = jax.ShapeDtypeStruct


def _cp(*sem):
    return pltpu.CompilerParams(dimension_semantics=sem, vmem_limit_bytes=VMEM_LIMIT)


def _dot(a, b):
    return jnp.dot(a.astype(_MXU), b.astype(_MXU), preferred_element_type=F32)


def _dot_nt(a, b):
    return lax.dot_general(a.astype(_MXU), b.astype(_MXU), (((1,), (1,)), ((), ())), preferred_element_type=F32)


def _dot_tn(a, b):
    return lax.dot_general(a.astype(_MXU), b.astype(_MXU), (((0,), (0,)), ((), ())), preferred_element_type=F32)


def _dot_exact(a, b, dims=(((1,), (0,)), ((), ()))):
    return lax.dot_general(a, b, dims, precision=lax.Precision.HIGHEST, preferred_element_type=F32)


def _rms(x, g):
    return x * lax.rsqrt(jnp.mean(x * x, axis=-1, keepdims=True) + EPS) * g


def _rms_bwd(x, g, dy):
    xh = x * lax.rsqrt(jnp.mean(x * x, axis=-1, keepdims=True) + EPS)
    dxh = dy * g
    r = lax.rsqrt(jnp.mean(x * x, axis=-1, keepdims=True) + EPS)
    dx = r * (dxh - xh * jnp.mean(dxh * xh, axis=-1, keepdims=True))
    return dx, jnp.sum(dy * xh, axis=0, keepdims=True)


def _standardize(t):
    mu = jnp.mean(t, axis=-1, keepdims=True)
    tc = t - mu
    rs = lax.rsqrt(jnp.mean(tc * tc, axis=-1, keepdims=True) + EPS)
    return tc * rs, rs


def _standardize_bwd(yh, rs, dy):
    return rs * (dy - jnp.mean(dy, axis=-1, keepdims=True) - yh * jnp.mean(dy * yh, axis=-1, keepdims=True))


_GELU_C = 0.7978845608028654


def _gelu(x):
    return 0.5 * x * (1.0 + jnp.tanh(_GELU_C * (x + 0.044715 * x * x * x)))


def _gelu_grad(x):
    t = jnp.tanh(_GELU_C * (x + 0.044715 * x * x * x))
    return 0.5 * (1.0 + t) + 0.5 * x * (1.0 - t * t) * _GELU_C * (1.0 + 3 * 0.044715 * x * x)


def _sigmoid(x):
    return 1.0 / (1.0 + jnp.exp(-x))


def _swap_half(t, half):
    n = t.shape[-1]
    lane = lax.broadcasted_iota(jnp.int32, t.shape, t.ndim - 1)
    return jnp.where((lane % (2 * half)) < half, pltpu.roll(t, n - half, t.ndim - 1), pltpu.roll(t, half, t.ndim - 1))


def _lanes(table, width):
    return jnp.concatenate([table] * (width // table.shape[-1]), axis=-1)


def _rope(t, cos, sin, half):
    return t * cos + _swap_half(t, half) * sin


def _rope_bwd(d, cos, sin, half):
    return d * cos - _swap_half(d, half) * sin


def _tables(s):
    pos = jnp.arange(s, dtype=F32)[:, None]

    def cs(half):
        inv = jnp.power(10000.0, -jnp.arange(half, dtype=F32) / half)
        ang = pos * inv[None, :]
        return jnp.cos(ang), jnp.sin(ang)

    c32, s32 = cs(32)
    c16, s16 = cs(16)
    z = lambda w: jnp.zeros((s, w), F32)
    o = lambda w: jnp.ones((s, w), F32)
    t = {}
    t["b_cos"] = jnp.concatenate([c32, c32, c32, c32], 1)
    t["b_sin"] = jnp.concatenate([-s32, s32, -s32, s32], 1)
    t["q_cos"] = jnp.concatenate([o(64), c16, c16, z(32)], 1)
    t["q_sin"] = jnp.concatenate([z(64), -s16, s16, z(32)], 1)
    t["k_cos"] = jnp.concatenate([z(32), c16, c16, z(64)], 1)
    t["k_sin"] = jnp.concatenate([z(32), -s16, s16, z(64)], 1)
    lg = jnp.log1p(-jnp.exp2(-5.0 - jnp.arange(N_HEADS, dtype=F32)))
    j = jnp.arange(CHUNK, dtype=F32)
    rel = j[:, None] - j[None, :]
    t["decay"] = jnp.where(rel[None] >= 0, jnp.exp(jnp.maximum(rel, 0.0)[None] * lg[:, None, None]), 0.0)
    t["decay_t"] = jnp.swapaxes(t["decay"], 1, 2)

    def rows(e):
        return jnp.repeat(e.T, HEAD_DIM, axis=1)

    t["qw"] = rows(jnp.exp((j + 1.0)[None, :] * lg[:, None]))
    t["kw"] = rows(jnp.exp((CHUNK - 1 - j)[None, :] * lg[:, None]))
    t["kw2"] = rows(jnp.exp((CHUNK - j)[None, :] * lg[:, None]))
    t["qw0"] = rows(jnp.exp(j[None, :] * lg[:, None]))
    t["cd"] = jnp.repeat(jnp.exp(CHUNK * lg), HEAD_DIM)[None, :]
    e = np.zeros((128, 512), np.float32)
    for h in range(N_HEADS):
        for r in range(32):
            e[MISC_KR + r, 128 * h + 64 + r] = 1.0
    t["place"] = jnp.asarray(e)
    lane_head = np.arange(GROUP) // HEAD_DIM
    t["grp"] = jnp.asarray((lane_head[:, None] == lane_head[None, :]) / HEAD_DIM, _MXU)
    hsel = (np.arange(128)[:, None] == lane_head[None, :]).astype(np.float32)
    t["hsel"] = jnp.asarray(hsel)
    t["hselt"] = jnp.asarray(hsel.T, _MXU)
    return t


def _norm_matmul(x, g, w, name, comm=None):
    s, d = x.shape
    n = w.shape[1]
    tm, tn = min(512, s), 256
    ni = s // tm

    def body(*refs):
        (x_ref, g_ref, w_ref), (z_ref, h_ref), _, cc = _split_refs(refs, 3, 2, comm)
        i = pl.program_id(0)
        _host_gather(comm, cc, i, ni, late=True)
        h = _rms(x_ref[...], g_ref[...]).astype(h_ref.dtype)
        h_ref[...] = h
        for j in range(n // tn):
            z_ref[:, tn * j:tn * (j + 1)] = jnp.dot(h, w_ref[:, tn * j:tn * (j + 1)], preferred_element_type=F32)
        if comm is not None:
            @pl.when(i == ni - 1)
            def _():
                comm.wait(*cc)

    in_specs = [pl.BlockSpec((tm, d), lambda i: (i, 0)), pl.BlockSpec((1, d), lambda i: (0, 0)),
                pl.BlockSpec((d, n), lambda i: (0, 0))]
    out_specs = [pl.BlockSpec((tm, n), lambda i: (i, 0)), pl.BlockSpec((tm, d), lambda i: (i, 0))]
    out_shape = [SDS((s, n), F32), SDS((s, d), _MXU)]
    return _call_with_comm(body, (ni,), in_specs, out_specs, out_shape, [], [x, g, w], comm, ("arbitrary",), name)


def _mm_tn(a, b, name, *, a_fn=None, blocked=False, out_dtype=F32):
    k, m = a.shape
    n = b.shape[1]
    tm, tk = min(1024, m), min(1024, k)
    tn = next(t for t in (1408, 1024, 512, 256, 128) if n % t == 0)
    assert m % tm == 0 and k % tk == 0
    nk = k // tk

    def body(a_ref, b_ref, o_ref, acc):
        kk = pl.program_id(2)

        @pl.when(kk == 0)
        def _():
            acc[...] = jnp.zeros_like(acc)

        av = a_ref[...]
        if a_fn is not None:
            av = a_fn(av.astype(F32))
        acc[...] += _dot_tn(av, b_ref[...])

        @pl.when(kk == nk - 1)
        def _():
            if blocked:
                for c in range(tn // 512):
                    o_ref[c] = acc[:, 512 * c:512 * (c + 1)].astype(o_ref.dtype)
            else:
                o_ref[...] = acc[...].astype(o_ref.dtype)

    if blocked:
        assert tn % 512 == 0
        out_spec = pl.BlockSpec((tn // 512, tm, 512), lambda i, j, kk: (j, i, 0))
        out_shape = SDS((n // 512, m, 512), out_dtype)
    else:
        out_spec = pl.BlockSpec((tm, tn), lambda i, j, kk: (i, j))
        out_shape = SDS((m, n), out_dtype)
    return pl.pallas_call(
        body, grid=(m // tm, n // tn, nk),
        in_specs=[pl.BlockSpec((tk, tm), lambda i, j, kk: (kk, i)), pl.BlockSpec((tk, tn), lambda i, j, kk: (kk, j))],
        out_specs=out_spec, out_shape=out_shape, scratch_shapes=[pltpu.VMEM((tm, tn), F32)],
        compiler_params=_cp("parallel", "parallel", "arbitrary"), name=name)(a, b)


def _split_dot(x, m):
    hi = x.astype(_MXU)
    lo = (x - hi.astype(F32)).astype(_MXU)
    return jnp.dot(hi, m, preferred_element_type=F32) + jnp.dot(lo, m, preferred_element_type=F32)


def _gstandardize(t, grp):
    tc = t - _split_dot(t, grp)
    rs = lax.rsqrt(_split_dot(tc * tc, grp) + EPS)
    return tc * rs, rs


def _gstandardize_bwd(yh, rs, dy, grp):
    return rs * (dy - _split_dot(dy, grp) - yh * _split_dot(dy * yh, grp))


def _head_select(parts):
    hid = lax.broadcasted_iota(jnp.int32, parts[0].shape, 1) // HEAD_DIM
    return jnp.where(hid == 0, parts[0], jnp.where(hid == 1, parts[1], jnp.where(hid == 2, parts[2], parts[3])))


def _head_masked(x):
    hid = lax.broadcasted_iota(jnp.int32, x.shape, 1) // HEAD_DIM
    return [jnp.where(hid == h, x, jnp.zeros_like(x)) for h in range(N_HEADS)]


def _tril(w):
    r = lax.broadcasted_iota(jnp.int32, w.shape, 0)
    c = lax.broadcasted_iota(jnp.int32, w.shape, 1)
    return jnp.where(r >= c, w, 0.0)


def _sgu_mixed(vgb, wcs, bias, nchunk):
    ms = [[jnp.dot(wcs[h], vgb[CHUNK * c:CHUNK * (c + 1)], preferred_element_type=F32) for h in range(N_HEADS)]
          for c in range(nchunk)]
    return [_head_select(ms[c]) + bias for c in range(nchunk)]


def _sgu_fwd(z, gain, w_s, b_t, tb, name):
    s = z.shape[0]
    tm = min(512, s)
    const = lambda a: pl.BlockSpec(a.shape, lambda i: (0,) * a.ndim)

    def body(u_ref, v_ref, g_ref, w_ref, b_ref, grp_ref, hsel_ref, y_ref):
        u = _gelu(u_ref[...])
        vh, _ = _gstandardize(_gelu(v_ref[...]), grp_ref[...])
        vgb = (vh * g_ref[...]).astype(_MXU)
        bias = _dot_exact(b_ref[...], hsel_ref[...])
        wcs = [_tril(w_ref[h]).astype(_MXU) for h in range(N_HEADS)]
        for c, mixed in enumerate(_sgu_mixed(vgb, wcs, bias, tm // CHUNK)):
            r = slice(CHUNK * c, CHUNK * (c + 1))
            y_ref[r, :] = u[r] * mixed

    return pl.pallas_call(
        body, grid=(s // tm,),
        in_specs=[pl.BlockSpec((tm, GROUP), lambda i: (i, 0)), pl.BlockSpec((tm, GROUP), lambda i: (i, 1)),
                  pl.BlockSpec((1, GROUP), lambda i: (0, 0)), pl.BlockSpec((N_HEADS, CHUNK, CHUNK), lambda i: (0, 0, 0)),
                  pl.BlockSpec((CHUNK, 128), lambda i: (0, 0)), const(tb["grp"]), const(tb["hsel"])],
        out_specs=pl.BlockSpec((tm, GROUP), lambda i: (i, 0)), out_shape=SDS((s, GROUP), F32),
        compiler_params=_cp("parallel"), name=name)(z, z, gain, w_s, b_t, tb["grp"], tb["hsel"])


def _sgu_bwd(dy, z, gain, w_s, b_t, tb, name):
    s = z.shape[0]
    tm = min(512, s)
    nchunk = tm // CHUNK
    const = lambda a: pl.BlockSpec(a.shape, lambda i: (0,) * a.ndim)

    def body(dy_ref, u_ref, v_ref, g_ref, w_ref, b_ref, grp_ref, hsel_ref, hselt_ref, dz_ref, dg_ref, dw_ref, db_ref):
        @pl.when(pl.program_id(0) == 0)
        def _():
            dg_ref[...] = jnp.zeros_like(dg_ref)
            dw_ref[...] = jnp.zeros_like(dw_ref)
            db_ref[...] = jnp.zeros_like(db_ref)

        grp = grp_ref[...]
        u_pre, v_pre, gain_v = u_ref[...], v_ref[...], g_ref[...]
        u = _gelu(u_pre)
        vh, rs = _gstandardize(_gelu(v_pre), grp)
        vgb = (vh * gain_v).astype(_MXU)
        dyv = dy_ref[...]
        bias = _dot_exact(b_ref[...], hsel_ref[...])
        wfs = [_tril(w_ref[h]) for h in range(N_HEADS)]
        wcs = [w.astype(_MXU) for w in wfs]
        wts = [w.T.astype(_MXU) for w in wfs]
        mixed = _sgu_mixed(vgb, wcs, bias, nchunk)
        gu = _gelu_grad(u_pre)
        dms, dmh = [], []
        for c in range(nchunk):
            r = slice(CHUNK * c, CHUNK * (c + 1))
            dz_ref[r, 0:GROUP] = (dyv[r] * mixed[c] * gu[r]).astype(dz_ref.dtype)
            dm = dyv[r] * u[r]
            dms.append(dm)
            dmh.append([m.astype(_MXU) for m in _head_masked(dm)])
        dws = [sum(lax.dot_general(dmh[c][h], vgb[CHUNK * c:CHUNK * (c + 1)], (((1,), (1,)), ((), ())),
                                   preferred_element_type=F32) for c in range(nchunk)) for h in range(N_HEADS)]
        dvg = jnp.concatenate([sum(jnp.dot(wts[h], dmh[c][h], preferred_element_type=F32) for h in range(N_HEADS))
                               for c in range(nchunk)], axis=0)
        for h in range(N_HEADS):
            dw_ref[h] += _tril(dws[h])
        db_ref[...] += sum(_split_dot(dm, hselt_ref[...]) for dm in dms)
        dg_ref[...] += jnp.sum(dvg * vh, axis=0, keepdims=True)
        dv = _gstandardize_bwd(vh, rs, dvg * gain_v, grp)
        dz_ref[:, GROUP:2 * GROUP] = (dv * _gelu_grad(v_pre)).astype(dz_ref.dtype)

    consts = [tb["grp"], tb["hsel"], tb["hselt"]]
    return pl.pallas_call(
        body, grid=(s // tm,),
        in_specs=[pl.BlockSpec((tm, GROUP), lambda i: (i, 0)),
                  pl.BlockSpec((tm, GROUP), lambda i: (i, 0)), pl.BlockSpec((tm, GROUP), lambda i: (i, 1)),
                  pl.BlockSpec((1, GROUP), lambda i: (0, 0)), pl.BlockSpec((N_HEADS, CHUNK, CHUNK), lambda i: (0, 0, 0)),
                  pl.BlockSpec((CHUNK, 128), lambda i: (0, 0))] + [const(a) for a in consts],
        out_specs=[pl.BlockSpec((tm, 2 * GROUP), lambda i: (i, 0)), pl.BlockSpec((1, GROUP), lambda i: (0, 0)),
                   pl.BlockSpec((N_HEADS, CHUNK, CHUNK), lambda i: (0, 0, 0)), pl.BlockSpec((CHUNK, 128), lambda i: (0, 0))],
        out_shape=[SDS((s, 2 * GROUP), _MXU), SDS((1, GROUP), F32), SDS((N_HEADS, CHUNK, CHUNK), F32), SDS((CHUNK, 128), F32)],
        compiler_params=_cp("arbitrary"), name=name)(dy, z, z, gain, w_s, b_t, *consts)


_SCALE_B = HEAD_DIM ** -0.5


def _block_diag(compact):
    full = jnp.concatenate([compact] * N_HEADS, axis=0)
    r = lax.broadcasted_iota(jnp.int32, full.shape, 0) // HEAD_DIM
    c = lax.broadcasted_iota(jnp.int32, full.shape, 1) // HEAD_DIM
    return jnp.where(r == c, full, 0.0)


def _diag_blocks(full):
    c = lax.broadcasted_iota(jnp.int32, (HEAD_DIM, GROUP), 1) // HEAD_DIM
    return sum(jnp.where(c == h, full[HEAD_DIM * h:HEAD_DIM * (h + 1), :], 0.0) for h in range(N_HEADS))


def _ret_fwd(z, tb, name):
    s = z.shape[0]
    nc = s // CHUNK
    row = lambda col: pl.BlockSpec((CHUNK, GROUP), lambda n, col=col: (n, col))
    const = lambda shape: pl.BlockSpec(shape, lambda n: (0,) * len(shape))

    def body(q_ref, k_ref, v_ref, g_ref, cos_ref, sin_ref, dec_ref, qw_ref, kw_ref, cd_ref, grp_ref, y_ref, o_ref, st_ref, state):
        @pl.when(pl.program_id(0) == 0)
        def _():
            state[...] = jnp.zeros_like(state)

        cos, sin = _lanes(cos_ref[...], GROUP), _lanes(sin_ref[...], GROUP)
        q = _rope(q_ref[...], cos, sin, 32)
        k = _rope(k_ref[...], cos, sin, 32) * _SCALE_B
        v = v_ref[...]
        g = g_ref[...]
        st_ref[0] = state[...]
        qm = [t.astype(_MXU) for t in _head_masked(q)]
        vm = [t.astype(_MXU) for t in _head_masked(v)]
        scs = [_dot_nt(qm[h], k) for h in range(N_HEADS)]
        cross = _dot(q * qw_ref[...], _block_diag(state[...]))
        kv = _dot_tn(k * kw_ref[...], v)
        scd = [(scs[h] * dec_ref[h]).astype(_MXU) for h in range(N_HEADS)]
        o = cross + sum(jnp.dot(scd[h], vm[h], preferred_element_type=F32) for h in range(N_HEADS))
        o_ref[...] = o
        yh, _ = _gstandardize(o, grp_ref[...])
        y_ref[...] = g * _sigmoid(g) * yh
        state[...] = cd_ref[...] * state[...] + _diag_blocks(kv)

    return pl.pallas_call(
        body, grid=(nc,),
        in_specs=[row(2), row(3), row(4), row(5), pl.BlockSpec((CHUNK, 128), lambda n: (n, 0)),
                  pl.BlockSpec((CHUNK, 128), lambda n: (n, 0)), const((N_HEADS, CHUNK, CHUNK)),
                  const((CHUNK, GROUP)), const((CHUNK, GROUP)), const((1, GROUP)), const((GROUP, GROUP))],
        out_specs=[pl.BlockSpec((CHUNK, GROUP), lambda n: (n, 0)), pl.BlockSpec((CHUNK, GROUP), lambda n: (n, 0)),
                   pl.BlockSpec((1, HEAD_DIM, GROUP), lambda n: (n, 0, 0))],
        out_shape=[SDS((s, GROUP), F32), SDS((s, GROUP), F32), SDS((nc, HEAD_DIM, GROUP), F32)],
        scratch_shapes=[pltpu.VMEM((HEAD_DIM, GROUP), F32)],
        compiler_params=_cp("arbitrary"), name=name)(z, z, z, z, tb["b_cos"], tb["b_sin"], tb["decay"], tb["qw"], tb["kw"], tb["cd"],
                                                       tb["grp"])


def _ret_bwd(dy, z, o_pre, states, tb, name):
    s = z.shape[0]
    nc = s // CHUNK
    rev = lambda col: pl.BlockSpec((CHUNK, GROUP), lambda n, col=col: (nc - 1 - n, col))
    const = lambda shape: pl.BlockSpec(shape, lambda n: (0,) * len(shape))

    def body(dy_ref, q_ref, k_ref, v_ref, g_ref, o_ref, st_ref, cos_ref, sin_ref, dec_ref, dect_ref, qw_ref, kw2_ref, qw0_ref,
             cd_ref, grp_ref, dz_ref, rstate):
        @pl.when(pl.program_id(0) == 0)
        def _():
            rstate[...] = jnp.zeros_like(rstate)

        cos, sin = _lanes(cos_ref[...], GROUP), _lanes(sin_ref[...], GROUP)
        q = _rope(q_ref[...], cos, sin, 32)
        k = _rope(k_ref[...], cos, sin, 32) * _SCALE_B
        v = v_ref[...]
        g = g_ref[...]
        dyv = dy_ref[...]
        sg = _sigmoid(g)
        yh, rs = _gstandardize(o_ref[...], grp_ref[...])
        dz_ref[:, 3 * GROUP:4 * GROUP] = (dyv * yh * (sg * (1.0 + g * (1.0 - sg)))).astype(dz_ref.dtype)
        do = _gstandardize_bwd(yh, rs, dyv * (g * sg), grp_ref[...])
        qm = [t.astype(_MXU) for t in _head_masked(q)]
        km = [t.astype(_MXU) for t in _head_masked(k)]
        vm = [t.astype(_MXU) for t in _head_masked(v)]
        dom = [t.astype(_MXU) for t in _head_masked(do)]
        s_bd = _block_diag(st_ref[0])
        r_bd = _block_diag(rstate[...])
        hs = range(N_HEADS)
        dps = [_dot_nt(dom[h], v) for h in hs]
        pts = [_dot_nt(km[h], q) for h in hs]
        dpts = [_dot_nt(vm[h], do) for h in hs]
        dq_x = _dot_nt(do * qw_ref[...], s_bd)
        dk_x = _dot_nt(v * kw2_ref[...], r_bd)
        dv_x = _dot(k * kw2_ref[...], r_bd)
        r_new = _dot_tn(q * qw0_ref[...], do)
        dpd = [(dps[h] * dec_ref[h]).astype(_MXU) for h in hs]
        dptd = [(dpts[h] * dect_ref[h]).astype(_MXU) for h in hs]
        ptd = [(pts[h] * dect_ref[h]).astype(_MXU) for h in hs]
        dq = dq_x + sum(jnp.dot(dpd[h], km[h], preferred_element_type=F32) for h in hs)
        dk = dk_x + sum(jnp.dot(dptd[h], qm[h], preferred_element_type=F32) for h in hs)
        dv = dv_x + sum(jnp.dot(ptd[h], dom[h], preferred_element_type=F32) for h in hs)
        dz_ref[:, 0:GROUP] = _rope_bwd(dq, cos, sin, 32).astype(dz_ref.dtype)
        dz_ref[:, GROUP:2 * GROUP] = _rope_bwd(dk * _SCALE_B, cos, sin, 32).astype(dz_ref.dtype)
        dz_ref[:, 2 * GROUP:3 * GROUP] = dv.astype(dz_ref.dtype)
        rstate[...] = cd_ref[...] * rstate[...] + _diag_blocks(r_new)

    r0 = lambda: pl.BlockSpec((CHUNK, GROUP), lambda n: (nc - 1 - n, 0))
    r128 = lambda: pl.BlockSpec((CHUNK, 128), lambda n: (nc - 1 - n, 0))
    return pl.pallas_call(
        body, grid=(nc,),
        in_specs=[r0(), rev(2), rev(3), rev(4), rev(5), r0(), pl.BlockSpec((1, HEAD_DIM, GROUP), lambda n: (nc - 1 - n, 0, 0)),
                  r128(), r128(), const((N_HEADS, CHUNK, CHUNK)), const((N_HEADS, CHUNK, CHUNK)), const((CHUNK, GROUP)),
                  const((CHUNK, GROUP)), const((CHUNK, GROUP)), const((1, GROUP)), const((GROUP, GROUP))],
        out_specs=pl.BlockSpec((CHUNK, 4 * GROUP), lambda n: (nc - 1 - n, 0)),
        out_shape=SDS((s, 4 * GROUP), _MXU), scratch_shapes=[pltpu.VMEM((HEAD_DIM, GROUP), F32)],
        compiler_params=_cp("arbitrary"), name=name)(
            dy, z, z, z, z, o_pre, states, tb["b_cos"], tb["b_sin"], tb["decay"], tb["decay_t"], tb["qw"], tb["kw2"], tb["qw0"],
            tb["cd"], tb["grp"])


TQ = 256


def _log_sigmoid(x):
    return jnp.minimum(x, 0.0) - jnp.log1p(jnp.exp(-jnp.abs(x)))


def _fox_prep(z, b_f, name):
    s = z.shape[0]
    nb = s // TQ

    def body(m_ref, b_ref, cc_ref, carry):
        @pl.when(pl.program_id(0) == 0)
        def _():
            carry[...] = jnp.zeros_like(carry)

        lane = lax.broadcasted_iota(jnp.int32, (TQ, 128), 1)
        logf = jnp.where(lane < N_HEADS, _log_sigmoid(m_ref[...] + b_ref[...]), 0.0)
        r = lax.broadcasted_iota(jnp.int32, (TQ, TQ), 0)
        c = lax.broadcasted_iota(jnp.int32, (TQ, TQ), 1)
        tri = jnp.where(r >= c, 1.0, 0.0).astype(F32)
        cum = _dot_exact(tri, logf) + carry[...]
        cc_ref[...] = cum * LOG2E
        carry[...] = cum[TQ - 1:TQ, :]

    return pl.pallas_call(
        body, grid=(nb,),
        in_specs=[pl.BlockSpec((TQ, 128), lambda i: (i, NZ // 128 - 1)), pl.BlockSpec((1, 128), lambda i: (0, 0))],
        out_specs=pl.BlockSpec((TQ, 128), lambda i: (i, 0)),
        out_shape=SDS((s, 128), F32), scratch_shapes=[pltpu.VMEM((1, 128), F32)],
        compiler_params=_cp("arbitrary"), name=name)(z, b_f)


def _fox_post(dcr, dcq, z, b_f, dkr, name):
    s = z.shape[0]
    nb = s // TQ

    def body(dc_ref, dcq_ref, m_ref, b_ref, dkr_ref, dz_ref, db_ref, carry):
        @pl.when(pl.program_id(0) == 0)
        def _():
            carry[...] = jnp.zeros_like(carry)
            db_ref[...] = jnp.zeros_like(db_ref)

        r = lax.broadcasted_iota(jnp.int32, (TQ, TQ), 0)
        c = lax.broadcasted_iota(jnp.int32, (TQ, TQ), 1)
        triu = jnp.where(c >= r, 1.0, 0.0).astype(F32)
        dc = jnp.concatenate([dc_ref[0], jnp.zeros((120, TQ), F32)], axis=0)
        dlogf = _dot_exact(triu, dc, (((1,), (1,)), ((), ()))) + _dot_exact(triu, dcq_ref[...]) + carry[...]
        carry[...] = dlogf[0:1, :]
        x = m_ref[...] + b_ref[...]
        lane = lax.broadcasted_iota(jnp.int32, (TQ, 128), 1)
        df = jnp.where(lane < N_HEADS, dlogf * _sigmoid(-x), 0.0)
        db_ref[...] += jnp.sum(df, axis=0, keepdims=True)
        dz_ref[...] = (df + dkr_ref[...]).astype(dz_ref.dtype)

    rv = lambda i: nb - 1 - i
    return pl.pallas_call(
        body, grid=(nb,),
        in_specs=[pl.BlockSpec((1, 8, TQ), lambda i: (rv(i), 0, 0)), pl.BlockSpec((TQ, 128), lambda i: (rv(i), 0)),
                  pl.BlockSpec((TQ, 128), lambda i: (rv(i), NZ // 128 - 1)),
                  pl.BlockSpec((1, 128), lambda i: (0, 0)), pl.BlockSpec((TQ, 128), lambda i: (rv(i), 0))],
        out_specs=[pl.BlockSpec((TQ, 128), lambda i: (rv(i), 0)), pl.BlockSpec((1, 128), lambda i: (0, 0))],
        out_shape=[SDS((s, 128), _MXU), SDS((1, 128), F32)], scratch_shapes=[pltpu.VMEM((1, 128), F32)],
        compiler_params=_cp("arbitrary"), name=name)(dcr, dcq, z, b_f, dkr)


NEG = -1e30


def _causal_mask(shape, transposed=False):
    r = lax.broadcasted_iota(jnp.int32, shape, 0)
    c = lax.broadcasted_iota(jnp.int32, shape, 1)
    return (c >= r) if transposed else (r >= c)


TKV = 512


def _key_block(s):
    return min(TKV, s)


def _diag_mask(shape, off):
    r = lax.broadcasted_iota(jnp.int32, shape, 0)
    c = lax.broadcasted_iota(jnp.int32, shape, 1)
    return c + off >= r


def _head_lanes(h, dqk):
    return slice(128 * (h // 2), 128 * (h // 2) + 128) if dqk == HEAD_DIM else slice(128 * h, 128 * h + 128)


def _keep_half(x, a, axis):
    idx = lax.broadcasted_iota(jnp.int32, x.shape, axis)
    return jnp.where((idx < HEAD_DIM) if a == 0 else (idx >= HEAD_DIM), x, jnp.zeros_like(x))


def _scaled_qt(q, scale):
    qs = q.astype(F32) * (scale * LOG2E)
    return [qs[TQ * b:TQ * (b + 1)].T.astype(_MXU) for b in range(q.shape[0] // TQ)]


def _kv_prep(z, qcol, kcol, vcol, scale, name):
    s = z.shape[0]
    tk = _key_block(s)
    nk = s // tk

    def body(q_ref, k_ref, v_ref, kb_ref, vb_ref, vt_ref, qt_ref):
        kb_ref[...] = k_ref[...].astype(_MXU)
        v = v_ref[...]
        vb_ref[...] = v.astype(_MXU)
        vt_ref[0] = v.T.astype(_MXU)
        for b, t in enumerate(_scaled_qt(q_ref[...], scale)):
            qt_ref[b] = t

    blk = pl.BlockSpec((tk, GROUP), lambda i: (i, 0))
    col = lambda c: pl.BlockSpec((tk, GROUP), lambda i, c=c: (i, c))
    return pl.pallas_call(
        body, grid=(nk,), in_specs=[col(qcol), col(kcol), col(vcol)],
        out_specs=[blk, blk, pl.BlockSpec((1, GROUP, tk), lambda i: (i, 0, 0)),
                   pl.BlockSpec((tk // TQ, GROUP, TQ), lambda i: (i, 0, 0))],
        out_shape=[SDS((s, GROUP), _MXU), SDS((s, GROUP), _MXU), SDS((nk, GROUP, tk), _MXU), SDS((s // TQ, GROUP, TQ), _MXU)],
        compiler_params=_cp("parallel"), name=name)(z, z, z)


LOG2E = 1.4426950408889634


def _attn_fwd(q, qcol, dqk, kb, vt, scale, ck2, name, comm=None):
    s = q.shape[0]
    nq = s // TQ
    tk = _key_block(s)
    ratio = tk // TQ
    wq = N_HEADS * dqk
    bias = ck2 is not None

    def body(*refs):
        ins, (o_ref, l_ref), _, cc = _split_refs(refs, 4 if bias else 3, 2, comm)
        if bias:
            q_ref, k_ref, vt_ref, cc_ref = ins
        else:
            q_ref, k_ref, vt_ref = ins
        i = pl.program_id(0)
        _host_gather(comm, cc, i, nq)
        qts = []
        for h in range(N_HEADS):
            qt = (q_ref[:, _head_lanes(h, dqk)].astype(F32) * (scale * LOG2E)).T
            qts.append((_keep_half(qt, h % 2, 0) if dqk == HEAD_DIM else qt).astype(_MXU))

        def step(j, carry, off):
            r0 = pl.multiple_of(j * tk, tk)
            vtj = vt_ref[j]
            sts = [jnp.dot(k_ref[pl.ds(r0, tk), _head_lanes(h, dqk)], qts[h], preferred_element_type=F32)
                   for h in range(N_HEADS)]
            stats, ps = [], []
            for h in range(N_HEADS):
                m, l, _ = carry[3 * h:3 * h + 3]
                st = sts[h]
                if bias:
                    st = st - cc_ref[pl.ds(r0, tk), h:h + 1]
                if off is not None:
                    st = jnp.where(_diag_mask(st.shape, off), st, NEG)
                m_new = jnp.maximum(m, jnp.max(st, axis=0, keepdims=True))
                alpha = jnp.exp2(m - m_new)
                p = jnp.exp2(st - m_new)
                stats.append((m_new, alpha * l + jnp.sum(p, axis=0, keepdims=True), alpha))
                ps.append(p.astype(_MXU))
            out = []
            for h in range(N_HEADS):
                m_new, l, alpha = stats[h]
                acc = alpha * carry[3 * h + 2] + jnp.dot(vtj[HEAD_DIM * h:HEAD_DIM * (h + 1), :], ps[h],
                                                         preferred_element_type=F32)
                out += [m_new, l, acc]
            return tuple(out)

        init = (jnp.full((1, TQ), NEG, F32), jnp.zeros((1, TQ), F32), jnp.zeros((HEAD_DIM, TQ), F32)) * N_HEADS
        jd = i // ratio
        carry = lax.fori_loop(0, jd, functools.partial(step, off=None), init)
        carry = step(jd, carry, TQ * (i % ratio))
        l_ref[...] = jnp.zeros_like(l_ref)
        for h in range(N_HEADS):
            l_ref[0, h:h + 1, :] = carry[3 * h] + jnp.log2(carry[3 * h + 1])
        for p in range(2):
            ot = jnp.concatenate([carry[6 * p + 2] / carry[6 * p + 1], carry[6 * p + 5] / carry[6 * p + 4]], axis=0)
            o_ref[:, 128 * p:128 * (p + 1)] = ot.T
        if comm is not None:
            @pl.when(i == nq - 1)
            def _():
                comm.wait(*cc)

    rows = pl.BlockSpec((1, 8, TQ), lambda i: (i, 0, 0))
    in_specs = [pl.BlockSpec((TQ, wq), lambda i: (i, qcol)), pl.BlockSpec((s, wq), lambda i: (0, 0)),
                pl.BlockSpec((s // tk, GROUP, tk), lambda i: (0, 0, 0))]
    args = [q, kb, vt]
    if bias:
        in_specs.append(pl.BlockSpec((s, 128), lambda i: (0, 0)))
        args.append(ck2)
    out_specs = [pl.BlockSpec((TQ, GROUP), lambda i: (i, 0)), rows]
    out_shape = [SDS((s, GROUP), F32), SDS((nq, 8, TQ), F32)]
    return _call_with_comm(body, (nq,), in_specs, out_specs, out_shape, [], args, comm, ("arbitrary",), name)


def _call_with_comm(body, grid, in_specs, out_specs, out_shape, scratch, args, comm, semantics, name):
    n_out = len(out_shape)
    if comm is not None:
        in_specs, out_specs = in_specs + comm.in_specs, out_specs + comm.out_specs
        out_shape, scratch, args = out_shape + comm.out_shape, scratch + comm.scratch, list(args) + comm.arrs
    res = pl.pallas_call(body, grid=grid, in_specs=in_specs, out_specs=out_specs, out_shape=out_shape,
                         scratch_shapes=scratch, compiler_params=_cp(*semantics), name=name)(*args)
    return (*res[:n_out], list(res[n_out:]))


def _attn_bwd(kb, vb, qt, dot, lse, dl, dqk, scale, ck2, name, kv_dtype, comm=None):
    s = kb.shape[0]
    nq = s // TQ
    tk = _key_block(s)
    ratio = tk // TQ
    nkb = s // tk
    wq = N_HEADS * dqk
    bias = ck2 is not None

    def body(*refs):
        ins, outs, _, cc = _split_refs(refs, 7 if bias else 6, 5 if bias else 3, comm)
        if bias:
            k_ref, v_ref, qt_ref, dot_ref, l_ref, d_ref, cc_ref = ins
            dqt_ref, dk_ref, dv_ref, dck_ref, dcq_ref = outs
        else:
            k_ref, v_ref, qt_ref, dot_ref, l_ref, d_ref = ins
            dqt_ref, dk_ref, dv_ref = outs
        j = pl.program_id(0)

        @pl.when(j == 0)
        def _():
            if comm is not None:
                comm.start(*cc)
            dqt_ref[...] = jnp.zeros_like(dqt_ref)
            if bias:
                dcq_ref[...] = jnp.zeros_like(dcq_ref)

        ks, kts, vs = [], [], []
        for h in range(N_HEADS):
            k2 = k_ref[:, _head_lanes(h, dqk)]
            if dqk == HEAD_DIM:
                k2 = _keep_half(k2, h % 2, 1)
            ks.append(k2)
            kts.append(k2.astype(F32).T.astype(_MXU))
            vs.append(_keep_half(v_ref[:, _head_lanes(h, HEAD_DIM)], h % 2, 1))
        cks = [cc_ref[:, h:h + 1] for h in range(N_HEADS)] if bias else None

        nt = (((1,), (1,)), ((), ()))

        def step(i, carry, off):
            qti, doti, li, di = qt_ref[i], dot_ref[i], l_ref[i], d_ref[i]
            qls = [_head_lanes(h, dqk) for h in range(N_HEADS)]
            vls = [_head_lanes(h, HEAD_DIM) for h in range(N_HEADS)]
            sts, dpts = [], []
            for h in range(N_HEADS):
                sts.append(jnp.dot(ks[h], qti[qls[h], :], preferred_element_type=F32))
                dpts.append(jnp.dot(vs[h], doti[vls[h], :], preferred_element_type=F32))
            pbs, dsbs, dcks = [], [], []
            for h in range(N_HEADS):
                st = sts[h] - li[h:h + 1, :]
                if bias:
                    st = st - cks[h]
                p = jnp.exp2(st)
                if off is not None:
                    p = jnp.where(_diag_mask(p.shape, off), p, 0.0)
                dst = p * (dpts[h] - di[h:h + 1, :])
                pbs.append(p.astype(_MXU))
                dsbs.append(dst.astype(_MXU))
                if bias:
                    dcks.append(carry[3 * h + 2] + jnp.sum(dst, axis=1, keepdims=True))
                    dcq_ref[i, h:h + 1, :] += jnp.sum(dst, axis=0, keepdims=True)
                else:
                    dcks.append(carry[3 * h + 2])
            out = []
            for h in range(N_HEADS):
                dvt = carry[3 * h + 1] + lax.dot_general(doti[HEAD_DIM * h:HEAD_DIM * (h + 1), :], pbs[h], nt,
                                                         preferred_element_type=F32)
                dkt = carry[3 * h] + lax.dot_general(qti[dqk * h:dqk * (h + 1), :], dsbs[h], nt, preferred_element_type=F32)
                dqt_ref[i, qls[h], :] += jnp.dot(kts[h], dsbs[h], preferred_element_type=F32) * scale
                out += [dkt, dvt, dcks[h]]
            return tuple(out)

        carry = (jnp.zeros((dqk, tk), F32), jnp.zeros((HEAD_DIM, tk), F32), jnp.zeros((tk, 1), F32)) * N_HEADS
        for r in range(ratio):
            carry = step(ratio * j + r, carry, TQ * r)
        carry = lax.fori_loop(ratio * (j + 1), nq, functools.partial(step, off=None), carry)
        for p in range(2):
            dv_ref[:, 128 * p:128 * (p + 1)] = jnp.concatenate([carry[6 * p + 1], carry[6 * p + 4]], axis=0).T.astype(dv_ref.dtype)
            if dqk == HEAD_DIM:
                dk_ref[:, 128 * p:128 * (p + 1)] = (jnp.concatenate([carry[6 * p], carry[6 * p + 3]], axis=0).T
                                                    * (1.0 / LOG2E)).astype(dk_ref.dtype)
        if dqk != HEAD_DIM:
            for h in range(N_HEADS):
                dk_ref[:, 128 * h:128 * (h + 1)] = (carry[3 * h].T * (1.0 / LOG2E)).astype(dk_ref.dtype)
        if bias:
            dck_ref[...] = jnp.zeros_like(dck_ref)
            for h in range(N_HEADS):
                dck_ref[:, h:h + 1] = -carry[3 * h + 2]
        if comm is not None:
            @pl.when(j == nkb - 1)
            def _():
                comm.wait(*cc)

    blk = lambda w: pl.BlockSpec((tk, w), lambda j: (j, 0))
    full3 = lambda w: pl.BlockSpec((nq, w, TQ), lambda j: (0, 0, 0))
    in_specs = [blk(wq), blk(GROUP), full3(wq), full3(GROUP), full3(8), full3(8)]
    args = [kb, vb, qt, dot, lse, dl]
    out_specs = [full3(wq), blk(wq), blk(GROUP)]
    out_shape = [SDS((nq, wq, TQ), F32), SDS((s, wq), kv_dtype), SDS((s, GROUP), kv_dtype)]
    if bias:
        in_specs.append(blk(128))
        args.append(ck2)
        out_specs += [blk(128), full3(8)]
        out_shape += [SDS((s, 128), F32), SDS((nq, 8, TQ), F32)]
    return _call_with_comm(body, (nkb,), in_specs, out_specs, out_shape, [], args, comm, ("arbitrary",), name)


def _untranspose(xt, dtype, name):
    nq, w, _ = xt.shape

    def body(x_ref, o_ref):
        o_ref[...] = x_ref[0].T.astype(o_ref.dtype)

    return pl.pallas_call(
        body, grid=(nq,), in_specs=[pl.BlockSpec((1, w, TQ), lambda i: (i, 0, 0))],
        out_specs=pl.BlockSpec((TQ, w), lambda i: (i, 0)), out_shape=SDS((nq * TQ, w), dtype),
        compiler_params=_cp("parallel"), name=name)(xt)


_SCALE_D = (64 + 32) ** -0.5
_COL_CQ, _COL_CKV, _COL_MISC = 2304 // 256, 2560 // 128, 2688 // 128


def _mla_prep(z, gq, gkv, wq, wk, wv, tb, name):
    s = z.shape[0]
    tm = _key_block(s)
    row = lambda w, c: pl.BlockSpec((tm, w), lambda i, c=c: (i, c))
    const = lambda a: pl.BlockSpec(a.shape, lambda i: (0,) * a.ndim)

    def body(cq_ref, ckv_ref, m_ref, gq_ref, gkv_ref, wq_ref, wk_ref, wv_ref, e_ref, qc_ref, qs_ref, kc_ref, ks_ref,
             q_ref, k_ref, v_ref, vt_ref, cqn_ref, ckvn_ref, qt_ref):
        cqn = _rms(cq_ref[...], gq_ref[...]).astype(_MXU)
        ckvn = _rms(ckv_ref[...], gkv_ref[...]).astype(_MXU)
        cqn_ref[...] = cqn
        ckvn_ref[...] = ckvn
        qb = _rope(_dot(cqn, wq_ref[...]), _lanes(qc_ref[...], 512), _lanes(qs_ref[...], 512), 16).astype(q_ref.dtype)
        q_ref[...] = qb
        for b, t in enumerate(_scaled_qt(qb, _SCALE_D)):
            qt_ref[b] = t
        kr = _rope(m_ref[...], kc_ref[...], ks_ref[...], 16)
        k_ref[...] = (_dot(ckvn, wk_ref[...]) + _dot(kr, e_ref[...])).astype(k_ref.dtype)
        v = _dot(ckvn, wv_ref[...])
        v_ref[...] = v.astype(v_ref.dtype)
        vt_ref[0] = v.T.astype(vt_ref.dtype)

    e = tb["place"]
    return pl.pallas_call(
        body, grid=(s // tm,),
        in_specs=[row(256, _COL_CQ), row(128, _COL_CKV), row(128, _COL_MISC), const(gq), const(gkv), const(wq), const(wk),
                  const(wv), const(e), row(128, 0), row(128, 0), row(128, 0), row(128, 0)],
        out_specs=[row(512, 0), row(512, 0), row(256, 0), pl.BlockSpec((1, GROUP, tm), lambda i: (i, 0, 0)), row(256, 0),
                   row(128, 0), pl.BlockSpec((tm // TQ, 512, TQ), lambda i: (i, 0, 0))],
        out_shape=[SDS((s, 512), _MXU), SDS((s, 512), _MXU), SDS((s, 256), _MXU), SDS((s // tm, GROUP, tm), _MXU),
                   SDS((s, 256), _MXU), SDS((s, 128), _MXU), SDS((s // TQ, 512, TQ), _MXU)],
        compiler_params=_cp("parallel"), name=name)(
            z, z, z, gq, gkv, wq, wk, wv, e, tb["q_cos"], tb["q_sin"], tb["k_cos"], tb["k_sin"])


def _mla_prep_bwd(dqt, dk, dv, z, cqn, ckvn, gq, gkv, wq, wk, wv, tb, name):
    s = z.shape[0]
    tm = min(512, s)
    row = lambda w, c: pl.BlockSpec((tm, w), lambda i, c=c: (i, c))
    const = lambda a: pl.BlockSpec(a.shape, lambda i: (0,) * a.ndim)
    acc = lambda shape: pl.BlockSpec(shape, lambda i: (0, 0))

    def body(dq_ref, dk_ref, dv_ref, cq_ref, ckv_ref, cqn_ref, ckvn_ref, gq_ref, gkv_ref, wq_ref, wk_ref, wv_ref, e_ref,
             qc_ref, qs_ref, kc_ref, ks_ref, dcq_ref, dckv_ref, dkr_ref, dwq_ref, dwk_ref, dwv_ref, dgq_ref, dgkv_ref):
        @pl.when(pl.program_id(0) == 0)
        def _():
            for r in (dwq_ref, dwk_ref, dwv_ref, dgq_ref, dgkv_ref):
                r[...] = jnp.zeros_like(r)

        dq = jnp.concatenate([dq_ref[b].T for b in range(tm // TQ)], axis=0)
        dqp = _rope_bwd(dq, _lanes(qc_ref[...], 512), _lanes(qs_ref[...], 512), 16)
        dkd = dk_ref[...]
        dvd = dv_ref[...]
        dwq_ref[...] += _dot_tn(cqn_ref[...], dqp)
        dwk_ref[...] += _dot_tn(ckvn_ref[...], dkd)
        dwv_ref[...] += _dot_tn(ckvn_ref[...], dvd)
        dcq, dgq = _rms_bwd(cq_ref[...], gq_ref[...], _dot_nt(dqp, wq_ref[...]))
        dckv, dgkv = _rms_bwd(ckv_ref[...], gkv_ref[...], _dot_nt(dkd, wk_ref[...]) + _dot_nt(dvd, wv_ref[...]))
        dcq_ref[...] = dcq.astype(dcq_ref.dtype)
        dckv_ref[...] = dckv.astype(dckv_ref.dtype)
        dgq_ref[...] += dgq
        dgkv_ref[...] += dgkv
        dkr = _dot_exact(dkd, e_ref[...], (((1,), (1,)), ((), ())))
        dkr_ref[...] = _rope_bwd(dkr, kc_ref[...], ks_ref[...], 16)

    e = tb["place"]
    return pl.pallas_call(
        body, grid=(s // tm,),
        in_specs=[pl.BlockSpec((tm // TQ, 512, TQ), lambda i: (i, 0, 0)), row(512, 0), row(256, 0), row(256, _COL_CQ),
                  row(128, _COL_CKV), row(256, 0), row(128, 0),
                  const(gq), const(gkv), const(wq), const(wk), const(wv), const(e), row(128, 0), row(128, 0), row(128, 0), row(128, 0)],
        out_specs=[row(256, 0), row(128, 0), row(128, 0), acc((256, 512)), acc((128, 512)), acc((128, 256)), acc((1, 256)),
                   acc((1, 128))],
        out_shape=[SDS((s, 256), _MXU), SDS((s, 128), _MXU), SDS((s, 128), F32), SDS((256, 512), F32), SDS((128, 512), F32),
                   SDS((128, 256), F32), SDS((1, 256), F32), SDS((1, 128), F32)],
        compiler_params=_cp("arbitrary"), name=name)(
            dqt, dk, dv, z, z, cqn, ckvn, gq, gkv, wq, wk, wv, e, tb["q_cos"], tb["q_sin"], tb["k_cos"], tb["k_sin"])


def _out_proj(ys, g, w, x, name):
    s, d = x.shape
    tm = min(512, s)

    def body(ya, yb, yc, yd, g_ref, w_ref, x_ref, o_ref, yn_ref):
        acc = x_ref[...]
        for i, y_ref in enumerate((ya, yb, yc, yd)):
            sl = slice(GROUP * i, GROUP * (i + 1))
            yn = _rms(y_ref[...], g_ref[:, sl]).astype(_MXU)
            yn_ref[:, sl] = yn
            acc = acc + jnp.dot(yn, w_ref[sl, :], preferred_element_type=F32)
        o_ref[...] = acc

    yspec = pl.BlockSpec((tm, GROUP), lambda i: (i, 0))
    return pl.pallas_call(
        body, grid=(s // tm,),
        in_specs=[yspec, yspec, yspec, yspec, pl.BlockSpec((1, d), lambda i: (0, 0)), pl.BlockSpec((d, d), lambda i: (0, 0)),
                  pl.BlockSpec((tm, d), lambda i: (i, 0))],
        out_specs=[pl.BlockSpec((tm, d), lambda i: (i, 0)), pl.BlockSpec((tm, d), lambda i: (i, 0))],
        out_shape=[SDS((s, d), F32), SDS((s, d), _MXU)], compiler_params=_cp("parallel"), name=name)(*ys, g, w, x)


def _out_proj_bwd(dx, w, ys, g, name):
    s, d = dx.shape
    tm = min(512, s)
    nb = tm // TQ

    def body(dx_ref, w_ref, ya, yb, yc, yd, g_ref, da, db, dg_ref, dtc_ref, dtd_ref, dlc_ref, dld_ref):
        @pl.when(pl.program_id(0) == 0)
        def _():
            dg_ref[...] = jnp.zeros_like(dg_ref)

        dyn = _dot_nt(dx_ref[...], w_ref[...])
        for i, y_ref in enumerate((ya, yb, yc, yd)):
            sl = slice(GROUP * i, GROUP * (i + 1))
            y = y_ref[...]
            dy, dg = _rms_bwd(y, g_ref[:, sl], dyn[:, sl])
            dg_ref[:, sl] += dg
            if i < 2:
                (da, db)[i][...] = dy
                continue
            dt_ref, dl_ref = ((dtc_ref, dlc_ref), (dtd_ref, dld_ref))[i - 2]
            dl_ref[...] = jnp.zeros_like(dl_ref)
            for b in range(nb):
                r = slice(TQ * b, TQ * (b + 1))
                dt_ref[b] = dy[r].T.astype(dt_ref.dtype)
                pt = (dy[r] * y[r]).T
                for h in range(N_HEADS):
                    dl_ref[b, h:h + 1, :] = jnp.sum(pt[HEAD_DIM * h:HEAD_DIM * (h + 1), :], axis=0, keepdims=True)

    yspec = pl.BlockSpec((tm, GROUP), lambda i: (i, 0))
    tspec = pl.BlockSpec((nb, GROUP, TQ), lambda i: (i, 0, 0))
    lspec = pl.BlockSpec((nb, 8, TQ), lambda i: (i, 0, 0))
    return pl.pallas_call(
        body, grid=(s // tm,),
        in_specs=[pl.BlockSpec((tm, d), lambda i: (i, 0)), pl.BlockSpec((d, d), lambda i: (0, 0)), yspec, yspec, yspec, yspec,
                  pl.BlockSpec((1, d), lambda i: (0, 0))],
        out_specs=[yspec, yspec, pl.BlockSpec((1, d), lambda i: (0, 0)), tspec, tspec, lspec, lspec],
        out_shape=[SDS((s, GROUP), F32)] * 2 + [SDS((1, d), F32)] + [SDS((s // TQ, GROUP, TQ), _MXU)] * 2
        + [SDS((s // TQ, 8, TQ), F32)] * 2,
        compiler_params=_cp("arbitrary"), name=name)(dx, w, *ys, g)


FF_BLOCK = 512
FF_ROWS = 1024


def _ffn_fwd(x, g, wu, wd, name, comm=None):
    s, d = x.shape
    nj = wu.shape[0]
    tm = min(FF_ROWS, s)
    ni = s // tm

    def body(*refs):
        (x_ref, g_ref, wu_ref, wd_ref), (o_ref, u_ref, h_ref), (acc,), cc = _split_refs(refs, 4, 3, comm)
        i, j = pl.program_id(0), pl.program_id(1)
        _host_gather(comm, cc, i * nj + j, ni * nj)

        @pl.when(j == 0)
        def _():
            h_ref[...] = _rms(x_ref[...], g_ref[...]).astype(h_ref.dtype)
            acc[...] = jnp.zeros_like(acc)

        halves = [slice(r, r + tm // 2) for r in range(0, tm, tm // 2)]
        us = [jnp.dot(h_ref[r, :], wu_ref[0], preferred_element_type=F32) for r in halves]
        for r, u in zip(halves, us):
            u_ref[r, :] = u.astype(u_ref.dtype)
            acc[r, :] += _dot(jnp.square(jnp.maximum(u, 0.0)), wd_ref[...])

        @pl.when(j == nj - 1)
        def _():
            o_ref[...] = x_ref[...] + acc[...]

        if comm is not None:
            @pl.when((i == ni - 1) & (j == nj - 1))
            def _():
                comm.wait(*cc)

    in_specs = [pl.BlockSpec((tm, d), lambda i, j: (i, 0)), pl.BlockSpec((1, d), lambda i, j: (0, 0)),
                pl.BlockSpec((1, d, FF_BLOCK), lambda i, j: (j, 0, 0)), pl.BlockSpec((FF_BLOCK, d), lambda i, j: (j, 0))]
    out_specs = [pl.BlockSpec((tm, d), lambda i, j: (i, 0)), pl.BlockSpec((tm, FF_BLOCK), lambda i, j: (i, j)),
                 pl.BlockSpec((tm, d), lambda i, j: (i, 0))]
    out_shape = [SDS((s, d), F32), SDS((s, nj * FF_BLOCK), _MXU), SDS((s, d), _MXU)]
    return _call_with_comm(body, (ni, nj), in_specs, out_specs, out_shape, [pltpu.VMEM((tm, d), F32)], [x, g, wu, wd], comm,
                           ("arbitrary", "arbitrary"), name)


def _ffn_bwd(dx2, x, u, g, wu, wd, name, comm=None):
    s, d = x.shape
    nj = wu.shape[0]
    tm = min(FF_ROWS, s)
    ni = s // tm

    def body(*refs):
        (dx_ref, x_ref, u_ref, g_ref, wu_ref, wd_ref), (o_ref, du_ref, dg_ref), (acc, dxb), cc = _split_refs(refs, 6, 3, comm)
        i, j = pl.program_id(0), pl.program_id(1)

        @pl.when((i == 0) & (j == 0))
        def _():
            if comm is not None:
                comm.start(*cc)
            dg_ref[...] = jnp.zeros_like(dg_ref)

        @pl.when(j == 0)
        def _():
            dxb[...] = dx_ref[...].astype(dxb.dtype)
            acc[...] = jnp.zeros_like(acc)

        nt = (((1,), (1,)), ((), ()))
        halves = [slice(r, r + tm // 2) for r in range(0, tm, tm // 2)]
        das = [lax.dot_general(dxb[r, :], wd_ref[...], nt, preferred_element_type=F32) for r in halves]
        for r, da in zip(halves, das):
            du = (da * 2.0 * jnp.maximum(u_ref[r, :].astype(F32), 0.0)).astype(du_ref.dtype)
            du_ref[r, :] = du
            acc[r, :] += lax.dot_general(du, wu_ref[0], nt, preferred_element_type=F32)

        @pl.when(j == nj - 1)
        def _():
            dxn, dg = _rms_bwd(x_ref[...], g_ref[...], acc[...])
            o_ref[...] = dx_ref[...] + dxn
            dg_ref[...] += dg

        if comm is not None:
            @pl.when((i == ni - 1) & (j == nj - 1))
            def _():
                comm.wait(*cc)

    in_specs = [pl.BlockSpec((tm, d), lambda i, j: (i, 0)), pl.BlockSpec((tm, d), lambda i, j: (i, 0)),
                pl.BlockSpec((tm, FF_BLOCK), lambda i, j: (i, j)), pl.BlockSpec((1, d), lambda i, j: (0, 0)),
                pl.BlockSpec((1, d, FF_BLOCK), lambda i, j: (j, 0, 0)), pl.BlockSpec((FF_BLOCK, d), lambda i, j: (j, 0))]
    out_specs = [pl.BlockSpec((tm, d), lambda i, j: (i, 0)), pl.BlockSpec((tm, FF_BLOCK), lambda i, j: (i, j)),
                 pl.BlockSpec((1, d), lambda i, j: (0, 0))]
    out_shape = [SDS((s, d), F32), SDS((s, nj * FF_BLOCK), _MXU), SDS((1, d), F32)]
    return _call_with_comm(body, (ni, nj), in_specs, out_specs, out_shape,
                           [pltpu.VMEM((tm, d), F32), pltpu.VMEM((tm, d), _MXU)], [dx2, x, u, g, wu, wd], comm,
                           ("arbitrary", "arbitrary"), name)


def _in_proj_bwd(dz, w, x, g, dx_up, name, comm=None):
    s, d = x.shape
    n = w.shape[1]
    tm = min(512, s)
    ni = s // tm

    def body(*refs):
        (dz_ref, w_ref, x_ref, g_ref, up_ref), (o_ref, dg_ref), _, cc = _split_refs(refs, 5, 2, comm)
        i = pl.program_id(0)

        @pl.when(i == 0)
        def _():
            if comm is not None:
                comm.start(*cc)
            dg_ref[...] = jnp.zeros_like(dg_ref)

        dh = lax.dot_general(dz_ref[...], w_ref[...], (((1,), (1,)), ((), ())), preferred_element_type=F32)
        dxn, dg = _rms_bwd(x_ref[...], g_ref[...], dh)
        o_ref[...] = up_ref[...] + dxn
        dg_ref[...] += dg
        if comm is not None:
            @pl.when(i == ni - 1)
            def _():
                comm.wait(*cc)

    in_specs = [pl.BlockSpec((tm, n), lambda i: (i, 0)), pl.BlockSpec((d, n), lambda i: (0, 0)),
                pl.BlockSpec((tm, d), lambda i: (i, 0)), pl.BlockSpec((1, d), lambda i: (0, 0)),
                pl.BlockSpec((tm, d), lambda i: (i, 0))]
    out_specs = [pl.BlockSpec((tm, d), lambda i: (i, 0)), pl.BlockSpec((1, d), lambda i: (0, 0))]
    out_shape = [SDS((s, d), F32), SDS((1, d), F32)]
    return _call_with_comm(body, (ni,), in_specs, out_specs, out_shape, [], [dz, w, x, g, dx_up], comm, ("arbitrary",), name)


def _loss_head(x, g, target, name):
    s, d = x.shape
    tm = min(512, s)

    def body(x_ref, g_ref, t_ref, l_ref, dx_ref, dg_ref):
        @pl.when(pl.program_id(0) == 0)
        def _():
            l_ref[...] = jnp.zeros_like(l_ref)
            dg_ref[...] = jnp.zeros_like(dg_ref)

        xv = x_ref[...]
        err = _rms(xv, g_ref[...]) - t_ref[...]
        l_ref[...] += jnp.sum(err * err, axis=0, keepdims=True) * (0.5 / d)
        dx, dg = _rms_bwd(xv, g_ref[...], err * (1.0 / d))
        dx_ref[...] = dx
        dg_ref[...] += dg

    return pl.pallas_call(
        body, grid=(s // tm,),
        in_specs=[pl.BlockSpec((tm, d), lambda i: (i, 0)), pl.BlockSpec((1, d), lambda i: (0, 0)),
                  pl.BlockSpec((tm, d), lambda i: (i, 0))],
        out_specs=[pl.BlockSpec((1, d), lambda i: (0, 0)), pl.BlockSpec((tm, d), lambda i: (i, 0)),
                   pl.BlockSpec((1, d), lambda i: (0, 0))],
        out_shape=[SDS((1, d), F32), SDS((s, d), F32), SDS((1, d), F32)], compiler_params=_cp("arbitrary"), name=name)(x, g, target)


def _me_and_peer():
    x, y, c = lax.axis_index("x"), lax.axis_index("y"), lax.axis_index("c")
    me = 4 * x + 2 * y + c

    def peer(k):
        px, py, pc = x ^ (k >> 2), y ^ ((k >> 1) & 1), c ^ (k & 1)
        return (px, py, pc), 4 * px + 2 * py + pc

    return me, peer


class _Comm:
    CHIPS = (2, 4, 6)

    def __init__(self, kind, arrs):
        assert kind in ("gather", "exchange")
        self.kind, self.arrs, self.n = kind, list(arrs), len(arrs)
        anyspec = pl.BlockSpec(memory_space=pl.ANY)
        self.in_specs = [anyspec] * self.n
        self.out_specs = [anyspec] * self.n
        self.out_shape = [SDS(((NDEV,) + a.shape) if kind == "gather" else a.shape, a.dtype) for a in self.arrs]
        npair = NDEV - 1 + len(self.CHIPS)
        self.scratch = [pltpu.SemaphoreType.DMA((self.n, npair)), pltpu.SemaphoreType.DMA((self.n, npair)),
                        pltpu.SemaphoreType.DMA((self.n,))]

    def _copies(self, ins, outs, sems):
        send, recv, loc = sems
        me, peer = _me_and_peer()
        gather = self.kind == "gather"
        sibling = peer(1)[0]
        local = [pltpu.make_async_copy(ins[a] if gather else ins[a].at[me], outs[a].at[me], loc.at[a]) for a in range(self.n)]
        outgoing, incoming, forwards, forwarded = [], [], [], []
        for k in ((1,) + self.CHIPS) if gather else range(1, NDEV):
            dev, pid = peer(k)
            for a in range(self.n):
                pair = dict(send_sem=send.at[a, k - 1], recv_sem=recv.at[a, k - 1], device_id=dev, device_id_type=MESH)
                outgoing.append(pltpu.make_async_remote_copy(src_ref=ins[a] if gather else ins[a].at[pid],
                                                             dst_ref=outs[a].at[me], **pair))
                incoming.append(pltpu.make_async_remote_copy(src_ref=ins[a] if gather else ins[a].at[me],
                                                             dst_ref=outs[a].at[pid], **pair))
        if gather:
            for idx, k in enumerate(self.CHIPS):
                got, theirs = peer(k)[1], peer(k + 1)[1]
                for a in range(self.n):
                    pair = dict(send_sem=send.at[a, NDEV - 1 + idx], recv_sem=recv.at[a, NDEV - 1 + idx], device_id=sibling,
                                device_id_type=MESH)
                    forwards.append(pltpu.make_async_remote_copy(src_ref=outs[a].at[got], dst_ref=outs[a].at[got], **pair))
                    forwarded.append(pltpu.make_async_remote_copy(src_ref=outs[a].at[theirs], dst_ref=outs[a].at[theirs], **pair))
        return local, outgoing, incoming, forwards, forwarded

    def start(self, ins, outs, sems):
        local, outgoing, _, _, _ = self._copies(ins, outs, sems)
        for cp in local + outgoing:
            cp.start()

    def forward(self, ins, outs, sems):
        _, _, incoming, forwards, _ = self._copies(ins, outs, sems)
        per = self.n
        for idx in range(len(forwards) // per if per else 0):
            for a in range(per):
                incoming[(1 + idx) * per + a].wait_recv()
                forwards[idx * per + a].start()

    def wait(self, ins, outs, sems):
        local, outgoing, incoming, forwards, forwarded = self._copies(ins, outs, sems)
        for cp in (incoming[:self.n] if self.kind == "gather" else incoming) + forwarded:
            cp.wait_recv()
        for cp in outgoing + forwards:
            cp.wait_send()
        for cp in local:
            cp.wait()


def _host_gather(comm, cc, step, nsteps, late=False):
    if comm is None:
        return

    @pl.when(step == 0)
    def _():
        comm.start(*cc)

    @pl.when(step == (nsteps - 1 if late else (2 * nsteps) // 3))
    def _():
        comm.forward(*cc)


def _split_refs(refs, n_in, n_out, comm):
    c = comm.n if comm is not None else 0
    ins, cin = refs[:n_in], refs[n_in:n_in + c]
    outs, cout = refs[n_in + c:n_in + c + n_out], refs[n_in + c + n_out:n_in + 2 * c + n_out]
    rest = refs[n_in + 2 * c + n_out:]
    scratch, csem = (rest[:len(rest) - 3], rest[len(rest) - 3:]) if c else (rest, ())
    return ins, outs, scratch, (cin, cout, csem)


def _comm_call(kind, arrs, name):
    comm = _Comm(kind, arrs)

    def body(*refs):
        _, _, _, c = _split_refs(refs, 0, 0, comm)
        comm.start(*c)
        if kind == "gather":
            comm.forward(*c)
        comm.wait(*c)

    return pl.pallas_call(body, in_specs=comm.in_specs, out_specs=comm.out_specs, out_shape=comm.out_shape,
                          scratch_shapes=comm.scratch, compiler_params=pltpu.CompilerParams(has_side_effects=True),
                          name=name)(*arrs)


def _all_gather(arrs, name):
    return _comm_call("gather", arrs, name)


def _exchange(arrs, name):
    return _comm_call("exchange", arrs, name)


def _sum_slots(parts, name):
    _, r, c = parts.shape
    tr = r if r <= 512 else 512

    def body(p_ref, o_ref):
        acc = p_ref[0].astype(F32)
        for q in range(1, NDEV):
            acc = acc + p_ref[q].astype(F32)
        o_ref[...] = acc

    return pl.pallas_call(
        body, grid=(r // tr,), in_specs=[pl.BlockSpec((NDEV, tr, c), lambda i: (0, i, 0))],
        out_specs=pl.BlockSpec((tr, c), lambda i: (i, 0)), out_shape=SDS((r, c), F32),
        compiler_params=_cp("parallel"), name=name)(parts)


def _adamw(g, w, m, v, name):
    r, c = w.shape
    parts = g.ndim == 3
    tr = r
    for cand in (512, 256, 128, 64, 32, 16, 8):
        if r > cand and r % cand == 0 and cand * c * 4 <= 2 * 1024 * 1024:
            tr = cand
            break
    bc1 = 1.0 / (1.0 - ADAM_B1 ** ADAM_STEP)
    bc2 = 1.0 / (1.0 - ADAM_B2 ** ADAM_STEP)

    def body(g_ref, w_ref, m_ref, v_ref, go_ref, d_ref, mo_ref, vo_ref):
        if parts:
            gv = g_ref[0].astype(F32)
            for q in range(1, NDEV):
                gv = gv + g_ref[q].astype(F32)
        else:
            gv = g_ref[...]
        mn = ADAM_B1 * m_ref[...] + (1.0 - ADAM_B1) * gv
        vn = ADAM_B2 * v_ref[...] + (1.0 - ADAM_B2) * (gv * gv)
        go_ref[...] = gv
        mo_ref[...] = mn
        vo_ref[...] = vn
        d_ref[...] = -ADAM_LR * ((mn * bc1) / (jnp.sqrt(vn * bc2) + ADAM_EPS) + ADAM_WD * w_ref[...])

    spec = pl.BlockSpec((tr, c), lambda i: (i, 0))
    gspec = pl.BlockSpec((NDEV, tr, c), lambda i: (0, i, 0)) if parts else spec
    return pl.pallas_call(
        body, grid=(r // tr,), in_specs=[gspec, spec, spec, spec], out_specs=[spec] * 4,
        out_shape=[SDS((r, c), F32)] * 4, compiler_params=_cp("parallel"), name=name)(g, w, m, v)


def _adamw_layer(parts, w, m, v, l, prev, name):
    depth, r, c = w.shape
    tr = next(t for t in (512, 256, 128, 64, 32, 16, 8) if r % t == 0 and t * c * 4 <= 2 * 1024 * 1024)
    bc1 = 1.0 / (1.0 - ADAM_B1 ** ADAM_STEP)
    bc2 = 1.0 / (1.0 - ADAM_B2 ** ADAM_STEP)

    def body(g_ref, w_ref, m_ref, v_ref, *rest):
        go_ref, d_ref, mo_ref, vo_ref = rest[-4:]
        gv = g_ref[0].astype(F32)
        for q in range(1, NDEV):
            gv = gv + g_ref[q].astype(F32)
        mn = ADAM_B1 * m_ref[0] + (1.0 - ADAM_B1) * gv
        vn = ADAM_B2 * v_ref[0] + (1.0 - ADAM_B2) * (gv * gv)
        go_ref[0] = gv
        mo_ref[0] = mn
        vo_ref[0] = vn
        d_ref[0] = -ADAM_LR * ((mn * bc1) / (jnp.sqrt(vn * bc2) + ADAM_EPS) + ADAM_WD * w_ref[0])

    spec = pl.BlockSpec((1, tr, c), lambda i: (l, i, 0))
    in_specs = [pl.BlockSpec((NDEV, tr, c), lambda i: (0, i, 0)), spec, spec, spec]
    args = [parts, w, m, v]
    aliases = {}
    if prev is not None:
        in_specs += [pl.BlockSpec(memory_space=pl.ANY)] * 4
        args += list(prev)
        aliases = {4 + k: k for k in range(4)}
    return pl.pallas_call(
        body, grid=(r // tr,), in_specs=in_specs, out_specs=[spec] * 4, out_shape=[SDS((depth, r, c), F32)] * 4,
        input_output_aliases=aliases, compiler_params=_cp("parallel"), name=name)(*args)


def _pad_in_cols(w):
    r = w.shape[0]
    zeros = lambda n: jnp.zeros((r, n), w.dtype)
    return jnp.concatenate([w[:, :2304], w[:, 2308:2692], w[:, 2304:2308], zeros(28), w[:, 2692:2724], zeros(64)], axis=1)


def _unpad_in_cols(w):
    return jnp.concatenate([w[..., :2304], w[..., 2688:2692], w[..., 2304:2688], w[..., 2720:2752]], axis=-1)


def _pad_uq(w):
    return jnp.pad(w.reshape(256, N_HEADS, 96), ((0, 0), (0, 0), (0, 32))).reshape(256, 512)


def _unpad_uq(w):
    return w.reshape(256, N_HEADS, 128)[:, :, :96].reshape(256, 384)


def _split_ukv(w):
    r = w.reshape(128, N_HEADS, 128)
    return jnp.pad(r[:, :, :64], ((0, 0), (0, 0), (0, 64))).reshape(128, 512), r[:, :, 64:].reshape(128, 256)


def _join_ukv(dk, dv):
    return jnp.concatenate([dk.reshape(128, N_HEADS, 128)[:, :, :64], dv.reshape(128, N_HEADS, 64)], axis=-1).reshape(128, 512)


def _cols_to_full(g):
    return jnp.transpose(g, (1, 0, 2)).reshape(g.shape[1], NDEV * g.shape[2])


def kernel(x, g_mix_norm, w_in, b_forget, g_sgu, w_spatial, b_spatial, g_mla_q, w_uq, g_mla_kv, w_ukv, g_group_out, w_out, g_ffn_norm, w_up, w_down, g_final, loss_target, m_g_mix_norm, m_w_in, m_b_forget, m_g_sgu, m_w_spatial, m_b_spatial, m_g_mla_q, m_w_uq, m_g_mla_kv, m_w_ukv, m_g_group_out, m_w_out, m_g_ffn_norm, m_w_up, m_w_down, m_g_final, v_g_mix_norm, v_w_in, v_b_forget, v_g_sgu, v_w_spatial, v_b_spatial, v_g_mla_q, v_w_uq, v_g_mla_kv, v_w_ukv, v_g_group_out, v_w_out, v_g_ffn_norm, v_w_up, v_w_down, v_g_final):
    depth = w_in.shape[0]
    s, d = x.shape[1], x.shape[2]
    x0 = x.reshape(s, d)
    target = loss_target.reshape(s, d)
    tb = _tables(s)
    me = 4 * lax.axis_index("x") + 2 * lax.axis_index("y") + lax.axis_index("c")

    assert depth == 2
    shards = {}
    for l in range(depth):
        shards.update({(l, "w_in"): _pad_in_cols(w_in[l]).astype(_WIRE), (l, "w_out"): w_out[l].astype(_WIRE),
                       (l, "w_up"): w_up[l].astype(_WIRE), (l, "w_down"): w_down[l].astype(_WIRE),
                       (l, "w_uq"): w_uq[l].astype(_WIRE), (l, "w_ukv"): w_ukv[l].astype(_WIRE)})
    wts = _ShardedWeights(shards)
    wts.full[(0, "w_in")] = _all_gather([shards[(0, "w_in")]], "gather_w_in0")[0]

    row = lambda a: a.reshape(1, -1)

    def small(l):
        bf = jnp.pad(b_forget[l].reshape(1, N_HEADS), ((0, 0), (0, 128 - N_HEADS)))
        bt = jnp.pad(b_spatial[l].T, ((0, 0), (0, 128 - N_HEADS)))
        return dict(g_mix=row(g_mix_norm[l]), g_sgu=row(g_sgu[l]), w_s=w_spatial[l], b_t=bt, b_f=bf, gq=row(g_mla_q[l]),
                    gkv=row(g_mla_kv[l]), g_go=row(g_group_out[l]), g_ffn=row(g_ffn_norm[l]))

    smalls = [small(l) for l in range(depth)]
    lrow, dx, sm, dg_final = _local_step(x0, target, wts, smalls, row(g_final), tb)
    loss = lax.psum(jnp.sum(lrow), AXES)
    grad_x = dx.reshape(1, s, d)
    return _reduce_and_update(loss, grad_x, wts.recv, sm, dg_final, me, dict(
        g_mix_norm=(g_mix_norm, m_g_mix_norm, v_g_mix_norm), w_in=(w_in, m_w_in, v_w_in),
        b_forget=(b_forget, m_b_forget, v_b_forget), g_sgu=(g_sgu, m_g_sgu, v_g_sgu),
        w_spatial=(w_spatial, m_w_spatial, v_w_spatial), b_spatial=(b_spatial, m_b_spatial, v_b_spatial),
        g_mla_q=(g_mla_q, m_g_mla_q, v_g_mla_q), w_uq=(w_uq, m_w_uq, v_w_uq), g_mla_kv=(g_mla_kv, m_g_mla_kv, v_g_mla_kv),
        w_ukv=(w_ukv, m_w_ukv, v_w_ukv), g_group_out=(g_group_out, m_g_group_out, v_g_group_out),
        w_out=(w_out, m_w_out, v_w_out), g_ffn_norm=(g_ffn_norm, m_g_ffn_norm, v_g_ffn_norm), w_up=(w_up, m_w_up, v_w_up),
        w_down=(w_down, m_w_down, v_w_down), g_final=(g_final, m_g_final, v_g_final)))


_GATHER_AT = {
    "in_proj0": [(0, "w_up")],
    "fox_attn0": [(0, "w_uq"), (0, "w_ukv"), (0, "w_down")],
    "mla_attn0": [(0, "w_out"), (1, "w_in")],
    "ffn_fwd0": [(1, "w_uq"), (1, "w_ukv"), (1, "w_down")],
    "fox_attn1": [(1, "w_out")],
    "mla_attn1": [(1, "w_up")],
}
_SCATTER_AT = {
    "fox_attn_bwd1": [(1, "w_down")],
    "mla_attn_bwd1": [(1, "w_up"), (1, "w_out")],
    "ffn_bwd0": [(1, "w_in")],
    "fox_attn_bwd0": [(0, "w_down")],
    "mla_attn_bwd0": [(0, "w_up"), (0, "w_out")],
    "in_proj_bwd0": [(0, "w_in")],
}


class _FullWeights:
    def __init__(self, per_layer):
        self.per_layer, self.grads = per_layer, {}

    def get(self, l, name):
        return self.per_layer[l][name]

    def comm(self, host):
        return None

    def done(self, host, results):
        pass

    def grad(self, l, name, blocks):
        self.grads[(l, name)] = blocks


class _ShardedWeights(_FullWeights):
    def __init__(self, shards):
        self.shards, self.full, self.grads, self.recv = shards, {}, {}, {}

    def get(self, l, name):
        if name in ("wk", "wv"):
            return _split_ukv(_cols_to_full(self.full[(l, "w_ukv")]))[0 if name == "wk" else 1]
        if name == "wq":
            return _pad_uq(_cols_to_full(self.full[(l, "w_uq")]))
        g = self.full[(l, name)]
        return g if name == "w_up" else g.reshape(NDEV * g.shape[1], g.shape[2])

    def comm(self, host):
        if host in _GATHER_AT:
            return _Comm("gather", [self.shards[k] for k in _GATHER_AT[host]])
        if host in _SCATTER_AT:
            return _Comm("exchange", [self.grads[k] for k in _SCATTER_AT[host]])
        return None

    def done(self, host, results):
        if host in _GATHER_AT:
            self.full.update(zip(_GATHER_AT[host], results))
        if host in _SCATTER_AT:
            self.recv.update(zip(_SCATTER_AT[host], results))


def _local_step(x0, target, wts, smalls, g_final, tb):
    depth = len(smalls)
    s, d = x0.shape
    saved = []
    xl = x0
    for l in range(depth):
        p = smalls[l]
        z, h, got = _norm_matmul(xl, p["g_mix"], wts.get(l, "w_in"), f"in_proj{l}", wts.comm(f"in_proj{l}"))
        wts.done(f"in_proj{l}", got)
        ya = _sgu_fwd(z, p["g_sgu"], p["w_s"], p["b_t"], tb, f"sgu_fwd{l}")
        yb, ret, states = _ret_fwd(z, tb, f"ret_fwd{l}")
        cum = _fox_prep(z, p["b_f"], f"fox_prep{l}")
        kc, vc, vtc, qtc = _kv_prep(z, 6, 7, 8, HEAD_DIM ** -0.5, f"fox_kv{l}")
        yc, lse_c, got = _attn_fwd(z, 6, HEAD_DIM, kc, vtc, HEAD_DIM ** -0.5, cum, f"fox_attn{l}", wts.comm(f"fox_attn{l}"))
        wts.done(f"fox_attn{l}", got)
        wq, wk, wv = wts.get(l, "wq"), wts.get(l, "wk"), wts.get(l, "wv")
        qd, kd, vd, vtd, cqn, ckvn, qtd = _mla_prep(z, p["gq"], p["gkv"], wq, wk, wv, tb, f"mla_prep{l}")
        yd, lse_d, got = _attn_fwd(qd, 0, 128, kd, vtd, _SCALE_D, None, f"mla_attn{l}", wts.comm(f"mla_attn{l}"))
        wts.done(f"mla_attn{l}", got)
        ys = (ya, yb, yc, yd)
        x1, yn = _out_proj(ys, p["g_go"], wts.get(l, "w_out"), xl, f"out_proj{l}")
        x2, u, h2, got = _ffn_fwd(x1, p["g_ffn"], wts.get(l, "w_up"), wts.get(l, "w_down"), f"ffn_fwd{l}", wts.comm(f"ffn_fwd{l}"))
        wts.done(f"ffn_fwd{l}", got)
        saved.append(dict(x=xl, z=z, h=h, ys=ys, ret=ret, states=states, cum=cum, lse_c=lse_c, kc=kc, vc=vc, qd=qd, kd=kd, vd=vd,
                          cqn=cqn, ckvn=ckvn, lse_d=lse_d, x1=x1, yn=yn, u=u, h2=h2, wq=wq, wk=wk, wv=wv, qtc=qtc, qtd=qtd))
        xl = x2

    lrow, dx, dg_final = _loss_head(xl, g_final, target, "loss_head")

    sm = [None] * depth
    for l in reversed(range(depth)):
        p, a = smalls[l], saved[l]
        dx1, du, dg_ffn, got = _ffn_bwd(dx, a["x1"], a["u"], p["g_ffn"], wts.get(l, "w_up"), wts.get(l, "w_down"), f"ffn_bwd{l}",
                                        wts.comm(f"ffn_bwd{l}"))
        wts.done(f"ffn_bwd{l}", got)
        dw_down = _mm_tn(a["u"], dx, f"dw_down{l}", a_fn=lambda t: jnp.square(jnp.maximum(t, 0.0)), out_dtype=_WIRE)
        wts.grad(l, "w_down", dw_down.reshape(NDEV, dw_down.shape[0] // NDEV, d))
        wts.grad(l, "w_up", _mm_tn(a["h2"], du, f"dw_up{l}", blocked=True, out_dtype=_WIRE))
        dya, dyb, dg_go, dot_c, dot_d, dl_c, dl_d = _out_proj_bwd(dx1, wts.get(l, "w_out"), a["ys"], p["g_go"],
                                                                  f"out_proj_bwd{l}")
        wts.grad(l, "w_out", _mm_tn(a["yn"], dx1, f"dw_out{l}", out_dtype=_WIRE).reshape(NDEV, d // NDEV, d))
        dz_a, dg_sgu, dw_s, db_t = _sgu_bwd(dya, a["z"], p["g_sgu"], p["w_s"], p["b_t"], tb, f"sgu_bwd{l}")
        dz_b = _ret_bwd(dyb, a["z"], a["ret"], a["states"], tb, f"ret_bwd{l}")
        dqt_c, dk_c, dv_c, dck, dcq, got = _attn_bwd(a["kc"], a["vc"], a["qtc"], dot_c, a["lse_c"], dl_c, HEAD_DIM,
                                                     HEAD_DIM ** -0.5, a["cum"], f"fox_attn_bwd{l}", _MXU,
                                                     wts.comm(f"fox_attn_bwd{l}"))
        wts.done(f"fox_attn_bwd{l}", got)
        dq_c = _untranspose(dqt_c, _MXU, f"fox_dq{l}")
        dqt_d, dk_d, dv_d, got = _attn_bwd(a["kd"], a["vd"], a["qtd"], dot_d, a["lse_d"], dl_d, 128, _SCALE_D, None,
                                           f"mla_attn_bwd{l}", F32, wts.comm(f"mla_attn_bwd{l}"))
        wts.done(f"mla_attn_bwd{l}", got)
        dz_cq, dz_ckv, dkr, dwq, dwk, dwv, dgq, dgkv = _mla_prep_bwd(dqt_d, dk_d, dv_d, a["z"], a["cqn"], a["ckvn"], p["gq"],
                                                                     p["gkv"], a["wq"], a["wk"], a["wv"], tb, f"mla_prep_bwd{l}")
        dz_misc, db_f = _fox_post(dcq, dck, a["z"], p["b_f"], dkr, f"fox_post{l}")
        dz = jnp.concatenate([dz_a, dz_b, dq_c, dk_c, dv_c, dz_cq, dz_ckv, dz_misc], axis=1)
        wts.grad(l, "w_in", _unpad_in_cols(_mm_tn(a["h"], dz, f"dw_in{l}", out_dtype=_WIRE)).reshape(NDEV, d // NDEV, N_IN))
        dx, dg_mix, got = _in_proj_bwd(dz, wts.get(l, "w_in"), a["x"], p["g_mix"], dx1, f"in_proj_bwd{l}",
                                       wts.comm(f"in_proj_bwd{l}"))
        wts.done(f"in_proj_bwd{l}", got)
        sm[l] = [dg_mix, dg_go, dg_ffn, dg_sgu, dw_s, db_t[:, :N_HEADS].T, db_f[0, :N_HEADS], dgq, dgkv, _unpad_uq(dwq),
                 _join_ukv(dwk, dwv)]
    return lrow, dx, sm, dg_final


def _reduce_and_update(loss, grad_x, recv, sm, dg_final, me, given):
    depth = len(sm)
    pieces = [t for l in range(depth) for t in sm[l]] + [dg_final]
    flat = jnp.concatenate([t.reshape(-1) for t in pieces])
    n_flat = flat.shape[0]
    unit = NDEV * 8 * 128
    n_pad = -(-n_flat // unit) * unit
    packed = jnp.pad(flat, (0, n_pad - n_flat)).reshape(NDEV, n_pad // (NDEV * 128), 128)
    red = _sum_slots(_exchange([packed], "scatter_small")[0], "sum_small")
    full = _all_gather([red], "gather_small")[0].reshape(-1)
    offs = np.cumsum([0] + [int(np.prod(t.shape)) for t in pieces])
    red_pieces = [full[int(offs[i]):int(offs[i + 1])].reshape(pieces[i].shape) for i in range(len(pieces))]
    per = len(sm[0])
    stack = lambda i: jnp.stack([red_pieces[l * per + i] for l in range(depth)])
    g_small = dict(g_mix_norm=stack(0), g_group_out=stack(1), g_ffn_norm=stack(2), g_sgu=stack(3), w_spatial=stack(4),
                   b_spatial=stack(5), b_forget=stack(6), g_mla_q=stack(7), g_mla_kv=stack(8), g_final=red_pieces[-1])
    cq, ckv = given["w_uq"][0].shape[2], given["w_ukv"][0].shape[2]
    g_small["w_uq"] = lax.dynamic_slice_in_dim(stack(9), me * cq, cq, axis=2)
    g_small["w_ukv"] = lax.dynamic_slice_in_dim(stack(10), me * ckv, ckv, axis=2)

    names = list(given)
    outs = {}
    for nme in names:
        wv_, mv_, vv_ = given[nme]
        shape = wv_.shape
        if nme in ("w_in", "w_out", "w_up", "w_down"):
            res = None
            for l in range(depth):
                res = _adamw_layer(recv[(l, nme)], wv_, mv_, vv_, l, res, f"adamw_{nme}{l}")
            outs[nme] = list(res)
        else:
            two = lambda t: t.reshape(-1, shape[-1]) if t.ndim > 1 else t.reshape(1, -1)
            res = _adamw(two(g_small[nme]), two(wv_), two(mv_), two(vv_), f"adamw_{nme}")
            outs[nme] = [r.reshape(shape) for r in res]
    return (loss, grad_x, *[outs[n][0] for n in names], *[outs[n][1] for n in names], *[outs[n][2] for n in names],
            *[outs[n][3] for n in names])
```

```python
import functools

import jax
import jax.numpy as jnp
import numpy as np
from jax import lax
from jax.experimental import pallas as pl
from jax.experimental.pallas import tpu as pltpu

F32 = jnp.float32
_MXU = jnp.bfloat16
_WIRE = jnp.bfloat16
EPS = 1e-6
NDEV = 8
AXES = ("x", "y", "c")
MESH = pl.DeviceIdType.MESH

N_HEADS = 4
HEAD_DIM = 64
GROUP = 256
CHUNK = 128
NZ = 2816
N_IN = 2724
MISC_F, MISC_KR = 0, 32
VMEM_LIMIT = 56 * 1024 * 1024

ADAM_LR, ADAM_B1, ADAM_B2, ADAM_EPS, ADAM_WD, ADAM_STEP = 0.001, 0.9, 0.999, 1e-08, 0.01, 10

SDS = jax.ShapeDtypeStruct


def _cp(*sem):
    return pltpu.CompilerParams(dimension_semantics=sem, vmem_limit_bytes=VMEM_LIMIT)


def _dot(a, b):
    return jnp.dot(a.astype(_MXU), b.astype(_MXU), preferred_element_type=F32)


def _dot_nt(a, b):
    return lax.dot_general(a.astype(_MXU), b.astype(_MXU), (((1,), (1,)), ((), ())), preferred_element_type=F32)


def _dot_tn(a, b):
    return lax.dot_general(a.astype(_MXU), b.astype(_MXU), (((0,), (0,)), ((), ())), preferred_element_type=F32)


def _dot_exact(a, b, dims=(((1,), (0,)), ((), ()))):
    return lax.dot_general(a, b, dims, precision=lax.Precision.HIGHEST, preferred_element_type=F32)


def _rms(x, g):
    return x * lax.rsqrt(jnp.mean(x * x, axis=-1, keepdims=True) + EPS) * g


def _rms_bwd(x, g, dy):
    xh = x * lax.rsqrt(jnp.mean(x * x, axis=-1, keepdims=True) + EPS)
    dxh = dy * g
    r = lax.rsqrt(jnp.mean(x * x, axis=-1, keepdims=True) + EPS)
    dx = r * (dxh - xh * jnp.mean(dxh * xh, axis=-1, keepdims=True))
    return dx, jnp.sum(dy * xh, axis=0, keepdims=True)


def _standardize(t):
    mu = jnp.mean(t, axis=-1, keepdims=True)
    tc = t - mu
    rs = lax.rsqrt(jnp.mean(tc * tc, axis=-1, keepdims=True) + EPS)
    return tc * rs, rs


def _standardize_bwd(yh, rs, dy):
    return rs * (dy - jnp.mean(dy, axis=-1, keepdims=True) - yh * jnp.mean(dy * yh, axis=-1, keepdims=True))


_GELU_C = 0.7978845608028654


def _gelu(x):
    return 0.5 * x * (1.0 + jnp.tanh(_GELU_C * (x + 0.044715 * x * x * x)))


def _gelu_grad(x):
    t = jnp.tanh(_GELU_C * (x + 0.044715 * x * x * x))
    return 0.5 * (1.0 + t) + 0.5 * x * (1.0 - t * t) * _GELU_C * (1.0 + 3 * 0.044715 * x * x)


def _sigmoid(x):
    return 1.0 / (1.0 + jnp.exp(-x))


def _swap_half(t, half):
    n = t.shape[-1]
    lane = lax.broadcasted_iota(jnp.int32, t.shape, t.ndim - 1)
    return jnp.where((lane % (2 * half)) < half, pltpu.roll(t, n - half, t.ndim - 1), pltpu.roll(t, half, t.ndim - 1))


def _lanes(table, width):
    return jnp.concatenate([table] * (width // table.shape[-1]), axis=-1)


def _rope(t, cos, sin, half):
    return t * cos + _swap_half(t, half) * sin


def _rope_bwd(d, cos, sin, half):
    return d * cos - _swap_half(d, half) * sin


def _tables(s):
    pos = jnp.arange(s, dtype=F32)[:, None]

    def cs(half):
        inv = jnp.power(10000.0, -jnp.arange(half, dtype=F32) / half)
        ang = pos * inv[None, :]
        return jnp.cos(ang), jnp.sin(ang)

    c32, s32 = cs(32)
    c16, s16 = cs(16)
    z = lambda w: jnp.zeros((s, w), F32)
    o = lambda w: jnp.ones((s, w), F32)
    t = {}
    t["b_cos"] = jnp.concatenate([c32, c32, c32, c32], 1)
    t["b_sin"] = jnp.concatenate([-s32, s32, -s32, s32], 1)
    t["q_cos"] = jnp.concatenate([o(64), c16, c16, z(32)], 1)
    t["q_sin"] = jnp.concatenate([z(64), -s16, s16, z(32)], 1)
    t["k_cos"] = jnp.concatenate([z(32), c16, c16, z(64)], 1)
    t["k_sin"] = jnp.concatenate([z(32), -s16, s16, z(64)], 1)
    lg = jnp.log1p(-jnp.exp2(-5.0 - jnp.arange(N_HEADS, dtype=F32)))
    j = jnp.arange(CHUNK, dtype=F32)
    rel = j[:, None] - j[None, :]
    t["decay"] = jnp.where(rel[None] >= 0, jnp.exp(jnp.maximum(rel, 0.0)[None] * lg[:, None, None]), 0.0)
    t["decay_t"] = jnp.swapaxes(t["decay"], 1, 2)

    def rows(e):
        return jnp.repeat(e.T, HEAD_DIM, axis=1)

    t["qw"] = rows(jnp.exp((j + 1.0)[None, :] * lg[:, None]))
    t["kw"] = rows(jnp.exp((CHUNK - 1 - j)[None, :] * lg[:, None]))
    t["kw2"] = rows(jnp.exp((CHUNK - j)[None, :] * lg[:, None]))
    t["qw0"] = rows(jnp.exp(j[None, :] * lg[:, None]))
    t["cd"] = jnp.repeat(jnp.exp(CHUNK * lg), HEAD_DIM)[None, :]
    e = np.zeros((128, 512), np.float32)
    for h in range(N_HEADS):
        for r in range(32):
            e[MISC_KR + r, 128 * h + 64 + r] = 1.0
    t["place"] = jnp.asarray(e)
    lane_head = np.arange(GROUP) // HEAD_DIM
    t["grp"] = jnp.asarray((lane_head[:, None] == lane_head[None, :]) / HEAD_DIM, _MXU)
    hsel = (np.arange(128)[:, None] == lane_head[None, :]).astype(np.float32)
    t["hsel"] = jnp.asarray(hsel)
    t["hselt"] = jnp.asarray(hsel.T, _MXU)
    return t


def _norm_matmul(x, g, w, name, comm=None):
    s, d = x.shape
    n = w.shape[1]
    tm, tn = min(512, s), 256
    ni = s // tm

    def body(*refs):
        (x_ref, g_ref, w_ref), (z_ref, h_ref), _, cc = _split_refs(refs, 3, 2, comm)
        i = pl.program_id(0)
        _host_gather(comm, cc, i, ni, late=True)
        h = _rms(x_ref[...], g_ref[...]).astype(h_ref.dtype)
        h_ref[...] = h
        for j in range(n // tn):
            z_ref[:, tn * j:tn * (j + 1)] = jnp.dot(h, w_ref[:, tn * j:tn * (j + 1)], preferred_element_type=F32)
        if comm is not None:
            @pl.when(i == ni - 1)
            def _():
                comm.wait(*cc)

    in_specs = [pl.BlockSpec((tm, d), lambda i: (i, 0)), pl.BlockSpec((1, d), lambda i: (0, 0)),
                pl.BlockSpec((d, n), lambda i: (0, 0))]
    out_specs = [pl.BlockSpec((tm, n), lambda i: (i, 0)), pl.BlockSpec((tm, d), lambda i: (i, 0))]
    out_shape = [SDS((s, n), F32), SDS((s, d), _MXU)]
    return _call_with_comm(body, (ni,), in_specs, out_specs, out_shape, [], [x, g, w], comm, ("arbitrary",), name)


def _mm_tn(a, b, name, *, a_fn=None, blocked=False, out_dtype=F32):
    k, m = a.shape
    n = b.shape[1]
    tm, tk = min(1024, m), min(1024, k)
    tn = next(t for t in (1408, 1024, 512, 256, 128) if n % t == 0)
    assert m % tm == 0 and k % tk == 0
    nk = k // tk

    def body(a_ref, b_ref, o_ref, acc):
        kk = pl.program_id(2)

        @pl.when(kk == 0)
        def _():
            acc[...] = jnp.zeros_like(acc)

        av = a_ref[...]
        if a_fn is not None:
            av = a_fn(av.astype(F32))
        acc[...] += _dot_tn(av, b_ref[...])

        @pl.when(kk == nk - 1)
        def _():
            if blocked:
                for c in range(tn // 512):
                    o_ref[c] = acc[:, 512 * c:512 * (c + 1)].astype(o_ref.dtype)
            else:
                o_ref[...] = acc[...].astype(o_ref.dtype)

    if blocked:
        assert tn % 512 == 0
        out_spec = pl.BlockSpec((tn // 512, tm, 512), lambda i, j, kk: (j, i, 0))
        out_shape = SDS((n // 512, m, 512), out_dtype)
    else:
        out_spec = pl.BlockSpec((tm, tn), lambda i, j, kk: (i, j))
        out_shape = SDS((m, n), out_dtype)
    return pl.pallas_call(
        body, grid=(m // tm, n // tn, nk),
        in_specs=[pl.BlockSpec((tk, tm), lambda i, j, kk: (kk, i)), pl.BlockSpec((tk, tn), lambda i, j, kk: (kk, j))],
        out_specs=out_spec, out_shape=out_shape, scratch_shapes=[pltpu.VMEM((tm, tn), F32)],
        compiler_params=_cp("parallel", "parallel", "arbitrary"), name=name)(a, b)


def _split_dot(x, m):
    hi = x.astype(_MXU)
    lo = (x - hi.astype(F32)).astype(_MXU)
    return jnp.dot(hi, m, preferred_element_type=F32) + jnp.dot(lo, m, preferred_element_type=F32)


def _gstandardize(t, grp):
    tc = t - _split_dot(t, grp)
    rs = lax.rsqrt(_split_dot(tc * tc, grp) + EPS)
    return tc * rs, rs


def _gstandardize_bwd(yh, rs, dy, grp):
    return rs * (dy - _split_dot(dy, grp) - yh * _split_dot(dy * yh, grp))


def _head_select(parts):
    hid = lax.broadcasted_iota(jnp.int32, parts[0].shape, 1) // HEAD_DIM
    return jnp.where(hid == 0, parts[0], jnp.where(hid == 1, parts[1], jnp.where(hid == 2, parts[2], parts[3])))


def _head_masked(x):
    hid = lax.broadcasted_iota(jnp.int32, x.shape, 1) // HEAD_DIM
    return [jnp.where(hid == h, x, jnp.zeros_like(x)) for h in range(N_HEADS)]


def _tril(w):
    r = lax.broadcasted_iota(jnp.int32, w.shape, 0)
    c = lax.broadcasted_iota(jnp.int32, w.shape, 1)
    return jnp.where(r >= c, w, 0.0)


def _sgu_mixed(vgb, wcs, bias, nchunk):
    ms = [[jnp.dot(wcs[h], vgb[CHUNK * c:CHUNK * (c + 1)], preferred_element_type=F32) for h in range(N_HEADS)]
          for c in range(nchunk)]
    return [_head_select(ms[c]) + bias for c in range(nchunk)]


def _sgu_fwd(z, gain, w_s, b_t, tb, name):
    s = z.shape[0]
    tm = min(512, s)
    const = lambda a: pl.BlockSpec(a.shape, lambda i: (0,) * a.ndim)

    def body(u_ref, v_ref, g_ref, w_ref, b_ref, grp_ref, hsel_ref, y_ref):
        u = _gelu(u_ref[...])
        vh, _ = _gstandardize(_gelu(v_ref[...]), grp_ref[...])
        vgb = (vh * g_ref[...]).astype(_MXU)
        bias = _dot_exact(b_ref[...], hsel_ref[...])
        wcs = [_tril(w_ref[h]).astype(_MXU) for h in range(N_HEADS)]
        for c, mixed in enumerate(_sgu_mixed(vgb, wcs, bias, tm // CHUNK)):
            r = slice(CHUNK * c, CHUNK * (c + 1))
            y_ref[r, :] = u[r] * mixed

    return pl.pallas_call(
        body, grid=(s // tm,),
        in_specs=[pl.BlockSpec((tm, GROUP), lambda i: (i, 0)), pl.BlockSpec((tm, GROUP), lambda i: (i, 1)),
                  pl.BlockSpec((1, GROUP), lambda i: (0, 0)), pl.BlockSpec((N_HEADS, CHUNK, CHUNK), lambda i: (0, 0, 0)),
                  pl.BlockSpec((CHUNK, 128), lambda i: (0, 0)), const(tb["grp"]), const(tb["hsel"])],
        out_specs=pl.BlockSpec((tm, GROUP), lambda i: (i, 0)), out_shape=SDS((s, GROUP), F32),
        compiler_params=_cp("parallel"), name=name)(z, z, gain, w_s, b_t, tb["grp"], tb["hsel"])


def _sgu_bwd(dy, z, gain, w_s, b_t, tb, name):
    s = z.shape[0]
    tm = min(512, s)
    nchunk = tm // CHUNK
    const = lambda a: pl.BlockSpec(a.shape, lambda i: (0,) * a.ndim)

    def body(dy_ref, u_ref, v_ref, g_ref, w_ref, b_ref, grp_ref, hsel_ref, hselt_ref, dz_ref, dg_ref, dw_ref, db_ref):
        @pl.when(pl.program_id(0) == 0)
        def _():
            dg_ref[...] = jnp.zeros_like(dg_ref)
            dw_ref[...] = jnp.zeros_like(dw_ref)
            db_ref[...] = jnp.zeros_like(db_ref)

        grp = grp_ref[...]
        u_pre, v_pre, gain_v = u_ref[...], v_ref[...], g_ref[...]
        u = _gelu(u_pre)
        vh, rs = _gstandardize(_gelu(v_pre), grp)
        vgb = (vh * gain_v).astype(_MXU)
        dyv = dy_ref[...]
        bias = _dot_exact(b_ref[...], hsel_ref[...])
        wfs = [_tril(w_ref[h]) for h in range(N_HEADS)]
        wcs = [w.astype(_MXU) for w in wfs]
        wts = [w.T.astype(_MXU) for w in wfs]
        mixed = _sgu_mixed(vgb, wcs, bias, nchunk)
        gu = _gelu_grad(u_pre)
        dms, dmh = [], []
        for c in range(nchunk):
            r = slice(CHUNK * c, CHUNK * (c + 1))
            dz_ref[r, 0:GROUP] = (dyv[r] * mixed[c] * gu[r]).astype(dz_ref.dtype)
            dm = dyv[r] * u[r]
            dms.append(dm)
            dmh.append([m.astype(_MXU) for m in _head_masked(dm)])
        dws = [sum(lax.dot_general(dmh[c][h], vgb[CHUNK * c:CHUNK * (c + 1)], (((1,), (1,)), ((), ())),
                                   preferred_element_type=F32) for c in range(nchunk)) for h in range(N_HEADS)]
        dvg = jnp.concatenate([sum(jnp.dot(wts[h], dmh[c][h], preferred_element_type=F32) for h in range(N_HEADS))
                               for c in range(nchunk)], axis=0)
        for h in range(N_HEADS):
            dw_ref[h] += _tril(dws[h])
        db_ref[...] += sum(_split_dot(dm, hselt_ref[...]) for dm in dms)
        dg_ref[...] += jnp.sum(dvg * vh, axis=0, keepdims=True)
        dv = _gstandardize_bwd(vh, rs, dvg * gain_v, grp)
        dz_ref[:, GROUP:2 * GROUP] = (dv * _gelu_grad(v_pre)).astype(dz_ref.dtype)

    consts = [tb["grp"], tb["hsel"], tb["hselt"]]
    return pl.pallas_call(
        body, grid=(s // tm,),
        in_specs=[pl.BlockSpec((tm, GROUP), lambda i: (i, 0)),
                  pl.BlockSpec((tm, GROUP), lambda i: (i, 0)), pl.BlockSpec((tm, GROUP), lambda i: (i, 1)),
                  pl.BlockSpec((1, GROUP), lambda i: (0, 0)), pl.BlockSpec((N_HEADS, CHUNK, CHUNK), lambda i: (0, 0, 0)),
                  pl.BlockSpec((CHUNK, 128), lambda i: (0, 0))] + [const(a) for a in consts],
        out_specs=[pl.BlockSpec((tm, 2 * GROUP), lambda i: (i, 0)), pl.BlockSpec((1, GROUP), lambda i: (0, 0)),
                   pl.BlockSpec((N_HEADS, CHUNK, CHUNK), lambda i: (0, 0, 0)), pl.BlockSpec((CHUNK, 128), lambda i: (0, 0))],
        out_shape=[SDS((s, 2 * GROUP), _MXU), SDS((1, GROUP), F32), SDS((N_HEADS, CHUNK, CHUNK), F32), SDS((CHUNK, 128), F32)],
        compiler_params=_cp("arbitrary"), name=name)(dy, z, z, gain, w_s, b_t, *consts)


_SCALE_B = HEAD_DIM ** -0.5
RET_CHUNKS = 4


def _block_diag(compact):
    full = jnp.concatenate([compact] * N_HEADS, axis=0)
    r = lax.broadcasted_iota(jnp.int32, full.shape, 0) // HEAD_DIM
    c = lax.broadcasted_iota(jnp.int32, full.shape, 1) // HEAD_DIM
    return jnp.where(r == c, full, 0.0)


def _diag_blocks(full):
    c = lax.broadcasted_iota(jnp.int32, (HEAD_DIM, GROUP), 1) // HEAD_DIM
    return sum(jnp.where(c == h, full[HEAD_DIM * h:HEAD_DIM * (h + 1), :], 0.0) for h in range(N_HEADS))


def _ret_fwd(z, tb, name):
    s = z.shape[0]
    nc = s // CHUNK
    per = min(RET_CHUNKS, nc)
    rows = per * CHUNK
    row = lambda col: pl.BlockSpec((rows, GROUP), lambda n, col=col: (n, col))
    const = lambda shape: pl.BlockSpec(shape, lambda n: (0,) * len(shape))

    def body(q_ref, k_ref, v_ref, g_ref, cos_ref, sin_ref, dec_ref, qw_ref, kw_ref, cd_ref, grp_ref, y_ref, o_ref, st_ref, state):
        @pl.when(pl.program_id(0) == 0)
        def _():
            state[...] = jnp.zeros_like(state)

        cos, sin = _lanes(cos_ref[...], GROUP), _lanes(sin_ref[...], GROUP)
        q = _rope(q_ref[...], cos, sin, 32)
        k = _rope(k_ref[...], cos, sin, 32) * _SCALE_B
        v = v_ref[...]
        g = g_ref[...]
        rcs = [slice(CHUNK * c, CHUNK * (c + 1)) for c in range(per)]
        vms = [[t.astype(_MXU) for t in _head_masked(v[r])] for r in rcs]
        scs = [[_dot_nt(t.astype(_MXU), k[r]) for t in _head_masked(q[r])] for r in rcs]
        kvs = [_dot_tn(k[r] * kw_ref[...], v[r]) for r in rcs]
        st = state[...]
        crosses = []
        for c, r in enumerate(rcs):
            st_ref[c] = st
            crosses.append(_dot(q[r] * qw_ref[...], _block_diag(st)))
            st = cd_ref[...] * st + _diag_blocks(kvs[c])
        state[...] = st
        outs = []
        for c in range(per):
            scd = [(scs[c][h] * dec_ref[h]).astype(_MXU) for h in range(N_HEADS)]
            outs.append(crosses[c] + sum(jnp.dot(scd[h], vms[c][h], preferred_element_type=F32) for h in range(N_HEADS)))
        o = jnp.concatenate(outs, axis=0)
        o_ref[...] = o
        yh, _ = _gstandardize(o, grp_ref[...])
        y_ref[...] = g * _sigmoid(g) * yh

    return pl.pallas_call(
        body, grid=(nc // per,),
        in_specs=[row(2), row(3), row(4), row(5), pl.BlockSpec((rows, 128), lambda n: (n, 0)),
                  pl.BlockSpec((rows, 128), lambda n: (n, 0)), const((N_HEADS, CHUNK, CHUNK)),
                  const((CHUNK, GROUP)), const((CHUNK, GROUP)), const((1, GROUP)), const((GROUP, GROUP))],
        out_specs=[pl.BlockSpec((rows, GROUP), lambda n: (n, 0)), pl.BlockSpec((rows, GROUP), lambda n: (n, 0)),
                   pl.BlockSpec((per, HEAD_DIM, GROUP), lambda n: (n, 0, 0))],
        out_shape=[SDS((s, GROUP), F32), SDS((s, GROUP), F32), SDS((nc, HEAD_DIM, GROUP), F32)],
        scratch_shapes=[pltpu.VMEM((HEAD_DIM, GROUP), F32)],
        compiler_params=_cp("arbitrary"), name=name)(z, z, z, z, tb["b_cos"], tb["b_sin"], tb["decay"], tb["qw"], tb["kw"], tb["cd"],
                                                       tb["grp"])


def _ret_bwd(dy, z, o_pre, states, tb, name):
    s = z.shape[0]
    nc = s // CHUNK
    per = min(RET_CHUNKS, nc)
    rows = per * CHUNK
    ns = nc // per
    rev = lambda col: pl.BlockSpec((rows, GROUP), lambda n, col=col: (ns - 1 - n, col))
    const = lambda shape: pl.BlockSpec(shape, lambda n: (0,) * len(shape))

    def body(dy_ref, q_ref, k_ref, v_ref, g_ref, o_ref, st_ref, cos_ref, sin_ref, dec_ref, dect_ref, qw_ref, kw2_ref, qw0_ref,
             cd_ref, grp_ref, dz_ref, rstate):
        @pl.when(pl.program_id(0) == 0)
        def _():
            rstate[...] = jnp.zeros_like(rstate)

        cos, sin = _lanes(cos_ref[...], GROUP), _lanes(sin_ref[...], GROUP)
        q = _rope(q_ref[...], cos, sin, 32)
        k = _rope(k_ref[...], cos, sin, 32) * _SCALE_B
        v = v_ref[...]
        g = g_ref[...]
        dyv = dy_ref[...]
        sg = _sigmoid(g)
        yh, rs = _gstandardize(o_ref[...], grp_ref[...])
        dz_ref[:, 3 * GROUP:4 * GROUP] = (dyv * yh * (sg * (1.0 + g * (1.0 - sg)))).astype(dz_ref.dtype)
        do = _gstandardize_bwd(yh, rs, dyv * (g * sg), grp_ref[...])
        hs = range(N_HEADS)
        rcs = [slice(CHUNK * c, CHUNK * (c + 1)) for c in range(per)]
        mask = lambda t: [m.astype(_MXU) for m in _head_masked(t)]
        qms, kms, vms, doms = ([mask(t[r]) for r in rcs] for t in (q, k, v, do))
        dps = [[_dot_nt(doms[c][h], v[r]) for h in hs] for c, r in enumerate(rcs)]
        pts = [[_dot_nt(kms[c][h], q[r]) for h in hs] for c, r in enumerate(rcs)]
        dpts = [[_dot_nt(vms[c][h], do[r]) for h in hs] for c, r in enumerate(rcs)]
        dq_x = [_dot_nt(do[r] * qw_ref[...], _block_diag(st_ref[c])) for c, r in enumerate(rcs)]
        r_new = [_dot_tn(q[r] * qw0_ref[...], do[r]) for r in rcs]
        rr = rstate[...]
        dk_x, dv_x = [None] * per, [None] * per
        for c in reversed(range(per)):
            r_bd = _block_diag(rr)
            dk_x[c] = _dot_nt(v[rcs[c]] * kw2_ref[...], r_bd)
            dv_x[c] = _dot(k[rcs[c]] * kw2_ref[...], r_bd)
            rr = cd_ref[...] * rr + _diag_blocks(r_new[c])
        rstate[...] = rr
        dqs, dks, dvs = [], [], []
        for c in range(per):
            dpd = [(dps[c][h] * dec_ref[h]).astype(_MXU) for h in hs]
            dptd = [(dpts[c][h] * dect_ref[h]).astype(_MXU) for h in hs]
            ptd = [(pts[c][h] * dect_ref[h]).astype(_MXU) for h in hs]
            dqs.append(dq_x[c] + sum(jnp.dot(dpd[h], kms[c][h], preferred_element_type=F32) for h in hs))
            dks.append(dk_x[c] + sum(jnp.dot(dptd[h], qms[c][h], preferred_element_type=F32) for h in hs))
            dvs.append(dv_x[c] + sum(jnp.dot(ptd[h], doms[c][h], preferred_element_type=F32) for h in hs))
        dz_ref[:, 0:GROUP] = _rope_bwd(jnp.concatenate(dqs, axis=0), cos, sin, 32).astype(dz_ref.dtype)
        dz_ref[:, GROUP:2 * GROUP] = _rope_bwd(jnp.concatenate(dks, axis=0) * _SCALE_B, cos, sin, 32).astype(dz_ref.dtype)
        dz_ref[:, 2 * GROUP:3 * GROUP] = jnp.concatenate(dvs, axis=0).astype(dz_ref.dtype)

    r0 = lambda: pl.BlockSpec((rows, GROUP), lambda n: (ns - 1 - n, 0))
    r128 = lambda: pl.BlockSpec((rows, 128), lambda n: (ns - 1 - n, 0))
    return pl.pallas_call(
        body, grid=(ns,),
        in_specs=[r0(), rev(2), rev(3), rev(4), rev(5), r0(), pl.BlockSpec((per, HEAD_DIM, GROUP), lambda n: (ns - 1 - n, 0, 0)),
                  r128(), r128(), const((N_HEADS, CHUNK, CHUNK)), const((N_HEADS, CHUNK, CHUNK)), const((CHUNK, GROUP)),
                  const((CHUNK, GROUP)), const((CHUNK, GROUP)), const((1, GROUP)), const((GROUP, GROUP))],
        out_specs=pl.BlockSpec((rows, 4 * GROUP), lambda n: (ns - 1 - n, 0)),
        out_shape=SDS((s, 4 * GROUP), _MXU), scratch_shapes=[pltpu.VMEM((HEAD_DIM, GROUP), F32)],
        compiler_params=_cp("arbitrary"), name=name)(
            dy, z, z, z, z, o_pre, states, tb["b_cos"], tb["b_sin"], tb["decay"], tb["decay_t"], tb["qw"], tb["kw2"], tb["qw0"],
            tb["cd"], tb["grp"])


TQ = 256


def _log_sigmoid(x):
    return jnp.minimum(x, 0.0) - jnp.log1p(jnp.exp(-jnp.abs(x)))


def _fox_prep(z, b_f, name):
    s = z.shape[0]
    nb = s // TQ

    def body(m_ref, b_ref, cc_ref, carry):
        @pl.when(pl.program_id(0) == 0)
        def _():
            carry[...] = jnp.zeros_like(carry)

        lane = lax.broadcasted_iota(jnp.int32, (TQ, 128), 1)
        logf = jnp.where(lane < N_HEADS, _log_sigmoid(m_ref[...] + b_ref[...]), 0.0)
        r = lax.broadcasted_iota(jnp.int32, (TQ, TQ), 0)
        c = lax.broadcasted_iota(jnp.int32, (TQ, TQ), 1)
        tri = jnp.where(r >= c, 1.0, 0.0).astype(F32)
        cum = _dot_exact(tri, logf) + carry[...]
        cc_ref[...] = cum * LOG2E
        carry[...] = cum[TQ - 1:TQ, :]

    return pl.pallas_call(
        body, grid=(nb,),
        in_specs=[pl.BlockSpec((TQ, 128), lambda i: (i, NZ // 128 - 1)), pl.BlockSpec((1, 128), lambda i: (0, 0))],
        out_specs=pl.BlockSpec((TQ, 128), lambda i: (i, 0)),
        out_shape=SDS((s, 128), F32), scratch_shapes=[pltpu.VMEM((1, 128), F32)],
        compiler_params=_cp("arbitrary"), name=name)(z, b_f)


def _fox_post(dcr, dcq, z, b_f, dkr, name):
    s = z.shape[0]
    nb = s // TQ

    def body(dc_ref, dcq_ref, m_ref, b_ref, dkr_ref, dz_ref, db_ref, carry):
        @pl.when(pl.program_id(0) == 0)
        def _():
            carry[...] = jnp.zeros_like(carry)
            db_ref[...] = jnp.zeros_like(db_ref)

        r = lax.broadcasted_iota(jnp.int32, (TQ, TQ), 0)
        c = lax.broadcasted_iota(jnp.int32, (TQ, TQ), 1)
        triu = jnp.where(c >= r, 1.0, 0.0).astype(F32)
        dc = jnp.concatenate([dc_ref[0], jnp.zeros((120, TQ), F32)], axis=0)
        dlogf = _dot_exact(triu, dc, (((1,), (1,)), ((), ()))) + _dot_exact(triu, dcq_ref[...]) + carry[...]
        carry[...] = dlogf[0:1, :]
        x = m_ref[...] + b_ref[...]
        lane = lax.broadcasted_iota(jnp.int32, (TQ, 128), 1)
        df = jnp.where(lane < N_HEADS, dlogf * _sigmoid(-x), 0.0)
        db_ref[...] += jnp.sum(df, axis=0, keepdims=True)
        dz_ref[...] = (df + dkr_ref[...]).astype(dz_ref.dtype)

    rv = lambda i: nb - 1 - i
    return pl.pallas_call(
        body, grid=(nb,),
        in_specs=[pl.BlockSpec((1, 8, TQ), lambda i: (rv(i), 0, 0)), pl.BlockSpec((TQ, 128), lambda i: (rv(i), 0)),
                  pl.BlockSpec((TQ, 128), lambda i: (rv(i), NZ // 128 - 1)),
                  pl.BlockSpec((1, 128), lambda i: (0, 0)), pl.BlockSpec((TQ, 128), lambda i: (rv(i), 0))],
        out_specs=[pl.BlockSpec((TQ, 128), lambda i: (rv(i), 0)), pl.BlockSpec((1, 128), lambda i: (0, 0))],
        out_shape=[SDS((s, 128), _MXU), SDS((1, 128), F32)], scratch_shapes=[pltpu.VMEM((1, 128), F32)],
        compiler_params=_cp("arbitrary"), name=name)(dcr, dcq, z, b_f, dkr)


NEG = -1e30


def _causal_mask(shape, transposed=False):
    r = lax.broadcasted_iota(jnp.int32, shape, 0)
    c = lax.broadcasted_iota(jnp.int32, shape, 1)
    return (c >= r) if transposed else (r >= c)


TKV = 512


def _key_block(s):
    return min(TKV, s)


def _diag_mask(shape, off):
    r = lax.broadcasted_iota(jnp.int32, shape, 0)
    c = lax.broadcasted_iota(jnp.int32, shape, 1)
    return c + off >= r


def _head_lanes(h, dqk):
    return slice(128 * (h // 2), 128 * (h // 2) + 128) if dqk == HEAD_DIM else slice(128 * h, 128 * h + 128)


def _keep_half(x, a, axis):
    idx = lax.broadcasted_iota(jnp.int32, x.shape, axis)
    return jnp.where((idx < HEAD_DIM) if a == 0 else (idx >= HEAD_DIM), x, jnp.zeros_like(x))


def _scaled_qt(q, scale):
    qs = q.astype(F32) * (scale * LOG2E)
    return [qs[TQ * b:TQ * (b + 1)].T.astype(_MXU) for b in range(q.shape[0] // TQ)]


def _kv_prep(z, qcol, kcol, vcol, scale, name):
    s = z.shape[0]
    tk = _key_block(s)
    nk = s // tk

    def body(q_ref, k_ref, v_ref, kb_ref, vb_ref, vt_ref, qt_ref):
        kb_ref[...] = k_ref[...].astype(_MXU)
        v = v_ref[...]
        vb_ref[...] = v.astype(_MXU)
        vt_ref[0] = v.T.astype(_MXU)
        for b, t in enumerate(_scaled_qt(q_ref[...], scale)):
            qt_ref[b] = t

    blk = pl.BlockSpec((tk, GROUP), lambda i: (i, 0))
    col = lambda c: pl.BlockSpec((tk, GROUP), lambda i, c=c: (i, c))
    return pl.pallas_call(
        body, grid=(nk,), in_specs=[col(qcol), col(kcol), col(vcol)],
        out_specs=[blk, blk, pl.BlockSpec((1, GROUP, tk), lambda i: (i, 0, 0)),
                   pl.BlockSpec((tk // TQ, GROUP, TQ), lambda i: (i, 0, 0))],
        out_shape=[SDS((s, GROUP), _MXU), SDS((s, GROUP), _MXU), SDS((nk, GROUP, tk), _MXU), SDS((s // TQ, GROUP, TQ), _MXU)],
        compiler_params=_cp("parallel"), name=name)(z, z, z)


LOG2E = 1.4426950408889634


def _attn_fwd(q, qcol, dqk, kb, vt, scale, ck2, name, comm=None):
    s = q.shape[0]
    nq = s // TQ
    tk = _key_block(s)
    ratio = tk // TQ
    wq = N_HEADS * dqk
    bias = ck2 is not None

    def body(*refs):
        ins, (o_ref, l_ref), _, cc = _split_refs(refs, 4 if bias else 3, 2, comm)
        if bias:
            q_ref, k_ref, vt_ref, cc_ref = ins
        else:
            q_ref, k_ref, vt_ref = ins
        i = pl.program_id(0)
        _host_gather(comm, cc, i, nq)
        qts = []
        for h in range(N_HEADS):
            qt = (q_ref[:, _head_lanes(h, dqk)].astype(F32) * (scale * LOG2E)).T
            qts.append((_keep_half(qt, h % 2, 0) if dqk == HEAD_DIM else qt).astype(_MXU))

        def step(j, carry, off):
            r0 = pl.multiple_of(j * tk, tk)
            vtj = vt_ref[j]
            sts = [jnp.dot(k_ref[pl.ds(r0, tk), _head_lanes(h, dqk)], qts[h], preferred_element_type=F32)
                   for h in range(N_HEADS)]
            stats, ps = [], []
            for h in range(N_HEADS):
                m, l, _ = carry[3 * h:3 * h + 3]
                st = sts[h]
                if bias:
                    st = st - cc_ref[pl.ds(r0, tk), h:h + 1]
                if off is not None:
                    st = jnp.where(_diag_mask(st.shape, off), st, NEG)
                m_new = jnp.maximum(m, jnp.max(st, axis=0, keepdims=True))
                alpha = jnp.exp2(m - m_new)
                p = jnp.exp2(st - m_new)
                stats.append((m_new, alpha * l + jnp.sum(p, axis=0, keepdims=True), alpha))
                ps.append(p.astype(_MXU))
            out = []
            for h in range(N_HEADS):
                m_new, l, alpha = stats[h]
                acc = alpha * carry[3 * h + 2] + jnp.dot(vtj[HEAD_DIM * h:HEAD_DIM * (h + 1), :], ps[h],
                                                         preferred_element_type=F32)
                out += [m_new, l, acc]
            return tuple(out)

        init = (jnp.full((1, TQ), NEG, F32), jnp.zeros((1, TQ), F32), jnp.zeros((HEAD_DIM, TQ), F32)) * N_HEADS
        jd = i // ratio
        carry = lax.fori_loop(0, jd, functools.partial(step, off=None), init)
        carry = step(jd, carry, TQ * (i % ratio))
        l_ref[...] = jnp.zeros_like(l_ref)
        for h in range(N_HEADS):
            l_ref[0, h:h + 1, :] = carry[3 * h] + jnp.log2(carry[3 * h + 1])
        for p in range(2):
            ot = jnp.concatenate([carry[6 * p + 2] / carry[6 * p + 1], carry[6 * p + 5] / carry[6 * p + 4]], axis=0)
            o_ref[:, 128 * p:128 * (p + 1)] = ot.T
        if comm is not None:
            @pl.when(i == nq - 1)
            def _():
                comm.wait(*cc)

    rows = pl.BlockSpec((1, 8, TQ), lambda i: (i, 0, 0))
    in_specs = [pl.BlockSpec((TQ, wq), lambda i: (i, qcol)), pl.BlockSpec((s, wq), lambda i: (0, 0)),
                pl.BlockSpec((s // tk, GROUP, tk), lambda i: (0, 0, 0))]
    args = [q, kb, vt]
    if bias:
        in_specs.append(pl.BlockSpec((s, 128), lambda i: (0, 0)))
        args.append(ck2)
    out_specs = [pl.BlockSpec((TQ, GROUP), lambda i: (i, 0)), rows]
    out_shape = [SDS((s, GROUP), F32), SDS((nq, 8, TQ), F32)]
    return _call_with_comm(body, (nq,), in_specs, out_specs, out_shape, [], args, comm, ("arbitrary",), name)


def _call_with_comm(body, grid, in_specs, out_specs, out_shape, scratch, args, comm, semantics, name):
    n_out = len(out_shape)
    if comm is not None:
        in_specs, out_specs = in_specs + comm.in_specs, out_specs + comm.out_specs
        out_shape, scratch, args = out_shape + comm.out_shape, scratch + comm.scratch, list(args) + comm.arrs
    res = pl.pallas_call(body, grid=grid, in_specs=in_specs, out_specs=out_specs, out_shape=out_shape,
                         scratch_shapes=scratch, compiler_params=_cp(*semantics), name=name)(*args)
    return (*res[:n_out], list(res[n_out:]))


def _attn_bwd(kb, vb, qt, dot, lse, dl, dqk, scale, ck2, name, kv_dtype, comm=None):
    s = kb.shape[0]
    nq = s // TQ
    tk = _key_block(s)
    ratio = tk // TQ
    nkb = s // tk
    wq = N_HEADS * dqk
    bias = ck2 is not None

    def body(*refs):
        ins, outs, _, cc = _split_refs(refs, 7 if bias else 6, 5 if bias else 3, comm)
        if bias:
            k_ref, v_ref, qt_ref, dot_ref, l_ref, d_ref, cc_ref = ins
            dqt_ref, dk_ref, dv_ref, dck_ref, dcq_ref = outs
        else:
            k_ref, v_ref, qt_ref, dot_ref, l_ref, d_ref = ins
            dqt_ref, dk_ref, dv_ref = outs
        j = pl.program_id(0)

        @pl.when(j == 0)
        def _():
            if comm is not None:
                comm.start(*cc)
            dqt_ref[...] = jnp.zeros_like(dqt_ref)
            if bias:
                dcq_ref[...] = jnp.zeros_like(dcq_ref)

        ks, kts, vs = [], [], []
        for h in range(N_HEADS):
            k2 = k_ref[:, _head_lanes(h, dqk)]
            if dqk == HEAD_DIM:
                k2 = _keep_half(k2, h % 2, 1)
            ks.append(k2)
            kts.append(k2.astype(F32).T.astype(_MXU))
            vs.append(_keep_half(v_ref[:, _head_lanes(h, HEAD_DIM)], h % 2, 1))
        cks = [cc_ref[:, h:h + 1] for h in range(N_HEADS)] if bias else None

        nt = (((1,), (1,)), ((), ()))

        def step(i, carry, off):
            qti, doti, li, di = qt_ref[i], dot_ref[i], l_ref[i], d_ref[i]
            qls = [_head_lanes(h, dqk) for h in range(N_HEADS)]
            vls = [_head_lanes(h, HEAD_DIM) for h in range(N_HEADS)]
            sts, dpts = [], []
            for h in range(N_HEADS):
                sts.append(jnp.dot(ks[h], qti[qls[h], :], preferred_element_type=F32))
                dpts.append(jnp.dot(vs[h], doti[vls[h], :], preferred_element_type=F32))
            pbs, dsbs, dcks = [], [], []
            for h in range(N_HEADS):
                st = sts[h] - li[h:h + 1, :]
                if bias:
                    st = st - cks[h]
                p = jnp.exp2(st)
                if off is not None:
                    p = jnp.where(_diag_mask(p.shape, off), p, 0.0)
                dst = p * (dpts[h] - di[h:h + 1, :])
                pbs.append(p.astype(_MXU))
                dsbs.append(dst.astype(_MXU))
                if bias:
                    dcks.append(carry[3 * h + 2] + jnp.sum(dst, axis=1, keepdims=True))
                    dcq_ref[i, h:h + 1, :] += jnp.sum(dst, axis=0, keepdims=True)
                else:
                    dcks.append(carry[3 * h + 2])
            out = []
            for h in range(N_HEADS):
                dvt = carry[3 * h + 1] + lax.dot_general(doti[HEAD_DIM * h:HEAD_DIM * (h + 1), :], pbs[h], nt,
                                                         preferred_element_type=F32)
                dkt = carry[3 * h] + lax.dot_general(qti[dqk * h:dqk * (h + 1), :], dsbs[h], nt, preferred_element_type=F32)
                dqt_ref[i, qls[h], :] += jnp.dot(kts[h], dsbs[h], preferred_element_type=F32) * scale
                out += [dkt, dvt, dcks[h]]
            return tuple(out)

        carry = (jnp.zeros((dqk, tk), F32), jnp.zeros((HEAD_DIM, tk), F32), jnp.zeros((tk, 1), F32)) * N_HEADS
        for r in range(ratio):
            carry = step(ratio * j + r, carry, TQ * r)
        carry = lax.fori_loop(ratio * (j + 1), nq, functools.partial(step, off=None), carry)
        for p in range(2):
            dv_ref[:, 128 * p:128 * (p + 1)] = jnp.concatenate([carry[6 * p + 1], carry[6 * p + 4]], axis=0).T.astype(dv_ref.dtype)
            if dqk == HEAD_DIM:
                dk_ref[:, 128 * p:128 * (p + 1)] = (jnp.concatenate([carry[6 * p], carry[6 * p + 3]], axis=0).T
                                                    * (1.0 / LOG2E)).astype(dk_ref.dtype)
        if dqk != HEAD_DIM:
            for h in range(N_HEADS):
                dk_ref[:, 128 * h:128 * (h + 1)] = (carry[3 * h].T * (1.0 / LOG2E)).astype(dk_ref.dtype)
        if bias:
            dck_ref[...] = jnp.zeros_like(dck_ref)
            for h in range(N_HEADS):
                dck_ref[:, h:h + 1] = -carry[3 * h + 2]
        if comm is not None:
            @pl.when(j == nkb - 1)
            def _():
                comm.wait(*cc)

    blk = lambda w: pl.BlockSpec((tk, w), lambda j: (j, 0))
    full3 = lambda w: pl.BlockSpec((nq, w, TQ), lambda j: (0, 0, 0))
    in_specs = [blk(wq), blk(GROUP), full3(wq), full3(GROUP), full3(8), full3(8)]
    args = [kb, vb, qt, dot, lse, dl]
    out_specs = [full3(wq), blk(wq), blk(GROUP)]
    out_shape = [SDS((nq, wq, TQ), F32), SDS((s, wq), kv_dtype), SDS((s, GROUP), kv_dtype)]
    if bias:
        in_specs.append(blk(128))
        args.append(ck2)
        out_specs += [blk(128), full3(8)]
        out_shape += [SDS((s, 128), F32), SDS((nq, 8, TQ), F32)]
    return _call_with_comm(body, (nkb,), in_specs, out_specs, out_shape, [], args, comm, ("arbitrary",), name)


def _untranspose(xt, dtype, name):
    nq, w, _ = xt.shape

    def body(x_ref, o_ref):
        o_ref[...] = x_ref[0].T.astype(o_ref.dtype)

    return pl.pallas_call(
        body, grid=(nq,), in_specs=[pl.BlockSpec((1, w, TQ), lambda i: (i, 0, 0))],
        out_specs=pl.BlockSpec((TQ, w), lambda i: (i, 0)), out_shape=SDS((nq * TQ, w), dtype),
        compiler_params=_cp("parallel"), name=name)(xt)


_SCALE_D = (64 + 32) ** -0.5
_COL_CQ, _COL_CKV, _COL_MISC = 2304 // 256, 2560 // 128, 2688 // 128


def _mla_prep(z, gq, gkv, wq, wk, wv, tb, name):
    s = z.shape[0]
    tm = _key_block(s)
    row = lambda w, c: pl.BlockSpec((tm, w), lambda i, c=c: (i, c))
    const = lambda a: pl.BlockSpec(a.shape, lambda i: (0,) * a.ndim)

    def body(cq_ref, ckv_ref, m_ref, gq_ref, gkv_ref, wq_ref, wk_ref, wv_ref, e_ref, qc_ref, qs_ref, kc_ref, ks_ref,
             q_ref, k_ref, v_ref, vt_ref, cqn_ref, ckvn_ref, qt_ref):
        cqn = _rms(cq_ref[...], gq_ref[...]).astype(_MXU)
        ckvn = _rms(ckv_ref[...], gkv_ref[...]).astype(_MXU)
        cqn_ref[...] = cqn
        ckvn_ref[...] = ckvn
        qb = _rope(_dot(cqn, wq_ref[...]), _lanes(qc_ref[...], 512), _lanes(qs_ref[...], 512), 16).astype(q_ref.dtype)
        q_ref[...] = qb
        for b, t in enumerate(_scaled_qt(qb, _SCALE_D)):
            qt_ref[b] = t
        kr = _rope(m_ref[...], kc_ref[...], ks_ref[...], 16)
        k_ref[...] = (_dot(ckvn, wk_ref[...]) + _dot(kr, e_ref[...])).astype(k_ref.dtype)
        v = _dot(ckvn, wv_ref[...])
        v_ref[...] = v.astype(v_ref.dtype)
        vt_ref[0] = v.T.astype(vt_ref.dtype)

    e = tb["place"]
    return pl.pallas_call(
        body, grid=(s // tm,),
        in_specs=[row(256, _COL_CQ), row(128, _COL_CKV), row(128, _COL_MISC), const(gq), const(gkv), const(wq), const(wk),
                  const(wv), const(e), row(128, 0), row(128, 0), row(128, 0), row(128, 0)],
        out_specs=[row(512, 0), row(512, 0), row(256, 0), pl.BlockSpec((1, GROUP, tm), lambda i: (i, 0, 0)), row(256, 0),
                   row(128, 0), pl.BlockSpec((tm // TQ, 512, TQ), lambda i: (i, 0, 0))],
        out_shape=[SDS((s, 512), _MXU), SDS((s, 512), _MXU), SDS((s, 256), _MXU), SDS((s // tm, GROUP, tm), _MXU),
                   SDS((s, 256), _MXU), SDS((s, 128), _MXU), SDS((s // TQ, 512, TQ), _MXU)],
        compiler_params=_cp("parallel"), name=name)(
            z, z, z, gq, gkv, wq, wk, wv, e, tb["q_cos"], tb["q_sin"], tb["k_cos"], tb["k_sin"])


def _mla_prep_bwd(dqt, dk, dv, z, cqn, ckvn, gq, gkv, wq, wk, wv, tb, name):
    s = z.shape[0]
    tm = min(512, s)
    row = lambda w, c: pl.BlockSpec((tm, w), lambda i, c=c: (i, c))
    const = lambda a: pl.BlockSpec(a.shape, lambda i: (0,) * a.ndim)
    acc = lambda shape: pl.BlockSpec(shape, lambda i: (0, 0))

    def body(dq_ref, dk_ref, dv_ref, cq_ref, ckv_ref, cqn_ref, ckvn_ref, gq_ref, gkv_ref, wq_ref, wk_ref, wv_ref, e_ref,
             qc_ref, qs_ref, kc_ref, ks_ref, dcq_ref, dckv_ref, dkr_ref, dwq_ref, dwk_ref, dwv_ref, dgq_ref, dgkv_ref):
        @pl.when(pl.program_id(0) == 0)
        def _():
            for r in (dwq_ref, dwk_ref, dwv_ref, dgq_ref, dgkv_ref):
                r[...] = jnp.zeros_like(r)

        dq = jnp.concatenate([dq_ref[b].T for b in range(tm // TQ)], axis=0)
        dqp = _rope_bwd(dq, _lanes(qc_ref[...], 512), _lanes(qs_ref[...], 512), 16)
        dkd = dk_ref[...]
        dvd = dv_ref[...]
        dwq_ref[...] += _dot_tn(cqn_ref[...], dqp)
        dwk_ref[...] += _dot_tn(ckvn_ref[...], dkd)
        dwv_ref[...] += _dot_tn(ckvn_ref[...], dvd)
        dcq, dgq = _rms_bwd(cq_ref[...], gq_ref[...], _dot_nt(dqp, wq_ref[...]))
        dckv, dgkv = _rms_bwd(ckv_ref[...], gkv_ref[...], _dot_nt(dkd, wk_ref[...]) + _dot_nt(dvd, wv_ref[...]))
        dcq_ref[...] = dcq.astype(dcq_ref.dtype)
        dckv_ref[...] = dckv.astype(dckv_ref.dtype)
        dgq_ref[...] += dgq
        dgkv_ref[...] += dgkv
        dkr = _dot_exact(dkd, e_ref[...], (((1,), (1,)), ((), ())))
        dkr_ref[...] = _rope_bwd(dkr, kc_ref[...], ks_ref[...], 16)

    e = tb["place"]
    return pl.pallas_call(
        body, grid=(s // tm,),
        in_specs=[pl.BlockSpec((tm // TQ, 512, TQ), lambda i: (i, 0, 0)), row(512, 0), row(256, 0), row(256, _COL_CQ),
                  row(128, _COL_CKV), row(256, 0), row(128, 0),
                  const(gq), const(gkv), const(wq), const(wk), const(wv), const(e), row(128, 0), row(128, 0), row(128, 0), row(128, 0)],
        out_specs=[row(256, 0), row(128, 0), row(128, 0), acc((256, 512)), acc((128, 512)), acc((128, 256)), acc((1, 256)),
                   acc((1, 128))],
        out_shape=[SDS((s, 256), _MXU), SDS((s, 128), _MXU), SDS((s, 128), F32), SDS((256, 512), F32), SDS((128, 512), F32),
                   SDS((128, 256), F32), SDS((1, 256), F32), SDS((1, 128), F32)],
        compiler_params=_cp("arbitrary"), name=name)(
            dqt, dk, dv, z, z, cqn, ckvn, gq, gkv, wq, wk, wv, e, tb["q_cos"], tb["q_sin"], tb["k_cos"], tb["k_sin"])


def _out_proj(ys, g, w, x, name):
    s, d = x.shape
    tm = min(512, s)

    def body(ya, yb, yc, yd, g_ref, w_ref, x_ref, o_ref, yn_ref):
        acc = x_ref[...]
        for i, y_ref in enumerate((ya, yb, yc, yd)):
            sl = slice(GROUP * i, GROUP * (i + 1))
            yn = _rms(y_ref[...], g_ref[:, sl]).astype(_MXU)
            yn_ref[:, sl] = yn
            acc = acc + jnp.dot(yn, w_ref[sl, :], preferred_element_type=F32)
        o_ref[...] = acc

    yspec = pl.BlockSpec((tm, GROUP), lambda i: (i, 0))
    return pl.pallas_call(
        body, grid=(s // tm,),
        in_specs=[yspec, yspec, yspec, yspec, pl.BlockSpec((1, d), lambda i: (0, 0)), pl.BlockSpec((d, d), lambda i: (0, 0)),
                  pl.BlockSpec((tm, d), lambda i: (i, 0))],
        out_specs=[pl.BlockSpec((tm, d), lambda i: (i, 0)), pl.BlockSpec((tm, d), lambda i: (i, 0))],
        out_shape=[SDS((s, d), F32), SDS((s, d), _MXU)], compiler_params=_cp("parallel"), name=name)(*ys, g, w, x)


def _out_proj_bwd(dx, w, ys, g, name):
    s, d = dx.shape
    tm = min(512, s)
    nb = tm // TQ

    def body(dx_ref, w_ref, ya, yb, yc, yd, g_ref, da, db, dg_ref, dtc_ref, dtd_ref, dlc_ref, dld_ref):
        @pl.when(pl.program_id(0) == 0)
        def _():
            dg_ref[...] = jnp.zeros_like(dg_ref)

        dyn = _dot_nt(dx_ref[...], w_ref[...])
        for i, y_ref in enumerate((ya, yb, yc, yd)):
            sl = slice(GROUP * i, GROUP * (i + 1))
            y = y_ref[...]
            dy, dg = _rms_bwd(y, g_ref[:, sl], dyn[:, sl])
            dg_ref[:, sl] += dg
            if i < 2:
                (da, db)[i][...] = dy
                continue
            dt_ref, dl_ref = ((dtc_ref, dlc_ref), (dtd_ref, dld_ref))[i - 2]
            dl_ref[...] = jnp.zeros_like(dl_ref)
            for b in range(nb):
                r = slice(TQ * b, TQ * (b + 1))
                dt_ref[b] = dy[r].T.astype(dt_ref.dtype)
                pt = (dy[r] * y[r]).T
                for h in range(N_HEADS):
                    dl_ref[b, h:h + 1, :] = jnp.sum(pt[HEAD_DIM * h:HEAD_DIM * (h + 1), :], axis=0, keepdims=True)

    yspec = pl.BlockSpec((tm, GROUP), lambda i: (i, 0))
    tspec = pl.BlockSpec((nb, GROUP, TQ), lambda i: (i, 0, 0))
    lspec = pl.BlockSpec((nb, 8, TQ), lambda i: (i, 0, 0))
    return pl.pallas_call(
        body, grid=(s // tm,),
        in_specs=[pl.BlockSpec((tm, d), lambda i: (i, 0)), pl.BlockSpec((d, d), lambda i: (0, 0)), yspec, yspec, yspec, yspec,
                  pl.BlockSpec((1, d), lambda i: (0, 0))],
        out_specs=[yspec, yspec, pl.BlockSpec((1, d), lambda i: (0, 0)), tspec, tspec, lspec, lspec],
        out_shape=[SDS((s, GROUP), F32)] * 2 + [SDS((1, d), F32)] + [SDS((s // TQ, GROUP, TQ), _MXU)] * 2
        + [SDS((s // TQ, 8, TQ), F32)] * 2,
        compiler_params=_cp("arbitrary"), name=name)(dx, w, *ys, g)


FF_BLOCK = 512
FF_ROWS = 1024


def _ffn_fwd(x, g, wu, wd, name, comm=None):
    s, d = x.shape
    nj = wu.shape[0]
    tm = min(FF_ROWS, s)
    ni = s // tm

    def body(*refs):
        (x_ref, g_ref, wu_ref, wd_ref), (o_ref, u_ref, h_ref), (acc,), cc = _split_refs(refs, 4, 3, comm)
        i, j = pl.program_id(0), pl.program_id(1)
        _host_gather(comm, cc, i * nj + j, ni * nj)

        @pl.when(j == 0)
        def _():
            h_ref[...] = _rms(x_ref[...], g_ref[...]).astype(h_ref.dtype)
            acc[...] = jnp.zeros_like(acc)

        halves = [slice(r, r + tm // 2) for r in range(0, tm, tm // 2)]
        us = [jnp.dot(h_ref[r, :], wu_ref[0], preferred_element_type=F32) for r in halves]
        for r, u in zip(halves, us):
            u_ref[r, :] = u.astype(u_ref.dtype)
            acc[r, :] += _dot(jnp.square(jnp.maximum(u, 0.0)), wd_ref[...])

        @pl.when(j == nj - 1)
        def _():
            o_ref[...] = x_ref[...] + acc[...]

        if comm is not None:
            @pl.when((i == ni - 1) & (j == nj - 1))
            def _():
                comm.wait(*cc)

    in_specs = [pl.BlockSpec((tm, d), lambda i, j: (i, 0)), pl.BlockSpec((1, d), lambda i, j: (0, 0)),
                pl.BlockSpec((1, d, FF_BLOCK), lambda i, j: (j, 0, 0)), pl.BlockSpec((FF_BLOCK, d), lambda i, j: (j, 0))]
    out_specs = [pl.BlockSpec((tm, d), lambda i, j: (i, 0)), pl.BlockSpec((tm, FF_BLOCK), lambda i, j: (i, j)),
                 pl.BlockSpec((tm, d), lambda i, j: (i, 0))]
    out_shape = [SDS((s, d), F32), SDS((s, nj * FF_BLOCK), _MXU), SDS((s, d), _MXU)]
    return _call_with_comm(body, (ni, nj), in_specs, out_specs, out_shape, [pltpu.VMEM((tm, d), F32)], [x, g, wu, wd], comm,
                           ("arbitrary", "arbitrary"), name)


def _ffn_bwd(dx2, x, u, g, wu, wd, name, comm=None):
    s, d = x.shape
    nj = wu.shape[0]
    tm = min(FF_ROWS, s)
    ni = s // tm

    def body(*refs):
        (dx_ref, x_ref, u_ref, g_ref, wu_ref, wd_ref), (o_ref, du_ref, dg_ref), (acc, dxb), cc = _split_refs(refs, 6, 3, comm)
        i, j = pl.program_id(0), pl.program_id(1)

        @pl.when((i == 0) & (j == 0))
        def _():
            if comm is not None:
                comm.start(*cc)
            dg_ref[...] = jnp.zeros_like(dg_ref)

        @pl.when(j == 0)
        def _():
            dxb[...] = dx_ref[...].astype(dxb.dtype)
            acc[...] = jnp.zeros_like(acc)

        nt = (((1,), (1,)), ((), ()))
        halves = [slice(r, r + tm // 2) for r in range(0, tm, tm // 2)]
        das = [lax.dot_general(dxb[r, :], wd_ref[...], nt, preferred_element_type=F32) for r in halves]
        for r, da in zip(halves, das):
            du = (da * 2.0 * jnp.maximum(u_ref[r, :].astype(F32), 0.0)).astype(du_ref.dtype)
            du_ref[r, :] = du
            acc[r, :] += lax.dot_general(du, wu_ref[0], nt, preferred_element_type=F32)

        @pl.when(j == nj - 1)
        def _():
            dxn, dg = _rms_bwd(x_ref[...], g_ref[...], acc[...])
            o_ref[...] = dx_ref[...] + dxn
            dg_ref[...] += dg

        if comm is not None:
            @pl.when((i == ni - 1) & (j == nj - 1))
            def _():
                comm.wait(*cc)

    in_specs = [pl.BlockSpec((tm, d), lambda i, j: (i, 0)), pl.BlockSpec((tm, d), lambda i, j: (i, 0)),
                pl.BlockSpec((tm, FF_BLOCK), lambda i, j: (i, j)), pl.BlockSpec((1, d), lambda i, j: (0, 0)),
                pl.BlockSpec((1, d, FF_BLOCK), lambda i, j: (j, 0, 0)), pl.BlockSpec((FF_BLOCK, d), lambda i, j: (j, 0))]
    out_specs = [pl.BlockSpec((tm, d), lambda i, j: (i, 0)), pl.BlockSpec((tm, FF_BLOCK), lambda i, j: (i, j)),
                 pl.BlockSpec((1, d), lambda i, j: (0, 0))]
    out_shape = [SDS((s, d), F32), SDS((s, nj * FF_BLOCK), _MXU), SDS((1, d), F32)]
    return _call_with_comm(body, (ni, nj), in_specs, out_specs, out_shape,
                           [pltpu.VMEM((tm, d), F32), pltpu.VMEM((tm, d), _MXU)], [dx2, x, u, g, wu, wd], comm,
                           ("arbitrary", "arbitrary"), name)


def _in_proj_bwd(dz, w, x, g, dx_up, name, comm=None):
    s, d = x.shape
    n = w.shape[1]
    tm = min(512, s)
    ni = s // tm

    def body(*refs):
        (dz_ref, w_ref, x_ref, g_ref, up_ref), (o_ref, dg_ref), _, cc = _split_refs(refs, 5, 2, comm)
        i = pl.program_id(0)

        @pl.when(i == 0)
        def _():
            if comm is not None:
                comm.start(*cc)
            dg_ref[...] = jnp.zeros_like(dg_ref)

        dh = lax.dot_general(dz_ref[...], w_ref[...], (((1,), (1,)), ((), ())), preferred_element_type=F32)
        dxn, dg = _rms_bwd(x_ref[...], g_ref[...], dh)
        o_ref[...] = up_ref[...] + dxn
        dg_ref[...] += dg
        if comm is not None:
            @pl.when(i == ni - 1)
            def _():
                comm.wait(*cc)

    in_specs = [pl.BlockSpec((tm, n), lambda i: (i, 0)), pl.BlockSpec((d, n), lambda i: (0, 0)),
                pl.BlockSpec((tm, d), lambda i: (i, 0)), pl.BlockSpec((1, d), lambda i: (0, 0)),
                pl.BlockSpec((tm, d), lambda i: (i, 0))]
    out_specs = [pl.BlockSpec((tm, d), lambda i: (i, 0)), pl.BlockSpec((1, d), lambda i: (0, 0))]
    out_shape = [SDS((s, d), F32), SDS((1, d), F32)]
    return _call_with_comm(body, (ni,), in_specs, out_specs, out_shape, [], [dz, w, x, g, dx_up], comm, ("arbitrary",), name)


def _loss_head(x, g, target, name):
    s, d = x.shape
    tm = min(512, s)

    def body(x_ref, g_ref, t_ref, l_ref, dx_ref, dg_ref):
        @pl.when(pl.program_id(0) == 0)
        def _():
            l_ref[...] = jnp.zeros_like(l_ref)
            dg_ref[...] = jnp.zeros_like(dg_ref)

        xv = x_ref[...]
        err = _rms(xv, g_ref[...]) - t_ref[...]
        l_ref[...] += jnp.sum(err * err, axis=0, keepdims=True) * (0.5 / d)
        dx, dg = _rms_bwd(xv, g_ref[...], err * (1.0 / d))
        dx_ref[...] = dx
        dg_ref[...] += dg

    return pl.pallas_call(
        body, grid=(s // tm,),
        in_specs=[pl.BlockSpec((tm, d), lambda i: (i, 0)), pl.BlockSpec((1, d), lambda i: (0, 0)),
                  pl.BlockSpec((tm, d), lambda i: (i, 0))],
        out_specs=[pl.BlockSpec((1, d), lambda i: (0, 0)), pl.BlockSpec((tm, d), lambda i: (i, 0)),
                   pl.BlockSpec((1, d), lambda i: (0, 0))],
        out_shape=[SDS((1, d), F32), SDS((s, d), F32), SDS((1, d), F32)], compiler_params=_cp("arbitrary"), name=name)(x, g, target)


def _me_and_peer():
    x, y, c = lax.axis_index("x"), lax.axis_index("y"), lax.axis_index("c")
    me = 4 * x + 2 * y + c

    def peer(k):
        px, py, pc = x ^ (k >> 2), y ^ ((k >> 1) & 1), c ^ (k & 1)
        return (px, py, pc), 4 * px + 2 * py + pc

    return me, peer


class _Comm:
    CHIPS = (2, 4, 6)

    def __init__(self, kind, arrs):
        assert kind in ("gather", "exchange")
        self.kind, self.arrs, self.n = kind, list(arrs), len(arrs)
        anyspec = pl.BlockSpec(memory_space=pl.ANY)
        self.in_specs = [anyspec] * self.n
        self.out_specs = [anyspec] * self.n
        self.out_shape = [SDS(((NDEV,) + a.shape) if kind == "gather" else a.shape, a.dtype) for a in self.arrs]
        npair = NDEV - 1 + len(self.CHIPS)
        self.scratch = [pltpu.SemaphoreType.DMA((self.n, npair)), pltpu.SemaphoreType.DMA((self.n, npair)),
                        pltpu.SemaphoreType.DMA((self.n,))]

    def _copies(self, ins, outs, sems):
        send, recv, loc = sems
        me, peer = _me_and_peer()
        gather = self.kind == "gather"
        sibling = peer(1)[0]
        local = [pltpu.make_async_copy(ins[a] if gather else ins[a].at[me], outs[a].at[me], loc.at[a]) for a in range(self.n)]
        outgoing, incoming, forwards, forwarded = [], [], [], []
        for k in ((1,) + self.CHIPS) if gather else range(1, NDEV):
            dev, pid = peer(k)
            for a in range(self.n):
                pair = dict(send_sem=send.at[a, k - 1], recv_sem=recv.at[a, k - 1], device_id=dev, device_id_type=MESH)
                outgoing.append(pltpu.make_async_remote_copy(src_ref=ins[a] if gather else ins[a].at[pid],
                                                             dst_ref=outs[a].at[me], **pair))
                incoming.append(pltpu.make_async_remote_copy(src_ref=ins[a] if gather else ins[a].at[me],
                                                             dst_ref=outs[a].at[pid], **pair))
        if gather:
            for idx, k in enumerate(self.CHIPS):
                got, theirs = peer(k)[1], peer(k + 1)[1]
                for a in range(self.n):
                    pair = dict(send_sem=send.at[a, NDEV - 1 + idx], recv_sem=recv.at[a, NDEV - 1 + idx], device_id=sibling,
                                device_id_type=MESH)
                    forwards.append(pltpu.make_async_remote_copy(src_ref=outs[a].at[got], dst_ref=outs[a].at[got], **pair))
                    forwarded.append(pltpu.make_async_remote_copy(src_ref=outs[a].at[theirs], dst_ref=outs[a].at[theirs], **pair))
        return local, outgoing, incoming, forwards, forwarded

    def start(self, ins, outs, sems):
        local, outgoing, _, _, _ = self._copies(ins, outs, sems)
        for cp in local + outgoing:
            cp.start()

    def forward(self, ins, outs, sems):
        _, _, incoming, forwards, _ = self._copies(ins, outs, sems)
        per = self.n
        for idx in range(len(forwards) // per if per else 0):
            for a in range(per):
                incoming[(1 + idx) * per + a].wait_recv()
                forwards[idx * per + a].start()

    def wait(self, ins, outs, sems):
        local, outgoing, incoming, forwards, forwarded = self._copies(ins, outs, sems)
        for cp in (incoming[:self.n] if self.kind == "gather" else incoming) + forwarded:
            cp.wait_recv()
        for cp in outgoing + forwards:
            cp.wait_send()
        for cp in local:
            cp.wait()


def _host_gather(comm, cc, step, nsteps, late=False):
    if comm is None:
        return

    @pl.when(step == 0)
    def _():
        comm.start(*cc)

    @pl.when(step == (nsteps - 1 if late else (2 * nsteps) // 3))
    def _():
        comm.forward(*cc)


def _split_refs(refs, n_in, n_out, comm):
    c = comm.n if comm is not None else 0
    ins, cin = refs[:n_in], refs[n_in:n_in + c]
    outs, cout = refs[n_in + c:n_in + c + n_out], refs[n_in + c + n_out:n_in + 2 * c + n_out]
    rest = refs[n_in + 2 * c + n_out:]
    scratch, csem = (rest[:len(rest) - 3], rest[len(rest) - 3:]) if c else (rest, ())
    return ins, outs, scratch, (cin, cout, csem)


def _comm_call(kind, arrs, name):
    comm = _Comm(kind, arrs)

    def body(*refs):
        _, _, _, c = _split_refs(refs, 0, 0, comm)
        comm.start(*c)
        if kind == "gather":
            comm.forward(*c)
        comm.wait(*c)

    return pl.pallas_call(body, in_specs=comm.in_specs, out_specs=comm.out_specs, out_shape=comm.out_shape,
                          scratch_shapes=comm.scratch, compiler_params=pltpu.CompilerParams(has_side_effects=True),
                          name=name)(*arrs)


def _all_gather(arrs, name):
    return _comm_call("gather", arrs, name)


def _exchange(arrs, name):
    return _comm_call("exchange", arrs, name)


def _sum_slots(parts, name):
    _, r, c = parts.shape
    tr = r if r <= 512 else 512

    def body(p_ref, o_ref):
        acc = p_ref[0].astype(F32)
        for q in range(1, NDEV):
            acc = acc + p_ref[q].astype(F32)
        o_ref[...] = acc

    return pl.pallas_call(
        body, grid=(r // tr,), in_specs=[pl.BlockSpec((NDEV, tr, c), lambda i: (0, i, 0))],
        out_specs=pl.BlockSpec((tr, c), lambda i: (i, 0)), out_shape=SDS((r, c), F32),
        compiler_params=_cp("parallel"), name=name)(parts)


def _adamw(g, w, m, v, name):
    r, c = w.shape
    parts = g.ndim == 3
    tr = r
    for cand in (512, 256, 128, 64, 32, 16, 8):
        if r > cand and r % cand == 0 and cand * c * 4 <= 2 * 1024 * 1024:
            tr = cand
            break
    bc1 = 1.0 / (1.0 - ADAM_B1 ** ADAM_STEP)
    bc2 = 1.0 / (1.0 - ADAM_B2 ** ADAM_STEP)

    def body(g_ref, w_ref, m_ref, v_ref, go_ref, d_ref, mo_ref, vo_ref):
        if parts:
            gv = g_ref[0].astype(F32)
            for q in range(1, NDEV):
                gv = gv + g_ref[q].astype(F32)
        else:
            gv = g_ref[...]
        mn = ADAM_B1 * m_ref[...] + (1.0 - ADAM_B1) * gv
        vn = ADAM_B2 * v_ref[...] + (1.0 - ADAM_B2) * (gv * gv)
        go_ref[...] = gv
        mo_ref[...] = mn
        vo_ref[...] = vn
        d_ref[...] = -ADAM_LR * ((mn * bc1) / (jnp.sqrt(vn * bc2) + ADAM_EPS) + ADAM_WD * w_ref[...])

    spec = pl.BlockSpec((tr, c), lambda i: (i, 0))
    gspec = pl.BlockSpec((NDEV, tr, c), lambda i: (0, i, 0)) if parts else spec
    return pl.pallas_call(
        body, grid=(r // tr,), in_specs=[gspec, spec, spec, spec], out_specs=[spec] * 4,
        out_shape=[SDS((r, c), F32)] * 4, compiler_params=_cp("parallel"), name=name)(g, w, m, v)


def _adamw_layer(parts, w, m, v, l, prev, name):
    depth, r, c = w.shape
    tr = next(t for t in (512, 256, 128, 64, 32, 16, 8) if r % t == 0 and t * c * 4 <= 2 * 1024 * 1024)
    bc1 = 1.0 / (1.0 - ADAM_B1 ** ADAM_STEP)
    bc2 = 1.0 / (1.0 - ADAM_B2 ** ADAM_STEP)

    def body(g_ref, w_ref, m_ref, v_ref, *rest):
        go_ref, d_ref, mo_ref, vo_ref = rest[-4:]
        gv = g_ref[0].astype(F32)
        for q in range(1, NDEV):
            gv = gv + g_ref[q].astype(F32)
        mn = ADAM_B1 * m_ref[0] + (1.0 - ADAM_B1) * gv
        vn = ADAM_B2 * v_ref[0] + (1.0 - ADAM_B2) * (gv * gv)
        go_ref[0] = gv
        mo_ref[0] = mn
        vo_ref[0] = vn
        d_ref[0] = -ADAM_LR * ((mn * bc1) / (jnp.sqrt(vn * bc2) + ADAM_EPS) + ADAM_WD * w_ref[0])

    spec = pl.BlockSpec((1, tr, c), lambda i: (l, i, 0))
    in_specs = [pl.BlockSpec((NDEV, tr, c), lambda i: (0, i, 0)), spec, spec, spec]
    args = [parts, w, m, v]
    aliases = {}
    if prev is not None:
        in_specs += [pl.BlockSpec(memory_space=pl.ANY)] * 4
        args += list(prev)
        aliases = {4 + k: k for k in range(4)}
    return pl.pallas_call(
        body, grid=(r // tr,), in_specs=in_specs, out_specs=[spec] * 4, out_shape=[SDS((depth, r, c), F32)] * 4,
        input_output_aliases=aliases, compiler_params=_cp("parallel"), name=name)(*args)


def _pad_in_cols(w):
    r = w.shape[0]
    zeros = lambda n: jnp.zeros((r, n), w.dtype)
    return jnp.concatenate([w[:, :2304], w[:, 2308:2692], w[:, 2304:2308], zeros(28), w[:, 2692:2724], zeros(64)], axis=1)


def _unpad_in_cols(w):
    return jnp.concatenate([w[..., :2304], w[..., 2688:2692], w[..., 2304:2688], w[..., 2720:2752]], axis=-1)


def _pad_uq(w):
    return jnp.pad(w.reshape(256, N_HEADS, 96), ((0, 0), (0, 0), (0, 32))).reshape(256, 512)


def _unpad_uq(w):
    return w.reshape(256, N_HEADS, 128)[:, :, :96].reshape(256, 384)


def _split_ukv(w):
    r = w.reshape(128, N_HEADS, 128)
    return jnp.pad(r[:, :, :64], ((0, 0), (0, 0), (0, 64))).reshape(128, 512), r[:, :, 64:].reshape(128, 256)


def _join_ukv(dk, dv):
    return jnp.concatenate([dk.reshape(128, N_HEADS, 128)[:, :, :64], dv.reshape(128, N_HEADS, 64)], axis=-1).reshape(128, 512)


def _cols_to_full(g):
    return jnp.transpose(g, (1, 0, 2)).reshape(g.shape[1], NDEV * g.shape[2])


def kernel(x, g_mix_norm, w_in, b_forget, g_sgu, w_spatial, b_spatial, g_mla_q, w_uq, g_mla_kv, w_ukv, g_group_out, w_out, g_ffn_norm, w_up, w_down, g_final, loss_target, m_g_mix_norm, m_w_in, m_b_forget, m_g_sgu, m_w_spatial, m_b_spatial, m_g_mla_q, m_w_uq, m_g_mla_kv, m_w_ukv, m_g_group_out, m_w_out, m_g_ffn_norm, m_w_up, m_w_down, m_g_final, v_g_mix_norm, v_w_in, v_b_forget, v_g_sgu, v_w_spatial, v_b_spatial, v_g_mla_q, v_w_uq, v_g_mla_kv, v_w_ukv, v_g_group_out, v_w_out, v_g_ffn_norm, v_w_up, v_w_down, v_g_final):
    depth = w_in.shape[0]
    s, d = x.shape[1], x.shape[2]
    x0 = x.reshape(s, d)
    target = loss_target.reshape(s, d)
    tb = _tables(s)
    me = 4 * lax.axis_index("x") + 2 * lax.axis_index("y") + lax.axis_index("c")

    assert depth == 2
    shards = {}
    for l in range(depth):
        shards.update({(l, "w_in"): _pad_in_cols(w_in[l]).astype(_WIRE), (l, "w_out"): w_out[l].astype(_WIRE),
                       (l, "w_up"): w_up[l].astype(_WIRE), (l, "w_down"): w_down[l].astype(_WIRE),
                       (l, "w_uq"): w_uq[l].astype(_WIRE), (l, "w_ukv"): w_ukv[l].astype(_WIRE)})
    wts = _ShardedWeights(shards)
    first = [(0, "w_in"), (0, "w_uq"), (0, "w_ukv"), (1, "w_uq"), (1, "w_ukv")]
    wts.full.update(zip(first, _all_gather([shards[k] for k in first], "gather_first")))

    row = lambda a: a.reshape(1, -1)

    def small(l):
        bf = jnp.pad(b_forget[l].reshape(1, N_HEADS), ((0, 0), (0, 128 - N_HEADS)))
        bt = jnp.pad(b_spatial[l].T, ((0, 0), (0, 128 - N_HEADS)))
        return dict(g_mix=row(g_mix_norm[l]), g_sgu=row(g_sgu[l]), w_s=w_spatial[l], b_t=bt, b_f=bf, gq=row(g_mla_q[l]),
                    gkv=row(g_mla_kv[l]), g_go=row(g_group_out[l]), g_ffn=row(g_ffn_norm[l]))

    smalls = [small(l) for l in range(depth)]
    lrow, dx, sm, dg_final = _local_step(x0, target, wts, smalls, row(g_final), tb)
    loss = lax.psum(jnp.sum(lrow), AXES)
    grad_x = dx.reshape(1, s, d)
    return _reduce_and_update(loss, grad_x, wts.recv, sm, dg_final, me, dict(
        g_mix_norm=(g_mix_norm, m_g_mix_norm, v_g_mix_norm), w_in=(w_in, m_w_in, v_w_in),
        b_forget=(b_forget, m_b_forget, v_b_forget), g_sgu=(g_sgu, m_g_sgu, v_g_sgu),
        w_spatial=(w_spatial, m_w_spatial, v_w_spatial), b_spatial=(b_spatial, m_b_spatial, v_b_spatial),
        g_mla_q=(g_mla_q, m_g_mla_q, v_g_mla_q), w_uq=(w_uq, m_w_uq, v_w_uq), g_mla_kv=(g_mla_kv, m_g_mla_kv, v_g_mla_kv),
        w_ukv=(w_ukv, m_w_ukv, v_w_ukv), g_group_out=(g_group_out, m_g_group_out, v_g_group_out),
        w_out=(w_out, m_w_out, v_w_out), g_ffn_norm=(g_ffn_norm, m_g_ffn_norm, v_g_ffn_norm), w_up=(w_up, m_w_up, v_w_up),
        w_down=(w_down, m_w_down, v_w_down), g_final=(g_final, m_g_final, v_g_final)))


_GATHER_AT = {
    "in_proj0": [(0, "w_up")],
    "fox_attn0": [(0, "w_down")],
    "mla_attn0": [(0, "w_out"), (1, "w_in")],
    "ffn_fwd0": [(1, "w_down")],
    "fox_attn1": [(1, "w_out")],
    "mla_attn1": [(1, "w_up")],
}
_SCATTER_AT = {
    "fox_attn_bwd1": [(1, "w_down")],
    "mla_attn_bwd1": [(1, "w_up"), (1, "w_out")],
    "ffn_bwd0": [(1, "w_in")],
    "fox_attn_bwd0": [(0, "w_down")],
    "mla_attn_bwd0": [(0, "w_up"), (0, "w_out")],
    "in_proj_bwd0": [(0, "w_in")],
}


class _FullWeights:
    def __init__(self, per_layer):
        self.per_layer, self.grads = per_layer, {}

    def get(self, l, name):
        return self.per_layer[l][name]

    def comm(self, host):
        return None

    def done(self, host, results):
        pass

    def grad(self, l, name, blocks):
        self.grads[(l, name)] = blocks


class _ShardedWeights(_FullWeights):
    def __init__(self, shards):
        self.shards, self.full, self.grads, self.recv = shards, {}, {}, {}

    def get(self, l, name):
        if name in ("wk", "wv"):
            return _split_ukv(_cols_to_full(self.full[(l, "w_ukv")]))[0 if name == "wk" else 1]
        if name == "wq":
            return _pad_uq(_cols_to_full(self.full[(l, "w_uq")]))
        g = self.full[(l, name)]
        return g if name == "w_up" else g.reshape(NDEV * g.shape[1], g.shape[2])

    def comm(self, host):
        if host in _GATHER_AT:
            return _Comm("gather", [self.shards[k] for k in _GATHER_AT[host]])
        if host in _SCATTER_AT:
            return _Comm("exchange", [self.grads[k] for k in _SCATTER_AT[host]])
        return None

    def done(self, host, results):
        if host in _GATHER_AT:
            self.full.update(zip(_GATHER_AT[host], results))
        if host in _SCATTER_AT:
            self.recv.update(zip(_SCATTER_AT[host], results))


def _local_step(x0, target, wts, smalls, g_final, tb):
    depth = len(smalls)
    s, d = x0.shape
    saved = []
    xl = x0
    for l in range(depth):
        p = smalls[l]
        z, h, got = _norm_matmul(xl, p["g_mix"], wts.get(l, "w_in"), f"in_proj{l}", wts.comm(f"in_proj{l}"))
        wts.done(f"in_proj{l}", got)
        ya = _sgu_fwd(z, p["g_sgu"], p["w_s"], p["b_t"], tb, f"sgu_fwd{l}")
        yb, ret, states = _ret_fwd(z, tb, f"ret_fwd{l}")
        cum = _fox_prep(z, p["b_f"], f"fox_prep{l}")
        kc, vc, vtc, qtc = _kv_prep(z, 6, 7, 8, HEAD_DIM ** -0.5, f"fox_kv{l}")
        yc, lse_c, got = _attn_fwd(z, 6, HEAD_DIM, kc, vtc, HEAD_DIM ** -0.5, cum, f"fox_attn{l}", wts.comm(f"fox_attn{l}"))
        wts.done(f"fox_attn{l}", got)
        wq, wk, wv = wts.get(l, "wq"), wts.get(l, "wk"), wts.get(l, "wv")
        qd, kd, vd, vtd, cqn, ckvn, qtd = _mla_prep(z, p["gq"], p["gkv"], wq, wk, wv, tb, f"mla_prep{l}")
        yd, lse_d, got = _attn_fwd(qd, 0, 128, kd, vtd, _SCALE_D, None, f"mla_attn{l}", wts.comm(f"mla_attn{l}"))
        wts.done(f"mla_attn{l}", got)
        ys = (ya, yb, yc, yd)
        x1, yn = _out_proj(ys, p["g_go"], wts.get(l, "w_out"), xl, f"out_proj{l}")
        x2, u, h2, got = _ffn_fwd(x1, p["g_ffn"], wts.get(l, "w_up"), wts.get(l, "w_down"), f"ffn_fwd{l}", wts.comm(f"ffn_fwd{l}"))
        wts.done(f"ffn_fwd{l}", got)
        saved.append(dict(x=xl, z=z, h=h, ys=ys, ret=ret, states=states, cum=cum, lse_c=lse_c, kc=kc, vc=vc, qd=qd, kd=kd, vd=vd,
                          cqn=cqn, ckvn=ckvn, lse_d=lse_d, x1=x1, yn=yn, u=u, h2=h2, wq=wq, wk=wk, wv=wv, qtc=qtc, qtd=qtd))
        xl = x2

    lrow, dx, dg_final = _loss_head(xl, g_final, target, "loss_head")

    sm = [None] * depth
    for l in reversed(range(depth)):
        p, a = smalls[l], saved[l]
        dx1, du, dg_ffn, got = _ffn_bwd(dx, a["x1"], a["u"], p["g_ffn"], wts.get(l, "w_up"), wts.get(l, "w_down"), f"ffn_bwd{l}",
                                        wts.comm(f"ffn_bwd{l}"))
        wts.done(f"ffn_bwd{l}", got)
        dw_down = _mm_tn(a["u"], dx, f"dw_down{l}", a_fn=lambda t: jnp.square(jnp.maximum(t, 0.0)), out_dtype=_WIRE)
        wts.grad(l, "w_down", dw_down.reshape(NDEV, dw_down.shape[0] // NDEV, d))
        wts.grad(l, "w_up", _mm_tn(a["h2"], du, f"dw_up{l}", blocked=True, out_dtype=_WIRE))
        dya, dyb, dg_go, dot_c, dot_d, dl_c, dl_d = _out_proj_bwd(dx1, wts.get(l, "w_out"), a["ys"], p["g_go"],
                                                                  f"out_proj_bwd{l}")
        wts.grad(l, "w_out", _mm_tn(a["yn"], dx1, f"dw_out{l}", out_dtype=_WIRE).reshape(NDEV, d // NDEV, d))
        dz_a, dg_sgu, dw_s, db_t = _sgu_bwd(dya, a["z"], p["g_sgu"], p["w_s"], p["b_t"], tb, f"sgu_bwd{l}")
        dz_b = _ret_bwd(dyb, a["z"], a["ret"], a["states"], tb, f"ret_bwd{l}")
        dqt_c, dk_c, dv_c, dck, dcq, got = _attn_bwd(a["kc"], a["vc"], a["qtc"], dot_c, a["lse_c"], dl_c, HEAD_DIM,
                                                     HEAD_DIM ** -0.5, a["cum"], f"fox_attn_bwd{l}", _MXU,
                                                     wts.comm(f"fox_attn_bwd{l}"))
        wts.done(f"fox_attn_bwd{l}", got)
        dq_c = _untranspose(dqt_c, _MXU, f"fox_dq{l}")
        dqt_d, dk_d, dv_d, got = _attn_bwd(a["kd"], a["vd"], a["qtd"], dot_d, a["lse_d"], dl_d, 128, _SCALE_D, None,
                                           f"mla_attn_bwd{l}", F32, wts.comm(f"mla_attn_bwd{l}"))
        wts.done(f"mla_attn_bwd{l}", got)
        dz_cq, dz_ckv, dkr, dwq, dwk, dwv, dgq, dgkv = _mla_prep_bwd(dqt_d, dk_d, dv_d, a["z"], a["cqn"], a["ckvn"], p["gq"],
                                                                     p["gkv"], a["wq"], a["wk"], a["wv"], tb, f"mla_prep_bwd{l}")
        dz_misc, db_f = _fox_post(dcq, dck, a["z"], p["b_f"], dkr, f"fox_post{l}")
        dz = jnp.concatenate([dz_a, dz_b, dq_c, dk_c, dv_c, dz_cq, dz_ckv, dz_misc], axis=1)
        wts.grad(l, "w_in", _unpad_in_cols(_mm_tn(a["h"], dz, f"dw_in{l}", out_dtype=_WIRE)).reshape(NDEV, d // NDEV, N_IN))
        dx, dg_mix, got = _in_proj_bwd(dz, wts.get(l, "w_in"), a["x"], p["g_mix"], dx1, f"in_proj_bwd{l}",
                                       wts.comm(f"in_proj_bwd{l}"))
        wts.done(f"in_proj_bwd{l}", got)
        sm[l] = [dg_mix, dg_go, dg_ffn, dg_sgu, dw_s, db_t[:, :N_HEADS].T, db_f[0, :N_HEADS], dgq, dgkv, _unpad_uq(dwq),
                 _join_ukv(dwk, dwv)]
    return lrow, dx, sm, dg_final


def _reduce_and_update(loss, grad_x, recv, sm, dg_final, me, given):
    depth = len(sm)
    pieces = [t for l in range(depth) for t in sm[l]] + [dg_final]
    flat = jnp.concatenate([t.reshape(-1) for t in pieces])
    n_flat = flat.shape[0]
    unit = NDEV * 8 * 128
    n_pad = -(-n_flat // unit) * unit
    packed = jnp.pad(flat, (0, n_pad - n_flat)).reshape(NDEV, n_pad // (NDEV * 128), 128)
    red = _sum_slots(_exchange([packed], "scatter_small")[0], "sum_small")
    full = _all_gather([red], "gather_small")[0].reshape(-1)
    offs = np.cumsum([0] + [int(np.prod(t.shape)) for t in pieces])
    red_pieces = [full[int(offs[i]):int(offs[i + 1])].reshape(pieces[i].shape) for i in range(len(pieces))]
    per = len(sm[0])
    stack = lambda i: jnp.stack([red_pieces[l * per + i] for l in range(depth)])
    g_small = dict(g_mix_norm=stack(0), g_group_out=stack(1), g_ffn_norm=stack(2), g_sgu=stack(3), w_spatial=stack(4),
                   b_spatial=stack(5), b_forget=stack(6), g_mla_q=stack(7), g_mla_kv=stack(8), g_final=red_pieces[-1])
    cq, ckv = given["w_uq"][0].shape[2], given["w_ukv"][0].shape[2]
    g_small["w_uq"] = lax.dynamic_slice_in_dim(stack(9), me * cq, cq, axis=2)
    g_small["w_ukv"] = lax.dynamic_slice_in_dim(stack(10), me * ckv, ckv, axis=2)

    names = list(given)
    outs = {}
    for nme in names:
        wv_, mv_, vv_ = given[nme]
        shape = wv_.shape
        if nme in ("w_in", "w_out", "w_up", "w_down"):
            res = None
            for l in range(depth):
                res = _adamw_layer(recv[(l, nme)], wv_, mv_, vv_, l, res, f"adamw_{nme}{l}")
            outs[nme] = list(res)
        else:
            two = lambda t: t.reshape(-1, shape[-1]) if t.ndim > 1 else t.reshape(1, -1)
            res = _adamw(two(g_small[nme]), two(wv_), two(mv_), two(vv_), f"adamw_{nme}")
            outs[nme] = [r.reshape(shape) for r in res]
    return (loss, grad_x, *[outs[n][0] for n in names], *[outs[n][1] for n in names], *[outs[n][2] for n in names],
            *[outs[n][3] for n in names])
```

```python
import functools

import jax
import jax.numpy as jnp
import numpy as np
from jax import lax
from jax.experimental import pallas as pl
from jax.experimental.pallas import tpu as pltpu

F32 = jnp.float32
_MXU = jnp.bfloat16
_WIRE = jnp.bfloat16
EPS = 1e-6
NDEV = 8
AXES = ("x", "y", "c")
MESH = pl.DeviceIdType.MESH

N_HEADS = 4
HEAD_DIM = 64
GROUP = 256
CHUNK = 128
NZ = 2816
N_IN = 2724
MISC_F, MISC_KR = 0, 32
VMEM_LIMIT = 56 * 1024 * 1024

ADAM_LR, ADAM_B1, ADAM_B2, ADAM_EPS, ADAM_WD, ADAM_STEP = 0.001, 0.9, 0.999, 1e-08, 0.01, 10

SDS = jax.ShapeDtypeStruct


def _cp(*sem):
    return pltpu.CompilerParams(dimension_semantics=sem, vmem_limit_bytes=VMEM_LIMIT)


def _dot(a, b):
    return jnp.dot(a.astype(_MXU), b.astype(_MXU), preferred_element_type=F32)


def _dot_nt(a, b):
    return lax.dot_general(a.astype(_MXU), b.astype(_MXU), (((1,), (1,)), ((), ())), preferred_element_type=F32)


def _dot_tn(a, b):
    return lax.dot_general(a.astype(_MXU), b.astype(_MXU), (((0,), (0,)), ((), ())), preferred_element_type=F32)


def _dot_exact(a, b, dims=(((1,), (0,)), ((), ()))):
    return lax.dot_general(a, b, dims, precision=lax.Precision.HIGHEST, preferred_element_type=F32)


def _rms(x, g):
    return x * lax.rsqrt(jnp.mean(x * x, axis=-1, keepdims=True) + EPS) * g


def _rms_bwd(x, g, dy):
    xh = x * lax.rsqrt(jnp.mean(x * x, axis=-1, keepdims=True) + EPS)
    dxh = dy * g
    r = lax.rsqrt(jnp.mean(x * x, axis=-1, keepdims=True) + EPS)
    dx = r * (dxh - xh * jnp.mean(dxh * xh, axis=-1, keepdims=True))
    return dx, jnp.sum(dy * xh, axis=0, keepdims=True)


def _standardize(t):
    mu = jnp.mean(t, axis=-1, keepdims=True)
    tc = t - mu
    rs = lax.rsqrt(jnp.mean(tc * tc, axis=-1, keepdims=True) + EPS)
    return tc * rs, rs


def _standardize_bwd(yh, rs, dy):
    return rs * (dy - jnp.mean(dy, axis=-1, keepdims=True) - yh * jnp.mean(dy * yh, axis=-1, keepdims=True))


_GELU_C = 0.7978845608028654


def _gelu(x):
    return 0.5 * x * (1.0 + jnp.tanh(_GELU_C * (x + 0.044715 * x * x * x)))


def _gelu_grad(x):
    t = jnp.tanh(_GELU_C * (x + 0.044715 * x * x * x))
    return 0.5 * (1.0 + t) + 0.5 * x * (1.0 - t * t) * _GELU_C * (1.0 + 3 * 0.044715 * x * x)


def _sigmoid(x):
    return 1.0 / (1.0 + jnp.exp(-x))


def _swap_half(t, half):
    n = t.shape[-1]
    lane = lax.broadcasted_iota(jnp.int32, t.shape, t.ndim - 1)
    return jnp.where((lane % (2 * half)) < half, pltpu.roll(t, n - half, t.ndim - 1), pltpu.roll(t, half, t.ndim - 1))


def _lanes(table, width):
    return jnp.concatenate([table] * (width // table.shape[-1]), axis=-1)


def _rope(t, cos, sin, half):
    return t * cos + _swap_half(t, half) * sin


def _rope_bwd(d, cos, sin, half):
    return d * cos - _swap_half(d, half) * sin


def _tables(s):
    pos = jnp.arange(s, dtype=F32)[:, None]

    def cs(half):
        inv = jnp.power(10000.0, -jnp.arange(half, dtype=F32) / half)
        ang = pos * inv[None, :]
        return jnp.cos(ang), jnp.sin(ang)

    c32, s32 = cs(32)
    c16, s16 = cs(16)
    z = lambda w: jnp.zeros((s, w), F32)
    o = lambda w: jnp.ones((s, w), F32)
    t = {}
    t["b_cos"] = jnp.concatenate([c32, c32, c32, c32], 1)
    t["b_sin"] = jnp.concatenate([-s32, s32, -s32, s32], 1)
    t["q_cos"] = jnp.concatenate([o(64), c16, c16, z(32)], 1)
    t["q_sin"] = jnp.concatenate([z(64), -s16, s16, z(32)], 1)
    t["k_cos"] = jnp.concatenate([z(32), c16, c16, z(64)], 1)
    t["k_sin"] = jnp.concatenate([z(32), -s16, s16, z(64)], 1)
    lg = jnp.log1p(-jnp.exp2(-5.0 - jnp.arange(N_HEADS, dtype=F32)))
    j = jnp.arange(CHUNK, dtype=F32)
    rel = j[:, None] - j[None, :]
    t["decay"] = jnp.where(rel[None] >= 0, jnp.exp(jnp.maximum(rel, 0.0)[None] * lg[:, None, None]), 0.0)
    t["decay_t"] = jnp.swapaxes(t["decay"], 1, 2)

    def rows(e):
        return jnp.repeat(e.T, HEAD_DIM, axis=1)

    t["qw"] = rows(jnp.exp((j + 1.0)[None, :] * lg[:, None]))
    t["kw"] = rows(jnp.exp((CHUNK - 1 - j)[None, :] * lg[:, None]))
    t["kw2"] = rows(jnp.exp((CHUNK - j)[None, :] * lg[:, None]))
    t["qw0"] = rows(jnp.exp(j[None, :] * lg[:, None]))
    t["cd"] = jnp.repeat(jnp.exp(CHUNK * lg), HEAD_DIM)[None, :]
    e = np.zeros((128, 512), np.float32)
    for h in range(N_HEADS):
        for r in range(32):
            e[MISC_KR + r, 128 * h + 64 + r] = 1.0
    t["place"] = jnp.asarray(e)
    lane_head = np.arange(GROUP) // HEAD_DIM
    t["grp"] = jnp.asarray((lane_head[:, None] == lane_head[None, :]) / HEAD_DIM, _MXU)
    hsel = (np.arange(128)[:, None] == lane_head[None, :]).astype(np.float32)
    t["hsel"] = jnp.asarray(hsel)
    t["hselt"] = jnp.asarray(hsel.T, _MXU)
    return t


def _norm_matmul(x, g, w, name, comm=None):
    s, d = x.shape
    n = w.shape[1]
    tm, tn = min(512, s), 256
    ni = s // tm

    def body(*refs):
        (x_ref, g_ref, w_ref), (z_ref, h_ref), _, cc = _split_refs(refs, 3, 2, comm)
        i = pl.program_id(0)
        _host_gather(comm, cc, i, ni, late=True)
        h = _rms(x_ref[...], g_ref[...]).astype(h_ref.dtype)
        h_ref[...] = h
        for j in range(n // tn):
            z_ref[:, tn * j:tn * (j + 1)] = jnp.dot(h, w_ref[:, tn * j:tn * (j + 1)], preferred_element_type=F32)
        if comm is not None:
            @pl.when(i == ni - 1)
            def _():
                comm.wait(*cc)

    in_specs = [pl.BlockSpec((tm, d), lambda i: (i, 0)), pl.BlockSpec((1, d), lambda i: (0, 0)),
                pl.BlockSpec((d, n), lambda i: (0, 0))]
    out_specs = [pl.BlockSpec((tm, n), lambda i: (i, 0)), pl.BlockSpec((tm, d), lambda i: (i, 0))]
    out_shape = [SDS((s, n), F32), SDS((s, d), _MXU)]
    return _call_with_comm(body, (ni,), in_specs, out_specs, out_shape, [], [x, g, w], comm, ("arbitrary",), name)


def _mm_tn(a, b, name, *, a_fn=None, blocked=False, out_dtype=F32):
    k, m = a.shape
    n = b.shape[1]
    tm, tk = min(1024, m), min(1024, k)
    tn = next(t for t in (1408, 1024, 512, 256, 128) if n % t == 0)
    assert m % tm == 0 and k % tk == 0
    nk = k // tk

    def body(a_ref, b_ref, o_ref, acc):
        kk = pl.program_id(2)

        @pl.when(kk == 0)
        def _():
            acc[...] = jnp.zeros_like(acc)

        av = a_ref[...]
        if a_fn is not None:
            av = a_fn(av.astype(F32))
        acc[...] += _dot_tn(av, b_ref[...])

        @pl.when(kk == nk - 1)
        def _():
            if blocked:
                for c in range(tn // 512):
                    o_ref[c] = acc[:, 512 * c:512 * (c + 1)].astype(o_ref.dtype)
            else:
                o_ref[...] = acc[...].astype(o_ref.dtype)

    if blocked:
        assert tn % 512 == 0
        out_spec = pl.BlockSpec((tn // 512, tm, 512), lambda i, j, kk: (j, i, 0))
        out_shape = SDS((n // 512, m, 512), out_dtype)
    else:
        out_spec = pl.BlockSpec((tm, tn), lambda i, j, kk: (i, j))
        out_shape = SDS((m, n), out_dtype)
    return pl.pallas_call(
        body, grid=(m // tm, n // tn, nk),
        in_specs=[pl.BlockSpec((tk, tm), lambda i, j, kk: (kk, i)), pl.BlockSpec((tk, tn), lambda i, j, kk: (kk, j))],
        out_specs=out_spec, out_shape=out_shape, scratch_shapes=[pltpu.VMEM((tm, tn), F32)],
        compiler_params=_cp("parallel", "parallel", "arbitrary"), name=name)(a, b)


def _split_dot(x, m):
    hi = x.astype(_MXU)
    lo = (x - hi.astype(F32)).astype(_MXU)
    return jnp.dot(hi, m, preferred_element_type=F32) + jnp.dot(lo, m, preferred_element_type=F32)


def _gstandardize(t, grp):
    tc = t - _split_dot(t, grp)
    rs = lax.rsqrt(_split_dot(tc * tc, grp) + EPS)
    return tc * rs, rs


def _gstandardize_bwd(yh, rs, dy, grp):
    return rs * (dy - _split_dot(dy, grp) - yh * _split_dot(dy * yh, grp))


def _head_select(parts):
    hid = lax.broadcasted_iota(jnp.int32, parts[0].shape, 1) // HEAD_DIM
    return jnp.where(hid == 0, parts[0], jnp.where(hid == 1, parts[1], jnp.where(hid == 2, parts[2], parts[3])))


def _head_masked(x):
    hid = lax.broadcasted_iota(jnp.int32, x.shape, 1) // HEAD_DIM
    return [jnp.where(hid == h, x, jnp.zeros_like(x)) for h in range(N_HEADS)]


def _tril(w):
    r = lax.broadcasted_iota(jnp.int32, w.shape, 0)
    c = lax.broadcasted_iota(jnp.int32, w.shape, 1)
    return jnp.where(r >= c, w, 0.0)


def _sgu_mixed(vgb, wcs, bias, nchunk):
    ms = [[jnp.dot(wcs[h], vgb[CHUNK * c:CHUNK * (c + 1)], preferred_element_type=F32) for h in range(N_HEADS)]
          for c in range(nchunk)]
    return [_head_select(ms[c]) + bias for c in range(nchunk)]


def _sgu_fwd(z, gain, w_s, b_t, tb, name):
    s = z.shape[0]
    tm = min(512, s)
    const = lambda a: pl.BlockSpec(a.shape, lambda i: (0,) * a.ndim)

    def body(u_ref, v_ref, g_ref, w_ref, b_ref, grp_ref, hsel_ref, y_ref):
        u = _gelu(u_ref[...])
        vh, _ = _gstandardize(_gelu(v_ref[...]), grp_ref[...])
        vgb = (vh * g_ref[...]).astype(_MXU)
        bias = _dot_exact(b_ref[...], hsel_ref[...])
        wcs = [_tril(w_ref[h]).astype(_MXU) for h in range(N_HEADS)]
        for c, mixed in enumerate(_sgu_mixed(vgb, wcs, bias, tm // CHUNK)):
            r = slice(CHUNK * c, CHUNK * (c + 1))
            y_ref[r, :] = u[r] * mixed

    return pl.pallas_call(
        body, grid=(s // tm,),
        in_specs=[pl.BlockSpec((tm, GROUP), lambda i: (i, _Z_SGU[0])), pl.BlockSpec((tm, GROUP), lambda i: (i, _Z_SGU[1])),
                  pl.BlockSpec((1, GROUP), lambda i: (0, 0)), pl.BlockSpec((N_HEADS, CHUNK, CHUNK), lambda i: (0, 0, 0)),
                  pl.BlockSpec((CHUNK, 128), lambda i: (0, 0)), const(tb["grp"]), const(tb["hsel"])],
        out_specs=pl.BlockSpec((tm, GROUP), lambda i: (i, 0)), out_shape=SDS((s, GROUP), F32),
        compiler_params=_cp("parallel"), name=name)(z, z, gain, w_s, b_t, tb["grp"], tb["hsel"])


def _sgu_bwd(dy, z, gain, w_s, b_t, tb, dz, name):
    s = z.shape[0]
    tm = min(512, s)
    nchunk = tm // CHUNK
    const = lambda a: pl.BlockSpec(a.shape, lambda i: (0,) * a.ndim)

    def body(dy_ref, u_ref, v_ref, g_ref, w_ref, b_ref, grp_ref, hsel_ref, hselt_ref, _, dz_ref, dg_ref, dw_ref, db_ref):
        @pl.when(pl.program_id(0) == 0)
        def _():
            dg_ref[...] = jnp.zeros_like(dg_ref)
            dw_ref[...] = jnp.zeros_like(dw_ref)
            db_ref[...] = jnp.zeros_like(db_ref)

        grp = grp_ref[...]
        u_pre, v_pre, gain_v = u_ref[...], v_ref[...], g_ref[...]
        u = _gelu(u_pre)
        vh, rs = _gstandardize(_gelu(v_pre), grp)
        vgb = (vh * gain_v).astype(_MXU)
        dyv = dy_ref[...]
        bias = _dot_exact(b_ref[...], hsel_ref[...])
        wfs = [_tril(w_ref[h]) for h in range(N_HEADS)]
        wcs = [w.astype(_MXU) for w in wfs]
        wts = [w.T.astype(_MXU) for w in wfs]
        mixed = _sgu_mixed(vgb, wcs, bias, nchunk)
        gu = _gelu_grad(u_pre)
        dms, dmh = [], []
        for c in range(nchunk):
            r = slice(CHUNK * c, CHUNK * (c + 1))
            dz_ref[r, 0:GROUP] = (dyv[r] * mixed[c] * gu[r]).astype(dz_ref.dtype)
            dm = dyv[r] * u[r]
            dms.append(dm)
            dmh.append([m.astype(_MXU) for m in _head_masked(dm)])
        dws = [sum(lax.dot_general(dmh[c][h], vgb[CHUNK * c:CHUNK * (c + 1)], (((1,), (1,)), ((), ())),
                                   preferred_element_type=F32) for c in range(nchunk)) for h in range(N_HEADS)]
        dvg = jnp.concatenate([sum(jnp.dot(wts[h], dmh[c][h], preferred_element_type=F32) for h in range(N_HEADS))
                               for c in range(nchunk)], axis=0)
        for h in range(N_HEADS):
            dw_ref[h] += _tril(dws[h])
        db_ref[...] += sum(_split_dot(dm, hselt_ref[...]) for dm in dms)
        dg_ref[...] += jnp.sum(dvg * vh, axis=0, keepdims=True)
        dv = _gstandardize_bwd(vh, rs, dvg * gain_v, grp)
        dz_ref[:, GROUP:2 * GROUP] = (dv * _gelu_grad(v_pre)).astype(dz_ref.dtype)

    consts = [tb["grp"], tb["hsel"], tb["hselt"]]
    return pl.pallas_call(
        body, grid=(s // tm,),
        in_specs=[pl.BlockSpec((tm, GROUP), lambda i: (i, 0)),
                  pl.BlockSpec((tm, GROUP), lambda i: (i, _Z_SGU[0])), pl.BlockSpec((tm, GROUP), lambda i: (i, _Z_SGU[1])),
                  pl.BlockSpec((1, GROUP), lambda i: (0, 0)), pl.BlockSpec((N_HEADS, CHUNK, CHUNK), lambda i: (0, 0, 0)),
                  pl.BlockSpec((CHUNK, 128), lambda i: (0, 0))] + [const(a) for a in consts]
        + [pl.BlockSpec(memory_space=pl.ANY)],
        out_specs=[pl.BlockSpec((tm, 2 * GROUP), lambda i: (i, _DZ_SGU)), pl.BlockSpec((1, GROUP), lambda i: (0, 0)),
                   pl.BlockSpec((N_HEADS, CHUNK, CHUNK), lambda i: (0, 0, 0)), pl.BlockSpec((CHUNK, 128), lambda i: (0, 0))],
        out_shape=[SDS(dz.shape, dz.dtype), SDS((1, GROUP), F32), SDS((N_HEADS, CHUNK, CHUNK), F32), SDS((CHUNK, 128), F32)],
        input_output_aliases={9: 0}, compiler_params=_cp("arbitrary"), name=name)(dy, z, z, gain, w_s, b_t, *consts, dz)


_SCALE_B = HEAD_DIM ** -0.5
RET_CHUNKS = 4


def _block_diag(compact):
    full = jnp.concatenate([compact] * N_HEADS, axis=0)
    r = lax.broadcasted_iota(jnp.int32, full.shape, 0) // HEAD_DIM
    c = lax.broadcasted_iota(jnp.int32, full.shape, 1) // HEAD_DIM
    return jnp.where(r == c, full, 0.0)


def _diag_blocks(full):
    c = lax.broadcasted_iota(jnp.int32, (HEAD_DIM, GROUP), 1) // HEAD_DIM
    return sum(jnp.where(c == h, full[HEAD_DIM * h:HEAD_DIM * (h + 1), :], 0.0) for h in range(N_HEADS))


def _ret_fwd(z, tb, name):
    s = z.shape[0]
    nc = s // CHUNK
    per = min(RET_CHUNKS, nc)
    rows = per * CHUNK
    row = lambda col: pl.BlockSpec((rows, GROUP), lambda n, col=col: (n, col))
    const = lambda shape: pl.BlockSpec(shape, lambda n: (0,) * len(shape))

    def body(q_ref, k_ref, v_ref, g_ref, cos_ref, sin_ref, dec_ref, qw_ref, kw_ref, cd_ref, grp_ref, y_ref, o_ref, st_ref, state):
        @pl.when(pl.program_id(0) == 0)
        def _():
            state[...] = jnp.zeros_like(state)

        cos, sin = _lanes(cos_ref[...], GROUP), _lanes(sin_ref[...], GROUP)
        q = _rope(q_ref[...], cos, sin, 32)
        k = _rope(k_ref[...], cos, sin, 32) * _SCALE_B
        v = v_ref[...]
        g = g_ref[...]
        rcs = [slice(CHUNK * c, CHUNK * (c + 1)) for c in range(per)]
        vms = [[t.astype(_MXU) for t in _head_masked(v[r])] for r in rcs]
        scs = [[_dot_nt(t.astype(_MXU), k[r]) for t in _head_masked(q[r])] for r in rcs]
        kvs = [_dot_tn(k[r] * kw_ref[...], v[r]) for r in rcs]
        st = state[...]
        crosses = []
        for c, r in enumerate(rcs):
            st_ref[c] = st
            crosses.append(_dot(q[r] * qw_ref[...], _block_diag(st)))
            st = cd_ref[...] * st + _diag_blocks(kvs[c])
        state[...] = st
        outs = []
        for c in range(per):
            scd = [(scs[c][h] * dec_ref[h]).astype(_MXU) for h in range(N_HEADS)]
            outs.append(crosses[c] + sum(jnp.dot(scd[h], vms[c][h], preferred_element_type=F32) for h in range(N_HEADS)))
        o = jnp.concatenate(outs, axis=0)
        o_ref[...] = o
        yh, _ = _gstandardize(o, grp_ref[...])
        y_ref[...] = g * _sigmoid(g) * yh

    return pl.pallas_call(
        body, grid=(nc // per,),
        in_specs=[row(_Z_RET[0]), row(_Z_RET[1]), row(_Z_RET[2]), row(_Z_RET[3]), pl.BlockSpec((rows, 128), lambda n: (n, 0)),
                  pl.BlockSpec((rows, 128), lambda n: (n, 0)), const((N_HEADS, CHUNK, CHUNK)),
                  const((CHUNK, GROUP)), const((CHUNK, GROUP)), const((1, GROUP)), const((GROUP, GROUP))],
        out_specs=[pl.BlockSpec((rows, GROUP), lambda n: (n, 0)), pl.BlockSpec((rows, GROUP), lambda n: (n, 0)),
                   pl.BlockSpec((per, HEAD_DIM, GROUP), lambda n: (n, 0, 0))],
        out_shape=[SDS((s, GROUP), F32), SDS((s, GROUP), F32), SDS((nc, HEAD_DIM, GROUP), F32)],
        scratch_shapes=[pltpu.VMEM((HEAD_DIM, GROUP), F32)],
        compiler_params=_cp("arbitrary"), name=name)(z, z, z, z, tb["b_cos"], tb["b_sin"], tb["decay"], tb["qw"], tb["kw"], tb["cd"],
                                                       tb["grp"])


def _ret_bwd(dy, z, o_pre, states, tb, name):
    s = z.shape[0]
    nc = s // CHUNK
    per = min(RET_CHUNKS, nc)
    rows = per * CHUNK
    ns = nc // per
    rev = lambda col: pl.BlockSpec((rows, GROUP), lambda n, col=col: (ns - 1 - n, col))
    const = lambda shape: pl.BlockSpec(shape, lambda n: (0,) * len(shape))

    def body(dy_ref, q_ref, k_ref, v_ref, g_ref, o_ref, st_ref, cos_ref, sin_ref, dec_ref, dect_ref, qw_ref, kw2_ref, qw0_ref,
             cd_ref, grp_ref, dz_ref, rstate):
        @pl.when(pl.program_id(0) == 0)
        def _():
            rstate[...] = jnp.zeros_like(rstate)

        cos, sin = _lanes(cos_ref[...], GROUP), _lanes(sin_ref[...], GROUP)
        q = _rope(q_ref[...], cos, sin, 32)
        k = _rope(k_ref[...], cos, sin, 32) * _SCALE_B
        v = v_ref[...]
        g = g_ref[...]
        dyv = dy_ref[...]
        sg = _sigmoid(g)
        yh, rs = _gstandardize(o_ref[...], grp_ref[...])
        dz_ref[:, 3 * GROUP:4 * GROUP] = (dyv * yh * (sg * (1.0 + g * (1.0 - sg)))).astype(dz_ref.dtype)
        do = _gstandardize_bwd(yh, rs, dyv * (g * sg), grp_ref[...])
        hs = range(N_HEADS)
        rcs = [slice(CHUNK * c, CHUNK * (c + 1)) for c in range(per)]
        mask = lambda t: [m.astype(_MXU) for m in _head_masked(t)]
        qms, kms, vms, doms = ([mask(t[r]) for r in rcs] for t in (q, k, v, do))
        dps = [[_dot_nt(doms[c][h], v[r]) for h in hs] for c, r in enumerate(rcs)]
        pts = [[_dot_nt(kms[c][h], q[r]) for h in hs] for c, r in enumerate(rcs)]
        dpts = [[_dot_nt(vms[c][h], do[r]) for h in hs] for c, r in enumerate(rcs)]
        dq_x = [_dot_nt(do[r] * qw_ref[...], _block_diag(st_ref[c])) for c, r in enumerate(rcs)]
        r_new = [_dot_tn(q[r] * qw0_ref[...], do[r]) for r in rcs]
        rr = rstate[...]
        dk_x, dv_x = [None] * per, [None] * per
        for c in reversed(range(per)):
            r_bd = _block_diag(rr)
            dk_x[c] = _dot_nt(v[rcs[c]] * kw2_ref[...], r_bd)
            dv_x[c] = _dot(k[rcs[c]] * kw2_ref[...], r_bd)
            rr = cd_ref[...] * rr + _diag_blocks(r_new[c])
        rstate[...] = rr
        dqs, dks, dvs = [], [], []
        for c in range(per):
            dpd = [(dps[c][h] * dec_ref[h]).astype(_MXU) for h in hs]
            dptd = [(dpts[c][h] * dect_ref[h]).astype(_MXU) for h in hs]
            ptd = [(pts[c][h] * dect_ref[h]).astype(_MXU) for h in hs]
            dqs.append(dq_x[c] + sum(jnp.dot(dpd[h], kms[c][h], preferred_element_type=F32) for h in hs))
            dks.append(dk_x[c] + sum(jnp.dot(dptd[h], qms[c][h], preferred_element_type=F32) for h in hs))
            dvs.append(dv_x[c] + sum(jnp.dot(ptd[h], doms[c][h], preferred_element_type=F32) for h in hs))
        dz_ref[:, 0:GROUP] = _rope_bwd(jnp.concatenate(dqs, axis=0), cos, sin, 32).astype(dz_ref.dtype)
        dz_ref[:, GROUP:2 * GROUP] = _rope_bwd(jnp.concatenate(dks, axis=0) * _SCALE_B, cos, sin, 32).astype(dz_ref.dtype)
        dz_ref[:, 2 * GROUP:3 * GROUP] = jnp.concatenate(dvs, axis=0).astype(dz_ref.dtype)

    r0 = lambda: pl.BlockSpec((rows, GROUP), lambda n: (ns - 1 - n, 0))
    r128 = lambda: pl.BlockSpec((rows, 128), lambda n: (ns - 1 - n, 0))
    return pl.pallas_call(
        body, grid=(ns,),
        in_specs=[r0(), rev(_Z_RET[0]), rev(_Z_RET[1]), rev(_Z_RET[2]), rev(_Z_RET[3]), r0(),
                  pl.BlockSpec((per, HEAD_DIM, GROUP), lambda n: (ns - 1 - n, 0, 0)),
                  r128(), r128(), const((N_HEADS, CHUNK, CHUNK)), const((N_HEADS, CHUNK, CHUNK)), const((CHUNK, GROUP)),
                  const((CHUNK, GROUP)), const((CHUNK, GROUP)), const((1, GROUP)), const((GROUP, GROUP))],
        out_specs=pl.BlockSpec((rows, 4 * GROUP), lambda n: (ns - 1 - n, _DZ_RET)),
        out_shape=SDS((s, NZ), _MXU), scratch_shapes=[pltpu.VMEM((HEAD_DIM, GROUP), F32)],
        compiler_params=_cp("arbitrary"), name=name)(
            dy, z, z, z, z, o_pre, states, tb["b_cos"], tb["b_sin"], tb["decay"], tb["decay_t"], tb["qw"], tb["kw2"], tb["qw0"],
            tb["cd"], tb["grp"])


TQ = 256


def _log_sigmoid(x):
    return jnp.minimum(x, 0.0) - jnp.log1p(jnp.exp(-jnp.abs(x)))


def _fox_prep(z, b_f, name):
    s = z.shape[0]
    nb = s // TQ

    def body(m_ref, b_ref, cc_ref, carry):
        @pl.when(pl.program_id(0) == 0)
        def _():
            carry[...] = jnp.zeros_like(carry)

        lane = lax.broadcasted_iota(jnp.int32, (TQ, 128), 1)
        logf = jnp.where(lane < N_HEADS, _log_sigmoid(m_ref[...] + b_ref[...]), 0.0)
        r = lax.broadcasted_iota(jnp.int32, (TQ, TQ), 0)
        c = lax.broadcasted_iota(jnp.int32, (TQ, TQ), 1)
        tri = jnp.where(r >= c, 1.0, 0.0).astype(F32)
        cum = _dot_exact(tri, logf) + carry[...]
        cc_ref[...] = cum * LOG2E
        carry[...] = cum[TQ - 1:TQ, :]

    return pl.pallas_call(
        body, grid=(nb,),
        in_specs=[pl.BlockSpec((TQ, 128), lambda i: (i, NZ // 128 - 1)), pl.BlockSpec((1, 128), lambda i: (0, 0))],
        out_specs=pl.BlockSpec((TQ, 128), lambda i: (i, 0)),
        out_shape=SDS((s, 128), F32), scratch_shapes=[pltpu.VMEM((1, 128), F32)],
        compiler_params=_cp("arbitrary"), name=name)(z, b_f)


def _fox_post(dcr, dcq, z, b_f, dkr, dz, name):
    s = z.shape[0]
    nb = s // TQ

    def body(dc_ref, dcq_ref, m_ref, b_ref, dkr_ref, _, dz_ref, db_ref, carry):
        @pl.when(pl.program_id(0) == 0)
        def _():
            carry[...] = jnp.zeros_like(carry)
            db_ref[...] = jnp.zeros_like(db_ref)

        r = lax.broadcasted_iota(jnp.int32, (TQ, TQ), 0)
        c = lax.broadcasted_iota(jnp.int32, (TQ, TQ), 1)
        triu = jnp.where(c >= r, 1.0, 0.0).astype(F32)
        dc = jnp.concatenate([dc_ref[0], jnp.zeros((120, TQ), F32)], axis=0)
        dlogf = _dot_exact(triu, dc, (((1,), (1,)), ((), ()))) + _dot_exact(triu, dcq_ref[...]) + carry[...]
        carry[...] = dlogf[0:1, :]
        x = m_ref[...] + b_ref[...]
        lane = lax.broadcasted_iota(jnp.int32, (TQ, 128), 1)
        df = jnp.where(lane < N_HEADS, dlogf * _sigmoid(-x), 0.0)
        db_ref[...] += jnp.sum(df, axis=0, keepdims=True)
        dz_ref[...] = (df + dkr_ref[...]).astype(dz_ref.dtype)

    rv = lambda i: nb - 1 - i
    return pl.pallas_call(
        body, grid=(nb,),
        in_specs=[pl.BlockSpec((1, 8, TQ), lambda i: (rv(i), 0, 0)), pl.BlockSpec((TQ, 128), lambda i: (rv(i), 0)),
                  pl.BlockSpec((TQ, 128), lambda i: (rv(i), NZ // 128 - 1)),
                  pl.BlockSpec((1, 128), lambda i: (0, 0)), pl.BlockSpec((TQ, 128), lambda i: (rv(i), 0)),
                  pl.BlockSpec(memory_space=pl.ANY)],
        out_specs=[pl.BlockSpec((TQ, 128), lambda i: (rv(i), _DZ_MISC)), pl.BlockSpec((1, 128), lambda i: (0, 0))],
        out_shape=[SDS(dz.shape, dz.dtype), SDS((1, 128), F32)], scratch_shapes=[pltpu.VMEM((1, 128), F32)],
        input_output_aliases={5: 0}, compiler_params=_cp("arbitrary"), name=name)(dcr, dcq, z, b_f, dkr, dz)


NEG = -1e30


def _causal_mask(shape, transposed=False):
    r = lax.broadcasted_iota(jnp.int32, shape, 0)
    c = lax.broadcasted_iota(jnp.int32, shape, 1)
    return (c >= r) if transposed else (r >= c)


TKV = 512


def _key_block(s):
    return min(TKV, s)


def _diag_mask(shape, off):
    r = lax.broadcasted_iota(jnp.int32, shape, 0)
    c = lax.broadcasted_iota(jnp.int32, shape, 1)
    return c + off >= r


def _head_lanes(h, dqk):
    return slice(128 * (h // 2), 128 * (h // 2) + 128) if dqk == HEAD_DIM else slice(128 * h, 128 * h + 128)


def _keep_half(x, a, axis):
    idx = lax.broadcasted_iota(jnp.int32, x.shape, axis)
    return jnp.where((idx < HEAD_DIM) if a == 0 else (idx >= HEAD_DIM), x, jnp.zeros_like(x))


def _scaled_qt(q, scale):
    qs = q.astype(F32) * (scale * LOG2E)
    return [qs[TQ * b:TQ * (b + 1)].T.astype(_MXU) for b in range(q.shape[0] // TQ)]


def _kv_prep(z, qcol, kcol, vcol, scale, name):
    s = z.shape[0]
    tk = _key_block(s)
    nk = s // tk

    def body(q_ref, k_ref, v_ref, kb_ref, vb_ref, vt_ref, qt_ref):
        kb_ref[...] = k_ref[...].astype(_MXU)
        v = v_ref[...]
        vb_ref[...] = v.astype(_MXU)
        vt_ref[0] = v.T.astype(_MXU)
        for b, t in enumerate(_scaled_qt(q_ref[...], scale)):
            qt_ref[b] = t

    blk = pl.BlockSpec((tk, GROUP), lambda i: (i, 0))
    col = lambda c: pl.BlockSpec((tk, GROUP), lambda i, c=c: (i, c))
    return pl.pallas_call(
        body, grid=(nk,), in_specs=[col(qcol), col(kcol), col(vcol)],
        out_specs=[blk, blk, pl.BlockSpec((1, GROUP, tk), lambda i: (i, 0, 0)),
                   pl.BlockSpec((tk // TQ, GROUP, TQ), lambda i: (i, 0, 0))],
        out_shape=[SDS((s, GROUP), _MXU), SDS((s, GROUP), _MXU), SDS((nk, GROUP, tk), _MXU), SDS((s // TQ, GROUP, TQ), _MXU)],
        compiler_params=_cp("parallel"), name=name)(z, z, z)


LOG2E = 1.4426950408889634


def _attn_fwd(q, qcol, dqk, kb, vt, scale, ck2, name, comm=None):
    s = q.shape[0]
    nq = s // TQ
    tk = _key_block(s)
    ratio = tk // TQ
    wq = N_HEADS * dqk
    bias = ck2 is not None

    def body(*refs):
        ins, (o_ref, l_ref), _, cc = _split_refs(refs, 4 if bias else 3, 2, comm)
        if bias:
            q_ref, k_ref, vt_ref, cc_ref = ins
        else:
            q_ref, k_ref, vt_ref = ins
        i = pl.program_id(0)
        _host_gather(comm, cc, i, nq)
        qts = []
        for h in range(N_HEADS):
            qt = (q_ref[:, _head_lanes(h, dqk)].astype(F32) * (scale * LOG2E)).T
            qts.append((_keep_half(qt, h % 2, 0) if dqk == HEAD_DIM else qt).astype(_MXU))

        def step(j, carry, off):
            r0 = pl.multiple_of(j * tk, tk)
            vtj = vt_ref[j]
            sts = [jnp.dot(k_ref[pl.ds(r0, tk), _head_lanes(h, dqk)], qts[h], preferred_element_type=F32)
                   for h in range(N_HEADS)]
            stats, ps = [], []
            for h in range(N_HEADS):
                m, l, _ = carry[3 * h:3 * h + 3]
                st = sts[h]
                if bias:
                    st = st - cc_ref[pl.ds(r0, tk), h:h + 1]
                if off is not None:
                    st = jnp.where(_diag_mask(st.shape, off), st, NEG)
                m_new = jnp.maximum(m, jnp.max(st, axis=0, keepdims=True))
                alpha = jnp.exp2(m - m_new)
                p = jnp.exp2(st - m_new)
                stats.append((m_new, alpha * l + jnp.sum(p, axis=0, keepdims=True), alpha))
                ps.append(p.astype(_MXU))
            out = []
            for h in range(N_HEADS):
                m_new, l, alpha = stats[h]
                acc = alpha * carry[3 * h + 2] + jnp.dot(vtj[HEAD_DIM * h:HEAD_DIM * (h + 1), :], ps[h],
                                                         preferred_element_type=F32)
                out += [m_new, l, acc]
            return tuple(out)

        init = (jnp.full((1, TQ), NEG, F32), jnp.zeros((1, TQ), F32), jnp.zeros((HEAD_DIM, TQ), F32)) * N_HEADS
        jd = i // ratio
        carry = lax.fori_loop(0, jd, functools.partial(step, off=None), init)
        carry = step(jd, carry, TQ * (i % ratio))
        l_ref[...] = jnp.zeros_like(l_ref)
        for h in range(N_HEADS):
            l_ref[0, h:h + 1, :] = carry[3 * h] + jnp.log2(carry[3 * h + 1])
        for p in range(2):
            ot = jnp.concatenate([carry[6 * p + 2] / carry[6 * p + 1], carry[6 * p + 5] / carry[6 * p + 4]], axis=0)
            o_ref[:, 128 * p:128 * (p + 1)] = ot.T
        if comm is not None:
            @pl.when(i == nq - 1)
            def _():
                comm.wait(*cc)

    rows = pl.BlockSpec((1, 8, TQ), lambda i: (i, 0, 0))
    in_specs = [pl.BlockSpec((TQ, wq), lambda i: (i, qcol)), pl.BlockSpec((s, wq), lambda i: (0, 0)),
                pl.BlockSpec((s // tk, GROUP, tk), lambda i: (0, 0, 0))]
    args = [q, kb, vt]
    if bias:
        in_specs.append(pl.BlockSpec((s, 128), lambda i: (0, 0)))
        args.append(ck2)
    out_specs = [pl.BlockSpec((TQ, GROUP), lambda i: (i, 0)), rows]
    out_shape = [SDS((s, GROUP), F32), SDS((nq, 8, TQ), F32)]
    return _call_with_comm(body, (nq,), in_specs, out_specs, out_shape, [], args, comm, ("arbitrary",), name)


def _call_with_comm(body, grid, in_specs, out_specs, out_shape, scratch, args, comm, semantics, name, aliases=None):
    n_out = len(out_shape)
    if comm is not None:
        in_specs, out_specs = in_specs + comm.in_specs, out_specs + comm.out_specs
        out_shape, scratch, args = out_shape + comm.out_shape, scratch + comm.scratch, list(args) + comm.arrs
    res = pl.pallas_call(body, grid=grid, in_specs=in_specs, out_specs=out_specs, out_shape=out_shape,
                         scratch_shapes=scratch, input_output_aliases=aliases or {}, compiler_params=_cp(*semantics),
                         name=name)(*args)
    return (*res[:n_out], list(res[n_out:]))


def _attn_bwd(kb, vb, qt, dot, lse, dl, dqk, scale, ck2, name, kv_dtype, comm=None, kv_into=None):
    s = kb.shape[0]
    nq = s // TQ
    tk = _key_block(s)
    ratio = tk // TQ
    nkb = s // tk
    wq = N_HEADS * dqk
    bias = ck2 is not None

    merged = kv_into is not None
    n_in = 6 + bias + merged
    n_out = 3 + 2 * bias - merged

    def body(*refs):
        ins, outs, _, cc = _split_refs(refs, n_in, n_out, comm)
        k_ref, v_ref, qt_ref, dot_ref, l_ref, d_ref = ins[:6]
        cc_ref = ins[6] if bias else None
        dqt_ref = outs[0]
        if merged:
            dk_ref, dv_ref = outs[1].at[:, 0:wq], outs[1].at[:, wq:wq + GROUP]
        else:
            dk_ref, dv_ref = outs[1], outs[2]
        if bias:
            dck_ref, dcq_ref = outs[-2:]
        j = pl.program_id(0)

        @pl.when(j == 0)
        def _():
            if comm is not None:
                comm.start(*cc)
            dqt_ref[...] = jnp.zeros_like(dqt_ref)
            if bias:
                dcq_ref[...] = jnp.zeros_like(dcq_ref)

        ks, kts, vs = [], [], []
        for h in range(N_HEADS):
            k2 = k_ref[:, _head_lanes(h, dqk)]
            if dqk == HEAD_DIM:
                k2 = _keep_half(k2, h % 2, 1)
            ks.append(k2)
            kts.append(k2.astype(F32).T.astype(_MXU))
            vs.append(_keep_half(v_ref[:, _head_lanes(h, HEAD_DIM)], h % 2, 1))
        cks = [cc_ref[:, h:h + 1] for h in range(N_HEADS)] if bias else None

        nt = (((1,), (1,)), ((), ()))

        def step(i, carry, off):
            qti, doti, li, di = qt_ref[i], dot_ref[i], l_ref[i], d_ref[i]
            qls = [_head_lanes(h, dqk) for h in range(N_HEADS)]
            vls = [_head_lanes(h, HEAD_DIM) for h in range(N_HEADS)]
            sts, dpts = [], []
            for h in range(N_HEADS):
                sts.append(jnp.dot(ks[h], qti[qls[h], :], preferred_element_type=F32))
                dpts.append(jnp.dot(vs[h], doti[vls[h], :], preferred_element_type=F32))
            pbs, dsbs, dcks = [], [], []
            for h in range(N_HEADS):
                st = sts[h] - li[h:h + 1, :]
                if bias:
                    st = st - cks[h]
                p = jnp.exp2(st)
                if off is not None:
                    p = jnp.where(_diag_mask(p.shape, off), p, 0.0)
                dst = p * (dpts[h] - di[h:h + 1, :])
                pbs.append(p.astype(_MXU))
                dsbs.append(dst.astype(_MXU))
                if bias:
                    dcks.append(carry[3 * h + 2] + jnp.sum(dst, axis=1, keepdims=True))
                    dcq_ref[i, h:h + 1, :] += jnp.sum(dst, axis=0, keepdims=True)
                else:
                    dcks.append(carry[3 * h + 2])
            out = []
            for h in range(N_HEADS):
                dvt = carry[3 * h + 1] + lax.dot_general(doti[HEAD_DIM * h:HEAD_DIM * (h + 1), :], pbs[h], nt,
                                                         preferred_element_type=F32)
                dkt = carry[3 * h] + lax.dot_general(qti[dqk * h:dqk * (h + 1), :], dsbs[h], nt, preferred_element_type=F32)
                dqt_ref[i, qls[h], :] += jnp.dot(kts[h], dsbs[h], preferred_element_type=F32) * scale
                out += [dkt, dvt, dcks[h]]
            return tuple(out)

        carry = (jnp.zeros((dqk, tk), F32), jnp.zeros((HEAD_DIM, tk), F32), jnp.zeros((tk, 1), F32)) * N_HEADS
        for r in range(ratio):
            carry = step(ratio * j + r, carry, TQ * r)
        carry = lax.fori_loop(ratio * (j + 1), nq, functools.partial(step, off=None), carry)
        for p in range(2):
            dv_ref[:, 128 * p:128 * (p + 1)] = jnp.concatenate([carry[6 * p + 1], carry[6 * p + 4]], axis=0).T.astype(dv_ref.dtype)
            if dqk == HEAD_DIM:
                dk_ref[:, 128 * p:128 * (p + 1)] = (jnp.concatenate([carry[6 * p], carry[6 * p + 3]], axis=0).T
                                                    * (1.0 / LOG2E)).astype(dk_ref.dtype)
        if dqk != HEAD_DIM:
            for h in range(N_HEADS):
                dk_ref[:, 128 * h:128 * (h + 1)] = (carry[3 * h].T * (1.0 / LOG2E)).astype(dk_ref.dtype)
        if bias:
            dck_ref[...] = jnp.zeros_like(dck_ref)
            for h in range(N_HEADS):
                dck_ref[:, h:h + 1] = -carry[3 * h + 2]
        if comm is not None:
            @pl.when(j == nkb - 1)
            def _():
                comm.wait(*cc)

    blk = lambda w: pl.BlockSpec((tk, w), lambda j: (j, 0))
    full3 = lambda w: pl.BlockSpec((nq, w, TQ), lambda j: (0, 0, 0))
    in_specs = [blk(wq), blk(GROUP), full3(wq), full3(GROUP), full3(8), full3(8)]
    args = [kb, vb, qt, dot, lse, dl]
    if merged:
        assert wq == GROUP
        out_specs = [full3(wq), pl.BlockSpec((tk, wq + GROUP), lambda j: (j, _DZ_FOX_KV))]
        out_shape = [SDS((nq, wq, TQ), F32), SDS(kv_into.shape, kv_into.dtype)]
    else:
        out_specs = [full3(wq), blk(wq), blk(GROUP)]
        out_shape = [SDS((nq, wq, TQ), F32), SDS((s, wq), kv_dtype), SDS((s, GROUP), kv_dtype)]
    if bias:
        in_specs.append(blk(128))
        args.append(ck2)
        out_specs += [blk(128), full3(8)]
        out_shape += [SDS((s, 128), F32), SDS((nq, 8, TQ), F32)]
    aliases = {}
    if merged:
        in_specs.append(pl.BlockSpec(memory_space=pl.ANY))
        args.append(kv_into)
        aliases = {len(args) - 1: 1}
    return _call_with_comm(body, (nkb,), in_specs, out_specs, out_shape, [], args, comm, ("arbitrary",), name, aliases)


def _untranspose(xt, dtype, name, into=None, col=0):
    nq, w, _ = xt.shape
    if into is not None:
        def body_into(x_ref, _, o_ref):
            o_ref[...] = x_ref[0].T.astype(o_ref.dtype)

        return pl.pallas_call(
            body_into, grid=(nq,),
            in_specs=[pl.BlockSpec((1, w, TQ), lambda i: (i, 0, 0)), pl.BlockSpec(memory_space=pl.ANY)],
            out_specs=pl.BlockSpec((TQ, w), lambda i: (i, col)), out_shape=SDS(into.shape, into.dtype),
            input_output_aliases={1: 0}, compiler_params=_cp("parallel"), name=name)(xt, into)

    def body(x_ref, o_ref):
        o_ref[...] = x_ref[0].T.astype(o_ref.dtype)

    return pl.pallas_call(
        body, grid=(nq,), in_specs=[pl.BlockSpec((1, w, TQ), lambda i: (i, 0, 0))],
        out_specs=pl.BlockSpec((TQ, w), lambda i: (i, 0)), out_shape=SDS((nq * TQ, w), dtype),
        compiler_params=_cp("parallel"), name=name)(xt)


_SCALE_D = (64 + 32) ** -0.5
_COL_CQ, _COL_CKV, _COL_MISC = 2304 // 256, 2560 // 128, 2688 // 128


def _mla_prep(z, gq, gkv, wq, wk, wv, tb, name):
    s = z.shape[0]
    tm = _key_block(s)
    row = lambda w, c: pl.BlockSpec((tm, w), lambda i, c=c: (i, c))
    const = lambda a: pl.BlockSpec(a.shape, lambda i: (0,) * a.ndim)

    def body(cq_ref, ckv_ref, m_ref, gq_ref, gkv_ref, wq_ref, wk_ref, wv_ref, e_ref, qc_ref, qs_ref, kc_ref, ks_ref,
             q_ref, k_ref, v_ref, vt_ref, cqn_ref, ckvn_ref, qt_ref):
        cqn = _rms(cq_ref[...], gq_ref[...]).astype(_MXU)
        ckvn = _rms(ckv_ref[...], gkv_ref[...]).astype(_MXU)
        cqn_ref[...] = cqn
        ckvn_ref[...] = ckvn
        qb = _rope(_dot(cqn, wq_ref[...]), _lanes(qc_ref[...], 512), _lanes(qs_ref[...], 512), 16).astype(q_ref.dtype)
        q_ref[...] = qb
        for b, t in enumerate(_scaled_qt(qb, _SCALE_D)):
            qt_ref[b] = t
        kr = _rope(m_ref[...], kc_ref[...], ks_ref[...], 16)
        k_ref[...] = (_dot(ckvn, wk_ref[...]) + _dot(kr, e_ref[...])).astype(k_ref.dtype)
        v = _dot(ckvn, wv_ref[...])
        v_ref[...] = v.astype(v_ref.dtype)
        vt_ref[0] = v.T.astype(vt_ref.dtype)

    e = tb["place"]
    return pl.pallas_call(
        body, grid=(s // tm,),
        in_specs=[row(256, _COL_CQ), row(128, _COL_CKV), row(128, _COL_MISC), const(gq), const(gkv), const(wq), const(wk),
                  const(wv), const(e), row(128, 0), row(128, 0), row(128, 0), row(128, 0)],
        out_specs=[row(512, 0), row(512, 0), row(256, 0), pl.BlockSpec((1, GROUP, tm), lambda i: (i, 0, 0)), row(256, 0),
                   row(128, 0), pl.BlockSpec((tm // TQ, 512, TQ), lambda i: (i, 0, 0))],
        out_shape=[SDS((s, 512), _MXU), SDS((s, 512), _MXU), SDS((s, 256), _MXU), SDS((s // tm, GROUP, tm), _MXU),
                   SDS((s, 256), _MXU), SDS((s, 128), _MXU), SDS((s // TQ, 512, TQ), _MXU)],
        compiler_params=_cp("parallel"), name=name)(
            z, z, z, gq, gkv, wq, wk, wv, e, tb["q_cos"], tb["q_sin"], tb["k_cos"], tb["k_sin"])


def _mla_prep_bwd(dqt, dk, dv, z, cqn, ckvn, gq, gkv, wq, wk, wv, tb, dz, name):
    s = z.shape[0]
    tm = min(512, s)
    row = lambda w, c: pl.BlockSpec((tm, w), lambda i, c=c: (i, c))
    const = lambda a: pl.BlockSpec(a.shape, lambda i: (0,) * a.ndim)
    acc = lambda shape: pl.BlockSpec(shape, lambda i: (0, 0))

    def body(dq_ref, dk_ref, dv_ref, cq_ref, ckv_ref, cqn_ref, ckvn_ref, gq_ref, gkv_ref, wq_ref, wk_ref, wv_ref, e_ref,
             qc_ref, qs_ref, kc_ref, ks_ref, _, dz_ref, dkr_ref, dwq_ref, dwk_ref, dwv_ref, dgq_ref, dgkv_ref):
        dcq_ref, dckv_ref = dz_ref.at[:, 0:256], dz_ref.at[:, 256:384]

        @pl.when(pl.program_id(0) == 0)
        def _():
            for r in (dwq_ref, dwk_ref, dwv_ref, dgq_ref, dgkv_ref):
                r[...] = jnp.zeros_like(r)

        dq = jnp.concatenate([dq_ref[b].T for b in range(tm // TQ)], axis=0)
        dqp = _rope_bwd(dq, _lanes(qc_ref[...], 512), _lanes(qs_ref[...], 512), 16)
        dkd = dk_ref[...]
        dvd = dv_ref[...]
        dwq_ref[...] += _dot_tn(cqn_ref[...], dqp)
        dwk_ref[...] += _dot_tn(ckvn_ref[...], dkd)
        dwv_ref[...] += _dot_tn(ckvn_ref[...], dvd)
        dcq, dgq = _rms_bwd(cq_ref[...], gq_ref[...], _dot_nt(dqp, wq_ref[...]))
        dckv, dgkv = _rms_bwd(ckv_ref[...], gkv_ref[...], _dot_nt(dkd, wk_ref[...]) + _dot_nt(dvd, wv_ref[...]))
        dcq_ref[...] = dcq.astype(dcq_ref.dtype)
        dckv_ref[...] = dckv.astype(dckv_ref.dtype)
        dgq_ref[...] += dgq
        dgkv_ref[...] += dgkv
        dkr = _dot_exact(dkd, e_ref[...], (((1,), (1,)), ((), ())))
        dkr_ref[...] = _rope_bwd(dkr, kc_ref[...], ks_ref[...], 16)

    e = tb["place"]
    return pl.pallas_call(
        body, grid=(s // tm,),
        in_specs=[pl.BlockSpec((tm // TQ, 512, TQ), lambda i: (i, 0, 0)), row(512, 0), row(256, 0), row(256, _COL_CQ),
                  row(128, _COL_CKV), row(256, 0), row(128, 0),
                  const(gq), const(gkv), const(wq), const(wk), const(wv), const(e), row(128, 0), row(128, 0), row(128, 0), row(128, 0),
                  pl.BlockSpec(memory_space=pl.ANY)],
        out_specs=[row(384, _DZ_MLA), row(128, 0), acc((256, 512)), acc((128, 512)), acc((128, 256)), acc((1, 256)),
                   acc((1, 128))],
        out_shape=[SDS(dz.shape, dz.dtype), SDS((s, 128), F32), SDS((256, 512), F32), SDS((128, 512), F32),
                   SDS((128, 256), F32), SDS((1, 256), F32), SDS((1, 128), F32)],
        input_output_aliases={17: 0}, compiler_params=_cp("arbitrary"), name=name)(
            dqt, dk, dv, z, z, cqn, ckvn, gq, gkv, wq, wk, wv, e, tb["q_cos"], tb["q_sin"], tb["k_cos"], tb["k_sin"], dz)


def _out_proj(ys, g, w, x, name):
    s, d = x.shape
    tm = min(512, s)

    def body(ya, yb, yc, yd, g_ref, w_ref, x_ref, o_ref, yn_ref):
        acc = x_ref[...]
        for i, y_ref in enumerate((ya, yb, yc, yd)):
            sl = slice(GROUP * i, GROUP * (i + 1))
            yn = _rms(y_ref[...], g_ref[:, sl]).astype(_MXU)
            yn_ref[:, sl] = yn
            acc = acc + jnp.dot(yn, w_ref[sl, :], preferred_element_type=F32)
        o_ref[...] = acc

    yspec = pl.BlockSpec((tm, GROUP), lambda i: (i, 0))
    return pl.pallas_call(
        body, grid=(s // tm,),
        in_specs=[yspec, yspec, yspec, yspec, pl.BlockSpec((1, d), lambda i: (0, 0)), pl.BlockSpec((d, d), lambda i: (0, 0)),
                  pl.BlockSpec((tm, d), lambda i: (i, 0))],
        out_specs=[pl.BlockSpec((tm, d), lambda i: (i, 0)), pl.BlockSpec((tm, d), lambda i: (i, 0))],
        out_shape=[SDS((s, d), F32), SDS((s, d), _MXU)], compiler_params=_cp("parallel"), name=name)(*ys, g, w, x)


def _out_proj_bwd(dx, w, ys, g, name):
    s, d = dx.shape
    tm = min(512, s)
    nb = tm // TQ

    def body(dx_ref, w_ref, ya, yb, yc, yd, g_ref, da, db, dg_ref, dtc_ref, dtd_ref, dlc_ref, dld_ref):
        @pl.when(pl.program_id(0) == 0)
        def _():
            dg_ref[...] = jnp.zeros_like(dg_ref)

        dyn = _dot_nt(dx_ref[...], w_ref[...])
        for i, y_ref in enumerate((ya, yb, yc, yd)):
            sl = slice(GROUP * i, GROUP * (i + 1))
            y = y_ref[...]
            dy, dg = _rms_bwd(y, g_ref[:, sl], dyn[:, sl])
            dg_ref[:, sl] += dg
            if i < 2:
                (da, db)[i][...] = dy
                continue
            dt_ref, dl_ref = ((dtc_ref, dlc_ref), (dtd_ref, dld_ref))[i - 2]
            dl_ref[...] = jnp.zeros_like(dl_ref)
            for b in range(nb):
                r = slice(TQ * b, TQ * (b + 1))
                dt_ref[b] = dy[r].T.astype(dt_ref.dtype)
                pt = (dy[r] * y[r]).T
                for h in range(N_HEADS):
                    dl_ref[b, h:h + 1, :] = jnp.sum(pt[HEAD_DIM * h:HEAD_DIM * (h + 1), :], axis=0, keepdims=True)

    yspec = pl.BlockSpec((tm, GROUP), lambda i: (i, 0))
    tspec = pl.BlockSpec((nb, GROUP, TQ), lambda i: (i, 0, 0))
    lspec = pl.BlockSpec((nb, 8, TQ), lambda i: (i, 0, 0))
    return pl.pallas_call(
        body, grid=(s // tm,),
        in_specs=[pl.BlockSpec((tm, d), lambda i: (i, 0)), pl.BlockSpec((d, d), lambda i: (0, 0)), yspec, yspec, yspec, yspec,
                  pl.BlockSpec((1, d), lambda i: (0, 0))],
        out_specs=[yspec, yspec, pl.BlockSpec((1, d), lambda i: (0, 0)), tspec, tspec, lspec, lspec],
        out_shape=[SDS((s, GROUP), F32)] * 2 + [SDS((1, d), F32)] + [SDS((s // TQ, GROUP, TQ), _MXU)] * 2
        + [SDS((s // TQ, 8, TQ), F32)] * 2,
        compiler_params=_cp("arbitrary"), name=name)(dx, w, *ys, g)


FF_BLOCK = 512
FF_ROWS = 1024


def _ffn_fwd(x, g, wu, wd, name, comm=None):
    s, d = x.shape
    nj = wu.shape[0]
    tm = min(FF_ROWS, s)
    ni = s // tm

    def body(*refs):
        (x_ref, g_ref, wu_ref, wd_ref), (o_ref, u_ref, h_ref), (acc,), cc = _split_refs(refs, 4, 3, comm)
        i, j = pl.program_id(0), pl.program_id(1)
        _host_gather(comm, cc, i * nj + j, ni * nj)

        @pl.when(j == 0)
        def _():
            h_ref[...] = _rms(x_ref[...], g_ref[...]).astype(h_ref.dtype)
            acc[...] = jnp.zeros_like(acc)

        halves = [slice(r, r + tm // 2) for r in range(0, tm, tm // 2)]
        us = [jnp.dot(h_ref[r, :], wu_ref[0], preferred_element_type=F32) for r in halves]
        for r, u in zip(halves, us):
            u_ref[r, :] = u.astype(u_ref.dtype)
            acc[r, :] += _dot(jnp.square(jnp.maximum(u, 0.0)), wd_ref[...])

        @pl.when(j == nj - 1)
        def _():
            o_ref[...] = x_ref[...] + acc[...]

        if comm is not None:
            @pl.when((i == ni - 1) & (j == nj - 1))
            def _():
                comm.wait(*cc)

    in_specs = [pl.BlockSpec((tm, d), lambda i, j: (i, 0)), pl.BlockSpec((1, d), lambda i, j: (0, 0)),
                pl.BlockSpec((1, d, FF_BLOCK), lambda i, j: (j, 0, 0)), pl.BlockSpec((FF_BLOCK, d), lambda i, j: (j, 0))]
    out_specs = [pl.BlockSpec((tm, d), lambda i, j: (i, 0)), pl.BlockSpec((tm, FF_BLOCK), lambda i, j: (i, j)),
                 pl.BlockSpec((tm, d), lambda i, j: (i, 0))]
    out_shape = [SDS((s, d), F32), SDS((s, nj * FF_BLOCK), _MXU), SDS((s, d), _MXU)]
    return _call_with_comm(body, (ni, nj), in_specs, out_specs, out_shape, [pltpu.VMEM((tm, d), F32)], [x, g, wu, wd], comm,
                           ("arbitrary", "arbitrary"), name)


def _ffn_bwd(dx2, x, u, g, wu, wd, name, comm=None):
    s, d = x.shape
    nj = wu.shape[0]
    tm = min(FF_ROWS, s)
    ni = s // tm

    def body(*refs):
        (dx_ref, x_ref, u_ref, g_ref, wu_ref, wd_ref), (o_ref, du_ref, dg_ref), (acc, dxb), cc = _split_refs(refs, 6, 3, comm)
        i, j = pl.program_id(0), pl.program_id(1)

        @pl.when((i == 0) & (j == 0))
        def _():
            if comm is not None:
                comm.start(*cc)
            dg_ref[...] = jnp.zeros_like(dg_ref)

        @pl.when(j == 0)
        def _():
            dxb[...] = dx_ref[...].astype(dxb.dtype)
            acc[...] = jnp.zeros_like(acc)

        nt = (((1,), (1,)), ((), ()))
        halves = [slice(r, r + tm // 2) for r in range(0, tm, tm // 2)]
        das = [lax.dot_general(dxb[r, :], wd_ref[...], nt, preferred_element_type=F32) for r in halves]
        for r, da in zip(halves, das):
            du = (da * 2.0 * jnp.maximum(u_ref[r, :].astype(F32), 0.0)).astype(du_ref.dtype)
            du_ref[r, :] = du
            acc[r, :] += lax.dot_general(du, wu_ref[0], nt, preferred_element_type=F32)

        @pl.when(j == nj - 1)
        def _():
            dxn, dg = _rms_bwd(x_ref[...], g_ref[...], acc[...])
            o_ref[...] = dx_ref[...] + dxn
            dg_ref[...] += dg

        if comm is not None:
            @pl.when((i == ni - 1) & (j == nj - 1))
            def _():
                comm.wait(*cc)

    in_specs = [pl.BlockSpec((tm, d), lambda i, j: (i, 0)), pl.BlockSpec((tm, d), lambda i, j: (i, 0)),
                pl.BlockSpec((tm, FF_BLOCK), lambda i, j: (i, j)), pl.BlockSpec((1, d), lambda i, j: (0, 0)),
                pl.BlockSpec((1, d, FF_BLOCK), lambda i, j: (j, 0, 0)), pl.BlockSpec((FF_BLOCK, d), lambda i, j: (j, 0))]
    out_specs = [pl.BlockSpec((tm, d), lambda i, j: (i, 0)), pl.BlockSpec((tm, FF_BLOCK), lambda i, j: (i, j)),
                 pl.BlockSpec((1, d), lambda i, j: (0, 0))]
    out_shape = [SDS((s, d), F32), SDS((s, nj * FF_BLOCK), _MXU), SDS((1, d), F32)]
    return _call_with_comm(body, (ni, nj), in_specs, out_specs, out_shape,
                           [pltpu.VMEM((tm, d), F32), pltpu.VMEM((tm, d), _MXU)], [dx2, x, u, g, wu, wd], comm,
                           ("arbitrary", "arbitrary"), name)


def _in_proj_bwd(dz, w, x, g, dx_up, name, comm=None):
    s, d = x.shape
    n = w.shape[1]
    tm = min(512, s)
    ni = s // tm

    def body(*refs):
        (dz_ref, w_ref, x_ref, g_ref, up_ref), (o_ref, dg_ref), _, cc = _split_refs(refs, 5, 2, comm)
        i = pl.program_id(0)

        @pl.when(i == 0)
        def _():
            if comm is not None:
                comm.start(*cc)
            dg_ref[...] = jnp.zeros_like(dg_ref)

        dh = lax.dot_general(dz_ref[...], w_ref[...], (((1,), (1,)), ((), ())), preferred_element_type=F32)
        dxn, dg = _rms_bwd(x_ref[...], g_ref[...], dh)
        o_ref[...] = up_ref[...] + dxn
        dg_ref[...] += dg
        if comm is not None:
            @pl.when(i == ni - 1)
            def _():
                comm.wait(*cc)

    in_specs = [pl.BlockSpec((tm, n), lambda i: (i, 0)), pl.BlockSpec((d, n), lambda i: (0, 0)),
                pl.BlockSpec((tm, d), lambda i: (i, 0)), pl.BlockSpec((1, d), lambda i: (0, 0)),
                pl.BlockSpec((tm, d), lambda i: (i, 0))]
    out_specs = [pl.BlockSpec((tm, d), lambda i: (i, 0)), pl.BlockSpec((1, d), lambda i: (0, 0))]
    out_shape = [SDS((s, d), F32), SDS((1, d), F32)]
    return _call_with_comm(body, (ni,), in_specs, out_specs, out_shape, [], [dz, w, x, g, dx_up], comm, ("arbitrary",), name)


def _loss_head(x, g, target, name):
    s, d = x.shape
    tm = min(512, s)

    def body(x_ref, g_ref, t_ref, l_ref, dx_ref, dg_ref):
        @pl.when(pl.program_id(0) == 0)
        def _():
            l_ref[...] = jnp.zeros_like(l_ref)
            dg_ref[...] = jnp.zeros_like(dg_ref)

        xv = x_ref[...]
        err = _rms(xv, g_ref[...]) - t_ref[...]
        l_ref[...] += jnp.sum(err * err, axis=0, keepdims=True) * (0.5 / d)
        dx, dg = _rms_bwd(xv, g_ref[...], err * (1.0 / d))
        dx_ref[...] = dx
        dg_ref[...] += dg

    return pl.pallas_call(
        body, grid=(s // tm,),
        in_specs=[pl.BlockSpec((tm, d), lambda i: (i, 0)), pl.BlockSpec((1, d), lambda i: (0, 0)),
                  pl.BlockSpec((tm, d), lambda i: (i, 0))],
        out_specs=[pl.BlockSpec((1, d), lambda i: (0, 0)), pl.BlockSpec((tm, d), lambda i: (i, 0)),
                   pl.BlockSpec((1, d), lambda i: (0, 0))],
        out_shape=[SDS((1, d), F32), SDS((s, d), F32), SDS((1, d), F32)], compiler_params=_cp("arbitrary"), name=name)(x, g, target)


def _me_and_peer():
    x, y, c = lax.axis_index("x"), lax.axis_index("y"), lax.axis_index("c")
    me = 4 * x + 2 * y + c

    def peer(k):
        px, py, pc = x ^ (k >> 2), y ^ ((k >> 1) & 1), c ^ (k & 1)
        return (px, py, pc), 4 * px + 2 * py + pc

    return me, peer


class _Comm:
    CHIPS = (2, 4, 6)

    def __init__(self, kind, arrs):
        assert kind in ("gather", "exchange")
        self.kind, self.arrs, self.n = kind, list(arrs), len(arrs)
        anyspec = pl.BlockSpec(memory_space=pl.ANY)
        self.in_specs = [anyspec] * self.n
        self.out_specs = [anyspec] * self.n
        self.out_shape = [SDS(((NDEV,) + a.shape) if kind == "gather" else a.shape, a.dtype) for a in self.arrs]
        npair = NDEV - 1 + len(self.CHIPS)
        self.scratch = [pltpu.SemaphoreType.DMA((self.n, npair)), pltpu.SemaphoreType.DMA((self.n, npair)),
                        pltpu.SemaphoreType.DMA((self.n,))]

    def _copies(self, ins, outs, sems):
        send, recv, loc = sems
        me, peer = _me_and_peer()
        gather = self.kind == "gather"
        sibling = peer(1)[0]
        local = [pltpu.make_async_copy(ins[a] if gather else ins[a].at[me], outs[a].at[me], loc.at[a]) for a in range(self.n)]
        outgoing, incoming, forwards, forwarded = [], [], [], []
        for k in ((1,) + self.CHIPS) if gather else range(1, NDEV):
            dev, pid = peer(k)
            for a in range(self.n):
                pair = dict(send_sem=send.at[a, k - 1], recv_sem=recv.at[a, k - 1], device_id=dev, device_id_type=MESH)
                outgoing.append(pltpu.make_async_remote_copy(src_ref=ins[a] if gather else ins[a].at[pid],
                                                             dst_ref=outs[a].at[me], **pair))
                incoming.append(pltpu.make_async_remote_copy(src_ref=ins[a] if gather else ins[a].at[me],
                                                             dst_ref=outs[a].at[pid], **pair))
        if gather:
            for idx, k in enumerate(self.CHIPS):
                got, theirs = peer(k)[1], peer(k + 1)[1]
                for a in range(self.n):
                    pair = dict(send_sem=send.at[a, NDEV - 1 + idx], recv_sem=recv.at[a, NDEV - 1 + idx], device_id=sibling,
                                device_id_type=MESH)
                    forwards.append(pltpu.make_async_remote_copy(src_ref=outs[a].at[got], dst_ref=outs[a].at[got], **pair))
                    forwarded.append(pltpu.make_async_remote_copy(src_ref=outs[a].at[theirs], dst_ref=outs[a].at[theirs], **pair))
        return local, outgoing, incoming, forwards, forwarded

    def start(self, ins, outs, sems):
        local, outgoing, _, _, _ = self._copies(ins, outs, sems)
        for cp in local + outgoing:
            cp.start()

    def forward(self, ins, outs, sems):
        _, _, incoming, forwards, _ = self._copies(ins, outs, sems)
        per = self.n
        for idx in range(len(forwards) // per if per else 0):
            for a in range(per):
                incoming[(1 + idx) * per + a].wait_recv()
                forwards[idx * per + a].start()

    def wait(self, ins, outs, sems):
        local, outgoing, incoming, forwards, forwarded = self._copies(ins, outs, sems)
        for cp in (incoming[:self.n] if self.kind == "gather" else incoming) + forwarded:
            cp.wait_recv()
        for cp in outgoing + forwards:
            cp.wait_send()
        for cp in local:
            cp.wait()


def _host_gather(comm, cc, step, nsteps, late=False):
    if comm is None:
        return

    @pl.when(step == 0)
    def _():
        comm.start(*cc)

    @pl.when(step == (nsteps - 1 if late else (2 * nsteps) // 3))
    def _():
        comm.forward(*cc)


def _split_refs(refs, n_in, n_out, comm):
    c = comm.n if comm is not None else 0
    ins, cin = refs[:n_in], refs[n_in:n_in + c]
    outs, cout = refs[n_in + c:n_in + c + n_out], refs[n_in + c + n_out:n_in + 2 * c + n_out]
    rest = refs[n_in + 2 * c + n_out:]
    scratch, csem = (rest[:len(rest) - 3], rest[len(rest) - 3:]) if c else (rest, ())
    return ins, outs, scratch, (cin, cout, csem)


def _comm_call(kind, arrs, name):
    comm = _Comm(kind, arrs)

    def body(*refs):
        _, _, _, c = _split_refs(refs, 0, 0, comm)
        comm.start(*c)
        if kind == "gather":
            comm.forward(*c)
        comm.wait(*c)

    return pl.pallas_call(body, in_specs=comm.in_specs, out_specs=comm.out_specs, out_shape=comm.out_shape,
                          scratch_shapes=comm.scratch, compiler_params=pltpu.CompilerParams(has_side_effects=True),
                          name=name)(*arrs)


def _all_gather(arrs, name):
    return _comm_call("gather", arrs, name)


def _exchange(arrs, name):
    return _comm_call("exchange", arrs, name)


def _sum_slots(parts, name):
    _, r, c = parts.shape
    tr = r if r <= 512 else 512

    def body(p_ref, o_ref):
        acc = p_ref[0].astype(F32)
        for q in range(1, NDEV):
            acc = acc + p_ref[q].astype(F32)
        o_ref[...] = acc

    return pl.pallas_call(
        body, grid=(r // tr,), in_specs=[pl.BlockSpec((NDEV, tr, c), lambda i: (0, i, 0))],
        out_specs=pl.BlockSpec((tr, c), lambda i: (i, 0)), out_shape=SDS((r, c), F32),
        compiler_params=_cp("parallel"), name=name)(parts)


def _adamw(g, w, m, v, name):
    r, c = w.shape
    parts = g.ndim == 3
    tr = r
    for cand in (512, 256, 128, 64, 32, 16, 8):
        if r > cand and r % cand == 0 and cand * c * 4 <= 2 * 1024 * 1024:
            tr = cand
            break
    bc1 = 1.0 / (1.0 - ADAM_B1 ** ADAM_STEP)
    bc2 = 1.0 / (1.0 - ADAM_B2 ** ADAM_STEP)

    def body(g_ref, w_ref, m_ref, v_ref, go_ref, d_ref, mo_ref, vo_ref):
        if parts:
            gv = g_ref[0].astype(F32)
            for q in range(1, NDEV):
                gv = gv + g_ref[q].astype(F32)
        else:
            gv = g_ref[...]
        mn = ADAM_B1 * m_ref[...] + (1.0 - ADAM_B1) * gv
        vn = ADAM_B2 * v_ref[...] + (1.0 - ADAM_B2) * (gv * gv)
        go_ref[...] = gv
        mo_ref[...] = mn
        vo_ref[...] = vn
        d_ref[...] = -ADAM_LR * ((mn * bc1) / (jnp.sqrt(vn * bc2) + ADAM_EPS) + ADAM_WD * w_ref[...])

    spec = pl.BlockSpec((tr, c), lambda i: (i, 0))
    gspec = pl.BlockSpec((NDEV, tr, c), lambda i: (0, i, 0)) if parts else spec
    return pl.pallas_call(
        body, grid=(r // tr,), in_specs=[gspec, spec, spec, spec], out_specs=[spec] * 4,
        out_shape=[SDS((r, c), F32)] * 4, compiler_params=_cp("parallel"), name=name)(g, w, m, v)


def _adamw_layer(parts, w, m, v, l, prev, name):
    depth, r, c = w.shape
    tr = next(t for t in (512, 256, 128, 64, 32, 16, 8) if r % t == 0 and t * c * 4 <= 2 * 1024 * 1024)
    bc1 = 1.0 / (1.0 - ADAM_B1 ** ADAM_STEP)
    bc2 = 1.0 / (1.0 - ADAM_B2 ** ADAM_STEP)

    def body(g_ref, w_ref, m_ref, v_ref, *rest):
        go_ref, d_ref, mo_ref, vo_ref = rest[-4:]
        gv = g_ref[0].astype(F32)
        for q in range(1, NDEV):
            gv = gv + g_ref[q].astype(F32)
        mn = ADAM_B1 * m_ref[0] + (1.0 - ADAM_B1) * gv
        vn = ADAM_B2 * v_ref[0] + (1.0 - ADAM_B2) * (gv * gv)
        go_ref[0] = gv
        mo_ref[0] = mn
        vo_ref[0] = vn
        d_ref[0] = -ADAM_LR * ((mn * bc1) / (jnp.sqrt(vn * bc2) + ADAM_EPS) + ADAM_WD * w_ref[0])

    spec = pl.BlockSpec((1, tr, c), lambda i: (l, i, 0))
    in_specs = [pl.BlockSpec((NDEV, tr, c), lambda i: (0, i, 0)), spec, spec, spec]
    args = [parts, w, m, v]
    aliases = {}
    if prev is not None:
        in_specs += [pl.BlockSpec(memory_space=pl.ANY)] * 4
        args += list(prev)
        aliases = {4 + k: k for k in range(4)}
    return pl.pallas_call(
        body, grid=(r // tr,), in_specs=in_specs, out_specs=[spec] * 4, out_shape=[SDS((depth, r, c), F32)] * 4,
        input_output_aliases=aliases, compiler_params=_cp("parallel"), name=name)(*args)


def _pad_in_cols(w):
    r = w.shape[0]
    zeros = lambda n: jnp.zeros((r, n), w.dtype)
    return jnp.concatenate([w[:, 512:1536], w[:, 0:512], w[:, 1792:2304], w[:, 1536:1792], w[:, 2308:2692], w[:, 2304:2308],
                            zeros(28), w[:, 2692:2724], zeros(64)], axis=1)


def _unpad_in_cols(w):
    return jnp.concatenate([w[..., 1024:1536], w[..., 0:1024], w[..., 2048:2304], w[..., 1536:2048], w[..., 2688:2692],
                            w[..., 2304:2688], w[..., 2720:2752]], axis=-1)


_Z_RET = (0, 1, 2, 3)
_Z_SGU = (4, 5)
_Z_FOX_Q, _Z_FOX_K, _Z_FOX_V = 8, 6, 7
_DZ_RET, _DZ_SGU, _DZ_FOX_KV, _DZ_FOX_Q, _DZ_MLA, _DZ_MISC = 0, 2, 3, 8, 6, 21


def _pad_uq(w):
    return jnp.pad(w.reshape(256, N_HEADS, 96), ((0, 0), (0, 0), (0, 32))).reshape(256, 512)


def _unpad_uq(w):
    return w.reshape(256, N_HEADS, 128)[:, :, :96].reshape(256, 384)


def _split_ukv(w):
    r = w.reshape(128, N_HEADS, 128)
    return jnp.pad(r[:, :, :64], ((0, 0), (0, 0), (0, 64))).reshape(128, 512), r[:, :, 64:].reshape(128, 256)


def _join_ukv(dk, dv):
    return jnp.concatenate([dk.reshape(128, N_HEADS, 128)[:, :, :64], dv.reshape(128, N_HEADS, 64)], axis=-1).reshape(128, 512)


def _cols_to_full(g):
    return jnp.transpose(g, (1, 0, 2)).reshape(g.shape[1], NDEV * g.shape[2])


def kernel(x, g_mix_norm, w_in, b_forget, g_sgu, w_spatial, b_spatial, g_mla_q, w_uq, g_mla_kv, w_ukv, g_group_out, w_out, g_ffn_norm, w_up, w_down, g_final, loss_target, m_g_mix_norm, m_w_in, m_b_forget, m_g_sgu, m_w_spatial, m_b_spatial, m_g_mla_q, m_w_uq, m_g_mla_kv, m_w_ukv, m_g_group_out, m_w_out, m_g_ffn_norm, m_w_up, m_w_down, m_g_final, v_g_mix_norm, v_w_in, v_b_forget, v_g_sgu, v_w_spatial, v_b_spatial, v_g_mla_q, v_w_uq, v_g_mla_kv, v_w_ukv, v_g_group_out, v_w_out, v_g_ffn_norm, v_w_up, v_w_down, v_g_final):
    depth = w_in.shape[0]
    s, d = x.shape[1], x.shape[2]
    x0 = x.reshape(s, d)
    target = loss_target.reshape(s, d)
    tb = _tables(s)
    me = 4 * lax.axis_index("x") + 2 * lax.axis_index("y") + lax.axis_index("c")

    assert depth == 2
    shards = {}
    for l in range(depth):
        shards.update({(l, "w_in"): _pad_in_cols(w_in[l]).astype(_WIRE), (l, "w_out"): w_out[l].astype(_WIRE),
                       (l, "w_up"): w_up[l].astype(_WIRE), (l, "w_down"): w_down[l].astype(_WIRE),
                       (l, "w_uq"): w_uq[l].astype(_WIRE), (l, "w_ukv"): w_ukv[l].astype(_WIRE)})
    wts = _ShardedWeights(shards)
    first = [(0, "w_in"), (0, "w_uq"), (0, "w_ukv"), (1, "w_uq"), (1, "w_ukv")]
    wts.full.update(zip(first, _all_gather([shards[k] for k in first], "gather_first")))

    row = lambda a: a.reshape(1, -1)

    def small(l):
        bf = jnp.pad(b_forget[l].reshape(1, N_HEADS), ((0, 0), (0, 128 - N_HEADS)))
        bt = jnp.pad(b_spatial[l].T, ((0, 0), (0, 128 - N_HEADS)))
        return dict(g_mix=row(g_mix_norm[l]), g_sgu=row(g_sgu[l]), w_s=w_spatial[l], b_t=bt, b_f=bf, gq=row(g_mla_q[l]),
                    gkv=row(g_mla_kv[l]), g_go=row(g_group_out[l]), g_ffn=row(g_ffn_norm[l]))

    smalls = [small(l) for l in range(depth)]
    lrow, dx, sm, dg_final = _local_step(x0, target, wts, smalls, row(g_final), tb)
    loss = lax.psum(jnp.sum(lrow), AXES)
    grad_x = dx.reshape(1, s, d)
    return _reduce_and_update(loss, grad_x, wts.recv, sm, dg_final, me, dict(
        g_mix_norm=(g_mix_norm, m_g_mix_norm, v_g_mix_norm), w_in=(w_in, m_w_in, v_w_in),
        b_forget=(b_forget, m_b_forget, v_b_forget), g_sgu=(g_sgu, m_g_sgu, v_g_sgu),
        w_spatial=(w_spatial, m_w_spatial, v_w_spatial), b_spatial=(b_spatial, m_b_spatial, v_b_spatial),
        g_mla_q=(g_mla_q, m_g_mla_q, v_g_mla_q), w_uq=(w_uq, m_w_uq, v_w_uq), g_mla_kv=(g_mla_kv, m_g_mla_kv, v_g_mla_kv),
        w_ukv=(w_ukv, m_w_ukv, v_w_ukv), g_group_out=(g_group_out, m_g_group_out, v_g_group_out),
        w_out=(w_out, m_w_out, v_w_out), g_ffn_norm=(g_ffn_norm, m_g_ffn_norm, v_g_ffn_norm), w_up=(w_up, m_w_up, v_w_up),
        w_down=(w_down, m_w_down, v_w_down), g_final=(g_final, m_g_final, v_g_final)))


_GATHER_AT = {
    "in_proj0": [(0, "w_up")],
    "fox_attn0": [(0, "w_down")],
    "mla_attn0": [(0, "w_out"), (1, "w_in")],
    "ffn_fwd0": [(1, "w_down")],
    "fox_attn1": [(1, "w_out")],
    "mla_attn1": [(1, "w_up")],
}
_SCATTER_AT = {
    "fox_attn_bwd1": [(1, "w_down")],
    "mla_attn_bwd1": [(1, "w_up"), (1, "w_out")],
    "ffn_bwd0": [(1, "w_in")],
    "fox_attn_bwd0": [(0, "w_down")],
    "mla_attn_bwd0": [(0, "w_up"), (0, "w_out")],
    "in_proj_bwd0": [(0, "w_in")],
}


class _FullWeights:
    def __init__(self, per_layer):
        self.per_layer, self.grads = per_layer, {}

    def get(self, l, name):
        return self.per_layer[l][name]

    def comm(self, host):
        return None

    def done(self, host, results):
        pass

    def grad(self, l, name, blocks):
        self.grads[(l, name)] = blocks


class _ShardedWeights(_FullWeights):
    def __init__(self, shards):
        self.shards, self.full, self.grads, self.recv = shards, {}, {}, {}

    def get(self, l, name):
        if name in ("wk", "wv"):
            return _split_ukv(_cols_to_full(self.full[(l, "w_ukv")]))[0 if name == "wk" else 1]
        if name == "wq":
            return _pad_uq(_cols_to_full(self.full[(l, "w_uq")]))
        g = self.full[(l, name)]
        return g if name == "w_up" else g.reshape(NDEV * g.shape[1], g.shape[2])

    def comm(self, host):
        if host in _GATHER_AT:
            return _Comm("gather", [self.shards[k] for k in _GATHER_AT[host]])
        if host in _SCATTER_AT:
            return _Comm("exchange", [self.grads[k] for k in _SCATTER_AT[host]])
        return None

    def done(self, host, results):
        if host in _GATHER_AT:
            self.full.update(zip(_GATHER_AT[host], results))
        if host in _SCATTER_AT:
            self.recv.update(zip(_SCATTER_AT[host], results))


def _local_step(x0, target, wts, smalls, g_final, tb):
    depth = len(smalls)
    s, d = x0.shape
    saved = []
    xl = x0
    for l in range(depth):
        p = smalls[l]
        z, h, got = _norm_matmul(xl, p["g_mix"], wts.get(l, "w_in"), f"in_proj{l}", wts.comm(f"in_proj{l}"))
        wts.done(f"in_proj{l}", got)
        ya = _sgu_fwd(z, p["g_sgu"], p["w_s"], p["b_t"], tb, f"sgu_fwd{l}")
        yb, ret, states = _ret_fwd(z, tb, f"ret_fwd{l}")
        cum = _fox_prep(z, p["b_f"], f"fox_prep{l}")
        kc, vc, vtc, qtc = _kv_prep(z, _Z_FOX_Q, _Z_FOX_K, _Z_FOX_V, HEAD_DIM ** -0.5, f"fox_kv{l}")
        yc, lse_c, got = _attn_fwd(z, _Z_FOX_Q, HEAD_DIM, kc, vtc, HEAD_DIM ** -0.5, cum, f"fox_attn{l}", wts.comm(f"fox_attn{l}"))
        wts.done(f"fox_attn{l}", got)
        wq, wk, wv = wts.get(l, "wq"), wts.get(l, "wk"), wts.get(l, "wv")
        qd, kd, vd, vtd, cqn, ckvn, qtd = _mla_prep(z, p["gq"], p["gkv"], wq, wk, wv, tb, f"mla_prep{l}")
        yd, lse_d, got = _attn_fwd(qd, 0, 128, kd, vtd, _SCALE_D, None, f"mla_attn{l}", wts.comm(f"mla_attn{l}"))
        wts.done(f"mla_attn{l}", got)
        ys = (ya, yb, yc, yd)
        x1, yn = _out_proj(ys, p["g_go"], wts.get(l, "w_out"), xl, f"out_proj{l}")
        x2, u, h2, got = _ffn_fwd(x1, p["g_ffn"], wts.get(l, "w_up"), wts.get(l, "w_down"), f"ffn_fwd{l}", wts.comm(f"ffn_fwd{l}"))
        wts.done(f"ffn_fwd{l}", got)
        saved.append(dict(x=xl, z=z, h=h, ys=ys, ret=ret, states=states, cum=cum, lse_c=lse_c, kc=kc, vc=vc, qd=qd, kd=kd, vd=vd,
                          cqn=cqn, ckvn=ckvn, lse_d=lse_d, x1=x1, yn=yn, u=u, h2=h2, wq=wq, wk=wk, wv=wv, qtc=qtc, qtd=qtd))
        xl = x2

    lrow, dx, dg_final = _loss_head(xl, g_final, target, "loss_head")

    sm = [None] * depth
    for l in reversed(range(depth)):
        p, a = smalls[l], saved[l]
        dx1, du, dg_ffn, got = _ffn_bwd(dx, a["x1"], a["u"], p["g_ffn"], wts.get(l, "w_up"), wts.get(l, "w_down"), f"ffn_bwd{l}",
                                        wts.comm(f"ffn_bwd{l}"))
        wts.done(f"ffn_bwd{l}", got)
        dw_down = _mm_tn(a["u"], dx, f"dw_down{l}", a_fn=lambda t: jnp.square(jnp.maximum(t, 0.0)), out_dtype=_WIRE)
        wts.grad(l, "w_down", dw_down.reshape(NDEV, dw_down.shape[0] // NDEV, d))
        wts.grad(l, "w_up", _mm_tn(a["h2"], du, f"dw_up{l}", blocked=True, out_dtype=_WIRE))
        dya, dyb, dg_go, dot_c, dot_d, dl_c, dl_d = _out_proj_bwd(dx1, wts.get(l, "w_out"), a["ys"], p["g_go"],
                                                                  f"out_proj_bwd{l}")
        wts.grad(l, "w_out", _mm_tn(a["yn"], dx1, f"dw_out{l}", out_dtype=_WIRE).reshape(NDEV, d // NDEV, d))
        dz = _ret_bwd(dyb, a["z"], a["ret"], a["states"], tb, f"ret_bwd{l}")
        dz, dg_sgu, dw_s, db_t = _sgu_bwd(dya, a["z"], p["g_sgu"], p["w_s"], p["b_t"], tb, dz, f"sgu_bwd{l}")
        dqt_c, dz, dck, dcq, got = _attn_bwd(a["kc"], a["vc"], a["qtc"], dot_c, a["lse_c"], dl_c, HEAD_DIM,
                                             HEAD_DIM ** -0.5, a["cum"], f"fox_attn_bwd{l}", _MXU,
                                             wts.comm(f"fox_attn_bwd{l}"), kv_into=dz)
        wts.done(f"fox_attn_bwd{l}", got)
        dz = _untranspose(dqt_c, _MXU, f"fox_dq{l}", into=dz, col=_DZ_FOX_Q)
        dqt_d, dk_d, dv_d, got = _attn_bwd(a["kd"], a["vd"], a["qtd"], dot_d, a["lse_d"], dl_d, 128, _SCALE_D, None,
                                           f"mla_attn_bwd{l}", F32, wts.comm(f"mla_attn_bwd{l}"))
        wts.done(f"mla_attn_bwd{l}", got)
        dz, dkr, dwq, dwk, dwv, dgq, dgkv = _mla_prep_bwd(dqt_d, dk_d, dv_d, a["z"], a["cqn"], a["ckvn"], p["gq"], p["gkv"],
                                                          a["wq"], a["wk"], a["wv"], tb, dz, f"mla_prep_bwd{l}")
        dz, db_f = _fox_post(dcq, dck, a["z"], p["b_f"], dkr, dz, f"fox_post{l}")
        wts.grad(l, "w_in", _unpad_in_cols(_mm_tn(a["h"], dz, f"dw_in{l}", out_dtype=_WIRE)).reshape(NDEV, d // NDEV, N_IN))
        dx, dg_mix, got = _in_proj_bwd(dz, wts.get(l, "w_in"), a["x"], p["g_mix"], dx1, f"in_proj_bwd{l}",
                                       wts.comm(f"in_proj_bwd{l}"))
        wts.done(f"in_proj_bwd{l}", got)
        sm[l] = [dg_mix, dg_go, dg_ffn, dg_sgu, dw_s, db_t[:, :N_HEADS].T, db_f[0, :N_HEADS], dgq, dgkv, _unpad_uq(dwq),
                 _join_ukv(dwk, dwv)]
    return lrow, dx, sm, dg_final


def _reduce_and_update(loss, grad_x, recv, sm, dg_final, me, given):
    depth = len(sm)
    pieces = [t for l in range(depth) for t in sm[l]] + [dg_final]
    flat = jnp.concatenate([t.reshape(-1) for t in pieces])
    n_flat = flat.shape[0]
    unit = NDEV * 8 * 128
    n_pad = -(-n_flat // unit) * unit
    packed = jnp.pad(flat, (0, n_pad - n_flat)).reshape(NDEV, n_pad // (NDEV * 128), 128)
    red = _sum_slots(_exchange([packed], "scatter_small")[0], "sum_small")
    full = _all_gather([red], "gather_small")[0].reshape(-1)
    offs = np.cumsum([0] + [int(np.prod(t.shape)) for t in pieces])
    red_pieces = [full[int(offs[i]):int(offs[i + 1])].reshape(pieces[i].shape) for i in range(len(pieces))]
    per = len(sm[0])
    stack = lambda i: jnp.stack([red_pieces[l * per + i] for l in range(depth)])
    g_small = dict(g_mix_norm=stack(0), g_group_out=stack(1), g_ffn_norm=stack(2), g_sgu=stack(3), w_spatial=stack(4),
                   b_spatial=stack(5), b_forget=stack(6), g_mla_q=stack(7), g_mla_kv=stack(8), g_final=red_pieces[-1])
    cq, ckv = given["w_uq"][0].shape[2], given["w_ukv"][0].shape[2]
    g_small["w_uq"] = lax.dynamic_slice_in_dim(stack(9), me * cq, cq, axis=2)
    g_small["w_ukv"] = lax.dynamic_slice_in_dim(stack(10), me * ckv, ckv, axis=2)

    names = list(given)
    outs = {}
    for nme in names:
        wv_, mv_, vv_ = given[nme]
        shape = wv_.shape
        if nme in ("w_in", "w_out", "w_up", "w_down"):
            res = None
            for l in range(depth):
                res = _adamw_layer(recv[(l, nme)], wv_, mv_, vv_, l, res, f"adamw_{nme}{l}")
            outs[nme] = list(res)
        else:
            two = lambda t: t.reshape(-1, shape[-1]) if t.ndim > 1 else t.reshape(1, -1)
            res = _adamw(two(g_small[nme]), two(wv_), two(mv_), two(vv_), f"adamw_{nme}")
            outs[nme] = [r.reshape(shape) for r in res]
    return (loss, grad_x, *[outs[n][0] for n in names], *[outs[n][1] for n in names], *[outs[n][2] for n in names],
            *[outs[n][3] for n in names])
```

```python
import functools

import jax
import jax.numpy as jnp
import numpy as np
from jax import lax
from jax.experimental import pallas as pl
from jax.experimental.pallas import tpu as pltpu

F32 = jnp.float32
_MXU = jnp.bfloat16
_WIRE = jnp.bfloat16
EPS = 1e-6
NDEV = 8
AXES = ("x", "y", "c")
MESH = pl.DeviceIdType.MESH

N_HEADS = 4
HEAD_DIM = 64
GROUP = 256
CHUNK = 128
NZ = 2816
N_IN = 2724
MISC_F, MISC_KR = 0, 32
VMEM_LIMIT = 56 * 1024 * 1024

ADAM_LR, ADAM_B1, ADAM_B2, ADAM_EPS, ADAM_WD, ADAM_STEP = 0.001, 0.9, 0.999, 1e-08, 0.01, 10

SDS = jax.ShapeDtypeStruct


def _cp(*sem):
    return pltpu.CompilerParams(dimension_semantics=sem, vmem_limit_bytes=VMEM_LIMIT)


def _dot(a, b):
    return jnp.dot(a.astype(_MXU), b.astype(_MXU), preferred_element_type=F32)


def _dot_nt(a, b):
    return lax.dot_general(a.astype(_MXU), b.astype(_MXU), (((1,), (1,)), ((), ())), preferred_element_type=F32)


def _dot_tn(a, b):
    return lax.dot_general(a.astype(_MXU), b.astype(_MXU), (((0,), (0,)), ((), ())), preferred_element_type=F32)


def _dot_exact(a, b, dims=(((1,), (0,)), ((), ()))):
    return lax.dot_general(a, b, dims, precision=lax.Precision.HIGHEST, preferred_element_type=F32)


def _rms(x, g):
    return x * lax.rsqrt(jnp.mean(x * x, axis=-1, keepdims=True) + EPS) * g


def _rms_bwd(x, g, dy):
    xh = x * lax.rsqrt(jnp.mean(x * x, axis=-1, keepdims=True) + EPS)
    dxh = dy * g
    r = lax.rsqrt(jnp.mean(x * x, axis=-1, keepdims=True) + EPS)
    dx = r * (dxh - xh * jnp.mean(dxh * xh, axis=-1, keepdims=True))
    return dx, jnp.sum(dy * xh, axis=0, keepdims=True)


_GELU_C = 0.7978845608028654


def _gelu(x):
    return 0.5 * x * (1.0 + jnp.tanh(_GELU_C * (x + 0.044715 * x * x * x)))


def _gelu_grad(x):
    t = jnp.tanh(_GELU_C * (x + 0.044715 * x * x * x))
    return 0.5 * (1.0 + t) + 0.5 * x * (1.0 - t * t) * _GELU_C * (1.0 + 3 * 0.044715 * x * x)


def _sigmoid(x):
    return 1.0 / (1.0 + jnp.exp(-x))


def _swap_half(t, half):
    n = t.shape[-1]
    lane = lax.broadcasted_iota(jnp.int32, t.shape, t.ndim - 1)
    return jnp.where((lane % (2 * half)) < half, pltpu.roll(t, n - half, t.ndim - 1), pltpu.roll(t, half, t.ndim - 1))


def _lanes(table, width):
    return jnp.concatenate([table] * (width // table.shape[-1]), axis=-1)


def _rope(t, cos, sin, half):
    return t * cos + _swap_half(t, half) * sin


def _rope_bwd(d, cos, sin, half):
    return d * cos - _swap_half(d, half) * sin


def _tables(s):
    pos = jnp.arange(s, dtype=F32)[:, None]

    def cs(half):
        inv = jnp.power(10000.0, -jnp.arange(half, dtype=F32) / half)
        ang = pos * inv[None, :]
        return jnp.cos(ang), jnp.sin(ang)

    c32, s32 = cs(32)
    c16, s16 = cs(16)
    z = lambda w: jnp.zeros((s, w), F32)
    o = lambda w: jnp.ones((s, w), F32)
    t = {}
    t["b_cos"] = jnp.concatenate([c32, c32, c32, c32], 1)
    t["b_sin"] = jnp.concatenate([-s32, s32, -s32, s32], 1)
    t["q_cos"] = jnp.concatenate([o(64), c16, c16, z(32)], 1)
    t["q_sin"] = jnp.concatenate([z(64), -s16, s16, z(32)], 1)
    t["k_cos"] = jnp.concatenate([z(32), c16, c16, z(64)], 1)
    t["k_sin"] = jnp.concatenate([z(32), -s16, s16, z(64)], 1)
    lg = jnp.log1p(-jnp.exp2(-5.0 - jnp.arange(N_HEADS, dtype=F32)))
    j = jnp.arange(CHUNK, dtype=F32)
    rel = j[:, None] - j[None, :]
    t["decay"] = jnp.where(rel[None] >= 0, jnp.exp(jnp.maximum(rel, 0.0)[None] * lg[:, None, None]), 0.0)
    t["decay_t"] = jnp.swapaxes(t["decay"], 1, 2)

    def rows(e):
        return jnp.repeat(e.T, HEAD_DIM, axis=1)

    t["qw"] = rows(jnp.exp((j + 1.0)[None, :] * lg[:, None]))
    t["kw"] = rows(jnp.exp((CHUNK - 1 - j)[None, :] * lg[:, None]))
    t["kw2"] = rows(jnp.exp((CHUNK - j)[None, :] * lg[:, None]))
    t["qw0"] = rows(jnp.exp(j[None, :] * lg[:, None]))
    t["cd"] = jnp.repeat(jnp.exp(CHUNK * lg), HEAD_DIM)[None, :]
    e = np.zeros((128, 512), np.float32)
    for h in range(N_HEADS):
        for r in range(32):
            e[MISC_KR + r, 128 * h + 64 + r] = 1.0
    t["place"] = jnp.asarray(e)
    lane_head = np.arange(GROUP) // HEAD_DIM
    t["grp"] = jnp.asarray((lane_head[:, None] == lane_head[None, :]) / HEAD_DIM, _MXU)
    hsel = (np.arange(128)[:, None] == lane_head[None, :]).astype(np.float32)
    t["hsel"] = jnp.asarray(hsel)
    t["hselt"] = jnp.asarray(hsel.T, _MXU)
    return t


def _norm_matmul(x, g, w, name, comm=None):
    s, d = x.shape
    n = w.shape[1]
    tm, tn = min(512, s), 256
    ni = s // tm

    def body(*refs):
        (x_ref, g_ref, w_ref), (z_ref, h_ref), _, cc = _split_refs(refs, 3, 2, comm)
        i = pl.program_id(0)
        _host_gather(comm, cc, i, ni, late=True)
        h = _rms(x_ref[...], g_ref[...]).astype(h_ref.dtype)
        h_ref[...] = h
        for j in range(n // tn):
            z_ref[:, tn * j:tn * (j + 1)] = jnp.dot(h, w_ref[:, tn * j:tn * (j + 1)], preferred_element_type=F32)
        if comm is not None:
            @pl.when(i == ni - 1)
            def _():
                comm.wait(*cc)

    in_specs = [pl.BlockSpec((tm, d), lambda i: (i, 0)), pl.BlockSpec((1, d), lambda i: (0, 0)),
                pl.BlockSpec((d, n), lambda i: (0, 0))]
    out_specs = [pl.BlockSpec((tm, n), lambda i: (i, 0)), pl.BlockSpec((tm, d), lambda i: (i, 0))]
    out_shape = [SDS((s, n), F32), SDS((s, d), _MXU)]
    return _call_with_comm(body, (ni,), in_specs, out_specs, out_shape, [], [x, g, w], comm, ("arbitrary",), name)


def _mm_tn(a, b, name, *, a_fn=None, blocked=False, out_dtype=F32):
    k, m = a.shape
    n = b.shape[1]
    tm, tk = min(1024, m), min(1024, k)
    tn = next(t for t in (1408, 1024, 512, 256, 128) if n % t == 0)
    assert m % tm == 0 and k % tk == 0
    nk = k // tk

    def body(a_ref, b_ref, o_ref, acc):
        kk = pl.program_id(2)

        @pl.when(kk == 0)
        def _():
            acc[...] = jnp.zeros_like(acc)

        av = a_ref[...]
        if a_fn is not None:
            av = a_fn(av.astype(F32))
        acc[...] += _dot_tn(av, b_ref[...])

        @pl.when(kk == nk - 1)
        def _():
            if blocked:
                for c in range(tn // 512):
                    o_ref[c] = acc[:, 512 * c:512 * (c + 1)].astype(o_ref.dtype)
            else:
                o_ref[...] = acc[...].astype(o_ref.dtype)

    if blocked:
        assert tn % 512 == 0
        out_spec = pl.BlockSpec((tn // 512, tm, 512), lambda i, j, kk: (j, i, 0))
        out_shape = SDS((n // 512, m, 512), out_dtype)
    else:
        out_spec = pl.BlockSpec((tm, tn), lambda i, j, kk: (i, j))
        out_shape = SDS((m, n), out_dtype)
    return pl.pallas_call(
        body, grid=(m // tm, n // tn, nk),
        in_specs=[pl.BlockSpec((tk, tm), lambda i, j, kk: (kk, i)), pl.BlockSpec((tk, tn), lambda i, j, kk: (kk, j))],
        out_specs=out_spec, out_shape=out_shape, scratch_shapes=[pltpu.VMEM((tm, tn), F32)],
        compiler_params=_cp("parallel", "parallel", "arbitrary"), name=name)(a, b)


def _split_dot(x, m):
    hi = x.astype(_MXU)
    lo = (x - hi.astype(F32)).astype(_MXU)
    return jnp.dot(hi, m, preferred_element_type=F32) + jnp.dot(lo, m, preferred_element_type=F32)


def _gstandardize(t, grp):
    tc = t - _split_dot(t, grp)
    rs = lax.rsqrt(_split_dot(tc * tc, grp) + EPS)
    return tc * rs, rs


def _gstandardize_bwd(yh, rs, dy, grp):
    return rs * (dy - _split_dot(dy, grp) - yh * _split_dot(dy * yh, grp))


def _head_select(parts):
    hid = lax.broadcasted_iota(jnp.int32, parts[0].shape, 1) // HEAD_DIM
    return jnp.where(hid == 0, parts[0], jnp.where(hid == 1, parts[1], jnp.where(hid == 2, parts[2], parts[3])))


def _head_masked(x):
    hid = lax.broadcasted_iota(jnp.int32, x.shape, 1) // HEAD_DIM
    return [jnp.where(hid == h, x, jnp.zeros_like(x)) for h in range(N_HEADS)]


def _tril(w):
    r = lax.broadcasted_iota(jnp.int32, w.shape, 0)
    c = lax.broadcasted_iota(jnp.int32, w.shape, 1)
    return jnp.where(r >= c, w, 0.0)


def _sgu_mixed(vgb, wcs, bias, nchunk):
    ms = [[jnp.dot(wcs[h], vgb[CHUNK * c:CHUNK * (c + 1)], preferred_element_type=F32) for h in range(N_HEADS)]
          for c in range(nchunk)]
    return [_head_select(ms[c]) + bias for c in range(nchunk)]


def _sgu_fwd(z, gain, w_s, b_t, tb, name):
    s = z.shape[0]
    tm = min(512, s)
    const = lambda a: pl.BlockSpec(a.shape, lambda i: (0,) * a.ndim)

    def body(u_ref, v_ref, g_ref, w_ref, b_ref, grp_ref, hsel_ref, y_ref):
        u = _gelu(u_ref[...])
        vh, _ = _gstandardize(_gelu(v_ref[...]), grp_ref[...])
        vgb = (vh * g_ref[...]).astype(_MXU)
        bias = _dot_exact(b_ref[...], hsel_ref[...])
        wcs = [_tril(w_ref[h]).astype(_MXU) for h in range(N_HEADS)]
        for c, mixed in enumerate(_sgu_mixed(vgb, wcs, bias, tm // CHUNK)):
            r = slice(CHUNK * c, CHUNK * (c + 1))
            y_ref[r, :] = u[r] * mixed

    return pl.pallas_call(
        body, grid=(s // tm,),
        in_specs=[pl.BlockSpec((tm, GROUP), lambda i: (i, _Z_SGU[0])), pl.BlockSpec((tm, GROUP), lambda i: (i, _Z_SGU[1])),
                  pl.BlockSpec((1, GROUP), lambda i: (0, 0)), pl.BlockSpec((N_HEADS, CHUNK, CHUNK), lambda i: (0, 0, 0)),
                  pl.BlockSpec((CHUNK, 128), lambda i: (0, 0)), const(tb["grp"]), const(tb["hsel"])],
        out_specs=pl.BlockSpec((tm, GROUP), lambda i: (i, 0)), out_shape=SDS((s, GROUP), F32),
        compiler_params=_cp("parallel"), name=name)(z, z, gain, w_s, b_t, tb["grp"], tb["hsel"])


def _sgu_bwd(dy, z, gain, w_s, b_t, tb, dz, name):
    s = z.shape[0]
    tm = min(512, s)
    nchunk = tm // CHUNK
    const = lambda a: pl.BlockSpec(a.shape, lambda i: (0,) * a.ndim)

    def body(dy_ref, u_ref, v_ref, g_ref, w_ref, b_ref, grp_ref, hsel_ref, hselt_ref, _, dz_ref, dg_ref, dw_ref, db_ref):
        @pl.when(pl.program_id(0) == 0)
        def _():
            dg_ref[...] = jnp.zeros_like(dg_ref)
            dw_ref[...] = jnp.zeros_like(dw_ref)
            db_ref[...] = jnp.zeros_like(db_ref)

        grp = grp_ref[...]
        u_pre, v_pre, gain_v = u_ref[...], v_ref[...], g_ref[...]
        u = _gelu(u_pre)
        vh, rs = _gstandardize(_gelu(v_pre), grp)
        vgb = (vh * gain_v).astype(_MXU)
        dyv = dy_ref[...]
        bias = _dot_exact(b_ref[...], hsel_ref[...])
        wfs = [_tril(w_ref[h]) for h in range(N_HEADS)]
        wcs = [w.astype(_MXU) for w in wfs]
        wts = [w.T.astype(_MXU) for w in wfs]
        mixed = _sgu_mixed(vgb, wcs, bias, nchunk)
        gu = _gelu_grad(u_pre)
        dms, dmh = [], []
        for c in range(nchunk):
            r = slice(CHUNK * c, CHUNK * (c + 1))
            dz_ref[r, 0:GROUP] = (dyv[r] * mixed[c] * gu[r]).astype(dz_ref.dtype)
            dm = dyv[r] * u[r]
            dms.append(dm)
            dmh.append([m.astype(_MXU) for m in _head_masked(dm)])
        dws = [sum(lax.dot_general(dmh[c][h], vgb[CHUNK * c:CHUNK * (c + 1)], (((1,), (1,)), ((), ())),
                                   preferred_element_type=F32) for c in range(nchunk)) for h in range(N_HEADS)]
        dvg = jnp.concatenate([sum(jnp.dot(wts[h], dmh[c][h], preferred_element_type=F32) for h in range(N_HEADS))
                               for c in range(nchunk)], axis=0)
        for h in range(N_HEADS):
            dw_ref[h] += _tril(dws[h])
        db_ref[...] += sum(_split_dot(dm, hselt_ref[...]) for dm in dms)
        dg_ref[...] += jnp.sum(dvg * vh, axis=0, keepdims=True)
        dv = _gstandardize_bwd(vh, rs, dvg * gain_v, grp)
        dz_ref[:, GROUP:2 * GROUP] = (dv * _gelu_grad(v_pre)).astype(dz_ref.dtype)

    consts = [tb["grp"], tb["hsel"], tb["hselt"]]
    return pl.pallas_call(
        body, grid=(s // tm,),
        in_specs=[pl.BlockSpec((tm, GROUP), lambda i: (i, 0)),
                  pl.BlockSpec((tm, GROUP), lambda i: (i, _Z_SGU[0])), pl.BlockSpec((tm, GROUP), lambda i: (i, _Z_SGU[1])),
                  pl.BlockSpec((1, GROUP), lambda i: (0, 0)), pl.BlockSpec((N_HEADS, CHUNK, CHUNK), lambda i: (0, 0, 0)),
                  pl.BlockSpec((CHUNK, 128), lambda i: (0, 0))] + [const(a) for a in consts]
        + [pl.BlockSpec(memory_space=pl.ANY)],
        out_specs=[pl.BlockSpec((tm, 2 * GROUP), lambda i: (i, _DZ_SGU)), pl.BlockSpec((1, GROUP), lambda i: (0, 0)),
                   pl.BlockSpec((N_HEADS, CHUNK, CHUNK), lambda i: (0, 0, 0)), pl.BlockSpec((CHUNK, 128), lambda i: (0, 0))],
        out_shape=[SDS(dz.shape, dz.dtype), SDS((1, GROUP), F32), SDS((N_HEADS, CHUNK, CHUNK), F32), SDS((CHUNK, 128), F32)],
        input_output_aliases={9: 0}, compiler_params=_cp("arbitrary"), name=name)(dy, z, z, gain, w_s, b_t, *consts, dz)


_SCALE_B = HEAD_DIM ** -0.5
RET_CHUNKS = 4


def _block_diag(compact):
    full = jnp.concatenate([compact] * N_HEADS, axis=0)
    r = lax.broadcasted_iota(jnp.int32, full.shape, 0) // HEAD_DIM
    c = lax.broadcasted_iota(jnp.int32, full.shape, 1) // HEAD_DIM
    return jnp.where(r == c, full, 0.0)


def _diag_blocks(full):
    c = lax.broadcasted_iota(jnp.int32, (HEAD_DIM, GROUP), 1) // HEAD_DIM
    return sum(jnp.where(c == h, full[HEAD_DIM * h:HEAD_DIM * (h + 1), :], 0.0) for h in range(N_HEADS))


def _ret_fwd(z, tb, name):
    s = z.shape[0]
    nc = s // CHUNK
    per = min(RET_CHUNKS, nc)
    rows = per * CHUNK
    row = lambda col: pl.BlockSpec((rows, GROUP), lambda n, col=col: (n, col))
    const = lambda shape: pl.BlockSpec(shape, lambda n: (0,) * len(shape))

    def body(q_ref, k_ref, v_ref, g_ref, cos_ref, sin_ref, dec_ref, qw_ref, kw_ref, cd_ref, grp_ref, y_ref, o_ref, st_ref, state):
        @pl.when(pl.program_id(0) == 0)
        def _():
            state[...] = jnp.zeros_like(state)

        cos, sin = _lanes(cos_ref[...], GROUP), _lanes(sin_ref[...], GROUP)
        q = _rope(q_ref[...], cos, sin, 32)
        k = _rope(k_ref[...], cos, sin, 32) * _SCALE_B
        v = v_ref[...]
        g = g_ref[...]
        rcs = [slice(CHUNK * c, CHUNK * (c + 1)) for c in range(per)]
        vms = [[t.astype(_MXU) for t in _head_masked(v[r])] for r in rcs]
        scs = [[_dot_nt(t.astype(_MXU), k[r]) for t in _head_masked(q[r])] for r in rcs]
        kvs = [_dot_tn(k[r] * kw_ref[...], v[r]) for r in rcs]
        st = state[...]
        crosses = []
        for c, r in enumerate(rcs):
            st_ref[c] = st
            crosses.append(_dot(q[r] * qw_ref[...], _block_diag(st)))
            st = cd_ref[...] * st + _diag_blocks(kvs[c])
        state[...] = st
        outs = []
        for c in range(per):
            scd = [(scs[c][h] * dec_ref[h]).astype(_MXU) for h in range(N_HEADS)]
            outs.append(crosses[c] + sum(jnp.dot(scd[h], vms[c][h], preferred_element_type=F32) for h in range(N_HEADS)))
        o = jnp.concatenate(outs, axis=0)
        o_ref[...] = o
        yh, _ = _gstandardize(o, grp_ref[...])
        y_ref[...] = g * _sigmoid(g) * yh

    return pl.pallas_call(
        body, grid=(nc // per,),
        in_specs=[row(_Z_RET[0]), row(_Z_RET[1]), row(_Z_RET[2]), row(_Z_RET[3]), pl.BlockSpec((rows, 128), lambda n: (n, 0)),
                  pl.BlockSpec((rows, 128), lambda n: (n, 0)), const((N_HEADS, CHUNK, CHUNK)),
                  const((CHUNK, GROUP)), const((CHUNK, GROUP)), const((1, GROUP)), const((GROUP, GROUP))],
        out_specs=[pl.BlockSpec((rows, GROUP), lambda n: (n, 0)), pl.BlockSpec((rows, GROUP), lambda n: (n, 0)),
                   pl.BlockSpec((per, HEAD_DIM, GROUP), lambda n: (n, 0, 0))],
        out_shape=[SDS((s, GROUP), F32), SDS((s, GROUP), F32), SDS((nc, HEAD_DIM, GROUP), F32)],
        scratch_shapes=[pltpu.VMEM((HEAD_DIM, GROUP), F32)],
        compiler_params=_cp("arbitrary"), name=name)(z, z, z, z, tb["b_cos"], tb["b_sin"], tb["decay"], tb["qw"], tb["kw"], tb["cd"],
                                                       tb["grp"])


def _ret_bwd(dy, z, o_pre, states, tb, name):
    s = z.shape[0]
    nc = s // CHUNK
    per = min(RET_CHUNKS, nc)
    rows = per * CHUNK
    ns = nc // per
    rev = lambda col: pl.BlockSpec((rows, GROUP), lambda n, col=col: (ns - 1 - n, col))
    const = lambda shape: pl.BlockSpec(shape, lambda n: (0,) * len(shape))

    def body(dy_ref, q_ref, k_ref, v_ref, g_ref, o_ref, st_ref, cos_ref, sin_ref, dec_ref, dect_ref, qw_ref, kw2_ref, qw0_ref,
             cd_ref, grp_ref, dz_ref, rstate):
        @pl.when(pl.program_id(0) == 0)
        def _():
            rstate[...] = jnp.zeros_like(rstate)

        cos, sin = _lanes(cos_ref[...], GROUP), _lanes(sin_ref[...], GROUP)
        q = _rope(q_ref[...], cos, sin, 32)
        k = _rope(k_ref[...], cos, sin, 32) * _SCALE_B
        v = v_ref[...]
        g = g_ref[...]
        dyv = dy_ref[...]
        sg = _sigmoid(g)
        yh, rs = _gstandardize(o_ref[...], grp_ref[...])
        dz_ref[:, 3 * GROUP:4 * GROUP] = (dyv * yh * (sg * (1.0 + g * (1.0 - sg)))).astype(dz_ref.dtype)
        do = _gstandardize_bwd(yh, rs, dyv * (g * sg), grp_ref[...])
        hs = range(N_HEADS)
        rcs = [slice(CHUNK * c, CHUNK * (c + 1)) for c in range(per)]
        mask = lambda t: [m.astype(_MXU) for m in _head_masked(t)]
        qms, kms, vms, doms = ([mask(t[r]) for r in rcs] for t in (q, k, v, do))
        dps = [[_dot_nt(doms[c][h], v[r]) for h in hs] for c, r in enumerate(rcs)]
        pts = [[_dot_nt(kms[c][h], q[r]) for h in hs] for c, r in enumerate(rcs)]
        dpts = [[_dot_nt(vms[c][h], do[r]) for h in hs] for c, r in enumerate(rcs)]
        dq_x = [_dot_nt(do[r] * qw_ref[...], _block_diag(st_ref[c])) for c, r in enumerate(rcs)]
        r_new = [_dot_tn(q[r] * qw0_ref[...], do[r]) for r in rcs]
        rr = rstate[...]
        dk_x, dv_x = [None] * per, [None] * per
        for c in reversed(range(per)):
            r_bd = _block_diag(rr)
            dk_x[c] = _dot_nt(v[rcs[c]] * kw2_ref[...], r_bd)
            dv_x[c] = _dot(k[rcs[c]] * kw2_ref[...], r_bd)
            rr = cd_ref[...] * rr + _diag_blocks(r_new[c])
        rstate[...] = rr
        dqs, dks, dvs = [], [], []
        for c in range(per):
            dpd = [(dps[c][h] * dec_ref[h]).astype(_MXU) for h in hs]
            dptd = [(dpts[c][h] * dect_ref[h]).astype(_MXU) for h in hs]
            ptd = [(pts[c][h] * dect_ref[h]).astype(_MXU) for h in hs]
            dqs.append(dq_x[c] + sum(jnp.dot(dpd[h], kms[c][h], preferred_element_type=F32) for h in hs))
            dks.append(dk_x[c] + sum(jnp.dot(dptd[h], qms[c][h], preferred_element_type=F32) for h in hs))
            dvs.append(dv_x[c] + sum(jnp.dot(ptd[h], doms[c][h], preferred_element_type=F32) for h in hs))
        dz_ref[:, 0:GROUP] = _rope_bwd(jnp.concatenate(dqs, axis=0), cos, sin, 32).astype(dz_ref.dtype)
        dz_ref[:, GROUP:2 * GROUP] = _rope_bwd(jnp.concatenate(dks, axis=0) * _SCALE_B, cos, sin, 32).astype(dz_ref.dtype)
        dz_ref[:, 2 * GROUP:3 * GROUP] = jnp.concatenate(dvs, axis=0).astype(dz_ref.dtype)

    r0 = lambda: pl.BlockSpec((rows, GROUP), lambda n: (ns - 1 - n, 0))
    r128 = lambda: pl.BlockSpec((rows, 128), lambda n: (ns - 1 - n, 0))
    return pl.pallas_call(
        body, grid=(ns,),
        in_specs=[r0(), rev(_Z_RET[0]), rev(_Z_RET[1]), rev(_Z_RET[2]), rev(_Z_RET[3]), r0(),
                  pl.BlockSpec((per, HEAD_DIM, GROUP), lambda n: (ns - 1 - n, 0, 0)),
                  r128(), r128(), const((N_HEADS, CHUNK, CHUNK)), const((N_HEADS, CHUNK, CHUNK)), const((CHUNK, GROUP)),
                  const((CHUNK, GROUP)), const((CHUNK, GROUP)), const((1, GROUP)), const((GROUP, GROUP))],
        out_specs=pl.BlockSpec((rows, 4 * GROUP), lambda n: (ns - 1 - n, _DZ_RET)),
        out_shape=SDS((s, NZ), _MXU), scratch_shapes=[pltpu.VMEM((HEAD_DIM, GROUP), F32)],
        compiler_params=_cp("arbitrary"), name=name)(
            dy, z, z, z, z, o_pre, states, tb["b_cos"], tb["b_sin"], tb["decay"], tb["decay_t"], tb["qw"], tb["kw2"], tb["qw0"],
            tb["cd"], tb["grp"])


TQ = 256


def _log_sigmoid(x):
    return jnp.minimum(x, 0.0) - jnp.log1p(jnp.exp(-jnp.abs(x)))


def _fox_prep(z, b_f, name):
    s = z.shape[0]
    nb = s // TQ

    def body(m_ref, b_ref, cc_ref, carry):
        @pl.when(pl.program_id(0) == 0)
        def _():
            carry[...] = jnp.zeros_like(carry)

        lane = lax.broadcasted_iota(jnp.int32, (TQ, 128), 1)
        logf = jnp.where(lane < N_HEADS, _log_sigmoid(m_ref[...] + b_ref[...]), 0.0)
        r = lax.broadcasted_iota(jnp.int32, (TQ, TQ), 0)
        c = lax.broadcasted_iota(jnp.int32, (TQ, TQ), 1)
        tri = jnp.where(r >= c, 1.0, 0.0).astype(F32)
        cum = _dot_exact(tri, logf) + carry[...]
        cc_ref[...] = cum * LOG2E
        carry[...] = cum[TQ - 1:TQ, :]

    return pl.pallas_call(
        body, grid=(nb,),
        in_specs=[pl.BlockSpec((TQ, 128), lambda i: (i, NZ // 128 - 1)), pl.BlockSpec((1, 128), lambda i: (0, 0))],
        out_specs=pl.BlockSpec((TQ, 128), lambda i: (i, 0)),
        out_shape=SDS((s, 128), F32), scratch_shapes=[pltpu.VMEM((1, 128), F32)],
        compiler_params=_cp("arbitrary"), name=name)(z, b_f)


def _fox_post(dcr, dcq, z, b_f, dkr, dz, name):
    s = z.shape[0]
    nb = s // TQ

    def body(dc_ref, dcq_ref, m_ref, b_ref, dkr_ref, _, dz_ref, db_ref, carry):
        @pl.when(pl.program_id(0) == 0)
        def _():
            carry[...] = jnp.zeros_like(carry)
            db_ref[...] = jnp.zeros_like(db_ref)

        r = lax.broadcasted_iota(jnp.int32, (TQ, TQ), 0)
        c = lax.broadcasted_iota(jnp.int32, (TQ, TQ), 1)
        triu = jnp.where(c >= r, 1.0, 0.0).astype(F32)
        dc = jnp.concatenate([dc_ref[0], jnp.zeros((120, TQ), F32)], axis=0)
        dlogf = _dot_exact(triu, dc, (((1,), (1,)), ((), ()))) + _dot_exact(triu, dcq_ref[...]) + carry[...]
        carry[...] = dlogf[0:1, :]
        x = m_ref[...] + b_ref[...]
        lane = lax.broadcasted_iota(jnp.int32, (TQ, 128), 1)
        df = jnp.where(lane < N_HEADS, dlogf * _sigmoid(-x), 0.0)
        db_ref[...] += jnp.sum(df, axis=0, keepdims=True)
        dz_ref[...] = (df + dkr_ref[...]).astype(dz_ref.dtype)

    rv = lambda i: nb - 1 - i
    return pl.pallas_call(
        body, grid=(nb,),
        in_specs=[pl.BlockSpec((1, 8, TQ), lambda i: (rv(i), 0, 0)), pl.BlockSpec((TQ, 128), lambda i: (rv(i), 0)),
                  pl.BlockSpec((TQ, 128), lambda i: (rv(i), NZ // 128 - 1)),
                  pl.BlockSpec((1, 128), lambda i: (0, 0)), pl.BlockSpec((TQ, 128), lambda i: (rv(i), 0)),
                  pl.BlockSpec(memory_space=pl.ANY)],
        out_specs=[pl.BlockSpec((TQ, 128), lambda i: (rv(i), _DZ_MISC)), pl.BlockSpec((1, 128), lambda i: (0, 0))],
        out_shape=[SDS(dz.shape, dz.dtype), SDS((1, 128), F32)], scratch_shapes=[pltpu.VMEM((1, 128), F32)],
        input_output_aliases={5: 0}, compiler_params=_cp("arbitrary"), name=name)(dcr, dcq, z, b_f, dkr, dz)


NEG = -1e30


TKV = 512


def _key_block(s):
    return min(TKV, s)


def _diag_mask(shape, off):
    r = lax.broadcasted_iota(jnp.int32, shape, 0)
    c = lax.broadcasted_iota(jnp.int32, shape, 1)
    return c + off >= r


def _head_lanes(h, dqk):
    return slice(128 * (h // 2), 128 * (h // 2) + 128) if dqk == HEAD_DIM else slice(128 * h, 128 * h + 128)


def _keep_half(x, a, axis):
    idx = lax.broadcasted_iota(jnp.int32, x.shape, axis)
    return jnp.where((idx < HEAD_DIM) if a == 0 else (idx >= HEAD_DIM), x, jnp.zeros_like(x))


def _scaled_qt(q, scale):
    qs = q.astype(F32) * (scale * LOG2E)
    return [qs[TQ * b:TQ * (b + 1)].T.astype(_MXU) for b in range(q.shape[0] // TQ)]


def _kv_prep(z, qcol, kcol, vcol, scale, name):
    s = z.shape[0]
    tk = _key_block(s)
    nk = s // tk

    def body(q_ref, k_ref, v_ref, kb_ref, vb_ref, vt_ref, qt_ref):
        kb_ref[...] = k_ref[...].astype(_MXU)
        v = v_ref[...]
        vb_ref[...] = v.astype(_MXU)
        vt_ref[0] = v.T.astype(_MXU)
        for b, t in enumerate(_scaled_qt(q_ref[...], scale)):
            qt_ref[b] = t

    blk = pl.BlockSpec((tk, GROUP), lambda i: (i, 0))
    col = lambda c: pl.BlockSpec((tk, GROUP), lambda i, c=c: (i, c))
    return pl.pallas_call(
        body, grid=(nk,), in_specs=[col(qcol), col(kcol), col(vcol)],
        out_specs=[blk, blk, pl.BlockSpec((1, GROUP, tk), lambda i: (i, 0, 0)),
                   pl.BlockSpec((tk // TQ, GROUP, TQ), lambda i: (i, 0, 0))],
        out_shape=[SDS((s, GROUP), _MXU), SDS((s, GROUP), _MXU), SDS((nk, GROUP, tk), _MXU), SDS((s // TQ, GROUP, TQ), _MXU)],
        compiler_params=_cp("parallel"), name=name)(z, z, z)


LOG2E = 1.4426950408889634


def _attn_fwd(q, qcol, dqk, kb, vt, scale, ck2, name, comm=None):
    s = q.shape[0]
    nq = s // TQ
    tk = _key_block(s)
    ratio = tk // TQ
    wq = N_HEADS * dqk
    bias = ck2 is not None

    def body(*refs):
        ins, (o_ref, l_ref), _, cc = _split_refs(refs, 4 if bias else 3, 2, comm)
        if bias:
            q_ref, k_ref, vt_ref, cc_ref = ins
        else:
            q_ref, k_ref, vt_ref = ins
        i = pl.program_id(0)
        _host_gather(comm, cc, i, nq)
        qts = []
        for h in range(N_HEADS):
            qt = (q_ref[:, _head_lanes(h, dqk)].astype(F32) * (scale * LOG2E)).T
            qts.append((_keep_half(qt, h % 2, 0) if dqk == HEAD_DIM else qt).astype(_MXU))

        def step(j, carry, off):
            r0 = pl.multiple_of(j * tk, tk)
            vtj = vt_ref[j]
            sts = [jnp.dot(k_ref[pl.ds(r0, tk), _head_lanes(h, dqk)], qts[h], preferred_element_type=F32)
                   for h in range(N_HEADS)]
            stats, ps = [], []
            for h in range(N_HEADS):
                m, l, _ = carry[3 * h:3 * h + 3]
                st = sts[h]
                if bias:
                    st = st - cc_ref[pl.ds(r0, tk), h:h + 1]
                if off is not None:
                    st = jnp.where(_diag_mask(st.shape, off), st, NEG)
                m_new = jnp.maximum(m, jnp.max(st, axis=0, keepdims=True))
                alpha = jnp.exp2(m - m_new)
                p = jnp.exp2(st - m_new)
                stats.append((m_new, alpha * l + jnp.sum(p, axis=0, keepdims=True), alpha))
                ps.append(p.astype(_MXU))
            out = []
            for h in range(N_HEADS):
                m_new, l, alpha = stats[h]
                acc = alpha * carry[3 * h + 2] + jnp.dot(vtj[HEAD_DIM * h:HEAD_DIM * (h + 1), :], ps[h],
                                                         preferred_element_type=F32)
                out += [m_new, l, acc]
            return tuple(out)

        init = (jnp.full((1, TQ), NEG, F32), jnp.zeros((1, TQ), F32), jnp.zeros((HEAD_DIM, TQ), F32)) * N_HEADS
        jd = i // ratio
        carry = lax.fori_loop(0, jd, functools.partial(step, off=None), init)
        carry = step(jd, carry, TQ * (i % ratio))
        l_ref[...] = jnp.zeros_like(l_ref)
        for h in range(N_HEADS):
            l_ref[0, h:h + 1, :] = carry[3 * h] + jnp.log2(carry[3 * h + 1])
        for p in range(2):
            ot = jnp.concatenate([carry[6 * p + 2] / carry[6 * p + 1], carry[6 * p + 5] / carry[6 * p + 4]], axis=0)
            o_ref[:, 128 * p:128 * (p + 1)] = ot.T
        if comm is not None:
            @pl.when(i == nq - 1)
            def _():
                comm.wait(*cc)

    rows = pl.BlockSpec((1, 8, TQ), lambda i: (i, 0, 0))
    in_specs = [pl.BlockSpec((TQ, wq), lambda i: (i, qcol)), pl.BlockSpec((s, wq), lambda i: (0, 0)),
                pl.BlockSpec((s // tk, GROUP, tk), lambda i: (0, 0, 0))]
    args = [q, kb, vt]
    if bias:
        in_specs.append(pl.BlockSpec((s, 128), lambda i: (0, 0)))
        args.append(ck2)
    out_specs = [pl.BlockSpec((TQ, GROUP), lambda i: (i, 0)), rows]
    out_shape = [SDS((s, GROUP), F32), SDS((nq, 8, TQ), F32)]
    return _call_with_comm(body, (nq,), in_specs, out_specs, out_shape, [], args, comm, ("arbitrary",), name)


def _call_with_comm(body, grid, in_specs, out_specs, out_shape, scratch, args, comm, semantics, name, aliases=None):
    n_out = len(out_shape)
    if comm is not None:
        in_specs, out_specs = in_specs + comm.in_specs, out_specs + comm.out_specs
        out_shape, scratch, args = out_shape + comm.out_shape, scratch + comm.scratch, list(args) + comm.arrs
    res = pl.pallas_call(body, grid=grid, in_specs=in_specs, out_specs=out_specs, out_shape=out_shape,
                         scratch_shapes=scratch, input_output_aliases=aliases or {}, compiler_params=_cp(*semantics),
                         name=name)(*args)
    return (*res[:n_out], list(res[n_out:]))


def _attn_bwd(kb, vb, qt, dot, lse, dl, dqk, scale, ck2, name, kv_dtype, comm=None, kv_into=None):
    s = kb.shape[0]
    nq = s // TQ
    tk = _key_block(s)
    ratio = tk // TQ
    nkb = s // tk
    wq = N_HEADS * dqk
    bias = ck2 is not None

    merged = kv_into is not None
    n_in = 6 + bias + merged
    n_out = 3 + 2 * bias - merged

    def body(*refs):
        ins, outs, _, cc = _split_refs(refs, n_in, n_out, comm)
        k_ref, v_ref, qt_ref, dot_ref, l_ref, d_ref = ins[:6]
        cc_ref = ins[6] if bias else None
        dqt_ref = outs[0]
        if merged:
            dk_ref, dv_ref = outs[1].at[:, 0:wq], outs[1].at[:, wq:wq + GROUP]
        else:
            dk_ref, dv_ref = outs[1], outs[2]
        if bias:
            dck_ref, dcq_ref = outs[-2:]
        j = pl.program_id(0)

        @pl.when(j == 0)
        def _():
            if comm is not None:
                comm.start(*cc)
            dqt_ref[...] = jnp.zeros_like(dqt_ref)
            if bias:
                dcq_ref[...] = jnp.zeros_like(dcq_ref)

        ks, kts, vs = [], [], []
        for h in range(N_HEADS):
            k2 = k_ref[:, _head_lanes(h, dqk)]
            if dqk == HEAD_DIM:
                k2 = _keep_half(k2, h % 2, 1)
            ks.append(k2)
            kts.append(k2.astype(F32).T.astype(_MXU))
            vs.append(_keep_half(v_ref[:, _head_lanes(h, HEAD_DIM)], h % 2, 1))
        cks = [cc_ref[:, h:h + 1] for h in range(N_HEADS)] if bias else None

        nt = (((1,), (1,)), ((), ()))

        def step(i, carry, off):
            qti, doti, li, di = qt_ref[i], dot_ref[i], l_ref[i], d_ref[i]
            qls = [_head_lanes(h, dqk) for h in range(N_HEADS)]
            vls = [_head_lanes(h, HEAD_DIM) for h in range(N_HEADS)]
            sts, dpts = [], []
            for h in range(N_HEADS):
                sts.append(jnp.dot(ks[h], qti[qls[h], :], preferred_element_type=F32))
                dpts.append(jnp.dot(vs[h], doti[vls[h], :], preferred_element_type=F32))
            pbs, dsbs, dcks = [], [], []
            for h in range(N_HEADS):
                st = sts[h] - li[h:h + 1, :]
                if bias:
                    st = st - cks[h]
                p = jnp.exp2(st)
                if off is not None:
                    p = jnp.where(_diag_mask(p.shape, off), p, 0.0)
                dst = p * (dpts[h] - di[h:h + 1, :])
                pbs.append(p.astype(_MXU))
                dsbs.append(dst.astype(_MXU))
                if bias:
                    dcks.append(carry[3 * h + 2] + jnp.sum(dst, axis=1, keepdims=True))
                    dcq_ref[i, h:h + 1, :] += jnp.sum(dst, axis=0, keepdims=True)
                else:
                    dcks.append(carry[3 * h + 2])
            out = []
            for h in range(N_HEADS):
                dvt = carry[3 * h + 1] + lax.dot_general(doti[HEAD_DIM * h:HEAD_DIM * (h + 1), :], pbs[h], nt,
                                                         preferred_element_type=F32)
                dkt = carry[3 * h] + lax.dot_general(qti[dqk * h:dqk * (h + 1), :], dsbs[h], nt, preferred_element_type=F32)
                dqt_ref[i, qls[h], :] += jnp.dot(kts[h], dsbs[h], preferred_element_type=F32) * scale
                out += [dkt, dvt, dcks[h]]
            return tuple(out)

        carry = (jnp.zeros((dqk, tk), F32), jnp.zeros((HEAD_DIM, tk), F32), jnp.zeros((tk, 1), F32)) * N_HEADS
        for r in range(ratio):
            carry = step(ratio * j + r, carry, TQ * r)
        carry = lax.fori_loop(ratio * (j + 1), nq, functools.partial(step, off=None), carry)
        for p in range(2):
            dv_ref[:, 128 * p:128 * (p + 1)] = jnp.concatenate([carry[6 * p + 1], carry[6 * p + 4]], axis=0).T.astype(dv_ref.dtype)
            if dqk == HEAD_DIM:
                dk_ref[:, 128 * p:128 * (p + 1)] = (jnp.concatenate([carry[6 * p], carry[6 * p + 3]], axis=0).T
                                                    * (1.0 / LOG2E)).astype(dk_ref.dtype)
        if dqk != HEAD_DIM:
            for h in range(N_HEADS):
                dk_ref[:, 128 * h:128 * (h + 1)] = (carry[3 * h].T * (1.0 / LOG2E)).astype(dk_ref.dtype)
        if bias:
            dck_ref[...] = jnp.zeros_like(dck_ref)
            for h in range(N_HEADS):
                dck_ref[:, h:h + 1] = -carry[3 * h + 2]
        if comm is not None:
            @pl.when(j == nkb - 1)
            def _():
                comm.wait(*cc)

    blk = lambda w: pl.BlockSpec((tk, w), lambda j: (j, 0))
    full3 = lambda w: pl.BlockSpec((nq, w, TQ), lambda j: (0, 0, 0))
    in_specs = [blk(wq), blk(GROUP), full3(wq), full3(GROUP), full3(8), full3(8)]
    args = [kb, vb, qt, dot, lse, dl]
    if merged:
        assert wq == GROUP
        out_specs = [full3(wq), pl.BlockSpec((tk, wq + GROUP), lambda j: (j, _DZ_FOX_KV))]
        out_shape = [SDS((nq, wq, TQ), F32), SDS(kv_into.shape, kv_into.dtype)]
    else:
        out_specs = [full3(wq), blk(wq), blk(GROUP)]
        out_shape = [SDS((nq, wq, TQ), F32), SDS((s, wq), kv_dtype), SDS((s, GROUP), kv_dtype)]
    if bias:
        in_specs.append(blk(128))
        args.append(ck2)
        out_specs += [blk(128), full3(8)]
        out_shape += [SDS((s, 128), F32), SDS((nq, 8, TQ), F32)]
    aliases = {}
    if merged:
        in_specs.append(pl.BlockSpec(memory_space=pl.ANY))
        args.append(kv_into)
        aliases = {len(args) - 1: 1}
    return _call_with_comm(body, (nkb,), in_specs, out_specs, out_shape, [], args, comm, ("arbitrary",), name, aliases)


def _untranspose(xt, dtype, name, into=None, col=0):
    nq, w, _ = xt.shape
    if into is not None:
        def body_into(x_ref, _, o_ref):
            o_ref[...] = x_ref[0].T.astype(o_ref.dtype)

        return pl.pallas_call(
            body_into, grid=(nq,),
            in_specs=[pl.BlockSpec((1, w, TQ), lambda i: (i, 0, 0)), pl.BlockSpec(memory_space=pl.ANY)],
            out_specs=pl.BlockSpec((TQ, w), lambda i: (i, col)), out_shape=SDS(into.shape, into.dtype),
            input_output_aliases={1: 0}, compiler_params=_cp("parallel"), name=name)(xt, into)

    def body(x_ref, o_ref):
        o_ref[...] = x_ref[0].T.astype(o_ref.dtype)

    return pl.pallas_call(
        body, grid=(nq,), in_specs=[pl.BlockSpec((1, w, TQ), lambda i: (i, 0, 0))],
        out_specs=pl.BlockSpec((TQ, w), lambda i: (i, 0)), out_shape=SDS((nq * TQ, w), dtype),
        compiler_params=_cp("parallel"), name=name)(xt)


_SCALE_D = (64 + 32) ** -0.5
_COL_CQ, _COL_CKV, _COL_MISC = 2304 // 256, 2560 // 128, 2688 // 128


def _mla_prep(z, gq, gkv, wq, wk, wv, tb, name):
    s = z.shape[0]
    tm = _key_block(s)
    row = lambda w, c: pl.BlockSpec((tm, w), lambda i, c=c: (i, c))
    const = lambda a: pl.BlockSpec(a.shape, lambda i: (0,) * a.ndim)

    def body(cq_ref, ckv_ref, m_ref, gq_ref, gkv_ref, wq_ref, wk_ref, wv_ref, e_ref, qc_ref, qs_ref, kc_ref, ks_ref,
             q_ref, k_ref, v_ref, vt_ref, cqn_ref, ckvn_ref, qt_ref):
        cqn = _rms(cq_ref[...], gq_ref[...]).astype(_MXU)
        ckvn = _rms(ckv_ref[...], gkv_ref[...]).astype(_MXU)
        cqn_ref[...] = cqn
        ckvn_ref[...] = ckvn
        qb = _rope(_dot(cqn, wq_ref[...]), _lanes(qc_ref[...], 512), _lanes(qs_ref[...], 512), 16).astype(q_ref.dtype)
        q_ref[...] = qb
        for b, t in enumerate(_scaled_qt(qb, _SCALE_D)):
            qt_ref[b] = t
        kr = _rope(m_ref[...], kc_ref[...], ks_ref[...], 16)
        k_ref[...] = (_dot(ckvn, wk_ref[...]) + _dot(kr, e_ref[...])).astype(k_ref.dtype)
        v = _dot(ckvn, wv_ref[...])
        v_ref[...] = v.astype(v_ref.dtype)
        vt_ref[0] = v.T.astype(vt_ref.dtype)

    e = tb["place"]
    return pl.pallas_call(
        body, grid=(s // tm,),
        in_specs=[row(256, _COL_CQ), row(128, _COL_CKV), row(128, _COL_MISC), const(gq), const(gkv), const(wq), const(wk),
                  const(wv), const(e), row(128, 0), row(128, 0), row(128, 0), row(128, 0)],
        out_specs=[row(512, 0), row(512, 0), row(256, 0), pl.BlockSpec((1, GROUP, tm), lambda i: (i, 0, 0)), row(256, 0),
                   row(128, 0), pl.BlockSpec((tm // TQ, 512, TQ), lambda i: (i, 0, 0))],
        out_shape=[SDS((s, 512), _MXU), SDS((s, 512), _MXU), SDS((s, 256), _MXU), SDS((s // tm, GROUP, tm), _MXU),
                   SDS((s, 256), _MXU), SDS((s, 128), _MXU), SDS((s // TQ, 512, TQ), _MXU)],
        compiler_params=_cp("parallel"), name=name)(
            z, z, z, gq, gkv, wq, wk, wv, e, tb["q_cos"], tb["q_sin"], tb["k_cos"], tb["k_sin"])


def _mla_prep_bwd(dqt, dk, dv, z, cqn, ckvn, gq, gkv, wq, wk, wv, tb, dz, name):
    s = z.shape[0]
    tm = min(512, s)
    row = lambda w, c: pl.BlockSpec((tm, w), lambda i, c=c: (i, c))
    const = lambda a: pl.BlockSpec(a.shape, lambda i: (0,) * a.ndim)
    acc = lambda shape: pl.BlockSpec(shape, lambda i: (0, 0))

    def body(dq_ref, dk_ref, dv_ref, cq_ref, ckv_ref, cqn_ref, ckvn_ref, gq_ref, gkv_ref, wq_ref, wk_ref, wv_ref, e_ref,
             qc_ref, qs_ref, kc_ref, ks_ref, _, dz_ref, dkr_ref, dwq_ref, dwk_ref, dwv_ref, dgq_ref, dgkv_ref):
        dcq_ref, dckv_ref = dz_ref.at[:, 0:256], dz_ref.at[:, 256:384]

        @pl.when(pl.program_id(0) == 0)
        def _():
            for r in (dwq_ref, dwk_ref, dwv_ref, dgq_ref, dgkv_ref):
                r[...] = jnp.zeros_like(r)

        dq = jnp.concatenate([dq_ref[b].T for b in range(tm // TQ)], axis=0)
        dqp = _rope_bwd(dq, _lanes(qc_ref[...], 512), _lanes(qs_ref[...], 512), 16)
        dkd = dk_ref[...]
        dvd = dv_ref[...]
        dwq_ref[...] += _dot_tn(cqn_ref[...], dqp)
        dwk_ref[...] += _dot_tn(ckvn_ref[...], dkd)
        dwv_ref[...] += _dot_tn(ckvn_ref[...], dvd)
        dcq, dgq = _rms_bwd(cq_ref[...], gq_ref[...], _dot_nt(dqp, wq_ref[...]))
        dckv, dgkv = _rms_bwd(ckv_ref[...], gkv_ref[...], _dot_nt(dkd, wk_ref[...]) + _dot_nt(dvd, wv_ref[...]))
        dcq_ref[...] = dcq.astype(dcq_ref.dtype)
        dckv_ref[...] = dckv.astype(dckv_ref.dtype)
        dgq_ref[...] += dgq
        dgkv_ref[...] += dgkv
        dkr = _dot_exact(dkd, e_ref[...], (((1,), (1,)), ((), ())))
        dkr_ref[...] = _rope_bwd(dkr, kc_ref[...], ks_ref[...], 16)

    e = tb["place"]
    return pl.pallas_call(
        body, grid=(s // tm,),
        in_specs=[pl.BlockSpec((tm // TQ, 512, TQ), lambda i: (i, 0, 0)), row(512, 0), row(256, 0), row(256, _COL_CQ),
                  row(128, _COL_CKV), row(256, 0), row(128, 0),
                  const(gq), const(gkv), const(wq), const(wk), const(wv), const(e), row(128, 0), row(128, 0), row(128, 0), row(128, 0),
                  pl.BlockSpec(memory_space=pl.ANY)],
        out_specs=[row(384, _DZ_MLA), row(128, 0), acc((256, 512)), acc((128, 512)), acc((128, 256)), acc((1, 256)),
                   acc((1, 128))],
        out_shape=[SDS(dz.shape, dz.dtype), SDS((s, 128), F32), SDS((256, 512), F32), SDS((128, 512), F32),
                   SDS((128, 256), F32), SDS((1, 256), F32), SDS((1, 128), F32)],
        input_output_aliases={17: 0}, compiler_params=_cp("arbitrary"), name=name)(
            dqt, dk, dv, z, z, cqn, ckvn, gq, gkv, wq, wk, wv, e, tb["q_cos"], tb["q_sin"], tb["k_cos"], tb["k_sin"], dz)


def _out_proj(ys, g, w, x, name):
    s, d = x.shape
    tm = min(512, s)

    def body(ya, yb, yc, yd, g_ref, w_ref, x_ref, o_ref, yn_ref):
        acc = x_ref[...]
        for i, y_ref in enumerate((ya, yb, yc, yd)):
            sl = slice(GROUP * i, GROUP * (i + 1))
            yn = _rms(y_ref[...], g_ref[:, sl]).astype(_MXU)
            yn_ref[:, sl] = yn
            acc = acc + jnp.dot(yn, w_ref[sl, :], preferred_element_type=F32)
        o_ref[...] = acc

    yspec = pl.BlockSpec((tm, GROUP), lambda i: (i, 0))
    return pl.pallas_call(
        body, grid=(s // tm,),
        in_specs=[yspec, yspec, yspec, yspec, pl.BlockSpec((1, d), lambda i: (0, 0)), pl.BlockSpec((d, d), lambda i: (0, 0)),
                  pl.BlockSpec((tm, d), lambda i: (i, 0))],
        out_specs=[pl.BlockSpec((tm, d), lambda i: (i, 0)), pl.BlockSpec((tm, d), lambda i: (i, 0))],
        out_shape=[SDS((s, d), F32), SDS((s, d), _MXU)], compiler_params=_cp("parallel"), name=name)(*ys, g, w, x)


def _out_proj_bwd(dx, w, ys, g, name):
    s, d = dx.shape
    tm = min(512, s)
    nb = tm // TQ

    def body(dx_ref, w_ref, ya, yb, yc, yd, g_ref, da, db, dg_ref, dtc_ref, dtd_ref, dlc_ref, dld_ref):
        @pl.when(pl.program_id(0) == 0)
        def _():
            dg_ref[...] = jnp.zeros_like(dg_ref)

        dyn = _dot_nt(dx_ref[...], w_ref[...])
        for i, y_ref in enumerate((ya, yb, yc, yd)):
            sl = slice(GROUP * i, GROUP * (i + 1))
            y = y_ref[...]
            dy, dg = _rms_bwd(y, g_ref[:, sl], dyn[:, sl])
            dg_ref[:, sl] += dg
            if i < 2:
                (da, db)[i][...] = dy
                continue
            dt_ref, dl_ref = ((dtc_ref, dlc_ref), (dtd_ref, dld_ref))[i - 2]
            dl_ref[...] = jnp.zeros_like(dl_ref)
            for b in range(nb):
                r = slice(TQ * b, TQ * (b + 1))
                dt_ref[b] = dy[r].T.astype(dt_ref.dtype)
                pt = (dy[r] * y[r]).T
                for h in range(N_HEADS):
                    dl_ref[b, h:h + 1, :] = jnp.sum(pt[HEAD_DIM * h:HEAD_DIM * (h + 1), :], axis=0, keepdims=True)

    yspec = pl.BlockSpec((tm, GROUP), lambda i: (i, 0))
    tspec = pl.BlockSpec((nb, GROUP, TQ), lambda i: (i, 0, 0))
    lspec = pl.BlockSpec((nb, 8, TQ), lambda i: (i, 0, 0))
    return pl.pallas_call(
        body, grid=(s // tm,),
        in_specs=[pl.BlockSpec((tm, d), lambda i: (i, 0)), pl.BlockSpec((d, d), lambda i: (0, 0)), yspec, yspec, yspec, yspec,
                  pl.BlockSpec((1, d), lambda i: (0, 0))],
        out_specs=[yspec, yspec, pl.BlockSpec((1, d), lambda i: (0, 0)), tspec, tspec, lspec, lspec],
        out_shape=[SDS((s, GROUP), F32)] * 2 + [SDS((1, d), F32)] + [SDS((s // TQ, GROUP, TQ), _MXU)] * 2
        + [SDS((s // TQ, 8, TQ), F32)] * 2,
        compiler_params=_cp("arbitrary"), name=name)(dx, w, *ys, g)


FF_BLOCK = 512
FF_ROWS = 1024


def _ffn_fwd(x, g, wu, wd, name, comm=None):
    s, d = x.shape
    nj = wu.shape[0]
    tm = min(FF_ROWS, s)
    ni = s // tm

    def body(*refs):
        (x_ref, g_ref, wu_ref, wd_ref), (o_ref, u_ref, h_ref), (acc,), cc = _split_refs(refs, 4, 3, comm)
        i, j = pl.program_id(0), pl.program_id(1)
        _host_gather(comm, cc, i * nj + j, ni * nj)

        @pl.when(j == 0)
        def _():
            h_ref[...] = _rms(x_ref[...], g_ref[...]).astype(h_ref.dtype)
            acc[...] = jnp.zeros_like(acc)

        halves = [slice(r, r + tm // 2) for r in range(0, tm, tm // 2)]
        us = [jnp.dot(h_ref[r, :], wu_ref[0], preferred_element_type=F32) for r in halves]
        for r, u in zip(halves, us):
            u_ref[r, :] = u.astype(u_ref.dtype)
            acc[r, :] += _dot(jnp.square(jnp.maximum(u, 0.0)), wd_ref[...])

        @pl.when(j == nj - 1)
        def _():
            o_ref[...] = x_ref[...] + acc[...]

        if comm is not None:
            @pl.when((i == ni - 1) & (j == nj - 1))
            def _():
                comm.wait(*cc)

    in_specs = [pl.BlockSpec((tm, d), lambda i, j: (i, 0)), pl.BlockSpec((1, d), lambda i, j: (0, 0)),
                pl.BlockSpec((1, d, FF_BLOCK), lambda i, j: (j, 0, 0)), pl.BlockSpec((FF_BLOCK, d), lambda i, j: (j, 0))]
    out_specs = [pl.BlockSpec((tm, d), lambda i, j: (i, 0)), pl.BlockSpec((tm, FF_BLOCK), lambda i, j: (i, j)),
                 pl.BlockSpec((tm, d), lambda i, j: (i, 0))]
    out_shape = [SDS((s, d), F32), SDS((s, nj * FF_BLOCK), _MXU), SDS((s, d), _MXU)]
    return _call_with_comm(body, (ni, nj), in_specs, out_specs, out_shape, [pltpu.VMEM((tm, d), F32)], [x, g, wu, wd], comm,
                           ("arbitrary", "arbitrary"), name)


def _ffn_bwd(dx2, x, u, g, wu, wd, name, comm=None):
    s, d = x.shape
    nj = wu.shape[0]
    tm = min(FF_ROWS, s)
    ni = s // tm

    def body(*refs):
        (dx_ref, x_ref, u_ref, g_ref, wu_ref, wd_ref), (o_ref, du_ref, dg_ref), (acc, dxb), cc = _split_refs(refs, 6, 3, comm)
        i, j = pl.program_id(0), pl.program_id(1)

        @pl.when((i == 0) & (j == 0))
        def _():
            if comm is not None:
                comm.start(*cc)
            dg_ref[...] = jnp.zeros_like(dg_ref)

        @pl.when(j == 0)
        def _():
            dxb[...] = dx_ref[...].astype(dxb.dtype)
            acc[...] = jnp.zeros_like(acc)

        nt = (((1,), (1,)), ((), ()))
        halves = [slice(r, r + tm // 2) for r in range(0, tm, tm // 2)]
        das = [lax.dot_general(dxb[r, :], wd_ref[...], nt, preferred_element_type=F32) for r in halves]
        for r, da in zip(halves, das):
            du = (da * 2.0 * jnp.maximum(u_ref[r, :].astype(F32), 0.0)).astype(du_ref.dtype)
            du_ref[r, :] = du
            acc[r, :] += lax.dot_general(du, wu_ref[0], nt, preferred_element_type=F32)

        @pl.when(j == nj - 1)
        def _():
            dxn, dg = _rms_bwd(x_ref[...], g_ref[...], acc[...])
            o_ref[...] = dx_ref[...] + dxn
            dg_ref[...] += dg

        if comm is not None:
            @pl.when((i == ni - 1) & (j == nj - 1))
            def _():
                comm.wait(*cc)

    in_specs = [pl.BlockSpec((tm, d), lambda i, j: (i, 0)), pl.BlockSpec((tm, d), lambda i, j: (i, 0)),
                pl.BlockSpec((tm, FF_BLOCK), lambda i, j: (i, j)), pl.BlockSpec((1, d), lambda i, j: (0, 0)),
                pl.BlockSpec((1, d, FF_BLOCK), lambda i, j: (j, 0, 0)), pl.BlockSpec((FF_BLOCK, d), lambda i, j: (j, 0))]
    out_specs = [pl.BlockSpec((tm, d), lambda i, j: (i, 0)), pl.BlockSpec((tm, FF_BLOCK), lambda i, j: (i, j)),
                 pl.BlockSpec((1, d), lambda i, j: (0, 0))]
    out_shape = [SDS((s, d), F32), SDS((s, nj * FF_BLOCK), _MXU), SDS((1, d), F32)]
    return _call_with_comm(body, (ni, nj), in_specs, out_specs, out_shape,
                           [pltpu.VMEM((tm, d), F32), pltpu.VMEM((tm, d), _MXU)], [dx2, x, u, g, wu, wd], comm,
                           ("arbitrary", "arbitrary"), name)


def _in_proj_bwd(dz, w, x, g, dx_up, name, comm=None):
    s, d = x.shape
    n = w.shape[1]
    tm = min(512, s)
    ni = s // tm

    def body(*refs):
        (dz_ref, w_ref, x_ref, g_ref, up_ref), (o_ref, dg_ref), _, cc = _split_refs(refs, 5, 2, comm)
        i = pl.program_id(0)

        @pl.when(i == 0)
        def _():
            if comm is not None:
                comm.start(*cc)
            dg_ref[...] = jnp.zeros_like(dg_ref)

        dh = lax.dot_general(dz_ref[...], w_ref[...], (((1,), (1,)), ((), ())), preferred_element_type=F32)
        dxn, dg = _rms_bwd(x_ref[...], g_ref[...], dh)
        o_ref[...] = up_ref[...] + dxn
        dg_ref[...] += dg
        if comm is not None:
            @pl.when(i == ni - 1)
            def _():
                comm.wait(*cc)

    in_specs = [pl.BlockSpec((tm, n), lambda i: (i, 0)), pl.BlockSpec((d, n), lambda i: (0, 0)),
                pl.BlockSpec((tm, d), lambda i: (i, 0)), pl.BlockSpec((1, d), lambda i: (0, 0)),
                pl.BlockSpec((tm, d), lambda i: (i, 0))]
    out_specs = [pl.BlockSpec((tm, d), lambda i: (i, 0)), pl.BlockSpec((1, d), lambda i: (0, 0))]
    out_shape = [SDS((s, d), F32), SDS((1, d), F32)]
    return _call_with_comm(body, (ni,), in_specs, out_specs, out_shape, [], [dz, w, x, g, dx_up], comm, ("arbitrary",), name)


def _loss_head(x, g, target, name):
    s, d = x.shape
    tm = min(512, s)

    def body(x_ref, g_ref, t_ref, l_ref, dx_ref, dg_ref):
        @pl.when(pl.program_id(0) == 0)
        def _():
            l_ref[...] = jnp.zeros_like(l_ref)
            dg_ref[...] = jnp.zeros_like(dg_ref)

        xv = x_ref[...]
        err = _rms(xv, g_ref[...]) - t_ref[...]
        l_ref[...] += jnp.sum(err * err, axis=0, keepdims=True) * (0.5 / d)
        dx, dg = _rms_bwd(xv, g_ref[...], err * (1.0 / d))
        dx_ref[...] = dx
        dg_ref[...] += dg

    return pl.pallas_call(
        body, grid=(s // tm,),
        in_specs=[pl.BlockSpec((tm, d), lambda i: (i, 0)), pl.BlockSpec((1, d), lambda i: (0, 0)),
                  pl.BlockSpec((tm, d), lambda i: (i, 0))],
        out_specs=[pl.BlockSpec((1, d), lambda i: (0, 0)), pl.BlockSpec((tm, d), lambda i: (i, 0)),
                   pl.BlockSpec((1, d), lambda i: (0, 0))],
        out_shape=[SDS((1, d), F32), SDS((s, d), F32), SDS((1, d), F32)], compiler_params=_cp("arbitrary"), name=name)(x, g, target)


def _me_and_peer():
    x, y, c = lax.axis_index("x"), lax.axis_index("y"), lax.axis_index("c")
    me = 4 * x + 2 * y + c

    def peer(k):
        px, py, pc = x ^ (k >> 2), y ^ ((k >> 1) & 1), c ^ (k & 1)
        return (px, py, pc), 4 * px + 2 * py + pc

    return me, peer


class _Comm:
    CHIPS = (2, 4, 6)

    def __init__(self, kind, arrs):
        assert kind in ("gather", "exchange")
        self.kind, self.arrs, self.n = kind, list(arrs), len(arrs)
        anyspec = pl.BlockSpec(memory_space=pl.ANY)
        self.in_specs = [anyspec] * self.n
        self.out_specs = [anyspec] * self.n
        self.out_shape = [SDS(((NDEV,) + a.shape) if kind == "gather" else a.shape, a.dtype) for a in self.arrs]
        npair = NDEV - 1 + len(self.CHIPS)
        self.scratch = [pltpu.SemaphoreType.DMA((self.n, npair)), pltpu.SemaphoreType.DMA((self.n, npair)),
                        pltpu.SemaphoreType.DMA((self.n,))]

    def _copies(self, ins, outs, sems):
        send, recv, loc = sems
        me, peer = _me_and_peer()
        gather = self.kind == "gather"
        sibling = peer(1)[0]
        local = [pltpu.make_async_copy(ins[a] if gather else ins[a].at[me], outs[a].at[me], loc.at[a]) for a in range(self.n)]
        outgoing, incoming, forwards, forwarded = [], [], [], []
        for k in ((1,) + self.CHIPS) if gather else range(1, NDEV):
            dev, pid = peer(k)
            for a in range(self.n):
                pair = dict(send_sem=send.at[a, k - 1], recv_sem=recv.at[a, k - 1], device_id=dev, device_id_type=MESH)
                outgoing.append(pltpu.make_async_remote_copy(src_ref=ins[a] if gather else ins[a].at[pid],
                                                             dst_ref=outs[a].at[me], **pair))
                incoming.append(pltpu.make_async_remote_copy(src_ref=ins[a] if gather else ins[a].at[me],
                                                             dst_ref=outs[a].at[pid], **pair))
        if gather:
            for idx, k in enumerate(self.CHIPS):
                got, theirs = peer(k)[1], peer(k + 1)[1]
                for a in range(self.n):
                    pair = dict(send_sem=send.at[a, NDEV - 1 + idx], recv_sem=recv.at[a, NDEV - 1 + idx], device_id=sibling,
                                device_id_type=MESH)
                    forwards.append(pltpu.make_async_remote_copy(src_ref=outs[a].at[got], dst_ref=outs[a].at[got], **pair))
                    forwarded.append(pltpu.make_async_remote_copy(src_ref=outs[a].at[theirs], dst_ref=outs[a].at[theirs], **pair))
        return local, outgoing, incoming, forwards, forwarded

    def start(self, ins, outs, sems):
        local, outgoing, _, _, _ = self._copies(ins, outs, sems)
        for cp in local + outgoing:
            cp.start()

    def forward(self, ins, outs, sems):
        _, _, incoming, forwards, _ = self._copies(ins, outs, sems)
        per = self.n
        for idx in range(len(forwards) // per if per else 0):
            for a in range(per):
                incoming[(1 + idx) * per + a].wait_recv()
                forwards[idx * per + a].start()

    def wait(self, ins, outs, sems):
        local, outgoing, incoming, forwards, forwarded = self._copies(ins, outs, sems)
        for cp in (incoming[:self.n] if self.kind == "gather" else incoming) + forwarded:
            cp.wait_recv()
        for cp in outgoing + forwards:
            cp.wait_send()
        for cp in local:
            cp.wait()


def _host_gather(comm, cc, step, nsteps, late=False):
    if comm is None:
        return

    @pl.when(step == 0)
    def _():
        comm.start(*cc)

    @pl.when(step == (nsteps - 1 if late else (2 * nsteps) // 3))
    def _():
        comm.forward(*cc)


def _split_refs(refs, n_in, n_out, comm):
    c = comm.n if comm is not None else 0
    ins, cin = refs[:n_in], refs[n_in:n_in + c]
    outs, cout = refs[n_in + c:n_in + c + n_out], refs[n_in + c + n_out:n_in + 2 * c + n_out]
    rest = refs[n_in + 2 * c + n_out:]
    scratch, csem = (rest[:len(rest) - 3], rest[len(rest) - 3:]) if c else (rest, ())
    return ins, outs, scratch, (cin, cout, csem)


def _comm_call(kind, arrs, name):
    comm = _Comm(kind, arrs)

    def body(*refs):
        _, _, _, c = _split_refs(refs, 0, 0, comm)
        comm.start(*c)
        if kind == "gather":
            comm.forward(*c)
        comm.wait(*c)

    return pl.pallas_call(body, in_specs=comm.in_specs, out_specs=comm.out_specs, out_shape=comm.out_shape,
                          scratch_shapes=comm.scratch, compiler_params=pltpu.CompilerParams(has_side_effects=True),
                          name=name)(*arrs)


def _all_gather(arrs, name):
    return _comm_call("gather", arrs, name)


def _exchange(arrs, name):
    return _comm_call("exchange", arrs, name)


def _sum_slots(parts, name):
    _, r, c = parts.shape
    tr = r if r <= 512 else 512

    def body(p_ref, o_ref):
        acc = p_ref[0].astype(F32)
        for q in range(1, NDEV):
            acc = acc + p_ref[q].astype(F32)
        o_ref[...] = acc

    return pl.pallas_call(
        body, grid=(r // tr,), in_specs=[pl.BlockSpec((NDEV, tr, c), lambda i: (0, i, 0))],
        out_specs=pl.BlockSpec((tr, c), lambda i: (i, 0)), out_shape=SDS((r, c), F32),
        compiler_params=_cp("parallel"), name=name)(parts)


def _adamw(g, w, m, v, name):
    r, c = w.shape
    parts = g.ndim == 3
    tr = r
    for cand in (512, 256, 128, 64, 32, 16, 8):
        if r > cand and r % cand == 0 and cand * c * 4 <= 2 * 1024 * 1024:
            tr = cand
            break
    bc1 = 1.0 / (1.0 - ADAM_B1 ** ADAM_STEP)
    bc2 = 1.0 / (1.0 - ADAM_B2 ** ADAM_STEP)

    def body(g_ref, w_ref, m_ref, v_ref, go_ref, d_ref, mo_ref, vo_ref):
        if parts:
            gv = g_ref[0].astype(F32)
            for q in range(1, NDEV):
                gv = gv + g_ref[q].astype(F32)
        else:
            gv = g_ref[...]
        mn = ADAM_B1 * m_ref[...] + (1.0 - ADAM_B1) * gv
        vn = ADAM_B2 * v_ref[...] + (1.0 - ADAM_B2) * (gv * gv)
        go_ref[...] = gv
        mo_ref[...] = mn
        vo_ref[...] = vn
        d_ref[...] = -ADAM_LR * ((mn * bc1) / (jnp.sqrt(vn * bc2) + ADAM_EPS) + ADAM_WD * w_ref[...])

    spec = pl.BlockSpec((tr, c), lambda i: (i, 0))
    gspec = pl.BlockSpec((NDEV, tr, c), lambda i: (0, i, 0)) if parts else spec
    return pl.pallas_call(
        body, grid=(r // tr,), in_specs=[gspec, spec, spec, spec], out_specs=[spec] * 4,
        out_shape=[SDS((r, c), F32)] * 4, compiler_params=_cp("parallel"), name=name)(g, w, m, v)


def _adamw_layer(parts, w, m, v, l, prev, name):
    r, c = parts.shape[1:]
    rows = w.shape[0]
    tr = next(t for t in (512, 256, 128, 64, 32, 16, 8) if r % t == 0 and t * c * 4 <= 2 * 1024 * 1024)
    bc1 = 1.0 / (1.0 - ADAM_B1 ** ADAM_STEP)
    bc2 = 1.0 / (1.0 - ADAM_B2 ** ADAM_STEP)

    def body(g_ref, w_ref, m_ref, v_ref, *rest):
        go_ref, d_ref, mo_ref, vo_ref = rest[-4:]
        gv = g_ref[0].astype(F32)
        for q in range(1, NDEV):
            gv = gv + g_ref[q].astype(F32)
        mn = ADAM_B1 * m_ref[...] + (1.0 - ADAM_B1) * gv
        vn = ADAM_B2 * v_ref[...] + (1.0 - ADAM_B2) * (gv * gv)
        go_ref[...] = gv
        mo_ref[...] = mn
        vo_ref[...] = vn
        d_ref[...] = -ADAM_LR * ((mn * bc1) / (jnp.sqrt(vn * bc2) + ADAM_EPS) + ADAM_WD * w_ref[...])

    spec = pl.BlockSpec((tr, c), lambda i: (l * (r // tr) + i, 0))
    in_specs = [pl.BlockSpec((NDEV, tr, c), lambda i: (0, i, 0)), spec, spec, spec]
    args = [parts, w, m, v]
    aliases = {}
    if prev is not None:
        in_specs += [pl.BlockSpec(memory_space=pl.ANY)] * 4
        args += list(prev)
        aliases = {4 + k: k for k in range(4)}
    return pl.pallas_call(
        body, grid=(r // tr,), in_specs=in_specs, out_specs=[spec] * 4, out_shape=[SDS((rows, c), F32)] * 4,
        input_output_aliases=aliases, compiler_params=_cp("parallel"), name=name)(*args)


def _pad_in_cols(w):
    r = w.shape[0]
    zeros = lambda n: jnp.zeros((r, n), w.dtype)
    return jnp.concatenate([w[:, 512:1536], w[:, 0:512], w[:, 1792:2304], w[:, 1536:1792], w[:, 2308:2692], w[:, 2304:2308],
                            zeros(28), w[:, 2692:2724], zeros(64)], axis=1)


def _unpad_in_cols(w):
    return jnp.concatenate([w[..., 1024:1536], w[..., 0:1024], w[..., 2048:2304], w[..., 1536:2048], w[..., 2688:2692],
                            w[..., 2304:2688], w[..., 2720:2752]], axis=-1)


_Z_RET = (0, 1, 2, 3)
_Z_SGU = (4, 5)
_Z_FOX_Q, _Z_FOX_K, _Z_FOX_V = 8, 6, 7
_DZ_RET, _DZ_SGU, _DZ_FOX_KV, _DZ_FOX_Q, _DZ_MLA, _DZ_MISC = 0, 2, 3, 8, 6, 21


def _pad_uq(w):
    return jnp.pad(w.reshape(256, N_HEADS, 96), ((0, 0), (0, 0), (0, 32))).reshape(256, 512)


def _unpad_uq(w):
    return w.reshape(256, N_HEADS, 128)[:, :, :96].reshape(256, 384)


def _split_ukv(w):
    r = w.reshape(128, N_HEADS, 128)
    return jnp.pad(r[:, :, :64], ((0, 0), (0, 0), (0, 64))).reshape(128, 512), r[:, :, 64:].reshape(128, 256)


def _join_ukv(dk, dv):
    return jnp.concatenate([dk.reshape(128, N_HEADS, 128)[:, :, :64], dv.reshape(128, N_HEADS, 64)], axis=-1).reshape(128, 512)


def _cols_to_full(g):
    return jnp.transpose(g, (1, 0, 2)).reshape(g.shape[1], NDEV * g.shape[2])


def kernel(x, g_mix_norm, w_in, b_forget, g_sgu, w_spatial, b_spatial, g_mla_q, w_uq, g_mla_kv, w_ukv, g_group_out, w_out, g_ffn_norm, w_up, w_down, g_final, loss_target, m_g_mix_norm, m_w_in, m_b_forget, m_g_sgu, m_w_spatial, m_b_spatial, m_g_mla_q, m_w_uq, m_g_mla_kv, m_w_ukv, m_g_group_out, m_w_out, m_g_ffn_norm, m_w_up, m_w_down, m_g_final, v_g_mix_norm, v_w_in, v_b_forget, v_g_sgu, v_w_spatial, v_b_spatial, v_g_mla_q, v_w_uq, v_g_mla_kv, v_w_ukv, v_g_group_out, v_w_out, v_g_ffn_norm, v_w_up, v_w_down, v_g_final):
    depth = w_in.shape[0]
    s, d = x.shape[1], x.shape[2]
    x0 = x.reshape(s, d)
    target = loss_target.reshape(s, d)
    tb = _tables(s)
    me = 4 * lax.axis_index("x") + 2 * lax.axis_index("y") + lax.axis_index("c")

    assert depth == 2
    shards = {}
    for l in range(depth):
        shards.update({(l, "w_in"): _pad_in_cols(w_in[l]).astype(_WIRE), (l, "w_out"): w_out[l].astype(_WIRE),
                       (l, "w_up"): w_up[l].astype(_WIRE), (l, "w_down"): w_down[l].astype(_WIRE),
                       (l, "w_uq"): w_uq[l].astype(_WIRE), (l, "w_ukv"): w_ukv[l].astype(_WIRE)})
    wts = _ShardedWeights(shards)
    first = [(0, "w_in"), (0, "w_uq"), (0, "w_ukv"), (1, "w_uq"), (1, "w_ukv")]
    wts.full.update(zip(first, _all_gather([shards[k] for k in first], "gather_first")))

    row = lambda a: a.reshape(1, -1)

    def small(l):
        bf = jnp.pad(b_forget[l].reshape(1, N_HEADS), ((0, 0), (0, 128 - N_HEADS)))
        bt = jnp.pad(b_spatial[l].T, ((0, 0), (0, 128 - N_HEADS)))
        return dict(g_mix=row(g_mix_norm[l]), g_sgu=row(g_sgu[l]), w_s=w_spatial[l], b_t=bt, b_f=bf, gq=row(g_mla_q[l]),
                    gkv=row(g_mla_kv[l]), g_go=row(g_group_out[l]), g_ffn=row(g_ffn_norm[l]))

    smalls = [small(l) for l in range(depth)]
    lrow, dx, sm, dg_final = _local_step(x0, target, wts, smalls, row(g_final), tb)
    loss = lax.psum(jnp.sum(lrow), AXES)
    grad_x = dx.reshape(1, s, d)
    return _reduce_and_update(loss, grad_x, wts.recv, sm, dg_final, me, dict(
        g_mix_norm=(g_mix_norm, m_g_mix_norm, v_g_mix_norm), w_in=(w_in, m_w_in, v_w_in),
        b_forget=(b_forget, m_b_forget, v_b_forget), g_sgu=(g_sgu, m_g_sgu, v_g_sgu),
        w_spatial=(w_spatial, m_w_spatial, v_w_spatial), b_spatial=(b_spatial, m_b_spatial, v_b_spatial),
        g_mla_q=(g_mla_q, m_g_mla_q, v_g_mla_q), w_uq=(w_uq, m_w_uq, v_w_uq), g_mla_kv=(g_mla_kv, m_g_mla_kv, v_g_mla_kv),
        w_ukv=(w_ukv, m_w_ukv, v_w_ukv), g_group_out=(g_group_out, m_g_group_out, v_g_group_out),
        w_out=(w_out, m_w_out, v_w_out), g_ffn_norm=(g_ffn_norm, m_g_ffn_norm, v_g_ffn_norm), w_up=(w_up, m_w_up, v_w_up),
        w_down=(w_down, m_w_down, v_w_down), g_final=(g_final, m_g_final, v_g_final)))


_GATHER_AT = {
    "in_proj0": [(0, "w_up")],
    "fox_attn0": [(0, "w_down")],
    "mla_attn0": [(0, "w_out"), (1, "w_in")],
    "ffn_fwd0": [(1, "w_down")],
    "fox_attn1": [(1, "w_out")],
    "mla_attn1": [(1, "w_up")],
}
_SCATTER_AT = {
    "fox_attn_bwd1": [(1, "w_down")],
    "mla_attn_bwd1": [(1, "w_up"), (1, "w_out")],
    "ffn_bwd0": [(1, "w_in")],
    "fox_attn_bwd0": [(0, "w_down")],
    "mla_attn_bwd0": [(0, "w_up"), (0, "w_out")],
    "in_proj_bwd0": [(0, "w_in")],
}


class _FullWeights:
    def __init__(self, per_layer):
        self.per_layer, self.grads = per_layer, {}

    def get(self, l, name):
        return self.per_layer[l][name]

    def comm(self, host):
        return None

    def done(self, host, results):
        pass

    def grad(self, l, name, blocks):
        self.grads[(l, name)] = blocks


class _ShardedWeights(_FullWeights):
    def __init__(self, shards):
        self.shards, self.full, self.grads, self.recv = shards, {}, {}, {}

    def get(self, l, name):
        if name in ("wk", "wv"):
            return _split_ukv(_cols_to_full(self.full[(l, "w_ukv")]))[0 if name == "wk" else 1]
        if name == "wq":
            return _pad_uq(_cols_to_full(self.full[(l, "w_uq")]))
        g = self.full[(l, name)]
        return g if name == "w_up" else g.reshape(NDEV * g.shape[1], g.shape[2])

    def comm(self, host):
        if host in _GATHER_AT:
            return _Comm("gather", [self.shards[k] for k in _GATHER_AT[host]])
        if host in _SCATTER_AT:
            return _Comm("exchange", [self.grads[k] for k in _SCATTER_AT[host]])
        return None

    def done(self, host, results):
        if host in _GATHER_AT:
            self.full.update(zip(_GATHER_AT[host], results))
        if host in _SCATTER_AT:
            self.recv.update(zip(_SCATTER_AT[host], results))


def _local_step(x0, target, wts, smalls, g_final, tb):
    depth = len(smalls)
    s, d = x0.shape
    saved = []
    xl = x0
    for l in range(depth):
        p = smalls[l]
        z, h, got = _norm_matmul(xl, p["g_mix"], wts.get(l, "w_in"), f"in_proj{l}", wts.comm(f"in_proj{l}"))
        wts.done(f"in_proj{l}", got)
        ya = _sgu_fwd(z, p["g_sgu"], p["w_s"], p["b_t"], tb, f"sgu_fwd{l}")
        yb, ret, states = _ret_fwd(z, tb, f"ret_fwd{l}")
        cum = _fox_prep(z, p["b_f"], f"fox_prep{l}")
        kc, vc, vtc, qtc = _kv_prep(z, _Z_FOX_Q, _Z_FOX_K, _Z_FOX_V, HEAD_DIM ** -0.5, f"fox_kv{l}")
        yc, lse_c, got = _attn_fwd(z, _Z_FOX_Q, HEAD_DIM, kc, vtc, HEAD_DIM ** -0.5, cum, f"fox_attn{l}", wts.comm(f"fox_attn{l}"))
        wts.done(f"fox_attn{l}", got)
        wq, wk, wv = wts.get(l, "wq"), wts.get(l, "wk"), wts.get(l, "wv")
        qd, kd, vd, vtd, cqn, ckvn, qtd = _mla_prep(z, p["gq"], p["gkv"], wq, wk, wv, tb, f"mla_prep{l}")
        yd, lse_d, got = _attn_fwd(qd, 0, 128, kd, vtd, _SCALE_D, None, f"mla_attn{l}", wts.comm(f"mla_attn{l}"))
        wts.done(f"mla_attn{l}", got)
        ys = (ya, yb, yc, yd)
        x1, yn = _out_proj(ys, p["g_go"], wts.get(l, "w_out"), xl, f"out_proj{l}")
        x2, u, h2, got = _ffn_fwd(x1, p["g_ffn"], wts.get(l, "w_up"), wts.get(l, "w_down"), f"ffn_fwd{l}", wts.comm(f"ffn_fwd{l}"))
        wts.done(f"ffn_fwd{l}", got)
        saved.append(dict(x=xl, z=z, h=h, ys=ys, ret=ret, states=states, cum=cum, lse_c=lse_c, kc=kc, vc=vc, qd=qd, kd=kd, vd=vd,
                          cqn=cqn, ckvn=ckvn, lse_d=lse_d, x1=x1, yn=yn, u=u, h2=h2, wq=wq, wk=wk, wv=wv, qtc=qtc, qtd=qtd))
        xl = x2

    lrow, dx, dg_final = _loss_head(xl, g_final, target, "loss_head")

    sm = [None] * depth
    for l in reversed(range(depth)):
        p, a = smalls[l], saved[l]
        dx1, du, dg_ffn, got = _ffn_bwd(dx, a["x1"], a["u"], p["g_ffn"], wts.get(l, "w_up"), wts.get(l, "w_down"), f"ffn_bwd{l}",
                                        wts.comm(f"ffn_bwd{l}"))
        wts.done(f"ffn_bwd{l}", got)
        dw_down = _mm_tn(a["u"], dx, f"dw_down{l}", a_fn=lambda t: jnp.square(jnp.maximum(t, 0.0)), out_dtype=_WIRE)
        wts.grad(l, "w_down", dw_down.reshape(NDEV, dw_down.shape[0] // NDEV, d))
        wts.grad(l, "w_up", _mm_tn(a["h2"], du, f"dw_up{l}", blocked=True, out_dtype=_WIRE))
        dya, dyb, dg_go, dot_c, dot_d, dl_c, dl_d = _out_proj_bwd(dx1, wts.get(l, "w_out"), a["ys"], p["g_go"],
                                                                  f"out_proj_bwd{l}")
        wts.grad(l, "w_out", _mm_tn(a["yn"], dx1, f"dw_out{l}", out_dtype=_WIRE).reshape(NDEV, d // NDEV, d))
        dz = _ret_bwd(dyb, a["z"], a["ret"], a["states"], tb, f"ret_bwd{l}")
        dz, dg_sgu, dw_s, db_t = _sgu_bwd(dya, a["z"], p["g_sgu"], p["w_s"], p["b_t"], tb, dz, f"sgu_bwd{l}")
        dqt_c, dz, dck, dcq, got = _attn_bwd(a["kc"], a["vc"], a["qtc"], dot_c, a["lse_c"], dl_c, HEAD_DIM,
                                             HEAD_DIM ** -0.5, a["cum"], f"fox_attn_bwd{l}", _MXU,
                                             wts.comm(f"fox_attn_bwd{l}"), kv_into=dz)
        wts.done(f"fox_attn_bwd{l}", got)
        dz = _untranspose(dqt_c, _MXU, f"fox_dq{l}", into=dz, col=_DZ_FOX_Q)
        dqt_d, dk_d, dv_d, got = _attn_bwd(a["kd"], a["vd"], a["qtd"], dot_d, a["lse_d"], dl_d, 128, _SCALE_D, None,
                                           f"mla_attn_bwd{l}", F32, wts.comm(f"mla_attn_bwd{l}"))
        wts.done(f"mla_attn_bwd{l}", got)
        dz, dkr, dwq, dwk, dwv, dgq, dgkv = _mla_prep_bwd(dqt_d, dk_d, dv_d, a["z"], a["cqn"], a["ckvn"], p["gq"], p["gkv"],
                                                          a["wq"], a["wk"], a["wv"], tb, dz, f"mla_prep_bwd{l}")
        dz, db_f = _fox_post(dcq, dck, a["z"], p["b_f"], dkr, dz, f"fox_post{l}")
        wts.grad(l, "w_in", _unpad_in_cols(_mm_tn(a["h"], dz, f"dw_in{l}", out_dtype=_WIRE)).reshape(NDEV, d // NDEV, N_IN))
        dx, dg_mix, got = _in_proj_bwd(dz, wts.get(l, "w_in"), a["x"], p["g_mix"], dx1, f"in_proj_bwd{l}",
                                       wts.comm(f"in_proj_bwd{l}"))
        wts.done(f"in_proj_bwd{l}", got)
        sm[l] = [dg_mix, dg_go, dg_ffn, dg_sgu, dw_s, db_t[:, :N_HEADS].T, db_f[0, :N_HEADS], dgq, dgkv, _unpad_uq(dwq),
                 _join_ukv(dwk, dwv)]
    return lrow, dx, sm, dg_final


def _reduce_and_update(loss, grad_x, recv, sm, dg_final, me, given):
    depth = len(sm)
    pieces = [t for l in range(depth) for t in sm[l]] + [dg_final]
    flat = jnp.concatenate([t.reshape(-1) for t in pieces])
    n_flat = flat.shape[0]
    unit = NDEV * 8 * 128
    n_pad = -(-n_flat // unit) * unit
    packed = jnp.pad(flat, (0, n_pad - n_flat)).reshape(NDEV, n_pad // (NDEV * 128), 128)
    red = _sum_slots(_exchange([packed], "scatter_small")[0], "sum_small")
    full = _all_gather([red], "gather_small")[0].reshape(-1)
    offs = np.cumsum([0] + [int(np.prod(t.shape)) for t in pieces])
    red_pieces = [full[int(offs[i]):int(offs[i + 1])].reshape(pieces[i].shape) for i in range(len(pieces))]
    per = len(sm[0])
    stack = lambda i: jnp.stack([red_pieces[l * per + i] for l in range(depth)])
    g_small = dict(g_mix_norm=stack(0), g_group_out=stack(1), g_ffn_norm=stack(2), g_sgu=stack(3), w_spatial=stack(4),
                   b_spatial=stack(5), b_forget=stack(6), g_mla_q=stack(7), g_mla_kv=stack(8), g_final=red_pieces[-1])
    cq, ckv = given["w_uq"][0].shape[2], given["w_ukv"][0].shape[2]
    g_small["w_uq"] = lax.dynamic_slice_in_dim(stack(9), me * cq, cq, axis=2)
    g_small["w_ukv"] = lax.dynamic_slice_in_dim(stack(10), me * ckv, ckv, axis=2)

    names = list(given)
    outs = {}
    for nme in names:
        wv_, mv_, vv_ = given[nme]
        shape = wv_.shape
        if nme in ("w_in", "w_out", "w_up", "w_down"):
            res = None
            flat2 = lambda t: t.reshape(-1, shape[-1])
            for l in range(depth):
                res = _adamw_layer(recv[(l, nme)], flat2(wv_), flat2(mv_), flat2(vv_), l, res, f"adamw_{nme}{l}")
            outs[nme] = [t.reshape(shape) for t in res]
        else:
            two = lambda t: t.reshape(-1, shape[-1]) if t.ndim > 1 else t.reshape(1, -1)
            res = _adamw(two(g_small[nme]), two(wv_), two(mv_), two(vv_), f"adamw_{nme}")
            outs[nme] = [r.reshape(shape) for r in res]
    return (loss, grad_x, *[outs[n][0] for n in names], *[outs[n][1] for n in names], *[outs[n][2] for n in names],
            *[outs[n][3] for n in names])
```

```python
import functools

import jax
import jax.numpy as jnp
import numpy as np
from jax import lax
from jax.experimental import pallas as pl
from jax.experimental.pallas import tpu as pltpu

F32 = jnp.float32
_MXU = jnp.bfloat16
_WIRE = jnp.bfloat16
EPS = 1e-6
NDEV = 8
AXES = ("x", "y", "c")
MESH = pl.DeviceIdType.MESH

N_HEADS = 4
HEAD_DIM = 64
GROUP = 256
CHUNK = 128
NZ = 2816
N_IN = 2724
MISC_F, MISC_KR = 0, 32
VMEM_LIMIT = 56 * 1024 * 1024

ADAM_LR, ADAM_B1, ADAM_B2, ADAM_EPS, ADAM_WD, ADAM_STEP = 0.001, 0.9, 0.999, 1e-08, 0.01, 10

SDS = jax.ShapeDtypeStruct


def _cp(*sem):
    return pltpu.CompilerParams(dimension_semantics=sem, vmem_limit_bytes=VMEM_LIMIT)


def _dot(a, b):
    return jnp.dot(a.astype(_MXU), b.astype(_MXU), preferred_element_type=F32)


def _dot_nt(a, b):
    return lax.dot_general(a.astype(_MXU), b.astype(_MXU), (((1,), (1,)), ((), ())), preferred_element_type=F32)


def _dot_tn(a, b):
    return lax.dot_general(a.astype(_MXU), b.astype(_MXU), (((0,), (0,)), ((), ())), preferred_element_type=F32)


def _dot_exact(a, b, dims=(((1,), (0,)), ((), ()))):
    return lax.dot_general(a, b, dims, precision=lax.Precision.HIGHEST, preferred_element_type=F32)


def _rms(x, g):
    return x * lax.rsqrt(jnp.mean(x * x, axis=-1, keepdims=True) + EPS) * g


def _rms_bwd(x, g, dy):
    xh = x * lax.rsqrt(jnp.mean(x * x, axis=-1, keepdims=True) + EPS)
    dxh = dy * g
    r = lax.rsqrt(jnp.mean(x * x, axis=-1, keepdims=True) + EPS)
    dx = r * (dxh - xh * jnp.mean(dxh * xh, axis=-1, keepdims=True))
    return dx, jnp.sum(dy * xh, axis=0, keepdims=True)


_GELU_C = 0.7978845608028654


def _gelu(x):
    return 0.5 * x * (1.0 + jnp.tanh(_GELU_C * (x + 0.044715 * x * x * x)))


def _gelu_grad(x):
    t = jnp.tanh(_GELU_C * (x + 0.044715 * x * x * x))
    return 0.5 * (1.0 + t) + 0.5 * x * (1.0 - t * t) * _GELU_C * (1.0 + 3 * 0.044715 * x * x)


def _sigmoid(x):
    return 1.0 / (1.0 + jnp.exp(-x))


def _swap_half(t, half):
    n = t.shape[-1]
    lane = lax.broadcasted_iota(jnp.int32, t.shape, t.ndim - 1)
    return jnp.where((lane % (2 * half)) < half, pltpu.roll(t, n - half, t.ndim - 1), pltpu.roll(t, half, t.ndim - 1))


def _lanes(table, width):
    return jnp.concatenate([table] * (width // table.shape[-1]), axis=-1)


def _rope(t, cos, sin, half):
    return t * cos + _swap_half(t, half) * sin


def _rope_bwd(d, cos, sin, half):
    return d * cos - _swap_half(d, half) * sin


def _tables(s):
    pos = jnp.arange(s, dtype=F32)[:, None]

    def cs(half):
        inv = jnp.power(10000.0, -jnp.arange(half, dtype=F32) / half)
        ang = pos * inv[None, :]
        return jnp.cos(ang), jnp.sin(ang)

    c32, s32 = cs(32)
    c16, s16 = cs(16)
    z = lambda w: jnp.zeros((s, w), F32)
    o = lambda w: jnp.ones((s, w), F32)
    t = {}
    t["b_cos"] = jnp.concatenate([c32, c32, c32, c32], 1)
    t["b_sin"] = jnp.concatenate([-s32, s32, -s32, s32], 1)
    t["q_cos"] = jnp.concatenate([o(64), c16, c16, z(32)], 1)
    t["q_sin"] = jnp.concatenate([z(64), -s16, s16, z(32)], 1)
    t["k_cos"] = jnp.concatenate([z(32), c16, c16, z(64)], 1)
    t["k_sin"] = jnp.concatenate([z(32), -s16, s16, z(64)], 1)
    lg = jnp.log1p(-jnp.exp2(-5.0 - jnp.arange(N_HEADS, dtype=F32)))
    j = jnp.arange(CHUNK, dtype=F32)
    rel = j[:, None] - j[None, :]
    t["decay"] = jnp.where(rel[None] >= 0, jnp.exp(jnp.maximum(rel, 0.0)[None] * lg[:, None, None]), 0.0)
    t["decay_t"] = jnp.swapaxes(t["decay"], 1, 2)

    def rows(e):
        return jnp.repeat(e.T, HEAD_DIM, axis=1)

    t["qw"] = rows(jnp.exp((j + 1.0)[None, :] * lg[:, None]))
    t["kw"] = rows(jnp.exp((CHUNK - 1 - j)[None, :] * lg[:, None]))
    t["kw2"] = rows(jnp.exp((CHUNK - j)[None, :] * lg[:, None]))
    t["qw0"] = rows(jnp.exp(j[None, :] * lg[:, None]))
    t["cd"] = jnp.repeat(jnp.exp(CHUNK * lg), HEAD_DIM)[None, :]
    e = np.zeros((128, 512), np.float32)
    for h in range(N_HEADS):
        for r in range(32):
            e[MISC_KR + r, 128 * h + 64 + r] = 1.0
    t["place"] = jnp.asarray(e)
    lane_head = np.arange(GROUP) // HEAD_DIM
    t["grp"] = jnp.asarray((lane_head[:, None] == lane_head[None, :]) / HEAD_DIM, _MXU)
    hsel = (np.arange(128)[:, None] == lane_head[None, :]).astype(np.float32)
    t["hsel"] = jnp.asarray(hsel)
    t["hselt"] = jnp.asarray(hsel.T, _MXU)
    return t


def _norm_matmul(x, g, w, name, comm=None):
    s, d = x.shape
    n = w.shape[1]
    tm, tn = min(512, s), 256
    ni = s // tm

    def body(*refs):
        (x_ref, g_ref, w_ref), (z_ref, h_ref), _, cc = _split_refs(refs, 3, 2, comm)
        i = pl.program_id(0)
        _host_gather(comm, cc, i, ni, late=True)
        h = _rms(x_ref[...], g_ref[...]).astype(h_ref.dtype)
        h_ref[...] = h
        for j in range(n // tn):
            z_ref[:, tn * j:tn * (j + 1)] = jnp.dot(h, w_ref[:, tn * j:tn * (j + 1)], preferred_element_type=F32)
        if comm is not None:
            @pl.when(i == ni - 1)
            def _():
                comm.wait(*cc)

    in_specs = [pl.BlockSpec((tm, d), lambda i: (i, 0)), pl.BlockSpec((1, d), lambda i: (0, 0)),
                pl.BlockSpec((d, n), lambda i: (0, 0))]
    out_specs = [pl.BlockSpec((tm, n), lambda i: (i, 0)), pl.BlockSpec((tm, d), lambda i: (i, 0))]
    out_shape = [SDS((s, n), F32), SDS((s, d), _MXU)]
    return _call_with_comm(body, (ni,), in_specs, out_specs, out_shape, [], [x, g, w], comm, ("arbitrary",), name)


def _mm_tn(a, b, name, *, a_fn=None, blocked=False, out_dtype=F32):
    k, m = a.shape
    n = b.shape[1]
    tm, tk = min(1024, m), min(1024, k)
    tn = next(t for t in (2816, 1024, 512, 256, 128) if n % t == 0)
    assert m % tm == 0 and k % tk == 0
    nk = k // tk

    def body(a_ref, b_ref, o_ref, acc):
        kk = pl.program_id(2)

        @pl.when(kk == 0)
        def _():
            acc[...] = jnp.zeros_like(acc)

        av = a_ref[...]
        if a_fn is not None:
            av = a_fn(av.astype(F32))
        acc[...] += _dot_tn(av, b_ref[...])

        @pl.when(kk == nk - 1)
        def _():
            if blocked:
                for c in range(tn // 512):
                    o_ref[c] = acc[:, 512 * c:512 * (c + 1)].astype(o_ref.dtype)
            else:
                o_ref[...] = acc[...].astype(o_ref.dtype)

    if blocked:
        assert tn % 512 == 0
        out_spec = pl.BlockSpec((tn // 512, tm, 512), lambda i, j, kk: (j, i, 0))
        out_shape = SDS((n // 512, m, 512), out_dtype)
    else:
        out_spec = pl.BlockSpec((tm, tn), lambda i, j, kk: (i, j))
        out_shape = SDS((m, n), out_dtype)
    return pl.pallas_call(
        body, grid=(m // tm, n // tn, nk),
        in_specs=[pl.BlockSpec((tk, tm), lambda i, j, kk: (kk, i)), pl.BlockSpec((tk, tn), lambda i, j, kk: (kk, j))],
        out_specs=out_spec, out_shape=out_shape, scratch_shapes=[pltpu.VMEM((tm, tn), F32)],
        compiler_params=_cp("parallel", "parallel", "arbitrary"), name=name)(a, b)


def _split_dot(x, m):
    hi = x.astype(_MXU)
    lo = (x - hi.astype(F32)).astype(_MXU)
    return jnp.dot(hi, m, preferred_element_type=F32) + jnp.dot(lo, m, preferred_element_type=F32)


def _gstandardize(t, grp):
    tc = t - _split_dot(t, grp)
    rs = lax.rsqrt(_split_dot(tc * tc, grp) + EPS)
    return tc * rs, rs


def _gstandardize_bwd(yh, rs, dy, grp):
    return rs * (dy - _split_dot(dy, grp) - yh * _split_dot(dy * yh, grp))


def _head_select(parts):
    hid = lax.broadcasted_iota(jnp.int32, parts[0].shape, 1) // HEAD_DIM
    return jnp.where(hid == 0, parts[0], jnp.where(hid == 1, parts[1], jnp.where(hid == 2, parts[2], parts[3])))


def _head_masked(x):
    hid = lax.broadcasted_iota(jnp.int32, x.shape, 1) // HEAD_DIM
    return [jnp.where(hid == h, x, jnp.zeros_like(x)) for h in range(N_HEADS)]


def _tril(w):
    r = lax.broadcasted_iota(jnp.int32, w.shape, 0)
    c = lax.broadcasted_iota(jnp.int32, w.shape, 1)
    return jnp.where(r >= c, w, 0.0)


def _sgu_mixed(vgb, wcs, bias, nchunk):
    ms = [[jnp.dot(wcs[h], vgb[CHUNK * c:CHUNK * (c + 1)], preferred_element_type=F32) for h in range(N_HEADS)]
          for c in range(nchunk)]
    return [_head_select(ms[c]) + bias for c in range(nchunk)]


def _sgu_fwd(z, gain, w_s, b_t, tb, name):
    s = z.shape[0]
    tm = min(512, s)
    const = lambda a: pl.BlockSpec(a.shape, lambda i: (0,) * a.ndim)

    def body(u_ref, v_ref, g_ref, w_ref, b_ref, grp_ref, hsel_ref, y_ref):
        u = _gelu(u_ref[...])
        vh, _ = _gstandardize(_gelu(v_ref[...]), grp_ref[...])
        vgb = (vh * g_ref[...]).astype(_MXU)
        bias = _dot_exact(b_ref[...], hsel_ref[...])
        wcs = [_tril(w_ref[h]).astype(_MXU) for h in range(N_HEADS)]
        for c, mixed in enumerate(_sgu_mixed(vgb, wcs, bias, tm // CHUNK)):
            r = slice(CHUNK * c, CHUNK * (c + 1))
            y_ref[r, :] = u[r] * mixed

    return pl.pallas_call(
        body, grid=(s // tm,),
        in_specs=[pl.BlockSpec((tm, GROUP), lambda i: (i, _Z_SGU[0])), pl.BlockSpec((tm, GROUP), lambda i: (i, _Z_SGU[1])),
                  pl.BlockSpec((1, GROUP), lambda i: (0, 0)), pl.BlockSpec((N_HEADS, CHUNK, CHUNK), lambda i: (0, 0, 0)),
                  pl.BlockSpec((CHUNK, 128), lambda i: (0, 0)), const(tb["grp"]), const(tb["hsel"])],
        out_specs=pl.BlockSpec((tm, GROUP), lambda i: (i, 0)), out_shape=SDS((s, GROUP), F32),
        compiler_params=_cp("parallel"), name=name)(z, z, gain, w_s, b_t, tb["grp"], tb["hsel"])


def _sgu_bwd(dy, z, gain, w_s, b_t, tb, dz, name):
    s = z.shape[0]
    tm = min(512, s)
    nchunk = tm // CHUNK
    const = lambda a: pl.BlockSpec(a.shape, lambda i: (0,) * a.ndim)

    def body(dy_ref, u_ref, v_ref, g_ref, w_ref, b_ref, grp_ref, hsel_ref, hselt_ref, _, dz_ref, dg_ref, dw_ref, db_ref):
        @pl.when(pl.program_id(0) == 0)
        def _():
            dg_ref[...] = jnp.zeros_like(dg_ref)
            dw_ref[...] = jnp.zeros_like(dw_ref)
            db_ref[...] = jnp.zeros_like(db_ref)

        grp = grp_ref[...]
        u_pre, v_pre, gain_v = u_ref[...], v_ref[...], g_ref[...]
        u = _gelu(u_pre)
        vh, rs = _gstandardize(_gelu(v_pre), grp)
        vgb = (vh * gain_v).astype(_MXU)
        dyv = dy_ref[...]
        bias = _dot_exact(b_ref[...], hsel_ref[...])
        wfs = [_tril(w_ref[h]) for h in range(N_HEADS)]
        wcs = [w.astype(_MXU) for w in wfs]
        wts = [w.T.astype(_MXU) for w in wfs]
        mixed = _sgu_mixed(vgb, wcs, bias, nchunk)
        gu = _gelu_grad(u_pre)
        dms, dmh = [], []
        for c in range(nchunk):
            r = slice(CHUNK * c, CHUNK * (c + 1))
            dz_ref[r, 0:GROUP] = (dyv[r] * mixed[c] * gu[r]).astype(dz_ref.dtype)
            dm = dyv[r] * u[r]
            dms.append(dm)
            dmh.append([m.astype(_MXU) for m in _head_masked(dm)])
        dws = [sum(lax.dot_general(dmh[c][h], vgb[CHUNK * c:CHUNK * (c + 1)], (((1,), (1,)), ((), ())),
                                   preferred_element_type=F32) for c in range(nchunk)) for h in range(N_HEADS)]
        dvg = jnp.concatenate([sum(jnp.dot(wts[h], dmh[c][h], preferred_element_type=F32) for h in range(N_HEADS))
                               for c in range(nchunk)], axis=0)
        for h in range(N_HEADS):
            dw_ref[h] += _tril(dws[h])
        db_ref[...] += sum(_split_dot(dm, hselt_ref[...]) for dm in dms)
        dg_ref[...] += jnp.sum(dvg * vh, axis=0, keepdims=True)
        dv = _gstandardize_bwd(vh, rs, dvg * gain_v, grp)
        dz_ref[:, GROUP:2 * GROUP] = (dv * _gelu_grad(v_pre)).astype(dz_ref.dtype)

    consts = [tb["grp"], tb["hsel"], tb["hselt"]]
    return pl.pallas_call(
        body, grid=(s // tm,),
        in_specs=[pl.BlockSpec((tm, GROUP), lambda i: (i, 0)),
                  pl.BlockSpec((tm, GROUP), lambda i: (i, _Z_SGU[0])), pl.BlockSpec((tm, GROUP), lambda i: (i, _Z_SGU[1])),
                  pl.BlockSpec((1, GROUP), lambda i: (0, 0)), pl.BlockSpec((N_HEADS, CHUNK, CHUNK), lambda i: (0, 0, 0)),
                  pl.BlockSpec((CHUNK, 128), lambda i: (0, 0))] + [const(a) for a in consts]
        + [pl.BlockSpec(memory_space=pl.ANY)],
        out_specs=[pl.BlockSpec((tm, 2 * GROUP), lambda i: (i, _DZ_SGU)), pl.BlockSpec((1, GROUP), lambda i: (0, 0)),
                   pl.BlockSpec((N_HEADS, CHUNK, CHUNK), lambda i: (0, 0, 0)), pl.BlockSpec((CHUNK, 128), lambda i: (0, 0))],
        out_shape=[SDS(dz.shape, dz.dtype), SDS((1, GROUP), F32), SDS((N_HEADS, CHUNK, CHUNK), F32), SDS((CHUNK, 128), F32)],
        input_output_aliases={9: 0}, compiler_params=_cp("arbitrary"), name=name)(dy, z, z, gain, w_s, b_t, *consts, dz)


_SCALE_B = HEAD_DIM ** -0.5
RET_CHUNKS = 4


def _block_diag(compact):
    full = jnp.concatenate([compact] * N_HEADS, axis=0)
    r = lax.broadcasted_iota(jnp.int32, full.shape, 0) // HEAD_DIM
    c = lax.broadcasted_iota(jnp.int32, full.shape, 1) // HEAD_DIM
    return jnp.where(r == c, full, 0.0)


def _diag_blocks(full):
    c = lax.broadcasted_iota(jnp.int32, (HEAD_DIM, GROUP), 1) // HEAD_DIM
    return sum(jnp.where(c == h, full[HEAD_DIM * h:HEAD_DIM * (h + 1), :], 0.0) for h in range(N_HEADS))


def _ret_fwd(z, tb, name):
    s = z.shape[0]
    nc = s // CHUNK
    per = min(RET_CHUNKS, nc)
    rows = per * CHUNK
    row = lambda col: pl.BlockSpec((rows, GROUP), lambda n, col=col: (n, col))
    const = lambda shape: pl.BlockSpec(shape, lambda n: (0,) * len(shape))

    def body(q_ref, k_ref, v_ref, g_ref, cos_ref, sin_ref, dec_ref, qw_ref, kw_ref, cd_ref, grp_ref, y_ref, o_ref, st_ref, state):
        @pl.when(pl.program_id(0) == 0)
        def _():
            state[...] = jnp.zeros_like(state)

        cos, sin = _lanes(cos_ref[...], GROUP), _lanes(sin_ref[...], GROUP)
        q = _rope(q_ref[...], cos, sin, 32)
        k = _rope(k_ref[...], cos, sin, 32) * _SCALE_B
        v = v_ref[...]
        g = g_ref[...]
        rcs = [slice(CHUNK * c, CHUNK * (c + 1)) for c in range(per)]
        vms = [[t.astype(_MXU) for t in _head_masked(v[r])] for r in rcs]
        scs = [[_dot_nt(t.astype(_MXU), k[r]) for t in _head_masked(q[r])] for r in rcs]
        kvs = [_dot_tn(k[r] * kw_ref[...], v[r]) for r in rcs]
        st = state[...]
        crosses = []
        for c, r in enumerate(rcs):
            st_ref[c] = st
            crosses.append(_dot(q[r] * qw_ref[...], _block_diag(st)))
            st = cd_ref[...] * st + _diag_blocks(kvs[c])
        state[...] = st
        outs = []
        for c in range(per):
            scd = [(scs[c][h] * dec_ref[h]).astype(_MXU) for h in range(N_HEADS)]
            outs.append(crosses[c] + sum(jnp.dot(scd[h], vms[c][h], preferred_element_type=F32) for h in range(N_HEADS)))
        o = jnp.concatenate(outs, axis=0)
        o_ref[...] = o
        yh, _ = _gstandardize(o, grp_ref[...])
        y_ref[...] = g * _sigmoid(g) * yh

    return pl.pallas_call(
        body, grid=(nc // per,),
        in_specs=[row(_Z_RET[0]), row(_Z_RET[1]), row(_Z_RET[2]), row(_Z_RET[3]), pl.BlockSpec((rows, 128), lambda n: (n, 0)),
                  pl.BlockSpec((rows, 128), lambda n: (n, 0)), const((N_HEADS, CHUNK, CHUNK)),
                  const((CHUNK, GROUP)), const((CHUNK, GROUP)), const((1, GROUP)), const((GROUP, GROUP))],
        out_specs=[pl.BlockSpec((rows, GROUP), lambda n: (n, 0)), pl.BlockSpec((rows, GROUP), lambda n: (n, 0)),
                   pl.BlockSpec((per, HEAD_DIM, GROUP), lambda n: (n, 0, 0))],
        out_shape=[SDS((s, GROUP), F32), SDS((s, GROUP), F32), SDS((nc, HEAD_DIM, GROUP), F32)],
        scratch_shapes=[pltpu.VMEM((HEAD_DIM, GROUP), F32)],
        compiler_params=_cp("arbitrary"), name=name)(z, z, z, z, tb["b_cos"], tb["b_sin"], tb["decay"], tb["qw"], tb["kw"], tb["cd"],
                                                       tb["grp"])


def _ret_bwd(dy, z, o_pre, states, tb, name):
    s = z.shape[0]
    nc = s // CHUNK
    per = min(RET_CHUNKS, nc)
    rows = per * CHUNK
    ns = nc // per
    rev = lambda col: pl.BlockSpec((rows, GROUP), lambda n, col=col: (ns - 1 - n, col))
    const = lambda shape: pl.BlockSpec(shape, lambda n: (0,) * len(shape))

    def body(dy_ref, q_ref, k_ref, v_ref, g_ref, o_ref, st_ref, cos_ref, sin_ref, dec_ref, dect_ref, qw_ref, kw2_ref, qw0_ref,
             cd_ref, grp_ref, dz_ref, rstate):
        @pl.when(pl.program_id(0) == 0)
        def _():
            rstate[...] = jnp.zeros_like(rstate)

        cos, sin = _lanes(cos_ref[...], GROUP), _lanes(sin_ref[...], GROUP)
        q = _rope(q_ref[...], cos, sin, 32)
        k = _rope(k_ref[...], cos, sin, 32) * _SCALE_B
        v = v_ref[...]
        g = g_ref[...]
        dyv = dy_ref[...]
        sg = _sigmoid(g)
        yh, rs = _gstandardize(o_ref[...], grp_ref[...])
        dz_ref[:, 3 * GROUP:4 * GROUP] = (dyv * yh * (sg * (1.0 + g * (1.0 - sg)))).astype(dz_ref.dtype)
        do = _gstandardize_bwd(yh, rs, dyv * (g * sg), grp_ref[...])
        hs = range(N_HEADS)
        rcs = [slice(CHUNK * c, CHUNK * (c + 1)) for c in range(per)]
        mask = lambda t: [m.astype(_MXU) for m in _head_masked(t)]
        qms, kms, vms, doms = ([mask(t[r]) for r in rcs] for t in (q, k, v, do))
        dps = [[_dot_nt(doms[c][h], v[r]) for h in hs] for c, r in enumerate(rcs)]
        pts = [[_dot_nt(kms[c][h], q[r]) for h in hs] for c, r in enumerate(rcs)]
        dpts = [[_dot_nt(vms[c][h], do[r]) for h in hs] for c, r in enumerate(rcs)]
        dq_x = [_dot_nt(do[r] * qw_ref[...], _block_diag(st_ref[c])) for c, r in enumerate(rcs)]
        r_new = [_dot_tn(q[r] * qw0_ref[...], do[r]) for r in rcs]
        rr = rstate[...]
        dk_x, dv_x = [None] * per, [None] * per
        for c in reversed(range(per)):
            r_bd = _block_diag(rr)
            dk_x[c] = _dot_nt(v[rcs[c]] * kw2_ref[...], r_bd)
            dv_x[c] = _dot(k[rcs[c]] * kw2_ref[...], r_bd)
            rr = cd_ref[...] * rr + _diag_blocks(r_new[c])
        rstate[...] = rr
        dqs, dks, dvs = [], [], []
        for c in range(per):
            dpd = [(dps[c][h] * dec_ref[h]).astype(_MXU) for h in hs]
            dptd = [(dpts[c][h] * dect_ref[h]).astype(_MXU) for h in hs]
            ptd = [(pts[c][h] * dect_ref[h]).astype(_MXU) for h in hs]
            dqs.append(dq_x[c] + sum(jnp.dot(dpd[h], kms[c][h], preferred_element_type=F32) for h in hs))
            dks.append(dk_x[c] + sum(jnp.dot(dptd[h], qms[c][h], preferred_element_type=F32) for h in hs))
            dvs.append(dv_x[c] + sum(jnp.dot(ptd[h], doms[c][h], preferred_element_type=F32) for h in hs))
        dz_ref[:, 0:GROUP] = _rope_bwd(jnp.concatenate(dqs, axis=0), cos, sin, 32).astype(dz_ref.dtype)
        dz_ref[:, GROUP:2 * GROUP] = _rope_bwd(jnp.concatenate(dks, axis=0) * _SCALE_B, cos, sin, 32).astype(dz_ref.dtype)
        dz_ref[:, 2 * GROUP:3 * GROUP] = jnp.concatenate(dvs, axis=0).astype(dz_ref.dtype)

    r0 = lambda: pl.BlockSpec((rows, GROUP), lambda n: (ns - 1 - n, 0))
    r128 = lambda: pl.BlockSpec((rows, 128), lambda n: (ns - 1 - n, 0))
    return pl.pallas_call(
        body, grid=(ns,),
        in_specs=[r0(), rev(_Z_RET[0]), rev(_Z_RET[1]), rev(_Z_RET[2]), rev(_Z_RET[3]), r0(),
                  pl.BlockSpec((per, HEAD_DIM, GROUP), lambda n: (ns - 1 - n, 0, 0)),
                  r128(), r128(), const((N_HEADS, CHUNK, CHUNK)), const((N_HEADS, CHUNK, CHUNK)), const((CHUNK, GROUP)),
                  const((CHUNK, GROUP)), const((CHUNK, GROUP)), const((1, GROUP)), const((GROUP, GROUP))],
        out_specs=pl.BlockSpec((rows, 4 * GROUP), lambda n: (ns - 1 - n, _DZ_RET)),
        out_shape=SDS((s, NZ), _MXU), scratch_shapes=[pltpu.VMEM((HEAD_DIM, GROUP), F32)],
        compiler_params=_cp("arbitrary"), name=name)(
            dy, z, z, z, z, o_pre, states, tb["b_cos"], tb["b_sin"], tb["decay"], tb["decay_t"], tb["qw"], tb["kw2"], tb["qw0"],
            tb["cd"], tb["grp"])


TQ = 256


def _log_sigmoid(x):
    return jnp.minimum(x, 0.0) - jnp.log1p(jnp.exp(-jnp.abs(x)))


def _fox_prep(z, b_f, name):
    s = z.shape[0]
    nb = s // TQ

    def body(m_ref, b_ref, cc_ref, carry):
        @pl.when(pl.program_id(0) == 0)
        def _():
            carry[...] = jnp.zeros_like(carry)

        lane = lax.broadcasted_iota(jnp.int32, (TQ, 128), 1)
        logf = jnp.where(lane < N_HEADS, _log_sigmoid(m_ref[...] + b_ref[...]), 0.0)
        r = lax.broadcasted_iota(jnp.int32, (TQ, TQ), 0)
        c = lax.broadcasted_iota(jnp.int32, (TQ, TQ), 1)
        tri = jnp.where(r >= c, 1.0, 0.0).astype(F32)
        cum = _dot_exact(tri, logf) + carry[...]
        cc_ref[...] = cum * LOG2E
        carry[...] = cum[TQ - 1:TQ, :]

    return pl.pallas_call(
        body, grid=(nb,),
        in_specs=[pl.BlockSpec((TQ, 128), lambda i: (i, NZ // 128 - 1)), pl.BlockSpec((1, 128), lambda i: (0, 0))],
        out_specs=pl.BlockSpec((TQ, 128), lambda i: (i, 0)),
        out_shape=SDS((s, 128), F32), scratch_shapes=[pltpu.VMEM((1, 128), F32)],
        compiler_params=_cp("arbitrary"), name=name)(z, b_f)


def _fox_post(dcr, dcq, z, b_f, dkr, dz, name):
    s = z.shape[0]
    nb = s // TQ

    def body(dc_ref, dcq_ref, m_ref, b_ref, dkr_ref, _, dz_ref, db_ref, carry):
        @pl.when(pl.program_id(0) == 0)
        def _():
            carry[...] = jnp.zeros_like(carry)
            db_ref[...] = jnp.zeros_like(db_ref)

        r = lax.broadcasted_iota(jnp.int32, (TQ, TQ), 0)
        c = lax.broadcasted_iota(jnp.int32, (TQ, TQ), 1)
        triu = jnp.where(c >= r, 1.0, 0.0).astype(F32)
        dc = jnp.concatenate([dc_ref[0], jnp.zeros((120, TQ), F32)], axis=0)
        dlogf = _dot_exact(triu, dc, (((1,), (1,)), ((), ()))) + _dot_exact(triu, dcq_ref[...]) + carry[...]
        carry[...] = dlogf[0:1, :]
        x = m_ref[...] + b_ref[...]
        lane = lax.broadcasted_iota(jnp.int32, (TQ, 128), 1)
        df = jnp.where(lane < N_HEADS, dlogf * _sigmoid(-x), 0.0)
        db_ref[...] += jnp.sum(df, axis=0, keepdims=True)
        dz_ref[...] = (df + dkr_ref[...]).astype(dz_ref.dtype)

    rv = lambda i: nb - 1 - i
    return pl.pallas_call(
        body, grid=(nb,),
        in_specs=[pl.BlockSpec((1, 8, TQ), lambda i: (rv(i), 0, 0)), pl.BlockSpec((TQ, 128), lambda i: (rv(i), 0)),
                  pl.BlockSpec((TQ, 128), lambda i: (rv(i), NZ // 128 - 1)),
                  pl.BlockSpec((1, 128), lambda i: (0, 0)), pl.BlockSpec((TQ, 128), lambda i: (rv(i), 0)),
                  pl.BlockSpec(memory_space=pl.ANY)],
        out_specs=[pl.BlockSpec((TQ, 128), lambda i: (rv(i), _DZ_MISC)), pl.BlockSpec((1, 128), lambda i: (0, 0))],
        out_shape=[SDS(dz.shape, dz.dtype), SDS((1, 128), F32)], scratch_shapes=[pltpu.VMEM((1, 128), F32)],
        input_output_aliases={5: 0}, compiler_params=_cp("arbitrary"), name=name)(dcr, dcq, z, b_f, dkr, dz)


NEG = -1e30


TKV = 512


def _key_block(s):
    return min(TKV, s)


def _diag_mask(shape, off):
    r = lax.broadcasted_iota(jnp.int32, shape, 0)
    c = lax.broadcasted_iota(jnp.int32, shape, 1)
    return c + off >= r


def _head_lanes(h, dqk):
    return slice(128 * (h // 2), 128 * (h // 2) + 128) if dqk == HEAD_DIM else slice(128 * h, 128 * h + 128)


def _keep_half(x, a, axis):
    idx = lax.broadcasted_iota(jnp.int32, x.shape, axis)
    return jnp.where((idx < HEAD_DIM) if a == 0 else (idx >= HEAD_DIM), x, jnp.zeros_like(x))


def _scaled_qt(q, scale):
    qs = q.astype(F32) * (scale * LOG2E)
    return [qs[TQ * b:TQ * (b + 1)].T.astype(_MXU) for b in range(q.shape[0] // TQ)]


def _kv_prep(z, qcol, kcol, vcol, scale, name):
    s = z.shape[0]
    tk = _key_block(s)
    nk = s // tk

    def body(q_ref, k_ref, v_ref, kb_ref, vb_ref, vt_ref, qt_ref):
        kb_ref[...] = k_ref[...].astype(_MXU)
        v = v_ref[...]
        vb_ref[...] = v.astype(_MXU)
        vt_ref[0] = v.T.astype(_MXU)
        for b, t in enumerate(_scaled_qt(q_ref[...], scale)):
            qt_ref[b] = t

    blk = pl.BlockSpec((tk, GROUP), lambda i: (i, 0))
    col = lambda c: pl.BlockSpec((tk, GROUP), lambda i, c=c: (i, c))
    return pl.pallas_call(
        body, grid=(nk,), in_specs=[col(qcol), col(kcol), col(vcol)],
        out_specs=[blk, blk, pl.BlockSpec((1, GROUP, tk), lambda i: (i, 0, 0)),
                   pl.BlockSpec((tk // TQ, GROUP, TQ), lambda i: (i, 0, 0))],
        out_shape=[SDS((s, GROUP), _MXU), SDS((s, GROUP), _MXU), SDS((nk, GROUP, tk), _MXU), SDS((s // TQ, GROUP, TQ), _MXU)],
        compiler_params=_cp("parallel"), name=name)(z, z, z)


LOG2E = 1.4426950408889634


def _attn_fwd(q, qcol, dqk, kb, vt, scale, ck2, name, comm=None):
    s = q.shape[0]
    nq = s // TQ
    tk = _key_block(s)
    ratio = tk // TQ
    wq = N_HEADS * dqk
    bias = ck2 is not None

    def body(*refs):
        ins, (o_ref, l_ref), _, cc = _split_refs(refs, 4 if bias else 3, 2, comm)
        if bias:
            q_ref, k_ref, vt_ref, cc_ref = ins
        else:
            q_ref, k_ref, vt_ref = ins
        i = pl.program_id(0)
        _host_gather(comm, cc, i, nq)
        qts = []
        for h in range(N_HEADS):
            qt = (q_ref[:, _head_lanes(h, dqk)].astype(F32) * (scale * LOG2E)).T
            qts.append((_keep_half(qt, h % 2, 0) if dqk == HEAD_DIM else qt).astype(_MXU))

        def step(j, carry, off):
            r0 = pl.multiple_of(j * tk, tk)
            vtj = vt_ref[j]
            sts = [jnp.dot(k_ref[pl.ds(r0, tk), _head_lanes(h, dqk)], qts[h], preferred_element_type=F32)
                   for h in range(N_HEADS)]
            stats, ps = [], []
            for h in range(N_HEADS):
                m, l, _ = carry[3 * h:3 * h + 3]
                st = sts[h]
                if bias:
                    st = st - cc_ref[pl.ds(r0, tk), h:h + 1]
                if off is not None:
                    st = jnp.where(_diag_mask(st.shape, off), st, NEG)
                m_new = jnp.maximum(m, jnp.max(st, axis=0, keepdims=True))
                alpha = jnp.exp2(m - m_new)
                p = jnp.exp2(st - m_new)
                stats.append((m_new, alpha * l + jnp.sum(p, axis=0, keepdims=True), alpha))
                ps.append(p.astype(_MXU))
            out = []
            for h in range(N_HEADS):
                m_new, l, alpha = stats[h]
                acc = alpha * carry[3 * h + 2] + jnp.dot(vtj[HEAD_DIM * h:HEAD_DIM * (h + 1), :], ps[h],
                                                         preferred_element_type=F32)
                out += [m_new, l, acc]
            return tuple(out)

        init = (jnp.full((1, TQ), NEG, F32), jnp.zeros((1, TQ), F32), jnp.zeros((HEAD_DIM, TQ), F32)) * N_HEADS
        jd = i // ratio
        carry = lax.fori_loop(0, jd, functools.partial(step, off=None), init)
        carry = step(jd, carry, TQ * (i % ratio))
        l_ref[...] = jnp.zeros_like(l_ref)
        for h in range(N_HEADS):
            l_ref[0, h:h + 1, :] = carry[3 * h] + jnp.log2(carry[3 * h + 1])
        for p in range(2):
            ot = jnp.concatenate([carry[6 * p + 2] / carry[6 * p + 1], carry[6 * p + 5] / carry[6 * p + 4]], axis=0)
            o_ref[:, 128 * p:128 * (p + 1)] = ot.T
        if comm is not None:
            @pl.when(i == nq - 1)
            def _():
                comm.wait(*cc)

    rows = pl.BlockSpec((1, 8, TQ), lambda i: (i, 0, 0))
    in_specs = [pl.BlockSpec((TQ, wq), lambda i: (i, qcol)), pl.BlockSpec((s, wq), lambda i: (0, 0)),
                pl.BlockSpec((s // tk, GROUP, tk), lambda i: (0, 0, 0))]
    args = [q, kb, vt]
    if bias:
        in_specs.append(pl.BlockSpec((s, 128), lambda i: (0, 0)))
        args.append(ck2)
    out_specs = [pl.BlockSpec((TQ, GROUP), lambda i: (i, 0)), rows]
    out_shape = [SDS((s, GROUP), F32), SDS((nq, 8, TQ), F32)]
    return _call_with_comm(body, (nq,), in_specs, out_specs, out_shape, [], args, comm, ("arbitrary",), name)


def _call_with_comm(body, grid, in_specs, out_specs, out_shape, scratch, args, comm, semantics, name, aliases=None):
    n_out = len(out_shape)
    if comm is not None:
        in_specs, out_specs = in_specs + comm.in_specs, out_specs + comm.out_specs
        out_shape, scratch, args = out_shape + comm.out_shape, scratch + comm.scratch, list(args) + comm.arrs
    res = pl.pallas_call(body, grid=grid, in_specs=in_specs, out_specs=out_specs, out_shape=out_shape,
                         scratch_shapes=scratch, input_output_aliases=aliases or {}, compiler_params=_cp(*semantics),
                         name=name)(*args)
    return (*res[:n_out], list(res[n_out:]))


def _attn_bwd(kb, vb, qt, dot, lse, dl, dqk, scale, ck2, name, kv_dtype, comm=None, kv_into=None):
    s = kb.shape[0]
    nq = s // TQ
    tk = _key_block(s)
    ratio = tk // TQ
    nkb = s // tk
    wq = N_HEADS * dqk
    bias = ck2 is not None

    merged = kv_into is not None
    n_in = 6 + bias + merged
    n_out = 3 + 2 * bias - merged

    def body(*refs):
        ins, outs, _, cc = _split_refs(refs, n_in, n_out, comm)
        k_ref, v_ref, qt_ref, dot_ref, l_ref, d_ref = ins[:6]
        cc_ref = ins[6] if bias else None
        dqt_ref = outs[0]
        if merged:
            dk_ref, dv_ref = outs[1].at[:, 0:wq], outs[1].at[:, wq:wq + GROUP]
        else:
            dk_ref, dv_ref = outs[1], outs[2]
        if bias:
            dck_ref, dcq_ref = outs[-2:]
        j = pl.program_id(0)

        @pl.when(j == 0)
        def _():
            if comm is not None:
                comm.start(*cc)
            dqt_ref[...] = jnp.zeros_like(dqt_ref)
            if bias:
                dcq_ref[...] = jnp.zeros_like(dcq_ref)

        ks, kts, vs = [], [], []
        for h in range(N_HEADS):
            k2 = k_ref[:, _head_lanes(h, dqk)]
            if dqk == HEAD_DIM:
                k2 = _keep_half(k2, h % 2, 1)
            ks.append(k2)
            kts.append(k2.astype(F32).T.astype(_MXU))
            vs.append(_keep_half(v_ref[:, _head_lanes(h, HEAD_DIM)], h % 2, 1))
        cks = [cc_ref[:, h:h + 1] for h in range(N_HEADS)] if bias else None

        nt = (((1,), (1,)), ((), ()))

        def step(i, carry, off):
            qti, doti, li, di = qt_ref[i], dot_ref[i], l_ref[i], d_ref[i]
            qls = [_head_lanes(h, dqk) for h in range(N_HEADS)]
            vls = [_head_lanes(h, HEAD_DIM) for h in range(N_HEADS)]
            sts, dpts = [], []
            for h in range(N_HEADS):
                sts.append(jnp.dot(ks[h], qti[qls[h], :], preferred_element_type=F32))
                dpts.append(jnp.dot(vs[h], doti[vls[h], :], preferred_element_type=F32))
            pbs, dsbs, dcks = [], [], []
            for h in range(N_HEADS):
                st = sts[h] - li[h:h + 1, :]
                if bias:
                    st = st - cks[h]
                p = jnp.exp2(st)
                if off is not None:
                    p = jnp.where(_diag_mask(p.shape, off), p, 0.0)
                dst = p * (dpts[h] - di[h:h + 1, :])
                pbs.append(p.astype(_MXU))
                dsbs.append(dst.astype(_MXU))
                if bias:
                    dcks.append(carry[3 * h + 2] + jnp.sum(dst, axis=1, keepdims=True))
                    dcq_ref[i, h:h + 1, :] += jnp.sum(dst, axis=0, keepdims=True)
                else:
                    dcks.append(carry[3 * h + 2])
            out = []
            for h in range(N_HEADS):
                dvt = carry[3 * h + 1] + lax.dot_general(doti[HEAD_DIM * h:HEAD_DIM * (h + 1), :], pbs[h], nt,
                                                         preferred_element_type=F32)
                dkt = carry[3 * h] + lax.dot_general(qti[dqk * h:dqk * (h + 1), :], dsbs[h], nt, preferred_element_type=F32)
                dqt_ref[i, qls[h], :] += jnp.dot(kts[h], dsbs[h], preferred_element_type=F32) * scale
                out += [dkt, dvt, dcks[h]]
            return tuple(out)

        carry = (jnp.zeros((dqk, tk), F32), jnp.zeros((HEAD_DIM, tk), F32), jnp.zeros((tk, 1), F32)) * N_HEADS
        for r in range(ratio):
            carry = step(ratio * j + r, carry, TQ * r)
        carry = lax.fori_loop(ratio * (j + 1), nq, functools.partial(step, off=None), carry)
        for p in range(2):
            dv_ref[:, 128 * p:128 * (p + 1)] = jnp.concatenate([carry[6 * p + 1], carry[6 * p + 4]], axis=0).T.astype(dv_ref.dtype)
            if dqk == HEAD_DIM:
                dk_ref[:, 128 * p:128 * (p + 1)] = (jnp.concatenate([carry[6 * p], carry[6 * p + 3]], axis=0).T
                                                    * (1.0 / LOG2E)).astype(dk_ref.dtype)
        if dqk != HEAD_DIM:
            for h in range(N_HEADS):
                dk_ref[:, 128 * h:128 * (h + 1)] = (carry[3 * h].T * (1.0 / LOG2E)).astype(dk_ref.dtype)
        if bias:
            dck_ref[...] = jnp.zeros_like(dck_ref)
            for h in range(N_HEADS):
                dck_ref[:, h:h + 1] = -carry[3 * h + 2]
        if comm is not None:
            @pl.when(j == nkb - 1)
            def _():
                comm.wait(*cc)

    blk = lambda w: pl.BlockSpec((tk, w), lambda j: (j, 0))
    full3 = lambda w: pl.BlockSpec((nq, w, TQ), lambda j: (0, 0, 0))
    in_specs = [blk(wq), blk(GROUP), full3(wq), full3(GROUP), full3(8), full3(8)]
    args = [kb, vb, qt, dot, lse, dl]
    if merged:
        assert wq == GROUP
        out_specs = [full3(wq), pl.BlockSpec((tk, wq + GROUP), lambda j: (j, _DZ_FOX_KV))]
        out_shape = [SDS((nq, wq, TQ), F32), SDS(kv_into.shape, kv_into.dtype)]
    else:
        out_specs = [full3(wq), blk(wq), blk(GROUP)]
        out_shape = [SDS((nq, wq, TQ), F32), SDS((s, wq), kv_dtype), SDS((s, GROUP), kv_dtype)]
    if bias:
        in_specs.append(blk(128))
        args.append(ck2)
        out_specs += [blk(128), full3(8)]
        out_shape += [SDS((s, 128), F32), SDS((nq, 8, TQ), F32)]
    aliases = {}
    if merged:
        in_specs.append(pl.BlockSpec(memory_space=pl.ANY))
        args.append(kv_into)
        aliases = {len(args) - 1: 1}
    return _call_with_comm(body, (nkb,), in_specs, out_specs, out_shape, [], args, comm, ("arbitrary",), name, aliases)


def _untranspose(xt, dtype, name, into=None, col=0):
    nq, w, _ = xt.shape
    if into is not None:
        def body_into(x_ref, _, o_ref):
            o_ref[...] = x_ref[0].T.astype(o_ref.dtype)

        return pl.pallas_call(
            body_into, grid=(nq,),
            in_specs=[pl.BlockSpec((1, w, TQ), lambda i: (i, 0, 0)), pl.BlockSpec(memory_space=pl.ANY)],
            out_specs=pl.BlockSpec((TQ, w), lambda i: (i, col)), out_shape=SDS(into.shape, into.dtype),
            input_output_aliases={1: 0}, compiler_params=_cp("parallel"), name=name)(xt, into)

    def body(x_ref, o_ref):
        o_ref[...] = x_ref[0].T.astype(o_ref.dtype)

    return pl.pallas_call(
        body, grid=(nq,), in_specs=[pl.BlockSpec((1, w, TQ), lambda i: (i, 0, 0))],
        out_specs=pl.BlockSpec((TQ, w), lambda i: (i, 0)), out_shape=SDS((nq * TQ, w), dtype),
        compiler_params=_cp("parallel"), name=name)(xt)


_SCALE_D = (64 + 32) ** -0.5
_COL_CQ, _COL_CKV, _COL_MISC = 2304 // 256, 2560 // 128, 2688 // 128


def _mla_prep(z, gq, gkv, wq, wk, wv, tb, name):
    s = z.shape[0]
    tm = _key_block(s)
    row = lambda w, c: pl.BlockSpec((tm, w), lambda i, c=c: (i, c))
    const = lambda a: pl.BlockSpec(a.shape, lambda i: (0,) * a.ndim)

    def body(cq_ref, ckv_ref, m_ref, gq_ref, gkv_ref, wq_ref, wk_ref, wv_ref, e_ref, qc_ref, qs_ref, kc_ref, ks_ref,
             q_ref, k_ref, v_ref, vt_ref, cqn_ref, ckvn_ref, qt_ref):
        cqn = _rms(cq_ref[...], gq_ref[...]).astype(_MXU)
        ckvn = _rms(ckv_ref[...], gkv_ref[...]).astype(_MXU)
        cqn_ref[...] = cqn
        ckvn_ref[...] = ckvn
        qb = _rope(_dot(cqn, wq_ref[...]), _lanes(qc_ref[...], 512), _lanes(qs_ref[...], 512), 16).astype(q_ref.dtype)
        q_ref[...] = qb
        for b, t in enumerate(_scaled_qt(qb, _SCALE_D)):
            qt_ref[b] = t
        kr = _rope(m_ref[...], kc_ref[...], ks_ref[...], 16)
        k_ref[...] = (_dot(ckvn, wk_ref[...]) + _dot(kr, e_ref[...])).astype(k_ref.dtype)
        v = _dot(ckvn, wv_ref[...])
        v_ref[...] = v.astype(v_ref.dtype)
        vt_ref[0] = v.T.astype(vt_ref.dtype)

    e = tb["place"]
    return pl.pallas_call(
        body, grid=(s // tm,),
        in_specs=[row(256, _COL_CQ), row(128, _COL_CKV), row(128, _COL_MISC), const(gq), const(gkv), const(wq), const(wk),
                  const(wv), const(e), row(128, 0), row(128, 0), row(128, 0), row(128, 0)],
        out_specs=[row(512, 0), row(512, 0), row(256, 0), pl.BlockSpec((1, GROUP, tm), lambda i: (i, 0, 0)), row(256, 0),
                   row(128, 0), pl.BlockSpec((tm // TQ, 512, TQ), lambda i: (i, 0, 0))],
        out_shape=[SDS((s, 512), _MXU), SDS((s, 512), _MXU), SDS((s, 256), _MXU), SDS((s // tm, GROUP, tm), _MXU),
                   SDS((s, 256), _MXU), SDS((s, 128), _MXU), SDS((s // TQ, 512, TQ), _MXU)],
        compiler_params=_cp("parallel"), name=name)(
            z, z, z, gq, gkv, wq, wk, wv, e, tb["q_cos"], tb["q_sin"], tb["k_cos"], tb["k_sin"])


def _mla_prep_bwd(dqt, dk, dv, z, cqn, ckvn, gq, gkv, wq, wk, wv, tb, dz, name):
    s = z.shape[0]
    tm = min(512, s)
    row = lambda w, c: pl.BlockSpec((tm, w), lambda i, c=c: (i, c))
    const = lambda a: pl.BlockSpec(a.shape, lambda i: (0,) * a.ndim)
    acc = lambda shape: pl.BlockSpec(shape, lambda i: (0, 0))

    def body(dq_ref, dk_ref, dv_ref, cq_ref, ckv_ref, cqn_ref, ckvn_ref, gq_ref, gkv_ref, wq_ref, wk_ref, wv_ref, e_ref,
             qc_ref, qs_ref, kc_ref, ks_ref, _, dz_ref, dkr_ref, dwq_ref, dwk_ref, dwv_ref, dgq_ref, dgkv_ref):
        dcq_ref, dckv_ref = dz_ref.at[:, 0:256], dz_ref.at[:, 256:384]

        @pl.when(pl.program_id(0) == 0)
        def _():
            for r in (dwq_ref, dwk_ref, dwv_ref, dgq_ref, dgkv_ref):
                r[...] = jnp.zeros_like(r)

        dq = jnp.concatenate([dq_ref[b].T for b in range(tm // TQ)], axis=0)
        dqp = _rope_bwd(dq, _lanes(qc_ref[...], 512), _lanes(qs_ref[...], 512), 16)
        dkd = dk_ref[...]
        dvd = dv_ref[...]
        dwq_ref[...] += _dot_tn(cqn_ref[...], dqp)
        dwk_ref[...] += _dot_tn(ckvn_ref[...], dkd)
        dwv_ref[...] += _dot_tn(ckvn_ref[...], dvd)
        dcq, dgq = _rms_bwd(cq_ref[...], gq_ref[...], _dot_nt(dqp, wq_ref[...]))
        dckv, dgkv = _rms_bwd(ckv_ref[...], gkv_ref[...], _dot_nt(dkd, wk_ref[...]) + _dot_nt(dvd, wv_ref[...]))
        dcq_ref[...] = dcq.astype(dcq_ref.dtype)
        dckv_ref[...] = dckv.astype(dckv_ref.dtype)
        dgq_ref[...] += dgq
        dgkv_ref[...] += dgkv
        dkr = _dot_exact(dkd, e_ref[...], (((1,), (1,)), ((), ())))
        dkr_ref[...] = _rope_bwd(dkr, kc_ref[...], ks_ref[...], 16)

    e = tb["place"]
    return pl.pallas_call(
        body, grid=(s // tm,),
        in_specs=[pl.BlockSpec((tm // TQ, 512, TQ), lambda i: (i, 0, 0)), row(512, 0), row(256, 0), row(256, _COL_CQ),
                  row(128, _COL_CKV), row(256, 0), row(128, 0),
                  const(gq), const(gkv), const(wq), const(wk), const(wv), const(e), row(128, 0), row(128, 0), row(128, 0), row(128, 0),
                  pl.BlockSpec(memory_space=pl.ANY)],
        out_specs=[row(384, _DZ_MLA), row(128, 0), acc((256, 512)), acc((128, 512)), acc((128, 256)), acc((1, 256)),
                   acc((1, 128))],
        out_shape=[SDS(dz.shape, dz.dtype), SDS((s, 128), F32), SDS((256, 512), F32), SDS((128, 512), F32),
                   SDS((128, 256), F32), SDS((1, 256), F32), SDS((1, 128), F32)],
        input_output_aliases={17: 0}, compiler_params=_cp("arbitrary"), name=name)(
            dqt, dk, dv, z, z, cqn, ckvn, gq, gkv, wq, wk, wv, e, tb["q_cos"], tb["q_sin"], tb["k_cos"], tb["k_sin"], dz)


def _out_proj(ys, g, w, x, name):
    s, d = x.shape
    tm = min(512, s)

    def body(ya, yb, yc, yd, g_ref, w_ref, x_ref, o_ref, yn_ref):
        acc = x_ref[...]
        for i, y_ref in enumerate((ya, yb, yc, yd)):
            sl = slice(GROUP * i, GROUP * (i + 1))
            yn = _rms(y_ref[...], g_ref[:, sl]).astype(_MXU)
            yn_ref[:, sl] = yn
            acc = acc + jnp.dot(yn, w_ref[sl, :], preferred_element_type=F32)
        o_ref[...] = acc

    yspec = pl.BlockSpec((tm, GROUP), lambda i: (i, 0))
    return pl.pallas_call(
        body, grid=(s // tm,),
        in_specs=[yspec, yspec, yspec, yspec, pl.BlockSpec((1, d), lambda i: (0, 0)), pl.BlockSpec((d, d), lambda i: (0, 0)),
                  pl.BlockSpec((tm, d), lambda i: (i, 0))],
        out_specs=[pl.BlockSpec((tm, d), lambda i: (i, 0)), pl.BlockSpec((tm, d), lambda i: (i, 0))],
        out_shape=[SDS((s, d), F32), SDS((s, d), _MXU)], compiler_params=_cp("parallel"), name=name)(*ys, g, w, x)


def _out_proj_bwd(dx, w, ys, g, name):
    s, d = dx.shape
    tm = min(512, s)
    nb = tm // TQ

    def body(dx_ref, w_ref, ya, yb, yc, yd, g_ref, da, db, dg_ref, dtc_ref, dtd_ref, dlc_ref, dld_ref):
        @pl.when(pl.program_id(0) == 0)
        def _():
            dg_ref[...] = jnp.zeros_like(dg_ref)

        dyn = _dot_nt(dx_ref[...], w_ref[...])
        for i, y_ref in enumerate((ya, yb, yc, yd)):
            sl = slice(GROUP * i, GROUP * (i + 1))
            y = y_ref[...]
            dy, dg = _rms_bwd(y, g_ref[:, sl], dyn[:, sl])
            dg_ref[:, sl] += dg
            if i < 2:
                (da, db)[i][...] = dy
                continue
            dt_ref, dl_ref = ((dtc_ref, dlc_ref), (dtd_ref, dld_ref))[i - 2]
            dl_ref[...] = jnp.zeros_like(dl_ref)
            for b in range(nb):
                r = slice(TQ * b, TQ * (b + 1))
                dt_ref[b] = dy[r].T.astype(dt_ref.dtype)
                pt = (dy[r] * y[r]).T
                for h in range(N_HEADS):
                    dl_ref[b, h:h + 1, :] = jnp.sum(pt[HEAD_DIM * h:HEAD_DIM * (h + 1), :], axis=0, keepdims=True)

    yspec = pl.BlockSpec((tm, GROUP), lambda i: (i, 0))
    tspec = pl.BlockSpec((nb, GROUP, TQ), lambda i: (i, 0, 0))
    lspec = pl.BlockSpec((nb, 8, TQ), lambda i: (i, 0, 0))
    return pl.pallas_call(
        body, grid=(s // tm,),
        in_specs=[pl.BlockSpec((tm, d), lambda i: (i, 0)), pl.BlockSpec((d, d), lambda i: (0, 0)), yspec, yspec, yspec, yspec,
                  pl.BlockSpec((1, d), lambda i: (0, 0))],
        out_specs=[yspec, yspec, pl.BlockSpec((1, d), lambda i: (0, 0)), tspec, tspec, lspec, lspec],
        out_shape=[SDS((s, GROUP), F32)] * 2 + [SDS((1, d), F32)] + [SDS((s // TQ, GROUP, TQ), _MXU)] * 2
        + [SDS((s // TQ, 8, TQ), F32)] * 2,
        compiler_params=_cp("arbitrary"), name=name)(dx, w, *ys, g)


FF_BLOCK = 512
FF_ROWS = 1024


def _ffn_fwd(x, g, wu, wd, name, comm=None):
    s, d = x.shape
    nj = wu.shape[0]
    tm = min(FF_ROWS, s)
    ni = s // tm

    def body(*refs):
        (x_ref, g_ref, wu_ref, wd_ref), (o_ref, u_ref, h_ref), (acc,), cc = _split_refs(refs, 4, 3, comm)
        i, j = pl.program_id(0), pl.program_id(1)
        _host_gather(comm, cc, i * nj + j, ni * nj)

        @pl.when(j == 0)
        def _():
            h_ref[...] = _rms(x_ref[...], g_ref[...]).astype(h_ref.dtype)
            acc[...] = jnp.zeros_like(acc)

        halves = [slice(r, r + tm // 2) for r in range(0, tm, tm // 2)]
        us = [jnp.dot(h_ref[r, :], wu_ref[0], preferred_element_type=F32) for r in halves]
        for r, u in zip(halves, us):
            u_ref[r, :] = u.astype(u_ref.dtype)
            acc[r, :] += _dot(jnp.square(jnp.maximum(u, 0.0)), wd_ref[...])

        @pl.when(j == nj - 1)
        def _():
            o_ref[...] = x_ref[...] + acc[...]

        if comm is not None:
            @pl.when((i == ni - 1) & (j == nj - 1))
            def _():
                comm.wait(*cc)

    in_specs = [pl.BlockSpec((tm, d), lambda i, j: (i, 0)), pl.BlockSpec((1, d), lambda i, j: (0, 0)),
                pl.BlockSpec((1, d, FF_BLOCK), lambda i, j: (j, 0, 0)), pl.BlockSpec((FF_BLOCK, d), lambda i, j: (j, 0))]
    out_specs = [pl.BlockSpec((tm, d), lambda i, j: (i, 0)), pl.BlockSpec((tm, FF_BLOCK), lambda i, j: (i, j)),
                 pl.BlockSpec((tm, d), lambda i, j: (i, 0))]
    out_shape = [SDS((s, d), F32), SDS((s, nj * FF_BLOCK), _MXU), SDS((s, d), _MXU)]
    return _call_with_comm(body, (ni, nj), in_specs, out_specs, out_shape, [pltpu.VMEM((tm, d), F32)], [x, g, wu, wd], comm,
                           ("arbitrary", "arbitrary"), name)


def _ffn_bwd(dx2, x, u, g, wu, wd, name, comm=None):
    s, d = x.shape
    nj = wu.shape[0]
    tm = min(FF_ROWS, s)
    ni = s // tm

    def body(*refs):
        (dx_ref, x_ref, u_ref, g_ref, wu_ref, wd_ref), (o_ref, du_ref, dg_ref), (acc, dxb), cc = _split_refs(refs, 6, 3, comm)
        i, j = pl.program_id(0), pl.program_id(1)

        @pl.when((i == 0) & (j == 0))
        def _():
            if comm is not None:
                comm.start(*cc)
            dg_ref[...] = jnp.zeros_like(dg_ref)

        @pl.when(j == 0)
        def _():
            dxb[...] = dx_ref[...].astype(dxb.dtype)
            acc[...] = jnp.zeros_like(acc)

        nt = (((1,), (1,)), ((), ()))
        halves = [slice(r, r + tm // 2) for r in range(0, tm, tm // 2)]
        das = [lax.dot_general(dxb[r, :], wd_ref[...], nt, preferred_element_type=F32) for r in halves]
        for r, da in zip(halves, das):
            du = (da * 2.0 * jnp.maximum(u_ref[r, :].astype(F32), 0.0)).astype(du_ref.dtype)
            du_ref[r, :] = du
            acc[r, :] += lax.dot_general(du, wu_ref[0], nt, preferred_element_type=F32)

        @pl.when(j == nj - 1)
        def _():
            dxn, dg = _rms_bwd(x_ref[...], g_ref[...], acc[...])
            o_ref[...] = dx_ref[...] + dxn
            dg_ref[...] += dg

        if comm is not None:
            @pl.when((i == ni - 1) & (j == nj - 1))
            def _():
                comm.wait(*cc)

    in_specs = [pl.BlockSpec((tm, d), lambda i, j: (i, 0)), pl.BlockSpec((tm, d), lambda i, j: (i, 0)),
                pl.BlockSpec((tm, FF_BLOCK), lambda i, j: (i, j)), pl.BlockSpec((1, d), lambda i, j: (0, 0)),
                pl.BlockSpec((1, d, FF_BLOCK), lambda i, j: (j, 0, 0)), pl.BlockSpec((FF_BLOCK, d), lambda i, j: (j, 0))]
    out_specs = [pl.BlockSpec((tm, d), lambda i, j: (i, 0)), pl.BlockSpec((tm, FF_BLOCK), lambda i, j: (i, j)),
                 pl.BlockSpec((1, d), lambda i, j: (0, 0))]
    out_shape = [SDS((s, d), F32), SDS((s, nj * FF_BLOCK), _MXU), SDS((1, d), F32)]
    return _call_with_comm(body, (ni, nj), in_specs, out_specs, out_shape,
                           [pltpu.VMEM((tm, d), F32), pltpu.VMEM((tm, d), _MXU)], [dx2, x, u, g, wu, wd], comm,
                           ("arbitrary", "arbitrary"), name)


def _in_proj_bwd(dz, w, x, g, dx_up, name, comm=None):
    s, d = x.shape
    n = w.shape[1]
    tm = min(512, s)
    ni = s // tm

    def body(*refs):
        (dz_ref, w_ref, x_ref, g_ref, up_ref), (o_ref, dg_ref), _, cc = _split_refs(refs, 5, 2, comm)
        i = pl.program_id(0)

        @pl.when(i == 0)
        def _():
            if comm is not None:
                comm.start(*cc)
            dg_ref[...] = jnp.zeros_like(dg_ref)

        dh = lax.dot_general(dz_ref[...], w_ref[...], (((1,), (1,)), ((), ())), preferred_element_type=F32)
        dxn, dg = _rms_bwd(x_ref[...], g_ref[...], dh)
        o_ref[...] = up_ref[...] + dxn
        dg_ref[...] += dg
        if comm is not None:
            @pl.when(i == ni - 1)
            def _():
                comm.wait(*cc)

    in_specs = [pl.BlockSpec((tm, n), lambda i: (i, 0)), pl.BlockSpec((d, n), lambda i: (0, 0)),
                pl.BlockSpec((tm, d), lambda i: (i, 0)), pl.BlockSpec((1, d), lambda i: (0, 0)),
                pl.BlockSpec((tm, d), lambda i: (i, 0))]
    out_specs = [pl.BlockSpec((tm, d), lambda i: (i, 0)), pl.BlockSpec((1, d), lambda i: (0, 0))]
    out_shape = [SDS((s, d), F32), SDS((1, d), F32)]
    return _call_with_comm(body, (ni,), in_specs, out_specs, out_shape, [], [dz, w, x, g, dx_up], comm, ("arbitrary",), name)


def _loss_head(x, g, target, name):
    s, d = x.shape
    tm = min(512, s)

    def body(x_ref, g_ref, t_ref, l_ref, dx_ref, dg_ref):
        @pl.when(pl.program_id(0) == 0)
        def _():
            l_ref[...] = jnp.zeros_like(l_ref)
            dg_ref[...] = jnp.zeros_like(dg_ref)

        xv = x_ref[...]
        err = _rms(xv, g_ref[...]) - t_ref[...]
        l_ref[...] += jnp.sum(err * err, axis=0, keepdims=True) * (0.5 / d)
        dx, dg = _rms_bwd(xv, g_ref[...], err * (1.0 / d))
        dx_ref[...] = dx
        dg_ref[...] += dg

    return pl.pallas_call(
        body, grid=(s // tm,),
        in_specs=[pl.BlockSpec((tm, d), lambda i: (i, 0)), pl.BlockSpec((1, d), lambda i: (0, 0)),
                  pl.BlockSpec((tm, d), lambda i: (i, 0))],
        out_specs=[pl.BlockSpec((1, d), lambda i: (0, 0)), pl.BlockSpec((tm, d), lambda i: (i, 0)),
                   pl.BlockSpec((1, d), lambda i: (0, 0))],
        out_shape=[SDS((1, d), F32), SDS((s, d), F32), SDS((1, d), F32)], compiler_params=_cp("arbitrary"), name=name)(x, g, target)


def _me_and_peer():
    x, y, c = lax.axis_index("x"), lax.axis_index("y"), lax.axis_index("c")
    me = 4 * x + 2 * y + c

    def peer(k):
        px, py, pc = x ^ (k >> 2), y ^ ((k >> 1) & 1), c ^ (k & 1)
        return (px, py, pc), 4 * px + 2 * py + pc

    return me, peer


class _Comm:
    CHIPS = (2, 4, 6)

    def __init__(self, kind, arrs):
        assert kind in ("gather", "exchange")
        self.kind, self.arrs, self.n = kind, list(arrs), len(arrs)
        anyspec = pl.BlockSpec(memory_space=pl.ANY)
        self.in_specs = [anyspec] * self.n
        self.out_specs = [anyspec] * self.n
        self.out_shape = [SDS(((NDEV,) + a.shape) if kind == "gather" else a.shape, a.dtype) for a in self.arrs]
        npair = NDEV - 1 + len(self.CHIPS)
        self.scratch = [pltpu.SemaphoreType.DMA((self.n, npair)), pltpu.SemaphoreType.DMA((self.n, npair)),
                        pltpu.SemaphoreType.DMA((self.n,))]

    def _copies(self, ins, outs, sems):
        send, recv, loc = sems
        me, peer = _me_and_peer()
        gather = self.kind == "gather"
        sibling = peer(1)[0]
        local = [pltpu.make_async_copy(ins[a] if gather else ins[a].at[me], outs[a].at[me], loc.at[a]) for a in range(self.n)]
        outgoing, incoming, forwards, forwarded = [], [], [], []
        for k in ((1,) + self.CHIPS) if gather else range(1, NDEV):
            dev, pid = peer(k)
            for a in range(self.n):
                pair = dict(send_sem=send.at[a, k - 1], recv_sem=recv.at[a, k - 1], device_id=dev, device_id_type=MESH)
                outgoing.append(pltpu.make_async_remote_copy(src_ref=ins[a] if gather else ins[a].at[pid],
                                                             dst_ref=outs[a].at[me], **pair))
                incoming.append(pltpu.make_async_remote_copy(src_ref=ins[a] if gather else ins[a].at[me],
                                                             dst_ref=outs[a].at[pid], **pair))
        if gather:
            for idx, k in enumerate(self.CHIPS):
                got, theirs = peer(k)[1], peer(k + 1)[1]
                for a in range(self.n):
                    pair = dict(send_sem=send.at[a, NDEV - 1 + idx], recv_sem=recv.at[a, NDEV - 1 + idx], device_id=sibling,
                                device_id_type=MESH)
                    forwards.append(pltpu.make_async_remote_copy(src_ref=outs[a].at[got], dst_ref=outs[a].at[got], **pair))
                    forwarded.append(pltpu.make_async_remote_copy(src_ref=outs[a].at[theirs], dst_ref=outs[a].at[theirs], **pair))
        return local, outgoing, incoming, forwards, forwarded

    def start(self, ins, outs, sems):
        local, outgoing, _, _, _ = self._copies(ins, outs, sems)
        for cp in local + outgoing:
            cp.start()

    def forward(self, ins, outs, sems):
        _, _, incoming, forwards, _ = self._copies(ins, outs, sems)
        per = self.n
        for idx in range(len(forwards) // per if per else 0):
            for a in range(per):
                incoming[(1 + idx) * per + a].wait_recv()
                forwards[idx * per + a].start()

    def wait(self, ins, outs, sems):
        local, outgoing, incoming, forwards, forwarded = self._copies(ins, outs, sems)
        for cp in (incoming[:self.n] if self.kind == "gather" else incoming) + forwarded:
            cp.wait_recv()
        for cp in outgoing + forwards:
            cp.wait_send()
        for cp in local:
            cp.wait()


def _host_gather(comm, cc, step, nsteps, late=False):
    if comm is None:
        return

    @pl.when(step == 0)
    def _():
        comm.start(*cc)

    @pl.when(step == (nsteps - 1 if late else (2 * nsteps) // 3))
    def _():
        comm.forward(*cc)


def _split_refs(refs, n_in, n_out, comm):
    c = comm.n if comm is not None else 0
    ins, cin = refs[:n_in], refs[n_in:n_in + c]
    outs, cout = refs[n_in + c:n_in + c + n_out], refs[n_in + c + n_out:n_in + 2 * c + n_out]
    rest = refs[n_in + 2 * c + n_out:]
    scratch, csem = (rest[:len(rest) - 3], rest[len(rest) - 3:]) if c else (rest, ())
    return ins, outs, scratch, (cin, cout, csem)


def _comm_call(kind, arrs, name):
    comm = _Comm(kind, arrs)

    def body(*refs):
        _, _, _, c = _split_refs(refs, 0, 0, comm)
        comm.start(*c)
        if kind == "gather":
            comm.forward(*c)
        comm.wait(*c)

    return pl.pallas_call(body, in_specs=comm.in_specs, out_specs=comm.out_specs, out_shape=comm.out_shape,
                          scratch_shapes=comm.scratch, compiler_params=pltpu.CompilerParams(has_side_effects=True),
                          name=name)(*arrs)


def _all_gather(arrs, name):
    return _comm_call("gather", arrs, name)


def _exchange(arrs, name):
    return _comm_call("exchange", arrs, name)


def _sum_slots(parts, name):
    _, r, c = parts.shape
    tr = r if r <= 512 else 512

    def body(p_ref, o_ref):
        acc = p_ref[0].astype(F32)
        for q in range(1, NDEV):
            acc = acc + p_ref[q].astype(F32)
        o_ref[...] = acc

    return pl.pallas_call(
        body, grid=(r // tr,), in_specs=[pl.BlockSpec((NDEV, tr, c), lambda i: (0, i, 0))],
        out_specs=pl.BlockSpec((tr, c), lambda i: (i, 0)), out_shape=SDS((r, c), F32),
        compiler_params=_cp("parallel"), name=name)(parts)


def _adamw(g, w, m, v, name):
    r, c = w.shape
    parts = g.ndim == 3
    tr = r
    for cand in (512, 256, 128, 64, 32, 16, 8):
        if r > cand and r % cand == 0 and cand * c * 4 <= 2 * 1024 * 1024:
            tr = cand
            break
    bc1 = 1.0 / (1.0 - ADAM_B1 ** ADAM_STEP)
    bc2 = 1.0 / (1.0 - ADAM_B2 ** ADAM_STEP)

    def body(g_ref, w_ref, m_ref, v_ref, go_ref, d_ref, mo_ref, vo_ref):
        if parts:
            gv = g_ref[0].astype(F32)
            for q in range(1, NDEV):
                gv = gv + g_ref[q].astype(F32)
        else:
            gv = g_ref[...]
        mn = ADAM_B1 * m_ref[...] + (1.0 - ADAM_B1) * gv
        vn = ADAM_B2 * v_ref[...] + (1.0 - ADAM_B2) * (gv * gv)
        go_ref[...] = gv
        mo_ref[...] = mn
        vo_ref[...] = vn
        d_ref[...] = -ADAM_LR * ((mn * bc1) / (jnp.sqrt(vn * bc2) + ADAM_EPS) + ADAM_WD * w_ref[...])

    spec = pl.BlockSpec((tr, c), lambda i: (i, 0))
    gspec = pl.BlockSpec((NDEV, tr, c), lambda i: (0, i, 0)) if parts else spec
    return pl.pallas_call(
        body, grid=(r // tr,), in_specs=[gspec, spec, spec, spec], out_specs=[spec] * 4,
        out_shape=[SDS((r, c), F32)] * 4, compiler_params=_cp("parallel"), name=name)(g, w, m, v)


def _adamw_layer(parts, w, m, v, l, prev, name):
    r, c = parts.shape[1:]
    rows = w.shape[0]
    tr = next(t for t in (512, 256, 128, 64, 32, 16, 8) if r % t == 0 and t * c * 4 <= 2 * 1024 * 1024)
    bc1 = 1.0 / (1.0 - ADAM_B1 ** ADAM_STEP)
    bc2 = 1.0 / (1.0 - ADAM_B2 ** ADAM_STEP)

    def body(g_ref, w_ref, m_ref, v_ref, *rest):
        go_ref, d_ref, mo_ref, vo_ref = rest[-4:]
        gv = g_ref[0].astype(F32)
        for q in range(1, NDEV):
            gv = gv + g_ref[q].astype(F32)
        mn = ADAM_B1 * m_ref[...] + (1.0 - ADAM_B1) * gv
        vn = ADAM_B2 * v_ref[...] + (1.0 - ADAM_B2) * (gv * gv)
        go_ref[...] = gv
        mo_ref[...] = mn
        vo_ref[...] = vn
        d_ref[...] = -ADAM_LR * ((mn * bc1) / (jnp.sqrt(vn * bc2) + ADAM_EPS) + ADAM_WD * w_ref[...])

    spec = pl.BlockSpec((tr, c), lambda i: (l * (r // tr) + i, 0))
    in_specs = [pl.BlockSpec((NDEV, tr, c), lambda i: (0, i, 0)), spec, spec, spec]
    args = [parts, w, m, v]
    aliases = {}
    if prev is not None:
        in_specs += [pl.BlockSpec(memory_space=pl.ANY)] * 4
        args += list(prev)
        aliases = {4 + k: k for k in range(4)}
    return pl.pallas_call(
        body, grid=(r // tr,), in_specs=in_specs, out_specs=[spec] * 4, out_shape=[SDS((rows, c), F32)] * 4,
        input_output_aliases=aliases, compiler_params=_cp("parallel"), name=name)(*args)


def _pad_in_cols(w):
    r = w.shape[0]
    zeros = lambda n: jnp.zeros((r, n), w.dtype)
    return jnp.concatenate([w[:, 512:1536], w[:, 0:512], w[:, 1792:2304], w[:, 1536:1792], w[:, 2308:2692], w[:, 2304:2308],
                            zeros(28), w[:, 2692:2724], zeros(64)], axis=1)


def _unpad_in_cols(w):
    return jnp.concatenate([w[..., 1024:1536], w[..., 0:1024], w[..., 2048:2304], w[..., 1536:2048], w[..., 2688:2692],
                            w[..., 2304:2688], w[..., 2720:2752]], axis=-1)


_Z_RET = (0, 1, 2, 3)
_Z_SGU = (4, 5)
_Z_FOX_Q, _Z_FOX_K, _Z_FOX_V = 8, 6, 7
_DZ_RET, _DZ_SGU, _DZ_FOX_KV, _DZ_FOX_Q, _DZ_MLA, _DZ_MISC = 0, 2, 3, 8, 6, 21


def _pad_uq(w):
    return jnp.pad(w.reshape(256, N_HEADS, 96), ((0, 0), (0, 0), (0, 32))).reshape(256, 512)


def _unpad_uq(w):
    return w.reshape(256, N_HEADS, 128)[:, :, :96].reshape(256, 384)


def _split_ukv(w):
    r = w.reshape(128, N_HEADS, 128)
    return jnp.pad(r[:, :, :64], ((0, 0), (0, 0), (0, 64))).reshape(128, 512), r[:, :, 64:].reshape(128, 256)


def _join_ukv(dk, dv):
    return jnp.concatenate([dk.reshape(128, N_HEADS, 128)[:, :, :64], dv.reshape(128, N_HEADS, 64)], axis=-1).reshape(128, 512)


def _cols_to_full(g):
    return jnp.transpose(g, (1, 0, 2)).reshape(g.shape[1], NDEV * g.shape[2])


def kernel(x, g_mix_norm, w_in, b_forget, g_sgu, w_spatial, b_spatial, g_mla_q, w_uq, g_mla_kv, w_ukv, g_group_out, w_out, g_ffn_norm, w_up, w_down, g_final, loss_target, m_g_mix_norm, m_w_in, m_b_forget, m_g_sgu, m_w_spatial, m_b_spatial, m_g_mla_q, m_w_uq, m_g_mla_kv, m_w_ukv, m_g_group_out, m_w_out, m_g_ffn_norm, m_w_up, m_w_down, m_g_final, v_g_mix_norm, v_w_in, v_b_forget, v_g_sgu, v_w_spatial, v_b_spatial, v_g_mla_q, v_w_uq, v_g_mla_kv, v_w_ukv, v_g_group_out, v_w_out, v_g_ffn_norm, v_w_up, v_w_down, v_g_final):
    depth = w_in.shape[0]
    s, d = x.shape[1], x.shape[2]
    x0 = x.reshape(s, d)
    target = loss_target.reshape(s, d)
    tb = _tables(s)
    me = 4 * lax.axis_index("x") + 2 * lax.axis_index("y") + lax.axis_index("c")

    assert depth == 2
    shards = {}
    for l in range(depth):
        shards.update({(l, "w_in"): _pad_in_cols(w_in[l]).astype(_WIRE), (l, "w_out"): w_out[l].astype(_WIRE),
                       (l, "w_up"): w_up[l].astype(_WIRE), (l, "w_down"): w_down[l].astype(_WIRE),
                       (l, "w_uq"): w_uq[l].astype(_WIRE), (l, "w_ukv"): w_ukv[l].astype(_WIRE)})
    wts = _ShardedWeights(shards)
    first = [(0, "w_in"), (0, "w_uq"), (0, "w_ukv"), (1, "w_uq"), (1, "w_ukv")]
    wts.full.update(zip(first, _all_gather([shards[k] for k in first], "gather_first")))

    row = lambda a: a.reshape(1, -1)

    def small(l):
        bf = jnp.pad(b_forget[l].reshape(1, N_HEADS), ((0, 0), (0, 128 - N_HEADS)))
        bt = jnp.pad(b_spatial[l].T, ((0, 0), (0, 128 - N_HEADS)))
        return dict(g_mix=row(g_mix_norm[l]), g_sgu=row(g_sgu[l]), w_s=w_spatial[l], b_t=bt, b_f=bf, gq=row(g_mla_q[l]),
                    gkv=row(g_mla_kv[l]), g_go=row(g_group_out[l]), g_ffn=row(g_ffn_norm[l]))

    smalls = [small(l) for l in range(depth)]
    lrow, dx, sm, dg_final = _local_step(x0, target, wts, smalls, row(g_final), tb)
    loss = lax.psum(jnp.sum(lrow), AXES)
    grad_x = dx.reshape(1, s, d)
    return _reduce_and_update(loss, grad_x, wts.recv, sm, dg_final, me, dict(
        g_mix_norm=(g_mix_norm, m_g_mix_norm, v_g_mix_norm), w_in=(w_in, m_w_in, v_w_in),
        b_forget=(b_forget, m_b_forget, v_b_forget), g_sgu=(g_sgu, m_g_sgu, v_g_sgu),
        w_spatial=(w_spatial, m_w_spatial, v_w_spatial), b_spatial=(b_spatial, m_b_spatial, v_b_spatial),
        g_mla_q=(g_mla_q, m_g_mla_q, v_g_mla_q), w_uq=(w_uq, m_w_uq, v_w_uq), g_mla_kv=(g_mla_kv, m_g_mla_kv, v_g_mla_kv),
        w_ukv=(w_ukv, m_w_ukv, v_w_ukv), g_group_out=(g_group_out, m_g_group_out, v_g_group_out),
        w_out=(w_out, m_w_out, v_w_out), g_ffn_norm=(g_ffn_norm, m_g_ffn_norm, v_g_ffn_norm), w_up=(w_up, m_w_up, v_w_up),
        w_down=(w_down, m_w_down, v_w_down), g_final=(g_final, m_g_final, v_g_final)))


_GATHER_AT = {
    "in_proj0": [(0, "w_out")],
    "fox_attn0": [(0, "w_down")],
    "mla_attn0": [(0, "w_up"), (1, "w_in")],
    "ffn_fwd0": [(1, "w_down")],
    "fox_attn1": [(1, "w_out")],
    "mla_attn1": [(1, "w_up")],
}
_SCATTER_AT = {
    "fox_attn_bwd1": [(1, "w_down")],
    "mla_attn_bwd1": [(1, "w_up"), (1, "w_out")],
    "ffn_bwd0": [(1, "w_in")],
    "fox_attn_bwd0": [(0, "w_down")],
    "mla_attn_bwd0": [(0, "w_up"), (0, "w_out")],
    "in_proj_bwd0": [(0, "w_in")],
}


class _FullWeights:
    def __init__(self, per_layer):
        self.per_layer, self.grads = per_layer, {}

    def get(self, l, name):
        return self.per_layer[l][name]

    def comm(self, host):
        return None

    def done(self, host, results):
        pass

    def grad(self, l, name, blocks):
        self.grads[(l, name)] = blocks


class _ShardedWeights(_FullWeights):
    def __init__(self, shards):
        self.shards, self.full, self.grads, self.recv = shards, {}, {}, {}

    def get(self, l, name):
        if name in ("wk", "wv"):
            return _split_ukv(_cols_to_full(self.full[(l, "w_ukv")]))[0 if name == "wk" else 1]
        if name == "wq":
            return _pad_uq(_cols_to_full(self.full[(l, "w_uq")]))
        g = self.full[(l, name)]
        return g if name == "w_up" else g.reshape(NDEV * g.shape[1], g.shape[2])

    def comm(self, host):
        if host in _GATHER_AT:
            return _Comm("gather", [self.shards[k] for k in _GATHER_AT[host]])
        if host in _SCATTER_AT:
            return _Comm("exchange", [self.grads[k] for k in _SCATTER_AT[host]])
        return None

    def done(self, host, results):
        if host in _GATHER_AT:
            self.full.update(zip(_GATHER_AT[host], results))
        if host in _SCATTER_AT:
            self.recv.update(zip(_SCATTER_AT[host], results))


def _local_step(x0, target, wts, smalls, g_final, tb):
    depth = len(smalls)
    s, d = x0.shape
    saved = []
    xl = x0
    for l in range(depth):
        p = smalls[l]
        z, h, got = _norm_matmul(xl, p["g_mix"], wts.get(l, "w_in"), f"in_proj{l}", wts.comm(f"in_proj{l}"))
        wts.done(f"in_proj{l}", got)
        ya = _sgu_fwd(z, p["g_sgu"], p["w_s"], p["b_t"], tb, f"sgu_fwd{l}")
        yb, ret, states = _ret_fwd(z, tb, f"ret_fwd{l}")
        cum = _fox_prep(z, p["b_f"], f"fox_prep{l}")
        kc, vc, vtc, qtc = _kv_prep(z, _Z_FOX_Q, _Z_FOX_K, _Z_FOX_V, HEAD_DIM ** -0.5, f"fox_kv{l}")
        yc, lse_c, got = _attn_fwd(z, _Z_FOX_Q, HEAD_DIM, kc, vtc, HEAD_DIM ** -0.5, cum, f"fox_attn{l}", wts.comm(f"fox_attn{l}"))
        wts.done(f"fox_attn{l}", got)
        wq, wk, wv = wts.get(l, "wq"), wts.get(l, "wk"), wts.get(l, "wv")
        qd, kd, vd, vtd, cqn, ckvn, qtd = _mla_prep(z, p["gq"], p["gkv"], wq, wk, wv, tb, f"mla_prep{l}")
        yd, lse_d, got = _attn_fwd(qd, 0, 128, kd, vtd, _SCALE_D, None, f"mla_attn{l}", wts.comm(f"mla_attn{l}"))
        wts.done(f"mla_attn{l}", got)
        ys = (ya, yb, yc, yd)
        x1, yn = _out_proj(ys, p["g_go"], wts.get(l, "w_out"), xl, f"out_proj{l}")
        x2, u, h2, got = _ffn_fwd(x1, p["g_ffn"], wts.get(l, "w_up"), wts.get(l, "w_down"), f"ffn_fwd{l}", wts.comm(f"ffn_fwd{l}"))
        wts.done(f"ffn_fwd{l}", got)
        saved.append(dict(x=xl, z=z, h=h, ys=ys, ret=ret, states=states, cum=cum, lse_c=lse_c, kc=kc, vc=vc, qd=qd, kd=kd, vd=vd,
                          cqn=cqn, ckvn=ckvn, lse_d=lse_d, x1=x1, yn=yn, u=u, h2=h2, wq=wq, wk=wk, wv=wv, qtc=qtc, qtd=qtd))
        xl = x2

    lrow, dx, dg_final = _loss_head(xl, g_final, target, "loss_head")

    sm = [None] * depth
    for l in reversed(range(depth)):
        p, a = smalls[l], saved[l]
        dx1, du, dg_ffn, got = _ffn_bwd(dx, a["x1"], a["u"], p["g_ffn"], wts.get(l, "w_up"), wts.get(l, "w_down"), f"ffn_bwd{l}",
                                        wts.comm(f"ffn_bwd{l}"))
        wts.done(f"ffn_bwd{l}", got)
        dw_down = _mm_tn(a["u"], dx, f"dw_down{l}", a_fn=lambda t: jnp.square(jnp.maximum(t, 0.0)), out_dtype=_WIRE)
        wts.grad(l, "w_down", dw_down.reshape(NDEV, dw_down.shape[0] // NDEV, d))
        wts.grad(l, "w_up", _mm_tn(a["h2"], du, f"dw_up{l}", blocked=True, out_dtype=_WIRE))
        dya, dyb, dg_go, dot_c, dot_d, dl_c, dl_d = _out_proj_bwd(dx1, wts.get(l, "w_out"), a["ys"], p["g_go"],
                                                                  f"out_proj_bwd{l}")
        wts.grad(l, "w_out", _mm_tn(a["yn"], dx1, f"dw_out{l}", out_dtype=_WIRE).reshape(NDEV, d // NDEV, d))
        dz = _ret_bwd(dyb, a["z"], a["ret"], a["states"], tb, f"ret_bwd{l}")
        dz, dg_sgu, dw_s, db_t = _sgu_bwd(dya, a["z"], p["g_sgu"], p["w_s"], p["b_t"], tb, dz, f"sgu_bwd{l}")
        dqt_c, dz, dck, dcq, got = _attn_bwd(a["kc"], a["vc"], a["qtc"], dot_c, a["lse_c"], dl_c, HEAD_DIM,
                                             HEAD_DIM ** -0.5, a["cum"], f"fox_attn_bwd{l}", _MXU,
                                             wts.comm(f"fox_attn_bwd{l}"), kv_into=dz)
        wts.done(f"fox_attn_bwd{l}", got)
        dz = _untranspose(dqt_c, _MXU, f"fox_dq{l}", into=dz, col=_DZ_FOX_Q)
        dqt_d, dk_d, dv_d, got = _attn_bwd(a["kd"], a["vd"], a["qtd"], dot_d, a["lse_d"], dl_d, 128, _SCALE_D, None,
                                           f"mla_attn_bwd{l}", F32, wts.comm(f"mla_attn_bwd{l}"))
        wts.done(f"mla_attn_bwd{l}", got)
        dz, dkr, dwq, dwk, dwv, dgq, dgkv = _mla_prep_bwd(dqt_d, dk_d, dv_d, a["z"], a["cqn"], a["ckvn"], p["gq"], p["gkv"],
                                                          a["wq"], a["wk"], a["wv"], tb, dz, f"mla_prep_bwd{l}")
        dz, db_f = _fox_post(dcq, dck, a["z"], p["b_f"], dkr, dz, f"fox_post{l}")
        wts.grad(l, "w_in", _unpad_in_cols(_mm_tn(a["h"], dz, f"dw_in{l}", out_dtype=_WIRE)).reshape(NDEV, d // NDEV, N_IN))
        dx, dg_mix, got = _in_proj_bwd(dz, wts.get(l, "w_in"), a["x"], p["g_mix"], dx1, f"in_proj_bwd{l}",
                                       wts.comm(f"in_proj_bwd{l}"))
        wts.done(f"in_proj_bwd{l}", got)
        sm[l] = [dg_mix, dg_go, dg_ffn, dg_sgu, dw_s, db_t[:, :N_HEADS].T, db_f[0, :N_HEADS], dgq, dgkv, _unpad_uq(dwq),
                 _join_ukv(dwk, dwv)]
    return lrow, dx, sm, dg_final


def _reduce_and_update(loss, grad_x, recv, sm, dg_final, me, given):
    depth = len(sm)
    pieces = [t for l in range(depth) for t in sm[l]] + [dg_final]
    flat = jnp.concatenate([t.reshape(-1) for t in pieces])
    n_flat = flat.shape[0]
    unit = NDEV * 8 * 128
    n_pad = -(-n_flat // unit) * unit
    packed = jnp.pad(flat, (0, n_pad - n_flat)).reshape(NDEV, n_pad // (NDEV * 128), 128)
    red = _sum_slots(_exchange([packed], "scatter_small")[0], "sum_small")
    full = _all_gather([red], "gather_small")[0].reshape(-1)
    offs = np.cumsum([0] + [int(np.prod(t.shape)) for t in pieces])
    red_pieces = [full[int(offs[i]):int(offs[i + 1])].reshape(pieces[i].shape) for i in range(len(pieces))]
    per = len(sm[0])
    stack = lambda i: jnp.stack([red_pieces[l * per + i] for l in range(depth)])
    g_small = dict(g_mix_norm=stack(0), g_group_out=stack(1), g_ffn_norm=stack(2), g_sgu=stack(3), w_spatial=stack(4),
                   b_spatial=stack(5), b_forget=stack(6), g_mla_q=stack(7), g_mla_kv=stack(8), g_final=red_pieces[-1])
    cq, ckv = given["w_uq"][0].shape[2], given["w_ukv"][0].shape[2]
    g_small["w_uq"] = lax.dynamic_slice_in_dim(stack(9), me * cq, cq, axis=2)
    g_small["w_ukv"] = lax.dynamic_slice_in_dim(stack(10), me * ckv, ckv, axis=2)

    names = list(given)
    outs = {}
    for nme in names:
        wv_, mv_, vv_ = given[nme]
        shape = wv_.shape
        if nme in ("w_in", "w_out", "w_up", "w_down"):
            res = None
            flat2 = lambda t: t.reshape(-1, shape[-1])
            for l in range(depth):
                res = _adamw_layer(recv[(l, nme)], flat2(wv_), flat2(mv_), flat2(vv_), l, res, f"adamw_{nme}{l}")
            outs[nme] = [t.reshape(shape) for t in res]
        else:
            two = lambda t: t.reshape(-1, shape[-1]) if t.ndim > 1 else t.reshape(1, -1)
            res = _adamw(two(g_small[nme]), two(wv_), two(mv_), two(vv_), f"adamw_{nme}")
            outs[nme] = [r.reshape(shape) for r in res]
    return (loss, grad_x, *[outs[n][0] for n in names], *[outs[n][1] for n in names], *[outs[n][2] for n in names],
            *[outs[n][3] for n in names])
```

```python
import functools

import jax
import jax.numpy as jnp
import numpy as np
from jax import lax
from jax.experimental import pallas as pl
from jax.experimental.pallas import tpu as pltpu

F32 = jnp.float32
_MXU = jnp.bfloat16
_WIRE = jnp.bfloat16
EPS = 1e-6
NDEV = 8
AXES = ("x", "y", "c")
MESH = pl.DeviceIdType.MESH

N_HEADS = 4
HEAD_DIM = 64
GROUP = 256
CHUNK = 128
NZ = 2816
N_IN = 2724
MISC_F, MISC_KR = 0, 32
VMEM_LIMIT = 56 * 1024 * 1024

ADAM_LR, ADAM_B1, ADAM_B2, ADAM_EPS, ADAM_WD, ADAM_STEP = 0.001, 0.9, 0.999, 1e-08, 0.01, 10

SDS = jax.ShapeDtypeStruct


def _cp(*sem):
    return pltpu.CompilerParams(dimension_semantics=sem, vmem_limit_bytes=VMEM_LIMIT)


def _dot(a, b):
    return jnp.dot(a.astype(_MXU), b.astype(_MXU), preferred_element_type=F32)


def _dot_nt(a, b):
    return lax.dot_general(a.astype(_MXU), b.astype(_MXU), (((1,), (1,)), ((), ())), preferred_element_type=F32)


def _dot_tn(a, b):
    return lax.dot_general(a.astype(_MXU), b.astype(_MXU), (((0,), (0,)), ((), ())), preferred_element_type=F32)


def _dot_exact(a, b, dims=(((1,), (0,)), ((), ()))):
    return lax.dot_general(a, b, dims, precision=lax.Precision.HIGHEST, preferred_element_type=F32)


def _rms(x, g):
    return x * lax.rsqrt(jnp.mean(x * x, axis=-1, keepdims=True) + EPS) * g


def _rms_bwd(x, g, dy):
    xh = x * lax.rsqrt(jnp.mean(x * x, axis=-1, keepdims=True) + EPS)
    dxh = dy * g
    r = lax.rsqrt(jnp.mean(x * x, axis=-1, keepdims=True) + EPS)
    dx = r * (dxh - xh * jnp.mean(dxh * xh, axis=-1, keepdims=True))
    return dx, jnp.sum(dy * xh, axis=0, keepdims=True)


_GELU_C = 0.7978845608028654


def _gelu(x):
    return 0.5 * x * (1.0 + jnp.tanh(_GELU_C * (x + 0.044715 * x * x * x)))


def _gelu_grad(x):
    t = jnp.tanh(_GELU_C * (x + 0.044715 * x * x * x))
    return 0.5 * (1.0 + t) + 0.5 * x * (1.0 - t * t) * _GELU_C * (1.0 + 3 * 0.044715 * x * x)


def _sigmoid(x):
    return 1.0 / (1.0 + jnp.exp(-x))


def _swap_half(t, half):
    n = t.shape[-1]
    lane = lax.broadcasted_iota(jnp.int32, t.shape, t.ndim - 1)
    return jnp.where((lane % (2 * half)) < half, pltpu.roll(t, n - half, t.ndim - 1), pltpu.roll(t, half, t.ndim - 1))


def _lanes(table, width):
    return jnp.concatenate([table] * (width // table.shape[-1]), axis=-1)


def _rope(t, cos, sin, half):
    return t * cos + _swap_half(t, half) * sin


def _rope_bwd(d, cos, sin, half):
    return d * cos - _swap_half(d, half) * sin


def _tables(s):
    pos = jnp.arange(s, dtype=F32)[:, None]

    def cs(half):
        inv = jnp.power(10000.0, -jnp.arange(half, dtype=F32) / half)
        ang = pos * inv[None, :]
        return jnp.cos(ang), jnp.sin(ang)

    c32, s32 = cs(32)
    c16, s16 = cs(16)
    z = lambda w: jnp.zeros((s, w), F32)
    o = lambda w: jnp.ones((s, w), F32)
    t = {}
    t["b_cos"] = jnp.concatenate([c32, c32, c32, c32], 1)
    t["b_sin"] = jnp.concatenate([-s32, s32, -s32, s32], 1)
    t["q_cos"] = jnp.concatenate([o(64), c16, c16, z(32)], 1)
    t["q_sin"] = jnp.concatenate([z(64), -s16, s16, z(32)], 1)
    t["k_cos"] = jnp.concatenate([z(32), c16, c16, z(64)], 1)
    t["k_sin"] = jnp.concatenate([z(32), -s16, s16, z(64)], 1)
    lg = jnp.log1p(-jnp.exp2(-5.0 - jnp.arange(N_HEADS, dtype=F32)))
    j = jnp.arange(CHUNK, dtype=F32)
    rel = j[:, None] - j[None, :]
    t["decay"] = jnp.where(rel[None] >= 0, jnp.exp(jnp.maximum(rel, 0.0)[None] * lg[:, None, None]), 0.0)
    t["decay_t"] = jnp.swapaxes(t["decay"], 1, 2)

    def rows(e):
        return jnp.repeat(e.T, HEAD_DIM, axis=1)

    t["qw"] = rows(jnp.exp((j + 1.0)[None, :] * lg[:, None]))
    t["kw"] = rows(jnp.exp((CHUNK - 1 - j)[None, :] * lg[:, None]))
    t["kw2"] = rows(jnp.exp((CHUNK - j)[None, :] * lg[:, None]))
    t["qw0"] = rows(jnp.exp(j[None, :] * lg[:, None]))
    t["cd"] = jnp.repeat(jnp.exp(CHUNK * lg), HEAD_DIM)[None, :]
    e = np.zeros((128, 512), np.float32)
    for h in range(N_HEADS):
        for r in range(32):
            e[MISC_KR + r, 128 * h + 64 + r] = 1.0
    t["place"] = jnp.asarray(e)
    lane_head = np.arange(GROUP) // HEAD_DIM
    t["grp"] = jnp.asarray((lane_head[:, None] == lane_head[None, :]) / HEAD_DIM, _MXU)
    hsel = (np.arange(128)[:, None] == lane_head[None, :]).astype(np.float32)
    t["hsel"] = jnp.asarray(hsel)
    t["hselt"] = jnp.asarray(hsel.T, _MXU)
    return t


def _norm_matmul(x, g, w, name, comm=None):
    s, d = x.shape
    n = w.shape[1]
    tm, tn = min(512, s), 256
    ni = s // tm

    def body(*refs):
        (x_ref, g_ref, w_ref), (z_ref, h_ref), _, cc = _split_refs(refs, 3, 2, comm)
        i = pl.program_id(0)
        _host_gather(comm, cc, i, ni, late=True)
        h = _rms(x_ref[...], g_ref[...]).astype(h_ref.dtype)
        h_ref[...] = h
        for j in range(n // tn):
            z_ref[:, tn * j:tn * (j + 1)] = jnp.dot(h, w_ref[:, tn * j:tn * (j + 1)], preferred_element_type=F32)
        if comm is not None:
            @pl.when(i == ni - 1)
            def _():
                comm.wait(*cc)

    in_specs = [pl.BlockSpec((tm, d), lambda i: (i, 0)), pl.BlockSpec((1, d), lambda i: (0, 0)),
                pl.BlockSpec((d, n), lambda i: (0, 0))]
    out_specs = [pl.BlockSpec((tm, n), lambda i: (i, 0)), pl.BlockSpec((tm, d), lambda i: (i, 0))]
    out_shape = [SDS((s, n), F32), SDS((s, d), _MXU)]
    return _call_with_comm(body, (ni,), in_specs, out_specs, out_shape, [], [x, g, w], comm, ("arbitrary",), name)


def _mm_tn(a, b, name, *, a_fn=None, blocked=False, out_dtype=F32):
    k, m = a.shape
    n = b.shape[1]
    tm, tk = min(1024, m), min(1024, k)
    tn = next(t for t in (2816, 1024, 512, 256, 128) if n % t == 0)
    assert m % tm == 0 and k % tk == 0
    nk = k // tk

    def body(a_ref, b_ref, o_ref, acc):
        kk = pl.program_id(2)

        @pl.when(kk == 0)
        def _():
            acc[...] = jnp.zeros_like(acc)

        av = a_ref[...]
        if a_fn is not None:
            av = a_fn(av.astype(F32))
        acc[...] += _dot_tn(av, b_ref[...])

        @pl.when(kk == nk - 1)
        def _():
            if blocked:
                for c in range(tn // 512):
                    o_ref[c] = acc[:, 512 * c:512 * (c + 1)].astype(o_ref.dtype)
            else:
                o_ref[...] = acc[...].astype(o_ref.dtype)

    if blocked:
        assert tn % 512 == 0
        out_spec = pl.BlockSpec((tn // 512, tm, 512), lambda i, j, kk: (j, i, 0))
        out_shape = SDS((n // 512, m, 512), out_dtype)
    else:
        out_spec = pl.BlockSpec((tm, tn), lambda i, j, kk: (i, j))
        out_shape = SDS((m, n), out_dtype)
    return pl.pallas_call(
        body, grid=(m // tm, n // tn, nk),
        in_specs=[pl.BlockSpec((tk, tm), lambda i, j, kk: (kk, i)), pl.BlockSpec((tk, tn), lambda i, j, kk: (kk, j))],
        out_specs=out_spec, out_shape=out_shape, scratch_shapes=[pltpu.VMEM((tm, tn), F32)],
        compiler_params=_cp("parallel", "parallel", "arbitrary"), name=name)(a, b)


def _split_dot(x, m):
    hi = x.astype(_MXU)
    lo = (x - hi.astype(F32)).astype(_MXU)
    return jnp.dot(hi, m, preferred_element_type=F32) + jnp.dot(lo, m, preferred_element_type=F32)


def _gstandardize(t, grp):
    tc = t - _split_dot(t, grp)
    rs = lax.rsqrt(_split_dot(tc * tc, grp) + EPS)
    return tc * rs, rs


def _gstandardize_bwd(yh, rs, dy, grp):
    return rs * (dy - _split_dot(dy, grp) - yh * _split_dot(dy * yh, grp))


def _head_select(parts):
    hid = lax.broadcasted_iota(jnp.int32, parts[0].shape, 1) // HEAD_DIM
    return jnp.where(hid == 0, parts[0], jnp.where(hid == 1, parts[1], jnp.where(hid == 2, parts[2], parts[3])))


def _head_masked(x):
    hid = lax.broadcasted_iota(jnp.int32, x.shape, 1) // HEAD_DIM
    return [jnp.where(hid == h, x, jnp.zeros_like(x)) for h in range(N_HEADS)]


def _tril(w):
    r = lax.broadcasted_iota(jnp.int32, w.shape, 0)
    c = lax.broadcasted_iota(jnp.int32, w.shape, 1)
    return jnp.where(r >= c, w, 0.0)


def _sgu_mixed(vgb, wcs, bias, nchunk):
    ms = [[jnp.dot(wcs[h], vgb[CHUNK * c:CHUNK * (c + 1)], preferred_element_type=F32) for h in range(N_HEADS)]
          for c in range(nchunk)]
    return [_head_select(ms[c]) + bias for c in range(nchunk)]


def _sgu_fwd(z, gain, w_s, b_t, tb, name):
    s = z.shape[0]
    tm = min(512, s)
    const = lambda a: pl.BlockSpec(a.shape, lambda i: (0,) * a.ndim)

    def body(u_ref, v_ref, g_ref, w_ref, b_ref, grp_ref, hsel_ref, y_ref):
        u = _gelu(u_ref[...])
        vh, _ = _gstandardize(_gelu(v_ref[...]), grp_ref[...])
        vgb = (vh * g_ref[...]).astype(_MXU)
        bias = _dot_exact(b_ref[...], hsel_ref[...])
        wcs = [_tril(w_ref[h]).astype(_MXU) for h in range(N_HEADS)]
        for c, mixed in enumerate(_sgu_mixed(vgb, wcs, bias, tm // CHUNK)):
            r = slice(CHUNK * c, CHUNK * (c + 1))
            y_ref[r, :] = u[r] * mixed

    return pl.pallas_call(
        body, grid=(s // tm,),
        in_specs=[pl.BlockSpec((tm, GROUP), lambda i: (i, _Z_SGU[0])), pl.BlockSpec((tm, GROUP), lambda i: (i, _Z_SGU[1])),
                  pl.BlockSpec((1, GROUP), lambda i: (0, 0)), pl.BlockSpec((N_HEADS, CHUNK, CHUNK), lambda i: (0, 0, 0)),
                  pl.BlockSpec((CHUNK, 128), lambda i: (0, 0)), const(tb["grp"]), const(tb["hsel"])],
        out_specs=pl.BlockSpec((tm, GROUP), lambda i: (i, 0)), out_shape=SDS((s, GROUP), F32),
        compiler_params=_cp("parallel"), name=name)(z, z, gain, w_s, b_t, tb["grp"], tb["hsel"])


def _sgu_bwd(dy, z, gain, w_s, b_t, tb, dz, name):
    s = z.shape[0]
    tm = min(512, s)
    nchunk = tm // CHUNK
    const = lambda a: pl.BlockSpec(a.shape, lambda i: (0,) * a.ndim)

    def body(dy_ref, u_ref, v_ref, g_ref, w_ref, b_ref, grp_ref, hsel_ref, hselt_ref, _, dz_ref, dg_ref, dw_ref, db_ref):
        @pl.when(pl.program_id(0) == 0)
        def _():
            dg_ref[...] = jnp.zeros_like(dg_ref)
            dw_ref[...] = jnp.zeros_like(dw_ref)
            db_ref[...] = jnp.zeros_like(db_ref)

        grp = grp_ref[...]
        u_pre, v_pre, gain_v = u_ref[...], v_ref[...], g_ref[...]
        u = _gelu(u_pre)
        vh, rs = _gstandardize(_gelu(v_pre), grp)
        vgb = (vh * gain_v).astype(_MXU)
        dyv = dy_ref[...]
        bias = _dot_exact(b_ref[...], hsel_ref[...])
        wfs = [_tril(w_ref[h]) for h in range(N_HEADS)]
        wcs = [w.astype(_MXU) for w in wfs]
        wts = [w.T.astype(_MXU) for w in wfs]
        mixed = _sgu_mixed(vgb, wcs, bias, nchunk)
        gu = _gelu_grad(u_pre)
        dms, dmh = [], []
        for c in range(nchunk):
            r = slice(CHUNK * c, CHUNK * (c + 1))
            dz_ref[r, 0:GROUP] = (dyv[r] * mixed[c] * gu[r]).astype(dz_ref.dtype)
            dm = dyv[r] * u[r]
            dms.append(dm)
            dmh.append([m.astype(_MXU) for m in _head_masked(dm)])
        dws = [sum(lax.dot_general(dmh[c][h], vgb[CHUNK * c:CHUNK * (c + 1)], (((1,), (1,)), ((), ())),
                                   preferred_element_type=F32) for c in range(nchunk)) for h in range(N_HEADS)]
        dvg = jnp.concatenate([sum(jnp.dot(wts[h], dmh[c][h], preferred_element_type=F32) for h in range(N_HEADS))
                               for c in range(nchunk)], axis=0)
        for h in range(N_HEADS):
            dw_ref[h] += _tril(dws[h])
        db_ref[...] += sum(_split_dot(dm, hselt_ref[...]) for dm in dms)
        dg_ref[...] += jnp.sum(dvg * vh, axis=0, keepdims=True)
        dv = _gstandardize_bwd(vh, rs, dvg * gain_v, grp)
        dz_ref[:, GROUP:2 * GROUP] = (dv * _gelu_grad(v_pre)).astype(dz_ref.dtype)

    consts = [tb["grp"], tb["hsel"], tb["hselt"]]
    return pl.pallas_call(
        body, grid=(s // tm,),
        in_specs=[pl.BlockSpec((tm, GROUP), lambda i: (i, 0)),
                  pl.BlockSpec((tm, GROUP), lambda i: (i, _Z_SGU[0])), pl.BlockSpec((tm, GROUP), lambda i: (i, _Z_SGU[1])),
                  pl.BlockSpec((1, GROUP), lambda i: (0, 0)), pl.BlockSpec((N_HEADS, CHUNK, CHUNK), lambda i: (0, 0, 0)),
                  pl.BlockSpec((CHUNK, 128), lambda i: (0, 0))] + [const(a) for a in consts]
        + [pl.BlockSpec(memory_space=pl.ANY)],
        out_specs=[pl.BlockSpec((tm, 2 * GROUP), lambda i: (i, _DZ_SGU)), pl.BlockSpec((1, GROUP), lambda i: (0, 0)),
                   pl.BlockSpec((N_HEADS, CHUNK, CHUNK), lambda i: (0, 0, 0)), pl.BlockSpec((CHUNK, 128), lambda i: (0, 0))],
        out_shape=[SDS(dz.shape, dz.dtype), SDS((1, GROUP), F32), SDS((N_HEADS, CHUNK, CHUNK), F32), SDS((CHUNK, 128), F32)],
        input_output_aliases={9: 0}, compiler_params=_cp("arbitrary"), name=name)(dy, z, z, gain, w_s, b_t, *consts, dz)


_SCALE_B = HEAD_DIM ** -0.5
RET_CHUNKS = 4


def _block_diag(compact):
    full = jnp.concatenate([compact] * N_HEADS, axis=0)
    r = lax.broadcasted_iota(jnp.int32, full.shape, 0) // HEAD_DIM
    c = lax.broadcasted_iota(jnp.int32, full.shape, 1) // HEAD_DIM
    return jnp.where(r == c, full, 0.0)


def _diag_blocks(full):
    c = lax.broadcasted_iota(jnp.int32, (HEAD_DIM, GROUP), 1) // HEAD_DIM
    return sum(jnp.where(c == h, full[HEAD_DIM * h:HEAD_DIM * (h + 1), :], 0.0) for h in range(N_HEADS))


def _ret_fwd(z, tb, name):
    s = z.shape[0]
    nc = s // CHUNK
    per = min(RET_CHUNKS, nc)
    rows = per * CHUNK
    row = lambda col: pl.BlockSpec((rows, GROUP), lambda n, col=col: (n, col))
    const = lambda shape: pl.BlockSpec(shape, lambda n: (0,) * len(shape))

    def body(q_ref, k_ref, v_ref, g_ref, cos_ref, sin_ref, dec_ref, qw_ref, kw_ref, cd_ref, grp_ref, y_ref, o_ref, st_ref, state):
        @pl.when(pl.program_id(0) == 0)
        def _():
            state[...] = jnp.zeros_like(state)

        cos, sin = _lanes(cos_ref[...], GROUP), _lanes(sin_ref[...], GROUP)
        q = _rope(q_ref[...], cos, sin, 32)
        k = _rope(k_ref[...], cos, sin, 32) * _SCALE_B
        v = v_ref[...]
        g = g_ref[...]
        rcs = [slice(CHUNK * c, CHUNK * (c + 1)) for c in range(per)]
        vms = [[t.astype(_MXU) for t in _head_masked(v[r])] for r in rcs]
        scs = [[_dot_nt(t.astype(_MXU), k[r]) for t in _head_masked(q[r])] for r in rcs]
        kvs = [_dot_tn(k[r] * kw_ref[...], v[r]) for r in rcs]
        st = state[...]
        crosses = []
        for c, r in enumerate(rcs):
            st_ref[c] = st
            crosses.append(_dot(q[r] * qw_ref[...], _block_diag(st)))
            st = cd_ref[...] * st + _diag_blocks(kvs[c])
        state[...] = st
        outs = []
        for c in range(per):
            scd = [(scs[c][h] * dec_ref[h]).astype(_MXU) for h in range(N_HEADS)]
            outs.append(crosses[c] + sum(jnp.dot(scd[h], vms[c][h], preferred_element_type=F32) for h in range(N_HEADS)))
        o = jnp.concatenate(outs, axis=0)
        o_ref[...] = o
        yh, _ = _gstandardize(o, grp_ref[...])
        y_ref[...] = g * _sigmoid(g) * yh

    return pl.pallas_call(
        body, grid=(nc // per,),
        in_specs=[row(_Z_RET[0]), row(_Z_RET[1]), row(_Z_RET[2]), row(_Z_RET[3]), pl.BlockSpec((rows, 128), lambda n: (n, 0)),
                  pl.BlockSpec((rows, 128), lambda n: (n, 0)), const((N_HEADS, CHUNK, CHUNK)),
                  const((CHUNK, GROUP)), const((CHUNK, GROUP)), const((1, GROUP)), const((GROUP, GROUP))],
        out_specs=[pl.BlockSpec((rows, GROUP), lambda n: (n, 0)), pl.BlockSpec((rows, GROUP), lambda n: (n, 0)),
                   pl.BlockSpec((per, HEAD_DIM, GROUP), lambda n: (n, 0, 0))],
        out_shape=[SDS((s, GROUP), F32), SDS((s, GROUP), F32), SDS((nc, HEAD_DIM, GROUP), F32)],
        scratch_shapes=[pltpu.VMEM((HEAD_DIM, GROUP), F32)],
        compiler_params=_cp("arbitrary"), name=name)(z, z, z, z, tb["b_cos"], tb["b_sin"], tb["decay"], tb["qw"], tb["kw"], tb["cd"],
                                                       tb["grp"])


def _ret_bwd(dy, z, o_pre, states, tb, name):
    s = z.shape[0]
    nc = s // CHUNK
    per = min(RET_CHUNKS, nc)
    rows = per * CHUNK
    ns = nc // per
    rev = lambda col: pl.BlockSpec((rows, GROUP), lambda n, col=col: (ns - 1 - n, col))
    const = lambda shape: pl.BlockSpec(shape, lambda n: (0,) * len(shape))

    def body(dy_ref, q_ref, k_ref, v_ref, g_ref, o_ref, st_ref, cos_ref, sin_ref, dec_ref, dect_ref, qw_ref, kw2_ref, qw0_ref,
             cd_ref, grp_ref, dz_ref, rstate):
        @pl.when(pl.program_id(0) == 0)
        def _():
            rstate[...] = jnp.zeros_like(rstate)

        cos, sin = _lanes(cos_ref[...], GROUP), _lanes(sin_ref[...], GROUP)
        q = _rope(q_ref[...], cos, sin, 32)
        k = _rope(k_ref[...], cos, sin, 32) * _SCALE_B
        v = v_ref[...]
        g = g_ref[...]
        dyv = dy_ref[...]
        sg = _sigmoid(g)
        yh, rs = _gstandardize(o_ref[...], grp_ref[...])
        dz_ref[:, 3 * GROUP:4 * GROUP] = (dyv * yh * (sg * (1.0 + g * (1.0 - sg)))).astype(dz_ref.dtype)
        do = _gstandardize_bwd(yh, rs, dyv * (g * sg), grp_ref[...])
        hs = range(N_HEADS)
        rcs = [slice(CHUNK * c, CHUNK * (c + 1)) for c in range(per)]
        mask = lambda t: [m.astype(_MXU) for m in _head_masked(t)]
        qms, kms, vms, doms = ([mask(t[r]) for r in rcs] for t in (q, k, v, do))
        dps = [[_dot_nt(doms[c][h], v[r]) for h in hs] for c, r in enumerate(rcs)]
        pts = [[_dot_nt(kms[c][h], q[r]) for h in hs] for c, r in enumerate(rcs)]
        dpts = [[_dot_nt(vms[c][h], do[r]) for h in hs] for c, r in enumerate(rcs)]
        dq_x = [_dot_nt(do[r] * qw_ref[...], _block_diag(st_ref[c])) for c, r in enumerate(rcs)]
        r_new = [_dot_tn(q[r] * qw0_ref[...], do[r]) for r in rcs]
        rr = rstate[...]
        dk_x, dv_x = [None] * per, [None] * per
        for c in reversed(range(per)):
            r_bd = _block_diag(rr)
            dk_x[c] = _dot_nt(v[rcs[c]] * kw2_ref[...], r_bd)
            dv_x[c] = _dot(k[rcs[c]] * kw2_ref[...], r_bd)
            rr = cd_ref[...] * rr + _diag_blocks(r_new[c])
        rstate[...] = rr
        dqs, dks, dvs = [], [], []
        for c in range(per):
            dpd = [(dps[c][h] * dec_ref[h]).astype(_MXU) for h in hs]
            dptd = [(dpts[c][h] * dect_ref[h]).astype(_MXU) for h in hs]
            ptd = [(pts[c][h] * dect_ref[h]).astype(_MXU) for h in hs]
            dqs.append(dq_x[c] + sum(jnp.dot(dpd[h], kms[c][h], preferred_element_type=F32) for h in hs))
            dks.append(dk_x[c] + sum(jnp.dot(dptd[h], qms[c][h], preferred_element_type=F32) for h in hs))
            dvs.append(dv_x[c] + sum(jnp.dot(ptd[h], doms[c][h], preferred_element_type=F32) for h in hs))
        dz_ref[:, 0:GROUP] = _rope_bwd(jnp.concatenate(dqs, axis=0), cos, sin, 32).astype(dz_ref.dtype)
        dz_ref[:, GROUP:2 * GROUP] = _rope_bwd(jnp.concatenate(dks, axis=0) * _SCALE_B, cos, sin, 32).astype(dz_ref.dtype)
        dz_ref[:, 2 * GROUP:3 * GROUP] = jnp.concatenate(dvs, axis=0).astype(dz_ref.dtype)

    r0 = lambda: pl.BlockSpec((rows, GROUP), lambda n: (ns - 1 - n, 0))
    r128 = lambda: pl.BlockSpec((rows, 128), lambda n: (ns - 1 - n, 0))
    return pl.pallas_call(
        body, grid=(ns,),
        in_specs=[r0(), rev(_Z_RET[0]), rev(_Z_RET[1]), rev(_Z_RET[2]), rev(_Z_RET[3]), r0(),
                  pl.BlockSpec((per, HEAD_DIM, GROUP), lambda n: (ns - 1 - n, 0, 0)),
                  r128(), r128(), const((N_HEADS, CHUNK, CHUNK)), const((N_HEADS, CHUNK, CHUNK)), const((CHUNK, GROUP)),
                  const((CHUNK, GROUP)), const((CHUNK, GROUP)), const((1, GROUP)), const((GROUP, GROUP))],
        out_specs=pl.BlockSpec((rows, 4 * GROUP), lambda n: (ns - 1 - n, _DZ_RET)),
        out_shape=SDS((s, NZ), _MXU), scratch_shapes=[pltpu.VMEM((HEAD_DIM, GROUP), F32)],
        compiler_params=_cp("arbitrary"), name=name)(
            dy, z, z, z, z, o_pre, states, tb["b_cos"], tb["b_sin"], tb["decay"], tb["decay_t"], tb["qw"], tb["kw2"], tb["qw0"],
            tb["cd"], tb["grp"])


TQ = 256


def _log_sigmoid(x):
    return jnp.minimum(x, 0.0) - jnp.log1p(jnp.exp(-jnp.abs(x)))


def _fox_prep(z, b_f, name):
    s = z.shape[0]
    nb = s // TQ

    def body(m_ref, b_ref, cc_ref, carry):
        @pl.when(pl.program_id(0) == 0)
        def _():
            carry[...] = jnp.zeros_like(carry)

        lane = lax.broadcasted_iota(jnp.int32, (TQ, 128), 1)
        logf = jnp.where(lane < N_HEADS, _log_sigmoid(m_ref[...] + b_ref[...]), 0.0)
        r = lax.broadcasted_iota(jnp.int32, (TQ, TQ), 0)
        c = lax.broadcasted_iota(jnp.int32, (TQ, TQ), 1)
        tri = jnp.where(r >= c, 1.0, 0.0).astype(F32)
        cum = _dot_exact(tri, logf) + carry[...]
        cc_ref[...] = cum * LOG2E
        carry[...] = cum[TQ - 1:TQ, :]

    return pl.pallas_call(
        body, grid=(nb,),
        in_specs=[pl.BlockSpec((TQ, 128), lambda i: (i, NZ // 128 - 1)), pl.BlockSpec((1, 128), lambda i: (0, 0))],
        out_specs=pl.BlockSpec((TQ, 128), lambda i: (i, 0)),
        out_shape=SDS((s, 128), F32), scratch_shapes=[pltpu.VMEM((1, 128), F32)],
        compiler_params=_cp("arbitrary"), name=name)(z, b_f)


def _fox_post(dcr, dcq, z, b_f, dkr, dz, name):
    s = z.shape[0]
    nb = s // TQ

    def body(dc_ref, dcq_ref, m_ref, b_ref, dkr_ref, _, dz_ref, db_ref, carry):
        @pl.when(pl.program_id(0) == 0)
        def _():
            carry[...] = jnp.zeros_like(carry)
            db_ref[...] = jnp.zeros_like(db_ref)

        r = lax.broadcasted_iota(jnp.int32, (TQ, TQ), 0)
        c = lax.broadcasted_iota(jnp.int32, (TQ, TQ), 1)
        triu = jnp.where(c >= r, 1.0, 0.0).astype(F32)
        dc = jnp.concatenate([dc_ref[0], jnp.zeros((120, TQ), F32)], axis=0)
        dlogf = _dot_exact(triu, dc, (((1,), (1,)), ((), ()))) + _dot_exact(triu, dcq_ref[...]) + carry[...]
        carry[...] = dlogf[0:1, :]
        x = m_ref[...] + b_ref[...]
        lane = lax.broadcasted_iota(jnp.int32, (TQ, 128), 1)
        df = jnp.where(lane < N_HEADS, dlogf * _sigmoid(-x), 0.0)
        db_ref[...] += jnp.sum(df, axis=0, keepdims=True)
        dz_ref[...] = (df + dkr_ref[...]).astype(dz_ref.dtype)

    rv = lambda i: nb - 1 - i
    return pl.pallas_call(
        body, grid=(nb,),
        in_specs=[pl.BlockSpec((1, 8, TQ), lambda i: (rv(i), 0, 0)), pl.BlockSpec((TQ, 128), lambda i: (rv(i), 0)),
                  pl.BlockSpec((TQ, 128), lambda i: (rv(i), NZ // 128 - 1)),
                  pl.BlockSpec((1, 128), lambda i: (0, 0)), pl.BlockSpec((TQ, 128), lambda i: (rv(i), 0)),
                  pl.BlockSpec(memory_space=pl.ANY)],
        out_specs=[pl.BlockSpec((TQ, 128), lambda i: (rv(i), _DZ_MISC)), pl.BlockSpec((1, 128), lambda i: (0, 0))],
        out_shape=[SDS(dz.shape, dz.dtype), SDS((1, 128), F32)], scratch_shapes=[pltpu.VMEM((1, 128), F32)],
        input_output_aliases={5: 0}, compiler_params=_cp("arbitrary"), name=name)(dcr, dcq, z, b_f, dkr, dz)


NEG = -1e30


TKV = 512


def _key_block(s):
    return min(TKV, s)


def _diag_mask(shape, off):
    r = lax.broadcasted_iota(jnp.int32, shape, 0)
    c = lax.broadcasted_iota(jnp.int32, shape, 1)
    return c + off >= r


def _head_lanes(h, dqk):
    return slice(128 * (h // 2), 128 * (h // 2) + 128) if dqk == HEAD_DIM else slice(128 * h, 128 * h + 128)


def _keep_half(x, a, axis):
    idx = lax.broadcasted_iota(jnp.int32, x.shape, axis)
    return jnp.where((idx < HEAD_DIM) if a == 0 else (idx >= HEAD_DIM), x, jnp.zeros_like(x))


def _scaled_qt(q, scale):
    qs = q.astype(F32) * (scale * LOG2E)
    return [qs[TQ * b:TQ * (b + 1)].T.astype(_MXU) for b in range(q.shape[0] // TQ)]


def _kv_prep(z, qcol, kcol, vcol, scale, name):
    s = z.shape[0]
    tk = _key_block(s)
    nk = s // tk

    def body(q_ref, k_ref, v_ref, kb_ref, vb_ref, vt_ref, qt_ref):
        kb_ref[...] = k_ref[...].astype(_MXU)
        v = v_ref[...]
        vb_ref[...] = v.astype(_MXU)
        vt_ref[0] = v.T.astype(_MXU)
        for b, t in enumerate(_scaled_qt(q_ref[...], scale)):
            qt_ref[b] = t

    blk = pl.BlockSpec((tk, GROUP), lambda i: (i, 0))
    col = lambda c: pl.BlockSpec((tk, GROUP), lambda i, c=c: (i, c))
    return pl.pallas_call(
        body, grid=(nk,), in_specs=[col(qcol), col(kcol), col(vcol)],
        out_specs=[blk, blk, pl.BlockSpec((1, GROUP, tk), lambda i: (i, 0, 0)),
                   pl.BlockSpec((tk // TQ, GROUP, TQ), lambda i: (i, 0, 0))],
        out_shape=[SDS((s, GROUP), _MXU), SDS((s, GROUP), _MXU), SDS((nk, GROUP, tk), _MXU), SDS((s // TQ, GROUP, TQ), _MXU)],
        compiler_params=_cp("parallel"), name=name)(z, z, z)


LOG2E = 1.4426950408889634


def _attn_fwd(q, qcol, dqk, kb, vt, scale, ck2, name, comm=None):
    s = q.shape[0]
    nq = s // TQ
    tk = _key_block(s)
    ratio = tk // TQ
    wq = N_HEADS * dqk
    bias = ck2 is not None

    def body(*refs):
        ins, (o_ref, l_ref), _, cc = _split_refs(refs, 4 if bias else 3, 2, comm)
        if bias:
            q_ref, k_ref, vt_ref, cc_ref = ins
        else:
            q_ref, k_ref, vt_ref = ins
        i = pl.program_id(0)
        _host_gather(comm, cc, i, nq)
        qts = []
        for h in range(N_HEADS):
            qt = (q_ref[:, _head_lanes(h, dqk)].astype(F32) * (scale * LOG2E)).T
            qts.append((_keep_half(qt, h % 2, 0) if dqk == HEAD_DIM else qt).astype(_MXU))

        def step(j, carry, off):
            r0 = pl.multiple_of(j * tk, tk)
            vtj = vt_ref[j]
            sts = [jnp.dot(k_ref[pl.ds(r0, tk), _head_lanes(h, dqk)], qts[h], preferred_element_type=F32)
                   for h in range(N_HEADS)]
            stats, ps = [], []
            for h in range(N_HEADS):
                m, l, _ = carry[3 * h:3 * h + 3]
                st = sts[h]
                if bias:
                    st = st - cc_ref[pl.ds(r0, tk), h:h + 1]
                if off is not None:
                    st = jnp.where(_diag_mask(st.shape, off), st, NEG)
                m_new = jnp.maximum(m, jnp.max(st, axis=0, keepdims=True))
                alpha = jnp.exp2(m - m_new)
                p = jnp.exp2(st - m_new)
                stats.append((m_new, alpha * l + jnp.sum(p, axis=0, keepdims=True), alpha))
                ps.append(p.astype(_MXU))
            out = []
            for h in range(N_HEADS):
                m_new, l, alpha = stats[h]
                acc = alpha * carry[3 * h + 2] + jnp.dot(vtj[HEAD_DIM * h:HEAD_DIM * (h + 1), :], ps[h],
                                                         preferred_element_type=F32)
                out += [m_new, l, acc]
            return tuple(out)

        init = (jnp.full((1, TQ), NEG, F32), jnp.zeros((1, TQ), F32), jnp.zeros((HEAD_DIM, TQ), F32)) * N_HEADS
        jd = i // ratio
        carry = lax.fori_loop(0, jd, functools.partial(step, off=None), init)
        carry = step(jd, carry, TQ * (i % ratio))
        l_ref[...] = jnp.zeros_like(l_ref)
        for h in range(N_HEADS):
            l_ref[0, h:h + 1, :] = carry[3 * h] + jnp.log2(carry[3 * h + 1])
        for p in range(2):
            ot = jnp.concatenate([carry[6 * p + 2] / carry[6 * p + 1], carry[6 * p + 5] / carry[6 * p + 4]], axis=0)
            o_ref[:, 128 * p:128 * (p + 1)] = ot.T
        if comm is not None:
            @pl.when(i == nq - 1)
            def _():
                comm.wait(*cc)

    rows = pl.BlockSpec((1, 8, TQ), lambda i: (i, 0, 0))
    in_specs = [pl.BlockSpec((TQ, wq), lambda i: (i, qcol)), pl.BlockSpec((s, wq), lambda i: (0, 0)),
                pl.BlockSpec((s // tk, GROUP, tk), lambda i: (0, 0, 0))]
    args = [q, kb, vt]
    if bias:
        in_specs.append(pl.BlockSpec((s, 128), lambda i: (0, 0)))
        args.append(ck2)
    out_specs = [pl.BlockSpec((TQ, GROUP), lambda i: (i, 0)), rows]
    out_shape = [SDS((s, GROUP), F32), SDS((nq, 8, TQ), F32)]
    return _call_with_comm(body, (nq,), in_specs, out_specs, out_shape, [], args, comm, ("arbitrary",), name)


def _call_with_comm(body, grid, in_specs, out_specs, out_shape, scratch, args, comm, semantics, name, aliases=None):
    n_out = len(out_shape)
    if comm is not None:
        in_specs, out_specs = in_specs + comm.in_specs, out_specs + comm.out_specs
        out_shape, scratch, args = out_shape + comm.out_shape, scratch + comm.scratch, list(args) + comm.arrs
    res = pl.pallas_call(body, grid=grid, in_specs=in_specs, out_specs=out_specs, out_shape=out_shape,
                         scratch_shapes=scratch, input_output_aliases=aliases or {}, compiler_params=_cp(*semantics),
                         name=name)(*args)
    return (*res[:n_out], list(res[n_out:]))


def _attn_bwd(kb, vb, qt, dot, lse, dl, dqk, scale, ck2, name, kv_dtype, comm=None, kv_into=None):
    s = kb.shape[0]
    nq = s // TQ
    tk = _key_block(s)
    ratio = tk // TQ
    nkb = s // tk
    wq = N_HEADS * dqk
    bias = ck2 is not None

    merged = kv_into is not None
    n_in = 6 + bias + merged
    n_out = 3 + 2 * bias - merged

    def body(*refs):
        ins, outs, _, cc = _split_refs(refs, n_in, n_out, comm)
        k_ref, v_ref, qt_ref, dot_ref, l_ref, d_ref = ins[:6]
        cc_ref = ins[6] if bias else None
        dqt_ref = outs[0]
        if merged:
            dk_ref, dv_ref = outs[1].at[:, 0:wq], outs[1].at[:, wq:wq + GROUP]
        else:
            dk_ref, dv_ref = outs[1], outs[2]
        if bias:
            dck_ref, dcq_ref = outs[-2:]
        j = pl.program_id(0)

        @pl.when(j == 0)
        def _():
            if comm is not None:
                comm.start(*cc)
            dqt_ref[...] = jnp.zeros_like(dqt_ref)
            if bias:
                dcq_ref[...] = jnp.zeros_like(dcq_ref)

        ks, kts, vs = [], [], []
        for h in range(N_HEADS):
            k2 = k_ref[:, _head_lanes(h, dqk)]
            if dqk == HEAD_DIM:
                k2 = _keep_half(k2, h % 2, 1)
            ks.append(k2)
            kts.append(k2.astype(F32).T.astype(_MXU))
            vs.append(_keep_half(v_ref[:, _head_lanes(h, HEAD_DIM)], h % 2, 1))
        cks = [cc_ref[:, h:h + 1] for h in range(N_HEADS)] if bias else None

        nt = (((1,), (1,)), ((), ()))

        def step(i, carry, off):
            qti, doti, li, di = qt_ref[i], dot_ref[i], l_ref[i], d_ref[i]
            qls = [_head_lanes(h, dqk) for h in range(N_HEADS)]
            vls = [_head_lanes(h, HEAD_DIM) for h in range(N_HEADS)]
            sts, dpts = [], []
            for h in range(N_HEADS):
                sts.append(jnp.dot(ks[h], qti[qls[h], :], preferred_element_type=F32))
                dpts.append(jnp.dot(vs[h], doti[vls[h], :], preferred_element_type=F32))
            pbs, dsbs, dcks = [], [], []
            for h in range(N_HEADS):
                st = sts[h] - li[h:h + 1, :]
                if bias:
                    st = st - cks[h]
                p = jnp.exp2(st)
                if off is not None:
                    p = jnp.where(_diag_mask(p.shape, off), p, 0.0)
                dst = p * (dpts[h] - di[h:h + 1, :])
                pbs.append(p.astype(_MXU))
                dsbs.append(dst.astype(_MXU))
                if bias:
                    dcks.append(carry[3 * h + 2] + jnp.sum(dst, axis=1, keepdims=True))
                    dcq_ref[i, h:h + 1, :] += jnp.sum(dst, axis=0, keepdims=True)
                else:
                    dcks.append(carry[3 * h + 2])
            out = []
            for h in range(N_HEADS):
                dvt = carry[3 * h + 1] + lax.dot_general(doti[HEAD_DIM * h:HEAD_DIM * (h + 1), :], pbs[h], nt,
                                                         preferred_element_type=F32)
                dkt = carry[3 * h] + lax.dot_general(qti[dqk * h:dqk * (h + 1), :], dsbs[h], nt, preferred_element_type=F32)
                dqt_ref[i, qls[h], :] += jnp.dot(kts[h], dsbs[h], preferred_element_type=F32) * scale
                out += [dkt, dvt, dcks[h]]
            return tuple(out)

        carry = (jnp.zeros((dqk, tk), F32), jnp.zeros((HEAD_DIM, tk), F32), jnp.zeros((tk, 1), F32)) * N_HEADS
        for r in range(ratio):
            carry = step(ratio * j + r, carry, TQ * r)
        carry = lax.fori_loop(ratio * (j + 1), nq, functools.partial(step, off=None), carry)
        for p in range(2):
            dv_ref[:, 128 * p:128 * (p + 1)] = jnp.concatenate([carry[6 * p + 1], carry[6 * p + 4]], axis=0).T.astype(dv_ref.dtype)
            if dqk == HEAD_DIM:
                dk_ref[:, 128 * p:128 * (p + 1)] = (jnp.concatenate([carry[6 * p], carry[6 * p + 3]], axis=0).T
                                                    * (1.0 / LOG2E)).astype(dk_ref.dtype)
        if dqk != HEAD_DIM:
            for h in range(N_HEADS):
                dk_ref[:, 128 * h:128 * (h + 1)] = (carry[3 * h].T * (1.0 / LOG2E)).astype(dk_ref.dtype)
        if bias:
            dck_ref[...] = jnp.zeros_like(dck_ref)
            for h in range(N_HEADS):
                dck_ref[:, h:h + 1] = -carry[3 * h + 2]
        if comm is not None:
            @pl.when(j == nkb - 1)
            def _():
                comm.wait(*cc)

    blk = lambda w: pl.BlockSpec((tk, w), lambda j: (j, 0))
    full3 = lambda w: pl.BlockSpec((nq, w, TQ), lambda j: (0, 0, 0))
    in_specs = [blk(wq), blk(GROUP), full3(wq), full3(GROUP), full3(8), full3(8)]
    args = [kb, vb, qt, dot, lse, dl]
    if merged:
        assert wq == GROUP
        out_specs = [full3(wq), pl.BlockSpec((tk, wq + GROUP), lambda j: (j, _DZ_FOX_KV))]
        out_shape = [SDS((nq, wq, TQ), F32), SDS(kv_into.shape, kv_into.dtype)]
    else:
        out_specs = [full3(wq), blk(wq), blk(GROUP)]
        out_shape = [SDS((nq, wq, TQ), F32), SDS((s, wq), kv_dtype), SDS((s, GROUP), kv_dtype)]
    if bias:
        in_specs.append(blk(128))
        args.append(ck2)
        out_specs += [blk(128), full3(8)]
        out_shape += [SDS((s, 128), F32), SDS((nq, 8, TQ), F32)]
    aliases = {}
    if merged:
        in_specs.append(pl.BlockSpec(memory_space=pl.ANY))
        args.append(kv_into)
        aliases = {len(args) - 1: 1}
    return _call_with_comm(body, (nkb,), in_specs, out_specs, out_shape, [], args, comm, ("arbitrary",), name, aliases)


def _untranspose(xt, dtype, name, into=None, col=0):
    nq, w, _ = xt.shape
    if into is not None:
        def body_into(x_ref, _, o_ref):
            o_ref[...] = x_ref[0].T.astype(o_ref.dtype)

        return pl.pallas_call(
            body_into, grid=(nq,),
            in_specs=[pl.BlockSpec((1, w, TQ), lambda i: (i, 0, 0)), pl.BlockSpec(memory_space=pl.ANY)],
            out_specs=pl.BlockSpec((TQ, w), lambda i: (i, col)), out_shape=SDS(into.shape, into.dtype),
            input_output_aliases={1: 0}, compiler_params=_cp("parallel"), name=name)(xt, into)

    def body(x_ref, o_ref):
        o_ref[...] = x_ref[0].T.astype(o_ref.dtype)

    return pl.pallas_call(
        body, grid=(nq,), in_specs=[pl.BlockSpec((1, w, TQ), lambda i: (i, 0, 0))],
        out_specs=pl.BlockSpec((TQ, w), lambda i: (i, 0)), out_shape=SDS((nq * TQ, w), dtype),
        compiler_params=_cp("parallel"), name=name)(xt)


_SCALE_D = (64 + 32) ** -0.5
_COL_CQ, _COL_CKV, _COL_MISC = 2304 // 256, 2560 // 128, 2688 // 128


def _mla_prep(z, gq, gkv, wq, wk, wv, tb, name):
    s = z.shape[0]
    tm = _key_block(s)
    row = lambda w, c: pl.BlockSpec((tm, w), lambda i, c=c: (i, c))
    const = lambda a: pl.BlockSpec(a.shape, lambda i: (0,) * a.ndim)

    def body(cq_ref, ckv_ref, m_ref, gq_ref, gkv_ref, wq_ref, wk_ref, wv_ref, e_ref, qc_ref, qs_ref, kc_ref, ks_ref,
             q_ref, k_ref, v_ref, vt_ref, cqn_ref, ckvn_ref, qt_ref):
        cqn = _rms(cq_ref[...], gq_ref[...]).astype(_MXU)
        ckvn = _rms(ckv_ref[...], gkv_ref[...]).astype(_MXU)
        cqn_ref[...] = cqn
        ckvn_ref[...] = ckvn
        qb = _rope(_dot(cqn, wq_ref[...]), _lanes(qc_ref[...], 512), _lanes(qs_ref[...], 512), 16).astype(q_ref.dtype)
        q_ref[...] = qb
        for b, t in enumerate(_scaled_qt(qb, _SCALE_D)):
            qt_ref[b] = t
        kr = _rope(m_ref[...], kc_ref[...], ks_ref[...], 16)
        k_ref[...] = (_dot(ckvn, wk_ref[...]) + _dot(kr, e_ref[...])).astype(k_ref.dtype)
        v = _dot(ckvn, wv_ref[...])
        v_ref[...] = v.astype(v_ref.dtype)
        vt_ref[0] = v.T.astype(vt_ref.dtype)

    e = tb["place"]
    return pl.pallas_call(
        body, grid=(s // tm,),
        in_specs=[row(256, _COL_CQ), row(128, _COL_CKV), row(128, _COL_MISC), const(gq), const(gkv), const(wq), const(wk),
                  const(wv), const(e), row(128, 0), row(128, 0), row(128, 0), row(128, 0)],
        out_specs=[row(512, 0), row(512, 0), row(256, 0), pl.BlockSpec((1, GROUP, tm), lambda i: (i, 0, 0)), row(256, 0),
                   row(128, 0), pl.BlockSpec((tm // TQ, 512, TQ), lambda i: (i, 0, 0))],
        out_shape=[SDS((s, 512), _MXU), SDS((s, 512), _MXU), SDS((s, 256), _MXU), SDS((s // tm, GROUP, tm), _MXU),
                   SDS((s, 256), _MXU), SDS((s, 128), _MXU), SDS((s // TQ, 512, TQ), _MXU)],
        compiler_params=_cp("parallel"), name=name)(
            z, z, z, gq, gkv, wq, wk, wv, e, tb["q_cos"], tb["q_sin"], tb["k_cos"], tb["k_sin"])


def _mla_prep_bwd(dqt, dk, dv, z, cqn, ckvn, gq, gkv, wq, wk, wv, tb, dz, name):
    s = z.shape[0]
    tm = min(512, s)
    row = lambda w, c: pl.BlockSpec((tm, w), lambda i, c=c: (i, c))
    const = lambda a: pl.BlockSpec(a.shape, lambda i: (0,) * a.ndim)
    acc = lambda shape: pl.BlockSpec(shape, lambda i: (0, 0))

    def body(dq_ref, dk_ref, dv_ref, cq_ref, ckv_ref, cqn_ref, ckvn_ref, gq_ref, gkv_ref, wq_ref, wk_ref, wv_ref, e_ref,
             qc_ref, qs_ref, kc_ref, ks_ref, _, dz_ref, dkr_ref, dwq_ref, dwk_ref, dwv_ref, dgq_ref, dgkv_ref):
        dcq_ref, dckv_ref = dz_ref.at[:, 0:256], dz_ref.at[:, 256:384]

        @pl.when(pl.program_id(0) == 0)
        def _():
            for r in (dwq_ref, dwk_ref, dwv_ref, dgq_ref, dgkv_ref):
                r[...] = jnp.zeros_like(r)

        dq = jnp.concatenate([dq_ref[b].T for b in range(tm // TQ)], axis=0)
        dqp = _rope_bwd(dq, _lanes(qc_ref[...], 512), _lanes(qs_ref[...], 512), 16)
        dkd = dk_ref[...]
        dvd = dv_ref[...]
        dwq_ref[...] += _dot_tn(cqn_ref[...], dqp)
        dwk_ref[...] += _dot_tn(ckvn_ref[...], dkd)
        dwv_ref[...] += _dot_tn(ckvn_ref[...], dvd)
        dcq, dgq = _rms_bwd(cq_ref[...], gq_ref[...], _dot_nt(dqp, wq_ref[...]))
        dckv, dgkv = _rms_bwd(ckv_ref[...], gkv_ref[...], _dot_nt(dkd, wk_ref[...]) + _dot_nt(dvd, wv_ref[...]))
        dcq_ref[...] = dcq.astype(dcq_ref.dtype)
        dckv_ref[...] = dckv.astype(dckv_ref.dtype)
        dgq_ref[...] += dgq
        dgkv_ref[...] += dgkv
        dkr = _dot_exact(dkd, e_ref[...], (((1,), (1,)), ((), ())))
        dkr_ref[...] = _rope_bwd(dkr, kc_ref[...], ks_ref[...], 16)

    e = tb["place"]
    return pl.pallas_call(
        body, grid=(s // tm,),
        in_specs=[pl.BlockSpec((tm // TQ, 512, TQ), lambda i: (i, 0, 0)), row(512, 0), row(256, 0), row(256, _COL_CQ),
                  row(128, _COL_CKV), row(256, 0), row(128, 0),
                  const(gq), const(gkv), const(wq), const(wk), const(wv), const(e), row(128, 0), row(128, 0), row(128, 0), row(128, 0),
                  pl.BlockSpec(memory_space=pl.ANY)],
        out_specs=[row(384, _DZ_MLA), row(128, 0), acc((256, 512)), acc((128, 512)), acc((128, 256)), acc((1, 256)),
                   acc((1, 128))],
        out_shape=[SDS(dz.shape, dz.dtype), SDS((s, 128), F32), SDS((256, 512), F32), SDS((128, 512), F32),
                   SDS((128, 256), F32), SDS((1, 256), F32), SDS((1, 128), F32)],
        input_output_aliases={17: 0}, compiler_params=_cp("arbitrary"), name=name)(
            dqt, dk, dv, z, z, cqn, ckvn, gq, gkv, wq, wk, wv, e, tb["q_cos"], tb["q_sin"], tb["k_cos"], tb["k_sin"], dz)


def _out_proj(ys, g, w, x, name):
    s, d = x.shape
    tm = min(512, s)

    def body(ya, yb, yc, yd, g_ref, w_ref, x_ref, o_ref, yn_ref):
        acc = x_ref[...]
        for i, y_ref in enumerate((ya, yb, yc, yd)):
            sl = slice(GROUP * i, GROUP * (i + 1))
            yn = _rms(y_ref[...], g_ref[:, sl]).astype(_MXU)
            yn_ref[:, sl] = yn
            acc = acc + jnp.dot(yn, w_ref[sl, :], preferred_element_type=F32)
        o_ref[...] = acc

    yspec = pl.BlockSpec((tm, GROUP), lambda i: (i, 0))
    return pl.pallas_call(
        body, grid=(s // tm,),
        in_specs=[yspec, yspec, yspec, yspec, pl.BlockSpec((1, d), lambda i: (0, 0)), pl.BlockSpec((d, d), lambda i: (0, 0)),
                  pl.BlockSpec((tm, d), lambda i: (i, 0))],
        out_specs=[pl.BlockSpec((tm, d), lambda i: (i, 0)), pl.BlockSpec((tm, d), lambda i: (i, 0))],
        out_shape=[SDS((s, d), F32), SDS((s, d), _MXU)], compiler_params=_cp("parallel"), name=name)(*ys, g, w, x)


def _out_proj_bwd(dx, w, ys, g, name):
    s, d = dx.shape
    tm = min(512, s)
    nb = tm // TQ

    def body(dx_ref, w_ref, ya, yb, yc, yd, g_ref, da, db, dg_ref, dtc_ref, dtd_ref, dlc_ref, dld_ref):
        @pl.when(pl.program_id(0) == 0)
        def _():
            dg_ref[...] = jnp.zeros_like(dg_ref)

        dyn = _dot_nt(dx_ref[...], w_ref[...])
        for i, y_ref in enumerate((ya, yb, yc, yd)):
            sl = slice(GROUP * i, GROUP * (i + 1))
            y = y_ref[...]
            dy, dg = _rms_bwd(y, g_ref[:, sl], dyn[:, sl])
            dg_ref[:, sl] += dg
            if i < 2:
                (da, db)[i][...] = dy
                continue
            dt_ref, dl_ref = ((dtc_ref, dlc_ref), (dtd_ref, dld_ref))[i - 2]
            dl_ref[...] = jnp.zeros_like(dl_ref)
            for b in range(nb):
                r = slice(TQ * b, TQ * (b + 1))
                dt_ref[b] = dy[r].T.astype(dt_ref.dtype)
                pt = (dy[r] * y[r]).T
                for h in range(N_HEADS):
                    dl_ref[b, h:h + 1, :] = jnp.sum(pt[HEAD_DIM * h:HEAD_DIM * (h + 1), :], axis=0, keepdims=True)

    yspec = pl.BlockSpec((tm, GROUP), lambda i: (i, 0))
    tspec = pl.BlockSpec((nb, GROUP, TQ), lambda i: (i, 0, 0))
    lspec = pl.BlockSpec((nb, 8, TQ), lambda i: (i, 0, 0))
    return pl.pallas_call(
        body, grid=(s // tm,),
        in_specs=[pl.BlockSpec((tm, d), lambda i: (i, 0)), pl.BlockSpec((d, d), lambda i: (0, 0)), yspec, yspec, yspec, yspec,
                  pl.BlockSpec((1, d), lambda i: (0, 0))],
        out_specs=[yspec, yspec, pl.BlockSpec((1, d), lambda i: (0, 0)), tspec, tspec, lspec, lspec],
        out_shape=[SDS((s, GROUP), F32)] * 2 + [SDS((1, d), F32)] + [SDS((s // TQ, GROUP, TQ), _MXU)] * 2
        + [SDS((s // TQ, 8, TQ), F32)] * 2,
        compiler_params=_cp("arbitrary"), name=name)(dx, w, *ys, g)


FF_BLOCK = 512
FF_ROWS = 1024


def _ffn_fwd(x, g, wu, wd, name, comm=None):
    s, d = x.shape
    nj = wu.shape[0]
    tm = min(FF_ROWS, s)
    ni = s // tm

    def body(*refs):
        (x_ref, g_ref, wu_ref, wd_ref), (o_ref, u_ref, h_ref), (acc,), cc = _split_refs(refs, 4, 3, comm)
        i, j = pl.program_id(0), pl.program_id(1)
        _host_gather(comm, cc, i * nj + j, ni * nj)

        @pl.when(j == 0)
        def _():
            h_ref[...] = _rms(x_ref[...], g_ref[...]).astype(h_ref.dtype)
            acc[...] = jnp.zeros_like(acc)

        halves = [slice(r, r + tm // 2) for r in range(0, tm, tm // 2)]
        us = [jnp.dot(h_ref[r, :], wu_ref[0], preferred_element_type=F32) for r in halves]
        for r, u in zip(halves, us):
            u_ref[r, :] = u.astype(u_ref.dtype)
            acc[r, :] += _dot(jnp.square(jnp.maximum(u, 0.0)), wd_ref[...])

        @pl.when(j == nj - 1)
        def _():
            o_ref[...] = x_ref[...] + acc[...]

        if comm is not None:
            @pl.when((i == ni - 1) & (j == nj - 1))
            def _():
                comm.wait(*cc)

    in_specs = [pl.BlockSpec((tm, d), lambda i, j: (i, 0)), pl.BlockSpec((1, d), lambda i, j: (0, 0)),
                pl.BlockSpec((1, d, FF_BLOCK), lambda i, j: (j, 0, 0)), pl.BlockSpec((FF_BLOCK, d), lambda i, j: (j, 0))]
    out_specs = [pl.BlockSpec((tm, d), lambda i, j: (i, 0)), pl.BlockSpec((tm, FF_BLOCK), lambda i, j: (i, j)),
                 pl.BlockSpec((tm, d), lambda i, j: (i, 0))]
    out_shape = [SDS((s, d), F32), SDS((s, nj * FF_BLOCK), _MXU), SDS((s, d), _MXU)]
    return _call_with_comm(body, (ni, nj), in_specs, out_specs, out_shape, [pltpu.VMEM((tm, d), F32)], [x, g, wu, wd], comm,
                           ("arbitrary", "arbitrary"), name)


def _ffn_bwd(dx2, x, u, g, wu, wd, name, comm=None):
    s, d = x.shape
    nj = wu.shape[0]
    tm = min(FF_ROWS, s)
    ni = s // tm

    def body(*refs):
        (dx_ref, x_ref, u_ref, g_ref, wu_ref, wd_ref), (o_ref, du_ref, dg_ref), (acc, dxb), cc = _split_refs(refs, 6, 3, comm)
        i, j = pl.program_id(0), pl.program_id(1)

        @pl.when((i == 0) & (j == 0))
        def _():
            if comm is not None:
                comm.start(*cc)
            dg_ref[...] = jnp.zeros_like(dg_ref)

        @pl.when(j == 0)
        def _():
            dxb[...] = dx_ref[...].astype(dxb.dtype)
            acc[...] = jnp.zeros_like(acc)

        nt = (((1,), (1,)), ((), ()))
        halves = [slice(r, r + tm // 2) for r in range(0, tm, tm // 2)]
        das = [lax.dot_general(dxb[r, :], wd_ref[...], nt, preferred_element_type=F32) for r in halves]
        for r, da in zip(halves, das):
            du = (da * 2.0 * jnp.maximum(u_ref[r, :].astype(F32), 0.0)).astype(du_ref.dtype)
            du_ref[r, :] = du
            acc[r, :] += lax.dot_general(du, wu_ref[0], nt, preferred_element_type=F32)

        @pl.when(j == nj - 1)
        def _():
            dxn, dg = _rms_bwd(x_ref[...], g_ref[...], acc[...])
            o_ref[...] = dx_ref[...] + dxn
            dg_ref[...] += dg

        if comm is not None:
            @pl.when((i == ni - 1) & (j == nj - 1))
            def _():
                comm.wait(*cc)

    in_specs = [pl.BlockSpec((tm, d), lambda i, j: (i, 0)), pl.BlockSpec((tm, d), lambda i, j: (i, 0)),
                pl.BlockSpec((tm, FF_BLOCK), lambda i, j: (i, j)), pl.BlockSpec((1, d), lambda i, j: (0, 0)),
                pl.BlockSpec((1, d, FF_BLOCK), lambda i, j: (j, 0, 0)), pl.BlockSpec((FF_BLOCK, d), lambda i, j: (j, 0))]
    out_specs = [pl.BlockSpec((tm, d), lambda i, j: (i, 0)), pl.BlockSpec((tm, FF_BLOCK), lambda i, j: (i, j)),
                 pl.BlockSpec((1, d), lambda i, j: (0, 0))]
    out_shape = [SDS((s, d), F32), SDS((s, nj * FF_BLOCK), _MXU), SDS((1, d), F32)]
    return _call_with_comm(body, (ni, nj), in_specs, out_specs, out_shape,
                           [pltpu.VMEM((tm, d), F32), pltpu.VMEM((tm, d), _MXU)], [dx2, x, u, g, wu, wd], comm,
                           ("arbitrary", "arbitrary"), name)


def _in_proj_bwd(dz, w, x, g, dx_up, name, comm=None):
    s, d = x.shape
    n = w.shape[1]
    tm = min(512, s)
    ni = s // tm

    def body(*refs):
        (dz_ref, w_ref, x_ref, g_ref, up_ref), (o_ref, dg_ref), _, cc = _split_refs(refs, 5, 2, comm)
        i = pl.program_id(0)

        @pl.when(i == 0)
        def _():
            if comm is not None:
                comm.start(*cc)
            dg_ref[...] = jnp.zeros_like(dg_ref)

        dh = lax.dot_general(dz_ref[...], w_ref[...], (((1,), (1,)), ((), ())), preferred_element_type=F32)
        dxn, dg = _rms_bwd(x_ref[...], g_ref[...], dh)
        o_ref[...] = up_ref[...] + dxn
        dg_ref[...] += dg
        if comm is not None:
            @pl.when(i == ni - 1)
            def _():
                comm.wait(*cc)

    in_specs = [pl.BlockSpec((tm, n), lambda i: (i, 0)), pl.BlockSpec((d, n), lambda i: (0, 0)),
                pl.BlockSpec((tm, d), lambda i: (i, 0)), pl.BlockSpec((1, d), lambda i: (0, 0)),
                pl.BlockSpec((tm, d), lambda i: (i, 0))]
    out_specs = [pl.BlockSpec((tm, d), lambda i: (i, 0)), pl.BlockSpec((1, d), lambda i: (0, 0))]
    out_shape = [SDS((s, d), F32), SDS((1, d), F32)]
    return _call_with_comm(body, (ni,), in_specs, out_specs, out_shape, [], [dz, w, x, g, dx_up], comm, ("arbitrary",), name)


def _loss_head(x, g, target, name):
    s, d = x.shape
    tm = min(512, s)

    def body(x_ref, g_ref, t_ref, l_ref, dx_ref, dg_ref):
        @pl.when(pl.program_id(0) == 0)
        def _():
            l_ref[...] = jnp.zeros_like(l_ref)
            dg_ref[...] = jnp.zeros_like(dg_ref)

        xv = x_ref[...]
        err = _rms(xv, g_ref[...]) - t_ref[...]
        l_ref[...] += jnp.sum(err * err, axis=0, keepdims=True) * (0.5 / d)
        dx, dg = _rms_bwd(xv, g_ref[...], err * (1.0 / d))
        dx_ref[...] = dx
        dg_ref[...] += dg

    return pl.pallas_call(
        body, grid=(s // tm,),
        in_specs=[pl.BlockSpec((tm, d), lambda i: (i, 0)), pl.BlockSpec((1, d), lambda i: (0, 0)),
                  pl.BlockSpec((tm, d), lambda i: (i, 0))],
        out_specs=[pl.BlockSpec((1, d), lambda i: (0, 0)), pl.BlockSpec((tm, d), lambda i: (i, 0)),
                   pl.BlockSpec((1, d), lambda i: (0, 0))],
        out_shape=[SDS((1, d), F32), SDS((s, d), F32), SDS((1, d), F32)], compiler_params=_cp("arbitrary"), name=name)(x, g, target)


def _me_and_peer():
    x, y, c = lax.axis_index("x"), lax.axis_index("y"), lax.axis_index("c")
    me = 4 * x + 2 * y + c

    def peer(k):
        px, py, pc = x ^ (k >> 2), y ^ ((k >> 1) & 1), c ^ (k & 1)
        return (px, py, pc), 4 * px + 2 * py + pc

    return me, peer


class _Comm:
    CHIPS = (2, 4, 6)

    def __init__(self, kind, arrs):
        assert kind in ("gather", "exchange")
        self.kind, self.arrs, self.n = kind, list(arrs), len(arrs)
        anyspec = pl.BlockSpec(memory_space=pl.ANY)
        self.in_specs = [anyspec] * self.n
        self.out_specs = [anyspec] * self.n
        self.out_shape = [SDS(((NDEV,) + a.shape) if kind == "gather" else a.shape, a.dtype) for a in self.arrs]
        npair = NDEV - 1 + len(self.CHIPS)
        self.scratch = [pltpu.SemaphoreType.DMA((self.n, npair)), pltpu.SemaphoreType.DMA((self.n, npair)),
                        pltpu.SemaphoreType.DMA((self.n,))]

    def _copies(self, ins, outs, sems):
        send, recv, loc = sems
        me, peer = _me_and_peer()
        gather = self.kind == "gather"
        sibling = peer(1)[0]
        local = [pltpu.make_async_copy(ins[a] if gather else ins[a].at[me], outs[a].at[me], loc.at[a]) for a in range(self.n)]
        outgoing, incoming, forwards, forwarded = [], [], [], []
        for k in ((1,) + self.CHIPS) if gather else range(1, NDEV):
            dev, pid = peer(k)
            for a in range(self.n):
                pair = dict(send_sem=send.at[a, k - 1], recv_sem=recv.at[a, k - 1], device_id=dev, device_id_type=MESH)
                outgoing.append(pltpu.make_async_remote_copy(src_ref=ins[a] if gather else ins[a].at[pid],
                                                             dst_ref=outs[a].at[me], **pair))
                incoming.append(pltpu.make_async_remote_copy(src_ref=ins[a] if gather else ins[a].at[me],
                                                             dst_ref=outs[a].at[pid], **pair))
        if gather:
            for idx, k in enumerate(self.CHIPS):
                got, theirs = peer(k)[1], peer(k + 1)[1]
                for a in range(self.n):
                    pair = dict(send_sem=send.at[a, NDEV - 1 + idx], recv_sem=recv.at[a, NDEV - 1 + idx], device_id=sibling,
                                device_id_type=MESH)
                    forwards.append(pltpu.make_async_remote_copy(src_ref=outs[a].at[got], dst_ref=outs[a].at[got], **pair))
                    forwarded.append(pltpu.make_async_remote_copy(src_ref=outs[a].at[theirs], dst_ref=outs[a].at[theirs], **pair))
        return local, outgoing, incoming, forwards, forwarded

    def start(self, ins, outs, sems):
        local, outgoing, _, _, _ = self._copies(ins, outs, sems)
        for cp in local + outgoing:
            cp.start()

    def forward(self, ins, outs, sems):
        _, _, incoming, forwards, _ = self._copies(ins, outs, sems)
        per = self.n
        for idx in range(len(forwards) // per if per else 0):
            for a in range(per):
                incoming[(1 + idx) * per + a].wait_recv()
                forwards[idx * per + a].start()

    def wait(self, ins, outs, sems):
        local, outgoing, incoming, forwards, forwarded = self._copies(ins, outs, sems)
        for cp in (incoming[:self.n] if self.kind == "gather" else incoming) + forwarded:
            cp.wait_recv()
        for cp in outgoing + forwards:
            cp.wait_send()
        for cp in local:
            cp.wait()


LATE_FORWARD_BYTES = 3 << 19


def _host_gather(comm, cc, step, nsteps, late=None):
    if comm is None:
        return
    if late is None:
        late = sum(a.size * a.dtype.itemsize for a in comm.arrs) > LATE_FORWARD_BYTES

    @pl.when(step == 0)
    def _():
        comm.start(*cc)

    @pl.when(step == (nsteps - 1 if late else (2 * nsteps) // 3))
    def _():
        comm.forward(*cc)


def _split_refs(refs, n_in, n_out, comm):
    c = comm.n if comm is not None else 0
    ins, cin = refs[:n_in], refs[n_in:n_in + c]
    outs, cout = refs[n_in + c:n_in + c + n_out], refs[n_in + c + n_out:n_in + 2 * c + n_out]
    rest = refs[n_in + 2 * c + n_out:]
    scratch, csem = (rest[:len(rest) - 3], rest[len(rest) - 3:]) if c else (rest, ())
    return ins, outs, scratch, (cin, cout, csem)


def _comm_call(kind, arrs, name):
    comm = _Comm(kind, arrs)

    def body(*refs):
        _, _, _, c = _split_refs(refs, 0, 0, comm)
        comm.start(*c)
        if kind == "gather":
            comm.forward(*c)
        comm.wait(*c)

    return pl.pallas_call(body, in_specs=comm.in_specs, out_specs=comm.out_specs, out_shape=comm.out_shape,
                          scratch_shapes=comm.scratch, compiler_params=pltpu.CompilerParams(has_side_effects=True),
                          name=name)(*arrs)


def _all_gather(arrs, name):
    return _comm_call("gather", arrs, name)


def _exchange(arrs, name):
    return _comm_call("exchange", arrs, name)


def _sum_slots(parts, name):
    _, r, c = parts.shape
    tr = r if r <= 512 else 512

    def body(p_ref, o_ref):
        acc = p_ref[0].astype(F32)
        for q in range(1, NDEV):
            acc = acc + p_ref[q].astype(F32)
        o_ref[...] = acc

    return pl.pallas_call(
        body, grid=(r // tr,), in_specs=[pl.BlockSpec((NDEV, tr, c), lambda i: (0, i, 0))],
        out_specs=pl.BlockSpec((tr, c), lambda i: (i, 0)), out_shape=SDS((r, c), F32),
        compiler_params=_cp("parallel"), name=name)(parts)


def _adamw(g, w, m, v, name):
    r, c = w.shape
    parts = g.ndim == 3
    tr = r
    for cand in (512, 256, 128, 64, 32, 16, 8):
        if r > cand and r % cand == 0 and cand * c * 4 <= 2 * 1024 * 1024:
            tr = cand
            break
    bc1 = 1.0 / (1.0 - ADAM_B1 ** ADAM_STEP)
    bc2 = 1.0 / (1.0 - ADAM_B2 ** ADAM_STEP)

    def body(g_ref, w_ref, m_ref, v_ref, go_ref, d_ref, mo_ref, vo_ref):
        if parts:
            gv = g_ref[0].astype(F32)
            for q in range(1, NDEV):
                gv = gv + g_ref[q].astype(F32)
        else:
            gv = g_ref[...]
        mn = ADAM_B1 * m_ref[...] + (1.0 - ADAM_B1) * gv
        vn = ADAM_B2 * v_ref[...] + (1.0 - ADAM_B2) * (gv * gv)
        go_ref[...] = gv
        mo_ref[...] = mn
        vo_ref[...] = vn
        d_ref[...] = -ADAM_LR * ((mn * bc1) / (jnp.sqrt(vn * bc2) + ADAM_EPS) + ADAM_WD * w_ref[...])

    spec = pl.BlockSpec((tr, c), lambda i: (i, 0))
    gspec = pl.BlockSpec((NDEV, tr, c), lambda i: (0, i, 0)) if parts else spec
    return pl.pallas_call(
        body, grid=(r // tr,), in_specs=[gspec, spec, spec, spec], out_specs=[spec] * 4,
        out_shape=[SDS((r, c), F32)] * 4, compiler_params=_cp("parallel"), name=name)(g, w, m, v)


def _adamw_layer(parts, w, m, v, l, prev, name):
    r, c = parts.shape[1:]
    rows = w.shape[0]
    tr = next(t for t in (512, 256, 128, 64, 32, 16, 8) if r % t == 0 and t * c * 4 <= 2 * 1024 * 1024)
    bc1 = 1.0 / (1.0 - ADAM_B1 ** ADAM_STEP)
    bc2 = 1.0 / (1.0 - ADAM_B2 ** ADAM_STEP)

    def body(g_ref, w_ref, m_ref, v_ref, *rest):
        go_ref, d_ref, mo_ref, vo_ref = rest[-4:]
        gv = g_ref[0].astype(F32)
        for q in range(1, NDEV):
            gv = gv + g_ref[q].astype(F32)
        mn = ADAM_B1 * m_ref[...] + (1.0 - ADAM_B1) * gv
        vn = ADAM_B2 * v_ref[...] + (1.0 - ADAM_B2) * (gv * gv)
        go_ref[...] = gv
        mo_ref[...] = mn
        vo_ref[...] = vn
        d_ref[...] = -ADAM_LR * ((mn * bc1) / (jnp.sqrt(vn * bc2) + ADAM_EPS) + ADAM_WD * w_ref[...])

    spec = pl.BlockSpec((tr, c), lambda i: (l * (r // tr) + i, 0))
    in_specs = [pl.BlockSpec((NDEV, tr, c), lambda i: (0, i, 0)), spec, spec, spec]
    args = [parts, w, m, v]
    aliases = {}
    if prev is not None:
        in_specs += [pl.BlockSpec(memory_space=pl.ANY)] * 4
        args += list(prev)
        aliases = {4 + k: k for k in range(4)}
    return pl.pallas_call(
        body, grid=(r // tr,), in_specs=in_specs, out_specs=[spec] * 4, out_shape=[SDS((rows, c), F32)] * 4,
        input_output_aliases=aliases, compiler_params=_cp("parallel"), name=name)(*args)


def _pad_in_cols(w):
    r = w.shape[0]
    zeros = lambda n: jnp.zeros((r, n), w.dtype)
    return jnp.concatenate([w[:, 512:1536], w[:, 0:512], w[:, 1792:2304], w[:, 1536:1792], w[:, 2308:2692], w[:, 2304:2308],
                            zeros(28), w[:, 2692:2724], zeros(64)], axis=1)


def _unpad_in_cols(w):
    return jnp.concatenate([w[..., 1024:1536], w[..., 0:1024], w[..., 2048:2304], w[..., 1536:2048], w[..., 2688:2692],
                            w[..., 2304:2688], w[..., 2720:2752]], axis=-1)


_Z_RET = (0, 1, 2, 3)
_Z_SGU = (4, 5)
_Z_FOX_Q, _Z_FOX_K, _Z_FOX_V = 8, 6, 7
_DZ_RET, _DZ_SGU, _DZ_FOX_KV, _DZ_FOX_Q, _DZ_MLA, _DZ_MISC = 0, 2, 3, 8, 6, 21


def _pad_uq(w):
    return jnp.pad(w.reshape(256, N_HEADS, 96), ((0, 0), (0, 0), (0, 32))).reshape(256, 512)


def _unpad_uq(w):
    return w.reshape(256, N_HEADS, 128)[:, :, :96].reshape(256, 384)


def _split_ukv(w):
    r = w.reshape(128, N_HEADS, 128)
    return jnp.pad(r[:, :, :64], ((0, 0), (0, 0), (0, 64))).reshape(128, 512), r[:, :, 64:].reshape(128, 256)


def _join_ukv(dk, dv):
    return jnp.concatenate([dk.reshape(128, N_HEADS, 128)[:, :, :64], dv.reshape(128, N_HEADS, 64)], axis=-1).reshape(128, 512)


def _cols_to_full(g):
    return jnp.transpose(g, (1, 0, 2)).reshape(g.shape[1], NDEV * g.shape[2])


def kernel(x, g_mix_norm, w_in, b_forget, g_sgu, w_spatial, b_spatial, g_mla_q, w_uq, g_mla_kv, w_ukv, g_group_out, w_out, g_ffn_norm, w_up, w_down, g_final, loss_target, m_g_mix_norm, m_w_in, m_b_forget, m_g_sgu, m_w_spatial, m_b_spatial, m_g_mla_q, m_w_uq, m_g_mla_kv, m_w_ukv, m_g_group_out, m_w_out, m_g_ffn_norm, m_w_up, m_w_down, m_g_final, v_g_mix_norm, v_w_in, v_b_forget, v_g_sgu, v_w_spatial, v_b_spatial, v_g_mla_q, v_w_uq, v_g_mla_kv, v_w_ukv, v_g_group_out, v_w_out, v_g_ffn_norm, v_w_up, v_w_down, v_g_final):
    depth = w_in.shape[0]
    s, d = x.shape[1], x.shape[2]
    x0 = x.reshape(s, d)
    target = loss_target.reshape(s, d)
    tb = _tables(s)
    me = 4 * lax.axis_index("x") + 2 * lax.axis_index("y") + lax.axis_index("c")

    assert depth == 2
    shards = {}
    for l in range(depth):
        shards.update({(l, "w_in"): _pad_in_cols(w_in[l]).astype(_WIRE), (l, "w_out"): w_out[l].astype(_WIRE),
                       (l, "w_up"): w_up[l].astype(_WIRE), (l, "w_down"): w_down[l].astype(_WIRE),
                       (l, "w_uq"): w_uq[l].astype(_WIRE), (l, "w_ukv"): w_ukv[l].astype(_WIRE)})
    wts = _ShardedWeights(shards)
    first = [(0, "w_in"), (0, "w_uq"), (0, "w_ukv"), (1, "w_uq"), (1, "w_ukv")]
    wts.full.update(zip(first, _all_gather([shards[k] for k in first], "gather_first")))

    row = lambda a: a.reshape(1, -1)

    def small(l):
        bf = jnp.pad(b_forget[l].reshape(1, N_HEADS), ((0, 0), (0, 128 - N_HEADS)))
        bt = jnp.pad(b_spatial[l].T, ((0, 0), (0, 128 - N_HEADS)))
        return dict(g_mix=row(g_mix_norm[l]), g_sgu=row(g_sgu[l]), w_s=w_spatial[l], b_t=bt, b_f=bf, gq=row(g_mla_q[l]),
                    gkv=row(g_mla_kv[l]), g_go=row(g_group_out[l]), g_ffn=row(g_ffn_norm[l]))

    smalls = [small(l) for l in range(depth)]
    lrow, dx, sm, dg_final = _local_step(x0, target, wts, smalls, row(g_final), tb)
    loss = lax.psum(jnp.sum(lrow), AXES)
    grad_x = dx.reshape(1, s, d)
    return _reduce_and_update(loss, grad_x, wts.recv, sm, dg_final, me, dict(
        g_mix_norm=(g_mix_norm, m_g_mix_norm, v_g_mix_norm), w_in=(w_in, m_w_in, v_w_in),
        b_forget=(b_forget, m_b_forget, v_b_forget), g_sgu=(g_sgu, m_g_sgu, v_g_sgu),
        w_spatial=(w_spatial, m_w_spatial, v_w_spatial), b_spatial=(b_spatial, m_b_spatial, v_b_spatial),
        g_mla_q=(g_mla_q, m_g_mla_q, v_g_mla_q), w_uq=(w_uq, m_w_uq, v_w_uq), g_mla_kv=(g_mla_kv, m_g_mla_kv, v_g_mla_kv),
        w_ukv=(w_ukv, m_w_ukv, v_w_ukv), g_group_out=(g_group_out, m_g_group_out, v_g_group_out),
        w_out=(w_out, m_w_out, v_w_out), g_ffn_norm=(g_ffn_norm, m_g_ffn_norm, v_g_ffn_norm), w_up=(w_up, m_w_up, v_w_up),
        w_down=(w_down, m_w_down, v_w_down), g_final=(g_final, m_g_final, v_g_final)))


_GATHER_AT = {
    "in_proj0": [(0, "w_out")],
    "fox_attn0": [(0, "w_down"), (0, "w_up")],
    "mla_attn0": [(1, "w_in")],
    "ffn_fwd0": [(1, "w_down")],
    "fox_attn1": [(1, "w_out")],
    "mla_attn1": [(1, "w_up")],
}
_SCATTER_AT = {
    "fox_attn_bwd1": [(1, "w_down")],
    "mla_attn_bwd1": [(1, "w_up"), (1, "w_out")],
    "ffn_bwd0": [(1, "w_in")],
    "fox_attn_bwd0": [(0, "w_down")],
    "mla_attn_bwd0": [(0, "w_up"), (0, "w_out")],
    "in_proj_bwd0": [(0, "w_in")],
}


class _FullWeights:
    def __init__(self, per_layer):
        self.per_layer, self.grads = per_layer, {}

    def get(self, l, name):
        return self.per_layer[l][name]

    def comm(self, host):
        return None

    def done(self, host, results):
        pass

    def grad(self, l, name, blocks):
        self.grads[(l, name)] = blocks


class _ShardedWeights(_FullWeights):
    def __init__(self, shards):
        self.shards, self.full, self.grads, self.recv = shards, {}, {}, {}

    def get(self, l, name):
        if name in ("wk", "wv"):
            return _split_ukv(_cols_to_full(self.full[(l, "w_ukv")]))[0 if name == "wk" else 1]
        if name == "wq":
            return _pad_uq(_cols_to_full(self.full[(l, "w_uq")]))
        g = self.full[(l, name)]
        return g if name == "w_up" else g.reshape(NDEV * g.shape[1], g.shape[2])

    def comm(self, host):
        if host in _GATHER_AT:
            return _Comm("gather", [self.shards[k] for k in _GATHER_AT[host]])
        if host in _SCATTER_AT:
            return _Comm("exchange", [self.grads[k] for k in _SCATTER_AT[host]])
        return None

    def done(self, host, results):
        if host in _GATHER_AT:
            self.full.update(zip(_GATHER_AT[host], results))
        if host in _SCATTER_AT:
            self.recv.update(zip(_SCATTER_AT[host], results))


def _local_step(x0, target, wts, smalls, g_final, tb):
    depth = len(smalls)
    s, d = x0.shape
    saved = []
    xl = x0
    for l in range(depth):
        p = smalls[l]
        z, h, got = _norm_matmul(xl, p["g_mix"], wts.get(l, "w_in"), f"in_proj{l}", wts.comm(f"in_proj{l}"))
        wts.done(f"in_proj{l}", got)
        ya = _sgu_fwd(z, p["g_sgu"], p["w_s"], p["b_t"], tb, f"sgu_fwd{l}")
        yb, ret, states = _ret_fwd(z, tb, f"ret_fwd{l}")
        cum = _fox_prep(z, p["b_f"], f"fox_prep{l}")
        kc, vc, vtc, qtc = _kv_prep(z, _Z_FOX_Q, _Z_FOX_K, _Z_FOX_V, HEAD_DIM ** -0.5, f"fox_kv{l}")
        yc, lse_c, got = _attn_fwd(z, _Z_FOX_Q, HEAD_DIM, kc, vtc, HEAD_DIM ** -0.5, cum, f"fox_attn{l}", wts.comm(f"fox_attn{l}"))
        wts.done(f"fox_attn{l}", got)
        wq, wk, wv = wts.get(l, "wq"), wts.get(l, "wk"), wts.get(l, "wv")
        qd, kd, vd, vtd, cqn, ckvn, qtd = _mla_prep(z, p["gq"], p["gkv"], wq, wk, wv, tb, f"mla_prep{l}")
        yd, lse_d, got = _attn_fwd(qd, 0, 128, kd, vtd, _SCALE_D, None, f"mla_attn{l}", wts.comm(f"mla_attn{l}"))
        wts.done(f"mla_attn{l}", got)
        ys = (ya, yb, yc, yd)
        x1, yn = _out_proj(ys, p["g_go"], wts.get(l, "w_out"), xl, f"out_proj{l}")
        x2, u, h2, got = _ffn_fwd(x1, p["g_ffn"], wts.get(l, "w_up"), wts.get(l, "w_down"), f"ffn_fwd{l}", wts.comm(f"ffn_fwd{l}"))
        wts.done(f"ffn_fwd{l}", got)
        saved.append(dict(x=xl, z=z, h=h, ys=ys, ret=ret, states=states, cum=cum, lse_c=lse_c, kc=kc, vc=vc, qd=qd, kd=kd, vd=vd,
                          cqn=cqn, ckvn=ckvn, lse_d=lse_d, x1=x1, yn=yn, u=u, h2=h2, wq=wq, wk=wk, wv=wv, qtc=qtc, qtd=qtd))
        xl = x2

    lrow, dx, dg_final = _loss_head(xl, g_final, target, "loss_head")

    sm = [None] * depth
    for l in reversed(range(depth)):
        p, a = smalls[l], saved[l]
        dx1, du, dg_ffn, got = _ffn_bwd(dx, a["x1"], a["u"], p["g_ffn"], wts.get(l, "w_up"), wts.get(l, "w_down"), f"ffn_bwd{l}",
                                        wts.comm(f"ffn_bwd{l}"))
        wts.done(f"ffn_bwd{l}", got)
        dw_down = _mm_tn(a["u"], dx, f"dw_down{l}", a_fn=lambda t: jnp.square(jnp.maximum(t, 0.0)), out_dtype=_WIRE)
        wts.grad(l, "w_down", dw_down.reshape(NDEV, dw_down.shape[0] // NDEV, d))
        wts.grad(l, "w_up", _mm_tn(a["h2"], du, f"dw_up{l}", blocked=True, out_dtype=_WIRE))
        dya, dyb, dg_go, dot_c, dot_d, dl_c, dl_d = _out_proj_bwd(dx1, wts.get(l, "w_out"), a["ys"], p["g_go"],
                                                                  f"out_proj_bwd{l}")
        wts.grad(l, "w_out", _mm_tn(a["yn"], dx1, f"dw_out{l}", out_dtype=_WIRE).reshape(NDEV, d // NDEV, d))
        dz = _ret_bwd(dyb, a["z"], a["ret"], a["states"], tb, f"ret_bwd{l}")
        dz, dg_sgu, dw_s, db_t = _sgu_bwd(dya, a["z"], p["g_sgu"], p["w_s"], p["b_t"], tb, dz, f"sgu_bwd{l}")
        dqt_c, dz, dck, dcq, got = _attn_bwd(a["kc"], a["vc"], a["qtc"], dot_c, a["lse_c"], dl_c, HEAD_DIM,
                                             HEAD_DIM ** -0.5, a["cum"], f"fox_attn_bwd{l}", _MXU,
                                             wts.comm(f"fox_attn_bwd{l}"), kv_into=dz)
        wts.done(f"fox_attn_bwd{l}", got)
        dz = _untranspose(dqt_c, _MXU, f"fox_dq{l}", into=dz, col=_DZ_FOX_Q)
        dqt_d, dk_d, dv_d, got = _attn_bwd(a["kd"], a["vd"], a["qtd"], dot_d, a["lse_d"], dl_d, 128, _SCALE_D, None,
                                           f"mla_attn_bwd{l}", F32, wts.comm(f"mla_attn_bwd{l}"))
        wts.done(f"mla_attn_bwd{l}", got)
        dz, dkr, dwq, dwk, dwv, dgq, dgkv = _mla_prep_bwd(dqt_d, dk_d, dv_d, a["z"], a["cqn"], a["ckvn"], p["gq"], p["gkv"],
                                                          a["wq"], a["wk"], a["wv"], tb, dz, f"mla_prep_bwd{l}")
        dz, db_f = _fox_post(dcq, dck, a["z"], p["b_f"], dkr, dz, f"fox_post{l}")
        wts.grad(l, "w_in", _unpad_in_cols(_mm_tn(a["h"], dz, f"dw_in{l}", out_dtype=_WIRE)).reshape(NDEV, d // NDEV, N_IN))
        dx, dg_mix, got = _in_proj_bwd(dz, wts.get(l, "w_in"), a["x"], p["g_mix"], dx1, f"in_proj_bwd{l}",
                                       wts.comm(f"in_proj_bwd{l}"))
        wts.done(f"in_proj_bwd{l}", got)
        sm[l] = [dg_mix, dg_go, dg_ffn, dg_sgu, dw_s, db_t[:, :N_HEADS].T, db_f[0, :N_HEADS], dgq, dgkv, _unpad_uq(dwq),
                 _join_ukv(dwk, dwv)]
    return lrow, dx, sm, dg_final


def _reduce_and_update(loss, grad_x, recv, sm, dg_final, me, given):
    depth = len(sm)
    pieces = [t for l in range(depth) for t in sm[l]] + [dg_final]
    flat = jnp.concatenate([t.reshape(-1) for t in pieces])
    n_flat = flat.shape[0]
    unit = NDEV * 8 * 128
    n_pad = -(-n_flat // unit) * unit
    packed = jnp.pad(flat, (0, n_pad - n_flat)).reshape(NDEV, n_pad // (NDEV * 128), 128)
    red = _sum_slots(_exchange([packed], "scatter_small")[0], "sum_small")
    full = _all_gather([red], "gather_small")[0].reshape(-1)
    offs = np.cumsum([0] + [int(np.prod(t.shape)) for t in pieces])
    red_pieces = [full[int(offs[i]):int(offs[i + 1])].reshape(pieces[i].shape) for i in range(len(pieces))]
    per = len(sm[0])
    stack = lambda i: jnp.stack([red_pieces[l * per + i] for l in range(depth)])
    g_small = dict(g_mix_norm=stack(0), g_group_out=stack(1), g_ffn_norm=stack(2), g_sgu=stack(3), w_spatial=stack(4),
                   b_spatial=stack(5), b_forget=stack(6), g_mla_q=stack(7), g_mla_kv=stack(8), g_final=red_pieces[-1])
    cq, ckv = given["w_uq"][0].shape[2], given["w_ukv"][0].shape[2]
    g_small["w_uq"] = lax.dynamic_slice_in_dim(stack(9), me * cq, cq, axis=2)
    g_small["w_ukv"] = lax.dynamic_slice_in_dim(stack(10), me * ckv, ckv, axis=2)

    names = list(given)
    outs = {}
    for nme in names:
        wv_, mv_, vv_ = given[nme]
        shape = wv_.shape
        if nme in ("w_in", "w_out", "w_up", "w_down"):
            res = None
            flat2 = lambda t: t.reshape(-1, shape[-1])
            for l in range(depth):
                res = _adamw_layer(recv[(l, nme)], flat2(wv_), flat2(mv_), flat2(vv_), l, res, f"adamw_{nme}{l}")
            outs[nme] = [t.reshape(shape) for t in res]
        else:
            two = lambda t: t.reshape(-1, shape[-1]) if t.ndim > 1 else t.reshape(1, -1)
            res = _adamw(two(g_small[nme]), two(wv_), two(mv_), two(vv_), f"adamw_{nme}")
            outs[nme] = [r.reshape(shape) for r in res]
    return (loss, grad_x, *[outs[n][0] for n in names], *[outs[n][1] for n in names], *[outs[n][2] for n in names],
            *[outs[n][3] for n in names])
```

```python
import functools

import jax
import jax.numpy as jnp
import numpy as np
from jax import lax
from jax.experimental import pallas as pl
from jax.experimental.pallas import tpu as pltpu

F32 = jnp.float32
_MXU = jnp.bfloat16
_WIRE = jnp.bfloat16
EPS = 1e-6
NDEV = 8
AXES = ("x", "y", "c")
MESH = pl.DeviceIdType.MESH

N_HEADS = 4
HEAD_DIM = 64
GROUP = 256
CHUNK = 128
NZ = 2816
N_IN = 2724
MISC_F, MISC_KR = 0, 32
VMEM_LIMIT = 56 * 1024 * 1024

ADAM_LR, ADAM_B1, ADAM_B2, ADAM_EPS, ADAM_WD, ADAM_STEP = 0.001, 0.9, 0.999, 1e-08, 0.01, 10

SDS = jax.ShapeDtypeStruct


def _cp(*sem):
    return pltpu.CompilerParams(dimension_semantics=sem, vmem_limit_bytes=VMEM_LIMIT)


def _dot(a, b):
    return jnp.dot(a.astype(_MXU), b.astype(_MXU), preferred_element_type=F32)


def _dot_nt(a, b):
    return lax.dot_general(a.astype(_MXU), b.astype(_MXU), (((1,), (1,)), ((), ())), preferred_element_type=F32)


def _dot_tn(a, b):
    return lax.dot_general(a.astype(_MXU), b.astype(_MXU), (((0,), (0,)), ((), ())), preferred_element_type=F32)


def _dot_exact(a, b, dims=(((1,), (0,)), ((), ()))):
    return lax.dot_general(a, b, dims, precision=lax.Precision.HIGHEST, preferred_element_type=F32)


def _rms(x, g):
    return x * lax.rsqrt(jnp.mean(x * x, axis=-1, keepdims=True) + EPS) * g


def _rms_bwd(x, g, dy):
    xh = x * lax.rsqrt(jnp.mean(x * x, axis=-1, keepdims=True) + EPS)
    dxh = dy * g
    r = lax.rsqrt(jnp.mean(x * x, axis=-1, keepdims=True) + EPS)
    dx = r * (dxh - xh * jnp.mean(dxh * xh, axis=-1, keepdims=True))
    return dx, jnp.sum(dy * xh, axis=0, keepdims=True)


_GELU_C = 0.7978845608028654


def _gelu(x):
    return 0.5 * x * (1.0 + jnp.tanh(_GELU_C * (x + 0.044715 * x * x * x)))


def _gelu_grad(x):
    t = jnp.tanh(_GELU_C * (x + 0.044715 * x * x * x))
    return 0.5 * (1.0 + t) + 0.5 * x * (1.0 - t * t) * _GELU_C * (1.0 + 3 * 0.044715 * x * x)


def _sigmoid(x):
    return 1.0 / (1.0 + jnp.exp(-x))


def _swap_half(t, half):
    n = t.shape[-1]
    lane = lax.broadcasted_iota(jnp.int32, t.shape, t.ndim - 1)
    return jnp.where((lane % (2 * half)) < half, pltpu.roll(t, n - half, t.ndim - 1), pltpu.roll(t, half, t.ndim - 1))


def _lanes(table, width):
    return jnp.concatenate([table] * (width // table.shape[-1]), axis=-1)


def _rope(t, cos, sin, half):
    return t * cos + _swap_half(t, half) * sin


def _rope_bwd(d, cos, sin, half):
    return d * cos - _swap_half(d, half) * sin


def _tables(s):
    pos = jnp.arange(s, dtype=F32)[:, None]

    def cs(half):
        inv = jnp.power(10000.0, -jnp.arange(half, dtype=F32) / half)
        ang = pos * inv[None, :]
        return jnp.cos(ang), jnp.sin(ang)

    c32, s32 = cs(32)
    c16, s16 = cs(16)
    z = lambda w: jnp.zeros((s, w), F32)
    o = lambda w: jnp.ones((s, w), F32)
    t = {}
    t["b_cos"] = jnp.concatenate([c32, c32, c32, c32], 1)
    t["b_sin"] = jnp.concatenate([-s32, s32, -s32, s32], 1)
    t["q_cos"] = jnp.concatenate([o(64), c16, c16, z(32)], 1)
    t["q_sin"] = jnp.concatenate([z(64), -s16, s16, z(32)], 1)
    t["k_cos"] = jnp.concatenate([z(32), c16, c16, z(64)], 1)
    t["k_sin"] = jnp.concatenate([z(32), -s16, s16, z(64)], 1)
    lg = jnp.log1p(-jnp.exp2(-5.0 - jnp.arange(N_HEADS, dtype=F32)))
    j = jnp.arange(CHUNK, dtype=F32)
    rel = j[:, None] - j[None, :]
    t["decay"] = jnp.where(rel[None] >= 0, jnp.exp(jnp.maximum(rel, 0.0)[None] * lg[:, None, None]), 0.0)
    t["decay_t"] = jnp.swapaxes(t["decay"], 1, 2)

    def rows(e):
        return jnp.repeat(e.T, HEAD_DIM, axis=1)

    t["qw"] = rows(jnp.exp((j + 1.0)[None, :] * lg[:, None]))
    t["kw"] = rows(jnp.exp((CHUNK - 1 - j)[None, :] * lg[:, None]))
    t["kw2"] = rows(jnp.exp((CHUNK - j)[None, :] * lg[:, None]))
    t["qw0"] = rows(jnp.exp(j[None, :] * lg[:, None]))
    t["cd"] = jnp.repeat(jnp.exp(CHUNK * lg), HEAD_DIM)[None, :]
    e = np.zeros((128, 512), np.float32)
    for h in range(N_HEADS):
        for r in range(32):
            e[MISC_KR + r, 128 * h + 64 + r] = 1.0
    t["place"] = jnp.asarray(e)
    lane_head = np.arange(GROUP) // HEAD_DIM
    t["grp"] = jnp.asarray((lane_head[:, None] == lane_head[None, :]) / HEAD_DIM, _MXU)
    hsel = (np.arange(128)[:, None] == lane_head[None, :]).astype(np.float32)
    t["hsel"] = jnp.asarray(hsel)
    t["hselt"] = jnp.asarray(hsel.T, _MXU)
    return t


def _norm_matmul(x, g, w, name, comm=None):
    s, d = x.shape
    n = w.shape[1]
    tm, tn = min(512, s), 256
    ni = s // tm

    def body(*refs):
        (x_ref, g_ref, w_ref), (z_ref, h_ref), _, cc = _split_refs(refs, 3, 2, comm)
        i = pl.program_id(0)
        _host_gather(comm, cc, i, ni, late=True)
        h = _rms(x_ref[...], g_ref[...]).astype(h_ref.dtype)
        h_ref[...] = h
        for j in range(n // tn):
            z_ref[:, tn * j:tn * (j + 1)] = jnp.dot(h, w_ref[:, tn * j:tn * (j + 1)], preferred_element_type=F32)
        if comm is not None:
            @pl.when(i == ni - 1)
            def _():
                comm.wait(*cc)

    in_specs = [pl.BlockSpec((tm, d), lambda i: (i, 0)), pl.BlockSpec((1, d), lambda i: (0, 0)),
                pl.BlockSpec((d, n), lambda i: (0, 0))]
    out_specs = [pl.BlockSpec((tm, n), lambda i: (i, 0)), pl.BlockSpec((tm, d), lambda i: (i, 0))]
    out_shape = [SDS((s, n), F32), SDS((s, d), _MXU)]
    return _call_with_comm(body, (ni,), in_specs, out_specs, out_shape, [], [x, g, w], comm, ("arbitrary",), name)


def _mm_tn(a, b, name, *, a_fn=None, blocked=False, out_dtype=F32):
    k, m = a.shape
    n = b.shape[1]
    tm, tk = min(1024, m), min(1024, k)
    tn = next(t for t in (2816, 1024, 512, 256, 128) if n % t == 0)
    assert m % tm == 0 and k % tk == 0
    nk = k // tk

    def body(a_ref, b_ref, o_ref, acc):
        kk = pl.program_id(2)

        @pl.when(kk == 0)
        def _():
            acc[...] = jnp.zeros_like(acc)

        av = a_ref[...]
        if a_fn is not None:
            av = a_fn(av.astype(F32))
        acc[...] += _dot_tn(av, b_ref[...])

        @pl.when(kk == nk - 1)
        def _():
            if blocked:
                for c in range(tn // 512):
                    o_ref[c] = acc[:, 512 * c:512 * (c + 1)].astype(o_ref.dtype)
            else:
                o_ref[...] = acc[...].astype(o_ref.dtype)

    if blocked:
        assert tn % 512 == 0
        out_spec = pl.BlockSpec((tn // 512, tm, 512), lambda i, j, kk: (j, i, 0))
        out_shape = SDS((n // 512, m, 512), out_dtype)
    else:
        out_spec = pl.BlockSpec((tm, tn), lambda i, j, kk: (i, j))
        out_shape = SDS((m, n), out_dtype)
    return pl.pallas_call(
        body, grid=(m // tm, n // tn, nk),
        in_specs=[pl.BlockSpec((tk, tm), lambda i, j, kk: (kk, i)), pl.BlockSpec((tk, tn), lambda i, j, kk: (kk, j))],
        out_specs=out_spec, out_shape=out_shape, scratch_shapes=[pltpu.VMEM((tm, tn), F32)],
        compiler_params=_cp("parallel", "parallel", "arbitrary"), name=name)(a, b)


def _split_dot(x, m):
    hi = x.astype(_MXU)
    lo = (x - hi.astype(F32)).astype(_MXU)
    return jnp.dot(hi, m, preferred_element_type=F32) + jnp.dot(lo, m, preferred_element_type=F32)


def _gstandardize(t, grp):
    tc = t - _split_dot(t, grp)
    rs = lax.rsqrt(_split_dot(tc * tc, grp) + EPS)
    return tc * rs, rs


def _gstandardize_bwd(yh, rs, dy, grp):
    return rs * (dy - _split_dot(dy, grp) - yh * _split_dot(dy * yh, grp))


def _head_select(parts):
    hid = lax.broadcasted_iota(jnp.int32, parts[0].shape, 1) // HEAD_DIM
    return jnp.where(hid == 0, parts[0], jnp.where(hid == 1, parts[1], jnp.where(hid == 2, parts[2], parts[3])))


def _head_masked(x):
    hid = lax.broadcasted_iota(jnp.int32, x.shape, 1) // HEAD_DIM
    return [jnp.where(hid == h, x, jnp.zeros_like(x)) for h in range(N_HEADS)]


def _tril(w):
    r = lax.broadcasted_iota(jnp.int32, w.shape, 0)
    c = lax.broadcasted_iota(jnp.int32, w.shape, 1)
    return jnp.where(r >= c, w, 0.0)


def _sgu_mixed(vgb, wcs, bias, nchunk):
    ms = [[jnp.dot(wcs[h], vgb[CHUNK * c:CHUNK * (c + 1)], preferred_element_type=F32) for h in range(N_HEADS)]
          for c in range(nchunk)]
    return [_head_select(ms[c]) + bias for c in range(nchunk)]


def _sgu_fwd(z, gain, w_s, b_t, tb, name):
    s = z.shape[0]
    tm = min(512, s)
    const = lambda a: pl.BlockSpec(a.shape, lambda i: (0,) * a.ndim)

    def body(u_ref, v_ref, g_ref, w_ref, b_ref, grp_ref, hsel_ref, y_ref):
        u = _gelu(u_ref[...])
        vh, _ = _gstandardize(_gelu(v_ref[...]), grp_ref[...])
        vgb = (vh * g_ref[...]).astype(_MXU)
        bias = _dot_exact(b_ref[...], hsel_ref[...])
        wcs = [_tril(w_ref[h]).astype(_MXU) for h in range(N_HEADS)]
        for c, mixed in enumerate(_sgu_mixed(vgb, wcs, bias, tm // CHUNK)):
            r = slice(CHUNK * c, CHUNK * (c + 1))
            y_ref[r, :] = u[r] * mixed

    return pl.pallas_call(
        body, grid=(s // tm,),
        in_specs=[pl.BlockSpec((tm, GROUP), lambda i: (i, _Z_SGU[0])), pl.BlockSpec((tm, GROUP), lambda i: (i, _Z_SGU[1])),
                  pl.BlockSpec((1, GROUP), lambda i: (0, 0)), pl.BlockSpec((N_HEADS, CHUNK, CHUNK), lambda i: (0, 0, 0)),
                  pl.BlockSpec((CHUNK, 128), lambda i: (0, 0)), const(tb["grp"]), const(tb["hsel"])],
        out_specs=pl.BlockSpec((tm, GROUP), lambda i: (i, 0)), out_shape=SDS((s, GROUP), F32),
        compiler_params=_cp("parallel"), name=name)(z, z, gain, w_s, b_t, tb["grp"], tb["hsel"])


def _sgu_bwd(dy, z, gain, w_s, b_t, tb, dz, name):
    s = z.shape[0]
    tm = min(512, s)
    nchunk = tm // CHUNK
    const = lambda a: pl.BlockSpec(a.shape, lambda i: (0,) * a.ndim)

    def body(dy_ref, u_ref, v_ref, g_ref, w_ref, b_ref, grp_ref, hsel_ref, hselt_ref, _, dz_ref, dg_ref, dw_ref, db_ref):
        @pl.when(pl.program_id(0) == 0)
        def _():
            dg_ref[...] = jnp.zeros_like(dg_ref)
            dw_ref[...] = jnp.zeros_like(dw_ref)
            db_ref[...] = jnp.zeros_like(db_ref)

        grp = grp_ref[...]
        u_pre, v_pre, gain_v = u_ref[...], v_ref[...], g_ref[...]
        u = _gelu(u_pre)
        vh, rs = _gstandardize(_gelu(v_pre), grp)
        vgb = (vh * gain_v).astype(_MXU)
        dyv = dy_ref[...]
        bias = _dot_exact(b_ref[...], hsel_ref[...])
        wfs = [_tril(w_ref[h]) for h in range(N_HEADS)]
        wcs = [w.astype(_MXU) for w in wfs]
        wts = [w.T.astype(_MXU) for w in wfs]
        mixed = _sgu_mixed(vgb, wcs, bias, nchunk)
        gu = _gelu_grad(u_pre)
        dms, dmh = [], []
        for c in range(nchunk):
            r = slice(CHUNK * c, CHUNK * (c + 1))
            dz_ref[r, 0:GROUP] = (dyv[r] * mixed[c] * gu[r]).astype(dz_ref.dtype)
            dm = dyv[r] * u[r]
            dms.append(dm)
            dmh.append([m.astype(_MXU) for m in _head_masked(dm)])
        dws = [sum(lax.dot_general(dmh[c][h], vgb[CHUNK * c:CHUNK * (c + 1)], (((1,), (1,)), ((), ())),
                                   preferred_element_type=F32) for c in range(nchunk)) for h in range(N_HEADS)]
        dvg = jnp.concatenate([sum(jnp.dot(wts[h], dmh[c][h], preferred_element_type=F32) for h in range(N_HEADS))
                               for c in range(nchunk)], axis=0)
        for h in range(N_HEADS):
            dw_ref[h] += _tril(dws[h])
        db_ref[...] += sum(_split_dot(dm, hselt_ref[...]) for dm in dms)
        dg_ref[...] += jnp.sum(dvg * vh, axis=0, keepdims=True)
        dv = _gstandardize_bwd(vh, rs, dvg * gain_v, grp)
        dz_ref[:, GROUP:2 * GROUP] = (dv * _gelu_grad(v_pre)).astype(dz_ref.dtype)

    consts = [tb["grp"], tb["hsel"], tb["hselt"]]
    return pl.pallas_call(
        body, grid=(s // tm,),
        in_specs=[pl.BlockSpec((tm, GROUP), lambda i: (i, 0)),
                  pl.BlockSpec((tm, GROUP), lambda i: (i, _Z_SGU[0])), pl.BlockSpec((tm, GROUP), lambda i: (i, _Z_SGU[1])),
                  pl.BlockSpec((1, GROUP), lambda i: (0, 0)), pl.BlockSpec((N_HEADS, CHUNK, CHUNK), lambda i: (0, 0, 0)),
                  pl.BlockSpec((CHUNK, 128), lambda i: (0, 0))] + [const(a) for a in consts]
        + [pl.BlockSpec(memory_space=pl.ANY)],
        out_specs=[pl.BlockSpec((tm, 2 * GROUP), lambda i: (i, _DZ_SGU)), pl.BlockSpec((1, GROUP), lambda i: (0, 0)),
                   pl.BlockSpec((N_HEADS, CHUNK, CHUNK), lambda i: (0, 0, 0)), pl.BlockSpec((CHUNK, 128), lambda i: (0, 0))],
        out_shape=[SDS(dz.shape, dz.dtype), SDS((1, GROUP), F32), SDS((N_HEADS, CHUNK, CHUNK), F32), SDS((CHUNK, 128), F32)],
        input_output_aliases={9: 0}, compiler_params=_cp("arbitrary"), name=name)(dy, z, z, gain, w_s, b_t, *consts, dz)


_SCALE_B = HEAD_DIM ** -0.5
RET_CHUNKS = 8


def _block_diag(compact):
    full = jnp.concatenate([compact] * N_HEADS, axis=0)
    r = lax.broadcasted_iota(jnp.int32, full.shape, 0) // HEAD_DIM
    c = lax.broadcasted_iota(jnp.int32, full.shape, 1) // HEAD_DIM
    return jnp.where(r == c, full, 0.0)


def _diag_blocks(full):
    c = lax.broadcasted_iota(jnp.int32, (HEAD_DIM, GROUP), 1) // HEAD_DIM
    return sum(jnp.where(c == h, full[HEAD_DIM * h:HEAD_DIM * (h + 1), :], 0.0) for h in range(N_HEADS))


def _ret_fwd(z, tb, name):
    s = z.shape[0]
    nc = s // CHUNK
    per = min(RET_CHUNKS, nc)
    rows = per * CHUNK
    row = lambda col: pl.BlockSpec((rows, GROUP), lambda n, col=col: (n, col))
    const = lambda shape: pl.BlockSpec(shape, lambda n: (0,) * len(shape))

    def body(q_ref, k_ref, v_ref, g_ref, cos_ref, sin_ref, dec_ref, qw_ref, kw_ref, cd_ref, grp_ref, y_ref, o_ref, st_ref, state):
        @pl.when(pl.program_id(0) == 0)
        def _():
            state[...] = jnp.zeros_like(state)

        cos, sin = _lanes(cos_ref[...], GROUP), _lanes(sin_ref[...], GROUP)
        q = _rope(q_ref[...], cos, sin, 32)
        k = _rope(k_ref[...], cos, sin, 32) * _SCALE_B
        v = v_ref[...]
        g = g_ref[...]
        rcs = [slice(CHUNK * c, CHUNK * (c + 1)) for c in range(per)]
        vms = [[t.astype(_MXU) for t in _head_masked(v[r])] for r in rcs]
        scs = [[_dot_nt(t.astype(_MXU), k[r]) for t in _head_masked(q[r])] for r in rcs]
        kvs = [_dot_tn(k[r] * kw_ref[...], v[r]) for r in rcs]
        st = state[...]
        crosses = []
        for c, r in enumerate(rcs):
            st_ref[c] = st
            crosses.append(_dot(q[r] * qw_ref[...], _block_diag(st)))
            st = cd_ref[...] * st + _diag_blocks(kvs[c])
        state[...] = st
        outs = []
        for c in range(per):
            scd = [(scs[c][h] * dec_ref[h]).astype(_MXU) for h in range(N_HEADS)]
            outs.append(crosses[c] + sum(jnp.dot(scd[h], vms[c][h], preferred_element_type=F32) for h in range(N_HEADS)))
        o = jnp.concatenate(outs, axis=0)
        o_ref[...] = o
        yh, _ = _gstandardize(o, grp_ref[...])
        y_ref[...] = g * _sigmoid(g) * yh

    return pl.pallas_call(
        body, grid=(nc // per,),
        in_specs=[row(_Z_RET[0]), row(_Z_RET[1]), row(_Z_RET[2]), row(_Z_RET[3]), pl.BlockSpec((rows, 128), lambda n: (n, 0)),
                  pl.BlockSpec((rows, 128), lambda n: (n, 0)), const((N_HEADS, CHUNK, CHUNK)),
                  const((CHUNK, GROUP)), const((CHUNK, GROUP)), const((1, GROUP)), const((GROUP, GROUP))],
        out_specs=[pl.BlockSpec((rows, GROUP), lambda n: (n, 0)), pl.BlockSpec((rows, GROUP), lambda n: (n, 0)),
                   pl.BlockSpec((per, HEAD_DIM, GROUP), lambda n: (n, 0, 0))],
        out_shape=[SDS((s, GROUP), F32), SDS((s, GROUP), F32), SDS((nc, HEAD_DIM, GROUP), F32)],
        scratch_shapes=[pltpu.VMEM((HEAD_DIM, GROUP), F32)],
        compiler_params=_cp("arbitrary"), name=name)(z, z, z, z, tb["b_cos"], tb["b_sin"], tb["decay"], tb["qw"], tb["kw"], tb["cd"],
                                                       tb["grp"])


def _ret_bwd(dy, z, o_pre, states, tb, name):
    s = z.shape[0]
    nc = s // CHUNK
    per = min(RET_CHUNKS, nc)
    rows = per * CHUNK
    ns = nc // per
    rev = lambda col: pl.BlockSpec((rows, GROUP), lambda n, col=col: (ns - 1 - n, col))
    const = lambda shape: pl.BlockSpec(shape, lambda n: (0,) * len(shape))

    def body(dy_ref, q_ref, k_ref, v_ref, g_ref, o_ref, st_ref, cos_ref, sin_ref, dec_ref, dect_ref, qw_ref, kw2_ref, qw0_ref,
             cd_ref, grp_ref, dz_ref, rstate):
        @pl.when(pl.program_id(0) == 0)
        def _():
            rstate[...] = jnp.zeros_like(rstate)

        cos, sin = _lanes(cos_ref[...], GROUP), _lanes(sin_ref[...], GROUP)
        q = _rope(q_ref[...], cos, sin, 32)
        k = _rope(k_ref[...], cos, sin, 32) * _SCALE_B
        v = v_ref[...]
        g = g_ref[...]
        dyv = dy_ref[...]
        sg = _sigmoid(g)
        yh, rs = _gstandardize(o_ref[...], grp_ref[...])
        dz_ref[:, 3 * GROUP:4 * GROUP] = (dyv * yh * (sg * (1.0 + g * (1.0 - sg)))).astype(dz_ref.dtype)
        do = _gstandardize_bwd(yh, rs, dyv * (g * sg), grp_ref[...])
        hs = range(N_HEADS)
        rcs = [slice(CHUNK * c, CHUNK * (c + 1)) for c in range(per)]
        mask = lambda t: [m.astype(_MXU) for m in _head_masked(t)]
        qms, kms, vms, doms = ([mask(t[r]) for r in rcs] for t in (q, k, v, do))
        dps = [[_dot_nt(doms[c][h], v[r]) for h in hs] for c, r in enumerate(rcs)]
        pts = [[_dot_nt(kms[c][h], q[r]) for h in hs] for c, r in enumerate(rcs)]
        dpts = [[_dot_nt(vms[c][h], do[r]) for h in hs] for c, r in enumerate(rcs)]
        dq_x = [_dot_nt(do[r] * qw_ref[...], _block_diag(st_ref[c])) for c, r in enumerate(rcs)]
        r_new = [_dot_tn(q[r] * qw0_ref[...], do[r]) for r in rcs]
        rr = rstate[...]
        dk_x, dv_x = [None] * per, [None] * per
        for c in reversed(range(per)):
            r_bd = _block_diag(rr)
            dk_x[c] = _dot_nt(v[rcs[c]] * kw2_ref[...], r_bd)
            dv_x[c] = _dot(k[rcs[c]] * kw2_ref[...], r_bd)
            rr = cd_ref[...] * rr + _diag_blocks(r_new[c])
        rstate[...] = rr
        dqs, dks, dvs = [], [], []
        for c in range(per):
            dpd = [(dps[c][h] * dec_ref[h]).astype(_MXU) for h in hs]
            dptd = [(dpts[c][h] * dect_ref[h]).astype(_MXU) for h in hs]
            ptd = [(pts[c][h] * dect_ref[h]).astype(_MXU) for h in hs]
            dqs.append(dq_x[c] + sum(jnp.dot(dpd[h], kms[c][h], preferred_element_type=F32) for h in hs))
            dks.append(dk_x[c] + sum(jnp.dot(dptd[h], qms[c][h], preferred_element_type=F32) for h in hs))
            dvs.append(dv_x[c] + sum(jnp.dot(ptd[h], doms[c][h], preferred_element_type=F32) for h in hs))
        dz_ref[:, 0:GROUP] = _rope_bwd(jnp.concatenate(dqs, axis=0), cos, sin, 32).astype(dz_ref.dtype)
        dz_ref[:, GROUP:2 * GROUP] = _rope_bwd(jnp.concatenate(dks, axis=0) * _SCALE_B, cos, sin, 32).astype(dz_ref.dtype)
        dz_ref[:, 2 * GROUP:3 * GROUP] = jnp.concatenate(dvs, axis=0).astype(dz_ref.dtype)

    r0 = lambda: pl.BlockSpec((rows, GROUP), lambda n: (ns - 1 - n, 0))
    r128 = lambda: pl.BlockSpec((rows, 128), lambda n: (ns - 1 - n, 0))
    return pl.pallas_call(
        body, grid=(ns,),
        in_specs=[r0(), rev(_Z_RET[0]), rev(_Z_RET[1]), rev(_Z_RET[2]), rev(_Z_RET[3]), r0(),
                  pl.BlockSpec((per, HEAD_DIM, GROUP), lambda n: (ns - 1 - n, 0, 0)),
                  r128(), r128(), const((N_HEADS, CHUNK, CHUNK)), const((N_HEADS, CHUNK, CHUNK)), const((CHUNK, GROUP)),
                  const((CHUNK, GROUP)), const((CHUNK, GROUP)), const((1, GROUP)), const((GROUP, GROUP))],
        out_specs=pl.BlockSpec((rows, 4 * GROUP), lambda n: (ns - 1 - n, _DZ_RET)),
        out_shape=SDS((s, NZ), _MXU), scratch_shapes=[pltpu.VMEM((HEAD_DIM, GROUP), F32)],
        compiler_params=_cp("arbitrary"), name=name)(
            dy, z, z, z, z, o_pre, states, tb["b_cos"], tb["b_sin"], tb["decay"], tb["decay_t"], tb["qw"], tb["kw2"], tb["qw0"],
            tb["cd"], tb["grp"])


TQ = 256


def _log_sigmoid(x):
    return jnp.minimum(x, 0.0) - jnp.log1p(jnp.exp(-jnp.abs(x)))


def _fox_prep(z, b_f, name):
    s = z.shape[0]
    nb = s // TQ

    def body(m_ref, b_ref, cc_ref, carry):
        @pl.when(pl.program_id(0) == 0)
        def _():
            carry[...] = jnp.zeros_like(carry)

        lane = lax.broadcasted_iota(jnp.int32, (TQ, 128), 1)
        logf = jnp.where(lane < N_HEADS, _log_sigmoid(m_ref[...] + b_ref[...]), 0.0)
        r = lax.broadcasted_iota(jnp.int32, (TQ, TQ), 0)
        c = lax.broadcasted_iota(jnp.int32, (TQ, TQ), 1)
        tri = jnp.where(r >= c, 1.0, 0.0).astype(F32)
        cum = _dot_exact(tri, logf) + carry[...]
        cc_ref[...] = cum * LOG2E
        carry[...] = cum[TQ - 1:TQ, :]

    return pl.pallas_call(
        body, grid=(nb,),
        in_specs=[pl.BlockSpec((TQ, 128), lambda i: (i, NZ // 128 - 1)), pl.BlockSpec((1, 128), lambda i: (0, 0))],
        out_specs=pl.BlockSpec((TQ, 128), lambda i: (i, 0)),
        out_shape=SDS((s, 128), F32), scratch_shapes=[pltpu.VMEM((1, 128), F32)],
        compiler_params=_cp("arbitrary"), name=name)(z, b_f)


def _fox_post(dcr, dcq, z, b_f, dkr, dz, name):
    s = z.shape[0]
    nb = s // TQ

    def body(dc_ref, dcq_ref, m_ref, b_ref, dkr_ref, _, dz_ref, db_ref, carry):
        @pl.when(pl.program_id(0) == 0)
        def _():
            carry[...] = jnp.zeros_like(carry)
            db_ref[...] = jnp.zeros_like(db_ref)

        r = lax.broadcasted_iota(jnp.int32, (TQ, TQ), 0)
        c = lax.broadcasted_iota(jnp.int32, (TQ, TQ), 1)
        triu = jnp.where(c >= r, 1.0, 0.0).astype(F32)
        dc = jnp.concatenate([dc_ref[0], jnp.zeros((120, TQ), F32)], axis=0)
        dlogf = _dot_exact(triu, dc, (((1,), (1,)), ((), ()))) + _dot_exact(triu, dcq_ref[...]) + carry[...]
        carry[...] = dlogf[0:1, :]
        x = m_ref[...] + b_ref[...]
        lane = lax.broadcasted_iota(jnp.int32, (TQ, 128), 1)
        df = jnp.where(lane < N_HEADS, dlogf * _sigmoid(-x), 0.0)
        db_ref[...] += jnp.sum(df, axis=0, keepdims=True)
        dz_ref[...] = (df + dkr_ref[...]).astype(dz_ref.dtype)

    rv = lambda i: nb - 1 - i
    return pl.pallas_call(
        body, grid=(nb,),
        in_specs=[pl.BlockSpec((1, 8, TQ), lambda i: (rv(i), 0, 0)), pl.BlockSpec((TQ, 128), lambda i: (rv(i), 0)),
                  pl.BlockSpec((TQ, 128), lambda i: (rv(i), NZ // 128 - 1)),
                  pl.BlockSpec((1, 128), lambda i: (0, 0)), pl.BlockSpec((TQ, 128), lambda i: (rv(i), 0)),
                  pl.BlockSpec(memory_space=pl.ANY)],
        out_specs=[pl.BlockSpec((TQ, 128), lambda i: (rv(i), _DZ_MISC)), pl.BlockSpec((1, 128), lambda i: (0, 0))],
        out_shape=[SDS(dz.shape, dz.dtype), SDS((1, 128), F32)], scratch_shapes=[pltpu.VMEM((1, 128), F32)],
        input_output_aliases={5: 0}, compiler_params=_cp("arbitrary"), name=name)(dcr, dcq, z, b_f, dkr, dz)


NEG = -1e30
TKV = 512


def _key_block(s):
    return min(TKV, s)


def _diag_mask(shape, off):
    r = lax.broadcasted_iota(jnp.int32, shape, 0)
    c = lax.broadcasted_iota(jnp.int32, shape, 1)
    return c + off >= r


def _head_lanes(h, dqk):
    return slice(128 * (h // 2), 128 * (h // 2) + 128) if dqk == HEAD_DIM else slice(128 * h, 128 * h + 128)


def _keep_half(x, a, axis):
    idx = lax.broadcasted_iota(jnp.int32, x.shape, axis)
    return jnp.where((idx < HEAD_DIM) if a == 0 else (idx >= HEAD_DIM), x, jnp.zeros_like(x))


def _scaled_qt(q, scale):
    qs = q.astype(F32) * (scale * LOG2E)
    return [qs[TQ * b:TQ * (b + 1)].T.astype(_MXU) for b in range(q.shape[0] // TQ)]


def _kv_prep(z, qcol, kcol, vcol, scale, name):
    s = z.shape[0]
    tk = _key_block(s)
    nk = s // tk

    def body(q_ref, k_ref, v_ref, kb_ref, vb_ref, vt_ref, qt_ref):
        kb_ref[...] = k_ref[...].astype(_MXU)
        v = v_ref[...]
        vb_ref[...] = v.astype(_MXU)
        vt_ref[0] = v.T.astype(_MXU)
        for b, t in enumerate(_scaled_qt(q_ref[...], scale)):
            qt_ref[b] = t

    blk = pl.BlockSpec((tk, GROUP), lambda i: (i, 0))
    col = lambda c: pl.BlockSpec((tk, GROUP), lambda i, c=c: (i, c))
    return pl.pallas_call(
        body, grid=(nk,), in_specs=[col(qcol), col(kcol), col(vcol)],
        out_specs=[blk, blk, pl.BlockSpec((1, GROUP, tk), lambda i: (i, 0, 0)),
                   pl.BlockSpec((tk // TQ, GROUP, TQ), lambda i: (i, 0, 0))],
        out_shape=[SDS((s, GROUP), _MXU), SDS((s, GROUP), _MXU), SDS((nk, GROUP, tk), _MXU), SDS((s // TQ, GROUP, TQ), _MXU)],
        compiler_params=_cp("parallel"), name=name)(z, z, z)


LOG2E = 1.4426950408889634


def _attn_fwd(q, qcol, dqk, kb, vt, scale, ck2, name, comm=None):
    s = q.shape[0]
    nq = s // TQ
    tk = _key_block(s)
    ratio = tk // TQ
    wq = N_HEADS * dqk
    bias = ck2 is not None

    def body(*refs):
        ins, (o_ref, l_ref), _, cc = _split_refs(refs, 4 if bias else 3, 2, comm)
        if bias:
            q_ref, k_ref, vt_ref, cc_ref = ins
        else:
            q_ref, k_ref, vt_ref = ins
        i = pl.program_id(0)
        _host_gather(comm, cc, i, nq)
        qts = []
        for h in range(N_HEADS):
            qt = (q_ref[:, _head_lanes(h, dqk)].astype(F32) * (scale * LOG2E)).T
            qts.append((_keep_half(qt, h % 2, 0) if dqk == HEAD_DIM else qt).astype(_MXU))

        def step(j, carry, off):
            r0 = pl.multiple_of(j * tk, tk)
            vtj = vt_ref[j]
            sts = [jnp.dot(k_ref[pl.ds(r0, tk), _head_lanes(h, dqk)], qts[h], preferred_element_type=F32)
                   for h in range(N_HEADS)]
            stats, ps = [], []
            for h in range(N_HEADS):
                m, l, _ = carry[3 * h:3 * h + 3]
                st = sts[h]
                if bias:
                    st = st - cc_ref[pl.ds(r0, tk), h:h + 1]
                if off is not None:
                    st = jnp.where(_diag_mask(st.shape, off), st, NEG)
                m_new = jnp.maximum(m, jnp.max(st, axis=0, keepdims=True))
                alpha = jnp.exp2(m - m_new)
                p = jnp.exp2(st - m_new)
                stats.append((m_new, alpha * l + jnp.sum(p, axis=0, keepdims=True), alpha))
                ps.append(p.astype(_MXU))
            out = []
            for h in range(N_HEADS):
                m_new, l, alpha = stats[h]
                acc = alpha * carry[3 * h + 2] + jnp.dot(vtj[HEAD_DIM * h:HEAD_DIM * (h + 1), :], ps[h],
                                                         preferred_element_type=F32)
                out += [m_new, l, acc]
            return tuple(out)

        init = (jnp.full((1, TQ), NEG, F32), jnp.zeros((1, TQ), F32), jnp.zeros((HEAD_DIM, TQ), F32)) * N_HEADS
        jd = i // ratio
        carry = lax.fori_loop(0, jd, functools.partial(step, off=None), init)
        carry = step(jd, carry, TQ * (i % ratio))
        l_ref[...] = jnp.zeros_like(l_ref)
        for h in range(N_HEADS):
            l_ref[0, h:h + 1, :] = carry[3 * h] + jnp.log2(carry[3 * h + 1])
        for p in range(2):
            ot = jnp.concatenate([carry[6 * p + 2] / carry[6 * p + 1], carry[6 * p + 5] / carry[6 * p + 4]], axis=0)
            o_ref[:, 128 * p:128 * (p + 1)] = ot.T
        if comm is not None:
            @pl.when(i == nq - 1)
            def _():
                comm.wait(*cc)

    rows = pl.BlockSpec((1, 8, TQ), lambda i: (i, 0, 0))
    in_specs = [pl.BlockSpec((TQ, wq), lambda i: (i, qcol)), pl.BlockSpec((s, wq), lambda i: (0, 0)),
                pl.BlockSpec((s // tk, GROUP, tk), lambda i: (0, 0, 0))]
    args = [q, kb, vt]
    if bias:
        in_specs.append(pl.BlockSpec((s, 128), lambda i: (0, 0)))
        args.append(ck2)
    out_specs = [pl.BlockSpec((TQ, GROUP), lambda i: (i, 0)), rows]
    out_shape = [SDS((s, GROUP), F32), SDS((nq, 8, TQ), F32)]
    return _call_with_comm(body, (nq,), in_specs, out_specs, out_shape, [], args, comm, ("arbitrary",), name)


def _call_with_comm(body, grid, in_specs, out_specs, out_shape, scratch, args, comm, semantics, name, aliases=None):
    n_out = len(out_shape)
    if comm is not None:
        in_specs, out_specs = in_specs + comm.in_specs, out_specs + comm.out_specs
        out_shape, scratch, args = out_shape + comm.out_shape, scratch + comm.scratch, list(args) + comm.arrs
    res = pl.pallas_call(body, grid=grid, in_specs=in_specs, out_specs=out_specs, out_shape=out_shape,
                         scratch_shapes=scratch, input_output_aliases=aliases or {}, compiler_params=_cp(*semantics),
                         name=name)(*args)
    return (*res[:n_out], list(res[n_out:]))


def _attn_bwd(kb, vb, qt, dot, lse, dl, dqk, scale, ck2, name, kv_dtype, comm=None, kv_into=None):
    s = kb.shape[0]
    nq = s // TQ
    tk = _key_block(s)
    ratio = tk // TQ
    nkb = s // tk
    wq = N_HEADS * dqk
    bias = ck2 is not None

    merged = kv_into is not None
    n_in = 6 + bias + merged
    n_out = 3 + 2 * bias - merged

    def body(*refs):
        ins, outs, _, cc = _split_refs(refs, n_in, n_out, comm)
        k_ref, v_ref, qt_ref, dot_ref, l_ref, d_ref = ins[:6]
        cc_ref = ins[6] if bias else None
        dqt_ref = outs[0]
        if merged:
            dk_ref, dv_ref = outs[1].at[:, 0:wq], outs[1].at[:, wq:wq + GROUP]
        else:
            dk_ref, dv_ref = outs[1], outs[2]
        if bias:
            dck_ref, dcq_ref = outs[-2:]
        j = pl.program_id(0)

        @pl.when(j == 0)
        def _():
            if comm is not None:
                comm.start(*cc)
            dqt_ref[...] = jnp.zeros_like(dqt_ref)
            if bias:
                dcq_ref[...] = jnp.zeros_like(dcq_ref)

        ks, kts, vs = [], [], []
        for h in range(N_HEADS):
            k2 = k_ref[:, _head_lanes(h, dqk)]
            if dqk == HEAD_DIM:
                k2 = _keep_half(k2, h % 2, 1)
            ks.append(k2)
            kts.append(k2.astype(F32).T.astype(_MXU))
            vs.append(_keep_half(v_ref[:, _head_lanes(h, HEAD_DIM)], h % 2, 1))
        cks = [cc_ref[:, h:h + 1] for h in range(N_HEADS)] if bias else None

        nt = (((1,), (1,)), ((), ()))

        def step(i, carry, off):
            qti, doti, li, di = qt_ref[i], dot_ref[i], l_ref[i], d_ref[i]
            qls = [_head_lanes(h, dqk) for h in range(N_HEADS)]
            vls = [_head_lanes(h, HEAD_DIM) for h in range(N_HEADS)]
            sts, dpts = [], []
            for h in range(N_HEADS):
                sts.append(jnp.dot(ks[h], qti[qls[h], :], preferred_element_type=F32))
                dpts.append(jnp.dot(vs[h], doti[vls[h], :], preferred_element_type=F32))
            pbs, dsbs, dcks = [], [], []
            for h in range(N_HEADS):
                st = sts[h] - li[h:h + 1, :]
                if bias:
                    st = st - cks[h]
                p = jnp.exp2(st)
                if off is not None:
                    p = jnp.where(_diag_mask(p.shape, off), p, 0.0)
                dst = p * (dpts[h] - di[h:h + 1, :])
                pbs.append(p.astype(_MXU))
                dsbs.append(dst.astype(_MXU))
                if bias:
                    dcks.append(carry[3 * h + 2] + jnp.sum(dst, axis=1, keepdims=True))
                    dcq_ref[i, h:h + 1, :] += jnp.sum(dst, axis=0, keepdims=True)
                else:
                    dcks.append(carry[3 * h + 2])
            out = []
            for h in range(N_HEADS):
                dvt = carry[3 * h + 1] + lax.dot_general(doti[HEAD_DIM * h:HEAD_DIM * (h + 1), :], pbs[h], nt,
                                                         preferred_element_type=F32)
                dkt = carry[3 * h] + lax.dot_general(qti[dqk * h:dqk * (h + 1), :], dsbs[h], nt, preferred_element_type=F32)
                dqt_ref[i, qls[h], :] += jnp.dot(kts[h], dsbs[h], preferred_element_type=F32) * scale
                out += [dkt, dvt, dcks[h]]
            return tuple(out)

        carry = (jnp.zeros((dqk, tk), F32), jnp.zeros((HEAD_DIM, tk), F32), jnp.zeros((tk, 1), F32)) * N_HEADS
        for r in range(ratio):
            carry = step(ratio * j + r, carry, TQ * r)
        carry = lax.fori_loop(ratio * (j + 1), nq, functools.partial(step, off=None), carry)
        for p in range(2):
            dv_ref[:, 128 * p:128 * (p + 1)] = jnp.concatenate([carry[6 * p + 1], carry[6 * p + 4]], axis=0).T.astype(dv_ref.dtype)
            if dqk == HEAD_DIM:
                dk_ref[:, 128 * p:128 * (p + 1)] = (jnp.concatenate([carry[6 * p], carry[6 * p + 3]], axis=0).T
                                                    * (1.0 / LOG2E)).astype(dk_ref.dtype)
        if dqk != HEAD_DIM:
            for h in range(N_HEADS):
                dk_ref[:, 128 * h:128 * (h + 1)] = (carry[3 * h].T * (1.0 / LOG2E)).astype(dk_ref.dtype)
        if bias:
            dck_ref[...] = jnp.zeros_like(dck_ref)
            for h in range(N_HEADS):
                dck_ref[:, h:h + 1] = -carry[3 * h + 2]
        if comm is not None:
            @pl.when(j == nkb - 1)
            def _():
                comm.wait(*cc)

    blk = lambda w: pl.BlockSpec((tk, w), lambda j: (j, 0))
    full3 = lambda w: pl.BlockSpec((nq, w, TQ), lambda j: (0, 0, 0))
    in_specs = [blk(wq), blk(GROUP), full3(wq), full3(GROUP), full3(8), full3(8)]
    args = [kb, vb, qt, dot, lse, dl]
    if merged:
        assert wq == GROUP
        out_specs = [full3(wq), pl.BlockSpec((tk, wq + GROUP), lambda j: (j, _DZ_FOX_KV))]
        out_shape = [SDS((nq, wq, TQ), F32), SDS(kv_into.shape, kv_into.dtype)]
    else:
        out_specs = [full3(wq), blk(wq), blk(GROUP)]
        out_shape = [SDS((nq, wq, TQ), F32), SDS((s, wq), kv_dtype), SDS((s, GROUP), kv_dtype)]
    if bias:
        in_specs.append(blk(128))
        args.append(ck2)
        out_specs += [blk(128), full3(8)]
        out_shape += [SDS((s, 128), F32), SDS((nq, 8, TQ), F32)]
    aliases = {}
    if merged:
        in_specs.append(pl.BlockSpec(memory_space=pl.ANY))
        args.append(kv_into)
        aliases = {len(args) - 1: 1}
    return _call_with_comm(body, (nkb,), in_specs, out_specs, out_shape, [], args, comm, ("arbitrary",), name, aliases)


def _untranspose(xt, dtype, name, into=None, col=0):
    nq, w, _ = xt.shape
    if into is not None:
        def body_into(x_ref, _, o_ref):
            o_ref[...] = x_ref[0].T.astype(o_ref.dtype)

        return pl.pallas_call(
            body_into, grid=(nq,),
            in_specs=[pl.BlockSpec((1, w, TQ), lambda i: (i, 0, 0)), pl.BlockSpec(memory_space=pl.ANY)],
            out_specs=pl.BlockSpec((TQ, w), lambda i: (i, col)), out_shape=SDS(into.shape, into.dtype),
            input_output_aliases={1: 0}, compiler_params=_cp("parallel"), name=name)(xt, into)

    def body(x_ref, o_ref):
        o_ref[...] = x_ref[0].T.astype(o_ref.dtype)

    return pl.pallas_call(
        body, grid=(nq,), in_specs=[pl.BlockSpec((1, w, TQ), lambda i: (i, 0, 0))],
        out_specs=pl.BlockSpec((TQ, w), lambda i: (i, 0)), out_shape=SDS((nq * TQ, w), dtype),
        compiler_params=_cp("parallel"), name=name)(xt)


_SCALE_D = (64 + 32) ** -0.5
_COL_CQ, _COL_CKV, _COL_MISC = 2304 // 256, 2560 // 128, 2688 // 128


def _mla_prep(z, gq, gkv, wq, wk, wv, tb, name):
    s = z.shape[0]
    tm = _key_block(s)
    row = lambda w, c: pl.BlockSpec((tm, w), lambda i, c=c: (i, c))
    const = lambda a: pl.BlockSpec(a.shape, lambda i: (0,) * a.ndim)

    def body(cq_ref, ckv_ref, m_ref, gq_ref, gkv_ref, wq_ref, wk_ref, wv_ref, e_ref, qc_ref, qs_ref, kc_ref, ks_ref,
             q_ref, k_ref, v_ref, vt_ref, cqn_ref, ckvn_ref, qt_ref):
        cqn = _rms(cq_ref[...], gq_ref[...]).astype(_MXU)
        ckvn = _rms(ckv_ref[...], gkv_ref[...]).astype(_MXU)
        cqn_ref[...] = cqn
        ckvn_ref[...] = ckvn
        qb = _rope(_dot(cqn, wq_ref[...]), _lanes(qc_ref[...], 512), _lanes(qs_ref[...], 512), 16).astype(q_ref.dtype)
        q_ref[...] = qb
        for b, t in enumerate(_scaled_qt(qb, _SCALE_D)):
            qt_ref[b] = t
        kr = _rope(m_ref[...], kc_ref[...], ks_ref[...], 16)
        k_ref[...] = (_dot(ckvn, wk_ref[...]) + _dot(kr, e_ref[...])).astype(k_ref.dtype)
        v = _dot(ckvn, wv_ref[...])
        v_ref[...] = v.astype(v_ref.dtype)
        vt_ref[0] = v.T.astype(vt_ref.dtype)

    e = tb["place"]
    return pl.pallas_call(
        body, grid=(s // tm,),
        in_specs=[row(256, _COL_CQ), row(128, _COL_CKV), row(128, _COL_MISC), const(gq), const(gkv), const(wq), const(wk),
                  const(wv), const(e), row(128, 0), row(128, 0), row(128, 0), row(128, 0)],
        out_specs=[row(512, 0), row(512, 0), row(256, 0), pl.BlockSpec((1, GROUP, tm), lambda i: (i, 0, 0)), row(256, 0),
                   row(128, 0), pl.BlockSpec((tm // TQ, 512, TQ), lambda i: (i, 0, 0))],
        out_shape=[SDS((s, 512), _MXU), SDS((s, 512), _MXU), SDS((s, 256), _MXU), SDS((s // tm, GROUP, tm), _MXU),
                   SDS((s, 256), _MXU), SDS((s, 128), _MXU), SDS((s // TQ, 512, TQ), _MXU)],
        compiler_params=_cp("parallel"), name=name)(
            z, z, z, gq, gkv, wq, wk, wv, e, tb["q_cos"], tb["q_sin"], tb["k_cos"], tb["k_sin"])


def _mla_prep_bwd(dqt, dk, dv, z, cqn, ckvn, gq, gkv, wq, wk, wv, tb, dz, name):
    s = z.shape[0]
    tm = min(512, s)
    row = lambda w, c: pl.BlockSpec((tm, w), lambda i, c=c: (i, c))
    const = lambda a: pl.BlockSpec(a.shape, lambda i: (0,) * a.ndim)
    acc = lambda shape: pl.BlockSpec(shape, lambda i: (0, 0))

    def body(dq_ref, dk_ref, dv_ref, cq_ref, ckv_ref, cqn_ref, ckvn_ref, gq_ref, gkv_ref, wq_ref, wk_ref, wv_ref, e_ref,
             qc_ref, qs_ref, kc_ref, ks_ref, _, dz_ref, dkr_ref, dwq_ref, dwk_ref, dwv_ref, dgq_ref, dgkv_ref):
        dcq_ref, dckv_ref = dz_ref.at[:, 0:256], dz_ref.at[:, 256:384]

        @pl.when(pl.program_id(0) == 0)
        def _():
            for r in (dwq_ref, dwk_ref, dwv_ref, dgq_ref, dgkv_ref):
                r[...] = jnp.zeros_like(r)

        dq = jnp.concatenate([dq_ref[b].T for b in range(tm // TQ)], axis=0)
        dqp = _rope_bwd(dq, _lanes(qc_ref[...], 512), _lanes(qs_ref[...], 512), 16)
        dkd = dk_ref[...]
        dvd = dv_ref[...]
        dwq_ref[...] += _dot_tn(cqn_ref[...], dqp)
        dwk_ref[...] += _dot_tn(ckvn_ref[...], dkd)
        dwv_ref[...] += _dot_tn(ckvn_ref[...], dvd)
        dcq, dgq = _rms_bwd(cq_ref[...], gq_ref[...], _dot_nt(dqp, wq_ref[...]))
        dckv, dgkv = _rms_bwd(ckv_ref[...], gkv_ref[...], _dot_nt(dkd, wk_ref[...]) + _dot_nt(dvd, wv_ref[...]))
        dcq_ref[...] = dcq.astype(dcq_ref.dtype)
        dckv_ref[...] = dckv.astype(dckv_ref.dtype)
        dgq_ref[...] += dgq
        dgkv_ref[...] += dgkv
        dkr = _dot_exact(dkd, e_ref[...], (((1,), (1,)), ((), ())))
        dkr_ref[...] = _rope_bwd(dkr, kc_ref[...], ks_ref[...], 16)

    e = tb["place"]
    return pl.pallas_call(
        body, grid=(s // tm,),
        in_specs=[pl.BlockSpec((tm // TQ, 512, TQ), lambda i: (i, 0, 0)), row(512, 0), row(256, 0), row(256, _COL_CQ),
                  row(128, _COL_CKV), row(256, 0), row(128, 0),
                  const(gq), const(gkv), const(wq), const(wk), const(wv), const(e), row(128, 0), row(128, 0), row(128, 0), row(128, 0),
                  pl.BlockSpec(memory_space=pl.ANY)],
        out_specs=[row(384, _DZ_MLA), row(128, 0), acc((256, 512)), acc((128, 512)), acc((128, 256)), acc((1, 256)),
                   acc((1, 128))],
        out_shape=[SDS(dz.shape, dz.dtype), SDS((s, 128), F32), SDS((256, 512), F32), SDS((128, 512), F32),
                   SDS((128, 256), F32), SDS((1, 256), F32), SDS((1, 128), F32)],
        input_output_aliases={17: 0}, compiler_params=_cp("arbitrary"), name=name)(
            dqt, dk, dv, z, z, cqn, ckvn, gq, gkv, wq, wk, wv, e, tb["q_cos"], tb["q_sin"], tb["k_cos"], tb["k_sin"], dz)


def _out_proj(ys, g, w, x, name):
    s, d = x.shape
    tm = min(512, s)

    def body(ya, yb, yc, yd, g_ref, w_ref, x_ref, o_ref, yn_ref):
        acc = x_ref[...]
        for i, y_ref in enumerate((ya, yb, yc, yd)):
            sl = slice(GROUP * i, GROUP * (i + 1))
            yn = _rms(y_ref[...], g_ref[:, sl]).astype(_MXU)
            yn_ref[:, sl] = yn
            acc = acc + jnp.dot(yn, w_ref[sl, :], preferred_element_type=F32)
        o_ref[...] = acc

    yspec = pl.BlockSpec((tm, GROUP), lambda i: (i, 0))
    return pl.pallas_call(
        body, grid=(s // tm,),
        in_specs=[yspec, yspec, yspec, yspec, pl.BlockSpec((1, d), lambda i: (0, 0)), pl.BlockSpec((d, d), lambda i: (0, 0)),
                  pl.BlockSpec((tm, d), lambda i: (i, 0))],
        out_specs=[pl.BlockSpec((tm, d), lambda i: (i, 0)), pl.BlockSpec((tm, d), lambda i: (i, 0))],
        out_shape=[SDS((s, d), F32), SDS((s, d), _MXU)], compiler_params=_cp("parallel"), name=name)(*ys, g, w, x)


def _out_proj_bwd(dx, w, ys, g, name):
    s, d = dx.shape
    tm = min(512, s)
    nb = tm // TQ

    def body(dx_ref, w_ref, ya, yb, yc, yd, g_ref, da, db, dg_ref, dtc_ref, dtd_ref, dlc_ref, dld_ref):
        @pl.when(pl.program_id(0) == 0)
        def _():
            dg_ref[...] = jnp.zeros_like(dg_ref)

        dyn = _dot_nt(dx_ref[...], w_ref[...])
        for i, y_ref in enumerate((ya, yb, yc, yd)):
            sl = slice(GROUP * i, GROUP * (i + 1))
            y = y_ref[...]
            dy, dg = _rms_bwd(y, g_ref[:, sl], dyn[:, sl])
            dg_ref[:, sl] += dg
            if i < 2:
                (da, db)[i][...] = dy
                continue
            dt_ref, dl_ref = ((dtc_ref, dlc_ref), (dtd_ref, dld_ref))[i - 2]
            dl_ref[...] = jnp.zeros_like(dl_ref)
            for b in range(nb):
                r = slice(TQ * b, TQ * (b + 1))
                dt_ref[b] = dy[r].T.astype(dt_ref.dtype)
                pt = (dy[r] * y[r]).T
                for h in range(N_HEADS):
                    dl_ref[b, h:h + 1, :] = jnp.sum(pt[HEAD_DIM * h:HEAD_DIM * (h + 1), :], axis=0, keepdims=True)

    yspec = pl.BlockSpec((tm, GROUP), lambda i: (i, 0))
    tspec = pl.BlockSpec((nb, GROUP, TQ), lambda i: (i, 0, 0))
    lspec = pl.BlockSpec((nb, 8, TQ), lambda i: (i, 0, 0))
    return pl.pallas_call(
        body, grid=(s // tm,),
        in_specs=[pl.BlockSpec((tm, d), lambda i: (i, 0)), pl.BlockSpec((d, d), lambda i: (0, 0)), yspec, yspec, yspec, yspec,
                  pl.BlockSpec((1, d), lambda i: (0, 0))],
        out_specs=[yspec, yspec, pl.BlockSpec((1, d), lambda i: (0, 0)), tspec, tspec, lspec, lspec],
        out_shape=[SDS((s, GROUP), F32)] * 2 + [SDS((1, d), F32)] + [SDS((s // TQ, GROUP, TQ), _MXU)] * 2
        + [SDS((s // TQ, 8, TQ), F32)] * 2,
        compiler_params=_cp("arbitrary"), name=name)(dx, w, *ys, g)


FF_BLOCK = 512
FF_ROWS = 1024


def _ffn_fwd(x, g, wu, wd, name, comm=None):
    s, d = x.shape
    nj = wu.shape[0]
    tm = min(FF_ROWS, s)
    ni = s // tm

    def body(*refs):
        (x_ref, g_ref, wu_ref, wd_ref), (o_ref, u_ref, h_ref), (acc,), cc = _split_refs(refs, 4, 3, comm)
        i, j = pl.program_id(0), pl.program_id(1)
        _host_gather(comm, cc, i * nj + j, ni * nj)

        @pl.when(j == 0)
        def _():
            h_ref[...] = _rms(x_ref[...], g_ref[...]).astype(h_ref.dtype)
            acc[...] = jnp.zeros_like(acc)

        halves = [slice(r, r + tm // 2) for r in range(0, tm, tm // 2)]
        us = [jnp.dot(h_ref[r, :], wu_ref[0], preferred_element_type=F32) for r in halves]
        for r, u in zip(halves, us):
            u_ref[r, :] = u.astype(u_ref.dtype)
            acc[r, :] += _dot(jnp.square(jnp.maximum(u, 0.0)), wd_ref[...])

        @pl.when(j == nj - 1)
        def _():
            o_ref[...] = x_ref[...] + acc[...]

        if comm is not None:
            @pl.when((i == ni - 1) & (j == nj - 1))
            def _():
                comm.wait(*cc)

    in_specs = [pl.BlockSpec((tm, d), lambda i, j: (i, 0)), pl.BlockSpec((1, d), lambda i, j: (0, 0)),
                pl.BlockSpec((1, d, FF_BLOCK), lambda i, j: (j, 0, 0)), pl.BlockSpec((FF_BLOCK, d), lambda i, j: (j, 0))]
    out_specs = [pl.BlockSpec((tm, d), lambda i, j: (i, 0)), pl.BlockSpec((tm, FF_BLOCK), lambda i, j: (i, j)),
                 pl.BlockSpec((tm, d), lambda i, j: (i, 0))]
    out_shape = [SDS((s, d), F32), SDS((s, nj * FF_BLOCK), _MXU), SDS((s, d), _MXU)]
    return _call_with_comm(body, (ni, nj), in_specs, out_specs, out_shape, [pltpu.VMEM((tm, d), F32)], [x, g, wu, wd], comm,
                           ("arbitrary", "arbitrary"), name)


def _ffn_bwd(dx2, x, u, g, wu, wd, name, comm=None):
    s, d = x.shape
    nj = wu.shape[0]
    tm = min(FF_ROWS, s)
    ni = s // tm

    def body(*refs):
        (dx_ref, x_ref, u_ref, g_ref, wu_ref, wd_ref), (o_ref, du_ref, dg_ref), (acc, dxb), cc = _split_refs(refs, 6, 3, comm)
        i, j = pl.program_id(0), pl.program_id(1)

        @pl.when((i == 0) & (j == 0))
        def _():
            if comm is not None:
                comm.start(*cc)
            dg_ref[...] = jnp.zeros_like(dg_ref)

        @pl.when(j == 0)
        def _():
            dxb[...] = dx_ref[...].astype(dxb.dtype)
            acc[...] = jnp.zeros_like(acc)

        nt = (((1,), (1,)), ((), ()))
        halves = [slice(r, r + tm // 2) for r in range(0, tm, tm // 2)]
        das = [lax.dot_general(dxb[r, :], wd_ref[...], nt, preferred_element_type=F32) for r in halves]
        for r, da in zip(halves, das):
            du = (da * 2.0 * jnp.maximum(u_ref[r, :].astype(F32), 0.0)).astype(du_ref.dtype)
            du_ref[r, :] = du
            acc[r, :] += lax.dot_general(du, wu_ref[0], nt, preferred_element_type=F32)

        @pl.when(j == nj - 1)
        def _():
            dxn, dg = _rms_bwd(x_ref[...], g_ref[...], acc[...])
            o_ref[...] = dx_ref[...] + dxn
            dg_ref[...] += dg

        if comm is not None:
            @pl.when((i == ni - 1) & (j == nj - 1))
            def _():
                comm.wait(*cc)

    in_specs = [pl.BlockSpec((tm, d), lambda i, j: (i, 0)), pl.BlockSpec((tm, d), lambda i, j: (i, 0)),
                pl.BlockSpec((tm, FF_BLOCK), lambda i, j: (i, j)), pl.BlockSpec((1, d), lambda i, j: (0, 0)),
                pl.BlockSpec((1, d, FF_BLOCK), lambda i, j: (j, 0, 0)), pl.BlockSpec((FF_BLOCK, d), lambda i, j: (j, 0))]
    out_specs = [pl.BlockSpec((tm, d), lambda i, j: (i, 0)), pl.BlockSpec((tm, FF_BLOCK), lambda i, j: (i, j)),
                 pl.BlockSpec((1, d), lambda i, j: (0, 0))]
    out_shape = [SDS((s, d), F32), SDS((s, nj * FF_BLOCK), _MXU), SDS((1, d), F32)]
    return _call_with_comm(body, (ni, nj), in_specs, out_specs, out_shape,
                           [pltpu.VMEM((tm, d), F32), pltpu.VMEM((tm, d), _MXU)], [dx2, x, u, g, wu, wd], comm,
                           ("arbitrary", "arbitrary"), name)


def _in_proj_bwd(dz, w, x, g, dx_up, name, comm=None):
    s, d = x.shape
    n = w.shape[1]
    tm = min(512, s)
    ni = s // tm

    def body(*refs):
        (dz_ref, w_ref, x_ref, g_ref, up_ref), (o_ref, dg_ref), _, cc = _split_refs(refs, 5, 2, comm)
        i = pl.program_id(0)

        @pl.when(i == 0)
        def _():
            if comm is not None:
                comm.start(*cc)
            dg_ref[...] = jnp.zeros_like(dg_ref)

        dh = lax.dot_general(dz_ref[...], w_ref[...], (((1,), (1,)), ((), ())), preferred_element_type=F32)
        dxn, dg = _rms_bwd(x_ref[...], g_ref[...], dh)
        o_ref[...] = up_ref[...] + dxn
        dg_ref[...] += dg
        if comm is not None:
            @pl.when(i == ni - 1)
            def _():
                comm.wait(*cc)

    in_specs = [pl.BlockSpec((tm, n), lambda i: (i, 0)), pl.BlockSpec((d, n), lambda i: (0, 0)),
                pl.BlockSpec((tm, d), lambda i: (i, 0)), pl.BlockSpec((1, d), lambda i: (0, 0)),
                pl.BlockSpec((tm, d), lambda i: (i, 0))]
    out_specs = [pl.BlockSpec((tm, d), lambda i: (i, 0)), pl.BlockSpec((1, d), lambda i: (0, 0))]
    out_shape = [SDS((s, d), F32), SDS((1, d), F32)]
    return _call_with_comm(body, (ni,), in_specs, out_specs, out_shape, [], [dz, w, x, g, dx_up], comm, ("arbitrary",), name)


def _loss_head(x, g, target, name):
    s, d = x.shape
    tm = min(512, s)

    def body(x_ref, g_ref, t_ref, l_ref, dx_ref, dg_ref):
        @pl.when(pl.program_id(0) == 0)
        def _():
            l_ref[...] = jnp.zeros_like(l_ref)
            dg_ref[...] = jnp.zeros_like(dg_ref)

        xv = x_ref[...]
        err = _rms(xv, g_ref[...]) - t_ref[...]
        l_ref[...] += jnp.sum(err * err, axis=0, keepdims=True) * (0.5 / d)
        dx, dg = _rms_bwd(xv, g_ref[...], err * (1.0 / d))
        dx_ref[...] = dx
        dg_ref[...] += dg

    return pl.pallas_call(
        body, grid=(s // tm,),
        in_specs=[pl.BlockSpec((tm, d), lambda i: (i, 0)), pl.BlockSpec((1, d), lambda i: (0, 0)),
                  pl.BlockSpec((tm, d), lambda i: (i, 0))],
        out_specs=[pl.BlockSpec((1, d), lambda i: (0, 0)), pl.BlockSpec((tm, d), lambda i: (i, 0)),
                   pl.BlockSpec((1, d), lambda i: (0, 0))],
        out_shape=[SDS((1, d), F32), SDS((s, d), F32), SDS((1, d), F32)], compiler_params=_cp("arbitrary"), name=name)(x, g, target)


def _me_and_peer():
    x, y, c = lax.axis_index("x"), lax.axis_index("y"), lax.axis_index("c")
    me = 4 * x + 2 * y + c

    def peer(k):
        px, py, pc = x ^ (k >> 2), y ^ ((k >> 1) & 1), c ^ (k & 1)
        return (px, py, pc), 4 * px + 2 * py + pc

    return me, peer


class _Comm:
    CHIPS = (2, 4, 6)

    def __init__(self, kind, arrs):
        assert kind in ("gather", "exchange")
        self.kind, self.arrs, self.n = kind, list(arrs), len(arrs)
        anyspec = pl.BlockSpec(memory_space=pl.ANY)
        self.in_specs = [anyspec] * self.n
        self.out_specs = [anyspec] * self.n
        self.out_shape = [SDS(((NDEV,) + a.shape) if kind == "gather" else a.shape, a.dtype) for a in self.arrs]
        npair = NDEV - 1 + len(self.CHIPS)
        self.scratch = [pltpu.SemaphoreType.DMA((self.n, npair)), pltpu.SemaphoreType.DMA((self.n, npair)),
                        pltpu.SemaphoreType.DMA((self.n,))]

    def _copies(self, ins, outs, sems):
        send, recv, loc = sems
        me, peer = _me_and_peer()
        gather = self.kind == "gather"
        sibling = peer(1)[0]
        local = [pltpu.make_async_copy(ins[a] if gather else ins[a].at[me], outs[a].at[me], loc.at[a]) for a in range(self.n)]
        outgoing, incoming, forwards, forwarded = [], [], [], []
        for k in ((1,) + self.CHIPS) if gather else range(1, NDEV):
            dev, pid = peer(k)
            for a in range(self.n):
                pair = dict(send_sem=send.at[a, k - 1], recv_sem=recv.at[a, k - 1], device_id=dev, device_id_type=MESH)
                outgoing.append(pltpu.make_async_remote_copy(src_ref=ins[a] if gather else ins[a].at[pid],
                                                             dst_ref=outs[a].at[me], **pair))
                incoming.append(pltpu.make_async_remote_copy(src_ref=ins[a] if gather else ins[a].at[me],
                                                             dst_ref=outs[a].at[pid], **pair))
        if gather:
            for idx, k in enumerate(self.CHIPS):
                got, theirs = peer(k)[1], peer(k + 1)[1]
                for a in range(self.n):
                    pair = dict(send_sem=send.at[a, NDEV - 1 + idx], recv_sem=recv.at[a, NDEV - 1 + idx], device_id=sibling,
                                device_id_type=MESH)
                    forwards.append(pltpu.make_async_remote_copy(src_ref=outs[a].at[got], dst_ref=outs[a].at[got], **pair))
                    forwarded.append(pltpu.make_async_remote_copy(src_ref=outs[a].at[theirs], dst_ref=outs[a].at[theirs], **pair))
        return local, outgoing, incoming, forwards, forwarded

    def start(self, ins, outs, sems):
        local, outgoing, _, _, _ = self._copies(ins, outs, sems)
        for cp in local + outgoing:
            cp.start()

    def forward(self, ins, outs, sems):
        _, _, incoming, forwards, _ = self._copies(ins, outs, sems)
        per = self.n
        for idx in range(len(forwards) // per if per else 0):
            for a in range(per):
                incoming[(1 + idx) * per + a].wait_recv()
                forwards[idx * per + a].start()

    def wait(self, ins, outs, sems):
        local, outgoing, incoming, forwards, forwarded = self._copies(ins, outs, sems)
        for cp in (incoming[:self.n] if self.kind == "gather" else incoming) + forwarded:
            cp.wait_recv()
        for cp in outgoing + forwards:
            cp.wait_send()
        for cp in local:
            cp.wait()


LATE_FORWARD_BYTES = 1 << 20


def _host_gather(comm, cc, step, nsteps, late=None):
    if comm is None:
        return
    if late is None:
        late = sum(a.size * a.dtype.itemsize for a in comm.arrs) >= LATE_FORWARD_BYTES

    @pl.when(step == 0)
    def _():
        comm.start(*cc)

    @pl.when(step == (nsteps - 1 if late else (2 * nsteps) // 3))
    def _():
        comm.forward(*cc)


def _split_refs(refs, n_in, n_out, comm):
    c = comm.n if comm is not None else 0
    ins, cin = refs[:n_in], refs[n_in:n_in + c]
    outs, cout = refs[n_in + c:n_in + c + n_out], refs[n_in + c + n_out:n_in + 2 * c + n_out]
    rest = refs[n_in + 2 * c + n_out:]
    scratch, csem = (rest[:len(rest) - 3], rest[len(rest) - 3:]) if c else (rest, ())
    return ins, outs, scratch, (cin, cout, csem)


def _comm_call(kind, arrs, name):
    comm = _Comm(kind, arrs)

    def body(*refs):
        _, _, _, c = _split_refs(refs, 0, 0, comm)
        comm.start(*c)
        if kind == "gather":
            comm.forward(*c)
        comm.wait(*c)

    return pl.pallas_call(body, in_specs=comm.in_specs, out_specs=comm.out_specs, out_shape=comm.out_shape,
                          scratch_shapes=comm.scratch, compiler_params=pltpu.CompilerParams(has_side_effects=True),
                          name=name)(*arrs)


def _all_gather(arrs, name):
    return _comm_call("gather", arrs, name)


def _exchange(arrs, name):
    return _comm_call("exchange", arrs, name)


def _sum_slots(parts, name):
    _, r, c = parts.shape
    tr = r if r <= 512 else 512

    def body(p_ref, o_ref):
        acc = p_ref[0].astype(F32)
        for q in range(1, NDEV):
            acc = acc + p_ref[q].astype(F32)
        o_ref[...] = acc

    return pl.pallas_call(
        body, grid=(r // tr,), in_specs=[pl.BlockSpec((NDEV, tr, c), lambda i: (0, i, 0))],
        out_specs=pl.BlockSpec((tr, c), lambda i: (i, 0)), out_shape=SDS((r, c), F32),
        compiler_params=_cp("parallel"), name=name)(parts)


def _adamw(g, w, m, v, name):
    r, c = w.shape
    parts = g.ndim == 3
    tr = r
    for cand in (512, 256, 128, 64, 32, 16, 8):
        if r > cand and r % cand == 0 and cand * c * 4 <= 2 * 1024 * 1024:
            tr = cand
            break
    bc1 = 1.0 / (1.0 - ADAM_B1 ** ADAM_STEP)
    bc2 = 1.0 / (1.0 - ADAM_B2 ** ADAM_STEP)

    def body(g_ref, w_ref, m_ref, v_ref, go_ref, d_ref, mo_ref, vo_ref):
        if parts:
            gv = g_ref[0].astype(F32)
            for q in range(1, NDEV):
                gv = gv + g_ref[q].astype(F32)
        else:
            gv = g_ref[...]
        mn = ADAM_B1 * m_ref[...] + (1.0 - ADAM_B1) * gv
        vn = ADAM_B2 * v_ref[...] + (1.0 - ADAM_B2) * (gv * gv)
        go_ref[...] = gv
        mo_ref[...] = mn
        vo_ref[...] = vn
        d_ref[...] = -ADAM_LR * ((mn * bc1) / (jnp.sqrt(vn * bc2) + ADAM_EPS) + ADAM_WD * w_ref[...])

    spec = pl.BlockSpec((tr, c), lambda i: (i, 0))
    gspec = pl.BlockSpec((NDEV, tr, c), lambda i: (0, i, 0)) if parts else spec
    return pl.pallas_call(
        body, grid=(r // tr,), in_specs=[gspec, spec, spec, spec], out_specs=[spec] * 4,
        out_shape=[SDS((r, c), F32)] * 4, compiler_params=_cp("parallel"), name=name)(g, w, m, v)


def _adamw_layer(parts, w, m, v, l, prev, name):
    r, c = parts.shape[1:]
    rows = w.shape[0]
    tr = next(t for t in (512, 256, 128, 64, 32, 16, 8) if r % t == 0 and t * c * 4 <= 2 * 1024 * 1024)
    bc1 = 1.0 / (1.0 - ADAM_B1 ** ADAM_STEP)
    bc2 = 1.0 / (1.0 - ADAM_B2 ** ADAM_STEP)

    def body(g_ref, w_ref, m_ref, v_ref, *rest):
        go_ref, d_ref, mo_ref, vo_ref = rest[-4:]
        gv = g_ref[0].astype(F32)
        for q in range(1, NDEV):
            gv = gv + g_ref[q].astype(F32)
        mn = ADAM_B1 * m_ref[...] + (1.0 - ADAM_B1) * gv
        vn = ADAM_B2 * v_ref[...] + (1.0 - ADAM_B2) * (gv * gv)
        go_ref[...] = gv
        mo_ref[...] = mn
        vo_ref[...] = vn
        d_ref[...] = -ADAM_LR * ((mn * bc1) / (jnp.sqrt(vn * bc2) + ADAM_EPS) + ADAM_WD * w_ref[...])

    spec = pl.BlockSpec((tr, c), lambda i: (l * (r // tr) + i, 0))
    in_specs = [pl.BlockSpec((NDEV, tr, c), lambda i: (0, i, 0)), spec, spec, spec]
    args = [parts, w, m, v]
    aliases = {}
    if prev is not None:
        in_specs += [pl.BlockSpec(memory_space=pl.ANY)] * 4
        args += list(prev)
        aliases = {4 + k: k for k in range(4)}
    return pl.pallas_call(
        body, grid=(r // tr,), in_specs=in_specs, out_specs=[spec] * 4, out_shape=[SDS((rows, c), F32)] * 4,
        input_output_aliases=aliases, compiler_params=_cp("parallel"), name=name)(*args)


def _pad_in_cols(w):
    r = w.shape[0]
    zeros = lambda n: jnp.zeros((r, n), w.dtype)
    return jnp.concatenate([w[:, 512:1536], w[:, 0:512], w[:, 1792:2304], w[:, 1536:1792], w[:, 2308:2692], w[:, 2304:2308],
                            zeros(28), w[:, 2692:2724], zeros(64)], axis=1)


def _unpad_in_cols(w):
    return jnp.concatenate([w[..., 1024:1536], w[..., 0:1024], w[..., 2048:2304], w[..., 1536:2048], w[..., 2688:2692],
                            w[..., 2304:2688], w[..., 2720:2752]], axis=-1)


_Z_RET = (0, 1, 2, 3)
_Z_SGU = (4, 5)
_Z_FOX_Q, _Z_FOX_K, _Z_FOX_V = 8, 6, 7
_DZ_RET, _DZ_SGU, _DZ_FOX_KV, _DZ_FOX_Q, _DZ_MLA, _DZ_MISC = 0, 2, 3, 8, 6, 21


def _pad_uq(w):
    return jnp.pad(w.reshape(256, N_HEADS, 96), ((0, 0), (0, 0), (0, 32))).reshape(256, 512)


def _unpad_uq(w):
    return w.reshape(256, N_HEADS, 128)[:, :, :96].reshape(256, 384)


def _split_ukv(w):
    r = w.reshape(128, N_HEADS, 128)
    return jnp.pad(r[:, :, :64], ((0, 0), (0, 0), (0, 64))).reshape(128, 512), r[:, :, 64:].reshape(128, 256)


def _join_ukv(dk, dv):
    return jnp.concatenate([dk.reshape(128, N_HEADS, 128)[:, :, :64], dv.reshape(128, N_HEADS, 64)], axis=-1).reshape(128, 512)


def _cols_to_full(g):
    return jnp.transpose(g, (1, 0, 2)).reshape(g.shape[1], NDEV * g.shape[2])


def kernel(x, g_mix_norm, w_in, b_forget, g_sgu, w_spatial, b_spatial, g_mla_q, w_uq, g_mla_kv, w_ukv, g_group_out, w_out, g_ffn_norm, w_up, w_down, g_final, loss_target, m_g_mix_norm, m_w_in, m_b_forget, m_g_sgu, m_w_spatial, m_b_spatial, m_g_mla_q, m_w_uq, m_g_mla_kv, m_w_ukv, m_g_group_out, m_w_out, m_g_ffn_norm, m_w_up, m_w_down, m_g_final, v_g_mix_norm, v_w_in, v_b_forget, v_g_sgu, v_w_spatial, v_b_spatial, v_g_mla_q, v_w_uq, v_g_mla_kv, v_w_ukv, v_g_group_out, v_w_out, v_g_ffn_norm, v_w_up, v_w_down, v_g_final):
    depth = w_in.shape[0]
    s, d = x.shape[1], x.shape[2]
    x0 = x.reshape(s, d)
    target = loss_target.reshape(s, d)
    tb = _tables(s)
    me = 4 * lax.axis_index("x") + 2 * lax.axis_index("y") + lax.axis_index("c")

    assert depth == 2
    shards = {}
    for l in range(depth):
        shards.update({(l, "w_in"): _pad_in_cols(w_in[l]).astype(_WIRE), (l, "w_out"): w_out[l].astype(_WIRE),
                       (l, "w_up"): w_up[l].astype(_WIRE), (l, "w_down"): w_down[l].astype(_WIRE),
                       (l, "w_uq"): w_uq[l].astype(_WIRE), (l, "w_ukv"): w_ukv[l].astype(_WIRE)})
    wts = _ShardedWeights(shards)
    first = [(0, "w_in"), (0, "w_uq"), (0, "w_ukv"), (1, "w_uq"), (1, "w_ukv")]
    wts.full.update(zip(first, _all_gather([shards[k] for k in first], "gather_first")))

    row = lambda a: a.reshape(1, -1)

    def small(l):
        bf = jnp.pad(b_forget[l].reshape(1, N_HEADS), ((0, 0), (0, 128 - N_HEADS)))
        bt = jnp.pad(b_spatial[l].T, ((0, 0), (0, 128 - N_HEADS)))
        return dict(g_mix=row(g_mix_norm[l]), g_sgu=row(g_sgu[l]), w_s=w_spatial[l], b_t=bt, b_f=bf, gq=row(g_mla_q[l]),
                    gkv=row(g_mla_kv[l]), g_go=row(g_group_out[l]), g_ffn=row(g_ffn_norm[l]))

    smalls = [small(l) for l in range(depth)]
    lrow, dx, sm, dg_final = _local_step(x0, target, wts, smalls, row(g_final), tb)
    loss = lax.psum(jnp.sum(lrow), AXES)
    grad_x = dx.reshape(1, s, d)
    return _reduce_and_update(loss, grad_x, wts.recv, sm, dg_final, me, dict(
        g_mix_norm=(g_mix_norm, m_g_mix_norm, v_g_mix_norm), w_in=(w_in, m_w_in, v_w_in),
        b_forget=(b_forget, m_b_forget, v_b_forget), g_sgu=(g_sgu, m_g_sgu, v_g_sgu),
        w_spatial=(w_spatial, m_w_spatial, v_w_spatial), b_spatial=(b_spatial, m_b_spatial, v_b_spatial),
        g_mla_q=(g_mla_q, m_g_mla_q, v_g_mla_q), w_uq=(w_uq, m_w_uq, v_w_uq), g_mla_kv=(g_mla_kv, m_g_mla_kv, v_g_mla_kv),
        w_ukv=(w_ukv, m_w_ukv, v_w_ukv), g_group_out=(g_group_out, m_g_group_out, v_g_group_out),
        w_out=(w_out, m_w_out, v_w_out), g_ffn_norm=(g_ffn_norm, m_g_ffn_norm, v_g_ffn_norm), w_up=(w_up, m_w_up, v_w_up),
        w_down=(w_down, m_w_down, v_w_down), g_final=(g_final, m_g_final, v_g_final)))


_GATHER_AT = {
    "in_proj0": [(0, "w_out")],
    "fox_attn0": [(0, "w_down"), (0, "w_up")],
    "mla_attn0": [(1, "w_in")],
    "ffn_fwd0": [(1, "w_down")],
    "fox_attn1": [(1, "w_out")],
    "mla_attn1": [(1, "w_up")],
}
_SCATTER_AT = {
    "fox_attn_bwd1": [(1, "w_down")],
    "mla_attn_bwd1": [(1, "w_up"), (1, "w_out")],
    "ffn_bwd0": [(1, "w_in")],
    "fox_attn_bwd0": [(0, "w_down")],
    "mla_attn_bwd0": [(0, "w_up"), (0, "w_out")],
    "in_proj_bwd0": [(0, "w_in")],
}


class _FullWeights:
    def __init__(self, per_layer):
        self.per_layer, self.grads = per_layer, {}

    def get(self, l, name):
        return self.per_layer[l][name]

    def comm(self, host):
        return None

    def done(self, host, results):
        pass

    def grad(self, l, name, blocks):
        self.grads[(l, name)] = blocks


class _ShardedWeights(_FullWeights):
    def __init__(self, shards):
        self.shards, self.full, self.grads, self.recv = shards, {}, {}, {}

    def get(self, l, name):
        if name in ("wk", "wv"):
            return _split_ukv(_cols_to_full(self.full[(l, "w_ukv")]))[0 if name == "wk" else 1]
        if name == "wq":
            return _pad_uq(_cols_to_full(self.full[(l, "w_uq")]))
        g = self.full[(l, name)]
        return g if name == "w_up" else g.reshape(NDEV * g.shape[1], g.shape[2])

    def comm(self, host):
        if host in _GATHER_AT:
            return _Comm("gather", [self.shards[k] for k in _GATHER_AT[host]])
        if host in _SCATTER_AT:
            return _Comm("exchange", [self.grads[k] for k in _SCATTER_AT[host]])
        return None

    def done(self, host, results):
        if host in _GATHER_AT:
            self.full.update(zip(_GATHER_AT[host], results))
        if host in _SCATTER_AT:
            self.recv.update(zip(_SCATTER_AT[host], results))


def _local_step(x0, target, wts, smalls, g_final, tb):
    depth = len(smalls)
    s, d = x0.shape
    saved = []
    xl = x0
    for l in range(depth):
        p = smalls[l]
        z, h, got = _norm_matmul(xl, p["g_mix"], wts.get(l, "w_in"), f"in_proj{l}", wts.comm(f"in_proj{l}"))
        wts.done(f"in_proj{l}", got)
        ya = _sgu_fwd(z, p["g_sgu"], p["w_s"], p["b_t"], tb, f"sgu_fwd{l}")
        yb, ret, states = _ret_fwd(z, tb, f"ret_fwd{l}")
        cum = _fox_prep(z, p["b_f"], f"fox_prep{l}")
        kc, vc, vtc, qtc = _kv_prep(z, _Z_FOX_Q, _Z_FOX_K, _Z_FOX_V, HEAD_DIM ** -0.5, f"fox_kv{l}")
        yc, lse_c, got = _attn_fwd(z, _Z_FOX_Q, HEAD_DIM, kc, vtc, HEAD_DIM ** -0.5, cum, f"fox_attn{l}", wts.comm(f"fox_attn{l}"))
        wts.done(f"fox_attn{l}", got)
        wq, wk, wv = wts.get(l, "wq"), wts.get(l, "wk"), wts.get(l, "wv")
        qd, kd, vd, vtd, cqn, ckvn, qtd = _mla_prep(z, p["gq"], p["gkv"], wq, wk, wv, tb, f"mla_prep{l}")
        yd, lse_d, got = _attn_fwd(qd, 0, 128, kd, vtd, _SCALE_D, None, f"mla_attn{l}", wts.comm(f"mla_attn{l}"))
        wts.done(f"mla_attn{l}", got)
        ys = (ya, yb, yc, yd)
        x1, yn = _out_proj(ys, p["g_go"], wts.get(l, "w_out"), xl, f"out_proj{l}")
        x2, u, h2, got = _ffn_fwd(x1, p["g_ffn"], wts.get(l, "w_up"), wts.get(l, "w_down"), f"ffn_fwd{l}", wts.comm(f"ffn_fwd{l}"))
        wts.done(f"ffn_fwd{l}", got)
        saved.append(dict(x=xl, z=z, h=h, ys=ys, ret=ret, states=states, cum=cum, lse_c=lse_c, kc=kc, vc=vc, qd=qd, kd=kd, vd=vd,
                          cqn=cqn, ckvn=ckvn, lse_d=lse_d, x1=x1, yn=yn, u=u, h2=h2, wq=wq, wk=wk, wv=wv, qtc=qtc, qtd=qtd))
        xl = x2

    lrow, dx, dg_final = _loss_head(xl, g_final, target, "loss_head")

    sm = [None] * depth
    for l in reversed(range(depth)):
        p, a = smalls[l], saved[l]
        dx1, du, dg_ffn, got = _ffn_bwd(dx, a["x1"], a["u"], p["g_ffn"], wts.get(l, "w_up"), wts.get(l, "w_down"), f"ffn_bwd{l}",
                                        wts.comm(f"ffn_bwd{l}"))
        wts.done(f"ffn_bwd{l}", got)
        dw_down = _mm_tn(a["u"], dx, f"dw_down{l}", a_fn=lambda t: jnp.square(jnp.maximum(t, 0.0)), out_dtype=_WIRE)
        wts.grad(l, "w_down", dw_down.reshape(NDEV, dw_down.shape[0] // NDEV, d))
        wts.grad(l, "w_up", _mm_tn(a["h2"], du, f"dw_up{l}", blocked=True, out_dtype=_WIRE))
        dya, dyb, dg_go, dot_c, dot_d, dl_c, dl_d = _out_proj_bwd(dx1, wts.get(l, "w_out"), a["ys"], p["g_go"],
                                                                  f"out_proj_bwd{l}")
        wts.grad(l, "w_out", _mm_tn(a["yn"], dx1, f"dw_out{l}", out_dtype=_WIRE).reshape(NDEV, d // NDEV, d))
        dz = _ret_bwd(dyb, a["z"], a["ret"], a["states"], tb, f"ret_bwd{l}")
        dz, dg_sgu, dw_s, db_t = _sgu_bwd(dya, a["z"], p["g_sgu"], p["w_s"], p["b_t"], tb, dz, f"sgu_bwd{l}")
        dqt_c, dz, dck, dcq, got = _attn_bwd(a["kc"], a["vc"], a["qtc"], dot_c, a["lse_c"], dl_c, HEAD_DIM,
                                             HEAD_DIM ** -0.5, a["cum"], f"fox_attn_bwd{l}", _MXU,
                                             wts.comm(f"fox_attn_bwd{l}"), kv_into=dz)
        wts.done(f"fox_attn_bwd{l}", got)
        dz = _untranspose(dqt_c, _MXU, f"fox_dq{l}", into=dz, col=_DZ_FOX_Q)
        dqt_d, dk_d, dv_d, got = _attn_bwd(a["kd"], a["vd"], a["qtd"], dot_d, a["lse_d"], dl_d, 128, _SCALE_D, None,
                                           f"mla_attn_bwd{l}", F32, wts.comm(f"mla_attn_bwd{l}"))
        wts.done(f"mla_attn_bwd{l}", got)
        dz, dkr, dwq, dwk, dwv, dgq, dgkv = _mla_prep_bwd(dqt_d, dk_d, dv_d, a["z"], a["cqn"], a["ckvn"], p["gq"], p["gkv"],
                                                          a["wq"], a["wk"], a["wv"], tb, dz, f"mla_prep_bwd{l}")
        dz, db_f = _fox_post(dcq, dck, a["z"], p["b_f"], dkr, dz, f"fox_post{l}")
        wts.grad(l, "w_in", _unpad_in_cols(_mm_tn(a["h"], dz, f"dw_in{l}", out_dtype=_WIRE)).reshape(NDEV, d // NDEV, N_IN))
        dx, dg_mix, got = _in_proj_bwd(dz, wts.get(l, "w_in"), a["x"], p["g_mix"], dx1, f"in_proj_bwd{l}",
                                       wts.comm(f"in_proj_bwd{l}"))
        wts.done(f"in_proj_bwd{l}", got)
        sm[l] = [dg_mix, dg_go, dg_ffn, dg_sgu, dw_s, db_t[:, :N_HEADS].T, db_f[0, :N_HEADS], dgq, dgkv, _unpad_uq(dwq),
                 _join_ukv(dwk, dwv)]
    return lrow, dx, sm, dg_final


def _reduce_and_update(loss, grad_x, recv, sm, dg_final, me, given):
    depth = len(sm)
    pieces = [t for l in range(depth) for t in sm[l]] + [dg_final]
    flat = jnp.concatenate([t.reshape(-1) for t in pieces])
    n_flat = flat.shape[0]
    unit = NDEV * 8 * 128
    n_pad = -(-n_flat // unit) * unit
    packed = jnp.pad(flat, (0, n_pad - n_flat)).reshape(NDEV, n_pad // (NDEV * 128), 128)
    red = _sum_slots(_exchange([packed], "scatter_small")[0], "sum_small")
    full = _all_gather([red], "gather_small")[0].reshape(-1)
    offs = np.cumsum([0] + [int(np.prod(t.shape)) for t in pieces])
    red_pieces = [full[int(offs[i]):int(offs[i + 1])].reshape(pieces[i].shape) for i in range(len(pieces))]
    per = len(sm[0])
    stack = lambda i: jnp.stack([red_pieces[l * per + i] for l in range(depth)])
    g_small = dict(g_mix_norm=stack(0), g_group_out=stack(1), g_ffn_norm=stack(2), g_sgu=stack(3), w_spatial=stack(4),
                   b_spatial=stack(5), b_forget=stack(6), g_mla_q=stack(7), g_mla_kv=stack(8), g_final=red_pieces[-1])
    cq, ckv = given["w_uq"][0].shape[2], given["w_ukv"][0].shape[2]
    g_small["w_uq"] = lax.dynamic_slice_in_dim(stack(9), me * cq, cq, axis=2)
    g_small["w_ukv"] = lax.dynamic_slice_in_dim(stack(10), me * ckv, ckv, axis=2)

    names = list(given)
    outs = {}
    for nme in names:
        wv_, mv_, vv_ = given[nme]
        shape = wv_.shape
        if nme in ("w_in", "w_out", "w_up", "w_down"):
            res = None
            flat2 = lambda t: t.reshape(-1, shape[-1])
            for l in range(depth):
                res = _adamw_layer(recv[(l, nme)], flat2(wv_), flat2(mv_), flat2(vv_), l, res, f"adamw_{nme}{l}")
            outs[nme] = [t.reshape(shape) for t in res]
        else:
            two = lambda t: t.reshape(-1, shape[-1]) if t.ndim > 1 else t.reshape(1, -1)
            res = _adamw(two(g_small[nme]), two(wv_), two(mv_), two(vv_), f"adamw_{nme}")
            outs[nme] = [r.reshape(shape) for r in res]
    return (loss, grad_x, *[outs[n][0] for n in names], *[outs[n][1] for n in names], *[outs[n][2] for n in names],
            *[outs[n][3] for n in names])
```

```python
import functools

import jax
import jax.numpy as jnp
import numpy as np
from jax import lax
from jax.experimental import pallas as pl
from jax.experimental.pallas import tpu as pltpu

F32 = jnp.float32
_MXU = jnp.bfloat16
_WIRE = jnp.bfloat16
EPS = 1e-6
NDEV = 8
AXES = ("x", "y", "c")
MESH = pl.DeviceIdType.MESH

N_HEADS = 4
HEAD_DIM = 64
GROUP = 256
CHUNK = 128
NZ = 2816
N_IN = 2724
MISC_F, MISC_KR = 0, 32
VMEM_LIMIT = 56 * 1024 * 1024

ADAM_LR, ADAM_B1, ADAM_B2, ADAM_EPS, ADAM_WD, ADAM_STEP = 0.001, 0.9, 0.999, 1e-08, 0.01, 10

SDS = jax.ShapeDtypeStruct


def _cp(*sem):
    return pltpu.CompilerParams(dimension_semantics=sem, vmem_limit_bytes=VMEM_LIMIT)


def _dot(a, b):
    return jnp.dot(a.astype(_MXU), b.astype(_MXU), preferred_element_type=F32)


def _dot_nt(a, b):
    return lax.dot_general(a.astype(_MXU), b.astype(_MXU), (((1,), (1,)), ((), ())), preferred_element_type=F32)


def _dot_tn(a, b):
    return lax.dot_general(a.astype(_MXU), b.astype(_MXU), (((0,), (0,)), ((), ())), preferred_element_type=F32)


def _dot_exact(a, b, dims=(((1,), (0,)), ((), ()))):
    return lax.dot_general(a, b, dims, precision=lax.Precision.HIGHEST, preferred_element_type=F32)


def _rms(x, g):
    return x * lax.rsqrt(jnp.mean(x * x, axis=-1, keepdims=True) + EPS) * g


def _rms_bwd(x, g, dy):
    xh = x * lax.rsqrt(jnp.mean(x * x, axis=-1, keepdims=True) + EPS)
    dxh = dy * g
    r = lax.rsqrt(jnp.mean(x * x, axis=-1, keepdims=True) + EPS)
    dx = r * (dxh - xh * jnp.mean(dxh * xh, axis=-1, keepdims=True))
    return dx, jnp.sum(dy * xh, axis=0, keepdims=True)


_GELU_C = 0.7978845608028654


def _gelu(x):
    return 0.5 * x * (1.0 + jnp.tanh(_GELU_C * (x + 0.044715 * x * x * x)))


def _gelu_grad(x):
    t = jnp.tanh(_GELU_C * (x + 0.044715 * x * x * x))
    return 0.5 * (1.0 + t) + 0.5 * x * (1.0 - t * t) * _GELU_C * (1.0 + 3 * 0.044715 * x * x)


def _sigmoid(x):
    return 1.0 / (1.0 + jnp.exp(-x))


def _swap_half(t, half):
    n = t.shape[-1]
    lane = lax.broadcasted_iota(jnp.int32, t.shape, t.ndim - 1)
    return jnp.where((lane % (2 * half)) < half, pltpu.roll(t, n - half, t.ndim - 1), pltpu.roll(t, half, t.ndim - 1))


def _lanes(table, width):
    return jnp.concatenate([table] * (width // table.shape[-1]), axis=-1)


def _rope(t, cos, sin, half):
    return t * cos + _swap_half(t, half) * sin


def _rope_bwd(d, cos, sin, half):
    return d * cos - _swap_half(d, half) * sin


def _tables(s):
    pos = jnp.arange(s, dtype=F32)[:, None]

    def cs(half):
        inv = jnp.power(10000.0, -jnp.arange(half, dtype=F32) / half)
        ang = pos * inv[None, :]
        return jnp.cos(ang), jnp.sin(ang)

    c32, s32 = cs(32)
    c16, s16 = cs(16)
    z = lambda w: jnp.zeros((s, w), F32)
    o = lambda w: jnp.ones((s, w), F32)
    t = {}
    t["b_cos"] = jnp.concatenate([c32, c32, c32, c32], 1)
    t["b_sin"] = jnp.concatenate([-s32, s32, -s32, s32], 1)
    t["q_cos"] = jnp.concatenate([o(64), c16, c16, z(32)], 1)
    t["q_sin"] = jnp.concatenate([z(64), -s16, s16, z(32)], 1)
    t["k_cos"] = jnp.concatenate([z(32), c16, c16, z(64)], 1)
    t["k_sin"] = jnp.concatenate([z(32), -s16, s16, z(64)], 1)
    lg = jnp.log1p(-jnp.exp2(-5.0 - jnp.arange(N_HEADS, dtype=F32)))
    j = jnp.arange(CHUNK, dtype=F32)
    rel = j[:, None] - j[None, :]
    t["decay"] = jnp.where(rel[None] >= 0, jnp.exp(jnp.maximum(rel, 0.0)[None] * lg[:, None, None]), 0.0)
    t["decay_t"] = jnp.swapaxes(t["decay"], 1, 2)

    def rows(e):
        return jnp.repeat(e.T, HEAD_DIM, axis=1)

    t["qw"] = rows(jnp.exp((j + 1.0)[None, :] * lg[:, None]))
    t["kw"] = rows(jnp.exp((CHUNK - 1 - j)[None, :] * lg[:, None]))
    t["kw2"] = rows(jnp.exp((CHUNK - j)[None, :] * lg[:, None]))
    t["qw0"] = rows(jnp.exp(j[None, :] * lg[:, None]))
    t["cd"] = jnp.repeat(jnp.exp(CHUNK * lg), HEAD_DIM)[None, :]
    e = np.zeros((128, 512), np.float32)
    for h in range(N_HEADS):
        for r in range(32):
            e[MISC_KR + r, 128 * h + 64 + r] = 1.0
    t["place"] = jnp.asarray(e)
    lane_head = np.arange(GROUP) // HEAD_DIM
    t["grp"] = jnp.asarray((lane_head[:, None] == lane_head[None, :]) / HEAD_DIM, _MXU)
    hsel = (np.arange(128)[:, None] == lane_head[None, :]).astype(np.float32)
    t["hsel"] = jnp.asarray(hsel)
    t["hselt"] = jnp.asarray(hsel.T, _MXU)
    return t


def _norm_matmul(x, g, w, name, comm=None):
    s, d = x.shape
    n = w.shape[1]
    tm, tn = min(512, s), 256
    ni = s // tm

    def body(*refs):
        (x_ref, g_ref, w_ref), (z_ref, h_ref), _, cc = _split_refs(refs, 3, 2, comm)
        i = pl.program_id(0)
        _host_gather(comm, cc, i, ni, late=True)
        h = _rms(x_ref[...], g_ref[...]).astype(h_ref.dtype)
        h_ref[...] = h
        for j in range(n // tn):
            z_ref[:, tn * j:tn * (j + 1)] = jnp.dot(h, w_ref[:, tn * j:tn * (j + 1)], preferred_element_type=F32)
        if comm is not None:
            @pl.when(i == ni - 1)
            def _():
                comm.wait(*cc)

    in_specs = [pl.BlockSpec((tm, d), lambda i: (i, 0)), pl.BlockSpec((1, d), lambda i: (0, 0)),
                pl.BlockSpec((d, n), lambda i: (0, 0))]
    out_specs = [pl.BlockSpec((tm, n), lambda i: (i, 0)), pl.BlockSpec((tm, d), lambda i: (i, 0))]
    out_shape = [SDS((s, n), F32), SDS((s, d), _MXU)]
    return _call_with_comm(body, (ni,), in_specs, out_specs, out_shape, [], [x, g, w], comm, ("arbitrary",), name)


def _mm_tn(a, b, name, *, a_fn=None, blocked=False, out_dtype=F32):
    k, m = a.shape
    n = b.shape[1]
    tm, tk = min(1024, m), min(1024, k)
    tn = next(t for t in (2816, 1024, 512, 256, 128) if n % t == 0)
    assert m % tm == 0 and k % tk == 0
    nk = k // tk

    def body(a_ref, b_ref, o_ref, acc):
        kk = pl.program_id(2)

        @pl.when(kk == 0)
        def _():
            acc[...] = jnp.zeros_like(acc)

        av = a_ref[...]
        if a_fn is not None:
            av = a_fn(av.astype(F32))
        acc[...] += _dot_tn(av, b_ref[...])

        @pl.when(kk == nk - 1)
        def _():
            if blocked:
                for c in range(tn // 512):
                    o_ref[c] = acc[:, 512 * c:512 * (c + 1)].astype(o_ref.dtype)
            else:
                o_ref[...] = acc[...].astype(o_ref.dtype)

    if blocked:
        assert tn % 512 == 0
        out_spec = pl.BlockSpec((tn // 512, tm, 512), lambda i, j, kk: (j, i, 0))
        out_shape = SDS((n // 512, m, 512), out_dtype)
    else:
        out_spec = pl.BlockSpec((tm, tn), lambda i, j, kk: (i, j))
        out_shape = SDS((m, n), out_dtype)
    return pl.pallas_call(
        body, grid=(m // tm, n // tn, nk),
        in_specs=[pl.BlockSpec((tk, tm), lambda i, j, kk: (kk, i)), pl.BlockSpec((tk, tn), lambda i, j, kk: (kk, j))],
        out_specs=out_spec, out_shape=out_shape, scratch_shapes=[pltpu.VMEM((tm, tn), F32)],
        compiler_params=_cp("parallel", "parallel", "arbitrary"), name=name)(a, b)


def _split_dot(x, m):
    hi = x.astype(_MXU)
    lo = (x - hi.astype(F32)).astype(_MXU)
    return jnp.dot(hi, m, preferred_element_type=F32) + jnp.dot(lo, m, preferred_element_type=F32)


def _gstandardize(t, grp):
    tc = t - _split_dot(t, grp)
    rs = lax.rsqrt(_split_dot(tc * tc, grp) + EPS)
    return tc * rs, rs


def _gstandardize_bwd(yh, rs, dy, grp):
    return rs * (dy - _split_dot(dy, grp) - yh * _split_dot(dy * yh, grp))


def _head_select(parts):
    hid = lax.broadcasted_iota(jnp.int32, parts[0].shape, 1) // HEAD_DIM
    return jnp.where(hid == 0, parts[0], jnp.where(hid == 1, parts[1], jnp.where(hid == 2, parts[2], parts[3])))


def _head_masked(x):
    hid = lax.broadcasted_iota(jnp.int32, x.shape, 1) // HEAD_DIM
    return [jnp.where(hid == h, x, jnp.zeros_like(x)) for h in range(N_HEADS)]


def _tril(w):
    r = lax.broadcasted_iota(jnp.int32, w.shape, 0)
    c = lax.broadcasted_iota(jnp.int32, w.shape, 1)
    return jnp.where(r >= c, w, 0.0)


def _sgu_mixed(vgb, wcs, bias, nchunk):
    ms = [[jnp.dot(wcs[h], vgb[CHUNK * c:CHUNK * (c + 1)], preferred_element_type=F32) for h in range(N_HEADS)]
          for c in range(nchunk)]
    return [_head_select(ms[c]) + bias for c in range(nchunk)]


def _sgu_fwd(z, gain, w_s, b_t, tb, name):
    s = z.shape[0]
    tm = min(512, s)
    const = lambda a: pl.BlockSpec(a.shape, lambda i: (0,) * a.ndim)

    def body(u_ref, v_ref, g_ref, w_ref, b_ref, grp_ref, hsel_ref, y_ref):
        u = _gelu(u_ref[...])
        vh, _ = _gstandardize(_gelu(v_ref[...]), grp_ref[...])
        vgb = (vh * g_ref[...]).astype(_MXU)
        bias = _dot_exact(b_ref[...], hsel_ref[...])
        wcs = [_tril(w_ref[h]).astype(_MXU) for h in range(N_HEADS)]
        for c, mixed in enumerate(_sgu_mixed(vgb, wcs, bias, tm // CHUNK)):
            r = slice(CHUNK * c, CHUNK * (c + 1))
            y_ref[r, :] = u[r] * mixed

    return pl.pallas_call(
        body, grid=(s // tm,),
        in_specs=[pl.BlockSpec((tm, GROUP), lambda i: (i, _Z_SGU[0])), pl.BlockSpec((tm, GROUP), lambda i: (i, _Z_SGU[1])),
                  pl.BlockSpec((1, GROUP), lambda i: (0, 0)), pl.BlockSpec((N_HEADS, CHUNK, CHUNK), lambda i: (0, 0, 0)),
                  pl.BlockSpec((CHUNK, 128), lambda i: (0, 0)), const(tb["grp"]), const(tb["hsel"])],
        out_specs=pl.BlockSpec((tm, GROUP), lambda i: (i, 0)), out_shape=SDS((s, GROUP), F32),
        compiler_params=_cp("parallel"), name=name)(z, z, gain, w_s, b_t, tb["grp"], tb["hsel"])


def _sgu_bwd(dy, z, gain, w_s, b_t, tb, dz, name):
    s = z.shape[0]
    tm = min(512, s)
    nchunk = tm // CHUNK
    const = lambda a: pl.BlockSpec(a.shape, lambda i: (0,) * a.ndim)

    def body(dy_ref, u_ref, v_ref, g_ref, w_ref, b_ref, grp_ref, hsel_ref, hselt_ref, _, dz_ref, dg_ref, dw_ref, db_ref):
        @pl.when(pl.program_id(0) == 0)
        def _():
            dg_ref[...] = jnp.zeros_like(dg_ref)
            dw_ref[...] = jnp.zeros_like(dw_ref)
            db_ref[...] = jnp.zeros_like(db_ref)

        grp = grp_ref[...]
        u_pre, v_pre, gain_v = u_ref[...], v_ref[...], g_ref[...]
        u = _gelu(u_pre)
        vh, rs = _gstandardize(_gelu(v_pre), grp)
        vgb = (vh * gain_v).astype(_MXU)
        dyv = dy_ref[...]
        bias = _dot_exact(b_ref[...], hsel_ref[...])
        wfs = [_tril(w_ref[h]) for h in range(N_HEADS)]
        wcs = [w.astype(_MXU) for w in wfs]
        wts = [w.T.astype(_MXU) for w in wfs]
        mixed = _sgu_mixed(vgb, wcs, bias, nchunk)
        gu = _gelu_grad(u_pre)
        dms, dmh = [], []
        for c in range(nchunk):
            r = slice(CHUNK * c, CHUNK * (c + 1))
            dz_ref[r, 0:GROUP] = (dyv[r] * mixed[c] * gu[r]).astype(dz_ref.dtype)
            dm = dyv[r] * u[r]
            dms.append(dm)
            dmh.append([m.astype(_MXU) for m in _head_masked(dm)])
        dws = [sum(lax.dot_general(dmh[c][h], vgb[CHUNK * c:CHUNK * (c + 1)], (((1,), (1,)), ((), ())),
                                   preferred_element_type=F32) for c in range(nchunk)) for h in range(N_HEADS)]
        dvg = jnp.concatenate([sum(jnp.dot(wts[h], dmh[c][h], preferred_element_type=F32) for h in range(N_HEADS))
                               for c in range(nchunk)], axis=0)
        for h in range(N_HEADS):
            dw_ref[h] += _tril(dws[h])
        db_ref[...] += sum(_split_dot(dm, hselt_ref[...]) for dm in dms)
        dg_ref[...] += jnp.sum(dvg * vh, axis=0, keepdims=True)
        dv = _gstandardize_bwd(vh, rs, dvg * gain_v, grp)
        dz_ref[:, GROUP:2 * GROUP] = (dv * _gelu_grad(v_pre)).astype(dz_ref.dtype)

    consts = [tb["grp"], tb["hsel"], tb["hselt"]]
    return pl.pallas_call(
        body, grid=(s // tm,),
        in_specs=[pl.BlockSpec((tm, GROUP), lambda i: (i, 0)),
                  pl.BlockSpec((tm, GROUP), lambda i: (i, _Z_SGU[0])), pl.BlockSpec((tm, GROUP), lambda i: (i, _Z_SGU[1])),
                  pl.BlockSpec((1, GROUP), lambda i: (0, 0)), pl.BlockSpec((N_HEADS, CHUNK, CHUNK), lambda i: (0, 0, 0)),
                  pl.BlockSpec((CHUNK, 128), lambda i: (0, 0))] + [const(a) for a in consts]
        + [pl.BlockSpec(memory_space=pl.ANY)],
        out_specs=[pl.BlockSpec((tm, 2 * GROUP), lambda i: (i, _DZ_SGU)), pl.BlockSpec((1, GROUP), lambda i: (0, 0)),
                   pl.BlockSpec((N_HEADS, CHUNK, CHUNK), lambda i: (0, 0, 0)), pl.BlockSpec((CHUNK, 128), lambda i: (0, 0))],
        out_shape=[SDS(dz.shape, dz.dtype), SDS((1, GROUP), F32), SDS((N_HEADS, CHUNK, CHUNK), F32), SDS((CHUNK, 128), F32)],
        input_output_aliases={9: 0}, compiler_params=_cp("arbitrary"), name=name)(dy, z, z, gain, w_s, b_t, *consts, dz)


_SCALE_B = HEAD_DIM ** -0.5
RET_CHUNKS = 8
RET_CHUNKS_BWD = 4


def _block_diag(compact):
    full = jnp.concatenate([compact] * N_HEADS, axis=0)
    r = lax.broadcasted_iota(jnp.int32, full.shape, 0) // HEAD_DIM
    c = lax.broadcasted_iota(jnp.int32, full.shape, 1) // HEAD_DIM
    return jnp.where(r == c, full, 0.0)


def _diag_blocks(full):
    c = lax.broadcasted_iota(jnp.int32, (HEAD_DIM, GROUP), 1) // HEAD_DIM
    return sum(jnp.where(c == h, full[HEAD_DIM * h:HEAD_DIM * (h + 1), :], 0.0) for h in range(N_HEADS))


def _ret_fwd(z, tb, name):
    s = z.shape[0]
    nc = s // CHUNK
    per = min(RET_CHUNKS, nc)
    rows = per * CHUNK
    row = lambda col: pl.BlockSpec((rows, GROUP), lambda n, col=col: (n, col))
    const = lambda shape: pl.BlockSpec(shape, lambda n: (0,) * len(shape))

    def body(q_ref, k_ref, v_ref, g_ref, cos_ref, sin_ref, dec_ref, qw_ref, kw_ref, cd_ref, grp_ref, y_ref, o_ref, st_ref, state):
        @pl.when(pl.program_id(0) == 0)
        def _():
            state[...] = jnp.zeros_like(state)

        cos, sin = _lanes(cos_ref[...], GROUP), _lanes(sin_ref[...], GROUP)
        q = _rope(q_ref[...], cos, sin, 32)
        k = _rope(k_ref[...], cos, sin, 32) * _SCALE_B
        v = v_ref[...]
        g = g_ref[...]
        rcs = [slice(CHUNK * c, CHUNK * (c + 1)) for c in range(per)]
        vms = [[t.astype(_MXU) for t in _head_masked(v[r])] for r in rcs]
        scs = [[_dot_nt(t.astype(_MXU), k[r]) for t in _head_masked(q[r])] for r in rcs]
        kvs = [_dot_tn(k[r] * kw_ref[...], v[r]) for r in rcs]
        st = state[...]
        crosses = []
        for c, r in enumerate(rcs):
            st_ref[c] = st
            crosses.append(_dot(q[r] * qw_ref[...], _block_diag(st)))
            st = cd_ref[...] * st + _diag_blocks(kvs[c])
        state[...] = st
        outs = []
        for c in range(per):
            scd = [(scs[c][h] * dec_ref[h]).astype(_MXU) for h in range(N_HEADS)]
            outs.append(crosses[c] + sum(jnp.dot(scd[h], vms[c][h], preferred_element_type=F32) for h in range(N_HEADS)))
        o = jnp.concatenate(outs, axis=0)
        o_ref[...] = o
        yh, _ = _gstandardize(o, grp_ref[...])
        y_ref[...] = g * _sigmoid(g) * yh

    return pl.pallas_call(
        body, grid=(nc // per,),
        in_specs=[row(_Z_RET[0]), row(_Z_RET[1]), row(_Z_RET[2]), row(_Z_RET[3]), pl.BlockSpec((rows, 128), lambda n: (n, 0)),
                  pl.BlockSpec((rows, 128), lambda n: (n, 0)), const((N_HEADS, CHUNK, CHUNK)),
                  const((CHUNK, GROUP)), const((CHUNK, GROUP)), const((1, GROUP)), const((GROUP, GROUP))],
        out_specs=[pl.BlockSpec((rows, GROUP), lambda n: (n, 0)), pl.BlockSpec((rows, GROUP), lambda n: (n, 0)),
                   pl.BlockSpec((per, HEAD_DIM, GROUP), lambda n: (n, 0, 0))],
        out_shape=[SDS((s, GROUP), F32), SDS((s, GROUP), F32), SDS((nc, HEAD_DIM, GROUP), F32)],
        scratch_shapes=[pltpu.VMEM((HEAD_DIM, GROUP), F32)],
        compiler_params=_cp("arbitrary"), name=name)(z, z, z, z, tb["b_cos"], tb["b_sin"], tb["decay"], tb["qw"], tb["kw"], tb["cd"],
                                                       tb["grp"])


def _ret_bwd(dy, z, o_pre, states, tb, name):
    s = z.shape[0]
    nc = s // CHUNK
    per = min(RET_CHUNKS_BWD, nc)
    rows = per * CHUNK
    ns = nc // per
    rev = lambda col: pl.BlockSpec((rows, GROUP), lambda n, col=col: (ns - 1 - n, col))
    const = lambda shape: pl.BlockSpec(shape, lambda n: (0,) * len(shape))

    def body(dy_ref, q_ref, k_ref, v_ref, g_ref, o_ref, st_ref, cos_ref, sin_ref, dec_ref, dect_ref, qw_ref, kw2_ref, qw0_ref,
             cd_ref, grp_ref, dz_ref, rstate):
        @pl.when(pl.program_id(0) == 0)
        def _():
            rstate[...] = jnp.zeros_like(rstate)

        cos, sin = _lanes(cos_ref[...], GROUP), _lanes(sin_ref[...], GROUP)
        q = _rope(q_ref[...], cos, sin, 32)
        k = _rope(k_ref[...], cos, sin, 32) * _SCALE_B
        v = v_ref[...]
        g = g_ref[...]
        dyv = dy_ref[...]
        sg = _sigmoid(g)
        yh, rs = _gstandardize(o_ref[...], grp_ref[...])
        dz_ref[:, 3 * GROUP:4 * GROUP] = (dyv * yh * (sg * (1.0 + g * (1.0 - sg)))).astype(dz_ref.dtype)
        do = _gstandardize_bwd(yh, rs, dyv * (g * sg), grp_ref[...])
        hs = range(N_HEADS)
        rcs = [slice(CHUNK * c, CHUNK * (c + 1)) for c in range(per)]
        mask = lambda t: [m.astype(_MXU) for m in _head_masked(t)]
        qms, kms, vms, doms = ([mask(t[r]) for r in rcs] for t in (q, k, v, do))
        dps = [[_dot_nt(doms[c][h], v[r]) for h in hs] for c, r in enumerate(rcs)]
        pts = [[_dot_nt(kms[c][h], q[r]) for h in hs] for c, r in enumerate(rcs)]
        dpts = [[_dot_nt(vms[c][h], do[r]) for h in hs] for c, r in enumerate(rcs)]
        dq_x = [_dot_nt(do[r] * qw_ref[...], _block_diag(st_ref[c])) for c, r in enumerate(rcs)]
        r_new = [_dot_tn(q[r] * qw0_ref[...], do[r]) for r in rcs]
        rr = rstate[...]
        dk_x, dv_x = [None] * per, [None] * per
        for c in reversed(range(per)):
            r_bd = _block_diag(rr)
            dk_x[c] = _dot_nt(v[rcs[c]] * kw2_ref[...], r_bd)
            dv_x[c] = _dot(k[rcs[c]] * kw2_ref[...], r_bd)
            rr = cd_ref[...] * rr + _diag_blocks(r_new[c])
        rstate[...] = rr
        dqs, dks, dvs = [], [], []
        for c in range(per):
            dpd = [(dps[c][h] * dec_ref[h]).astype(_MXU) for h in hs]
            dptd = [(dpts[c][h] * dect_ref[h]).astype(_MXU) for h in hs]
            ptd = [(pts[c][h] * dect_ref[h]).astype(_MXU) for h in hs]
            dqs.append(dq_x[c] + sum(jnp.dot(dpd[h], kms[c][h], preferred_element_type=F32) for h in hs))
            dks.append(dk_x[c] + sum(jnp.dot(dptd[h], qms[c][h], preferred_element_type=F32) for h in hs))
            dvs.append(dv_x[c] + sum(jnp.dot(ptd[h], doms[c][h], preferred_element_type=F32) for h in hs))
        dz_ref[:, 0:GROUP] = _rope_bwd(jnp.concatenate(dqs, axis=0), cos, sin, 32).astype(dz_ref.dtype)
        dz_ref[:, GROUP:2 * GROUP] = _rope_bwd(jnp.concatenate(dks, axis=0) * _SCALE_B, cos, sin, 32).astype(dz_ref.dtype)
        dz_ref[:, 2 * GROUP:3 * GROUP] = jnp.concatenate(dvs, axis=0).astype(dz_ref.dtype)

    r0 = lambda: pl.BlockSpec((rows, GROUP), lambda n: (ns - 1 - n, 0))
    r128 = lambda: pl.BlockSpec((rows, 128), lambda n: (ns - 1 - n, 0))
    return pl.pallas_call(
        body, grid=(ns,),
        in_specs=[r0(), rev(_Z_RET[0]), rev(_Z_RET[1]), rev(_Z_RET[2]), rev(_Z_RET[3]), r0(),
                  pl.BlockSpec((per, HEAD_DIM, GROUP), lambda n: (ns - 1 - n, 0, 0)),
                  r128(), r128(), const((N_HEADS, CHUNK, CHUNK)), const((N_HEADS, CHUNK, CHUNK)), const((CHUNK, GROUP)),
                  const((CHUNK, GROUP)), const((CHUNK, GROUP)), const((1, GROUP)), const((GROUP, GROUP))],
        out_specs=pl.BlockSpec((rows, 4 * GROUP), lambda n: (ns - 1 - n, _DZ_RET)),
        out_shape=SDS((s, NZ), _MXU), scratch_shapes=[pltpu.VMEM((HEAD_DIM, GROUP), F32)],
        compiler_params=_cp("arbitrary"), name=name)(
            dy, z, z, z, z, o_pre, states, tb["b_cos"], tb["b_sin"], tb["decay"], tb["decay_t"], tb["qw"], tb["kw2"], tb["qw0"],
            tb["cd"], tb["grp"])


TQ = 256


def _log_sigmoid(x):
    return jnp.minimum(x, 0.0) - jnp.log1p(jnp.exp(-jnp.abs(x)))


def _fox_prep(z, b_f, name):
    s = z.shape[0]
    nb = s // TQ

    def body(m_ref, b_ref, cc_ref, carry):
        @pl.when(pl.program_id(0) == 0)
        def _():
            carry[...] = jnp.zeros_like(carry)

        lane = lax.broadcasted_iota(jnp.int32, (TQ, 128), 1)
        logf = jnp.where(lane < N_HEADS, _log_sigmoid(m_ref[...] + b_ref[...]), 0.0)
        r = lax.broadcasted_iota(jnp.int32, (TQ, TQ), 0)
        c = lax.broadcasted_iota(jnp.int32, (TQ, TQ), 1)
        tri = jnp.where(r >= c, 1.0, 0.0).astype(F32)
        cum = _dot_exact(tri, logf) + carry[...]
        cc_ref[...] = cum * LOG2E
        carry[...] = cum[TQ - 1:TQ, :]

    return pl.pallas_call(
        body, grid=(nb,),
        in_specs=[pl.BlockSpec((TQ, 128), lambda i: (i, NZ // 128 - 1)), pl.BlockSpec((1, 128), lambda i: (0, 0))],
        out_specs=pl.BlockSpec((TQ, 128), lambda i: (i, 0)),
        out_shape=SDS((s, 128), F32), scratch_shapes=[pltpu.VMEM((1, 128), F32)],
        compiler_params=_cp("arbitrary"), name=name)(z, b_f)


def _fox_post(dcr, dcq, z, b_f, dkr, dz, name):
    s = z.shape[0]
    nb = s // TQ

    def body(dc_ref, dcq_ref, m_ref, b_ref, dkr_ref, _, dz_ref, db_ref, carry):
        @pl.when(pl.program_id(0) == 0)
        def _():
            carry[...] = jnp.zeros_like(carry)
            db_ref[...] = jnp.zeros_like(db_ref)

        r = lax.broadcasted_iota(jnp.int32, (TQ, TQ), 0)
        c = lax.broadcasted_iota(jnp.int32, (TQ, TQ), 1)
        triu = jnp.where(c >= r, 1.0, 0.0).astype(F32)
        dc = jnp.concatenate([dc_ref[0], jnp.zeros((120, TQ), F32)], axis=0)
        dlogf = _dot_exact(triu, dc, (((1,), (1,)), ((), ()))) + _dot_exact(triu, dcq_ref[...]) + carry[...]
        carry[...] = dlogf[0:1, :]
        x = m_ref[...] + b_ref[...]
        lane = lax.broadcasted_iota(jnp.int32, (TQ, 128), 1)
        df = jnp.where(lane < N_HEADS, dlogf * _sigmoid(-x), 0.0)
        db_ref[...] += jnp.sum(df, axis=0, keepdims=True)
        dz_ref[...] = (df + dkr_ref[...]).astype(dz_ref.dtype)

    rv = lambda i: nb - 1 - i
    return pl.pallas_call(
        body, grid=(nb,),
        in_specs=[pl.BlockSpec((1, 8, TQ), lambda i: (rv(i), 0, 0)), pl.BlockSpec((TQ, 128), lambda i: (rv(i), 0)),
                  pl.BlockSpec((TQ, 128), lambda i: (rv(i), NZ // 128 - 1)),
                  pl.BlockSpec((1, 128), lambda i: (0, 0)), pl.BlockSpec((TQ, 128), lambda i: (rv(i), 0)),
                  pl.BlockSpec(memory_space=pl.ANY)],
        out_specs=[pl.BlockSpec((TQ, 128), lambda i: (rv(i), _DZ_MISC)), pl.BlockSpec((1, 128), lambda i: (0, 0))],
        out_shape=[SDS(dz.shape, dz.dtype), SDS((1, 128), F32)], scratch_shapes=[pltpu.VMEM((1, 128), F32)],
        input_output_aliases={5: 0}, compiler_params=_cp("arbitrary"), name=name)(dcr, dcq, z, b_f, dkr, dz)


NEG = -1e30
TKV = 512


def _key_block(s):
    return min(TKV, s)


def _diag_mask(shape, off):
    r = lax.broadcasted_iota(jnp.int32, shape, 0)
    c = lax.broadcasted_iota(jnp.int32, shape, 1)
    return c + off >= r


def _head_lanes(h, dqk):
    return slice(128 * (h // 2), 128 * (h // 2) + 128) if dqk == HEAD_DIM else slice(128 * h, 128 * h + 128)


def _keep_half(x, a, axis):
    idx = lax.broadcasted_iota(jnp.int32, x.shape, axis)
    return jnp.where((idx < HEAD_DIM) if a == 0 else (idx >= HEAD_DIM), x, jnp.zeros_like(x))


def _scaled_qt(q, scale):
    qs = q.astype(F32) * (scale * LOG2E)
    return [qs[TQ * b:TQ * (b + 1)].T.astype(_MXU) for b in range(q.shape[0] // TQ)]


def _kv_prep(z, qcol, kcol, vcol, scale, name):
    s = z.shape[0]
    tk = _key_block(s)
    nk = s // tk

    def body(q_ref, k_ref, v_ref, kb_ref, vb_ref, vt_ref, qt_ref):
        kb_ref[...] = k_ref[...].astype(_MXU)
        v = v_ref[...]
        vb_ref[...] = v.astype(_MXU)
        vt_ref[0] = v.T.astype(_MXU)
        for b, t in enumerate(_scaled_qt(q_ref[...], scale)):
            qt_ref[b] = t

    blk = pl.BlockSpec((tk, GROUP), lambda i: (i, 0))
    col = lambda c: pl.BlockSpec((tk, GROUP), lambda i, c=c: (i, c))
    return pl.pallas_call(
        body, grid=(nk,), in_specs=[col(qcol), col(kcol), col(vcol)],
        out_specs=[blk, blk, pl.BlockSpec((1, GROUP, tk), lambda i: (i, 0, 0)),
                   pl.BlockSpec((tk // TQ, GROUP, TQ), lambda i: (i, 0, 0))],
        out_shape=[SDS((s, GROUP), _MXU), SDS((s, GROUP), _MXU), SDS((nk, GROUP, tk), _MXU), SDS((s // TQ, GROUP, TQ), _MXU)],
        compiler_params=_cp("parallel"), name=name)(z, z, z)


LOG2E = 1.4426950408889634


def _attn_fwd(q, qcol, dqk, kb, vt, scale, ck2, name, comm=None):
    s = q.shape[0]
    nq = s // TQ
    tk = _key_block(s)
    ratio = tk // TQ
    wq = N_HEADS * dqk
    bias = ck2 is not None

    def body(*refs):
        ins, (o_ref, l_ref), _, cc = _split_refs(refs, 4 if bias else 3, 2, comm)
        if bias:
            q_ref, k_ref, vt_ref, cc_ref = ins
        else:
            q_ref, k_ref, vt_ref = ins
        i = pl.program_id(0)
        _host_gather(comm, cc, i, nq)
        qts = []
        for h in range(N_HEADS):
            qt = (q_ref[:, _head_lanes(h, dqk)].astype(F32) * (scale * LOG2E)).T
            qts.append((_keep_half(qt, h % 2, 0) if dqk == HEAD_DIM else qt).astype(_MXU))

        def step(j, carry, off):
            r0 = pl.multiple_of(j * tk, tk)
            vtj = vt_ref[j]
            sts = [jnp.dot(k_ref[pl.ds(r0, tk), _head_lanes(h, dqk)], qts[h], preferred_element_type=F32)
                   for h in range(N_HEADS)]
            stats, ps = [], []
            for h in range(N_HEADS):
                m, l, _ = carry[3 * h:3 * h + 3]
                st = sts[h]
                if bias:
                    st = st - cc_ref[pl.ds(r0, tk), h:h + 1]
                if off is not None:
                    st = jnp.where(_diag_mask(st.shape, off), st, NEG)
                m_new = jnp.maximum(m, jnp.max(st, axis=0, keepdims=True))
                alpha = jnp.exp2(m - m_new)
                p = jnp.exp2(st - m_new)
                stats.append((m_new, alpha * l + jnp.sum(p, axis=0, keepdims=True), alpha))
                ps.append(p.astype(_MXU))
            out = []
            for h in range(N_HEADS):
                m_new, l, alpha = stats[h]
                acc = alpha * carry[3 * h + 2] + jnp.dot(vtj[HEAD_DIM * h:HEAD_DIM * (h + 1), :], ps[h],
                                                         preferred_element_type=F32)
                out += [m_new, l, acc]
            return tuple(out)

        init = (jnp.full((1, TQ), NEG, F32), jnp.zeros((1, TQ), F32), jnp.zeros((HEAD_DIM, TQ), F32)) * N_HEADS
        jd = i // ratio
        carry = lax.fori_loop(0, jd, functools.partial(step, off=None), init)
        carry = step(jd, carry, TQ * (i % ratio))
        l_ref[...] = jnp.zeros_like(l_ref)
        for h in range(N_HEADS):
            l_ref[0, h:h + 1, :] = carry[3 * h] + jnp.log2(carry[3 * h + 1])
        for p in range(2):
            ot = jnp.concatenate([carry[6 * p + 2] / carry[6 * p + 1], carry[6 * p + 5] / carry[6 * p + 4]], axis=0)
            o_ref[:, 128 * p:128 * (p + 1)] = ot.T
        if comm is not None:
            @pl.when(i == nq - 1)
            def _():
                comm.wait(*cc)

    rows = pl.BlockSpec((1, 8, TQ), lambda i: (i, 0, 0))
    in_specs = [pl.BlockSpec((TQ, wq), lambda i: (i, qcol)), pl.BlockSpec((s, wq), lambda i: (0, 0)),
                pl.BlockSpec((s // tk, GROUP, tk), lambda i: (0, 0, 0))]
    args = [q, kb, vt]
    if bias:
        in_specs.append(pl.BlockSpec((s, 128), lambda i: (0, 0)))
        args.append(ck2)
    out_specs = [pl.BlockSpec((TQ, GROUP), lambda i: (i, 0)), rows]
    out_shape = [SDS((s, GROUP), F32), SDS((nq, 8, TQ), F32)]
    return _call_with_comm(body, (nq,), in_specs, out_specs, out_shape, [], args, comm, ("arbitrary",), name)


def _call_with_comm(body, grid, in_specs, out_specs, out_shape, scratch, args, comm, semantics, name, aliases=None):
    n_out = len(out_shape)
    if comm is not None:
        in_specs, out_specs = in_specs + comm.in_specs, out_specs + comm.out_specs
        out_shape, scratch, args = out_shape + comm.out_shape, scratch + comm.scratch, list(args) + comm.arrs
    res = pl.pallas_call(body, grid=grid, in_specs=in_specs, out_specs=out_specs, out_shape=out_shape,
                         scratch_shapes=scratch, input_output_aliases=aliases or {}, compiler_params=_cp(*semantics),
                         name=name)(*args)
    return (*res[:n_out], list(res[n_out:]))


def _attn_bwd(kb, vb, qt, dot, lse, dl, dqk, scale, ck2, name, kv_dtype, comm=None, kv_into=None):
    s = kb.shape[0]
    nq = s // TQ
    tk = _key_block(s)
    ratio = tk // TQ
    nkb = s // tk
    wq = N_HEADS * dqk
    bias = ck2 is not None

    merged = kv_into is not None
    n_in = 6 + bias + merged
    n_out = 3 + 2 * bias - merged

    def body(*refs):
        ins, outs, _, cc = _split_refs(refs, n_in, n_out, comm)
        k_ref, v_ref, qt_ref, dot_ref, l_ref, d_ref = ins[:6]
        cc_ref = ins[6] if bias else None
        dqt_ref = outs[0]
        if merged:
            dk_ref, dv_ref = outs[1].at[:, 0:wq], outs[1].at[:, wq:wq + GROUP]
        else:
            dk_ref, dv_ref = outs[1], outs[2]
        if bias:
            dck_ref, dcq_ref = outs[-2:]
        j = pl.program_id(0)

        @pl.when(j == 0)
        def _():
            if comm is not None:
                comm.start(*cc)
            dqt_ref[...] = jnp.zeros_like(dqt_ref)
            if bias:
                dcq_ref[...] = jnp.zeros_like(dcq_ref)

        ks, kts, vs = [], [], []
        for h in range(N_HEADS):
            k2 = k_ref[:, _head_lanes(h, dqk)]
            if dqk == HEAD_DIM:
                k2 = _keep_half(k2, h % 2, 1)
            ks.append(k2)
            kts.append(k2.astype(F32).T.astype(_MXU))
            vs.append(_keep_half(v_ref[:, _head_lanes(h, HEAD_DIM)], h % 2, 1))
        cks = [cc_ref[:, h:h + 1] for h in range(N_HEADS)] if bias else None

        nt = (((1,), (1,)), ((), ()))

        def step(i, carry, off):
            qti, doti, li, di = qt_ref[i], dot_ref[i], l_ref[i], d_ref[i]
            qls = [_head_lanes(h, dqk) for h in range(N_HEADS)]
            vls = [_head_lanes(h, HEAD_DIM) for h in range(N_HEADS)]
            sts, dpts = [], []
            for h in range(N_HEADS):
                sts.append(jnp.dot(ks[h], qti[qls[h], :], preferred_element_type=F32))
                dpts.append(jnp.dot(vs[h], doti[vls[h], :], preferred_element_type=F32))
            pbs, dsbs, dcks = [], [], []
            for h in range(N_HEADS):
                st = sts[h] - li[h:h + 1, :]
                if bias:
                    st = st - cks[h]
                p = jnp.exp2(st)
                if off is not None:
                    p = jnp.where(_diag_mask(p.shape, off), p, 0.0)
                dst = p * (dpts[h] - di[h:h + 1, :])
                pbs.append(p.astype(_MXU))
                dsbs.append(dst.astype(_MXU))
                if bias:
                    dcks.append(carry[3 * h + 2] + jnp.sum(dst, axis=1, keepdims=True))
                    dcq_ref[i, h:h + 1, :] += jnp.sum(dst, axis=0, keepdims=True)
                else:
                    dcks.append(carry[3 * h + 2])
            out = []
            for h in range(N_HEADS):
                dvt = carry[3 * h + 1] + lax.dot_general(doti[HEAD_DIM * h:HEAD_DIM * (h + 1), :], pbs[h], nt,
                                                         preferred_element_type=F32)
                dkt = carry[3 * h] + lax.dot_general(qti[dqk * h:dqk * (h + 1), :], dsbs[h], nt, preferred_element_type=F32)
                dqt_ref[i, qls[h], :] += jnp.dot(kts[h], dsbs[h], preferred_element_type=F32) * scale
                out += [dkt, dvt, dcks[h]]
            return tuple(out)

        carry = (jnp.zeros((dqk, tk), F32), jnp.zeros((HEAD_DIM, tk), F32), jnp.zeros((tk, 1), F32)) * N_HEADS
        for r in range(ratio):
            carry = step(ratio * j + r, carry, TQ * r)
        carry = lax.fori_loop(ratio * (j + 1), nq, functools.partial(step, off=None), carry)
        for p in range(2):
            dv_ref[:, 128 * p:128 * (p + 1)] = jnp.concatenate([carry[6 * p + 1], carry[6 * p + 4]], axis=0).T.astype(dv_ref.dtype)
            if dqk == HEAD_DIM:
                dk_ref[:, 128 * p:128 * (p + 1)] = (jnp.concatenate([carry[6 * p], carry[6 * p + 3]], axis=0).T
                                                    * (1.0 / LOG2E)).astype(dk_ref.dtype)
        if dqk != HEAD_DIM:
            for h in range(N_HEADS):
                dk_ref[:, 128 * h:128 * (h + 1)] = (carry[3 * h].T * (1.0 / LOG2E)).astype(dk_ref.dtype)
        if bias:
            dck_ref[...] = jnp.zeros_like(dck_ref)
            for h in range(N_HEADS):
                dck_ref[:, h:h + 1] = -carry[3 * h + 2]
        if comm is not None:
            @pl.when(j == nkb - 1)
            def _():
                comm.wait(*cc)

    blk = lambda w: pl.BlockSpec((tk, w), lambda j: (j, 0))
    full3 = lambda w: pl.BlockSpec((nq, w, TQ), lambda j: (0, 0, 0))
    in_specs = [blk(wq), blk(GROUP), full3(wq), full3(GROUP), full3(8), full3(8)]
    args = [kb, vb, qt, dot, lse, dl]
    if merged:
        assert wq == GROUP
        out_specs = [full3(wq), pl.BlockSpec((tk, wq + GROUP), lambda j: (j, _DZ_FOX_KV))]
        out_shape = [SDS((nq, wq, TQ), F32), SDS(kv_into.shape, kv_into.dtype)]
    else:
        out_specs = [full3(wq), blk(wq), blk(GROUP)]
        out_shape = [SDS((nq, wq, TQ), F32), SDS((s, wq), kv_dtype), SDS((s, GROUP), kv_dtype)]
    if bias:
        in_specs.append(blk(128))
        args.append(ck2)
        out_specs += [blk(128), full3(8)]
        out_shape += [SDS((s, 128), F32), SDS((nq, 8, TQ), F32)]
    aliases = {}
    if merged:
        in_specs.append(pl.BlockSpec(memory_space=pl.ANY))
        args.append(kv_into)
        aliases = {len(args) - 1: 1}
    return _call_with_comm(body, (nkb,), in_specs, out_specs, out_shape, [], args, comm, ("arbitrary",), name, aliases)


def _untranspose(xt, dtype, name, into=None, col=0):
    nq, w, _ = xt.shape
    if into is not None:
        def body_into(x_ref, _, o_ref):
            o_ref[...] = x_ref[0].T.astype(o_ref.dtype)

        return pl.pallas_call(
            body_into, grid=(nq,),
            in_specs=[pl.BlockSpec((1, w, TQ), lambda i: (i, 0, 0)), pl.BlockSpec(memory_space=pl.ANY)],
            out_specs=pl.BlockSpec((TQ, w), lambda i: (i, col)), out_shape=SDS(into.shape, into.dtype),
            input_output_aliases={1: 0}, compiler_params=_cp("parallel"), name=name)(xt, into)

    def body(x_ref, o_ref):
        o_ref[...] = x_ref[0].T.astype(o_ref.dtype)

    return pl.pallas_call(
        body, grid=(nq,), in_specs=[pl.BlockSpec((1, w, TQ), lambda i: (i, 0, 0))],
        out_specs=pl.BlockSpec((TQ, w), lambda i: (i, 0)), out_shape=SDS((nq * TQ, w), dtype),
        compiler_params=_cp("parallel"), name=name)(xt)


_SCALE_D = (64 + 32) ** -0.5
_COL_CQ, _COL_CKV, _COL_MISC = 2304 // 256, 2560 // 128, 2688 // 128


def _mla_prep(z, gq, gkv, wq, wk, wv, tb, name):
    s = z.shape[0]
    tm = _key_block(s)
    row = lambda w, c: pl.BlockSpec((tm, w), lambda i, c=c: (i, c))
    const = lambda a: pl.BlockSpec(a.shape, lambda i: (0,) * a.ndim)

    def body(cq_ref, ckv_ref, m_ref, gq_ref, gkv_ref, wq_ref, wk_ref, wv_ref, e_ref, qc_ref, qs_ref, kc_ref, ks_ref,
             q_ref, k_ref, v_ref, vt_ref, cqn_ref, ckvn_ref, qt_ref):
        cqn = _rms(cq_ref[...], gq_ref[...]).astype(_MXU)
        ckvn = _rms(ckv_ref[...], gkv_ref[...]).astype(_MXU)
        cqn_ref[...] = cqn
        ckvn_ref[...] = ckvn
        qb = _rope(_dot(cqn, wq_ref[...]), _lanes(qc_ref[...], 512), _lanes(qs_ref[...], 512), 16).astype(q_ref.dtype)
        q_ref[...] = qb
        for b, t in enumerate(_scaled_qt(qb, _SCALE_D)):
            qt_ref[b] = t
        kr = _rope(m_ref[...], kc_ref[...], ks_ref[...], 16)
        k_ref[...] = (_dot(ckvn, wk_ref[...]) + _dot(kr, e_ref[...])).astype(k_ref.dtype)
        v = _dot(ckvn, wv_ref[...])
        v_ref[...] = v.astype(v_ref.dtype)
        vt_ref[0] = v.T.astype(vt_ref.dtype)

    e = tb["place"]
    return pl.pallas_call(
        body, grid=(s // tm,),
        in_specs=[row(256, _COL_CQ), row(128, _COL_CKV), row(128, _COL_MISC), const(gq), const(gkv), const(wq), const(wk),
                  const(wv), const(e), row(128, 0), row(128, 0), row(128, 0), row(128, 0)],
        out_specs=[row(512, 0), row(512, 0), row(256, 0), pl.BlockSpec((1, GROUP, tm), lambda i: (i, 0, 0)), row(256, 0),
                   row(128, 0), pl.BlockSpec((tm // TQ, 512, TQ), lambda i: (i, 0, 0))],
        out_shape=[SDS((s, 512), _MXU), SDS((s, 512), _MXU), SDS((s, 256), _MXU), SDS((s // tm, GROUP, tm), _MXU),
                   SDS((s, 256), _MXU), SDS((s, 128), _MXU), SDS((s // TQ, 512, TQ), _MXU)],
        compiler_params=_cp("parallel"), name=name)(
            z, z, z, gq, gkv, wq, wk, wv, e, tb["q_cos"], tb["q_sin"], tb["k_cos"], tb["k_sin"])


def _mla_prep_bwd(dqt, dk, dv, z, cqn, ckvn, gq, gkv, wq, wk, wv, tb, dz, name):
    s = z.shape[0]
    tm = min(512, s)
    row = lambda w, c: pl.BlockSpec((tm, w), lambda i, c=c: (i, c))
    const = lambda a: pl.BlockSpec(a.shape, lambda i: (0,) * a.ndim)
    acc = lambda shape: pl.BlockSpec(shape, lambda i: (0, 0))

    def body(dq_ref, dk_ref, dv_ref, cq_ref, ckv_ref, cqn_ref, ckvn_ref, gq_ref, gkv_ref, wq_ref, wk_ref, wv_ref, e_ref,
             qc_ref, qs_ref, kc_ref, ks_ref, _, dz_ref, dkr_ref, dwq_ref, dwk_ref, dwv_ref, dgq_ref, dgkv_ref):
        dcq_ref, dckv_ref = dz_ref.at[:, 0:256], dz_ref.at[:, 256:384]

        @pl.when(pl.program_id(0) == 0)
        def _():
            for r in (dwq_ref, dwk_ref, dwv_ref, dgq_ref, dgkv_ref):
                r[...] = jnp.zeros_like(r)

        dq = jnp.concatenate([dq_ref[b].T for b in range(tm // TQ)], axis=0)
        dqp = _rope_bwd(dq, _lanes(qc_ref[...], 512), _lanes(qs_ref[...], 512), 16)
        dkd = dk_ref[...]
        dvd = dv_ref[...]
        dwq_ref[...] += _dot_tn(cqn_ref[...], dqp)
        dwk_ref[...] += _dot_tn(ckvn_ref[...], dkd)
        dwv_ref[...] += _dot_tn(ckvn_ref[...], dvd)
        dcq, dgq = _rms_bwd(cq_ref[...], gq_ref[...], _dot_nt(dqp, wq_ref[...]))
        dckv, dgkv = _rms_bwd(ckv_ref[...], gkv_ref[...], _dot_nt(dkd, wk_ref[...]) + _dot_nt(dvd, wv_ref[...]))
        dcq_ref[...] = dcq.astype(dcq_ref.dtype)
        dckv_ref[...] = dckv.astype(dckv_ref.dtype)
        dgq_ref[...] += dgq
        dgkv_ref[...] += dgkv
        dkr = _dot_exact(dkd, e_ref[...], (((1,), (1,)), ((), ())))
        dkr_ref[...] = _rope_bwd(dkr, kc_ref[...], ks_ref[...], 16)

    e = tb["place"]
    return pl.pallas_call(
        body, grid=(s // tm,),
        in_specs=[pl.BlockSpec((tm // TQ, 512, TQ), lambda i: (i, 0, 0)), row(512, 0), row(256, 0), row(256, _COL_CQ),
                  row(128, _COL_CKV), row(256, 0), row(128, 0),
                  const(gq), const(gkv), const(wq), const(wk), const(wv), const(e), row(128, 0), row(128, 0), row(128, 0), row(128, 0),
                  pl.BlockSpec(memory_space=pl.ANY)],
        out_specs=[row(384, _DZ_MLA), row(128, 0), acc((256, 512)), acc((128, 512)), acc((128, 256)), acc((1, 256)),
                   acc((1, 128))],
        out_shape=[SDS(dz.shape, dz.dtype), SDS((s, 128), F32), SDS((256, 512), F32), SDS((128, 512), F32),
                   SDS((128, 256), F32), SDS((1, 256), F32), SDS((1, 128), F32)],
        input_output_aliases={17: 0}, compiler_params=_cp("arbitrary"), name=name)(
            dqt, dk, dv, z, z, cqn, ckvn, gq, gkv, wq, wk, wv, e, tb["q_cos"], tb["q_sin"], tb["k_cos"], tb["k_sin"], dz)


def _out_proj(ys, g, w, x, name):
    s, d = x.shape
    tm = min(512, s)

    def body(ya, yb, yc, yd, g_ref, w_ref, x_ref, o_ref, yn_ref):
        acc = x_ref[...]
        for i, y_ref in enumerate((ya, yb, yc, yd)):
            sl = slice(GROUP * i, GROUP * (i + 1))
            yn = _rms(y_ref[...], g_ref[:, sl]).astype(_MXU)
            yn_ref[:, sl] = yn
            acc = acc + jnp.dot(yn, w_ref[sl, :], preferred_element_type=F32)
        o_ref[...] = acc

    yspec = pl.BlockSpec((tm, GROUP), lambda i: (i, 0))
    return pl.pallas_call(
        body, grid=(s // tm,),
        in_specs=[yspec, yspec, yspec, yspec, pl.BlockSpec((1, d), lambda i: (0, 0)), pl.BlockSpec((d, d), lambda i: (0, 0)),
                  pl.BlockSpec((tm, d), lambda i: (i, 0))],
        out_specs=[pl.BlockSpec((tm, d), lambda i: (i, 0)), pl.BlockSpec((tm, d), lambda i: (i, 0))],
        out_shape=[SDS((s, d), F32), SDS((s, d), _MXU)], compiler_params=_cp("parallel"), name=name)(*ys, g, w, x)


def _out_proj_bwd(dx, w, ys, g, name):
    s, d = dx.shape
    tm = min(512, s)
    nb = tm // TQ

    def body(dx_ref, w_ref, ya, yb, yc, yd, g_ref, da, db, dg_ref, dtc_ref, dtd_ref, dlc_ref, dld_ref):
        @pl.when(pl.program_id(0) == 0)
        def _():
            dg_ref[...] = jnp.zeros_like(dg_ref)

        dyn = _dot_nt(dx_ref[...], w_ref[...])
        for i, y_ref in enumerate((ya, yb, yc, yd)):
            sl = slice(GROUP * i, GROUP * (i + 1))
            y = y_ref[...]
            dy, dg = _rms_bwd(y, g_ref[:, sl], dyn[:, sl])
            dg_ref[:, sl] += dg
            if i < 2:
                (da, db)[i][...] = dy
                continue
            dt_ref, dl_ref = ((dtc_ref, dlc_ref), (dtd_ref, dld_ref))[i - 2]
            dl_ref[...] = jnp.zeros_like(dl_ref)
            for b in range(nb):
                r = slice(TQ * b, TQ * (b + 1))
                dt_ref[b] = dy[r].T.astype(dt_ref.dtype)
                pt = (dy[r] * y[r]).T
                for h in range(N_HEADS):
                    dl_ref[b, h:h + 1, :] = jnp.sum(pt[HEAD_DIM * h:HEAD_DIM * (h + 1), :], axis=0, keepdims=True)

    yspec = pl.BlockSpec((tm, GROUP), lambda i: (i, 0))
    tspec = pl.BlockSpec((nb, GROUP, TQ), lambda i: (i, 0, 0))
    lspec = pl.BlockSpec((nb, 8, TQ), lambda i: (i, 0, 0))
    return pl.pallas_call(
        body, grid=(s // tm,),
        in_specs=[pl.BlockSpec((tm, d), lambda i: (i, 0)), pl.BlockSpec((d, d), lambda i: (0, 0)), yspec, yspec, yspec, yspec,
                  pl.BlockSpec((1, d), lambda i: (0, 0))],
        out_specs=[yspec, yspec, pl.BlockSpec((1, d), lambda i: (0, 0)), tspec, tspec, lspec, lspec],
        out_shape=[SDS((s, GROUP), F32)] * 2 + [SDS((1, d), F32)] + [SDS((s // TQ, GROUP, TQ), _MXU)] * 2
        + [SDS((s // TQ, 8, TQ), F32)] * 2,
        compiler_params=_cp("arbitrary"), name=name)(dx, w, *ys, g)


FF_BLOCK = 512
FF_ROWS = 1024


def _ffn_fwd(x, g, wu, wd, name, comm=None):
    s, d = x.shape
    nj = wu.shape[0]
    tm = min(FF_ROWS, s)
    ni = s // tm

    def body(*refs):
        (x_ref, g_ref, wu_ref, wd_ref), (o_ref, u_ref, h_ref), (acc,), cc = _split_refs(refs, 4, 3, comm)
        i, j = pl.program_id(0), pl.program_id(1)
        _host_gather(comm, cc, i * nj + j, ni * nj, late=False)

        @pl.when(j == 0)
        def _():
            h_ref[...] = _rms(x_ref[...], g_ref[...]).astype(h_ref.dtype)
            acc[...] = jnp.zeros_like(acc)

        halves = [slice(r, r + tm // 2) for r in range(0, tm, tm // 2)]
        us = [jnp.dot(h_ref[r, :], wu_ref[0], preferred_element_type=F32) for r in halves]
        for r, u in zip(halves, us):
            u_ref[r, :] = u.astype(u_ref.dtype)
            acc[r, :] += _dot(jnp.square(jnp.maximum(u, 0.0)), wd_ref[...])

        @pl.when(j == nj - 1)
        def _():
            o_ref[...] = x_ref[...] + acc[...]

        if comm is not None:
            @pl.when((i == ni - 1) & (j == nj - 1))
            def _():
                comm.wait(*cc)

    in_specs = [pl.BlockSpec((tm, d), lambda i, j: (i, 0)), pl.BlockSpec((1, d), lambda i, j: (0, 0)),
                pl.BlockSpec((1, d, FF_BLOCK), lambda i, j: (j, 0, 0)), pl.BlockSpec((FF_BLOCK, d), lambda i, j: (j, 0))]
    out_specs = [pl.BlockSpec((tm, d), lambda i, j: (i, 0)), pl.BlockSpec((tm, FF_BLOCK), lambda i, j: (i, j)),
                 pl.BlockSpec((tm, d), lambda i, j: (i, 0))]
    out_shape = [SDS((s, d), F32), SDS((s, nj * FF_BLOCK), _MXU), SDS((s, d), _MXU)]
    return _call_with_comm(body, (ni, nj), in_specs, out_specs, out_shape, [pltpu.VMEM((tm, d), F32)], [x, g, wu, wd], comm,
                           ("arbitrary", "arbitrary"), name)


def _ffn_bwd(dx2, x, u, g, wu, wd, name, comm=None):
    s, d = x.shape
    nj = wu.shape[0]
    tm = min(FF_ROWS, s)
    ni = s // tm

    def body(*refs):
        (dx_ref, x_ref, u_ref, g_ref, wu_ref, wd_ref), (o_ref, du_ref, dg_ref), (acc, dxb), cc = _split_refs(refs, 6, 3, comm)
        i, j = pl.program_id(0), pl.program_id(1)

        @pl.when((i == 0) & (j == 0))
        def _():
            if comm is not None:
                comm.start(*cc)
            dg_ref[...] = jnp.zeros_like(dg_ref)

        @pl.when(j == 0)
        def _():
            dxb[...] = dx_ref[...].astype(dxb.dtype)
            acc[...] = jnp.zeros_like(acc)

        nt = (((1,), (1,)), ((), ()))
        halves = [slice(r, r + tm // 2) for r in range(0, tm, tm // 2)]
        das = [lax.dot_general(dxb[r, :], wd_ref[...], nt, preferred_element_type=F32) for r in halves]
        for r, da in zip(halves, das):
            du = (da * 2.0 * jnp.maximum(u_ref[r, :].astype(F32), 0.0)).astype(du_ref.dtype)
            du_ref[r, :] = du
            acc[r, :] += lax.dot_general(du, wu_ref[0], nt, preferred_element_type=F32)

        @pl.when(j == nj - 1)
        def _():
            dxn, dg = _rms_bwd(x_ref[...], g_ref[...], acc[...])
            o_ref[...] = dx_ref[...] + dxn
            dg_ref[...] += dg

        if comm is not None:
            @pl.when((i == ni - 1) & (j == nj - 1))
            def _():
                comm.wait(*cc)

    in_specs = [pl.BlockSpec((tm, d), lambda i, j: (i, 0)), pl.BlockSpec((tm, d), lambda i, j: (i, 0)),
                pl.BlockSpec((tm, FF_BLOCK), lambda i, j: (i, j)), pl.BlockSpec((1, d), lambda i, j: (0, 0)),
                pl.BlockSpec((1, d, FF_BLOCK), lambda i, j: (j, 0, 0)), pl.BlockSpec((FF_BLOCK, d), lambda i, j: (j, 0))]
    out_specs = [pl.BlockSpec((tm, d), lambda i, j: (i, 0)), pl.BlockSpec((tm, FF_BLOCK), lambda i, j: (i, j)),
                 pl.BlockSpec((1, d), lambda i, j: (0, 0))]
    out_shape = [SDS((s, d), F32), SDS((s, nj * FF_BLOCK), _MXU), SDS((1, d), F32)]
    return _call_with_comm(body, (ni, nj), in_specs, out_specs, out_shape,
                           [pltpu.VMEM((tm, d), F32), pltpu.VMEM((tm, d), _MXU)], [dx2, x, u, g, wu, wd], comm,
                           ("arbitrary", "arbitrary"), name)


def _in_proj_bwd(dz, w, x, g, dx_up, name, comm=None):
    s, d = x.shape
    n = w.shape[1]
    tm = min(512, s)
    ni = s // tm

    def body(*refs):
        (dz_ref, w_ref, x_ref, g_ref, up_ref), (o_ref, dg_ref), _, cc = _split_refs(refs, 5, 2, comm)
        i = pl.program_id(0)

        @pl.when(i == 0)
        def _():
            if comm is not None:
                comm.start(*cc)
            dg_ref[...] = jnp.zeros_like(dg_ref)

        dh = lax.dot_general(dz_ref[...], w_ref[...], (((1,), (1,)), ((), ())), preferred_element_type=F32)
        dxn, dg = _rms_bwd(x_ref[...], g_ref[...], dh)
        o_ref[...] = up_ref[...] + dxn
        dg_ref[...] += dg
        if comm is not None:
            @pl.when(i == ni - 1)
            def _():
                comm.wait(*cc)

    in_specs = [pl.BlockSpec((tm, n), lambda i: (i, 0)), pl.BlockSpec((d, n), lambda i: (0, 0)),
                pl.BlockSpec((tm, d), lambda i: (i, 0)), pl.BlockSpec((1, d), lambda i: (0, 0)),
                pl.BlockSpec((tm, d), lambda i: (i, 0))]
    out_specs = [pl.BlockSpec((tm, d), lambda i: (i, 0)), pl.BlockSpec((1, d), lambda i: (0, 0))]
    out_shape = [SDS((s, d), F32), SDS((1, d), F32)]
    return _call_with_comm(body, (ni,), in_specs, out_specs, out_shape, [], [dz, w, x, g, dx_up], comm, ("arbitrary",), name)


def _loss_head(x, g, target, name):
    s, d = x.shape
    tm = min(512, s)

    def body(x_ref, g_ref, t_ref, l_ref, dx_ref, dg_ref):
        @pl.when(pl.program_id(0) == 0)
        def _():
            l_ref[...] = jnp.zeros_like(l_ref)
            dg_ref[...] = jnp.zeros_like(dg_ref)

        xv = x_ref[...]
        err = _rms(xv, g_ref[...]) - t_ref[...]
        l_ref[...] += jnp.sum(err * err, axis=0, keepdims=True) * (0.5 / d)
        dx, dg = _rms_bwd(xv, g_ref[...], err * (1.0 / d))
        dx_ref[...] = dx
        dg_ref[...] += dg

    return pl.pallas_call(
        body, grid=(s // tm,),
        in_specs=[pl.BlockSpec((tm, d), lambda i: (i, 0)), pl.BlockSpec((1, d), lambda i: (0, 0)),
                  pl.BlockSpec((tm, d), lambda i: (i, 0))],
        out_specs=[pl.BlockSpec((1, d), lambda i: (0, 0)), pl.BlockSpec((tm, d), lambda i: (i, 0)),
                   pl.BlockSpec((1, d), lambda i: (0, 0))],
        out_shape=[SDS((1, d), F32), SDS((s, d), F32), SDS((1, d), F32)], compiler_params=_cp("arbitrary"), name=name)(x, g, target)


def _me_and_peer():
    x, y, c = lax.axis_index("x"), lax.axis_index("y"), lax.axis_index("c")
    me = 4 * x + 2 * y + c

    def peer(k):
        px, py, pc = x ^ (k >> 2), y ^ ((k >> 1) & 1), c ^ (k & 1)
        return (px, py, pc), 4 * px + 2 * py + pc

    return me, peer


class _Comm:
    CHIPS = (2, 4, 6)

    def __init__(self, kind, arrs):
        assert kind in ("gather", "exchange")
        self.kind, self.arrs, self.n = kind, list(arrs), len(arrs)
        anyspec = pl.BlockSpec(memory_space=pl.ANY)
        self.in_specs = [anyspec] * self.n
        self.out_specs = [anyspec] * self.n
        self.out_shape = [SDS(((NDEV,) + a.shape) if kind == "gather" else a.shape, a.dtype) for a in self.arrs]
        npair = NDEV - 1 + len(self.CHIPS)
        self.scratch = [pltpu.SemaphoreType.DMA((self.n, npair)), pltpu.SemaphoreType.DMA((self.n, npair)),
                        pltpu.SemaphoreType.DMA((self.n,))]

    def _copies(self, ins, outs, sems):
        send, recv, loc = sems
        me, peer = _me_and_peer()
        gather = self.kind == "gather"
        sibling = peer(1)[0]
        local = [pltpu.make_async_copy(ins[a] if gather else ins[a].at[me], outs[a].at[me], loc.at[a]) for a in range(self.n)]
        outgoing, incoming, forwards, forwarded = [], [], [], []
        for k in ((1,) + self.CHIPS) if gather else range(1, NDEV):
            dev, pid = peer(k)
            for a in range(self.n):
                pair = dict(send_sem=send.at[a, k - 1], recv_sem=recv.at[a, k - 1], device_id=dev, device_id_type=MESH)
                outgoing.append(pltpu.make_async_remote_copy(src_ref=ins[a] if gather else ins[a].at[pid],
                                                             dst_ref=outs[a].at[me], **pair))
                incoming.append(pltpu.make_async_remote_copy(src_ref=ins[a] if gather else ins[a].at[me],
                                                             dst_ref=outs[a].at[pid], **pair))
        if gather:
            for idx, k in enumerate(self.CHIPS):
                got, theirs = peer(k)[1], peer(k + 1)[1]
                for a in range(self.n):
                    pair = dict(send_sem=send.at[a, NDEV - 1 + idx], recv_sem=recv.at[a, NDEV - 1 + idx], device_id=sibling,
                                device_id_type=MESH)
                    forwards.append(pltpu.make_async_remote_copy(src_ref=outs[a].at[got], dst_ref=outs[a].at[got], **pair))
                    forwarded.append(pltpu.make_async_remote_copy(src_ref=outs[a].at[theirs], dst_ref=outs[a].at[theirs], **pair))
        return local, outgoing, incoming, forwards, forwarded

    def start(self, ins, outs, sems):
        local, outgoing, _, _, _ = self._copies(ins, outs, sems)
        for cp in local + outgoing:
            cp.start()

    def forward(self, ins, outs, sems):
        _, _, incoming, forwards, _ = self._copies(ins, outs, sems)
        per = self.n
        for idx in range(len(forwards) // per if per else 0):
            for a in range(per):
                incoming[(1 + idx) * per + a].wait_recv()
                forwards[idx * per + a].start()

    def wait(self, ins, outs, sems):
        local, outgoing, incoming, forwards, forwarded = self._copies(ins, outs, sems)
        for cp in (incoming[:self.n] if self.kind == "gather" else incoming) + forwarded:
            cp.wait_recv()
        for cp in outgoing + forwards:
            cp.wait_send()
        for cp in local:
            cp.wait()


LATE_FORWARD_BYTES = 1 << 20


def _host_gather(comm, cc, step, nsteps, late=None):
    if comm is None:
        return
    if late is None:
        late = sum(a.size * a.dtype.itemsize for a in comm.arrs) >= LATE_FORWARD_BYTES

    @pl.when(step == 0)
    def _():
        comm.start(*cc)

    @pl.when(step == (nsteps - 1 if late else (2 * nsteps) // 3))
    def _():
        comm.forward(*cc)


def _split_refs(refs, n_in, n_out, comm):
    c = comm.n if comm is not None else 0
    ins, cin = refs[:n_in], refs[n_in:n_in + c]
    outs, cout = refs[n_in + c:n_in + c + n_out], refs[n_in + c + n_out:n_in + 2 * c + n_out]
    rest = refs[n_in + 2 * c + n_out:]
    scratch, csem = (rest[:len(rest) - 3], rest[len(rest) - 3:]) if c else (rest, ())
    return ins, outs, scratch, (cin, cout, csem)


def _comm_call(kind, arrs, name):
    comm = _Comm(kind, arrs)

    def body(*refs):
        _, _, _, c = _split_refs(refs, 0, 0, comm)
        comm.start(*c)
        if kind == "gather":
            comm.forward(*c)
        comm.wait(*c)

    return pl.pallas_call(body, in_specs=comm.in_specs, out_specs=comm.out_specs, out_shape=comm.out_shape,
                          scratch_shapes=comm.scratch, compiler_params=pltpu.CompilerParams(has_side_effects=True),
                          name=name)(*arrs)


def _all_gather(arrs, name):
    return _comm_call("gather", arrs, name)


def _exchange(arrs, name):
    return _comm_call("exchange", arrs, name)


def _sum_slots(parts, name):
    _, r, c = parts.shape
    tr = r if r <= 512 else 512

    def body(p_ref, o_ref):
        acc = p_ref[0].astype(F32)
        for q in range(1, NDEV):
            acc = acc + p_ref[q].astype(F32)
        o_ref[...] = acc

    return pl.pallas_call(
        body, grid=(r // tr,), in_specs=[pl.BlockSpec((NDEV, tr, c), lambda i: (0, i, 0))],
        out_specs=pl.BlockSpec((tr, c), lambda i: (i, 0)), out_shape=SDS((r, c), F32),
        compiler_params=_cp("parallel"), name=name)(parts)


def _adamw(g, w, m, v, name):
    r, c = w.shape
    parts = g.ndim == 3
    tr = r
    for cand in (512, 256, 128, 64, 32, 16, 8):
        if r > cand and r % cand == 0 and cand * c * 4 <= 2 * 1024 * 1024:
            tr = cand
            break
    bc1 = 1.0 / (1.0 - ADAM_B1 ** ADAM_STEP)
    bc2 = 1.0 / (1.0 - ADAM_B2 ** ADAM_STEP)

    def body(g_ref, w_ref, m_ref, v_ref, go_ref, d_ref, mo_ref, vo_ref):
        if parts:
            gv = g_ref[0].astype(F32)
            for q in range(1, NDEV):
                gv = gv + g_ref[q].astype(F32)
        else:
            gv = g_ref[...]
        mn = ADAM_B1 * m_ref[...] + (1.0 - ADAM_B1) * gv
        vn = ADAM_B2 * v_ref[...] + (1.0 - ADAM_B2) * (gv * gv)
        go_ref[...] = gv
        mo_ref[...] = mn
        vo_ref[...] = vn
        d_ref[...] = -ADAM_LR * ((mn * bc1) / (jnp.sqrt(vn * bc2) + ADAM_EPS) + ADAM_WD * w_ref[...])

    spec = pl.BlockSpec((tr, c), lambda i: (i, 0))
    gspec = pl.BlockSpec((NDEV, tr, c), lambda i: (0, i, 0)) if parts else spec
    return pl.pallas_call(
        body, grid=(r // tr,), in_specs=[gspec, spec, spec, spec], out_specs=[spec] * 4,
        out_shape=[SDS((r, c), F32)] * 4, compiler_params=_cp("parallel"), name=name)(g, w, m, v)


def _adamw_layer(parts, w, m, v, l, prev, name):
    r, c = parts.shape[1:]
    rows = w.shape[0]
    tr = next(t for t in (512, 256, 128, 64, 32, 16, 8) if r % t == 0 and t * c * 4 <= 2 * 1024 * 1024)
    bc1 = 1.0 / (1.0 - ADAM_B1 ** ADAM_STEP)
    bc2 = 1.0 / (1.0 - ADAM_B2 ** ADAM_STEP)

    def body(g_ref, w_ref, m_ref, v_ref, *rest):
        go_ref, d_ref, mo_ref, vo_ref = rest[-4:]
        gv = g_ref[0].astype(F32)
        for q in range(1, NDEV):
            gv = gv + g_ref[q].astype(F32)
        mn = ADAM_B1 * m_ref[...] + (1.0 - ADAM_B1) * gv
        vn = ADAM_B2 * v_ref[...] + (1.0 - ADAM_B2) * (gv * gv)
        go_ref[...] = gv
        mo_ref[...] = mn
        vo_ref[...] = vn
        d_ref[...] = -ADAM_LR * ((mn * bc1) / (jnp.sqrt(vn * bc2) + ADAM_EPS) + ADAM_WD * w_ref[...])

    spec = pl.BlockSpec((tr, c), lambda i: (l * (r // tr) + i, 0))
    in_specs = [pl.BlockSpec((NDEV, tr, c), lambda i: (0, i, 0)), spec, spec, spec]
    args = [parts, w, m, v]
    aliases = {}
    if prev is not None:
        in_specs += [pl.BlockSpec(memory_space=pl.ANY)] * 4
        args += list(prev)
        aliases = {4 + k: k for k in range(4)}
    return pl.pallas_call(
        body, grid=(r // tr,), in_specs=in_specs, out_specs=[spec] * 4, out_shape=[SDS((rows, c), F32)] * 4,
        input_output_aliases=aliases, compiler_params=_cp("parallel"), name=name)(*args)


def _pad_in_cols(w):
    r = w.shape[0]
    zeros = lambda n: jnp.zeros((r, n), w.dtype)
    return jnp.concatenate([w[:, 512:1536], w[:, 0:512], w[:, 1792:2304], w[:, 1536:1792], w[:, 2308:2692], w[:, 2304:2308],
                            zeros(28), w[:, 2692:2724], zeros(64)], axis=1)


def _unpad_in_cols(w):
    return jnp.concatenate([w[..., 1024:1536], w[..., 0:1024], w[..., 2048:2304], w[..., 1536:2048], w[..., 2688:2692],
                            w[..., 2304:2688], w[..., 2720:2752]], axis=-1)


_Z_RET = (0, 1, 2, 3)
_Z_SGU = (4, 5)
_Z_FOX_Q, _Z_FOX_K, _Z_FOX_V = 8, 6, 7
_DZ_RET, _DZ_SGU, _DZ_FOX_KV, _DZ_FOX_Q, _DZ_MLA, _DZ_MISC = 0, 2, 3, 8, 6, 21


def _pad_uq(w):
    return jnp.pad(w.reshape(256, N_HEADS, 96), ((0, 0), (0, 0), (0, 32))).reshape(256, 512)


def _unpad_uq(w):
    return w.reshape(256, N_HEADS, 128)[:, :, :96].reshape(256, 384)


def _split_ukv(w):
    r = w.reshape(128, N_HEADS, 128)
    return jnp.pad(r[:, :, :64], ((0, 0), (0, 0), (0, 64))).reshape(128, 512), r[:, :, 64:].reshape(128, 256)


def _join_ukv(dk, dv):
    return jnp.concatenate([dk.reshape(128, N_HEADS, 128)[:, :, :64], dv.reshape(128, N_HEADS, 64)], axis=-1).reshape(128, 512)


def _cols_to_full(g):
    return jnp.transpose(g, (1, 0, 2)).reshape(g.shape[1], NDEV * g.shape[2])


def kernel(x, g_mix_norm, w_in, b_forget, g_sgu, w_spatial, b_spatial, g_mla_q, w_uq, g_mla_kv, w_ukv, g_group_out, w_out, g_ffn_norm, w_up, w_down, g_final, loss_target, m_g_mix_norm, m_w_in, m_b_forget, m_g_sgu, m_w_spatial, m_b_spatial, m_g_mla_q, m_w_uq, m_g_mla_kv, m_w_ukv, m_g_group_out, m_w_out, m_g_ffn_norm, m_w_up, m_w_down, m_g_final, v_g_mix_norm, v_w_in, v_b_forget, v_g_sgu, v_w_spatial, v_b_spatial, v_g_mla_q, v_w_uq, v_g_mla_kv, v_w_ukv, v_g_group_out, v_w_out, v_g_ffn_norm, v_w_up, v_w_down, v_g_final):
    depth = w_in.shape[0]
    s, d = x.shape[1], x.shape[2]
    x0 = x.reshape(s, d)
    target = loss_target.reshape(s, d)
    tb = _tables(s)
    me = 4 * lax.axis_index("x") + 2 * lax.axis_index("y") + lax.axis_index("c")

    assert depth == 2
    shards = {}
    for l in range(depth):
        shards.update({(l, "w_in"): _pad_in_cols(w_in[l]).astype(_WIRE), (l, "w_out"): w_out[l].astype(_WIRE),
                       (l, "w_up"): w_up[l].astype(_WIRE), (l, "w_down"): w_down[l].astype(_WIRE),
                       (l, "w_uq"): w_uq[l].astype(_WIRE), (l, "w_ukv"): w_ukv[l].astype(_WIRE)})
    wts = _ShardedWeights(shards)
    first = [(0, "w_in"), (0, "w_uq"), (0, "w_ukv"), (1, "w_uq"), (1, "w_ukv")]
    wts.full.update(zip(first, _all_gather([shards[k] for k in first], "gather_first")))

    row = lambda a: a.reshape(1, -1)

    def small(l):
        bf = jnp.pad(b_forget[l].reshape(1, N_HEADS), ((0, 0), (0, 128 - N_HEADS)))
        bt = jnp.pad(b_spatial[l].T, ((0, 0), (0, 128 - N_HEADS)))
        return dict(g_mix=row(g_mix_norm[l]), g_sgu=row(g_sgu[l]), w_s=w_spatial[l], b_t=bt, b_f=bf, gq=row(g_mla_q[l]),
                    gkv=row(g_mla_kv[l]), g_go=row(g_group_out[l]), g_ffn=row(g_ffn_norm[l]))

    smalls = [small(l) for l in range(depth)]
    lrow, dx, sm, dg_final = _local_step(x0, target, wts, smalls, row(g_final), tb)
    loss = lax.psum(jnp.sum(lrow), AXES)
    grad_x = dx.reshape(1, s, d)
    return _reduce_and_update(loss, grad_x, wts.recv, sm, dg_final, me, dict(
        g_mix_norm=(g_mix_norm, m_g_mix_norm, v_g_mix_norm), w_in=(w_in, m_w_in, v_w_in),
        b_forget=(b_forget, m_b_forget, v_b_forget), g_sgu=(g_sgu, m_g_sgu, v_g_sgu),
        w_spatial=(w_spatial, m_w_spatial, v_w_spatial), b_spatial=(b_spatial, m_b_spatial, v_b_spatial),
        g_mla_q=(g_mla_q, m_g_mla_q, v_g_mla_q), w_uq=(w_uq, m_w_uq, v_w_uq), g_mla_kv=(g_mla_kv, m_g_mla_kv, v_g_mla_kv),
        w_ukv=(w_ukv, m_w_ukv, v_w_ukv), g_group_out=(g_group_out, m_g_group_out, v_g_group_out),
        w_out=(w_out, m_w_out, v_w_out), g_ffn_norm=(g_ffn_norm, m_g_ffn_norm, v_g_ffn_norm), w_up=(w_up, m_w_up, v_w_up),
        w_down=(w_down, m_w_down, v_w_down), g_final=(g_final, m_g_final, v_g_final)))


_GATHER_AT = {
    "in_proj0": [(0, "w_out")],
    "fox_attn0": [(0, "w_down"), (0, "w_up")],
    "mla_attn0": [(1, "w_in")],
    "ffn_fwd0": [(1, "w_down")],
    "fox_attn1": [(1, "w_out")],
    "mla_attn1": [(1, "w_up")],
}
_SCATTER_AT = {
    "fox_attn_bwd1": [(1, "w_down")],
    "mla_attn_bwd1": [(1, "w_up"), (1, "w_out")],
    "ffn_bwd0": [(1, "w_in")],
    "fox_attn_bwd0": [(0, "w_down")],
    "mla_attn_bwd0": [(0, "w_up"), (0, "w_out")],
    "in_proj_bwd0": [(0, "w_in")],
}


class _FullWeights:
    def __init__(self, per_layer):
        self.per_layer, self.grads = per_layer, {}

    def get(self, l, name):
        return self.per_layer[l][name]

    def comm(self, host):
        return None

    def done(self, host, results):
        pass

    def grad(self, l, name, blocks):
        self.grads[(l, name)] = blocks


class _ShardedWeights(_FullWeights):
    def __init__(self, shards):
        self.shards, self.full, self.grads, self.recv = shards, {}, {}, {}

    def get(self, l, name):
        if name in ("wk", "wv"):
            return _split_ukv(_cols_to_full(self.full[(l, "w_ukv")]))[0 if name == "wk" else 1]
        if name == "wq":
            return _pad_uq(_cols_to_full(self.full[(l, "w_uq")]))
        g = self.full[(l, name)]
        return g if name == "w_up" else g.reshape(NDEV * g.shape[1], g.shape[2])

    def comm(self, host):
        if host in _GATHER_AT:
            return _Comm("gather", [self.shards[k] for k in _GATHER_AT[host]])
        if host in _SCATTER_AT:
            return _Comm("exchange", [self.grads[k] for k in _SCATTER_AT[host]])
        return None

    def done(self, host, results):
        if host in _GATHER_AT:
            self.full.update(zip(_GATHER_AT[host], results))
        if host in _SCATTER_AT:
            self.recv.update(zip(_SCATTER_AT[host], results))


def _local_step(x0, target, wts, smalls, g_final, tb):
    depth = len(smalls)
    s, d = x0.shape
    saved = []
    xl = x0
    for l in range(depth):
        p = smalls[l]
        z, h, got = _norm_matmul(xl, p["g_mix"], wts.get(l, "w_in"), f"in_proj{l}", wts.comm(f"in_proj{l}"))
        wts.done(f"in_proj{l}", got)
        ya = _sgu_fwd(z, p["g_sgu"], p["w_s"], p["b_t"], tb, f"sgu_fwd{l}")
        yb, ret, states = _ret_fwd(z, tb, f"ret_fwd{l}")
        cum = _fox_prep(z, p["b_f"], f"fox_prep{l}")
        kc, vc, vtc, qtc = _kv_prep(z, _Z_FOX_Q, _Z_FOX_K, _Z_FOX_V, HEAD_DIM ** -0.5, f"fox_kv{l}")
        yc, lse_c, got = _attn_fwd(z, _Z_FOX_Q, HEAD_DIM, kc, vtc, HEAD_DIM ** -0.5, cum, f"fox_attn{l}", wts.comm(f"fox_attn{l}"))
        wts.done(f"fox_attn{l}", got)
        wq, wk, wv = wts.get(l, "wq"), wts.get(l, "wk"), wts.get(l, "wv")
        qd, kd, vd, vtd, cqn, ckvn, qtd = _mla_prep(z, p["gq"], p["gkv"], wq, wk, wv, tb, f"mla_prep{l}")
        yd, lse_d, got = _attn_fwd(qd, 0, 128, kd, vtd, _SCALE_D, None, f"mla_attn{l}", wts.comm(f"mla_attn{l}"))
        wts.done(f"mla_attn{l}", got)
        ys = (ya, yb, yc, yd)
        x1, yn = _out_proj(ys, p["g_go"], wts.get(l, "w_out"), xl, f"out_proj{l}")
        x2, u, h2, got = _ffn_fwd(x1, p["g_ffn"], wts.get(l, "w_up"), wts.get(l, "w_down"), f"ffn_fwd{l}", wts.comm(f"ffn_fwd{l}"))
        wts.done(f"ffn_fwd{l}", got)
        saved.append(dict(x=xl, z=z, h=h, ys=ys, ret=ret, states=states, cum=cum, lse_c=lse_c, kc=kc, vc=vc, qd=qd, kd=kd, vd=vd,
                          cqn=cqn, ckvn=ckvn, lse_d=lse_d, x1=x1, yn=yn, u=u, h2=h2, wq=wq, wk=wk, wv=wv, qtc=qtc, qtd=qtd))
        xl = x2

    lrow, dx, dg_final = _loss_head(xl, g_final, target, "loss_head")

    sm = [None] * depth
    for l in reversed(range(depth)):
        p, a = smalls[l], saved[l]
        dx1, du, dg_ffn, got = _ffn_bwd(dx, a["x1"], a["u"], p["g_ffn"], wts.get(l, "w_up"), wts.get(l, "w_down"), f"ffn_bwd{l}",
                                        wts.comm(f"ffn_bwd{l}"))
        wts.done(f"ffn_bwd{l}", got)
        dw_down = _mm_tn(a["u"], dx, f"dw_down{l}", a_fn=lambda t: jnp.square(jnp.maximum(t, 0.0)), out_dtype=_WIRE)
        wts.grad(l, "w_down", dw_down.reshape(NDEV, dw_down.shape[0] // NDEV, d))
        wts.grad(l, "w_up", _mm_tn(a["h2"], du, f"dw_up{l}", blocked=True, out_dtype=_WIRE))
        dya, dyb, dg_go, dot_c, dot_d, dl_c, dl_d = _out_proj_bwd(dx1, wts.get(l, "w_out"), a["ys"], p["g_go"],
                                                                  f"out_proj_bwd{l}")
        wts.grad(l, "w_out", _mm_tn(a["yn"], dx1, f"dw_out{l}", out_dtype=_WIRE).reshape(NDEV, d // NDEV, d))
        dz = _ret_bwd(dyb, a["z"], a["ret"], a["states"], tb, f"ret_bwd{l}")
        dz, dg_sgu, dw_s, db_t = _sgu_bwd(dya, a["z"], p["g_sgu"], p["w_s"], p["b_t"], tb, dz, f"sgu_bwd{l}")
        dqt_c, dz, dck, dcq, got = _attn_bwd(a["kc"], a["vc"], a["qtc"], dot_c, a["lse_c"], dl_c, HEAD_DIM,
                                             HEAD_DIM ** -0.5, a["cum"], f"fox_attn_bwd{l}", _MXU,
                                             wts.comm(f"fox_attn_bwd{l}"), kv_into=dz)
        wts.done(f"fox_attn_bwd{l}", got)
        dz = _untranspose(dqt_c, _MXU, f"fox_dq{l}", into=dz, col=_DZ_FOX_Q)
        dqt_d, dk_d, dv_d, got = _attn_bwd(a["kd"], a["vd"], a["qtd"], dot_d, a["lse_d"], dl_d, 128, _SCALE_D, None,
                                           f"mla_attn_bwd{l}", F32, wts.comm(f"mla_attn_bwd{l}"))
        wts.done(f"mla_attn_bwd{l}", got)
        dz, dkr, dwq, dwk, dwv, dgq, dgkv = _mla_prep_bwd(dqt_d, dk_d, dv_d, a["z"], a["cqn"], a["ckvn"], p["gq"], p["gkv"],
                                                          a["wq"], a["wk"], a["wv"], tb, dz, f"mla_prep_bwd{l}")
        dz, db_f = _fox_post(dcq, dck, a["z"], p["b_f"], dkr, dz, f"fox_post{l}")
        wts.grad(l, "w_in", _unpad_in_cols(_mm_tn(a["h"], dz, f"dw_in{l}", out_dtype=_WIRE)).reshape(NDEV, d // NDEV, N_IN))
        dx, dg_mix, got = _in_proj_bwd(dz, wts.get(l, "w_in"), a["x"], p["g_mix"], dx1, f"in_proj_bwd{l}",
                                       wts.comm(f"in_proj_bwd{l}"))
        wts.done(f"in_proj_bwd{l}", got)
        sm[l] = [dg_mix, dg_go, dg_ffn, dg_sgu, dw_s, db_t[:, :N_HEADS].T, db_f[0, :N_HEADS], dgq, dgkv, _unpad_uq(dwq),
                 _join_ukv(dwk, dwv)]
    return lrow, dx, sm, dg_final


def _reduce_and_update(loss, grad_x, recv, sm, dg_final, me, given):
    depth = len(sm)
    pieces = [t for l in range(depth) for t in sm[l]] + [dg_final]
    flat = jnp.concatenate([t.reshape(-1) for t in pieces])
    n_flat = flat.shape[0]
    unit = NDEV * 8 * 128
    n_pad = -(-n_flat // unit) * unit
    packed = jnp.pad(flat, (0, n_pad - n_flat)).reshape(NDEV, n_pad // (NDEV * 128), 128)
    red = _sum_slots(_exchange([packed], "scatter_small")[0], "sum_small")
    full = _all_gather([red], "gather_small")[0].reshape(-1)
    offs = np.cumsum([0] + [int(np.prod(t.shape)) for t in pieces])
    red_pieces = [full[int(offs[i]):int(offs[i + 1])].reshape(pieces[i].shape) for i in range(len(pieces))]
    per = len(sm[0])
    stack = lambda i: jnp.stack([red_pieces[l * per + i] for l in range(depth)])
    g_small = dict(g_mix_norm=stack(0), g_group_out=stack(1), g_ffn_norm=stack(2), g_sgu=stack(3), w_spatial=stack(4),
                   b_spatial=stack(5), b_forget=stack(6), g_mla_q=stack(7), g_mla_kv=stack(8), g_final=red_pieces[-1])
    cq, ckv = given["w_uq"][0].shape[2], given["w_ukv"][0].shape[2]
    g_small["w_uq"] = lax.dynamic_slice_in_dim(stack(9), me * cq, cq, axis=2)
    g_small["w_ukv"] = lax.dynamic_slice_in_dim(stack(10), me * ckv, ckv, axis=2)

    names = list(given)
    outs = {}
    for nme in names:
        wv_, mv_, vv_ = given[nme]
        shape = wv_.shape
        if nme in ("w_in", "w_out", "w_up", "w_down"):
            res = None
            flat2 = lambda t: t.reshape(-1, shape[-1])
            for l in range(depth):
                res = _adamw_layer(recv[(l, nme)], flat2(wv_), flat2(mv_), flat2(vv_), l, res, f"adamw_{nme}{l}")
            outs[nme] = [t.reshape(shape) for t in res]
        else:
            two = lambda t: t.reshape(-1, shape[-1]) if t.ndim > 1 else t.reshape(1, -1)
            res = _adamw(two(g_small[nme]), two(wv_), two(mv_), two(vv_), f"adamw_{nme}")
            outs[nme] = [r.reshape(shape) for r in res]
    return (loss, grad_x, *[outs[n][0] for n in names], *[outs[n][1] for n in names], *[outs[n][2] for n in names],
            *[outs[n][3] for n in names])
```

```python
import functools

import jax
import jax.numpy as jnp
import numpy as np
from jax import lax
from jax.experimental import pallas as pl
from jax.experimental.pallas import tpu as pltpu

F32 = jnp.float32
_MXU = jnp.bfloat16
_WIRE = jnp.bfloat16
EPS = 1e-6
NDEV = 8
AXES = ("x", "y", "c")
MESH = pl.DeviceIdType.MESH

N_HEADS = 4
HEAD_DIM = 64
GROUP = 256
CHUNK = 128
NZ = 2816
N_IN = 2724
MISC_F, MISC_KR = 0, 32
VMEM_LIMIT = 56 * 1024 * 1024

ADAM_LR, ADAM_B1, ADAM_B2, ADAM_EPS, ADAM_WD, ADAM_STEP = 0.001, 0.9, 0.999, 1e-08, 0.01, 10

SDS = jax.ShapeDtypeStruct


def _cp(*sem):
    return pltpu.CompilerParams(dimension_semantics=sem, vmem_limit_bytes=VMEM_LIMIT)


def _dot(a, b):
    return jnp.dot(a.astype(_MXU), b.astype(_MXU), preferred_element_type=F32)


def _dot_nt(a, b):
    return lax.dot_general(a.astype(_MXU), b.astype(_MXU), (((1,), (1,)), ((), ())), preferred_element_type=F32)


def _dot_tn(a, b):
    return lax.dot_general(a.astype(_MXU), b.astype(_MXU), (((0,), (0,)), ((), ())), preferred_element_type=F32)


def _dot_exact(a, b, dims=(((1,), (0,)), ((), ()))):
    return lax.dot_general(a, b, dims, precision=lax.Precision.HIGHEST, preferred_element_type=F32)


def _rms(x, g):
    return x * lax.rsqrt(jnp.mean(x * x, axis=-1, keepdims=True) + EPS) * g


def _rms_bwd(x, g, dy):
    xh = x * lax.rsqrt(jnp.mean(x * x, axis=-1, keepdims=True) + EPS)
    dxh = dy * g
    r = lax.rsqrt(jnp.mean(x * x, axis=-1, keepdims=True) + EPS)
    dx = r * (dxh - xh * jnp.mean(dxh * xh, axis=-1, keepdims=True))
    return dx, jnp.sum(dy * xh, axis=0, keepdims=True)


_GELU_C = 0.7978845608028654


def _gelu(x):
    return 0.5 * x * (1.0 + jnp.tanh(_GELU_C * (x + 0.044715 * x * x * x)))


def _gelu_grad(x):
    t = jnp.tanh(_GELU_C * (x + 0.044715 * x * x * x))
    return 0.5 * (1.0 + t) + 0.5 * x * (1.0 - t * t) * _GELU_C * (1.0 + 3 * 0.044715 * x * x)


def _sigmoid(x):
    return 1.0 / (1.0 + jnp.exp(-x))


def _swap_half(t, half):
    n = t.shape[-1]
    lane = lax.broadcasted_iota(jnp.int32, t.shape, t.ndim - 1)
    return jnp.where((lane % (2 * half)) < half, pltpu.roll(t, n - half, t.ndim - 1), pltpu.roll(t, half, t.ndim - 1))


def _lanes(table, width):
    return jnp.concatenate([table] * (width // table.shape[-1]), axis=-1)


def _rope(t, cos, sin, half):
    return t * cos + _swap_half(t, half) * sin


def _rope_bwd(d, cos, sin, half):
    return d * cos - _swap_half(d, half) * sin


def _tables(s):
    pos = jnp.arange(s, dtype=F32)[:, None]

    def cs(half):
        inv = jnp.power(10000.0, -jnp.arange(half, dtype=F32) / half)
        ang = pos * inv[None, :]
        return jnp.cos(ang), jnp.sin(ang)

    c32, s32 = cs(32)
    c16, s16 = cs(16)
    z = lambda w: jnp.zeros((s, w), F32)
    o = lambda w: jnp.ones((s, w), F32)
    t = {}
    t["b_cos"] = jnp.concatenate([c32, c32, c32, c32], 1)
    t["b_sin"] = jnp.concatenate([-s32, s32, -s32, s32], 1)
    t["q_cos"] = jnp.concatenate([o(64), c16, c16, z(32)], 1)
    t["q_sin"] = jnp.concatenate([z(64), -s16, s16, z(32)], 1)
    t["k_cos"] = jnp.concatenate([z(32), c16, c16, z(64)], 1)
    t["k_sin"] = jnp.concatenate([z(32), -s16, s16, z(64)], 1)
    lg = jnp.log1p(-jnp.exp2(-5.0 - jnp.arange(N_HEADS, dtype=F32)))
    j = jnp.arange(CHUNK, dtype=F32)
    rel = j[:, None] - j[None, :]
    t["decay"] = jnp.where(rel[None] >= 0, jnp.exp(jnp.maximum(rel, 0.0)[None] * lg[:, None, None]), 0.0)
    t["decay_t"] = jnp.swapaxes(t["decay"], 1, 2)

    def rows(e):
        return jnp.repeat(e.T, HEAD_DIM, axis=1)

    t["qw"] = rows(jnp.exp((j + 1.0)[None, :] * lg[:, None]))
    t["kw"] = rows(jnp.exp((CHUNK - 1 - j)[None, :] * lg[:, None]))
    t["kw2"] = rows(jnp.exp((CHUNK - j)[None, :] * lg[:, None]))
    t["qw0"] = rows(jnp.exp(j[None, :] * lg[:, None]))
    t["cd"] = jnp.repeat(jnp.exp(CHUNK * lg), HEAD_DIM)[None, :]
    e = np.zeros((128, 512), np.float32)
    for h in range(N_HEADS):
        for r in range(32):
            e[MISC_KR + r, 128 * h + 64 + r] = 1.0
    t["place"] = jnp.asarray(e)
    lane_head = np.arange(GROUP) // HEAD_DIM
    t["grp"] = jnp.asarray((lane_head[:, None] == lane_head[None, :]) / HEAD_DIM, _MXU)
    hsel = (np.arange(128)[:, None] == lane_head[None, :]).astype(np.float32)
    t["hsel"] = jnp.asarray(hsel)
    t["hselt"] = jnp.asarray(hsel.T, _MXU)
    return t


def _norm_matmul(x, g, w, name, comm=None):
    s, d = x.shape
    n = w.shape[1]
    tm, tn = min(512, s), 256
    ni = s // tm

    def body(*refs):
        (x_ref, g_ref, w_ref), (z_ref, h_ref), _, cc = _split_refs(refs, 3, 2, comm)
        i = pl.program_id(0)
        _host_gather(comm, cc, i, ni, late=True)
        h = _rms(x_ref[...], g_ref[...]).astype(h_ref.dtype)
        h_ref[...] = h
        for j in range(n // tn):
            z_ref[:, tn * j:tn * (j + 1)] = jnp.dot(h, w_ref[:, tn * j:tn * (j + 1)], preferred_element_type=F32)
        if comm is not None:
            @pl.when(i == ni - 1)
            def _():
                comm.wait(*cc)

    in_specs = [pl.BlockSpec((tm, d), lambda i: (i, 0)), pl.BlockSpec((1, d), lambda i: (0, 0)),
                pl.BlockSpec((d, n), lambda i: (0, 0))]
    out_specs = [pl.BlockSpec((tm, n), lambda i: (i, 0)), pl.BlockSpec((tm, d), lambda i: (i, 0))]
    out_shape = [SDS((s, n), F32), SDS((s, d), _MXU)]
    return _call_with_comm(body, (ni,), in_specs, out_specs, out_shape, [], [x, g, w], comm, ("arbitrary",), name)


def _mm_tn(a, b, name, *, a_fn=None, blocked=False, out_dtype=F32):
    k, m = a.shape
    n = b.shape[1]
    tm, tk = min(1024, m), min(1024, k)
    tn = next(t for t in (2816, 1024, 512, 256, 128) if n % t == 0)
    assert m % tm == 0 and k % tk == 0
    nk = k // tk

    def body(a_ref, b_ref, o_ref, acc):
        kk = pl.program_id(2)

        @pl.when(kk == 0)
        def _():
            acc[...] = jnp.zeros_like(acc)

        av = a_ref[...]
        if a_fn is not None:
            av = a_fn(av.astype(F32))
        acc[...] += _dot_tn(av, b_ref[...])

        @pl.when(kk == nk - 1)
        def _():
            if blocked:
                for c in range(tn // 512):
                    o_ref[c] = acc[:, 512 * c:512 * (c + 1)].astype(o_ref.dtype)
            else:
                o_ref[...] = acc[...].astype(o_ref.dtype)

    if blocked:
        assert tn % 512 == 0
        out_spec = pl.BlockSpec((tn // 512, tm, 512), lambda i, j, kk: (j, i, 0))
        out_shape = SDS((n // 512, m, 512), out_dtype)
    else:
        out_spec = pl.BlockSpec((tm, tn), lambda i, j, kk: (i, j))
        out_shape = SDS((m, n), out_dtype)
    return pl.pallas_call(
        body, grid=(m // tm, n // tn, nk),
        in_specs=[pl.BlockSpec((tk, tm), lambda i, j, kk: (kk, i)), pl.BlockSpec((tk, tn), lambda i, j, kk: (kk, j))],
        out_specs=out_spec, out_shape=out_shape, scratch_shapes=[pltpu.VMEM((tm, tn), F32)],
        compiler_params=_cp("parallel", "parallel", "arbitrary"), name=name)(a, b)


def _split_dot(x, m):
    hi = x.astype(_MXU)
    lo = (x - hi.astype(F32)).astype(_MXU)
    return jnp.dot(hi, m, preferred_element_type=F32) + jnp.dot(lo, m, preferred_element_type=F32)


def _gstandardize(t, grp):
    tc = t - _split_dot(t, grp)
    rs = lax.rsqrt(_split_dot(tc * tc, grp) + EPS)
    return tc * rs, rs


def _gstandardize_bwd(yh, rs, dy, grp):
    return rs * (dy - _split_dot(dy, grp) - yh * _split_dot(dy * yh, grp))


def _head_select(parts):
    hid = lax.broadcasted_iota(jnp.int32, parts[0].shape, 1) // HEAD_DIM
    return jnp.where(hid == 0, parts[0], jnp.where(hid == 1, parts[1], jnp.where(hid == 2, parts[2], parts[3])))


def _head_masked(x):
    hid = lax.broadcasted_iota(jnp.int32, x.shape, 1) // HEAD_DIM
    return [jnp.where(hid == h, x, jnp.zeros_like(x)) for h in range(N_HEADS)]


def _tril(w):
    r = lax.broadcasted_iota(jnp.int32, w.shape, 0)
    c = lax.broadcasted_iota(jnp.int32, w.shape, 1)
    return jnp.where(r >= c, w, 0.0)


def _sgu_mixed(vgb, wcs, bias, nchunk):
    ms = [[jnp.dot(wcs[h], vgb[CHUNK * c:CHUNK * (c + 1)], preferred_element_type=F32) for h in range(N_HEADS)]
          for c in range(nchunk)]
    return [_head_select(ms[c]) + bias for c in range(nchunk)]


def _sgu_fwd(z, gain, w_s, b_t, tb, name):
    s = z.shape[0]
    tm = min(512, s)
    const = lambda a: pl.BlockSpec(a.shape, lambda i: (0,) * a.ndim)

    def body(u_ref, v_ref, g_ref, w_ref, b_ref, grp_ref, hsel_ref, y_ref):
        u = _gelu(u_ref[...])
        vh, _ = _gstandardize(_gelu(v_ref[...]), grp_ref[...])
        vgb = (vh * g_ref[...]).astype(_MXU)
        bias = _dot_exact(b_ref[...], hsel_ref[...])
        wcs = [_tril(w_ref[h]).astype(_MXU) for h in range(N_HEADS)]
        for c, mixed in enumerate(_sgu_mixed(vgb, wcs, bias, tm // CHUNK)):
            r = slice(CHUNK * c, CHUNK * (c + 1))
            y_ref[r, :] = u[r] * mixed

    return pl.pallas_call(
        body, grid=(s // tm,),
        in_specs=[pl.BlockSpec((tm, GROUP), lambda i: (i, _Z_SGU[0])), pl.BlockSpec((tm, GROUP), lambda i: (i, _Z_SGU[1])),
                  pl.BlockSpec((1, GROUP), lambda i: (0, 0)), pl.BlockSpec((N_HEADS, CHUNK, CHUNK), lambda i: (0, 0, 0)),
                  pl.BlockSpec((CHUNK, 128), lambda i: (0, 0)), const(tb["grp"]), const(tb["hsel"])],
        out_specs=pl.BlockSpec((tm, GROUP), lambda i: (i, 0)), out_shape=SDS((s, GROUP), F32),
        compiler_params=_cp("parallel"), name=name)(z, z, gain, w_s, b_t, tb["grp"], tb["hsel"])


def _sgu_bwd(dy, z, gain, w_s, b_t, tb, dz, name):
    s = z.shape[0]
    tm = min(512, s)
    nchunk = tm // CHUNK
    const = lambda a: pl.BlockSpec(a.shape, lambda i: (0,) * a.ndim)

    def body(dy_ref, u_ref, v_ref, g_ref, w_ref, b_ref, grp_ref, hsel_ref, hselt_ref, _, dz_ref, dg_ref, dw_ref, db_ref):
        @pl.when(pl.program_id(0) == 0)
        def _():
            dg_ref[...] = jnp.zeros_like(dg_ref)
            dw_ref[...] = jnp.zeros_like(dw_ref)
            db_ref[...] = jnp.zeros_like(db_ref)

        grp = grp_ref[...]
        u_pre, v_pre, gain_v = u_ref[...], v_ref[...], g_ref[...]
        u = _gelu(u_pre)
        vh, rs = _gstandardize(_gelu(v_pre), grp)
        vgb = (vh * gain_v).astype(_MXU)
        dyv = dy_ref[...]
        bias = _dot_exact(b_ref[...], hsel_ref[...])
        wfs = [_tril(w_ref[h]) for h in range(N_HEADS)]
        wcs = [w.astype(_MXU) for w in wfs]
        wts = [w.T.astype(_MXU) for w in wfs]
        mixed = _sgu_mixed(vgb, wcs, bias, nchunk)
        gu = _gelu_grad(u_pre)
        dms, dmh = [], []
        for c in range(nchunk):
            r = slice(CHUNK * c, CHUNK * (c + 1))
            dz_ref[r, 0:GROUP] = (dyv[r] * mixed[c] * gu[r]).astype(dz_ref.dtype)
            dm = dyv[r] * u[r]
            dms.append(dm)
            dmh.append([m.astype(_MXU) for m in _head_masked(dm)])
        dws = [sum(lax.dot_general(dmh[c][h], vgb[CHUNK * c:CHUNK * (c + 1)], (((1,), (1,)), ((), ())),
                                   preferred_element_type=F32) for c in range(nchunk)) for h in range(N_HEADS)]
        dvg = jnp.concatenate([sum(jnp.dot(wts[h], dmh[c][h], preferred_element_type=F32) for h in range(N_HEADS))
                               for c in range(nchunk)], axis=0)
        for h in range(N_HEADS):
            dw_ref[h] += _tril(dws[h])
        db_ref[...] += sum(_split_dot(dm, hselt_ref[...]) for dm in dms)
        dg_ref[...] += jnp.sum(dvg * vh, axis=0, keepdims=True)
        dv = _gstandardize_bwd(vh, rs, dvg * gain_v, grp)
        dz_ref[:, GROUP:2 * GROUP] = (dv * _gelu_grad(v_pre)).astype(dz_ref.dtype)

    consts = [tb["grp"], tb["hsel"], tb["hselt"]]
    return pl.pallas_call(
        body, grid=(s // tm,),
        in_specs=[pl.BlockSpec((tm, GROUP), lambda i: (i, 0)),
                  pl.BlockSpec((tm, GROUP), lambda i: (i, _Z_SGU[0])), pl.BlockSpec((tm, GROUP), lambda i: (i, _Z_SGU[1])),
                  pl.BlockSpec((1, GROUP), lambda i: (0, 0)), pl.BlockSpec((N_HEADS, CHUNK, CHUNK), lambda i: (0, 0, 0)),
                  pl.BlockSpec((CHUNK, 128), lambda i: (0, 0))] + [const(a) for a in consts]
        + [pl.BlockSpec(memory_space=pl.ANY)],
        out_specs=[pl.BlockSpec((tm, 2 * GROUP), lambda i: (i, _DZ_SGU)), pl.BlockSpec((1, GROUP), lambda i: (0, 0)),
                   pl.BlockSpec((N_HEADS, CHUNK, CHUNK), lambda i: (0, 0, 0)), pl.BlockSpec((CHUNK, 128), lambda i: (0, 0))],
        out_shape=[SDS(dz.shape, dz.dtype), SDS((1, GROUP), F32), SDS((N_HEADS, CHUNK, CHUNK), F32), SDS((CHUNK, 128), F32)],
        input_output_aliases={9: 0}, compiler_params=_cp("arbitrary"), name=name)(dy, z, z, gain, w_s, b_t, *consts, dz)


_SCALE_B = HEAD_DIM ** -0.5
RET_CHUNKS = 8
RET_CHUNKS_BWD = 4


def _block_diag(compact):
    full = jnp.concatenate([compact] * N_HEADS, axis=0)
    r = lax.broadcasted_iota(jnp.int32, full.shape, 0) // HEAD_DIM
    c = lax.broadcasted_iota(jnp.int32, full.shape, 1) // HEAD_DIM
    return jnp.where(r == c, full, 0.0)


def _diag_blocks(full):
    c = lax.broadcasted_iota(jnp.int32, (HEAD_DIM, GROUP), 1) // HEAD_DIM
    return sum(jnp.where(c == h, full[HEAD_DIM * h:HEAD_DIM * (h + 1), :], 0.0) for h in range(N_HEADS))


def _ret_fwd(z, tb, name):
    s = z.shape[0]
    nc = s // CHUNK
    per = min(RET_CHUNKS, nc)
    rows = per * CHUNK
    row = lambda col: pl.BlockSpec((rows, GROUP), lambda n, col=col: (n, col))
    const = lambda shape: pl.BlockSpec(shape, lambda n: (0,) * len(shape))

    def body(q_ref, k_ref, v_ref, g_ref, cos_ref, sin_ref, dec_ref, qw_ref, kw_ref, cd_ref, grp_ref, y_ref, o_ref, st_ref, state):
        @pl.when(pl.program_id(0) == 0)
        def _():
            state[...] = jnp.zeros_like(state)

        cos, sin = _lanes(cos_ref[...], GROUP), _lanes(sin_ref[...], GROUP)
        q = _rope(q_ref[...], cos, sin, 32)
        k = _rope(k_ref[...], cos, sin, 32) * _SCALE_B
        v = v_ref[...]
        g = g_ref[...]
        rcs = [slice(CHUNK * c, CHUNK * (c + 1)) for c in range(per)]
        vms = [[t.astype(_MXU) for t in _head_masked(v[r])] for r in rcs]
        scs = [[_dot_nt(t.astype(_MXU), k[r]) for t in _head_masked(q[r])] for r in rcs]
        kvs = [_dot_tn(k[r] * kw_ref[...], v[r]) for r in rcs]
        st = state[...]
        crosses = []
        for c, r in enumerate(rcs):
            st_ref[c] = st
            crosses.append(_dot(q[r] * qw_ref[...], _block_diag(st)))
            st = cd_ref[...] * st + _diag_blocks(kvs[c])
        state[...] = st
        outs = []
        for c in range(per):
            scd = [(scs[c][h] * dec_ref[h]).astype(_MXU) for h in range(N_HEADS)]
            outs.append(crosses[c] + sum(jnp.dot(scd[h], vms[c][h], preferred_element_type=F32) for h in range(N_HEADS)))
        o = jnp.concatenate(outs, axis=0)
        o_ref[...] = o
        yh, _ = _gstandardize(o, grp_ref[...])
        y_ref[...] = g * _sigmoid(g) * yh

    return pl.pallas_call(
        body, grid=(nc // per,),
        in_specs=[row(_Z_RET[0]), row(_Z_RET[1]), row(_Z_RET[2]), row(_Z_RET[3]), pl.BlockSpec((rows, 128), lambda n: (n, 0)),
                  pl.BlockSpec((rows, 128), lambda n: (n, 0)), const((N_HEADS, CHUNK, CHUNK)),
                  const((CHUNK, GROUP)), const((CHUNK, GROUP)), const((1, GROUP)), const((GROUP, GROUP))],
        out_specs=[pl.BlockSpec((rows, GROUP), lambda n: (n, 0)), pl.BlockSpec((rows, GROUP), lambda n: (n, 0)),
                   pl.BlockSpec((per, HEAD_DIM, GROUP), lambda n: (n, 0, 0))],
        out_shape=[SDS((s, GROUP), F32), SDS((s, GROUP), F32), SDS((nc, HEAD_DIM, GROUP), F32)],
        scratch_shapes=[pltpu.VMEM((HEAD_DIM, GROUP), F32)],
        compiler_params=_cp("arbitrary"), name=name)(z, z, z, z, tb["b_cos"], tb["b_sin"], tb["decay"], tb["qw"], tb["kw"], tb["cd"],
                                                       tb["grp"])


def _ret_bwd(dy, z, o_pre, states, tb, name):
    s = z.shape[0]
    nc = s // CHUNK
    per = min(RET_CHUNKS_BWD, nc)
    rows = per * CHUNK
    ns = nc // per
    rev = lambda col: pl.BlockSpec((rows, GROUP), lambda n, col=col: (ns - 1 - n, col))
    const = lambda shape: pl.BlockSpec(shape, lambda n: (0,) * len(shape))

    def body(dy_ref, q_ref, k_ref, v_ref, g_ref, o_ref, st_ref, cos_ref, sin_ref, dec_ref, dect_ref, qw_ref, kw2_ref, qw0_ref,
             cd_ref, grp_ref, dz_ref, rstate):
        @pl.when(pl.program_id(0) == 0)
        def _():
            rstate[...] = jnp.zeros_like(rstate)

        cos, sin = _lanes(cos_ref[...], GROUP), _lanes(sin_ref[...], GROUP)
        q = _rope(q_ref[...], cos, sin, 32)
        k = _rope(k_ref[...], cos, sin, 32) * _SCALE_B
        v = v_ref[...]
        g = g_ref[...]
        dyv = dy_ref[...]
        sg = _sigmoid(g)
        yh, rs = _gstandardize(o_ref[...], grp_ref[...])
        dz_ref[:, 3 * GROUP:4 * GROUP] = (dyv * yh * (sg * (1.0 + g * (1.0 - sg)))).astype(dz_ref.dtype)
        do = _gstandardize_bwd(yh, rs, dyv * (g * sg), grp_ref[...])
        hs = range(N_HEADS)
        rcs = [slice(CHUNK * c, CHUNK * (c + 1)) for c in range(per)]
        mask = lambda t: [m.astype(_MXU) for m in _head_masked(t)]
        qms, kms, vms, doms = ([mask(t[r]) for r in rcs] for t in (q, k, v, do))
        dps = [[_dot_nt(doms[c][h], v[r]) for h in hs] for c, r in enumerate(rcs)]
        pts = [[_dot_nt(kms[c][h], q[r]) for h in hs] for c, r in enumerate(rcs)]
        dpts = [[_dot_nt(vms[c][h], do[r]) for h in hs] for c, r in enumerate(rcs)]
        dq_x = [_dot_nt(do[r] * qw_ref[...], _block_diag(st_ref[c])) for c, r in enumerate(rcs)]
        r_new = [_dot_tn(q[r] * qw0_ref[...], do[r]) for r in rcs]
        rr = rstate[...]
        dk_x, dv_x = [None] * per, [None] * per
        for c in reversed(range(per)):
            r_bd = _block_diag(rr)
            dk_x[c] = _dot_nt(v[rcs[c]] * kw2_ref[...], r_bd)
            dv_x[c] = _dot(k[rcs[c]] * kw2_ref[...], r_bd)
            rr = cd_ref[...] * rr + _diag_blocks(r_new[c])
        rstate[...] = rr
        dqs, dks, dvs = [], [], []
        for c in range(per):
            dpd = [(dps[c][h] * dec_ref[h]).astype(_MXU) for h in hs]
            dptd = [(dpts[c][h] * dect_ref[h]).astype(_MXU) for h in hs]
            ptd = [(pts[c][h] * dect_ref[h]).astype(_MXU) for h in hs]
            dqs.append(dq_x[c] + sum(jnp.dot(dpd[h], kms[c][h], preferred_element_type=F32) for h in hs))
            dks.append(dk_x[c] + sum(jnp.dot(dptd[h], qms[c][h], preferred_element_type=F32) for h in hs))
            dvs.append(dv_x[c] + sum(jnp.dot(ptd[h], doms[c][h], preferred_element_type=F32) for h in hs))
        dz_ref[:, 0:GROUP] = _rope_bwd(jnp.concatenate(dqs, axis=0), cos, sin, 32).astype(dz_ref.dtype)
        dz_ref[:, GROUP:2 * GROUP] = _rope_bwd(jnp.concatenate(dks, axis=0) * _SCALE_B, cos, sin, 32).astype(dz_ref.dtype)
        dz_ref[:, 2 * GROUP:3 * GROUP] = jnp.concatenate(dvs, axis=0).astype(dz_ref.dtype)

    r0 = lambda: pl.BlockSpec((rows, GROUP), lambda n: (ns - 1 - n, 0))
    r128 = lambda: pl.BlockSpec((rows, 128), lambda n: (ns - 1 - n, 0))
    return pl.pallas_call(
        body, grid=(ns,),
        in_specs=[r0(), rev(_Z_RET[0]), rev(_Z_RET[1]), rev(_Z_RET[2]), rev(_Z_RET[3]), r0(),
                  pl.BlockSpec((per, HEAD_DIM, GROUP), lambda n: (ns - 1 - n, 0, 0)),
                  r128(), r128(), const((N_HEADS, CHUNK, CHUNK)), const((N_HEADS, CHUNK, CHUNK)), const((CHUNK, GROUP)),
                  const((CHUNK, GROUP)), const((CHUNK, GROUP)), const((1, GROUP)), const((GROUP, GROUP))],
        out_specs=pl.BlockSpec((rows, 4 * GROUP), lambda n: (ns - 1 - n, _DZ_RET)),
        out_shape=SDS((s, NZ), _MXU), scratch_shapes=[pltpu.VMEM((HEAD_DIM, GROUP), F32)],
        compiler_params=_cp("arbitrary"), name=name)(
            dy, z, z, z, z, o_pre, states, tb["b_cos"], tb["b_sin"], tb["decay"], tb["decay_t"], tb["qw"], tb["kw2"], tb["qw0"],
            tb["cd"], tb["grp"])


TQ = 256


def _log_sigmoid(x):
    return jnp.minimum(x, 0.0) - jnp.log1p(jnp.exp(-jnp.abs(x)))


def _fox_prep(z, b_f, name):
    s = z.shape[0]
    nb = s // TQ

    def body(m_ref, b_ref, cc_ref, carry):
        @pl.when(pl.program_id(0) == 0)
        def _():
            carry[...] = jnp.zeros_like(carry)

        lane = lax.broadcasted_iota(jnp.int32, (TQ, 128), 1)
        logf = jnp.where(lane < N_HEADS, _log_sigmoid(m_ref[...] + b_ref[...]), 0.0)
        r = lax.broadcasted_iota(jnp.int32, (TQ, TQ), 0)
        c = lax.broadcasted_iota(jnp.int32, (TQ, TQ), 1)
        tri = jnp.where(r >= c, 1.0, 0.0).astype(F32)
        cum = _dot_exact(tri, logf) + carry[...]
        cc_ref[...] = cum * LOG2E
        carry[...] = cum[TQ - 1:TQ, :]

    return pl.pallas_call(
        body, grid=(nb,),
        in_specs=[pl.BlockSpec((TQ, 128), lambda i: (i, NZ // 128 - 1)), pl.BlockSpec((1, 128), lambda i: (0, 0))],
        out_specs=pl.BlockSpec((TQ, 128), lambda i: (i, 0)),
        out_shape=SDS((s, 128), F32), scratch_shapes=[pltpu.VMEM((1, 128), F32)],
        compiler_params=_cp("arbitrary"), name=name)(z, b_f)


def _fox_post(dcr, dcq, z, b_f, dkr, dz, name):
    s = z.shape[0]
    nb = s // TQ

    def body(dc_ref, dcq_ref, m_ref, b_ref, dkr_ref, _, dz_ref, db_ref, carry):
        @pl.when(pl.program_id(0) == 0)
        def _():
            carry[...] = jnp.zeros_like(carry)
            db_ref[...] = jnp.zeros_like(db_ref)

        r = lax.broadcasted_iota(jnp.int32, (TQ, TQ), 0)
        c = lax.broadcasted_iota(jnp.int32, (TQ, TQ), 1)
        triu = jnp.where(c >= r, 1.0, 0.0).astype(F32)
        dc = jnp.concatenate([dc_ref[0], jnp.zeros((120, TQ), F32)], axis=0)
        dlogf = _dot_exact(triu, dc, (((1,), (1,)), ((), ()))) + _dot_exact(triu, dcq_ref[...]) + carry[...]
        carry[...] = dlogf[0:1, :]
        x = m_ref[...] + b_ref[...]
        lane = lax.broadcasted_iota(jnp.int32, (TQ, 128), 1)
        df = jnp.where(lane < N_HEADS, dlogf * _sigmoid(-x), 0.0)
        db_ref[...] += jnp.sum(df, axis=0, keepdims=True)
        dz_ref[...] = (df + dkr_ref[...]).astype(dz_ref.dtype)

    rv = lambda i: nb - 1 - i
    return pl.pallas_call(
        body, grid=(nb,),
        in_specs=[pl.BlockSpec((1, 8, TQ), lambda i: (rv(i), 0, 0)), pl.BlockSpec((TQ, 128), lambda i: (rv(i), 0)),
                  pl.BlockSpec((TQ, 128), lambda i: (rv(i), NZ // 128 - 1)),
                  pl.BlockSpec((1, 128), lambda i: (0, 0)), pl.BlockSpec((TQ, 128), lambda i: (rv(i), 0)),
                  pl.BlockSpec(memory_space=pl.ANY)],
        out_specs=[pl.BlockSpec((TQ, 128), lambda i: (rv(i), _DZ_MISC)), pl.BlockSpec((1, 128), lambda i: (0, 0))],
        out_shape=[SDS(dz.shape, dz.dtype), SDS((1, 128), F32)], scratch_shapes=[pltpu.VMEM((1, 128), F32)],
        input_output_aliases={5: 0}, compiler_params=_cp("arbitrary"), name=name)(dcr, dcq, z, b_f, dkr, dz)


NEG = -1e30
TKV = 512


def _key_block(s):
    return min(TKV, s)


def _diag_mask(shape, off):
    r = lax.broadcasted_iota(jnp.int32, shape, 0)
    c = lax.broadcasted_iota(jnp.int32, shape, 1)
    return c + off >= r


def _head_lanes(h, dqk):
    return slice(128 * (h // 2), 128 * (h // 2) + 128) if dqk == HEAD_DIM else slice(128 * h, 128 * h + 128)


def _keep_half(x, a, axis):
    idx = lax.broadcasted_iota(jnp.int32, x.shape, axis)
    return jnp.where((idx < HEAD_DIM) if a == 0 else (idx >= HEAD_DIM), x, jnp.zeros_like(x))


def _scaled_qt(q, scale):
    qs = q.astype(F32) * (scale * LOG2E)
    return [qs[TQ * b:TQ * (b + 1)].T.astype(_MXU) for b in range(q.shape[0] // TQ)]


def _kv_prep(z, qcol, kcol, vcol, scale, name):
    s = z.shape[0]
    tk = _key_block(s)
    nk = s // tk

    def body(q_ref, k_ref, v_ref, kb_ref, vb_ref, vt_ref, qt_ref):
        kb_ref[...] = k_ref[...].astype(_MXU)
        v = v_ref[...]
        vb_ref[...] = v.astype(_MXU)
        vt_ref[0] = v.T.astype(_MXU)
        for b, t in enumerate(_scaled_qt(q_ref[...], scale)):
            qt_ref[b] = t

    blk = pl.BlockSpec((tk, GROUP), lambda i: (i, 0))
    col = lambda c: pl.BlockSpec((tk, GROUP), lambda i, c=c: (i, c))
    return pl.pallas_call(
        body, grid=(nk,), in_specs=[col(qcol), col(kcol), col(vcol)],
        out_specs=[blk, blk, pl.BlockSpec((1, GROUP, tk), lambda i: (i, 0, 0)),
                   pl.BlockSpec((tk // TQ, GROUP, TQ), lambda i: (i, 0, 0))],
        out_shape=[SDS((s, GROUP), _MXU), SDS((s, GROUP), _MXU), SDS((nk, GROUP, tk), _MXU), SDS((s // TQ, GROUP, TQ), _MXU)],
        compiler_params=_cp("parallel"), name=name)(z, z, z)


LOG2E = 1.4426950408889634


def _attn_fwd(q, qcol, dqk, kb, vt, scale, ck2, name, comm=None):
    s = q.shape[0]
    nq = s // TQ
    tk = _key_block(s)
    ratio = tk // TQ
    wq = N_HEADS * dqk
    bias = ck2 is not None

    def body(*refs):
        ins, (o_ref, l_ref), _, cc = _split_refs(refs, 4 if bias else 3, 2, comm)
        if bias:
            q_ref, k_ref, vt_ref, cc_ref = ins
        else:
            q_ref, k_ref, vt_ref = ins
        i = pl.program_id(0)
        _host_gather(comm, cc, i, nq)
        qts = []
        for h in range(N_HEADS):
            qt = (q_ref[:, _head_lanes(h, dqk)].astype(F32) * (scale * LOG2E)).T
            qts.append((_keep_half(qt, h % 2, 0) if dqk == HEAD_DIM else qt).astype(_MXU))

        def step(j, carry, off):
            r0 = pl.multiple_of(j * tk, tk)
            vtj = vt_ref[j]
            sts = [jnp.dot(k_ref[pl.ds(r0, tk), _head_lanes(h, dqk)], qts[h], preferred_element_type=F32)
                   for h in range(N_HEADS)]
            stats, ps = [], []
            for h in range(N_HEADS):
                m, l, _ = carry[3 * h:3 * h + 3]
                st = sts[h]
                if bias:
                    st = st - cc_ref[pl.ds(r0, tk), h:h + 1]
                if off is not None:
                    st = jnp.where(_diag_mask(st.shape, off), st, NEG)
                m_new = jnp.maximum(m, jnp.max(st, axis=0, keepdims=True))
                alpha = jnp.exp2(m - m_new)
                p = jnp.exp2(st - m_new)
                stats.append((m_new, alpha * l + jnp.sum(p, axis=0, keepdims=True), alpha))
                ps.append(p.astype(_MXU))
            out = []
            for h in range(N_HEADS):
                m_new, l, alpha = stats[h]
                acc = alpha * carry[3 * h + 2] + jnp.dot(vtj[HEAD_DIM * h:HEAD_DIM * (h + 1), :], ps[h],
                                                         preferred_element_type=F32)
                out += [m_new, l, acc]
            return tuple(out)

        init = (jnp.full((1, TQ), NEG, F32), jnp.zeros((1, TQ), F32), jnp.zeros((HEAD_DIM, TQ), F32)) * N_HEADS
        jd = i // ratio
        carry = lax.fori_loop(0, jd, functools.partial(step, off=None), init)
        carry = step(jd, carry, TQ * (i % ratio))
        l_ref[...] = jnp.zeros_like(l_ref)
        for h in range(N_HEADS):
            l_ref[0, h:h + 1, :] = carry[3 * h] + jnp.log2(carry[3 * h + 1])
        for p in range(2):
            ot = jnp.concatenate([carry[6 * p + 2] / carry[6 * p + 1], carry[6 * p + 5] / carry[6 * p + 4]], axis=0)
            o_ref[:, 128 * p:128 * (p + 1)] = ot.T
        if comm is not None:
            @pl.when(i == nq - 1)
            def _():
                comm.wait(*cc)

    rows = pl.BlockSpec((1, 8, TQ), lambda i: (i, 0, 0))
    in_specs = [pl.BlockSpec((TQ, wq), lambda i: (i, qcol)), pl.BlockSpec((s, wq), lambda i: (0, 0)),
                pl.BlockSpec((s // tk, GROUP, tk), lambda i: (0, 0, 0))]
    args = [q, kb, vt]
    if bias:
        in_specs.append(pl.BlockSpec((s, 128), lambda i: (0, 0)))
        args.append(ck2)
    out_specs = [pl.BlockSpec((TQ, GROUP), lambda i: (i, 0)), rows]
    out_shape = [SDS((s, GROUP), F32), SDS((nq, 8, TQ), F32)]
    return _call_with_comm(body, (nq,), in_specs, out_specs, out_shape, [], args, comm, ("arbitrary",), name)


def _call_with_comm(body, grid, in_specs, out_specs, out_shape, scratch, args, comm, semantics, name, aliases=None):
    n_out = len(out_shape)
    if comm is not None:
        in_specs, out_specs = in_specs + comm.in_specs, out_specs + comm.out_specs
        out_shape, scratch, args = out_shape + comm.out_shape, scratch + comm.scratch, list(args) + comm.arrs
    res = pl.pallas_call(body, grid=grid, in_specs=in_specs, out_specs=out_specs, out_shape=out_shape,
                         scratch_shapes=scratch, input_output_aliases=aliases or {}, compiler_params=_cp(*semantics),
                         name=name)(*args)
    return (*res[:n_out], list(res[n_out:]))


def _attn_bwd(kb, vb, qt, dot, lse, dl, dqk, scale, ck2, name, kv_dtype, comm=None, kv_into=None):
    s = kb.shape[0]
    nq = s // TQ
    tk = _key_block(s)
    ratio = tk // TQ
    nkb = s // tk
    wq = N_HEADS * dqk
    bias = ck2 is not None

    merged = kv_into is not None
    n_in = 6 + bias + merged
    n_out = 3 + 2 * bias - merged

    def body(*refs):
        ins, outs, _, cc = _split_refs(refs, n_in, n_out, comm)
        k_ref, v_ref, qt_ref, dot_ref, l_ref, d_ref = ins[:6]
        cc_ref = ins[6] if bias else None
        dqt_ref = outs[0]
        if merged:
            dk_ref, dv_ref = outs[1].at[:, 0:wq], outs[1].at[:, wq:wq + GROUP]
        else:
            dk_ref, dv_ref = outs[1], outs[2]
        if bias:
            dck_ref, dcq_ref = outs[-2:]
        j = pl.program_id(0)

        @pl.when(j == 0)
        def _():
            if comm is not None:
                comm.start(*cc)
            dqt_ref[...] = jnp.zeros_like(dqt_ref)
            if bias:
                dcq_ref[...] = jnp.zeros_like(dcq_ref)

        ks, kts, vs = [], [], []
        for h in range(N_HEADS):
            k2 = k_ref[:, _head_lanes(h, dqk)]
            if dqk == HEAD_DIM:
                k2 = _keep_half(k2, h % 2, 1)
            ks.append(k2)
            kts.append(k2.astype(F32).T.astype(_MXU))
            vs.append(_keep_half(v_ref[:, _head_lanes(h, HEAD_DIM)], h % 2, 1))
        cks = [cc_ref[:, h:h + 1] for h in range(N_HEADS)] if bias else None

        nt = (((1,), (1,)), ((), ()))

        def step(i, carry, off):
            qti, doti, li, di = qt_ref[i], dot_ref[i], l_ref[i], d_ref[i]
            qls = [_head_lanes(h, dqk) for h in range(N_HEADS)]
            vls = [_head_lanes(h, HEAD_DIM) for h in range(N_HEADS)]
            sts, dpts = [], []
            for h in range(N_HEADS):
                sts.append(jnp.dot(ks[h], qti[qls[h], :], preferred_element_type=F32))
                dpts.append(jnp.dot(vs[h], doti[vls[h], :], preferred_element_type=F32))
            pbs, dsbs, dcks = [], [], []
            for h in range(N_HEADS):
                st = sts[h] - li[h:h + 1, :]
                if bias:
                    st = st - cks[h]
                p = jnp.exp2(st)
                if off is not None:
                    p = jnp.where(_diag_mask(p.shape, off), p, 0.0)
                dst = p * (dpts[h] - di[h:h + 1, :])
                pbs.append(p.astype(_MXU))
                dsbs.append(dst.astype(_MXU))
                if bias:
                    dcks.append(carry[3 * h + 2] + jnp.sum(dst, axis=1, keepdims=True))
                    dcq_ref[i, h:h + 1, :] += jnp.sum(dst, axis=0, keepdims=True)
                else:
                    dcks.append(carry[3 * h + 2])
            out = []
            for h in range(N_HEADS):
                dvt = carry[3 * h + 1] + lax.dot_general(doti[HEAD_DIM * h:HEAD_DIM * (h + 1), :], pbs[h], nt,
                                                         preferred_element_type=F32)
                dkt = carry[3 * h] + lax.dot_general(qti[dqk * h:dqk * (h + 1), :], dsbs[h], nt, preferred_element_type=F32)
                dqt_ref[i, qls[h], :] += jnp.dot(kts[h], dsbs[h], preferred_element_type=F32) * scale
                out += [dkt, dvt, dcks[h]]
            return tuple(out)

        carry = (jnp.zeros((dqk, tk), F32), jnp.zeros((HEAD_DIM, tk), F32), jnp.zeros((tk, 1), F32)) * N_HEADS
        for r in range(ratio):
            carry = step(ratio * j + r, carry, TQ * r)
        carry = lax.fori_loop(ratio * (j + 1), nq, functools.partial(step, off=None), carry)
        for p in range(2):
            dv_ref[:, 128 * p:128 * (p + 1)] = jnp.concatenate([carry[6 * p + 1], carry[6 * p + 4]], axis=0).T.astype(dv_ref.dtype)
            if dqk == HEAD_DIM:
                dk_ref[:, 128 * p:128 * (p + 1)] = (jnp.concatenate([carry[6 * p], carry[6 * p + 3]], axis=0).T
                                                    * (1.0 / LOG2E)).astype(dk_ref.dtype)
        if dqk != HEAD_DIM:
            for h in range(N_HEADS):
                dk_ref[:, 128 * h:128 * (h + 1)] = (carry[3 * h].T * (1.0 / LOG2E)).astype(dk_ref.dtype)
        if bias:
            dck_ref[...] = jnp.zeros_like(dck_ref)
            for h in range(N_HEADS):
                dck_ref[:, h:h + 1] = -carry[3 * h + 2]
        if comm is not None:
            @pl.when(j == nkb - 1)
            def _():
                comm.wait(*cc)

    blk = lambda w: pl.BlockSpec((tk, w), lambda j: (j, 0))
    full3 = lambda w: pl.BlockSpec((nq, w, TQ), lambda j: (0, 0, 0))
    in_specs = [blk(wq), blk(GROUP), full3(wq), full3(GROUP), full3(8), full3(8)]
    args = [kb, vb, qt, dot, lse, dl]
    if merged:
        assert wq == GROUP
        out_specs = [full3(wq), pl.BlockSpec((tk, wq + GROUP), lambda j: (j, _DZ_FOX_KV))]
        out_shape = [SDS((nq, wq, TQ), F32), SDS(kv_into.shape, kv_into.dtype)]
    else:
        out_specs = [full3(wq), blk(wq), blk(GROUP)]
        out_shape = [SDS((nq, wq, TQ), F32), SDS((s, wq), kv_dtype), SDS((s, GROUP), kv_dtype)]
    if bias:
        in_specs.append(blk(128))
        args.append(ck2)
        out_specs += [blk(128), full3(8)]
        out_shape += [SDS((s, 128), F32), SDS((nq, 8, TQ), F32)]
    aliases = {}
    if merged:
        in_specs.append(pl.BlockSpec(memory_space=pl.ANY))
        args.append(kv_into)
        aliases = {len(args) - 1: 1}
    return _call_with_comm(body, (nkb,), in_specs, out_specs, out_shape, [], args, comm, ("arbitrary",), name, aliases)


def _untranspose(xt, dtype, name, into=None, col=0):
    nq, w, _ = xt.shape
    if into is not None:
        def body_into(x_ref, _, o_ref):
            o_ref[...] = x_ref[0].T.astype(o_ref.dtype)

        return pl.pallas_call(
            body_into, grid=(nq,),
            in_specs=[pl.BlockSpec((1, w, TQ), lambda i: (i, 0, 0)), pl.BlockSpec(memory_space=pl.ANY)],
            out_specs=pl.BlockSpec((TQ, w), lambda i: (i, col)), out_shape=SDS(into.shape, into.dtype),
            input_output_aliases={1: 0}, compiler_params=_cp("parallel"), name=name)(xt, into)

    def body(x_ref, o_ref):
        o_ref[...] = x_ref[0].T.astype(o_ref.dtype)

    return pl.pallas_call(
        body, grid=(nq,), in_specs=[pl.BlockSpec((1, w, TQ), lambda i: (i, 0, 0))],
        out_specs=pl.BlockSpec((TQ, w), lambda i: (i, 0)), out_shape=SDS((nq * TQ, w), dtype),
        compiler_params=_cp("parallel"), name=name)(xt)


_SCALE_D = (64 + 32) ** -0.5
_COL_CQ, _COL_CKV, _COL_MISC = 2304 // 256, 2560 // 128, 2688 // 128


def _mla_prep(z, gq, gkv, wq, wk, wv, tb, name):
    s = z.shape[0]
    tm = _key_block(s)
    row = lambda w, c: pl.BlockSpec((tm, w), lambda i, c=c: (i, c))
    const = lambda a: pl.BlockSpec(a.shape, lambda i: (0,) * a.ndim)

    def body(cq_ref, ckv_ref, m_ref, gq_ref, gkv_ref, wq_ref, wk_ref, wv_ref, e_ref, qc_ref, qs_ref, kc_ref, ks_ref,
             q_ref, k_ref, v_ref, vt_ref, cqn_ref, ckvn_ref, qt_ref):
        cqn = _rms(cq_ref[...], gq_ref[...]).astype(_MXU)
        ckvn = _rms(ckv_ref[...], gkv_ref[...]).astype(_MXU)
        cqn_ref[...] = cqn
        ckvn_ref[...] = ckvn
        qb = _rope(_dot(cqn, wq_ref[...]), _lanes(qc_ref[...], 512), _lanes(qs_ref[...], 512), 16).astype(q_ref.dtype)
        q_ref[...] = qb
        for b, t in enumerate(_scaled_qt(qb, _SCALE_D)):
            qt_ref[b] = t
        kr = _rope(m_ref[...], kc_ref[...], ks_ref[...], 16)
        k_ref[...] = (_dot(ckvn, wk_ref[...]) + _dot(kr, e_ref[...])).astype(k_ref.dtype)
        v = _dot(ckvn, wv_ref[...])
        v_ref[...] = v.astype(v_ref.dtype)
        vt_ref[0] = v.T.astype(vt_ref.dtype)

    e = tb["place"]
    return pl.pallas_call(
        body, grid=(s // tm,),
        in_specs=[row(256, _COL_CQ), row(128, _COL_CKV), row(128, _COL_MISC), const(gq), const(gkv), const(wq), const(wk),
                  const(wv), const(e), row(128, 0), row(128, 0), row(128, 0), row(128, 0)],
        out_specs=[row(512, 0), row(512, 0), row(256, 0), pl.BlockSpec((1, GROUP, tm), lambda i: (i, 0, 0)), row(256, 0),
                   row(128, 0), pl.BlockSpec((tm // TQ, 512, TQ), lambda i: (i, 0, 0))],
        out_shape=[SDS((s, 512), _MXU), SDS((s, 512), _MXU), SDS((s, 256), _MXU), SDS((s // tm, GROUP, tm), _MXU),
                   SDS((s, 256), _MXU), SDS((s, 128), _MXU), SDS((s // TQ, 512, TQ), _MXU)],
        compiler_params=_cp("parallel"), name=name)(
            z, z, z, gq, gkv, wq, wk, wv, e, tb["q_cos"], tb["q_sin"], tb["k_cos"], tb["k_sin"])


def _mla_prep_bwd(dqt, dk, dv, z, cqn, ckvn, gq, gkv, wq, wk, wv, tb, dz, name):
    s = z.shape[0]
    tm = min(512, s)
    row = lambda w, c: pl.BlockSpec((tm, w), lambda i, c=c: (i, c))
    const = lambda a: pl.BlockSpec(a.shape, lambda i: (0,) * a.ndim)
    acc = lambda shape: pl.BlockSpec(shape, lambda i: (0, 0))

    def body(dq_ref, dk_ref, dv_ref, cq_ref, ckv_ref, cqn_ref, ckvn_ref, gq_ref, gkv_ref, wq_ref, wk_ref, wv_ref, e_ref,
             qc_ref, qs_ref, kc_ref, ks_ref, _, dz_ref, dkr_ref, dwq_ref, dwk_ref, dwv_ref, dgq_ref, dgkv_ref):
        dcq_ref, dckv_ref = dz_ref.at[:, 0:256], dz_ref.at[:, 256:384]

        @pl.when(pl.program_id(0) == 0)
        def _():
            for r in (dwq_ref, dwk_ref, dwv_ref, dgq_ref, dgkv_ref):
                r[...] = jnp.zeros_like(r)

        dq = jnp.concatenate([dq_ref[b].T for b in range(tm // TQ)], axis=0)
        dqp = _rope_bwd(dq, _lanes(qc_ref[...], 512), _lanes(qs_ref[...], 512), 16)
        dkd = dk_ref[...]
        dvd = dv_ref[...]
        dwq_ref[...] += _dot_tn(cqn_ref[...], dqp)
        dwk_ref[...] += _dot_tn(ckvn_ref[...], dkd)
        dwv_ref[...] += _dot_tn(ckvn_ref[...], dvd)
        dcq, dgq = _rms_bwd(cq_ref[...], gq_ref[...], _dot_nt(dqp, wq_ref[...]))
        dckv, dgkv = _rms_bwd(ckv_ref[...], gkv_ref[...], _dot_nt(dkd, wk_ref[...]) + _dot_nt(dvd, wv_ref[...]))
        dcq_ref[...] = dcq.astype(dcq_ref.dtype)
        dckv_ref[...] = dckv.astype(dckv_ref.dtype)
        dgq_ref[...] += dgq
        dgkv_ref[...] += dgkv
        dkr = _dot_exact(dkd, e_ref[...], (((1,), (1,)), ((), ())))
        dkr_ref[...] = _rope_bwd(dkr, kc_ref[...], ks_ref[...], 16)

    e = tb["place"]
    return pl.pallas_call(
        body, grid=(s // tm,),
        in_specs=[pl.BlockSpec((tm // TQ, 512, TQ), lambda i: (i, 0, 0)), row(512, 0), row(256, 0), row(256, _COL_CQ),
                  row(128, _COL_CKV), row(256, 0), row(128, 0),
                  const(gq), const(gkv), const(wq), const(wk), const(wv), const(e), row(128, 0), row(128, 0), row(128, 0), row(128, 0),
                  pl.BlockSpec(memory_space=pl.ANY)],
        out_specs=[row(384, _DZ_MLA), row(128, 0), acc((256, 512)), acc((128, 512)), acc((128, 256)), acc((1, 256)),
                   acc((1, 128))],
        out_shape=[SDS(dz.shape, dz.dtype), SDS((s, 128), F32), SDS((256, 512), F32), SDS((128, 512), F32),
                   SDS((128, 256), F32), SDS((1, 256), F32), SDS((1, 128), F32)],
        input_output_aliases={17: 0}, compiler_params=_cp("arbitrary"), name=name)(
            dqt, dk, dv, z, z, cqn, ckvn, gq, gkv, wq, wk, wv, e, tb["q_cos"], tb["q_sin"], tb["k_cos"], tb["k_sin"], dz)


def _out_proj(ys, g, w, x, name):
    s, d = x.shape
    tm = min(512, s)

    def body(ya, yb, yc, yd, g_ref, w_ref, x_ref, o_ref, yn_ref):
        acc = x_ref[...]
        for i, y_ref in enumerate((ya, yb, yc, yd)):
            sl = slice(GROUP * i, GROUP * (i + 1))
            yn = _rms(y_ref[...], g_ref[:, sl]).astype(_MXU)
            yn_ref[:, sl] = yn
            acc = acc + jnp.dot(yn, w_ref[sl, :], preferred_element_type=F32)
        o_ref[...] = acc

    yspec = pl.BlockSpec((tm, GROUP), lambda i: (i, 0))
    return pl.pallas_call(
        body, grid=(s // tm,),
        in_specs=[yspec, yspec, yspec, yspec, pl.BlockSpec((1, d), lambda i: (0, 0)), pl.BlockSpec((d, d), lambda i: (0, 0)),
                  pl.BlockSpec((tm, d), lambda i: (i, 0))],
        out_specs=[pl.BlockSpec((tm, d), lambda i: (i, 0)), pl.BlockSpec((tm, d), lambda i: (i, 0))],
        out_shape=[SDS((s, d), F32), SDS((s, d), _MXU)], compiler_params=_cp("parallel"), name=name)(*ys, g, w, x)


def _out_proj_bwd(dx, w, ys, g, name):
    s, d = dx.shape
    tm = min(512, s)
    nb = tm // TQ

    def body(dx_ref, w_ref, ya, yb, yc, yd, g_ref, da, db, dg_ref, dtc_ref, dtd_ref, dlc_ref, dld_ref):
        @pl.when(pl.program_id(0) == 0)
        def _():
            dg_ref[...] = jnp.zeros_like(dg_ref)

        dyn = _dot_nt(dx_ref[...], w_ref[...])
        for i, y_ref in enumerate((ya, yb, yc, yd)):
            sl = slice(GROUP * i, GROUP * (i + 1))
            y = y_ref[...]
            dy, dg = _rms_bwd(y, g_ref[:, sl], dyn[:, sl])
            dg_ref[:, sl] += dg
            if i < 2:
                (da, db)[i][...] = dy
                continue
            dt_ref, dl_ref = ((dtc_ref, dlc_ref), (dtd_ref, dld_ref))[i - 2]
            dl_ref[...] = jnp.zeros_like(dl_ref)
            for b in range(nb):
                r = slice(TQ * b, TQ * (b + 1))
                dt_ref[b] = dy[r].T.astype(dt_ref.dtype)
                pt = (dy[r] * y[r]).T
                for h in range(N_HEADS):
                    dl_ref[b, h:h + 1, :] = jnp.sum(pt[HEAD_DIM * h:HEAD_DIM * (h + 1), :], axis=0, keepdims=True)

    yspec = pl.BlockSpec((tm, GROUP), lambda i: (i, 0))
    tspec = pl.BlockSpec((nb, GROUP, TQ), lambda i: (i, 0, 0))
    lspec = pl.BlockSpec((nb, 8, TQ), lambda i: (i, 0, 0))
    return pl.pallas_call(
        body, grid=(s // tm,),
        in_specs=[pl.BlockSpec((tm, d), lambda i: (i, 0)), pl.BlockSpec((d, d), lambda i: (0, 0)), yspec, yspec, yspec, yspec,
                  pl.BlockSpec((1, d), lambda i: (0, 0))],
        out_specs=[yspec, yspec, pl.BlockSpec((1, d), lambda i: (0, 0)), tspec, tspec, lspec, lspec],
        out_shape=[SDS((s, GROUP), F32)] * 2 + [SDS((1, d), F32)] + [SDS((s // TQ, GROUP, TQ), _MXU)] * 2
        + [SDS((s // TQ, 8, TQ), F32)] * 2,
        compiler_params=_cp("arbitrary"), name=name)(dx, w, *ys, g)


FF_BLOCK = 512
FF_ROWS = 1024


def _ffn_fwd(x, g, wu, wd, name, comm=None):
    s, d = x.shape
    nj = wu.shape[0]
    tm = min(FF_ROWS, s)
    ni = s // tm

    def body(*refs):
        (x_ref, g_ref, wu_ref, wd_ref), (o_ref, u_ref, h_ref), (acc,), cc = _split_refs(refs, 4, 3, comm)
        i, j = pl.program_id(0), pl.program_id(1)
        _host_gather(comm, cc, i * nj + j, ni * nj, late=False)

        @pl.when(j == 0)
        def _():
            h_ref[...] = _rms(x_ref[...], g_ref[...]).astype(h_ref.dtype)
            acc[...] = jnp.zeros_like(acc)

        halves = [slice(r, r + tm // 2) for r in range(0, tm, tm // 2)]
        us = [jnp.dot(h_ref[r, :], wu_ref[0], preferred_element_type=F32) for r in halves]
        for r, u in zip(halves, us):
            u_ref[r, :] = u.astype(u_ref.dtype)
            acc[r, :] += _dot(jnp.square(jnp.maximum(u, 0.0)), wd_ref[...])

        @pl.when(j == nj - 1)
        def _():
            o_ref[...] = x_ref[...] + acc[...]

        if comm is not None:
            @pl.when((i == ni - 1) & (j == nj - 1))
            def _():
                comm.wait(*cc)

    in_specs = [pl.BlockSpec((tm, d), lambda i, j: (i, 0)), pl.BlockSpec((1, d), lambda i, j: (0, 0)),
                pl.BlockSpec((1, d, FF_BLOCK), lambda i, j: (j, 0, 0)), pl.BlockSpec((FF_BLOCK, d), lambda i, j: (j, 0))]
    out_specs = [pl.BlockSpec((tm, d), lambda i, j: (i, 0)), pl.BlockSpec((tm, FF_BLOCK), lambda i, j: (i, j)),
                 pl.BlockSpec((tm, d), lambda i, j: (i, 0))]
    out_shape = [SDS((s, d), F32), SDS((s, nj * FF_BLOCK), _MXU), SDS((s, d), _MXU)]
    return _call_with_comm(body, (ni, nj), in_specs, out_specs, out_shape, [pltpu.VMEM((tm, d), F32)], [x, g, wu, wd], comm,
                           ("arbitrary", "arbitrary"), name)


def _ffn_bwd(dx2, x, u, g, wu, wd, name, comm=None):
    s, d = x.shape
    nj = wu.shape[0]
    tm = min(FF_ROWS, s)
    ni = s // tm

    def body(*refs):
        (dx_ref, x_ref, u_ref, g_ref, wu_ref, wd_ref), (o_ref, du_ref, dg_ref), (acc, dxb), cc = _split_refs(refs, 6, 3, comm)
        i, j = pl.program_id(0), pl.program_id(1)

        @pl.when((i == 0) & (j == 0))
        def _():
            if comm is not None:
                comm.start(*cc)
            dg_ref[...] = jnp.zeros_like(dg_ref)

        @pl.when(j == 0)
        def _():
            dxb[...] = dx_ref[...].astype(dxb.dtype)
            acc[...] = jnp.zeros_like(acc)

        nt = (((1,), (1,)), ((), ()))
        halves = [slice(r, r + tm // 2) for r in range(0, tm, tm // 2)]
        das = [lax.dot_general(dxb[r, :], wd_ref[...], nt, preferred_element_type=F32) for r in halves]
        for r, da in zip(halves, das):
            du = (da * 2.0 * jnp.maximum(u_ref[r, :].astype(F32), 0.0)).astype(du_ref.dtype)
            du_ref[r, :] = du
            acc[r, :] += lax.dot_general(du, wu_ref[0], nt, preferred_element_type=F32)

        @pl.when(j == nj - 1)
        def _():
            dxn, dg = _rms_bwd(x_ref[...], g_ref[...], acc[...])
            o_ref[...] = dx_ref[...] + dxn
            dg_ref[...] += dg

        if comm is not None:
            @pl.when((i == ni - 1) & (j == nj - 1))
            def _():
                comm.wait(*cc)

    in_specs = [pl.BlockSpec((tm, d), lambda i, j: (i, 0)), pl.BlockSpec((tm, d), lambda i, j: (i, 0)),
                pl.BlockSpec((tm, FF_BLOCK), lambda i, j: (i, j)), pl.BlockSpec((1, d), lambda i, j: (0, 0)),
                pl.BlockSpec((1, d, FF_BLOCK), lambda i, j: (j, 0, 0)), pl.BlockSpec((FF_BLOCK, d), lambda i, j: (j, 0))]
    out_specs = [pl.BlockSpec((tm, d), lambda i, j: (i, 0)), pl.BlockSpec((tm, FF_BLOCK), lambda i, j: (i, j)),
                 pl.BlockSpec((1, d), lambda i, j: (0, 0))]
    out_shape = [SDS((s, d), F32), SDS((s, nj * FF_BLOCK), _MXU), SDS((1, d), F32)]
    return _call_with_comm(body, (ni, nj), in_specs, out_specs, out_shape,
                           [pltpu.VMEM((tm, d), F32), pltpu.VMEM((tm, d), _MXU)], [dx2, x, u, g, wu, wd], comm,
                           ("arbitrary", "arbitrary"), name)


def _in_proj_bwd(dz, w, x, g, dx_up, name, comm=None):
    s, d = x.shape
    n = w.shape[1]
    tm = min(512, s)
    ni = s // tm

    def body(*refs):
        (dz_ref, w_ref, x_ref, g_ref, up_ref), (o_ref, dg_ref), _, cc = _split_refs(refs, 5, 2, comm)
        i = pl.program_id(0)

        @pl.when(i == 0)
        def _():
            if comm is not None:
                comm.start(*cc)
            dg_ref[...] = jnp.zeros_like(dg_ref)

        dh = lax.dot_general(dz_ref[...], w_ref[...], (((1,), (1,)), ((), ())), preferred_element_type=F32)
        dxn, dg = _rms_bwd(x_ref[...], g_ref[...], dh)
        o_ref[...] = up_ref[...] + dxn
        dg_ref[...] += dg
        if comm is not None:
            @pl.when(i == ni - 1)
            def _():
                comm.wait(*cc)

    in_specs = [pl.BlockSpec((tm, n), lambda i: (i, 0)), pl.BlockSpec((d, n), lambda i: (0, 0)),
                pl.BlockSpec((tm, d), lambda i: (i, 0)), pl.BlockSpec((1, d), lambda i: (0, 0)),
                pl.BlockSpec((tm, d), lambda i: (i, 0))]
    out_specs = [pl.BlockSpec((tm, d), lambda i: (i, 0)), pl.BlockSpec((1, d), lambda i: (0, 0))]
    out_shape = [SDS((s, d), F32), SDS((1, d), F32)]
    return _call_with_comm(body, (ni,), in_specs, out_specs, out_shape, [], [dz, w, x, g, dx_up], comm, ("arbitrary",), name)


def _loss_head(x, g, target, name):
    s, d = x.shape
    tm = min(512, s)

    def body(x_ref, g_ref, t_ref, l_ref, dx_ref, dg_ref):
        @pl.when(pl.program_id(0) == 0)
        def _():
            l_ref[...] = jnp.zeros_like(l_ref)
            dg_ref[...] = jnp.zeros_like(dg_ref)

        xv = x_ref[...]
        err = _rms(xv, g_ref[...]) - t_ref[...]
        l_ref[...] += jnp.sum(err * err, axis=0, keepdims=True) * (0.5 / d)
        dx, dg = _rms_bwd(xv, g_ref[...], err * (1.0 / d))
        dx_ref[...] = dx
        dg_ref[...] += dg

    return pl.pallas_call(
        body, grid=(s // tm,),
        in_specs=[pl.BlockSpec((tm, d), lambda i: (i, 0)), pl.BlockSpec((1, d), lambda i: (0, 0)),
                  pl.BlockSpec((tm, d), lambda i: (i, 0))],
        out_specs=[pl.BlockSpec((1, d), lambda i: (0, 0)), pl.BlockSpec((tm, d), lambda i: (i, 0)),
                   pl.BlockSpec((1, d), lambda i: (0, 0))],
        out_shape=[SDS((1, d), F32), SDS((s, d), F32), SDS((1, d), F32)], compiler_params=_cp("arbitrary"), name=name)(x, g, target)


def _me_and_peer():
    x, y, c = lax.axis_index("x"), lax.axis_index("y"), lax.axis_index("c")
    me = 4 * x + 2 * y + c

    def peer(k):
        px, py, pc = x ^ (k >> 2), y ^ ((k >> 1) & 1), c ^ (k & 1)
        return (px, py, pc), 4 * px + 2 * py + pc

    return me, peer


class _Comm:
    CHIPS = (2, 4, 6)

    def __init__(self, kind, arrs):
        assert kind in ("gather", "exchange")
        self.kind, self.arrs, self.n = kind, list(arrs), len(arrs)
        anyspec = pl.BlockSpec(memory_space=pl.ANY)
        self.in_specs = [anyspec] * self.n
        self.out_specs = [anyspec] * self.n
        self.out_shape = [SDS(((NDEV,) + a.shape) if kind == "gather" else a.shape, a.dtype) for a in self.arrs]
        npair = NDEV - 1 + len(self.CHIPS)
        self.scratch = [pltpu.SemaphoreType.DMA((self.n, npair)), pltpu.SemaphoreType.DMA((self.n, npair)),
                        pltpu.SemaphoreType.DMA((self.n,))]

    def _copies(self, ins, outs, sems):
        send, recv, loc = sems
        me, peer = _me_and_peer()
        gather = self.kind == "gather"
        sibling = peer(1)[0]
        local = [pltpu.make_async_copy(ins[a] if gather else ins[a].at[me], outs[a].at[me], loc.at[a]) for a in range(self.n)]
        outgoing, incoming, forwards, forwarded = [], [], [], []
        for k in ((1,) + self.CHIPS) if gather else range(1, NDEV):
            dev, pid = peer(k)
            for a in range(self.n):
                pair = dict(send_sem=send.at[a, k - 1], recv_sem=recv.at[a, k - 1], device_id=dev, device_id_type=MESH)
                outgoing.append(pltpu.make_async_remote_copy(src_ref=ins[a] if gather else ins[a].at[pid],
                                                             dst_ref=outs[a].at[me], **pair))
                incoming.append(pltpu.make_async_remote_copy(src_ref=ins[a] if gather else ins[a].at[me],
                                                             dst_ref=outs[a].at[pid], **pair))
        if gather:
            for idx, k in enumerate(self.CHIPS):
                got, theirs = peer(k)[1], peer(k + 1)[1]
                for a in range(self.n):
                    pair = dict(send_sem=send.at[a, NDEV - 1 + idx], recv_sem=recv.at[a, NDEV - 1 + idx], device_id=sibling,
                                device_id_type=MESH)
                    forwards.append(pltpu.make_async_remote_copy(src_ref=outs[a].at[got], dst_ref=outs[a].at[got], **pair))
                    forwarded.append(pltpu.make_async_remote_copy(src_ref=outs[a].at[theirs], dst_ref=outs[a].at[theirs], **pair))
        return local, outgoing, incoming, forwards, forwarded

    def start(self, ins, outs, sems):
        local, outgoing, _, _, _ = self._copies(ins, outs, sems)
        for cp in local + outgoing:
            cp.start()

    def forward(self, ins, outs, sems):
        _, _, incoming, forwards, _ = self._copies(ins, outs, sems)
        per = self.n
        for idx in range(len(forwards) // per if per else 0):
            for a in range(per):
                incoming[(1 + idx) * per + a].wait_recv()
                forwards[idx * per + a].start()

    def wait(self, ins, outs, sems):
        local, outgoing, incoming, forwards, forwarded = self._copies(ins, outs, sems)
        for cp in (incoming[:self.n] if self.kind == "gather" else incoming) + forwarded:
            cp.wait_recv()
        for cp in outgoing + forwards:
            cp.wait_send()
        for cp in local:
            cp.wait()


LATE_FORWARD_BYTES = 1 << 20


def _host_gather(comm, cc, step, nsteps, late=None):
    if comm is None:
        return
    if late is None:
        late = sum(a.size * a.dtype.itemsize for a in comm.arrs) >= LATE_FORWARD_BYTES

    @pl.when(step == 0)
    def _():
        comm.start(*cc)

    @pl.when(step == (nsteps - 1 if late else (2 * nsteps) // 3))
    def _():
        comm.forward(*cc)


def _split_refs(refs, n_in, n_out, comm):
    c = comm.n if comm is not None else 0
    ins, cin = refs[:n_in], refs[n_in:n_in + c]
    outs, cout = refs[n_in + c:n_in + c + n_out], refs[n_in + c + n_out:n_in + 2 * c + n_out]
    rest = refs[n_in + 2 * c + n_out:]
    scratch, csem = (rest[:len(rest) - 3], rest[len(rest) - 3:]) if c else (rest, ())
    return ins, outs, scratch, (cin, cout, csem)


def _comm_call(kind, arrs, name):
    comm = _Comm(kind, arrs)

    def body(*refs):
        _, _, _, c = _split_refs(refs, 0, 0, comm)
        comm.start(*c)
        if kind == "gather":
            comm.forward(*c)
        comm.wait(*c)

    return pl.pallas_call(body, in_specs=comm.in_specs, out_specs=comm.out_specs, out_shape=comm.out_shape,
                          scratch_shapes=comm.scratch, compiler_params=pltpu.CompilerParams(has_side_effects=True),
                          name=name)(*arrs)


def _all_gather(arrs, name):
    return _comm_call("gather", arrs, name)


def _exchange(arrs, name):
    return _comm_call("exchange", arrs, name)


def _sum_slots(parts, name):
    _, r, c = parts.shape
    tr = r if r <= 512 else 512

    def body(p_ref, o_ref):
        acc = p_ref[0].astype(F32)
        for q in range(1, NDEV):
            acc = acc + p_ref[q].astype(F32)
        o_ref[...] = acc

    return pl.pallas_call(
        body, grid=(r // tr,), in_specs=[pl.BlockSpec((NDEV, tr, c), lambda i: (0, i, 0))],
        out_specs=pl.BlockSpec((tr, c), lambda i: (i, 0)), out_shape=SDS((r, c), F32),
        compiler_params=_cp("parallel"), name=name)(parts)


def _adamw(g, w, m, v, name):
    r, c = w.shape
    parts = g.ndim == 3
    tr = r
    for cand in (512, 256, 128, 64, 32, 16, 8):
        if r > cand and r % cand == 0 and cand * c * 4 <= 2 * 1024 * 1024:
            tr = cand
            break
    bc1 = 1.0 / (1.0 - ADAM_B1 ** ADAM_STEP)
    bc2 = 1.0 / (1.0 - ADAM_B2 ** ADAM_STEP)

    def body(g_ref, w_ref, m_ref, v_ref, go_ref, d_ref, mo_ref, vo_ref):
        if parts:
            gv = g_ref[0].astype(F32)
            for q in range(1, NDEV):
                gv = gv + g_ref[q].astype(F32)
        else:
            gv = g_ref[...]
        mn = ADAM_B1 * m_ref[...] + (1.0 - ADAM_B1) * gv
        vn = ADAM_B2 * v_ref[...] + (1.0 - ADAM_B2) * (gv * gv)
        go_ref[...] = gv
        mo_ref[...] = mn
        vo_ref[...] = vn
        d_ref[...] = -ADAM_LR * ((mn * bc1) / (jnp.sqrt(vn * bc2) + ADAM_EPS) + ADAM_WD * w_ref[...])

    spec = pl.BlockSpec((tr, c), lambda i: (i, 0))
    gspec = pl.BlockSpec((NDEV, tr, c), lambda i: (0, i, 0)) if parts else spec
    return pl.pallas_call(
        body, grid=(r // tr,), in_specs=[gspec, spec, spec, spec], out_specs=[spec] * 4,
        out_shape=[SDS((r, c), F32)] * 4, compiler_params=_cp("parallel"), name=name)(g, w, m, v)


def _adamw_layer(parts, w, m, v, l, prev, name):
    r, c = parts.shape[1:]
    rows = w.shape[0]
    tr = next(t for t in (512, 256, 128, 64, 32, 16, 8) if r % t == 0 and t * c * 4 <= 2 * 1024 * 1024)
    bc1 = 1.0 / (1.0 - ADAM_B1 ** ADAM_STEP)
    bc2 = 1.0 / (1.0 - ADAM_B2 ** ADAM_STEP)

    def body(g_ref, w_ref, m_ref, v_ref, *rest):
        go_ref, d_ref, mo_ref, vo_ref = rest[-4:]
        gv = g_ref[0].astype(F32)
        for q in range(1, NDEV):
            gv = gv + g_ref[q].astype(F32)
        mn = ADAM_B1 * m_ref[...] + (1.0 - ADAM_B1) * gv
        vn = ADAM_B2 * v_ref[...] + (1.0 - ADAM_B2) * (gv * gv)
        go_ref[...] = gv
        mo_ref[...] = mn
        vo_ref[...] = vn
        d_ref[...] = -ADAM_LR * ((mn * bc1) / (jnp.sqrt(vn * bc2) + ADAM_EPS) + ADAM_WD * w_ref[...])

    spec = pl.BlockSpec((tr, c), lambda i: (l * (r // tr) + i, 0))
    in_specs = [pl.BlockSpec((NDEV, tr, c), lambda i: (0, i, 0)), spec, spec, spec]
    args = [parts, w, m, v]
    aliases = {}
    if prev is not None:
        in_specs += [pl.BlockSpec(memory_space=pl.ANY)] * 4
        args += list(prev)
        aliases = {4 + k: k for k in range(4)}
    return pl.pallas_call(
        body, grid=(r // tr,), in_specs=in_specs, out_specs=[spec] * 4, out_shape=[SDS((rows, c), F32)] * 4,
        input_output_aliases=aliases, compiler_params=_cp("parallel"), name=name)(*args)


def _pad_in_cols(w):
    r = w.shape[0]
    zeros = lambda n: jnp.zeros((r, n), w.dtype)
    return jnp.concatenate([w[:, 512:1536], w[:, 0:512], w[:, 1792:2304], w[:, 1536:1792], w[:, 2308:2692], w[:, 2304:2308],
                            zeros(28), w[:, 2692:2724], zeros(64)], axis=1)


def _unpad_in_cols(w):
    return jnp.concatenate([w[..., 1024:1536], w[..., 0:1024], w[..., 2048:2304], w[..., 1536:2048], w[..., 2688:2692],
                            w[..., 2304:2688], w[..., 2720:2752]], axis=-1)


_Z_RET = (0, 1, 2, 3)
_Z_SGU = (4, 5)
_Z_FOX_Q, _Z_FOX_K, _Z_FOX_V = 8, 6, 7
_DZ_RET, _DZ_SGU, _DZ_FOX_KV, _DZ_FOX_Q, _DZ_MLA, _DZ_MISC = 0, 2, 3, 8, 6, 21


def _pad_uq(w):
    return jnp.pad(w.reshape(256, N_HEADS, 96), ((0, 0), (0, 0), (0, 32))).reshape(256, 512)


def _unpad_uq(w):
    return w.reshape(256, N_HEADS, 128)[:, :, :96].reshape(256, 384)


def _split_ukv(w):
    r = w.reshape(128, N_HEADS, 128)
    return jnp.pad(r[:, :, :64], ((0, 0), (0, 0), (0, 64))).reshape(128, 512), r[:, :, 64:].reshape(128, 256)


def _join_ukv(dk, dv):
    return jnp.concatenate([dk.reshape(128, N_HEADS, 128)[:, :, :64], dv.reshape(128, N_HEADS, 64)], axis=-1).reshape(128, 512)


def _cols_to_full(g):
    return jnp.transpose(g, (1, 0, 2)).reshape(g.shape[1], NDEV * g.shape[2])


def kernel(x, g_mix_norm, w_in, b_forget, g_sgu, w_spatial, b_spatial, g_mla_q, w_uq, g_mla_kv, w_ukv, g_group_out, w_out, g_ffn_norm, w_up, w_down, g_final, loss_target, m_g_mix_norm, m_w_in, m_b_forget, m_g_sgu, m_w_spatial, m_b_spatial, m_g_mla_q, m_w_uq, m_g_mla_kv, m_w_ukv, m_g_group_out, m_w_out, m_g_ffn_norm, m_w_up, m_w_down, m_g_final, v_g_mix_norm, v_w_in, v_b_forget, v_g_sgu, v_w_spatial, v_b_spatial, v_g_mla_q, v_w_uq, v_g_mla_kv, v_w_ukv, v_g_group_out, v_w_out, v_g_ffn_norm, v_w_up, v_w_down, v_g_final):
    depth = w_in.shape[0]
    s, d = x.shape[1], x.shape[2]
    x0 = x.reshape(s, d)
    target = loss_target.reshape(s, d)
    tb = _tables(s)
    me = 4 * lax.axis_index("x") + 2 * lax.axis_index("y") + lax.axis_index("c")

    assert depth == 2
    shards = {}
    for l in range(depth):
        shards.update({(l, "w_in"): _pad_in_cols(w_in[l]).astype(_WIRE), (l, "w_out"): w_out[l].astype(_WIRE),
                       (l, "w_up"): w_up[l].astype(_WIRE), (l, "w_down"): w_down[l].astype(_WIRE),
                       (l, "w_uq"): w_uq[l].astype(_WIRE), (l, "w_ukv"): w_ukv[l].astype(_WIRE)})
    wts = _ShardedWeights(shards)
    first = [(0, "w_in"), (0, "w_uq"), (0, "w_ukv"), (1, "w_uq"), (1, "w_ukv")]
    wts.full.update(zip(first, _all_gather([shards[k] for k in first], "gather_first")))

    row = lambda a: a.reshape(1, -1)

    def small(l):
        bf = jnp.pad(b_forget[l].reshape(1, N_HEADS), ((0, 0), (0, 128 - N_HEADS)))
        bt = jnp.pad(b_spatial[l].T, ((0, 0), (0, 128 - N_HEADS)))
        return dict(g_mix=row(g_mix_norm[l]), g_sgu=row(g_sgu[l]), w_s=w_spatial[l], b_t=bt, b_f=bf, gq=row(g_mla_q[l]),
                    gkv=row(g_mla_kv[l]), g_go=row(g_group_out[l]), g_ffn=row(g_ffn_norm[l]))

    smalls = [small(l) for l in range(depth)]
    lrow, dx, sm, dg_final = _local_step(x0, target, wts, smalls, row(g_final), tb)
    loss = lax.psum(jnp.sum(lrow), AXES)
    grad_x = dx.reshape(1, s, d)
    return _reduce_and_update(loss, grad_x, wts.recv, sm, dg_final, me, dict(
        g_mix_norm=(g_mix_norm, m_g_mix_norm, v_g_mix_norm), w_in=(w_in, m_w_in, v_w_in),
        b_forget=(b_forget, m_b_forget, v_b_forget), g_sgu=(g_sgu, m_g_sgu, v_g_sgu),
        w_spatial=(w_spatial, m_w_spatial, v_w_spatial), b_spatial=(b_spatial, m_b_spatial, v_b_spatial),
        g_mla_q=(g_mla_q, m_g_mla_q, v_g_mla_q), w_uq=(w_uq, m_w_uq, v_w_uq), g_mla_kv=(g_mla_kv, m_g_mla_kv, v_g_mla_kv),
        w_ukv=(w_ukv, m_w_ukv, v_w_ukv), g_group_out=(g_group_out, m_g_group_out, v_g_group_out),
        w_out=(w_out, m_w_out, v_w_out), g_ffn_norm=(g_ffn_norm, m_g_ffn_norm, v_g_ffn_norm), w_up=(w_up, m_w_up, v_w_up),
        w_down=(w_down, m_w_down, v_w_down), g_final=(g_final, m_g_final, v_g_final)))


_GATHER_AT = {
    "in_proj0": [(0, "w_out")],
    "fox_attn0": [(0, "w_down")],
    "mla_attn0": [(0, "w_up"), (1, "w_in")],
    "ffn_fwd0": [(1, "w_down")],
    "fox_attn1": [(1, "w_out")],
    "mla_attn1": [(1, "w_up")],
}
_SCATTER_AT = {
    "fox_attn_bwd1": [(1, "w_down")],
    "mla_attn_bwd1": [(1, "w_up"), (1, "w_out")],
    "ffn_bwd0": [(1, "w_in")],
    "fox_attn_bwd0": [(0, "w_down")],
    "mla_attn_bwd0": [(0, "w_up"), (0, "w_out")],
    "in_proj_bwd0": [(0, "w_in")],
}


class _FullWeights:
    def __init__(self, per_layer):
        self.per_layer, self.grads = per_layer, {}

    def get(self, l, name):
        return self.per_layer[l][name]

    def comm(self, host):
        return None

    def done(self, host, results):
        pass

    def grad(self, l, name, blocks):
        self.grads[(l, name)] = blocks


class _ShardedWeights(_FullWeights):
    def __init__(self, shards):
        self.shards, self.full, self.grads, self.recv = shards, {}, {}, {}

    def get(self, l, name):
        if name in ("wk", "wv"):
            return _split_ukv(_cols_to_full(self.full[(l, "w_ukv")]))[0 if name == "wk" else 1]
        if name == "wq":
            return _pad_uq(_cols_to_full(self.full[(l, "w_uq")]))
        g = self.full[(l, name)]
        return g if name == "w_up" else g.reshape(NDEV * g.shape[1], g.shape[2])

    def comm(self, host):
        if host in _GATHER_AT:
            return _Comm("gather", [self.shards[k] for k in _GATHER_AT[host]])
        if host in _SCATTER_AT:
            return _Comm("exchange", [self.grads[k] for k in _SCATTER_AT[host]])
        return None

    def done(self, host, results):
        if host in _GATHER_AT:
            self.full.update(zip(_GATHER_AT[host], results))
        if host in _SCATTER_AT:
            self.recv.update(zip(_SCATTER_AT[host], results))


def _local_step(x0, target, wts, smalls, g_final, tb):
    depth = len(smalls)
    s, d = x0.shape
    saved = []
    xl = x0
    for l in range(depth):
        p = smalls[l]
        z, h, got = _norm_matmul(xl, p["g_mix"], wts.get(l, "w_in"), f"in_proj{l}", wts.comm(f"in_proj{l}"))
        wts.done(f"in_proj{l}", got)
        ya = _sgu_fwd(z, p["g_sgu"], p["w_s"], p["b_t"], tb, f"sgu_fwd{l}")
        yb, ret, states = _ret_fwd(z, tb, f"ret_fwd{l}")
        cum = _fox_prep(z, p["b_f"], f"fox_prep{l}")
        kc, vc, vtc, qtc = _kv_prep(z, _Z_FOX_Q, _Z_FOX_K, _Z_FOX_V, HEAD_DIM ** -0.5, f"fox_kv{l}")
        yc, lse_c, got = _attn_fwd(z, _Z_FOX_Q, HEAD_DIM, kc, vtc, HEAD_DIM ** -0.5, cum, f"fox_attn{l}", wts.comm(f"fox_attn{l}"))
        wts.done(f"fox_attn{l}", got)
        wq, wk, wv = wts.get(l, "wq"), wts.get(l, "wk"), wts.get(l, "wv")
        qd, kd, vd, vtd, cqn, ckvn, qtd = _mla_prep(z, p["gq"], p["gkv"], wq, wk, wv, tb, f"mla_prep{l}")
        yd, lse_d, got = _attn_fwd(qd, 0, 128, kd, vtd, _SCALE_D, None, f"mla_attn{l}", wts.comm(f"mla_attn{l}"))
        wts.done(f"mla_attn{l}", got)
        ys = (ya, yb, yc, yd)
        x1, yn = _out_proj(ys, p["g_go"], wts.get(l, "w_out"), xl, f"out_proj{l}")
        x2, u, h2, got = _ffn_fwd(x1, p["g_ffn"], wts.get(l, "w_up"), wts.get(l, "w_down"), f"ffn_fwd{l}", wts.comm(f"ffn_fwd{l}"))
        wts.done(f"ffn_fwd{l}", got)
        saved.append(dict(x=xl, z=z, h=h, ys=ys, ret=ret, states=states, cum=cum, lse_c=lse_c, kc=kc, vc=vc, qd=qd, kd=kd, vd=vd,
                          cqn=cqn, ckvn=ckvn, lse_d=lse_d, x1=x1, yn=yn, u=u, h2=h2, wq=wq, wk=wk, wv=wv, qtc=qtc, qtd=qtd))
        xl = x2

    lrow, dx, dg_final = _loss_head(xl, g_final, target, "loss_head")

    sm = [None] * depth
    for l in reversed(range(depth)):
        p, a = smalls[l], saved[l]
        dx1, du, dg_ffn, got = _ffn_bwd(dx, a["x1"], a["u"], p["g_ffn"], wts.get(l, "w_up"), wts.get(l, "w_down"), f"ffn_bwd{l}",
                                        wts.comm(f"ffn_bwd{l}"))
        wts.done(f"ffn_bwd{l}", got)
        dw_down = _mm_tn(a["u"], dx, f"dw_down{l}", a_fn=lambda t: jnp.square(jnp.maximum(t, 0.0)), out_dtype=_WIRE)
        wts.grad(l, "w_down", dw_down.reshape(NDEV, dw_down.shape[0] // NDEV, d))
        wts.grad(l, "w_up", _mm_tn(a["h2"], du, f"dw_up{l}", blocked=True, out_dtype=_WIRE))
        dya, dyb, dg_go, dot_c, dot_d, dl_c, dl_d = _out_proj_bwd(dx1, wts.get(l, "w_out"), a["ys"], p["g_go"],
                                                                  f"out_proj_bwd{l}")
        wts.grad(l, "w_out", _mm_tn(a["yn"], dx1, f"dw_out{l}", out_dtype=_WIRE).reshape(NDEV, d // NDEV, d))
        dz = _ret_bwd(dyb, a["z"], a["ret"], a["states"], tb, f"ret_bwd{l}")
        dz, dg_sgu, dw_s, db_t = _sgu_bwd(dya, a["z"], p["g_sgu"], p["w_s"], p["b_t"], tb, dz, f"sgu_bwd{l}")
        dqt_c, dz, dck, dcq, got = _attn_bwd(a["kc"], a["vc"], a["qtc"], dot_c, a["lse_c"], dl_c, HEAD_DIM,
                                             HEAD_DIM ** -0.5, a["cum"], f"fox_attn_bwd{l}", _MXU,
                                             wts.comm(f"fox_attn_bwd{l}"), kv_into=dz)
        wts.done(f"fox_attn_bwd{l}", got)
        dz = _untranspose(dqt_c, _MXU, f"fox_dq{l}", into=dz, col=_DZ_FOX_Q)
        dqt_d, dk_d, dv_d, got = _attn_bwd(a["kd"], a["vd"], a["qtd"], dot_d, a["lse_d"], dl_d, 128, _SCALE_D, None,
                                           f"mla_attn_bwd{l}", F32, wts.comm(f"mla_attn_bwd{l}"))
        wts.done(f"mla_attn_bwd{l}", got)
        dz, dkr, dwq, dwk, dwv, dgq, dgkv = _mla_prep_bwd(dqt_d, dk_d, dv_d, a["z"], a["cqn"], a["ckvn"], p["gq"], p["gkv"],
                                                          a["wq"], a["wk"], a["wv"], tb, dz, f"mla_prep_bwd{l}")
        dz, db_f = _fox_post(dcq, dck, a["z"], p["b_f"], dkr, dz, f"fox_post{l}")
        wts.grad(l, "w_in", _unpad_in_cols(_mm_tn(a["h"], dz, f"dw_in{l}", out_dtype=_WIRE)).reshape(NDEV, d // NDEV, N_IN))
        dx, dg_mix, got = _in_proj_bwd(dz, wts.get(l, "w_in"), a["x"], p["g_mix"], dx1, f"in_proj_bwd{l}",
                                       wts.comm(f"in_proj_bwd{l}"))
        wts.done(f"in_proj_bwd{l}", got)
        sm[l] = [dg_mix, dg_go, dg_ffn, dg_sgu, dw_s, db_t[:, :N_HEADS].T, db_f[0, :N_HEADS], dgq, dgkv, _unpad_uq(dwq),
                 _join_ukv(dwk, dwv)]
    return lrow, dx, sm, dg_final


def _reduce_and_update(loss, grad_x, recv, sm, dg_final, me, given):
    depth = len(sm)
    pieces = [t for l in range(depth) for t in sm[l]] + [dg_final]
    flat = jnp.concatenate([t.reshape(-1) for t in pieces])
    n_flat = flat.shape[0]
    unit = NDEV * 8 * 128
    n_pad = -(-n_flat // unit) * unit
    packed = jnp.pad(flat, (0, n_pad - n_flat)).reshape(NDEV, n_pad // (NDEV * 128), 128)
    red = _sum_slots(_exchange([packed], "scatter_small")[0], "sum_small")
    full = _all_gather([red], "gather_small")[0].reshape(-1)
    offs = np.cumsum([0] + [int(np.prod(t.shape)) for t in pieces])
    red_pieces = [full[int(offs[i]):int(offs[i + 1])].reshape(pieces[i].shape) for i in range(len(pieces))]
    per = len(sm[0])
    stack = lambda i: jnp.stack([red_pieces[l * per + i] for l in range(depth)])
    g_small = dict(g_mix_norm=stack(0), g_group_out=stack(1), g_ffn_norm=stack(2), g_sgu=stack(3), w_spatial=stack(4),
                   b_spatial=stack(5), b_forget=stack(6), g_mla_q=stack(7), g_mla_kv=stack(8), g_final=red_pieces[-1])
    cq, ckv = given["w_uq"][0].shape[2], given["w_ukv"][0].shape[2]
    g_small["w_uq"] = lax.dynamic_slice_in_dim(stack(9), me * cq, cq, axis=2)
    g_small["w_ukv"] = lax.dynamic_slice_in_dim(stack(10), me * ckv, ckv, axis=2)

    names = list(given)
    outs = {}
    for nme in names:
        wv_, mv_, vv_ = given[nme]
        shape = wv_.shape
        if nme in ("w_in", "w_out", "w_up", "w_down"):
            res = None
            flat2 = lambda t: t.reshape(-1, shape[-1])
            for l in range(depth):
                res = _adamw_layer(recv[(l, nme)], flat2(wv_), flat2(mv_), flat2(vv_), l, res, f"adamw_{nme}{l}")
            outs[nme] = [t.reshape(shape) for t in res]
        else:
            two = lambda t: t.reshape(-1, shape[-1]) if t.ndim > 1 else t.reshape(1, -1)
            res = _adamw(two(g_small[nme]), two(wv_), two(mv_), two(vv_), f"adamw_{nme}")
            outs[nme] = [r.reshape(shape) for r in res]
    return (loss, grad_x, *[outs[n][0] for n in names], *[outs[n][1] for n in names], *[outs[n][2] for n in names],
            *[outs[n][3] for n in names])
```

```python
import functools

import jax
import jax.numpy as jnp
import numpy as np
from jax import lax
from jax.experimental import pallas as pl
from jax.experimental.pallas import tpu as pltpu

F32 = jnp.float32
_MXU = jnp.bfloat16
_WIRE = jnp.bfloat16
EPS = 1e-6
NDEV = 8
AXES = ("x", "y", "c")
MESH = pl.DeviceIdType.MESH

N_HEADS = 4
HEAD_DIM = 64
GROUP = 256
CHUNK = 128
NZ = 2816
N_IN = 2724
MISC_F, MISC_KR = 0, 32
VMEM_LIMIT = 56 * 1024 * 1024

ADAM_LR, ADAM_B1, ADAM_B2, ADAM_EPS, ADAM_WD, ADAM_STEP = 0.001, 0.9, 0.999, 1e-08, 0.01, 10

SDS = jax.ShapeDtypeStruct


def _cp(*sem):
    return pltpu.CompilerParams(dimension_semantics=sem, vmem_limit_bytes=VMEM_LIMIT)


def _dot(a, b):
    return jnp.dot(a.astype(_MXU), b.astype(_MXU), preferred_element_type=F32)


def _dot_nt(a, b):
    return lax.dot_general(a.astype(_MXU), b.astype(_MXU), (((1,), (1,)), ((), ())), preferred_element_type=F32)


def _dot_tn(a, b):
    return lax.dot_general(a.astype(_MXU), b.astype(_MXU), (((0,), (0,)), ((), ())), preferred_element_type=F32)


def _dot_exact(a, b, dims=(((1,), (0,)), ((), ()))):
    return lax.dot_general(a, b, dims, precision=lax.Precision.HIGHEST, preferred_element_type=F32)


def _rms(x, g):
    return x * lax.rsqrt(jnp.mean(x * x, axis=-1, keepdims=True) + EPS) * g


def _rms_bwd(x, g, dy):
    xh = x * lax.rsqrt(jnp.mean(x * x, axis=-1, keepdims=True) + EPS)
    dxh = dy * g
    r = lax.rsqrt(jnp.mean(x * x, axis=-1, keepdims=True) + EPS)
    dx = r * (dxh - xh * jnp.mean(dxh * xh, axis=-1, keepdims=True))
    return dx, jnp.sum(dy * xh, axis=0, keepdims=True)


_GELU_C = 0.7978845608028654


def _gelu(x):
    return 0.5 * x * (1.0 + jnp.tanh(_GELU_C * (x + 0.044715 * x * x * x)))


def _gelu_grad(x):
    t = jnp.tanh(_GELU_C * (x + 0.044715 * x * x * x))
    return 0.5 * (1.0 + t) + 0.5 * x * (1.0 - t * t) * _GELU_C * (1.0 + 3 * 0.044715 * x * x)


def _sigmoid(x):
    return 1.0 / (1.0 + jnp.exp(-x))


def _swap_half(t, half):
    n = t.shape[-1]
    lane = lax.broadcasted_iota(jnp.int32, t.shape, t.ndim - 1)
    return jnp.where((lane % (2 * half)) < half, pltpu.roll(t, n - half, t.ndim - 1), pltpu.roll(t, half, t.ndim - 1))


def _lanes(table, width):
    return jnp.concatenate([table] * (width // table.shape[-1]), axis=-1)


def _rope(t, cos, sin, half):
    return t * cos + _swap_half(t, half) * sin


def _rope_bwd(d, cos, sin, half):
    return d * cos - _swap_half(d, half) * sin


def _tables(s):
    pos = jnp.arange(s, dtype=F32)[:, None]

    def cs(half):
        inv = jnp.power(10000.0, -jnp.arange(half, dtype=F32) / half)
        ang = pos * inv[None, :]
        return jnp.cos(ang), jnp.sin(ang)

    c32, s32 = cs(32)
    c16, s16 = cs(16)
    z = lambda w: jnp.zeros((s, w), F32)
    o = lambda w: jnp.ones((s, w), F32)
    t = {}
    t["b_cos"] = jnp.concatenate([c32, c32, c32, c32], 1)
    t["b_sin"] = jnp.concatenate([-s32, s32, -s32, s32], 1)
    t["q_cos"] = jnp.concatenate([o(64), c16, c16, z(32)], 1)
    t["q_sin"] = jnp.concatenate([z(64), -s16, s16, z(32)], 1)
    t["k_cos"] = jnp.concatenate([z(32), c16, c16, z(64)], 1)
    t["k_sin"] = jnp.concatenate([z(32), -s16, s16, z(64)], 1)
    lg = jnp.log1p(-jnp.exp2(-5.0 - jnp.arange(N_HEADS, dtype=F32)))
    j = jnp.arange(CHUNK, dtype=F32)
    rel = j[:, None] - j[None, :]
    t["decay"] = jnp.where(rel[None] >= 0, jnp.exp(jnp.maximum(rel, 0.0)[None] * lg[:, None, None]), 0.0)
    t["decay_t"] = jnp.swapaxes(t["decay"], 1, 2)

    def rows(e):
        return jnp.repeat(e.T, HEAD_DIM, axis=1)

    t["qw"] = rows(jnp.exp((j + 1.0)[None, :] * lg[:, None]))
    t["kw"] = rows(jnp.exp((CHUNK - 1 - j)[None, :] * lg[:, None]))
    t["kw2"] = rows(jnp.exp((CHUNK - j)[None, :] * lg[:, None]))
    t["qw0"] = rows(jnp.exp(j[None, :] * lg[:, None]))
    t["cd"] = jnp.repeat(jnp.exp(CHUNK * lg), HEAD_DIM)[None, :]
    e = np.zeros((128, 512), np.float32)
    for h in range(N_HEADS):
        for r in range(32):
            e[MISC_KR + r, 128 * h + 64 + r] = 1.0
    t["place"] = jnp.asarray(e)
    lane_head = np.arange(GROUP) // HEAD_DIM
    t["grp"] = jnp.asarray((lane_head[:, None] == lane_head[None, :]) / HEAD_DIM, _MXU)
    hsel = (np.arange(128)[:, None] == lane_head[None, :]).astype(np.float32)
    t["hsel"] = jnp.asarray(hsel)
    t["hselt"] = jnp.asarray(hsel.T, _MXU)
    return t


def _norm_matmul(x, g, w, name, comm=None):
    s, d = x.shape
    n = w.shape[1]
    tm, tn = min(512, s), 256
    ni = s // tm

    def body(*refs):
        (x_ref, g_ref, w_ref), (z_ref, h_ref), _, cc = _split_refs(refs, 3, 2, comm)
        i = pl.program_id(0)
        _host_gather(comm, cc, i, ni, late=True)
        h = _rms(x_ref[...], g_ref[...]).astype(h_ref.dtype)
        h_ref[...] = h
        for j in range(n // tn):
            z_ref[:, tn * j:tn * (j + 1)] = jnp.dot(h, w_ref[:, tn * j:tn * (j + 1)], preferred_element_type=F32)
        if comm is not None:
            @pl.when(i == ni - 1)
            def _():
                comm.wait(*cc)

    in_specs = [pl.BlockSpec((tm, d), lambda i: (i, 0)), pl.BlockSpec((1, d), lambda i: (0, 0)),
                pl.BlockSpec((d, n), lambda i: (0, 0))]
    out_specs = [pl.BlockSpec((tm, n), lambda i: (i, 0)), pl.BlockSpec((tm, d), lambda i: (i, 0))]
    out_shape = [SDS((s, n), F32), SDS((s, d), _MXU)]
    return _call_with_comm(body, (ni,), in_specs, out_specs, out_shape, [], [x, g, w], comm, ("arbitrary",), name)


def _mm_tn(a, b, name, *, a_fn=None, blocked=False, out_dtype=F32):
    k, m = a.shape
    n = b.shape[1]
    tm, tk = min(1024, m), min(1024, k)
    tn = next(t for t in (2816, 1024, 512, 256, 128) if n % t == 0)
    assert m % tm == 0 and k % tk == 0
    nk = k // tk

    def body(a_ref, b_ref, o_ref, acc):
        kk = pl.program_id(2)

        @pl.when(kk == 0)
        def _():
            acc[...] = jnp.zeros_like(acc)

        av = a_ref[...]
        if a_fn is not None:
            av = a_fn(av.astype(F32))
        acc[...] += _dot_tn(av, b_ref[...])

        @pl.when(kk == nk - 1)
        def _():
            if blocked:
                for c in range(tn // 512):
                    o_ref[c] = acc[:, 512 * c:512 * (c + 1)].astype(o_ref.dtype)
            else:
                o_ref[...] = acc[...].astype(o_ref.dtype)

    if blocked:
        assert tn % 512 == 0
        out_spec = pl.BlockSpec((tn // 512, tm, 512), lambda i, j, kk: (j, i, 0))
        out_shape = SDS((n // 512, m, 512), out_dtype)
    else:
        out_spec = pl.BlockSpec((tm, tn), lambda i, j, kk: (i, j))
        out_shape = SDS((m, n), out_dtype)
    return pl.pallas_call(
        body, grid=(m // tm, n // tn, nk),
        in_specs=[pl.BlockSpec((tk, tm), lambda i, j, kk: (kk, i)), pl.BlockSpec((tk, tn), lambda i, j, kk: (kk, j))],
        out_specs=out_spec, out_shape=out_shape, scratch_shapes=[pltpu.VMEM((tm, tn), F32)],
        compiler_params=_cp("parallel", "parallel", "arbitrary"), name=name)(a, b)


def _split_dot(x, m):
    hi = x.astype(_MXU)
    lo = (x - hi.astype(F32)).astype(_MXU)
    return jnp.dot(hi, m, preferred_element_type=F32) + jnp.dot(lo, m, preferred_element_type=F32)


def _gstandardize(t, grp):
    tc = t - _split_dot(t, grp)
    rs = lax.rsqrt(_split_dot(tc * tc, grp) + EPS)
    return tc * rs, rs


def _gstandardize_bwd(yh, rs, dy, grp):
    return rs * (dy - _split_dot(dy, grp) - yh * _split_dot(dy * yh, grp))


def _head_select(parts):
    hid = lax.broadcasted_iota(jnp.int32, parts[0].shape, 1) // HEAD_DIM
    return jnp.where(hid == 0, parts[0], jnp.where(hid == 1, parts[1], jnp.where(hid == 2, parts[2], parts[3])))


def _head_masked(x):
    hid = lax.broadcasted_iota(jnp.int32, x.shape, 1) // HEAD_DIM
    return [jnp.where(hid == h, x, jnp.zeros_like(x)) for h in range(N_HEADS)]


def _tril(w):
    r = lax.broadcasted_iota(jnp.int32, w.shape, 0)
    c = lax.broadcasted_iota(jnp.int32, w.shape, 1)
    return jnp.where(r >= c, w, 0.0)


def _sgu_mixed(vgb, wcs, bias, nchunk):
    ms = [[jnp.dot(wcs[h], vgb[CHUNK * c:CHUNK * (c + 1)], preferred_element_type=F32) for h in range(N_HEADS)]
          for c in range(nchunk)]
    return [_head_select(ms[c]) + bias for c in range(nchunk)]


def _sgu_fwd(z, gain, w_s, b_t, tb, name):
    s = z.shape[0]
    tm = min(512, s)
    const = lambda a: pl.BlockSpec(a.shape, lambda i: (0,) * a.ndim)

    def body(u_ref, v_ref, g_ref, w_ref, b_ref, grp_ref, hsel_ref, y_ref):
        u = _gelu(u_ref[...])
        vh, _ = _gstandardize(_gelu(v_ref[...]), grp_ref[...])
        vgb = (vh * g_ref[...]).astype(_MXU)
        bias = _dot_exact(b_ref[...], hsel_ref[...])
        wcs = [_tril(w_ref[h]).astype(_MXU) for h in range(N_HEADS)]
        for c, mixed in enumerate(_sgu_mixed(vgb, wcs, bias, tm // CHUNK)):
            r = slice(CHUNK * c, CHUNK * (c + 1))
            y_ref[r, :] = u[r] * mixed

    return pl.pallas_call(
        body, grid=(s // tm,),
        in_specs=[pl.BlockSpec((tm, GROUP), lambda i: (i, _Z_SGU[0])), pl.BlockSpec((tm, GROUP), lambda i: (i, _Z_SGU[1])),
                  pl.BlockSpec((1, GROUP), lambda i: (0, 0)), pl.BlockSpec((N_HEADS, CHUNK, CHUNK), lambda i: (0, 0, 0)),
                  pl.BlockSpec((CHUNK, 128), lambda i: (0, 0)), const(tb["grp"]), const(tb["hsel"])],
        out_specs=pl.BlockSpec((tm, GROUP), lambda i: (i, 0)), out_shape=SDS((s, GROUP), F32),
        compiler_params=_cp("parallel"), name=name)(z, z, gain, w_s, b_t, tb["grp"], tb["hsel"])


def _sgu_bwd(dy, z, gain, w_s, b_t, tb, dz, name):
    s = z.shape[0]
    tm = min(512, s)
    nchunk = tm // CHUNK
    const = lambda a: pl.BlockSpec(a.shape, lambda i: (0,) * a.ndim)

    def body(dy_ref, u_ref, v_ref, g_ref, w_ref, b_ref, grp_ref, hsel_ref, hselt_ref, _, dz_ref, dg_ref, dw_ref, db_ref):
        @pl.when(pl.program_id(0) == 0)
        def _():
            dg_ref[...] = jnp.zeros_like(dg_ref)
            dw_ref[...] = jnp.zeros_like(dw_ref)
            db_ref[...] = jnp.zeros_like(db_ref)

        grp = grp_ref[...]
        u_pre, v_pre, gain_v = u_ref[...], v_ref[...], g_ref[...]
        u = _gelu(u_pre)
        vh, rs = _gstandardize(_gelu(v_pre), grp)
        vgb = (vh * gain_v).astype(_MXU)
        dyv = dy_ref[...]
        bias = _dot_exact(b_ref[...], hsel_ref[...])
        wfs = [_tril(w_ref[h]) for h in range(N_HEADS)]
        wcs = [w.astype(_MXU) for w in wfs]
        wts = [w.T.astype(_MXU) for w in wfs]
        mixed = _sgu_mixed(vgb, wcs, bias, nchunk)
        gu = _gelu_grad(u_pre)
        dms, dmh = [], []
        for c in range(nchunk):
            r = slice(CHUNK * c, CHUNK * (c + 1))
            dz_ref[r, 0:GROUP] = (dyv[r] * mixed[c] * gu[r]).astype(dz_ref.dtype)
            dm = dyv[r] * u[r]
            dms.append(dm)
            dmh.append([m.astype(_MXU) for m in _head_masked(dm)])
        dws = [sum(lax.dot_general(dmh[c][h], vgb[CHUNK * c:CHUNK * (c + 1)], (((1,), (1,)), ((), ())),
                                   preferred_element_type=F32) for c in range(nchunk)) for h in range(N_HEADS)]
        dvg = jnp.concatenate([sum(jnp.dot(wts[h], dmh[c][h], preferred_element_type=F32) for h in range(N_HEADS))
                               for c in range(nchunk)], axis=0)
        for h in range(N_HEADS):
            dw_ref[h] += _tril(dws[h])
        db_ref[...] += sum(_split_dot(dm, hselt_ref[...]) for dm in dms)
        dg_ref[...] += jnp.sum(dvg * vh, axis=0, keepdims=True)
        dv = _gstandardize_bwd(vh, rs, dvg * gain_v, grp)
        dz_ref[:, GROUP:2 * GROUP] = (dv * _gelu_grad(v_pre)).astype(dz_ref.dtype)

    consts = [tb["grp"], tb["hsel"], tb["hselt"]]
    return pl.pallas_call(
        body, grid=(s // tm,),
        in_specs=[pl.BlockSpec((tm, GROUP), lambda i: (i, 0)),
                  pl.BlockSpec((tm, GROUP), lambda i: (i, _Z_SGU[0])), pl.BlockSpec((tm, GROUP), lambda i: (i, _Z_SGU[1])),
                  pl.BlockSpec((1, GROUP), lambda i: (0, 0)), pl.BlockSpec((N_HEADS, CHUNK, CHUNK), lambda i: (0, 0, 0)),
                  pl.BlockSpec((CHUNK, 128), lambda i: (0, 0))] + [const(a) for a in consts]
        + [pl.BlockSpec(memory_space=pl.ANY)],
        out_specs=[pl.BlockSpec((tm, 2 * GROUP), lambda i: (i, _DZ_SGU)), pl.BlockSpec((1, GROUP), lambda i: (0, 0)),
                   pl.BlockSpec((N_HEADS, CHUNK, CHUNK), lambda i: (0, 0, 0)), pl.BlockSpec((CHUNK, 128), lambda i: (0, 0))],
        out_shape=[SDS(dz.shape, dz.dtype), SDS((1, GROUP), F32), SDS((N_HEADS, CHUNK, CHUNK), F32), SDS((CHUNK, 128), F32)],
        input_output_aliases={9: 0}, compiler_params=_cp("arbitrary"), name=name)(dy, z, z, gain, w_s, b_t, *consts, dz)


_SCALE_B = HEAD_DIM ** -0.5
RET_CHUNKS = 8
RET_CHUNKS_BWD = 4


def _block_diag(compact):
    full = jnp.concatenate([compact] * N_HEADS, axis=0)
    r = lax.broadcasted_iota(jnp.int32, full.shape, 0) // HEAD_DIM
    c = lax.broadcasted_iota(jnp.int32, full.shape, 1) // HEAD_DIM
    return jnp.where(r == c, full, 0.0)


def _diag_blocks(full):
    c = lax.broadcasted_iota(jnp.int32, (HEAD_DIM, GROUP), 1) // HEAD_DIM
    return sum(jnp.where(c == h, full[HEAD_DIM * h:HEAD_DIM * (h + 1), :], 0.0) for h in range(N_HEADS))


def _ret_fwd(z, tb, name):
    s = z.shape[0]
    nc = s // CHUNK
    per = min(RET_CHUNKS, nc)
    rows = per * CHUNK
    row = lambda col: pl.BlockSpec((rows, GROUP), lambda n, col=col: (n, col))
    const = lambda shape: pl.BlockSpec(shape, lambda n: (0,) * len(shape))

    def body(q_ref, k_ref, v_ref, g_ref, cos_ref, sin_ref, dec_ref, qw_ref, kw_ref, cd_ref, grp_ref, y_ref, o_ref, st_ref, state):
        @pl.when(pl.program_id(0) == 0)
        def _():
            state[...] = jnp.zeros_like(state)

        cos, sin = _lanes(cos_ref[...], GROUP), _lanes(sin_ref[...], GROUP)
        q = _rope(q_ref[...], cos, sin, 32)
        k = _rope(k_ref[...], cos, sin, 32) * _SCALE_B
        v = v_ref[...]
        g = g_ref[...]
        rcs = [slice(CHUNK * c, CHUNK * (c + 1)) for c in range(per)]
        vms = [[t.astype(_MXU) for t in _head_masked(v[r])] for r in rcs]
        scs = [[_dot_nt(t.astype(_MXU), k[r]) for t in _head_masked(q[r])] for r in rcs]
        kvs = [_dot_tn(k[r] * kw_ref[...], v[r]) for r in rcs]
        st = state[...]
        crosses = []
        for c, r in enumerate(rcs):
            st_ref[c] = st
            crosses.append(_dot(q[r] * qw_ref[...], _block_diag(st)))
            st = cd_ref[...] * st + _diag_blocks(kvs[c])
        state[...] = st
        outs = []
        for c in range(per):
            scd = [(scs[c][h] * dec_ref[h]).astype(_MXU) for h in range(N_HEADS)]
            outs.append(crosses[c] + sum(jnp.dot(scd[h], vms[c][h], preferred_element_type=F32) for h in range(N_HEADS)))
        o = jnp.concatenate(outs, axis=0)
        o_ref[...] = o
        yh, _ = _gstandardize(o, grp_ref[...])
        y_ref[...] = g * _sigmoid(g) * yh

    return pl.pallas_call(
        body, grid=(nc // per,),
        in_specs=[row(_Z_RET[0]), row(_Z_RET[1]), row(_Z_RET[2]), row(_Z_RET[3]), pl.BlockSpec((rows, 128), lambda n: (n, 0)),
                  pl.BlockSpec((rows, 128), lambda n: (n, 0)), const((N_HEADS, CHUNK, CHUNK)),
                  const((CHUNK, GROUP)), const((CHUNK, GROUP)), const((1, GROUP)), const((GROUP, GROUP))],
        out_specs=[pl.BlockSpec((rows, GROUP), lambda n: (n, 0)), pl.BlockSpec((rows, GROUP), lambda n: (n, 0)),
                   pl.BlockSpec((per, HEAD_DIM, GROUP), lambda n: (n, 0, 0))],
        out_shape=[SDS((s, GROUP), F32), SDS((s, GROUP), F32), SDS((nc, HEAD_DIM, GROUP), F32)],
        scratch_shapes=[pltpu.VMEM((HEAD_DIM, GROUP), F32)],
        compiler_params=_cp("arbitrary"), name=name)(z, z, z, z, tb["b_cos"], tb["b_sin"], tb["decay"], tb["qw"], tb["kw"], tb["cd"],
                                                       tb["grp"])


def _ret_bwd(dy, z, o_pre, states, tb, name):
    s = z.shape[0]
    nc = s // CHUNK
    per = min(RET_CHUNKS_BWD, nc)
    rows = per * CHUNK
    ns = nc // per
    rev = lambda col: pl.BlockSpec((rows, GROUP), lambda n, col=col: (ns - 1 - n, col))
    const = lambda shape: pl.BlockSpec(shape, lambda n: (0,) * len(shape))

    def body(dy_ref, q_ref, k_ref, v_ref, g_ref, o_ref, st_ref, cos_ref, sin_ref, dec_ref, dect_ref, qw_ref, kw2_ref, qw0_ref,
             cd_ref, grp_ref, dz_ref, rstate):
        @pl.when(pl.program_id(0) == 0)
        def _():
            rstate[...] = jnp.zeros_like(rstate)

        cos, sin = _lanes(cos_ref[...], GROUP), _lanes(sin_ref[...], GROUP)
        q = _rope(q_ref[...], cos, sin, 32)
        k = _rope(k_ref[...], cos, sin, 32) * _SCALE_B
        v = v_ref[...]
        g = g_ref[...]
        dyv = dy_ref[...]
        sg = _sigmoid(g)
        yh, rs = _gstandardize(o_ref[...], grp_ref[...])
        dz_ref[:, 3 * GROUP:4 * GROUP] = (dyv * yh * (sg * (1.0 + g * (1.0 - sg)))).astype(dz_ref.dtype)
        do = _gstandardize_bwd(yh, rs, dyv * (g * sg), grp_ref[...])
        hs = range(N_HEADS)
        rcs = [slice(CHUNK * c, CHUNK * (c + 1)) for c in range(per)]
        mask = lambda t: [m.astype(_MXU) for m in _head_masked(t)]
        qms, kms, vms, doms = ([mask(t[r]) for r in rcs] for t in (q, k, v, do))
        dps = [[_dot_nt(doms[c][h], v[r]) for h in hs] for c, r in enumerate(rcs)]
        pts = [[_dot_nt(kms[c][h], q[r]) for h in hs] for c, r in enumerate(rcs)]
        dpts = [[_dot_nt(vms[c][h], do[r]) for h in hs] for c, r in enumerate(rcs)]
        dq_x = [_dot_nt(do[r] * qw_ref[...], _block_diag(st_ref[c])) for c, r in enumerate(rcs)]
        r_new = [_dot_tn(q[r] * qw0_ref[...], do[r]) for r in rcs]
        rr = rstate[...]
        dk_x, dv_x = [None] * per, [None] * per
        for c in reversed(range(per)):
            r_bd = _block_diag(rr)
            dk_x[c] = _dot_nt(v[rcs[c]] * kw2_ref[...], r_bd)
            dv_x[c] = _dot(k[rcs[c]] * kw2_ref[...], r_bd)
            rr = cd_ref[...] * rr + _diag_blocks(r_new[c])
        rstate[...] = rr
        dqs, dks, dvs = [], [], []
        for c in range(per):
            dpd = [(dps[c][h] * dec_ref[h]).astype(_MXU) for h in hs]
            dptd = [(dpts[c][h] * dect_ref[h]).astype(_MXU) for h in hs]
            ptd = [(pts[c][h] * dect_ref[h]).astype(_MXU) for h in hs]
            dqs.append(dq_x[c] + sum(jnp.dot(dpd[h], kms[c][h], preferred_element_type=F32) for h in hs))
            dks.append(dk_x[c] + sum(jnp.dot(dptd[h], qms[c][h], preferred_element_type=F32) for h in hs))
            dvs.append(dv_x[c] + sum(jnp.dot(ptd[h], doms[c][h], preferred_element_type=F32) for h in hs))
        dz_ref[:, 0:GROUP] = _rope_bwd(jnp.concatenate(dqs, axis=0), cos, sin, 32).astype(dz_ref.dtype)
        dz_ref[:, GROUP:2 * GROUP] = _rope_bwd(jnp.concatenate(dks, axis=0) * _SCALE_B, cos, sin, 32).astype(dz_ref.dtype)
        dz_ref[:, 2 * GROUP:3 * GROUP] = jnp.concatenate(dvs, axis=0).astype(dz_ref.dtype)

    r0 = lambda: pl.BlockSpec((rows, GROUP), lambda n: (ns - 1 - n, 0))
    r128 = lambda: pl.BlockSpec((rows, 128), lambda n: (ns - 1 - n, 0))
    return pl.pallas_call(
        body, grid=(ns,),
        in_specs=[r0(), rev(_Z_RET[0]), rev(_Z_RET[1]), rev(_Z_RET[2]), rev(_Z_RET[3]), r0(),
                  pl.BlockSpec((per, HEAD_DIM, GROUP), lambda n: (ns - 1 - n, 0, 0)),
                  r128(), r128(), const((N_HEADS, CHUNK, CHUNK)), const((N_HEADS, CHUNK, CHUNK)), const((CHUNK, GROUP)),
                  const((CHUNK, GROUP)), const((CHUNK, GROUP)), const((1, GROUP)), const((GROUP, GROUP))],
        out_specs=pl.BlockSpec((rows, 4 * GROUP), lambda n: (ns - 1 - n, _DZ_RET)),
        out_shape=SDS((s, NZ), _MXU), scratch_shapes=[pltpu.VMEM((HEAD_DIM, GROUP), F32)],
        compiler_params=_cp("arbitrary"), name=name)(
            dy, z, z, z, z, o_pre, states, tb["b_cos"], tb["b_sin"], tb["decay"], tb["decay_t"], tb["qw"], tb["kw2"], tb["qw0"],
            tb["cd"], tb["grp"])


TQ = 256


def _log_sigmoid(x):
    return jnp.minimum(x, 0.0) - jnp.log1p(jnp.exp(-jnp.abs(x)))


def _fox_prep(z, b_f, name):
    s = z.shape[0]
    nb = s // TQ

    def body(m_ref, b_ref, cc_ref, carry):
        @pl.when(pl.program_id(0) == 0)
        def _():
            carry[...] = jnp.zeros_like(carry)

        lane = lax.broadcasted_iota(jnp.int32, (TQ, 128), 1)
        logf = jnp.where(lane < N_HEADS, _log_sigmoid(m_ref[...] + b_ref[...]), 0.0)
        r = lax.broadcasted_iota(jnp.int32, (TQ, TQ), 0)
        c = lax.broadcasted_iota(jnp.int32, (TQ, TQ), 1)
        tri = jnp.where(r >= c, 1.0, 0.0).astype(F32)
        cum = _dot_exact(tri, logf) + carry[...]
        cc_ref[...] = cum * LOG2E
        carry[...] = cum[TQ - 1:TQ, :]

    return pl.pallas_call(
        body, grid=(nb,),
        in_specs=[pl.BlockSpec((TQ, 128), lambda i: (i, NZ // 128 - 1)), pl.BlockSpec((1, 128), lambda i: (0, 0))],
        out_specs=pl.BlockSpec((TQ, 128), lambda i: (i, 0)),
        out_shape=SDS((s, 128), F32), scratch_shapes=[pltpu.VMEM((1, 128), F32)],
        compiler_params=_cp("arbitrary"), name=name)(z, b_f)


def _fox_post(dcr, dcq, z, b_f, dkr, dz, name):
    s = z.shape[0]
    nb = s // TQ

    def body(dc_ref, dcq_ref, m_ref, b_ref, dkr_ref, _, dz_ref, db_ref, carry):
        @pl.when(pl.program_id(0) == 0)
        def _():
            carry[...] = jnp.zeros_like(carry)
            db_ref[...] = jnp.zeros_like(db_ref)

        r = lax.broadcasted_iota(jnp.int32, (TQ, TQ), 0)
        c = lax.broadcasted_iota(jnp.int32, (TQ, TQ), 1)
        triu = jnp.where(c >= r, 1.0, 0.0).astype(F32)
        dc = jnp.concatenate([dc_ref[0], jnp.zeros((120, TQ), F32)], axis=0)
        dlogf = _dot_exact(triu, dc, (((1,), (1,)), ((), ()))) + _dot_exact(triu, dcq_ref[...]) + carry[...]
        carry[...] = dlogf[0:1, :]
        x = m_ref[...] + b_ref[...]
        lane = lax.broadcasted_iota(jnp.int32, (TQ, 128), 1)
        df = jnp.where(lane < N_HEADS, dlogf * _sigmoid(-x), 0.0)
        db_ref[...] += jnp.sum(df, axis=0, keepdims=True)
        dz_ref[...] = (df + dkr_ref[...]).astype(dz_ref.dtype)

    rv = lambda i: nb - 1 - i
    return pl.pallas_call(
        body, grid=(nb,),
        in_specs=[pl.BlockSpec((1, 8, TQ), lambda i: (rv(i), 0, 0)), pl.BlockSpec((TQ, 128), lambda i: (rv(i), 0)),
                  pl.BlockSpec((TQ, 128), lambda i: (rv(i), NZ // 128 - 1)),
                  pl.BlockSpec((1, 128), lambda i: (0, 0)), pl.BlockSpec((TQ, 128), lambda i: (rv(i), 0)),
                  pl.BlockSpec(memory_space=pl.ANY)],
        out_specs=[pl.BlockSpec((TQ, 128), lambda i: (rv(i), _DZ_MISC)), pl.BlockSpec((1, 128), lambda i: (0, 0))],
        out_shape=[SDS(dz.shape, dz.dtype), SDS((1, 128), F32)], scratch_shapes=[pltpu.VMEM((1, 128), F32)],
        input_output_aliases={5: 0}, compiler_params=_cp("arbitrary"), name=name)(dcr, dcq, z, b_f, dkr, dz)


NEG = -1e30
TKV = 512


def _key_block(s):
    return min(TKV, s)


def _diag_mask(shape, off):
    r = lax.broadcasted_iota(jnp.int32, shape, 0)
    c = lax.broadcasted_iota(jnp.int32, shape, 1)
    return c + off >= r


def _head_lanes(h, dqk):
    return slice(128 * (h // 2), 128 * (h // 2) + 128) if dqk == HEAD_DIM else slice(128 * h, 128 * h + 128)


def _keep_half(x, a, axis):
    idx = lax.broadcasted_iota(jnp.int32, x.shape, axis)
    return jnp.where((idx < HEAD_DIM) if a == 0 else (idx >= HEAD_DIM), x, jnp.zeros_like(x))


def _scaled_qt(q, scale):
    qs = q.astype(F32) * (scale * LOG2E)
    return [qs[TQ * b:TQ * (b + 1)].T.astype(_MXU) for b in range(q.shape[0] // TQ)]


def _kv_prep(z, qcol, kcol, vcol, scale, name):
    s = z.shape[0]
    tk = _key_block(s)
    nk = s // tk

    def body(q_ref, k_ref, v_ref, kb_ref, vb_ref, vt_ref, qt_ref):
        kb_ref[...] = k_ref[...].astype(_MXU)
        v = v_ref[...]
        vb_ref[...] = v.astype(_MXU)
        vt_ref[0] = v.T.astype(_MXU)
        for b, t in enumerate(_scaled_qt(q_ref[...], scale)):
            qt_ref[b] = t

    blk = pl.BlockSpec((tk, GROUP), lambda i: (i, 0))
    col = lambda c: pl.BlockSpec((tk, GROUP), lambda i, c=c: (i, c))
    return pl.pallas_call(
        body, grid=(nk,), in_specs=[col(qcol), col(kcol), col(vcol)],
        out_specs=[blk, blk, pl.BlockSpec((1, GROUP, tk), lambda i: (i, 0, 0)),
                   pl.BlockSpec((tk // TQ, GROUP, TQ), lambda i: (i, 0, 0))],
        out_shape=[SDS((s, GROUP), _MXU), SDS((s, GROUP), _MXU), SDS((nk, GROUP, tk), _MXU), SDS((s // TQ, GROUP, TQ), _MXU)],
        compiler_params=_cp("parallel"), name=name)(z, z, z)


LOG2E = 1.4426950408889634


def _attn_fwd(q, qcol, dqk, kb, vt, scale, ck2, name, comm=None):
    s = q.shape[0]
    nq = s // TQ
    tk = _key_block(s)
    ratio = tk // TQ
    wq = N_HEADS * dqk
    bias = ck2 is not None

    def body(*refs):
        ins, (o_ref, l_ref), _, cc = _split_refs(refs, 4 if bias else 3, 2, comm)
        if bias:
            q_ref, k_ref, vt_ref, cc_ref = ins
        else:
            q_ref, k_ref, vt_ref = ins
        i = pl.program_id(0)
        _host_gather(comm, cc, i, nq)
        qts = []
        for h in range(N_HEADS):
            qt = (q_ref[:, _head_lanes(h, dqk)].astype(F32) * (scale * LOG2E)).T
            qts.append((_keep_half(qt, h % 2, 0) if dqk == HEAD_DIM else qt).astype(_MXU))

        def step(j, carry, off):
            r0 = pl.multiple_of(j * tk, tk)
            vtj = vt_ref[j]
            sts = [jnp.dot(k_ref[pl.ds(r0, tk), _head_lanes(h, dqk)], qts[h], preferred_element_type=F32)
                   for h in range(N_HEADS)]
            stats, ps = [], []
            for h in range(N_HEADS):
                m, l, _ = carry[3 * h:3 * h + 3]
                st = sts[h]
                if bias:
                    st = st - cc_ref[pl.ds(r0, tk), h:h + 1]
                if off is not None:
                    st = jnp.where(_diag_mask(st.shape, off), st, NEG)
                m_new = jnp.maximum(m, jnp.max(st, axis=0, keepdims=True))
                alpha = jnp.exp2(m - m_new)
                p = jnp.exp2(st - m_new)
                stats.append((m_new, alpha * l + jnp.sum(p, axis=0, keepdims=True), alpha))
                ps.append(p.astype(_MXU))
            out = []
            for h in range(N_HEADS):
                m_new, l, alpha = stats[h]
                acc = alpha * carry[3 * h + 2] + jnp.dot(vtj[HEAD_DIM * h:HEAD_DIM * (h + 1), :], ps[h],
                                                         preferred_element_type=F32)
                out += [m_new, l, acc]
            return tuple(out)

        init = (jnp.full((1, TQ), NEG, F32), jnp.zeros((1, TQ), F32), jnp.zeros((HEAD_DIM, TQ), F32)) * N_HEADS
        jd = i // ratio
        carry = lax.fori_loop(0, jd, functools.partial(step, off=None), init)
        carry = step(jd, carry, TQ * (i % ratio))
        l_ref[...] = jnp.zeros_like(l_ref)
        for h in range(N_HEADS):
            l_ref[0, h:h + 1, :] = carry[3 * h] + jnp.log2(carry[3 * h + 1])
        for p in range(2):
            ot = jnp.concatenate([carry[6 * p + 2] / carry[6 * p + 1], carry[6 * p + 5] / carry[6 * p + 4]], axis=0)
            o_ref[:, 128 * p:128 * (p + 1)] = ot.T
        if comm is not None:
            @pl.when(i == nq - 1)
            def _():
                comm.wait(*cc)

    rows = pl.BlockSpec((1, 8, TQ), lambda i: (i, 0, 0))
    in_specs = [pl.BlockSpec((TQ, wq), lambda i: (i, qcol)), pl.BlockSpec((s, wq), lambda i: (0, 0)),
                pl.BlockSpec((s // tk, GROUP, tk), lambda i: (0, 0, 0))]
    args = [q, kb, vt]
    if bias:
        in_specs.append(pl.BlockSpec((s, 128), lambda i: (0, 0)))
        args.append(ck2)
    out_specs = [pl.BlockSpec((TQ, GROUP), lambda i: (i, 0)), rows]
    out_shape = [SDS((s, GROUP), F32), SDS((nq, 8, TQ), F32)]
    return _call_with_comm(body, (nq,), in_specs, out_specs, out_shape, [], args, comm, ("arbitrary",), name)


def _call_with_comm(body, grid, in_specs, out_specs, out_shape, scratch, args, comm, semantics, name, aliases=None):
    n_out = len(out_shape)
    if comm is not None:
        in_specs, out_specs = in_specs + comm.in_specs, out_specs + comm.out_specs
        out_shape, scratch, args = out_shape + comm.out_shape, scratch + comm.scratch, list(args) + comm.arrs
    res = pl.pallas_call(body, grid=grid, in_specs=in_specs, out_specs=out_specs, out_shape=out_shape,
                         scratch_shapes=scratch, input_output_aliases=aliases or {}, compiler_params=_cp(*semantics),
                         name=name)(*args)
    return (*res[:n_out], list(res[n_out:]))


def _attn_bwd(kb, vb, qt, dot, lse, dl, dqk, scale, ck2, name, kv_dtype, comm=None, kv_into=None):
    s = kb.shape[0]
    nq = s // TQ
    tk = _key_block(s)
    ratio = tk // TQ
    nkb = s // tk
    wq = N_HEADS * dqk
    bias = ck2 is not None

    merged = kv_into is not None
    n_in = 6 + bias + merged
    n_out = 3 + 2 * bias - merged

    def body(*refs):
        ins, outs, _, cc = _split_refs(refs, n_in, n_out, comm)
        k_ref, v_ref, qt_ref, dot_ref, l_ref, d_ref = ins[:6]
        cc_ref = ins[6] if bias else None
        dqt_ref = outs[0]
        if merged:
            dk_ref, dv_ref = outs[1].at[:, 0:wq], outs[1].at[:, wq:wq + GROUP]
        else:
            dk_ref, dv_ref = outs[1], outs[2]
        if bias:
            dck_ref, dcq_ref = outs[-2:]
        j = pl.program_id(0)

        @pl.when(j == 0)
        def _():
            if comm is not None:
                comm.start(*cc)
            dqt_ref[...] = jnp.zeros_like(dqt_ref)
            if bias:
                dcq_ref[...] = jnp.zeros_like(dcq_ref)

        ks, kts, vs = [], [], []
        for h in range(N_HEADS):
            k2 = k_ref[:, _head_lanes(h, dqk)]
            if dqk == HEAD_DIM:
                k2 = _keep_half(k2, h % 2, 1)
            ks.append(k2)
            kts.append(k2.astype(F32).T.astype(_MXU))
            vs.append(_keep_half(v_ref[:, _head_lanes(h, HEAD_DIM)], h % 2, 1))
        cks = [cc_ref[:, h:h + 1] for h in range(N_HEADS)] if bias else None

        nt = (((1,), (1,)), ((), ()))

        def step(i, carry, off):
            qti, doti, li, di = qt_ref[i], dot_ref[i], l_ref[i], d_ref[i]
            qls = [_head_lanes(h, dqk) for h in range(N_HEADS)]
            vls = [_head_lanes(h, HEAD_DIM) for h in range(N_HEADS)]
            sts, dpts = [], []
            for h in range(N_HEADS):
                sts.append(jnp.dot(ks[h], qti[qls[h], :], preferred_element_type=F32))
                dpts.append(jnp.dot(vs[h], doti[vls[h], :], preferred_element_type=F32))
            pbs, dsbs, dcks = [], [], []
            for h in range(N_HEADS):
                st = sts[h] - li[h:h + 1, :]
                if bias:
                    st = st - cks[h]
                p = jnp.exp2(st)
                if off is not None:
                    p = jnp.where(_diag_mask(p.shape, off), p, 0.0)
                dst = p * (dpts[h] - di[h:h + 1, :])
                pbs.append(p.astype(_MXU))
                dsbs.append(dst.astype(_MXU))
                if bias:
                    dcks.append(carry[3 * h + 2] + jnp.sum(dst, axis=1, keepdims=True))
                    dcq_ref[i, h:h + 1, :] += jnp.sum(dst, axis=0, keepdims=True)
                else:
                    dcks.append(carry[3 * h + 2])
            out = []
            for h in range(N_HEADS):
                dvt = carry[3 * h + 1] + lax.dot_general(doti[HEAD_DIM * h:HEAD_DIM * (h + 1), :], pbs[h], nt,
                                                         preferred_element_type=F32)
                dkt = carry[3 * h] + lax.dot_general(qti[dqk * h:dqk * (h + 1), :], dsbs[h], nt, preferred_element_type=F32)
                dqt_ref[i, qls[h], :] += jnp.dot(kts[h], dsbs[h], preferred_element_type=F32) * scale
                out += [dkt, dvt, dcks[h]]
            return tuple(out)

        carry = (jnp.zeros((dqk, tk), F32), jnp.zeros((HEAD_DIM, tk), F32), jnp.zeros((tk, 1), F32)) * N_HEADS
        for r in range(ratio):
            carry = step(ratio * j + r, carry, TQ * r)
        carry = lax.fori_loop(ratio * (j + 1), nq, functools.partial(step, off=None), carry)
        for p in range(2):
            dv_ref[:, 128 * p:128 * (p + 1)] = jnp.concatenate([carry[6 * p + 1], carry[6 * p + 4]], axis=0).T.astype(dv_ref.dtype)
            if dqk == HEAD_DIM:
                dk_ref[:, 128 * p:128 * (p + 1)] = (jnp.concatenate([carry[6 * p], carry[6 * p + 3]], axis=0).T
                                                    * (1.0 / LOG2E)).astype(dk_ref.dtype)
        if dqk != HEAD_DIM:
            for h in range(N_HEADS):
                dk_ref[:, 128 * h:128 * (h + 1)] = (carry[3 * h].T * (1.0 / LOG2E)).astype(dk_ref.dtype)
        if bias:
            dck_ref[...] = jnp.zeros_like(dck_ref)
            for h in range(N_HEADS):
                dck_ref[:, h:h + 1] = -carry[3 * h + 2]
        if comm is not None:
            @pl.when(j == nkb - 1)
            def _():
                comm.wait(*cc)

    blk = lambda w: pl.BlockSpec((tk, w), lambda j: (j, 0))
    full3 = lambda w: pl.BlockSpec((nq, w, TQ), lambda j: (0, 0, 0))
    in_specs = [blk(wq), blk(GROUP), full3(wq), full3(GROUP), full3(8), full3(8)]
    args = [kb, vb, qt, dot, lse, dl]
    if merged:
        assert wq == GROUP
        out_specs = [full3(wq), pl.BlockSpec((tk, wq + GROUP), lambda j: (j, _DZ_FOX_KV))]
        out_shape = [SDS((nq, wq, TQ), F32), SDS(kv_into.shape, kv_into.dtype)]
    else:
        out_specs = [full3(wq), blk(wq), blk(GROUP)]
        out_shape = [SDS((nq, wq, TQ), F32), SDS((s, wq), kv_dtype), SDS((s, GROUP), kv_dtype)]
    if bias:
        in_specs.append(blk(128))
        args.append(ck2)
        out_specs += [blk(128), full3(8)]
        out_shape += [SDS((s, 128), F32), SDS((nq, 8, TQ), F32)]
    aliases = {}
    if merged:
        in_specs.append(pl.BlockSpec(memory_space=pl.ANY))
        args.append(kv_into)
        aliases = {len(args) - 1: 1}
    return _call_with_comm(body, (nkb,), in_specs, out_specs, out_shape, [], args, comm, ("arbitrary",), name, aliases)


def _untranspose(xt, dtype, name, into=None, col=0):
    nq, w, _ = xt.shape
    if into is not None:
        def body_into(x_ref, _, o_ref):
            o_ref[...] = x_ref[0].T.astype(o_ref.dtype)

        return pl.pallas_call(
            body_into, grid=(nq,),
            in_specs=[pl.BlockSpec((1, w, TQ), lambda i: (i, 0, 0)), pl.BlockSpec(memory_space=pl.ANY)],
            out_specs=pl.BlockSpec((TQ, w), lambda i: (i, col)), out_shape=SDS(into.shape, into.dtype),
            input_output_aliases={1: 0}, compiler_params=_cp("parallel"), name=name)(xt, into)

    def body(x_ref, o_ref):
        o_ref[...] = x_ref[0].T.astype(o_ref.dtype)

    return pl.pallas_call(
        body, grid=(nq,), in_specs=[pl.BlockSpec((1, w, TQ), lambda i: (i, 0, 0))],
        out_specs=pl.BlockSpec((TQ, w), lambda i: (i, 0)), out_shape=SDS((nq * TQ, w), dtype),
        compiler_params=_cp("parallel"), name=name)(xt)


_SCALE_D = (64 + 32) ** -0.5
_COL_CQ, _COL_CKV, _COL_MISC = 2304 // 256, 2560 // 128, 2688 // 128


def _mla_prep(z, gq, gkv, wq, wk, wv, tb, name):
    s = z.shape[0]
    tm = _key_block(s)
    row = lambda w, c: pl.BlockSpec((tm, w), lambda i, c=c: (i, c))
    const = lambda a: pl.BlockSpec(a.shape, lambda i: (0,) * a.ndim)

    def body(cq_ref, ckv_ref, m_ref, gq_ref, gkv_ref, wq_ref, wk_ref, wv_ref, e_ref, qc_ref, qs_ref, kc_ref, ks_ref,
             q_ref, k_ref, v_ref, vt_ref, cqn_ref, ckvn_ref, qt_ref):
        cqn = _rms(cq_ref[...], gq_ref[...]).astype(_MXU)
        ckvn = _rms(ckv_ref[...], gkv_ref[...]).astype(_MXU)
        cqn_ref[...] = cqn
        ckvn_ref[...] = ckvn
        qb = _rope(_dot(cqn, wq_ref[...]), _lanes(qc_ref[...], 512), _lanes(qs_ref[...], 512), 16).astype(q_ref.dtype)
        q_ref[...] = qb
        for b, t in enumerate(_scaled_qt(qb, _SCALE_D)):
            qt_ref[b] = t
        kr = _rope(m_ref[...], kc_ref[...], ks_ref[...], 16)
        k_ref[...] = (_dot(ckvn, wk_ref[...]) + _dot(kr, e_ref[...])).astype(k_ref.dtype)
        v = _dot(ckvn, wv_ref[...])
        v_ref[...] = v.astype(v_ref.dtype)
        vt_ref[0] = v.T.astype(vt_ref.dtype)

    e = tb["place"]
    return pl.pallas_call(
        body, grid=(s // tm,),
        in_specs=[row(256, _COL_CQ), row(128, _COL_CKV), row(128, _COL_MISC), const(gq), const(gkv), const(wq), const(wk),
                  const(wv), const(e), row(128, 0), row(128, 0), row(128, 0), row(128, 0)],
        out_specs=[row(512, 0), row(512, 0), row(256, 0), pl.BlockSpec((1, GROUP, tm), lambda i: (i, 0, 0)), row(256, 0),
                   row(128, 0), pl.BlockSpec((tm // TQ, 512, TQ), lambda i: (i, 0, 0))],
        out_shape=[SDS((s, 512), _MXU), SDS((s, 512), _MXU), SDS((s, 256), _MXU), SDS((s // tm, GROUP, tm), _MXU),
                   SDS((s, 256), _MXU), SDS((s, 128), _MXU), SDS((s // TQ, 512, TQ), _MXU)],
        compiler_params=_cp("parallel"), name=name)(
            z, z, z, gq, gkv, wq, wk, wv, e, tb["q_cos"], tb["q_sin"], tb["k_cos"], tb["k_sin"])


def _mla_prep_bwd(dqt, dk, dv, z, cqn, ckvn, gq, gkv, wq, wk, wv, tb, dz, name):
    s = z.shape[0]
    tm = min(512, s)
    row = lambda w, c: pl.BlockSpec((tm, w), lambda i, c=c: (i, c))
    const = lambda a: pl.BlockSpec(a.shape, lambda i: (0,) * a.ndim)
    acc = lambda shape: pl.BlockSpec(shape, lambda i: (0, 0))

    def body(dq_ref, dk_ref, dv_ref, cq_ref, ckv_ref, cqn_ref, ckvn_ref, gq_ref, gkv_ref, wq_ref, wk_ref, wv_ref, e_ref,
             qc_ref, qs_ref, kc_ref, ks_ref, _, dz_ref, dkr_ref, dwq_ref, dwk_ref, dwv_ref, dgq_ref, dgkv_ref):
        dcq_ref, dckv_ref = dz_ref.at[:, 0:256], dz_ref.at[:, 256:384]

        @pl.when(pl.program_id(0) == 0)
        def _():
            for r in (dwq_ref, dwk_ref, dwv_ref, dgq_ref, dgkv_ref):
                r[...] = jnp.zeros_like(r)

        dq = jnp.concatenate([dq_ref[b].T for b in range(tm // TQ)], axis=0)
        dqp = _rope_bwd(dq, _lanes(qc_ref[...], 512), _lanes(qs_ref[...], 512), 16)
        dkd = dk_ref[...]
        dvd = dv_ref[...]
        dwq_ref[...] += _dot_tn(cqn_ref[...], dqp)
        dwk_ref[...] += _dot_tn(ckvn_ref[...], dkd)
        dwv_ref[...] += _dot_tn(ckvn_ref[...], dvd)
        dcq, dgq = _rms_bwd(cq_ref[...], gq_ref[...], _dot_nt(dqp, wq_ref[...]))
        dckv, dgkv = _rms_bwd(ckv_ref[...], gkv_ref[...], _dot_nt(dkd, wk_ref[...]) + _dot_nt(dvd, wv_ref[...]))
        dcq_ref[...] = dcq.astype(dcq_ref.dtype)
        dckv_ref[...] = dckv.astype(dckv_ref.dtype)
        dgq_ref[...] += dgq
        dgkv_ref[...] += dgkv
        dkr = _dot_exact(dkd, e_ref[...], (((1,), (1,)), ((), ())))
        dkr_ref[...] = _rope_bwd(dkr, kc_ref[...], ks_ref[...], 16)

    e = tb["place"]
    return pl.pallas_call(
        body, grid=(s // tm,),
        in_specs=[pl.BlockSpec((tm // TQ, 512, TQ), lambda i: (i, 0, 0)), row(512, 0), row(256, 0), row(256, _COL_CQ),
                  row(128, _COL_CKV), row(256, 0), row(128, 0),
                  const(gq), const(gkv), const(wq), const(wk), const(wv), const(e), row(128, 0), row(128, 0), row(128, 0), row(128, 0),
                  pl.BlockSpec(memory_space=pl.ANY)],
        out_specs=[row(384, _DZ_MLA), row(128, 0), acc((256, 512)), acc((128, 512)), acc((128, 256)), acc((1, 256)),
                   acc((1, 128))],
        out_shape=[SDS(dz.shape, dz.dtype), SDS((s, 128), F32), SDS((256, 512), F32), SDS((128, 512), F32),
                   SDS((128, 256), F32), SDS((1, 256), F32), SDS((1, 128), F32)],
        input_output_aliases={17: 0}, compiler_params=_cp("arbitrary"), name=name)(
            dqt, dk, dv, z, z, cqn, ckvn, gq, gkv, wq, wk, wv, e, tb["q_cos"], tb["q_sin"], tb["k_cos"], tb["k_sin"], dz)


def _out_proj(ys, g, w, x, name):
    s, d = x.shape
    tm = min(512, s)

    def body(ya, yb, yc, yd, g_ref, w_ref, x_ref, o_ref, yn_ref):
        acc = x_ref[...]
        for i, y_ref in enumerate((ya, yb, yc, yd)):
            sl = slice(GROUP * i, GROUP * (i + 1))
            yn = _rms(y_ref[...], g_ref[:, sl]).astype(_MXU)
            yn_ref[:, sl] = yn
            acc = acc + jnp.dot(yn, w_ref[sl, :], preferred_element_type=F32)
        o_ref[...] = acc

    yspec = pl.BlockSpec((tm, GROUP), lambda i: (i, 0))
    return pl.pallas_call(
        body, grid=(s // tm,),
        in_specs=[yspec, yspec, yspec, yspec, pl.BlockSpec((1, d), lambda i: (0, 0)), pl.BlockSpec((d, d), lambda i: (0, 0)),
                  pl.BlockSpec((tm, d), lambda i: (i, 0))],
        out_specs=[pl.BlockSpec((tm, d), lambda i: (i, 0)), pl.BlockSpec((tm, d), lambda i: (i, 0))],
        out_shape=[SDS((s, d), F32), SDS((s, d), _MXU)], compiler_params=_cp("parallel"), name=name)(*ys, g, w, x)


def _out_proj_bwd(dx, w, ys, g, name):
    s, d = dx.shape
    tm = min(512, s)
    nb = tm // TQ

    def body(dx_ref, w_ref, ya, yb, yc, yd, g_ref, da, db, dg_ref, dtc_ref, dtd_ref, dlc_ref, dld_ref):
        @pl.when(pl.program_id(0) == 0)
        def _():
            dg_ref[...] = jnp.zeros_like(dg_ref)

        dyn = _dot_nt(dx_ref[...], w_ref[...])
        for i, y_ref in enumerate((ya, yb, yc, yd)):
            sl = slice(GROUP * i, GROUP * (i + 1))
            y = y_ref[...]
            dy, dg = _rms_bwd(y, g_ref[:, sl], dyn[:, sl])
            dg_ref[:, sl] += dg
            if i < 2:
                (da, db)[i][...] = dy
                continue
            dt_ref, dl_ref = ((dtc_ref, dlc_ref), (dtd_ref, dld_ref))[i - 2]
            dl_ref[...] = jnp.zeros_like(dl_ref)
            for b in range(nb):
                r = slice(TQ * b, TQ * (b + 1))
                dt_ref[b] = dy[r].T.astype(dt_ref.dtype)
                pt = (dy[r] * y[r]).T
                for h in range(N_HEADS):
                    dl_ref[b, h:h + 1, :] = jnp.sum(pt[HEAD_DIM * h:HEAD_DIM * (h + 1), :], axis=0, keepdims=True)

    yspec = pl.BlockSpec((tm, GROUP), lambda i: (i, 0))
    tspec = pl.BlockSpec((nb, GROUP, TQ), lambda i: (i, 0, 0))
    lspec = pl.BlockSpec((nb, 8, TQ), lambda i: (i, 0, 0))
    return pl.pallas_call(
        body, grid=(s // tm,),
        in_specs=[pl.BlockSpec((tm, d), lambda i: (i, 0)), pl.BlockSpec((d, d), lambda i: (0, 0)), yspec, yspec, yspec, yspec,
                  pl.BlockSpec((1, d), lambda i: (0, 0))],
        out_specs=[yspec, yspec, pl.BlockSpec((1, d), lambda i: (0, 0)), tspec, tspec, lspec, lspec],
        out_shape=[SDS((s, GROUP), F32)] * 2 + [SDS((1, d), F32)] + [SDS((s // TQ, GROUP, TQ), _MXU)] * 2
        + [SDS((s // TQ, 8, TQ), F32)] * 2,
        compiler_params=_cp("arbitrary"), name=name)(dx, w, *ys, g)


FF_BLOCK = 512
FF_ROWS = 1024


def _ffn_fwd(x, g, wu, wd, name, comm=None):
    s, d = x.shape
    nj = wu.shape[0]
    tm = min(FF_ROWS, s)
    ni = s // tm

    def body(*refs):
        (x_ref, g_ref, wu_ref, wd_ref), (o_ref, u_ref, h_ref), (acc,), cc = _split_refs(refs, 4, 3, comm)
        i, j = pl.program_id(0), pl.program_id(1)
        _host_gather(comm, cc, i * nj + j, ni * nj, late=False)

        @pl.when(j == 0)
        def _():
            h_ref[...] = _rms(x_ref[...], g_ref[...]).astype(h_ref.dtype)
            acc[...] = jnp.zeros_like(acc)

        halves = [slice(r, r + tm // 2) for r in range(0, tm, tm // 2)]
        us = [jnp.dot(h_ref[r, :], wu_ref[0], preferred_element_type=F32) for r in halves]
        for r, u in zip(halves, us):
            u_ref[r, :] = u.astype(u_ref.dtype)
            acc[r, :] += _dot(jnp.square(jnp.maximum(u, 0.0)), wd_ref[...])

        @pl.when(j == nj - 1)
        def _():
            o_ref[...] = x_ref[...] + acc[...]

        if comm is not None:
            @pl.when((i == ni - 1) & (j == nj - 1))
            def _():
                comm.wait(*cc)

    in_specs = [pl.BlockSpec((tm, d), lambda i, j: (i, 0)), pl.BlockSpec((1, d), lambda i, j: (0, 0)),
                pl.BlockSpec((1, d, FF_BLOCK), lambda i, j: (j, 0, 0)), pl.BlockSpec((FF_BLOCK, d), lambda i, j: (j, 0))]
    out_specs = [pl.BlockSpec((tm, d), lambda i, j: (i, 0)), pl.BlockSpec((tm, FF_BLOCK), lambda i, j: (i, j)),
                 pl.BlockSpec((tm, d), lambda i, j: (i, 0))]
    out_shape = [SDS((s, d), F32), SDS((s, nj * FF_BLOCK), _MXU), SDS((s, d), _MXU)]
    return _call_with_comm(body, (ni, nj), in_specs, out_specs, out_shape, [pltpu.VMEM((tm, d), F32)], [x, g, wu, wd], comm,
                           ("arbitrary", "arbitrary"), name)


def _ffn_bwd(dx2, x, u, g, wu, wd, name, comm=None):
    s, d = x.shape
    nj = wu.shape[0]
    tm = min(FF_ROWS, s)
    ni = s // tm

    def body(*refs):
        (dx_ref, x_ref, u_ref, g_ref, wu_ref, wd_ref), (o_ref, du_ref, dg_ref), (acc, dxb), cc = _split_refs(refs, 6, 3, comm)
        i, j = pl.program_id(0), pl.program_id(1)

        @pl.when((i == 0) & (j == 0))
        def _():
            if comm is not None:
                comm.start(*cc)
            dg_ref[...] = jnp.zeros_like(dg_ref)

        @pl.when(j == 0)
        def _():
            dxb[...] = dx_ref[...].astype(dxb.dtype)
            acc[...] = jnp.zeros_like(acc)

        nt = (((1,), (1,)), ((), ()))
        halves = [slice(r, r + tm // 2) for r in range(0, tm, tm // 2)]
        das = [lax.dot_general(dxb[r, :], wd_ref[...], nt, preferred_element_type=F32) for r in halves]
        for r, da in zip(halves, das):
            du = (da * 2.0 * jnp.maximum(u_ref[r, :].astype(F32), 0.0)).astype(du_ref.dtype)
            du_ref[r, :] = du
            acc[r, :] += lax.dot_general(du, wu_ref[0], nt, preferred_element_type=F32)

        @pl.when(j == nj - 1)
        def _():
            dxn, dg = _rms_bwd(x_ref[...], g_ref[...], acc[...])
            o_ref[...] = dx_ref[...] + dxn
            dg_ref[...] += dg

        if comm is not None:
            @pl.when((i == ni - 1) & (j == nj - 1))
            def _():
                comm.wait(*cc)

    in_specs = [pl.BlockSpec((tm, d), lambda i, j: (i, 0)), pl.BlockSpec((tm, d), lambda i, j: (i, 0)),
                pl.BlockSpec((tm, FF_BLOCK), lambda i, j: (i, j)), pl.BlockSpec((1, d), lambda i, j: (0, 0)),
                pl.BlockSpec((1, d, FF_BLOCK), lambda i, j: (j, 0, 0)), pl.BlockSpec((FF_BLOCK, d), lambda i, j: (j, 0))]
    out_specs = [pl.BlockSpec((tm, d), lambda i, j: (i, 0)), pl.BlockSpec((tm, FF_BLOCK), lambda i, j: (i, j)),
                 pl.BlockSpec((1, d), lambda i, j: (0, 0))]
    out_shape = [SDS((s, d), F32), SDS((s, nj * FF_BLOCK), _MXU), SDS((1, d), F32)]
    return _call_with_comm(body, (ni, nj), in_specs, out_specs, out_shape,
                           [pltpu.VMEM((tm, d), F32), pltpu.VMEM((tm, d), _MXU)], [dx2, x, u, g, wu, wd], comm,
                           ("arbitrary", "arbitrary"), name)


def _in_proj_bwd(dz, w, x, g, dx_up, name, comm=None):
    s, d = x.shape
    n = w.shape[1]
    tm = min(512, s)
    ni = s // tm

    def body(*refs):
        (dz_ref, w_ref, x_ref, g_ref, up_ref), (o_ref, dg_ref), _, cc = _split_refs(refs, 5, 2, comm)
        i = pl.program_id(0)

        @pl.when(i == 0)
        def _():
            if comm is not None:
                comm.start(*cc)
            dg_ref[...] = jnp.zeros_like(dg_ref)

        dh = lax.dot_general(dz_ref[...], w_ref[...], (((1,), (1,)), ((), ())), preferred_element_type=F32)
        dxn, dg = _rms_bwd(x_ref[...], g_ref[...], dh)
        o_ref[...] = up_ref[...] + dxn
        dg_ref[...] += dg
        if comm is not None:
            @pl.when(i == ni - 1)
            def _():
                comm.wait(*cc)

    in_specs = [pl.BlockSpec((tm, n), lambda i: (i, 0)), pl.BlockSpec((d, n), lambda i: (0, 0)),
                pl.BlockSpec((tm, d), lambda i: (i, 0)), pl.BlockSpec((1, d), lambda i: (0, 0)),
                pl.BlockSpec((tm, d), lambda i: (i, 0))]
    out_specs = [pl.BlockSpec((tm, d), lambda i: (i, 0)), pl.BlockSpec((1, d), lambda i: (0, 0))]
    out_shape = [SDS((s, d), F32), SDS((1, d), F32)]
    return _call_with_comm(body, (ni,), in_specs, out_specs, out_shape, [], [dz, w, x, g, dx_up], comm, ("arbitrary",), name)


def _loss_head(x, g, target, name):
    s, d = x.shape
    tm = min(512, s)

    def body(x_ref, g_ref, t_ref, l_ref, dx_ref, dg_ref):
        @pl.when(pl.program_id(0) == 0)
        def _():
            l_ref[...] = jnp.zeros_like(l_ref)
            dg_ref[...] = jnp.zeros_like(dg_ref)

        xv = x_ref[...]
        err = _rms(xv, g_ref[...]) - t_ref[...]
        l_ref[...] += jnp.sum(err * err, axis=0, keepdims=True) * (0.5 / d)
        dx, dg = _rms_bwd(xv, g_ref[...], err * (1.0 / d))
        dx_ref[...] = dx
        dg_ref[...] += dg

    return pl.pallas_call(
        body, grid=(s // tm,),
        in_specs=[pl.BlockSpec((tm, d), lambda i: (i, 0)), pl.BlockSpec((1, d), lambda i: (0, 0)),
                  pl.BlockSpec((tm, d), lambda i: (i, 0))],
        out_specs=[pl.BlockSpec((1, d), lambda i: (0, 0)), pl.BlockSpec((tm, d), lambda i: (i, 0)),
                   pl.BlockSpec((1, d), lambda i: (0, 0))],
        out_shape=[SDS((1, d), F32), SDS((s, d), F32), SDS((1, d), F32)], compiler_params=_cp("arbitrary"), name=name)(x, g, target)


def _me_and_peer():
    x, y, c = lax.axis_index("x"), lax.axis_index("y"), lax.axis_index("c")
    me = 4 * x + 2 * y + c

    def peer(k):
        px, py, pc = x ^ (k >> 2), y ^ ((k >> 1) & 1), c ^ (k & 1)
        return (px, py, pc), 4 * px + 2 * py + pc

    return me, peer


class _Comm:
    CHIPS = (2, 4, 6)

    def __init__(self, kind, arrs):
        assert kind in ("gather", "exchange")
        self.kind, self.arrs, self.n = kind, list(arrs), len(arrs)
        anyspec = pl.BlockSpec(memory_space=pl.ANY)
        self.in_specs = [anyspec] * self.n
        self.out_specs = [anyspec] * self.n
        self.out_shape = [SDS(((NDEV,) + a.shape) if kind == "gather" else a.shape, a.dtype) for a in self.arrs]
        npair = NDEV - 1 + len(self.CHIPS)
        self.scratch = [pltpu.SemaphoreType.DMA((self.n, npair)), pltpu.SemaphoreType.DMA((self.n, npair)),
                        pltpu.SemaphoreType.DMA((self.n,))]

    def _copies(self, ins, outs, sems):
        send, recv, loc = sems
        me, peer = _me_and_peer()
        gather = self.kind == "gather"
        sibling = peer(1)[0]
        local = [pltpu.make_async_copy(ins[a] if gather else ins[a].at[me], outs[a].at[me], loc.at[a]) for a in range(self.n)]
        outgoing, incoming, forwards, forwarded = [], [], [], []
        for k in ((1,) + self.CHIPS) if gather else range(1, NDEV):
            dev, pid = peer(k)
            for a in range(self.n):
                pair = dict(send_sem=send.at[a, k - 1], recv_sem=recv.at[a, k - 1], device_id=dev, device_id_type=MESH)
                outgoing.append(pltpu.make_async_remote_copy(src_ref=ins[a] if gather else ins[a].at[pid],
                                                             dst_ref=outs[a].at[me], **pair))
                incoming.append(pltpu.make_async_remote_copy(src_ref=ins[a] if gather else ins[a].at[me],
                                                             dst_ref=outs[a].at[pid], **pair))
        if gather:
            for idx, k in enumerate(self.CHIPS):
                got, theirs = peer(k)[1], peer(k + 1)[1]
                for a in range(self.n):
                    pair = dict(send_sem=send.at[a, NDEV - 1 + idx], recv_sem=recv.at[a, NDEV - 1 + idx], device_id=sibling,
                                device_id_type=MESH)
                    forwards.append(pltpu.make_async_remote_copy(src_ref=outs[a].at[got], dst_ref=outs[a].at[got], **pair))
                    forwarded.append(pltpu.make_async_remote_copy(src_ref=outs[a].at[theirs], dst_ref=outs[a].at[theirs], **pair))
        return local, outgoing, incoming, forwards, forwarded

    def start(self, ins, outs, sems):
        local, outgoing, _, _, _ = self._copies(ins, outs, sems)
        for cp in local + outgoing:
            cp.start()

    def forward(self, ins, outs, sems):
        _, _, incoming, forwards, _ = self._copies(ins, outs, sems)
        per = self.n
        for idx in range(len(forwards) // per if per else 0):
            for a in range(per):
                incoming[(1 + idx) * per + a].wait_recv()
                forwards[idx * per + a].start()

    def wait(self, ins, outs, sems):
        local, outgoing, incoming, forwards, forwarded = self._copies(ins, outs, sems)
        for cp in (incoming[:self.n] if self.kind == "gather" else incoming) + forwarded:
            cp.wait_recv()
        for cp in outgoing + forwards:
            cp.wait_send()
        for cp in local:
            cp.wait()


LATE_FORWARD_BYTES = 1 << 20


def _host_gather(comm, cc, step, nsteps, late=None):
    if comm is None:
        return
    if late is None:
        late = sum(a.size * a.dtype.itemsize for a in comm.arrs) >= LATE_FORWARD_BYTES

    @pl.when(step == 0)
    def _():
        comm.start(*cc)

    @pl.when(step == (nsteps - 1 if late else (2 * nsteps) // 3))
    def _():
        comm.forward(*cc)


def _split_refs(refs, n_in, n_out, comm):
    c = comm.n if comm is not None else 0
    ins, cin = refs[:n_in], refs[n_in:n_in + c]
    outs, cout = refs[n_in + c:n_in + c + n_out], refs[n_in + c + n_out:n_in + 2 * c + n_out]
    rest = refs[n_in + 2 * c + n_out:]
    scratch, csem = (rest[:len(rest) - 3], rest[len(rest) - 3:]) if c else (rest, ())
    return ins, outs, scratch, (cin, cout, csem)


def _comm_call(kind, arrs, name):
    comm = _Comm(kind, arrs)

    def body(*refs):
        _, _, _, c = _split_refs(refs, 0, 0, comm)
        comm.start(*c)
        if kind == "gather":
            comm.forward(*c)
        comm.wait(*c)

    return pl.pallas_call(body, in_specs=comm.in_specs, out_specs=comm.out_specs, out_shape=comm.out_shape,
                          scratch_shapes=comm.scratch, compiler_params=pltpu.CompilerParams(has_side_effects=True),
                          name=name)(*arrs)


def _all_gather(arrs, name):
    return _comm_call("gather", arrs, name)


def _exchange(arrs, name):
    return _comm_call("exchange", arrs, name)


def _sum_slots(parts, name):
    _, r, c = parts.shape
    tr = r if r <= 512 else 512

    def body(p_ref, o_ref):
        acc = p_ref[0].astype(F32)
        for q in range(1, NDEV):
            acc = acc + p_ref[q].astype(F32)
        o_ref[...] = acc

    return pl.pallas_call(
        body, grid=(r // tr,), in_specs=[pl.BlockSpec((NDEV, tr, c), lambda i: (0, i, 0))],
        out_specs=pl.BlockSpec((tr, c), lambda i: (i, 0)), out_shape=SDS((r, c), F32),
        compiler_params=_cp("parallel"), name=name)(parts)


def _adamw(g, w, m, v, name):
    r, c = w.shape
    parts = g.ndim == 3
    tr = r
    for cand in (512, 256, 128, 64, 32, 16, 8):
        if r > cand and r % cand == 0 and cand * c * 4 <= 2 * 1024 * 1024:
            tr = cand
            break
    bc1 = 1.0 / (1.0 - ADAM_B1 ** ADAM_STEP)
    bc2 = 1.0 / (1.0 - ADAM_B2 ** ADAM_STEP)

    def body(g_ref, w_ref, m_ref, v_ref, go_ref, d_ref, mo_ref, vo_ref):
        if parts:
            gv = g_ref[0].astype(F32)
            for q in range(1, NDEV):
                gv = gv + g_ref[q].astype(F32)
        else:
            gv = g_ref[...]
        mn = ADAM_B1 * m_ref[...] + (1.0 - ADAM_B1) * gv
        vn = ADAM_B2 * v_ref[...] + (1.0 - ADAM_B2) * (gv * gv)
        go_ref[...] = gv
        mo_ref[...] = mn
        vo_ref[...] = vn
        d_ref[...] = -ADAM_LR * ((mn * bc1) / (jnp.sqrt(vn * bc2) + ADAM_EPS) + ADAM_WD * w_ref[...])

    spec = pl.BlockSpec((tr, c), lambda i: (i, 0))
    gspec = pl.BlockSpec((NDEV, tr, c), lambda i: (0, i, 0)) if parts else spec
    return pl.pallas_call(
        body, grid=(r // tr,), in_specs=[gspec, spec, spec, spec], out_specs=[spec] * 4,
        out_shape=[SDS((r, c), F32)] * 4, compiler_params=_cp("parallel"), name=name)(g, w, m, v)


def _adamw_layer(parts, w, m, v, l, prev, name):
    r, c = parts.shape[1:]
    rows = w.shape[0]
    tr = next(t for t in (512, 256, 128, 64, 32, 16, 8) if r % t == 0 and t * c * 4 <= 384 * 1024)
    bc1 = 1.0 / (1.0 - ADAM_B1 ** ADAM_STEP)
    bc2 = 1.0 / (1.0 - ADAM_B2 ** ADAM_STEP)

    def body(g_ref, w_ref, m_ref, v_ref, *rest):
        go_ref, d_ref, mo_ref, vo_ref = rest[-4:]
        gv = g_ref[0].astype(F32)
        for q in range(1, NDEV):
            gv = gv + g_ref[q].astype(F32)
        mn = ADAM_B1 * m_ref[...] + (1.0 - ADAM_B1) * gv
        vn = ADAM_B2 * v_ref[...] + (1.0 - ADAM_B2) * (gv * gv)
        go_ref[...] = gv
        mo_ref[...] = mn
        vo_ref[...] = vn
        d_ref[...] = -ADAM_LR * ((mn * bc1) / (jnp.sqrt(vn * bc2) + ADAM_EPS) + ADAM_WD * w_ref[...])

    spec = pl.BlockSpec((tr, c), lambda i: (l * (r // tr) + i, 0))
    in_specs = [pl.BlockSpec((NDEV, tr, c), lambda i: (0, i, 0)), spec, spec, spec]
    args = [parts, w, m, v]
    aliases = {}
    if prev is not None:
        in_specs += [pl.BlockSpec(memory_space=pl.ANY)] * 4
        args += list(prev)
        aliases = {4 + k: k for k in range(4)}
    return pl.pallas_call(
        body, grid=(r // tr,), in_specs=in_specs, out_specs=[spec] * 4, out_shape=[SDS((rows, c), F32)] * 4,
        input_output_aliases=aliases, compiler_params=_cp("parallel"), name=name)(*args)


def _pad_in_cols(w):
    r = w.shape[0]
    zeros = lambda n: jnp.zeros((r, n), w.dtype)
    return jnp.concatenate([w[:, 512:1536], w[:, 0:512], w[:, 1792:2304], w[:, 1536:1792], w[:, 2308:2692], w[:, 2304:2308],
                            zeros(28), w[:, 2692:2724], zeros(64)], axis=1)


def _unpad_in_cols(w):
    return jnp.concatenate([w[..., 1024:1536], w[..., 0:1024], w[..., 2048:2304], w[..., 1536:2048], w[..., 2688:2692],
                            w[..., 2304:2688], w[..., 2720:2752]], axis=-1)


_Z_RET = (0, 1, 2, 3)
_Z_SGU = (4, 5)
_Z_FOX_Q, _Z_FOX_K, _Z_FOX_V = 8, 6, 7
_DZ_RET, _DZ_SGU, _DZ_FOX_KV, _DZ_FOX_Q, _DZ_MLA, _DZ_MISC = 0, 2, 3, 8, 6, 21


def _pad_uq(w):
    return jnp.pad(w.reshape(256, N_HEADS, 96), ((0, 0), (0, 0), (0, 32))).reshape(256, 512)


def _unpad_uq(w):
    return w.reshape(256, N_HEADS, 128)[:, :, :96].reshape(256, 384)


def _split_ukv(w):
    r = w.reshape(128, N_HEADS, 128)
    return jnp.pad(r[:, :, :64], ((0, 0), (0, 0), (0, 64))).reshape(128, 512), r[:, :, 64:].reshape(128, 256)


def _join_ukv(dk, dv):
    return jnp.concatenate([dk.reshape(128, N_HEADS, 128)[:, :, :64], dv.reshape(128, N_HEADS, 64)], axis=-1).reshape(128, 512)


def _cols_to_full(g):
    return jnp.transpose(g, (1, 0, 2)).reshape(g.shape[1], NDEV * g.shape[2])


def kernel(x, g_mix_norm, w_in, b_forget, g_sgu, w_spatial, b_spatial, g_mla_q, w_uq, g_mla_kv, w_ukv, g_group_out, w_out, g_ffn_norm, w_up, w_down, g_final, loss_target, m_g_mix_norm, m_w_in, m_b_forget, m_g_sgu, m_w_spatial, m_b_spatial, m_g_mla_q, m_w_uq, m_g_mla_kv, m_w_ukv, m_g_group_out, m_w_out, m_g_ffn_norm, m_w_up, m_w_down, m_g_final, v_g_mix_norm, v_w_in, v_b_forget, v_g_sgu, v_w_spatial, v_b_spatial, v_g_mla_q, v_w_uq, v_g_mla_kv, v_w_ukv, v_g_group_out, v_w_out, v_g_ffn_norm, v_w_up, v_w_down, v_g_final):
    depth = w_in.shape[0]
    s, d = x.shape[1], x.shape[2]
    x0 = x.reshape(s, d)
    target = loss_target.reshape(s, d)
    tb = _tables(s)
    me = 4 * lax.axis_index("x") + 2 * lax.axis_index("y") + lax.axis_index("c")

    assert depth == 2
    shards = {}
    for l in range(depth):
        shards.update({(l, "w_in"): _pad_in_cols(w_in[l]).astype(_WIRE), (l, "w_out"): w_out[l].astype(_WIRE),
                       (l, "w_up"): w_up[l].astype(_WIRE), (l, "w_down"): w_down[l].astype(_WIRE),
                       (l, "w_uq"): w_uq[l].astype(_WIRE), (l, "w_ukv"): w_ukv[l].astype(_WIRE)})
    wts = _ShardedWeights(shards)
    first = [(0, "w_in"), (0, "w_uq"), (0, "w_ukv"), (1, "w_uq"), (1, "w_ukv")]
    wts.full.update(zip(first, _all_gather([shards[k] for k in first], "gather_first")))

    row = lambda a: a.reshape(1, -1)

    def small(l):
        bf = jnp.pad(b_forget[l].reshape(1, N_HEADS), ((0, 0), (0, 128 - N_HEADS)))
        bt = jnp.pad(b_spatial[l].T, ((0, 0), (0, 128 - N_HEADS)))
        return dict(g_mix=row(g_mix_norm[l]), g_sgu=row(g_sgu[l]), w_s=w_spatial[l], b_t=bt, b_f=bf, gq=row(g_mla_q[l]),
                    gkv=row(g_mla_kv[l]), g_go=row(g_group_out[l]), g_ffn=row(g_ffn_norm[l]))

    smalls = [small(l) for l in range(depth)]
    lrow, dx, sm, dg_final = _local_step(x0, target, wts, smalls, row(g_final), tb)
    loss = lax.psum(jnp.sum(lrow), AXES)
    grad_x = dx.reshape(1, s, d)
    return _reduce_and_update(loss, grad_x, wts.recv, sm, dg_final, me, dict(
        g_mix_norm=(g_mix_norm, m_g_mix_norm, v_g_mix_norm), w_in=(w_in, m_w_in, v_w_in),
        b_forget=(b_forget, m_b_forget, v_b_forget), g_sgu=(g_sgu, m_g_sgu, v_g_sgu),
        w_spatial=(w_spatial, m_w_spatial, v_w_spatial), b_spatial=(b_spatial, m_b_spatial, v_b_spatial),
        g_mla_q=(g_mla_q, m_g_mla_q, v_g_mla_q), w_uq=(w_uq, m_w_uq, v_w_uq), g_mla_kv=(g_mla_kv, m_g_mla_kv, v_g_mla_kv),
        w_ukv=(w_ukv, m_w_ukv, v_w_ukv), g_group_out=(g_group_out, m_g_group_out, v_g_group_out),
        w_out=(w_out, m_w_out, v_w_out), g_ffn_norm=(g_ffn_norm, m_g_ffn_norm, v_g_ffn_norm), w_up=(w_up, m_w_up, v_w_up),
        w_down=(w_down, m_w_down, v_w_down), g_final=(g_final, m_g_final, v_g_final)))


_GATHER_AT = {
    "in_proj0": [(0, "w_out")],
    "fox_attn0": [(0, "w_down")],
    "mla_attn0": [(0, "w_up"), (1, "w_in")],
    "ffn_fwd0": [(1, "w_down")],
    "fox_attn1": [(1, "w_out")],
    "mla_attn1": [(1, "w_up")],
}
_SCATTER_AT = {
    "fox_attn_bwd1": [(1, "w_down")],
    "mla_attn_bwd1": [(1, "w_up"), (1, "w_out")],
    "ffn_bwd0": [(1, "w_in")],
    "fox_attn_bwd0": [(0, "w_down")],
    "mla_attn_bwd0": [(0, "w_up"), (0, "w_out")],
    "in_proj_bwd0": [(0, "w_in")],
}


class _FullWeights:
    def __init__(self, per_layer):
        self.per_layer, self.grads = per_layer, {}

    def get(self, l, name):
        return self.per_layer[l][name]

    def comm(self, host):
        return None

    def done(self, host, results):
        pass

    def grad(self, l, name, blocks):
        self.grads[(l, name)] = blocks


class _ShardedWeights(_FullWeights):
    def __init__(self, shards):
        self.shards, self.full, self.grads, self.recv = shards, {}, {}, {}

    def get(self, l, name):
        if name in ("wk", "wv"):
            return _split_ukv(_cols_to_full(self.full[(l, "w_ukv")]))[0 if name == "wk" else 1]
        if name == "wq":
            return _pad_uq(_cols_to_full(self.full[(l, "w_uq")]))
        g = self.full[(l, name)]
        return g if name == "w_up" else g.reshape(NDEV * g.shape[1], g.shape[2])

    def comm(self, host):
        if host in _GATHER_AT:
            return _Comm("gather", [self.shards[k] for k in _GATHER_AT[host]])
        if host in _SCATTER_AT:
            return _Comm("exchange", [self.grads[k] for k in _SCATTER_AT[host]])
        return None

    def done(self, host, results):
        if host in _GATHER_AT:
            self.full.update(zip(_GATHER_AT[host], results))
        if host in _SCATTER_AT:
            self.recv.update(zip(_SCATTER_AT[host], results))


def _local_step(x0, target, wts, smalls, g_final, tb):
    depth = len(smalls)
    s, d = x0.shape
    saved = []
    xl = x0
    for l in range(depth):
        p = smalls[l]
        z, h, got = _norm_matmul(xl, p["g_mix"], wts.get(l, "w_in"), f"in_proj{l}", wts.comm(f"in_proj{l}"))
        wts.done(f"in_proj{l}", got)
        ya = _sgu_fwd(z, p["g_sgu"], p["w_s"], p["b_t"], tb, f"sgu_fwd{l}")
        yb, ret, states = _ret_fwd(z, tb, f"ret_fwd{l}")
        cum = _fox_prep(z, p["b_f"], f"fox_prep{l}")
        kc, vc, vtc, qtc = _kv_prep(z, _Z_FOX_Q, _Z_FOX_K, _Z_FOX_V, HEAD_DIM ** -0.5, f"fox_kv{l}")
        yc, lse_c, got = _attn_fwd(z, _Z_FOX_Q, HEAD_DIM, kc, vtc, HEAD_DIM ** -0.5, cum, f"fox_attn{l}", wts.comm(f"fox_attn{l}"))
        wts.done(f"fox_attn{l}", got)
        wq, wk, wv = wts.get(l, "wq"), wts.get(l, "wk"), wts.get(l, "wv")
        qd, kd, vd, vtd, cqn, ckvn, qtd = _mla_prep(z, p["gq"], p["gkv"], wq, wk, wv, tb, f"mla_prep{l}")
        yd, lse_d, got = _attn_fwd(qd, 0, 128, kd, vtd, _SCALE_D, None, f"mla_attn{l}", wts.comm(f"mla_attn{l}"))
        wts.done(f"mla_attn{l}", got)
        ys = (ya, yb, yc, yd)
        x1, yn = _out_proj(ys, p["g_go"], wts.get(l, "w_out"), xl, f"out_proj{l}")
        x2, u, h2, got = _ffn_fwd(x1, p["g_ffn"], wts.get(l, "w_up"), wts.get(l, "w_down"), f"ffn_fwd{l}", wts.comm(f"ffn_fwd{l}"))
        wts.done(f"ffn_fwd{l}", got)
        saved.append(dict(x=xl, z=z, h=h, ys=ys, ret=ret, states=states, cum=cum, lse_c=lse_c, kc=kc, vc=vc, qd=qd, kd=kd, vd=vd,
                          cqn=cqn, ckvn=ckvn, lse_d=lse_d, x1=x1, yn=yn, u=u, h2=h2, wq=wq, wk=wk, wv=wv, qtc=qtc, qtd=qtd))
        xl = x2

    lrow, dx, dg_final = _loss_head(xl, g_final, target, "loss_head")

    sm = [None] * depth
    for l in reversed(range(depth)):
        p, a = smalls[l], saved[l]
        dx1, du, dg_ffn, got = _ffn_bwd(dx, a["x1"], a["u"], p["g_ffn"], wts.get(l, "w_up"), wts.get(l, "w_down"), f"ffn_bwd{l}",
                                        wts.comm(f"ffn_bwd{l}"))
        wts.done(f"ffn_bwd{l}", got)
        dw_down = _mm_tn(a["u"], dx, f"dw_down{l}", a_fn=lambda t: jnp.square(jnp.maximum(t, 0.0)), out_dtype=_WIRE)
        wts.grad(l, "w_down", dw_down.reshape(NDEV, dw_down.shape[0] // NDEV, d))
        wts.grad(l, "w_up", _mm_tn(a["h2"], du, f"dw_up{l}", blocked=True, out_dtype=_WIRE))
        dya, dyb, dg_go, dot_c, dot_d, dl_c, dl_d = _out_proj_bwd(dx1, wts.get(l, "w_out"), a["ys"], p["g_go"],
                                                                  f"out_proj_bwd{l}")
        wts.grad(l, "w_out", _mm_tn(a["yn"], dx1, f"dw_out{l}", out_dtype=_WIRE).reshape(NDEV, d // NDEV, d))
        dz = _ret_bwd(dyb, a["z"], a["ret"], a["states"], tb, f"ret_bwd{l}")
        dz, dg_sgu, dw_s, db_t = _sgu_bwd(dya, a["z"], p["g_sgu"], p["w_s"], p["b_t"], tb, dz, f"sgu_bwd{l}")
        dqt_c, dz, dck, dcq, got = _attn_bwd(a["kc"], a["vc"], a["qtc"], dot_c, a["lse_c"], dl_c, HEAD_DIM,
                                             HEAD_DIM ** -0.5, a["cum"], f"fox_attn_bwd{l}", _MXU,
                                             wts.comm(f"fox_attn_bwd{l}"), kv_into=dz)
        wts.done(f"fox_attn_bwd{l}", got)
        dz = _untranspose(dqt_c, _MXU, f"fox_dq{l}", into=dz, col=_DZ_FOX_Q)
        dqt_d, dk_d, dv_d, got = _attn_bwd(a["kd"], a["vd"], a["qtd"], dot_d, a["lse_d"], dl_d, 128, _SCALE_D, None,
                                           f"mla_attn_bwd{l}", F32, wts.comm(f"mla_attn_bwd{l}"))
        wts.done(f"mla_attn_bwd{l}", got)
        dz, dkr, dwq, dwk, dwv, dgq, dgkv = _mla_prep_bwd(dqt_d, dk_d, dv_d, a["z"], a["cqn"], a["ckvn"], p["gq"], p["gkv"],
                                                          a["wq"], a["wk"], a["wv"], tb, dz, f"mla_prep_bwd{l}")
        dz, db_f = _fox_post(dcq, dck, a["z"], p["b_f"], dkr, dz, f"fox_post{l}")
        wts.grad(l, "w_in", _unpad_in_cols(_mm_tn(a["h"], dz, f"dw_in{l}", out_dtype=_WIRE)).reshape(NDEV, d // NDEV, N_IN))
        dx, dg_mix, got = _in_proj_bwd(dz, wts.get(l, "w_in"), a["x"], p["g_mix"], dx1, f"in_proj_bwd{l}",
                                       wts.comm(f"in_proj_bwd{l}"))
        wts.done(f"in_proj_bwd{l}", got)
        sm[l] = [dg_mix, dg_go, dg_ffn, dg_sgu, dw_s, db_t[:, :N_HEADS].T, db_f[0, :N_HEADS], dgq, dgkv, _unpad_uq(dwq),
                 _join_ukv(dwk, dwv)]
    return lrow, dx, sm, dg_final


def _reduce_and_update(loss, grad_x, recv, sm, dg_final, me, given):
    depth = len(sm)
    pieces = [t for l in range(depth) for t in sm[l]] + [dg_final]
    flat = jnp.concatenate([t.reshape(-1) for t in pieces])
    n_flat = flat.shape[0]
    unit = NDEV * 8 * 128
    n_pad = -(-n_flat // unit) * unit
    packed = jnp.pad(flat, (0, n_pad - n_flat)).reshape(NDEV, n_pad // (NDEV * 128), 128)
    red = _sum_slots(_exchange([packed], "scatter_small")[0], "sum_small")
    full = _all_gather([red], "gather_small")[0].reshape(-1)
    offs = np.cumsum([0] + [int(np.prod(t.shape)) for t in pieces])
    red_pieces = [full[int(offs[i]):int(offs[i + 1])].reshape(pieces[i].shape) for i in range(len(pieces))]
    per = len(sm[0])
    stack = lambda i: jnp.stack([red_pieces[l * per + i] for l in range(depth)])
    g_small = dict(g_mix_norm=stack(0), g_group_out=stack(1), g_ffn_norm=stack(2), g_sgu=stack(3), w_spatial=stack(4),
                   b_spatial=stack(5), b_forget=stack(6), g_mla_q=stack(7), g_mla_kv=stack(8), g_final=red_pieces[-1])
    cq, ckv = given["w_uq"][0].shape[2], given["w_ukv"][0].shape[2]
    g_small["w_uq"] = lax.dynamic_slice_in_dim(stack(9), me * cq, cq, axis=2)
    g_small["w_ukv"] = lax.dynamic_slice_in_dim(stack(10), me * ckv, ckv, axis=2)

    names = list(given)
    outs = {}
    for nme in names:
        wv_, mv_, vv_ = given[nme]
        shape = wv_.shape
        if nme in ("w_in", "w_out", "w_up", "w_down"):
            res = None
            flat2 = lambda t: t.reshape(-1, shape[-1])
            for l in range(depth):
                res = _adamw_layer(recv[(l, nme)], flat2(wv_), flat2(mv_), flat2(vv_), l, res, f"adamw_{nme}{l}")
            outs[nme] = [t.reshape(shape) for t in res]
        else:
            two = lambda t: t.reshape(-1, shape[-1]) if t.ndim > 1 else t.reshape(1, -1)
            res = _adamw(two(g_small[nme]), two(wv_), two(mv_), two(vv_), f"adamw_{nme}")
            outs[nme] = [r.reshape(shape) for r in res]
    return (loss, grad_x, *[outs[n][0] for n in names], *[outs[n][1] for n in names], *[outs[n][2] for n in names],
            *[outs[n][3] for n in names])
```
